```python
import jax, jax.numpy as jnp
from jax import lax
import numpy as np

D_MODEL = 1024
BATCH = 8
SEQ = 4096
DEPTH = 1

CHUNK = 64
A_HEADS = 8
A_EXPAND = 128
A_FDIM = A_HEADS * A_EXPAND
A_IDIM = D_MODEL
A_HEAD_I = A_IDIM // A_HEADS
B_EXPAND = 2
B_INNER = B_EXPAND * D_MODEL
B_HEADDIM = 64
B_HEADS = B_INNER // B_HEADDIM
B_GROUPS = 4
B_HG = B_HEADS // B_GROUPS
B_STATE = 128
B_CONV = 4
B_CONV_DIM = B_INNER + 2 * B_GROUPS * B_STATE
D_FF = -(-8 * D_MODEL // (3 * 256)) * 256
ALPHA = (2.0 * DEPTH) ** 0.25
BETA = (8.0 * DEPTH) ** -0.25
LN_EPS = 1e-5
RMS_EPS = 1e-6
IN_SPLITS = (A_FDIM, A_FDIM, A_IDIM, A_IDIM, B_INNER, B_CONV_DIM, B_HEADS, D_MODEL, D_MODEL)
IN_DIM = sum(IN_SPLITS)

kernel_name = "hybrid_hgrn2_mamba2_deepnorm_adaln"

F32 = jnp.float32


def layer_norm(x, g=None, b=None):
    x32 = x.astype(F32)
    mu = jnp.mean(x32, axis=-1, keepdims=True)
    xc = x32 - mu
    y = xc * lax.rsqrt(jnp.mean(xc * xc, axis=-1, keepdims=True) + LN_EPS)
    if g is not None:
        y = y * g.astype(F32) + b.astype(F32)
    return y.astype(x.dtype)


def rms_norm(x, w=None):
    x32 = x.astype(F32)
    y = x32 * lax.rsqrt(jnp.mean(x32 * x32, axis=-1, keepdims=True) + RMS_EPS)
    if w is not None:
        y = y * w.astype(F32)
    return y


def to_chunks(t):
    b, s = t.shape[:2]
    return jnp.moveaxis(t.reshape(b, s // CHUNK, CHUNK, *t.shape[2:]), 1, 0)


def from_chunks(t):
    t = jnp.moveaxis(t, 0, 1)
    return t.reshape(t.shape[0], t.shape[1] * t.shape[2], *t.shape[3:])


def causal_dwconv(x, w, b):
    k, ch = w.shape
    y = lax.conv_general_dilated(x, w[:, None, :], window_strides=(1,), padding=[(k - 1, 0)],
                                 dimension_numbers=("NWC", "WIO", "NWC"), feature_group_count=ch)
    return y + b


def hgrn2_mixer(q, f_logit, i, g, lb, w_gnorm):
    bsz, s, _ = q.shape
    f = lb + (1.0 - lb) * jax.nn.sigmoid(f_logit.astype(F32))
    log_f = jnp.log(f)
    k = 1.0 - f
    qf = jax.nn.silu(q.astype(F32)) * (A_EXPAND ** -0.5)
    hk = (bsz, s, A_HEADS, A_EXPAND)
    qh, kh, gh = qf.reshape(hk), k.reshape(hk), log_f.reshape(hk)
    vh = i.astype(F32).reshape(bsz, s, A_HEADS, A_HEAD_I)
    mask = jnp.tril(jnp.ones((CHUNK, CHUNK), bool))[None, :, :, None, None]

    def step(state, inp):
        qc, kc, gc, vc = inp
        bc = jnp.cumsum(gc, axis=1)
        diff = bc[:, :, None] - bc[:, None, :]
        decay = jnp.exp(jnp.where(mask, diff, -jnp.inf))
        scores = jnp.einsum("bthk,bshk,btshk->bhts", qc, kc, decay)
        o = jnp.einsum("bhts,bshv->bthv", scores, vc)
        o = o + jnp.einsum("bthk,bhkv->bthv", qc * jnp.exp(bc), state)
        b_last = bc[:, -1]
        state = state * jnp.exp(b_last)[..., None] + jnp.einsum(
            "bshk,bshv->bhkv", kc * jnp.exp(b_last[:, None] - bc), vc)
        return state, o

    s0 = jnp.zeros((bsz, A_HEADS, A_EXPAND, A_HEAD_I), F32)
    _, o = lax.scan(step, s0, (to_chunks(qh), to_chunks(kh), to_chunks(gh), to_chunks(vh)))
    o = from_chunks(o)
    o = rms_norm(o, w_gnorm) * jax.nn.silu(g.astype(F32)).reshape(o.shape)
    return o.reshape(bsz, s, A_IDIM)


def mamba2_mixer(z, xbc, dt, conv_w, conv_b, dt_bias, a_log, d_skip, w_norm):
    bsz, s, _ = z.shape
    xbc = jax.nn.silu(causal_dwconv(xbc, conv_w, conv_b)).astype(F32)
    xs = xbc[..., :B_INNER].reshape(bsz, s, B_GROUPS, B_HG, B_HEADDIM)
    bm = xbc[..., B_INNER:B_INNER + B_GROUPS * B_STATE].reshape(bsz, s, B_GROUPS, B_STATE)
    cm = xbc[..., B_INNER + B_GROUPS * B_STATE:].reshape(bsz, s, B_GROUPS, B_STATE)
    delta = jax.nn.softplus(dt.astype(F32) + dt_bias.astype(F32)).reshape(bsz, s, B_GROUPS, B_HG)
    a = -jnp.exp(a_log.astype(F32)).reshape(B_GROUPS, B_HG) * delta
    xdt = xs * delta[..., None]
    mask = jnp.tril(jnp.ones((CHUNK, CHUNK), bool))[None, :, :, None, None]

    def step(state, inp):
        xc, ac, bc, cc = inp
        acum = jnp.cumsum(ac, axis=1)
        seg = acum[:, :, None] - acum[:, None, :]
        decay = jnp.exp(jnp.where(mask, seg, -jnp.inf))
        cb = jnp.einsum("btgn,bsgn->btsg", cc, bc)
        y = jnp.einsum("btsg,btsgh,bsghp->btghp", cb, decay, xc)
        y = y + jnp.einsum("btgn,bghpn->btghp", cc, state) * jnp.exp(acum)[..., None]
        a_last = acum[:, -1]
        state = state * jnp.exp(a_last)[..., None, None] + jnp.einsum(
            "bsgn,bsgh,bsghp->bghpn", bc, jnp.exp(a_last[:, None] - acum), xc)
        return state, y

    s0 = jnp.zeros((bsz, B_GROUPS, B_HG, B_HEADDIM, B_STATE), F32)
    _, y = lax.scan(step, s0, (to_chunks(xdt), to_chunks(a), to_chunks(bm), to_chunks(cm)))
    y = from_chunks(y) + xs * d_skip.astype(F32).reshape(B_GROUPS, B_HG)[..., None]
    y = y.reshape(bsz, s, B_INNER) * jax.nn.silu(z.astype(F32))
    y = rms_norm(y.reshape(bsz, s, B_GROUPS, B_INNER // B_GROUPS)).reshape(bsz, s, B_INNER)
    return y * w_norm.astype(F32)


def token_mixer(u, w_in, lb, gnorm, conv_w, conv_b, dt_bias, a_log, d_skip, ssm_norm,
                w_branch_a, w_branch_b, w_o):
    proj = u @ w_in
    offs = np.cumsum(IN_SPLITS)[:-1].tolist()
    q, f, i, g, z, xbc, dt, gate_a, gate_b = jnp.split(proj, offs, axis=-1)
    y_a = hgrn2_mixer(q, f, i, g, lb, gnorm).astype(u.dtype) @ w_branch_a
    y_b = mamba2_mixer(z, xbc, dt, conv_w, conv_b, dt_bias, a_log, d_skip,
                       ssm_norm).astype(u.dtype) @ w_branch_b
    merged = jax.nn.sigmoid(gate_a) * y_a + jax.nn.sigmoid(gate_b) * y_b
    return merged @ w_o


def swiglu(u, w_gate, w_up, w_down):
    return (jax.nn.silu(u @ w_gate) * (u @ w_up)) @ w_down


def _fwd_setup_inputs(seed: int = 0) -> dict:
    key = jax.random.key(seed)
    ks = jax.random.split(key, 24)
    n = lambda k, shape, s: jax.random.normal(k, shape, F32) * s
    L, D = DEPTH, D_MODEL
    dt0 = jnp.exp(jax.random.uniform(ks[8], (L, B_HEADS), F32, np.log(1e-3), np.log(1e-1)))
    return {
        "x": jax.random.normal(ks[0], (BATCH, SEQ, D), F32),
        "c": jax.random.normal(ks[1], (BATCH, D), F32),
        "w_ada": n(ks[2], (L, D, 6 * D), D ** -0.5),
        "b_ada": n(ks[3], (L, 6 * D), 0.02),
        "w_in": n(ks[4], (L, D, IN_DIM), D ** -0.5),
        "hgrn_lb": n(ks[5], (DEPTH + 1, A_FDIM), 0.1),
        "hgrn_gnorm": 1.0 + n(ks[6], (L, A_HEAD_I), 0.02),
        "ssm_conv_w": n(ks[7], (L, B_CONV, B_CONV_DIM), B_CONV ** -0.5),
        "ssm_conv_b": n(ks[9], (L, B_CONV_DIM), 0.02),
        "ssm_dt_bias": dt0 + jnp.log(-jnp.expm1(-dt0)),
        "ssm_a_log": jnp.log(jax.random.uniform(ks[10], (L, B_HEADS), F32, 1.0, 16.0)),
        "ssm_d": 1.0 + n(ks[11], (L, B_HEADS), 0.02),
        "ssm_norm": 1.0 + n(ks[12], (L, B_INNER), 0.02),
        "w_branch_a": n(ks[13], (L, A_IDIM, D), A_IDIM ** -0.5),
        "w_branch_b": n(ks[14], (L, B_INNER, D), B_INNER ** -0.5),
        "w_o": n(ks[15], (L, D, D), BETA * D ** -0.5),
        "ln1_g": 1.0 + n(ks[16], (L, D), 0.02),
        "ln1_b": n(ks[17], (L, D), 0.02),
        "w_ffn_gate": n(ks[18], (L, D, D_FF), D ** -0.5),
        "w_ffn_up": n(ks[19], (L, D, D_FF), D ** -0.5),
        "w_ffn_down": n(ks[20], (L, D_FF, D), BETA * D_FF ** -0.5),
        "ln2_g": 1.0 + n(ks[21], (L, D), 0.02),
        "ln2_b": n(ks[22], (L, D), 0.02),
    }


def _fwd_reference(x, c, w_ada, b_ada, w_in, hgrn_lb, hgrn_gnorm, ssm_conv_w, ssm_conv_b,
              ssm_dt_bias, ssm_a_log, ssm_d, ssm_norm, w_branch_a, w_branch_b, w_o,
              ln1_g, ln1_b, w_ffn_gate, w_ffn_up, w_ffn_down, ln2_g, ln2_b):
    cond = jax.nn.silu(c)
    lb_table = jnp.cumsum(jax.nn.softmax(hgrn_lb.astype(F32), axis=0), axis=0)
    for l in range(DEPTH):
        mod = (cond @ w_ada[l] + b_ada[l])[:, None, :]
        sh1, sc1, g1, sh2, sc2, g2 = jnp.split(mod, 6, axis=-1)
        u = layer_norm(x) * (1.0 + sc1) + sh1
        h = token_mixer(u, w_in[l], lb_table[l], hgrn_gnorm[l], ssm_conv_w[l], ssm_conv_b[l],
                        ssm_dt_bias[l], ssm_a_log[l], ssm_d[l], ssm_norm[l],
                        w_branch_a[l], w_branch_b[l], w_o[l])
        x = layer_norm(ALPHA * x + g1 * h, ln1_g[l], ln1_b[l])
        u = layer_norm(x) * (1.0 + sc2) + sh2
        h = swiglu(u, w_ffn_gate[l], w_ffn_up[l], w_ffn_down[l])
        x = layer_norm(ALPHA * x + g2 * h, ln2_g[l], ln2_b[l])
    return x


import jax as _jax
import jax.numpy as _jnp

TWIN_FORMAT = 'train_step'
FWD_PARAMS = ['x', 'c', 'w_ada', 'b_ada', 'w_in', 'hgrn_lb', 'hgrn_gnorm', 'ssm_conv_w', 'ssm_conv_b', 'ssm_dt_bias', 'ssm_a_log', 'ssm_d', 'ssm_norm', 'w_branch_a', 'w_branch_b', 'w_o', 'ln1_g', 'ln1_b', 'w_ffn_gate', 'w_ffn_up', 'w_ffn_down', 'ln2_g', 'ln2_b']
TWIN_WEIGHTS = ['w_ada', 'b_ada', 'w_in', 'hgrn_lb', 'hgrn_gnorm', 'ssm_conv_w', 'ssm_conv_b', 'ssm_dt_bias', 'ssm_a_log', 'ssm_d', 'ssm_norm', 'w_branch_a', 'w_branch_b', 'w_o', 'ln1_g', 'ln1_b', 'w_ffn_gate', 'w_ffn_up', 'w_ffn_down', 'ln2_g', 'ln2_b']
TWIN_DIFF_INPUT = 'x'
TWIN_INPUTS = ['x', 'c', 'w_ada', 'b_ada', 'w_in', 'hgrn_lb', 'hgrn_gnorm', 'ssm_conv_w', 'ssm_conv_b', 'ssm_dt_bias', 'ssm_a_log', 'ssm_d', 'ssm_norm', 'w_branch_a', 'w_branch_b', 'w_o', 'ln1_g', 'ln1_b', 'w_ffn_gate', 'w_ffn_up', 'w_ffn_down', 'ln2_g', 'ln2_b', 'loss_target', 'm_w_ada', 'm_b_ada', 'm_w_in', 'm_hgrn_lb', 'm_hgrn_gnorm', 'm_ssm_conv_w', 'm_ssm_conv_b', 'm_ssm_dt_bias', 'm_ssm_a_log', 'm_ssm_d', 'm_ssm_norm', 'm_w_branch_a', 'm_w_branch_b', 'm_w_o', 'm_ln1_g', 'm_ln1_b', 'm_w_ffn_gate', 'm_w_ffn_up', 'm_w_ffn_down', 'm_ln2_g', 'm_ln2_b', 'v_w_ada', 'v_b_ada', 'v_w_in', 'v_hgrn_lb', 'v_hgrn_gnorm', 'v_ssm_conv_w', 'v_ssm_conv_b', 'v_ssm_dt_bias', 'v_ssm_a_log', 'v_ssm_d', 'v_ssm_norm', 'v_w_branch_a', 'v_w_branch_b', 'v_w_o', 'v_ln1_g', 'v_ln1_b', 'v_w_ffn_gate', 'v_w_ffn_up', 'v_w_ffn_down', 'v_ln2_g', 'v_ln2_b']
TWIN_OUTPUTS = ['loss', 'grad_x', 'grad_w_ada', 'grad_b_ada', 'grad_w_in', 'grad_hgrn_lb', 'grad_hgrn_gnorm', 'grad_ssm_conv_w', 'grad_ssm_conv_b', 'grad_ssm_dt_bias', 'grad_ssm_a_log', 'grad_ssm_d', 'grad_ssm_norm', 'grad_w_branch_a', 'grad_w_branch_b', 'grad_w_o', 'grad_ln1_g', 'grad_ln1_b', 'grad_w_ffn_gate', 'grad_w_ffn_up', 'grad_w_ffn_down', 'grad_ln2_g', 'grad_ln2_b', 'delta_w_ada', 'delta_b_ada', 'delta_w_in', 'delta_hgrn_lb', 'delta_hgrn_gnorm', 'delta_ssm_conv_w', 'delta_ssm_conv_b', 'delta_ssm_dt_bias', 'delta_ssm_a_log', 'delta_ssm_d', 'delta_ssm_norm', 'delta_w_branch_a', 'delta_w_branch_b', 'delta_w_o', 'delta_ln1_g', 'delta_ln1_b', 'delta_w_ffn_gate', 'delta_w_ffn_up', 'delta_w_ffn_down', 'delta_ln2_g', 'delta_ln2_b', 'new_m_w_ada', 'new_m_b_ada', 'new_m_w_in', 'new_m_hgrn_lb', 'new_m_hgrn_gnorm', 'new_m_ssm_conv_w', 'new_m_ssm_conv_b', 'new_m_ssm_dt_bias', 'new_m_ssm_a_log', 'new_m_ssm_d', 'new_m_ssm_norm', 'new_m_w_branch_a', 'new_m_w_branch_b', 'new_m_w_o', 'new_m_ln1_g', 'new_m_ln1_b', 'new_m_w_ffn_gate', 'new_m_w_ffn_up', 'new_m_w_ffn_down', 'new_m_ln2_g', 'new_m_ln2_b', 'new_v_w_ada', 'new_v_b_ada', 'new_v_w_in', 'new_v_hgrn_lb', 'new_v_hgrn_gnorm', 'new_v_ssm_conv_w', 'new_v_ssm_conv_b', 'new_v_ssm_dt_bias', 'new_v_ssm_a_log', 'new_v_ssm_d', 'new_v_ssm_norm', 'new_v_w_branch_a', 'new_v_w_branch_b', 'new_v_w_o', 'new_v_ln1_g', 'new_v_ln1_b', 'new_v_w_ffn_gate', 'new_v_w_ffn_up', 'new_v_w_ffn_down', 'new_v_ln2_g', 'new_v_ln2_b']
TWIN_LEAF_KINDS = {'loss': 'loss', 'grad_x': 'grad_x', 'grad_w_ada': 'grad_w', 'grad_b_ada': 'grad_w', 'grad_w_in': 'grad_w', 'grad_hgrn_lb': 'grad_w', 'grad_hgrn_gnorm': 'grad_w', 'grad_ssm_conv_w': 'grad_w', 'grad_ssm_conv_b': 'grad_w', 'grad_ssm_dt_bias': 'grad_w', 'grad_ssm_a_log': 'grad_w', 'grad_ssm_d': 'grad_w', 'grad_ssm_norm': 'grad_w', 'grad_w_branch_a': 'grad_w', 'grad_w_branch_b': 'grad_w', 'grad_w_o': 'grad_w', 'grad_ln1_g': 'grad_w', 'grad_ln1_b': 'grad_w', 'grad_w_ffn_gate': 'grad_w', 'grad_w_ffn_up': 'grad_w', 'grad_w_ffn_down': 'grad_w', 'grad_ln2_g': 'grad_w', 'grad_ln2_b': 'grad_w', 'delta_w_ada': 'delta_w', 'delta_b_ada': 'delta_w', 'delta_w_in': 'delta_w', 'delta_hgrn_lb': 'delta_w', 'delta_hgrn_gnorm': 'delta_w', 'delta_ssm_conv_w': 'delta_w', 'delta_ssm_conv_b': 'delta_w', 'delta_ssm_dt_bias': 'delta_w', 'delta_ssm_a_log': 'delta_w', 'delta_ssm_d': 'delta_w', 'delta_ssm_norm': 'delta_w', 'delta_w_branch_a': 'delta_w', 'delta_w_branch_b': 'delta_w', 'delta_w_o': 'delta_w', 'delta_ln1_g': 'delta_w', 'delta_ln1_b': 'delta_w', 'delta_w_ffn_gate': 'delta_w', 'delta_w_ffn_up': 'delta_w', 'delta_w_ffn_down': 'delta_w', 'delta_ln2_g': 'delta_w', 'delta_ln2_b': 'delta_w', 'new_m_w_ada': 'new_m', 'new_m_b_ada': 'new_m', 'new_m_w_in': 'new_m', 'new_m_hgrn_lb': 'new_m', 'new_m_hgrn_gnorm': 'new_m', 'new_m_ssm_conv_w': 'new_m', 'new_m_ssm_conv_b': 'new_m', 'new_m_ssm_dt_bias': 'new_m', 'new_m_ssm_a_log': 'new_m', 'new_m_ssm_d': 'new_m', 'new_m_ssm_norm': 'new_m', 'new_m_w_branch_a': 'new_m', 'new_m_w_branch_b': 'new_m', 'new_m_w_o': 'new_m', 'new_m_ln1_g': 'new_m', 'new_m_ln1_b': 'new_m', 'new_m_w_ffn_gate': 'new_m', 'new_m_w_ffn_up': 'new_m', 'new_m_w_ffn_down': 'new_m', 'new_m_ln2_g': 'new_m', 'new_m_ln2_b': 'new_m', 'new_v_w_ada': 'new_v', 'new_v_b_ada': 'new_v', 'new_v_w_in': 'new_v', 'new_v_hgrn_lb': 'new_v', 'new_v_hgrn_gnorm': 'new_v', 'new_v_ssm_conv_w': 'new_v', 'new_v_ssm_conv_b': 'new_v', 'new_v_ssm_dt_bias': 'new_v', 'new_v_ssm_a_log': 'new_v', 'new_v_ssm_d': 'new_v', 'new_v_ssm_norm': 'new_v', 'new_v_w_branch_a': 'new_v', 'new_v_w_branch_b': 'new_v', 'new_v_w_o': 'new_v', 'new_v_ln1_g': 'new_v', 'new_v_ln1_b': 'new_v', 'new_v_w_ffn_gate': 'new_v', 'new_v_w_ffn_up': 'new_v', 'new_v_w_ffn_down': 'new_v', 'new_v_ln2_g': 'new_v', 'new_v_ln2_b': 'new_v'}


def _forward(args):
    return _fwd_reference(*[args[k] for k in FWD_PARAMS])


def _output_shape():
    out = _jax.eval_shape(lambda: _forward(_fwd_setup_inputs(0)))
    return out.shape, out.dtype

N_MICROBATCH = 1
ADAM_LR = 0.001
ADAM_B1 = 0.9
ADAM_B2 = 0.999
ADAM_EPS = 1e-08
ADAM_WD = 0.01
ADAM_STEP = 10
PER_EXAMPLE_BATCH_AXIS = {'x': 0, 'c': 0, 'loss_target': 0}
SHARED_INPUTS = []
_WEIGHT_DTYPES = {'w_ada': _jnp.float32, 'b_ada': _jnp.float32, 'w_in': _jnp.float32, 'hgrn_lb': _jnp.float32, 'hgrn_gnorm': _jnp.float32, 'ssm_conv_w': _jnp.float32, 'ssm_conv_b': _jnp.float32, 'ssm_dt_bias': _jnp.float32, 'ssm_a_log': _jnp.float32, 'ssm_d': _jnp.float32, 'ssm_norm': _jnp.float32, 'w_branch_a': _jnp.float32, 'w_branch_b': _jnp.float32, 'w_o': _jnp.float32, 'ln1_g': _jnp.float32, 'ln1_b': _jnp.float32, 'w_ffn_gate': _jnp.float32, 'w_ffn_up': _jnp.float32, 'w_ffn_down': _jnp.float32, 'ln2_g': _jnp.float32, 'ln2_b': _jnp.float32}
MOMENT_SCALE = {'w_ada': 4.491001e-02, 'b_ada': 7.676749e-02, 'w_in': 1.890841e-02, 'hgrn_lb': 1.502948e-03, 'hgrn_gnorm': 6.980811e-02, 'ssm_conv_w': 1.908713e-02, 'ssm_conv_b': 2.105746e-02, 'ssm_dt_bias': 6.277553e-02, 'ssm_a_log': 1.062508e-01, 'ssm_d': 1.092046e-01, 'ssm_norm': 2.368337e-02, 'w_branch_a': 2.700304e-02, 'w_branch_b': 3.346041e-02, 'w_o': 7.241297e-02, 'ln1_g': 9.162105e-01, 'ln1_b': 4.187229e-01, 'w_ffn_gate': 3.554040e-02, 'w_ffn_up': 3.479975e-02, 'w_ffn_down': 9.685206e-02, 'ln2_g': 3.212499e+01, 'ln2_b': 1.916390e+00}


def _to_microbatches(a, axis):
    t = _jnp.moveaxis(a, axis, 0)
    t = t.reshape((N_MICROBATCH, t.shape[0] // N_MICROBATCH) + t.shape[1:])
    return _jnp.moveaxis(t, 1, axis + 1)


def setup_inputs(seed: int = 0) -> dict:
    inp = _fwd_setup_inputs(seed)
    key = _jax.random.fold_in(_jax.random.key(seed), 7919)
    shape, _ = _output_shape()
    out = dict(inp)
    out["loss_target"] = _jax.random.normal(_jax.random.fold_in(key, 0), shape, _jnp.float32)
    for i, name in enumerate(TWIN_WEIGHTS):
        w = inp[name].astype(_jnp.float32)
        if MOMENT_SCALE is None:
            s = _jnp.sqrt(_jnp.mean(_jnp.square(w)) + 1e-30)
        else:
            s = MOMENT_SCALE[name]
        km, kv = _jax.random.split(_jax.random.fold_in(key, i + 1))
        out[name] = w
        out["m_" + name] = s * _jax.random.normal(km, w.shape, _jnp.float32)
        out["v_" + name] = (s * s) * _jax.random.uniform(kv, w.shape, _jnp.float32, 0.5, 1.5)
    if N_MICROBATCH > 1:
        for name, axis in PER_EXAMPLE_BATCH_AXIS.items():
            out[name] = _to_microbatches(out[name], axis)
    return {'x': out['x'], 'c': out['c'], 'w_ada': out['w_ada'], 'b_ada': out['b_ada'], 'w_in': out['w_in'], 'hgrn_lb': out['hgrn_lb'], 'hgrn_gnorm': out['hgrn_gnorm'], 'ssm_conv_w': out['ssm_conv_w'], 'ssm_conv_b': out['ssm_conv_b'], 'ssm_dt_bias': out['ssm_dt_bias'], 'ssm_a_log': out['ssm_a_log'], 'ssm_d': out['ssm_d'], 'ssm_norm': out['ssm_norm'], 'w_branch_a': out['w_branch_a'], 'w_branch_b': out['w_branch_b'], 'w_o': out['w_o'], 'ln1_g': out['ln1_g'], 'ln1_b': out['ln1_b'], 'w_ffn_gate': out['w_ffn_gate'], 'w_ffn_up': out['w_ffn_up'], 'w_ffn_down': out['w_ffn_down'], 'ln2_g': out['ln2_g'], 'ln2_b': out['ln2_b'], 'loss_target': out['loss_target'], 'm_w_ada': out['m_w_ada'], 'm_b_ada': out['m_b_ada'], 'm_w_in': out['m_w_in'], 'm_hgrn_lb': out['m_hgrn_lb'], 'm_hgrn_gnorm': out['m_hgrn_gnorm'], 'm_ssm_conv_w': out['m_ssm_conv_w'], 'm_ssm_conv_b': out['m_ssm_conv_b'], 'm_ssm_dt_bias': out['m_ssm_dt_bias'], 'm_ssm_a_log': out['m_ssm_a_log'], 'm_ssm_d': out['m_ssm_d'], 'm_ssm_norm': out['m_ssm_norm'], 'm_w_branch_a': out['m_w_branch_a'], 'm_w_branch_b': out['m_w_branch_b'], 'm_w_o': out['m_w_o'], 'm_ln1_g': out['m_ln1_g'], 'm_ln1_b': out['m_ln1_b'], 'm_w_ffn_gate': out['m_w_ffn_gate'], 'm_w_ffn_up': out['m_w_ffn_up'], 'm_w_ffn_down': out['m_w_ffn_down'], 'm_ln2_g': out['m_ln2_g'], 'm_ln2_b': out['m_ln2_b'], 'v_w_ada': out['v_w_ada'], 'v_b_ada': out['v_b_ada'], 'v_w_in': out['v_w_in'], 'v_hgrn_lb': out['v_hgrn_lb'], 'v_hgrn_gnorm': out['v_hgrn_gnorm'], 'v_ssm_conv_w': out['v_ssm_conv_w'], 'v_ssm_conv_b': out['v_ssm_conv_b'], 'v_ssm_dt_bias': out['v_ssm_dt_bias'], 'v_ssm_a_log': out['v_ssm_a_log'], 'v_ssm_d': out['v_ssm_d'], 'v_ssm_norm': out['v_ssm_norm'], 'v_w_branch_a': out['v_w_branch_a'], 'v_w_branch_b': out['v_w_branch_b'], 'v_w_o': out['v_w_o'], 'v_ln1_g': out['v_ln1_g'], 'v_ln1_b': out['v_ln1_b'], 'v_w_ffn_gate': out['v_w_ffn_gate'], 'v_w_ffn_up': out['v_w_ffn_up'], 'v_w_ffn_down': out['v_w_ffn_down'], 'v_ln2_g': out['v_ln2_g'], 'v_ln2_b': out['v_ln2_b']}


def _loss(weights, diff, rest, loss_target):
    with _jax.named_scope("forward"):
        args = {**rest, TWIN_DIFF_INPUT: diff, **{k: w.astype(_WEIGHT_DTYPES[k]) for k, w in weights.items()}}
        y = _forward(args)
    with _jax.named_scope("loss_head"):
        err = _jnp.square(y.astype(_jnp.float32) - loss_target)
        return 0.5 * _jnp.sum(_jnp.mean(err, axis=-1)) if err.ndim else 0.5 * err


def _adamw(w, g, m, v):
    m = ADAM_B1 * m + (1.0 - ADAM_B1) * g
    v = ADAM_B2 * v + (1.0 - ADAM_B2) * _jnp.square(g)
    m_hat = m / (1.0 - ADAM_B1 ** ADAM_STEP)
    v_hat = v / (1.0 - ADAM_B2 ** ADAM_STEP)
    delta = -ADAM_LR * (m_hat / (_jnp.sqrt(v_hat) + ADAM_EPS) + ADAM_WD * w)
    return delta, m, v


def reference(x, c, w_ada, b_ada, w_in, hgrn_lb, hgrn_gnorm, ssm_conv_w, ssm_conv_b, ssm_dt_bias, ssm_a_log, ssm_d, ssm_norm, w_branch_a, w_branch_b, w_o, ln1_g, ln1_b, w_ffn_gate, w_ffn_up, w_ffn_down, ln2_g, ln2_b, loss_target, m_w_ada, m_b_ada, m_w_in, m_hgrn_lb, m_hgrn_gnorm, m_ssm_conv_w, m_ssm_conv_b, m_ssm_dt_bias, m_ssm_a_log, m_ssm_d, m_ssm_norm, m_w_branch_a, m_w_branch_b, m_w_o, m_ln1_g, m_ln1_b, m_w_ffn_gate, m_w_ffn_up, m_w_ffn_down, m_ln2_g, m_ln2_b, v_w_ada, v_b_ada, v_w_in, v_hgrn_lb, v_hgrn_gnorm, v_ssm_conv_w, v_ssm_conv_b, v_ssm_dt_bias, v_ssm_a_log, v_ssm_d, v_ssm_norm, v_w_branch_a, v_w_branch_b, v_w_o, v_ln1_g, v_ln1_b, v_w_ffn_gate, v_w_ffn_up, v_w_ffn_down, v_ln2_g, v_ln2_b):
    given = dict(x=x, c=c, w_ada=w_ada, b_ada=b_ada, w_in=w_in, hgrn_lb=hgrn_lb, hgrn_gnorm=hgrn_gnorm, ssm_conv_w=ssm_conv_w, ssm_conv_b=ssm_conv_b, ssm_dt_bias=ssm_dt_bias, ssm_a_log=ssm_a_log, ssm_d=ssm_d, ssm_norm=ssm_norm, w_branch_a=w_branch_a, w_branch_b=w_branch_b, w_o=w_o, ln1_g=ln1_g, ln1_b=ln1_b, w_ffn_gate=w_ffn_gate, w_ffn_up=w_ffn_up, w_ffn_down=w_ffn_down, ln2_g=ln2_g, ln2_b=ln2_b, loss_target=loss_target, m_w_ada=m_w_ada, m_b_ada=m_b_ada, m_w_in=m_w_in, m_hgrn_lb=m_hgrn_lb, m_hgrn_gnorm=m_hgrn_gnorm, m_ssm_conv_w=m_ssm_conv_w, m_ssm_conv_b=m_ssm_conv_b, m_ssm_dt_bias=m_ssm_dt_bias, m_ssm_a_log=m_ssm_a_log, m_ssm_d=m_ssm_d, m_ssm_norm=m_ssm_norm, m_w_branch_a=m_w_branch_a, m_w_branch_b=m_w_branch_b, m_w_o=m_w_o, m_ln1_g=m_ln1_g, m_ln1_b=m_ln1_b, m_w_ffn_gate=m_w_ffn_gate, m_w_ffn_up=m_w_ffn_up, m_w_ffn_down=m_w_ffn_down, m_ln2_g=m_ln2_g, m_ln2_b=m_ln2_b, v_w_ada=v_w_ada, v_b_ada=v_b_ada, v_w_in=v_w_in, v_hgrn_lb=v_hgrn_lb, v_hgrn_gnorm=v_hgrn_gnorm, v_ssm_conv_w=v_ssm_conv_w, v_ssm_conv_b=v_ssm_conv_b, v_ssm_dt_bias=v_ssm_dt_bias, v_ssm_a_log=v_ssm_a_log, v_ssm_d=v_ssm_d, v_ssm_norm=v_ssm_norm, v_w_branch_a=v_w_branch_a, v_w_branch_b=v_w_branch_b, v_w_o=v_w_o, v_ln1_g=v_ln1_g, v_ln1_b=v_ln1_b, v_w_ffn_gate=v_w_ffn_gate, v_w_ffn_up=v_w_ffn_up, v_w_ffn_down=v_w_ffn_down, v_ln2_g=v_ln2_g, v_ln2_b=v_ln2_b)
    weights = {n: given[n] for n in TWIN_WEIGHTS}
    shared = {n: given[n] for n in SHARED_INPUTS}
    per_example = {n: given[n] for n in ['x', 'c']}
    grad_fn = _jax.value_and_grad(_loss, argnums=(0, 1))

    def one_microbatch(ex, loss_target):
        ex = dict(ex)
        diff = ex.pop(TWIN_DIFF_INPUT)
        return grad_fn(weights, diff, {**shared, **ex}, loss_target)

    if N_MICROBATCH == 1:
        loss, (grad_w, grad_x) = one_microbatch(per_example, given["loss_target"])
    else:
        def body(carry, xs):
            loss_sum, grad_sum = carry
            l_k, (gw_k, gx_k) = one_microbatch(xs[0], xs[1])
            with _jax.named_scope("update"):
                return (loss_sum + l_k, _jax.tree.map(_jnp.add, grad_sum, gw_k)), gx_k

        init = (_jnp.zeros((), _jnp.float32), _jax.tree.map(_jnp.zeros_like, weights))
        (loss, grad_w), grad_x = _jax.lax.scan(body, init, (per_example, given["loss_target"]))
    with _jax.named_scope("update"):
        delta_w, new_m, new_v = {}, {}, {}
        for n in TWIN_WEIGHTS:
            delta_w[n], new_m[n], new_v[n] = _adamw(weights[n], grad_w[n], given["m_" + n], given["v_" + n])
    return (loss, grad_x, *[grad_w[n] for n in TWIN_WEIGHTS], *[delta_w[n] for n in TWIN_WEIGHTS],
            *[new_m[n] for n in TWIN_WEIGHTS], *[new_v[n] for n in TWIN_WEIGHTS])
```

```python
import functools

import jax
import jax.numpy as jnp
from jax import lax
from jax.experimental import pallas as pl
from jax.experimental.pallas import tpu as pltpu

F32, BF16 = jnp.float32, jnp.bfloat16
HI = lax.Precision.HIGHEST
MESH = pl.DeviceIdType.MESH

D = 1024
CHUNK = 64
LANES = 128
N_HEADS_A = 8
N_GROUPS_B = 4
B_INNER = 2048
CONV_DIM = 3072
D_FF = 2816
ALPHA = 2.0 ** 0.25
LN_EPS = 1e-5
RMS_EPS = 1e-6
ADAM_LR, ADAM_B1, ADAM_B2, ADAM_EPS, ADAM_WD, ADAM_STEP = 0.001, 0.9, 0.999, 1e-08, 0.01, 10

IN_ORIG = 11296
IN_PAD = 11520
COL_Z, COL_XBC, COL_GA, COL_GB, COL_DT = 4096, 6144, 9216, 10240, 11264
ORIG_DT = 9216

PACK_SEGS = (("w_in", 22592), ("w_branch_a", 2048), ("w_branch_b", 4096), ("w_o", 2048),
             ("w_ffn_gate", 5632), ("w_ffn_up", 5632), ("w_ffn_down", 5632))
PACK_ROWS = 49152
VMEM_LIMIT = 48 * 1024 * 1024

_DIMS = {"nn": (((1,), (0,)), ((), ())), "nt": (((1,), (1,)), ((), ())), "tn": (((0,), (0,)), ((), ()))}


def _bd(a, b, mode):
    return lax.dot_general(a.astype(BF16), b.astype(BF16), _DIMS[mode], preferred_element_type=F32)


@functools.partial(jax.custom_vjp, nondiff_argnums=(2,))
def bdot(a, b, mode):
    return _bd(a, b, mode)


def _bdot_fwd(a, b, mode):
    return _bd(a, b, mode), (a, b)


def _bdot_bwd(mode, res, g):
    a, b = res
    if mode == "nn":
        return _bd(g, b, "nt"), _bd(a, g, "tn")
    if mode == "nt":
        return _bd(g, b, "nn"), _bd(g, a, "tn")
    return _bd(b, g, "nt"), _bd(a, g, "nn")


bdot.defvjp(_bdot_fwd, _bdot_bwd)


def hdot(a, b, mode="nn"):
    return lax.dot_general(a, b, _DIMS[mode], precision=HI, preferred_element_type=F32)


def sigmoid(x):
    return 1.0 / (1.0 + jnp.exp(-x))


def silu(x):
    return x * sigmoid(x)


def softplus(x):
    return jnp.maximum(x, 0.0) + jnp.log1p(jnp.exp(jnp.minimum(x, -x)))


def _ln(x):
    mu = jnp.mean(x, axis=-1, keepdims=True)
    xc = x - mu
    return xc * lax.rsqrt(jnp.mean(xc * xc, axis=-1, keepdims=True) + LN_EPS)


def _tril64():
    r = lax.broadcasted_iota(jnp.int32, (CHUNK, CHUNK), 0)
    c = lax.broadcasted_iota(jnp.int32, (CHUNK, CHUNK), 1)
    return (r >= c).astype(F32)


def hgrn_chunk(q, fl, iv, gr, st, lb, gn, tril):
    f = lb + (1.0 - lb) * sigmoid(fl)
    gl = jnp.log(f)
    k = 1.0 - f
    qf = silu(q) * (128 ** -0.5)
    b = hdot(tril, gl)
    blast = jnp.sum(gl, axis=0, keepdims=True)
    ref = lax.stop_gradient(0.5 * blast)
    qp = qf * jnp.exp(b - ref)
    kp = k * jnp.exp(ref - b)
    sc = hdot(qp, kp, "nt") * tril
    o = bdot(sc, iv, "nn") + bdot(qf * jnp.exp(b), st, "nt")
    st_new = st * jnp.exp(blast) + bdot(iv, k * jnp.exp(blast - b), "tn")
    on = o * lax.rsqrt(jnp.mean(o * o, axis=-1, keepdims=True) + RMS_EPS) * gn
    return on * silu(gr), st_new


def ssd_consts(g):
    i32 = jnp.int32
    ej = lax.broadcasted_iota(i32, (LANES, 512), 0)
    ec = lax.broadcasted_iota(i32, (LANES, 512), 1)
    expand = (ej == g * 8 + (ec >> 6)).astype(F32)
    ts = lax.broadcasted_iota(i32, (CHUNK, 512), 0)
    tc = lax.broadcasted_iota(i32, (CHUNK, 512), 1)
    itile = (ts == (tc & 63)).astype(F32)
    maskall = ts >= (tc & 63)
    br = lax.broadcasted_iota(i32, (512, 512), 0)
    bc = lax.broadcasted_iota(i32, (512, 512), 1)
    blockmask = ((br >> 6) == (bc >> 6)).astype(F32)
    return expand, itile, maskall, blockmask, _tril64()


def ssd_chunk(x, bm, cm, dt, z, st, dtb, alog, dsk, nw, cs):
    expand, itile, maskall, blockmask, tril = cs
    delta = softplus(hdot(dt, expand) + dtb)
    a = -jnp.exp(alog) * delta
    acum = hdot(tril, a)
    alast = jnp.sum(a, axis=0, keepdims=True)
    xdt = x * delta
    cb = hdot(bdot(cm, bm, "nt"), itile)
    arow = jnp.sum(acum * itile, axis=0, keepdims=True)
    dec = jnp.where(maskall, jnp.exp(jnp.minimum(acum - arow, 0.0)), 0.0)
    blockdiag = jnp.concatenate([xdt] * 8, axis=0) * blockmask
    y = bdot(cb * dec, blockdiag, "nn") + bdot(cm, st, "nn") * jnp.exp(acum)
    st_new = st * jnp.exp(alast) + bdot(bm, xdt * jnp.exp(alast - acum), "tn")
    yz = (y + x * dsk) * silu(z)
    return yz * lax.rsqrt(jnp.mean(yz * yz, axis=-1, keepdims=True) + RMS_EPS) * nw, st_new


def adamw(w, g, m, v):
    m = ADAM_B1 * m + (1.0 - ADAM_B1) * g
    v = ADAM_B2 * v + (1.0 - ADAM_B2) * jnp.square(g)
    m_hat = m / (1.0 - ADAM_B1 ** ADAM_STEP)
    v_hat = v / (1.0 - ADAM_B2 ** ADAM_STEP)
    return -ADAM_LR * (m_hat / (jnp.sqrt(v_hat) + ADAM_EPS) + ADAM_WD * w), m, v


def _pick(n, cands):
    for c in cands:
        if n % c == 0:
            return c
    return n


def _params(sem):
    return pltpu.CompilerParams(dimension_semantics=sem, vmem_limit_bytes=VMEM_LIMIT)


def matmul(a, b, mode, out_dtype, name):
    if mode == "nn":
        (m, k), n = a.shape, b.shape[1]
    elif mode == "nt":
        (m, k), n = a.shape, b.shape[0]
    else:
        (k, m), n = a.shape, b.shape[1]
    tm = _pick(m, (512, 256, 128))
    tn = _pick(n, (768, 512, 256, 128))
    tk = _pick(k, (1024, 768, 512, 256, 128))
    nk = k // tk
    a_spec = pl.BlockSpec((tk, tm), lambda i, j, kk: (kk, i)) if mode == "tn" else pl.BlockSpec((tm, tk), lambda i, j, kk: (i, kk))
    b_spec = pl.BlockSpec((tn, tk), lambda i, j, kk: (j, kk)) if mode == "nt" else pl.BlockSpec((tk, tn), lambda i, j, kk: (kk, j))

    def body(a_ref, b_ref, o_ref, acc_ref):
        kk = pl.program_id(2)

        @pl.when(kk == 0)
        def _():
            acc_ref[...] = jnp.zeros_like(acc_ref)

        acc_ref[...] += _bd(a_ref[...], b_ref[...], mode)

        @pl.when(kk == nk - 1)
        def _():
            o_ref[...] = acc_ref[...].astype(o_ref.dtype)

    return pl.pallas_call(
        body, name=name, grid=(m // tm, n // tn, nk),
        in_specs=[a_spec, b_spec], out_specs=pl.BlockSpec((tm, tn), lambda i, j, kk: (i, j)),
        out_shape=jax.ShapeDtypeStruct((m, n), out_dtype),
        scratch_shapes=[pltpu.VMEM((tm, tn), F32)],
        compiler_params=_params(("parallel", "parallel", "arbitrary")),
    )(a, b)


def rowwise(name, fn, rows, consts, out_rows, out_accs=(), tm_max=256):
    t = rows[0][0].shape[0]
    tm = _pick(t, (tm_max, 128, 64, 32, 16, 8))
    n_r, n_c, n_o = len(rows), len(consts), len(out_rows)

    def body(*refs):
        r_in = [r[...] for r in refs[:n_r]]
        c_in = [r[...] for r in refs[n_r:n_r + n_c]]
        o_refs = refs[n_r + n_c:n_r + n_c + n_o]
        a_refs = refs[n_r + n_c + n_o:]
        ro, ao = fn(r_in, c_in)
        for ref, val in zip(o_refs, ro, strict=True):
            ref[...] = val.astype(ref.dtype)
        if a_refs:
            @pl.when(pl.program_id(0) == 0)
            def _():
                for ref in a_refs:
                    ref[...] = jnp.zeros_like(ref)

            for ref, val in zip(a_refs, ao, strict=True):
                ref[...] += val

    in_specs = [pl.BlockSpec((tm, w), functools.partial(lambda i, cb: (i, cb), cb=cb)) for _, w, cb in rows]
    in_specs += [pl.BlockSpec(c.shape, lambda i: (0, 0)) for c in consts]
    out_specs = [pl.BlockSpec((tm, w), lambda i: (i, 0)) for w, _ in out_rows]
    out_specs += [pl.BlockSpec(s, lambda i: (0, 0)) for s in out_accs]
    out_shape = [jax.ShapeDtypeStruct((t, w), dt) for w, dt in out_rows]
    out_shape += [jax.ShapeDtypeStruct(s, F32) for s in out_accs]
    return pl.pallas_call(
        body, name=name, grid=(t // tm,), in_specs=in_specs, out_specs=out_specs, out_shape=out_shape,
        compiler_params=_params(("arbitrary",)),
    )(*[r[0] for r in rows], *consts)


def _full(a):
    return (a, a.shape[1], 0)


def _time_block(t):
    return _pick(t, (512, 256, 128, 64))


def hgrn_forward(proj, lb, gn):
    t = proj.shape[0]
    tb = _time_block(t)
    nb, nc = t // tb, tb // CHUNK

    def body(q_ref, f_ref, i_ref, g_ref, lb_ref, gn_ref, o_ref, st_ref, state):
        @pl.when(pl.program_id(1) == 0)
        def _():
            state[...] = jnp.zeros_like(state)

        tril = _tril64()
        lbv, gnv = lb_ref[...], gn_ref[...]

        def step(c, carry):
            sl = pl.ds(pl.multiple_of(c * CHUNK, CHUNK), CHUNK)
            st = state[...]
            st_ref[c] = st
            out, st_new = hgrn_chunk(q_ref[sl, :], f_ref[sl, :], i_ref[sl, :], g_ref[sl, :], st, lbv, gnv, tril)
            o_ref[sl, :] = out.astype(o_ref.dtype)
            state[...] = st_new
            return carry

        lax.fori_loop(0, nc, step, 0)

    col = lambda off: pl.BlockSpec((tb, LANES), functools.partial(lambda h, j, off: (j, off + h), off=off))
    return pl.pallas_call(
        body, name="hgrn_forward", grid=(N_HEADS_A, nb),
        in_specs=[col(0), col(8), col(16), col(24),
                  pl.BlockSpec((1, LANES), lambda h, j: (0, h)), pl.BlockSpec((1, LANES), lambda h, j: (0, 0))],
        out_specs=[pl.BlockSpec((tb, LANES), lambda h, j: (j, h)),
                   pl.BlockSpec((nc, None, LANES, LANES), lambda h, j: (j, h, 0, 0))],
        out_shape=[jax.ShapeDtypeStruct((t, D), BF16),
                   jax.ShapeDtypeStruct((t // CHUNK, N_HEADS_A, LANES, LANES), F32)],
        scratch_shapes=[pltpu.VMEM((LANES, LANES), F32)],
        compiler_params=_params(("arbitrary", "arbitrary")),
    )(proj, proj, proj, proj, lb, gn)


def hgrn_backward(proj, states, d_out, lb, gn):
    t = proj.shape[0]
    tb = _time_block(t)
    nb, nc = t // tb, tb // CHUNK

    def body(q_ref, f_ref, i_ref, g_ref, st_ref, do_ref, lb_ref, gn_ref,
             dq_ref, df_ref, di_ref, dg_ref, dlb_ref, dgn_ref, d_state):
        @pl.when(pl.program_id(1) == 0)
        def _():
            d_state[...] = jnp.zeros_like(d_state)
            dlb_ref[...] = jnp.zeros_like(dlb_ref)
            dgn_ref[...] = jnp.zeros_like(dgn_ref)

        tril = _tril64()
        lbv, gnv = lb_ref[...], gn_ref[...]

        def step(n, carry):
            c = nc - 1 - n
            sl = pl.ds(pl.multiple_of(c * CHUNK, CHUNK), CHUNK)
            fn = lambda q, fl, iv, gr, st, lb_, gn_: hgrn_chunk(q, fl, iv, gr, st, lb_, gn_, tril)
            _, vjp = jax.vjp(fn, q_ref[sl, :], f_ref[sl, :], i_ref[sl, :], g_ref[sl, :], st_ref[c], lbv, gnv)
            dq, df, di, dg, dst, dlb, dgn = vjp((do_ref[sl, :], d_state[...]))
            dq_ref[sl, :] = dq.astype(dq_ref.dtype)
            df_ref[sl, :] = df.astype(df_ref.dtype)
            di_ref[sl, :] = di.astype(di_ref.dtype)
            dg_ref[sl, :] = dg.astype(dg_ref.dtype)
            d_state[...] = dst
            dlb_ref[...] += dlb
            dgn_ref[...] += dgn
            return carry

        lax.fori_loop(0, nc, step, 0)

    col = lambda off: pl.BlockSpec((tb, LANES), functools.partial(lambda h, j, off: (nb - 1 - j, off + h), off=off))
    out_col = pl.BlockSpec((tb, LANES), lambda h, j: (nb - 1 - j, h))
    acc = pl.BlockSpec((None, 1, LANES), lambda h, j: (h, 0, 0))
    return pl.pallas_call(
        body, name="hgrn_backward", grid=(N_HEADS_A, nb),
        in_specs=[col(0), col(8), col(16), col(24),
                  pl.BlockSpec((nc, None, LANES, LANES), lambda h, j: (nb - 1 - j, h, 0, 0)),
                  pl.BlockSpec((tb, LANES), lambda h, j: (nb - 1 - j, h)),
                  pl.BlockSpec((1, LANES), lambda h, j: (0, h)), pl.BlockSpec((1, LANES), lambda h, j: (0, 0))],
        out_specs=[out_col, out_col, out_col, out_col, acc, acc],
        out_shape=[jax.ShapeDtypeStruct((t, D), BF16)] * 4 + [jax.ShapeDtypeStruct((N_HEADS_A, 1, LANES), F32)] * 2,
        scratch_shapes=[pltpu.VMEM((LANES, LANES), F32)],
        compiler_params=_params(("arbitrary", "arbitrary")),
    )(proj, proj, proj, proj, states, d_out, lb, gn)


def _ssd_in_specs(tb, tmap):
    return [pl.BlockSpec((tb, 512), lambda g, j: (tmap(j), g)),
            pl.BlockSpec((tb, LANES), lambda g, j: (tmap(j), 16 + g)),
            pl.BlockSpec((tb, LANES), lambda g, j: (tmap(j), 20 + g)),
            pl.BlockSpec((tb, LANES), lambda g, j: (tmap(j), COL_DT // LANES)),
            pl.BlockSpec((tb, 512), lambda g, j: (tmap(j), COL_Z // 512 + g))]


def ssd_forward(xc, proj, dtb, alog, dsk, nw):
    t = proj.shape[0]
    tb = _time_block(t)
    nb, nc = t // tb, tb // CHUNK

    def body(x_ref, b_ref, c_ref, dt_ref, z_ref, dtb_ref, alog_ref, dsk_ref, nw_ref, o_ref, st_ref, state):
        @pl.when(pl.program_id(1) == 0)
        def _():
            state[...] = jnp.zeros_like(state)

        cs = ssd_consts(pl.program_id(0))
        par = (dtb_ref[...], alog_ref[...], dsk_ref[...], nw_ref[...])

        def step(c, carry):
            sl = pl.ds(pl.multiple_of(c * CHUNK, CHUNK), CHUNK)
            st = state[...]
            st_ref[c] = st
            out, st_new = ssd_chunk(x_ref[sl, :], b_ref[sl, :], c_ref[sl, :], dt_ref[sl, :], z_ref[sl, :], st, *par, cs)
            o_ref[sl, :] = out.astype(o_ref.dtype)
            state[...] = st_new
            return carry

        lax.fori_loop(0, nc, step, 0)

    vec = pl.BlockSpec((1, 512), lambda g, j: (0, g))
    return pl.pallas_call(
        body, name="ssd_forward", grid=(N_GROUPS_B, nb),
        in_specs=_ssd_in_specs(tb, lambda j: j) + [vec] * 4,
        out_specs=[pl.BlockSpec((tb, 512), lambda g, j: (j, g)),
                   pl.BlockSpec((nc, None, LANES, 512), lambda g, j: (j, g, 0, 0))],
        out_shape=[jax.ShapeDtypeStruct((t, B_INNER), BF16),
                   jax.ShapeDtypeStruct((t // CHUNK, N_GROUPS_B, LANES, 512), F32)],
        scratch_shapes=[pltpu.VMEM((LANES, 512), F32)],
        compiler_params=_params(("arbitrary", "arbitrary")),
    )(xc, xc, xc, proj, proj, dtb, alog, dsk, nw)


def ssd_backward(xc, proj, states, d_out, dtb, alog, dsk, nw):
    t = proj.shape[0]
    tb = _time_block(t)
    nb, nc = t // tb, tb // CHUNK
    rev = lambda j: nb - 1 - j

    def body(x_ref, b_ref, c_ref, dt_ref, z_ref, st_ref, do_ref, dtb_ref, alog_ref, dsk_ref, nw_ref,
             dx_ref, db_ref, dc_ref, ddt_ref, dz_ref, ddtb_ref, dalog_ref, ddsk_ref, dnw_ref, d_state):
        accs = (ddtb_ref, dalog_ref, ddsk_ref, dnw_ref)

        @pl.when(pl.program_id(1) == 0)
        def _():
            d_state[...] = jnp.zeros_like(d_state)
            for ref in accs:
                ref[...] = jnp.zeros_like(ref)

        cs = ssd_consts(pl.program_id(0))
        par = (dtb_ref[...], alog_ref[...], dsk_ref[...], nw_ref[...])

        def step(n, carry):
            c = nc - 1 - n
            sl = pl.ds(pl.multiple_of(c * CHUNK, CHUNK), CHUNK)
            fn = lambda *a: ssd_chunk(*a, cs)
            _, vjp = jax.vjp(fn, x_ref[sl, :], b_ref[sl, :], c_ref[sl, :], dt_ref[sl, :], z_ref[sl, :], st_ref[c], *par)
            dx, db, dc, ddt, dz, dst, *dpar = vjp((do_ref[sl, :], d_state[...]))
            dx_ref[sl, :] = dx
            db_ref[sl, :] = db
            dc_ref[sl, :] = dc
            ddt_ref[sl, :] = ddt
            dz_ref[sl, :] = dz.astype(dz_ref.dtype)
            d_state[...] = dst
            for ref, val in zip(accs, dpar, strict=True):
                ref[...] += val
            return carry

        lax.fori_loop(0, nc, step, 0)

    vec = pl.BlockSpec((1, 512), lambda g, j: (0, g))
    acc = pl.BlockSpec((None, 1, 512), lambda g, j: (g, 0, 0))
    return pl.pallas_call(
        body, name="ssd_backward", grid=(N_GROUPS_B, nb),
        in_specs=_ssd_in_specs(tb, rev)
        + [pl.BlockSpec((nc, None, LANES, 512), lambda g, j: (rev(j), g, 0, 0)),
           pl.BlockSpec((tb, 512), lambda g, j: (rev(j), g))] + [vec] * 4,
        out_specs=[pl.BlockSpec((tb, 512), lambda g, j: (rev(j), g)),
                   pl.BlockSpec((tb, LANES), lambda g, j: (rev(j), g)),
                   pl.BlockSpec((tb, LANES), lambda g, j: (rev(j), g)),
                   pl.BlockSpec((None, tb, LANES), lambda g, j: (g, rev(j), 0)),
                   pl.BlockSpec((tb, 512), lambda g, j: (rev(j), g)), acc, acc, acc, acc],
        out_shape=[jax.ShapeDtypeStruct((t, B_INNER), F32), jax.ShapeDtypeStruct((t, 512), F32),
                   jax.ShapeDtypeStruct((t, 512), F32), jax.ShapeDtypeStruct((N_GROUPS_B, t, LANES), F32),
                   jax.ShapeDtypeStruct((t, B_INNER), BF16)] + [jax.ShapeDtypeStruct((N_GROUPS_B, 1, 512), F32)] * 4,
        scratch_shapes=[pltpu.VMEM((LANES, 512), F32)],
        compiler_params=_params(("arbitrary", "arbitrary")),
    )(xc, xc, xc, proj, proj, states, d_out, dtb, alog, dsk, nw)


CONV_HALO = 8


def _shift_down(halo_then_tile, s, tm):
    if s == 0:
        return halo_then_tile[CONV_HALO:CONV_HALO + tm]
    return pltpu.roll(halo_then_tile, s, 0)[CONV_HALO:CONV_HALO + tm]


def _conv_pre(cur, prev, w, b, tm):
    stacked = jnp.concatenate([prev, cur], axis=0)
    taps = [_shift_down(stacked, 3 - j, tm) for j in range(4)]
    pre = b + taps[0] * w[0:1] + taps[1] * w[1:2] + taps[2] * w[2:3] + taps[3] * w[3:4]
    return pre, taps


def _conv_specs(t, tm):
    per = tm // CONV_HALO
    cur = pl.BlockSpec((tm, CONV_DIM), lambda i: (i, COL_XBC // CONV_DIM))
    prev = pl.BlockSpec((CONV_HALO, CONV_DIM), lambda i: (jnp.maximum(i * per - 1, 0), COL_XBC // CONV_DIM))
    return cur, prev


def conv_forward(proj, w, b):
    t = proj.shape[0]
    tm = _pick(t, (256, 128, 64))

    def body(cur_ref, prev_ref, w_ref, b_ref, o_ref):
        prev = jnp.where(pl.program_id(0) == 0, 0.0, prev_ref[...])
        pre, _ = _conv_pre(cur_ref[...], prev, w_ref[...], b_ref[...], tm)
        o_ref[...] = silu(pre)

    cur, prev = _conv_specs(t, tm)
    return pl.pallas_call(
        body, name="conv_forward", grid=(t // tm,),
        in_specs=[cur, prev, pl.BlockSpec((4, CONV_DIM), lambda i: (0, 0)), pl.BlockSpec((1, CONV_DIM), lambda i: (0, 0))],
        out_specs=pl.BlockSpec((tm, CONV_DIM), lambda i: (i, 0)),
        out_shape=jax.ShapeDtypeStruct((t, CONV_DIM), F32),
        compiler_params=_params(("arbitrary",)),
    )(proj, proj, w, b)


def conv_backward_pre(proj, dx, db_, dc_, w, b):
    t = proj.shape[0]
    tm = _pick(t, (256, 128, 64))

    def body(cur_ref, prev_ref, dx_ref, dbm_ref, dcm_ref, w_ref, b_ref, dpre_ref, dw_ref, dbias_ref):
        @pl.when(pl.program_id(0) == 0)
        def _():
            dw_ref[...] = jnp.zeros_like(dw_ref)
            dbias_ref[...] = jnp.zeros_like(dbias_ref)

        prev = jnp.where(pl.program_id(0) == 0, 0.0, prev_ref[...])
        pre, taps = _conv_pre(cur_ref[...], prev, w_ref[...], b_ref[...], tm)
        sg = sigmoid(pre)
        d_out = jnp.concatenate([dx_ref[...], dbm_ref[...], dcm_ref[...]], axis=1)
        dpre = d_out * (sg * (1.0 + pre * (1.0 - sg)))
        dpre_ref[...] = dpre
        dbias_ref[...] += jnp.sum(dpre, axis=0, keepdims=True)
        for j in range(4):
            dw_ref[j:j + 1, :] += jnp.sum(dpre * taps[j], axis=0, keepdims=True)

    cur, prev = _conv_specs(t, tm)
    row = lambda w_: pl.BlockSpec((tm, w_), lambda i: (i, 0))
    return pl.pallas_call(
        body, name="conv_backward_pre", grid=(t // tm,),
        in_specs=[cur, prev, row(B_INNER), row(512), row(512),
                  pl.BlockSpec((4, CONV_DIM), lambda i: (0, 0)), pl.BlockSpec((1, CONV_DIM), lambda i: (0, 0))],
        out_specs=[row(CONV_DIM), pl.BlockSpec((4, CONV_DIM), lambda i: (0, 0)), pl.BlockSpec((1, CONV_DIM), lambda i: (0, 0))],
        out_shape=[jax.ShapeDtypeStruct((t, CONV_DIM), F32), jax.ShapeDtypeStruct((4, CONV_DIM), F32),
                   jax.ShapeDtypeStruct((1, CONV_DIM), F32)],
        compiler_params=_params(("arbitrary",)),
    )(proj, proj, dx, db_, dc_, w, b)


def conv_backward_input(dpre, w):
    t = dpre.shape[0]
    tm = _pick(t, (256, 128, 64))
    per = tm // CONV_HALO
    last = t // CONV_HALO - 1
    nt = t // tm

    def body(cur_ref, nxt_ref, w_ref, o_ref):
        nxt = jnp.where(pl.program_id(0) == nt - 1, 0.0, nxt_ref[...])
        stacked = jnp.concatenate([cur_ref[...], nxt], axis=0)
        w_ = w_ref[...]
        acc = stacked[0:tm] * w_[3:4]
        for j in range(3):
            s = 3 - j
            acc = acc + pltpu.roll(stacked, tm + CONV_HALO - s, 0)[0:tm] * w_[j:j + 1]
        o_ref[...] = acc.astype(o_ref.dtype)

    return pl.pallas_call(
        body, name="conv_backward_input", grid=(nt,),
        in_specs=[pl.BlockSpec((tm, CONV_DIM), lambda i: (i, 0)),
                  pl.BlockSpec((CONV_HALO, CONV_DIM), lambda i: (jnp.minimum((i + 1) * per, last), 0)),
                  pl.BlockSpec((4, CONV_DIM), lambda i: (0, 0))],
        out_specs=pl.BlockSpec((tm, CONV_DIM), lambda i: (i, 0)),
        out_shape=jax.ShapeDtypeStruct((t, CONV_DIM), BF16),
        compiler_params=_params(("arbitrary",)),
    )(dpre, dpre, w)


def stage_modulate(x, sc, sh):
    return _ln(x) * (1.0 + sc) + sh


def stage_merge(ga, gb, ya, yb):
    return sigmoid(ga) * ya + sigmoid(gb) * yb


def stage_post_mixer(x, h, g1, ln_g, ln_b, sc2, sh2):
    x1 = _ln(ALPHA * x + g1 * h) * ln_g + ln_b
    return x1, _ln(x1) * (1.0 + sc2) + sh2


def stage_swiglu(a, b):
    return silu(a) * b


def stage_loss(x1, hf, tgt, g2, ln_g, ln_b):
    x2 = _ln(ALPHA * x1 + g2 * hf) * ln_g + ln_b
    return 0.5 * jnp.sum(jnp.mean(jnp.square(x2 - tgt), axis=-1, keepdims=True), axis=0, keepdims=True)


def local_step(x, tgt, mod, wts, small):
    sh1, sc1, g1, sh2, sc2, g2 = mod
    w_in, w_a, w_b, w_o, w_gu, w_d = wts
    lb, gn, conv_w, conv_b, dtb, alog, dsk, nw, ln1_g, ln1_b, ln2_g, ln2_b = small
    vec = (1, D)

    (u1,) = rowwise("modulate1", lambda r, c: ((stage_modulate(r[0], *c),), ()), [_full(x)], [sc1, sh1], [(D, BF16)])
    proj = matmul(u1, w_in, "nn", F32, "in_proj")
    ya_in, st_a = hgrn_forward(proj, lb, gn)
    xc = conv_forward(proj, conv_w, conv_b)
    yb_in, st_b = ssd_forward(xc, proj, dtb, alog, dsk, nw)
    ya = matmul(ya_in, w_a, "nn", F32, "branch_a")
    yb = matmul(yb_in, w_b, "nn", F32, "branch_b")
    gate_rows = [(proj, D, COL_GA // D), (proj, D, COL_GB // D), _full(ya), _full(yb)]
    (merged,) = rowwise("merge", lambda r, c: ((stage_merge(*r),), ()), gate_rows, [], [(D, BF16)])
    h = matmul(merged, w_o, "nn", F32, "out_proj")
    post_consts = [g1, ln1_g, ln1_b, sc2, sh2]
    x1, u2 = rowwise("post_mixer", lambda r, c: (stage_post_mixer(*r, *c), ()), [_full(x), _full(h)], post_consts,
                     [(D, F32), (D, BF16)])
    ab = matmul(u2, w_gu, "nn", F32, "ffn_in")
    (p,) = rowwise("swiglu", lambda r, c: ((stage_swiglu(*r),), ()), [(ab, D_FF, 0), (ab, D_FF, 1)], [], [(D_FF, BF16)])
    hf = matmul(p, w_d, "nn", F32, "ffn_out")

    def loss_bwd(r, c):
        loss, vjp = jax.vjp(stage_loss, *r, *c)
        dx1, dhf, _, dg2, dlg, dlb_ = vjp(jnp.ones((1, 1), F32))
        return (dx1, dhf), (loss, dg2, dlg, dlb_)

    dx1, dhf, loss, dg2, dln2_g, dln2_b = rowwise(
        "loss_backward", loss_bwd, [_full(x1), _full(hf), _full(tgt)], [g2, ln2_g, ln2_b],
        [(D, F32), (D, BF16)], [(1, 1), vec, vec, vec])
    dp = matmul(dhf, w_d, "nt", F32, "ffn_out_dx")
    dw_d = matmul(p, dhf, "tn", F32, "ffn_out_dw")

    def swiglu_bwd(r, c):
        _, vjp = jax.vjp(stage_swiglu, r[0], r[1])
        da, db_ = vjp(r[2])
        return (jnp.concatenate([da, db_], axis=1),), ()

    (dab,) = rowwise("swiglu_backward", swiglu_bwd, [(ab, D_FF, 0), (ab, D_FF, 1), _full(dp)], [], [(2 * D_FF, BF16)])
    du2 = matmul(dab, w_gu, "nt", F32, "ffn_in_dx")
    dw_gu = matmul(u2, dab, "tn", F32, "ffn_in_dw")

    def post_bwd(r, c):
        _, vjp = jax.vjp(stage_post_mixer, r[0], r[1], *c)
        dx, dh, *dc = vjp((r[2], r[3]))
        return (dx, dh), tuple(dc)

    dx_a, dh, dg1, dln1_g, dln1_b, dsc2, dsh2 = rowwise(
        "post_mixer_backward", post_bwd, [_full(x), _full(h), _full(dx1), _full(du2)], post_consts,
        [(D, F32), (D, BF16)], [vec] * 5)
    dmerged = matmul(dh, w_o, "nt", F32, "out_proj_dx")
    dw_o = matmul(merged, dh, "tn", F32, "out_proj_dw")

    def merge_bwd(r, c):
        _, vjp = jax.vjp(stage_merge, *r[:4])
        return vjp(r[4]), ()

    dga, dgb, dya, dyb = rowwise("merge_backward", merge_bwd, gate_rows + [_full(dmerged)], [],
                                 [(D, BF16)] * 4)
    dya_in = matmul(dya, w_a, "nt", F32, "branch_a_dx")
    dw_a = matmul(ya_in, dya, "tn", F32, "branch_a_dw")
    dyb_in = matmul(dyb, w_b, "nt", F32, "branch_b_dx")
    dw_b = matmul(yb_in, dyb, "tn", F32, "branch_b_dw")
    dq, df, di, dg, dlb, dgn = hgrn_backward(proj, st_a, dya_in, lb, gn)
    dxs, dbm, dcm, ddt, dz, ddtb, dalog, ddsk, dnw = ssd_backward(xc, proj, st_b, dyb_in, dtb, alog, dsk, nw)
    dpre, dconv_w, dconv_b = conv_backward_pre(proj, dxs, dbm, dcm, conv_w, conv_b)
    dxbc = conv_backward_input(dpre, conv_w)
    t = x.shape[0]
    dproj = jnp.concatenate([dq, df, di, dg, dz, dxbc, dga, dgb, jnp.sum(ddt, axis=0).astype(BF16),
                             jnp.zeros((t, IN_PAD - COL_DT - LANES), BF16)], axis=1)
    du1 = matmul(dproj, w_in, "nt", F32, "in_proj_dx")
    dw_in = matmul(u1, dproj, "tn", F32, "in_proj_dw")

    def mod_bwd(r, c):
        _, vjp = jax.vjp(stage_modulate, r[0], *c)
        dx, dsc, dsh = vjp(r[1])
        return (dx + r[2],), (dsc, dsh)

    grad_x, dsc1, dsh1 = rowwise("modulate1_backward", mod_bwd, [_full(x), _full(du1), _full(dx_a)], [sc1, sh1],
                                 [(D, F32)], [vec, vec])
    d_mod = (dsh1, dsc1, dg1, dsh2, dsc2, dg2)
    d_wts = (dw_in, dw_a, dw_b, dw_o, dw_gu, dw_d)
    d_small = (dlb.reshape(1, D), jnp.sum(dgn, axis=0), dconv_w, dconv_b, ddtb.reshape(1, B_INNER),
               dalog.reshape(1, B_INNER), ddsk.reshape(1, B_INNER), dnw.reshape(1, B_INNER),
               dln1_g, dln1_b, dln2_g, dln2_b)
    return loss, grad_x, d_mod, d_wts, d_small


HBM = pl.BlockSpec(memory_space=pltpu.HBM)


def _place():
    return lax.axis_index("x"), lax.axis_index("y"), lax.axis_index("c")


def _other_chips(x, y):
    return [(1 - x, y), (x, 1 - y), (1 - x, 1 - y)]


def _remote(src, dst, send_sem, recv_sem, device):
    return pltpu.make_async_remote_copy(src_ref=src, dst_ref=dst, send_sem=send_sem, recv_sem=recv_sem,
                                        device_id=device, device_id_type=MESH)


def gather_rows(v, name):
    n = v.shape[1]

    def body(v_ref, out_ref, send_sems, recv_sems, local_sem):
        x, y, c = _place()
        mine = pltpu.make_async_copy(v_ref, out_ref.at[4 * x + 2 * y + c], local_sem)
        mine.start()
        sends, recvs = [], []
        for m in range(1, 8):
            px = 1 - x if m & 4 else x
            py = 1 - y if m & 2 else y
            pc = 1 - c if m & 1 else c
            sends.append(_remote(v_ref, out_ref.at[4 * x + 2 * y + c], send_sems.at[m - 1], recv_sems.at[m - 1], (px, py, pc)))
            recvs.append(_remote(v_ref, out_ref.at[4 * px + 2 * py + pc], send_sems.at[m - 1], recv_sems.at[m - 1], (px, py, pc)))
        for cp in sends:
            cp.start()
        for cp in recvs:
            cp.wait_recv()
        for cp in sends:
            cp.wait_send()
        mine.wait()

    return pl.pallas_call(
        body, name=name, in_specs=[HBM], out_specs=HBM,
        out_shape=jax.ShapeDtypeStruct((8, 1, n), v.dtype),
        scratch_shapes=[pltpu.SemaphoreType.DMA((7,)), pltpu.SemaphoreType.DMA((7,)), pltpu.SemaphoreType.DMA],
    )(v)


def exchange_rows(part, name):
    w = part.shape[2]

    def body(p_ref, out_ref, send_sems, recv_sems, local_sem):
        x, y, c = _place()
        k = 2 * x + y
        mine = pltpu.make_async_copy(p_ref.at[4 * x + 2 * y + c], out_ref.at[k], local_sem)
        mine.start()
        sends, recvs = [], []
        for j, (px, py) in enumerate(_other_chips(x, y)):
            sends.append(_remote(p_ref.at[4 * px + 2 * py + c], out_ref.at[k], send_sems.at[j], recv_sems.at[j], (px, py, c)))
            recvs.append(_remote(p_ref.at[4 * px + 2 * py + c], out_ref.at[2 * px + py], send_sems.at[j], recv_sems.at[j], (px, py, c)))
        for cp in sends:
            cp.start()
        for cp in recvs:
            cp.wait_recv()
        for cp in sends:
            cp.wait_send()
        mine.wait()

    return pl.pallas_call(
        body, name=name, in_specs=[HBM], out_specs=HBM,
        out_shape=jax.ShapeDtypeStruct((4, 1, w), part.dtype),
        scratch_shapes=[pltpu.SemaphoreType.DMA((3,)), pltpu.SemaphoreType.DMA((3,)), pltpu.SemaphoreType.DMA],
    )(part)


def gather_weights(wp):
    r = wp.shape[0]
    hr = r // 2

    def body(w_ref, out_ref, send_sems, recv_sems, local_sem):
        x, y, c = _place()
        half = lambda px, py, pc: out_ref.at[2 * px + py, pl.ds(pc * hr, hr), :]
        mine = pltpu.make_async_copy(w_ref, out_ref.at[2 * x + y], local_sem)
        mine.start()
        chips = _other_chips(x, y)
        first = [_remote(w_ref.at[pl.ds(c * hr, hr), :], half(x, y, c), send_sems.at[j], recv_sems.at[j], (px, py, c))
                 for j, (px, py) in enumerate(chips)]
        for cp in first:
            cp.start()
        passed = []
        for j, (px, py) in enumerate(chips):
            _remote(half(px, py, c), half(px, py, c), send_sems.at[j], recv_sems.at[j], (px, py, c)).wait_recv()
            cp = _remote(half(px, py, c), half(px, py, c), send_sems.at[3 + j], recv_sems.at[3 + j], (x, y, 1 - c))
            cp.start()
            passed.append(cp)
        for j, (px, py) in enumerate(chips):
            _remote(half(px, py, 1 - c), half(px, py, 1 - c), send_sems.at[3 + j], recv_sems.at[3 + j], (x, y, 1 - c)).wait_recv()
        for cp in first + passed:
            cp.wait_send()
        mine.wait()

    return pl.pallas_call(
        body, name="gather_weights", in_specs=[HBM], out_specs=HBM,
        out_shape=jax.ShapeDtypeStruct((4, r, LANES), wp.dtype),
        scratch_shapes=[pltpu.SemaphoreType.DMA((6,)), pltpu.SemaphoreType.DMA((6,)), pltpu.SemaphoreType.DMA],
    )(wp)


def pair_exchange(g):
    r = g.shape[1]
    hr = r // 2

    def body(g_ref, out_ref, send_sem, recv_sem):
        x, y, c = _place()
        cp = _remote(g_ref.at[:, pl.ds((1 - c) * hr, hr), :], out_ref, send_sem, recv_sem, (x, y, 1 - c))
        cp.start()
        cp.wait()

    return pl.pallas_call(
        body, name="pair_exchange", in_specs=[HBM], out_specs=HBM,
        out_shape=jax.ShapeDtypeStruct((4, hr, LANES), g.dtype),
        scratch_shapes=[pltpu.SemaphoreType.DMA, pltpu.SemaphoreType.DMA],
    )(g)


def pair_add(g, p, c):
    hr = p.shape[1]
    tm = _pick(hr, (2048, 1024, 512, 256, 128, 64, 32, 16))
    per = hr // tm

    def body(c_ref, g_ref, p_ref, o_ref):
        o_ref[...] = (g_ref[...] + p_ref[...]).astype(o_ref.dtype)

    return pl.pallas_call(
        body, name="pair_add",
        grid_spec=pltpu.PrefetchScalarGridSpec(
            num_scalar_prefetch=1, grid=(4, per),
            in_specs=[pl.BlockSpec((None, tm, LANES), lambda k, i, c_ref: (k, c_ref[0] * per + i, 0)),
                      pl.BlockSpec((None, tm, LANES), lambda k, i, c_ref: (k, i, 0))],
            out_specs=pl.BlockSpec((None, tm, LANES), lambda k, i, c_ref: (k, i, 0))),
        out_shape=jax.ShapeDtypeStruct((4, hr, LANES), BF16),
        compiler_params=_params(("arbitrary", "arbitrary")),
    )(c.reshape(1).astype(jnp.int32), g, p)


def scatter_sums(s):
    hr = s.shape[1]

    def body(s_ref, out_ref, send_sems, recv_sems, local_sem):
        x, y, c = _place()
        k = 2 * x + y
        mine = pltpu.make_async_copy(s_ref.at[k], out_ref.at[k], local_sem)
        mine.start()
        sends, recvs = [], []
        for j, (px, py) in enumerate(_other_chips(x, y)):
            sends.append(_remote(s_ref.at[2 * px + py], out_ref.at[k], send_sems.at[j], recv_sems.at[j], (px, py, c)))
            recvs.append(_remote(s_ref.at[2 * px + py], out_ref.at[2 * px + py], send_sems.at[j], recv_sems.at[j], (px, py, c)))
        for cp in sends:
            cp.start()
        for cp in recvs:
            cp.wait_recv()
        for cp in sends:
            cp.wait_send()
        mine.wait()

    return pl.pallas_call(
        body, name="scatter_sums", in_specs=[HBM], out_specs=HBM,
        out_shape=jax.ShapeDtypeStruct((4, hr, LANES), s.dtype),
        scratch_shapes=[pltpu.SemaphoreType.DMA((3,)), pltpu.SemaphoreType.DMA((3,)), pltpu.SemaphoreType.DMA],
    )(s)


def sum_chips(rb):
    hr = rb.shape[1]
    tm = _pick(hr, (2048, 1024, 512, 256, 128, 64, 32, 16))

    def body(r_ref, o_ref):
        v = r_ref[...].astype(F32)
        o_ref[...] = ((v[0] + v[1]) + v[2]) + v[3]

    return pl.pallas_call(
        body, name="sum_chips", grid=(hr // tm,),
        in_specs=[pl.BlockSpec((4, tm, LANES), lambda i: (0, i, 0))],
        out_specs=pl.BlockSpec((tm, LANES), lambda i: (i, 0)),
        out_shape=jax.ShapeDtypeStruct((hr, LANES), F32),
        compiler_params=_params(("arbitrary",)),
    )(rb)


def exchange_halves(half):
    hr = half.shape[0]

    def body(h_ref, out_ref, send_sem, recv_sem, local_sem):
        x, y, c = _place()
        mine = pltpu.make_async_copy(h_ref, out_ref.at[pl.ds(c * hr, hr), :], local_sem)
        mine.start()
        send = _remote(h_ref, out_ref.at[pl.ds(c * hr, hr), :], send_sem, recv_sem, (x, y, 1 - c))
        send.start()
        _remote(h_ref, out_ref.at[pl.ds((1 - c) * hr, hr), :], send_sem, recv_sem, (x, y, 1 - c)).wait_recv()
        send.wait_send()
        mine.wait()

    return pl.pallas_call(
        body, name="exchange_halves", in_specs=[HBM], out_specs=HBM,
        out_shape=jax.ShapeDtypeStruct((2 * hr, LANES), half.dtype),
        scratch_shapes=[pltpu.SemaphoreType.DMA, pltpu.SemaphoreType.DMA, pltpu.SemaphoreType.DMA],
    )(half)


def ada_prepare(c_all, w_ada, hgrn_lb):
    def body(c_ref, w_ref, lb_ref, mod_ref, row_ref):
        mod_ref[...] = hdot(silu(c_ref[...]), w_ref[...])
        row_ref[...] = sigmoid(lb_ref[0:1, :] - lb_ref[1:2, :])

    return pl.pallas_call(
        body, name="ada_prepare",
        out_shape=[jax.ShapeDtypeStruct((8, w_ada.shape[1]), F32), jax.ShapeDtypeStruct((1, D), F32)],
        compiler_params=pltpu.CompilerParams(vmem_limit_bytes=VMEM_LIMIT),
    )(c_all, w_ada, hgrn_lb)


SMALL_SEGS = (("mod", 6 * D), ("lb", D), ("gnorm", LANES), ("conv_w", 4 * CONV_DIM), ("conv_b", CONV_DIM),
              ("dt_bias", B_INNER), ("a_log", B_INNER), ("d", B_INNER), ("ssm_norm", B_INNER),
              ("ln1_g", D), ("ln1_b", D), ("ln2_g", D), ("ln2_b", D))
SMALL_PARAMS = ("b_ada", "hgrn_lb", "hgrn_gnorm", "ssm_conv_b", "ssm_dt_bias", "ssm_a_log", "ssm_d", "ssm_norm",
                "ln1_g", "ln1_b", "ln2_g", "ln2_b")


def finalize_small(g_all, c_all, dmod_cols, params, m, v):
    n_p = len(SMALL_PARAMS)
    offs, o = {}, 0
    for nm, width in SMALL_SEGS:
        offs[nm] = (o, width)
        o += width

    def body(*refs):
        g_ref, c_ref, dm_ref = refs[:3]
        p_refs = refs[3:3 + n_p]
        m_refs = refs[3 + n_p:3 + 2 * n_p]
        v_refs = refs[3 + 2 * n_p:3 + 3 * n_p]
        outs = refs[3 + 3 * n_p:]
        gwa_ref, gcw_ref = outs[:2]
        res = outs[2:]
        total = jnp.sum(g_ref[...], axis=0, keepdims=True)
        seg = lambda nm: total[:, offs[nm][0]:offs[nm][0] + offs[nm][1]]
        gwa_ref[...] = hdot(silu(c_ref[...]), dm_ref[...], "tn")
        cw = seg("conv_w")
        for j in range(4):
            gcw_ref[j:j + 1, :] = cw[:, j * CONV_DIM:(j + 1) * CONV_DIM]
        hc = lax.broadcasted_iota(jnp.int32, (B_INNER, LANES), 0)
        hj = lax.broadcasted_iota(jnp.int32, (B_INNER, LANES), 1)
        per_head = ((hc >> 6) == hj).astype(F32)
        heads = lambda nm: hdot(jnp.broadcast_to(seg(nm), (8, B_INNER)), per_head)[0:1, 0:32]
        lbp = sigmoid(p_refs[1][0:1, :] - p_refs[1][1:2, :])
        g_row = seg("lb") * lbp * (1.0 - lbp)
        grads = {"b_ada": seg("mod"), "hgrn_gnorm": seg("gnorm"), "ssm_conv_b": seg("conv_b"),
                 "ssm_dt_bias": heads("dt_bias"), "ssm_a_log": heads("a_log"), "ssm_d": heads("d"),
                 "ssm_norm": seg("ssm_norm"), "ln1_g": seg("ln1_g"), "ln1_b": seg("ln1_b"),
                 "ln2_g": seg("ln2_g"), "ln2_b": seg("ln2_b")}
        for i, nm in enumerate(SMALL_PARAMS):
            g_out, d_out, m_out, v_out = res[4 * i:4 * i + 4]
            if nm == "hgrn_lb":
                for row, gv in ((0, g_row), (1, -g_row)):
                    sl = slice(row, row + 1)
                    dl, mn, vn = adamw(p_refs[i][sl, :], gv, m_refs[i][sl, :], v_refs[i][sl, :])
                    g_out[sl, :], d_out[sl, :], m_out[sl, :], v_out[sl, :] = gv, dl, mn, vn
            else:
                gv = grads[nm]
                dl, mn, vn = adamw(p_refs[i][...], gv, m_refs[i][...], v_refs[i][...])
                g_out[...], d_out[...], m_out[...], v_out[...] = gv, dl, mn, vn

    out_shape = [jax.ShapeDtypeStruct((D, dmod_cols.shape[1]), F32), jax.ShapeDtypeStruct((4, CONV_DIM), F32)]
    for p in params:
        out_shape += [jax.ShapeDtypeStruct(p.shape, F32)] * 4
    return pl.pallas_call(
        body, name="finalize_small", out_shape=out_shape,
        compiler_params=pltpu.CompilerParams(vmem_limit_bytes=VMEM_LIMIT),
    )(g_all, c_all, dmod_cols, *params, *m, *v)


def adam_update(w, g, m, v, name):
    cols = w.shape[1]
    return rowwise(name, lambda r, c: (adamw(*r), ()), [_full(w), _full(g), _full(m), _full(v)], [],
                   [(cols, F32)] * 3, tm_max=128)


def _pack_rows(parts, dtype):
    rows = sum(p.shape[-2] for p in parts)
    pad = jnp.zeros(parts[0].shape[:-2] + (PACK_ROWS - rows, LANES), dtype)
    return jnp.concatenate([p.astype(dtype) for p in parts] + [pad], axis=-2)


def _to_padded_cols(w):
    return jnp.concatenate([w[:, :ORIG_DT], w[:, ORIG_DT + 32:], w[:, ORIG_DT:ORIG_DT + 32],
                            jnp.zeros((w.shape[0], IN_PAD - IN_ORIG), w.dtype)], axis=1)


def _from_padded_cols(w):
    return jnp.concatenate([w[:, :COL_GA], w[:, COL_DT:COL_DT + 32], w[:, COL_GA:COL_DT]], axis=1)


def kernel(x, c, w_ada, b_ada, w_in, hgrn_lb, hgrn_gnorm, ssm_conv_w, ssm_conv_b, ssm_dt_bias, ssm_a_log, ssm_d, ssm_norm, w_branch_a, w_branch_b, w_o, ln1_g, ln1_b, w_ffn_gate, w_ffn_up, w_ffn_down, ln2_g, ln2_b, loss_target, m_w_ada, m_b_ada, m_w_in, m_hgrn_lb, m_hgrn_gnorm, m_ssm_conv_w, m_ssm_conv_b, m_ssm_dt_bias, m_ssm_a_log, m_ssm_d, m_ssm_norm, m_w_branch_a, m_w_branch_b, m_w_o, m_ln1_g, m_ln1_b, m_w_ffn_gate, m_w_ffn_up, m_w_ffn_down, m_ln2_g, m_ln2_b, v_w_ada, v_b_ada, v_w_in, v_hgrn_lb, v_hgrn_gnorm, v_ssm_conv_w, v_ssm_conv_b, v_ssm_dt_bias, v_ssm_a_log, v_ssm_d, v_ssm_norm, v_w_branch_a, v_w_branch_b, v_w_o, v_ln1_g, v_ln1_b, v_w_ffn_gate, v_w_ffn_up, v_w_ffn_down, v_ln2_g, v_ln2_b):
    given = dict(locals())
    chip = 2 * lax.axis_index("x") + lax.axis_index("y")
    core = lax.axis_index("c")
    t = x.shape[1]

    first = gather_rows(jnp.concatenate([c, ssm_conv_w.reshape(1, CONV_DIM)], axis=1), "gather_cond").reshape(8, D + CONV_DIM)
    c_all = first[:, :D]
    conv_w = first[0::2, D:].reshape(4, 4, CONV_DIM // 4).transpose(1, 0, 2).reshape(4, CONV_DIM)
    mod_part, lb_row = ada_prepare(c_all, w_ada[0], hgrn_lb)
    mod_cols = w_ada.shape[2]
    mod_row = exchange_rows(mod_part.reshape(8, 1, mod_cols), "exchange_mod").reshape(1, 6 * D) + b_ada
    mod = tuple(mod_row[:, i * D:(i + 1) * D] for i in range(6))

    sharded = (w_in, w_branch_a, w_branch_b, w_o, w_ffn_gate, w_ffn_up, w_ffn_down)
    gathered = gather_weights(_pack_rows([w[0].reshape(-1, LANES) for w in sharded], BF16))
    full, o = {}, 0
    for (nm, rows), w in zip(PACK_SEGS, sharded, strict=True):
        part = gathered[:, o:o + rows].reshape((4,) + w.shape[1:])
        o += rows
        if nm in ("w_in", "w_ffn_gate", "w_ffn_up"):
            full[nm] = part.transpose(1, 0, 2).reshape(w.shape[1], 4 * w.shape[2])
        else:
            full[nm] = part.reshape(4 * w.shape[1], w.shape[2])
    wts = (_to_padded_cols(full["w_in"]), full["w_branch_a"], full["w_branch_b"], full["w_o"],
           jnp.concatenate([full["w_ffn_gate"], full["w_ffn_up"]], axis=1), full["w_ffn_down"])

    per_channel = lambda p: jnp.repeat(p[0], B_INNER // 32)[None]
    small = (lb_row, hgrn_gnorm, conv_w, ssm_conv_b, per_channel(ssm_dt_bias), per_channel(ssm_a_log),
             per_channel(ssm_d), ssm_norm, ln1_g, ln1_b, ln2_g, ln2_b)
    loss, grad_x, d_mod, d_wts, d_small = local_step(x[0], loss_target[0], mod, wts, small)

    d_lb, d_gn, d_cw, d_cb, d_dtb, d_alog, d_dsk, d_nw, d_l1g, d_l1b, d_l2g, d_l2b = d_small
    row = jnp.concatenate(list(d_mod) + [d_lb, d_gn, d_cw.reshape(1, 4 * CONV_DIM), d_cb, d_dtb, d_alog, d_dsk, d_nw,
                                          d_l1g, d_l1b, d_l2g, d_l2b], axis=1)
    g_all = gather_rows(row, "gather_small_grads").reshape(8, row.shape[1])
    dmod_cols = lax.dynamic_slice_in_dim(g_all, chip * mod_cols, mod_cols, axis=1)
    fin = finalize_small(g_all, c_all, dmod_cols, [given[n] for n in SMALL_PARAMS],
                         [given["m_" + n] for n in SMALL_PARAMS], [given["v_" + n] for n in SMALL_PARAMS])
    grads, deltas, new_m, new_v = {}, {}, {}, {}
    grads["w_ada"] = fin[0][None]
    grads["ssm_conv_w"] = lax.dynamic_slice_in_dim(fin[1], chip * (CONV_DIM // 4), CONV_DIM // 4, axis=1)[None]
    for i, nm in enumerate(SMALL_PARAMS):
        grads[nm], deltas[nm], new_m[nm], new_v[nm] = fin[2 + 4 * i:6 + 4 * i]

    dw_in, dw_a, dw_b, dw_o, dw_gu, dw_d = d_wts
    by_cols = lambda g: g.reshape(g.shape[0], 4, g.shape[1] // 4).transpose(1, 0, 2).reshape(4, -1, LANES)
    by_rows = lambda g: g.reshape(4, -1, LANES)
    slabs = _pack_rows([by_cols(_from_padded_cols(dw_in)), by_rows(dw_a), by_rows(dw_b), by_rows(dw_o),
                        by_cols(dw_gu[:, :D_FF]), by_cols(dw_gu[:, D_FF:]), by_rows(dw_d)], F32)
    pair = pair_add(slabs, pair_exchange(slabs), core)
    reduced = exchange_halves(sum_chips(scatter_sums(pair)))
    o = 0
    for (nm, rows), w in zip(PACK_SEGS, sharded, strict=True):
        grads[nm] = reduced[o:o + rows].reshape(w.shape)
        o += rows
    for nm in ("w_ada", "ssm_conv_w") + tuple(nm for nm, _ in PACK_SEGS):
        shp = given[nm].shape
        two_d = lambda a: a.reshape(shp[-2], shp[-1])
        d_, m_, v_ = adam_update(two_d(given[nm]), two_d(grads[nm]), two_d(given["m_" + nm]), two_d(given["v_" + nm]),
                                 "adam_" + nm)
        deltas[nm], new_m[nm], new_v[nm] = d_.reshape(shp), m_.reshape(shp), v_.reshape(shp)

    names = ("w_ada", "b_ada", "w_in", "hgrn_lb", "hgrn_gnorm", "ssm_conv_w", "ssm_conv_b", "ssm_dt_bias", "ssm_a_log",
             "ssm_d", "ssm_norm", "w_branch_a", "w_branch_b", "w_o", "ln1_g", "ln1_b", "w_ffn_gate", "w_ffn_up",
             "w_ffn_down", "ln2_g", "ln2_b")
    total_loss = lax.psum(loss[0, 0], ("x", "y", "c"))
    return (total_loss, grad_x[None], *[grads[n] for n in names], *[deltas[n] for n in names],
            *[new_m[n] for n in names], *[new_v[n] for n in names])
```

```python
import functools

import jax
import jax.numpy as jnp
from jax import lax
from jax.experimental import pallas as pl
from jax.experimental.pallas import tpu as pltpu

F32, BF16 = jnp.float32, jnp.bfloat16
HI = lax.Precision.HIGHEST
MESH = pl.DeviceIdType.MESH

D = 1024
CHUNK = 64
LANES = 128
N_HEADS_A = 8
N_GROUPS_B = 4
B_INNER = 2048
CONV_DIM = 3072
D_FF = 2816
ALPHA = 2.0 ** 0.25
LN_EPS = 1e-5
RMS_EPS = 1e-6
ADAM_LR, ADAM_B1, ADAM_B2, ADAM_EPS, ADAM_WD, ADAM_STEP = 0.001, 0.9, 0.999, 1e-08, 0.01, 10

IN_ORIG = 11296
IN_PAD = 11520
COL_GA, COL_GB, COL_XBC, COL_Z, COL_DT = 4096, 5120, 6144, 9216, 11264
ORIG_Z, ORIG_XBC, ORIG_DT, ORIG_GA = 4096, 6144, 9216, 9248

SHARDED = ("w_in", "w_branch_a", "w_branch_b", "w_o", "w_ffn_gate", "w_ffn_up", "w_ffn_down")
VMEM_LIMIT = 48 * 1024 * 1024
BLOCK_BYTES = 2 * 1024 * 1024

_DIMS = {"nn": (((1,), (0,)), ((), ())), "nt": (((1,), (1,)), ((), ())), "tn": (((0,), (0,)), ((), ()))}


def _bd(a, b, mode):
    return lax.dot_general(a.astype(BF16), b.astype(BF16), _DIMS[mode], preferred_element_type=F32)


@functools.partial(jax.custom_vjp, nondiff_argnums=(2,))
def bdot(a, b, mode):
    return _bd(a, b, mode)


def _bdot_fwd(a, b, mode):
    return _bd(a, b, mode), (a, b)


def _bdot_bwd(mode, res, g):
    a, b = res
    if mode == "nn":
        return _bd(g, b, "nt"), _bd(a, g, "tn")
    if mode == "nt":
        return _bd(g, b, "nn"), _bd(g, a, "tn")
    return _bd(b, g, "nt"), _bd(a, g, "nn")


bdot.defvjp(_bdot_fwd, _bdot_bwd)


def hdot(a, b, mode="nn"):
    return lax.dot_general(a, b, _DIMS[mode], precision=HI, preferred_element_type=F32)


def _raw(a, b, mode):
    return lax.dot_general(a, b, _DIMS[mode], preferred_element_type=F32)


def _split(x, n):
    parts, rest = [], x
    for _ in range(n):
        p = rest.astype(BF16)
        parts.append(p)
        rest = rest - p.astype(F32)
    return parts


def _od(a, b, mode, exact):
    if exact == 1:
        e = b.astype(BF16)
        p = _split(a, 3)
        return (_raw(p[2], e, mode) + _raw(p[1], e, mode)) + _raw(p[0], e, mode)
    e = a.astype(BF16)
    p = _split(b, 3)
    return (_raw(e, p[2], mode) + _raw(e, p[1], mode)) + _raw(e, p[0], mode)


@functools.partial(jax.custom_vjp, nondiff_argnums=(2, 3))
def odot(a, b, mode, exact):
    return _od(a, b, mode, exact)


def _odot_fwd(a, b, mode, exact):
    return _od(a, b, mode, exact), (a, b)


def _odot_bwd(mode, exact, res, g):
    a, b = res
    if exact == 1:
        da = {"nn": lambda: _od(g, b, "nt", 1), "nt": lambda: _od(g, b, "nn", 1), "tn": lambda: _od(b, g, "nt", 0)}[mode]()
        return da, jnp.zeros_like(b)
    db = {"nn": lambda: _od(a, g, "tn", 0), "nt": lambda: _od(g, a, "tn", 1), "tn": lambda: _od(a, g, "nn", 0)}[mode]()
    return jnp.zeros_like(a), db


odot.defvjp(_odot_fwd, _odot_bwd)


def _d3(a, b, mode):
    ah, al = _split(a, 2)
    bh, bl = _split(b, 2)
    return _raw(ah, bh, mode) + (_raw(ah, bl, mode) + _raw(al, bh, mode))


@functools.partial(jax.custom_vjp, nondiff_argnums=(2,))
def dot3(a, b, mode):
    return _d3(a, b, mode)


def _dot3_fwd(a, b, mode):
    return _d3(a, b, mode), (a, b)


def _dot3_bwd(mode, res, g):
    a, b = res
    if mode == "nn":
        return _d3(g, b, "nt"), _d3(a, g, "tn")
    if mode == "nt":
        return _d3(g, b, "nn"), _d3(g, a, "tn")
    return _d3(b, g, "nt"), _d3(a, g, "nn")


dot3.defvjp(_dot3_fwd, _dot3_bwd)


@jax.custom_vjp
def split_lanes(x):
    return x[:, :256], x[:, 256:]


split_lanes.defvjp(lambda x: ((x[:, :256], x[:, 256:]), None),
                   lambda _, g: (jnp.concatenate(g, axis=1),))


def sigmoid(x):
    return 1.0 / (1.0 + jnp.exp(-x))


def silu(x):
    return x * sigmoid(x)


def softplus(x):
    return jnp.maximum(x, 0.0) + jnp.log1p(jnp.exp(jnp.minimum(x, -x)))


def _ln(x):
    mu = jnp.mean(x, axis=-1, keepdims=True)
    xc = x - mu
    return xc * lax.rsqrt(jnp.mean(xc * xc, axis=-1, keepdims=True) + LN_EPS)


def _tril64():
    r = lax.broadcasted_iota(jnp.int32, (CHUNK, CHUNK), 0)
    c = lax.broadcasted_iota(jnp.int32, (CHUNK, CHUNK), 1)
    return (r >= c).astype(F32)


def hgrn_chunk(q, fl, iv, gr, st, lb, gn, tril):
    f = lb + (1.0 - lb) * sigmoid(fl)
    gl = jnp.log(f)
    k = 1.0 - f
    qf = silu(q) * (128 ** -0.5)
    b = odot(tril, gl, "nn", 0)
    blast = jnp.sum(gl, axis=0, keepdims=True)
    ref = lax.stop_gradient(0.5 * blast)
    qp = qf * jnp.exp(b - ref)
    kp = k * jnp.exp(ref - b)
    sc = dot3(qp, kp, "nt") * tril
    o = bdot(sc, iv, "nn") + bdot(qf * jnp.exp(b), st, "nt")
    st_new = st * jnp.exp(blast) + bdot(iv, k * jnp.exp(blast - b), "tn")
    on = o * lax.rsqrt(jnp.mean(o * o, axis=-1, keepdims=True) + RMS_EPS) * gn
    return on * silu(gr), st_new


def ssd_consts(g):
    i32 = jnp.int32
    ej = lax.broadcasted_iota(i32, (LANES, 512), 0)
    ec = lax.broadcasted_iota(i32, (LANES, 512), 1)
    expand = (ej == g * 8 + (ec >> 6)).astype(F32)
    ts = lax.broadcasted_iota(i32, (CHUNK, 512), 0)
    tc = lax.broadcasted_iota(i32, (CHUNK, 512), 1)
    itile = (ts == (tc & 63)).astype(F32)
    maskall = ts >= (tc & 63)
    br = lax.broadcasted_iota(i32, (256, 256), 0)
    bc = lax.broadcasted_iota(i32, (256, 256), 1)
    blockmask = ((br >> 6) == (bc >> 6)).astype(F32)
    return expand, itile, maskall, blockmask, _tril64()


def ssd_chunk(x, bm, cm, dt, z, st, dtb, alog, dsk, nw, cs):
    expand, itile, maskall, blockmask, tril = cs
    delta = softplus(odot(dt, expand, "nn", 1) + dtb)
    a = -jnp.exp(alog) * delta
    acum = odot(tril, a, "nn", 0)
    alast = jnp.sum(a, axis=0, keepdims=True)
    xdt = x * delta
    cb = bdot(cm, jnp.concatenate([bm] * 8, axis=0), "nt")
    arow = jnp.sum(acum * itile, axis=0, keepdims=True)
    dec = jnp.where(maskall, jnp.exp(jnp.minimum(acum - arow, 0.0)), 0.0)
    intra = [bdot(m, jnp.concatenate([xh] * 4, axis=0) * blockmask, "nn")
             for m, xh in zip(split_lanes(cb * dec), split_lanes(xdt))]
    y = jnp.concatenate(intra, axis=1) + bdot(cm, st, "nn") * jnp.exp(acum)
    st_new = st * jnp.exp(alast) + bdot(bm, xdt * jnp.exp(alast - acum), "tn")
    yz = (y + x * dsk) * silu(z)
    return yz * lax.rsqrt(jnp.mean(yz * yz, axis=-1, keepdims=True) + RMS_EPS) * nw, st_new


def adamw(w, g, m, v):
    m = ADAM_B1 * m + (1.0 - ADAM_B1) * g
    v = ADAM_B2 * v + (1.0 - ADAM_B2) * jnp.square(g)
    m_hat = m / (1.0 - ADAM_B1 ** ADAM_STEP)
    v_hat = v / (1.0 - ADAM_B2 ** ADAM_STEP)
    return -ADAM_LR * (m_hat / (jnp.sqrt(v_hat) + ADAM_EPS) + ADAM_WD * w), m, v


def _pick(n, cands):
    for c in cands:
        if n % c == 0:
            return c
    return n


def _params(sem):
    return pltpu.CompilerParams(dimension_semantics=sem, vmem_limit_bytes=VMEM_LIMIT)


def matmul(a, b, mode, out_dtype, name):
    if mode == "nn":
        (m, k), n = a.shape, b.shape[1]
    elif mode == "nt":
        (m, k), n = a.shape, b.shape[0]
    else:
        (k, m), n = a.shape, b.shape[1]
    tm = _pick(m, (512, 256, 128))
    tn = _pick(n, (1408, 1024, 768, 512, 256, 128))
    tk = _pick(k, (2304, 2048, 1408, 1024, 768, 512, 256, 128))
    nk = k // tk
    a_spec = pl.BlockSpec((tk, tm), lambda i, j, kk: (kk, i)) if mode == "tn" else pl.BlockSpec((tm, tk), lambda i, j, kk: (i, kk))
    b_spec = pl.BlockSpec((tn, tk), lambda i, j, kk: (j, kk)) if mode == "nt" else pl.BlockSpec((tk, tn), lambda i, j, kk: (kk, j))

    def body(a_ref, b_ref, o_ref, *acc):
        part = _bd(a_ref[...], b_ref[...], mode)
        if nk == 1:
            o_ref[...] = part.astype(o_ref.dtype)
            return
        acc_ref, = acc
        kk = pl.program_id(2)

        @pl.when(kk == 0)
        def _():
            acc_ref[...] = part

        @pl.when(jnp.logical_and(kk > 0, kk < nk - 1))
        def _():
            acc_ref[...] += part

        @pl.when(kk == nk - 1)
        def _():
            o_ref[...] = (acc_ref[...] + part).astype(o_ref.dtype)

    return pl.pallas_call(
        body, name=name, grid=(m // tm, n // tn, nk),
        in_specs=[a_spec, b_spec], out_specs=pl.BlockSpec((tm, tn), lambda i, j, kk: (i, j)),
        out_shape=jax.ShapeDtypeStruct((m, n), out_dtype),
        scratch_shapes=[pltpu.VMEM((tm, tn), F32)] if nk > 1 else [],
        compiler_params=_params(("parallel", "parallel", "arbitrary")),
    )(a, b)


def rowwise(name, fn, rows, consts, out_rows, out_accs=(), tm_max=256, into=None, new_wide=None):
    t = rows[0][0].shape[0]
    tm = _pick(t, (tm_max, 128, 64, 32, 16, 8))
    n_r, n_c, n_o = len(rows), len(consts), len(out_rows)
    n_alias = 0 if into is None else 1

    def body(*refs):
        r_in = [r[...] for r in refs[:n_r]]
        c_in = [r[...] for r in refs[n_r:n_r + n_c]]
        refs = refs[:n_r + n_c] + refs[n_r + n_c + n_alias:]
        o_refs = refs[n_r + n_c:n_r + n_c + n_o]
        a_refs = refs[n_r + n_c + n_o:]
        ro, ao = fn(r_in, c_in)
        for ref, val in zip(o_refs, ro, strict=True):
            ref[...] = val.astype(ref.dtype)
        if a_refs:
            @pl.when(pl.program_id(0) == 0)
            def _():
                for ref in a_refs:
                    ref[...] = jnp.zeros_like(ref)

            for ref, val in zip(a_refs, ao, strict=True):
                ref[...] += val

    in_specs = [pl.BlockSpec((tm, w), functools.partial(lambda i, cb: (i, cb), cb=cb)) for _, w, cb in rows]
    in_specs += [pl.BlockSpec(c.shape, lambda i: (0, 0)) for c in consts]
    out_specs = [pl.BlockSpec((tm, w), lambda i: (i, 0)) for w, _ in out_rows]
    out_specs += [pl.BlockSpec(s, lambda i: (0, 0)) for s in out_accs]
    out_shape = [jax.ShapeDtypeStruct((t, w), dt) for w, dt in out_rows]
    out_shape += [jax.ShapeDtypeStruct(s, F32) for s in out_accs]
    operands = [r[0] for r in rows] + list(consts)
    aliases = {}
    if into is not None:
        target, cb = into
        in_specs.append(pl.BlockSpec(memory_space=pl.ANY))
        operands.append(target)
        out_specs[0] = pl.BlockSpec((tm, out_rows[0][0]), lambda i: (i, cb))
        out_shape[0] = jax.ShapeDtypeStruct(target.shape, target.dtype)
        aliases = {len(operands) - 1: 0}
    if new_wide is not None:
        width, cb = new_wide
        out_specs[0] = pl.BlockSpec((tm, out_rows[0][0]), lambda i: (i, cb))
        out_shape[0] = jax.ShapeDtypeStruct((t, width), out_rows[0][1])
    return pl.pallas_call(
        body, name=name, grid=(t // tm,), in_specs=in_specs, out_specs=out_specs, out_shape=out_shape,
        input_output_aliases=aliases, compiler_params=_params(("arbitrary",)),
    )(*operands)


def _full(a):
    return (a, a.shape[1], 0)


def _time_block(t):
    return _pick(t, (512, 256, 128, 64))


def hgrn_forward(proj, lb, gn):
    t = proj.shape[0]
    tb = _time_block(t)
    nb, nc = t // tb, tb // CHUNK
    lanes = [slice(h * LANES, (h + 1) * LANES) for h in range(N_HEADS_A)]

    def body(qfig_ref, lb_ref, gn_ref, o_ref, st_ref, state):
        @pl.when(pl.program_id(0) == 0)
        def _():
            state[...] = jnp.zeros_like(state)

        tril = _tril64()
        lbv, gnv = lb_ref[...], gn_ref[...]

        def step(c, carry):
            sl = pl.ds(pl.multiple_of(c * CHUNK, CHUNK), CHUNK)
            col = lambda seg, h: slice(seg * D + h * LANES, seg * D + (h + 1) * LANES)
            ins = [(qfig_ref[sl, col(0, h)], qfig_ref[sl, col(1, h)], qfig_ref[sl, col(2, h)], qfig_ref[sl, col(3, h)],
                    state[h], lbv[:, lanes[h]]) for h in range(N_HEADS_A)]
            res = [hgrn_chunk(*a, gnv, tril) for a in ins]
            for h in range(N_HEADS_A):
                st_ref[c, h] = ins[h][4]
                o_ref[sl, lanes[h]] = res[h][0].astype(o_ref.dtype)
                state[h] = res[h][1]
            return carry

        lax.fori_loop(0, nc, step, 0)

    return pl.pallas_call(
        body, name="hgrn_forward", grid=(nb,),
        in_specs=[pl.BlockSpec((tb, 4 * D), lambda j: (j, 0)),
                  pl.BlockSpec((1, D), lambda j: (0, 0)), pl.BlockSpec((1, LANES), lambda j: (0, 0))],
        out_specs=[pl.BlockSpec((tb, D), lambda j: (j, 0)),
                   pl.BlockSpec((nc, N_HEADS_A, LANES, LANES), lambda j: (j, 0, 0, 0))],
        out_shape=[jax.ShapeDtypeStruct((t, D), BF16),
                   jax.ShapeDtypeStruct((t // CHUNK, N_HEADS_A, LANES, LANES), F32)],
        scratch_shapes=[pltpu.VMEM((N_HEADS_A, LANES, LANES), F32)],
        compiler_params=_params(("arbitrary",)),
    )(proj, lb, gn)


def hgrn_backward(proj, states, d_out, lb, gn, d_proj):
    t = proj.shape[0]
    tb = _time_block(t)
    nb, nc = t // tb, tb // CHUNK
    lanes = [slice(h * LANES, (h + 1) * LANES) for h in range(N_HEADS_A)]

    def body(qfig_ref, st_ref, do_ref, lb_ref, gn_ref, _, dqfig_ref, dlb_ref, dgn_ref, d_state):
        @pl.when(pl.program_id(0) == 0)
        def _():
            d_state[...] = jnp.zeros_like(d_state)
            dlb_ref[...] = jnp.zeros_like(dlb_ref)
            dgn_ref[...] = jnp.zeros_like(dgn_ref)

        tril = _tril64()
        lbv, gnv = lb_ref[...], gn_ref[...]

        def step(n, carry):
            c = nc - 1 - n
            sl = pl.ds(pl.multiple_of(c * CHUNK, CHUNK), CHUNK)
            col = lambda seg, h: slice(seg * D + h * LANES, seg * D + (h + 1) * LANES)
            fn = lambda q, fl, iv, gr, st, lb_, gn_: hgrn_chunk(q, fl, iv, gr, st, lb_, gn_, tril)
            ins = [(qfig_ref[sl, col(0, h)], qfig_ref[sl, col(1, h)], qfig_ref[sl, col(2, h)], qfig_ref[sl, col(3, h)],
                    st_ref[c, h], lbv[:, lanes[h]], gnv) for h in range(N_HEADS_A)]
            cots = [(do_ref[sl, lanes[h]], d_state[h]) for h in range(N_HEADS_A)]
            res = [jax.vjp(fn, *a)[1](ct) for a, ct in zip(ins, cots)]
            dgn_total = dgn_ref[...]
            for h in range(N_HEADS_A):
                dq, df, di, dg, dst, dlb, dgn = res[h]
                for seg, val in enumerate((dq, df, di, dg)):
                    dqfig_ref[sl, col(seg, h)] = val.astype(dqfig_ref.dtype)
                d_state[h] = dst
                dlb_ref[:, lanes[h]] += dlb
                dgn_total = dgn_total + dgn
            dgn_ref[...] = dgn_total
            return carry

        lax.fori_loop(0, nc, step, 0)

    rev = lambda j: nb - 1 - j
    return pl.pallas_call(
        body, name="hgrn_backward", grid=(nb,),
        in_specs=[pl.BlockSpec((tb, 4 * D), lambda j: (rev(j), 0)),
                  pl.BlockSpec((nc, N_HEADS_A, LANES, LANES), lambda j: (rev(j), 0, 0, 0)),
                  pl.BlockSpec((tb, D), lambda j: (rev(j), 0)),
                  pl.BlockSpec((1, D), lambda j: (0, 0)), pl.BlockSpec((1, LANES), lambda j: (0, 0)),
                  pl.BlockSpec(memory_space=pl.ANY)],
        out_specs=[pl.BlockSpec((tb, 4 * D), lambda j: (rev(j), 0)),
                   pl.BlockSpec((1, D), lambda j: (0, 0)), pl.BlockSpec((1, LANES), lambda j: (0, 0))],
        out_shape=[jax.ShapeDtypeStruct(d_proj.shape, d_proj.dtype), jax.ShapeDtypeStruct((1, D), F32),
                   jax.ShapeDtypeStruct((1, LANES), F32)],
        input_output_aliases={5: 0},
        scratch_shapes=[pltpu.VMEM((N_HEADS_A, LANES, LANES), F32)],
        compiler_params=_params(("arbitrary",)),
    )(proj, states, d_out, lb, gn, d_proj)


def _ssd_in_specs(tb, tmap):
    return [pl.BlockSpec((tb, 512), lambda g, j: (tmap(j), g)),
            pl.BlockSpec((tb, LANES), lambda g, j: (tmap(j), 16 + g)),
            pl.BlockSpec((tb, LANES), lambda g, j: (tmap(j), 20 + g)),
            pl.BlockSpec((tb, LANES), lambda g, j: (tmap(j), COL_DT // LANES)),
            pl.BlockSpec((tb, 512), lambda g, j: (tmap(j), COL_Z // 512 + g))]


def ssd_forward(xc, proj, dtb, alog, dsk, nw):
    t = proj.shape[0]
    tb = _time_block(t)
    nb, nc = t // tb, tb // CHUNK

    def body(x_ref, b_ref, c_ref, dt_ref, z_ref, dtb_ref, alog_ref, dsk_ref, nw_ref, o_ref, st_ref, state):
        @pl.when(pl.program_id(1) == 0)
        def _():
            state[...] = jnp.zeros_like(state)

        cs = ssd_consts(pl.program_id(0))
        par = (dtb_ref[...], alog_ref[...], dsk_ref[...], nw_ref[...])

        def step(c, carry):
            sl = pl.ds(pl.multiple_of(c * CHUNK, CHUNK), CHUNK)
            st = state[...]
            st_ref[c] = st
            out, st_new = ssd_chunk(x_ref[sl, :], b_ref[sl, :], c_ref[sl, :], dt_ref[sl, :], z_ref[sl, :], st, *par, cs)
            o_ref[sl, :] = out.astype(o_ref.dtype)
            state[...] = st_new
            return carry

        lax.fori_loop(0, nc, step, 0)

    vec = pl.BlockSpec((1, 512), lambda g, j: (0, g))
    return pl.pallas_call(
        body, name="ssd_forward", grid=(N_GROUPS_B, nb),
        in_specs=_ssd_in_specs(tb, lambda j: j) + [vec] * 4,
        out_specs=[pl.BlockSpec((tb, 512), lambda g, j: (j, g)),
                   pl.BlockSpec((nc, None, LANES, 512), lambda g, j: (j, g, 0, 0))],
        out_shape=[jax.ShapeDtypeStruct((t, B_INNER), BF16),
                   jax.ShapeDtypeStruct((t // CHUNK, N_GROUPS_B, LANES, 512), F32)],
        scratch_shapes=[pltpu.VMEM((LANES, 512), F32)],
        compiler_params=_params(("arbitrary", "arbitrary")),
    )(xc, xc, xc, proj, proj, dtb, alog, dsk, nw)


def ssd_backward(xc, proj, states, d_out, dtb, alog, dsk, nw, d_proj):
    t = proj.shape[0]
    tb = _time_block(t)
    nb, nc = t // tb, tb // CHUNK
    rev = lambda j: nb - 1 - j

    def body(x_ref, b_ref, c_ref, dt_ref, z_ref, st_ref, do_ref, dtb_ref, alog_ref, dsk_ref, nw_ref, _,
             dx_ref, db_ref, dc_ref, ddt_ref, dz_ref, ddtb_ref, dalog_ref, ddsk_ref, dnw_ref, d_state):
        accs = (ddtb_ref, dalog_ref, ddsk_ref, dnw_ref)

        @pl.when(pl.program_id(1) == 0)
        def _():
            d_state[...] = jnp.zeros_like(d_state)
            for ref in accs:
                ref[...] = jnp.zeros_like(ref)

        cs = ssd_consts(pl.program_id(0))
        par = (dtb_ref[...], alog_ref[...], dsk_ref[...], nw_ref[...])

        def step(n, carry):
            c = nc - 1 - n
            sl = pl.ds(pl.multiple_of(c * CHUNK, CHUNK), CHUNK)
            fn = lambda *a: ssd_chunk(*a, cs)
            _, vjp = jax.vjp(fn, x_ref[sl, :], b_ref[sl, :], c_ref[sl, :], dt_ref[sl, :], z_ref[sl, :], st_ref[c], *par)
            dx, db, dc, ddt, dz, dst, *dpar = vjp((do_ref[sl, :], d_state[...]))
            dx_ref[sl, :] = dx
            db_ref[sl, :] = db
            dc_ref[sl, :] = dc
            ddt_ref[sl, :] = ddt
            dz_ref[sl, :] = dz.astype(dz_ref.dtype)
            d_state[...] = dst
            for ref, val in zip(accs, dpar, strict=True):
                ref[...] += val
            return carry

        lax.fori_loop(0, nc, step, 0)

    vec = pl.BlockSpec((1, 512), lambda g, j: (0, g))
    acc = pl.BlockSpec((None, 1, 512), lambda g, j: (g, 0, 0))
    return pl.pallas_call(
        body, name="ssd_backward", grid=(N_GROUPS_B, nb),
        in_specs=_ssd_in_specs(tb, rev)
        + [pl.BlockSpec((nc, None, LANES, 512), lambda g, j: (rev(j), g, 0, 0)),
           pl.BlockSpec((tb, 512), lambda g, j: (rev(j), g))] + [vec] * 4 + [pl.BlockSpec(memory_space=pl.ANY)],
        out_specs=[pl.BlockSpec((tb, 512), lambda g, j: (rev(j), g)),
                   pl.BlockSpec((tb, LANES), lambda g, j: (rev(j), g)),
                   pl.BlockSpec((tb, LANES), lambda g, j: (rev(j), g)),
                   pl.BlockSpec((None, tb, LANES), lambda g, j: (g, rev(j), 0)),
                   pl.BlockSpec((tb, 512), lambda g, j: (rev(j), COL_Z // 512 + g)), acc, acc, acc, acc],
        out_shape=[jax.ShapeDtypeStruct((t, B_INNER), F32), jax.ShapeDtypeStruct((t, 512), F32),
                   jax.ShapeDtypeStruct((t, 512), F32), jax.ShapeDtypeStruct((N_GROUPS_B, t, LANES), F32),
                   jax.ShapeDtypeStruct(d_proj.shape, d_proj.dtype)] + [jax.ShapeDtypeStruct((N_GROUPS_B, 1, 512), F32)] * 4,
        input_output_aliases={11: 4},
        scratch_shapes=[pltpu.VMEM((LANES, 512), F32)],
        compiler_params=_params(("arbitrary", "arbitrary")),
    )(xc, xc, xc, proj, proj, states, d_out, dtb, alog, dsk, nw, d_proj)


CONV_HALO = 8


def _shift_down(halo_then_tile, s, tm):
    if s == 0:
        return halo_then_tile[CONV_HALO:CONV_HALO + tm]
    return pltpu.roll(halo_then_tile, s, 0)[CONV_HALO:CONV_HALO + tm]


def _conv_pre(cur, prev, w, b, tm):
    stacked = jnp.concatenate([prev, cur], axis=0)
    taps = [_shift_down(stacked, 3 - j, tm) for j in range(4)]
    pre = b + taps[0] * w[0:1] + taps[1] * w[1:2] + taps[2] * w[2:3] + taps[3] * w[3:4]
    return pre, taps


def _conv_specs(t, tm):
    per = tm // CONV_HALO
    cur = pl.BlockSpec((tm, CONV_DIM), lambda i: (i, COL_XBC // CONV_DIM))
    prev = pl.BlockSpec((CONV_HALO, CONV_DIM), lambda i: (jnp.maximum(i * per - 1, 0), COL_XBC // CONV_DIM))
    return cur, prev


def conv_forward(proj, w, b):
    t = proj.shape[0]
    tm = _pick(t, (256, 128, 64))

    def body(cur_ref, prev_ref, w_ref, b_ref, o_ref):
        prev = jnp.where(pl.program_id(0) == 0, 0.0, prev_ref[...])
        pre, _ = _conv_pre(cur_ref[...], prev, w_ref[...], b_ref[...], tm)
        o_ref[...] = silu(pre)

    cur, prev = _conv_specs(t, tm)
    return pl.pallas_call(
        body, name="conv_forward", grid=(t // tm,),
        in_specs=[cur, prev, pl.BlockSpec((4, CONV_DIM), lambda i: (0, 0)), pl.BlockSpec((1, CONV_DIM), lambda i: (0, 0))],
        out_specs=pl.BlockSpec((tm, CONV_DIM), lambda i: (i, 0)),
        out_shape=jax.ShapeDtypeStruct((t, CONV_DIM), F32),
        compiler_params=_params(("arbitrary",)),
    )(proj, proj, w, b)


def conv_backward_pre(proj, dx, db_, dc_, w, b):
    t = proj.shape[0]
    tm = _pick(t, (256, 128, 64))

    def body(cur_ref, prev_ref, dx_ref, dbm_ref, dcm_ref, w_ref, b_ref, dpre_ref, dw_ref, dbias_ref):
        @pl.when(pl.program_id(0) == 0)
        def _():
            dw_ref[...] = jnp.zeros_like(dw_ref)
            dbias_ref[...] = jnp.zeros_like(dbias_ref)

        prev = jnp.where(pl.program_id(0) == 0, 0.0, prev_ref[...])
        pre, taps = _conv_pre(cur_ref[...], prev, w_ref[...], b_ref[...], tm)
        sg = sigmoid(pre)
        d_out = jnp.concatenate([dx_ref[...], dbm_ref[...], dcm_ref[...]], axis=1)
        dpre = d_out * (sg * (1.0 + pre * (1.0 - sg)))
        dpre_ref[...] = dpre
        dbias_ref[...] += jnp.sum(dpre, axis=0, keepdims=True)
        for j in range(4):
            dw_ref[j:j + 1, :] += jnp.sum(dpre * taps[j], axis=0, keepdims=True)

    cur, prev = _conv_specs(t, tm)
    row = lambda w_: pl.BlockSpec((tm, w_), lambda i: (i, 0))
    return pl.pallas_call(
        body, name="conv_backward_pre", grid=(t // tm,),
        in_specs=[cur, prev, row(B_INNER), row(512), row(512),
                  pl.BlockSpec((4, CONV_DIM), lambda i: (0, 0)), pl.BlockSpec((1, CONV_DIM), lambda i: (0, 0))],
        out_specs=[row(CONV_DIM), pl.BlockSpec((4, CONV_DIM), lambda i: (0, 0)), pl.BlockSpec((1, CONV_DIM), lambda i: (0, 0))],
        out_shape=[jax.ShapeDtypeStruct((t, CONV_DIM), F32), jax.ShapeDtypeStruct((4, CONV_DIM), F32),
                   jax.ShapeDtypeStruct((1, CONV_DIM), F32)],
        compiler_params=_params(("arbitrary",)),
    )(proj, proj, dx, db_, dc_, w, b)


def conv_backward_input(dpre, w, d_proj):
    t = dpre.shape[0]
    tm = _pick(t, (256, 128, 64))
    per = tm // CONV_HALO
    last = t // CONV_HALO - 1
    nt = t // tm

    def body(cur_ref, nxt_ref, w_ref, _, o_ref):
        nxt = jnp.where(pl.program_id(0) == nt - 1, 0.0, nxt_ref[...])
        stacked = jnp.concatenate([cur_ref[...], nxt], axis=0)
        w_ = w_ref[...]
        acc = stacked[0:tm] * w_[3:4]
        for j in range(3):
            s = 3 - j
            acc = acc + pltpu.roll(stacked, tm + CONV_HALO - s, 0)[0:tm] * w_[j:j + 1]
        o_ref[...] = acc.astype(o_ref.dtype)

    return pl.pallas_call(
        body, name="conv_backward_input", grid=(nt,),
        in_specs=[pl.BlockSpec((tm, CONV_DIM), lambda i: (i, 0)),
                  pl.BlockSpec((CONV_HALO, CONV_DIM), lambda i: (jnp.minimum((i + 1) * per, last), 0)),
                  pl.BlockSpec((4, CONV_DIM), lambda i: (0, 0)), pl.BlockSpec(memory_space=pl.ANY)],
        out_specs=pl.BlockSpec((tm, CONV_DIM), lambda i: (i, COL_XBC // CONV_DIM)),
        out_shape=jax.ShapeDtypeStruct(d_proj.shape, d_proj.dtype),
        input_output_aliases={3: 0},
        compiler_params=_params(("arbitrary",)),
    )(dpre, dpre, w, d_proj)


def stage_modulate(x, sc, sh):
    return _ln(x) * (1.0 + sc) + sh


def stage_merge(ga, gb, ya, yb):
    return sigmoid(ga) * ya + sigmoid(gb) * yb


def stage_post_mixer(x, h, g1, ln_g, ln_b, sc2, sh2):
    x1 = _ln(ALPHA * x + g1 * h) * ln_g + ln_b
    return x1, _ln(x1) * (1.0 + sc2) + sh2


def stage_swiglu(a, b):
    return silu(a) * b


def stage_loss(x1, hf, tgt, g2, ln_g, ln_b):
    x2 = _ln(ALPHA * x1 + g2 * hf) * ln_g + ln_b
    return 0.5 * jnp.sum(jnp.mean(jnp.square(x2 - tgt), axis=-1, keepdims=True), axis=0, keepdims=True)


def local_step(x, tgt, mod, wts, small):
    sh1, sc1, g1, sh2, sc2, g2 = mod
    w_in, w_a, w_b, w_o, w_gu, w_d = wts
    lb, gn, conv_w, conv_b, dtb, alog, dsk, nw, ln1_g, ln1_b, ln2_g, ln2_b = small
    vec = (1, D)

    (u1,) = rowwise("modulate1", lambda r, c: ((stage_modulate(r[0], *c),), ()), [_full(x)], [sc1, sh1], [(D, BF16)])
    proj = matmul(u1, w_in, "nn", F32, "in_proj")
    ya_in, st_a = hgrn_forward(proj, lb, gn)
    xc = conv_forward(proj, conv_w, conv_b)
    yb_in, st_b = ssd_forward(xc, proj, dtb, alog, dsk, nw)
    ya = matmul(ya_in, w_a, "nn", F32, "branch_a")
    yb = matmul(yb_in, w_b, "nn", F32, "branch_b")
    gate_rows = [(proj, D, COL_GA // D), (proj, D, COL_GB // D), _full(ya), _full(yb)]
    (merged,) = rowwise("merge", lambda r, c: ((stage_merge(*r),), ()), gate_rows, [], [(D, BF16)])
    h = matmul(merged, w_o, "nn", F32, "out_proj")
    post_consts = [g1, ln1_g, ln1_b, sc2, sh2]
    x1, u2 = rowwise("post_mixer", lambda r, c: (stage_post_mixer(*r, *c), ()), [_full(x), _full(h)], post_consts,
                     [(D, F32), (D, BF16)])
    ab = matmul(u2, w_gu, "nn", F32, "ffn_in")
    (p,) = rowwise("swiglu", lambda r, c: ((stage_swiglu(*r),), ()), [(ab, D_FF, 0), (ab, D_FF, 1)], [], [(D_FF, BF16)])
    hf = matmul(p, w_d, "nn", F32, "ffn_out")

    def loss_bwd(r, c):
        loss, vjp = jax.vjp(stage_loss, *r, *c)
        dx1, dhf, _, dg2, dlg, dlb_ = vjp(jnp.ones((1, 1), F32))
        return (dx1, dhf), (loss, dg2, dlg, dlb_)

    dx1, dhf, loss, dg2, dln2_g, dln2_b = rowwise(
        "loss_backward", loss_bwd, [_full(x1), _full(hf), _full(tgt)], [g2, ln2_g, ln2_b],
        [(D, F32), (D, BF16)], [(1, 1), vec, vec, vec])
    dp = matmul(dhf, w_d, "nt", F32, "ffn_out_dx")
    dw_d = matmul(p, dhf, "tn", F32, "ffn_out_dw")

    def swiglu_bwd(r, c):
        _, vjp = jax.vjp(stage_swiglu, r[0], r[1])
        da, db_ = vjp(r[2])
        return (jnp.concatenate([da, db_], axis=1),), ()

    (dab,) = rowwise("swiglu_backward", swiglu_bwd, [(ab, D_FF, 0), (ab, D_FF, 1), _full(dp)], [], [(2 * D_FF, BF16)])
    du2 = matmul(dab, w_gu, "nt", F32, "ffn_in_dx")
    dw_gu = matmul(u2, dab, "tn", F32, "ffn_in_dw")

    def post_bwd(r, c):
        _, vjp = jax.vjp(stage_post_mixer, r[0], r[1], *c)
        dx, dh, *dc = vjp((r[2], r[3]))
        return (dx, dh), tuple(dc)

    dx_a, dh, dg1, dln1_g, dln1_b, dsc2, dsh2 = rowwise(
        "post_mixer_backward", post_bwd, [_full(x), _full(h), _full(dx1), _full(du2)], post_consts,
        [(D, F32), (D, BF16)], [vec] * 5)
    dmerged = matmul(dh, w_o, "nt", F32, "out_proj_dx")
    dw_o = matmul(merged, dh, "tn", F32, "out_proj_dw")

    def merge_bwd(r, c):
        _, vjp = jax.vjp(stage_merge, *r[:4])
        dga, dgb, dya, dyb = vjp(r[4])
        return (jnp.concatenate([dga, dgb], axis=1), dya, dyb), ()

    dproj, dya, dyb = rowwise("merge_backward", merge_bwd, gate_rows + [_full(dmerged)], [],
                              [(2 * D, BF16), (D, BF16), (D, BF16)], new_wide=(IN_PAD, COL_GA // (2 * D)))
    dya_in = matmul(dya, w_a, "nt", F32, "branch_a_dx")
    dw_a = matmul(ya_in, dya, "tn", F32, "branch_a_dw")
    dyb_in = matmul(dyb, w_b, "nt", F32, "branch_b_dx")
    dw_b = matmul(yb_in, dyb, "tn", F32, "branch_b_dw")
    dproj, dlb, dgn = hgrn_backward(proj, st_a, dya_in, lb, gn, dproj)
    dxs, dbm, dcm, ddt, dproj, ddtb, dalog, ddsk, dnw = ssd_backward(xc, proj, st_b, dyb_in, dtb, alog, dsk, nw, dproj)
    dpre, dconv_w, dconv_b = conv_backward_pre(proj, dxs, dbm, dcm, conv_w, conv_b)
    dproj = conv_backward_input(dpre, conv_w, dproj)
    t = x.shape[0]
    tail = jnp.concatenate([jnp.sum(ddt, axis=0).astype(BF16), jnp.zeros((t, IN_PAD - COL_DT - LANES), BF16)], axis=1)
    dproj = lax.dynamic_update_slice(dproj, tail, (0, COL_DT))
    du1 = matmul(dproj, w_in, "nt", F32, "in_proj_dx")
    dw_in = matmul(u1, dproj, "tn", F32, "in_proj_dw")

    def mod_bwd(r, c):
        _, vjp = jax.vjp(stage_modulate, r[0], *c)
        dx, dsc, dsh = vjp(r[1])
        return (dx + r[2],), (dsc, dsh)

    grad_x, dsc1, dsh1 = rowwise("modulate1_backward", mod_bwd, [_full(x), _full(du1), _full(dx_a)], [sc1, sh1],
                                 [(D, F32)], [vec, vec])
    d_mod = (dsh1, dsc1, dg1, dsh2, dsc2, dg2)
    d_wts = (dw_in, dw_a, dw_b, dw_o, dw_gu, dw_d)
    d_small = (dlb, dgn, dconv_w, dconv_b, ddtb.reshape(1, B_INNER),
               dalog.reshape(1, B_INNER), ddsk.reshape(1, B_INNER), dnw.reshape(1, B_INNER),
               dln1_g, dln1_b, dln2_g, dln2_b)
    return loss, grad_x, d_mod, d_wts, d_small


HBM = pl.BlockSpec(memory_space=pltpu.HBM)


def _place():
    return lax.axis_index("x"), lax.axis_index("y"), lax.axis_index("c")


def _other_chips(x, y):
    return [(1 - x, y), (x, 1 - y), (1 - x, 1 - y)]


def _remote(src, dst, send_sem, recv_sem, device):
    return pltpu.make_async_remote_copy(src_ref=src, dst_ref=dst, send_sem=send_sem, recv_sem=recv_sem,
                                        device_id=device, device_id_type=MESH)


def gather_rows(v, name):
    n = v.shape[1]

    def body(v_ref, out_ref, send_sems, recv_sems, local_sem):
        x, y, c = _place()
        mine = pltpu.make_async_copy(v_ref, out_ref.at[4 * x + 2 * y + c], local_sem)
        mine.start()
        sends, recvs = [], []
        for m in range(1, 8):
            px = 1 - x if m & 4 else x
            py = 1 - y if m & 2 else y
            pc = 1 - c if m & 1 else c
            sends.append(_remote(v_ref, out_ref.at[4 * x + 2 * y + c], send_sems.at[m - 1], recv_sems.at[m - 1], (px, py, pc)))
            recvs.append(_remote(v_ref, out_ref.at[4 * px + 2 * py + pc], send_sems.at[m - 1], recv_sems.at[m - 1], (px, py, pc)))
        for cp in sends:
            cp.start()
        for cp in recvs:
            cp.wait_recv()
        for cp in sends:
            cp.wait_send()
        mine.wait()

    return pl.pallas_call(
        body, name=name, in_specs=[HBM], out_specs=HBM,
        out_shape=jax.ShapeDtypeStruct((8, 1, n), v.dtype),
        scratch_shapes=[pltpu.SemaphoreType.DMA((7,)), pltpu.SemaphoreType.DMA((7,)), pltpu.SemaphoreType.DMA],
    )(v)


def exchange_rows(part, name):
    w = part.shape[2]

    def body(p_ref, out_ref, send_sems, recv_sems, local_sem):
        x, y, c = _place()
        k = 2 * x + y
        mine = pltpu.make_async_copy(p_ref.at[4 * x + 2 * y + c], out_ref.at[k], local_sem)
        mine.start()
        sends, recvs = [], []
        for j, (px, py) in enumerate(_other_chips(x, y)):
            sends.append(_remote(p_ref.at[4 * px + 2 * py + c], out_ref.at[k], send_sems.at[j], recv_sems.at[j], (px, py, c)))
            recvs.append(_remote(p_ref.at[4 * px + 2 * py + c], out_ref.at[2 * px + py], send_sems.at[j], recv_sems.at[j], (px, py, c)))
        for cp in sends:
            cp.start()
        for cp in recvs:
            cp.wait_recv()
        for cp in sends:
            cp.wait_send()
        mine.wait()

    return pl.pallas_call(
        body, name=name, in_specs=[HBM], out_specs=HBM,
        out_shape=jax.ShapeDtypeStruct((4, 1, w), part.dtype),
        scratch_shapes=[pltpu.SemaphoreType.DMA((3,)), pltpu.SemaphoreType.DMA((3,)), pltpu.SemaphoreType.DMA],
    )(part)


def gather_weights(shards):
    n = len(shards)

    def body(*refs):
        w_refs, out_refs = refs[:n], refs[n:2 * n]
        send_sems, recv_sems, local_sems = refs[2 * n:]
        x, y, c = _place()
        chips = _other_chips(x, y)

        def half(i, px, py, pc):
            hr = shards[i].shape[0] // 2
            return out_refs[i].at[2 * px + py, pl.ds(pc * hr, hr), :]

        mine = [pltpu.make_async_copy(w_refs[i], out_refs[i].at[2 * x + y], local_sems.at[i]) for i in range(n)]
        for cp in mine:
            cp.start()
        first, passed = [], []
        for i in range(n):
            hr = shards[i].shape[0] // 2
            for j, (px, py) in enumerate(chips):
                cp = _remote(w_refs[i].at[pl.ds(c * hr, hr), :], half(i, x, y, c),
                             send_sems.at[j * n + i], recv_sems.at[j * n + i], (px, py, c))
                cp.start()
                first.append(cp)
        for i in range(n):
            for j, (px, py) in enumerate(chips):
                mine_half = half(i, px, py, c)
                _remote(mine_half, mine_half, send_sems.at[j * n + i], recv_sems.at[j * n + i], (px, py, c)).wait_recv()
                cp = _remote(mine_half, mine_half, send_sems.at[(3 + j) * n + i], recv_sems.at[(3 + j) * n + i], (x, y, 1 - c))
                cp.start()
                passed.append(cp)
        for i in range(n):
            for j, (px, py) in enumerate(chips):
                other = half(i, px, py, 1 - c)
                _remote(other, other, send_sems.at[(3 + j) * n + i], recv_sems.at[(3 + j) * n + i], (x, y, 1 - c)).wait_recv()
        for cp in first + passed:
            cp.wait_send()
        for cp in mine:
            cp.wait()

    return pl.pallas_call(
        body, name="gather_weights", in_specs=[HBM] * n, out_specs=[HBM] * n,
        out_shape=[jax.ShapeDtypeStruct((4,) + s.shape, s.dtype) for s in shards],
        scratch_shapes=[pltpu.SemaphoreType.DMA((6 * n,)), pltpu.SemaphoreType.DMA((6 * n,)), pltpu.SemaphoreType.DMA((n,))],
    )(*shards)


def pair_exchange(slabs):
    n = len(slabs)

    def body(*refs):
        g_refs, out_refs = refs[:n], refs[n:2 * n]
        send_sems, recv_sems = refs[2 * n:]
        x, y, c = _place()
        copies = []
        for i in range(n):
            hr = slabs[i].shape[1] // 2
            cp = _remote(g_refs[i].at[:, pl.ds((1 - c) * hr, hr), :], out_refs[i], send_sems.at[i], recv_sems.at[i], (x, y, 1 - c))
            cp.start()
            copies.append(cp)
        for cp in copies:
            cp.wait()

    return pl.pallas_call(
        body, name="pair_exchange", in_specs=[HBM] * n, out_specs=[HBM] * n,
        out_shape=[jax.ShapeDtypeStruct((4, s.shape[1] // 2, s.shape[2]), s.dtype) for s in slabs],
        scratch_shapes=[pltpu.SemaphoreType.DMA((n,)), pltpu.SemaphoreType.DMA((n,))],
    )(*slabs)


def _row_tile(rows, cols):
    fits = lambda r: r * cols * 4 <= BLOCK_BYTES
    if fits(rows):
        return rows
    return next(r for r in (1024, 512, 256, 128, 64, 32, 16) if rows % r == 0 and fits(r))


def pair_add(g, p, c, name):
    _, hr, cols = p.shape
    tm = _row_tile(hr, cols)
    per = hr // tm

    def body(c_ref, g_ref, p_ref, o_ref):
        o_ref[...] = (g_ref[...] + p_ref[...]).astype(o_ref.dtype)

    return pl.pallas_call(
        body, name=name,
        grid_spec=pltpu.PrefetchScalarGridSpec(
            num_scalar_prefetch=1, grid=(4, per),
            in_specs=[pl.BlockSpec((None, tm, cols), lambda k, i, c_ref: (k, c_ref[0] * per + i, 0)),
                      pl.BlockSpec((None, tm, cols), lambda k, i, c_ref: (k, i, 0))],
            out_specs=pl.BlockSpec((None, tm, cols), lambda k, i, c_ref: (k, i, 0))),
        out_shape=jax.ShapeDtypeStruct((4, hr, cols), BF16),
        compiler_params=_params(("arbitrary", "arbitrary")),
    )(c.reshape(1).astype(jnp.int32), g, p)


def scatter_sums(sums):
    n = len(sums)

    def body(*refs):
        s_refs, out_refs = refs[:n], refs[n:2 * n]
        send_sems, recv_sems, local_sems = refs[2 * n:]
        x, y, c = _place()
        k = 2 * x + y
        mine = [pltpu.make_async_copy(s_refs[i].at[k], out_refs[i].at[k], local_sems.at[i]) for i in range(n)]
        for cp in mine:
            cp.start()
        sends, recvs = [], []
        for i in range(n):
            for j, (px, py) in enumerate(_other_chips(x, y)):
                sems = (send_sems.at[j * n + i], recv_sems.at[j * n + i])
                sends.append(_remote(s_refs[i].at[2 * px + py], out_refs[i].at[k], *sems, (px, py, c)))
                recvs.append(_remote(s_refs[i].at[2 * px + py], out_refs[i].at[2 * px + py], *sems, (px, py, c)))
        for cp in sends:
            cp.start()
        for cp in recvs:
            cp.wait_recv()
        for cp in sends:
            cp.wait_send()
        for cp in mine:
            cp.wait()

    return pl.pallas_call(
        body, name="scatter_sums", in_specs=[HBM] * n, out_specs=[HBM] * n,
        out_shape=[jax.ShapeDtypeStruct(s.shape, s.dtype) for s in sums],
        scratch_shapes=[pltpu.SemaphoreType.DMA((3 * n,)), pltpu.SemaphoreType.DMA((3 * n,)), pltpu.SemaphoreType.DMA((n,))],
    )(*sums)


def sum_chips(rb, name):
    _, hr, cols = rb.shape
    tm = _row_tile(hr, 4 * cols)

    def body(r_ref, o_ref):
        v = r_ref[...].astype(F32)
        o_ref[...] = ((v[0] + v[1]) + v[2]) + v[3]

    return pl.pallas_call(
        body, name=name, grid=(hr // tm,),
        in_specs=[pl.BlockSpec((4, tm, cols), lambda i: (0, i, 0))],
        out_specs=pl.BlockSpec((tm, cols), lambda i: (i, 0)),
        out_shape=jax.ShapeDtypeStruct((hr, cols), F32),
        compiler_params=_params(("arbitrary",)),
    )(rb)


def exchange_halves(halves):
    n = len(halves)

    def body(*refs):
        h_refs, out_refs = refs[:n], refs[n:2 * n]
        send_sems, recv_sems, local_sems = refs[2 * n:]
        x, y, c = _place()
        mine, sends, recvs = [], [], []
        for i in range(n):
            hr = halves[i].shape[0]
            own = out_refs[i].at[pl.ds(c * hr, hr), :]
            mine.append(pltpu.make_async_copy(h_refs[i], own, local_sems.at[i]))
            sends.append(_remote(h_refs[i], own, send_sems.at[i], recv_sems.at[i], (x, y, 1 - c)))
            recvs.append(_remote(h_refs[i], out_refs[i].at[pl.ds((1 - c) * hr, hr), :], send_sems.at[i], recv_sems.at[i],
                                 (x, y, 1 - c)))
        for cp in mine + sends:
            cp.start()
        for cp in recvs:
            cp.wait_recv()
        for cp in sends:
            cp.wait_send()
        for cp in mine:
            cp.wait()

    return pl.pallas_call(
        body, name="exchange_halves", in_specs=[HBM] * n, out_specs=[HBM] * n,
        out_shape=[jax.ShapeDtypeStruct((2 * h.shape[0], h.shape[1]), h.dtype) for h in halves],
        scratch_shapes=[pltpu.SemaphoreType.DMA((n,)), pltpu.SemaphoreType.DMA((n,)), pltpu.SemaphoreType.DMA((n,))],
    )(*halves)


def _relayout(name, arrays, in_blocks, out_blocks, out_shapes, fn):
    rows = 128
    spec = lambda blk: pl.BlockSpec(blk, (lambda i: (0, i, 0)) if len(blk) == 3 else (lambda i: (i, 0)))

    def body(*refs):
        n_in = len(arrays)
        outs = fn(*[r[...] for r in refs[:n_in]])
        for ref, val in zip(refs[n_in:], outs, strict=True):
            if isinstance(val, list):
                for k, piece in enumerate(val):
                    ref[k] = piece
            else:
                ref[...] = val

    return pl.pallas_call(
        body, name=name, grid=(D // rows,),
        in_specs=[spec(b) for b in in_blocks], out_specs=[spec(b) for b in out_blocks], out_shape=out_shapes,
        compiler_params=_params(("arbitrary",)),
    )(*arrays)


def assemble_in_proj(g):
    def fn(v):
        w = jnp.concatenate([v[k] for k in range(4)], axis=1)
        return (jnp.concatenate([w[:, :ORIG_Z], w[:, ORIG_GA:], w[:, ORIG_XBC:ORIG_DT], w[:, ORIG_Z:ORIG_XBC],
                                 w[:, ORIG_DT:ORIG_GA], jnp.zeros((w.shape[0], IN_PAD - IN_ORIG), w.dtype)], axis=1),)

    cols = g.shape[2]
    return _relayout("assemble_in_proj", [g], [(4, 128, cols)], [(128, IN_PAD)],
                     [jax.ShapeDtypeStruct((D, IN_PAD), g.dtype)], fn)[0]


def split_in_proj(dw):
    cols = IN_ORIG // 4

    def fn(d):
        w = jnp.concatenate([d[:, :COL_GA], d[:, COL_Z:COL_DT], d[:, COL_XBC:COL_Z], d[:, COL_DT:COL_DT + 32],
                             d[:, COL_GA:COL_XBC]], axis=1)
        return ([w[:, k * cols:(k + 1) * cols] for k in range(4)],)

    return _relayout("split_in_proj", [dw], [(128, IN_PAD)], [(4, 128, cols)],
                     [jax.ShapeDtypeStruct((4, D, cols), dw.dtype)], fn)[0]


def assemble_ffn_in(gate, up):
    fn = lambda a, b: (jnp.concatenate([a[k] for k in range(4)] + [b[k] for k in range(4)], axis=1),)
    cols = gate.shape[2]
    return _relayout("assemble_ffn_in", [gate, up], [(4, 128, cols)] * 2, [(128, 2 * D_FF)],
                     [jax.ShapeDtypeStruct((D, 2 * D_FF), gate.dtype)], fn)[0]


def split_ffn_in(dw):
    cols = D_FF // 4

    def fn(d):
        return ([d[:, k * cols:(k + 1) * cols] for k in range(4)],
                [d[:, D_FF + k * cols:D_FF + (k + 1) * cols] for k in range(4)])

    shape = jax.ShapeDtypeStruct((4, D, cols), dw.dtype)
    return _relayout("split_ffn_in", [dw], [(128, 2 * D_FF)], [(4, 128, cols)] * 2, [shape, shape], fn)


def ada_prepare(c_all, w_ada, hgrn_lb):
    def body(c_ref, w_ref, lb_ref, mod_ref, row_ref):
        mod_ref[...] = hdot(silu(c_ref[...]), w_ref[...])
        row_ref[...] = sigmoid(lb_ref[0:1, :] - lb_ref[1:2, :])

    return pl.pallas_call(
        body, name="ada_prepare",
        out_shape=[jax.ShapeDtypeStruct((8, w_ada.shape[1]), F32), jax.ShapeDtypeStruct((1, D), F32)],
        compiler_params=pltpu.CompilerParams(vmem_limit_bytes=VMEM_LIMIT),
    )(c_all, w_ada, hgrn_lb)


SMALL_SEGS = (("mod", 6 * D), ("lb", D), ("gnorm", LANES), ("conv_w", 4 * CONV_DIM), ("conv_b", CONV_DIM),
              ("dt_bias", B_INNER), ("a_log", B_INNER), ("d", B_INNER), ("ssm_norm", B_INNER),
              ("ln1_g", D), ("ln1_b", D), ("ln2_g", D), ("ln2_b", D))
SMALL_PARAMS = ("b_ada", "hgrn_lb", "hgrn_gnorm", "ssm_conv_b", "ssm_dt_bias", "ssm_a_log", "ssm_d", "ssm_norm",
                "ln1_g", "ln1_b", "ln2_g", "ln2_b")


def finalize_small(g_all, c_all, dmod_cols, params, m, v):
    n_p = len(SMALL_PARAMS)
    offs, o = {}, 0
    for nm, width in SMALL_SEGS:
        offs[nm] = (o, width)
        o += width

    def body(*refs):
        g_ref, c_ref, dm_ref = refs[:3]
        p_refs = refs[3:3 + n_p]
        m_refs = refs[3 + n_p:3 + 2 * n_p]
        v_refs = refs[3 + 2 * n_p:3 + 3 * n_p]
        outs = refs[3 + 3 * n_p:]
        gwa_ref, gcw_ref = outs[:2]
        res = outs[2:]
        total = jnp.sum(g_ref[...], axis=0, keepdims=True)
        seg = lambda nm: total[:, offs[nm][0]:offs[nm][0] + offs[nm][1]]
        gwa_ref[...] = hdot(silu(c_ref[...]), dm_ref[...], "tn")
        cw = seg("conv_w")
        for j in range(4):
            gcw_ref[j:j + 1, :] = cw[:, j * CONV_DIM:(j + 1) * CONV_DIM]
        hc = lax.broadcasted_iota(jnp.int32, (B_INNER, LANES), 0)
        hj = lax.broadcasted_iota(jnp.int32, (B_INNER, LANES), 1)
        per_head = ((hc >> 6) == hj).astype(F32)
        heads = lambda nm: hdot(jnp.broadcast_to(seg(nm), (8, B_INNER)), per_head)[0:1, 0:32]
        lbp = sigmoid(p_refs[1][0:1, :] - p_refs[1][1:2, :])
        g_row = seg("lb") * lbp * (1.0 - lbp)
        grads = {"b_ada": seg("mod"), "hgrn_gnorm": seg("gnorm"), "ssm_conv_b": seg("conv_b"),
                 "ssm_dt_bias": heads("dt_bias"), "ssm_a_log": heads("a_log"), "ssm_d": heads("d"),
                 "ssm_norm": seg("ssm_norm"), "ln1_g": seg("ln1_g"), "ln1_b": seg("ln1_b"),
                 "ln2_g": seg("ln2_g"), "ln2_b": seg("ln2_b")}
        for i, nm in enumerate(SMALL_PARAMS):
            g_out, d_out, m_out, v_out = res[4 * i:4 * i + 4]
            if nm == "hgrn_lb":
                for row, gv in ((0, g_row), (1, -g_row)):
                    sl = slice(row, row + 1)
                    dl, mn, vn = adamw(p_refs[i][sl, :], gv, m_refs[i][sl, :], v_refs[i][sl, :])
                    g_out[sl, :], d_out[sl, :], m_out[sl, :], v_out[sl, :] = gv, dl, mn, vn
            else:
                gv = grads[nm]
                dl, mn, vn = adamw(p_refs[i][...], gv, m_refs[i][...], v_refs[i][...])
                g_out[...], d_out[...], m_out[...], v_out[...] = gv, dl, mn, vn

    out_shape = [jax.ShapeDtypeStruct((D, dmod_cols.shape[1]), F32), jax.ShapeDtypeStruct((4, CONV_DIM), F32)]
    for p in params:
        out_shape += [jax.ShapeDtypeStruct(p.shape, F32)] * 4
    return pl.pallas_call(
        body, name="finalize_small", out_shape=out_shape,
        compiler_params=pltpu.CompilerParams(vmem_limit_bytes=VMEM_LIMIT),
    )(g_all, c_all, dmod_cols, *params, *m, *v)


def adam_update(w, g, m, v, name):
    cols = w.shape[1]
    return rowwise(name, lambda r, c: (adamw(*r), ()), [_full(w), _full(g), _full(m), _full(v)], [],
                   [(cols, F32)] * 3, tm_max=128)


def kernel(x, c, w_ada, b_ada, w_in, hgrn_lb, hgrn_gnorm, ssm_conv_w, ssm_conv_b, ssm_dt_bias, ssm_a_log, ssm_d, ssm_norm, w_branch_a, w_branch_b, w_o, ln1_g, ln1_b, w_ffn_gate, w_ffn_up, w_ffn_down, ln2_g, ln2_b, loss_target, m_w_ada, m_b_ada, m_w_in, m_hgrn_lb, m_hgrn_gnorm, m_ssm_conv_w, m_ssm_conv_b, m_ssm_dt_bias, m_ssm_a_log, m_ssm_d, m_ssm_norm, m_w_branch_a, m_w_branch_b, m_w_o, m_ln1_g, m_ln1_b, m_w_ffn_gate, m_w_ffn_up, m_w_ffn_down, m_ln2_g, m_ln2_b, v_w_ada, v_b_ada, v_w_in, v_hgrn_lb, v_hgrn_gnorm, v_ssm_conv_w, v_ssm_conv_b, v_ssm_dt_bias, v_ssm_a_log, v_ssm_d, v_ssm_norm, v_w_branch_a, v_w_branch_b, v_w_o, v_ln1_g, v_ln1_b, v_w_ffn_gate, v_w_ffn_up, v_w_ffn_down, v_ln2_g, v_ln2_b):
    given = dict(locals())
    chip = 2 * lax.axis_index("x") + lax.axis_index("y")
    core = lax.axis_index("c")
    t = x.shape[1]

    first = gather_rows(jnp.concatenate([c, ssm_conv_w.reshape(1, CONV_DIM)], axis=1), "gather_cond").reshape(8, D + CONV_DIM)
    c_all = first[:, :D]
    conv_w = first[0::2, D:].reshape(4, 4, CONV_DIM // 4).transpose(1, 0, 2).reshape(4, CONV_DIM)
    mod_part, lb_row = ada_prepare(c_all, w_ada[0], hgrn_lb)
    mod_cols = w_ada.shape[2]
    mod_row = exchange_rows(mod_part.reshape(8, 1, mod_cols), "exchange_mod").reshape(1, 6 * D) + b_ada
    mod = tuple(mod_row[:, i * D:(i + 1) * D] for i in range(6))

    got = dict(zip(SHARDED, gather_weights([given[nm][0].astype(BF16) for nm in SHARDED]), strict=True))
    whole = lambda nm: got[nm].reshape(4 * got[nm].shape[1], got[nm].shape[2])
    wts = (assemble_in_proj(got["w_in"]), whole("w_branch_a"), whole("w_branch_b"), whole("w_o"),
           assemble_ffn_in(got["w_ffn_gate"], got["w_ffn_up"]), whole("w_ffn_down"))

    per_channel = lambda p: jnp.repeat(p[0], B_INNER // 32)[None]
    small = (lb_row, hgrn_gnorm, conv_w, ssm_conv_b, per_channel(ssm_dt_bias), per_channel(ssm_a_log),
             per_channel(ssm_d), ssm_norm, ln1_g, ln1_b, ln2_g, ln2_b)
    loss, grad_x, d_mod, d_wts, d_small = local_step(x[0], loss_target[0], mod, wts, small)

    d_lb, d_gn, d_cw, d_cb, d_dtb, d_alog, d_dsk, d_nw, d_l1g, d_l1b, d_l2g, d_l2b = d_small
    row = jnp.concatenate(list(d_mod) + [d_lb, d_gn, d_cw.reshape(1, 4 * CONV_DIM), d_cb, d_dtb, d_alog, d_dsk, d_nw,
                                          d_l1g, d_l1b, d_l2g, d_l2b], axis=1)
    g_all = gather_rows(row, "gather_small_grads").reshape(8, row.shape[1])
    dmod_cols = lax.dynamic_slice_in_dim(g_all, chip * mod_cols, mod_cols, axis=1)
    fin = finalize_small(g_all, c_all, dmod_cols, [given[n] for n in SMALL_PARAMS],
                         [given["m_" + n] for n in SMALL_PARAMS], [given["v_" + n] for n in SMALL_PARAMS])
    grads, deltas, new_m, new_v = {}, {}, {}, {}
    grads["w_ada"] = fin[0][None]
    grads["ssm_conv_w"] = lax.dynamic_slice_in_dim(fin[1], chip * (CONV_DIM // 4), CONV_DIM // 4, axis=1)[None]
    for i, nm in enumerate(SMALL_PARAMS):
        grads[nm], deltas[nm], new_m[nm], new_v[nm] = fin[2 + 4 * i:6 + 4 * i]

    dw_in, dw_a, dw_b, dw_o, dw_gu, dw_d = d_wts
    by_rows = lambda g: g.reshape(4, g.shape[0] // 4, g.shape[1])
    d_gate, d_up = split_ffn_in(dw_gu)
    slabs = [split_in_proj(dw_in), by_rows(dw_a), by_rows(dw_b), by_rows(dw_o), d_gate, d_up, by_rows(dw_d)]
    pairs = [pair_add(s, r, core, "pair_add_" + nm) for nm, s, r in zip(SHARDED, slabs, pair_exchange(slabs), strict=True)]
    halves = [sum_chips(r, "sum_chips_" + nm) for nm, r in zip(SHARDED, scatter_sums(pairs), strict=True)]
    for nm, r in zip(SHARDED, exchange_halves(halves), strict=True):
        grads[nm] = r[None]
    for nm in ("w_ada", "ssm_conv_w") + SHARDED:
        shp = given[nm].shape
        two_d = lambda a: a.reshape(shp[-2], shp[-1])
        d_, m_, v_ = adam_update(two_d(given[nm]), two_d(grads[nm]), two_d(given["m_" + nm]), two_d(given["v_" + nm]),
                                 "adam_" + nm)
        deltas[nm], new_m[nm], new_v[nm] = d_.reshape(shp), m_.reshape(shp), v_.reshape(shp)

    names = ("w_ada", "b_ada", "w_in", "hgrn_lb", "hgrn_gnorm", "ssm_conv_w", "ssm_conv_b", "ssm_dt_bias", "ssm_a_log",
             "ssm_d", "ssm_norm", "w_branch_a", "w_branch_b", "w_o", "ln1_g", "ln1_b", "w_ffn_gate", "w_ffn_up",
             "w_ffn_down", "ln2_g", "ln2_b")
    total_loss = lax.psum(loss[0, 0], ("x", "y", "c"))
    return (total_loss, grad_x[None], *[grads[n] for n in names], *[deltas[n] for n in names],
            *[new_m[n] for n in names], *[new_v[n] for n in names])
```

```python
import functools

import jax
import jax.numpy as jnp
from jax import lax
from jax.experimental import pallas as pl
from jax.experimental.pallas import tpu as pltpu

F32, BF16 = jnp.float32, jnp.bfloat16
HI = lax.Precision.HIGHEST
MESH = pl.DeviceIdType.MESH

D = 1024
CHUNK = 64
LANES = 128
N_HEADS_A = 8
N_GROUPS_B = 4
B_INNER = 2048
CONV_DIM = 3072
D_FF = 2816
ALPHA = 2.0 ** 0.25
LN_EPS = 1e-5
RMS_EPS = 1e-6
ADAM_LR, ADAM_B1, ADAM_B2, ADAM_EPS, ADAM_WD, ADAM_STEP = 0.001, 0.9, 0.999, 1e-08, 0.01, 10

IN_ORIG = 11296
IN_PAD = 11520
COL_GA, COL_GB, COL_XBC, COL_Z, COL_DT = 4096, 5120, 6144, 9216, 11264
ORIG_Z, ORIG_XBC, ORIG_DT, ORIG_GA = 4096, 6144, 9216, 9248

SHARDED = ("w_in", "w_branch_a", "w_branch_b", "w_o", "w_ffn_gate", "w_ffn_up", "w_ffn_down")
VMEM_LIMIT = 48 * 1024 * 1024
BLOCK_BYTES = 2 * 1024 * 1024

_DIMS = {"nn": (((1,), (0,)), ((), ())), "nt": (((1,), (1,)), ((), ())), "tn": (((0,), (0,)), ((), ()))}


def _bd(a, b, mode):
    return lax.dot_general(a.astype(BF16), b.astype(BF16), _DIMS[mode], preferred_element_type=F32)


@functools.partial(jax.custom_vjp, nondiff_argnums=(2,))
def bdot(a, b, mode):
    return _bd(a, b, mode)


def _bdot_fwd(a, b, mode):
    return _bd(a, b, mode), (a, b)


def _bdot_bwd(mode, res, g):
    a, b = res
    if mode == "nn":
        return _bd(g, b, "nt"), _bd(a, g, "tn")
    if mode == "nt":
        return _bd(g, b, "nn"), _bd(g, a, "tn")
    return _bd(b, g, "nt"), _bd(a, g, "nn")


bdot.defvjp(_bdot_fwd, _bdot_bwd)


def hdot(a, b, mode="nn"):
    return lax.dot_general(a, b, _DIMS[mode], precision=HI, preferred_element_type=F32)


def _raw(a, b, mode):
    return lax.dot_general(a, b, _DIMS[mode], preferred_element_type=F32)


def _split(x, n):
    parts, rest = [], x
    for _ in range(n):
        p = rest.astype(BF16)
        parts.append(p)
        rest = rest - p.astype(F32)
    return parts


def _od(a, b, mode, exact):
    if exact == 1:
        e = b.astype(BF16)
        p = _split(a, 3)
        return (_raw(p[2], e, mode) + _raw(p[1], e, mode)) + _raw(p[0], e, mode)
    e = a.astype(BF16)
    p = _split(b, 3)
    return (_raw(e, p[2], mode) + _raw(e, p[1], mode)) + _raw(e, p[0], mode)


@functools.partial(jax.custom_vjp, nondiff_argnums=(2, 3))
def odot(a, b, mode, exact):
    return _od(a, b, mode, exact)


def _odot_fwd(a, b, mode, exact):
    return _od(a, b, mode, exact), (a, b)


def _odot_bwd(mode, exact, res, g):
    a, b = res
    if exact == 1:
        da = {"nn": lambda: _od(g, b, "nt", 1), "nt": lambda: _od(g, b, "nn", 1), "tn": lambda: _od(b, g, "nt", 0)}[mode]()
        return da, jnp.zeros_like(b)
    db = {"nn": lambda: _od(a, g, "tn", 0), "nt": lambda: _od(g, a, "tn", 1), "tn": lambda: _od(a, g, "nn", 0)}[mode]()
    return jnp.zeros_like(a), db


odot.defvjp(_odot_fwd, _odot_bwd)


def _d3(a, b, mode):
    ah, al = _split(a, 2)
    bh, bl = _split(b, 2)
    return _raw(ah, bh, mode) + (_raw(ah, bl, mode) + _raw(al, bh, mode))


@functools.partial(jax.custom_vjp, nondiff_argnums=(2,))
def dot3(a, b, mode):
    return _d3(a, b, mode)


def _dot3_fwd(a, b, mode):
    return _d3(a, b, mode), (a, b)


def _dot3_bwd(mode, res, g):
    a, b = res
    if mode == "nn":
        return _d3(g, b, "nt"), _d3(a, g, "tn")
    if mode == "nt":
        return _d3(g, b, "nn"), _d3(g, a, "tn")
    return _d3(b, g, "nt"), _d3(a, g, "nn")


dot3.defvjp(_dot3_fwd, _dot3_bwd)


@jax.custom_vjp
def split_lanes(x):
    return x[:, :256], x[:, 256:]


split_lanes.defvjp(lambda x: ((x[:, :256], x[:, 256:]), None),
                   lambda _, g: (jnp.concatenate(g, axis=1),))


def sigmoid(x):
    return 1.0 / (1.0 + jnp.exp(-x))


def silu(x):
    return x * sigmoid(x)


def softplus(x):
    return jnp.maximum(x, 0.0) + jnp.log1p(jnp.exp(jnp.minimum(x, -x)))


def _ln(x):
    mu = jnp.mean(x, axis=-1, keepdims=True)
    xc = x - mu
    return xc * lax.rsqrt(jnp.mean(xc * xc, axis=-1, keepdims=True) + LN_EPS)


def _tril64():
    r = lax.broadcasted_iota(jnp.int32, (CHUNK, CHUNK), 0)
    c = lax.broadcasted_iota(jnp.int32, (CHUNK, CHUNK), 1)
    return (r >= c).astype(F32)


def hgrn_chunk(q, fl, iv, gr, st, lb, gn, tril):
    f = lb + (1.0 - lb) * sigmoid(fl)
    gl = jnp.log(f)
    k = 1.0 - f
    qf = silu(q) * (128 ** -0.5)
    b = odot(tril, gl, "nn", 0)
    blast = jnp.sum(gl, axis=0, keepdims=True)
    ref = lax.stop_gradient(0.5 * blast)
    qp = qf * jnp.exp(b - ref)
    kp = k * jnp.exp(ref - b)
    sc = dot3(qp, kp, "nt") * tril
    o = bdot(sc, iv, "nn") + bdot(qf * jnp.exp(b), st, "nt")
    st_new = st * jnp.exp(blast) + bdot(iv, k * jnp.exp(blast - b), "tn")
    on = o * lax.rsqrt(jnp.mean(o * o, axis=-1, keepdims=True) + RMS_EPS) * gn
    return on * silu(gr), st_new


def ssd_consts(g):
    i32 = jnp.int32
    ej = lax.broadcasted_iota(i32, (LANES, 512), 0)
    ec = lax.broadcasted_iota(i32, (LANES, 512), 1)
    expand = (ej == g * 8 + (ec >> 6)).astype(F32)
    ts = lax.broadcasted_iota(i32, (CHUNK, 512), 0)
    tc = lax.broadcasted_iota(i32, (CHUNK, 512), 1)
    itile = (ts == (tc & 63)).astype(F32)
    maskall = ts >= (tc & 63)
    br = lax.broadcasted_iota(i32, (256, 256), 0)
    bc = lax.broadcasted_iota(i32, (256, 256), 1)
    blockmask = ((br >> 6) == (bc >> 6)).astype(F32)
    return expand, itile, maskall, blockmask, _tril64()


def ssd_chunk(x, bm, cm, dt, z, st, dtb, alog, dsk, nw, cs):
    expand, itile, maskall, blockmask, tril = cs
    delta = softplus(odot(dt, expand, "nn", 1) + dtb)
    a = -jnp.exp(alog) * delta
    acum = odot(tril, a, "nn", 0)
    alast = jnp.sum(a, axis=0, keepdims=True)
    xdt = x * delta
    cb = bdot(cm, jnp.concatenate([bm] * 8, axis=0), "nt")
    arow = jnp.sum(acum * itile, axis=0, keepdims=True)
    dec = jnp.where(maskall, jnp.exp(jnp.minimum(acum - arow, 0.0)), 0.0)
    intra = [bdot(m, jnp.concatenate([xh] * 4, axis=0) * blockmask, "nn")
             for m, xh in zip(split_lanes(cb * dec), split_lanes(xdt))]
    y = jnp.concatenate(intra, axis=1) + bdot(cm, st, "nn") * jnp.exp(acum)
    st_new = st * jnp.exp(alast) + bdot(bm, xdt * jnp.exp(alast - acum), "tn")
    yz = (y + x * dsk) * silu(z)
    return yz * lax.rsqrt(jnp.mean(yz * yz, axis=-1, keepdims=True) + RMS_EPS) * nw, st_new


def adamw(w, g, m, v):
    m = ADAM_B1 * m + (1.0 - ADAM_B1) * g
    v = ADAM_B2 * v + (1.0 - ADAM_B2) * jnp.square(g)
    m_hat = m / (1.0 - ADAM_B1 ** ADAM_STEP)
    v_hat = v / (1.0 - ADAM_B2 ** ADAM_STEP)
    return -ADAM_LR * (m_hat / (jnp.sqrt(v_hat) + ADAM_EPS) + ADAM_WD * w), m, v


def _pick(n, cands):
    for c in cands:
        if n % c == 0:
            return c
    return n


def _params(sem):
    return pltpu.CompilerParams(dimension_semantics=sem, vmem_limit_bytes=VMEM_LIMIT)


def matmul(a, b, mode, out_dtype, name):
    if mode == "nn":
        (m, k), n = a.shape, b.shape[1]
    elif mode == "nt":
        (m, k), n = a.shape, b.shape[0]
    else:
        (k, m), n = a.shape, b.shape[1]
    tm = _pick(m, (512, 256, 128))
    tn = _pick(n, (1408, 1024, 768, 512, 256, 128))
    tk = _pick(k, (2304, 2048, 1408, 1024, 768, 512, 256, 128))
    nk = k // tk
    a_spec = pl.BlockSpec((tk, tm), lambda i, j, kk: (kk, i)) if mode == "tn" else pl.BlockSpec((tm, tk), lambda i, j, kk: (i, kk))
    b_spec = pl.BlockSpec((tn, tk), lambda i, j, kk: (j, kk)) if mode == "nt" else pl.BlockSpec((tk, tn), lambda i, j, kk: (kk, j))

    def body(a_ref, b_ref, o_ref, *acc):
        part = _bd(a_ref[...], b_ref[...], mode)
        if nk == 1:
            o_ref[...] = part.astype(o_ref.dtype)
            return
        acc_ref, = acc
        kk = pl.program_id(2)

        @pl.when(kk == 0)
        def _():
            acc_ref[...] = part

        @pl.when(jnp.logical_and(kk > 0, kk < nk - 1))
        def _():
            acc_ref[...] += part

        @pl.when(kk == nk - 1)
        def _():
            o_ref[...] = (acc_ref[...] + part).astype(o_ref.dtype)

    return pl.pallas_call(
        body, name=name, grid=(m // tm, n // tn, nk),
        in_specs=[a_spec, b_spec], out_specs=pl.BlockSpec((tm, tn), lambda i, j, kk: (i, j)),
        out_shape=jax.ShapeDtypeStruct((m, n), out_dtype),
        scratch_shapes=[pltpu.VMEM((tm, tn), F32)] if nk > 1 else [],
        compiler_params=_params(("parallel", "parallel", "arbitrary")),
    )(a, b)


def rowwise(name, fn, rows, consts, out_rows, out_accs=(), tm_max=256, into=None, new_wide=None):
    t = rows[0][0].shape[0]
    tm = _pick(t, (tm_max, 128, 64, 32, 16, 8))
    n_r, n_c, n_o = len(rows), len(consts), len(out_rows)
    n_alias = 0 if into is None else 1

    def body(*refs):
        r_in = [r[...] for r in refs[:n_r]]
        c_in = [r[...] for r in refs[n_r:n_r + n_c]]
        refs = refs[:n_r + n_c] + refs[n_r + n_c + n_alias:]
        o_refs = refs[n_r + n_c:n_r + n_c + n_o]
        a_refs = refs[n_r + n_c + n_o:]
        ro, ao = fn(r_in, c_in)
        for ref, val in zip(o_refs, ro, strict=True):
            ref[...] = val.astype(ref.dtype)
        if a_refs:
            @pl.when(pl.program_id(0) == 0)
            def _():
                for ref in a_refs:
                    ref[...] = jnp.zeros_like(ref)

            for ref, val in zip(a_refs, ao, strict=True):
                ref[...] += val

    in_specs = [pl.BlockSpec((tm, w), functools.partial(lambda i, cb: (i, cb), cb=cb)) for _, w, cb in rows]
    in_specs += [pl.BlockSpec(c.shape, lambda i: (0, 0)) for c in consts]
    out_specs = [pl.BlockSpec((tm, w), lambda i: (i, 0)) for w, _ in out_rows]
    out_specs += [pl.BlockSpec(s, lambda i: (0, 0)) for s in out_accs]
    out_shape = [jax.ShapeDtypeStruct((t, w), dt) for w, dt in out_rows]
    out_shape += [jax.ShapeDtypeStruct(s, F32) for s in out_accs]
    operands = [r[0] for r in rows] + list(consts)
    aliases = {}
    if into is not None:
        target, cb = into
        in_specs.append(pl.BlockSpec(memory_space=pl.ANY))
        operands.append(target)
        out_specs[0] = pl.BlockSpec((tm, out_rows[0][0]), lambda i: (i, cb))
        out_shape[0] = jax.ShapeDtypeStruct(target.shape, target.dtype)
        aliases = {len(operands) - 1: 0}
    if new_wide is not None:
        width, cb = new_wide
        out_specs[0] = pl.BlockSpec((tm, out_rows[0][0]), lambda i: (i, cb))
        out_shape[0] = jax.ShapeDtypeStruct((t, width), out_rows[0][1])
    return pl.pallas_call(
        body, name=name, grid=(t // tm,), in_specs=in_specs, out_specs=out_specs, out_shape=out_shape,
        input_output_aliases=aliases, compiler_params=_params(("arbitrary",)),
    )(*operands)


def _full(a):
    return (a, a.shape[1], 0)


def _time_block(t):
    return _pick(t, (512, 256, 128, 64))


def hgrn_forward(proj, lb, gn):
    t = proj.shape[0]
    tb = _time_block(t)
    nb, nc = t // tb, tb // CHUNK
    lanes = [slice(h * LANES, (h + 1) * LANES) for h in range(N_HEADS_A)]

    def body(qfig_ref, lb_ref, gn_ref, o_ref, st_ref, state):
        @pl.when(pl.program_id(0) == 0)
        def _():
            state[...] = jnp.zeros_like(state)

        tril = _tril64()
        lbv, gnv = lb_ref[...], gn_ref[...]

        def step(c, carry):
            sl = pl.ds(pl.multiple_of(c * CHUNK, CHUNK), CHUNK)
            col = lambda seg, h: slice(seg * D + h * LANES, seg * D + (h + 1) * LANES)
            ins = [(qfig_ref[sl, col(0, h)], qfig_ref[sl, col(1, h)], qfig_ref[sl, col(2, h)], qfig_ref[sl, col(3, h)],
                    state[h], lbv[:, lanes[h]]) for h in range(N_HEADS_A)]
            res = [hgrn_chunk(*a, gnv, tril) for a in ins]
            for h in range(N_HEADS_A):
                st_ref[c, h] = ins[h][4]
                o_ref[sl, lanes[h]] = res[h][0].astype(o_ref.dtype)
                state[h] = res[h][1]
            return carry

        lax.fori_loop(0, nc, step, 0)

    return pl.pallas_call(
        body, name="hgrn_forward", grid=(nb,),
        in_specs=[pl.BlockSpec((tb, 4 * D), lambda j: (j, 0)),
                  pl.BlockSpec((1, D), lambda j: (0, 0)), pl.BlockSpec((1, LANES), lambda j: (0, 0))],
        out_specs=[pl.BlockSpec((tb, D), lambda j: (j, 0)),
                   pl.BlockSpec((nc, N_HEADS_A, LANES, LANES), lambda j: (j, 0, 0, 0))],
        out_shape=[jax.ShapeDtypeStruct((t, D), BF16),
                   jax.ShapeDtypeStruct((t // CHUNK, N_HEADS_A, LANES, LANES), F32)],
        scratch_shapes=[pltpu.VMEM((N_HEADS_A, LANES, LANES), F32)],
        compiler_params=_params(("arbitrary",)),
    )(proj, lb, gn)


def hgrn_backward(proj, states, d_out, lb, gn, d_proj):
    t = proj.shape[0]
    tb = _time_block(t)
    nb, nc = t // tb, tb // CHUNK
    lanes = [slice(h * LANES, (h + 1) * LANES) for h in range(N_HEADS_A)]

    def body(qfig_ref, st_ref, do_ref, lb_ref, gn_ref, _, dqfig_ref, dlb_ref, dgn_ref, d_state):
        @pl.when(pl.program_id(0) == 0)
        def _():
            d_state[...] = jnp.zeros_like(d_state)
            dlb_ref[...] = jnp.zeros_like(dlb_ref)
            dgn_ref[...] = jnp.zeros_like(dgn_ref)

        tril = _tril64()
        lbv, gnv = lb_ref[...], gn_ref[...]

        def step(n, carry):
            c = nc - 1 - n
            sl = pl.ds(pl.multiple_of(c * CHUNK, CHUNK), CHUNK)
            col = lambda seg, h: slice(seg * D + h * LANES, seg * D + (h + 1) * LANES)
            fn = lambda q, fl, iv, gr, st, lb_, gn_: hgrn_chunk(q, fl, iv, gr, st, lb_, gn_, tril)
            ins = [(qfig_ref[sl, col(0, h)], qfig_ref[sl, col(1, h)], qfig_ref[sl, col(2, h)], qfig_ref[sl, col(3, h)],
                    st_ref[c, h], lbv[:, lanes[h]], gnv) for h in range(N_HEADS_A)]
            cots = [(do_ref[sl, lanes[h]], d_state[h]) for h in range(N_HEADS_A)]
            res = [jax.vjp(fn, *a)[1](ct) for a, ct in zip(ins, cots)]
            dgn_total = dgn_ref[...]
            for h in range(N_HEADS_A):
                dq, df, di, dg, dst, dlb, dgn = res[h]
                for seg, val in enumerate((dq, df, di, dg)):
                    dqfig_ref[sl, col(seg, h)] = val.astype(dqfig_ref.dtype)
                d_state[h] = dst
                dlb_ref[:, lanes[h]] += dlb
                dgn_total = dgn_total + dgn
            dgn_ref[...] = dgn_total
            return carry

        lax.fori_loop(0, nc, step, 0)

    rev = lambda j: nb - 1 - j
    return pl.pallas_call(
        body, name="hgrn_backward", grid=(nb,),
        in_specs=[pl.BlockSpec((tb, 4 * D), lambda j: (rev(j), 0)),
                  pl.BlockSpec((nc, N_HEADS_A, LANES, LANES), lambda j: (rev(j), 0, 0, 0)),
                  pl.BlockSpec((tb, D), lambda j: (rev(j), 0)),
                  pl.BlockSpec((1, D), lambda j: (0, 0)), pl.BlockSpec((1, LANES), lambda j: (0, 0)),
                  pl.BlockSpec(memory_space=pl.ANY)],
        out_specs=[pl.BlockSpec((tb, 4 * D), lambda j: (rev(j), 0)),
                   pl.BlockSpec((1, D), lambda j: (0, 0)), pl.BlockSpec((1, LANES), lambda j: (0, 0))],
        out_shape=[jax.ShapeDtypeStruct(d_proj.shape, d_proj.dtype), jax.ShapeDtypeStruct((1, D), F32),
                   jax.ShapeDtypeStruct((1, LANES), F32)],
        input_output_aliases={5: 0},
        scratch_shapes=[pltpu.VMEM((N_HEADS_A, LANES, LANES), F32)],
        compiler_params=_params(("arbitrary",)),
    )(proj, states, d_out, lb, gn, d_proj)


def _ssd_in_specs(tb, tmap):
    return [pl.BlockSpec((tb, 512), lambda g, j: (tmap(j), g)),
            pl.BlockSpec((tb, LANES), lambda g, j: (tmap(j), 16 + g)),
            pl.BlockSpec((tb, LANES), lambda g, j: (tmap(j), 20 + g)),
            pl.BlockSpec((tb, LANES), lambda g, j: (tmap(j), COL_DT // LANES)),
            pl.BlockSpec((tb, 512), lambda g, j: (tmap(j), COL_Z // 512 + g))]


def ssd_forward(xc, proj, dtb, alog, dsk, nw):
    t = proj.shape[0]
    tb = _time_block(t)
    nb, nc = t // tb, tb // CHUNK

    def body(x_ref, b_ref, c_ref, dt_ref, z_ref, dtb_ref, alog_ref, dsk_ref, nw_ref, o_ref, st_ref, state):
        @pl.when(pl.program_id(1) == 0)
        def _():
            state[...] = jnp.zeros_like(state)

        cs = ssd_consts(pl.program_id(0))
        par = (dtb_ref[...], alog_ref[...], dsk_ref[...], nw_ref[...])

        def step(c, carry):
            sl = pl.ds(pl.multiple_of(c * CHUNK, CHUNK), CHUNK)
            st = state[...]
            st_ref[c] = st
            out, st_new = ssd_chunk(x_ref[sl, :], b_ref[sl, :], c_ref[sl, :], dt_ref[sl, :], z_ref[sl, :], st, *par, cs)
            o_ref[sl, :] = out.astype(o_ref.dtype)
            state[...] = st_new
            return carry

        lax.fori_loop(0, nc, step, 0)

    vec = pl.BlockSpec((1, 512), lambda g, j: (0, g))
    return pl.pallas_call(
        body, name="ssd_forward", grid=(N_GROUPS_B, nb),
        in_specs=_ssd_in_specs(tb, lambda j: j) + [vec] * 4,
        out_specs=[pl.BlockSpec((tb, 512), lambda g, j: (j, g)),
                   pl.BlockSpec((nc, None, LANES, 512), lambda g, j: (j, g, 0, 0))],
        out_shape=[jax.ShapeDtypeStruct((t, B_INNER), BF16),
                   jax.ShapeDtypeStruct((t // CHUNK, N_GROUPS_B, LANES, 512), F32)],
        scratch_shapes=[pltpu.VMEM((LANES, 512), F32)],
        compiler_params=_params(("arbitrary", "arbitrary")),
    )(xc, xc, xc, proj, proj, dtb, alog, dsk, nw)


def ssd_backward(xc, proj, states, d_out, dtb, alog, dsk, nw, d_proj):
    t = proj.shape[0]
    tb = _time_block(t)
    nb, nc = t // tb, tb // CHUNK
    rev = lambda j: nb - 1 - j

    def body(x_ref, b_ref, c_ref, dt_ref, z_ref, st_ref, do_ref, dtb_ref, alog_ref, dsk_ref, nw_ref, _,
             dx_ref, db_ref, dc_ref, ddt_ref, dz_ref, ddtb_ref, dalog_ref, ddsk_ref, dnw_ref, d_state):
        accs = (ddtb_ref, dalog_ref, ddsk_ref, dnw_ref)

        @pl.when(pl.program_id(1) == 0)
        def _():
            d_state[...] = jnp.zeros_like(d_state)
            for ref in accs:
                ref[...] = jnp.zeros_like(ref)

        cs = ssd_consts(pl.program_id(0))
        par = (dtb_ref[...], alog_ref[...], dsk_ref[...], nw_ref[...])

        def step(n, carry):
            c = nc - 1 - n
            sl = pl.ds(pl.multiple_of(c * CHUNK, CHUNK), CHUNK)
            fn = lambda *a: ssd_chunk(*a, cs)
            _, vjp = jax.vjp(fn, x_ref[sl, :], b_ref[sl, :], c_ref[sl, :], dt_ref[sl, :], z_ref[sl, :], st_ref[c], *par)
            dx, db, dc, ddt, dz, dst, *dpar = vjp((do_ref[sl, :], d_state[...]))
            dx_ref[sl, :] = dx
            db_ref[sl, :] = db
            dc_ref[sl, :] = dc
            ddt_ref[sl, :] = ddt
            dz_ref[sl, :] = dz.astype(dz_ref.dtype)
            d_state[...] = dst
            for ref, val in zip(accs, dpar, strict=True):
                ref[...] += val
            return carry

        lax.fori_loop(0, nc, step, 0)

    vec = pl.BlockSpec((1, 512), lambda g, j: (0, g))
    acc = pl.BlockSpec((None, 1, 512), lambda g, j: (g, 0, 0))
    return pl.pallas_call(
        body, name="ssd_backward", grid=(N_GROUPS_B, nb),
        in_specs=_ssd_in_specs(tb, rev)
        + [pl.BlockSpec((nc, None, LANES, 512), lambda g, j: (rev(j), g, 0, 0)),
           pl.BlockSpec((tb, 512), lambda g, j: (rev(j), g))] + [vec] * 4 + [pl.BlockSpec(memory_space=pl.ANY)],
        out_specs=[pl.BlockSpec((tb, 512), lambda g, j: (rev(j), g)),
                   pl.BlockSpec((tb, LANES), lambda g, j: (rev(j), g)),
                   pl.BlockSpec((tb, LANES), lambda g, j: (rev(j), g)),
                   pl.BlockSpec((None, tb, LANES), lambda g, j: (g, rev(j), 0)),
                   pl.BlockSpec((tb, 512), lambda g, j: (rev(j), COL_Z // 512 + g)), acc, acc, acc, acc],
        out_shape=[jax.ShapeDtypeStruct((t, B_INNER), F32), jax.ShapeDtypeStruct((t, 512), F32),
                   jax.ShapeDtypeStruct((t, 512), F32), jax.ShapeDtypeStruct((N_GROUPS_B, t, LANES), F32),
                   jax.ShapeDtypeStruct(d_proj.shape, d_proj.dtype)] + [jax.ShapeDtypeStruct((N_GROUPS_B, 1, 512), F32)] * 4,
        input_output_aliases={11: 4},
        scratch_shapes=[pltpu.VMEM((LANES, 512), F32)],
        compiler_params=_params(("arbitrary", "arbitrary")),
    )(xc, xc, xc, proj, proj, states, d_out, dtb, alog, dsk, nw, d_proj)


CONV_HALO = 8


def _shift_down(halo_then_tile, s, tm):
    if s == 0:
        return halo_then_tile[CONV_HALO:CONV_HALO + tm]
    return pltpu.roll(halo_then_tile, s, 0)[CONV_HALO:CONV_HALO + tm]


def _conv_pre(cur, prev, w, b, tm):
    stacked = jnp.concatenate([prev, cur], axis=0)
    taps = [_shift_down(stacked, 3 - j, tm) for j in range(4)]
    pre = b + taps[0] * w[0:1] + taps[1] * w[1:2] + taps[2] * w[2:3] + taps[3] * w[3:4]
    return pre, taps


def _conv_specs(t, tm):
    per = tm // CONV_HALO
    cur = pl.BlockSpec((tm, CONV_DIM), lambda i: (i, COL_XBC // CONV_DIM))
    prev = pl.BlockSpec((CONV_HALO, CONV_DIM), lambda i: (jnp.maximum(i * per - 1, 0), COL_XBC // CONV_DIM))
    return cur, prev


def conv_forward(proj, w, b):
    t = proj.shape[0]
    tm = _pick(t, (256, 128, 64))

    def body(cur_ref, prev_ref, w_ref, b_ref, o_ref):
        prev = jnp.where(pl.program_id(0) == 0, 0.0, prev_ref[...])
        pre, _ = _conv_pre(cur_ref[...], prev, w_ref[...], b_ref[...], tm)
        o_ref[...] = silu(pre)

    cur, prev = _conv_specs(t, tm)
    return pl.pallas_call(
        body, name="conv_forward", grid=(t // tm,),
        in_specs=[cur, prev, pl.BlockSpec((4, CONV_DIM), lambda i: (0, 0)), pl.BlockSpec((1, CONV_DIM), lambda i: (0, 0))],
        out_specs=pl.BlockSpec((tm, CONV_DIM), lambda i: (i, 0)),
        out_shape=jax.ShapeDtypeStruct((t, CONV_DIM), F32),
        compiler_params=_params(("arbitrary",)),
    )(proj, proj, w, b)


def conv_backward_pre(proj, dx, db_, dc_, w, b):
    t = proj.shape[0]
    tm = _pick(t, (256, 128, 64))

    def body(cur_ref, prev_ref, dx_ref, dbm_ref, dcm_ref, w_ref, b_ref, dpre_ref, dw_ref, dbias_ref):
        @pl.when(pl.program_id(0) == 0)
        def _():
            dw_ref[...] = jnp.zeros_like(dw_ref)
            dbias_ref[...] = jnp.zeros_like(dbias_ref)

        prev = jnp.where(pl.program_id(0) == 0, 0.0, prev_ref[...])
        pre, taps = _conv_pre(cur_ref[...], prev, w_ref[...], b_ref[...], tm)
        sg = sigmoid(pre)
        d_out = jnp.concatenate([dx_ref[...], dbm_ref[...], dcm_ref[...]], axis=1)
        dpre = d_out * (sg * (1.0 + pre * (1.0 - sg)))
        dpre_ref[...] = dpre
        dbias_ref[...] += jnp.sum(dpre, axis=0, keepdims=True)
        for j in range(4):
            dw_ref[j:j + 1, :] += jnp.sum(dpre * taps[j], axis=0, keepdims=True)

    cur, prev = _conv_specs(t, tm)
    row = lambda w_: pl.BlockSpec((tm, w_), lambda i: (i, 0))
    return pl.pallas_call(
        body, name="conv_backward_pre", grid=(t // tm,),
        in_specs=[cur, prev, row(B_INNER), row(512), row(512),
                  pl.BlockSpec((4, CONV_DIM), lambda i: (0, 0)), pl.BlockSpec((1, CONV_DIM), lambda i: (0, 0))],
        out_specs=[row(CONV_DIM), pl.BlockSpec((4, CONV_DIM), lambda i: (0, 0)), pl.BlockSpec((1, CONV_DIM), lambda i: (0, 0))],
        out_shape=[jax.ShapeDtypeStruct((t, CONV_DIM), F32), jax.ShapeDtypeStruct((4, CONV_DIM), F32),
                   jax.ShapeDtypeStruct((1, CONV_DIM), F32)],
        compiler_params=_params(("arbitrary",)),
    )(proj, proj, dx, db_, dc_, w, b)


def conv_backward_input(dpre, w, d_proj):
    t = dpre.shape[0]
    tm = _pick(t, (256, 128, 64))
    per = tm // CONV_HALO
    last = t // CONV_HALO - 1
    nt = t // tm

    def body(cur_ref, nxt_ref, w_ref, _, o_ref):
        nxt = jnp.where(pl.program_id(0) == nt - 1, 0.0, nxt_ref[...])
        stacked = jnp.concatenate([cur_ref[...], nxt], axis=0)
        w_ = w_ref[...]
        acc = stacked[0:tm] * w_[3:4]
        for j in range(3):
            s = 3 - j
            acc = acc + pltpu.roll(stacked, tm + CONV_HALO - s, 0)[0:tm] * w_[j:j + 1]
        o_ref[...] = acc.astype(o_ref.dtype)

    return pl.pallas_call(
        body, name="conv_backward_input", grid=(nt,),
        in_specs=[pl.BlockSpec((tm, CONV_DIM), lambda i: (i, 0)),
                  pl.BlockSpec((CONV_HALO, CONV_DIM), lambda i: (jnp.minimum((i + 1) * per, last), 0)),
                  pl.BlockSpec((4, CONV_DIM), lambda i: (0, 0)), pl.BlockSpec(memory_space=pl.ANY)],
        out_specs=pl.BlockSpec((tm, CONV_DIM), lambda i: (i, COL_XBC // CONV_DIM)),
        out_shape=jax.ShapeDtypeStruct(d_proj.shape, d_proj.dtype),
        input_output_aliases={3: 0},
        compiler_params=_params(("arbitrary",)),
    )(dpre, dpre, w, d_proj)


def stage_modulate(x, sc, sh):
    return _ln(x) * (1.0 + sc) + sh


def stage_merge(ga, gb, ya, yb):
    return sigmoid(ga) * ya + sigmoid(gb) * yb


def stage_post_mixer(x, h, g1, ln_g, ln_b, sc2, sh2):
    x1 = _ln(ALPHA * x + g1 * h) * ln_g + ln_b
    return x1, _ln(x1) * (1.0 + sc2) + sh2


def stage_swiglu(a, b):
    return silu(a) * b


def stage_loss(x1, hf, tgt, g2, ln_g, ln_b):
    x2 = _ln(ALPHA * x1 + g2 * hf) * ln_g + ln_b
    return 0.5 * jnp.sum(jnp.mean(jnp.square(x2 - tgt), axis=-1, keepdims=True), axis=0, keepdims=True)


def local_step(x, tgt, mod, wts, small):
    sh1, sc1, g1, sh2, sc2, g2 = mod
    w_in, w_a, w_b, w_o, w_gu, w_d = wts
    lb, gn, conv_w, conv_b, dtb, alog, dsk, nw, ln1_g, ln1_b, ln2_g, ln2_b = small
    vec = (1, D)

    (u1,) = rowwise("modulate1", lambda r, c: ((stage_modulate(r[0], *c),), ()), [_full(x)], [sc1, sh1], [(D, BF16)])
    proj = matmul(u1, w_in, "nn", F32, "in_proj")
    ya_in, st_a = hgrn_forward(proj, lb, gn)
    xc = conv_forward(proj, conv_w, conv_b)
    yb_in, st_b = ssd_forward(xc, proj, dtb, alog, dsk, nw)
    ya = matmul(ya_in, w_a, "nn", F32, "branch_a")
    yb = matmul(yb_in, w_b, "nn", F32, "branch_b")
    gate_rows = [(proj, D, COL_GA // D), (proj, D, COL_GB // D), _full(ya), _full(yb)]
    (merged,) = rowwise("merge", lambda r, c: ((stage_merge(*r),), ()), gate_rows, [], [(D, BF16)])
    h = matmul(merged, w_o, "nn", F32, "out_proj")
    post_consts = [g1, ln1_g, ln1_b, sc2, sh2]
    x1, u2 = rowwise("post_mixer", lambda r, c: (stage_post_mixer(*r, *c), ()), [_full(x), _full(h)], post_consts,
                     [(D, F32), (D, BF16)])
    ab = matmul(u2, w_gu, "nn", F32, "ffn_in")
    (p,) = rowwise("swiglu", lambda r, c: ((stage_swiglu(*r),), ()), [(ab, D_FF, 0), (ab, D_FF, 1)], [], [(D_FF, BF16)])
    hf = matmul(p, w_d, "nn", F32, "ffn_out")

    def loss_bwd(r, c):
        loss, vjp = jax.vjp(stage_loss, *r, *c)
        dx1, dhf, _, dg2, dlg, dlb_ = vjp(jnp.ones((1, 1), F32))
        return (dx1, dhf), (loss, dg2, dlg, dlb_)

    dx1, dhf, loss, dg2, dln2_g, dln2_b = rowwise(
        "loss_backward", loss_bwd, [_full(x1), _full(hf), _full(tgt)], [g2, ln2_g, ln2_b],
        [(D, F32), (D, BF16)], [(1, 1), vec, vec, vec])
    dp = matmul(dhf, w_d, "nt", F32, "ffn_out_dx")
    dw_d = matmul(p, dhf, "tn", F32, "ffn_out_dw")

    def swiglu_bwd(r, c):
        _, vjp = jax.vjp(stage_swiglu, r[0], r[1])
        da, db_ = vjp(r[2])
        return (jnp.concatenate([da, db_], axis=1),), ()

    (dab,) = rowwise("swiglu_backward", swiglu_bwd, [(ab, D_FF, 0), (ab, D_FF, 1), _full(dp)], [], [(2 * D_FF, BF16)])
    du2 = matmul(dab, w_gu, "nt", F32, "ffn_in_dx")
    dw_gu = matmul(u2, dab, "tn", F32, "ffn_in_dw")

    def post_bwd(r, c):
        _, vjp = jax.vjp(stage_post_mixer, r[0], r[1], *c)
        dx, dh, *dc = vjp((r[2], r[3]))
        return (dx, dh), tuple(dc)

    dx_a, dh, dg1, dln1_g, dln1_b, dsc2, dsh2 = rowwise(
        "post_mixer_backward", post_bwd, [_full(x), _full(h), _full(dx1), _full(du2)], post_consts,
        [(D, F32), (D, BF16)], [vec] * 5)
    dmerged = matmul(dh, w_o, "nt", F32, "out_proj_dx")
    dw_o = matmul(merged, dh, "tn", F32, "out_proj_dw")

    def merge_bwd(r, c):
        _, vjp = jax.vjp(stage_merge, *r[:4])
        dga, dgb, dya, dyb = vjp(r[4])
        return (jnp.concatenate([dga, dgb], axis=1), dya, dyb), ()

    dproj, dya, dyb = rowwise("merge_backward", merge_bwd, gate_rows + [_full(dmerged)], [],
                              [(2 * D, BF16), (D, BF16), (D, BF16)], new_wide=(IN_PAD, COL_GA // (2 * D)))
    dya_in = matmul(dya, w_a, "nt", F32, "branch_a_dx")
    dw_a = matmul(ya_in, dya, "tn", F32, "branch_a_dw")
    dyb_in = matmul(dyb, w_b, "nt", F32, "branch_b_dx")
    dw_b = matmul(yb_in, dyb, "tn", F32, "branch_b_dw")
    dproj, dlb, dgn = hgrn_backward(proj, st_a, dya_in, lb, gn, dproj)
    dxs, dbm, dcm, ddt, dproj, ddtb, dalog, ddsk, dnw = ssd_backward(xc, proj, st_b, dyb_in, dtb, alog, dsk, nw, dproj)
    dpre, dconv_w, dconv_b = conv_backward_pre(proj, dxs, dbm, dcm, conv_w, conv_b)
    dproj = conv_backward_input(dpre, conv_w, dproj)
    t = x.shape[0]
    tail = jnp.concatenate([jnp.sum(ddt, axis=0).astype(BF16), jnp.zeros((t, IN_PAD - COL_DT - LANES), BF16)], axis=1)
    dproj = lax.dynamic_update_slice(dproj, tail, (0, COL_DT))
    du1 = matmul(dproj, w_in, "nt", F32, "in_proj_dx")
    dw_in = matmul(u1, dproj, "tn", F32, "in_proj_dw")

    def mod_bwd(r, c):
        _, vjp = jax.vjp(stage_modulate, r[0], *c)
        dx, dsc, dsh = vjp(r[1])
        return (dx + r[2],), (dsc, dsh)

    grad_x, dsc1, dsh1 = rowwise("modulate1_backward", mod_bwd, [_full(x), _full(du1), _full(dx_a)], [sc1, sh1],
                                 [(D, F32)], [vec, vec])
    d_mod = (dsh1, dsc1, dg1, dsh2, dsc2, dg2)
    d_wts = (dw_in, dw_a, dw_b, dw_o, dw_gu, dw_d)
    d_small = (dlb, dgn, dconv_w, dconv_b, ddtb.reshape(1, B_INNER),
               dalog.reshape(1, B_INNER), ddsk.reshape(1, B_INNER), dnw.reshape(1, B_INNER),
               dln1_g, dln1_b, dln2_g, dln2_b)
    return loss, grad_x, d_mod, d_wts, d_small


HBM = pl.BlockSpec(memory_space=pltpu.HBM)


def _place():
    return lax.axis_index("x"), lax.axis_index("y"), lax.axis_index("c")


def _other_chips(x, y):
    return [(1 - x, y), (x, 1 - y), (1 - x, 1 - y)]


def _remote(src, dst, send_sem, recv_sem, device):
    return pltpu.make_async_remote_copy(src_ref=src, dst_ref=dst, send_sem=send_sem, recv_sem=recv_sem,
                                        device_id=device, device_id_type=MESH)


def gather_rows(v, name):
    n = v.shape[1]

    def body(v_ref, out_ref, send_sems, recv_sems, local_sem):
        x, y, c = _place()
        mine = pltpu.make_async_copy(v_ref, out_ref.at[4 * x + 2 * y + c], local_sem)
        mine.start()
        sends, recvs = [], []
        for m in range(1, 8):
            px = 1 - x if m & 4 else x
            py = 1 - y if m & 2 else y
            pc = 1 - c if m & 1 else c
            sends.append(_remote(v_ref, out_ref.at[4 * x + 2 * y + c], send_sems.at[m - 1], recv_sems.at[m - 1], (px, py, pc)))
            recvs.append(_remote(v_ref, out_ref.at[4 * px + 2 * py + pc], send_sems.at[m - 1], recv_sems.at[m - 1], (px, py, pc)))
        for cp in sends:
            cp.start()
        for cp in recvs:
            cp.wait_recv()
        for cp in sends:
            cp.wait_send()
        mine.wait()

    return pl.pallas_call(
        body, name=name, in_specs=[HBM], out_specs=HBM,
        out_shape=jax.ShapeDtypeStruct((8, 1, n), v.dtype),
        scratch_shapes=[pltpu.SemaphoreType.DMA((7,)), pltpu.SemaphoreType.DMA((7,)), pltpu.SemaphoreType.DMA],
    )(v)


def exchange_rows(part, name):
    w = part.shape[2]

    def body(p_ref, out_ref, send_sems, recv_sems, local_sem):
        x, y, c = _place()
        k = 2 * x + y
        mine = pltpu.make_async_copy(p_ref.at[4 * x + 2 * y + c], out_ref.at[k], local_sem)
        mine.start()
        sends, recvs = [], []
        for j, (px, py) in enumerate(_other_chips(x, y)):
            sends.append(_remote(p_ref.at[4 * px + 2 * py + c], out_ref.at[k], send_sems.at[j], recv_sems.at[j], (px, py, c)))
            recvs.append(_remote(p_ref.at[4 * px + 2 * py + c], out_ref.at[2 * px + py], send_sems.at[j], recv_sems.at[j], (px, py, c)))
        for cp in sends:
            cp.start()
        for cp in recvs:
            cp.wait_recv()
        for cp in sends:
            cp.wait_send()
        mine.wait()

    return pl.pallas_call(
        body, name=name, in_specs=[HBM], out_specs=HBM,
        out_shape=jax.ShapeDtypeStruct((4, 1, w), part.dtype),
        scratch_shapes=[pltpu.SemaphoreType.DMA((3,)), pltpu.SemaphoreType.DMA((3,)), pltpu.SemaphoreType.DMA],
    )(part)


def gather_weights(shards):
    n = len(shards)

    def body(*refs):
        w_refs, out_refs = refs[:n], refs[n:2 * n]
        send_sems, recv_sems = refs[2 * n:]
        x, y, c = _place()
        chips = _other_chips(x, y)

        def half(i, px, py, pc):
            hr = shards[i].shape[0] // 2
            return out_refs[i].at[2 * px + py, pl.ds(pc * hr, hr), :]

        first, passed = [], []
        for i in range(n):
            hr = shards[i].shape[0] // 2
            for j, (px, py) in enumerate(chips):
                cp = _remote(w_refs[i].at[pl.ds(c * hr, hr), :], half(i, x, y, c),
                             send_sems.at[j * n + i], recv_sems.at[j * n + i], (px, py, c))
                cp.start()
                first.append(cp)
        for i in range(n):
            for j, (px, py) in enumerate(chips):
                mine_half = half(i, px, py, c)
                _remote(mine_half, mine_half, send_sems.at[j * n + i], recv_sems.at[j * n + i], (px, py, c)).wait_recv()
                cp = _remote(mine_half, mine_half, send_sems.at[(3 + j) * n + i], recv_sems.at[(3 + j) * n + i], (x, y, 1 - c))
                cp.start()
                passed.append(cp)
        for i in range(n):
            for j, (px, py) in enumerate(chips):
                other = half(i, px, py, 1 - c)
                _remote(other, other, send_sems.at[(3 + j) * n + i], recv_sems.at[(3 + j) * n + i], (x, y, 1 - c)).wait_recv()
        for cp in first + passed:
            cp.wait_send()

    return pl.pallas_call(
        body, name="gather_weights", in_specs=[HBM] * n, out_specs=[HBM] * n,
        out_shape=[jax.ShapeDtypeStruct((4,) + s.shape, s.dtype) for s in shards],
        scratch_shapes=[pltpu.SemaphoreType.DMA((6 * n,)), pltpu.SemaphoreType.DMA((6 * n,))],
    )(*shards)


def pair_exchange(slabs):
    n = len(slabs)

    def body(*refs):
        g_refs, out_refs = refs[:n], refs[n:2 * n]
        send_sems, recv_sems = refs[2 * n:]
        x, y, c = _place()
        copies = []
        for i in range(n):
            hr = slabs[i].shape[1] // 2
            cp = _remote(g_refs[i].at[:, pl.ds((1 - c) * hr, hr), :], out_refs[i], send_sems.at[i], recv_sems.at[i], (x, y, 1 - c))
            cp.start()
            copies.append(cp)
        for cp in copies:
            cp.wait()

    return pl.pallas_call(
        body, name="pair_exchange", in_specs=[HBM] * n, out_specs=[HBM] * n,
        out_shape=[jax.ShapeDtypeStruct((4, s.shape[1] // 2, s.shape[2]), s.dtype) for s in slabs],
        scratch_shapes=[pltpu.SemaphoreType.DMA((n,)), pltpu.SemaphoreType.DMA((n,))],
    )(*slabs)


def _row_tile(rows, cols):
    fits = lambda r: r * cols * 4 <= BLOCK_BYTES
    if fits(rows):
        return rows
    return next(r for r in (1024, 512, 256, 128, 64, 32, 16) if rows % r == 0 and fits(r))


def pair_add(g, p, c, name):
    _, hr, cols = p.shape
    tm = _row_tile(hr, cols)
    per = hr // tm

    def body(c_ref, g_ref, p_ref, o_ref):
        o_ref[...] = (g_ref[...] + p_ref[...]).astype(o_ref.dtype)

    return pl.pallas_call(
        body, name=name,
        grid_spec=pltpu.PrefetchScalarGridSpec(
            num_scalar_prefetch=1, grid=(4, per),
            in_specs=[pl.BlockSpec((None, tm, cols), lambda k, i, c_ref: (k, c_ref[0] * per + i, 0)),
                      pl.BlockSpec((None, tm, cols), lambda k, i, c_ref: (k, i, 0))],
            out_specs=pl.BlockSpec((None, tm, cols), lambda k, i, c_ref: (k, i, 0))),
        out_shape=jax.ShapeDtypeStruct((4, hr, cols), BF16),
        compiler_params=_params(("arbitrary", "arbitrary")),
    )(c.reshape(1).astype(jnp.int32), g, p)


def scatter_sums(sums):
    n = len(sums)

    def body(*refs):
        s_refs, out_refs = refs[:n], refs[n:2 * n]
        send_sems, recv_sems = refs[2 * n:]
        x, y, c = _place()
        k = 2 * x + y
        sends, recvs = [], []
        for i in range(n):
            for j, (px, py) in enumerate(_other_chips(x, y)):
                sems = (send_sems.at[j * n + i], recv_sems.at[j * n + i])
                sends.append(_remote(s_refs[i].at[2 * px + py], out_refs[i].at[k], *sems, (px, py, c)))
                recvs.append(_remote(s_refs[i].at[2 * px + py], out_refs[i].at[2 * px + py], *sems, (px, py, c)))
        for cp in sends:
            cp.start()
        for cp in recvs:
            cp.wait_recv()
        for cp in sends:
            cp.wait_send()

    return pl.pallas_call(
        body, name="scatter_sums", in_specs=[HBM] * n, out_specs=[HBM] * n,
        out_shape=[jax.ShapeDtypeStruct(s.shape, s.dtype) for s in sums],
        scratch_shapes=[pltpu.SemaphoreType.DMA((3 * n,)), pltpu.SemaphoreType.DMA((3 * n,))],
    )(*sums)


def sum_chips(landed, own, chip, core, name):
    _, hr, cols = landed.shape
    tm = _row_tile(hr, 4 * cols)
    per = hr // tm

    def body(idx_ref, l0, l1, l2, l3, own_ref, o_ref):
        mine = own_ref[...].astype(F32)
        v = [jnp.where(idx_ref[0] == k, mine, ref[...].astype(F32)) for k, ref in enumerate((l0, l1, l2, l3))]
        o_ref[...] = ((v[0] + v[1]) + v[2]) + v[3]

    slot = lambda k: pl.BlockSpec((None, tm, cols),
                                  lambda i, idx: (jnp.where(idx[0] == k, (k + 1) & 3, k), i, 0))
    return pl.pallas_call(
        body, name=name,
        grid_spec=pltpu.PrefetchScalarGridSpec(
            num_scalar_prefetch=1, grid=(per,),
            in_specs=[slot(0), slot(1), slot(2), slot(3),
                      pl.BlockSpec((None, tm, cols), lambda i, idx: (idx[0], i, 0))],
            out_specs=pl.BlockSpec((tm, cols), lambda i, idx: (idx[1] * per + i, 0))),
        out_shape=jax.ShapeDtypeStruct((2 * hr, cols), F32),
        compiler_params=_params(("arbitrary",)),
    )(jnp.stack([chip, core]).astype(jnp.int32), landed, landed, landed, landed, own)


def exchange_halves(bufs):
    n = len(bufs)

    def body(*refs):
        out_refs = refs[n:2 * n]
        send_sems, recv_sems = refs[2 * n:]
        x, y, c = _place()
        sends, recvs = [], []
        for i in range(n):
            hr = bufs[i].shape[0] // 2
            own = out_refs[i].at[pl.ds(c * hr, hr), :]
            other = out_refs[i].at[pl.ds((1 - c) * hr, hr), :]
            sends.append(_remote(own, own, send_sems.at[i], recv_sems.at[i], (x, y, 1 - c)))
            recvs.append(_remote(other, other, send_sems.at[i], recv_sems.at[i], (x, y, 1 - c)))
        for cp in sends:
            cp.start()
        for cp in recvs:
            cp.wait_recv()
        for cp in sends:
            cp.wait_send()

    return pl.pallas_call(
        body, name="exchange_halves", in_specs=[HBM] * n, out_specs=[HBM] * n,
        out_shape=[jax.ShapeDtypeStruct(b.shape, b.dtype) for b in bufs],
        input_output_aliases={i: i for i in range(n)},
        scratch_shapes=[pltpu.SemaphoreType.DMA((n,)), pltpu.SemaphoreType.DMA((n,))],
    )(*bufs)


def _relayout(name, arrays, in_blocks, out_blocks, out_shapes, fn):
    rows = 128
    spec = lambda blk: pl.BlockSpec(blk, (lambda i: (0, i, 0)) if len(blk) == 3 else (lambda i: (i, 0)))

    def body(*refs):
        n_in = len(arrays)
        outs = fn(*[r[...] for r in refs[:n_in]])
        for ref, val in zip(refs[n_in:], outs, strict=True):
            if isinstance(val, list):
                for k, piece in enumerate(val):
                    ref[k] = piece
            else:
                ref[...] = val

    return pl.pallas_call(
        body, name=name, grid=(D // rows,),
        in_specs=[spec(b) for b in in_blocks], out_specs=[spec(b) for b in out_blocks], out_shape=out_shapes,
        compiler_params=_params(("arbitrary",)),
    )(*arrays)


def assemble_in_proj(g):
    def fn(v):
        w = jnp.concatenate([v[k] for k in range(4)], axis=1)
        return (jnp.concatenate([w[:, :ORIG_Z], w[:, ORIG_GA:], w[:, ORIG_XBC:ORIG_DT], w[:, ORIG_Z:ORIG_XBC],
                                 w[:, ORIG_DT:ORIG_GA], jnp.zeros((w.shape[0], IN_PAD - IN_ORIG), w.dtype)], axis=1),)

    cols = g.shape[2]
    return _relayout("assemble_in_proj", [g], [(4, 128, cols)], [(128, IN_PAD)],
                     [jax.ShapeDtypeStruct((D, IN_PAD), g.dtype)], fn)[0]


def split_in_proj(dw):
    cols = IN_ORIG // 4

    def fn(d):
        w = jnp.concatenate([d[:, :COL_GA], d[:, COL_Z:COL_DT], d[:, COL_XBC:COL_Z], d[:, COL_DT:COL_DT + 32],
                             d[:, COL_GA:COL_XBC]], axis=1)
        return ([w[:, k * cols:(k + 1) * cols] for k in range(4)],)

    return _relayout("split_in_proj", [dw], [(128, IN_PAD)], [(4, 128, cols)],
                     [jax.ShapeDtypeStruct((4, D, cols), dw.dtype)], fn)[0]


def assemble_ffn_in(gate, up):
    fn = lambda a, b: (jnp.concatenate([a[k] for k in range(4)] + [b[k] for k in range(4)], axis=1),)
    cols = gate.shape[2]
    return _relayout("assemble_ffn_in", [gate, up], [(4, 128, cols)] * 2, [(128, 2 * D_FF)],
                     [jax.ShapeDtypeStruct((D, 2 * D_FF), gate.dtype)], fn)[0]


def split_ffn_in(dw):
    cols = D_FF // 4

    def fn(d):
        return ([d[:, k * cols:(k + 1) * cols] for k in range(4)],
                [d[:, D_FF + k * cols:D_FF + (k + 1) * cols] for k in range(4)])

    shape = jax.ShapeDtypeStruct((4, D, cols), dw.dtype)
    return _relayout("split_ffn_in", [dw], [(128, 2 * D_FF)], [(4, 128, cols)] * 2, [shape, shape], fn)


def ada_prepare(c_all, w_ada, hgrn_lb):
    def body(c_ref, w_ref, lb_ref, mod_ref, row_ref):
        mod_ref[...] = hdot(silu(c_ref[...]), w_ref[...])
        row_ref[...] = sigmoid(lb_ref[0:1, :] - lb_ref[1:2, :])

    return pl.pallas_call(
        body, name="ada_prepare",
        out_shape=[jax.ShapeDtypeStruct((8, w_ada.shape[1]), F32), jax.ShapeDtypeStruct((1, D), F32)],
        compiler_params=pltpu.CompilerParams(vmem_limit_bytes=VMEM_LIMIT),
    )(c_all, w_ada, hgrn_lb)


SMALL_SEGS = (("mod", 6 * D), ("lb", D), ("gnorm", LANES), ("conv_w", 4 * CONV_DIM), ("conv_b", CONV_DIM),
              ("dt_bias", B_INNER), ("a_log", B_INNER), ("d", B_INNER), ("ssm_norm", B_INNER),
              ("ln1_g", D), ("ln1_b", D), ("ln2_g", D), ("ln2_b", D))
SMALL_PARAMS = ("b_ada", "hgrn_lb", "hgrn_gnorm", "ssm_conv_b", "ssm_dt_bias", "ssm_a_log", "ssm_d", "ssm_norm",
                "ln1_g", "ln1_b", "ln2_g", "ln2_b")


def finalize_small(g_all, c_all, dmod_cols, params, m, v):
    n_p = len(SMALL_PARAMS)
    offs, o = {}, 0
    for nm, width in SMALL_SEGS:
        offs[nm] = (o, width)
        o += width

    def body(*refs):
        g_ref, c_ref, dm_ref = refs[:3]
        p_refs = refs[3:3 + n_p]
        m_refs = refs[3 + n_p:3 + 2 * n_p]
        v_refs = refs[3 + 2 * n_p:3 + 3 * n_p]
        outs = refs[3 + 3 * n_p:]
        gwa_ref, gcw_ref = outs[:2]
        res = outs[2:]
        total = jnp.sum(g_ref[...], axis=0, keepdims=True)
        seg = lambda nm: total[:, offs[nm][0]:offs[nm][0] + offs[nm][1]]
        gwa_ref[...] = hdot(silu(c_ref[...]), dm_ref[...], "tn")
        cw = seg("conv_w")
        for j in range(4):
            gcw_ref[j:j + 1, :] = cw[:, j * CONV_DIM:(j + 1) * CONV_DIM]
        hc = lax.broadcasted_iota(jnp.int32, (B_INNER, LANES), 0)
        hj = lax.broadcasted_iota(jnp.int32, (B_INNER, LANES), 1)
        per_head = ((hc >> 6) == hj).astype(F32)
        heads = lambda nm: hdot(jnp.broadcast_to(seg(nm), (8, B_INNER)), per_head)[0:1, 0:32]
        lbp = sigmoid(p_refs[1][0:1, :] - p_refs[1][1:2, :])
        g_row = seg("lb") * lbp * (1.0 - lbp)
        grads = {"b_ada": seg("mod"), "hgrn_gnorm": seg("gnorm"), "ssm_conv_b": seg("conv_b"),
                 "ssm_dt_bias": heads("dt_bias"), "ssm_a_log": heads("a_log"), "ssm_d": heads("d"),
                 "ssm_norm": seg("ssm_norm"), "ln1_g": seg("ln1_g"), "ln1_b": seg("ln1_b"),
                 "ln2_g": seg("ln2_g"), "ln2_b": seg("ln2_b")}
        for i, nm in enumerate(SMALL_PARAMS):
            g_out, d_out, m_out, v_out = res[4 * i:4 * i + 4]
            if nm == "hgrn_lb":
                for row, gv in ((0, g_row), (1, -g_row)):
                    sl = slice(row, row + 1)
                    dl, mn, vn = adamw(p_refs[i][sl, :], gv, m_refs[i][sl, :], v_refs[i][sl, :])
                    g_out[sl, :], d_out[sl, :], m_out[sl, :], v_out[sl, :] = gv, dl, mn, vn
            else:
                gv = grads[nm]
                dl, mn, vn = adamw(p_refs[i][...], gv, m_refs[i][...], v_refs[i][...])
                g_out[...], d_out[...], m_out[...], v_out[...] = gv, dl, mn, vn

    out_shape = [jax.ShapeDtypeStruct((D, dmod_cols.shape[1]), F32), jax.ShapeDtypeStruct((4, CONV_DIM), F32)]
    for p in params:
        out_shape += [jax.ShapeDtypeStruct(p.shape, F32)] * 4
    return pl.pallas_call(
        body, name="finalize_small", out_shape=out_shape,
        compiler_params=pltpu.CompilerParams(vmem_limit_bytes=VMEM_LIMIT),
    )(g_all, c_all, dmod_cols, *params, *m, *v)


def adam_update(w, g, m, v, name):
    cols = w.shape[1]
    return rowwise(name, lambda r, c: (adamw(*r), ()), [_full(w), _full(g), _full(m), _full(v)], [],
                   [(cols, F32)] * 3, tm_max=128)


def kernel(x, c, w_ada, b_ada, w_in, hgrn_lb, hgrn_gnorm, ssm_conv_w, ssm_conv_b, ssm_dt_bias, ssm_a_log, ssm_d, ssm_norm, w_branch_a, w_branch_b, w_o, ln1_g, ln1_b, w_ffn_gate, w_ffn_up, w_ffn_down, ln2_g, ln2_b, loss_target, m_w_ada, m_b_ada, m_w_in, m_hgrn_lb, m_hgrn_gnorm, m_ssm_conv_w, m_ssm_conv_b, m_ssm_dt_bias, m_ssm_a_log, m_ssm_d, m_ssm_norm, m_w_branch_a, m_w_branch_b, m_w_o, m_ln1_g, m_ln1_b, m_w_ffn_gate, m_w_ffn_up, m_w_ffn_down, m_ln2_g, m_ln2_b, v_w_ada, v_b_ada, v_w_in, v_hgrn_lb, v_hgrn_gnorm, v_ssm_conv_w, v_ssm_conv_b, v_ssm_dt_bias, v_ssm_a_log, v_ssm_d, v_ssm_norm, v_w_branch_a, v_w_branch_b, v_w_o, v_ln1_g, v_ln1_b, v_w_ffn_gate, v_w_ffn_up, v_w_ffn_down, v_ln2_g, v_ln2_b):
    given = dict(locals())
    chip = 2 * lax.axis_index("x") + lax.axis_index("y")
    core = lax.axis_index("c")
    t = x.shape[1]

    first = gather_rows(jnp.concatenate([c, ssm_conv_w.reshape(1, CONV_DIM)], axis=1), "gather_cond").reshape(8, D + CONV_DIM)
    c_all = first[:, :D]
    conv_w = first[0::2, D:].reshape(4, 4, CONV_DIM // 4).transpose(1, 0, 2).reshape(4, CONV_DIM)
    mod_part, lb_row = ada_prepare(c_all, w_ada[0], hgrn_lb)
    mod_cols = w_ada.shape[2]
    mod_row = exchange_rows(mod_part.reshape(8, 1, mod_cols), "exchange_mod").reshape(1, 6 * D) + b_ada
    mod = tuple(mod_row[:, i * D:(i + 1) * D] for i in range(6))

    shards = [given[nm][0].astype(BF16) for nm in SHARDED]
    got = {nm: lax.dynamic_update_slice(g, s[None], (chip, 0, 0))
           for nm, g, s in zip(SHARDED, gather_weights(shards), shards, strict=True)}
    whole = lambda nm: got[nm].reshape(4 * got[nm].shape[1], got[nm].shape[2])
    wts = (assemble_in_proj(got["w_in"]), whole("w_branch_a"), whole("w_branch_b"), whole("w_o"),
           assemble_ffn_in(got["w_ffn_gate"], got["w_ffn_up"]), whole("w_ffn_down"))

    per_channel = lambda p: jnp.repeat(p[0], B_INNER // 32)[None]
    small = (lb_row, hgrn_gnorm, conv_w, ssm_conv_b, per_channel(ssm_dt_bias), per_channel(ssm_a_log),
             per_channel(ssm_d), ssm_norm, ln1_g, ln1_b, ln2_g, ln2_b)
    loss, grad_x, d_mod, d_wts, d_small = local_step(x[0], loss_target[0], mod, wts, small)

    d_lb, d_gn, d_cw, d_cb, d_dtb, d_alog, d_dsk, d_nw, d_l1g, d_l1b, d_l2g, d_l2b = d_small
    row = jnp.concatenate(list(d_mod) + [d_lb, d_gn, d_cw.reshape(1, 4 * CONV_DIM), d_cb, d_dtb, d_alog, d_dsk, d_nw,
                                          d_l1g, d_l1b, d_l2g, d_l2b], axis=1)
    g_all = gather_rows(row, "gather_small_grads").reshape(8, row.shape[1])
    dmod_cols = lax.dynamic_slice_in_dim(g_all, chip * mod_cols, mod_cols, axis=1)
    fin = finalize_small(g_all, c_all, dmod_cols, [given[n] for n in SMALL_PARAMS],
                         [given["m_" + n] for n in SMALL_PARAMS], [given["v_" + n] for n in SMALL_PARAMS])
    grads, deltas, new_m, new_v = {}, {}, {}, {}
    grads["w_ada"] = fin[0][None]
    grads["ssm_conv_w"] = lax.dynamic_slice_in_dim(fin[1], chip * (CONV_DIM // 4), CONV_DIM // 4, axis=1)[None]
    for i, nm in enumerate(SMALL_PARAMS):
        grads[nm], deltas[nm], new_m[nm], new_v[nm] = fin[2 + 4 * i:6 + 4 * i]

    dw_in, dw_a, dw_b, dw_o, dw_gu, dw_d = d_wts
    by_rows = lambda g: g.reshape(4, g.shape[0] // 4, g.shape[1])
    d_gate, d_up = split_ffn_in(dw_gu)
    slabs = [split_in_proj(dw_in), by_rows(dw_a), by_rows(dw_b), by_rows(dw_o), d_gate, d_up, by_rows(dw_d)]
    pairs = [pair_add(s, r, core, "pair_add_" + nm) for nm, s, r in zip(SHARDED, slabs, pair_exchange(slabs), strict=True)]
    halves = [sum_chips(r, p, chip, core, "sum_chips_" + nm)
              for nm, r, p in zip(SHARDED, scatter_sums(pairs), pairs, strict=True)]
    for nm, r in zip(SHARDED, exchange_halves(halves), strict=True):
        grads[nm] = r[None]
    for nm in ("w_ada", "ssm_conv_w") + SHARDED:
        shp = given[nm].shape
        two_d = lambda a: a.reshape(shp[-2], shp[-1])
        d_, m_, v_ = adam_update(two_d(given[nm]), two_d(grads[nm]), two_d(given["m_" + nm]), two_d(given["v_" + nm]),
                                 "adam_" + nm)
        deltas[nm], new_m[nm], new_v[nm] = d_.reshape(shp), m_.reshape(shp), v_.reshape(shp)

    names = ("w_ada", "b_ada", "w_in", "hgrn_lb", "hgrn_gnorm", "ssm_conv_w", "ssm_conv_b", "ssm_dt_bias", "ssm_a_log",
             "ssm_d", "ssm_norm", "w_branch_a", "w_branch_b", "w_o", "ln1_g", "ln1_b", "w_ffn_gate", "w_ffn_up",
             "w_ffn_down", "ln2_g", "ln2_b")
    total_loss = lax.psum(loss[0, 0], ("x", "y", "c"))
    return (total_loss, grad_x[None], *[grads[n] for n in names], *[deltas[n] for n in names],
            *[new_m[n] for n in names], *[new_v[n] for n in names])
```

```python
import functools

import jax
import jax.numpy as jnp
from jax import lax
from jax.experimental import pallas as pl
from jax.experimental.pallas import tpu as pltpu

F32, BF16 = jnp.float32, jnp.bfloat16
HI = lax.Precision.HIGHEST
MESH = pl.DeviceIdType.MESH

D = 1024
CHUNK = 64
LANES = 128
N_HEADS_A = 8
N_GROUPS_B = 4
B_INNER = 2048
CONV_DIM = 3072
D_FF = 2816
ALPHA = 2.0 ** 0.25
LN_EPS = 1e-5
RMS_EPS = 1e-6
ADAM_LR, ADAM_B1, ADAM_B2, ADAM_EPS, ADAM_WD, ADAM_STEP = 0.001, 0.9, 0.999, 1e-08, 0.01, 10

IN_ORIG = 11296
IN_PAD = 11520
COL_GA, COL_GB, COL_XBC, COL_Z, COL_DT = 4096, 5120, 6144, 9216, 11264
ORIG_Z, ORIG_XBC, ORIG_DT, ORIG_GA = 4096, 6144, 9216, 9248

SHARDED = ("w_in", "w_branch_a", "w_branch_b", "w_o", "w_ffn_gate", "w_ffn_up", "w_ffn_down")
VMEM_LIMIT = 56 * 1024 * 1024
BLOCK_BYTES = 2 * 1024 * 1024

_DIMS = {"nn": (((1,), (0,)), ((), ())), "nt": (((1,), (1,)), ((), ())), "tn": (((0,), (0,)), ((), ()))}


def _bd(a, b, mode):
    return lax.dot_general(a.astype(BF16), b.astype(BF16), _DIMS[mode], preferred_element_type=F32)


@functools.partial(jax.custom_vjp, nondiff_argnums=(2,))
def bdot(a, b, mode):
    return _bd(a, b, mode)


def _bdot_fwd(a, b, mode):
    return _bd(a, b, mode), (a, b)


def _bdot_bwd(mode, res, g):
    a, b = res
    if mode == "nn":
        return _bd(g, b, "nt"), _bd(a, g, "tn")
    if mode == "nt":
        return _bd(g, b, "nn"), _bd(g, a, "tn")
    return _bd(b, g, "nt"), _bd(a, g, "nn")


bdot.defvjp(_bdot_fwd, _bdot_bwd)


def hdot(a, b, mode="nn"):
    return lax.dot_general(a, b, _DIMS[mode], precision=HI, preferred_element_type=F32)


def _raw(a, b, mode):
    return lax.dot_general(a, b, _DIMS[mode], preferred_element_type=F32)


def _split(x, n):
    parts, rest = [], x
    for _ in range(n):
        p = rest.astype(BF16)
        parts.append(p)
        rest = rest - p.astype(F32)
    return parts


def _od(a, b, mode, exact):
    if exact == 1:
        e = b.astype(BF16)
        p = _split(a, 3)
        return (_raw(p[2], e, mode) + _raw(p[1], e, mode)) + _raw(p[0], e, mode)
    e = a.astype(BF16)
    p = _split(b, 3)
    return (_raw(e, p[2], mode) + _raw(e, p[1], mode)) + _raw(e, p[0], mode)


@functools.partial(jax.custom_vjp, nondiff_argnums=(2, 3))
def odot(a, b, mode, exact):
    return _od(a, b, mode, exact)


def _odot_fwd(a, b, mode, exact):
    return _od(a, b, mode, exact), (a, b)


def _odot_bwd(mode, exact, res, g):
    a, b = res
    if exact == 1:
        da = {"nn": lambda: _od(g, b, "nt", 1), "nt": lambda: _od(g, b, "nn", 1), "tn": lambda: _od(b, g, "nt", 0)}[mode]()
        return da, jnp.zeros_like(b)
    db = {"nn": lambda: _od(a, g, "tn", 0), "nt": lambda: _od(g, a, "tn", 1), "tn": lambda: _od(a, g, "nn", 0)}[mode]()
    return jnp.zeros_like(a), db


odot.defvjp(_odot_fwd, _odot_bwd)


_BDIMS = {"bnn": (((2,), (1,)), ((0,), (0,))), "bnt": (((2,), (2,)), ((0,), (0,))), "btn": (((1,), (1,)), ((0,), (0,)))}


def _braw(a, b, mode):
    return lax.dot_general(a, b, _BDIMS[mode], preferred_element_type=F32)


def _bdb(a, b, mode):
    return _braw(a.astype(BF16), b.astype(BF16), mode)


def _d3b(a, b, mode):
    ah, al = _split(a, 2)
    bh, bl = _split(b, 2)
    return _braw(ah, bh, mode) + (_braw(ah, bl, mode) + _braw(al, bh, mode))


def _batched_bwd(f):
    def bwd(mode, res, g):
        a, b = res
        if mode == "bnn":
            return f(g, b, "bnt"), f(a, g, "btn")
        if mode == "bnt":
            return f(g, b, "bnn"), f(g, a, "btn")
        return f(b, g, "bnt"), f(a, g, "bnn")
    return bwd


@functools.partial(jax.custom_vjp, nondiff_argnums=(2,))
def bdot_b(a, b, mode):
    return _bdb(a, b, mode)


bdot_b.defvjp(lambda a, b, mode: (_bdb(a, b, mode), (a, b)), _batched_bwd(_bdb))


@functools.partial(jax.custom_vjp, nondiff_argnums=(2,))
def dot3_b(a, b, mode):
    return _d3b(a, b, mode)


dot3_b.defvjp(lambda a, b, mode: (_d3b(a, b, mode), (a, b)), _batched_bwd(_d3b))


def _cum(tril3, x, mode):
    e = tril3.astype(BF16)
    p = _split(x, 3)
    return (_braw(e, p[2], mode) + _braw(e, p[1], mode)) + _braw(e, p[0], mode)


@jax.custom_vjp
def chunk_cumsum(tril3, x):
    return _cum(tril3, x, "bnn")


chunk_cumsum.defvjp(lambda t, x: (_cum(t, x, "bnn"), t), lambda t, g: (jnp.zeros_like(t), _cum(t, g, "btn")))


def _unstack(axis, n):
    @jax.custom_vjp
    def un(x):
        return tuple(lax.index_in_dim(x, i, axis, keepdims=False) for i in range(n))

    un.defvjp(lambda x: (un(x), None), lambda _, g: (jnp.stack(g, axis=axis),))
    return un


def _split_last(n, w):
    @jax.custom_vjp
    def sp(x):
        return tuple(x[..., i * w:(i + 1) * w] for i in range(n))

    sp.defvjp(lambda x: (sp(x), None), lambda _, g: (jnp.concatenate(g, axis=-1),))
    return sp


def sigmoid(x):
    return 1.0 / (1.0 + jnp.exp(-x))


def silu(x):
    return x * sigmoid(x)


def softplus(x):
    return jnp.maximum(x, 0.0) + jnp.log1p(jnp.exp(jnp.minimum(x, -x)))


def _ln(x):
    mu = jnp.mean(x, axis=-1, keepdims=True)
    xc = x - mu
    return xc * lax.rsqrt(jnp.mean(xc * xc, axis=-1, keepdims=True) + LN_EPS)


def _tril64():
    r = lax.broadcasted_iota(jnp.int32, (CHUNK, CHUNK), 0)
    c = lax.broadcasted_iota(jnp.int32, (CHUNK, CHUNK), 1)
    return (r >= c).astype(F32)


def hgrn_block(q, fl, iv, gr, st, lb, gn):
    tb = q.shape[0]
    nc = tb // CHUNK
    nh = N_HEADS_A
    heads = _split_last(nh, LANES)
    to4 = lambda a: jnp.stack(heads(a), axis=0).reshape(nh, nc, CHUNK, LANES)
    flat = lambda a: a.reshape(nh * nc, CHUNK, LANES)
    f = lb + (1.0 - lb) * sigmoid(fl)
    gl4, k4, qf4, v4, gr4 = to4(jnp.log(f)), to4(1.0 - f), to4(silu(q) * (128 ** -0.5)), to4(iv), to4(gr)
    tril = _tril64()
    b4 = chunk_cumsum(jnp.broadcast_to(tril[None], (nh * nc, CHUNK, CHUNK)), flat(gl4)).reshape(gl4.shape)
    blast = jnp.sum(gl4, axis=2, keepdims=True)
    ref = lax.stop_gradient(0.5 * blast)
    sc = dot3_b(flat(qf4 * jnp.exp(b4 - ref)), flat(k4 * jnp.exp(ref - b4)), "bnt") * tril
    o_intra = bdot_b(sc, flat(v4), "bnn").reshape(gl4.shape)
    chunks = _unstack(1, nc)
    qe, v_c, kd, dec = chunks(qf4 * jnp.exp(b4)), chunks(v4), chunks(k4 * jnp.exp(blast - b4)), chunks(jnp.exp(blast))
    o_inter = []
    for c in range(nc):
        o_inter.append(bdot_b(qe[c], st, "bnt"))
        st = st * dec[c] + bdot_b(v_c[c], kd[c], "btn")
    o = o_intra + jnp.stack(o_inter, axis=1)
    on = o * lax.rsqrt(jnp.mean(o * o, axis=-1, keepdims=True) + RMS_EPS) * gn
    out = (on * silu(gr4)).reshape(nh, tb, LANES)
    return jnp.concatenate(_unstack(0, nh)(out), axis=1), st


def ssd_consts(g):
    i32 = jnp.int32
    ej = lax.broadcasted_iota(i32, (LANES, 512), 0)
    ec = lax.broadcasted_iota(i32, (LANES, 512), 1)
    expand = (ej == g * 8 + (ec >> 6)).astype(F32)
    ts = lax.broadcasted_iota(i32, (CHUNK, 512), 0)
    tc = lax.broadcasted_iota(i32, (CHUNK, 512), 1)
    itile = (ts == (tc & 63)).astype(F32)
    maskall = ts >= (tc & 63)
    br = lax.broadcasted_iota(i32, (256, 256), 0)
    bc = lax.broadcasted_iota(i32, (256, 256), 1)
    blockmask = ((br >> 6) == (bc >> 6)).astype(F32)
    return expand, itile, maskall, blockmask, _tril64()


def ssd_block(x, bm, cm, dt, z, st, dtb, alog, dsk, nw, cs):
    expand, itile, maskall, blockmask, tril = cs
    tb = x.shape[0]
    nc = tb // CHUNK
    delta = softplus(odot(dt, expand, "nn", 1) + dtb)
    a = -jnp.exp(alog) * delta
    xdt = x * delta
    by_chunk = lambda v: v.reshape(nc, CHUNK, v.shape[-1])
    a3, xdt3, bm3, cm3 = by_chunk(a), by_chunk(xdt), by_chunk(bm), by_chunk(cm)
    acum3 = chunk_cumsum(jnp.broadcast_to(tril[None], (nc, CHUNK, CHUNK)), a3)
    alast3 = jnp.sum(a3, axis=1, keepdims=True)
    cb3 = bdot_b(cm3, jnp.concatenate([bm3] * 8, axis=1), "bnt")
    arow3 = jnp.sum(acum3 * itile, axis=1, keepdims=True)
    dec3 = jnp.where(maskall, jnp.exp(jnp.minimum(acum3 - arow3, 0.0)), 0.0)
    halves = _split_last(2, 256)
    intra = [bdot_b(m, jnp.concatenate([xh] * 4, axis=1) * blockmask, "bnn")
             for m, xh in zip(halves(cb3 * dec3), halves(xdt3))]
    chunks = _unstack(0, nc)
    cm_c, bm_c, xw_c, dec_c = chunks(cm3), chunks(bm3), chunks(xdt3 * jnp.exp(alast3 - acum3)), chunks(jnp.exp(alast3))
    inter = []
    for c in range(nc):
        inter.append(bdot(cm_c[c], st, "nn"))
        st = st * dec_c[c] + bdot(bm_c[c], xw_c[c], "tn")
    st_new = st
    y = (jnp.concatenate(intra, axis=-1) + jnp.stack(inter, axis=0) * jnp.exp(acum3)).reshape(tb, 512)
    yz = (y + x * dsk) * silu(z)
    return yz * lax.rsqrt(jnp.mean(yz * yz, axis=-1, keepdims=True) + RMS_EPS) * nw, st_new


def adamw(w, g, m, v):
    m = ADAM_B1 * m + (1.0 - ADAM_B1) * g
    v = ADAM_B2 * v + (1.0 - ADAM_B2) * jnp.square(g)
    m_hat = m / (1.0 - ADAM_B1 ** ADAM_STEP)
    v_hat = v / (1.0 - ADAM_B2 ** ADAM_STEP)
    return -ADAM_LR * (m_hat / (jnp.sqrt(v_hat) + ADAM_EPS) + ADAM_WD * w), m, v


def _pick(n, cands):
    for c in cands:
        if n % c == 0:
            return c
    return n


def _params(sem):
    return pltpu.CompilerParams(dimension_semantics=sem, vmem_limit_bytes=VMEM_LIMIT)


def matmul(a, b, mode, out_dtype, name):
    if mode == "nn":
        (m, k), n = a.shape, b.shape[1]
    elif mode == "nt":
        (m, k), n = a.shape, b.shape[0]
    else:
        (k, m), n = a.shape, b.shape[1]
    tm = _pick(m, (1024, 512, 256, 128))
    tn = _pick(n, (1408, 1024, 768, 512, 256, 128))
    tk = _pick(k, (2304, 2048, 1408, 1024, 768, 512, 256, 128))
    nk = k // tk
    a_spec = pl.BlockSpec((tk, tm), lambda i, j, kk: (kk, i)) if mode == "tn" else pl.BlockSpec((tm, tk), lambda i, j, kk: (i, kk))
    b_spec = pl.BlockSpec((tn, tk), lambda i, j, kk: (j, kk)) if mode == "nt" else pl.BlockSpec((tk, tn), lambda i, j, kk: (kk, j))

    def body(a_ref, b_ref, o_ref, *acc):
        part = _bd(a_ref[...], b_ref[...], mode)
        if nk == 1:
            o_ref[...] = part.astype(o_ref.dtype)
            return
        acc_ref, = acc
        kk = pl.program_id(2)

        @pl.when(kk == 0)
        def _():
            acc_ref[...] = part

        @pl.when(jnp.logical_and(kk > 0, kk < nk - 1))
        def _():
            acc_ref[...] += part

        @pl.when(kk == nk - 1)
        def _():
            o_ref[...] = (acc_ref[...] + part).astype(o_ref.dtype)

    return pl.pallas_call(
        body, name=name, grid=(m // tm, n // tn, nk),
        in_specs=[a_spec, b_spec], out_specs=pl.BlockSpec((tm, tn), lambda i, j, kk: (i, j)),
        out_shape=jax.ShapeDtypeStruct((m, n), out_dtype),
        scratch_shapes=[pltpu.VMEM((tm, tn), F32)] if nk > 1 else [],
        compiler_params=_params(("parallel", "parallel", "arbitrary")),
    )(a, b)


def rowwise(name, fn, rows, consts, out_rows, out_accs=(), tm_max=256, into=None, new_wide=None):
    t = rows[0][0].shape[0]
    tm = _pick(t, (tm_max, 128, 64, 32, 16, 8))
    n_r, n_c, n_o = len(rows), len(consts), len(out_rows)
    n_alias = 0 if into is None else 1

    def body(*refs):
        r_in = [r[...] for r in refs[:n_r]]
        c_in = [r[...] for r in refs[n_r:n_r + n_c]]
        refs = refs[:n_r + n_c] + refs[n_r + n_c + n_alias:]
        o_refs = refs[n_r + n_c:n_r + n_c + n_o]
        a_refs = refs[n_r + n_c + n_o:]
        ro, ao = fn(r_in, c_in)
        for ref, val in zip(o_refs, ro, strict=True):
            ref[...] = val.astype(ref.dtype)
        if a_refs:
            @pl.when(pl.program_id(0) == 0)
            def _():
                for ref in a_refs:
                    ref[...] = jnp.zeros_like(ref)

            for ref, val in zip(a_refs, ao, strict=True):
                ref[...] += val

    in_specs = [pl.BlockSpec((tm, w), functools.partial(lambda i, cb: (i, cb), cb=cb)) for _, w, cb in rows]
    in_specs += [pl.BlockSpec(c.shape, lambda i: (0, 0)) for c in consts]
    out_specs = [pl.BlockSpec((tm, w), lambda i: (i, 0)) for w, _ in out_rows]
    out_specs += [pl.BlockSpec(s, lambda i: (0, 0)) for s in out_accs]
    out_shape = [jax.ShapeDtypeStruct((t, w), dt) for w, dt in out_rows]
    out_shape += [jax.ShapeDtypeStruct(s, F32) for s in out_accs]
    operands = [r[0] for r in rows] + list(consts)
    aliases = {}
    if into is not None:
        target, cb = into
        in_specs.append(pl.BlockSpec(memory_space=pl.ANY))
        operands.append(target)
        out_specs[0] = pl.BlockSpec((tm, out_rows[0][0]), lambda i: (i, cb))
        out_shape[0] = jax.ShapeDtypeStruct(target.shape, target.dtype)
        aliases = {len(operands) - 1: 0}
    if new_wide is not None:
        width, cb = new_wide
        out_specs[0] = pl.BlockSpec((tm, out_rows[0][0]), lambda i: (i, cb))
        out_shape[0] = jax.ShapeDtypeStruct((t, width), out_rows[0][1])
    return pl.pallas_call(
        body, name=name, grid=(t // tm,), in_specs=in_specs, out_specs=out_specs, out_shape=out_shape,
        input_output_aliases=aliases, compiler_params=_params(("arbitrary",)),
    )(*operands)


def _full(a):
    return (a, a.shape[1], 0)


def _time_block(t):
    return _pick(t, (256, 128, 64))


def _quarters(ref):
    return [ref[:, seg * D:(seg + 1) * D] for seg in range(4)]


def hgrn_forward(proj, lb, gn):
    t = proj.shape[0]
    tb = _time_block(t)
    nb = t // tb

    def body(qfig_ref, lb_ref, gn_ref, o_ref, st_ref, state):
        @pl.when(pl.program_id(0) == 0)
        def _():
            state[...] = jnp.zeros_like(state)

        st = state[...]
        st_ref[...] = st
        out, st_new = hgrn_block(*_quarters(qfig_ref), st, lb_ref[...], gn_ref[...])
        o_ref[...] = out.astype(o_ref.dtype)
        state[...] = st_new

    return pl.pallas_call(
        body, name="hgrn_forward", grid=(nb,),
        in_specs=[pl.BlockSpec((tb, 4 * D), lambda j: (j, 0)),
                  pl.BlockSpec((1, D), lambda j: (0, 0)), pl.BlockSpec((1, LANES), lambda j: (0, 0))],
        out_specs=[pl.BlockSpec((tb, D), lambda j: (j, 0)),
                   pl.BlockSpec((None, N_HEADS_A, LANES, LANES), lambda j: (j, 0, 0, 0))],
        out_shape=[jax.ShapeDtypeStruct((t, D), BF16),
                   jax.ShapeDtypeStruct((nb, N_HEADS_A, LANES, LANES), F32)],
        scratch_shapes=[pltpu.VMEM((N_HEADS_A, LANES, LANES), F32)],
        compiler_params=_params(("arbitrary",)),
    )(proj, lb, gn)


def hgrn_backward(proj, states, d_out, lb, gn, d_proj):
    t = proj.shape[0]
    tb = _time_block(t)
    nb = t // tb

    def body(qfig_ref, st_ref, do_ref, lb_ref, gn_ref, _, dqfig_ref, dlb_ref, dgn_ref, d_state):
        @pl.when(pl.program_id(0) == 0)
        def _():
            d_state[...] = jnp.zeros_like(d_state)
            dlb_ref[...] = jnp.zeros_like(dlb_ref)
            dgn_ref[...] = jnp.zeros_like(dgn_ref)

        _, vjp = jax.vjp(hgrn_block, *_quarters(qfig_ref), st_ref[...], lb_ref[...], gn_ref[...])
        dq, df, di, dg, dst, dlb, dgn = vjp((do_ref[...], d_state[...]))
        for seg, val in enumerate((dq, df, di, dg)):
            dqfig_ref[:, seg * D:(seg + 1) * D] = val.astype(dqfig_ref.dtype)
        d_state[...] = dst
        dlb_ref[...] += dlb
        dgn_ref[...] += dgn

    rev = lambda j: nb - 1 - j
    return pl.pallas_call(
        body, name="hgrn_backward", grid=(nb,),
        in_specs=[pl.BlockSpec((tb, 4 * D), lambda j: (rev(j), 0)),
                  pl.BlockSpec((None, N_HEADS_A, LANES, LANES), lambda j: (rev(j), 0, 0, 0)),
                  pl.BlockSpec((tb, D), lambda j: (rev(j), 0)),
                  pl.BlockSpec((1, D), lambda j: (0, 0)), pl.BlockSpec((1, LANES), lambda j: (0, 0)),
                  pl.BlockSpec(memory_space=pl.ANY)],
        out_specs=[pl.BlockSpec((tb, 4 * D), lambda j: (rev(j), 0)),
                   pl.BlockSpec((1, D), lambda j: (0, 0)), pl.BlockSpec((1, LANES), lambda j: (0, 0))],
        out_shape=[jax.ShapeDtypeStruct(d_proj.shape, d_proj.dtype), jax.ShapeDtypeStruct((1, D), F32),
                   jax.ShapeDtypeStruct((1, LANES), F32)],
        input_output_aliases={5: 0},
        scratch_shapes=[pltpu.VMEM((N_HEADS_A, LANES, LANES), F32)],
        compiler_params=_params(("arbitrary",)),
    )(proj, states, d_out, lb, gn, d_proj)


def _ssd_in_specs(tb, tmap):
    return [pl.BlockSpec((tb, 512), lambda g, j: (tmap(j), g)),
            pl.BlockSpec((tb, LANES), lambda g, j: (tmap(j), 16 + g)),
            pl.BlockSpec((tb, LANES), lambda g, j: (tmap(j), 20 + g)),
            pl.BlockSpec((tb, LANES), lambda g, j: (tmap(j), COL_DT // LANES)),
            pl.BlockSpec((tb, 512), lambda g, j: (tmap(j), COL_Z // 512 + g))]


def ssd_forward(xc, proj, dtb, alog, dsk, nw):
    t = proj.shape[0]
    tb = _time_block(t)
    nb = t // tb

    def body(x_ref, b_ref, c_ref, dt_ref, z_ref, dtb_ref, alog_ref, dsk_ref, nw_ref, o_ref, st_ref, state):
        @pl.when(pl.program_id(1) == 0)
        def _():
            state[...] = jnp.zeros_like(state)

        st = state[...]
        st_ref[...] = st
        out, st_new = ssd_block(x_ref[...], b_ref[...], c_ref[...], dt_ref[...], z_ref[...], st,
                                dtb_ref[...], alog_ref[...], dsk_ref[...], nw_ref[...], ssd_consts(pl.program_id(0)))
        o_ref[...] = out.astype(o_ref.dtype)
        state[...] = st_new

    vec = pl.BlockSpec((1, 512), lambda g, j: (0, g))
    return pl.pallas_call(
        body, name="ssd_forward", grid=(N_GROUPS_B, nb),
        in_specs=_ssd_in_specs(tb, lambda j: j) + [vec] * 4,
        out_specs=[pl.BlockSpec((tb, 512), lambda g, j: (j, g)),
                   pl.BlockSpec((None, None, LANES, 512), lambda g, j: (j, g, 0, 0))],
        out_shape=[jax.ShapeDtypeStruct((t, B_INNER), BF16),
                   jax.ShapeDtypeStruct((nb, N_GROUPS_B, LANES, 512), F32)],
        scratch_shapes=[pltpu.VMEM((LANES, 512), F32)],
        compiler_params=_params(("arbitrary", "arbitrary")),
    )(xc, xc, xc, proj, proj, dtb, alog, dsk, nw)


def ssd_backward(xc, proj, states, d_out, dtb, alog, dsk, nw, d_proj):
    t = proj.shape[0]
    tb = _time_block(t)
    nb = t // tb
    rev = lambda j: nb - 1 - j

    def body(x_ref, b_ref, c_ref, dt_ref, z_ref, st_ref, do_ref, dtb_ref, alog_ref, dsk_ref, nw_ref, _,
             dx_ref, db_ref, dc_ref, ddt_ref, dz_ref, ddtb_ref, dalog_ref, ddsk_ref, dnw_ref, d_state):
        accs = (ddtb_ref, dalog_ref, ddsk_ref, dnw_ref)

        @pl.when(pl.program_id(1) == 0)
        def _():
            d_state[...] = jnp.zeros_like(d_state)
            for ref in accs:
                ref[...] = jnp.zeros_like(ref)

        cs = ssd_consts(pl.program_id(0))
        fn = lambda *a: ssd_block(*a, cs)
        _, vjp = jax.vjp(fn, x_ref[...], b_ref[...], c_ref[...], dt_ref[...], z_ref[...], st_ref[...],
                         dtb_ref[...], alog_ref[...], dsk_ref[...], nw_ref[...])
        dx, db, dc, ddt, dz, dst, *dpar = vjp((do_ref[...], d_state[...]))
        dx_ref[...] = dx
        db_ref[...] = db
        dc_ref[...] = dc
        ddt_ref[...] = ddt
        dz_ref[...] = dz.astype(dz_ref.dtype)
        d_state[...] = dst
        for ref, val in zip(accs, dpar, strict=True):
            ref[...] += val

    vec = pl.BlockSpec((1, 512), lambda g, j: (0, g))
    acc = pl.BlockSpec((None, 1, 512), lambda g, j: (g, 0, 0))
    return pl.pallas_call(
        body, name="ssd_backward", grid=(N_GROUPS_B, nb),
        in_specs=_ssd_in_specs(tb, rev)
        + [pl.BlockSpec((None, None, LANES, 512), lambda g, j: (rev(j), g, 0, 0)),
           pl.BlockSpec((tb, 512), lambda g, j: (rev(j), g))] + [vec] * 4 + [pl.BlockSpec(memory_space=pl.ANY)],
        out_specs=[pl.BlockSpec((tb, 512), lambda g, j: (rev(j), g)),
                   pl.BlockSpec((tb, LANES), lambda g, j: (rev(j), g)),
                   pl.BlockSpec((tb, LANES), lambda g, j: (rev(j), g)),
                   pl.BlockSpec((None, tb, LANES), lambda g, j: (g, rev(j), 0)),
                   pl.BlockSpec((tb, 512), lambda g, j: (rev(j), COL_Z // 512 + g)), acc, acc, acc, acc],
        out_shape=[jax.ShapeDtypeStruct((t, B_INNER), F32), jax.ShapeDtypeStruct((t, 512), F32),
                   jax.ShapeDtypeStruct((t, 512), F32), jax.ShapeDtypeStruct((N_GROUPS_B, t, LANES), F32),
                   jax.ShapeDtypeStruct(d_proj.shape, d_proj.dtype)] + [jax.ShapeDtypeStruct((N_GROUPS_B, 1, 512), F32)] * 4,
        input_output_aliases={11: 4},
        scratch_shapes=[pltpu.VMEM((LANES, 512), F32)],
        compiler_params=_params(("arbitrary", "arbitrary")),
    )(xc, xc, xc, proj, proj, states, d_out, dtb, alog, dsk, nw, d_proj)


CONV_HALO = 8


def _shift_down(halo_then_tile, s, tm):
    if s == 0:
        return halo_then_tile[CONV_HALO:CONV_HALO + tm]
    return pltpu.roll(halo_then_tile, s, 0)[CONV_HALO:CONV_HALO + tm]


def _conv_pre(cur, prev, w, b, tm):
    stacked = jnp.concatenate([prev, cur], axis=0)
    taps = [_shift_down(stacked, 3 - j, tm) for j in range(4)]
    pre = b + taps[0] * w[0:1] + taps[1] * w[1:2] + taps[2] * w[2:3] + taps[3] * w[3:4]
    return pre, taps


def _conv_specs(t, tm):
    per = tm // CONV_HALO
    cur = pl.BlockSpec((tm, CONV_DIM), lambda i: (i, COL_XBC // CONV_DIM))
    prev = pl.BlockSpec((CONV_HALO, CONV_DIM), lambda i: (jnp.maximum(i * per - 1, 0), COL_XBC // CONV_DIM))
    return cur, prev


def conv_forward(proj, w, b):
    t = proj.shape[0]
    tm = _pick(t, (256, 128, 64))

    def body(cur_ref, prev_ref, w_ref, b_ref, o_ref):
        prev = jnp.where(pl.program_id(0) == 0, 0.0, prev_ref[...])
        pre, _ = _conv_pre(cur_ref[...], prev, w_ref[...], b_ref[...], tm)
        o_ref[...] = silu(pre)

    cur, prev = _conv_specs(t, tm)
    return pl.pallas_call(
        body, name="conv_forward", grid=(t // tm,),
        in_specs=[cur, prev, pl.BlockSpec((4, CONV_DIM), lambda i: (0, 0)), pl.BlockSpec((1, CONV_DIM), lambda i: (0, 0))],
        out_specs=pl.BlockSpec((tm, CONV_DIM), lambda i: (i, 0)),
        out_shape=jax.ShapeDtypeStruct((t, CONV_DIM), F32),
        compiler_params=_params(("arbitrary",)),
    )(proj, proj, w, b)


def conv_backward_pre(proj, dx, db_, dc_, w, b):
    t = proj.shape[0]
    tm = _pick(t, (256, 128, 64))

    def body(cur_ref, prev_ref, dx_ref, dbm_ref, dcm_ref, w_ref, b_ref, dpre_ref, dw_ref, dbias_ref):
        @pl.when(pl.program_id(0) == 0)
        def _():
            dw_ref[...] = jnp.zeros_like(dw_ref)
            dbias_ref[...] = jnp.zeros_like(dbias_ref)

        prev = jnp.where(pl.program_id(0) == 0, 0.0, prev_ref[...])
        pre, taps = _conv_pre(cur_ref[...], prev, w_ref[...], b_ref[...], tm)
        sg = sigmoid(pre)
        d_out = jnp.concatenate([dx_ref[...], dbm_ref[...], dcm_ref[...]], axis=1)
        dpre = d_out * (sg * (1.0 + pre * (1.0 - sg)))
        dpre_ref[...] = dpre
        dbias_ref[...] += jnp.sum(dpre, axis=0, keepdims=True)
        for j in range(4):
            dw_ref[j:j + 1, :] += jnp.sum(dpre * taps[j], axis=0, keepdims=True)

    cur, prev = _conv_specs(t, tm)
    row = lambda w_: pl.BlockSpec((tm, w_), lambda i: (i, 0))
    return pl.pallas_call(
        body, name="conv_backward_pre", grid=(t // tm,),
        in_specs=[cur, prev, row(B_INNER), row(512), row(512),
                  pl.BlockSpec((4, CONV_DIM), lambda i: (0, 0)), pl.BlockSpec((1, CONV_DIM), lambda i: (0, 0))],
        out_specs=[row(CONV_DIM), pl.BlockSpec((4, CONV_DIM), lambda i: (0, 0)), pl.BlockSpec((1, CONV_DIM), lambda i: (0, 0))],
        out_shape=[jax.ShapeDtypeStruct((t, CONV_DIM), F32), jax.ShapeDtypeStruct((4, CONV_DIM), F32),
                   jax.ShapeDtypeStruct((1, CONV_DIM), F32)],
        compiler_params=_params(("arbitrary",)),
    )(proj, proj, dx, db_, dc_, w, b)


def conv_backward_input(dpre, w, d_proj):
    t = dpre.shape[0]
    tm = _pick(t, (256, 128, 64))
    per = tm // CONV_HALO
    last = t // CONV_HALO - 1
    nt = t // tm

    def body(cur_ref, nxt_ref, w_ref, _, o_ref):
        nxt = jnp.where(pl.program_id(0) == nt - 1, 0.0, nxt_ref[...])
        stacked = jnp.concatenate([cur_ref[...], nxt], axis=0)
        w_ = w_ref[...]
        acc = stacked[0:tm] * w_[3:4]
        for j in range(3):
            s = 3 - j
            acc = acc + pltpu.roll(stacked, tm + CONV_HALO - s, 0)[0:tm] * w_[j:j + 1]
        o_ref[...] = acc.astype(o_ref.dtype)

    return pl.pallas_call(
        body, name="conv_backward_input", grid=(nt,),
        in_specs=[pl.BlockSpec((tm, CONV_DIM), lambda i: (i, 0)),
                  pl.BlockSpec((CONV_HALO, CONV_DIM), lambda i: (jnp.minimum((i + 1) * per, last), 0)),
                  pl.BlockSpec((4, CONV_DIM), lambda i: (0, 0)), pl.BlockSpec(memory_space=pl.ANY)],
        out_specs=pl.BlockSpec((tm, CONV_DIM), lambda i: (i, COL_XBC // CONV_DIM)),
        out_shape=jax.ShapeDtypeStruct(d_proj.shape, d_proj.dtype),
        input_output_aliases={3: 0},
        compiler_params=_params(("arbitrary",)),
    )(dpre, dpre, w, d_proj)


def stage_modulate(x, sc, sh):
    return _ln(x) * (1.0 + sc) + sh


def stage_merge(ga, gb, ya, yb):
    return sigmoid(ga) * ya + sigmoid(gb) * yb


def stage_post_mixer(x, h, g1, ln_g, ln_b, sc2, sh2):
    x1 = _ln(ALPHA * x + g1 * h) * ln_g + ln_b
    return x1, _ln(x1) * (1.0 + sc2) + sh2


def stage_swiglu(a, b):
    return silu(a) * b


def stage_loss(x1, hf, tgt, g2, ln_g, ln_b):
    x2 = _ln(ALPHA * x1 + g2 * hf) * ln_g + ln_b
    return 0.5 * jnp.sum(jnp.mean(jnp.square(x2 - tgt), axis=-1, keepdims=True), axis=0, keepdims=True)


def local_step(x, tgt, mod, wts, small):
    sh1, sc1, g1, sh2, sc2, g2 = mod
    w_in, w_a, w_b, w_o, w_gu, w_d = wts
    lb, gn, conv_w, conv_b, dtb, alog, dsk, nw, ln1_g, ln1_b, ln2_g, ln2_b = small
    vec = (1, D)

    (u1,) = rowwise("modulate1", lambda r, c: ((stage_modulate(r[0], *c),), ()), [_full(x)], [sc1, sh1], [(D, BF16)])
    proj = matmul(u1, w_in, "nn", F32, "in_proj")
    ya_in, st_a = hgrn_forward(proj, lb, gn)
    xc = conv_forward(proj, conv_w, conv_b)
    yb_in, st_b = ssd_forward(xc, proj, dtb, alog, dsk, nw)
    ya = matmul(ya_in, w_a, "nn", F32, "branch_a")
    yb = matmul(yb_in, w_b, "nn", F32, "branch_b")
    gate_rows = [(proj, D, COL_GA // D), (proj, D, COL_GB // D), _full(ya), _full(yb)]
    (merged,) = rowwise("merge", lambda r, c: ((stage_merge(*r),), ()), gate_rows, [], [(D, BF16)])
    h = matmul(merged, w_o, "nn", F32, "out_proj")
    post_consts = [g1, ln1_g, ln1_b, sc2, sh2]
    x1, u2 = rowwise("post_mixer", lambda r, c: (stage_post_mixer(*r, *c), ()), [_full(x), _full(h)], post_consts,
                     [(D, F32), (D, BF16)])
    ab = matmul(u2, w_gu, "nn", F32, "ffn_in")
    (p,) = rowwise("swiglu", lambda r, c: ((stage_swiglu(*r),), ()), [(ab, D_FF, 0), (ab, D_FF, 1)], [], [(D_FF, BF16)])
    hf = matmul(p, w_d, "nn", F32, "ffn_out")

    def loss_bwd(r, c):
        loss, vjp = jax.vjp(stage_loss, *r, *c)
        dx1, dhf, _, dg2, dlg, dlb_ = vjp(jnp.ones((1, 1), F32))
        return (dx1, dhf), (loss, dg2, dlg, dlb_)

    dx1, dhf, loss, dg2, dln2_g, dln2_b = rowwise(
        "loss_backward", loss_bwd, [_full(x1), _full(hf), _full(tgt)], [g2, ln2_g, ln2_b],
        [(D, F32), (D, BF16)], [(1, 1), vec, vec, vec])
    dp = matmul(dhf, w_d, "nt", F32, "ffn_out_dx")
    dw_d = matmul(p, dhf, "tn", F32, "ffn_out_dw")

    def swiglu_bwd(r, c):
        _, vjp = jax.vjp(stage_swiglu, r[0], r[1])
        da, db_ = vjp(r[2])
        return (jnp.concatenate([da, db_], axis=1),), ()

    (dab,) = rowwise("swiglu_backward", swiglu_bwd, [(ab, D_FF, 0), (ab, D_FF, 1), _full(dp)], [], [(2 * D_FF, BF16)])
    du2 = matmul(dab, w_gu, "nt", F32, "ffn_in_dx")
    dw_gu = matmul(u2, dab, "tn", F32, "ffn_in_dw")

    def post_bwd(r, c):
        _, vjp = jax.vjp(stage_post_mixer, r[0], r[1], *c)
        dx, dh, *dc = vjp((r[2], r[3]))
        return (dx, dh), tuple(dc)

    dx_a, dh, dg1, dln1_g, dln1_b, dsc2, dsh2 = rowwise(
        "post_mixer_backward", post_bwd, [_full(x), _full(h), _full(dx1), _full(du2)], post_consts,
        [(D, F32), (D, BF16)], [vec] * 5)
    dmerged = matmul(dh, w_o, "nt", F32, "out_proj_dx")
    dw_o = matmul(merged, dh, "tn", F32, "out_proj_dw")

    def merge_bwd(r, c):
        _, vjp = jax.vjp(stage_merge, *r[:4])
        dga, dgb, dya, dyb = vjp(r[4])
        return (jnp.concatenate([dga, dgb], axis=1), dya, dyb), ()

    dproj, dya, dyb = rowwise("merge_backward", merge_bwd, gate_rows + [_full(dmerged)], [],
                              [(2 * D, BF16), (D, BF16), (D, BF16)], new_wide=(IN_PAD, COL_GA // (2 * D)))
    dya_in = matmul(dya, w_a, "nt", F32, "branch_a_dx")
    dw_a = matmul(ya_in, dya, "tn", F32, "branch_a_dw")
    dyb_in = matmul(dyb, w_b, "nt", F32, "branch_b_dx")
    dw_b = matmul(yb_in, dyb, "tn", F32, "branch_b_dw")
    dproj, dlb, dgn = hgrn_backward(proj, st_a, dya_in, lb, gn, dproj)
    dxs, dbm, dcm, ddt, dproj, ddtb, dalog, ddsk, dnw = ssd_backward(xc, proj, st_b, dyb_in, dtb, alog, dsk, nw, dproj)
    dpre, dconv_w, dconv_b = conv_backward_pre(proj, dxs, dbm, dcm, conv_w, conv_b)
    dproj = conv_backward_input(dpre, conv_w, dproj)
    t = x.shape[0]
    tail = jnp.concatenate([jnp.sum(ddt, axis=0).astype(BF16), jnp.zeros((t, IN_PAD - COL_DT - LANES), BF16)], axis=1)
    dproj = lax.dynamic_update_slice(dproj, tail, (0, COL_DT))
    du1 = matmul(dproj, w_in, "nt", F32, "in_proj_dx")
    dw_in = matmul(u1, dproj, "tn", F32, "in_proj_dw")

    def mod_bwd(r, c):
        _, vjp = jax.vjp(stage_modulate, r[0], *c)
        dx, dsc, dsh = vjp(r[1])
        return (dx + r[2],), (dsc, dsh)

    grad_x, dsc1, dsh1 = rowwise("modulate1_backward", mod_bwd, [_full(x), _full(du1), _full(dx_a)], [sc1, sh1],
                                 [(D, F32)], [vec, vec])
    d_mod = (dsh1, dsc1, dg1, dsh2, dsc2, dg2)
    d_wts = (dw_in, dw_a, dw_b, dw_o, dw_gu, dw_d)
    d_small = (dlb, dgn, dconv_w, dconv_b, ddtb.reshape(1, B_INNER),
               dalog.reshape(1, B_INNER), ddsk.reshape(1, B_INNER), dnw.reshape(1, B_INNER),
               dln1_g, dln1_b, dln2_g, dln2_b)
    return loss, grad_x, d_mod, d_wts, d_small


HBM = pl.BlockSpec(memory_space=pltpu.HBM)


def _place():
    return lax.axis_index("x"), lax.axis_index("y"), lax.axis_index("c")


def _other_chips(x, y):
    return [(1 - x, y), (x, 1 - y), (1 - x, 1 - y)]


def _remote(src, dst, send_sem, recv_sem, device):
    return pltpu.make_async_remote_copy(src_ref=src, dst_ref=dst, send_sem=send_sem, recv_sem=recv_sem,
                                        device_id=device, device_id_type=MESH)


def gather_rows(v, name):
    n = v.shape[1]

    def body(v_ref, out_ref, send_sems, recv_sems, local_sem):
        x, y, c = _place()
        mine = pltpu.make_async_copy(v_ref, out_ref.at[4 * x + 2 * y + c], local_sem)
        mine.start()
        sends, recvs = [], []
        for m in range(1, 8):
            px = 1 - x if m & 4 else x
            py = 1 - y if m & 2 else y
            pc = 1 - c if m & 1 else c
            sends.append(_remote(v_ref, out_ref.at[4 * x + 2 * y + c], send_sems.at[m - 1], recv_sems.at[m - 1], (px, py, pc)))
            recvs.append(_remote(v_ref, out_ref.at[4 * px + 2 * py + pc], send_sems.at[m - 1], recv_sems.at[m - 1], (px, py, pc)))
        for cp in sends:
            cp.start()
        for cp in recvs:
            cp.wait_recv()
        for cp in sends:
            cp.wait_send()
        mine.wait()

    return pl.pallas_call(
        body, name=name, in_specs=[HBM], out_specs=HBM,
        out_shape=jax.ShapeDtypeStruct((8, 1, n), v.dtype),
        scratch_shapes=[pltpu.SemaphoreType.DMA((7,)), pltpu.SemaphoreType.DMA((7,)), pltpu.SemaphoreType.DMA],
    )(v)


def exchange_rows(part, name):
    w = part.shape[2]

    def body(p_ref, out_ref, send_sems, recv_sems, local_sem):
        x, y, c = _place()
        k = 2 * x + y
        mine = pltpu.make_async_copy(p_ref.at[4 * x + 2 * y + c], out_ref.at[k], local_sem)
        mine.start()
        sends, recvs = [], []
        for j, (px, py) in enumerate(_other_chips(x, y)):
            sends.append(_remote(p_ref.at[4 * px + 2 * py + c], out_ref.at[k], send_sems.at[j], recv_sems.at[j], (px, py, c)))
            recvs.append(_remote(p_ref.at[4 * px + 2 * py + c], out_ref.at[2 * px + py], send_sems.at[j], recv_sems.at[j], (px, py, c)))
        for cp in sends:
            cp.start()
        for cp in recvs:
            cp.wait_recv()
        for cp in sends:
            cp.wait_send()
        mine.wait()

    return pl.pallas_call(
        body, name=name, in_specs=[HBM], out_specs=HBM,
        out_shape=jax.ShapeDtypeStruct((4, 1, w), part.dtype),
        scratch_shapes=[pltpu.SemaphoreType.DMA((3,)), pltpu.SemaphoreType.DMA((3,)), pltpu.SemaphoreType.DMA],
    )(part)


def gather_weights(shards):
    n = len(shards)

    def body(*refs):
        w_refs, out_refs = refs[:n], refs[n:2 * n]
        send_sems, recv_sems = refs[2 * n:]
        x, y, c = _place()
        chips = _other_chips(x, y)

        def half(i, px, py, pc):
            hr = shards[i].shape[0] // 2
            return out_refs[i].at[2 * px + py, pl.ds(pc * hr, hr), :]

        first, passed = [], []
        for i in range(n):
            hr = shards[i].shape[0] // 2
            for j, (px, py) in enumerate(chips):
                cp = _remote(w_refs[i].at[pl.ds(c * hr, hr), :], half(i, x, y, c),
                             send_sems.at[j * n + i], recv_sems.at[j * n + i], (px, py, c))
                cp.start()
                first.append(cp)
        for i in range(n):
            for j, (px, py) in enumerate(chips):
                mine_half = half(i, px, py, c)
                _remote(mine_half, mine_half, send_sems.at[j * n + i], recv_sems.at[j * n + i], (px, py, c)).wait_recv()
                cp = _remote(mine_half, mine_half, send_sems.at[(3 + j) * n + i], recv_sems.at[(3 + j) * n + i], (x, y, 1 - c))
                cp.start()
                passed.append(cp)
        for i in range(n):
            for j, (px, py) in enumerate(chips):
                other = half(i, px, py, 1 - c)
                _remote(other, other, send_sems.at[(3 + j) * n + i], recv_sems.at[(3 + j) * n + i], (x, y, 1 - c)).wait_recv()
        for cp in first + passed:
            cp.wait_send()

    return pl.pallas_call(
        body, name="gather_weights", in_specs=[HBM] * n, out_specs=[HBM] * n,
        out_shape=[jax.ShapeDtypeStruct((4,) + s.shape, s.dtype) for s in shards],
        scratch_shapes=[pltpu.SemaphoreType.DMA((6 * n,)), pltpu.SemaphoreType.DMA((6 * n,))],
    )(*shards)


def pair_exchange(slabs):
    n = len(slabs)

    def body(*refs):
        g_refs, out_refs = refs[:n], refs[n:2 * n]
        send_sems, recv_sems = refs[2 * n:]
        x, y, c = _place()
        copies = []
        for i in range(n):
            hr = slabs[i].shape[1] // 2
            cp = _remote(g_refs[i].at[:, pl.ds((1 - c) * hr, hr), :], out_refs[i], send_sems.at[i], recv_sems.at[i], (x, y, 1 - c))
            cp.start()
            copies.append(cp)
        for cp in copies:
            cp.wait()

    return pl.pallas_call(
        body, name="pair_exchange", in_specs=[HBM] * n, out_specs=[HBM] * n,
        out_shape=[jax.ShapeDtypeStruct((4, s.shape[1] // 2, s.shape[2]), s.dtype) for s in slabs],
        scratch_shapes=[pltpu.SemaphoreType.DMA((n,)), pltpu.SemaphoreType.DMA((n,))],
    )(*slabs)


def _row_tile(rows, cols):
    fits = lambda r: r * cols * 4 <= BLOCK_BYTES
    if fits(rows):
        return rows
    return next(r for r in (1024, 512, 256, 128, 64, 32, 16) if rows % r == 0 and fits(r))


def pair_add(g, p, c, name):
    _, hr, cols = p.shape
    tm = _row_tile(hr, cols)
    per = hr // tm

    def body(c_ref, g_ref, p_ref, o_ref):
        o_ref[...] = (g_ref[...] + p_ref[...]).astype(o_ref.dtype)

    return pl.pallas_call(
        body, name=name,
        grid_spec=pltpu.PrefetchScalarGridSpec(
            num_scalar_prefetch=1, grid=(4, per),
            in_specs=[pl.BlockSpec((None, tm, cols), lambda k, i, c_ref: (k, c_ref[0] * per + i, 0)),
                      pl.BlockSpec((None, tm, cols), lambda k, i, c_ref: (k, i, 0))],
            out_specs=pl.BlockSpec((None, tm, cols), lambda k, i, c_ref: (k, i, 0))),
        out_shape=jax.ShapeDtypeStruct((4, hr, cols), BF16),
        compiler_params=_params(("arbitrary", "arbitrary")),
    )(c.reshape(1).astype(jnp.int32), g, p)


def scatter_sums(sums):
    n = len(sums)

    def body(*refs):
        s_refs, out_refs = refs[:n], refs[n:2 * n]
        send_sems, recv_sems = refs[2 * n:]
        x, y, c = _place()
        k = 2 * x + y
        sends, recvs = [], []
        for i in range(n):
            for j, (px, py) in enumerate(_other_chips(x, y)):
                sems = (send_sems.at[j * n + i], recv_sems.at[j * n + i])
                sends.append(_remote(s_refs[i].at[2 * px + py], out_refs[i].at[k], *sems, (px, py, c)))
                recvs.append(_remote(s_refs[i].at[2 * px + py], out_refs[i].at[2 * px + py], *sems, (px, py, c)))
        for cp in sends:
            cp.start()
        for cp in recvs:
            cp.wait_recv()
        for cp in sends:
            cp.wait_send()

    return pl.pallas_call(
        body, name="scatter_sums", in_specs=[HBM] * n, out_specs=[HBM] * n,
        out_shape=[jax.ShapeDtypeStruct(s.shape, s.dtype) for s in sums],
        scratch_shapes=[pltpu.SemaphoreType.DMA((3 * n,)), pltpu.SemaphoreType.DMA((3 * n,))],
    )(*sums)


def sum_chips(landed, own, chip, core, name):
    _, hr, cols = landed.shape
    tm = _row_tile(hr, 4 * cols)
    per = hr // tm

    def body(idx_ref, l0, l1, l2, l3, own_ref, o_ref):
        mine = own_ref[...].astype(F32)
        v = [jnp.where(idx_ref[0] == k, mine, ref[...].astype(F32)) for k, ref in enumerate((l0, l1, l2, l3))]
        o_ref[...] = ((v[0] + v[1]) + v[2]) + v[3]

    slot = lambda k: pl.BlockSpec((None, tm, cols),
                                  lambda i, idx: (jnp.where(idx[0] == k, (k + 1) & 3, k), i, 0))
    return pl.pallas_call(
        body, name=name,
        grid_spec=pltpu.PrefetchScalarGridSpec(
            num_scalar_prefetch=1, grid=(per,),
            in_specs=[slot(0), slot(1), slot(2), slot(3),
                      pl.BlockSpec((None, tm, cols), lambda i, idx: (idx[0], i, 0))],
            out_specs=pl.BlockSpec((tm, cols), lambda i, idx: (idx[1] * per + i, 0))),
        out_shape=jax.ShapeDtypeStruct((2 * hr, cols), F32),
        compiler_params=_params(("arbitrary",)),
    )(jnp.stack([chip, core]).astype(jnp.int32), landed, landed, landed, landed, own)


def exchange_halves(bufs):
    n = len(bufs)

    def body(*refs):
        out_refs = refs[n:2 * n]
        send_sems, recv_sems = refs[2 * n:]
        x, y, c = _place()
        sends, recvs = [], []
        for i in range(n):
            hr = bufs[i].shape[0] // 2
            own = out_refs[i].at[pl.ds(c * hr, hr), :]
            other = out_refs[i].at[pl.ds((1 - c) * hr, hr), :]
            sends.append(_remote(own, own, send_sems.at[i], recv_sems.at[i], (x, y, 1 - c)))
            recvs.append(_remote(other, other, send_sems.at[i], recv_sems.at[i], (x, y, 1 - c)))
        for cp in sends:
            cp.start()
        for cp in recvs:
            cp.wait_recv()
        for cp in sends:
            cp.wait_send()

    return pl.pallas_call(
        body, name="exchange_halves", in_specs=[HBM] * n, out_specs=[HBM] * n,
        out_shape=[jax.ShapeDtypeStruct(b.shape, b.dtype) for b in bufs],
        input_output_aliases={i: i for i in range(n)},
        scratch_shapes=[pltpu.SemaphoreType.DMA((n,)), pltpu.SemaphoreType.DMA((n,))],
    )(*bufs)


def _relayout(name, arrays, in_blocks, out_blocks, out_shapes, fn):
    rows = 128
    spec = lambda blk: pl.BlockSpec(blk, (lambda i: (0, i, 0)) if len(blk) == 3 else (lambda i: (i, 0)))

    def body(*refs):
        n_in = len(arrays)
        outs = fn(*[r[...] for r in refs[:n_in]])
        for ref, val in zip(refs[n_in:], outs, strict=True):
            if isinstance(val, list):
                for k, piece in enumerate(val):
                    ref[k] = piece
            else:
                ref[...] = val

    return pl.pallas_call(
        body, name=name, grid=(D // rows,),
        in_specs=[spec(b) for b in in_blocks], out_specs=[spec(b) for b in out_blocks], out_shape=out_shapes,
        compiler_params=_params(("arbitrary",)),
    )(*arrays)


def assemble_in_proj(g):
    def fn(v):
        w = jnp.concatenate([v[k] for k in range(4)], axis=1)
        return (jnp.concatenate([w[:, :ORIG_Z], w[:, ORIG_GA:], w[:, ORIG_XBC:ORIG_DT], w[:, ORIG_Z:ORIG_XBC],
                                 w[:, ORIG_DT:ORIG_GA], jnp.zeros((w.shape[0], IN_PAD - IN_ORIG), w.dtype)], axis=1),)

    cols = g.shape[2]
    return _relayout("assemble_in_proj", [g], [(4, 128, cols)], [(128, IN_PAD)],
                     [jax.ShapeDtypeStruct((D, IN_PAD), g.dtype)], fn)[0]


def split_in_proj(dw):
    cols = IN_ORIG // 4

    def fn(d):
        w = jnp.concatenate([d[:, :COL_GA], d[:, COL_Z:COL_DT], d[:, COL_XBC:COL_Z], d[:, COL_DT:COL_DT + 32],
                             d[:, COL_GA:COL_XBC]], axis=1)
        return ([w[:, k * cols:(k + 1) * cols] for k in range(4)],)

    return _relayout("split_in_proj", [dw], [(128, IN_PAD)], [(4, 128, cols)],
                     [jax.ShapeDtypeStruct((4, D, cols), dw.dtype)], fn)[0]


def assemble_ffn_in(gate, up):
    fn = lambda a, b: (jnp.concatenate([a[k] for k in range(4)] + [b[k] for k in range(4)], axis=1),)
    cols = gate.shape[2]
    return _relayout("assemble_ffn_in", [gate, up], [(4, 128, cols)] * 2, [(128, 2 * D_FF)],
                     [jax.ShapeDtypeStruct((D, 2 * D_FF), gate.dtype)], fn)[0]


def split_ffn_in(dw):
    cols = D_FF // 4

    def fn(d):
        return ([d[:, k * cols:(k + 1) * cols] for k in range(4)],
                [d[:, D_FF + k * cols:D_FF + (k + 1) * cols] for k in range(4)])

    shape = jax.ShapeDtypeStruct((4, D, cols), dw.dtype)
    return _relayout("split_ffn_in", [dw], [(128, 2 * D_FF)], [(4, 128, cols)] * 2, [shape, shape], fn)


def ada_prepare(c_all, w_ada, hgrn_lb):
    def body(c_ref, w_ref, lb_ref, mod_ref, row_ref):
        mod_ref[...] = hdot(silu(c_ref[...]), w_ref[...])
        row_ref[...] = sigmoid(lb_ref[0:1, :] - lb_ref[1:2, :])

    return pl.pallas_call(
        body, name="ada_prepare",
        out_shape=[jax.ShapeDtypeStruct((8, w_ada.shape[1]), F32), jax.ShapeDtypeStruct((1, D), F32)],
        compiler_params=pltpu.CompilerParams(vmem_limit_bytes=VMEM_LIMIT),
    )(c_all, w_ada, hgrn_lb)


SMALL_SEGS = (("mod", 6 * D), ("lb", D), ("gnorm", LANES), ("conv_w", 4 * CONV_DIM), ("conv_b", CONV_DIM),
              ("dt_bias", B_INNER), ("a_log", B_INNER), ("d", B_INNER), ("ssm_norm", B_INNER),
              ("ln1_g", D), ("ln1_b", D), ("ln2_g", D), ("ln2_b", D))
SMALL_PARAMS = ("b_ada", "hgrn_lb", "hgrn_gnorm", "ssm_conv_b", "ssm_dt_bias", "ssm_a_log", "ssm_d", "ssm_norm",
                "ln1_g", "ln1_b", "ln2_g", "ln2_b")


def finalize_small(g_all, c_all, dmod_cols, params, m, v):
    n_p = len(SMALL_PARAMS)
    offs, o = {}, 0
    for nm, width in SMALL_SEGS:
        offs[nm] = (o, width)
        o += width

    def body(*refs):
        g_ref, c_ref, dm_ref = refs[:3]
        p_refs = refs[3:3 + n_p]
        m_refs = refs[3 + n_p:3 + 2 * n_p]
        v_refs = refs[3 + 2 * n_p:3 + 3 * n_p]
        outs = refs[3 + 3 * n_p:]
        gwa_ref, gcw_ref = outs[:2]
        res = outs[2:]
        total = jnp.sum(g_ref[...], axis=0, keepdims=True)
        seg = lambda nm: total[:, offs[nm][0]:offs[nm][0] + offs[nm][1]]
        gwa_ref[...] = hdot(silu(c_ref[...]), dm_ref[...], "tn")
        cw = seg("conv_w")
        for j in range(4):
            gcw_ref[j:j + 1, :] = cw[:, j * CONV_DIM:(j + 1) * CONV_DIM]
        hc = lax.broadcasted_iota(jnp.int32, (B_INNER, LANES), 0)
        hj = lax.broadcasted_iota(jnp.int32, (B_INNER, LANES), 1)
        per_head = ((hc >> 6) == hj).astype(F32)
        heads = lambda nm: hdot(jnp.broadcast_to(seg(nm), (8, B_INNER)), per_head)[0:1, 0:32]
        lbp = sigmoid(p_refs[1][0:1, :] - p_refs[1][1:2, :])
        g_row = seg("lb") * lbp * (1.0 - lbp)
        grads = {"b_ada": seg("mod"), "hgrn_gnorm": seg("gnorm"), "ssm_conv_b": seg("conv_b"),
                 "ssm_dt_bias": heads("dt_bias"), "ssm_a_log": heads("a_log"), "ssm_d": heads("d"),
                 "ssm_norm": seg("ssm_norm"), "ln1_g": seg("ln1_g"), "ln1_b": seg("ln1_b"),
                 "ln2_g": seg("ln2_g"), "ln2_b": seg("ln2_b")}
        for i, nm in enumerate(SMALL_PARAMS):
            g_out, d_out, m_out, v_out = res[4 * i:4 * i + 4]
            if nm == "hgrn_lb":
                for row, gv in ((0, g_row), (1, -g_row)):
                    sl = slice(row, row + 1)
                    dl, mn, vn = adamw(p_refs[i][sl, :], gv, m_refs[i][sl, :], v_refs[i][sl, :])
                    g_out[sl, :], d_out[sl, :], m_out[sl, :], v_out[sl, :] = gv, dl, mn, vn
            else:
                gv = grads[nm]
                dl, mn, vn = adamw(p_refs[i][...], gv, m_refs[i][...], v_refs[i][...])
                g_out[...], d_out[...], m_out[...], v_out[...] = gv, dl, mn, vn

    out_shape = [jax.ShapeDtypeStruct((D, dmod_cols.shape[1]), F32), jax.ShapeDtypeStruct((4, CONV_DIM), F32)]
    for p in params:
        out_shape += [jax.ShapeDtypeStruct(p.shape, F32)] * 4
    return pl.pallas_call(
        body, name="finalize_small", out_shape=out_shape,
        compiler_params=pltpu.CompilerParams(vmem_limit_bytes=VMEM_LIMIT),
    )(g_all, c_all, dmod_cols, *params, *m, *v)


def adam_update(w, g, m, v, name):
    cols = w.shape[1]
    return rowwise(name, lambda r, c: (adamw(*r), ()), [_full(w), _full(g), _full(m), _full(v)], [],
                   [(cols, F32)] * 3, tm_max=128)


def kernel(x, c, w_ada, b_ada, w_in, hgrn_lb, hgrn_gnorm, ssm_conv_w, ssm_conv_b, ssm_dt_bias, ssm_a_log, ssm_d, ssm_norm, w_branch_a, w_branch_b, w_o, ln1_g, ln1_b, w_ffn_gate, w_ffn_up, w_ffn_down, ln2_g, ln2_b, loss_target, m_w_ada, m_b_ada, m_w_in, m_hgrn_lb, m_hgrn_gnorm, m_ssm_conv_w, m_ssm_conv_b, m_ssm_dt_bias, m_ssm_a_log, m_ssm_d, m_ssm_norm, m_w_branch_a, m_w_branch_b, m_w_o, m_ln1_g, m_ln1_b, m_w_ffn_gate, m_w_ffn_up, m_w_ffn_down, m_ln2_g, m_ln2_b, v_w_ada, v_b_ada, v_w_in, v_hgrn_lb, v_hgrn_gnorm, v_ssm_conv_w, v_ssm_conv_b, v_ssm_dt_bias, v_ssm_a_log, v_ssm_d, v_ssm_norm, v_w_branch_a, v_w_branch_b, v_w_o, v_ln1_g, v_ln1_b, v_w_ffn_gate, v_w_ffn_up, v_w_ffn_down, v_ln2_g, v_ln2_b):
    given = dict(locals())
    chip = 2 * lax.axis_index("x") + lax.axis_index("y")
    core = lax.axis_index("c")
    t = x.shape[1]

    first = gather_rows(jnp.concatenate([c, ssm_conv_w.reshape(1, CONV_DIM)], axis=1), "gather_cond").reshape(8, D + CONV_DIM)
    c_all = first[:, :D]
    conv_w = first[0::2, D:].reshape(4, 4, CONV_DIM // 4).transpose(1, 0, 2).reshape(4, CONV_DIM)
    mod_part, lb_row = ada_prepare(c_all, w_ada[0], hgrn_lb)
    mod_cols = w_ada.shape[2]
    mod_row = exchange_rows(mod_part.reshape(8, 1, mod_cols), "exchange_mod").reshape(1, 6 * D) + b_ada
    mod = tuple(mod_row[:, i * D:(i + 1) * D] for i in range(6))

    shards = [given[nm][0].astype(BF16) for nm in SHARDED]
    got = {nm: lax.dynamic_update_slice(g, s[None], (chip, 0, 0))
           for nm, g, s in zip(SHARDED, gather_weights(shards), shards, strict=True)}
    whole = lambda nm: got[nm].reshape(4 * got[nm].shape[1], got[nm].shape[2])
    wts = (assemble_in_proj(got["w_in"]), whole("w_branch_a"), whole("w_branch_b"), whole("w_o"),
           assemble_ffn_in(got["w_ffn_gate"], got["w_ffn_up"]), whole("w_ffn_down"))

    per_channel = lambda p: jnp.repeat(p[0], B_INNER // 32)[None]
    small = (lb_row, hgrn_gnorm, conv_w, ssm_conv_b, per_channel(ssm_dt_bias), per_channel(ssm_a_log),
             per_channel(ssm_d), ssm_norm, ln1_g, ln1_b, ln2_g, ln2_b)
    loss, grad_x, d_mod, d_wts, d_small = local_step(x[0], loss_target[0], mod, wts, small)

    d_lb, d_gn, d_cw, d_cb, d_dtb, d_alog, d_dsk, d_nw, d_l1g, d_l1b, d_l2g, d_l2b = d_small
    row = jnp.concatenate(list(d_mod) + [d_lb, d_gn, d_cw.reshape(1, 4 * CONV_DIM), d_cb, d_dtb, d_alog, d_dsk, d_nw,
                                          d_l1g, d_l1b, d_l2g, d_l2b], axis=1)
    g_all = gather_rows(row, "gather_small_grads").reshape(8, row.shape[1])
    dmod_cols = lax.dynamic_slice_in_dim(g_all, chip * mod_cols, mod_cols, axis=1)
    fin = finalize_small(g_all, c_all, dmod_cols, [given[n] for n in SMALL_PARAMS],
                         [given["m_" + n] for n in SMALL_PARAMS], [given["v_" + n] for n in SMALL_PARAMS])
    grads, deltas, new_m, new_v = {}, {}, {}, {}
    grads["w_ada"] = fin[0][None]
    grads["ssm_conv_w"] = lax.dynamic_slice_in_dim(fin[1], chip * (CONV_DIM // 4), CONV_DIM // 4, axis=1)[None]
    for i, nm in enumerate(SMALL_PARAMS):
        grads[nm], deltas[nm], new_m[nm], new_v[nm] = fin[2 + 4 * i:6 + 4 * i]

    dw_in, dw_a, dw_b, dw_o, dw_gu, dw_d = d_wts
    by_rows = lambda g: g.reshape(4, g.shape[0] // 4, g.shape[1])
    d_gate, d_up = split_ffn_in(dw_gu)
    slabs = [split_in_proj(dw_in), by_rows(dw_a), by_rows(dw_b), by_rows(dw_o), d_gate, d_up, by_rows(dw_d)]
    pairs = [pair_add(s, r, core, "pair_add_" + nm) for nm, s, r in zip(SHARDED, slabs, pair_exchange(slabs), strict=True)]
    halves = [sum_chips(r, p, chip, core, "sum_chips_" + nm)
              for nm, r, p in zip(SHARDED, scatter_sums(pairs), pairs, strict=True)]
    for nm, r in zip(SHARDED, exchange_halves(halves), strict=True):
        grads[nm] = r[None]
    for nm in ("w_ada", "ssm_conv_w") + SHARDED:
        shp = given[nm].shape
        two_d = lambda a: a.reshape(shp[-2], shp[-1])
        d_, m_, v_ = adam_update(two_d(given[nm]), two_d(grads[nm]), two_d(given["m_" + nm]), two_d(given["v_" + nm]),
                                 "adam_" + nm)
        deltas[nm], new_m[nm], new_v[nm] = d_.reshape(shp), m_.reshape(shp), v_.reshape(shp)

    names = ("w_ada", "b_ada", "w_in", "hgrn_lb", "hgrn_gnorm", "ssm_conv_w", "ssm_conv_b", "ssm_dt_bias", "ssm_a_log",
             "ssm_d", "ssm_norm", "w_branch_a", "w_branch_b", "w_o", "ln1_g", "ln1_b", "w_ffn_gate", "w_ffn_up",
             "w_ffn_down", "ln2_g", "ln2_b")
    total_loss = lax.psum(loss[0, 0], ("x", "y", "c"))
    return (total_loss, grad_x[None], *[grads[n] for n in names], *[deltas[n] for n in names],
            *[new_m[n] for n in names], *[new_v[n] for n in names])
```

```python
import functools

import jax
import jax.numpy as jnp
from jax import lax
from jax.experimental import pallas as pl
from jax.experimental.pallas import tpu as pltpu

F32, BF16 = jnp.float32, jnp.bfloat16
HI = lax.Precision.HIGHEST
MESH = pl.DeviceIdType.MESH

D = 1024
CHUNK = 64
LANES = 128
N_HEADS_A = 8
N_GROUPS_B = 4
B_INNER = 2048
CONV_DIM = 3072
D_FF = 2816
ALPHA = 2.0 ** 0.25
LN_EPS = 1e-5
RMS_EPS = 1e-6
ADAM_LR, ADAM_B1, ADAM_B2, ADAM_EPS, ADAM_WD, ADAM_STEP = 0.001, 0.9, 0.999, 1e-08, 0.01, 10

IN_ORIG = 11296
IN_PAD = 11520
COL_GA, COL_GB, COL_XBC, COL_Z, COL_DT = 4096, 5120, 6144, 9216, 11264
ORIG_Z, ORIG_XBC, ORIG_DT, ORIG_GA = 4096, 6144, 9216, 9248

SHARDED = ("w_in", "w_branch_a", "w_branch_b", "w_o", "w_ffn_gate", "w_ffn_up", "w_ffn_down")
VMEM_LIMIT = 56 * 1024 * 1024
BLOCK_BYTES = 2 * 1024 * 1024

_DIMS = {"nn": (((1,), (0,)), ((), ())), "nt": (((1,), (1,)), ((), ())), "tn": (((0,), (0,)), ((), ()))}


def _bd(a, b, mode):
    return lax.dot_general(a.astype(BF16), b.astype(BF16), _DIMS[mode], preferred_element_type=F32)


@functools.partial(jax.custom_vjp, nondiff_argnums=(2,))
def bdot(a, b, mode):
    return _bd(a, b, mode)


def _bdot_fwd(a, b, mode):
    return _bd(a, b, mode), (a, b)


def _bdot_bwd(mode, res, g):
    a, b = res
    if mode == "nn":
        return _bd(g, b, "nt"), _bd(a, g, "tn")
    if mode == "nt":
        return _bd(g, b, "nn"), _bd(g, a, "tn")
    return _bd(b, g, "nt"), _bd(a, g, "nn")


bdot.defvjp(_bdot_fwd, _bdot_bwd)


def hdot(a, b, mode="nn"):
    return lax.dot_general(a, b, _DIMS[mode], precision=HI, preferred_element_type=F32)


def _raw(a, b, mode):
    return lax.dot_general(a, b, _DIMS[mode], preferred_element_type=F32)


def _split(x, n):
    parts, rest = [], x
    for _ in range(n):
        p = rest.astype(BF16)
        parts.append(p)
        rest = rest - p.astype(F32)
    return parts


def _od(a, b, mode, exact):
    if exact == 1:
        e = b.astype(BF16)
        p = _split(a, 3)
        return (_raw(p[2], e, mode) + _raw(p[1], e, mode)) + _raw(p[0], e, mode)
    e = a.astype(BF16)
    p = _split(b, 3)
    return (_raw(e, p[2], mode) + _raw(e, p[1], mode)) + _raw(e, p[0], mode)


@functools.partial(jax.custom_vjp, nondiff_argnums=(2, 3))
def odot(a, b, mode, exact):
    return _od(a, b, mode, exact)


def _odot_fwd(a, b, mode, exact):
    return _od(a, b, mode, exact), (a, b)


def _odot_bwd(mode, exact, res, g):
    a, b = res
    if exact == 1:
        da = {"nn": lambda: _od(g, b, "nt", 1), "nt": lambda: _od(g, b, "nn", 1), "tn": lambda: _od(b, g, "nt", 0)}[mode]()
        return da, jnp.zeros_like(b)
    db = {"nn": lambda: _od(a, g, "tn", 0), "nt": lambda: _od(g, a, "tn", 1), "tn": lambda: _od(a, g, "nn", 0)}[mode]()
    return jnp.zeros_like(a), db


odot.defvjp(_odot_fwd, _odot_bwd)


_BDIMS = {"bnn": (((2,), (1,)), ((0,), (0,))), "bnt": (((2,), (2,)), ((0,), (0,))), "btn": (((1,), (1,)), ((0,), (0,)))}


def _braw(a, b, mode):
    return lax.dot_general(a, b, _BDIMS[mode], preferred_element_type=F32)


def _bdb(a, b, mode):
    return _braw(a.astype(BF16), b.astype(BF16), mode)


def _d3b(a, b, mode):
    ah, al = _split(a, 2)
    bh, bl = _split(b, 2)
    return _braw(ah, bh, mode) + (_braw(ah, bl, mode) + _braw(al, bh, mode))


def _batched_bwd(f):
    def bwd(mode, res, g):
        a, b = res
        if mode == "bnn":
            return f(g, b, "bnt"), f(a, g, "btn")
        if mode == "bnt":
            return f(g, b, "bnn"), f(g, a, "btn")
        return f(b, g, "bnt"), f(a, g, "bnn")
    return bwd


@functools.partial(jax.custom_vjp, nondiff_argnums=(2,))
def bdot_b(a, b, mode):
    return _bdb(a, b, mode)


bdot_b.defvjp(lambda a, b, mode: (_bdb(a, b, mode), (a, b)), _batched_bwd(_bdb))


@functools.partial(jax.custom_vjp, nondiff_argnums=(2,))
def dot3_b(a, b, mode):
    return _d3b(a, b, mode)


dot3_b.defvjp(lambda a, b, mode: (_d3b(a, b, mode), (a, b)), _batched_bwd(_d3b))


def _cum(tril3, x, mode):
    e = tril3.astype(BF16)
    p = _split(x, 3)
    return (_braw(e, p[2], mode) + _braw(e, p[1], mode)) + _braw(e, p[0], mode)


@jax.custom_vjp
def chunk_cumsum(tril3, x):
    return _cum(tril3, x, "bnn")


chunk_cumsum.defvjp(lambda t, x: (_cum(t, x, "bnn"), t), lambda t, g: (jnp.zeros_like(t), _cum(t, g, "btn")))


def _unstack(axis, n):
    @jax.custom_vjp
    def un(x):
        return tuple(lax.index_in_dim(x, i, axis, keepdims=False) for i in range(n))

    un.defvjp(lambda x: (un(x), None), lambda _, g: (jnp.stack(g, axis=axis),))
    return un


def _split_last(n, w):
    @jax.custom_vjp
    def sp(x):
        return tuple(x[..., i * w:(i + 1) * w] for i in range(n))

    sp.defvjp(lambda x: (sp(x), None), lambda _, g: (jnp.concatenate(g, axis=-1),))
    return sp


def sigmoid(x):
    return 1.0 / (1.0 + jnp.exp(-x))


def silu(x):
    return x * sigmoid(x)


def softplus(x):
    return jnp.maximum(x, 0.0) + jnp.log1p(jnp.exp(jnp.minimum(x, -x)))


def _ln(x):
    mu = jnp.mean(x, axis=-1, keepdims=True)
    xc = x - mu
    return xc * lax.rsqrt(jnp.mean(xc * xc, axis=-1, keepdims=True) + LN_EPS)


def _tril64():
    r = lax.broadcasted_iota(jnp.int32, (CHUNK, CHUNK), 0)
    c = lax.broadcasted_iota(jnp.int32, (CHUNK, CHUNK), 1)
    return (r >= c).astype(F32)


def hgrn_block(q, fl, iv, gr, st, lb, gn):
    tb = q.shape[0]
    nc = tb // CHUNK
    nh = N_HEADS_A
    heads = _split_last(nh, LANES)
    to4 = lambda a: jnp.stack(heads(a), axis=0).reshape(nh, nc, CHUNK, LANES)
    flat = lambda a: a.reshape(nh * nc, CHUNK, LANES)
    f = lb + (1.0 - lb) * sigmoid(fl)
    gl4, k4, qf4, v4, gr4 = to4(jnp.log(f)), to4(1.0 - f), to4(silu(q) * (128 ** -0.5)), to4(iv), to4(gr)
    tril = _tril64()
    b4 = chunk_cumsum(jnp.broadcast_to(tril[None], (nh * nc, CHUNK, CHUNK)), flat(gl4)).reshape(gl4.shape)
    blast = jnp.sum(gl4, axis=2, keepdims=True)
    ref = lax.stop_gradient(0.5 * blast)
    sc = dot3_b(flat(qf4 * jnp.exp(b4 - ref)), flat(k4 * jnp.exp(ref - b4)), "bnt") * tril
    o_intra = bdot_b(sc, flat(v4), "bnn").reshape(gl4.shape)
    chunks = _unstack(1, nc)
    qe, v_c, kd, dec = chunks(qf4 * jnp.exp(b4)), chunks(v4), chunks(k4 * jnp.exp(blast - b4)), chunks(jnp.exp(blast))
    o_inter = []
    for c in range(nc):
        o_inter.append(bdot_b(qe[c], st, "bnt"))
        st = st * dec[c] + bdot_b(v_c[c], kd[c], "btn")
    o = o_intra + jnp.stack(o_inter, axis=1)
    on = o * lax.rsqrt(jnp.mean(o * o, axis=-1, keepdims=True) + RMS_EPS) * gn
    out = (on * silu(gr4)).reshape(nh, tb, LANES)
    return jnp.concatenate(_unstack(0, nh)(out), axis=1), st


def ssd_consts(g):
    i32 = jnp.int32
    ej = lax.broadcasted_iota(i32, (LANES, 512), 0)
    ec = lax.broadcasted_iota(i32, (LANES, 512), 1)
    expand = (ej == g * 8 + (ec >> 6)).astype(F32)
    ts = lax.broadcasted_iota(i32, (CHUNK, 512), 0)
    tc = lax.broadcasted_iota(i32, (CHUNK, 512), 1)
    itile = (ts == (tc & 63)).astype(F32)
    maskall = ts >= (tc & 63)
    br = lax.broadcasted_iota(i32, (256, 256), 0)
    bc = lax.broadcasted_iota(i32, (256, 256), 1)
    blockmask = ((br >> 6) == (bc >> 6)).astype(F32)
    return expand, itile, maskall, blockmask, _tril64()


def ssd_block(x, bm, cm, dt, z, st, dtb, alog, dsk, nw, cs):
    expand, itile, maskall, blockmask, tril = cs
    tb = x.shape[0]
    nc = tb // CHUNK
    delta = softplus(odot(dt, expand, "nn", 1) + dtb)
    a = -jnp.exp(alog) * delta
    xdt = x * delta
    by_chunk = lambda v: v.reshape(nc, CHUNK, v.shape[-1])
    a3, xdt3, bm3, cm3 = by_chunk(a), by_chunk(xdt), by_chunk(bm), by_chunk(cm)
    acum3 = chunk_cumsum(jnp.broadcast_to(tril[None], (nc, CHUNK, CHUNK)), a3)
    alast3 = jnp.sum(a3, axis=1, keepdims=True)
    cb3 = bdot_b(cm3, jnp.concatenate([bm3] * 8, axis=1), "bnt")
    arow3 = jnp.sum(acum3 * itile, axis=1, keepdims=True)
    dec3 = jnp.where(maskall, jnp.exp(jnp.minimum(acum3 - arow3, 0.0)), 0.0)
    halves = _split_last(2, 256)
    intra = [bdot_b(m, jnp.concatenate([xh] * 4, axis=1) * blockmask, "bnn")
             for m, xh in zip(halves(cb3 * dec3), halves(xdt3))]
    chunks = _unstack(0, nc)
    cm_c, bm_c, xw_c, dec_c = chunks(cm3), chunks(bm3), chunks(xdt3 * jnp.exp(alast3 - acum3)), chunks(jnp.exp(alast3))
    inter = []
    for c in range(nc):
        inter.append(bdot(cm_c[c], st, "nn"))
        st = st * dec_c[c] + bdot(bm_c[c], xw_c[c], "tn")
    st_new = st
    y = (jnp.concatenate(intra, axis=-1) + jnp.stack(inter, axis=0) * jnp.exp(acum3)).reshape(tb, 512)
    yz = (y + x * dsk) * silu(z)
    return yz * lax.rsqrt(jnp.mean(yz * yz, axis=-1, keepdims=True) + RMS_EPS) * nw, st_new


def adamw(w, g, m, v):
    m = ADAM_B1 * m + (1.0 - ADAM_B1) * g
    v = ADAM_B2 * v + (1.0 - ADAM_B2) * jnp.square(g)
    m_hat = m / (1.0 - ADAM_B1 ** ADAM_STEP)
    v_hat = v / (1.0 - ADAM_B2 ** ADAM_STEP)
    return -ADAM_LR * (m_hat / (jnp.sqrt(v_hat) + ADAM_EPS) + ADAM_WD * w), m, v


def _pick(n, cands):
    for c in cands:
        if n % c == 0:
            return c
    return n


def _params(sem):
    return pltpu.CompilerParams(dimension_semantics=sem, vmem_limit_bytes=VMEM_LIMIT)


def matmul(a, b, mode, out_dtype, name):
    if mode == "nn":
        (m, k), n = a.shape, b.shape[1]
    elif mode == "nt":
        (m, k), n = a.shape, b.shape[0]
    else:
        (k, m), n = a.shape, b.shape[1]
    tm = _pick(m, (1024, 512, 256, 128))
    tn = _pick(n, (1408, 1024, 768, 512, 256, 128))
    tk = _pick(k, (2304, 2048, 1408, 1024, 768, 512, 256, 128))
    nk = k // tk
    a_spec = pl.BlockSpec((tk, tm), lambda i, j, kk: (kk, i)) if mode == "tn" else pl.BlockSpec((tm, tk), lambda i, j, kk: (i, kk))
    b_spec = pl.BlockSpec((tn, tk), lambda i, j, kk: (j, kk)) if mode == "nt" else pl.BlockSpec((tk, tn), lambda i, j, kk: (kk, j))

    def body(a_ref, b_ref, o_ref, *acc):
        part = _bd(a_ref[...], b_ref[...], mode)
        if nk == 1:
            o_ref[...] = part.astype(o_ref.dtype)
            return
        acc_ref, = acc
        kk = pl.program_id(2)

        @pl.when(kk == 0)
        def _():
            acc_ref[...] = part

        @pl.when(jnp.logical_and(kk > 0, kk < nk - 1))
        def _():
            acc_ref[...] += part

        @pl.when(kk == nk - 1)
        def _():
            o_ref[...] = (acc_ref[...] + part).astype(o_ref.dtype)

    return pl.pallas_call(
        body, name=name, grid=(m // tm, n // tn, nk),
        in_specs=[a_spec, b_spec], out_specs=pl.BlockSpec((tm, tn), lambda i, j, kk: (i, j)),
        out_shape=jax.ShapeDtypeStruct((m, n), out_dtype),
        scratch_shapes=[pltpu.VMEM((tm, tn), F32)] if nk > 1 else [],
        compiler_params=_params(("parallel", "parallel", "arbitrary")),
    )(a, b)


def rowwise(name, fn, rows, consts, out_rows, out_accs=(), tm_max=256, into=None, new_wide=None):
    t = rows[0][0].shape[0]
    tm = _pick(t, (tm_max, 128, 64, 32, 16, 8))
    n_r, n_c, n_o = len(rows), len(consts), len(out_rows)
    n_alias = 0 if into is None else 1

    def body(*refs):
        r_in = [r[...] for r in refs[:n_r]]
        c_in = [r[...] for r in refs[n_r:n_r + n_c]]
        refs = refs[:n_r + n_c] + refs[n_r + n_c + n_alias:]
        o_refs = refs[n_r + n_c:n_r + n_c + n_o]
        a_refs = refs[n_r + n_c + n_o:]
        ro, ao = fn(r_in, c_in)
        for ref, val in zip(o_refs, ro, strict=True):
            ref[...] = val.astype(ref.dtype)
        if a_refs:
            @pl.when(pl.program_id(0) == 0)
            def _():
                for ref in a_refs:
                    ref[...] = jnp.zeros_like(ref)

            for ref, val in zip(a_refs, ao, strict=True):
                ref[...] += val

    in_specs = [pl.BlockSpec((tm, w), functools.partial(lambda i, cb: (i, cb), cb=cb)) for _, w, cb in rows]
    in_specs += [pl.BlockSpec(c.shape, lambda i: (0, 0)) for c in consts]
    out_specs = [pl.BlockSpec((tm, w), lambda i: (i, 0)) for w, _ in out_rows]
    out_specs += [pl.BlockSpec(s, lambda i: (0, 0)) for s in out_accs]
    out_shape = [jax.ShapeDtypeStruct((t, w), dt) for w, dt in out_rows]
    out_shape += [jax.ShapeDtypeStruct(s, F32) for s in out_accs]
    operands = [r[0] for r in rows] + list(consts)
    aliases = {}
    if into is not None:
        target, cb = into
        in_specs.append(pl.BlockSpec(memory_space=pl.ANY))
        operands.append(target)
        out_specs[0] = pl.BlockSpec((tm, out_rows[0][0]), lambda i: (i, cb))
        out_shape[0] = jax.ShapeDtypeStruct(target.shape, target.dtype)
        aliases = {len(operands) - 1: 0}
    if new_wide is not None:
        width, cb = new_wide
        out_specs[0] = pl.BlockSpec((tm, out_rows[0][0]), lambda i: (i, cb))
        out_shape[0] = jax.ShapeDtypeStruct((t, width), out_rows[0][1])
    return pl.pallas_call(
        body, name=name, grid=(t // tm,), in_specs=in_specs, out_specs=out_specs, out_shape=out_shape,
        input_output_aliases=aliases, compiler_params=_params(("arbitrary",)),
    )(*operands)


def _full(a):
    return (a, a.shape[1], 0)


def _time_block(t):
    return _pick(t, (256, 128, 64))


def _quarters(ref):
    return [ref[:, seg * D:(seg + 1) * D] for seg in range(4)]


def hgrn_forward(proj, lb, gn):
    t = proj.shape[0]
    tb = _time_block(t)
    nb = t // tb

    def body(qfig_ref, lb_ref, gn_ref, o_ref, st_ref, state):
        @pl.when(pl.program_id(0) == 0)
        def _():
            state[...] = jnp.zeros_like(state)

        st = state[...]
        st_ref[...] = st
        out, st_new = hgrn_block(*_quarters(qfig_ref), st, lb_ref[...], gn_ref[...])
        o_ref[...] = out.astype(o_ref.dtype)
        state[...] = st_new

    return pl.pallas_call(
        body, name="hgrn_forward", grid=(nb,),
        in_specs=[pl.BlockSpec((tb, 4 * D), lambda j: (j, 0)),
                  pl.BlockSpec((1, D), lambda j: (0, 0)), pl.BlockSpec((1, LANES), lambda j: (0, 0))],
        out_specs=[pl.BlockSpec((tb, D), lambda j: (j, 0)),
                   pl.BlockSpec((None, N_HEADS_A, LANES, LANES), lambda j: (j, 0, 0, 0))],
        out_shape=[jax.ShapeDtypeStruct((t, D), BF16),
                   jax.ShapeDtypeStruct((nb, N_HEADS_A, LANES, LANES), F32)],
        scratch_shapes=[pltpu.VMEM((N_HEADS_A, LANES, LANES), F32)],
        compiler_params=_params(("arbitrary",)),
    )(proj, lb, gn)


def hgrn_backward(proj, states, d_out, lb, gn, d_proj):
    t = proj.shape[0]
    tb = _time_block(t)
    nb = t // tb

    def body(qfig_ref, st_ref, do_ref, lb_ref, gn_ref, _, dqfig_ref, dlb_ref, dgn_ref, d_state):
        @pl.when(pl.program_id(0) == 0)
        def _():
            d_state[...] = jnp.zeros_like(d_state)
            dlb_ref[...] = jnp.zeros_like(dlb_ref)
            dgn_ref[...] = jnp.zeros_like(dgn_ref)

        _, vjp = jax.vjp(hgrn_block, *_quarters(qfig_ref), st_ref[...], lb_ref[...], gn_ref[...])
        dq, df, di, dg, dst, dlb, dgn = vjp((do_ref[...], d_state[...]))
        for seg, val in enumerate((dq, df, di, dg)):
            dqfig_ref[:, seg * D:(seg + 1) * D] = val.astype(dqfig_ref.dtype)
        d_state[...] = dst
        dlb_ref[...] += dlb
        dgn_ref[...] += dgn

    rev = lambda j: nb - 1 - j
    return pl.pallas_call(
        body, name="hgrn_backward", grid=(nb,),
        in_specs=[pl.BlockSpec((tb, 4 * D), lambda j: (rev(j), 0)),
                  pl.BlockSpec((None, N_HEADS_A, LANES, LANES), lambda j: (rev(j), 0, 0, 0)),
                  pl.BlockSpec((tb, D), lambda j: (rev(j), 0)),
                  pl.BlockSpec((1, D), lambda j: (0, 0)), pl.BlockSpec((1, LANES), lambda j: (0, 0)),
                  pl.BlockSpec(memory_space=pl.ANY)],
        out_specs=[pl.BlockSpec((tb, 4 * D), lambda j: (rev(j), 0)),
                   pl.BlockSpec((1, D), lambda j: (0, 0)), pl.BlockSpec((1, LANES), lambda j: (0, 0))],
        out_shape=[jax.ShapeDtypeStruct(d_proj.shape, d_proj.dtype), jax.ShapeDtypeStruct((1, D), F32),
                   jax.ShapeDtypeStruct((1, LANES), F32)],
        input_output_aliases={5: 0},
        scratch_shapes=[pltpu.VMEM((N_HEADS_A, LANES, LANES), F32)],
        compiler_params=_params(("arbitrary",)),
    )(proj, states, d_out, lb, gn, d_proj)


def _ssd_in_specs(tb, tmap):
    return [pl.BlockSpec((tb, 512), lambda g, j: (tmap(j), g)),
            pl.BlockSpec((tb, LANES), lambda g, j: (tmap(j), 16 + g)),
            pl.BlockSpec((tb, LANES), lambda g, j: (tmap(j), 20 + g)),
            pl.BlockSpec((tb, LANES), lambda g, j: (tmap(j), COL_DT // LANES)),
            pl.BlockSpec((tb, 512), lambda g, j: (tmap(j), COL_Z // 512 + g))]


def ssd_forward(xc, proj, dtb, alog, dsk, nw):
    t = proj.shape[0]
    tb = _time_block(t)
    nb = t // tb

    def body(x_ref, b_ref, c_ref, dt_ref, z_ref, dtb_ref, alog_ref, dsk_ref, nw_ref, o_ref, st_ref, state):
        @pl.when(pl.program_id(1) == 0)
        def _():
            state[...] = jnp.zeros_like(state)

        st = state[...]
        st_ref[...] = st
        out, st_new = ssd_block(x_ref[...], b_ref[...], c_ref[...], dt_ref[...], z_ref[...], st,
                                dtb_ref[...], alog_ref[...], dsk_ref[...], nw_ref[...], ssd_consts(pl.program_id(0)))
        o_ref[...] = out.astype(o_ref.dtype)
        state[...] = st_new

    vec = pl.BlockSpec((1, 512), lambda g, j: (0, g))
    return pl.pallas_call(
        body, name="ssd_forward", grid=(N_GROUPS_B, nb),
        in_specs=_ssd_in_specs(tb, lambda j: j) + [vec] * 4,
        out_specs=[pl.BlockSpec((tb, 512), lambda g, j: (j, g)),
                   pl.BlockSpec((None, None, LANES, 512), lambda g, j: (j, g, 0, 0))],
        out_shape=[jax.ShapeDtypeStruct((t, B_INNER), BF16),
                   jax.ShapeDtypeStruct((nb, N_GROUPS_B, LANES, 512), F32)],
        scratch_shapes=[pltpu.VMEM((LANES, 512), F32)],
        compiler_params=_params(("arbitrary", "arbitrary")),
    )(xc, xc, xc, proj, proj, dtb, alog, dsk, nw)


def ssd_backward(xc, proj, states, d_out, dtb, alog, dsk, nw, d_proj):
    t = proj.shape[0]
    tb = _time_block(t)
    nb = t // tb
    rev = lambda j: nb - 1 - j

    def body(x_ref, b_ref, c_ref, dt_ref, z_ref, st_ref, do_ref, dtb_ref, alog_ref, dsk_ref, nw_ref, _,
             dx_ref, db_ref, dc_ref, ddt_ref, dz_ref, ddtb_ref, dalog_ref, ddsk_ref, dnw_ref, d_state):
        accs = (ddtb_ref, dalog_ref, ddsk_ref, dnw_ref)

        @pl.when(pl.program_id(1) == 0)
        def _():
            d_state[...] = jnp.zeros_like(d_state)
            for ref in accs:
                ref[...] = jnp.zeros_like(ref)

        cs = ssd_consts(pl.program_id(0))
        fn = lambda *a: ssd_block(*a, cs)
        _, vjp = jax.vjp(fn, x_ref[...], b_ref[...], c_ref[...], dt_ref[...], z_ref[...], st_ref[...],
                         dtb_ref[...], alog_ref[...], dsk_ref[...], nw_ref[...])
        dx, db, dc, ddt, dz, dst, *dpar = vjp((do_ref[...], d_state[...]))
        dx_ref[...] = dx
        db_ref[...] = db
        dc_ref[...] = dc
        ddt_ref[...] = ddt
        dz_ref[...] = dz.astype(dz_ref.dtype)
        d_state[...] = dst
        for ref, val in zip(accs, dpar, strict=True):
            ref[...] += val

    vec = pl.BlockSpec((1, 512), lambda g, j: (0, g))
    acc = pl.BlockSpec((None, 1, 512), lambda g, j: (g, 0, 0))
    return pl.pallas_call(
        body, name="ssd_backward", grid=(N_GROUPS_B, nb),
        in_specs=_ssd_in_specs(tb, rev)
        + [pl.BlockSpec((None, None, LANES, 512), lambda g, j: (rev(j), g, 0, 0)),
           pl.BlockSpec((tb, 512), lambda g, j: (rev(j), g))] + [vec] * 4 + [pl.BlockSpec(memory_space=pl.ANY)],
        out_specs=[pl.BlockSpec((tb, 512), lambda g, j: (rev(j), g)),
                   pl.BlockSpec((tb, LANES), lambda g, j: (rev(j), g)),
                   pl.BlockSpec((tb, LANES), lambda g, j: (rev(j), g)),
                   pl.BlockSpec((None, tb, LANES), lambda g, j: (g, rev(j), 0)),
                   pl.BlockSpec((tb, 512), lambda g, j: (rev(j), COL_Z // 512 + g)), acc, acc, acc, acc],
        out_shape=[jax.ShapeDtypeStruct((t, B_INNER), F32), jax.ShapeDtypeStruct((t, 512), F32),
                   jax.ShapeDtypeStruct((t, 512), F32), jax.ShapeDtypeStruct((N_GROUPS_B, t, LANES), F32),
                   jax.ShapeDtypeStruct(d_proj.shape, d_proj.dtype)] + [jax.ShapeDtypeStruct((N_GROUPS_B, 1, 512), F32)] * 4,
        input_output_aliases={11: 4},
        scratch_shapes=[pltpu.VMEM((LANES, 512), F32)],
        compiler_params=_params(("arbitrary", "arbitrary")),
    )(xc, xc, xc, proj, proj, states, d_out, dtb, alog, dsk, nw, d_proj)


CONV_HALO = 8


def _shift_down(halo_then_tile, s, tm):
    if s == 0:
        return halo_then_tile[CONV_HALO:CONV_HALO + tm]
    return pltpu.roll(halo_then_tile, s, 0)[CONV_HALO:CONV_HALO + tm]


def _conv_pre(cur, prev, w, b, tm):
    stacked = jnp.concatenate([prev, cur], axis=0)
    taps = [_shift_down(stacked, 3 - j, tm) for j in range(4)]
    pre = b + taps[0] * w[0:1] + taps[1] * w[1:2] + taps[2] * w[2:3] + taps[3] * w[3:4]
    return pre, taps


def _conv_specs(t, tm):
    per = tm // CONV_HALO
    cur = pl.BlockSpec((tm, CONV_DIM), lambda i: (i, COL_XBC // CONV_DIM))
    prev = pl.BlockSpec((CONV_HALO, CONV_DIM), lambda i: (jnp.maximum(i * per - 1, 0), COL_XBC // CONV_DIM))
    return cur, prev


def conv_forward(proj, w, b):
    t = proj.shape[0]
    tm = _pick(t, (256, 128, 64))

    def body(cur_ref, prev_ref, w_ref, b_ref, o_ref):
        prev = jnp.where(pl.program_id(0) == 0, 0.0, prev_ref[...])
        pre, _ = _conv_pre(cur_ref[...], prev, w_ref[...], b_ref[...], tm)
        o_ref[...] = silu(pre)

    cur, prev = _conv_specs(t, tm)
    return pl.pallas_call(
        body, name="conv_forward", grid=(t // tm,),
        in_specs=[cur, prev, pl.BlockSpec((4, CONV_DIM), lambda i: (0, 0)), pl.BlockSpec((1, CONV_DIM), lambda i: (0, 0))],
        out_specs=pl.BlockSpec((tm, CONV_DIM), lambda i: (i, 0)),
        out_shape=jax.ShapeDtypeStruct((t, CONV_DIM), F32),
        compiler_params=_params(("arbitrary",)),
    )(proj, proj, w, b)


def conv_backward_pre(proj, dx, db_, dc_, w, b):
    t = proj.shape[0]
    tm = _pick(t, (256, 128, 64))

    def body(cur_ref, prev_ref, dx_ref, dbm_ref, dcm_ref, w_ref, b_ref, dpre_ref, dw_ref, dbias_ref):
        @pl.when(pl.program_id(0) == 0)
        def _():
            dw_ref[...] = jnp.zeros_like(dw_ref)
            dbias_ref[...] = jnp.zeros_like(dbias_ref)

        prev = jnp.where(pl.program_id(0) == 0, 0.0, prev_ref[...])
        pre, taps = _conv_pre(cur_ref[...], prev, w_ref[...], b_ref[...], tm)
        sg = sigmoid(pre)
        d_out = jnp.concatenate([dx_ref[...], dbm_ref[...], dcm_ref[...]], axis=1)
        dpre = d_out * (sg * (1.0 + pre * (1.0 - sg)))
        dpre_ref[...] = dpre
        dbias_ref[...] += jnp.sum(dpre, axis=0, keepdims=True)
        for j in range(4):
            dw_ref[j:j + 1, :] += jnp.sum(dpre * taps[j], axis=0, keepdims=True)

    cur, prev = _conv_specs(t, tm)
    row = lambda w_: pl.BlockSpec((tm, w_), lambda i: (i, 0))
    return pl.pallas_call(
        body, name="conv_backward_pre", grid=(t // tm,),
        in_specs=[cur, prev, row(B_INNER), row(512), row(512),
                  pl.BlockSpec((4, CONV_DIM), lambda i: (0, 0)), pl.BlockSpec((1, CONV_DIM), lambda i: (0, 0))],
        out_specs=[row(CONV_DIM), pl.BlockSpec((4, CONV_DIM), lambda i: (0, 0)), pl.BlockSpec((1, CONV_DIM), lambda i: (0, 0))],
        out_shape=[jax.ShapeDtypeStruct((t, CONV_DIM), F32), jax.ShapeDtypeStruct((4, CONV_DIM), F32),
                   jax.ShapeDtypeStruct((1, CONV_DIM), F32)],
        compiler_params=_params(("arbitrary",)),
    )(proj, proj, dx, db_, dc_, w, b)


def conv_backward_input(dpre, w, d_proj):
    t = dpre.shape[0]
    tm = _pick(t, (256, 128, 64))
    per = tm // CONV_HALO
    last = t // CONV_HALO - 1
    nt = t // tm

    def body(cur_ref, nxt_ref, w_ref, _, o_ref):
        nxt = jnp.where(pl.program_id(0) == nt - 1, 0.0, nxt_ref[...])
        stacked = jnp.concatenate([cur_ref[...], nxt], axis=0)
        w_ = w_ref[...]
        acc = stacked[0:tm] * w_[3:4]
        for j in range(3):
            s = 3 - j
            acc = acc + pltpu.roll(stacked, tm + CONV_HALO - s, 0)[0:tm] * w_[j:j + 1]
        o_ref[...] = acc.astype(o_ref.dtype)

    return pl.pallas_call(
        body, name="conv_backward_input", grid=(nt,),
        in_specs=[pl.BlockSpec((tm, CONV_DIM), lambda i: (i, 0)),
                  pl.BlockSpec((CONV_HALO, CONV_DIM), lambda i: (jnp.minimum((i + 1) * per, last), 0)),
                  pl.BlockSpec((4, CONV_DIM), lambda i: (0, 0)), pl.BlockSpec(memory_space=pl.ANY)],
        out_specs=pl.BlockSpec((tm, CONV_DIM), lambda i: (i, COL_XBC // CONV_DIM)),
        out_shape=jax.ShapeDtypeStruct(d_proj.shape, d_proj.dtype),
        input_output_aliases={3: 0},
        compiler_params=_params(("arbitrary",)),
    )(dpre, dpre, w, d_proj)


def stage_modulate(x, sc, sh):
    return _ln(x) * (1.0 + sc) + sh


def stage_merge(ga, gb, ya, yb):
    return sigmoid(ga) * ya + sigmoid(gb) * yb


def stage_post_mixer(x, h, g1, ln_g, ln_b, sc2, sh2):
    x1 = _ln(ALPHA * x + g1 * h) * ln_g + ln_b
    return x1, _ln(x1) * (1.0 + sc2) + sh2


def stage_swiglu(a, b):
    return silu(a) * b


def stage_loss(x1, hf, tgt, g2, ln_g, ln_b):
    x2 = _ln(ALPHA * x1 + g2 * hf) * ln_g + ln_b
    return 0.5 * jnp.sum(jnp.mean(jnp.square(x2 - tgt), axis=-1, keepdims=True), axis=0, keepdims=True)


def local_step(x, tgt, mod, wts, small, early=None, late=None):
    sh1, sc1, g1, sh2, sc2, g2 = mod
    w_in, w_a, w_b, w_o, w_gu, w_d = wts
    lb, gn, conv_w, conv_b, dtb, alog, dsk, nw, ln1_g, ln1_b, ln2_g, ln2_b = small
    vec = (1, D)

    (u1,) = rowwise("modulate1", lambda r, c: ((stage_modulate(r[0], *c),), ()), [_full(x)], [sc1, sh1], [(D, BF16)])
    proj = matmul(u1, w_in, "nn", F32, "in_proj")
    ya_in, st_a = hgrn_forward(proj, lb, gn)
    xc = conv_forward(proj, conv_w, conv_b)
    yb_in, st_b = ssd_forward(xc, proj, dtb, alog, dsk, nw)
    ya = matmul(ya_in, w_a, "nn", F32, "branch_a")
    yb = matmul(yb_in, w_b, "nn", F32, "branch_b")
    gate_rows = [(proj, D, COL_GA // D), (proj, D, COL_GB // D), _full(ya), _full(yb)]
    (merged,) = rowwise("merge", lambda r, c: ((stage_merge(*r),), ()), gate_rows, [], [(D, BF16)])
    h = matmul(merged, w_o, "nn", F32, "out_proj")
    post_consts = [g1, ln1_g, ln1_b, sc2, sh2]
    x1, u2 = rowwise("post_mixer", lambda r, c: (stage_post_mixer(*r, *c), ()), [_full(x), _full(h)], post_consts,
                     [(D, F32), (D, BF16)])
    ab = matmul(u2, w_gu, "nn", F32, "ffn_in")
    (p,) = rowwise("swiglu", lambda r, c: ((stage_swiglu(*r),), ()), [(ab, D_FF, 0), (ab, D_FF, 1)], [], [(D_FF, BF16)])
    hf = matmul(p, w_d, "nn", F32, "ffn_out")

    def loss_bwd(r, c):
        loss, vjp = jax.vjp(stage_loss, *r, *c)
        dx1, dhf, _, dg2, dlg, dlb_ = vjp(jnp.ones((1, 1), F32))
        return (dx1, dhf), (loss, dg2, dlg, dlb_)

    dx1, dhf, loss, dg2, dln2_g, dln2_b = rowwise(
        "loss_backward", loss_bwd, [_full(x1), _full(hf), _full(tgt)], [g2, ln2_g, ln2_b],
        [(D, F32), (D, BF16)], [(1, 1), vec, vec, vec])
    dp = matmul(dhf, w_d, "nt", F32, "ffn_out_dx")
    dw_d = matmul(p, dhf, "tn", F32, "ffn_out_dw")

    def swiglu_bwd(r, c):
        _, vjp = jax.vjp(stage_swiglu, r[0], r[1])
        da, db_ = vjp(r[2])
        return (jnp.concatenate([da, db_], axis=1),), ()

    (dab,) = rowwise("swiglu_backward", swiglu_bwd, [(ab, D_FF, 0), (ab, D_FF, 1), _full(dp)], [], [(2 * D_FF, BF16)])
    du2 = matmul(dab, w_gu, "nt", F32, "ffn_in_dx")
    dw_gu = matmul(u2, dab, "tn", F32, "ffn_in_dw")

    def post_bwd(r, c):
        _, vjp = jax.vjp(stage_post_mixer, r[0], r[1], *c)
        dx, dh, *dc = vjp((r[2], r[3]))
        return (dx, dh), tuple(dc)

    dx_a, dh, dg1, dln1_g, dln1_b, dsc2, dsh2 = rowwise(
        "post_mixer_backward", post_bwd, [_full(x), _full(h), _full(dx1), _full(du2)], post_consts,
        [(D, F32), (D, BF16)], [vec] * 5)
    dmerged = matmul(dh, w_o, "nt", F32, "out_proj_dx")
    dw_o = matmul(merged, dh, "tn", F32, "out_proj_dw")

    def merge_bwd(r, c):
        _, vjp = jax.vjp(stage_merge, *r[:4])
        dga, dgb, dya, dyb = vjp(r[4])
        return (jnp.concatenate([dga, dgb], axis=1), dya, dyb), ()

    dproj, dya, dyb = rowwise("merge_backward", merge_bwd, gate_rows + [_full(dmerged)], [],
                              [(2 * D, BF16), (D, BF16), (D, BF16)], new_wide=(IN_PAD, COL_GA // (2 * D)))
    dya_in = matmul(dya, w_a, "nt", F32, "branch_a_dx")
    dw_a = matmul(ya_in, dya, "tn", F32, "branch_a_dw")
    dyb_in = matmul(dyb, w_b, "nt", F32, "branch_b_dx")
    dw_b = matmul(yb_in, dyb, "tn", F32, "branch_b_dw")
    gn_after = gn if early is None else gn + early((dw_a, dw_b, dw_o, dw_gu, dw_d))[0:1]
    dproj, dlb, dgn = hgrn_backward(proj, st_a, dya_in, lb, gn_after, dproj)
    dxs, dbm, dcm, ddt, dproj, ddtb, dalog, ddsk, dnw = ssd_backward(xc, proj, st_b, dyb_in, dtb, alog, dsk, nw, dproj)
    dpre, dconv_w, dconv_b = conv_backward_pre(proj, dxs, dbm, dcm, conv_w, conv_b)
    if late is not None:
        late(dconv_b)
    dproj = conv_backward_input(dpre, conv_w, dproj)
    t = x.shape[0]
    tail = jnp.concatenate([jnp.sum(ddt, axis=0).astype(BF16), jnp.zeros((t, IN_PAD - COL_DT - LANES), BF16)], axis=1)
    dproj = lax.dynamic_update_slice(dproj, tail, (0, COL_DT))
    du1 = matmul(dproj, w_in, "nt", F32, "in_proj_dx")
    dw_in = matmul(u1, dproj, "tn", F32, "in_proj_dw")

    def mod_bwd(r, c):
        _, vjp = jax.vjp(stage_modulate, r[0], *c)
        dx, dsc, dsh = vjp(r[1])
        return (dx + r[2],), (dsc, dsh)

    grad_x, dsc1, dsh1 = rowwise("modulate1_backward", mod_bwd, [_full(x), _full(du1), _full(dx_a)], [sc1, sh1],
                                 [(D, F32)], [vec, vec])
    d_mod = (dsh1, dsc1, dg1, dsh2, dsc2, dg2)
    d_wts = (dw_in, dw_a, dw_b, dw_o, dw_gu, dw_d)
    d_small = (dlb, dgn, dconv_w, dconv_b, ddtb.reshape(1, B_INNER),
               dalog.reshape(1, B_INNER), ddsk.reshape(1, B_INNER), dnw.reshape(1, B_INNER),
               dln1_g, dln1_b, dln2_g, dln2_b)
    return loss, grad_x, d_mod, d_wts, d_small


HBM = pl.BlockSpec(memory_space=pltpu.HBM)


def _place():
    return lax.axis_index("x"), lax.axis_index("y"), lax.axis_index("c")


def _other_chips(x, y):
    return [(1 - x, y), (x, 1 - y), (1 - x, 1 - y)]


def _remote(src, dst, send_sem, recv_sem, device):
    return pltpu.make_async_remote_copy(src_ref=src, dst_ref=dst, send_sem=send_sem, recv_sem=recv_sem,
                                        device_id=device, device_id_type=MESH)


def gather_rows(v, name):
    n = v.shape[1]

    def body(v_ref, out_ref, send_sems, recv_sems, local_sem):
        x, y, c = _place()
        mine = pltpu.make_async_copy(v_ref, out_ref.at[4 * x + 2 * y + c], local_sem)
        mine.start()
        sends, recvs = [], []
        for m in range(1, 8):
            px = 1 - x if m & 4 else x
            py = 1 - y if m & 2 else y
            pc = 1 - c if m & 1 else c
            sends.append(_remote(v_ref, out_ref.at[4 * x + 2 * y + c], send_sems.at[m - 1], recv_sems.at[m - 1], (px, py, pc)))
            recvs.append(_remote(v_ref, out_ref.at[4 * px + 2 * py + pc], send_sems.at[m - 1], recv_sems.at[m - 1], (px, py, pc)))
        for cp in sends:
            cp.start()
        for cp in recvs:
            cp.wait_recv()
        for cp in sends:
            cp.wait_send()
        mine.wait()

    return pl.pallas_call(
        body, name=name, in_specs=[HBM], out_specs=HBM,
        out_shape=jax.ShapeDtypeStruct((8, 1, n), v.dtype),
        scratch_shapes=[pltpu.SemaphoreType.DMA((7,)), pltpu.SemaphoreType.DMA((7,)), pltpu.SemaphoreType.DMA],
    )(v)


def exchange_rows(part, name):
    w = part.shape[2]

    def body(p_ref, out_ref, send_sems, recv_sems, local_sem):
        x, y, c = _place()
        k = 2 * x + y
        mine = pltpu.make_async_copy(p_ref.at[4 * x + 2 * y + c], out_ref.at[k], local_sem)
        mine.start()
        sends, recvs = [], []
        for j, (px, py) in enumerate(_other_chips(x, y)):
            sends.append(_remote(p_ref.at[4 * px + 2 * py + c], out_ref.at[k], send_sems.at[j], recv_sems.at[j], (px, py, c)))
            recvs.append(_remote(p_ref.at[4 * px + 2 * py + c], out_ref.at[2 * px + py], send_sems.at[j], recv_sems.at[j], (px, py, c)))
        for cp in sends:
            cp.start()
        for cp in recvs:
            cp.wait_recv()
        for cp in sends:
            cp.wait_send()
        mine.wait()

    return pl.pallas_call(
        body, name=name, in_specs=[HBM], out_specs=HBM,
        out_shape=jax.ShapeDtypeStruct((4, 1, w), part.dtype),
        scratch_shapes=[pltpu.SemaphoreType.DMA((3,)), pltpu.SemaphoreType.DMA((3,)), pltpu.SemaphoreType.DMA],
    )(part)


def gather_weights(shards):
    n = len(shards)

    def body(*refs):
        w_refs, out_refs = refs[:n], refs[n:2 * n]
        send_sems, recv_sems = refs[2 * n:]
        x, y, c = _place()
        chips = _other_chips(x, y)

        def half(i, px, py, pc):
            hr = shards[i].shape[0] // 2
            return out_refs[i].at[2 * px + py, pl.ds(pc * hr, hr), :]

        first, passed = [], []
        for i in range(n):
            hr = shards[i].shape[0] // 2
            for j, (px, py) in enumerate(chips):
                cp = _remote(w_refs[i].at[pl.ds(c * hr, hr), :], half(i, x, y, c),
                             send_sems.at[j * n + i], recv_sems.at[j * n + i], (px, py, c))
                cp.start()
                first.append(cp)
        for i in range(n):
            for j, (px, py) in enumerate(chips):
                mine_half = half(i, px, py, c)
                _remote(mine_half, mine_half, send_sems.at[j * n + i], recv_sems.at[j * n + i], (px, py, c)).wait_recv()
                cp = _remote(mine_half, mine_half, send_sems.at[(3 + j) * n + i], recv_sems.at[(3 + j) * n + i], (x, y, 1 - c))
                cp.start()
                passed.append(cp)
        for i in range(n):
            for j, (px, py) in enumerate(chips):
                other = half(i, px, py, 1 - c)
                _remote(other, other, send_sems.at[(3 + j) * n + i], recv_sems.at[(3 + j) * n + i], (x, y, 1 - c)).wait_recv()
        for cp in first + passed:
            cp.wait_send()

    return pl.pallas_call(
        body, name="gather_weights", in_specs=[HBM] * n, out_specs=[HBM] * n,
        out_shape=[jax.ShapeDtypeStruct((4,) + s.shape, s.dtype) for s in shards],
        scratch_shapes=[pltpu.SemaphoreType.DMA((6 * n,)), pltpu.SemaphoreType.DMA((6 * n,))],
    )(*shards)


def pair_exchange(slabs, name):
    n = len(slabs)

    def body(*refs):
        g_refs, out_refs = refs[:n], refs[n:2 * n]
        send_sems, recv_sems = refs[2 * n:]
        x, y, c = _place()
        copies = []
        for i in range(n):
            hr = slabs[i].shape[1] // 2
            cp = _remote(g_refs[i].at[:, pl.ds((1 - c) * hr, hr), :], out_refs[i], send_sems.at[i], recv_sems.at[i], (x, y, 1 - c))
            cp.start()
            copies.append(cp)
        for cp in copies:
            cp.wait()

    return pl.pallas_call(
        body, name=name, in_specs=[HBM] * n, out_specs=[HBM] * n,
        out_shape=[jax.ShapeDtypeStruct((4, s.shape[1] // 2, s.shape[2]), s.dtype) for s in slabs],
        scratch_shapes=[pltpu.SemaphoreType.DMA((n,)), pltpu.SemaphoreType.DMA((n,))],
    )(*slabs)


def _row_tile(rows, cols):
    fits = lambda r: r * cols * 4 <= BLOCK_BYTES
    if fits(rows):
        return rows
    return next(r for r in (1024, 512, 256, 128, 64, 32, 16) if rows % r == 0 and fits(r))


def pair_add(g, p, c, name):
    _, hr, cols = p.shape
    tm = _row_tile(hr, cols)
    per = hr // tm

    def body(c_ref, g_ref, p_ref, o_ref):
        o_ref[...] = (g_ref[...] + p_ref[...]).astype(o_ref.dtype)

    return pl.pallas_call(
        body, name=name,
        grid_spec=pltpu.PrefetchScalarGridSpec(
            num_scalar_prefetch=1, grid=(4, per),
            in_specs=[pl.BlockSpec((None, tm, cols), lambda k, i, c_ref: (k, c_ref[0] * per + i, 0)),
                      pl.BlockSpec((None, tm, cols), lambda k, i, c_ref: (k, i, 0))],
            out_specs=pl.BlockSpec((None, tm, cols), lambda k, i, c_ref: (k, i, 0))),
        out_shape=jax.ShapeDtypeStruct((4, hr, cols), BF16),
        compiler_params=_params(("arbitrary", "arbitrary")),
    )(c.reshape(1).astype(jnp.int32), g, p)


def scatter_sums(sums):
    n = len(sums)

    def body(*refs):
        s_refs, out_refs = refs[:n], refs[n:2 * n]
        send_sems, recv_sems = refs[2 * n:]
        x, y, c = _place()
        k = 2 * x + y
        sends, recvs = [], []
        for i in range(n):
            for j, (px, py) in enumerate(_other_chips(x, y)):
                sems = (send_sems.at[j * n + i], recv_sems.at[j * n + i])
                sends.append(_remote(s_refs[i].at[2 * px + py], out_refs[i].at[k], *sems, (px, py, c)))
                recvs.append(_remote(s_refs[i].at[2 * px + py], out_refs[i].at[2 * px + py], *sems, (px, py, c)))
        for cp in sends:
            cp.start()
        for cp in recvs:
            cp.wait_recv()
        for cp in sends:
            cp.wait_send()

    return pl.pallas_call(
        body, name="scatter_sums", in_specs=[HBM] * n, out_specs=[HBM] * n,
        out_shape=[jax.ShapeDtypeStruct(s.shape, s.dtype) for s in sums],
        scratch_shapes=[pltpu.SemaphoreType.DMA((3 * n,)), pltpu.SemaphoreType.DMA((3 * n,))],
    )(*sums)


SEM = pl.BlockSpec(memory_space=pltpu.SEMAPHORE)
DATAFLOW = pltpu.SideEffectType.DATAFLOW_SIDE_EFFECTING


def scatter_start(sums):
    n = len(sums)

    def body(*refs):
        s_refs, land_refs = refs[:n], refs[n:2 * n]
        send_sems, recv_sems = refs[2 * n], refs[2 * n + 1]
        token = refs[-1]
        x, y, c = _place()
        k = 2 * x + y
        for i in range(n):
            for j, (px, py) in enumerate(_other_chips(x, y)):
                _remote(s_refs[i].at[2 * px + py], land_refs[i].at[k], send_sems.at[j * n + i], recv_sems.at[j * n + i],
                        (px, py, c)).start()
        token[...] = jnp.zeros_like(token)

    hbm = lambda a: pltpu.with_memory_space_constraint(a, pltpu.HBM)
    return pl.pallas_call(
        body, name="scatter_start",
        out_shape=(pltpu.SemaphoreType.DMA((3 * n,)), pltpu.SemaphoreType.DMA((3 * n,)),
                   *[pltpu.HBM(s.shape, s.dtype) for s in sums], *[pltpu.HBM(s.shape, s.dtype) for s in sums],
                   jax.ShapeDtypeStruct((8, LANES), F32)),
        in_specs=[HBM] * (2 * n), out_specs=(SEM, SEM, *[HBM] * (2 * n), pl.BlockSpec(memory_space=pltpu.VMEM)),
        input_output_aliases={i: 2 + i for i in range(2 * n)},
        compiler_params=pltpu.CompilerParams(has_side_effects=DATAFLOW),
    )(*[hbm(s) for s in sums], *[hbm(lax.empty(s.shape, s.dtype)) for s in sums])


def scatter_wait(started, after):
    send_sems, recv_sems, *rest = started
    n = (len(rest) - 1) // 2
    sums, lands = rest[:n], rest[n:2 * n]

    def body(*refs):
        s_refs, land_refs = refs[:n], refs[n:2 * n]
        send_ref, recv_ref = refs[2 * n], refs[2 * n + 1]
        x, y, c = _place()
        for i in range(n):
            for j, (px, py) in enumerate(_other_chips(x, y)):
                cp = _remote(s_refs[i].at[2 * px + py], land_refs[i].at[2 * px + py], send_ref.at[j * n + i],
                             recv_ref.at[j * n + i], (px, py, c))
                cp.wait_send()
                cp.wait_recv()

    out = pl.pallas_call(
        body, name="scatter_wait",
        out_shape=tuple(pltpu.HBM(s.shape, s.dtype) for s in sums + lands),
        in_specs=[HBM] * (2 * n) + [SEM, SEM, pl.BlockSpec(memory_space=pl.ANY)], out_specs=tuple([HBM] * (2 * n)),
        input_output_aliases={i: i for i in range(2 * n)},
        compiler_params=pltpu.CompilerParams(has_side_effects=DATAFLOW),
    )(*sums, *lands, send_sems, recv_sems, after)
    return list(out[n:])


def sum_chips(landed, own, chip, core, name):
    _, hr, cols = landed.shape
    tm = _row_tile(hr, 4 * cols)
    per = hr // tm

    def body(idx_ref, l0, l1, l2, l3, own_ref, o_ref):
        mine = own_ref[...].astype(F32)
        v = [jnp.where(idx_ref[0] == k, mine, ref[...].astype(F32)) for k, ref in enumerate((l0, l1, l2, l3))]
        o_ref[...] = ((v[0] + v[1]) + v[2]) + v[3]

    slot = lambda k: pl.BlockSpec((None, tm, cols),
                                  lambda i, idx: (jnp.where(idx[0] == k, (k + 1) & 3, k), i, 0))
    return pl.pallas_call(
        body, name=name,
        grid_spec=pltpu.PrefetchScalarGridSpec(
            num_scalar_prefetch=1, grid=(per,),
            in_specs=[slot(0), slot(1), slot(2), slot(3),
                      pl.BlockSpec((None, tm, cols), lambda i, idx: (idx[0], i, 0))],
            out_specs=pl.BlockSpec((tm, cols), lambda i, idx: (idx[1] * per + i, 0))),
        out_shape=jax.ShapeDtypeStruct((2 * hr, cols), F32),
        compiler_params=_params(("arbitrary",)),
    )(jnp.stack([chip, core]).astype(jnp.int32), landed, landed, landed, landed, own)


def exchange_halves(bufs):
    n = len(bufs)

    def body(*refs):
        out_refs = refs[n:2 * n]
        send_sems, recv_sems = refs[2 * n:]
        x, y, c = _place()
        sends, recvs = [], []
        for i in range(n):
            hr = bufs[i].shape[0] // 2
            own = out_refs[i].at[pl.ds(c * hr, hr), :]
            other = out_refs[i].at[pl.ds((1 - c) * hr, hr), :]
            sends.append(_remote(own, own, send_sems.at[i], recv_sems.at[i], (x, y, 1 - c)))
            recvs.append(_remote(other, other, send_sems.at[i], recv_sems.at[i], (x, y, 1 - c)))
        for cp in sends:
            cp.start()
        for cp in recvs:
            cp.wait_recv()
        for cp in sends:
            cp.wait_send()

    return pl.pallas_call(
        body, name="exchange_halves", in_specs=[HBM] * n, out_specs=[HBM] * n,
        out_shape=[jax.ShapeDtypeStruct(b.shape, b.dtype) for b in bufs],
        input_output_aliases={i: i for i in range(n)},
        scratch_shapes=[pltpu.SemaphoreType.DMA((n,)), pltpu.SemaphoreType.DMA((n,))],
    )(*bufs)


def _relayout(name, arrays, in_blocks, out_blocks, out_shapes, fn):
    rows = 128
    spec = lambda blk: pl.BlockSpec(blk, (lambda i: (0, i, 0)) if len(blk) == 3 else (lambda i: (i, 0)))

    def body(*refs):
        n_in = len(arrays)
        outs = fn(*[r[...] for r in refs[:n_in]])
        for ref, val in zip(refs[n_in:], outs, strict=True):
            if isinstance(val, list):
                for k, piece in enumerate(val):
                    ref[k] = piece
            else:
                ref[...] = val

    return pl.pallas_call(
        body, name=name, grid=(D // rows,),
        in_specs=[spec(b) for b in in_blocks], out_specs=[spec(b) for b in out_blocks], out_shape=out_shapes,
        compiler_params=_params(("arbitrary",)),
    )(*arrays)


def assemble_in_proj(g):
    def fn(v):
        w = jnp.concatenate([v[k] for k in range(4)], axis=1)
        return (jnp.concatenate([w[:, :ORIG_Z], w[:, ORIG_GA:], w[:, ORIG_XBC:ORIG_DT], w[:, ORIG_Z:ORIG_XBC],
                                 w[:, ORIG_DT:ORIG_GA], jnp.zeros((w.shape[0], IN_PAD - IN_ORIG), w.dtype)], axis=1),)

    cols = g.shape[2]
    return _relayout("assemble_in_proj", [g], [(4, 128, cols)], [(128, IN_PAD)],
                     [jax.ShapeDtypeStruct((D, IN_PAD), g.dtype)], fn)[0]


def split_in_proj(dw):
    cols = IN_ORIG // 4

    def fn(d):
        w = jnp.concatenate([d[:, :COL_GA], d[:, COL_Z:COL_DT], d[:, COL_XBC:COL_Z], d[:, COL_DT:COL_DT + 32],
                             d[:, COL_GA:COL_XBC]], axis=1)
        return ([w[:, k * cols:(k + 1) * cols] for k in range(4)],)

    return _relayout("split_in_proj", [dw], [(128, IN_PAD)], [(4, 128, cols)],
                     [jax.ShapeDtypeStruct((4, D, cols), dw.dtype)], fn)[0]


def assemble_ffn_in(gate, up):
    fn = lambda a, b: (jnp.concatenate([a[k] for k in range(4)] + [b[k] for k in range(4)], axis=1),)
    cols = gate.shape[2]
    return _relayout("assemble_ffn_in", [gate, up], [(4, 128, cols)] * 2, [(128, 2 * D_FF)],
                     [jax.ShapeDtypeStruct((D, 2 * D_FF), gate.dtype)], fn)[0]


def split_ffn_in(dw):
    cols = D_FF // 4

    def fn(d):
        return ([d[:, k * cols:(k + 1) * cols] for k in range(4)],
                [d[:, D_FF + k * cols:D_FF + (k + 1) * cols] for k in range(4)])

    shape = jax.ShapeDtypeStruct((4, D, cols), dw.dtype)
    return _relayout("split_ffn_in", [dw], [(128, 2 * D_FF)], [(4, 128, cols)] * 2, [shape, shape], fn)


def ada_prepare(c_all, w_ada, hgrn_lb):
    def body(c_ref, w_ref, lb_ref, mod_ref, row_ref):
        mod_ref[...] = hdot(silu(c_ref[...]), w_ref[...])
        row_ref[...] = sigmoid(lb_ref[0:1, :] - lb_ref[1:2, :])

    return pl.pallas_call(
        body, name="ada_prepare",
        out_shape=[jax.ShapeDtypeStruct((8, w_ada.shape[1]), F32), jax.ShapeDtypeStruct((1, D), F32)],
        compiler_params=pltpu.CompilerParams(vmem_limit_bytes=VMEM_LIMIT),
    )(c_all, w_ada, hgrn_lb)


SMALL_SEGS = (("mod", 6 * D), ("lb", D), ("gnorm", LANES), ("conv_w", 4 * CONV_DIM), ("conv_b", CONV_DIM),
              ("dt_bias", B_INNER), ("a_log", B_INNER), ("d", B_INNER), ("ssm_norm", B_INNER),
              ("ln1_g", D), ("ln1_b", D), ("ln2_g", D), ("ln2_b", D))
SMALL_PARAMS = ("b_ada", "hgrn_lb", "hgrn_gnorm", "ssm_conv_b", "ssm_dt_bias", "ssm_a_log", "ssm_d", "ssm_norm",
                "ln1_g", "ln1_b", "ln2_g", "ln2_b")


def finalize_small(g_all, c_all, dmod_cols, params, m, v):
    n_p = len(SMALL_PARAMS)
    offs, o = {}, 0
    for nm, width in SMALL_SEGS:
        offs[nm] = (o, width)
        o += width

    def body(*refs):
        g_ref, c_ref, dm_ref = refs[:3]
        p_refs = refs[3:3 + n_p]
        m_refs = refs[3 + n_p:3 + 2 * n_p]
        v_refs = refs[3 + 2 * n_p:3 + 3 * n_p]
        outs = refs[3 + 3 * n_p:]
        gwa_ref, gcw_ref = outs[:2]
        res = outs[2:]
        total = jnp.sum(g_ref[...], axis=0, keepdims=True)
        seg = lambda nm: total[:, offs[nm][0]:offs[nm][0] + offs[nm][1]]
        gwa_ref[...] = hdot(silu(c_ref[...]), dm_ref[...], "tn")
        cw = seg("conv_w")
        for j in range(4):
            gcw_ref[j:j + 1, :] = cw[:, j * CONV_DIM:(j + 1) * CONV_DIM]
        hc = lax.broadcasted_iota(jnp.int32, (B_INNER, LANES), 0)
        hj = lax.broadcasted_iota(jnp.int32, (B_INNER, LANES), 1)
        per_head = ((hc >> 6) == hj).astype(F32)
        heads = lambda nm: hdot(jnp.broadcast_to(seg(nm), (8, B_INNER)), per_head)[0:1, 0:32]
        lbp = sigmoid(p_refs[1][0:1, :] - p_refs[1][1:2, :])
        g_row = seg("lb") * lbp * (1.0 - lbp)
        grads = {"b_ada": seg("mod"), "hgrn_gnorm": seg("gnorm"), "ssm_conv_b": seg("conv_b"),
                 "ssm_dt_bias": heads("dt_bias"), "ssm_a_log": heads("a_log"), "ssm_d": heads("d"),
                 "ssm_norm": seg("ssm_norm"), "ln1_g": seg("ln1_g"), "ln1_b": seg("ln1_b"),
                 "ln2_g": seg("ln2_g"), "ln2_b": seg("ln2_b")}
        for i, nm in enumerate(SMALL_PARAMS):
            g_out, d_out, m_out, v_out = res[4 * i:4 * i + 4]
            if nm == "hgrn_lb":
                for row, gv in ((0, g_row), (1, -g_row)):
                    sl = slice(row, row + 1)
                    dl, mn, vn = adamw(p_refs[i][sl, :], gv, m_refs[i][sl, :], v_refs[i][sl, :])
                    g_out[sl, :], d_out[sl, :], m_out[sl, :], v_out[sl, :] = gv, dl, mn, vn
            else:
                gv = grads[nm]
                dl, mn, vn = adamw(p_refs[i][...], gv, m_refs[i][...], v_refs[i][...])
                g_out[...], d_out[...], m_out[...], v_out[...] = gv, dl, mn, vn

    out_shape = [jax.ShapeDtypeStruct((D, dmod_cols.shape[1]), F32), jax.ShapeDtypeStruct((4, CONV_DIM), F32)]
    for p in params:
        out_shape += [jax.ShapeDtypeStruct(p.shape, F32)] * 4
    return pl.pallas_call(
        body, name="finalize_small", out_shape=out_shape,
        compiler_params=pltpu.CompilerParams(vmem_limit_bytes=VMEM_LIMIT),
    )(g_all, c_all, dmod_cols, *params, *m, *v)


def adam_update(w, g, m, v, name):
    cols = w.shape[1]
    return rowwise(name, lambda r, c: (adamw(*r), ()), [_full(w), _full(g), _full(m), _full(v)], [],
                   [(cols, F32)] * 3, tm_max=128)


def kernel(x, c, w_ada, b_ada, w_in, hgrn_lb, hgrn_gnorm, ssm_conv_w, ssm_conv_b, ssm_dt_bias, ssm_a_log, ssm_d, ssm_norm, w_branch_a, w_branch_b, w_o, ln1_g, ln1_b, w_ffn_gate, w_ffn_up, w_ffn_down, ln2_g, ln2_b, loss_target, m_w_ada, m_b_ada, m_w_in, m_hgrn_lb, m_hgrn_gnorm, m_ssm_conv_w, m_ssm_conv_b, m_ssm_dt_bias, m_ssm_a_log, m_ssm_d, m_ssm_norm, m_w_branch_a, m_w_branch_b, m_w_o, m_ln1_g, m_ln1_b, m_w_ffn_gate, m_w_ffn_up, m_w_ffn_down, m_ln2_g, m_ln2_b, v_w_ada, v_b_ada, v_w_in, v_hgrn_lb, v_hgrn_gnorm, v_ssm_conv_w, v_ssm_conv_b, v_ssm_dt_bias, v_ssm_a_log, v_ssm_d, v_ssm_norm, v_w_branch_a, v_w_branch_b, v_w_o, v_ln1_g, v_ln1_b, v_w_ffn_gate, v_w_ffn_up, v_w_ffn_down, v_ln2_g, v_ln2_b):
    given = dict(locals())
    chip = 2 * lax.axis_index("x") + lax.axis_index("y")
    core = lax.axis_index("c")
    t = x.shape[1]

    first = gather_rows(jnp.concatenate([c, ssm_conv_w.reshape(1, CONV_DIM)], axis=1), "gather_cond").reshape(8, D + CONV_DIM)
    c_all = first[:, :D]
    conv_w = first[0::2, D:].reshape(4, 4, CONV_DIM // 4).transpose(1, 0, 2).reshape(4, CONV_DIM)
    mod_part, lb_row = ada_prepare(c_all, w_ada[0], hgrn_lb)
    mod_cols = w_ada.shape[2]
    mod_row = exchange_rows(mod_part.reshape(8, 1, mod_cols), "exchange_mod").reshape(1, 6 * D) + b_ada
    mod = tuple(mod_row[:, i * D:(i + 1) * D] for i in range(6))

    shards = [given[nm][0].astype(BF16) for nm in SHARDED]
    got = {nm: lax.dynamic_update_slice(g, s[None], (chip, 0, 0))
           for nm, g, s in zip(SHARDED, gather_weights(shards), shards, strict=True)}
    whole = lambda nm: got[nm].reshape(4 * got[nm].shape[1], got[nm].shape[2])
    wts = (assemble_in_proj(got["w_in"]), whole("w_branch_a"), whole("w_branch_b"), whole("w_o"),
           assemble_ffn_in(got["w_ffn_gate"], got["w_ffn_up"]), whole("w_ffn_down"))

    per_channel = lambda p: jnp.repeat(p[0], B_INNER // 32)[None]
    small = (lb_row, hgrn_gnorm, conv_w, ssm_conv_b, per_channel(ssm_dt_bias), per_channel(ssm_a_log),
             per_channel(ssm_d), ssm_norm, ln1_g, ln1_b, ln2_g, ln2_b)
    by_rows = lambda g: g.reshape(4, g.shape[0] // 4, g.shape[1])
    travelling = {}

    def pair_sums(names, slabs, tag):
        received = pair_exchange(slabs, "pair_exchange_" + tag)
        return [pair_add(s, r, core, "pair_add_" + nm) for nm, s, r in zip(names, slabs, received, strict=True)]

    def start_early(dws):
        dw_a, dw_b, dw_o, dw_gu, dw_d = dws
        d_gate, d_up = split_ffn_in(dw_gu)
        travelling["pairs"] = pair_sums(SHARDED[1:], [by_rows(dw_a), by_rows(dw_b), by_rows(dw_o), d_gate, d_up, by_rows(dw_d)], "early")
        travelling["started"] = scatter_start(travelling["pairs"])
        return travelling["started"][-1]

    def finish_early(after):
        travelling["landed"] = scatter_wait(travelling["started"], after)

    loss, grad_x, d_mod, d_wts, d_small = local_step(x[0], loss_target[0], mod, wts, small, start_early, finish_early)

    d_lb, d_gn, d_cw, d_cb, d_dtb, d_alog, d_dsk, d_nw, d_l1g, d_l1b, d_l2g, d_l2b = d_small
    row = jnp.concatenate(list(d_mod) + [d_lb, d_gn, d_cw.reshape(1, 4 * CONV_DIM), d_cb, d_dtb, d_alog, d_dsk, d_nw,
                                          d_l1g, d_l1b, d_l2g, d_l2b], axis=1)
    g_all = gather_rows(row, "gather_small_grads").reshape(8, row.shape[1])
    dmod_cols = lax.dynamic_slice_in_dim(g_all, chip * mod_cols, mod_cols, axis=1)
    fin = finalize_small(g_all, c_all, dmod_cols, [given[n] for n in SMALL_PARAMS],
                         [given["m_" + n] for n in SMALL_PARAMS], [given["v_" + n] for n in SMALL_PARAMS])
    grads, deltas, new_m, new_v = {}, {}, {}, {}
    grads["w_ada"] = fin[0][None]
    grads["ssm_conv_w"] = lax.dynamic_slice_in_dim(fin[1], chip * (CONV_DIM // 4), CONV_DIM // 4, axis=1)[None]
    for i, nm in enumerate(SMALL_PARAMS):
        grads[nm], deltas[nm], new_m[nm], new_v[nm] = fin[2 + 4 * i:6 + 4 * i]

    pairs = pair_sums(SHARDED[:1], [split_in_proj(d_wts[0])], "late") + travelling["pairs"]
    landed = list(scatter_sums(pairs[:1])) + travelling["landed"]
    halves = [sum_chips(r, p, chip, core, "sum_chips_" + nm) for nm, r, p in zip(SHARDED, landed, pairs, strict=True)]
    for nm, r in zip(SHARDED, exchange_halves(halves), strict=True):
        grads[nm] = r[None]
    for nm in ("w_ada", "ssm_conv_w") + SHARDED:
        shp = given[nm].shape
        two_d = lambda a: a.reshape(shp[-2], shp[-1])
        d_, m_, v_ = adam_update(two_d(given[nm]), two_d(grads[nm]), two_d(given["m_" + nm]), two_d(given["v_" + nm]),
                                 "adam_" + nm)
        deltas[nm], new_m[nm], new_v[nm] = d_.reshape(shp), m_.reshape(shp), v_.reshape(shp)

    names = ("w_ada", "b_ada", "w_in", "hgrn_lb", "hgrn_gnorm", "ssm_conv_w", "ssm_conv_b", "ssm_dt_bias", "ssm_a_log",
             "ssm_d", "ssm_norm", "w_branch_a", "w_branch_b", "w_o", "ln1_g", "ln1_b", "w_ffn_gate", "w_ffn_up",
             "w_ffn_down", "ln2_g", "ln2_b")
    total_loss = lax.psum(loss[0, 0], ("x", "y", "c"))
    return (total_loss, grad_x[None], *[grads[n] for n in names], *[deltas[n] for n in names],
            *[new_m[n] for n in names], *[new_v[n] for n in names])
```

```python
import functools

import jax
import jax.numpy as jnp
from jax import lax
from jax.experimental import pallas as pl
from jax.experimental.pallas import tpu as pltpu

F32, BF16 = jnp.float32, jnp.bfloat16
HI = lax.Precision.HIGHEST
MESH = pl.DeviceIdType.MESH

D = 1024
CHUNK = 64
LANES = 128
N_HEADS_A = 8
N_GROUPS_B = 4
B_INNER = 2048
CONV_DIM = 3072
D_FF = 2816
ALPHA = 2.0 ** 0.25
LN_EPS = 1e-5
RMS_EPS = 1e-6
ADAM_LR, ADAM_B1, ADAM_B2, ADAM_EPS, ADAM_WD, ADAM_STEP = 0.001, 0.9, 0.999, 1e-08, 0.01, 10

IN_ORIG = 11296
IN_PAD = 11520
COL_GA, COL_GB, COL_XBC, COL_Z, COL_DT = 4096, 5120, 6144, 9216, 11264
ORIG_Z, ORIG_XBC, ORIG_DT, ORIG_GA = 4096, 6144, 9216, 9248

SHARDED = ("w_in", "w_branch_a", "w_branch_b", "w_o", "w_ffn_gate", "w_ffn_up", "w_ffn_down")
VMEM_LIMIT = 56 * 1024 * 1024
BLOCK_BYTES = 2 * 1024 * 1024

_DIMS = {"nn": (((1,), (0,)), ((), ())), "nt": (((1,), (1,)), ((), ())), "tn": (((0,), (0,)), ((), ()))}


def _bd(a, b, mode):
    return lax.dot_general(a.astype(BF16), b.astype(BF16), _DIMS[mode], preferred_element_type=F32)


@functools.partial(jax.custom_vjp, nondiff_argnums=(2,))
def bdot(a, b, mode):
    return _bd(a, b, mode)


def _bdot_fwd(a, b, mode):
    return _bd(a, b, mode), (a, b)


def _bdot_bwd(mode, res, g):
    a, b = res
    if mode == "nn":
        return _bd(g, b, "nt"), _bd(a, g, "tn")
    if mode == "nt":
        return _bd(g, b, "nn"), _bd(g, a, "tn")
    return _bd(b, g, "nt"), _bd(a, g, "nn")


bdot.defvjp(_bdot_fwd, _bdot_bwd)


def hdot(a, b, mode="nn"):
    return lax.dot_general(a, b, _DIMS[mode], precision=HI, preferred_element_type=F32)


def _raw(a, b, mode):
    return lax.dot_general(a, b, _DIMS[mode], preferred_element_type=F32)


def _split(x, n):
    parts, rest = [], x
    for _ in range(n):
        p = rest.astype(BF16)
        parts.append(p)
        rest = rest - p.astype(F32)
    return parts


def _od(a, b, mode, exact):
    if exact == 1:
        e = b.astype(BF16)
        p = _split(a, 3)
        return (_raw(p[2], e, mode) + _raw(p[1], e, mode)) + _raw(p[0], e, mode)
    e = a.astype(BF16)
    p = _split(b, 3)
    return (_raw(e, p[2], mode) + _raw(e, p[1], mode)) + _raw(e, p[0], mode)


@functools.partial(jax.custom_vjp, nondiff_argnums=(2, 3))
def odot(a, b, mode, exact):
    return _od(a, b, mode, exact)


def _odot_fwd(a, b, mode, exact):
    return _od(a, b, mode, exact), (a, b)


def _odot_bwd(mode, exact, res, g):
    a, b = res
    if exact == 1:
        da = {"nn": lambda: _od(g, b, "nt", 1), "nt": lambda: _od(g, b, "nn", 1), "tn": lambda: _od(b, g, "nt", 0)}[mode]()
        return da, jnp.zeros_like(b)
    db = {"nn": lambda: _od(a, g, "tn", 0), "nt": lambda: _od(g, a, "tn", 1), "tn": lambda: _od(a, g, "nn", 0)}[mode]()
    return jnp.zeros_like(a), db


odot.defvjp(_odot_fwd, _odot_bwd)


_BDIMS = {"bnn": (((2,), (1,)), ((0,), (0,))), "bnt": (((2,), (2,)), ((0,), (0,))), "btn": (((1,), (1,)), ((0,), (0,)))}


def _braw(a, b, mode):
    return lax.dot_general(a, b, _BDIMS[mode], preferred_element_type=F32)


def _bdb(a, b, mode):
    return _braw(a.astype(BF16), b.astype(BF16), mode)


def _d3b(a, b, mode):
    ah, al = _split(a, 2)
    bh, bl = _split(b, 2)
    return _braw(ah, bh, mode) + (_braw(ah, bl, mode) + _braw(al, bh, mode))


def _batched_bwd(f):
    def bwd(mode, res, g):
        a, b = res
        if mode == "bnn":
            return f(g, b, "bnt"), f(a, g, "btn")
        if mode == "bnt":
            return f(g, b, "bnn"), f(g, a, "btn")
        return f(b, g, "bnt"), f(a, g, "bnn")
    return bwd


@functools.partial(jax.custom_vjp, nondiff_argnums=(2,))
def bdot_b(a, b, mode):
    return _bdb(a, b, mode)


bdot_b.defvjp(lambda a, b, mode: (_bdb(a, b, mode), (a, b)), _batched_bwd(_bdb))


@functools.partial(jax.custom_vjp, nondiff_argnums=(2,))
def dot3_b(a, b, mode):
    return _d3b(a, b, mode)


dot3_b.defvjp(lambda a, b, mode: (_d3b(a, b, mode), (a, b)), _batched_bwd(_d3b))


def _cum(tril3, x, mode):
    e = tril3.astype(BF16)
    p = _split(x, 3)
    return (_braw(e, p[2], mode) + _braw(e, p[1], mode)) + _braw(e, p[0], mode)


@jax.custom_vjp
def chunk_cumsum(tril3, x):
    return _cum(tril3, x, "bnn")


chunk_cumsum.defvjp(lambda t, x: (_cum(t, x, "bnn"), t), lambda t, g: (jnp.zeros_like(t), _cum(t, g, "btn")))


def _unstack(axis, n):
    @jax.custom_vjp
    def un(x):
        return tuple(lax.index_in_dim(x, i, axis, keepdims=False) for i in range(n))

    un.defvjp(lambda x: (un(x), None), lambda _, g: (jnp.stack(g, axis=axis),))
    return un


def _split_last(n, w):
    @jax.custom_vjp
    def sp(x):
        return tuple(x[..., i * w:(i + 1) * w] for i in range(n))

    sp.defvjp(lambda x: (sp(x), None), lambda _, g: (jnp.concatenate(g, axis=-1),))
    return sp


def sigmoid(x):
    return 1.0 / (1.0 + jnp.exp(-x))


def silu(x):
    return x * sigmoid(x)


def softplus(x):
    return jnp.maximum(x, 0.0) + jnp.log1p(jnp.exp(jnp.minimum(x, -x)))


def _ln(x):
    mu = jnp.mean(x, axis=-1, keepdims=True)
    xc = x - mu
    return xc * lax.rsqrt(jnp.mean(xc * xc, axis=-1, keepdims=True) + LN_EPS)


def _tril64():
    r = lax.broadcasted_iota(jnp.int32, (CHUNK, CHUNK), 0)
    c = lax.broadcasted_iota(jnp.int32, (CHUNK, CHUNK), 1)
    return (r >= c).astype(F32)


def hgrn_block(q, fl, iv, gr, st, lb, gn):
    tb = q.shape[0]
    nc = tb // CHUNK
    nh = N_HEADS_A
    heads = _split_last(nh, LANES)
    to4 = lambda a: jnp.stack(heads(a), axis=0).reshape(nh, nc, CHUNK, LANES)
    flat = lambda a: a.reshape(nh * nc, CHUNK, LANES)
    f = lb + (1.0 - lb) * sigmoid(fl)
    gl4, k4, qf4, v4, gr4 = to4(jnp.log(f)), to4(1.0 - f), to4(silu(q) * (128 ** -0.5)), to4(iv), to4(gr)
    tril = _tril64()
    b4 = chunk_cumsum(jnp.broadcast_to(tril[None], (nh * nc, CHUNK, CHUNK)), flat(gl4)).reshape(gl4.shape)
    blast = jnp.sum(gl4, axis=2, keepdims=True)
    ref = lax.stop_gradient(0.5 * blast)
    sc = dot3_b(flat(qf4 * jnp.exp(b4 - ref)), flat(k4 * jnp.exp(ref - b4)), "bnt") * tril
    o_intra = bdot_b(sc, flat(v4), "bnn").reshape(gl4.shape)
    chunks = _unstack(1, nc)
    qe, v_c, kd, dec = chunks(qf4 * jnp.exp(b4)), chunks(v4), chunks(k4 * jnp.exp(blast - b4)), chunks(jnp.exp(blast))
    o_inter = []
    for c in range(nc):
        o_inter.append(bdot_b(qe[c], st, "bnt"))
        st = st * dec[c] + bdot_b(v_c[c], kd[c], "btn")
    o = o_intra + jnp.stack(o_inter, axis=1)
    on = o * lax.rsqrt(jnp.mean(o * o, axis=-1, keepdims=True) + RMS_EPS) * gn
    out = (on * silu(gr4)).reshape(nh, tb, LANES)
    return jnp.concatenate(_unstack(0, nh)(out), axis=1), st


def ssd_consts(g):
    i32 = jnp.int32
    ej = lax.broadcasted_iota(i32, (LANES, 512), 0)
    ec = lax.broadcasted_iota(i32, (LANES, 512), 1)
    expand = (ej == g * 8 + (ec >> 6)).astype(F32)
    ts = lax.broadcasted_iota(i32, (CHUNK, 512), 0)
    tc = lax.broadcasted_iota(i32, (CHUNK, 512), 1)
    itile = (ts == (tc & 63)).astype(F32)
    maskall = ts >= (tc & 63)
    br = lax.broadcasted_iota(i32, (256, 256), 0)
    bc = lax.broadcasted_iota(i32, (256, 256), 1)
    blockmask = ((br >> 6) == (bc >> 6)).astype(F32)
    return expand, itile, maskall, blockmask, _tril64()


def ssd_block(x, bm, cm, dt, z, st, dtb, alog, dsk, nw, cs):
    expand, itile, maskall, blockmask, tril = cs
    tb = x.shape[0]
    nc = tb // CHUNK
    delta = softplus(odot(dt, expand, "nn", 1) + dtb)
    a = -jnp.exp(alog) * delta
    xdt = x * delta
    by_chunk = lambda v: v.reshape(nc, CHUNK, v.shape[-1])
    a3, xdt3, bm3, cm3 = by_chunk(a), by_chunk(xdt), by_chunk(bm), by_chunk(cm)
    acum3 = chunk_cumsum(jnp.broadcast_to(tril[None], (nc, CHUNK, CHUNK)), a3)
    alast3 = jnp.sum(a3, axis=1, keepdims=True)
    cb3 = bdot_b(cm3, jnp.concatenate([bm3] * 8, axis=1), "bnt")
    arow3 = jnp.sum(acum3 * itile, axis=1, keepdims=True)
    dec3 = jnp.where(maskall, jnp.exp(jnp.minimum(acum3 - arow3, 0.0)), 0.0)
    halves = _split_last(2, 256)
    intra = [bdot_b(m, jnp.concatenate([xh] * 4, axis=1) * blockmask, "bnn")
             for m, xh in zip(halves(cb3 * dec3), halves(xdt3))]
    chunks = _unstack(0, nc)
    cm_c, bm_c, xw_c, dec_c = chunks(cm3), chunks(bm3), chunks(xdt3 * jnp.exp(alast3 - acum3)), chunks(jnp.exp(alast3))
    inter = []
    for c in range(nc):
        inter.append(bdot(cm_c[c], st, "nn"))
        st = st * dec_c[c] + bdot(bm_c[c], xw_c[c], "tn")
    st_new = st
    y = (jnp.concatenate(intra, axis=-1) + jnp.stack(inter, axis=0) * jnp.exp(acum3)).reshape(tb, 512)
    yz = (y + x * dsk) * silu(z)
    return yz * lax.rsqrt(jnp.mean(yz * yz, axis=-1, keepdims=True) + RMS_EPS) * nw, st_new


def adamw(w, g, m, v):
    m = ADAM_B1 * m + (1.0 - ADAM_B1) * g
    v = ADAM_B2 * v + (1.0 - ADAM_B2) * jnp.square(g)
    m_hat = m / (1.0 - ADAM_B1 ** ADAM_STEP)
    v_hat = v / (1.0 - ADAM_B2 ** ADAM_STEP)
    return -ADAM_LR * (m_hat / (jnp.sqrt(v_hat) + ADAM_EPS) + ADAM_WD * w), m, v


def _pick(n, cands):
    for c in cands:
        if n % c == 0:
            return c
    return n


def _params(sem):
    return pltpu.CompilerParams(dimension_semantics=sem, vmem_limit_bytes=VMEM_LIMIT)


def matmul(a, b, mode, out_dtype, name):
    if mode == "nn":
        (m, k), n = a.shape, b.shape[1]
    elif mode == "nt":
        (m, k), n = a.shape, b.shape[0]
    else:
        (k, m), n = a.shape, b.shape[1]
    tm = _pick(m, (1024, 512, 256, 128))
    tn = _pick(n, (1408, 1024, 768, 512, 256, 128))
    tk = _pick(k, (2304, 2048, 1408, 1024, 768, 512, 256, 128))
    nk = k // tk
    a_spec = pl.BlockSpec((tk, tm), lambda i, j, kk: (kk, i)) if mode == "tn" else pl.BlockSpec((tm, tk), lambda i, j, kk: (i, kk))
    b_spec = pl.BlockSpec((tn, tk), lambda i, j, kk: (j, kk)) if mode == "nt" else pl.BlockSpec((tk, tn), lambda i, j, kk: (kk, j))

    def body(a_ref, b_ref, o_ref, *acc):
        part = _bd(a_ref[...], b_ref[...], mode)
        if nk == 1:
            o_ref[...] = part.astype(o_ref.dtype)
            return
        acc_ref, = acc
        kk = pl.program_id(2)

        @pl.when(kk == 0)
        def _():
            acc_ref[...] = part

        @pl.when(jnp.logical_and(kk > 0, kk < nk - 1))
        def _():
            acc_ref[...] += part

        @pl.when(kk == nk - 1)
        def _():
            o_ref[...] = (acc_ref[...] + part).astype(o_ref.dtype)

    return pl.pallas_call(
        body, name=name, grid=(m // tm, n // tn, nk),
        in_specs=[a_spec, b_spec], out_specs=pl.BlockSpec((tm, tn), lambda i, j, kk: (i, j)),
        out_shape=jax.ShapeDtypeStruct((m, n), out_dtype),
        scratch_shapes=[pltpu.VMEM((tm, tn), F32)] if nk > 1 else [],
        compiler_params=_params(("parallel", "parallel", "arbitrary")),
    )(a, b)


def rowwise(name, fn, rows, consts, out_rows, out_accs=(), tm_max=256, into=None, new_wide=None):
    t = rows[0][0].shape[0]
    tm = _pick(t, (tm_max, 128, 64, 32, 16, 8))
    n_r, n_c, n_o = len(rows), len(consts), len(out_rows)
    n_alias = 0 if into is None else 1

    def body(*refs):
        r_in = [r[...] for r in refs[:n_r]]
        c_in = [r[...] for r in refs[n_r:n_r + n_c]]
        refs = refs[:n_r + n_c] + refs[n_r + n_c + n_alias:]
        o_refs = refs[n_r + n_c:n_r + n_c + n_o]
        a_refs = refs[n_r + n_c + n_o:]
        ro, ao = fn(r_in, c_in)
        for ref, val in zip(o_refs, ro, strict=True):
            ref[...] = val.astype(ref.dtype)
        if a_refs:
            @pl.when(pl.program_id(0) == 0)
            def _():
                for ref in a_refs:
                    ref[...] = jnp.zeros_like(ref)

            for ref, val in zip(a_refs, ao, strict=True):
                ref[...] += val

    in_specs = [pl.BlockSpec((tm, w), functools.partial(lambda i, cb: (i, cb), cb=cb)) for _, w, cb in rows]
    in_specs += [pl.BlockSpec(c.shape, lambda i: (0, 0)) for c in consts]
    out_specs = [pl.BlockSpec((tm, w), lambda i: (i, 0)) for w, _ in out_rows]
    out_specs += [pl.BlockSpec(s, lambda i: (0, 0)) for s in out_accs]
    out_shape = [jax.ShapeDtypeStruct((t, w), dt) for w, dt in out_rows]
    out_shape += [jax.ShapeDtypeStruct(s, F32) for s in out_accs]
    operands = [r[0] for r in rows] + list(consts)
    aliases = {}
    if into is not None:
        target, cb = into
        in_specs.append(pl.BlockSpec(memory_space=pl.ANY))
        operands.append(target)
        out_specs[0] = pl.BlockSpec((tm, out_rows[0][0]), lambda i: (i, cb))
        out_shape[0] = jax.ShapeDtypeStruct(target.shape, target.dtype)
        aliases = {len(operands) - 1: 0}
    if new_wide is not None:
        width, cb = new_wide
        out_specs[0] = pl.BlockSpec((tm, out_rows[0][0]), lambda i: (i, cb))
        out_shape[0] = jax.ShapeDtypeStruct((t, width), out_rows[0][1])
    return pl.pallas_call(
        body, name=name, grid=(t // tm,), in_specs=in_specs, out_specs=out_specs, out_shape=out_shape,
        input_output_aliases=aliases, compiler_params=_params(("arbitrary",)),
    )(*operands)


def _full(a):
    return (a, a.shape[1], 0)


def _time_block(t):
    return _pick(t, (256, 128, 64))


def _quarters(ref):
    return [ref[:, seg * D:(seg + 1) * D] for seg in range(4)]


def hgrn_forward(proj, lb, gn):
    t = proj.shape[0]
    tb = _time_block(t)
    nb = t // tb

    def body(qfig_ref, lb_ref, gn_ref, o_ref, st_ref, state):
        @pl.when(pl.program_id(0) == 0)
        def _():
            state[...] = jnp.zeros_like(state)

        st = state[...]
        st_ref[...] = st
        out, st_new = hgrn_block(*_quarters(qfig_ref), st, lb_ref[...], gn_ref[...])
        o_ref[...] = out.astype(o_ref.dtype)
        state[...] = st_new

    return pl.pallas_call(
        body, name="hgrn_forward", grid=(nb,),
        in_specs=[pl.BlockSpec((tb, 4 * D), lambda j: (j, 0)),
                  pl.BlockSpec((1, D), lambda j: (0, 0)), pl.BlockSpec((1, LANES), lambda j: (0, 0))],
        out_specs=[pl.BlockSpec((tb, D), lambda j: (j, 0)),
                   pl.BlockSpec((None, N_HEADS_A, LANES, LANES), lambda j: (j, 0, 0, 0))],
        out_shape=[jax.ShapeDtypeStruct((t, D), BF16),
                   jax.ShapeDtypeStruct((nb, N_HEADS_A, LANES, LANES), F32)],
        scratch_shapes=[pltpu.VMEM((N_HEADS_A, LANES, LANES), F32)],
        compiler_params=_params(("arbitrary",)),
    )(proj, lb, gn)


def hgrn_backward(proj, states, d_out, lb, gn, d_proj):
    t = proj.shape[0]
    tb = _time_block(t)
    nb = t // tb

    def body(qfig_ref, st_ref, do_ref, lb_ref, gn_ref, _, dqfig_ref, dlb_ref, dgn_ref, d_state):
        @pl.when(pl.program_id(0) == 0)
        def _():
            d_state[...] = jnp.zeros_like(d_state)
            dlb_ref[...] = jnp.zeros_like(dlb_ref)
            dgn_ref[...] = jnp.zeros_like(dgn_ref)

        _, vjp = jax.vjp(hgrn_block, *_quarters(qfig_ref), st_ref[...], lb_ref[...], gn_ref[...])
        dq, df, di, dg, dst, dlb, dgn = vjp((do_ref[...], d_state[...]))
        for seg, val in enumerate((dq, df, di, dg)):
            dqfig_ref[:, seg * D:(seg + 1) * D] = val.astype(dqfig_ref.dtype)
        d_state[...] = dst
        dlb_ref[...] += dlb
        dgn_ref[...] += dgn

    rev = lambda j: nb - 1 - j
    return pl.pallas_call(
        body, name="hgrn_backward", grid=(nb,),
        in_specs=[pl.BlockSpec((tb, 4 * D), lambda j: (rev(j), 0)),
                  pl.BlockSpec((None, N_HEADS_A, LANES, LANES), lambda j: (rev(j), 0, 0, 0)),
                  pl.BlockSpec((tb, D), lambda j: (rev(j), 0)),
                  pl.BlockSpec((1, D), lambda j: (0, 0)), pl.BlockSpec((1, LANES), lambda j: (0, 0)),
                  pl.BlockSpec(memory_space=pl.ANY)],
        out_specs=[pl.BlockSpec((tb, 4 * D), lambda j: (rev(j), 0)),
                   pl.BlockSpec((1, D), lambda j: (0, 0)), pl.BlockSpec((1, LANES), lambda j: (0, 0))],
        out_shape=[jax.ShapeDtypeStruct(d_proj.shape, d_proj.dtype), jax.ShapeDtypeStruct((1, D), F32),
                   jax.ShapeDtypeStruct((1, LANES), F32)],
        input_output_aliases={5: 0},
        scratch_shapes=[pltpu.VMEM((N_HEADS_A, LANES, LANES), F32)],
        compiler_params=_params(("arbitrary",)),
    )(proj, states, d_out, lb, gn, d_proj)


def _ssd_in_specs(tb, tmap):
    return [pl.BlockSpec((tb, 512), lambda g, j: (tmap(j), g)),
            pl.BlockSpec((tb, LANES), lambda g, j: (tmap(j), 16 + g)),
            pl.BlockSpec((tb, LANES), lambda g, j: (tmap(j), 20 + g)),
            pl.BlockSpec((tb, LANES), lambda g, j: (tmap(j), COL_DT // LANES)),
            pl.BlockSpec((tb, 512), lambda g, j: (tmap(j), COL_Z // 512 + g))]


def ssd_forward(xc, proj, dtb, alog, dsk, nw):
    t = proj.shape[0]
    tb = _time_block(t)
    nb = t // tb

    def body(x_ref, b_ref, c_ref, dt_ref, z_ref, dtb_ref, alog_ref, dsk_ref, nw_ref, o_ref, st_ref, state):
        @pl.when(pl.program_id(1) == 0)
        def _():
            state[...] = jnp.zeros_like(state)

        st = state[...]
        st_ref[...] = st
        out, st_new = ssd_block(x_ref[...], b_ref[...], c_ref[...], dt_ref[...], z_ref[...], st,
                                dtb_ref[...], alog_ref[...], dsk_ref[...], nw_ref[...], ssd_consts(pl.program_id(0)))
        o_ref[...] = out.astype(o_ref.dtype)
        state[...] = st_new

    vec = pl.BlockSpec((1, 512), lambda g, j: (0, g))
    return pl.pallas_call(
        body, name="ssd_forward", grid=(N_GROUPS_B, nb),
        in_specs=_ssd_in_specs(tb, lambda j: j) + [vec] * 4,
        out_specs=[pl.BlockSpec((tb, 512), lambda g, j: (j, g)),
                   pl.BlockSpec((None, None, LANES, 512), lambda g, j: (j, g, 0, 0))],
        out_shape=[jax.ShapeDtypeStruct((t, B_INNER), BF16),
                   jax.ShapeDtypeStruct((nb, N_GROUPS_B, LANES, 512), F32)],
        scratch_shapes=[pltpu.VMEM((LANES, 512), F32)],
        compiler_params=_params(("arbitrary", "arbitrary")),
    )(xc, xc, xc, proj, proj, dtb, alog, dsk, nw)


def ssd_backward(xc, proj, states, d_out, dtb, alog, dsk, nw, d_proj):
    t = proj.shape[0]
    tb = _time_block(t)
    nb = t // tb
    rev = lambda j: nb - 1 - j

    def body(x_ref, b_ref, c_ref, dt_ref, z_ref, st_ref, do_ref, dtb_ref, alog_ref, dsk_ref, nw_ref, _,
             dx_ref, db_ref, dc_ref, ddt_ref, dz_ref, ddtb_ref, dalog_ref, ddsk_ref, dnw_ref, d_state):
        accs = (ddtb_ref, dalog_ref, ddsk_ref, dnw_ref)

        @pl.when(pl.program_id(1) == 0)
        def _():
            d_state[...] = jnp.zeros_like(d_state)
            for ref in accs:
                ref[...] = jnp.zeros_like(ref)

        cs = ssd_consts(pl.program_id(0))
        fn = lambda *a: ssd_block(*a, cs)
        _, vjp = jax.vjp(fn, x_ref[...], b_ref[...], c_ref[...], dt_ref[...], z_ref[...], st_ref[...],
                         dtb_ref[...], alog_ref[...], dsk_ref[...], nw_ref[...])
        dx, db, dc, ddt, dz, dst, *dpar = vjp((do_ref[...], d_state[...]))
        dx_ref[...] = dx
        db_ref[...] = db
        dc_ref[...] = dc
        ddt_ref[...] = ddt
        dz_ref[...] = dz.astype(dz_ref.dtype)
        d_state[...] = dst
        for ref, val in zip(accs, dpar, strict=True):
            ref[...] += val

    vec = pl.BlockSpec((1, 512), lambda g, j: (0, g))
    acc = pl.BlockSpec((None, 1, 512), lambda g, j: (g, 0, 0))
    return pl.pallas_call(
        body, name="ssd_backward", grid=(N_GROUPS_B, nb),
        in_specs=_ssd_in_specs(tb, rev)
        + [pl.BlockSpec((None, None, LANES, 512), lambda g, j: (rev(j), g, 0, 0)),
           pl.BlockSpec((tb, 512), lambda g, j: (rev(j), g))] + [vec] * 4 + [pl.BlockSpec(memory_space=pl.ANY)],
        out_specs=[pl.BlockSpec((tb, 512), lambda g, j: (rev(j), g)),
                   pl.BlockSpec((tb, LANES), lambda g, j: (rev(j), g)),
                   pl.BlockSpec((tb, LANES), lambda g, j: (rev(j), g)),
                   pl.BlockSpec((None, tb, LANES), lambda g, j: (g, rev(j), 0)),
                   pl.BlockSpec((tb, 512), lambda g, j: (rev(j), COL_Z // 512 + g)), acc, acc, acc, acc],
        out_shape=[jax.ShapeDtypeStruct((t, B_INNER), F32), jax.ShapeDtypeStruct((t, 512), F32),
                   jax.ShapeDtypeStruct((t, 512), F32), jax.ShapeDtypeStruct((N_GROUPS_B, t, LANES), F32),
                   jax.ShapeDtypeStruct(d_proj.shape, d_proj.dtype)] + [jax.ShapeDtypeStruct((N_GROUPS_B, 1, 512), F32)] * 4,
        input_output_aliases={11: 4},
        scratch_shapes=[pltpu.VMEM((LANES, 512), F32)],
        compiler_params=_params(("arbitrary", "arbitrary")),
    )(xc, xc, xc, proj, proj, states, d_out, dtb, alog, dsk, nw, d_proj)


CONV_HALO = 8


def _shift_down(halo_then_tile, s, tm):
    if s == 0:
        return halo_then_tile[CONV_HALO:CONV_HALO + tm]
    return pltpu.roll(halo_then_tile, s, 0)[CONV_HALO:CONV_HALO + tm]


def _conv_pre(cur, prev, w, b, tm):
    stacked = jnp.concatenate([prev, cur], axis=0)
    taps = [_shift_down(stacked, 3 - j, tm) for j in range(4)]
    pre = b + taps[0] * w[0:1] + taps[1] * w[1:2] + taps[2] * w[2:3] + taps[3] * w[3:4]
    return pre, taps


def _conv_specs(t, tm):
    per = tm // CONV_HALO
    cur = pl.BlockSpec((tm, CONV_DIM), lambda i: (i, COL_XBC // CONV_DIM))
    prev = pl.BlockSpec((CONV_HALO, CONV_DIM), lambda i: (jnp.maximum(i * per - 1, 0), COL_XBC // CONV_DIM))
    return cur, prev


def conv_forward(proj, w, b):
    t = proj.shape[0]
    tm = _pick(t, (256, 128, 64))

    def body(cur_ref, prev_ref, w_ref, b_ref, o_ref):
        prev = jnp.where(pl.program_id(0) == 0, 0.0, prev_ref[...])
        pre, _ = _conv_pre(cur_ref[...], prev, w_ref[...], b_ref[...], tm)
        o_ref[...] = silu(pre)

    cur, prev = _conv_specs(t, tm)
    return pl.pallas_call(
        body, name="conv_forward", grid=(t // tm,),
        in_specs=[cur, prev, pl.BlockSpec((4, CONV_DIM), lambda i: (0, 0)), pl.BlockSpec((1, CONV_DIM), lambda i: (0, 0))],
        out_specs=pl.BlockSpec((tm, CONV_DIM), lambda i: (i, 0)),
        out_shape=jax.ShapeDtypeStruct((t, CONV_DIM), F32),
        compiler_params=_params(("arbitrary",)),
    )(proj, proj, w, b)


def conv_backward_pre(proj, dx, db_, dc_, w, b):
    t = proj.shape[0]
    tm = _pick(t, (256, 128, 64))

    def body(cur_ref, prev_ref, dx_ref, dbm_ref, dcm_ref, w_ref, b_ref, dpre_ref, dw_ref, dbias_ref):
        @pl.when(pl.program_id(0) == 0)
        def _():
            dw_ref[...] = jnp.zeros_like(dw_ref)
            dbias_ref[...] = jnp.zeros_like(dbias_ref)

        prev = jnp.where(pl.program_id(0) == 0, 0.0, prev_ref[...])
        pre, taps = _conv_pre(cur_ref[...], prev, w_ref[...], b_ref[...], tm)
        sg = sigmoid(pre)
        d_out = jnp.concatenate([dx_ref[...], dbm_ref[...], dcm_ref[...]], axis=1)
        dpre = d_out * (sg * (1.0 + pre * (1.0 - sg)))
        dpre_ref[...] = dpre
        dbias_ref[...] += jnp.sum(dpre, axis=0, keepdims=True)
        for j in range(4):
            dw_ref[j:j + 1, :] += jnp.sum(dpre * taps[j], axis=0, keepdims=True)

    cur, prev = _conv_specs(t, tm)
    row = lambda w_: pl.BlockSpec((tm, w_), lambda i: (i, 0))
    return pl.pallas_call(
        body, name="conv_backward_pre", grid=(t // tm,),
        in_specs=[cur, prev, row(B_INNER), row(512), row(512),
                  pl.BlockSpec((4, CONV_DIM), lambda i: (0, 0)), pl.BlockSpec((1, CONV_DIM), lambda i: (0, 0))],
        out_specs=[row(CONV_DIM), pl.BlockSpec((4, CONV_DIM), lambda i: (0, 0)), pl.BlockSpec((1, CONV_DIM), lambda i: (0, 0))],
        out_shape=[jax.ShapeDtypeStruct((t, CONV_DIM), F32), jax.ShapeDtypeStruct((4, CONV_DIM), F32),
                   jax.ShapeDtypeStruct((1, CONV_DIM), F32)],
        compiler_params=_params(("arbitrary",)),
    )(proj, proj, dx, db_, dc_, w, b)


def conv_backward_input(dpre, w, d_proj):
    t = dpre.shape[0]
    tm = _pick(t, (256, 128, 64))
    per = tm // CONV_HALO
    last = t // CONV_HALO - 1
    nt = t // tm

    def body(cur_ref, nxt_ref, w_ref, _, o_ref):
        nxt = jnp.where(pl.program_id(0) == nt - 1, 0.0, nxt_ref[...])
        stacked = jnp.concatenate([cur_ref[...], nxt], axis=0)
        w_ = w_ref[...]
        acc = stacked[0:tm] * w_[3:4]
        for j in range(3):
            s = 3 - j
            acc = acc + pltpu.roll(stacked, tm + CONV_HALO - s, 0)[0:tm] * w_[j:j + 1]
        o_ref[...] = acc.astype(o_ref.dtype)

    return pl.pallas_call(
        body, name="conv_backward_input", grid=(nt,),
        in_specs=[pl.BlockSpec((tm, CONV_DIM), lambda i: (i, 0)),
                  pl.BlockSpec((CONV_HALO, CONV_DIM), lambda i: (jnp.minimum((i + 1) * per, last), 0)),
                  pl.BlockSpec((4, CONV_DIM), lambda i: (0, 0)), pl.BlockSpec(memory_space=pl.ANY)],
        out_specs=pl.BlockSpec((tm, CONV_DIM), lambda i: (i, COL_XBC // CONV_DIM)),
        out_shape=jax.ShapeDtypeStruct(d_proj.shape, d_proj.dtype),
        input_output_aliases={3: 0},
        compiler_params=_params(("arbitrary",)),
    )(dpre, dpre, w, d_proj)


def stage_modulate(x, sc, sh):
    return _ln(x) * (1.0 + sc) + sh


def stage_merge(ga, gb, ya, yb):
    return sigmoid(ga) * ya + sigmoid(gb) * yb


def stage_post_mixer(x, h, g1, ln_g, ln_b, sc2, sh2):
    x1 = _ln(ALPHA * x + g1 * h) * ln_g + ln_b
    return x1, _ln(x1) * (1.0 + sc2) + sh2


def stage_swiglu(a, b):
    return silu(a) * b


def stage_loss(x1, hf, tgt, g2, ln_g, ln_b):
    x2 = _ln(ALPHA * x1 + g2 * hf) * ln_g + ln_b
    return 0.5 * jnp.sum(jnp.mean(jnp.square(x2 - tgt), axis=-1, keepdims=True), axis=0, keepdims=True)


def local_step(x, tgt, mod, wts, small, early=None, late=None, last=None):
    sh1, sc1, g1, sh2, sc2, g2 = mod
    w_in, w_a, w_b, w_o, w_gu, w_d = wts
    lb, gn, conv_w, conv_b, dtb, alog, dsk, nw, ln1_g, ln1_b, ln2_g, ln2_b = small
    vec = (1, D)

    (u1,) = rowwise("modulate1", lambda r, c: ((stage_modulate(r[0], *c),), ()), [_full(x)], [sc1, sh1], [(D, BF16)])
    proj = matmul(u1, w_in, "nn", F32, "in_proj")
    ya_in, st_a = hgrn_forward(proj, lb, gn)
    xc = conv_forward(proj, conv_w, conv_b)
    yb_in, st_b = ssd_forward(xc, proj, dtb, alog, dsk, nw)
    ya = matmul(ya_in, w_a, "nn", F32, "branch_a")
    yb = matmul(yb_in, w_b, "nn", F32, "branch_b")
    gate_rows = [(proj, D, COL_GA // D), (proj, D, COL_GB // D), _full(ya), _full(yb)]
    (merged,) = rowwise("merge", lambda r, c: ((stage_merge(*r),), ()), gate_rows, [], [(D, BF16)])
    h = matmul(merged, w_o, "nn", F32, "out_proj")
    post_consts = [g1, ln1_g, ln1_b, sc2, sh2]
    x1, u2 = rowwise("post_mixer", lambda r, c: (stage_post_mixer(*r, *c), ()), [_full(x), _full(h)], post_consts,
                     [(D, F32), (D, BF16)])
    ab = matmul(u2, w_gu, "nn", F32, "ffn_in")
    (p,) = rowwise("swiglu", lambda r, c: ((stage_swiglu(*r),), ()), [(ab, D_FF, 0), (ab, D_FF, 1)], [], [(D_FF, BF16)])
    hf = matmul(p, w_d, "nn", F32, "ffn_out")

    def loss_bwd(r, c):
        loss, vjp = jax.vjp(stage_loss, *r, *c)
        dx1, dhf, _, dg2, dlg, dlb_ = vjp(jnp.ones((1, 1), F32))
        return (dx1, dhf), (loss, dg2, dlg, dlb_)

    dx1, dhf, loss, dg2, dln2_g, dln2_b = rowwise(
        "loss_backward", loss_bwd, [_full(x1), _full(hf), _full(tgt)], [g2, ln2_g, ln2_b],
        [(D, F32), (D, BF16)], [(1, 1), vec, vec, vec])
    dp = matmul(dhf, w_d, "nt", F32, "ffn_out_dx")
    dw_d = matmul(p, dhf, "tn", F32, "ffn_out_dw")

    def swiglu_bwd(r, c):
        _, vjp = jax.vjp(stage_swiglu, r[0], r[1])
        da, db_ = vjp(r[2])
        return (jnp.concatenate([da, db_], axis=1),), ()

    (dab,) = rowwise("swiglu_backward", swiglu_bwd, [(ab, D_FF, 0), (ab, D_FF, 1), _full(dp)], [], [(2 * D_FF, BF16)])
    du2 = matmul(dab, w_gu, "nt", F32, "ffn_in_dx")
    dw_gu = matmul(u2, dab, "tn", F32, "ffn_in_dw")

    def post_bwd(r, c):
        _, vjp = jax.vjp(stage_post_mixer, r[0], r[1], *c)
        dx, dh, *dc = vjp((r[2], r[3]))
        return (dx, dh), tuple(dc)

    dx_a, dh, dg1, dln1_g, dln1_b, dsc2, dsh2 = rowwise(
        "post_mixer_backward", post_bwd, [_full(x), _full(h), _full(dx1), _full(du2)], post_consts,
        [(D, F32), (D, BF16)], [vec] * 5)
    dmerged = matmul(dh, w_o, "nt", F32, "out_proj_dx")
    dw_o = matmul(merged, dh, "tn", F32, "out_proj_dw")

    def merge_bwd(r, c):
        _, vjp = jax.vjp(stage_merge, *r[:4])
        dga, dgb, dya, dyb = vjp(r[4])
        return (jnp.concatenate([dga, dgb], axis=1), dya, dyb), ()

    dproj, dya, dyb = rowwise("merge_backward", merge_bwd, gate_rows + [_full(dmerged)], [],
                              [(2 * D, BF16), (D, BF16), (D, BF16)], new_wide=(IN_PAD, COL_GA // (2 * D)))
    dya_in = matmul(dya, w_a, "nt", F32, "branch_a_dx")
    dw_a = matmul(ya_in, dya, "tn", F32, "branch_a_dw")
    dyb_in = matmul(dyb, w_b, "nt", F32, "branch_b_dx")
    dw_b = matmul(yb_in, dyb, "tn", F32, "branch_b_dw")
    gn_after = gn if early is None else gn + early((dw_a, dw_b, dw_o, dw_gu, dw_d))[0:1]
    dproj, dlb, dgn = hgrn_backward(proj, st_a, dya_in, lb, gn_after, dproj)
    dxs, dbm, dcm, ddt, dproj, ddtb, dalog, ddsk, dnw = ssd_backward(xc, proj, st_b, dyb_in, dtb, alog, dsk, nw, dproj)
    dpre, dconv_w, dconv_b = conv_backward_pre(proj, dxs, dbm, dcm, conv_w, conv_b)
    if late is not None:
        late(dconv_b)
    dproj = conv_backward_input(dpre, conv_w, dproj)
    t = x.shape[0]
    tail = jnp.concatenate([jnp.sum(ddt, axis=0).astype(BF16), jnp.zeros((t, IN_PAD - COL_DT - LANES), BF16)], axis=1)
    dproj = lax.dynamic_update_slice(dproj, tail, (0, COL_DT))
    dw_in = matmul(u1, dproj, "tn", F32, "in_proj_dw")
    sc1_after = sc1 if last is None else sc1 + last(dw_in)[0:1, 0:1]
    du1 = matmul(dproj, w_in, "nt", F32, "in_proj_dx")

    def mod_bwd(r, c):
        _, vjp = jax.vjp(stage_modulate, r[0], *c)
        dx, dsc, dsh = vjp(r[1])
        return (dx + r[2],), (dsc, dsh)

    grad_x, dsc1, dsh1 = rowwise("modulate1_backward", mod_bwd, [_full(x), _full(du1), _full(dx_a)], [sc1_after, sh1],
                                 [(D, F32)], [vec, vec])
    d_mod = (dsh1, dsc1, dg1, dsh2, dsc2, dg2)
    d_wts = (dw_in, dw_a, dw_b, dw_o, dw_gu, dw_d)
    d_small = (dlb, dgn, dconv_w, dconv_b, ddtb.reshape(1, B_INNER),
               dalog.reshape(1, B_INNER), ddsk.reshape(1, B_INNER), dnw.reshape(1, B_INNER),
               dln1_g, dln1_b, dln2_g, dln2_b)
    return loss, grad_x, d_mod, d_wts, d_small


HBM = pl.BlockSpec(memory_space=pltpu.HBM)


def _place():
    return lax.axis_index("x"), lax.axis_index("y"), lax.axis_index("c")


def _other_chips(x, y):
    return [(1 - x, y), (x, 1 - y), (1 - x, 1 - y)]


def _remote(src, dst, send_sem, recv_sem, device):
    return pltpu.make_async_remote_copy(src_ref=src, dst_ref=dst, send_sem=send_sem, recv_sem=recv_sem,
                                        device_id=device, device_id_type=MESH)


def gather_rows(v, name):
    n = v.shape[1]

    def body(v_ref, out_ref, send_sems, recv_sems, local_sem):
        x, y, c = _place()
        mine = pltpu.make_async_copy(v_ref, out_ref.at[4 * x + 2 * y + c], local_sem)
        mine.start()
        sends, recvs = [], []
        for m in range(1, 8):
            px = 1 - x if m & 4 else x
            py = 1 - y if m & 2 else y
            pc = 1 - c if m & 1 else c
            sends.append(_remote(v_ref, out_ref.at[4 * x + 2 * y + c], send_sems.at[m - 1], recv_sems.at[m - 1], (px, py, pc)))
            recvs.append(_remote(v_ref, out_ref.at[4 * px + 2 * py + pc], send_sems.at[m - 1], recv_sems.at[m - 1], (px, py, pc)))
        for cp in sends:
            cp.start()
        for cp in recvs:
            cp.wait_recv()
        for cp in sends:
            cp.wait_send()
        mine.wait()

    return pl.pallas_call(
        body, name=name, in_specs=[HBM], out_specs=HBM,
        out_shape=jax.ShapeDtypeStruct((8, 1, n), v.dtype),
        scratch_shapes=[pltpu.SemaphoreType.DMA((7,)), pltpu.SemaphoreType.DMA((7,)), pltpu.SemaphoreType.DMA],
    )(v)


def exchange_rows(part, name):
    w = part.shape[2]

    def body(p_ref, out_ref, send_sems, recv_sems, local_sem):
        x, y, c = _place()
        k = 2 * x + y
        mine = pltpu.make_async_copy(p_ref.at[4 * x + 2 * y + c], out_ref.at[k], local_sem)
        mine.start()
        sends, recvs = [], []
        for j, (px, py) in enumerate(_other_chips(x, y)):
            sends.append(_remote(p_ref.at[4 * px + 2 * py + c], out_ref.at[k], send_sems.at[j], recv_sems.at[j], (px, py, c)))
            recvs.append(_remote(p_ref.at[4 * px + 2 * py + c], out_ref.at[2 * px + py], send_sems.at[j], recv_sems.at[j], (px, py, c)))
        for cp in sends:
            cp.start()
        for cp in recvs:
            cp.wait_recv()
        for cp in sends:
            cp.wait_send()
        mine.wait()

    return pl.pallas_call(
        body, name=name, in_specs=[HBM], out_specs=HBM,
        out_shape=jax.ShapeDtypeStruct((4, 1, w), part.dtype),
        scratch_shapes=[pltpu.SemaphoreType.DMA((3,)), pltpu.SemaphoreType.DMA((3,)), pltpu.SemaphoreType.DMA],
    )(part)


def gather_weights(shards):
    n = len(shards)

    def body(*refs):
        w_refs, out_refs = refs[:n], refs[n:2 * n]
        send_sems, recv_sems = refs[2 * n:]
        x, y, c = _place()
        chips = _other_chips(x, y)

        def half(i, px, py, pc):
            hr = shards[i].shape[0] // 2
            return out_refs[i].at[2 * px + py, pl.ds(pc * hr, hr), :]

        first, passed = [], []
        for i in range(n):
            hr = shards[i].shape[0] // 2
            for j, (px, py) in enumerate(chips):
                cp = _remote(w_refs[i].at[pl.ds(c * hr, hr), :], half(i, x, y, c),
                             send_sems.at[j * n + i], recv_sems.at[j * n + i], (px, py, c))
                cp.start()
                first.append(cp)
        for i in range(n):
            for j, (px, py) in enumerate(chips):
                mine_half = half(i, px, py, c)
                _remote(mine_half, mine_half, send_sems.at[j * n + i], recv_sems.at[j * n + i], (px, py, c)).wait_recv()
                cp = _remote(mine_half, mine_half, send_sems.at[(3 + j) * n + i], recv_sems.at[(3 + j) * n + i], (x, y, 1 - c))
                cp.start()
                passed.append(cp)
        for i in range(n):
            for j, (px, py) in enumerate(chips):
                other = half(i, px, py, 1 - c)
                _remote(other, other, send_sems.at[(3 + j) * n + i], recv_sems.at[(3 + j) * n + i], (x, y, 1 - c)).wait_recv()
        for cp in first + passed:
            cp.wait_send()

    return pl.pallas_call(
        body, name="gather_weights", in_specs=[HBM] * n, out_specs=[HBM] * n,
        out_shape=[jax.ShapeDtypeStruct((4,) + s.shape, s.dtype) for s in shards],
        scratch_shapes=[pltpu.SemaphoreType.DMA((6 * n,)), pltpu.SemaphoreType.DMA((6 * n,))],
    )(*shards)


def pair_exchange(slabs, name):
    n = len(slabs)

    def body(*refs):
        g_refs, out_refs = refs[:n], refs[n:2 * n]
        send_sems, recv_sems = refs[2 * n:]
        x, y, c = _place()
        copies = []
        for i in range(n):
            hr = slabs[i].shape[1] // 2
            cp = _remote(g_refs[i].at[:, pl.ds((1 - c) * hr, hr), :], out_refs[i], send_sems.at[i], recv_sems.at[i], (x, y, 1 - c))
            cp.start()
            copies.append(cp)
        for cp in copies:
            cp.wait()

    return pl.pallas_call(
        body, name=name, in_specs=[HBM] * n, out_specs=[HBM] * n,
        out_shape=[jax.ShapeDtypeStruct((4, s.shape[1] // 2, s.shape[2]), s.dtype) for s in slabs],
        scratch_shapes=[pltpu.SemaphoreType.DMA((n,)), pltpu.SemaphoreType.DMA((n,))],
    )(*slabs)


def _row_tile(rows, cols):
    fits = lambda r: r * cols * 4 <= BLOCK_BYTES
    if fits(rows):
        return rows
    return next(r for r in (1024, 512, 256, 128, 64, 32, 16) if rows % r == 0 and fits(r))


def pair_add(g, p, c, name):
    _, hr, cols = p.shape
    tm = _row_tile(hr, cols)
    per = hr // tm

    def body(c_ref, g_ref, p_ref, o_ref):
        o_ref[...] = (g_ref[...] + p_ref[...]).astype(o_ref.dtype)

    return pl.pallas_call(
        body, name=name,
        grid_spec=pltpu.PrefetchScalarGridSpec(
            num_scalar_prefetch=1, grid=(4, per),
            in_specs=[pl.BlockSpec((None, tm, cols), lambda k, i, c_ref: (k, c_ref[0] * per + i, 0)),
                      pl.BlockSpec((None, tm, cols), lambda k, i, c_ref: (k, i, 0))],
            out_specs=pl.BlockSpec((None, tm, cols), lambda k, i, c_ref: (k, i, 0))),
        out_shape=jax.ShapeDtypeStruct((4, hr, cols), BF16),
        compiler_params=_params(("arbitrary", "arbitrary")),
    )(c.reshape(1).astype(jnp.int32), g, p)


def scatter_sums(sums):
    n = len(sums)

    def body(*refs):
        s_refs, out_refs = refs[:n], refs[n:2 * n]
        send_sems, recv_sems = refs[2 * n:]
        x, y, c = _place()
        k = 2 * x + y
        sends, recvs = [], []
        for i in range(n):
            for j, (px, py) in enumerate(_other_chips(x, y)):
                sems = (send_sems.at[j * n + i], recv_sems.at[j * n + i])
                sends.append(_remote(s_refs[i].at[2 * px + py], out_refs[i].at[k], *sems, (px, py, c)))
                recvs.append(_remote(s_refs[i].at[2 * px + py], out_refs[i].at[2 * px + py], *sems, (px, py, c)))
        for cp in sends:
            cp.start()
        for cp in recvs:
            cp.wait_recv()
        for cp in sends:
            cp.wait_send()

    return pl.pallas_call(
        body, name="scatter_sums", in_specs=[HBM] * n, out_specs=[HBM] * n,
        out_shape=[jax.ShapeDtypeStruct(s.shape, s.dtype) for s in sums],
        scratch_shapes=[pltpu.SemaphoreType.DMA((3 * n,)), pltpu.SemaphoreType.DMA((3 * n,))],
    )(*sums)


SEM = pl.BlockSpec(memory_space=pltpu.SEMAPHORE)
DATAFLOW = pltpu.SideEffectType.DATAFLOW_SIDE_EFFECTING


def scatter_start(sums, tag):
    n = len(sums)

    def body(*refs):
        s_refs, land_refs = refs[:n], refs[n:2 * n]
        send_sems, recv_sems = refs[2 * n], refs[2 * n + 1]
        token = refs[-1]
        x, y, c = _place()
        k = 2 * x + y
        for i in range(n):
            for j, (px, py) in enumerate(_other_chips(x, y)):
                _remote(s_refs[i].at[2 * px + py], land_refs[i].at[k], send_sems.at[j * n + i], recv_sems.at[j * n + i],
                        (px, py, c)).start()
        token[...] = jnp.zeros_like(token)

    hbm = lambda a: pltpu.with_memory_space_constraint(a, pltpu.HBM)
    return pl.pallas_call(
        body, name="scatter_start_" + tag,
        out_shape=(pltpu.SemaphoreType.DMA((3 * n,)), pltpu.SemaphoreType.DMA((3 * n,)),
                   *[pltpu.HBM(s.shape, s.dtype) for s in sums], *[pltpu.HBM(s.shape, s.dtype) for s in sums],
                   jax.ShapeDtypeStruct((8, LANES), F32)),
        in_specs=[HBM] * (2 * n), out_specs=(SEM, SEM, *[HBM] * (2 * n), pl.BlockSpec(memory_space=pltpu.VMEM)),
        input_output_aliases={i: 2 + i for i in range(2 * n)},
        compiler_params=pltpu.CompilerParams(has_side_effects=DATAFLOW),
    )(*[hbm(s) for s in sums], *[hbm(lax.empty(s.shape, s.dtype)) for s in sums])


def scatter_wait(started, after, tag):
    send_sems, recv_sems, *rest = started
    n = (len(rest) - 1) // 2
    sums, lands = rest[:n], rest[n:2 * n]

    def body(*refs):
        s_refs, land_refs = refs[:n], refs[n:2 * n]
        send_ref, recv_ref = refs[2 * n], refs[2 * n + 1]
        x, y, c = _place()
        for i in range(n):
            for j, (px, py) in enumerate(_other_chips(x, y)):
                cp = _remote(s_refs[i].at[2 * px + py], land_refs[i].at[2 * px + py], send_ref.at[j * n + i],
                             recv_ref.at[j * n + i], (px, py, c))
                cp.wait_send()
                cp.wait_recv()

    out = pl.pallas_call(
        body, name="scatter_wait_" + tag,
        out_shape=tuple(pltpu.HBM(s.shape, s.dtype) for s in sums + lands),
        in_specs=[HBM] * (2 * n) + [SEM, SEM, pl.BlockSpec(memory_space=pl.ANY)], out_specs=tuple([HBM] * (2 * n)),
        input_output_aliases={i: i for i in range(2 * n)},
        compiler_params=pltpu.CompilerParams(has_side_effects=DATAFLOW),
    )(*sums, *lands, send_sems, recv_sems, after)
    return list(out[n:])


def sum_chips(landed, own, chip, core, name):
    _, hr, cols = landed.shape
    tm = _row_tile(hr, 4 * cols)
    per = hr // tm

    def body(idx_ref, l0, l1, l2, l3, own_ref, o_ref):
        mine = own_ref[...].astype(F32)
        v = [jnp.where(idx_ref[0] == k, mine, ref[...].astype(F32)) for k, ref in enumerate((l0, l1, l2, l3))]
        o_ref[...] = ((v[0] + v[1]) + v[2]) + v[3]

    slot = lambda k: pl.BlockSpec((None, tm, cols),
                                  lambda i, idx: (jnp.where(idx[0] == k, (k + 1) & 3, k), i, 0))
    return pl.pallas_call(
        body, name=name,
        grid_spec=pltpu.PrefetchScalarGridSpec(
            num_scalar_prefetch=1, grid=(per,),
            in_specs=[slot(0), slot(1), slot(2), slot(3),
                      pl.BlockSpec((None, tm, cols), lambda i, idx: (idx[0], i, 0))],
            out_specs=pl.BlockSpec((tm, cols), lambda i, idx: (idx[1] * per + i, 0))),
        out_shape=jax.ShapeDtypeStruct((2 * hr, cols), F32),
        compiler_params=_params(("arbitrary",)),
    )(jnp.stack([chip, core]).astype(jnp.int32), landed, landed, landed, landed, own)


def exchange_halves(bufs):
    n = len(bufs)

    def body(*refs):
        out_refs = refs[n:2 * n]
        send_sems, recv_sems = refs[2 * n:]
        x, y, c = _place()
        sends, recvs = [], []
        for i in range(n):
            hr = bufs[i].shape[0] // 2
            own = out_refs[i].at[pl.ds(c * hr, hr), :]
            other = out_refs[i].at[pl.ds((1 - c) * hr, hr), :]
            sends.append(_remote(own, own, send_sems.at[i], recv_sems.at[i], (x, y, 1 - c)))
            recvs.append(_remote(other, other, send_sems.at[i], recv_sems.at[i], (x, y, 1 - c)))
        for cp in sends:
            cp.start()
        for cp in recvs:
            cp.wait_recv()
        for cp in sends:
            cp.wait_send()

    return pl.pallas_call(
        body, name="exchange_halves", in_specs=[HBM] * n, out_specs=[HBM] * n,
        out_shape=[jax.ShapeDtypeStruct(b.shape, b.dtype) for b in bufs],
        input_output_aliases={i: i for i in range(n)},
        scratch_shapes=[pltpu.SemaphoreType.DMA((n,)), pltpu.SemaphoreType.DMA((n,))],
    )(*bufs)


def _relayout(name, arrays, in_blocks, out_blocks, out_shapes, fn):
    rows = 128
    spec = lambda blk: pl.BlockSpec(blk, (lambda i: (0, i, 0)) if len(blk) == 3 else (lambda i: (i, 0)))

    def body(*refs):
        n_in = len(arrays)
        outs = fn(*[r[...] for r in refs[:n_in]])
        for ref, val in zip(refs[n_in:], outs, strict=True):
            if isinstance(val, list):
                for k, piece in enumerate(val):
                    ref[k] = piece
            else:
                ref[...] = val

    return pl.pallas_call(
        body, name=name, grid=(D // rows,),
        in_specs=[spec(b) for b in in_blocks], out_specs=[spec(b) for b in out_blocks], out_shape=out_shapes,
        compiler_params=_params(("arbitrary",)),
    )(*arrays)


def assemble_in_proj(g):
    def fn(v):
        w = jnp.concatenate([v[k] for k in range(4)], axis=1)
        return (jnp.concatenate([w[:, :ORIG_Z], w[:, ORIG_GA:], w[:, ORIG_XBC:ORIG_DT], w[:, ORIG_Z:ORIG_XBC],
                                 w[:, ORIG_DT:ORIG_GA], jnp.zeros((w.shape[0], IN_PAD - IN_ORIG), w.dtype)], axis=1),)

    cols = g.shape[2]
    return _relayout("assemble_in_proj", [g], [(4, 128, cols)], [(128, IN_PAD)],
                     [jax.ShapeDtypeStruct((D, IN_PAD), g.dtype)], fn)[0]


def split_in_proj(dw):
    cols = IN_ORIG // 4

    def fn(d):
        w = jnp.concatenate([d[:, :COL_GA], d[:, COL_Z:COL_DT], d[:, COL_XBC:COL_Z], d[:, COL_DT:COL_DT + 32],
                             d[:, COL_GA:COL_XBC]], axis=1)
        return ([w[:, k * cols:(k + 1) * cols] for k in range(4)],)

    return _relayout("split_in_proj", [dw], [(128, IN_PAD)], [(4, 128, cols)],
                     [jax.ShapeDtypeStruct((4, D, cols), dw.dtype)], fn)[0]


def assemble_ffn_in(gate, up):
    fn = lambda a, b: (jnp.concatenate([a[k] for k in range(4)] + [b[k] for k in range(4)], axis=1),)
    cols = gate.shape[2]
    return _relayout("assemble_ffn_in", [gate, up], [(4, 128, cols)] * 2, [(128, 2 * D_FF)],
                     [jax.ShapeDtypeStruct((D, 2 * D_FF), gate.dtype)], fn)[0]


def split_ffn_in(dw):
    cols = D_FF // 4

    def fn(d):
        return ([d[:, k * cols:(k + 1) * cols] for k in range(4)],
                [d[:, D_FF + k * cols:D_FF + (k + 1) * cols] for k in range(4)])

    shape = jax.ShapeDtypeStruct((4, D, cols), dw.dtype)
    return _relayout("split_ffn_in", [dw], [(128, 2 * D_FF)], [(4, 128, cols)] * 2, [shape, shape], fn)


def ada_prepare(c_all, w_ada, hgrn_lb):
    def body(c_ref, w_ref, lb_ref, mod_ref, row_ref):
        mod_ref[...] = hdot(silu(c_ref[...]), w_ref[...])
        row_ref[...] = sigmoid(lb_ref[0:1, :] - lb_ref[1:2, :])

    return pl.pallas_call(
        body, name="ada_prepare",
        out_shape=[jax.ShapeDtypeStruct((8, w_ada.shape[1]), F32), jax.ShapeDtypeStruct((1, D), F32)],
        compiler_params=pltpu.CompilerParams(vmem_limit_bytes=VMEM_LIMIT),
    )(c_all, w_ada, hgrn_lb)


SMALL_SEGS = (("mod", 6 * D), ("lb", D), ("gnorm", LANES), ("conv_w", 4 * CONV_DIM), ("conv_b", CONV_DIM),
              ("dt_bias", B_INNER), ("a_log", B_INNER), ("d", B_INNER), ("ssm_norm", B_INNER),
              ("ln1_g", D), ("ln1_b", D), ("ln2_g", D), ("ln2_b", D))
SMALL_PARAMS = ("b_ada", "hgrn_lb", "hgrn_gnorm", "ssm_conv_b", "ssm_dt_bias", "ssm_a_log", "ssm_d", "ssm_norm",
                "ln1_g", "ln1_b", "ln2_g", "ln2_b")


def finalize_small(g_all, c_all, dmod_cols, params, m, v):
    n_p = len(SMALL_PARAMS)
    offs, o = {}, 0
    for nm, width in SMALL_SEGS:
        offs[nm] = (o, width)
        o += width

    def body(*refs):
        g_ref, c_ref, dm_ref = refs[:3]
        p_refs = refs[3:3 + n_p]
        m_refs = refs[3 + n_p:3 + 2 * n_p]
        v_refs = refs[3 + 2 * n_p:3 + 3 * n_p]
        outs = refs[3 + 3 * n_p:]
        gwa_ref, gcw_ref = outs[:2]
        res = outs[2:]
        total = jnp.sum(g_ref[...], axis=0, keepdims=True)
        seg = lambda nm: total[:, offs[nm][0]:offs[nm][0] + offs[nm][1]]
        gwa_ref[...] = hdot(silu(c_ref[...]), dm_ref[...], "tn")
        cw = seg("conv_w")
        for j in range(4):
            gcw_ref[j:j + 1, :] = cw[:, j * CONV_DIM:(j + 1) * CONV_DIM]
        hc = lax.broadcasted_iota(jnp.int32, (B_INNER, LANES), 0)
        hj = lax.broadcasted_iota(jnp.int32, (B_INNER, LANES), 1)
        per_head = ((hc >> 6) == hj).astype(F32)
        heads = lambda nm: hdot(jnp.broadcast_to(seg(nm), (8, B_INNER)), per_head)[0:1, 0:32]
        lbp = sigmoid(p_refs[1][0:1, :] - p_refs[1][1:2, :])
        g_row = seg("lb") * lbp * (1.0 - lbp)
        grads = {"b_ada": seg("mod"), "hgrn_gnorm": seg("gnorm"), "ssm_conv_b": seg("conv_b"),
                 "ssm_dt_bias": heads("dt_bias"), "ssm_a_log": heads("a_log"), "ssm_d": heads("d"),
                 "ssm_norm": seg("ssm_norm"), "ln1_g": seg("ln1_g"), "ln1_b": seg("ln1_b"),
                 "ln2_g": seg("ln2_g"), "ln2_b": seg("ln2_b")}
        for i, nm in enumerate(SMALL_PARAMS):
            g_out, d_out, m_out, v_out = res[4 * i:4 * i + 4]
            if nm == "hgrn_lb":
                for row, gv in ((0, g_row), (1, -g_row)):
                    sl = slice(row, row + 1)
                    dl, mn, vn = adamw(p_refs[i][sl, :], gv, m_refs[i][sl, :], v_refs[i][sl, :])
                    g_out[sl, :], d_out[sl, :], m_out[sl, :], v_out[sl, :] = gv, dl, mn, vn
            else:
                gv = grads[nm]
                dl, mn, vn = adamw(p_refs[i][...], gv, m_refs[i][...], v_refs[i][...])
                g_out[...], d_out[...], m_out[...], v_out[...] = gv, dl, mn, vn

    out_shape = [jax.ShapeDtypeStruct((D, dmod_cols.shape[1]), F32), jax.ShapeDtypeStruct((4, CONV_DIM), F32)]
    for p in params:
        out_shape += [jax.ShapeDtypeStruct(p.shape, F32)] * 4
    return pl.pallas_call(
        body, name="finalize_small", out_shape=out_shape,
        compiler_params=pltpu.CompilerParams(vmem_limit_bytes=VMEM_LIMIT),
    )(g_all, c_all, dmod_cols, *params, *m, *v)


def adam_update(w, g, m, v, name):
    cols = w.shape[1]
    return rowwise(name, lambda r, c: (adamw(*r), ()), [_full(w), _full(g), _full(m), _full(v)], [],
                   [(cols, F32)] * 3, tm_max=128)


def kernel(x, c, w_ada, b_ada, w_in, hgrn_lb, hgrn_gnorm, ssm_conv_w, ssm_conv_b, ssm_dt_bias, ssm_a_log, ssm_d, ssm_norm, w_branch_a, w_branch_b, w_o, ln1_g, ln1_b, w_ffn_gate, w_ffn_up, w_ffn_down, ln2_g, ln2_b, loss_target, m_w_ada, m_b_ada, m_w_in, m_hgrn_lb, m_hgrn_gnorm, m_ssm_conv_w, m_ssm_conv_b, m_ssm_dt_bias, m_ssm_a_log, m_ssm_d, m_ssm_norm, m_w_branch_a, m_w_branch_b, m_w_o, m_ln1_g, m_ln1_b, m_w_ffn_gate, m_w_ffn_up, m_w_ffn_down, m_ln2_g, m_ln2_b, v_w_ada, v_b_ada, v_w_in, v_hgrn_lb, v_hgrn_gnorm, v_ssm_conv_w, v_ssm_conv_b, v_ssm_dt_bias, v_ssm_a_log, v_ssm_d, v_ssm_norm, v_w_branch_a, v_w_branch_b, v_w_o, v_ln1_g, v_ln1_b, v_w_ffn_gate, v_w_ffn_up, v_w_ffn_down, v_ln2_g, v_ln2_b):
    given = dict(locals())
    chip = 2 * lax.axis_index("x") + lax.axis_index("y")
    core = lax.axis_index("c")
    t = x.shape[1]

    first = gather_rows(jnp.concatenate([c, ssm_conv_w.reshape(1, CONV_DIM)], axis=1), "gather_cond").reshape(8, D + CONV_DIM)
    c_all = first[:, :D]
    conv_w = first[0::2, D:].reshape(4, 4, CONV_DIM // 4).transpose(1, 0, 2).reshape(4, CONV_DIM)
    mod_part, lb_row = ada_prepare(c_all, w_ada[0], hgrn_lb)
    mod_cols = w_ada.shape[2]
    mod_row = exchange_rows(mod_part.reshape(8, 1, mod_cols), "exchange_mod").reshape(1, 6 * D) + b_ada
    mod = tuple(mod_row[:, i * D:(i + 1) * D] for i in range(6))

    shards = [given[nm][0].astype(BF16) for nm in SHARDED]
    got = {nm: lax.dynamic_update_slice(g, s[None], (chip, 0, 0))
           for nm, g, s in zip(SHARDED, gather_weights(shards), shards, strict=True)}
    whole = lambda nm: got[nm].reshape(4 * got[nm].shape[1], got[nm].shape[2])
    wts = (assemble_in_proj(got["w_in"]), whole("w_branch_a"), whole("w_branch_b"), whole("w_o"),
           assemble_ffn_in(got["w_ffn_gate"], got["w_ffn_up"]), whole("w_ffn_down"))

    per_channel = lambda p: jnp.repeat(p[0], B_INNER // 32)[None]
    small = (lb_row, hgrn_gnorm, conv_w, ssm_conv_b, per_channel(ssm_dt_bias), per_channel(ssm_a_log),
             per_channel(ssm_d), ssm_norm, ln1_g, ln1_b, ln2_g, ln2_b)
    by_rows = lambda g: g.reshape(4, g.shape[0] // 4, g.shape[1])
    travelling = {}

    def pair_sums(names, slabs, tag):
        received = pair_exchange(slabs, "pair_exchange_" + tag)
        return [pair_add(s, r, core, "pair_add_" + nm) for nm, s, r in zip(names, slabs, received, strict=True)]

    def start_early(dws):
        dw_a, dw_b, dw_o, dw_gu, dw_d = dws
        d_gate, d_up = split_ffn_in(dw_gu)
        travelling["pairs"] = pair_sums(SHARDED[1:], [by_rows(dw_a), by_rows(dw_b), by_rows(dw_o), d_gate, d_up, by_rows(dw_d)], "early")
        travelling["started"] = scatter_start(travelling["pairs"], "early")
        return travelling["started"][-1]

    def finish_early(after):
        travelling["landed"] = scatter_wait(travelling["started"], after, "early")

    def start_last(dw_in):
        travelling["pairs_in"] = pair_sums(SHARDED[:1], [split_in_proj(dw_in)], "last")
        travelling["started_in"] = scatter_start(travelling["pairs_in"], "last")
        return travelling["started_in"][-1]

    loss, grad_x, d_mod, d_wts, d_small = local_step(x[0], loss_target[0], mod, wts, small,
                                                     start_early, finish_early, start_last)

    d_lb, d_gn, d_cw, d_cb, d_dtb, d_alog, d_dsk, d_nw, d_l1g, d_l1b, d_l2g, d_l2b = d_small
    row = jnp.concatenate(list(d_mod) + [d_lb, d_gn, d_cw.reshape(1, 4 * CONV_DIM), d_cb, d_dtb, d_alog, d_dsk, d_nw,
                                          d_l1g, d_l1b, d_l2g, d_l2b], axis=1)
    g_all = gather_rows(row, "gather_small_grads").reshape(8, row.shape[1])
    dmod_cols = lax.dynamic_slice_in_dim(g_all, chip * mod_cols, mod_cols, axis=1)
    fin = finalize_small(g_all, c_all, dmod_cols, [given[n] for n in SMALL_PARAMS],
                         [given["m_" + n] for n in SMALL_PARAMS], [given["v_" + n] for n in SMALL_PARAMS])
    grads, deltas, new_m, new_v = {}, {}, {}, {}
    grads["w_ada"] = fin[0][None]
    grads["ssm_conv_w"] = lax.dynamic_slice_in_dim(fin[1], chip * (CONV_DIM // 4), CONV_DIM // 4, axis=1)[None]
    for i, nm in enumerate(SMALL_PARAMS):
        grads[nm], deltas[nm], new_m[nm], new_v[nm] = fin[2 + 4 * i:6 + 4 * i]

    pairs = travelling["pairs_in"] + travelling["pairs"]
    landed = scatter_wait(travelling["started_in"], fin[2], "last") + travelling["landed"]
    halves = [sum_chips(r, p, chip, core, "sum_chips_" + nm) for nm, r, p in zip(SHARDED, landed, pairs, strict=True)]
    for nm, r in zip(SHARDED, exchange_halves(halves), strict=True):
        grads[nm] = r[None]
    for nm in ("w_ada", "ssm_conv_w") + SHARDED:
        shp = given[nm].shape
        two_d = lambda a: a.reshape(shp[-2], shp[-1])
        d_, m_, v_ = adam_update(two_d(given[nm]), two_d(grads[nm]), two_d(given["m_" + nm]), two_d(given["v_" + nm]),
                                 "adam_" + nm)
        deltas[nm], new_m[nm], new_v[nm] = d_.reshape(shp), m_.reshape(shp), v_.reshape(shp)

    names = ("w_ada", "b_ada", "w_in", "hgrn_lb", "hgrn_gnorm", "ssm_conv_w", "ssm_conv_b", "ssm_dt_bias", "ssm_a_log",
             "ssm_d", "ssm_norm", "w_branch_a", "w_branch_b", "w_o", "ln1_g", "ln1_b", "w_ffn_gate", "w_ffn_up",
             "w_ffn_down", "ln2_g", "ln2_b")
    total_loss = lax.psum(loss[0, 0], ("x", "y", "c"))
    return (total_loss, grad_x[None], *[grads[n] for n in names], *[deltas[n] for n in names],
            *[new_m[n] for n in names], *[new_v[n] for n in names])
```

```python
import functools

import jax
import jax.numpy as jnp
from jax import lax
from jax.experimental import pallas as pl
from jax.experimental.pallas import tpu as pltpu

F32, BF16 = jnp.float32, jnp.bfloat16
HI = lax.Precision.HIGHEST
MESH = pl.DeviceIdType.MESH

D = 1024
CHUNK = 64
LANES = 128
N_HEADS_A = 8
N_GROUPS_B = 4
B_INNER = 2048
CONV_DIM = 3072
D_FF = 2816
ALPHA = 2.0 ** 0.25
LN_EPS = 1e-5
RMS_EPS = 1e-6
ADAM_LR, ADAM_B1, ADAM_B2, ADAM_EPS, ADAM_WD, ADAM_STEP = 0.001, 0.9, 0.999, 1e-08, 0.01, 10

IN_ORIG = 11296
IN_PAD = 11520
COL_GA, COL_GB, COL_XBC, COL_Z, COL_DT = 4096, 5120, 6144, 9216, 11264
ORIG_Z, ORIG_XBC, ORIG_DT, ORIG_GA = 4096, 6144, 9216, 9248

SHARDED = ("w_in", "w_branch_a", "w_branch_b", "w_o", "w_ffn_gate", "w_ffn_up", "w_ffn_down")
VMEM_LIMIT = 56 * 1024 * 1024
BLOCK_BYTES = 2 * 1024 * 1024

_DIMS = {"nn": (((1,), (0,)), ((), ())), "nt": (((1,), (1,)), ((), ())), "tn": (((0,), (0,)), ((), ()))}


def _bd(a, b, mode):
    return lax.dot_general(a.astype(BF16), b.astype(BF16), _DIMS[mode], preferred_element_type=F32)


@functools.partial(jax.custom_vjp, nondiff_argnums=(2,))
def bdot(a, b, mode):
    return _bd(a, b, mode)


def _bdot_fwd(a, b, mode):
    return _bd(a, b, mode), (a, b)


def _bdot_bwd(mode, res, g):
    a, b = res
    if mode == "nn":
        return _bd(g, b, "nt"), _bd(a, g, "tn")
    if mode == "nt":
        return _bd(g, b, "nn"), _bd(g, a, "tn")
    return _bd(b, g, "nt"), _bd(a, g, "nn")


bdot.defvjp(_bdot_fwd, _bdot_bwd)


def hdot(a, b, mode="nn"):
    return lax.dot_general(a, b, _DIMS[mode], precision=HI, preferred_element_type=F32)


def _raw(a, b, mode):
    return lax.dot_general(a, b, _DIMS[mode], preferred_element_type=F32)


def _split(x, n):
    parts, rest = [], x
    for _ in range(n):
        p = rest.astype(BF16)
        parts.append(p)
        rest = rest - p.astype(F32)
    return parts


def _od(a, b, mode, exact):
    if exact == 1:
        e = b.astype(BF16)
        p = _split(a, 3)
        return (_raw(p[2], e, mode) + _raw(p[1], e, mode)) + _raw(p[0], e, mode)
    e = a.astype(BF16)
    p = _split(b, 3)
    return (_raw(e, p[2], mode) + _raw(e, p[1], mode)) + _raw(e, p[0], mode)


@functools.partial(jax.custom_vjp, nondiff_argnums=(2, 3))
def odot(a, b, mode, exact):
    return _od(a, b, mode, exact)


def _odot_fwd(a, b, mode, exact):
    return _od(a, b, mode, exact), (a, b)


def _odot_bwd(mode, exact, res, g):
    a, b = res
    if exact == 1:
        da = {"nn": lambda: _od(g, b, "nt", 1), "nt": lambda: _od(g, b, "nn", 1), "tn": lambda: _od(b, g, "nt", 0)}[mode]()
        return da, jnp.zeros_like(b)
    db = {"nn": lambda: _od(a, g, "tn", 0), "nt": lambda: _od(g, a, "tn", 1), "tn": lambda: _od(a, g, "nn", 0)}[mode]()
    return jnp.zeros_like(a), db


odot.defvjp(_odot_fwd, _odot_bwd)


_BDIMS = {"bnn": (((2,), (1,)), ((0,), (0,))), "bnt": (((2,), (2,)), ((0,), (0,))), "btn": (((1,), (1,)), ((0,), (0,)))}


def _braw(a, b, mode):
    return lax.dot_general(a, b, _BDIMS[mode], preferred_element_type=F32)


def _bdb(a, b, mode):
    return _braw(a.astype(BF16), b.astype(BF16), mode)


def _d3b(a, b, mode):
    ah, al = _split(a, 2)
    bh, bl = _split(b, 2)
    return _braw(ah, bh, mode) + (_braw(ah, bl, mode) + _braw(al, bh, mode))


def _batched_bwd(f):
    def bwd(mode, res, g):
        a, b = res
        if mode == "bnn":
            return f(g, b, "bnt"), f(a, g, "btn")
        if mode == "bnt":
            return f(g, b, "bnn"), f(g, a, "btn")
        return f(b, g, "bnt"), f(a, g, "bnn")
    return bwd


@functools.partial(jax.custom_vjp, nondiff_argnums=(2,))
def bdot_b(a, b, mode):
    return _bdb(a, b, mode)


bdot_b.defvjp(lambda a, b, mode: (_bdb(a, b, mode), (a, b)), _batched_bwd(_bdb))


@functools.partial(jax.custom_vjp, nondiff_argnums=(2,))
def dot3_b(a, b, mode):
    return _d3b(a, b, mode)


dot3_b.defvjp(lambda a, b, mode: (_d3b(a, b, mode), (a, b)), _batched_bwd(_d3b))


def _cum(tril3, x, mode):
    e = tril3.astype(BF16)
    p = _split(x, 3)
    return (_braw(e, p[2], mode) + _braw(e, p[1], mode)) + _braw(e, p[0], mode)


@jax.custom_vjp
def chunk_cumsum(tril3, x):
    return _cum(tril3, x, "bnn")


chunk_cumsum.defvjp(lambda t, x: (_cum(t, x, "bnn"), t), lambda t, g: (jnp.zeros_like(t), _cum(t, g, "btn")))


def _unstack(axis, n):
    @jax.custom_vjp
    def un(x):
        return tuple(lax.index_in_dim(x, i, axis, keepdims=False) for i in range(n))

    un.defvjp(lambda x: (un(x), None), lambda _, g: (jnp.stack(g, axis=axis),))
    return un


def _split_last(n, w):
    @jax.custom_vjp
    def sp(x):
        return tuple(x[..., i * w:(i + 1) * w] for i in range(n))

    sp.defvjp(lambda x: (sp(x), None), lambda _, g: (jnp.concatenate(g, axis=-1),))
    return sp


def sigmoid(x):
    return 1.0 / (1.0 + jnp.exp(-x))


def silu(x):
    return x * sigmoid(x)


def softplus(x):
    return jnp.maximum(x, 0.0) + jnp.log1p(jnp.exp(jnp.minimum(x, -x)))


def _ln(x):
    mu = jnp.mean(x, axis=-1, keepdims=True)
    xc = x - mu
    return xc * lax.rsqrt(jnp.mean(xc * xc, axis=-1, keepdims=True) + LN_EPS)


def _tril64():
    r = lax.broadcasted_iota(jnp.int32, (CHUNK, CHUNK), 0)
    c = lax.broadcasted_iota(jnp.int32, (CHUNK, CHUNK), 1)
    return (r >= c).astype(F32)


def hgrn_block(q, fl, iv, gr, st, lb, gn):
    tb = q.shape[0]
    nc = tb // CHUNK
    nh = N_HEADS_A
    heads = _split_last(nh, LANES)
    to4 = lambda a: jnp.stack(heads(a), axis=0).reshape(nh, nc, CHUNK, LANES)
    flat = lambda a: a.reshape(nh * nc, CHUNK, LANES)
    f = lb + (1.0 - lb) * sigmoid(fl)
    gl4, k4, qf4, v4, gr4 = to4(jnp.log(f)), to4(1.0 - f), to4(silu(q) * (128 ** -0.5)), to4(iv), to4(gr)
    tril = _tril64()
    b4 = chunk_cumsum(jnp.broadcast_to(tril[None], (nh * nc, CHUNK, CHUNK)), flat(gl4)).reshape(gl4.shape)
    blast = jnp.sum(gl4, axis=2, keepdims=True)
    ref = lax.stop_gradient(0.5 * blast)
    sc = dot3_b(flat(qf4 * jnp.exp(b4 - ref)), flat(k4 * jnp.exp(ref - b4)), "bnt") * tril
    o_intra = bdot_b(sc, flat(v4), "bnn").reshape(gl4.shape)
    chunks = _unstack(1, nc)
    qe, v_c, kd, dec = chunks(qf4 * jnp.exp(b4)), chunks(v4), chunks(k4 * jnp.exp(blast - b4)), chunks(jnp.exp(blast))
    o_inter = []
    for c in range(nc):
        o_inter.append(bdot_b(qe[c], st, "bnt"))
        st = st * dec[c] + bdot_b(v_c[c], kd[c], "btn")
    o = o_intra + jnp.stack(o_inter, axis=1)
    on = o * lax.rsqrt(jnp.mean(o * o, axis=-1, keepdims=True) + RMS_EPS) * gn
    out = (on * silu(gr4)).reshape(nh, tb, LANES)
    return jnp.concatenate(_unstack(0, nh)(out), axis=1), st


def ssd_consts(g):
    i32 = jnp.int32
    ej = lax.broadcasted_iota(i32, (LANES, 512), 0)
    ec = lax.broadcasted_iota(i32, (LANES, 512), 1)
    expand = (ej == g * 8 + (ec >> 6)).astype(F32)
    ts = lax.broadcasted_iota(i32, (CHUNK, 512), 0)
    tc = lax.broadcasted_iota(i32, (CHUNK, 512), 1)
    itile = (ts == (tc & 63)).astype(F32)
    maskall = ts >= (tc & 63)
    br = lax.broadcasted_iota(i32, (256, 256), 0)
    bc = lax.broadcasted_iota(i32, (256, 256), 1)
    blockmask = ((br >> 6) == (bc >> 6)).astype(F32)
    return expand, itile, maskall, blockmask, _tril64()


def ssd_block(x, bm, cm, dt, z, st, dtb, alog, dsk, nw, cs):
    expand, itile, maskall, blockmask, tril = cs
    tb = x.shape[0]
    nc = tb // CHUNK
    delta = softplus(odot(dt, expand, "nn", 1) + dtb)
    a = -jnp.exp(alog) * delta
    xdt = x * delta
    by_chunk = lambda v: v.reshape(nc, CHUNK, v.shape[-1])
    a3, xdt3, bm3, cm3 = by_chunk(a), by_chunk(xdt), by_chunk(bm), by_chunk(cm)
    acum3 = chunk_cumsum(jnp.broadcast_to(tril[None], (nc, CHUNK, CHUNK)), a3)
    alast3 = jnp.sum(a3, axis=1, keepdims=True)
    cb3 = bdot_b(cm3, jnp.concatenate([bm3] * 8, axis=1), "bnt")
    arow3 = jnp.sum(acum3 * itile, axis=1, keepdims=True)
    dec3 = jnp.where(maskall, jnp.exp(jnp.minimum(acum3 - arow3, 0.0)), 0.0)
    halves = _split_last(2, 256)
    intra = [bdot_b(m, jnp.concatenate([xh] * 4, axis=1) * blockmask, "bnn")
             for m, xh in zip(halves(cb3 * dec3), halves(xdt3))]
    chunks = _unstack(0, nc)
    cm_c, bm_c, xw_c, dec_c = chunks(cm3), chunks(bm3), chunks(xdt3 * jnp.exp(alast3 - acum3)), chunks(jnp.exp(alast3))
    inter = []
    for c in range(nc):
        inter.append(bdot(cm_c[c], st, "nn"))
        st = st * dec_c[c] + bdot(bm_c[c], xw_c[c], "tn")
    st_new = st
    y = (jnp.concatenate(intra, axis=-1) + jnp.stack(inter, axis=0) * jnp.exp(acum3)).reshape(tb, 512)
    yz = (y + x * dsk) * silu(z)
    return yz * lax.rsqrt(jnp.mean(yz * yz, axis=-1, keepdims=True) + RMS_EPS) * nw, st_new


def adamw(w, g, m, v):
    m = ADAM_B1 * m + (1.0 - ADAM_B1) * g
    v = ADAM_B2 * v + (1.0 - ADAM_B2) * jnp.square(g)
    m_hat = m / (1.0 - ADAM_B1 ** ADAM_STEP)
    v_hat = v / (1.0 - ADAM_B2 ** ADAM_STEP)
    return -ADAM_LR * (m_hat / (jnp.sqrt(v_hat) + ADAM_EPS) + ADAM_WD * w), m, v


def _pick(n, cands):
    for c in cands:
        if n % c == 0:
            return c
    return n


def _params(sem):
    return pltpu.CompilerParams(dimension_semantics=sem, vmem_limit_bytes=VMEM_LIMIT)


def matmul(a, b, mode, out_dtype, name, after=None):
    if mode == "nn":
        (m, k), n = a.shape, b.shape[1]
    elif mode == "nt":
        (m, k), n = a.shape, b.shape[0]
    else:
        (k, m), n = a.shape, b.shape[1]
    tm = _pick(m, (1024, 512, 256, 128))
    tn = _pick(n, (1408, 1024, 768, 512, 256, 128))
    tk = _pick(k, (2304, 2048, 1408, 1024, 768, 512, 256, 128))
    nk = k // tk
    a_spec = pl.BlockSpec((tk, tm), lambda i, j, kk: (kk, i)) if mode == "tn" else pl.BlockSpec((tm, tk), lambda i, j, kk: (i, kk))
    b_spec = pl.BlockSpec((tn, tk), lambda i, j, kk: (j, kk)) if mode == "nt" else pl.BlockSpec((tk, tn), lambda i, j, kk: (kk, j))

    order = [] if after is None else [after]

    def body(a_ref, b_ref, *rest):
        o_ref, *acc = rest[len(order):]
        part = _bd(a_ref[...], b_ref[...], mode)
        if nk == 1:
            o_ref[...] = part.astype(o_ref.dtype)
            return
        acc_ref, = acc
        kk = pl.program_id(2)

        @pl.when(kk == 0)
        def _():
            acc_ref[...] = part

        @pl.when(jnp.logical_and(kk > 0, kk < nk - 1))
        def _():
            acc_ref[...] += part

        @pl.when(kk == nk - 1)
        def _():
            o_ref[...] = (acc_ref[...] + part).astype(o_ref.dtype)

    return pl.pallas_call(
        body, name=name, grid=(m // tm, n // tn, nk),
        in_specs=[a_spec, b_spec] + [pl.BlockSpec(memory_space=pl.ANY) for _ in order],
        out_specs=pl.BlockSpec((tm, tn), lambda i, j, kk: (i, j)),
        out_shape=jax.ShapeDtypeStruct((m, n), out_dtype),
        scratch_shapes=[pltpu.VMEM((tm, tn), F32)] if nk > 1 else [],
        compiler_params=_params(("parallel", "parallel", "arbitrary")),
    )(a, b, *order)


def rowwise(name, fn, rows, consts, out_rows, out_accs=(), tm_max=256, into=None, new_wide=None):
    t = rows[0][0].shape[0]
    tm = _pick(t, (tm_max, 128, 64, 32, 16, 8))
    n_r, n_c, n_o = len(rows), len(consts), len(out_rows)
    n_alias = 0 if into is None else 1

    def body(*refs):
        r_in = [r[...] for r in refs[:n_r]]
        c_in = [r[...] for r in refs[n_r:n_r + n_c]]
        refs = refs[:n_r + n_c] + refs[n_r + n_c + n_alias:]
        o_refs = refs[n_r + n_c:n_r + n_c + n_o]
        a_refs = refs[n_r + n_c + n_o:]
        ro, ao = fn(r_in, c_in)
        for ref, val in zip(o_refs, ro, strict=True):
            ref[...] = val.astype(ref.dtype)
        if a_refs:
            @pl.when(pl.program_id(0) == 0)
            def _():
                for ref in a_refs:
                    ref[...] = jnp.zeros_like(ref)

            for ref, val in zip(a_refs, ao, strict=True):
                ref[...] += val

    in_specs = [pl.BlockSpec((tm, w), functools.partial(lambda i, cb: (i, cb), cb=cb)) for _, w, cb in rows]
    in_specs += [pl.BlockSpec(c.shape, lambda i: (0, 0)) for c in consts]
    out_specs = [pl.BlockSpec((tm, w), lambda i: (i, 0)) for w, _ in out_rows]
    out_specs += [pl.BlockSpec(s, lambda i: (0, 0)) for s in out_accs]
    out_shape = [jax.ShapeDtypeStruct((t, w), dt) for w, dt in out_rows]
    out_shape += [jax.ShapeDtypeStruct(s, F32) for s in out_accs]
    operands = [r[0] for r in rows] + list(consts)
    aliases = {}
    if into is not None:
        target, cb = into
        in_specs.append(pl.BlockSpec(memory_space=pl.ANY))
        operands.append(target)
        out_specs[0] = pl.BlockSpec((tm, out_rows[0][0]), lambda i: (i, cb))
        out_shape[0] = jax.ShapeDtypeStruct(target.shape, target.dtype)
        aliases = {len(operands) - 1: 0}
    if new_wide is not None:
        width, cb = new_wide
        out_specs[0] = pl.BlockSpec((tm, out_rows[0][0]), lambda i: (i, cb))
        out_shape[0] = jax.ShapeDtypeStruct((t, width), out_rows[0][1])
    return pl.pallas_call(
        body, name=name, grid=(t // tm,), in_specs=in_specs, out_specs=out_specs, out_shape=out_shape,
        input_output_aliases=aliases, compiler_params=_params(("arbitrary",)),
    )(*operands)


def _full(a):
    return (a, a.shape[1], 0)


def _time_block(t):
    return _pick(t, (256, 128, 64))


def _quarters(ref):
    return [ref[:, seg * D:(seg + 1) * D] for seg in range(4)]


def hgrn_forward(proj, lb, gn):
    t = proj.shape[0]
    tb = _time_block(t)
    nb = t // tb

    def body(qfig_ref, lb_ref, gn_ref, o_ref, st_ref, state):
        @pl.when(pl.program_id(0) == 0)
        def _():
            state[...] = jnp.zeros_like(state)

        st = state[...]
        st_ref[...] = st
        out, st_new = hgrn_block(*_quarters(qfig_ref), st, lb_ref[...], gn_ref[...])
        o_ref[...] = out.astype(o_ref.dtype)
        state[...] = st_new

    return pl.pallas_call(
        body, name="hgrn_forward", grid=(nb,),
        in_specs=[pl.BlockSpec((tb, 4 * D), lambda j: (j, 0)),
                  pl.BlockSpec((1, D), lambda j: (0, 0)), pl.BlockSpec((1, LANES), lambda j: (0, 0))],
        out_specs=[pl.BlockSpec((tb, D), lambda j: (j, 0)),
                   pl.BlockSpec((None, N_HEADS_A, LANES, LANES), lambda j: (j, 0, 0, 0))],
        out_shape=[jax.ShapeDtypeStruct((t, D), BF16),
                   jax.ShapeDtypeStruct((nb, N_HEADS_A, LANES, LANES), F32)],
        scratch_shapes=[pltpu.VMEM((N_HEADS_A, LANES, LANES), F32)],
        compiler_params=_params(("arbitrary",)),
    )(proj, lb, gn)


def hgrn_backward(proj, states, d_out, lb, gn, d_proj):
    t = proj.shape[0]
    tb = _time_block(t)
    nb = t // tb

    def body(qfig_ref, st_ref, do_ref, lb_ref, gn_ref, _, dqfig_ref, dlb_ref, dgn_ref, d_state):
        @pl.when(pl.program_id(0) == 0)
        def _():
            d_state[...] = jnp.zeros_like(d_state)
            dlb_ref[...] = jnp.zeros_like(dlb_ref)
            dgn_ref[...] = jnp.zeros_like(dgn_ref)

        _, vjp = jax.vjp(hgrn_block, *_quarters(qfig_ref), st_ref[...], lb_ref[...], gn_ref[...])
        dq, df, di, dg, dst, dlb, dgn = vjp((do_ref[...], d_state[...]))
        for seg, val in enumerate((dq, df, di, dg)):
            dqfig_ref[:, seg * D:(seg + 1) * D] = val.astype(dqfig_ref.dtype)
        d_state[...] = dst
        dlb_ref[...] += dlb
        dgn_ref[...] += dgn

    rev = lambda j: nb - 1 - j
    return pl.pallas_call(
        body, name="hgrn_backward", grid=(nb,),
        in_specs=[pl.BlockSpec((tb, 4 * D), lambda j: (rev(j), 0)),
                  pl.BlockSpec((None, N_HEADS_A, LANES, LANES), lambda j: (rev(j), 0, 0, 0)),
                  pl.BlockSpec((tb, D), lambda j: (rev(j), 0)),
                  pl.BlockSpec((1, D), lambda j: (0, 0)), pl.BlockSpec((1, LANES), lambda j: (0, 0)),
                  pl.BlockSpec(memory_space=pl.ANY)],
        out_specs=[pl.BlockSpec((tb, 4 * D), lambda j: (rev(j), 0)),
                   pl.BlockSpec((1, D), lambda j: (0, 0)), pl.BlockSpec((1, LANES), lambda j: (0, 0))],
        out_shape=[jax.ShapeDtypeStruct(d_proj.shape, d_proj.dtype), jax.ShapeDtypeStruct((1, D), F32),
                   jax.ShapeDtypeStruct((1, LANES), F32)],
        input_output_aliases={5: 0},
        scratch_shapes=[pltpu.VMEM((N_HEADS_A, LANES, LANES), F32)],
        compiler_params=_params(("arbitrary",)),
    )(proj, states, d_out, lb, gn, d_proj)


def _ssd_in_specs(tb, tmap):
    return [pl.BlockSpec((tb, 512), lambda g, j: (tmap(j), g)),
            pl.BlockSpec((tb, LANES), lambda g, j: (tmap(j), 16 + g)),
            pl.BlockSpec((tb, LANES), lambda g, j: (tmap(j), 20 + g)),
            pl.BlockSpec((tb, LANES), lambda g, j: (tmap(j), COL_DT // LANES)),
            pl.BlockSpec((tb, 512), lambda g, j: (tmap(j), COL_Z // 512 + g))]


def ssd_forward(xc, proj, dtb, alog, dsk, nw):
    t = proj.shape[0]
    tb = _time_block(t)
    nb = t // tb

    def body(x_ref, b_ref, c_ref, dt_ref, z_ref, dtb_ref, alog_ref, dsk_ref, nw_ref, o_ref, st_ref, state):
        @pl.when(pl.program_id(1) == 0)
        def _():
            state[...] = jnp.zeros_like(state)

        st = state[...]
        st_ref[...] = st
        out, st_new = ssd_block(x_ref[...], b_ref[...], c_ref[...], dt_ref[...], z_ref[...], st,
                                dtb_ref[...], alog_ref[...], dsk_ref[...], nw_ref[...], ssd_consts(pl.program_id(0)))
        o_ref[...] = out.astype(o_ref.dtype)
        state[...] = st_new

    vec = pl.BlockSpec((1, 512), lambda g, j: (0, g))
    return pl.pallas_call(
        body, name="ssd_forward", grid=(N_GROUPS_B, nb),
        in_specs=_ssd_in_specs(tb, lambda j: j) + [vec] * 4,
        out_specs=[pl.BlockSpec((tb, 512), lambda g, j: (j, g)),
                   pl.BlockSpec((None, None, LANES, 512), lambda g, j: (j, g, 0, 0))],
        out_shape=[jax.ShapeDtypeStruct((t, B_INNER), BF16),
                   jax.ShapeDtypeStruct((nb, N_GROUPS_B, LANES, 512), F32)],
        scratch_shapes=[pltpu.VMEM((LANES, 512), F32)],
        compiler_params=_params(("arbitrary", "arbitrary")),
    )(xc, xc, xc, proj, proj, dtb, alog, dsk, nw)


def ssd_backward(xc, proj, states, d_out, dtb, alog, dsk, nw, d_proj):
    t = proj.shape[0]
    tb = _time_block(t)
    nb = t // tb
    rev = lambda j: nb - 1 - j

    def body(x_ref, b_ref, c_ref, dt_ref, z_ref, st_ref, do_ref, dtb_ref, alog_ref, dsk_ref, nw_ref, _,
             dx_ref, db_ref, dc_ref, ddt_ref, dz_ref, ddtb_ref, dalog_ref, ddsk_ref, dnw_ref, d_state):
        accs = (ddtb_ref, dalog_ref, ddsk_ref, dnw_ref)

        @pl.when(pl.program_id(1) == 0)
        def _():
            d_state[...] = jnp.zeros_like(d_state)
            for ref in accs:
                ref[...] = jnp.zeros_like(ref)

        cs = ssd_consts(pl.program_id(0))
        fn = lambda *a: ssd_block(*a, cs)
        _, vjp = jax.vjp(fn, x_ref[...], b_ref[...], c_ref[...], dt_ref[...], z_ref[...], st_ref[...],
                         dtb_ref[...], alog_ref[...], dsk_ref[...], nw_ref[...])
        dx, db, dc, ddt, dz, dst, *dpar = vjp((do_ref[...], d_state[...]))
        dx_ref[...] = dx
        db_ref[...] = db
        dc_ref[...] = dc
        ddt_ref[...] = ddt
        dz_ref[...] = dz.astype(dz_ref.dtype)
        d_state[...] = dst
        for ref, val in zip(accs, dpar, strict=True):
            ref[...] += val

    vec = pl.BlockSpec((1, 512), lambda g, j: (0, g))
    acc = pl.BlockSpec((None, 1, 512), lambda g, j: (g, 0, 0))
    return pl.pallas_call(
        body, name="ssd_backward", grid=(N_GROUPS_B, nb),
        in_specs=_ssd_in_specs(tb, rev)
        + [pl.BlockSpec((None, None, LANES, 512), lambda g, j: (rev(j), g, 0, 0)),
           pl.BlockSpec((tb, 512), lambda g, j: (rev(j), g))] + [vec] * 4 + [pl.BlockSpec(memory_space=pl.ANY)],
        out_specs=[pl.BlockSpec((tb, 512), lambda g, j: (rev(j), g)),
                   pl.BlockSpec((tb, LANES), lambda g, j: (rev(j), g)),
                   pl.BlockSpec((tb, LANES), lambda g, j: (rev(j), g)),
                   pl.BlockSpec((None, tb, LANES), lambda g, j: (g, rev(j), 0)),
                   pl.BlockSpec((tb, 512), lambda g, j: (rev(j), COL_Z // 512 + g)), acc, acc, acc, acc],
        out_shape=[jax.ShapeDtypeStruct((t, B_INNER), F32), jax.ShapeDtypeStruct((t, 512), F32),
                   jax.ShapeDtypeStruct((t, 512), F32), jax.ShapeDtypeStruct((N_GROUPS_B, t, LANES), F32),
                   jax.ShapeDtypeStruct(d_proj.shape, d_proj.dtype)] + [jax.ShapeDtypeStruct((N_GROUPS_B, 1, 512), F32)] * 4,
        input_output_aliases={11: 4},
        scratch_shapes=[pltpu.VMEM((LANES, 512), F32)],
        compiler_params=_params(("arbitrary", "arbitrary")),
    )(xc, xc, xc, proj, proj, states, d_out, dtb, alog, dsk, nw, d_proj)


CONV_HALO = 8


def _shift_down(halo_then_tile, s, tm):
    if s == 0:
        return halo_then_tile[CONV_HALO:CONV_HALO + tm]
    return pltpu.roll(halo_then_tile, s, 0)[CONV_HALO:CONV_HALO + tm]


def _conv_pre(cur, prev, w, b, tm):
    stacked = jnp.concatenate([prev, cur], axis=0)
    taps = [_shift_down(stacked, 3 - j, tm) for j in range(4)]
    pre = b + taps[0] * w[0:1] + taps[1] * w[1:2] + taps[2] * w[2:3] + taps[3] * w[3:4]
    return pre, taps


def _conv_specs(t, tm):
    per = tm // CONV_HALO
    cur = pl.BlockSpec((tm, CONV_DIM), lambda i: (i, COL_XBC // CONV_DIM))
    prev = pl.BlockSpec((CONV_HALO, CONV_DIM), lambda i: (jnp.maximum(i * per - 1, 0), COL_XBC // CONV_DIM))
    return cur, prev


def conv_forward(proj, w, b):
    t = proj.shape[0]
    tm = _pick(t, (256, 128, 64))

    def body(cur_ref, prev_ref, w_ref, b_ref, o_ref):
        prev = jnp.where(pl.program_id(0) == 0, 0.0, prev_ref[...])
        pre, _ = _conv_pre(cur_ref[...], prev, w_ref[...], b_ref[...], tm)
        o_ref[...] = silu(pre)

    cur, prev = _conv_specs(t, tm)
    return pl.pallas_call(
        body, name="conv_forward", grid=(t // tm,),
        in_specs=[cur, prev, pl.BlockSpec((4, CONV_DIM), lambda i: (0, 0)), pl.BlockSpec((1, CONV_DIM), lambda i: (0, 0))],
        out_specs=pl.BlockSpec((tm, CONV_DIM), lambda i: (i, 0)),
        out_shape=jax.ShapeDtypeStruct((t, CONV_DIM), F32),
        compiler_params=_params(("arbitrary",)),
    )(proj, proj, w, b)


def conv_backward_pre(proj, dx, db_, dc_, w, b):
    t = proj.shape[0]
    tm = _pick(t, (256, 128, 64))

    def body(cur_ref, prev_ref, dx_ref, dbm_ref, dcm_ref, w_ref, b_ref, dpre_ref, dw_ref, dbias_ref):
        @pl.when(pl.program_id(0) == 0)
        def _():
            dw_ref[...] = jnp.zeros_like(dw_ref)
            dbias_ref[...] = jnp.zeros_like(dbias_ref)

        prev = jnp.where(pl.program_id(0) == 0, 0.0, prev_ref[...])
        pre, taps = _conv_pre(cur_ref[...], prev, w_ref[...], b_ref[...], tm)
        sg = sigmoid(pre)
        d_out = jnp.concatenate([dx_ref[...], dbm_ref[...], dcm_ref[...]], axis=1)
        dpre = d_out * (sg * (1.0 + pre * (1.0 - sg)))
        dpre_ref[...] = dpre
        dbias_ref[...] += jnp.sum(dpre, axis=0, keepdims=True)
        for j in range(4):
            dw_ref[j:j + 1, :] += jnp.sum(dpre * taps[j], axis=0, keepdims=True)

    cur, prev = _conv_specs(t, tm)
    row = lambda w_: pl.BlockSpec((tm, w_), lambda i: (i, 0))
    return pl.pallas_call(
        body, name="conv_backward_pre", grid=(t // tm,),
        in_specs=[cur, prev, row(B_INNER), row(512), row(512),
                  pl.BlockSpec((4, CONV_DIM), lambda i: (0, 0)), pl.BlockSpec((1, CONV_DIM), lambda i: (0, 0))],
        out_specs=[row(CONV_DIM), pl.BlockSpec((4, CONV_DIM), lambda i: (0, 0)), pl.BlockSpec((1, CONV_DIM), lambda i: (0, 0))],
        out_shape=[jax.ShapeDtypeStruct((t, CONV_DIM), F32), jax.ShapeDtypeStruct((4, CONV_DIM), F32),
                   jax.ShapeDtypeStruct((1, CONV_DIM), F32)],
        compiler_params=_params(("arbitrary",)),
    )(proj, proj, dx, db_, dc_, w, b)


def conv_backward_input(dpre, w, d_proj):
    t = dpre.shape[0]
    tm = _pick(t, (256, 128, 64))
    per = tm // CONV_HALO
    last = t // CONV_HALO - 1
    nt = t // tm

    def body(cur_ref, nxt_ref, w_ref, _, o_ref):
        nxt = jnp.where(pl.program_id(0) == nt - 1, 0.0, nxt_ref[...])
        stacked = jnp.concatenate([cur_ref[...], nxt], axis=0)
        w_ = w_ref[...]
        acc = stacked[0:tm] * w_[3:4]
        for j in range(3):
            s = 3 - j
            acc = acc + pltpu.roll(stacked, tm + CONV_HALO - s, 0)[0:tm] * w_[j:j + 1]
        o_ref[...] = acc.astype(o_ref.dtype)

    return pl.pallas_call(
        body, name="conv_backward_input", grid=(nt,),
        in_specs=[pl.BlockSpec((tm, CONV_DIM), lambda i: (i, 0)),
                  pl.BlockSpec((CONV_HALO, CONV_DIM), lambda i: (jnp.minimum((i + 1) * per, last), 0)),
                  pl.BlockSpec((4, CONV_DIM), lambda i: (0, 0)), pl.BlockSpec(memory_space=pl.ANY)],
        out_specs=pl.BlockSpec((tm, CONV_DIM), lambda i: (i, COL_XBC // CONV_DIM)),
        out_shape=jax.ShapeDtypeStruct(d_proj.shape, d_proj.dtype),
        input_output_aliases={3: 0},
        compiler_params=_params(("arbitrary",)),
    )(dpre, dpre, w, d_proj)


def stage_modulate(x, sc, sh):
    return _ln(x) * (1.0 + sc) + sh


def stage_merge(ga, gb, ya, yb):
    return sigmoid(ga) * ya + sigmoid(gb) * yb


def stage_post_mixer(x, h, g1, ln_g, ln_b, sc2, sh2):
    x1 = _ln(ALPHA * x + g1 * h) * ln_g + ln_b
    return x1, _ln(x1) * (1.0 + sc2) + sh2


def stage_swiglu(a, b):
    return silu(a) * b


def stage_loss(x1, hf, tgt, g2, ln_g, ln_b):
    x2 = _ln(ALPHA * x1 + g2 * hf) * ln_g + ln_b
    return 0.5 * jnp.sum(jnp.mean(jnp.square(x2 - tgt), axis=-1, keepdims=True), axis=0, keepdims=True)


def local_step(x, tgt, mod, wts, small, early=None, late=None, last=None):
    sh1, sc1, g1, sh2, sc2, g2 = mod
    lb, gn, conv_w, conv_b, dtb, alog, dsk, nw, ln1_g, ln1_b, ln2_g, ln2_b = small
    vec = (1, D)

    (u1,) = rowwise("modulate1", lambda r, c: ((stage_modulate(r[0], *c),), ()), [_full(x)], [sc1, sh1], [(D, BF16)])
    w_in = wts.input_projection(u1)
    proj = matmul(u1, w_in, "nn", F32, "in_proj")
    ya_in, st_a = hgrn_forward(proj, lb, gn + wts.start_rest(proj)[0:1])
    xc = conv_forward(proj, conv_w, conv_b)
    w_a, w_b, w_o, w_gu, w_d = wts.rest(xc)
    yb_in, st_b = ssd_forward(xc, proj, dtb, alog, dsk, nw)
    ya = matmul(ya_in, w_a, "nn", F32, "branch_a")
    yb = matmul(yb_in, w_b, "nn", F32, "branch_b")
    gate_rows = [(proj, D, COL_GA // D), (proj, D, COL_GB // D), _full(ya), _full(yb)]
    (merged,) = rowwise("merge", lambda r, c: ((stage_merge(*r),), ()), gate_rows, [], [(D, BF16)])
    h = matmul(merged, w_o, "nn", F32, "out_proj")
    post_consts = [g1, ln1_g, ln1_b, sc2, sh2]
    x1, u2 = rowwise("post_mixer", lambda r, c: (stage_post_mixer(*r, *c), ()), [_full(x), _full(h)], post_consts,
                     [(D, F32), (D, BF16)])
    ab = matmul(u2, w_gu, "nn", F32, "ffn_in")
    (p,) = rowwise("swiglu", lambda r, c: ((stage_swiglu(*r),), ()), [(ab, D_FF, 0), (ab, D_FF, 1)], [], [(D_FF, BF16)])
    hf = matmul(p, w_d, "nn", F32, "ffn_out")

    def loss_bwd(r, c):
        loss, vjp = jax.vjp(stage_loss, *r, *c)
        dx1, dhf, _, dg2, dlg, dlb_ = vjp(jnp.ones((1, 1), F32))
        return (dx1, dhf), (loss, dg2, dlg, dlb_)

    dx1, dhf, loss, dg2, dln2_g, dln2_b = rowwise(
        "loss_backward", loss_bwd, [_full(x1), _full(hf), _full(tgt)], [g2, ln2_g, ln2_b],
        [(D, F32), (D, BF16)], [(1, 1), vec, vec, vec])
    dp = matmul(dhf, w_d, "nt", F32, "ffn_out_dx")
    dw_d = matmul(p, dhf, "tn", F32, "ffn_out_dw")

    def swiglu_bwd(r, c):
        _, vjp = jax.vjp(stage_swiglu, r[0], r[1])
        da, db_ = vjp(r[2])
        return (jnp.concatenate([da, db_], axis=1),), ()

    (dab,) = rowwise("swiglu_backward", swiglu_bwd, [(ab, D_FF, 0), (ab, D_FF, 1), _full(dp)], [], [(2 * D_FF, BF16)])
    du2 = matmul(dab, w_gu, "nt", F32, "ffn_in_dx")
    dw_gu = matmul(u2, dab, "tn", F32, "ffn_in_dw")

    def post_bwd(r, c):
        _, vjp = jax.vjp(stage_post_mixer, r[0], r[1], *c)
        dx, dh, *dc = vjp((r[2], r[3]))
        return (dx, dh), tuple(dc)

    dx_a, dh, dg1, dln1_g, dln1_b, dsc2, dsh2 = rowwise(
        "post_mixer_backward", post_bwd, [_full(x), _full(h), _full(dx1), _full(du2)], post_consts,
        [(D, F32), (D, BF16)], [vec] * 5)
    dmerged = matmul(dh, w_o, "nt", F32, "out_proj_dx")
    dw_o = matmul(merged, dh, "tn", F32, "out_proj_dw")

    def merge_bwd(r, c):
        _, vjp = jax.vjp(stage_merge, *r[:4])
        dga, dgb, dya, dyb = vjp(r[4])
        return (jnp.concatenate([dga, dgb], axis=1), dya, dyb), ()

    dproj, dya, dyb = rowwise("merge_backward", merge_bwd, gate_rows + [_full(dmerged)], [],
                              [(2 * D, BF16), (D, BF16), (D, BF16)], new_wide=(IN_PAD, COL_GA // (2 * D)))
    dya_in = matmul(dya, w_a, "nt", F32, "branch_a_dx")
    dw_a = matmul(ya_in, dya, "tn", F32, "branch_a_dw")
    dyb_in = matmul(dyb, w_b, "nt", F32, "branch_b_dx")
    dw_b = matmul(yb_in, dyb, "tn", F32, "branch_b_dw")
    gn_after = gn if early is None else gn + early((dw_a, dw_b, dw_o, dw_gu, dw_d))[0:1]
    dproj, dlb, dgn = hgrn_backward(proj, st_a, dya_in, lb, gn_after, dproj)
    dxs, dbm, dcm, ddt, dproj, ddtb, dalog, ddsk, dnw = ssd_backward(xc, proj, st_b, dyb_in, dtb, alog, dsk, nw, dproj)
    dpre, dconv_w, dconv_b = conv_backward_pre(proj, dxs, dbm, dcm, conv_w, conv_b)
    if late is not None:
        late(dconv_b)
    dproj = conv_backward_input(dpre, conv_w, dproj)
    t = x.shape[0]
    tail = jnp.concatenate([jnp.sum(ddt, axis=0).astype(BF16), jnp.zeros((t, IN_PAD - COL_DT - LANES), BF16)], axis=1)
    dproj = lax.dynamic_update_slice(dproj, tail, (0, COL_DT))
    dw_in = matmul(u1, dproj, "tn", F32, "in_proj_dw")
    du1 = matmul(dproj, w_in, "nt", F32, "in_proj_dx", after=None if last is None else last(dw_in))

    def mod_bwd(r, c):
        _, vjp = jax.vjp(stage_modulate, r[0], *c)
        dx, dsc, dsh = vjp(r[1])
        return (dx + r[2],), (dsc, dsh)

    grad_x, dsc1, dsh1 = rowwise("modulate1_backward", mod_bwd, [_full(x), _full(du1), _full(dx_a)], [sc1, sh1],
                                 [(D, F32)], [vec, vec])
    d_mod = (dsh1, dsc1, dg1, dsh2, dsc2, dg2)
    d_wts = (dw_in, dw_a, dw_b, dw_o, dw_gu, dw_d)
    d_small = (dlb, dgn, dconv_w, dconv_b, ddtb.reshape(1, B_INNER),
               dalog.reshape(1, B_INNER), ddsk.reshape(1, B_INNER), dnw.reshape(1, B_INNER),
               dln1_g, dln1_b, dln2_g, dln2_b)
    return loss, grad_x, d_mod, d_wts, d_small


HBM = pl.BlockSpec(memory_space=pltpu.HBM)


def _place():
    return lax.axis_index("x"), lax.axis_index("y"), lax.axis_index("c")


def _other_chips(x, y):
    return [(1 - x, y), (x, 1 - y), (1 - x, 1 - y)]


def _remote(src, dst, send_sem, recv_sem, device):
    return pltpu.make_async_remote_copy(src_ref=src, dst_ref=dst, send_sem=send_sem, recv_sem=recv_sem,
                                        device_id=device, device_id_type=MESH)


def gather_rows(v, name):
    n = v.shape[1]

    def body(v_ref, out_ref, send_sems, recv_sems, local_sem):
        x, y, c = _place()
        mine = pltpu.make_async_copy(v_ref, out_ref.at[4 * x + 2 * y + c], local_sem)
        mine.start()
        sends, recvs = [], []
        for m in range(1, 8):
            px = 1 - x if m & 4 else x
            py = 1 - y if m & 2 else y
            pc = 1 - c if m & 1 else c
            sends.append(_remote(v_ref, out_ref.at[4 * x + 2 * y + c], send_sems.at[m - 1], recv_sems.at[m - 1], (px, py, pc)))
            recvs.append(_remote(v_ref, out_ref.at[4 * px + 2 * py + pc], send_sems.at[m - 1], recv_sems.at[m - 1], (px, py, pc)))
        for cp in sends:
            cp.start()
        for cp in recvs:
            cp.wait_recv()
        for cp in sends:
            cp.wait_send()
        mine.wait()

    return pl.pallas_call(
        body, name=name, in_specs=[HBM], out_specs=HBM,
        out_shape=jax.ShapeDtypeStruct((8, 1, n), v.dtype),
        scratch_shapes=[pltpu.SemaphoreType.DMA((7,)), pltpu.SemaphoreType.DMA((7,)), pltpu.SemaphoreType.DMA],
    )(v)


def exchange_rows(part, name):
    w = part.shape[2]

    def body(p_ref, out_ref, send_sems, recv_sems, local_sem):
        x, y, c = _place()
        k = 2 * x + y
        mine = pltpu.make_async_copy(p_ref.at[4 * x + 2 * y + c], out_ref.at[k], local_sem)
        mine.start()
        sends, recvs = [], []
        for j, (px, py) in enumerate(_other_chips(x, y)):
            sends.append(_remote(p_ref.at[4 * px + 2 * py + c], out_ref.at[k], send_sems.at[j], recv_sems.at[j], (px, py, c)))
            recvs.append(_remote(p_ref.at[4 * px + 2 * py + c], out_ref.at[2 * px + py], send_sems.at[j], recv_sems.at[j], (px, py, c)))
        for cp in sends:
            cp.start()
        for cp in recvs:
            cp.wait_recv()
        for cp in sends:
            cp.wait_send()
        mine.wait()

    return pl.pallas_call(
        body, name=name, in_specs=[HBM], out_specs=HBM,
        out_shape=jax.ShapeDtypeStruct((4, 1, w), part.dtype),
        scratch_shapes=[pltpu.SemaphoreType.DMA((3,)), pltpu.SemaphoreType.DMA((3,)), pltpu.SemaphoreType.DMA],
    )(part)


def gather_weights(shards):
    n = len(shards)

    def body(*refs):
        w_refs, out_refs = refs[:n], refs[n:2 * n]
        send_sems, recv_sems = refs[2 * n:]
        x, y, c = _place()
        chips = _other_chips(x, y)

        def half(i, px, py, pc):
            hr = shards[i].shape[0] // 2
            return out_refs[i].at[2 * px + py, pl.ds(pc * hr, hr), :]

        first, passed = [], []
        for i in range(n):
            hr = shards[i].shape[0] // 2
            for j, (px, py) in enumerate(chips):
                cp = _remote(w_refs[i].at[pl.ds(c * hr, hr), :], half(i, x, y, c),
                             send_sems.at[j * n + i], recv_sems.at[j * n + i], (px, py, c))
                cp.start()
                first.append(cp)
        for i in range(n):
            for j, (px, py) in enumerate(chips):
                mine_half = half(i, px, py, c)
                _remote(mine_half, mine_half, send_sems.at[j * n + i], recv_sems.at[j * n + i], (px, py, c)).wait_recv()
                cp = _remote(mine_half, mine_half, send_sems.at[(3 + j) * n + i], recv_sems.at[(3 + j) * n + i], (x, y, 1 - c))
                cp.start()
                passed.append(cp)
        for i in range(n):
            for j, (px, py) in enumerate(chips):
                other = half(i, px, py, 1 - c)
                _remote(other, other, send_sems.at[(3 + j) * n + i], recv_sems.at[(3 + j) * n + i], (x, y, 1 - c)).wait_recv()
        for cp in first + passed:
            cp.wait_send()

    return pl.pallas_call(
        body, name="gather_weights", in_specs=[HBM] * n, out_specs=[HBM] * n,
        out_shape=[jax.ShapeDtypeStruct((4,) + s.shape, s.dtype) for s in shards],
        scratch_shapes=[pltpu.SemaphoreType.DMA((6 * n,)), pltpu.SemaphoreType.DMA((6 * n,))],
    )(*shards)


def _half_of_slot(ref, rows, px, py, pc):
    return ref.at[2 * px + py, pl.ds(pc * (rows // 2), rows // 2), :]


def gather_start(shards, after):
    n = len(shards)

    def body(*refs):
        w_refs, land_refs = refs[:n], refs[n:2 * n]
        send_a, recv_a, send_b, recv_b = refs[2 * n + 1:2 * n + 5]
        token = refs[-1]
        x, y, c = _place()
        for i in range(n):
            rows = shards[i].shape[0]
            for j, (px, py) in enumerate(_other_chips(x, y)):
                sems = (send_a.at[j], recv_a.at[j]) if i == 0 else (send_b.at[j * (n - 1) + i - 1], recv_b.at[j * (n - 1) + i - 1])
                _remote(w_refs[i].at[pl.ds(c * (rows // 2), rows // 2), :], _half_of_slot(land_refs[i], rows, x, y, c),
                        *sems, (px, py, c)).start()
        token[...] = jnp.zeros_like(token)

    hbm = lambda a: pltpu.with_memory_space_constraint(a, pltpu.HBM)
    lands = [lax.empty((4,) + s.shape, s.dtype) for s in shards]
    dma = pltpu.SemaphoreType.DMA
    return pl.pallas_call(
        body, name="gather_start",
        out_shape=(dma((3,)), dma((3,)), dma((3 * (n - 1),)), dma((3 * (n - 1),)),
                   *[pltpu.HBM(a.shape, a.dtype) for a in list(shards) + lands], jax.ShapeDtypeStruct((8, LANES), F32)),
        in_specs=[HBM] * (2 * n) + [pl.BlockSpec(memory_space=pl.ANY)],
        out_specs=(SEM, SEM, SEM, SEM, *[HBM] * (2 * n), pl.BlockSpec(memory_space=pltpu.VMEM)),
        input_output_aliases={i: 4 + i for i in range(2 * n)},
        compiler_params=pltpu.CompilerParams(has_side_effects=DATAFLOW),
    )(*[hbm(a) for a in list(shards) + lands], after)


def gather_wait(send_sems, recv_sems, shards, lands, after, tag):
    n = len(shards)

    def body(*refs):
        w_refs, land_refs = refs[:n], refs[n:2 * n]
        send_ref, recv_ref = refs[2 * n], refs[2 * n + 1]
        x, y, c = _place()
        for i in range(n):
            rows = shards[i].shape[0]
            for j, (px, py) in enumerate(_other_chips(x, y)):
                cp = _remote(w_refs[i].at[pl.ds(c * (rows // 2), rows // 2), :], _half_of_slot(land_refs[i], rows, px, py, c),
                             send_ref.at[j * n + i], recv_ref.at[j * n + i], (px, py, c))
                cp.wait_send()
                cp.wait_recv()

    out = pl.pallas_call(
        body, name="gather_wait_" + tag,
        out_shape=tuple(pltpu.HBM(a.shape, a.dtype) for a in list(shards) + list(lands)),
        in_specs=[HBM] * (2 * n) + [SEM, SEM, pl.BlockSpec(memory_space=pl.ANY)], out_specs=tuple([HBM] * (2 * n)),
        input_output_aliases={i: i for i in range(2 * n)},
        compiler_params=pltpu.CompilerParams(has_side_effects=DATAFLOW),
    )(*shards, *lands, send_sems, recv_sems, after)
    return list(out[n:])


def forward_start(lands, tag):
    n = len(lands)

    def body(*refs):
        land_refs = refs[:n]
        send_sems, recv_sems = refs[n], refs[n + 1]
        token = refs[-1]
        x, y, c = _place()
        for i in range(n):
            rows = lands[i].shape[1]
            for j, (px, py) in enumerate(_other_chips(x, y)):
                mine = _half_of_slot(land_refs[i], rows, px, py, c)
                _remote(mine, mine, send_sems.at[j * n + i], recv_sems.at[j * n + i], (x, y, 1 - c)).start()
        token[...] = jnp.zeros_like(token)

    dma = pltpu.SemaphoreType.DMA
    return pl.pallas_call(
        body, name="forward_start_" + tag,
        out_shape=(dma((3 * n,)), dma((3 * n,)), *[pltpu.HBM(a.shape, a.dtype) for a in lands],
                   jax.ShapeDtypeStruct((8, LANES), F32)),
        in_specs=[HBM] * n, out_specs=(SEM, SEM, *[HBM] * n, pl.BlockSpec(memory_space=pltpu.VMEM)),
        input_output_aliases={i: 2 + i for i in range(n)},
        compiler_params=pltpu.CompilerParams(has_side_effects=DATAFLOW),
    )(*lands)


def forward_wait(started, after, tag):
    send_sems, recv_sems, *rest = started
    lands = rest[:-1]
    n = len(lands)

    def body(*refs):
        land_refs = refs[:n]
        send_ref, recv_ref = refs[n], refs[n + 1]
        x, y, c = _place()
        for i in range(n):
            rows = lands[i].shape[1]
            for j, (px, py) in enumerate(_other_chips(x, y)):
                cp = _remote(_half_of_slot(land_refs[i], rows, px, py, c), _half_of_slot(land_refs[i], rows, px, py, 1 - c),
                             send_ref.at[j * n + i], recv_ref.at[j * n + i], (x, y, 1 - c))
                cp.wait_send()
                cp.wait_recv()

    out = pl.pallas_call(
        body, name="forward_wait_" + tag,
        out_shape=tuple(pltpu.HBM(a.shape, a.dtype) for a in lands),
        in_specs=[HBM] * n + [SEM, SEM, pl.BlockSpec(memory_space=pl.ANY)], out_specs=tuple([HBM] * n),
        input_output_aliases={i: i for i in range(n)},
        compiler_params=pltpu.CompilerParams(has_side_effects=DATAFLOW),
    )(*lands, send_sems, recv_sems, after)
    return list(out)


def pair_exchange(slabs, name):
    n = len(slabs)

    def body(*refs):
        g_refs, out_refs = refs[:n], refs[n:2 * n]
        send_sems, recv_sems = refs[2 * n:]
        x, y, c = _place()
        copies = []
        for i in range(n):
            hr = slabs[i].shape[1] // 2
            cp = _remote(g_refs[i].at[:, pl.ds((1 - c) * hr, hr), :], out_refs[i], send_sems.at[i], recv_sems.at[i], (x, y, 1 - c))
            cp.start()
            copies.append(cp)
        for cp in copies:
            cp.wait()

    return pl.pallas_call(
        body, name=name, in_specs=[HBM] * n, out_specs=[HBM] * n,
        out_shape=[jax.ShapeDtypeStruct((4, s.shape[1] // 2, s.shape[2]), s.dtype) for s in slabs],
        scratch_shapes=[pltpu.SemaphoreType.DMA((n,)), pltpu.SemaphoreType.DMA((n,))],
    )(*slabs)


def _row_tile(rows, cols):
    fits = lambda r: r * cols * 4 <= BLOCK_BYTES
    if fits(rows):
        return rows
    return next(r for r in (1024, 512, 256, 128, 64, 32, 16) if rows % r == 0 and fits(r))


def pair_add(g, p, c, name):
    _, hr, cols = p.shape
    tm = _row_tile(hr, cols)
    per = hr // tm

    def body(c_ref, g_ref, p_ref, o_ref):
        o_ref[...] = (g_ref[...] + p_ref[...]).astype(o_ref.dtype)

    return pl.pallas_call(
        body, name=name,
        grid_spec=pltpu.PrefetchScalarGridSpec(
            num_scalar_prefetch=1, grid=(4, per),
            in_specs=[pl.BlockSpec((None, tm, cols), lambda k, i, c_ref: (k, c_ref[0] * per + i, 0)),
                      pl.BlockSpec((None, tm, cols), lambda k, i, c_ref: (k, i, 0))],
            out_specs=pl.BlockSpec((None, tm, cols), lambda k, i, c_ref: (k, i, 0))),
        out_shape=jax.ShapeDtypeStruct((4, hr, cols), BF16),
        compiler_params=_params(("arbitrary", "arbitrary")),
    )(c.reshape(1).astype(jnp.int32), g, p)


def scatter_sums(sums):
    n = len(sums)

    def body(*refs):
        s_refs, out_refs = refs[:n], refs[n:2 * n]
        send_sems, recv_sems = refs[2 * n:]
        x, y, c = _place()
        k = 2 * x + y
        sends, recvs = [], []
        for i in range(n):
            for j, (px, py) in enumerate(_other_chips(x, y)):
                sems = (send_sems.at[j * n + i], recv_sems.at[j * n + i])
                sends.append(_remote(s_refs[i].at[2 * px + py], out_refs[i].at[k], *sems, (px, py, c)))
                recvs.append(_remote(s_refs[i].at[2 * px + py], out_refs[i].at[2 * px + py], *sems, (px, py, c)))
        for cp in sends:
            cp.start()
        for cp in recvs:
            cp.wait_recv()
        for cp in sends:
            cp.wait_send()

    return pl.pallas_call(
        body, name="scatter_sums", in_specs=[HBM] * n, out_specs=[HBM] * n,
        out_shape=[jax.ShapeDtypeStruct(s.shape, s.dtype) for s in sums],
        scratch_shapes=[pltpu.SemaphoreType.DMA((3 * n,)), pltpu.SemaphoreType.DMA((3 * n,))],
    )(*sums)


SEM = pl.BlockSpec(memory_space=pltpu.SEMAPHORE)
DATAFLOW = pltpu.SideEffectType.DATAFLOW_SIDE_EFFECTING


def scatter_start(sums, tag):
    n = len(sums)

    def body(*refs):
        s_refs, land_refs = refs[:n], refs[n:2 * n]
        send_sems, recv_sems = refs[2 * n], refs[2 * n + 1]
        token = refs[-1]
        x, y, c = _place()
        k = 2 * x + y
        for i in range(n):
            for j, (px, py) in enumerate(_other_chips(x, y)):
                _remote(s_refs[i].at[2 * px + py], land_refs[i].at[k], send_sems.at[j * n + i], recv_sems.at[j * n + i],
                        (px, py, c)).start()
        token[...] = jnp.zeros_like(token)

    hbm = lambda a: pltpu.with_memory_space_constraint(a, pltpu.HBM)
    return pl.pallas_call(
        body, name="scatter_start_" + tag,
        out_shape=(pltpu.SemaphoreType.DMA((3 * n,)), pltpu.SemaphoreType.DMA((3 * n,)),
                   *[pltpu.HBM(s.shape, s.dtype) for s in sums], *[pltpu.HBM(s.shape, s.dtype) for s in sums],
                   jax.ShapeDtypeStruct((8, LANES), F32)),
        in_specs=[HBM] * (2 * n), out_specs=(SEM, SEM, *[HBM] * (2 * n), pl.BlockSpec(memory_space=pltpu.VMEM)),
        input_output_aliases={i: 2 + i for i in range(2 * n)},
        compiler_params=pltpu.CompilerParams(has_side_effects=DATAFLOW),
    )(*[hbm(s) for s in sums], *[hbm(lax.empty(s.shape, s.dtype)) for s in sums])


def scatter_wait(started, after, tag):
    send_sems, recv_sems, *rest = started
    n = (len(rest) - 1) // 2
    sums, lands = rest[:n], rest[n:2 * n]

    def body(*refs):
        s_refs, land_refs = refs[:n], refs[n:2 * n]
        send_ref, recv_ref = refs[2 * n], refs[2 * n + 1]
        x, y, c = _place()
        for i in range(n):
            for j, (px, py) in enumerate(_other_chips(x, y)):
                cp = _remote(s_refs[i].at[2 * px + py], land_refs[i].at[2 * px + py], send_ref.at[j * n + i],
                             recv_ref.at[j * n + i], (px, py, c))
                cp.wait_send()
                cp.wait_recv()

    out = pl.pallas_call(
        body, name="scatter_wait_" + tag,
        out_shape=tuple(pltpu.HBM(s.shape, s.dtype) for s in sums + lands),
        in_specs=[HBM] * (2 * n) + [SEM, SEM, pl.BlockSpec(memory_space=pl.ANY)], out_specs=tuple([HBM] * (2 * n)),
        input_output_aliases={i: i for i in range(2 * n)},
        compiler_params=pltpu.CompilerParams(has_side_effects=DATAFLOW),
    )(*sums, *lands, send_sems, recv_sems, after)
    return list(out[n:])


def sum_chips(landed, own, chip, core, name):
    _, hr, cols = landed.shape
    tm = _row_tile(hr, 4 * cols)
    per = hr // tm

    def body(idx_ref, l0, l1, l2, l3, own_ref, o_ref):
        mine = own_ref[...].astype(F32)
        v = [jnp.where(idx_ref[0] == k, mine, ref[...].astype(F32)) for k, ref in enumerate((l0, l1, l2, l3))]
        o_ref[...] = ((v[0] + v[1]) + v[2]) + v[3]

    slot = lambda k: pl.BlockSpec((None, tm, cols),
                                  lambda i, idx: (jnp.where(idx[0] == k, (k + 1) & 3, k), i, 0))
    return pl.pallas_call(
        body, name=name,
        grid_spec=pltpu.PrefetchScalarGridSpec(
            num_scalar_prefetch=1, grid=(per,),
            in_specs=[slot(0), slot(1), slot(2), slot(3),
                      pl.BlockSpec((None, tm, cols), lambda i, idx: (idx[0], i, 0))],
            out_specs=pl.BlockSpec((tm, cols), lambda i, idx: (idx[1] * per + i, 0))),
        out_shape=jax.ShapeDtypeStruct((2 * hr, cols), F32),
        compiler_params=_params(("arbitrary",)),
    )(jnp.stack([chip, core]).astype(jnp.int32), landed, landed, landed, landed, own)


def exchange_halves(bufs):
    n = len(bufs)

    def body(*refs):
        out_refs = refs[n:2 * n]
        send_sems, recv_sems = refs[2 * n:]
        x, y, c = _place()
        sends, recvs = [], []
        for i in range(n):
            hr = bufs[i].shape[0] // 2
            own = out_refs[i].at[pl.ds(c * hr, hr), :]
            other = out_refs[i].at[pl.ds((1 - c) * hr, hr), :]
            sends.append(_remote(own, own, send_sems.at[i], recv_sems.at[i], (x, y, 1 - c)))
            recvs.append(_remote(other, other, send_sems.at[i], recv_sems.at[i], (x, y, 1 - c)))
        for cp in sends:
            cp.start()
        for cp in recvs:
            cp.wait_recv()
        for cp in sends:
            cp.wait_send()

    return pl.pallas_call(
        body, name="exchange_halves", in_specs=[HBM] * n, out_specs=[HBM] * n,
        out_shape=[jax.ShapeDtypeStruct(b.shape, b.dtype) for b in bufs],
        input_output_aliases={i: i for i in range(n)},
        scratch_shapes=[pltpu.SemaphoreType.DMA((n,)), pltpu.SemaphoreType.DMA((n,))],
    )(*bufs)


def _relayout(name, arrays, in_blocks, out_blocks, out_shapes, fn):
    rows = 128
    spec = lambda blk: pl.BlockSpec(blk, (lambda i: (0, i, 0)) if len(blk) == 3 else (lambda i: (i, 0)))

    def body(*refs):
        n_in = len(arrays)
        outs = fn(*[r[...] for r in refs[:n_in]])
        for ref, val in zip(refs[n_in:], outs, strict=True):
            if isinstance(val, list):
                for k, piece in enumerate(val):
                    ref[k] = piece
            else:
                ref[...] = val

    return pl.pallas_call(
        body, name=name, grid=(D // rows,),
        in_specs=[spec(b) for b in in_blocks], out_specs=[spec(b) for b in out_blocks], out_shape=out_shapes,
        compiler_params=_params(("arbitrary",)),
    )(*arrays)


def assemble_in_proj(g):
    def fn(v):
        w = jnp.concatenate([v[k] for k in range(4)], axis=1)
        return (jnp.concatenate([w[:, :ORIG_Z], w[:, ORIG_GA:], w[:, ORIG_XBC:ORIG_DT], w[:, ORIG_Z:ORIG_XBC],
                                 w[:, ORIG_DT:ORIG_GA], jnp.zeros((w.shape[0], IN_PAD - IN_ORIG), w.dtype)], axis=1),)

    cols = g.shape[2]
    return _relayout("assemble_in_proj", [g], [(4, 128, cols)], [(128, IN_PAD)],
                     [jax.ShapeDtypeStruct((D, IN_PAD), g.dtype)], fn)[0]


def split_in_proj(dw):
    cols = IN_ORIG // 4

    def fn(d):
        w = jnp.concatenate([d[:, :COL_GA], d[:, COL_Z:COL_DT], d[:, COL_XBC:COL_Z], d[:, COL_DT:COL_DT + 32],
                             d[:, COL_GA:COL_XBC]], axis=1)
        return ([w[:, k * cols:(k + 1) * cols] for k in range(4)],)

    return _relayout("split_in_proj", [dw], [(128, IN_PAD)], [(4, 128, cols)],
                     [jax.ShapeDtypeStruct((4, D, cols), dw.dtype)], fn)[0]


def assemble_ffn_in(gate, up):
    fn = lambda a, b: (jnp.concatenate([a[k] for k in range(4)] + [b[k] for k in range(4)], axis=1),)
    cols = gate.shape[2]
    return _relayout("assemble_ffn_in", [gate, up], [(4, 128, cols)] * 2, [(128, 2 * D_FF)],
                     [jax.ShapeDtypeStruct((D, 2 * D_FF), gate.dtype)], fn)[0]


def split_ffn_in(dw):
    cols = D_FF // 4

    def fn(d):
        return ([d[:, k * cols:(k + 1) * cols] for k in range(4)],
                [d[:, D_FF + k * cols:D_FF + (k + 1) * cols] for k in range(4)])

    shape = jax.ShapeDtypeStruct((4, D, cols), dw.dtype)
    return _relayout("split_ffn_in", [dw], [(128, 2 * D_FF)], [(4, 128, cols)] * 2, [shape, shape], fn)


def ada_prepare(c_all, w_ada, hgrn_lb):
    def body(c_ref, w_ref, lb_ref, mod_ref, row_ref):
        mod_ref[...] = hdot(silu(c_ref[...]), w_ref[...])
        row_ref[...] = sigmoid(lb_ref[0:1, :] - lb_ref[1:2, :])

    return pl.pallas_call(
        body, name="ada_prepare",
        out_shape=[jax.ShapeDtypeStruct((8, w_ada.shape[1]), F32), jax.ShapeDtypeStruct((1, D), F32)],
        compiler_params=pltpu.CompilerParams(vmem_limit_bytes=VMEM_LIMIT),
    )(c_all, w_ada, hgrn_lb)


SMALL_SEGS = (("mod", 6 * D), ("lb", D), ("gnorm", LANES), ("conv_w", 4 * CONV_DIM), ("conv_b", CONV_DIM),
              ("dt_bias", B_INNER), ("a_log", B_INNER), ("d", B_INNER), ("ssm_norm", B_INNER),
              ("ln1_g", D), ("ln1_b", D), ("ln2_g", D), ("ln2_b", D))
SMALL_PARAMS = ("b_ada", "hgrn_lb", "hgrn_gnorm", "ssm_conv_b", "ssm_dt_bias", "ssm_a_log", "ssm_d", "ssm_norm",
                "ln1_g", "ln1_b", "ln2_g", "ln2_b")


def finalize_small(g_all, c_all, dmod_cols, params, m, v):
    n_p = len(SMALL_PARAMS)
    offs, o = {}, 0
    for nm, width in SMALL_SEGS:
        offs[nm] = (o, width)
        o += width

    def body(*refs):
        g_ref, c_ref, dm_ref = refs[:3]
        p_refs = refs[3:3 + n_p]
        m_refs = refs[3 + n_p:3 + 2 * n_p]
        v_refs = refs[3 + 2 * n_p:3 + 3 * n_p]
        outs = refs[3 + 3 * n_p:]
        gwa_ref, gcw_ref = outs[:2]
        res = outs[2:]
        total = jnp.sum(g_ref[...], axis=0, keepdims=True)
        seg = lambda nm: total[:, offs[nm][0]:offs[nm][0] + offs[nm][1]]
        gwa_ref[...] = hdot(silu(c_ref[...]), dm_ref[...], "tn")
        cw = seg("conv_w")
        for j in range(4):
            gcw_ref[j:j + 1, :] = cw[:, j * CONV_DIM:(j + 1) * CONV_DIM]
        hc = lax.broadcasted_iota(jnp.int32, (B_INNER, LANES), 0)
        hj = lax.broadcasted_iota(jnp.int32, (B_INNER, LANES), 1)
        per_head = ((hc >> 6) == hj).astype(F32)
        heads = lambda nm: hdot(jnp.broadcast_to(seg(nm), (8, B_INNER)), per_head)[0:1, 0:32]
        lbp = sigmoid(p_refs[1][0:1, :] - p_refs[1][1:2, :])
        g_row = seg("lb") * lbp * (1.0 - lbp)
        grads = {"b_ada": seg("mod"), "hgrn_gnorm": seg("gnorm"), "ssm_conv_b": seg("conv_b"),
                 "ssm_dt_bias": heads("dt_bias"), "ssm_a_log": heads("a_log"), "ssm_d": heads("d"),
                 "ssm_norm": seg("ssm_norm"), "ln1_g": seg("ln1_g"), "ln1_b": seg("ln1_b"),
                 "ln2_g": seg("ln2_g"), "ln2_b": seg("ln2_b")}
        for i, nm in enumerate(SMALL_PARAMS):
            g_out, d_out, m_out, v_out = res[4 * i:4 * i + 4]
            if nm == "hgrn_lb":
                for row, gv in ((0, g_row), (1, -g_row)):
                    sl = slice(row, row + 1)
                    dl, mn, vn = adamw(p_refs[i][sl, :], gv, m_refs[i][sl, :], v_refs[i][sl, :])
                    g_out[sl, :], d_out[sl, :], m_out[sl, :], v_out[sl, :] = gv, dl, mn, vn
            else:
                gv = grads[nm]
                dl, mn, vn = adamw(p_refs[i][...], gv, m_refs[i][...], v_refs[i][...])
                g_out[...], d_out[...], m_out[...], v_out[...] = gv, dl, mn, vn

    out_shape = [jax.ShapeDtypeStruct((D, dmod_cols.shape[1]), F32), jax.ShapeDtypeStruct((4, CONV_DIM), F32)]
    for p in params:
        out_shape += [jax.ShapeDtypeStruct(p.shape, F32)] * 4
    return pl.pallas_call(
        body, name="finalize_small", out_shape=out_shape,
        compiler_params=pltpu.CompilerParams(vmem_limit_bytes=VMEM_LIMIT),
    )(g_all, c_all, dmod_cols, *params, *m, *v)


def adam_update(w, g, m, v, name):
    cols = w.shape[1]
    return rowwise(name, lambda r, c: (adamw(*r), ()), [_full(w), _full(g), _full(m), _full(v)], [],
                   [(cols, F32)] * 3, tm_max=128)


def kernel(x, c, w_ada, b_ada, w_in, hgrn_lb, hgrn_gnorm, ssm_conv_w, ssm_conv_b, ssm_dt_bias, ssm_a_log, ssm_d, ssm_norm, w_branch_a, w_branch_b, w_o, ln1_g, ln1_b, w_ffn_gate, w_ffn_up, w_ffn_down, ln2_g, ln2_b, loss_target, m_w_ada, m_b_ada, m_w_in, m_hgrn_lb, m_hgrn_gnorm, m_ssm_conv_w, m_ssm_conv_b, m_ssm_dt_bias, m_ssm_a_log, m_ssm_d, m_ssm_norm, m_w_branch_a, m_w_branch_b, m_w_o, m_ln1_g, m_ln1_b, m_w_ffn_gate, m_w_ffn_up, m_w_ffn_down, m_ln2_g, m_ln2_b, v_w_ada, v_b_ada, v_w_in, v_hgrn_lb, v_hgrn_gnorm, v_ssm_conv_w, v_ssm_conv_b, v_ssm_dt_bias, v_ssm_a_log, v_ssm_d, v_ssm_norm, v_w_branch_a, v_w_branch_b, v_w_o, v_ln1_g, v_ln1_b, v_w_ffn_gate, v_w_ffn_up, v_w_ffn_down, v_ln2_g, v_ln2_b):
    given = dict(locals())
    chip = 2 * lax.axis_index("x") + lax.axis_index("y")
    core = lax.axis_index("c")
    t = x.shape[1]

    first = gather_rows(jnp.concatenate([c, ssm_conv_w.reshape(1, CONV_DIM)], axis=1), "gather_cond").reshape(8, D + CONV_DIM)
    c_all = first[:, :D]
    conv_w = first[0::2, D:].reshape(4, 4, CONV_DIM // 4).transpose(1, 0, 2).reshape(4, CONV_DIM)
    mod_part, lb_row = ada_prepare(c_all, w_ada[0], hgrn_lb)
    mod_cols = w_ada.shape[2]
    mod_row = exchange_rows(mod_part.reshape(8, 1, mod_cols), "exchange_mod").reshape(1, 6 * D) + b_ada
    mod = tuple(mod_row[:, i * D:(i + 1) * D] for i in range(6))

    shards = [given[nm][0].astype(BF16) for nm in SHARDED]
    n_w = len(SHARDED)
    send_in, recv_in, send_rest, recv_rest, *flying = gather_start(shards, mod_row)
    sent, lands = flying[:n_w], flying[n_w:2 * n_w]
    with_own = lambda land, shard: lax.dynamic_update_slice(land, shard[None], (chip, 0, 0))

    class Weights:
        def input_projection(self, after):
            land = gather_wait(send_in, recv_in, sent[:1], lands[:1], after, "in")
            (land,) = forward_wait(forward_start(land, "in"), after, "in")
            return assemble_in_proj(with_own(land, shards[0]))

        def start_rest(self, after):
            self.started = forward_start(gather_wait(send_rest, recv_rest, sent[1:], lands[1:], after, "rest"), "rest")
            return self.started[-1]

        def rest(self, after):
            got = {nm: with_own(land, s) for nm, land, s in zip(SHARDED[1:], forward_wait(self.started, after, "rest"), shards[1:], strict=True)}
            whole = lambda nm: got[nm].reshape(4 * got[nm].shape[1], got[nm].shape[2])
            return (whole("w_branch_a"), whole("w_branch_b"), whole("w_o"),
                    assemble_ffn_in(got["w_ffn_gate"], got["w_ffn_up"]), whole("w_ffn_down"))

    wts = Weights()

    per_channel = lambda p: jnp.repeat(p[0], B_INNER // 32)[None]
    small = (lb_row, hgrn_gnorm, conv_w, ssm_conv_b, per_channel(ssm_dt_bias), per_channel(ssm_a_log),
             per_channel(ssm_d), ssm_norm, ln1_g, ln1_b, ln2_g, ln2_b)
    by_rows = lambda g: g.reshape(4, g.shape[0] // 4, g.shape[1])
    travelling = {}

    def pair_sums(names, slabs, tag):
        received = pair_exchange(slabs, "pair_exchange_" + tag)
        return [pair_add(s, r, core, "pair_add_" + nm) for nm, s, r in zip(names, slabs, received, strict=True)]

    def start_early(dws):
        dw_a, dw_b, dw_o, dw_gu, dw_d = dws
        d_gate, d_up = split_ffn_in(dw_gu)
        travelling["pairs"] = pair_sums(SHARDED[1:], [by_rows(dw_a), by_rows(dw_b), by_rows(dw_o), d_gate, d_up, by_rows(dw_d)], "early")
        travelling["started"] = scatter_start(travelling["pairs"], "early")
        return travelling["started"][-1]

    def finish_early(after):
        travelling["landed"] = scatter_wait(travelling["started"], after, "early")

    def start_last(dw_in):
        travelling["pairs_in"] = pair_sums(SHARDED[:1], [split_in_proj(dw_in)], "last")
        travelling["started_in"] = scatter_start(travelling["pairs_in"], "last")
        return travelling["started_in"][-1]

    loss, grad_x, d_mod, d_wts, d_small = local_step(x[0], loss_target[0], mod, wts, small,
                                                     start_early, finish_early, start_last)

    d_lb, d_gn, d_cw, d_cb, d_dtb, d_alog, d_dsk, d_nw, d_l1g, d_l1b, d_l2g, d_l2b = d_small
    row = jnp.concatenate(list(d_mod) + [d_lb, d_gn, d_cw.reshape(1, 4 * CONV_DIM), d_cb, d_dtb, d_alog, d_dsk, d_nw,
                                          d_l1g, d_l1b, d_l2g, d_l2b], axis=1)
    g_all = gather_rows(row, "gather_small_grads").reshape(8, row.shape[1])
    dmod_cols = lax.dynamic_slice_in_dim(g_all, chip * mod_cols, mod_cols, axis=1)
    fin = finalize_small(g_all, c_all, dmod_cols, [given[n] for n in SMALL_PARAMS],
                         [given["m_" + n] for n in SMALL_PARAMS], [given["v_" + n] for n in SMALL_PARAMS])
    grads, deltas, new_m, new_v = {}, {}, {}, {}
    grads["w_ada"] = fin[0][None]
    grads["ssm_conv_w"] = lax.dynamic_slice_in_dim(fin[1], chip * (CONV_DIM // 4), CONV_DIM // 4, axis=1)[None]
    for i, nm in enumerate(SMALL_PARAMS):
        grads[nm], deltas[nm], new_m[nm], new_v[nm] = fin[2 + 4 * i:6 + 4 * i]

    pairs = travelling["pairs_in"] + travelling["pairs"]
    landed = scatter_wait(travelling["started_in"], fin[2], "last") + travelling["landed"]
    halves = [sum_chips(r, p, chip, core, "sum_chips_" + nm) for nm, r, p in zip(SHARDED, landed, pairs, strict=True)]
    for nm, r in zip(SHARDED, exchange_halves(halves), strict=True):
        grads[nm] = r[None]
    for nm in ("w_ada", "ssm_conv_w") + SHARDED:
        shp = given[nm].shape
        two_d = lambda a: a.reshape(shp[-2], shp[-1])
        d_, m_, v_ = adam_update(two_d(given[nm]), two_d(grads[nm]), two_d(given["m_" + nm]), two_d(given["v_" + nm]),
                                 "adam_" + nm)
        deltas[nm], new_m[nm], new_v[nm] = d_.reshape(shp), m_.reshape(shp), v_.reshape(shp)

    names = ("w_ada", "b_ada", "w_in", "hgrn_lb", "hgrn_gnorm", "ssm_conv_w", "ssm_conv_b", "ssm_dt_bias", "ssm_a_log",
             "ssm_d", "ssm_norm", "w_branch_a", "w_branch_b", "w_o", "ln1_g", "ln1_b", "w_ffn_gate", "w_ffn_up",
             "w_ffn_down", "ln2_g", "ln2_b")
    total_loss = lax.psum(loss[0, 0], ("x", "y", "c"))
    return (total_loss, grad_x[None], *[grads[n] for n in names], *[deltas[n] for n in names],
            *[new_m[n] for n in names], *[new_v[n] for n in names])
```

```python
import functools

import jax
import jax.numpy as jnp
from jax import lax
from jax.experimental import pallas as pl
from jax.experimental.pallas import tpu as pltpu

F32, BF16 = jnp.float32, jnp.bfloat16
HI = lax.Precision.HIGHEST
MESH = pl.DeviceIdType.MESH

D = 1024
CHUNK = 64
LANES = 128
N_HEADS_A = 8
N_GROUPS_B = 4
B_INNER = 2048
CONV_DIM = 3072
D_FF = 2816
ALPHA = 2.0 ** 0.25
LN_EPS = 1e-5
RMS_EPS = 1e-6
ADAM_LR, ADAM_B1, ADAM_B2, ADAM_EPS, ADAM_WD, ADAM_STEP = 0.001, 0.9, 0.999, 1e-08, 0.01, 10

IN_ORIG = 11296
IN_PAD = 11520
COL_GA, COL_GB, COL_XBC, COL_Z, COL_DT = 4096, 5120, 6144, 9216, 11264
ORIG_Z, ORIG_XBC, ORIG_DT, ORIG_GA = 4096, 6144, 9216, 9248

SHARDED = ("w_in", "w_branch_a", "w_branch_b", "w_o", "w_ffn_gate", "w_ffn_up", "w_ffn_down")
TRANSPOSED = ("w_ffn_gate", "w_ffn_up")
VMEM_LIMIT = 56 * 1024 * 1024
BLOCK_BYTES = 2 * 1024 * 1024
_DIMS = {"nn": (((1,), (0,)), ((), ())), "nt": (((1,), (1,)), ((), ())), "tn": (((0,), (0,)), ((), ()))}


def _bd(a, b, mode):
    return lax.dot_general(a.astype(BF16), b.astype(BF16), _DIMS[mode], preferred_element_type=F32)


@functools.partial(jax.custom_vjp, nondiff_argnums=(2,))
def bdot(a, b, mode):
    return _bd(a, b, mode)


def _bdot_fwd(a, b, mode):
    return _bd(a, b, mode), (a, b)


def _bdot_bwd(mode, res, g):
    a, b = res
    if mode == "nn":
        return _bd(g, b, "nt"), _bd(a, g, "tn")
    if mode == "nt":
        return _bd(g, b, "nn"), _bd(g, a, "tn")
    return _bd(b, g, "nt"), _bd(a, g, "nn")


bdot.defvjp(_bdot_fwd, _bdot_bwd)


def hdot(a, b, mode="nn"):
    return lax.dot_general(a, b, _DIMS[mode], precision=HI, preferred_element_type=F32)


def _raw(a, b, mode):
    return lax.dot_general(a, b, _DIMS[mode], preferred_element_type=F32)


def _split(x, n):
    parts, rest = [], x
    for _ in range(n):
        p = rest.astype(BF16)
        parts.append(p)
        rest = rest - p.astype(F32)
    return parts


def _od(a, b, mode, exact):
    if exact == 1:
        e = b.astype(BF16)
        p = _split(a, 3)
        return (_raw(p[2], e, mode) + _raw(p[1], e, mode)) + _raw(p[0], e, mode)
    e = a.astype(BF16)
    p = _split(b, 3)
    return (_raw(e, p[2], mode) + _raw(e, p[1], mode)) + _raw(e, p[0], mode)


@functools.partial(jax.custom_vjp, nondiff_argnums=(2, 3))
def odot(a, b, mode, exact):
    return _od(a, b, mode, exact)


def _odot_fwd(a, b, mode, exact):
    return _od(a, b, mode, exact), (a, b)


def _odot_bwd(mode, exact, res, g):
    a, b = res
    if exact == 1:
        da = {"nn": lambda: _od(g, b, "nt", 1), "nt": lambda: _od(g, b, "nn", 1), "tn": lambda: _od(b, g, "nt", 0)}[mode]()
        return da, jnp.zeros_like(b)
    db = {"nn": lambda: _od(a, g, "tn", 0), "nt": lambda: _od(g, a, "tn", 1), "tn": lambda: _od(a, g, "nn", 0)}[mode]()
    return jnp.zeros_like(a), db


odot.defvjp(_odot_fwd, _odot_bwd)


_BDIMS = {"bnn": (((2,), (1,)), ((0,), (0,))), "bnt": (((2,), (2,)), ((0,), (0,))), "btn": (((1,), (1,)), ((0,), (0,)))}


def _braw(a, b, mode):
    return lax.dot_general(a, b, _BDIMS[mode], preferred_element_type=F32)


def _bdb(a, b, mode):
    return _braw(a.astype(BF16), b.astype(BF16), mode)


def _d3b(a, b, mode):
    ah, al = _split(a, 2)
    bh, bl = _split(b, 2)
    return _braw(ah, bh, mode) + (_braw(ah, bl, mode) + _braw(al, bh, mode))


def _batched_bwd(f):
    def bwd(mode, res, g):
        a, b = res
        if mode == "bnn":
            return f(g, b, "bnt"), f(a, g, "btn")
        if mode == "bnt":
            return f(g, b, "bnn"), f(g, a, "btn")
        return f(b, g, "bnt"), f(a, g, "bnn")
    return bwd


@functools.partial(jax.custom_vjp, nondiff_argnums=(2,))
def bdot_b(a, b, mode):
    return _bdb(a, b, mode)


bdot_b.defvjp(lambda a, b, mode: (_bdb(a, b, mode), (a, b)), _batched_bwd(_bdb))


@functools.partial(jax.custom_vjp, nondiff_argnums=(2,))
def dot3_b(a, b, mode):
    return _d3b(a, b, mode)


dot3_b.defvjp(lambda a, b, mode: (_d3b(a, b, mode), (a, b)), _batched_bwd(_d3b))


def _cum(tril3, x, mode):
    e = tril3.astype(BF16)
    p = _split(x, 3)
    return (_braw(e, p[2], mode) + _braw(e, p[1], mode)) + _braw(e, p[0], mode)


@jax.custom_vjp
def chunk_cumsum(tril3, x):
    return _cum(tril3, x, "bnn")


chunk_cumsum.defvjp(lambda t, x: (_cum(t, x, "bnn"), t), lambda t, g: (jnp.zeros_like(t), _cum(t, g, "btn")))


def _unstack(axis, n):
    @jax.custom_vjp
    def un(x):
        return tuple(lax.index_in_dim(x, i, axis, keepdims=False) for i in range(n))

    un.defvjp(lambda x: (un(x), None), lambda _, g: (jnp.stack(g, axis=axis),))
    return un


def _split_last(n, w):
    @jax.custom_vjp
    def sp(x):
        return tuple(x[..., i * w:(i + 1) * w] for i in range(n))

    sp.defvjp(lambda x: (sp(x), None), lambda _, g: (jnp.concatenate(g, axis=-1),))
    return sp


def sigmoid(x):
    return 1.0 / (1.0 + jnp.exp(-x))


def silu(x):
    return x * sigmoid(x)


def softplus(x):
    return jnp.maximum(x, 0.0) + jnp.log1p(jnp.exp(jnp.minimum(x, -x)))


def _ln(x):
    mu = jnp.mean(x, axis=-1, keepdims=True)
    xc = x - mu
    return xc * lax.rsqrt(jnp.mean(xc * xc, axis=-1, keepdims=True) + LN_EPS)


def _tril64():
    r = lax.broadcasted_iota(jnp.int32, (CHUNK, CHUNK), 0)
    c = lax.broadcasted_iota(jnp.int32, (CHUNK, CHUNK), 1)
    return (r >= c).astype(F32)


def hgrn_block(q, fl, iv, gr, st, lb, gn):
    tb = q.shape[0]
    nc = tb // CHUNK
    nh = N_HEADS_A
    heads = _split_last(nh, LANES)
    to4 = lambda a: jnp.stack(heads(a), axis=0).reshape(nh, nc, CHUNK, LANES)
    flat = lambda a: a.reshape(nh * nc, CHUNK, LANES)
    f = lb + (1.0 - lb) * sigmoid(fl)
    gl4, k4, qf4, v4, gr4 = to4(jnp.log(f)), to4(1.0 - f), to4(silu(q) * (128 ** -0.5)), to4(iv), to4(gr)
    tril = _tril64()
    b4 = chunk_cumsum(jnp.broadcast_to(tril[None], (nh * nc, CHUNK, CHUNK)), flat(gl4)).reshape(gl4.shape)
    blast = jnp.sum(gl4, axis=2, keepdims=True)
    ref = lax.stop_gradient(0.5 * blast)
    sc = dot3_b(flat(qf4 * jnp.exp(b4 - ref)), flat(k4 * jnp.exp(ref - b4)), "bnt") * tril
    o_intra = bdot_b(sc, flat(v4), "bnn").reshape(gl4.shape)
    chunks = _unstack(1, nc)
    qe, v_c, kd, dec = chunks(qf4 * jnp.exp(b4)), chunks(v4), chunks(k4 * jnp.exp(blast - b4)), chunks(jnp.exp(blast))
    o_inter = []
    for c in range(nc):
        o_inter.append(bdot_b(qe[c], st, "bnt"))
        st = st * dec[c] + bdot_b(v_c[c], kd[c], "btn")
    o = o_intra + jnp.stack(o_inter, axis=1)
    on = o * lax.rsqrt(jnp.mean(o * o, axis=-1, keepdims=True) + RMS_EPS) * gn
    out = (on * silu(gr4)).reshape(nh, tb, LANES)
    return jnp.concatenate(_unstack(0, nh)(out), axis=1), st


def ssd_consts(g):
    i32 = jnp.int32
    ej = lax.broadcasted_iota(i32, (LANES, 512), 0)
    ec = lax.broadcasted_iota(i32, (LANES, 512), 1)
    expand = (ej == g * 8 + (ec >> 6)).astype(F32)
    ts = lax.broadcasted_iota(i32, (CHUNK, 512), 0)
    tc = lax.broadcasted_iota(i32, (CHUNK, 512), 1)
    itile = (ts == (tc & 63)).astype(F32)
    maskall = ts >= (tc & 63)
    br = lax.broadcasted_iota(i32, (256, 256), 0)
    bc = lax.broadcasted_iota(i32, (256, 256), 1)
    blockmask = ((br >> 6) == (bc >> 6)).astype(F32)
    return expand, itile, maskall, blockmask, _tril64()


def ssd_block(x, bm, cm, dt, z, st, dtb, alog, dsk, nw, cs):
    expand, itile, maskall, blockmask, tril = cs
    tb = x.shape[0]
    nc = tb // CHUNK
    delta = softplus(odot(dt, expand, "nn", 1) + dtb)
    a = -jnp.exp(alog) * delta
    xdt = x * delta
    by_chunk = lambda v: v.reshape(nc, CHUNK, v.shape[-1])
    a3, xdt3, bm3, cm3 = by_chunk(a), by_chunk(xdt), by_chunk(bm), by_chunk(cm)
    acum3 = chunk_cumsum(jnp.broadcast_to(tril[None], (nc, CHUNK, CHUNK)), a3)
    alast3 = jnp.sum(a3, axis=1, keepdims=True)
    cb3 = bdot_b(cm3, jnp.concatenate([bm3] * 8, axis=1), "bnt")
    arow3 = jnp.sum(acum3 * itile, axis=1, keepdims=True)
    dec3 = jnp.where(maskall, jnp.exp(jnp.minimum(acum3 - arow3, 0.0)), 0.0)
    halves = _split_last(2, 256)
    intra = [bdot_b(m, jnp.concatenate([xh] * 4, axis=1) * blockmask, "bnn")
             for m, xh in zip(halves(cb3 * dec3), halves(xdt3))]
    chunks = _unstack(0, nc)
    cm_c, bm_c, xw_c, dec_c = chunks(cm3), chunks(bm3), chunks(xdt3 * jnp.exp(alast3 - acum3)), chunks(jnp.exp(alast3))
    inter = []
    for c in range(nc):
        inter.append(bdot(cm_c[c], st, "nn"))
        st = st * dec_c[c] + bdot(bm_c[c], xw_c[c], "tn")
    st_new = st
    y = (jnp.concatenate(intra, axis=-1) + jnp.stack(inter, axis=0) * jnp.exp(acum3)).reshape(tb, 512)
    yz = (y + x * dsk) * silu(z)
    return yz * lax.rsqrt(jnp.mean(yz * yz, axis=-1, keepdims=True) + RMS_EPS) * nw, st_new


def adamw(w, g, m, v):
    m = ADAM_B1 * m + (1.0 - ADAM_B1) * g
    v = ADAM_B2 * v + (1.0 - ADAM_B2) * jnp.square(g)
    m_hat = m / (1.0 - ADAM_B1 ** ADAM_STEP)
    v_hat = v / (1.0 - ADAM_B2 ** ADAM_STEP)
    return -ADAM_LR * (m_hat / (jnp.sqrt(v_hat) + ADAM_EPS) + ADAM_WD * w), m, v


def _pick(n, cands):
    for c in cands:
        if n % c == 0:
            return c
    return n


def _params(sem):
    return pltpu.CompilerParams(dimension_semantics=sem, vmem_limit_bytes=VMEM_LIMIT)


def matmul(a, b, mode, out_dtype, name, after=None):
    if mode == "nn":
        (m, k), n = a.shape, b.shape[1]
    elif mode == "nt":
        (m, k), n = a.shape, b.shape[0]
    else:
        (k, m), n = a.shape, b.shape[1]
    tm = _pick(m, (1408, 1024, 768, 512, 256, 128))
    tn = _pick(n, (1408, 1024, 768, 512, 256, 128))
    tk = _pick(k, (2304, 2048, 1408, 1024, 768, 512, 256, 128))
    nk = k // tk
    a_spec = pl.BlockSpec((tk, tm), lambda i, j, kk: (kk, i)) if mode == "tn" else pl.BlockSpec((tm, tk), lambda i, j, kk: (i, kk))
    b_spec = pl.BlockSpec((tn, tk), lambda i, j, kk: (j, kk)) if mode == "nt" else pl.BlockSpec((tk, tn), lambda i, j, kk: (kk, j))

    order = [] if after is None else [after]

    def body(a_ref, b_ref, *rest):
        o_ref, *acc = rest[len(order):]
        part = _bd(a_ref[...], b_ref[...], mode)
        if nk == 1:
            o_ref[...] = part.astype(o_ref.dtype)
            return
        acc_ref, = acc
        kk = pl.program_id(2)

        @pl.when(kk == 0)
        def _():
            acc_ref[...] = part

        @pl.when(jnp.logical_and(kk > 0, kk < nk - 1))
        def _():
            acc_ref[...] += part

        @pl.when(kk == nk - 1)
        def _():
            o_ref[...] = (acc_ref[...] + part).astype(o_ref.dtype)

    return pl.pallas_call(
        body, name=name, grid=(m // tm, n // tn, nk),
        in_specs=[a_spec, b_spec] + [pl.BlockSpec(memory_space=pl.ANY) for _ in order],
        out_specs=pl.BlockSpec((tm, tn), lambda i, j, kk: (i, j)),
        out_shape=jax.ShapeDtypeStruct((m, n), out_dtype),
        scratch_shapes=[pltpu.VMEM((tm, tn), F32)] if nk > 1 else [],
        compiler_params=_params(("parallel", "parallel", "arbitrary")),
    )(a, b, *order)


def rowwise(name, fn, rows, consts, out_rows, out_accs=(), tm_max=256, into=None, new_wide=None):
    t = rows[0][0].shape[0]
    tm = _pick(t, (tm_max, 128, 64, 32, 16, 8))
    n_r, n_c, n_o = len(rows), len(consts), len(out_rows)
    n_alias = 0 if into is None else 1

    def body(*refs):
        r_in = [r[...] for r in refs[:n_r]]
        c_in = [r[...] for r in refs[n_r:n_r + n_c]]
        refs = refs[:n_r + n_c] + refs[n_r + n_c + n_alias:]
        o_refs = refs[n_r + n_c:n_r + n_c + n_o]
        a_refs = refs[n_r + n_c + n_o:]
        ro, ao = fn(r_in, c_in)
        for ref, val in zip(o_refs, ro, strict=True):
            ref[...] = val.astype(ref.dtype)
        if a_refs:
            @pl.when(pl.program_id(0) == 0)
            def _():
                for ref in a_refs:
                    ref[...] = jnp.zeros_like(ref)

            for ref, val in zip(a_refs, ao, strict=True):
                ref[...] += val

    in_specs = [pl.BlockSpec((tm, w), functools.partial(lambda i, cb: (i, cb), cb=cb)) for _, w, cb in rows]
    in_specs += [pl.BlockSpec(c.shape, lambda i: (0, 0)) for c in consts]
    out_specs = [pl.BlockSpec((tm, w), lambda i: (i, 0)) for w, _ in out_rows]
    out_specs += [pl.BlockSpec(s, lambda i: (0, 0)) for s in out_accs]
    out_shape = [jax.ShapeDtypeStruct((t, w), dt) for w, dt in out_rows]
    out_shape += [jax.ShapeDtypeStruct(s, F32) for s in out_accs]
    operands = [r[0] for r in rows] + list(consts)
    aliases = {}
    if into is not None:
        target, cb = into
        in_specs.append(pl.BlockSpec(memory_space=pl.ANY))
        operands.append(target)
        out_specs[0] = pl.BlockSpec((tm, out_rows[0][0]), lambda i: (i, cb))
        out_shape[0] = jax.ShapeDtypeStruct(target.shape, target.dtype)
        aliases = {len(operands) - 1: 0}
    if new_wide is not None:
        width, cb = new_wide
        out_specs[0] = pl.BlockSpec((tm, out_rows[0][0]), lambda i: (i, cb))
        out_shape[0] = jax.ShapeDtypeStruct((t, width), out_rows[0][1])
    return pl.pallas_call(
        body, name=name, grid=(t // tm,), in_specs=in_specs, out_specs=out_specs, out_shape=out_shape,
        input_output_aliases=aliases, compiler_params=_params(("arbitrary",)),
    )(*operands)


def _full(a):
    return (a, a.shape[1], 0)


def _time_block(t):
    return _pick(t, (256, 128, 64))


def _quarters(ref):
    return [ref[:, seg * D:(seg + 1) * D] for seg in range(4)]


def hgrn_forward(proj, lb, gn):
    t = proj.shape[0]
    tb = _time_block(t)
    nb = t // tb

    def body(qfig_ref, lb_ref, gn_ref, o_ref, st_ref, state):
        @pl.when(pl.program_id(0) == 0)
        def _():
            state[...] = jnp.zeros_like(state)

        st = state[...]
        st_ref[...] = st
        out, st_new = hgrn_block(*_quarters(qfig_ref), st, lb_ref[...], gn_ref[...])
        o_ref[...] = out.astype(o_ref.dtype)
        state[...] = st_new

    return pl.pallas_call(
        body, name="hgrn_forward", grid=(nb,),
        in_specs=[pl.BlockSpec((tb, 4 * D), lambda j: (j, 0)),
                  pl.BlockSpec((1, D), lambda j: (0, 0)), pl.BlockSpec((1, LANES), lambda j: (0, 0))],
        out_specs=[pl.BlockSpec((tb, D), lambda j: (j, 0)),
                   pl.BlockSpec((None, N_HEADS_A, LANES, LANES), lambda j: (j, 0, 0, 0))],
        out_shape=[jax.ShapeDtypeStruct((t, D), BF16),
                   jax.ShapeDtypeStruct((nb, N_HEADS_A, LANES, LANES), F32)],
        scratch_shapes=[pltpu.VMEM((N_HEADS_A, LANES, LANES), F32)],
        compiler_params=_params(("arbitrary",)),
    )(proj, lb, gn)


def hgrn_backward(proj, states, d_out, lb, gn, d_proj):
    t = proj.shape[0]
    tb = _time_block(t)
    nb = t // tb

    def body(qfig_ref, st_ref, do_ref, lb_ref, gn_ref, _, dqfig_ref, dlb_ref, dgn_ref, d_state):
        @pl.when(pl.program_id(0) == 0)
        def _():
            d_state[...] = jnp.zeros_like(d_state)
            dlb_ref[...] = jnp.zeros_like(dlb_ref)
            dgn_ref[...] = jnp.zeros_like(dgn_ref)

        _, vjp = jax.vjp(hgrn_block, *_quarters(qfig_ref), st_ref[...], lb_ref[...], gn_ref[...])
        dq, df, di, dg, dst, dlb, dgn = vjp((do_ref[...], d_state[...]))
        for seg, val in enumerate((dq, df, di, dg)):
            dqfig_ref[:, seg * D:(seg + 1) * D] = val.astype(dqfig_ref.dtype)
        d_state[...] = dst
        dlb_ref[...] += dlb
        dgn_ref[...] += dgn

    rev = lambda j: nb - 1 - j
    return pl.pallas_call(
        body, name="hgrn_backward", grid=(nb,),
        in_specs=[pl.BlockSpec((tb, 4 * D), lambda j: (rev(j), 0)),
                  pl.BlockSpec((None, N_HEADS_A, LANES, LANES), lambda j: (rev(j), 0, 0, 0)),
                  pl.BlockSpec((tb, D), lambda j: (rev(j), 0)),
                  pl.BlockSpec((1, D), lambda j: (0, 0)), pl.BlockSpec((1, LANES), lambda j: (0, 0)),
                  pl.BlockSpec(memory_space=pl.ANY)],
        out_specs=[pl.BlockSpec((tb, 4 * D), lambda j: (rev(j), 0)),
                   pl.BlockSpec((1, D), lambda j: (0, 0)), pl.BlockSpec((1, LANES), lambda j: (0, 0))],
        out_shape=[jax.ShapeDtypeStruct(d_proj.shape, d_proj.dtype), jax.ShapeDtypeStruct((1, D), F32),
                   jax.ShapeDtypeStruct((1, LANES), F32)],
        input_output_aliases={5: 0},
        scratch_shapes=[pltpu.VMEM((N_HEADS_A, LANES, LANES), F32)],
        compiler_params=_params(("arbitrary",)),
    )(proj, states, d_out, lb, gn, d_proj)


def _ssd_in_specs(tb, tmap):
    return [pl.BlockSpec((tb, 512), lambda g, j: (tmap(j), g)),
            pl.BlockSpec((tb, LANES), lambda g, j: (tmap(j), 16 + g)),
            pl.BlockSpec((tb, LANES), lambda g, j: (tmap(j), 20 + g)),
            pl.BlockSpec((tb, LANES), lambda g, j: (tmap(j), COL_DT // LANES)),
            pl.BlockSpec((tb, 512), lambda g, j: (tmap(j), COL_Z // 512 + g))]


def ssd_forward(xc, proj, dtb, alog, dsk, nw):
    t = proj.shape[0]
    tb = _time_block(t)
    nb = t // tb

    def body(x_ref, b_ref, c_ref, dt_ref, z_ref, dtb_ref, alog_ref, dsk_ref, nw_ref, o_ref, st_ref, state):
        @pl.when(pl.program_id(1) == 0)
        def _():
            state[...] = jnp.zeros_like(state)

        st = state[...]
        st_ref[...] = st
        out, st_new = ssd_block(x_ref[...], b_ref[...], c_ref[...], dt_ref[...], z_ref[...], st,
                                dtb_ref[...], alog_ref[...], dsk_ref[...], nw_ref[...], ssd_consts(pl.program_id(0)))
        o_ref[...] = out.astype(o_ref.dtype)
        state[...] = st_new

    vec = pl.BlockSpec((1, 512), lambda g, j: (0, g))
    return pl.pallas_call(
        body, name="ssd_forward", grid=(N_GROUPS_B, nb),
        in_specs=_ssd_in_specs(tb, lambda j: j) + [vec] * 4,
        out_specs=[pl.BlockSpec((tb, 512), lambda g, j: (j, g)),
                   pl.BlockSpec((None, None, LANES, 512), lambda g, j: (j, g, 0, 0))],
        out_shape=[jax.ShapeDtypeStruct((t, B_INNER), BF16),
                   jax.ShapeDtypeStruct((nb, N_GROUPS_B, LANES, 512), F32)],
        scratch_shapes=[pltpu.VMEM((LANES, 512), F32)],
        compiler_params=_params(("arbitrary", "arbitrary")),
    )(xc, xc, xc, proj, proj, dtb, alog, dsk, nw)


def ssd_backward(xc, proj, states, d_out, dtb, alog, dsk, nw, d_proj):
    t = proj.shape[0]
    tb = _time_block(t)
    nb = t // tb
    rev = lambda j: nb - 1 - j

    def body(x_ref, b_ref, c_ref, dt_ref, z_ref, st_ref, do_ref, dtb_ref, alog_ref, dsk_ref, nw_ref, _,
             dx_ref, db_ref, dc_ref, ddt_ref, dz_ref, ddtb_ref, dalog_ref, ddsk_ref, dnw_ref, d_state):
        accs = (ddtb_ref, dalog_ref, ddsk_ref, dnw_ref)

        @pl.when(pl.program_id(1) == 0)
        def _():
            d_state[...] = jnp.zeros_like(d_state)
            for ref in accs:
                ref[...] = jnp.zeros_like(ref)

        cs = ssd_consts(pl.program_id(0))
        fn = lambda *a: ssd_block(*a, cs)
        _, vjp = jax.vjp(fn, x_ref[...], b_ref[...], c_ref[...], dt_ref[...], z_ref[...], st_ref[...],
                         dtb_ref[...], alog_ref[...], dsk_ref[...], nw_ref[...])
        dx, db, dc, ddt, dz, dst, *dpar = vjp((do_ref[...], d_state[...]))
        dx_ref[...] = dx
        db_ref[...] = db
        dc_ref[...] = dc
        ddt_ref[...] = ddt
        dz_ref[...] = dz.astype(dz_ref.dtype)
        d_state[...] = dst
        for ref, val in zip(accs, dpar, strict=True):
            ref[...] += val

    vec = pl.BlockSpec((1, 512), lambda g, j: (0, g))
    acc = pl.BlockSpec((None, 1, 512), lambda g, j: (g, 0, 0))
    return pl.pallas_call(
        body, name="ssd_backward", grid=(N_GROUPS_B, nb),
        in_specs=_ssd_in_specs(tb, rev)
        + [pl.BlockSpec((None, None, LANES, 512), lambda g, j: (rev(j), g, 0, 0)),
           pl.BlockSpec((tb, 512), lambda g, j: (rev(j), g))] + [vec] * 4 + [pl.BlockSpec(memory_space=pl.ANY)],
        out_specs=[pl.BlockSpec((tb, 512), lambda g, j: (rev(j), g)),
                   pl.BlockSpec((tb, LANES), lambda g, j: (rev(j), g)),
                   pl.BlockSpec((tb, LANES), lambda g, j: (rev(j), g)),
                   pl.BlockSpec((None, tb, LANES), lambda g, j: (g, rev(j), 0)),
                   pl.BlockSpec((tb, 512), lambda g, j: (rev(j), COL_Z // 512 + g)), acc, acc, acc, acc],
        out_shape=[jax.ShapeDtypeStruct((t, B_INNER), F32), jax.ShapeDtypeStruct((t, 512), F32),
                   jax.ShapeDtypeStruct((t, 512), F32), jax.ShapeDtypeStruct((N_GROUPS_B, t, LANES), F32),
                   jax.ShapeDtypeStruct(d_proj.shape, d_proj.dtype)] + [jax.ShapeDtypeStruct((N_GROUPS_B, 1, 512), F32)] * 4,
        input_output_aliases={11: 4},
        scratch_shapes=[pltpu.VMEM((LANES, 512), F32)],
        compiler_params=_params(("arbitrary", "arbitrary")),
    )(xc, xc, xc, proj, proj, states, d_out, dtb, alog, dsk, nw, d_proj)


CONV_HALO = 8


def _shift_down(halo_then_tile, s, tm):
    if s == 0:
        return halo_then_tile[CONV_HALO:CONV_HALO + tm]
    return pltpu.roll(halo_then_tile, s, 0)[CONV_HALO:CONV_HALO + tm]


def _conv_pre(cur, prev, w, b, tm):
    stacked = jnp.concatenate([prev, cur], axis=0)
    taps = [_shift_down(stacked, 3 - j, tm) for j in range(4)]
    pre = b + taps[0] * w[0:1] + taps[1] * w[1:2] + taps[2] * w[2:3] + taps[3] * w[3:4]
    return pre, taps


def _conv_specs(t, tm):
    per = tm // CONV_HALO
    cur = pl.BlockSpec((tm, CONV_DIM), lambda i: (i, COL_XBC // CONV_DIM))
    prev = pl.BlockSpec((CONV_HALO, CONV_DIM), lambda i: (jnp.maximum(i * per - 1, 0), COL_XBC // CONV_DIM))
    return cur, prev


def conv_forward(proj, w, b):
    t = proj.shape[0]
    tm = _pick(t, (256, 128, 64))

    def body(cur_ref, prev_ref, w_ref, b_ref, o_ref):
        prev = jnp.where(pl.program_id(0) == 0, 0.0, prev_ref[...])
        pre, _ = _conv_pre(cur_ref[...], prev, w_ref[...], b_ref[...], tm)
        o_ref[...] = silu(pre)

    cur, prev = _conv_specs(t, tm)
    return pl.pallas_call(
        body, name="conv_forward", grid=(t // tm,),
        in_specs=[cur, prev, pl.BlockSpec((4, CONV_DIM), lambda i: (0, 0)), pl.BlockSpec((1, CONV_DIM), lambda i: (0, 0))],
        out_specs=pl.BlockSpec((tm, CONV_DIM), lambda i: (i, 0)),
        out_shape=jax.ShapeDtypeStruct((t, CONV_DIM), F32),
        compiler_params=_params(("arbitrary",)),
    )(proj, proj, w, b)


def conv_backward_pre(proj, dx, db_, dc_, w, b):
    t = proj.shape[0]
    tm = _pick(t, (256, 128, 64))

    def body(cur_ref, prev_ref, dx_ref, dbm_ref, dcm_ref, w_ref, b_ref, dpre_ref, dw_ref, dbias_ref):
        @pl.when(pl.program_id(0) == 0)
        def _():
            dw_ref[...] = jnp.zeros_like(dw_ref)
            dbias_ref[...] = jnp.zeros_like(dbias_ref)

        prev = jnp.where(pl.program_id(0) == 0, 0.0, prev_ref[...])
        pre, taps = _conv_pre(cur_ref[...], prev, w_ref[...], b_ref[...], tm)
        sg = sigmoid(pre)
        d_out = jnp.concatenate([dx_ref[...], dbm_ref[...], dcm_ref[...]], axis=1)
        dpre = d_out * (sg * (1.0 + pre * (1.0 - sg)))
        dpre_ref[...] = dpre
        dbias_ref[...] += jnp.sum(dpre, axis=0, keepdims=True)
        for j in range(4):
            dw_ref[j:j + 1, :] += jnp.sum(dpre * taps[j], axis=0, keepdims=True)

    cur, prev = _conv_specs(t, tm)
    row = lambda w_: pl.BlockSpec((tm, w_), lambda i: (i, 0))
    return pl.pallas_call(
        body, name="conv_backward_pre", grid=(t // tm,),
        in_specs=[cur, prev, row(B_INNER), row(512), row(512),
                  pl.BlockSpec((4, CONV_DIM), lambda i: (0, 0)), pl.BlockSpec((1, CONV_DIM), lambda i: (0, 0))],
        out_specs=[row(CONV_DIM), pl.BlockSpec((4, CONV_DIM), lambda i: (0, 0)), pl.BlockSpec((1, CONV_DIM), lambda i: (0, 0))],
        out_shape=[jax.ShapeDtypeStruct((t, CONV_DIM), F32), jax.ShapeDtypeStruct((4, CONV_DIM), F32),
                   jax.ShapeDtypeStruct((1, CONV_DIM), F32)],
        compiler_params=_params(("arbitrary",)),
    )(proj, proj, dx, db_, dc_, w, b)


def conv_backward_input(dpre, w, d_proj):
    t = dpre.shape[0]
    tm = _pick(t, (256, 128, 64))
    per = tm // CONV_HALO
    last = t // CONV_HALO - 1
    nt = t // tm

    def body(cur_ref, nxt_ref, w_ref, _, o_ref):
        nxt = jnp.where(pl.program_id(0) == nt - 1, 0.0, nxt_ref[...])
        stacked = jnp.concatenate([cur_ref[...], nxt], axis=0)
        w_ = w_ref[...]
        acc = stacked[0:tm] * w_[3:4]
        for j in range(3):
            s = 3 - j
            acc = acc + pltpu.roll(stacked, tm + CONV_HALO - s, 0)[0:tm] * w_[j:j + 1]
        o_ref[...] = acc.astype(o_ref.dtype)

    return pl.pallas_call(
        body, name="conv_backward_input", grid=(nt,),
        in_specs=[pl.BlockSpec((tm, CONV_DIM), lambda i: (i, 0)),
                  pl.BlockSpec((CONV_HALO, CONV_DIM), lambda i: (jnp.minimum((i + 1) * per, last), 0)),
                  pl.BlockSpec((4, CONV_DIM), lambda i: (0, 0)), pl.BlockSpec(memory_space=pl.ANY)],
        out_specs=pl.BlockSpec((tm, CONV_DIM), lambda i: (i, COL_XBC // CONV_DIM)),
        out_shape=jax.ShapeDtypeStruct(d_proj.shape, d_proj.dtype),
        input_output_aliases={3: 0},
        compiler_params=_params(("arbitrary",)),
    )(dpre, dpre, w, d_proj)


def stage_modulate(x, sc, sh):
    return _ln(x) * (1.0 + sc) + sh


def stage_merge(ga, gb, ya, yb):
    return sigmoid(ga) * ya + sigmoid(gb) * yb


def stage_post_mixer(x, h, g1, ln_g, ln_b, sc2, sh2):
    x1 = _ln(ALPHA * x + g1 * h) * ln_g + ln_b
    return x1, _ln(x1) * (1.0 + sc2) + sh2


def stage_swiglu(a, b):
    return silu(a) * b


def stage_loss(x1, hf, tgt, g2, ln_g, ln_b):
    x2 = _ln(ALPHA * x1 + g2 * hf) * ln_g + ln_b
    return 0.5 * jnp.sum(jnp.mean(jnp.square(x2 - tgt), axis=-1, keepdims=True), axis=0, keepdims=True)


def local_step(x, tgt, mod, wts, small, early=None, late=None, last=None):
    sh1, sc1, g1, sh2, sc2, g2 = mod
    lb, gn, conv_w, conv_b, dtb, alog, dsk, nw, ln1_g, ln1_b, ln2_g, ln2_b = small
    vec = (1, D)

    (u1,) = rowwise("modulate1", lambda r, c: ((stage_modulate(r[0], *c),), ()), [_full(x)], [sc1, sh1], [(D, BF16)])
    w_in = wts.input_projection(u1)
    proj = matmul(u1, w_in, "nn", F32, "in_proj")
    ya_in, st_a = hgrn_forward(proj, lb, gn + wts.start_rest(proj)[0:1])
    xc = conv_forward(proj, conv_w, conv_b)
    w_a, w_b, w_o, w_gu, w_d = wts.rest(xc)
    yb_in, st_b = ssd_forward(xc, proj, dtb, alog, dsk, nw)
    ya = matmul(ya_in, w_a, "nn", F32, "branch_a")
    yb = matmul(yb_in, w_b, "nn", F32, "branch_b")
    gate_rows = [(proj, D, COL_GA // D), (proj, D, COL_GB // D), _full(ya), _full(yb)]
    (merged,) = rowwise("merge", lambda r, c: ((stage_merge(*r),), ()), gate_rows, [], [(D, BF16)])
    h = matmul(merged, w_o, "nn", F32, "out_proj")
    post_consts = [g1, ln1_g, ln1_b, sc2, sh2]
    x1, u2 = rowwise("post_mixer", lambda r, c: (stage_post_mixer(*r, *c), ()), [_full(x), _full(h)], post_consts,
                     [(D, F32), (D, BF16)])
    ab = matmul(u2, w_gu, "nt", F32, "ffn_in")
    (p,) = rowwise("swiglu", lambda r, c: ((stage_swiglu(*r),), ()), [(ab, D_FF, 0), (ab, D_FF, 1)], [], [(D_FF, BF16)])
    hf = matmul(p, w_d, "nn", F32, "ffn_out")

    def loss_bwd(r, c):
        loss, vjp = jax.vjp(stage_loss, *r, *c)
        dx1, dhf, _, dg2, dlg, dlb_ = vjp(jnp.ones((1, 1), F32))
        return (dx1, dhf), (loss, dg2, dlg, dlb_)

    dx1, dhf, loss, dg2, dln2_g, dln2_b = rowwise(
        "loss_backward", loss_bwd, [_full(x1), _full(hf), _full(tgt)], [g2, ln2_g, ln2_b],
        [(D, F32), (D, BF16)], [(1, 1), vec, vec, vec])
    dp = matmul(dhf, w_d, "nt", F32, "ffn_out_dx")
    dw_d = matmul(p, dhf, "tn", F32, "ffn_out_dw")

    def swiglu_bwd(r, c):
        _, vjp = jax.vjp(stage_swiglu, r[0], r[1])
        da, db_ = vjp(r[2])
        return (jnp.concatenate([da, db_], axis=1),), ()

    (dab,) = rowwise("swiglu_backward", swiglu_bwd, [(ab, D_FF, 0), (ab, D_FF, 1), _full(dp)], [], [(2 * D_FF, BF16)])
    du2 = matmul(dab, w_gu, "nn", F32, "ffn_in_dx")
    dw_gu = matmul(dab, u2, "tn", F32, "ffn_in_dw")

    def post_bwd(r, c):
        _, vjp = jax.vjp(stage_post_mixer, r[0], r[1], *c)
        dx, dh, *dc = vjp((r[2], r[3]))
        return (dx, dh), tuple(dc)

    dx_a, dh, dg1, dln1_g, dln1_b, dsc2, dsh2 = rowwise(
        "post_mixer_backward", post_bwd, [_full(x), _full(h), _full(dx1), _full(du2)], post_consts,
        [(D, F32), (D, BF16)], [vec] * 5)
    dmerged = matmul(dh, w_o, "nt", F32, "out_proj_dx")
    dw_o = matmul(merged, dh, "tn", F32, "out_proj_dw")

    def merge_bwd(r, c):
        _, vjp = jax.vjp(stage_merge, *r[:4])
        dga, dgb, dya, dyb = vjp(r[4])
        return (jnp.concatenate([dga, dgb], axis=1), dya, dyb), ()

    dproj, dya, dyb = rowwise("merge_backward", merge_bwd, gate_rows + [_full(dmerged)], [],
                              [(2 * D, BF16), (D, BF16), (D, BF16)], new_wide=(IN_PAD, COL_GA // (2 * D)))
    dya_in = matmul(dya, w_a, "nt", F32, "branch_a_dx")
    dw_a = matmul(ya_in, dya, "tn", F32, "branch_a_dw")
    dyb_in = matmul(dyb, w_b, "nt", F32, "branch_b_dx")
    dw_b = matmul(yb_in, dyb, "tn", F32, "branch_b_dw")
    gn_after = gn if early is None else gn + early((dw_a, dw_b, dw_o, dw_gu, dw_d))[0:1]
    dproj, dlb, dgn = hgrn_backward(proj, st_a, dya_in, lb, gn_after, dproj)
    dxs, dbm, dcm, ddt, dproj, ddtb, dalog, ddsk, dnw = ssd_backward(xc, proj, st_b, dyb_in, dtb, alog, dsk, nw, dproj)
    dpre, dconv_w, dconv_b = conv_backward_pre(proj, dxs, dbm, dcm, conv_w, conv_b)
    if late is not None:
        late(dconv_b)
    dproj = conv_backward_input(dpre, conv_w, dproj)
    t = x.shape[0]
    tail = jnp.concatenate([jnp.sum(ddt, axis=0).astype(BF16), jnp.zeros((t, IN_PAD - COL_DT - LANES), BF16)], axis=1)
    dproj = lax.dynamic_update_slice(dproj, tail, (0, COL_DT))
    dw_in = matmul(u1, dproj, "tn", F32, "in_proj_dw")
    du1 = matmul(dproj, w_in, "nt", F32, "in_proj_dx", after=None if last is None else last(dw_in))

    def mod_bwd(r, c):
        _, vjp = jax.vjp(stage_modulate, r[0], *c)
        dx, dsc, dsh = vjp(r[1])
        return (dx + r[2],), (dsc, dsh)

    grad_x, dsc1, dsh1 = rowwise("modulate1_backward", mod_bwd, [_full(x), _full(du1), _full(dx_a)], [sc1, sh1],
                                 [(D, F32)], [vec, vec])
    d_mod = (dsh1, dsc1, dg1, dsh2, dsc2, dg2)
    d_wts = (dw_in, dw_a, dw_b, dw_o, dw_gu, dw_d)
    d_small = (dlb, dgn, dconv_w, dconv_b, ddtb.reshape(1, B_INNER),
               dalog.reshape(1, B_INNER), ddsk.reshape(1, B_INNER), dnw.reshape(1, B_INNER),
               dln1_g, dln1_b, dln2_g, dln2_b)
    return loss, grad_x, d_mod, d_wts, d_small


HBM = pl.BlockSpec(memory_space=pltpu.HBM)
SEM = pl.BlockSpec(memory_space=pltpu.SEMAPHORE)
DATAFLOW = pltpu.SideEffectType.DATAFLOW_SIDE_EFFECTING


def _place():
    return lax.axis_index("x"), lax.axis_index("y"), lax.axis_index("c")


def _other_chips(x, y):
    return [(1 - x, y), (x, 1 - y), (1 - x, 1 - y)]


def _remote(src, dst, send_sem, recv_sem, device):
    return pltpu.make_async_remote_copy(src_ref=src, dst_ref=dst, send_sem=send_sem, recv_sem=recv_sem,
                                        device_id=device, device_id_type=MESH)


def gather_rows(v, name):
    n = v.shape[1]

    def body(v_ref, out_ref, send_sems, recv_sems, local_sem):
        x, y, c = _place()
        mine = pltpu.make_async_copy(v_ref, out_ref.at[4 * x + 2 * y + c], local_sem)
        mine.start()
        sends, recvs = [], []
        for m in range(1, 8):
            px = 1 - x if m & 4 else x
            py = 1 - y if m & 2 else y
            pc = 1 - c if m & 1 else c
            sends.append(_remote(v_ref, out_ref.at[4 * x + 2 * y + c], send_sems.at[m - 1], recv_sems.at[m - 1], (px, py, pc)))
            recvs.append(_remote(v_ref, out_ref.at[4 * px + 2 * py + pc], send_sems.at[m - 1], recv_sems.at[m - 1], (px, py, pc)))
        for cp in sends:
            cp.start()
        for cp in recvs:
            cp.wait_recv()
        for cp in sends:
            cp.wait_send()
        mine.wait()

    return pl.pallas_call(
        body, name=name, in_specs=[HBM], out_specs=HBM,
        out_shape=jax.ShapeDtypeStruct((8, 1, n), v.dtype),
        scratch_shapes=[pltpu.SemaphoreType.DMA((7,)), pltpu.SemaphoreType.DMA((7,)), pltpu.SemaphoreType.DMA],
    )(v)


def exchange_rows(part, name):
    w = part.shape[2]

    def body(p_ref, out_ref, send_sems, recv_sems, local_sem):
        x, y, c = _place()
        k = 2 * x + y
        mine = pltpu.make_async_copy(p_ref.at[4 * x + 2 * y + c], out_ref.at[k], local_sem)
        mine.start()
        sends, recvs = [], []
        for j, (px, py) in enumerate(_other_chips(x, y)):
            sends.append(_remote(p_ref.at[4 * px + 2 * py + c], out_ref.at[k], send_sems.at[j], recv_sems.at[j], (px, py, c)))
            recvs.append(_remote(p_ref.at[4 * px + 2 * py + c], out_ref.at[2 * px + py], send_sems.at[j], recv_sems.at[j], (px, py, c)))
        for cp in sends:
            cp.start()
        for cp in recvs:
            cp.wait_recv()
        for cp in sends:
            cp.wait_send()
        mine.wait()

    return pl.pallas_call(
        body, name=name, in_specs=[HBM], out_specs=HBM,
        out_shape=jax.ShapeDtypeStruct((4, 1, w), part.dtype),
        scratch_shapes=[pltpu.SemaphoreType.DMA((3,)), pltpu.SemaphoreType.DMA((3,)), pltpu.SemaphoreType.DMA],
    )(part)


def _half_of_slot(ref, rows, px, py, pc):
    return ref.at[2 * px + py, pl.ds(pc * (rows // 2), rows // 2), :]


def gather_start(shards, after):
    n = len(shards)

    def body(*refs):
        w_refs, land_refs = refs[:n], refs[n:2 * n]
        send_a, recv_a, send_b, recv_b = refs[2 * n + 1:2 * n + 5]
        token = refs[-1]
        x, y, c = _place()
        for i in range(n):
            rows = shards[i].shape[0]
            for j, (px, py) in enumerate(_other_chips(x, y)):
                sems = (send_a.at[j], recv_a.at[j]) if i == 0 else (send_b.at[j * (n - 1) + i - 1], recv_b.at[j * (n - 1) + i - 1])
                _remote(w_refs[i].at[pl.ds(c * (rows // 2), rows // 2), :], _half_of_slot(land_refs[i], rows, x, y, c),
                        *sems, (px, py, c)).start()
        token[...] = jnp.zeros_like(token)

    hbm = lambda a: pltpu.with_memory_space_constraint(a, pltpu.HBM)
    lands = [lax.empty((4,) + s.shape, s.dtype) for s in shards]
    dma = pltpu.SemaphoreType.DMA
    return pl.pallas_call(
        body, name="gather_start",
        out_shape=(dma((3,)), dma((3,)), dma((3 * (n - 1),)), dma((3 * (n - 1),)),
                   *[pltpu.HBM(a.shape, a.dtype) for a in list(shards) + lands], jax.ShapeDtypeStruct((8, LANES), F32)),
        in_specs=[HBM] * (2 * n) + [pl.BlockSpec(memory_space=pl.ANY)],
        out_specs=(SEM, SEM, SEM, SEM, *[HBM] * (2 * n), pl.BlockSpec(memory_space=pltpu.VMEM)),
        input_output_aliases={i: 4 + i for i in range(2 * n)},
        compiler_params=pltpu.CompilerParams(has_side_effects=DATAFLOW),
    )(*[hbm(a) for a in list(shards) + lands], after)


def gather_wait(send_sems, recv_sems, shards, lands, after, tag):
    n = len(shards)

    def body(*refs):
        w_refs, land_refs = refs[:n], refs[n:2 * n]
        send_ref, recv_ref = refs[2 * n], refs[2 * n + 1]
        x, y, c = _place()
        for i in range(n):
            rows = shards[i].shape[0]
            for j, (px, py) in enumerate(_other_chips(x, y)):
                cp = _remote(w_refs[i].at[pl.ds(c * (rows // 2), rows // 2), :], _half_of_slot(land_refs[i], rows, px, py, c),
                             send_ref.at[j * n + i], recv_ref.at[j * n + i], (px, py, c))
                cp.wait_send()
                cp.wait_recv()

    out = pl.pallas_call(
        body, name="gather_wait_" + tag,
        out_shape=tuple(pltpu.HBM(a.shape, a.dtype) for a in list(shards) + list(lands)),
        in_specs=[HBM] * (2 * n) + [SEM, SEM, pl.BlockSpec(memory_space=pl.ANY)], out_specs=tuple([HBM] * (2 * n)),
        input_output_aliases={i: i for i in range(2 * n)},
        compiler_params=pltpu.CompilerParams(has_side_effects=DATAFLOW),
    )(*shards, *lands, send_sems, recv_sems, after)
    return list(out[n:])


def forward_start(lands, tag):
    n = len(lands)

    def body(*refs):
        land_refs = refs[:n]
        send_sems, recv_sems = refs[n], refs[n + 1]
        token = refs[-1]
        x, y, c = _place()
        for i in range(n):
            rows = lands[i].shape[1]
            for j, (px, py) in enumerate(_other_chips(x, y)):
                mine = _half_of_slot(land_refs[i], rows, px, py, c)
                _remote(mine, mine, send_sems.at[j * n + i], recv_sems.at[j * n + i], (x, y, 1 - c)).start()
        token[...] = jnp.zeros_like(token)

    dma = pltpu.SemaphoreType.DMA
    return pl.pallas_call(
        body, name="forward_start_" + tag,
        out_shape=(dma((3 * n,)), dma((3 * n,)), *[pltpu.HBM(a.shape, a.dtype) for a in lands],
                   jax.ShapeDtypeStruct((8, LANES), F32)),
        in_specs=[HBM] * n, out_specs=(SEM, SEM, *[HBM] * n, pl.BlockSpec(memory_space=pltpu.VMEM)),
        input_output_aliases={i: 2 + i for i in range(n)},
        compiler_params=pltpu.CompilerParams(has_side_effects=DATAFLOW),
    )(*lands)


def forward_wait(started, after, tag):
    send_sems, recv_sems, *rest = started
    lands = rest[:-1]
    n = len(lands)

    def body(*refs):
        land_refs = refs[:n]
        send_ref, recv_ref = refs[n], refs[n + 1]
        x, y, c = _place()
        for i in range(n):
            rows = lands[i].shape[1]
            for j, (px, py) in enumerate(_other_chips(x, y)):
                cp = _remote(_half_of_slot(land_refs[i], rows, px, py, c), _half_of_slot(land_refs[i], rows, px, py, 1 - c),
                             send_ref.at[j * n + i], recv_ref.at[j * n + i], (x, y, 1 - c))
                cp.wait_send()
                cp.wait_recv()

    out = pl.pallas_call(
        body, name="forward_wait_" + tag,
        out_shape=tuple(pltpu.HBM(a.shape, a.dtype) for a in lands),
        in_specs=[HBM] * n + [SEM, SEM, pl.BlockSpec(memory_space=pl.ANY)], out_specs=tuple([HBM] * n),
        input_output_aliases={i: i for i in range(n)},
        compiler_params=pltpu.CompilerParams(has_side_effects=DATAFLOW),
    )(*lands, send_sems, recv_sems, after)
    return list(out)


def pair_exchange(slabs, name):
    n = len(slabs)

    def body(*refs):
        g_refs, out_refs = refs[:n], refs[n:2 * n]
        send_sems, recv_sems = refs[2 * n:]
        x, y, c = _place()
        copies = []
        for i in range(n):
            hr = slabs[i].shape[1] // 2
            cp = _remote(g_refs[i].at[:, pl.ds((1 - c) * hr, hr), :], out_refs[i], send_sems.at[i], recv_sems.at[i], (x, y, 1 - c))
            cp.start()
            copies.append(cp)
        for cp in copies:
            cp.wait()

    return pl.pallas_call(
        body, name=name, in_specs=[HBM] * n, out_specs=[HBM] * n,
        out_shape=[jax.ShapeDtypeStruct((4, s.shape[1] // 2, s.shape[2]), s.dtype) for s in slabs],
        scratch_shapes=[pltpu.SemaphoreType.DMA((n,)), pltpu.SemaphoreType.DMA((n,))],
    )(*slabs)


def _tile2(rows, cols):
    fits = lambda r, c: r * c * 4 <= BLOCK_BYTES
    if fits(rows, cols):
        return rows, cols
    for r in (1024, 512, 256, 128, 64):
        if rows % r == 0 and fits(r, cols):
            return r, cols
    return rows, next(cols // k for k in (2, 3, 4, 6, 8, 12, 16) if cols % (k * LANES) == 0 and fits(rows, cols // k))


def pair_add(g, p, c, name):
    _, hr, cols = p.shape
    tm, tc = _tile2(hr, cols)
    per = hr // tm

    def body(c_ref, g_ref, p_ref, o_ref):
        o_ref[...] = (g_ref[...] + p_ref[...]).astype(o_ref.dtype)

    return pl.pallas_call(
        body, name=name,
        grid_spec=pltpu.PrefetchScalarGridSpec(
            num_scalar_prefetch=1, grid=(4, per, cols // tc),
            in_specs=[pl.BlockSpec((None, tm, tc), lambda k, i, j, c_ref: (k, c_ref[0] * per + i, j)),
                      pl.BlockSpec((None, tm, tc), lambda k, i, j, c_ref: (k, i, j))],
            out_specs=pl.BlockSpec((None, tm, tc), lambda k, i, j, c_ref: (k, i, j))),
        out_shape=jax.ShapeDtypeStruct((4, hr, cols), BF16),
        compiler_params=_params(("arbitrary", "arbitrary", "arbitrary")),
    )(c.reshape(1).astype(jnp.int32), g, p)


def scatter_start(sums, tag):
    n = len(sums)

    def body(*refs):
        s_refs, land_refs = refs[:n], refs[n:2 * n]
        send_sems, recv_sems = refs[2 * n], refs[2 * n + 1]
        token = refs[-1]
        x, y, c = _place()
        k = 2 * x + y
        for i in range(n):
            for j, (px, py) in enumerate(_other_chips(x, y)):
                _remote(s_refs[i].at[2 * px + py], land_refs[i].at[k], send_sems.at[j * n + i], recv_sems.at[j * n + i],
                        (px, py, c)).start()
        token[...] = jnp.zeros_like(token)

    hbm = lambda a: pltpu.with_memory_space_constraint(a, pltpu.HBM)
    return pl.pallas_call(
        body, name="scatter_start_" + tag,
        out_shape=(pltpu.SemaphoreType.DMA((3 * n,)), pltpu.SemaphoreType.DMA((3 * n,)),
                   *[pltpu.HBM(s.shape, s.dtype) for s in sums], *[pltpu.HBM(s.shape, s.dtype) for s in sums],
                   jax.ShapeDtypeStruct((8, LANES), F32)),
        in_specs=[HBM] * (2 * n), out_specs=(SEM, SEM, *[HBM] * (2 * n), pl.BlockSpec(memory_space=pltpu.VMEM)),
        input_output_aliases={i: 2 + i for i in range(2 * n)},
        compiler_params=pltpu.CompilerParams(has_side_effects=DATAFLOW),
    )(*[hbm(s) for s in sums], *[hbm(lax.empty(s.shape, s.dtype)) for s in sums])


def scatter_wait(started, after, tag):
    send_sems, recv_sems, *rest = started
    n = (len(rest) - 1) // 2
    sums, lands = rest[:n], rest[n:2 * n]

    def body(*refs):
        s_refs, land_refs = refs[:n], refs[n:2 * n]
        send_ref, recv_ref = refs[2 * n], refs[2 * n + 1]
        x, y, c = _place()
        for i in range(n):
            for j, (px, py) in enumerate(_other_chips(x, y)):
                cp = _remote(s_refs[i].at[2 * px + py], land_refs[i].at[2 * px + py], send_ref.at[j * n + i],
                             recv_ref.at[j * n + i], (px, py, c))
                cp.wait_send()
                cp.wait_recv()

    out = pl.pallas_call(
        body, name="scatter_wait_" + tag,
        out_shape=tuple(pltpu.HBM(s.shape, s.dtype) for s in sums + lands),
        in_specs=[HBM] * (2 * n) + [SEM, SEM, pl.BlockSpec(memory_space=pl.ANY)], out_specs=tuple([HBM] * (2 * n)),
        input_output_aliases={i: i for i in range(2 * n)},
        compiler_params=pltpu.CompilerParams(has_side_effects=DATAFLOW),
    )(*sums, *lands, send_sems, recv_sems, after)
    return list(out[n:])


def sum_chips(landed, own, chip, core, name):
    _, hr, cols = landed.shape
    tm, tc = _tile2(hr, cols)
    per = hr // tm

    def body(idx_ref, l0, l1, l2, l3, own_ref, o_ref):
        mine = own_ref[...].astype(F32)
        v = [jnp.where(idx_ref[0] == k, mine, ref[...].astype(F32)) for k, ref in enumerate((l0, l1, l2, l3))]
        o_ref[...] = ((v[0] + v[1]) + v[2]) + v[3]

    slot = lambda k: pl.BlockSpec((None, tm, tc),
                                  lambda i, j, idx: (jnp.where(idx[0] == k, (k + 1) & 3, k), i, j))
    return pl.pallas_call(
        body, name=name,
        grid_spec=pltpu.PrefetchScalarGridSpec(
            num_scalar_prefetch=1, grid=(per, cols // tc),
            in_specs=[slot(0), slot(1), slot(2), slot(3),
                      pl.BlockSpec((None, tm, tc), lambda i, j, idx: (idx[0], i, j))],
            out_specs=pl.BlockSpec((tm, tc), lambda i, j, idx: (idx[1] * per + i, j))),
        out_shape=jax.ShapeDtypeStruct((2 * hr, cols), F32),
        compiler_params=_params(("arbitrary", "arbitrary")),
    )(jnp.stack([chip, core]).astype(jnp.int32), landed, landed, landed, landed, own)


def exchange_halves(bufs):
    n = len(bufs)

    def body(*refs):
        out_refs = refs[n:2 * n]
        send_sems, recv_sems = refs[2 * n:]
        x, y, c = _place()
        sends, recvs = [], []
        for i in range(n):
            hr = bufs[i].shape[0] // 2
            own = out_refs[i].at[pl.ds(c * hr, hr), :]
            other = out_refs[i].at[pl.ds((1 - c) * hr, hr), :]
            sends.append(_remote(own, own, send_sems.at[i], recv_sems.at[i], (x, y, 1 - c)))
            recvs.append(_remote(other, other, send_sems.at[i], recv_sems.at[i], (x, y, 1 - c)))
        for cp in sends:
            cp.start()
        for cp in recvs:
            cp.wait_recv()
        for cp in sends:
            cp.wait_send()

    return pl.pallas_call(
        body, name="exchange_halves", in_specs=[HBM] * n, out_specs=[HBM] * n,
        out_shape=[jax.ShapeDtypeStruct(b.shape, b.dtype) for b in bufs],
        input_output_aliases={i: i for i in range(n)},
        scratch_shapes=[pltpu.SemaphoreType.DMA((n,)), pltpu.SemaphoreType.DMA((n,))],
    )(*bufs)


def _relayout(name, arrays, in_blocks, out_blocks, out_shapes, fn):
    rows = 128
    spec = lambda blk: pl.BlockSpec(blk, (lambda i: (0, i, 0)) if len(blk) == 3 else (lambda i: (i, 0)))

    def body(*refs):
        n_in = len(arrays)
        outs = fn(*[r[...] for r in refs[:n_in]])
        for ref, val in zip(refs[n_in:], outs, strict=True):
            if isinstance(val, list):
                for k, piece in enumerate(val):
                    ref[k] = piece
            else:
                ref[...] = val

    return pl.pallas_call(
        body, name=name, grid=(D // rows,),
        in_specs=[spec(b) for b in in_blocks], out_specs=[spec(b) for b in out_blocks], out_shape=out_shapes,
        compiler_params=_params(("arbitrary",)),
    )(*arrays)


def assemble_in_proj(g):
    def fn(v):
        w = jnp.concatenate([v[k] for k in range(4)], axis=1)
        return (jnp.concatenate([w[:, :ORIG_Z], w[:, ORIG_GA:], w[:, ORIG_XBC:ORIG_DT], w[:, ORIG_Z:ORIG_XBC],
                                 w[:, ORIG_DT:ORIG_GA], jnp.zeros((w.shape[0], IN_PAD - IN_ORIG), w.dtype)], axis=1),)

    cols = g.shape[2]
    return _relayout("assemble_in_proj", [g], [(4, 128, cols)], [(128, IN_PAD)],
                     [jax.ShapeDtypeStruct((D, IN_PAD), g.dtype)], fn)[0]


def split_in_proj(dw):
    cols = IN_ORIG // 4

    def fn(d):
        w = jnp.concatenate([d[:, :COL_GA], d[:, COL_Z:COL_DT], d[:, COL_XBC:COL_Z], d[:, COL_DT:COL_DT + 32],
                             d[:, COL_GA:COL_XBC]], axis=1)
        return ([w[:, k * cols:(k + 1) * cols] for k in range(4)],)

    return _relayout("split_in_proj", [dw], [(128, IN_PAD)], [(4, 128, cols)],
                     [jax.ShapeDtypeStruct((4, D, cols), dw.dtype)], fn)[0]


def ada_prepare(c_all, w_ada, hgrn_lb):
    def body(c_ref, w_ref, lb_ref, mod_ref, row_ref):
        mod_ref[...] = hdot(silu(c_ref[...]), w_ref[...])
        row_ref[...] = sigmoid(lb_ref[0:1, :] - lb_ref[1:2, :])

    return pl.pallas_call(
        body, name="ada_prepare",
        out_shape=[jax.ShapeDtypeStruct((8, w_ada.shape[1]), F32), jax.ShapeDtypeStruct((1, D), F32)],
        compiler_params=pltpu.CompilerParams(vmem_limit_bytes=VMEM_LIMIT),
    )(c_all, w_ada, hgrn_lb)


SMALL_SEGS = (("mod", 6 * D), ("lb", D), ("gnorm", LANES), ("conv_w", 4 * CONV_DIM), ("conv_b", CONV_DIM),
              ("dt_bias", B_INNER), ("a_log", B_INNER), ("d", B_INNER), ("ssm_norm", B_INNER),
              ("ln1_g", D), ("ln1_b", D), ("ln2_g", D), ("ln2_b", D))
SMALL_PARAMS = ("b_ada", "hgrn_lb", "hgrn_gnorm", "ssm_conv_b", "ssm_dt_bias", "ssm_a_log", "ssm_d", "ssm_norm",
                "ln1_g", "ln1_b", "ln2_g", "ln2_b")


def finalize_small(g_all, c_all, dmod_cols, params, m, v):
    n_p = len(SMALL_PARAMS)
    offs, o = {}, 0
    for nm, width in SMALL_SEGS:
        offs[nm] = (o, width)
        o += width

    def body(*refs):
        g_ref, c_ref, dm_ref = refs[:3]
        p_refs = refs[3:3 + n_p]
        m_refs = refs[3 + n_p:3 + 2 * n_p]
        v_refs = refs[3 + 2 * n_p:3 + 3 * n_p]
        outs = refs[3 + 3 * n_p:]
        gwa_ref, gcw_ref = outs[:2]
        res = outs[2:]
        total = jnp.sum(g_ref[...], axis=0, keepdims=True)
        seg = lambda nm: total[:, offs[nm][0]:offs[nm][0] + offs[nm][1]]
        gwa_ref[...] = hdot(silu(c_ref[...]), dm_ref[...], "tn")
        cw = seg("conv_w")
        for j in range(4):
            gcw_ref[j:j + 1, :] = cw[:, j * CONV_DIM:(j + 1) * CONV_DIM]
        hc = lax.broadcasted_iota(jnp.int32, (B_INNER, LANES), 0)
        hj = lax.broadcasted_iota(jnp.int32, (B_INNER, LANES), 1)
        per_head = ((hc >> 6) == hj).astype(F32)
        heads = lambda nm: hdot(jnp.broadcast_to(seg(nm), (8, B_INNER)), per_head)[0:1, 0:32]
        lbp = sigmoid(p_refs[1][0:1, :] - p_refs[1][1:2, :])
        g_row = seg("lb") * lbp * (1.0 - lbp)
        grads = {"b_ada": seg("mod"), "hgrn_gnorm": seg("gnorm"), "ssm_conv_b": seg("conv_b"),
                 "ssm_dt_bias": heads("dt_bias"), "ssm_a_log": heads("a_log"), "ssm_d": heads("d"),
                 "ssm_norm": seg("ssm_norm"), "ln1_g": seg("ln1_g"), "ln1_b": seg("ln1_b"),
                 "ln2_g": seg("ln2_g"), "ln2_b": seg("ln2_b")}
        for i, nm in enumerate(SMALL_PARAMS):
            g_out, d_out, m_out, v_out = res[4 * i:4 * i + 4]
            if nm == "hgrn_lb":
                for row, gv in ((0, g_row), (1, -g_row)):
                    sl = slice(row, row + 1)
                    dl, mn, vn = adamw(p_refs[i][sl, :], gv, m_refs[i][sl, :], v_refs[i][sl, :])
                    g_out[sl, :], d_out[sl, :], m_out[sl, :], v_out[sl, :] = gv, dl, mn, vn
            else:
                gv = grads[nm]
                dl, mn, vn = adamw(p_refs[i][...], gv, m_refs[i][...], v_refs[i][...])
                g_out[...], d_out[...], m_out[...], v_out[...] = gv, dl, mn, vn

    out_shape = [jax.ShapeDtypeStruct((D, dmod_cols.shape[1]), F32), jax.ShapeDtypeStruct((4, CONV_DIM), F32)]
    for p in params:
        out_shape += [jax.ShapeDtypeStruct(p.shape, F32)] * 4
    return pl.pallas_call(
        body, name="finalize_small", out_shape=out_shape,
        compiler_params=pltpu.CompilerParams(vmem_limit_bytes=VMEM_LIMIT),
    )(g_all, c_all, dmod_cols, *params, *m, *v)


def adam_update(w, g, m, v, name):
    rows, cols = w.shape
    tm, tc = _tile2(rows, cols)

    def body(w_ref, g_ref, m_ref, v_ref, d_ref, mo_ref, vo_ref):
        d_ref[...], mo_ref[...], vo_ref[...] = adamw(w_ref[...], g_ref[...], m_ref[...], v_ref[...])

    spec = pl.BlockSpec((tm, tc), lambda i, j: (i, j))
    return pl.pallas_call(
        body, name=name, grid=(rows // tm, cols // tc), in_specs=[spec] * 4, out_specs=[spec] * 3,
        out_shape=[jax.ShapeDtypeStruct((rows, cols), F32)] * 3,
        compiler_params=_params(("arbitrary", "arbitrary")),
    )(w, g, m, v)


def kernel(x, c, w_ada, b_ada, w_in, hgrn_lb, hgrn_gnorm, ssm_conv_w, ssm_conv_b, ssm_dt_bias, ssm_a_log, ssm_d, ssm_norm, w_branch_a, w_branch_b, w_o, ln1_g, ln1_b, w_ffn_gate, w_ffn_up, w_ffn_down, ln2_g, ln2_b, loss_target, m_w_ada, m_b_ada, m_w_in, m_hgrn_lb, m_hgrn_gnorm, m_ssm_conv_w, m_ssm_conv_b, m_ssm_dt_bias, m_ssm_a_log, m_ssm_d, m_ssm_norm, m_w_branch_a, m_w_branch_b, m_w_o, m_ln1_g, m_ln1_b, m_w_ffn_gate, m_w_ffn_up, m_w_ffn_down, m_ln2_g, m_ln2_b, v_w_ada, v_b_ada, v_w_in, v_hgrn_lb, v_hgrn_gnorm, v_ssm_conv_w, v_ssm_conv_b, v_ssm_dt_bias, v_ssm_a_log, v_ssm_d, v_ssm_norm, v_w_branch_a, v_w_branch_b, v_w_o, v_ln1_g, v_ln1_b, v_w_ffn_gate, v_w_ffn_up, v_w_ffn_down, v_ln2_g, v_ln2_b):
    given = dict(locals())
    chip = 2 * lax.axis_index("x") + lax.axis_index("y")
    core = lax.axis_index("c")
    t = x.shape[1]

    first = gather_rows(jnp.concatenate([c, ssm_conv_w.reshape(1, CONV_DIM)], axis=1), "gather_cond").reshape(8, D + CONV_DIM)
    c_all = first[:, :D]
    conv_w = first[0::2, D:].reshape(4, 4, CONV_DIM // 4).transpose(1, 0, 2).reshape(4, CONV_DIM)
    mod_part, lb_row = ada_prepare(c_all, w_ada[0], hgrn_lb)
    mod_cols = w_ada.shape[2]
    mod_row = exchange_rows(mod_part.reshape(8, 1, mod_cols), "exchange_mod").reshape(1, 6 * D) + b_ada
    mod = tuple(mod_row[:, i * D:(i + 1) * D] for i in range(6))

    local = {nm: given[nm][0].T if nm in TRANSPOSED else given[nm][0] for nm in SHARDED}
    shards = [local[nm].astype(BF16) for nm in SHARDED]
    n_w = len(SHARDED)
    send_in, recv_in, send_rest, recv_rest, *flying = gather_start(shards, mod_row)
    sent, lands = flying[:n_w], flying[n_w:2 * n_w]
    with_own = lambda land, shard: lax.dynamic_update_slice(land, shard[None], (chip, 0, 0))

    class Weights:
        def input_projection(self, after):
            land = gather_wait(send_in, recv_in, sent[:1], lands[:1], after, "in")
            (land,) = forward_wait(forward_start(land, "in"), after, "in")
            return assemble_in_proj(with_own(land, shards[0]))

        def start_rest(self, after):
            self.started = forward_start(gather_wait(send_rest, recv_rest, sent[1:], lands[1:], after, "rest"), "rest")
            return self.started[-1]

        def rest(self, after):
            got = {nm: with_own(land, s) for nm, land, s in zip(SHARDED[1:], forward_wait(self.started, after, "rest"), shards[1:], strict=True)}
            whole = lambda nm: got[nm].reshape(4 * got[nm].shape[1], got[nm].shape[2])
            return (whole("w_branch_a"), whole("w_branch_b"), whole("w_o"),
                    jnp.concatenate([whole("w_ffn_gate"), whole("w_ffn_up")], axis=0), whole("w_ffn_down"))

    wts = Weights()

    per_channel = lambda p: jnp.repeat(p[0], B_INNER // 32)[None]
    small = (lb_row, hgrn_gnorm, conv_w, ssm_conv_b, per_channel(ssm_dt_bias), per_channel(ssm_a_log),
             per_channel(ssm_d), ssm_norm, ln1_g, ln1_b, ln2_g, ln2_b)
    by_rows = lambda g: g.reshape(4, g.shape[0] // 4, g.shape[1])
    travelling = {}

    def pair_sums(names, slabs, tag):
        received = pair_exchange(slabs, "pair_exchange_" + tag)
        return [pair_add(s, r, core, "pair_add_" + nm) for nm, s, r in zip(names, slabs, received, strict=True)]

    def start_early(dws):
        dw_a, dw_b, dw_o, dw_gu, dw_d = dws
        d_gate, d_up = by_rows(dw_gu[:D_FF]), by_rows(dw_gu[D_FF:])
        travelling["pairs"] = pair_sums(SHARDED[1:], [by_rows(dw_a), by_rows(dw_b), by_rows(dw_o), d_gate, d_up, by_rows(dw_d)], "early")
        travelling["started"] = scatter_start(travelling["pairs"], "early")
        return travelling["started"][-1]

    def finish_early(after):
        travelling["landed"] = scatter_wait(travelling["started"], after, "early")

    def start_last(dw_in):
        travelling["pairs_in"] = pair_sums(SHARDED[:1], [split_in_proj(dw_in)], "last")
        travelling["started_in"] = scatter_start(travelling["pairs_in"], "last")
        return travelling["started_in"][-1]

    loss, grad_x, d_mod, d_wts, d_small = local_step(x[0], loss_target[0], mod, wts, small,
                                                     start_early, finish_early, start_last)

    d_lb, d_gn, d_cw, d_cb, d_dtb, d_alog, d_dsk, d_nw, d_l1g, d_l1b, d_l2g, d_l2b = d_small
    row = jnp.concatenate(list(d_mod) + [d_lb, d_gn, d_cw.reshape(1, 4 * CONV_DIM), d_cb, d_dtb, d_alog, d_dsk, d_nw,
                                          d_l1g, d_l1b, d_l2g, d_l2b], axis=1)
    g_all = gather_rows(row, "gather_small_grads").reshape(8, row.shape[1])
    dmod_cols = lax.dynamic_slice_in_dim(g_all, chip * mod_cols, mod_cols, axis=1)
    fin = finalize_small(g_all, c_all, dmod_cols, [given[n] for n in SMALL_PARAMS],
                         [given["m_" + n] for n in SMALL_PARAMS], [given["v_" + n] for n in SMALL_PARAMS])
    grads, deltas, new_m, new_v = {}, {}, {}, {}
    grads["w_ada"] = fin[0][None]
    grads["ssm_conv_w"] = lax.dynamic_slice_in_dim(fin[1], chip * (CONV_DIM // 4), CONV_DIM // 4, axis=1)[None]
    for i, nm in enumerate(SMALL_PARAMS):
        grads[nm], deltas[nm], new_m[nm], new_v[nm] = fin[2 + 4 * i:6 + 4 * i]

    pairs = travelling["pairs_in"] + travelling["pairs"]
    landed = scatter_wait(travelling["started_in"], fin[2], "last") + travelling["landed"]
    halves = [sum_chips(r, p, chip, core, "sum_chips_" + nm) for nm, r, p in zip(SHARDED, landed, pairs, strict=True)]
    reduced = dict(zip(SHARDED, exchange_halves(halves), strict=True))
    reduced["w_ada"], reduced["ssm_conv_w"] = grads["w_ada"][0], grads["ssm_conv_w"][0]
    for nm in ("w_ada", "ssm_conv_w") + SHARDED:
        work = (lambda a: a[0].T) if nm in TRANSPOSED else (lambda a: a[0])
        back = (lambda a: a.T[None]) if nm in TRANSPOSED else (lambda a: a[None])
        d_, m_, v_ = adam_update(work(given[nm]), reduced[nm], work(given["m_" + nm]), work(given["v_" + nm]), "adam_" + nm)
        grads[nm], deltas[nm], new_m[nm], new_v[nm] = back(reduced[nm]), back(d_), back(m_), back(v_)

    names = ("w_ada", "b_ada", "w_in", "hgrn_lb", "hgrn_gnorm", "ssm_conv_w", "ssm_conv_b", "ssm_dt_bias", "ssm_a_log",
             "ssm_d", "ssm_norm", "w_branch_a", "w_branch_b", "w_o", "ln1_g", "ln1_b", "w_ffn_gate", "w_ffn_up",
             "w_ffn_down", "ln2_g", "ln2_b")
    total_loss = lax.psum(loss[0, 0], ("x", "y", "c"))
    return (total_loss, grad_x[None], *[grads[n] for n in names], *[deltas[n] for n in names],
            *[new_m[n] for n in names], *[new_v[n] for n in names])
```

```python
import functools

import jax
import jax.numpy as jnp
from jax import lax
from jax.experimental import pallas as pl
from jax.experimental.pallas import tpu as pltpu

F32, BF16 = jnp.float32, jnp.bfloat16
HI = lax.Precision.HIGHEST
MESH = pl.DeviceIdType.MESH

D = 1024
CHUNK = 64
LANES = 128
N_HEADS_A = 8
N_GROUPS_B = 4
B_INNER = 2048
CONV_DIM = 3072
D_FF = 2816
ALPHA = 2.0 ** 0.25
LN_EPS = 1e-5
RMS_EPS = 1e-6
ADAM_LR, ADAM_B1, ADAM_B2, ADAM_EPS, ADAM_WD, ADAM_STEP = 0.001, 0.9, 0.999, 1e-08, 0.01, 10

IN_ORIG = 11296
IN_PAD = 11520
COL_GA, COL_GB, COL_XBC, COL_Z, COL_DT = 4096, 5120, 6144, 9216, 11264
ORIG_Z, ORIG_XBC, ORIG_DT, ORIG_GA = 4096, 6144, 9216, 9248

SHARDED = ("w_in", "w_branch_a", "w_branch_b", "w_o", "w_ffn_gate", "w_ffn_up", "w_ffn_down")
TRANSPOSED = ("w_ffn_gate", "w_ffn_up")
VMEM_LIMIT = 56 * 1024 * 1024
BLOCK_BYTES = 2 * 1024 * 1024
_DIMS = {"nn": (((1,), (0,)), ((), ())), "nt": (((1,), (1,)), ((), ())), "tn": (((0,), (0,)), ((), ()))}


def _bd(a, b, mode):
    return lax.dot_general(a.astype(BF16), b.astype(BF16), _DIMS[mode], preferred_element_type=F32)


@functools.partial(jax.custom_vjp, nondiff_argnums=(2,))
def bdot(a, b, mode):
    return _bd(a, b, mode)


def _bdot_fwd(a, b, mode):
    return _bd(a, b, mode), (a, b)


def _bdot_bwd(mode, res, g):
    a, b = res
    if mode == "nn":
        return _bd(g, b, "nt"), _bd(a, g, "tn")
    if mode == "nt":
        return _bd(g, b, "nn"), _bd(g, a, "tn")
    return _bd(b, g, "nt"), _bd(a, g, "nn")


bdot.defvjp(_bdot_fwd, _bdot_bwd)


def hdot(a, b, mode="nn"):
    return lax.dot_general(a, b, _DIMS[mode], precision=HI, preferred_element_type=F32)


def _raw(a, b, mode):
    return lax.dot_general(a, b, _DIMS[mode], preferred_element_type=F32)


def _split(x, n):
    parts, rest = [], x
    for _ in range(n):
        p = rest.astype(BF16)
        parts.append(p)
        rest = rest - p.astype(F32)
    return parts


def _od(a, b, mode, exact):
    if exact == 1:
        e = b.astype(BF16)
        p = _split(a, 3)
        return (_raw(p[2], e, mode) + _raw(p[1], e, mode)) + _raw(p[0], e, mode)
    e = a.astype(BF16)
    p = _split(b, 3)
    return (_raw(e, p[2], mode) + _raw(e, p[1], mode)) + _raw(e, p[0], mode)


@functools.partial(jax.custom_vjp, nondiff_argnums=(2, 3))
def odot(a, b, mode, exact):
    return _od(a, b, mode, exact)


def _odot_fwd(a, b, mode, exact):
    return _od(a, b, mode, exact), (a, b)


def _odot_bwd(mode, exact, res, g):
    a, b = res
    if exact == 1:
        da = {"nn": lambda: _od(g, b, "nt", 1), "nt": lambda: _od(g, b, "nn", 1), "tn": lambda: _od(b, g, "nt", 0)}[mode]()
        return da, jnp.zeros_like(b)
    db = {"nn": lambda: _od(a, g, "tn", 0), "nt": lambda: _od(g, a, "tn", 1), "tn": lambda: _od(a, g, "nn", 0)}[mode]()
    return jnp.zeros_like(a), db


odot.defvjp(_odot_fwd, _odot_bwd)


_BDIMS = {"bnn": (((2,), (1,)), ((0,), (0,))), "bnt": (((2,), (2,)), ((0,), (0,))), "btn": (((1,), (1,)), ((0,), (0,)))}


def _braw(a, b, mode):
    return lax.dot_general(a, b, _BDIMS[mode], preferred_element_type=F32)


def _bdb(a, b, mode):
    return _braw(a.astype(BF16), b.astype(BF16), mode)


def _d3b(a, b, mode):
    ah, al = _split(a, 2)
    bh, bl = _split(b, 2)
    return _braw(ah, bh, mode) + (_braw(ah, bl, mode) + _braw(al, bh, mode))


def _batched_bwd(f):
    def bwd(mode, res, g):
        a, b = res
        if mode == "bnn":
            return f(g, b, "bnt"), f(a, g, "btn")
        if mode == "bnt":
            return f(g, b, "bnn"), f(g, a, "btn")
        return f(b, g, "bnt"), f(a, g, "bnn")
    return bwd


@functools.partial(jax.custom_vjp, nondiff_argnums=(2,))
def bdot_b(a, b, mode):
    return _bdb(a, b, mode)


bdot_b.defvjp(lambda a, b, mode: (_bdb(a, b, mode), (a, b)), _batched_bwd(_bdb))


@functools.partial(jax.custom_vjp, nondiff_argnums=(2,))
def dot3_b(a, b, mode):
    return _d3b(a, b, mode)


dot3_b.defvjp(lambda a, b, mode: (_d3b(a, b, mode), (a, b)), _batched_bwd(_d3b))


def _cum(tril3, x, mode):
    e = tril3.astype(BF16)
    p = _split(x, 3)
    return (_braw(e, p[2], mode) + _braw(e, p[1], mode)) + _braw(e, p[0], mode)


@jax.custom_vjp
def chunk_cumsum(tril3, x):
    return _cum(tril3, x, "bnn")


chunk_cumsum.defvjp(lambda t, x: (_cum(t, x, "bnn"), t), lambda t, g: (jnp.zeros_like(t), _cum(t, g, "btn")))


def _unstack(axis, n):
    @jax.custom_vjp
    def un(x):
        return tuple(lax.index_in_dim(x, i, axis, keepdims=False) for i in range(n))

    un.defvjp(lambda x: (un(x), None), lambda _, g: (jnp.stack(g, axis=axis),))
    return un


def _split_last(n, w):
    @jax.custom_vjp
    def sp(x):
        return tuple(x[..., i * w:(i + 1) * w] for i in range(n))

    sp.defvjp(lambda x: (sp(x), None), lambda _, g: (jnp.concatenate(g, axis=-1),))
    return sp


def sigmoid(x):
    return 1.0 / (1.0 + jnp.exp(-x))


def silu(x):
    return x * sigmoid(x)


def softplus(x):
    return jnp.maximum(x, 0.0) + jnp.log1p(jnp.exp(jnp.minimum(x, -x)))


def _ln(x):
    mu = jnp.mean(x, axis=-1, keepdims=True)
    xc = x - mu
    return xc * lax.rsqrt(jnp.mean(xc * xc, axis=-1, keepdims=True) + LN_EPS)


def _tril64():
    r = lax.broadcasted_iota(jnp.int32, (CHUNK, CHUNK), 0)
    c = lax.broadcasted_iota(jnp.int32, (CHUNK, CHUNK), 1)
    return (r >= c).astype(F32)


def hgrn_block(q, fl, iv, gr, st, lb, gn):
    tb = q.shape[0]
    nc = tb // CHUNK
    nh = N_HEADS_A
    heads = _split_last(nh, LANES)
    to4 = lambda a: jnp.stack(heads(a), axis=0).reshape(nh, nc, CHUNK, LANES)
    flat = lambda a: a.reshape(nh * nc, CHUNK, LANES)
    f = lb + (1.0 - lb) * sigmoid(fl)
    gl4, k4, qf4, v4, gr4 = to4(jnp.log(f)), to4(1.0 - f), to4(silu(q) * (128 ** -0.5)), to4(iv), to4(gr)
    tril = _tril64()
    b4 = chunk_cumsum(jnp.broadcast_to(tril[None], (nh * nc, CHUNK, CHUNK)), flat(gl4)).reshape(gl4.shape)
    blast = jnp.sum(gl4, axis=2, keepdims=True)
    ref = lax.stop_gradient(0.5 * blast)
    sc = dot3_b(flat(qf4 * jnp.exp(b4 - ref)), flat(k4 * jnp.exp(ref - b4)), "bnt") * tril
    o_intra = bdot_b(sc, flat(v4), "bnn").reshape(gl4.shape)
    chunks = _unstack(1, nc)
    qe, v_c, kd, dec = chunks(qf4 * jnp.exp(b4)), chunks(v4), chunks(k4 * jnp.exp(blast - b4)), chunks(jnp.exp(blast))
    o_inter = []
    for c in range(nc):
        o_inter.append(bdot_b(qe[c], st, "bnt"))
        st = st * dec[c] + bdot_b(v_c[c], kd[c], "btn")
    o = o_intra + jnp.stack(o_inter, axis=1)
    on = o * lax.rsqrt(jnp.mean(o * o, axis=-1, keepdims=True) + RMS_EPS) * gn
    out = (on * silu(gr4)).reshape(nh, tb, LANES)
    return jnp.concatenate(_unstack(0, nh)(out), axis=1), st


def ssd_consts(g):
    i32 = jnp.int32
    ej = lax.broadcasted_iota(i32, (LANES, 512), 0)
    ec = lax.broadcasted_iota(i32, (LANES, 512), 1)
    expand = (ej == g * 8 + (ec >> 6)).astype(F32)
    ts = lax.broadcasted_iota(i32, (CHUNK, 512), 0)
    tc = lax.broadcasted_iota(i32, (CHUNK, 512), 1)
    itile = (ts == (tc & 63)).astype(F32)
    maskall = ts >= (tc & 63)
    br = lax.broadcasted_iota(i32, (256, 256), 0)
    bc = lax.broadcasted_iota(i32, (256, 256), 1)
    blockmask = ((br >> 6) == (bc >> 6)).astype(F32)
    return expand, itile, maskall, blockmask, _tril64()


def ssd_block(x, bm, cm, dt, z, st, dtb, alog, dsk, nw, cs):
    expand, itile, maskall, blockmask, tril = cs
    tb = x.shape[0]
    nc = tb // CHUNK
    delta = softplus(odot(dt, expand, "nn", 1) + dtb)
    a = -jnp.exp(alog) * delta
    xdt = x * delta
    by_chunk = lambda v: v.reshape(nc, CHUNK, v.shape[-1])
    a3, xdt3, bm3, cm3 = by_chunk(a), by_chunk(xdt), by_chunk(bm), by_chunk(cm)
    acum3 = chunk_cumsum(jnp.broadcast_to(tril[None], (nc, CHUNK, CHUNK)), a3)
    alast3 = jnp.sum(a3, axis=1, keepdims=True)
    cb3 = bdot_b(cm3, jnp.concatenate([bm3] * 8, axis=1), "bnt")
    arow3 = jnp.sum(acum3 * itile, axis=1, keepdims=True)
    dec3 = jnp.where(maskall, jnp.exp(jnp.minimum(acum3 - arow3, 0.0)), 0.0)
    halves = _split_last(2, 256)
    intra = [bdot_b(m, jnp.concatenate([xh] * 4, axis=1) * blockmask, "bnn")
             for m, xh in zip(halves(cb3 * dec3), halves(xdt3))]
    chunks = _unstack(0, nc)
    cm_c, bm_c, xw_c, dec_c = chunks(cm3), chunks(bm3), chunks(xdt3 * jnp.exp(alast3 - acum3)), chunks(jnp.exp(alast3))
    inter = []
    for c in range(nc):
        inter.append(bdot(cm_c[c], st, "nn"))
        st = st * dec_c[c] + bdot(bm_c[c], xw_c[c], "tn")
    st_new = st
    y = (jnp.concatenate(intra, axis=-1) + jnp.stack(inter, axis=0) * jnp.exp(acum3)).reshape(tb, 512)
    yz = (y + x * dsk) * silu(z)
    return yz * lax.rsqrt(jnp.mean(yz * yz, axis=-1, keepdims=True) + RMS_EPS) * nw, st_new


def adamw(w, g, m, v):
    m = ADAM_B1 * m + (1.0 - ADAM_B1) * g
    v = ADAM_B2 * v + (1.0 - ADAM_B2) * jnp.square(g)
    m_hat = m / (1.0 - ADAM_B1 ** ADAM_STEP)
    v_hat = v / (1.0 - ADAM_B2 ** ADAM_STEP)
    return -ADAM_LR * (m_hat / (jnp.sqrt(v_hat) + ADAM_EPS) + ADAM_WD * w), m, v


def _pick(n, cands):
    for c in cands:
        if n % c == 0:
            return c
    return n


def _params(sem):
    return pltpu.CompilerParams(dimension_semantics=sem, vmem_limit_bytes=VMEM_LIMIT)


def matmul(a, b, mode, out_dtype, name, after=None):
    if mode == "nn":
        (m, k), n = a.shape, b.shape[1]
    elif mode == "nt":
        (m, k), n = a.shape, b.shape[0]
    else:
        (k, m), n = a.shape, b.shape[1]
    tm = _pick(m, (1408, 1024, 768, 512, 256, 128))
    tn = _pick(n, (1408, 1024, 768, 512, 256, 128))
    tk = _pick(k, (2304, 2048, 1408, 1024, 768, 512, 256, 128))
    nk = k // tk
    a_spec = pl.BlockSpec((tk, tm), lambda i, j, kk: (kk, i)) if mode == "tn" else pl.BlockSpec((tm, tk), lambda i, j, kk: (i, kk))
    b_spec = pl.BlockSpec((tn, tk), lambda i, j, kk: (j, kk)) if mode == "nt" else pl.BlockSpec((tk, tn), lambda i, j, kk: (kk, j))

    order = [] if after is None else [after]

    def body(a_ref, b_ref, *rest):
        o_ref, *acc = rest[len(order):]
        part = _bd(a_ref[...], b_ref[...], mode)
        if nk == 1:
            o_ref[...] = part.astype(o_ref.dtype)
            return
        acc_ref, = acc
        kk = pl.program_id(2)

        @pl.when(kk == 0)
        def _():
            acc_ref[...] = part

        @pl.when(jnp.logical_and(kk > 0, kk < nk - 1))
        def _():
            acc_ref[...] += part

        @pl.when(kk == nk - 1)
        def _():
            o_ref[...] = (acc_ref[...] + part).astype(o_ref.dtype)

    return pl.pallas_call(
        body, name=name, grid=(m // tm, n // tn, nk),
        in_specs=[a_spec, b_spec] + [pl.BlockSpec(memory_space=pl.ANY) for _ in order],
        out_specs=pl.BlockSpec((tm, tn), lambda i, j, kk: (i, j)),
        out_shape=jax.ShapeDtypeStruct((m, n), out_dtype),
        scratch_shapes=[pltpu.VMEM((tm, tn), F32)] if nk > 1 else [],
        compiler_params=_params(("parallel", "parallel", "arbitrary")),
    )(a, b, *order)


def rowwise(name, fn, rows, consts, out_rows, out_accs=(), tm_max=256, into=None, new_wide=None):
    t = rows[0][0].shape[0]
    tm = _pick(t, (tm_max, 128, 64, 32, 16, 8))
    n_r, n_c, n_o = len(rows), len(consts), len(out_rows)
    n_alias = 0 if into is None else 1

    def body(*refs):
        r_in = [r[...] for r in refs[:n_r]]
        c_in = [r[...] for r in refs[n_r:n_r + n_c]]
        refs = refs[:n_r + n_c] + refs[n_r + n_c + n_alias:]
        o_refs = refs[n_r + n_c:n_r + n_c + n_o]
        a_refs = refs[n_r + n_c + n_o:]
        ro, ao = fn(r_in, c_in)
        for ref, val in zip(o_refs, ro, strict=True):
            ref[...] = val.astype(ref.dtype)
        if a_refs:
            @pl.when(pl.program_id(0) == 0)
            def _():
                for ref in a_refs:
                    ref[...] = jnp.zeros_like(ref)

            for ref, val in zip(a_refs, ao, strict=True):
                ref[...] += val

    in_specs = [pl.BlockSpec((tm, w), functools.partial(lambda i, cb: (i, cb), cb=cb)) for _, w, cb in rows]
    in_specs += [pl.BlockSpec(c.shape, lambda i: (0, 0)) for c in consts]
    out_specs = [pl.BlockSpec((tm, w), lambda i: (i, 0)) for w, _ in out_rows]
    out_specs += [pl.BlockSpec(s, lambda i: (0, 0)) for s in out_accs]
    out_shape = [jax.ShapeDtypeStruct((t, w), dt) for w, dt in out_rows]
    out_shape += [jax.ShapeDtypeStruct(s, F32) for s in out_accs]
    operands = [r[0] for r in rows] + list(consts)
    aliases = {}
    if into is not None:
        target, cb = into
        in_specs.append(pl.BlockSpec(memory_space=pl.ANY))
        operands.append(target)
        out_specs[0] = pl.BlockSpec((tm, out_rows[0][0]), lambda i: (i, cb))
        out_shape[0] = jax.ShapeDtypeStruct(target.shape, target.dtype)
        aliases = {len(operands) - 1: 0}
    if new_wide is not None:
        width, cb = new_wide
        out_specs[0] = pl.BlockSpec((tm, out_rows[0][0]), lambda i: (i, cb))
        out_shape[0] = jax.ShapeDtypeStruct((t, width), out_rows[0][1])
    return pl.pallas_call(
        body, name=name, grid=(t // tm,), in_specs=in_specs, out_specs=out_specs, out_shape=out_shape,
        input_output_aliases=aliases, compiler_params=_params(("arbitrary",)),
    )(*operands)


def _full(a):
    return (a, a.shape[1], 0)


def _time_block(t):
    return _pick(t, (256, 128, 64))


def _quarters(ref):
    return [ref[:, seg * D:(seg + 1) * D] for seg in range(4)]


def hgrn_forward(proj, lb, gn):
    t = proj.shape[0]
    tb = _time_block(t)
    nb = t // tb

    def body(qfig_ref, lb_ref, gn_ref, o_ref, st_ref, state):
        @pl.when(pl.program_id(0) == 0)
        def _():
            state[...] = jnp.zeros_like(state)

        st = state[...]
        st_ref[...] = st
        out, st_new = hgrn_block(*_quarters(qfig_ref), st, lb_ref[...], gn_ref[...])
        o_ref[...] = out.astype(o_ref.dtype)
        state[...] = st_new

    return pl.pallas_call(
        body, name="hgrn_forward", grid=(nb,),
        in_specs=[pl.BlockSpec((tb, 4 * D), lambda j: (j, 0)),
                  pl.BlockSpec((1, D), lambda j: (0, 0)), pl.BlockSpec((1, LANES), lambda j: (0, 0))],
        out_specs=[pl.BlockSpec((tb, D), lambda j: (j, 0)),
                   pl.BlockSpec((None, N_HEADS_A, LANES, LANES), lambda j: (j, 0, 0, 0))],
        out_shape=[jax.ShapeDtypeStruct((t, D), BF16),
                   jax.ShapeDtypeStruct((nb, N_HEADS_A, LANES, LANES), F32)],
        scratch_shapes=[pltpu.VMEM((N_HEADS_A, LANES, LANES), F32)],
        compiler_params=_params(("arbitrary",)),
    )(proj, lb, gn)


def hgrn_backward(proj, states, d_out, lb, gn, d_proj):
    t = proj.shape[0]
    tb = _time_block(t)
    nb = t // tb

    def body(qfig_ref, st_ref, do_ref, lb_ref, gn_ref, _, dqfig_ref, dlb_ref, dgn_ref, d_state):
        @pl.when(pl.program_id(0) == 0)
        def _():
            d_state[...] = jnp.zeros_like(d_state)
            dlb_ref[...] = jnp.zeros_like(dlb_ref)
            dgn_ref[...] = jnp.zeros_like(dgn_ref)

        _, vjp = jax.vjp(hgrn_block, *_quarters(qfig_ref), st_ref[...], lb_ref[...], gn_ref[...])
        dq, df, di, dg, dst, dlb, dgn = vjp((do_ref[...], d_state[...]))
        for seg, val in enumerate((dq, df, di, dg)):
            dqfig_ref[:, seg * D:(seg + 1) * D] = val.astype(dqfig_ref.dtype)
        d_state[...] = dst
        dlb_ref[...] += dlb
        dgn_ref[...] += dgn

    rev = lambda j: nb - 1 - j
    return pl.pallas_call(
        body, name="hgrn_backward", grid=(nb,),
        in_specs=[pl.BlockSpec((tb, 4 * D), lambda j: (rev(j), 0)),
                  pl.BlockSpec((None, N_HEADS_A, LANES, LANES), lambda j: (rev(j), 0, 0, 0)),
                  pl.BlockSpec((tb, D), lambda j: (rev(j), 0)),
                  pl.BlockSpec((1, D), lambda j: (0, 0)), pl.BlockSpec((1, LANES), lambda j: (0, 0)),
                  pl.BlockSpec(memory_space=pl.ANY)],
        out_specs=[pl.BlockSpec((tb, 4 * D), lambda j: (rev(j), 0)),
                   pl.BlockSpec((1, D), lambda j: (0, 0)), pl.BlockSpec((1, LANES), lambda j: (0, 0))],
        out_shape=[jax.ShapeDtypeStruct(d_proj.shape, d_proj.dtype), jax.ShapeDtypeStruct((1, D), F32),
                   jax.ShapeDtypeStruct((1, LANES), F32)],
        input_output_aliases={5: 0},
        scratch_shapes=[pltpu.VMEM((N_HEADS_A, LANES, LANES), F32)],
        compiler_params=_params(("arbitrary",)),
    )(proj, states, d_out, lb, gn, d_proj)


def _ssd_in_specs(tb, tmap):
    return [pl.BlockSpec((tb, 512), lambda g, j: (tmap(j), g)),
            pl.BlockSpec((tb, LANES), lambda g, j: (tmap(j), 16 + g)),
            pl.BlockSpec((tb, LANES), lambda g, j: (tmap(j), 20 + g)),
            pl.BlockSpec((tb, LANES), lambda g, j: (tmap(j), COL_DT // LANES)),
            pl.BlockSpec((tb, 512), lambda g, j: (tmap(j), COL_Z // 512 + g))]


def ssd_forward(xc, proj, dtb, alog, dsk, nw):
    t = proj.shape[0]
    tb = _time_block(t)
    nb = t // tb

    def body(x_ref, b_ref, c_ref, dt_ref, z_ref, dtb_ref, alog_ref, dsk_ref, nw_ref, o_ref, st_ref, state):
        @pl.when(pl.program_id(1) == 0)
        def _():
            state[...] = jnp.zeros_like(state)

        st = state[...]
        st_ref[...] = st
        out, st_new = ssd_block(x_ref[...], b_ref[...], c_ref[...], dt_ref[...], z_ref[...], st,
                                dtb_ref[...], alog_ref[...], dsk_ref[...], nw_ref[...], ssd_consts(pl.program_id(0)))
        o_ref[...] = out.astype(o_ref.dtype)
        state[...] = st_new

    vec = pl.BlockSpec((1, 512), lambda g, j: (0, g))
    return pl.pallas_call(
        body, name="ssd_forward", grid=(N_GROUPS_B, nb),
        in_specs=_ssd_in_specs(tb, lambda j: j) + [vec] * 4,
        out_specs=[pl.BlockSpec((tb, 512), lambda g, j: (j, g)),
                   pl.BlockSpec((None, None, LANES, 512), lambda g, j: (j, g, 0, 0))],
        out_shape=[jax.ShapeDtypeStruct((t, B_INNER), BF16),
                   jax.ShapeDtypeStruct((nb, N_GROUPS_B, LANES, 512), F32)],
        scratch_shapes=[pltpu.VMEM((LANES, 512), F32)],
        compiler_params=_params(("arbitrary", "arbitrary")),
    )(xc, xc, xc, proj, proj, dtb, alog, dsk, nw)


def ssd_backward(xc, proj, states, d_out, dtb, alog, dsk, nw, d_proj):
    t = proj.shape[0]
    tb = _time_block(t)
    nb = t // tb
    rev = lambda j: nb - 1 - j

    def body(x_ref, b_ref, c_ref, dt_ref, z_ref, st_ref, do_ref, dtb_ref, alog_ref, dsk_ref, nw_ref, _,
             dx_ref, db_ref, dc_ref, ddt_ref, dz_ref, ddtb_ref, dalog_ref, ddsk_ref, dnw_ref, d_state):
        accs = (ddtb_ref, dalog_ref, ddsk_ref, dnw_ref)

        @pl.when(pl.program_id(1) == 0)
        def _():
            d_state[...] = jnp.zeros_like(d_state)
            for ref in accs:
                ref[...] = jnp.zeros_like(ref)

        cs = ssd_consts(pl.program_id(0))
        fn = lambda *a: ssd_block(*a, cs)
        _, vjp = jax.vjp(fn, x_ref[...], b_ref[...], c_ref[...], dt_ref[...], z_ref[...], st_ref[...],
                         dtb_ref[...], alog_ref[...], dsk_ref[...], nw_ref[...])
        dx, db, dc, ddt, dz, dst, *dpar = vjp((do_ref[...], d_state[...]))
        dx_ref[...] = dx
        db_ref[...] = db
        dc_ref[...] = dc
        ddt_ref[...] = ddt
        dz_ref[...] = dz.astype(dz_ref.dtype)
        d_state[...] = dst
        for ref, val in zip(accs, dpar, strict=True):
            ref[...] += val

    vec = pl.BlockSpec((1, 512), lambda g, j: (0, g))
    acc = pl.BlockSpec((None, 1, 512), lambda g, j: (g, 0, 0))
    return pl.pallas_call(
        body, name="ssd_backward", grid=(N_GROUPS_B, nb),
        in_specs=_ssd_in_specs(tb, rev)
        + [pl.BlockSpec((None, None, LANES, 512), lambda g, j: (rev(j), g, 0, 0)),
           pl.BlockSpec((tb, 512), lambda g, j: (rev(j), g))] + [vec] * 4 + [pl.BlockSpec(memory_space=pl.ANY)],
        out_specs=[pl.BlockSpec((tb, 512), lambda g, j: (rev(j), g)),
                   pl.BlockSpec((tb, LANES), lambda g, j: (rev(j), g)),
                   pl.BlockSpec((tb, LANES), lambda g, j: (rev(j), g)),
                   pl.BlockSpec((None, tb, LANES), lambda g, j: (g, rev(j), 0)),
                   pl.BlockSpec((tb, 512), lambda g, j: (rev(j), COL_Z // 512 + g)), acc, acc, acc, acc],
        out_shape=[jax.ShapeDtypeStruct((t, B_INNER), F32), jax.ShapeDtypeStruct((t, 512), F32),
                   jax.ShapeDtypeStruct((t, 512), F32), jax.ShapeDtypeStruct((N_GROUPS_B, t, LANES), F32),
                   jax.ShapeDtypeStruct(d_proj.shape, d_proj.dtype)] + [jax.ShapeDtypeStruct((N_GROUPS_B, 1, 512), F32)] * 4,
        input_output_aliases={11: 4},
        scratch_shapes=[pltpu.VMEM((LANES, 512), F32)],
        compiler_params=_params(("arbitrary", "arbitrary")),
    )(xc, xc, xc, proj, proj, states, d_out, dtb, alog, dsk, nw, d_proj)


CONV_HALO = 8


def _shift_down(halo_then_tile, s, tm):
    if s == 0:
        return halo_then_tile[CONV_HALO:CONV_HALO + tm]
    return pltpu.roll(halo_then_tile, s, 0)[CONV_HALO:CONV_HALO + tm]


def _conv_pre(cur, prev, w, b, tm):
    stacked = jnp.concatenate([prev, cur], axis=0)
    taps = [_shift_down(stacked, 3 - j, tm) for j in range(4)]
    pre = b + taps[0] * w[0:1] + taps[1] * w[1:2] + taps[2] * w[2:3] + taps[3] * w[3:4]
    return pre, taps


def _conv_specs(t, tm):
    per = tm // CONV_HALO
    cur = pl.BlockSpec((tm, CONV_DIM), lambda i: (i, COL_XBC // CONV_DIM))
    prev = pl.BlockSpec((CONV_HALO, CONV_DIM), lambda i: (jnp.maximum(i * per - 1, 0), COL_XBC // CONV_DIM))
    return cur, prev


def conv_forward(proj, w, b):
    t = proj.shape[0]
    tm = _pick(t, (256, 128, 64))

    def body(cur_ref, prev_ref, w_ref, b_ref, o_ref):
        prev = jnp.where(pl.program_id(0) == 0, 0.0, prev_ref[...])
        pre, _ = _conv_pre(cur_ref[...], prev, w_ref[...], b_ref[...], tm)
        o_ref[...] = silu(pre)

    cur, prev = _conv_specs(t, tm)
    return pl.pallas_call(
        body, name="conv_forward", grid=(t // tm,),
        in_specs=[cur, prev, pl.BlockSpec((4, CONV_DIM), lambda i: (0, 0)), pl.BlockSpec((1, CONV_DIM), lambda i: (0, 0))],
        out_specs=pl.BlockSpec((tm, CONV_DIM), lambda i: (i, 0)),
        out_shape=jax.ShapeDtypeStruct((t, CONV_DIM), F32),
        compiler_params=_params(("arbitrary",)),
    )(proj, proj, w, b)


def conv_backward_pre(proj, dx, db_, dc_, w, b):
    t = proj.shape[0]
    tm = _pick(t, (256, 128, 64))

    def body(cur_ref, prev_ref, dx_ref, dbm_ref, dcm_ref, w_ref, b_ref, dpre_ref, dw_ref, dbias_ref):
        @pl.when(pl.program_id(0) == 0)
        def _():
            dw_ref[...] = jnp.zeros_like(dw_ref)
            dbias_ref[...] = jnp.zeros_like(dbias_ref)

        prev = jnp.where(pl.program_id(0) == 0, 0.0, prev_ref[...])
        pre, taps = _conv_pre(cur_ref[...], prev, w_ref[...], b_ref[...], tm)
        sg = sigmoid(pre)
        d_out = jnp.concatenate([dx_ref[...], dbm_ref[...], dcm_ref[...]], axis=1)
        dpre = d_out * (sg * (1.0 + pre * (1.0 - sg)))
        dpre_ref[...] = dpre
        dbias_ref[...] += jnp.sum(dpre, axis=0, keepdims=True)
        for j in range(4):
            dw_ref[j:j + 1, :] += jnp.sum(dpre * taps[j], axis=0, keepdims=True)

    cur, prev = _conv_specs(t, tm)
    row = lambda w_: pl.BlockSpec((tm, w_), lambda i: (i, 0))
    return pl.pallas_call(
        body, name="conv_backward_pre", grid=(t // tm,),
        in_specs=[cur, prev, row(B_INNER), row(512), row(512),
                  pl.BlockSpec((4, CONV_DIM), lambda i: (0, 0)), pl.BlockSpec((1, CONV_DIM), lambda i: (0, 0))],
        out_specs=[row(CONV_DIM), pl.BlockSpec((4, CONV_DIM), lambda i: (0, 0)), pl.BlockSpec((1, CONV_DIM), lambda i: (0, 0))],
        out_shape=[jax.ShapeDtypeStruct((t, CONV_DIM), F32), jax.ShapeDtypeStruct((4, CONV_DIM), F32),
                   jax.ShapeDtypeStruct((1, CONV_DIM), F32)],
        compiler_params=_params(("arbitrary",)),
    )(proj, proj, dx, db_, dc_, w, b)


def conv_backward_input(dpre, w, d_proj):
    t = dpre.shape[0]
    tm = _pick(t, (256, 128, 64))
    per = tm // CONV_HALO
    last = t // CONV_HALO - 1
    nt = t // tm

    def body(cur_ref, nxt_ref, w_ref, _, o_ref):
        nxt = jnp.where(pl.program_id(0) == nt - 1, 0.0, nxt_ref[...])
        stacked = jnp.concatenate([cur_ref[...], nxt], axis=0)
        w_ = w_ref[...]
        acc = stacked[0:tm] * w_[3:4]
        for j in range(3):
            s = 3 - j
            acc = acc + pltpu.roll(stacked, tm + CONV_HALO - s, 0)[0:tm] * w_[j:j + 1]
        o_ref[...] = acc.astype(o_ref.dtype)

    return pl.pallas_call(
        body, name="conv_backward_input", grid=(nt,),
        in_specs=[pl.BlockSpec((tm, CONV_DIM), lambda i: (i, 0)),
                  pl.BlockSpec((CONV_HALO, CONV_DIM), lambda i: (jnp.minimum((i + 1) * per, last), 0)),
                  pl.BlockSpec((4, CONV_DIM), lambda i: (0, 0)), pl.BlockSpec(memory_space=pl.ANY)],
        out_specs=pl.BlockSpec((tm, CONV_DIM), lambda i: (i, COL_XBC // CONV_DIM)),
        out_shape=jax.ShapeDtypeStruct(d_proj.shape, d_proj.dtype),
        input_output_aliases={3: 0},
        compiler_params=_params(("arbitrary",)),
    )(dpre, dpre, w, d_proj)


def stage_modulate(x, sc, sh):
    return _ln(x) * (1.0 + sc) + sh


def stage_merge(ga, gb, ya, yb):
    return sigmoid(ga) * ya + sigmoid(gb) * yb


def stage_post_mixer(x, h, g1, ln_g, ln_b, sc2, sh2):
    x1 = _ln(ALPHA * x + g1 * h) * ln_g + ln_b
    return x1, _ln(x1) * (1.0 + sc2) + sh2


def stage_swiglu(a, b):
    return silu(a) * b


def stage_loss(x1, hf, tgt, g2, ln_g, ln_b):
    x2 = _ln(ALPHA * x1 + g2 * hf) * ln_g + ln_b
    return 0.5 * jnp.sum(jnp.mean(jnp.square(x2 - tgt), axis=-1, keepdims=True), axis=0, keepdims=True)


def local_step(x, tgt, mod, wts, small, early=None, mid=None, late=None, last=None):
    sh1, sc1, g1, sh2, sc2, g2 = mod
    lb, gn, conv_w, conv_b, dtb, alog, dsk, nw, ln1_g, ln1_b, ln2_g, ln2_b = small
    vec = (1, D)

    (u1,) = rowwise("modulate1", lambda r, c: ((stage_modulate(r[0], *c),), ()), [_full(x)], [sc1, sh1], [(D, BF16)])
    w_in = wts.input_projection(u1)
    proj = matmul(u1, w_in, "nn", F32, "in_proj")
    ya_in, st_a = hgrn_forward(proj, lb, gn + wts.start_rest(proj)[0:1])
    xc = conv_forward(proj, conv_w, conv_b)
    w_a, w_b, w_o, w_gu, w_d = wts.rest(xc)
    yb_in, st_b = ssd_forward(xc, proj, dtb, alog, dsk, nw)
    ya = matmul(ya_in, w_a, "nn", F32, "branch_a")
    yb = matmul(yb_in, w_b, "nn", F32, "branch_b")
    gate_rows = [(proj, D, COL_GA // D), (proj, D, COL_GB // D), _full(ya), _full(yb)]
    (merged,) = rowwise("merge", lambda r, c: ((stage_merge(*r),), ()), gate_rows, [], [(D, BF16)])
    h = matmul(merged, w_o, "nn", F32, "out_proj")
    post_consts = [g1, ln1_g, ln1_b, sc2, sh2]
    x1, u2 = rowwise("post_mixer", lambda r, c: (stage_post_mixer(*r, *c), ()), [_full(x), _full(h)], post_consts,
                     [(D, F32), (D, BF16)])
    ab = matmul(u2, w_gu, "nt", F32, "ffn_in")
    (p,) = rowwise("swiglu", lambda r, c: ((stage_swiglu(*r),), ()), [(ab, D_FF, 0), (ab, D_FF, 1)], [], [(D_FF, BF16)])
    hf = matmul(p, w_d, "nn", F32, "ffn_out")

    def loss_bwd(r, c):
        loss, vjp = jax.vjp(stage_loss, *r, *c)
        dx1, dhf, _, dg2, dlg, dlb_ = vjp(jnp.ones((1, 1), F32))
        return (dx1, dhf), (loss, dg2, dlg, dlb_)

    dx1, dhf, loss, dg2, dln2_g, dln2_b = rowwise(
        "loss_backward", loss_bwd, [_full(x1), _full(hf), _full(tgt)], [g2, ln2_g, ln2_b],
        [(D, F32), (D, BF16)], [(1, 1), vec, vec, vec])
    dp = matmul(dhf, w_d, "nt", F32, "ffn_out_dx")
    dw_d = matmul(p, dhf, "tn", F32, "ffn_out_dw")

    def swiglu_bwd(r, c):
        _, vjp = jax.vjp(stage_swiglu, r[0], r[1])
        da, db_ = vjp(r[2])
        return (jnp.concatenate([da, db_], axis=1),), ()

    (dab,) = rowwise("swiglu_backward", swiglu_bwd, [(ab, D_FF, 0), (ab, D_FF, 1), _full(dp)], [], [(2 * D_FF, BF16)])
    du2 = matmul(dab, w_gu, "nn", F32, "ffn_in_dx")
    dw_gu = matmul(dab, u2, "tn", F32, "ffn_in_dw")

    def post_bwd(r, c):
        _, vjp = jax.vjp(stage_post_mixer, r[0], r[1], *c)
        dx, dh, *dc = vjp((r[2], r[3]))
        return (dx, dh), tuple(dc)

    dx_a, dh, dg1, dln1_g, dln1_b, dsc2, dsh2 = rowwise(
        "post_mixer_backward", post_bwd, [_full(x), _full(h), _full(dx1), _full(du2)], post_consts,
        [(D, F32), (D, BF16)], [vec] * 5)
    dmerged = matmul(dh, w_o, "nt", F32, "out_proj_dx")
    dw_o = matmul(merged, dh, "tn", F32, "out_proj_dw")

    def merge_bwd(r, c):
        _, vjp = jax.vjp(stage_merge, *r[:4])
        dga, dgb, dya, dyb = vjp(r[4])
        return (jnp.concatenate([dga, dgb], axis=1), dya, dyb), ()

    dproj, dya, dyb = rowwise("merge_backward", merge_bwd, gate_rows + [_full(dmerged)], [],
                              [(2 * D, BF16), (D, BF16), (D, BF16)], new_wide=(IN_PAD, COL_GA // (2 * D)))
    dya_in = matmul(dya, w_a, "nt", F32, "branch_a_dx")
    dw_a = matmul(ya_in, dya, "tn", F32, "branch_a_dw")
    dyb_in = matmul(dyb, w_b, "nt", F32, "branch_b_dx")
    dw_b = matmul(yb_in, dyb, "tn", F32, "branch_b_dw")
    gn_after = gn if early is None else gn + early((dw_a, dw_b, dw_o, dw_gu, dw_d))[0:1]
    dproj, dlb, dgn = hgrn_backward(proj, st_a, dya_in, lb, gn_after, dproj)
    dtb_after = dtb if mid is None else dtb + mid(dlb)[0:1, 0:1]
    dxs, dbm, dcm, ddt, dproj, ddtb, dalog, ddsk, dnw = ssd_backward(xc, proj, st_b, dyb_in, dtb_after, alog, dsk, nw, dproj)
    dpre, dconv_w, dconv_b = conv_backward_pre(proj, dxs, dbm, dcm, conv_w, conv_b)
    if late is not None:
        late(dconv_b)
    dproj = conv_backward_input(dpre, conv_w, dproj)
    t = x.shape[0]
    tail = jnp.concatenate([jnp.sum(ddt, axis=0).astype(BF16), jnp.zeros((t, IN_PAD - COL_DT - LANES), BF16)], axis=1)
    dproj = lax.dynamic_update_slice(dproj, tail, (0, COL_DT))
    dw_in = matmul(u1, dproj, "tn", F32, "in_proj_dw")
    du1 = matmul(dproj, w_in, "nt", F32, "in_proj_dx", after=None if last is None else last(dw_in))

    def mod_bwd(r, c):
        _, vjp = jax.vjp(stage_modulate, r[0], *c)
        dx, dsc, dsh = vjp(r[1])
        return (dx + r[2],), (dsc, dsh)

    grad_x, dsc1, dsh1 = rowwise("modulate1_backward", mod_bwd, [_full(x), _full(du1), _full(dx_a)], [sc1, sh1],
                                 [(D, F32)], [vec, vec])
    d_mod = (dsh1, dsc1, dg1, dsh2, dsc2, dg2)
    d_wts = (dw_in, dw_a, dw_b, dw_o, dw_gu, dw_d)
    d_small = (dlb, dgn, dconv_w, dconv_b, ddtb.reshape(1, B_INNER),
               dalog.reshape(1, B_INNER), ddsk.reshape(1, B_INNER), dnw.reshape(1, B_INNER),
               dln1_g, dln1_b, dln2_g, dln2_b)
    return loss, grad_x, d_mod, d_wts, d_small


HBM = pl.BlockSpec(memory_space=pltpu.HBM)
SEM = pl.BlockSpec(memory_space=pltpu.SEMAPHORE)
DATAFLOW = pltpu.SideEffectType.DATAFLOW_SIDE_EFFECTING


def _place():
    return lax.axis_index("x"), lax.axis_index("y"), lax.axis_index("c")


def _other_chips(x, y):
    return [(1 - x, y), (x, 1 - y), (1 - x, 1 - y)]


def _remote(src, dst, send_sem, recv_sem, device):
    return pltpu.make_async_remote_copy(src_ref=src, dst_ref=dst, send_sem=send_sem, recv_sem=recv_sem,
                                        device_id=device, device_id_type=MESH)


def gather_rows(v, name):
    n = v.shape[1]

    def body(v_ref, out_ref, send_sems, recv_sems, local_sem):
        x, y, c = _place()
        mine = pltpu.make_async_copy(v_ref, out_ref.at[4 * x + 2 * y + c], local_sem)
        mine.start()
        sends, recvs = [], []
        for m in range(1, 8):
            px = 1 - x if m & 4 else x
            py = 1 - y if m & 2 else y
            pc = 1 - c if m & 1 else c
            sends.append(_remote(v_ref, out_ref.at[4 * x + 2 * y + c], send_sems.at[m - 1], recv_sems.at[m - 1], (px, py, pc)))
            recvs.append(_remote(v_ref, out_ref.at[4 * px + 2 * py + pc], send_sems.at[m - 1], recv_sems.at[m - 1], (px, py, pc)))
        for cp in sends:
            cp.start()
        for cp in recvs:
            cp.wait_recv()
        for cp in sends:
            cp.wait_send()
        mine.wait()

    return pl.pallas_call(
        body, name=name, in_specs=[HBM], out_specs=HBM,
        out_shape=jax.ShapeDtypeStruct((8, 1, n), v.dtype),
        scratch_shapes=[pltpu.SemaphoreType.DMA((7,)), pltpu.SemaphoreType.DMA((7,)), pltpu.SemaphoreType.DMA],
    )(v)


def exchange_rows(part, name):
    w = part.shape[2]

    def body(p_ref, out_ref, send_sems, recv_sems, local_sem):
        x, y, c = _place()
        k = 2 * x + y
        mine = pltpu.make_async_copy(p_ref.at[4 * x + 2 * y + c], out_ref.at[k], local_sem)
        mine.start()
        sends, recvs = [], []
        for j, (px, py) in enumerate(_other_chips(x, y)):
            sends.append(_remote(p_ref.at[4 * px + 2 * py + c], out_ref.at[k], send_sems.at[j], recv_sems.at[j], (px, py, c)))
            recvs.append(_remote(p_ref.at[4 * px + 2 * py + c], out_ref.at[2 * px + py], send_sems.at[j], recv_sems.at[j], (px, py, c)))
        for cp in sends:
            cp.start()
        for cp in recvs:
            cp.wait_recv()
        for cp in sends:
            cp.wait_send()
        mine.wait()

    return pl.pallas_call(
        body, name=name, in_specs=[HBM], out_specs=HBM,
        out_shape=jax.ShapeDtypeStruct((4, 1, w), part.dtype),
        scratch_shapes=[pltpu.SemaphoreType.DMA((3,)), pltpu.SemaphoreType.DMA((3,)), pltpu.SemaphoreType.DMA],
    )(part)


def _half_of_slot(ref, rows, px, py, pc):
    return ref.at[2 * px + py, pl.ds(pc * (rows // 2), rows // 2), :]


def gather_start(shards, after):
    n = len(shards)

    def body(*refs):
        w_refs, land_refs = refs[:n], refs[n:2 * n]
        send_a, recv_a, send_b, recv_b = refs[2 * n + 1:2 * n + 5]
        token = refs[-1]
        x, y, c = _place()
        for i in range(n):
            rows = shards[i].shape[0]
            for j, (px, py) in enumerate(_other_chips(x, y)):
                sems = (send_a.at[j], recv_a.at[j]) if i == 0 else (send_b.at[j * (n - 1) + i - 1], recv_b.at[j * (n - 1) + i - 1])
                _remote(w_refs[i].at[pl.ds(c * (rows // 2), rows // 2), :], _half_of_slot(land_refs[i], rows, x, y, c),
                        *sems, (px, py, c)).start()
        token[...] = jnp.zeros_like(token)

    hbm = lambda a: pltpu.with_memory_space_constraint(a, pltpu.HBM)
    lands = [lax.empty((4,) + s.shape, s.dtype) for s in shards]
    dma = pltpu.SemaphoreType.DMA
    return pl.pallas_call(
        body, name="gather_start",
        out_shape=(dma((3,)), dma((3,)), dma((3 * (n - 1),)), dma((3 * (n - 1),)),
                   *[pltpu.HBM(a.shape, a.dtype) for a in list(shards) + lands], jax.ShapeDtypeStruct((8, LANES), F32)),
        in_specs=[HBM] * (2 * n) + [pl.BlockSpec(memory_space=pl.ANY)],
        out_specs=(SEM, SEM, SEM, SEM, *[HBM] * (2 * n), pl.BlockSpec(memory_space=pltpu.VMEM)),
        input_output_aliases={i: 4 + i for i in range(2 * n)},
        compiler_params=pltpu.CompilerParams(has_side_effects=DATAFLOW),
    )(*[hbm(a) for a in list(shards) + lands], after)


def gather_wait(send_sems, recv_sems, shards, lands, after, tag):
    n = len(shards)

    def body(*refs):
        w_refs, land_refs = refs[:n], refs[n:2 * n]
        send_ref, recv_ref = refs[2 * n], refs[2 * n + 1]
        x, y, c = _place()
        for i in range(n):
            rows = shards[i].shape[0]
            for j, (px, py) in enumerate(_other_chips(x, y)):
                cp = _remote(w_refs[i].at[pl.ds(c * (rows // 2), rows // 2), :], _half_of_slot(land_refs[i], rows, px, py, c),
                             send_ref.at[j * n + i], recv_ref.at[j * n + i], (px, py, c))
                cp.wait_send()
                cp.wait_recv()

    out = pl.pallas_call(
        body, name="gather_wait_" + tag,
        out_shape=tuple(pltpu.HBM(a.shape, a.dtype) for a in list(shards) + list(lands)),
        in_specs=[HBM] * (2 * n) + [SEM, SEM, pl.BlockSpec(memory_space=pl.ANY)], out_specs=tuple([HBM] * (2 * n)),
        input_output_aliases={i: i for i in range(2 * n)},
        compiler_params=pltpu.CompilerParams(has_side_effects=DATAFLOW),
    )(*shards, *lands, send_sems, recv_sems, after)
    return list(out[n:])


def forward_start(lands, tag):
    n = len(lands)

    def body(*refs):
        land_refs = refs[:n]
        send_sems, recv_sems = refs[n], refs[n + 1]
        token = refs[-1]
        x, y, c = _place()
        for i in range(n):
            rows = lands[i].shape[1]
            for j, (px, py) in enumerate(_other_chips(x, y)):
                mine = _half_of_slot(land_refs[i], rows, px, py, c)
                _remote(mine, mine, send_sems.at[j * n + i], recv_sems.at[j * n + i], (x, y, 1 - c)).start()
        token[...] = jnp.zeros_like(token)

    dma = pltpu.SemaphoreType.DMA
    return pl.pallas_call(
        body, name="forward_start_" + tag,
        out_shape=(dma((3 * n,)), dma((3 * n,)), *[pltpu.HBM(a.shape, a.dtype) for a in lands],
                   jax.ShapeDtypeStruct((8, LANES), F32)),
        in_specs=[HBM] * n, out_specs=(SEM, SEM, *[HBM] * n, pl.BlockSpec(memory_space=pltpu.VMEM)),
        input_output_aliases={i: 2 + i for i in range(n)},
        compiler_params=pltpu.CompilerParams(has_side_effects=DATAFLOW),
    )(*lands)


def forward_wait(started, after, tag):
    send_sems, recv_sems, *rest = started
    lands = rest[:-1]
    n = len(lands)

    def body(*refs):
        land_refs = refs[:n]
        send_ref, recv_ref = refs[n], refs[n + 1]
        x, y, c = _place()
        for i in range(n):
            rows = lands[i].shape[1]
            for j, (px, py) in enumerate(_other_chips(x, y)):
                cp = _remote(_half_of_slot(land_refs[i], rows, px, py, c), _half_of_slot(land_refs[i], rows, px, py, 1 - c),
                             send_ref.at[j * n + i], recv_ref.at[j * n + i], (x, y, 1 - c))
                cp.wait_send()
                cp.wait_recv()

    out = pl.pallas_call(
        body, name="forward_wait_" + tag,
        out_shape=tuple(pltpu.HBM(a.shape, a.dtype) for a in lands),
        in_specs=[HBM] * n + [SEM, SEM, pl.BlockSpec(memory_space=pl.ANY)], out_specs=tuple([HBM] * n),
        input_output_aliases={i: i for i in range(n)},
        compiler_params=pltpu.CompilerParams(has_side_effects=DATAFLOW),
    )(*lands, send_sems, recv_sems, after)
    return list(out)


def pair_exchange(slabs, name):
    n = len(slabs)

    def body(*refs):
        g_refs, out_refs = refs[:n], refs[n:2 * n]
        send_sems, recv_sems = refs[2 * n:]
        x, y, c = _place()
        copies = []
        for i in range(n):
            hr = slabs[i].shape[1] // 2
            cp = _remote(g_refs[i].at[:, pl.ds((1 - c) * hr, hr), :], out_refs[i], send_sems.at[i], recv_sems.at[i], (x, y, 1 - c))
            cp.start()
            copies.append(cp)
        for cp in copies:
            cp.wait()

    return pl.pallas_call(
        body, name=name, in_specs=[HBM] * n, out_specs=[HBM] * n,
        out_shape=[jax.ShapeDtypeStruct((4, s.shape[1] // 2, s.shape[2]), s.dtype) for s in slabs],
        scratch_shapes=[pltpu.SemaphoreType.DMA((n,)), pltpu.SemaphoreType.DMA((n,))],
    )(*slabs)


def pair_start(slabs, tag):
    n = len(slabs)

    def body(*refs):
        g_refs, land_refs = refs[:n], refs[n:2 * n]
        send_sems, recv_sems = refs[2 * n], refs[2 * n + 1]
        token = refs[-1]
        x, y, c = _place()
        for i in range(n):
            hr = slabs[i].shape[1] // 2
            _remote(g_refs[i].at[:, pl.ds((1 - c) * hr, hr), :], land_refs[i], send_sems.at[i], recv_sems.at[i],
                    (x, y, 1 - c)).start()
        token[...] = jnp.zeros_like(token)

    hbm = lambda a: pltpu.with_memory_space_constraint(a, pltpu.HBM)
    lands = [lax.empty((4, s.shape[1] // 2, s.shape[2]), s.dtype) for s in slabs]
    dma = pltpu.SemaphoreType.DMA
    return pl.pallas_call(
        body, name="pair_start_" + tag,
        out_shape=(dma((n,)), dma((n,)), *[pltpu.HBM(a.shape, a.dtype) for a in list(slabs) + lands],
                   jax.ShapeDtypeStruct((8, LANES), F32)),
        in_specs=[HBM] * (2 * n), out_specs=(SEM, SEM, *[HBM] * (2 * n), pl.BlockSpec(memory_space=pltpu.VMEM)),
        input_output_aliases={i: 2 + i for i in range(2 * n)},
        compiler_params=pltpu.CompilerParams(has_side_effects=DATAFLOW),
    )(*[hbm(a) for a in list(slabs) + lands])


def pair_wait(started, after, tag):
    send_sems, recv_sems, *rest = started
    n = (len(rest) - 1) // 2
    slabs, lands = rest[:n], rest[n:2 * n]

    def body(*refs):
        g_refs, land_refs = refs[:n], refs[n:2 * n]
        send_ref, recv_ref = refs[2 * n], refs[2 * n + 1]
        x, y, c = _place()
        for i in range(n):
            hr = slabs[i].shape[1] // 2
            cp = _remote(g_refs[i].at[:, pl.ds((1 - c) * hr, hr), :], land_refs[i], send_ref.at[i], recv_ref.at[i], (x, y, 1 - c))
            cp.wait_send()
            cp.wait_recv()

    out = pl.pallas_call(
        body, name="pair_wait_" + tag,
        out_shape=tuple(pltpu.HBM(a.shape, a.dtype) for a in list(slabs) + list(lands)),
        in_specs=[HBM] * (2 * n) + [SEM, SEM, pl.BlockSpec(memory_space=pl.ANY)], out_specs=tuple([HBM] * (2 * n)),
        input_output_aliases={i: i for i in range(2 * n)},
        compiler_params=pltpu.CompilerParams(has_side_effects=DATAFLOW),
    )(*slabs, *lands, send_sems, recv_sems, after)
    return list(out[:n]), list(out[n:])


def _tile2(rows, cols):
    fits = lambda r, c: r * c * 4 <= BLOCK_BYTES
    if fits(rows, cols):
        return rows, cols
    for r in (1024, 512, 256, 128, 64):
        if rows % r == 0 and fits(r, cols):
            return r, cols
    return rows, next(cols // k for k in (2, 3, 4, 6, 8, 12, 16) if cols % (k * LANES) == 0 and fits(rows, cols // k))


def pair_add(g, p, c, name):
    _, hr, cols = p.shape
    tm, tc = _tile2(hr, cols)
    per = hr // tm

    def body(c_ref, g_ref, p_ref, o_ref):
        o_ref[...] = (g_ref[...] + p_ref[...]).astype(o_ref.dtype)

    return pl.pallas_call(
        body, name=name,
        grid_spec=pltpu.PrefetchScalarGridSpec(
            num_scalar_prefetch=1, grid=(4, per, cols // tc),
            in_specs=[pl.BlockSpec((None, tm, tc), lambda k, i, j, c_ref: (k, c_ref[0] * per + i, j)),
                      pl.BlockSpec((None, tm, tc), lambda k, i, j, c_ref: (k, i, j))],
            out_specs=pl.BlockSpec((None, tm, tc), lambda k, i, j, c_ref: (k, i, j))),
        out_shape=jax.ShapeDtypeStruct((4, hr, cols), BF16),
        compiler_params=_params(("arbitrary", "arbitrary", "arbitrary")),
    )(c.reshape(1).astype(jnp.int32), g, p)


def scatter_start(sums, tag):
    n = len(sums)

    def body(*refs):
        s_refs, land_refs = refs[:n], refs[n:2 * n]
        send_sems, recv_sems = refs[2 * n], refs[2 * n + 1]
        token = refs[-1]
        x, y, c = _place()
        k = 2 * x + y
        for i in range(n):
            for j, (px, py) in enumerate(_other_chips(x, y)):
                _remote(s_refs[i].at[2 * px + py], land_refs[i].at[k], send_sems.at[j * n + i], recv_sems.at[j * n + i],
                        (px, py, c)).start()
        token[...] = jnp.zeros_like(token)

    hbm = lambda a: pltpu.with_memory_space_constraint(a, pltpu.HBM)
    return pl.pallas_call(
        body, name="scatter_start_" + tag,
        out_shape=(pltpu.SemaphoreType.DMA((3 * n,)), pltpu.SemaphoreType.DMA((3 * n,)),
                   *[pltpu.HBM(s.shape, s.dtype) for s in sums], *[pltpu.HBM(s.shape, s.dtype) for s in sums],
                   jax.ShapeDtypeStruct((8, LANES), F32)),
        in_specs=[HBM] * (2 * n), out_specs=(SEM, SEM, *[HBM] * (2 * n), pl.BlockSpec(memory_space=pltpu.VMEM)),
        input_output_aliases={i: 2 + i for i in range(2 * n)},
        compiler_params=pltpu.CompilerParams(has_side_effects=DATAFLOW),
    )(*[hbm(s) for s in sums], *[hbm(lax.empty(s.shape, s.dtype)) for s in sums])


def scatter_wait(started, after, tag):
    send_sems, recv_sems, *rest = started
    n = (len(rest) - 1) // 2
    sums, lands = rest[:n], rest[n:2 * n]

    def body(*refs):
        s_refs, land_refs = refs[:n], refs[n:2 * n]
        send_ref, recv_ref = refs[2 * n], refs[2 * n + 1]
        x, y, c = _place()
        for i in range(n):
            for j, (px, py) in enumerate(_other_chips(x, y)):
                cp = _remote(s_refs[i].at[2 * px + py], land_refs[i].at[2 * px + py], send_ref.at[j * n + i],
                             recv_ref.at[j * n + i], (px, py, c))
                cp.wait_send()
                cp.wait_recv()

    out = pl.pallas_call(
        body, name="scatter_wait_" + tag,
        out_shape=tuple(pltpu.HBM(s.shape, s.dtype) for s in sums + lands),
        in_specs=[HBM] * (2 * n) + [SEM, SEM, pl.BlockSpec(memory_space=pl.ANY)], out_specs=tuple([HBM] * (2 * n)),
        input_output_aliases={i: i for i in range(2 * n)},
        compiler_params=pltpu.CompilerParams(has_side_effects=DATAFLOW),
    )(*sums, *lands, send_sems, recv_sems, after)
    return list(out[n:])


def sum_chips(landed, own, chip, core, name):
    _, hr, cols = landed.shape
    tm, tc = _tile2(hr, cols)
    per = hr // tm

    def body(idx_ref, l0, l1, l2, l3, own_ref, o_ref):
        mine = own_ref[...].astype(F32)
        v = [jnp.where(idx_ref[0] == k, mine, ref[...].astype(F32)) for k, ref in enumerate((l0, l1, l2, l3))]
        o_ref[...] = ((v[0] + v[1]) + v[2]) + v[3]

    slot = lambda k: pl.BlockSpec((None, tm, tc),
                                  lambda i, j, idx: (jnp.where(idx[0] == k, (k + 1) & 3, k), i, j))
    return pl.pallas_call(
        body, name=name,
        grid_spec=pltpu.PrefetchScalarGridSpec(
            num_scalar_prefetch=1, grid=(per, cols // tc),
            in_specs=[slot(0), slot(1), slot(2), slot(3),
                      pl.BlockSpec((None, tm, tc), lambda i, j, idx: (idx[0], i, j))],
            out_specs=pl.BlockSpec((tm, tc), lambda i, j, idx: (idx[1] * per + i, j))),
        out_shape=jax.ShapeDtypeStruct((2 * hr, cols), F32),
        compiler_params=_params(("arbitrary", "arbitrary")),
    )(jnp.stack([chip, core]).astype(jnp.int32), landed, landed, landed, landed, own)


def exchange_halves(bufs):
    n = len(bufs)

    def body(*refs):
        out_refs = refs[n:2 * n]
        send_sems, recv_sems = refs[2 * n:]
        x, y, c = _place()
        sends, recvs = [], []
        for i in range(n):
            hr = bufs[i].shape[0] // 2
            own = out_refs[i].at[pl.ds(c * hr, hr), :]
            other = out_refs[i].at[pl.ds((1 - c) * hr, hr), :]
            sends.append(_remote(own, own, send_sems.at[i], recv_sems.at[i], (x, y, 1 - c)))
            recvs.append(_remote(other, other, send_sems.at[i], recv_sems.at[i], (x, y, 1 - c)))
        for cp in sends:
            cp.start()
        for cp in recvs:
            cp.wait_recv()
        for cp in sends:
            cp.wait_send()

    return pl.pallas_call(
        body, name="exchange_halves", in_specs=[HBM] * n, out_specs=[HBM] * n,
        out_shape=[jax.ShapeDtypeStruct(b.shape, b.dtype) for b in bufs],
        input_output_aliases={i: i for i in range(n)},
        scratch_shapes=[pltpu.SemaphoreType.DMA((n,)), pltpu.SemaphoreType.DMA((n,))],
    )(*bufs)


def _relayout(name, arrays, in_blocks, out_blocks, out_shapes, fn):
    rows = 128
    spec = lambda blk: pl.BlockSpec(blk, (lambda i: (0, i, 0)) if len(blk) == 3 else (lambda i: (i, 0)))

    def body(*refs):
        n_in = len(arrays)
        outs = fn(*[r[...] for r in refs[:n_in]])
        for ref, val in zip(refs[n_in:], outs, strict=True):
            if isinstance(val, list):
                for k, piece in enumerate(val):
                    ref[k] = piece
            else:
                ref[...] = val

    return pl.pallas_call(
        body, name=name, grid=(D // rows,),
        in_specs=[spec(b) for b in in_blocks], out_specs=[spec(b) for b in out_blocks], out_shape=out_shapes,
        compiler_params=_params(("arbitrary",)),
    )(*arrays)


def assemble_in_proj(g):
    def fn(v):
        w = jnp.concatenate([v[k] for k in range(4)], axis=1)
        return (jnp.concatenate([w[:, :ORIG_Z], w[:, ORIG_GA:], w[:, ORIG_XBC:ORIG_DT], w[:, ORIG_Z:ORIG_XBC],
                                 w[:, ORIG_DT:ORIG_GA], jnp.zeros((w.shape[0], IN_PAD - IN_ORIG), w.dtype)], axis=1),)

    cols = g.shape[2]
    return _relayout("assemble_in_proj", [g], [(4, 128, cols)], [(128, IN_PAD)],
                     [jax.ShapeDtypeStruct((D, IN_PAD), g.dtype)], fn)[0]


def split_in_proj(dw):
    cols = IN_ORIG // 4

    def fn(d):
        w = jnp.concatenate([d[:, :COL_GA], d[:, COL_Z:COL_DT], d[:, COL_XBC:COL_Z], d[:, COL_DT:COL_DT + 32],
                             d[:, COL_GA:COL_XBC]], axis=1)
        return ([w[:, k * cols:(k + 1) * cols] for k in range(4)],)

    return _relayout("split_in_proj", [dw], [(128, IN_PAD)], [(4, 128, cols)],
                     [jax.ShapeDtypeStruct((4, D, cols), dw.dtype)], fn)[0]


def ada_prepare(c_all, w_ada, hgrn_lb):
    def body(c_ref, w_ref, lb_ref, mod_ref, row_ref):
        mod_ref[...] = hdot(silu(c_ref[...]), w_ref[...])
        row_ref[...] = sigmoid(lb_ref[0:1, :] - lb_ref[1:2, :])

    return pl.pallas_call(
        body, name="ada_prepare",
        out_shape=[jax.ShapeDtypeStruct((8, w_ada.shape[1]), F32), jax.ShapeDtypeStruct((1, D), F32)],
        compiler_params=pltpu.CompilerParams(vmem_limit_bytes=VMEM_LIMIT),
    )(c_all, w_ada, hgrn_lb)


SMALL_SEGS = (("mod", 6 * D), ("lb", D), ("gnorm", LANES), ("conv_w", 4 * CONV_DIM), ("conv_b", CONV_DIM),
              ("dt_bias", B_INNER), ("a_log", B_INNER), ("d", B_INNER), ("ssm_norm", B_INNER),
              ("ln1_g", D), ("ln1_b", D), ("ln2_g", D), ("ln2_b", D))
SMALL_PARAMS = ("b_ada", "hgrn_lb", "hgrn_gnorm", "ssm_conv_b", "ssm_dt_bias", "ssm_a_log", "ssm_d", "ssm_norm",
                "ln1_g", "ln1_b", "ln2_g", "ln2_b")


def finalize_small(g_all, c_all, dmod_cols, params, m, v):
    n_p = len(SMALL_PARAMS)
    offs, o = {}, 0
    for nm, width in SMALL_SEGS:
        offs[nm] = (o, width)
        o += width

    def body(*refs):
        g_ref, c_ref, dm_ref = refs[:3]
        p_refs = refs[3:3 + n_p]
        m_refs = refs[3 + n_p:3 + 2 * n_p]
        v_refs = refs[3 + 2 * n_p:3 + 3 * n_p]
        outs = refs[3 + 3 * n_p:]
        gwa_ref, gcw_ref = outs[:2]
        res = outs[2:]
        total = jnp.sum(g_ref[...], axis=0, keepdims=True)
        seg = lambda nm: total[:, offs[nm][0]:offs[nm][0] + offs[nm][1]]
        gwa_ref[...] = hdot(silu(c_ref[...]), dm_ref[...], "tn")
        cw = seg("conv_w")
        for j in range(4):
            gcw_ref[j:j + 1, :] = cw[:, j * CONV_DIM:(j + 1) * CONV_DIM]
        hc = lax.broadcasted_iota(jnp.int32, (B_INNER, LANES), 0)
        hj = lax.broadcasted_iota(jnp.int32, (B_INNER, LANES), 1)
        per_head = ((hc >> 6) == hj).astype(F32)
        heads = lambda nm: hdot(jnp.broadcast_to(seg(nm), (8, B_INNER)), per_head)[0:1, 0:32]
        lbp = sigmoid(p_refs[1][0:1, :] - p_refs[1][1:2, :])
        g_row = seg("lb") * lbp * (1.0 - lbp)
        grads = {"b_ada": seg("mod"), "hgrn_gnorm": seg("gnorm"), "ssm_conv_b": seg("conv_b"),
                 "ssm_dt_bias": heads("dt_bias"), "ssm_a_log": heads("a_log"), "ssm_d": heads("d"),
                 "ssm_norm": seg("ssm_norm"), "ln1_g": seg("ln1_g"), "ln1_b": seg("ln1_b"),
                 "ln2_g": seg("ln2_g"), "ln2_b": seg("ln2_b")}
        for i, nm in enumerate(SMALL_PARAMS):
            g_out, d_out, m_out, v_out = res[4 * i:4 * i + 4]
            if nm == "hgrn_lb":
                for row, gv in ((0, g_row), (1, -g_row)):
                    sl = slice(row, row + 1)
                    dl, mn, vn = adamw(p_refs[i][sl, :], gv, m_refs[i][sl, :], v_refs[i][sl, :])
                    g_out[sl, :], d_out[sl, :], m_out[sl, :], v_out[sl, :] = gv, dl, mn, vn
            else:
                gv = grads[nm]
                dl, mn, vn = adamw(p_refs[i][...], gv, m_refs[i][...], v_refs[i][...])
                g_out[...], d_out[...], m_out[...], v_out[...] = gv, dl, mn, vn

    out_shape = [jax.ShapeDtypeStruct((D, dmod_cols.shape[1]), F32), jax.ShapeDtypeStruct((4, CONV_DIM), F32)]
    for p in params:
        out_shape += [jax.ShapeDtypeStruct(p.shape, F32)] * 4
    return pl.pallas_call(
        body, name="finalize_small", out_shape=out_shape,
        compiler_params=pltpu.CompilerParams(vmem_limit_bytes=VMEM_LIMIT),
    )(g_all, c_all, dmod_cols, *params, *m, *v)


def adam_update(w, g, m, v, name):
    rows, cols = w.shape
    tm, tc = _tile2(rows, cols)

    def body(w_ref, g_ref, m_ref, v_ref, d_ref, mo_ref, vo_ref):
        d_ref[...], mo_ref[...], vo_ref[...] = adamw(w_ref[...], g_ref[...], m_ref[...], v_ref[...])

    spec = pl.BlockSpec((tm, tc), lambda i, j: (i, j))
    return pl.pallas_call(
        body, name=name, grid=(rows // tm, cols // tc), in_specs=[spec] * 4, out_specs=[spec] * 3,
        out_shape=[jax.ShapeDtypeStruct((rows, cols), F32)] * 3,
        compiler_params=_params(("arbitrary", "arbitrary")),
    )(w, g, m, v)


def kernel(x, c, w_ada, b_ada, w_in, hgrn_lb, hgrn_gnorm, ssm_conv_w, ssm_conv_b, ssm_dt_bias, ssm_a_log, ssm_d, ssm_norm, w_branch_a, w_branch_b, w_o, ln1_g, ln1_b, w_ffn_gate, w_ffn_up, w_ffn_down, ln2_g, ln2_b, loss_target, m_w_ada, m_b_ada, m_w_in, m_hgrn_lb, m_hgrn_gnorm, m_ssm_conv_w, m_ssm_conv_b, m_ssm_dt_bias, m_ssm_a_log, m_ssm_d, m_ssm_norm, m_w_branch_a, m_w_branch_b, m_w_o, m_ln1_g, m_ln1_b, m_w_ffn_gate, m_w_ffn_up, m_w_ffn_down, m_ln2_g, m_ln2_b, v_w_ada, v_b_ada, v_w_in, v_hgrn_lb, v_hgrn_gnorm, v_ssm_conv_w, v_ssm_conv_b, v_ssm_dt_bias, v_ssm_a_log, v_ssm_d, v_ssm_norm, v_w_branch_a, v_w_branch_b, v_w_o, v_ln1_g, v_ln1_b, v_w_ffn_gate, v_w_ffn_up, v_w_ffn_down, v_ln2_g, v_ln2_b):
    given = dict(locals())
    chip = 2 * lax.axis_index("x") + lax.axis_index("y")
    core = lax.axis_index("c")
    t = x.shape[1]

    first = gather_rows(jnp.concatenate([c, ssm_conv_w.reshape(1, CONV_DIM)], axis=1), "gather_cond").reshape(8, D + CONV_DIM)
    c_all = first[:, :D]
    conv_w = first[0::2, D:].reshape(4, 4, CONV_DIM // 4).transpose(1, 0, 2).reshape(4, CONV_DIM)
    mod_part, lb_row = ada_prepare(c_all, w_ada[0], hgrn_lb)
    mod_cols = w_ada.shape[2]
    mod_row = exchange_rows(mod_part.reshape(8, 1, mod_cols), "exchange_mod").reshape(1, 6 * D) + b_ada
    mod = tuple(mod_row[:, i * D:(i + 1) * D] for i in range(6))

    local = {nm: given[nm][0].T if nm in TRANSPOSED else given[nm][0] for nm in SHARDED}
    shards = [local[nm].astype(BF16) for nm in SHARDED]
    n_w = len(SHARDED)
    send_in, recv_in, send_rest, recv_rest, *flying = gather_start(shards, mod_row)
    sent, lands = flying[:n_w], flying[n_w:2 * n_w]
    with_own = lambda land, shard: lax.dynamic_update_slice(land, shard[None], (chip, 0, 0))

    class Weights:
        def input_projection(self, after):
            land = gather_wait(send_in, recv_in, sent[:1], lands[:1], after, "in")
            (land,) = forward_wait(forward_start(land, "in"), after, "in")
            return assemble_in_proj(with_own(land, shards[0]))

        def start_rest(self, after):
            self.started = forward_start(gather_wait(send_rest, recv_rest, sent[1:], lands[1:], after, "rest"), "rest")
            return self.started[-1]

        def rest(self, after):
            got = {nm: with_own(land, s) for nm, land, s in zip(SHARDED[1:], forward_wait(self.started, after, "rest"), shards[1:], strict=True)}
            whole = lambda nm: got[nm].reshape(4 * got[nm].shape[1], got[nm].shape[2])
            return (whole("w_branch_a"), whole("w_branch_b"), whole("w_o"),
                    jnp.concatenate([whole("w_ffn_gate"), whole("w_ffn_up")], axis=0), whole("w_ffn_down"))

    wts = Weights()

    per_channel = lambda p: jnp.repeat(p[0], B_INNER // 32)[None]
    small = (lb_row, hgrn_gnorm, conv_w, ssm_conv_b, per_channel(ssm_dt_bias), per_channel(ssm_a_log),
             per_channel(ssm_d), ssm_norm, ln1_g, ln1_b, ln2_g, ln2_b)
    by_rows = lambda g: g.reshape(4, g.shape[0] // 4, g.shape[1])
    travelling = {}

    def pair_sums(names, slabs, tag):
        received = pair_exchange(slabs, "pair_exchange_" + tag)
        return [pair_add(s, r, core, "pair_add_" + nm) for nm, s, r in zip(names, slabs, received, strict=True)]

    def start_early(dws):
        dw_a, dw_b, dw_o, dw_gu, dw_d = dws
        d_gate, d_up = by_rows(dw_gu[:D_FF]), by_rows(dw_gu[D_FF:])
        travelling["pair"] = pair_start([by_rows(dw_a), by_rows(dw_b), by_rows(dw_o), d_gate, d_up, by_rows(dw_d)], "early")
        return travelling["pair"][-1]

    def between_scans(after):
        slabs, received = pair_wait(travelling["pair"], after, "early")
        travelling["pairs"] = [pair_add(s, r, core, "pair_add_" + nm) for nm, s, r in zip(SHARDED[1:], slabs, received, strict=True)]
        travelling["started"] = scatter_start(travelling["pairs"], "early")
        return travelling["started"][-1]

    def finish_early(after):
        travelling["landed"] = scatter_wait(travelling["started"], after, "early")

    def start_last(dw_in):
        travelling["pairs_in"] = pair_sums(SHARDED[:1], [split_in_proj(dw_in)], "last")
        travelling["started_in"] = scatter_start(travelling["pairs_in"], "last")
        return travelling["started_in"][-1]

    loss, grad_x, d_mod, d_wts, d_small = local_step(x[0], loss_target[0], mod, wts, small,
                                                     start_early, between_scans, finish_early, start_last)

    d_lb, d_gn, d_cw, d_cb, d_dtb, d_alog, d_dsk, d_nw, d_l1g, d_l1b, d_l2g, d_l2b = d_small
    row = jnp.concatenate(list(d_mod) + [d_lb, d_gn, d_cw.reshape(1, 4 * CONV_DIM), d_cb, d_dtb, d_alog, d_dsk, d_nw,
                                          d_l1g, d_l1b, d_l2g, d_l2b], axis=1)
    g_all = gather_rows(row, "gather_small_grads").reshape(8, row.shape[1])
    dmod_cols = lax.dynamic_slice_in_dim(g_all, chip * mod_cols, mod_cols, axis=1)
    fin = finalize_small(g_all, c_all, dmod_cols, [given[n] for n in SMALL_PARAMS],
                         [given["m_" + n] for n in SMALL_PARAMS], [given["v_" + n] for n in SMALL_PARAMS])
    grads, deltas, new_m, new_v = {}, {}, {}, {}
    grads["w_ada"] = fin[0][None]
    grads["ssm_conv_w"] = lax.dynamic_slice_in_dim(fin[1], chip * (CONV_DIM // 4), CONV_DIM // 4, axis=1)[None]
    for i, nm in enumerate(SMALL_PARAMS):
        grads[nm], deltas[nm], new_m[nm], new_v[nm] = fin[2 + 4 * i:6 + 4 * i]

    pairs = travelling["pairs_in"] + travelling["pairs"]
    landed = scatter_wait(travelling["started_in"], fin[2], "last") + travelling["landed"]
    halves = [sum_chips(r, p, chip, core, "sum_chips_" + nm) for nm, r, p in zip(SHARDED, landed, pairs, strict=True)]
    reduced = dict(zip(SHARDED, exchange_halves(halves), strict=True))
    reduced["w_ada"], reduced["ssm_conv_w"] = grads["w_ada"][0], grads["ssm_conv_w"][0]
    reduced["w_in"] = reduced["w_in"].T
    for nm in ("w_ada", "ssm_conv_w") + SHARDED:
        flipped = nm in TRANSPOSED or nm == "w_in"
        work = (lambda a: a[0].T) if flipped else (lambda a: a[0])
        back = (lambda a: a.T[None]) if flipped else (lambda a: a[None])
        d_, m_, v_ = adam_update(work(given[nm]), reduced[nm], work(given["m_" + nm]), work(given["v_" + nm]), "adam_" + nm)
        grads[nm], deltas[nm], new_m[nm], new_v[nm] = back(reduced[nm]), back(d_), back(m_), back(v_)

    names = ("w_ada", "b_ada", "w_in", "hgrn_lb", "hgrn_gnorm", "ssm_conv_w", "ssm_conv_b", "ssm_dt_bias", "ssm_a_log",
             "ssm_d", "ssm_norm", "w_branch_a", "w_branch_b", "w_o", "ln1_g", "ln1_b", "w_ffn_gate", "w_ffn_up",
             "w_ffn_down", "ln2_g", "ln2_b")
    total_loss = lax.psum(loss[0, 0], ("x", "y", "c"))
    return (total_loss, grad_x[None], *[grads[n] for n in names], *[deltas[n] for n in names],
            *[new_m[n] for n in names], *[new_v[n] for n in names])
```

```python
import functools

import jax
import jax.numpy as jnp
from jax import lax
from jax.experimental import pallas as pl
from jax.experimental.pallas import tpu as pltpu

F32, BF16 = jnp.float32, jnp.bfloat16
HI = lax.Precision.HIGHEST
MESH = pl.DeviceIdType.MESH

D = 1024
CHUNK = 64
LANES = 128
N_HEADS_A = 8
N_GROUPS_B = 4
B_INNER = 2048
CONV_DIM = 3072
D_FF = 2816
ALPHA = 2.0 ** 0.25
LN_EPS = 1e-5
RMS_EPS = 1e-6
ADAM_LR, ADAM_B1, ADAM_B2, ADAM_EPS, ADAM_WD, ADAM_STEP = 0.001, 0.9, 0.999, 1e-08, 0.01, 10

IN_ORIG = 11296
IN_PAD = 11520
COL_GA, COL_GB, COL_XBC, COL_Z, COL_DT = 4096, 5120, 6144, 9216, 11264
ORIG_Z, ORIG_XBC, ORIG_DT, ORIG_GA = 4096, 6144, 9216, 9248

SHARDED = ("w_in", "w_branch_a", "w_branch_b", "w_o", "w_ffn_gate", "w_ffn_up", "w_ffn_down")
TRANSPOSED = ("w_ffn_gate", "w_ffn_up")
VMEM_LIMIT = 56 * 1024 * 1024
BLOCK_BYTES = 2 * 1024 * 1024
_DIMS = {"nn": (((1,), (0,)), ((), ())), "nt": (((1,), (1,)), ((), ())), "tn": (((0,), (0,)), ((), ()))}


def _bd(a, b, mode):
    return lax.dot_general(a.astype(BF16), b.astype(BF16), _DIMS[mode], preferred_element_type=F32)


@functools.partial(jax.custom_vjp, nondiff_argnums=(2,))
def bdot(a, b, mode):
    return _bd(a, b, mode)


def _bdot_fwd(a, b, mode):
    return _bd(a, b, mode), (a, b)


def _bdot_bwd(mode, res, g):
    a, b = res
    if mode == "nn":
        return _bd(g, b, "nt"), _bd(a, g, "tn")
    if mode == "nt":
        return _bd(g, b, "nn"), _bd(g, a, "tn")
    return _bd(b, g, "nt"), _bd(a, g, "nn")


bdot.defvjp(_bdot_fwd, _bdot_bwd)


def hdot(a, b, mode="nn"):
    return lax.dot_general(a, b, _DIMS[mode], precision=HI, preferred_element_type=F32)


def _raw(a, b, mode):
    return lax.dot_general(a, b, _DIMS[mode], preferred_element_type=F32)


def _split(x, n):
    parts, rest = [], x
    for _ in range(n):
        p = rest.astype(BF16)
        parts.append(p)
        rest = rest - p.astype(F32)
    return parts


def _od(a, b, mode, exact):
    if exact == 1:
        e = b.astype(BF16)
        p = _split(a, 3)
        return (_raw(p[2], e, mode) + _raw(p[1], e, mode)) + _raw(p[0], e, mode)
    e = a.astype(BF16)
    p = _split(b, 3)
    return (_raw(e, p[2], mode) + _raw(e, p[1], mode)) + _raw(e, p[0], mode)


@functools.partial(jax.custom_vjp, nondiff_argnums=(2, 3))
def odot(a, b, mode, exact):
    return _od(a, b, mode, exact)


def _odot_fwd(a, b, mode, exact):
    return _od(a, b, mode, exact), (a, b)


def _odot_bwd(mode, exact, res, g):
    a, b = res
    if exact == 1:
        da = {"nn": lambda: _od(g, b, "nt", 1), "nt": lambda: _od(g, b, "nn", 1), "tn": lambda: _od(b, g, "nt", 0)}[mode]()
        return da, jnp.zeros_like(b)
    db = {"nn": lambda: _od(a, g, "tn", 0), "nt": lambda: _od(g, a, "tn", 1), "tn": lambda: _od(a, g, "nn", 0)}[mode]()
    return jnp.zeros_like(a), db


odot.defvjp(_odot_fwd, _odot_bwd)


_BDIMS = {"bnn": (((2,), (1,)), ((0,), (0,))), "bnt": (((2,), (2,)), ((0,), (0,))), "btn": (((1,), (1,)), ((0,), (0,)))}


def _braw(a, b, mode):
    return lax.dot_general(a, b, _BDIMS[mode], preferred_element_type=F32)


def _bdb(a, b, mode):
    return _braw(a.astype(BF16), b.astype(BF16), mode)


def _d3b(a, b, mode):
    ah, al = _split(a, 2)
    bh, bl = _split(b, 2)
    return _braw(ah, bh, mode) + (_braw(ah, bl, mode) + _braw(al, bh, mode))


def _batched_bwd(f):
    def bwd(mode, res, g):
        a, b = res
        if mode == "bnn":
            return f(g, b, "bnt"), f(a, g, "btn")
        if mode == "bnt":
            return f(g, b, "bnn"), f(g, a, "btn")
        return f(b, g, "bnt"), f(a, g, "bnn")
    return bwd


@functools.partial(jax.custom_vjp, nondiff_argnums=(2,))
def bdot_b(a, b, mode):
    return _bdb(a, b, mode)


bdot_b.defvjp(lambda a, b, mode: (_bdb(a, b, mode), (a, b)), _batched_bwd(_bdb))


@functools.partial(jax.custom_vjp, nondiff_argnums=(2,))
def dot3_b(a, b, mode):
    return _d3b(a, b, mode)


dot3_b.defvjp(lambda a, b, mode: (_d3b(a, b, mode), (a, b)), _batched_bwd(_d3b))


def _cum(tril3, x, mode):
    e = tril3.astype(BF16)
    p = _split(x, 3)
    return (_braw(e, p[2], mode) + _braw(e, p[1], mode)) + _braw(e, p[0], mode)


@jax.custom_vjp
def chunk_cumsum(tril3, x):
    return _cum(tril3, x, "bnn")


chunk_cumsum.defvjp(lambda t, x: (_cum(t, x, "bnn"), t), lambda t, g: (jnp.zeros_like(t), _cum(t, g, "btn")))


def _unstack(axis, n):
    @jax.custom_vjp
    def un(x):
        return tuple(lax.index_in_dim(x, i, axis, keepdims=False) for i in range(n))

    un.defvjp(lambda x: (un(x), None), lambda _, g: (jnp.stack(g, axis=axis),))
    return un


def _split_last(n, w):
    @jax.custom_vjp
    def sp(x):
        return tuple(x[..., i * w:(i + 1) * w] for i in range(n))

    sp.defvjp(lambda x: (sp(x), None), lambda _, g: (jnp.concatenate(g, axis=-1),))
    return sp


def sigmoid(x):
    return 1.0 / (1.0 + jnp.exp(-x))


def silu(x):
    return x * sigmoid(x)


def softplus(x):
    return jnp.maximum(x, 0.0) + jnp.log1p(jnp.exp(jnp.minimum(x, -x)))


def _ln(x):
    mu = jnp.mean(x, axis=-1, keepdims=True)
    xc = x - mu
    return xc * lax.rsqrt(jnp.mean(xc * xc, axis=-1, keepdims=True) + LN_EPS)


def _tril64():
    r = lax.broadcasted_iota(jnp.int32, (CHUNK, CHUNK), 0)
    c = lax.broadcasted_iota(jnp.int32, (CHUNK, CHUNK), 1)
    return (r >= c).astype(F32)


def hgrn_block(q, fl, iv, gr, st, lb, gn):
    tb = q.shape[0]
    nc = tb // CHUNK
    nh = N_HEADS_A
    heads = _split_last(nh, LANES)
    to4 = lambda a: jnp.stack(heads(a), axis=0).reshape(nh, nc, CHUNK, LANES)
    flat = lambda a: a.reshape(nh * nc, CHUNK, LANES)
    f = lb + (1.0 - lb) * sigmoid(fl)
    gl4, k4, qf4, v4, gr4 = to4(jnp.log(f)), to4(1.0 - f), to4(silu(q) * (128 ** -0.5)), to4(iv), to4(gr)
    tril = _tril64()
    b4 = chunk_cumsum(jnp.broadcast_to(tril[None], (nh * nc, CHUNK, CHUNK)), flat(gl4)).reshape(gl4.shape)
    blast = jnp.sum(gl4, axis=2, keepdims=True)
    ref = lax.stop_gradient(0.5 * blast)
    sc = dot3_b(flat(qf4 * jnp.exp(b4 - ref)), flat(k4 * jnp.exp(ref - b4)), "bnt") * tril
    o_intra = bdot_b(sc, flat(v4), "bnn").reshape(gl4.shape)
    chunks = _unstack(1, nc)
    qe, v_c, kd, dec = chunks(qf4 * jnp.exp(b4)), chunks(v4), chunks(k4 * jnp.exp(blast - b4)), chunks(jnp.exp(blast))
    o_inter = []
    for c in range(nc):
        o_inter.append(bdot_b(qe[c], st, "bnt"))
        st = st * dec[c] + bdot_b(v_c[c], kd[c], "btn")
    o = o_intra + jnp.stack(o_inter, axis=1)
    on = o * lax.rsqrt(jnp.mean(o * o, axis=-1, keepdims=True) + RMS_EPS) * gn
    out = (on * silu(gr4)).reshape(nh, tb, LANES)
    return jnp.concatenate(_unstack(0, nh)(out), axis=1), st


def ssd_consts(g):
    i32 = jnp.int32
    ej = lax.broadcasted_iota(i32, (LANES, 512), 0)
    ec = lax.broadcasted_iota(i32, (LANES, 512), 1)
    expand = (ej == g * 8 + (ec >> 6)).astype(F32)
    ts = lax.broadcasted_iota(i32, (CHUNK, 512), 0)
    tc = lax.broadcasted_iota(i32, (CHUNK, 512), 1)
    itile = (ts == (tc & 63)).astype(F32)
    maskall = ts >= (tc & 63)
    br = lax.broadcasted_iota(i32, (256, 256), 0)
    bc = lax.broadcasted_iota(i32, (256, 256), 1)
    blockmask = ((br >> 6) == (bc >> 6)).astype(F32)
    return expand, itile, maskall, blockmask, _tril64()


def ssd_block(x, bm, cm, dt, z, st, dtb, alog, dsk, nw, cs):
    expand, itile, maskall, blockmask, tril = cs
    tb = x.shape[0]
    nc = tb // CHUNK
    delta_heads = softplus(dt + dtb)
    delta = odot(delta_heads, expand, "nn", 1)
    a = odot(-jnp.exp(alog) * delta_heads, expand, "nn", 1)
    xdt = x * delta
    by_chunk = lambda v: v.reshape(nc, CHUNK, v.shape[-1])
    a3, xdt3, bm3, cm3 = by_chunk(a), by_chunk(xdt), by_chunk(bm), by_chunk(cm)
    acum3 = chunk_cumsum(jnp.broadcast_to(tril[None], (nc, CHUNK, CHUNK)), a3)
    alast3 = jnp.sum(a3, axis=1, keepdims=True)
    cb3 = bdot_b(cm3, jnp.concatenate([bm3] * 8, axis=1), "bnt")
    arow3 = jnp.sum(acum3 * itile, axis=1, keepdims=True)
    dec3 = jnp.where(maskall, jnp.exp(jnp.minimum(acum3 - arow3, 0.0)), 0.0)
    halves = _split_last(2, 256)
    intra = [bdot_b(m, jnp.concatenate([xh] * 4, axis=1) * blockmask, "bnn")
             for m, xh in zip(halves(cb3 * dec3), halves(xdt3))]
    chunks = _unstack(0, nc)
    cm_c, bm_c, xw_c, dec_c = chunks(cm3), chunks(bm3), chunks(xdt3 * jnp.exp(alast3 - acum3)), chunks(jnp.exp(alast3))
    inter = []
    for c in range(nc):
        inter.append(bdot(cm_c[c], st, "nn"))
        st = st * dec_c[c] + bdot(bm_c[c], xw_c[c], "tn")
    st_new = st
    y = (jnp.concatenate(intra, axis=-1) + jnp.stack(inter, axis=0) * jnp.exp(acum3)).reshape(tb, 512)
    yz = (y + x * dsk) * silu(z)
    return yz * lax.rsqrt(jnp.mean(yz * yz, axis=-1, keepdims=True) + RMS_EPS) * nw, st_new


def adamw(w, g, m, v):
    m = ADAM_B1 * m + (1.0 - ADAM_B1) * g
    v = ADAM_B2 * v + (1.0 - ADAM_B2) * jnp.square(g)
    m_hat = m / (1.0 - ADAM_B1 ** ADAM_STEP)
    v_hat = v / (1.0 - ADAM_B2 ** ADAM_STEP)
    return -ADAM_LR * (m_hat / (jnp.sqrt(v_hat) + ADAM_EPS) + ADAM_WD * w), m, v


def _pick(n, cands):
    for c in cands:
        if n % c == 0:
            return c
    return n


def _params(sem):
    return pltpu.CompilerParams(dimension_semantics=sem, vmem_limit_bytes=VMEM_LIMIT)


def matmul(a, b, mode, out_dtype, name, after=None):
    if mode == "nn":
        (m, k), n = a.shape, b.shape[1]
    elif mode == "nt":
        (m, k), n = a.shape, b.shape[0]
    else:
        (k, m), n = a.shape, b.shape[1]
    tm = _pick(m, (1408, 1024, 768, 512, 256, 128))
    tn = _pick(n, (1408, 1024, 768, 512, 256, 128))
    tk = _pick(k, (2304, 2048, 1408, 1024, 768, 512, 256, 128))
    nk = k // tk
    a_spec = pl.BlockSpec((tk, tm), lambda i, j, kk: (kk, i)) if mode == "tn" else pl.BlockSpec((tm, tk), lambda i, j, kk: (i, kk))
    b_spec = pl.BlockSpec((tn, tk), lambda i, j, kk: (j, kk)) if mode == "nt" else pl.BlockSpec((tk, tn), lambda i, j, kk: (kk, j))

    order = [] if after is None else [after]

    def body(a_ref, b_ref, *rest):
        o_ref, *acc = rest[len(order):]
        part = _bd(a_ref[...], b_ref[...], mode)
        if nk == 1:
            o_ref[...] = part.astype(o_ref.dtype)
            return
        acc_ref, = acc
        kk = pl.program_id(2)

        @pl.when(kk == 0)
        def _():
            acc_ref[...] = part

        @pl.when(jnp.logical_and(kk > 0, kk < nk - 1))
        def _():
            acc_ref[...] += part

        @pl.when(kk == nk - 1)
        def _():
            o_ref[...] = (acc_ref[...] + part).astype(o_ref.dtype)

    return pl.pallas_call(
        body, name=name, grid=(m // tm, n // tn, nk),
        in_specs=[a_spec, b_spec] + [pl.BlockSpec(memory_space=pl.ANY) for _ in order],
        out_specs=pl.BlockSpec((tm, tn), lambda i, j, kk: (i, j)),
        out_shape=jax.ShapeDtypeStruct((m, n), out_dtype),
        scratch_shapes=[pltpu.VMEM((tm, tn), F32)] if nk > 1 else [],
        compiler_params=_params(("parallel", "parallel", "arbitrary")),
    )(a, b, *order)


def rowwise(name, fn, rows, consts, out_rows, out_accs=(), tm_max=256, into=None, new_wide=None):
    t = rows[0][0].shape[0]
    tm = _pick(t, (tm_max, 128, 64, 32, 16, 8))
    n_r, n_c, n_o = len(rows), len(consts), len(out_rows)
    n_alias = 0 if into is None else 1

    def body(*refs):
        r_in = [r[...] for r in refs[:n_r]]
        c_in = [r[...] for r in refs[n_r:n_r + n_c]]
        refs = refs[:n_r + n_c] + refs[n_r + n_c + n_alias:]
        o_refs = refs[n_r + n_c:n_r + n_c + n_o]
        a_refs = refs[n_r + n_c + n_o:]
        ro, ao = fn(r_in, c_in)
        for ref, val in zip(o_refs, ro, strict=True):
            ref[...] = val.astype(ref.dtype)
        if a_refs:
            @pl.when(pl.program_id(0) == 0)
            def _():
                for ref in a_refs:
                    ref[...] = jnp.zeros_like(ref)

            for ref, val in zip(a_refs, ao, strict=True):
                ref[...] += val

    in_specs = [pl.BlockSpec((tm, w), functools.partial(lambda i, cb: (i, cb), cb=cb)) for _, w, cb in rows]
    in_specs += [pl.BlockSpec(c.shape, lambda i: (0, 0)) for c in consts]
    out_specs = [pl.BlockSpec((tm, w), lambda i: (i, 0)) for w, _ in out_rows]
    out_specs += [pl.BlockSpec(s, lambda i: (0, 0)) for s in out_accs]
    out_shape = [jax.ShapeDtypeStruct((t, w), dt) for w, dt in out_rows]
    out_shape += [jax.ShapeDtypeStruct(s, F32) for s in out_accs]
    operands = [r[0] for r in rows] + list(consts)
    aliases = {}
    if into is not None:
        target, cb = into
        in_specs.append(pl.BlockSpec(memory_space=pl.ANY))
        operands.append(target)
        out_specs[0] = pl.BlockSpec((tm, out_rows[0][0]), lambda i: (i, cb))
        out_shape[0] = jax.ShapeDtypeStruct(target.shape, target.dtype)
        aliases = {len(operands) - 1: 0}
    if new_wide is not None:
        width, cb = new_wide
        out_specs[0] = pl.BlockSpec((tm, out_rows[0][0]), lambda i: (i, cb))
        out_shape[0] = jax.ShapeDtypeStruct((t, width), out_rows[0][1])
    return pl.pallas_call(
        body, name=name, grid=(t // tm,), in_specs=in_specs, out_specs=out_specs, out_shape=out_shape,
        input_output_aliases=aliases, compiler_params=_params(("arbitrary",)),
    )(*operands)


def _full(a):
    return (a, a.shape[1], 0)


def _time_block(t):
    return _pick(t, (256, 128, 64))


def _quarters(ref):
    return [ref[:, seg * D:(seg + 1) * D] for seg in range(4)]


def hgrn_forward(proj, lb, gn):
    t = proj.shape[0]
    tb = _time_block(t)
    nb = t // tb

    def body(qfig_ref, lb_ref, gn_ref, o_ref, st_ref, state):
        @pl.when(pl.program_id(0) == 0)
        def _():
            state[...] = jnp.zeros_like(state)

        st = state[...]
        st_ref[...] = st
        out, st_new = hgrn_block(*_quarters(qfig_ref), st, lb_ref[...], gn_ref[...])
        o_ref[...] = out.astype(o_ref.dtype)
        state[...] = st_new

    return pl.pallas_call(
        body, name="hgrn_forward", grid=(nb,),
        in_specs=[pl.BlockSpec((tb, 4 * D), lambda j: (j, 0)),
                  pl.BlockSpec((1, D), lambda j: (0, 0)), pl.BlockSpec((1, LANES), lambda j: (0, 0))],
        out_specs=[pl.BlockSpec((tb, D), lambda j: (j, 0)),
                   pl.BlockSpec((None, N_HEADS_A, LANES, LANES), lambda j: (j, 0, 0, 0))],
        out_shape=[jax.ShapeDtypeStruct((t, D), BF16),
                   jax.ShapeDtypeStruct((nb, N_HEADS_A, LANES, LANES), F32)],
        scratch_shapes=[pltpu.VMEM((N_HEADS_A, LANES, LANES), F32)],
        compiler_params=_params(("arbitrary",)),
    )(proj, lb, gn)


def hgrn_backward(proj, states, d_out, lb, gn, d_proj):
    t = proj.shape[0]
    tb = _time_block(t)
    nb = t // tb

    def body(qfig_ref, st_ref, do_ref, lb_ref, gn_ref, _, dqfig_ref, dlb_ref, dgn_ref, d_state):
        @pl.when(pl.program_id(0) == 0)
        def _():
            d_state[...] = jnp.zeros_like(d_state)
            dlb_ref[...] = jnp.zeros_like(dlb_ref)
            dgn_ref[...] = jnp.zeros_like(dgn_ref)

        _, vjp = jax.vjp(hgrn_block, *_quarters(qfig_ref), st_ref[...], lb_ref[...], gn_ref[...])
        dq, df, di, dg, dst, dlb, dgn = vjp((do_ref[...], d_state[...]))
        for seg, val in enumerate((dq, df, di, dg)):
            dqfig_ref[:, seg * D:(seg + 1) * D] = val.astype(dqfig_ref.dtype)
        d_state[...] = dst
        dlb_ref[...] += dlb
        dgn_ref[...] += dgn

    rev = lambda j: nb - 1 - j
    return pl.pallas_call(
        body, name="hgrn_backward", grid=(nb,),
        in_specs=[pl.BlockSpec((tb, 4 * D), lambda j: (rev(j), 0)),
                  pl.BlockSpec((None, N_HEADS_A, LANES, LANES), lambda j: (rev(j), 0, 0, 0)),
                  pl.BlockSpec((tb, D), lambda j: (rev(j), 0)),
                  pl.BlockSpec((1, D), lambda j: (0, 0)), pl.BlockSpec((1, LANES), lambda j: (0, 0)),
                  pl.BlockSpec(memory_space=pl.ANY)],
        out_specs=[pl.BlockSpec((tb, 4 * D), lambda j: (rev(j), 0)),
                   pl.BlockSpec((1, D), lambda j: (0, 0)), pl.BlockSpec((1, LANES), lambda j: (0, 0))],
        out_shape=[jax.ShapeDtypeStruct(d_proj.shape, d_proj.dtype), jax.ShapeDtypeStruct((1, D), F32),
                   jax.ShapeDtypeStruct((1, LANES), F32)],
        input_output_aliases={5: 0},
        scratch_shapes=[pltpu.VMEM((N_HEADS_A, LANES, LANES), F32)],
        compiler_params=_params(("arbitrary",)),
    )(proj, states, d_out, lb, gn, d_proj)


def _ssd_in_specs(tb, tmap):
    return [pl.BlockSpec((tb, 512), lambda g, j: (tmap(j), g)),
            pl.BlockSpec((tb, LANES), lambda g, j: (tmap(j), 16 + g)),
            pl.BlockSpec((tb, LANES), lambda g, j: (tmap(j), 20 + g)),
            pl.BlockSpec((tb, LANES), lambda g, j: (tmap(j), COL_DT // LANES)),
            pl.BlockSpec((tb, 512), lambda g, j: (tmap(j), COL_Z // 512 + g))]


def ssd_forward(xc, proj, dtb, alog, dsk, nw):
    t = proj.shape[0]
    tb = _time_block(t)
    nb = t // tb

    def body(x_ref, b_ref, c_ref, dt_ref, z_ref, dtb_ref, alog_ref, dsk_ref, nw_ref, o_ref, st_ref, state):
        @pl.when(pl.program_id(1) == 0)
        def _():
            state[...] = jnp.zeros_like(state)

        st = state[...]
        st_ref[...] = st
        out, st_new = ssd_block(x_ref[...], b_ref[...], c_ref[...], dt_ref[...], z_ref[...], st,
                                dtb_ref[...], alog_ref[...], dsk_ref[...], nw_ref[...], ssd_consts(pl.program_id(0)))
        o_ref[...] = out.astype(o_ref.dtype)
        state[...] = st_new

    vec = pl.BlockSpec((1, 512), lambda g, j: (0, g))
    heads = pl.BlockSpec((1, LANES), lambda g, j: (0, 0))
    return pl.pallas_call(
        body, name="ssd_forward", grid=(N_GROUPS_B, nb),
        in_specs=_ssd_in_specs(tb, lambda j: j) + [heads, heads, vec, vec],
        out_specs=[pl.BlockSpec((tb, 512), lambda g, j: (j, g)),
                   pl.BlockSpec((None, None, LANES, 512), lambda g, j: (j, g, 0, 0))],
        out_shape=[jax.ShapeDtypeStruct((t, B_INNER), BF16),
                   jax.ShapeDtypeStruct((nb, N_GROUPS_B, LANES, 512), F32)],
        scratch_shapes=[pltpu.VMEM((LANES, 512), F32)],
        compiler_params=_params(("arbitrary", "arbitrary")),
    )(xc, xc, xc, proj, proj, dtb, alog, dsk, nw)


def ssd_backward(xc, proj, states, d_out, dtb, alog, dsk, nw, d_proj):
    t = proj.shape[0]
    tb = _time_block(t)
    nb = t // tb
    rev = lambda j: nb - 1 - j

    def body(x_ref, b_ref, c_ref, dt_ref, z_ref, st_ref, do_ref, dtb_ref, alog_ref, dsk_ref, nw_ref, _,
             dx_ref, db_ref, dc_ref, ddt_ref, dz_ref, ddtb_ref, dalog_ref, ddsk_ref, dnw_ref, d_state):
        accs = (ddtb_ref, dalog_ref, ddsk_ref, dnw_ref)

        @pl.when(pl.program_id(1) == 0)
        def _():
            d_state[...] = jnp.zeros_like(d_state)
            for ref in accs:
                ref[...] = jnp.zeros_like(ref)

        cs = ssd_consts(pl.program_id(0))
        fn = lambda *a: ssd_block(*a, cs)
        _, vjp = jax.vjp(fn, x_ref[...], b_ref[...], c_ref[...], dt_ref[...], z_ref[...], st_ref[...],
                         dtb_ref[...], alog_ref[...], dsk_ref[...], nw_ref[...])
        dx, db, dc, ddt, dz, dst, *dpar = vjp((do_ref[...], d_state[...]))
        dx_ref[...] = dx
        db_ref[...] = db
        dc_ref[...] = dc
        ddt_ref[...] = ddt
        dz_ref[...] = dz.astype(dz_ref.dtype)
        d_state[...] = dst
        for ref, val in zip(accs, dpar, strict=True):
            ref[...] += val

    vec = pl.BlockSpec((1, 512), lambda g, j: (0, g))
    heads = pl.BlockSpec((1, LANES), lambda g, j: (0, 0))
    acc = pl.BlockSpec((None, 1, 512), lambda g, j: (g, 0, 0))
    acc_heads = pl.BlockSpec((None, 1, LANES), lambda g, j: (g, 0, 0))
    return pl.pallas_call(
        body, name="ssd_backward", grid=(N_GROUPS_B, nb),
        in_specs=_ssd_in_specs(tb, rev)
        + [pl.BlockSpec((None, None, LANES, 512), lambda g, j: (rev(j), g, 0, 0)),
           pl.BlockSpec((tb, 512), lambda g, j: (rev(j), g))] + [heads, heads, vec, vec] + [pl.BlockSpec(memory_space=pl.ANY)],
        out_specs=[pl.BlockSpec((tb, 512), lambda g, j: (rev(j), g)),
                   pl.BlockSpec((tb, LANES), lambda g, j: (rev(j), g)),
                   pl.BlockSpec((tb, LANES), lambda g, j: (rev(j), g)),
                   pl.BlockSpec((None, tb, LANES), lambda g, j: (g, rev(j), 0)),
                   pl.BlockSpec((tb, 512), lambda g, j: (rev(j), COL_Z // 512 + g)), acc_heads, acc_heads, acc, acc],
        out_shape=[jax.ShapeDtypeStruct((t, B_INNER), F32), jax.ShapeDtypeStruct((t, 512), F32),
                   jax.ShapeDtypeStruct((t, 512), F32), jax.ShapeDtypeStruct((N_GROUPS_B, t, LANES), F32),
                   jax.ShapeDtypeStruct(d_proj.shape, d_proj.dtype)]
        + [jax.ShapeDtypeStruct((N_GROUPS_B, 1, LANES), F32)] * 2 + [jax.ShapeDtypeStruct((N_GROUPS_B, 1, 512), F32)] * 2,
        input_output_aliases={11: 4},
        scratch_shapes=[pltpu.VMEM((LANES, 512), F32)],
        compiler_params=_params(("arbitrary", "arbitrary")),
    )(xc, xc, xc, proj, proj, states, d_out, dtb, alog, dsk, nw, d_proj)


CONV_HALO = 8


def _shift_down(halo_then_tile, s, tm):
    if s == 0:
        return halo_then_tile[CONV_HALO:CONV_HALO + tm]
    return pltpu.roll(halo_then_tile, s, 0)[CONV_HALO:CONV_HALO + tm]


def _conv_pre(cur, prev, w, b, tm):
    stacked = jnp.concatenate([prev, cur], axis=0)
    taps = [_shift_down(stacked, 3 - j, tm) for j in range(4)]
    pre = b + taps[0] * w[0:1] + taps[1] * w[1:2] + taps[2] * w[2:3] + taps[3] * w[3:4]
    return pre, taps


def _conv_specs(t, tm):
    per = tm // CONV_HALO
    cur = pl.BlockSpec((tm, CONV_DIM), lambda i: (i, COL_XBC // CONV_DIM))
    prev = pl.BlockSpec((CONV_HALO, CONV_DIM), lambda i: (jnp.maximum(i * per - 1, 0), COL_XBC // CONV_DIM))
    return cur, prev


def conv_forward(proj, w, b):
    t = proj.shape[0]
    tm = _pick(t, (256, 128, 64))

    def body(cur_ref, prev_ref, w_ref, b_ref, o_ref):
        prev = jnp.where(pl.program_id(0) == 0, 0.0, prev_ref[...])
        pre, _ = _conv_pre(cur_ref[...], prev, w_ref[...], b_ref[...], tm)
        o_ref[...] = silu(pre)

    cur, prev = _conv_specs(t, tm)
    return pl.pallas_call(
        body, name="conv_forward", grid=(t // tm,),
        in_specs=[cur, prev, pl.BlockSpec((4, CONV_DIM), lambda i: (0, 0)), pl.BlockSpec((1, CONV_DIM), lambda i: (0, 0))],
        out_specs=pl.BlockSpec((tm, CONV_DIM), lambda i: (i, 0)),
        out_shape=jax.ShapeDtypeStruct((t, CONV_DIM), F32),
        compiler_params=_params(("arbitrary",)),
    )(proj, proj, w, b)


def conv_backward_pre(proj, dx, db_, dc_, w, b):
    t = proj.shape[0]
    tm = _pick(t, (256, 128, 64))

    def body(cur_ref, prev_ref, dx_ref, dbm_ref, dcm_ref, w_ref, b_ref, dpre_ref, dw_ref, dbias_ref):
        @pl.when(pl.program_id(0) == 0)
        def _():
            dw_ref[...] = jnp.zeros_like(dw_ref)
            dbias_ref[...] = jnp.zeros_like(dbias_ref)

        prev = jnp.where(pl.program_id(0) == 0, 0.0, prev_ref[...])
        pre, taps = _conv_pre(cur_ref[...], prev, w_ref[...], b_ref[...], tm)
        sg = sigmoid(pre)
        d_out = jnp.concatenate([dx_ref[...], dbm_ref[...], dcm_ref[...]], axis=1)
        dpre = d_out * (sg * (1.0 + pre * (1.0 - sg)))
        dpre_ref[...] = dpre
        dbias_ref[...] += jnp.sum(dpre, axis=0, keepdims=True)
        for j in range(4):
            dw_ref[j:j + 1, :] += jnp.sum(dpre * taps[j], axis=0, keepdims=True)

    cur, prev = _conv_specs(t, tm)
    row = lambda w_: pl.BlockSpec((tm, w_), lambda i: (i, 0))
    return pl.pallas_call(
        body, name="conv_backward_pre", grid=(t // tm,),
        in_specs=[cur, prev, row(B_INNER), row(512), row(512),
                  pl.BlockSpec((4, CONV_DIM), lambda i: (0, 0)), pl.BlockSpec((1, CONV_DIM), lambda i: (0, 0))],
        out_specs=[row(CONV_DIM), pl.BlockSpec((4, CONV_DIM), lambda i: (0, 0)), pl.BlockSpec((1, CONV_DIM), lambda i: (0, 0))],
        out_shape=[jax.ShapeDtypeStruct((t, CONV_DIM), F32), jax.ShapeDtypeStruct((4, CONV_DIM), F32),
                   jax.ShapeDtypeStruct((1, CONV_DIM), F32)],
        compiler_params=_params(("arbitrary",)),
    )(proj, proj, dx, db_, dc_, w, b)


def conv_backward_input(dpre, w, d_proj):
    t = dpre.shape[0]
    tm = _pick(t, (256, 128, 64))
    per = tm // CONV_HALO
    last = t // CONV_HALO - 1
    nt = t // tm

    def body(cur_ref, nxt_ref, w_ref, _, o_ref):
        nxt = jnp.where(pl.program_id(0) == nt - 1, 0.0, nxt_ref[...])
        stacked = jnp.concatenate([cur_ref[...], nxt], axis=0)
        w_ = w_ref[...]
        acc = stacked[0:tm] * w_[3:4]
        for j in range(3):
            s = 3 - j
            acc = acc + pltpu.roll(stacked, tm + CONV_HALO - s, 0)[0:tm] * w_[j:j + 1]
        o_ref[...] = acc.astype(o_ref.dtype)

    return pl.pallas_call(
        body, name="conv_backward_input", grid=(nt,),
        in_specs=[pl.BlockSpec((tm, CONV_DIM), lambda i: (i, 0)),
                  pl.BlockSpec((CONV_HALO, CONV_DIM), lambda i: (jnp.minimum((i + 1) * per, last), 0)),
                  pl.BlockSpec((4, CONV_DIM), lambda i: (0, 0)), pl.BlockSpec(memory_space=pl.ANY)],
        out_specs=pl.BlockSpec((tm, CONV_DIM), lambda i: (i, COL_XBC // CONV_DIM)),
        out_shape=jax.ShapeDtypeStruct(d_proj.shape, d_proj.dtype),
        input_output_aliases={3: 0},
        compiler_params=_params(("arbitrary",)),
    )(dpre, dpre, w, d_proj)


def stage_modulate(x, sc, sh):
    return _ln(x) * (1.0 + sc) + sh


def stage_merge(ga, gb, ya, yb):
    return sigmoid(ga) * ya + sigmoid(gb) * yb


def stage_post_mixer(x, h, g1, ln_g, ln_b, sc2, sh2):
    x1 = _ln(ALPHA * x + g1 * h) * ln_g + ln_b
    return x1, _ln(x1) * (1.0 + sc2) + sh2


def stage_swiglu(a, b):
    return silu(a) * b


def stage_loss(x1, hf, tgt, g2, ln_g, ln_b):
    x2 = _ln(ALPHA * x1 + g2 * hf) * ln_g + ln_b
    return 0.5 * jnp.sum(jnp.mean(jnp.square(x2 - tgt), axis=-1, keepdims=True), axis=0, keepdims=True)


def local_step(x, tgt, mod, wts, small, early=None, mid=None, late=None, last=None):
    sh1, sc1, g1, sh2, sc2, g2 = mod
    lb, gn, conv_w, conv_b, dtb, alog, dsk, nw, ln1_g, ln1_b, ln2_g, ln2_b = small
    vec = (1, D)

    (u1,) = rowwise("modulate1", lambda r, c: ((stage_modulate(r[0], *c),), ()), [_full(x)], [sc1, sh1], [(D, BF16)])
    w_in = wts.input_projection(u1)
    proj = matmul(u1, w_in, "nn", F32, "in_proj")
    ya_in, st_a = hgrn_forward(proj, lb, gn + wts.start_rest(proj)[0:1])
    xc = conv_forward(proj, conv_w, conv_b)
    w_a, w_b, w_o, w_gu, w_d = wts.rest(xc)
    yb_in, st_b = ssd_forward(xc, proj, dtb, alog, dsk, nw)
    ya = matmul(ya_in, w_a, "nn", F32, "branch_a")
    yb = matmul(yb_in, w_b, "nn", F32, "branch_b")
    gate_rows = [(proj, D, COL_GA // D), (proj, D, COL_GB // D), _full(ya), _full(yb)]
    (merged,) = rowwise("merge", lambda r, c: ((stage_merge(*r),), ()), gate_rows, [], [(D, BF16)])
    h = matmul(merged, w_o, "nn", F32, "out_proj")
    post_consts = [g1, ln1_g, ln1_b, sc2, sh2]
    x1, u2 = rowwise("post_mixer", lambda r, c: (stage_post_mixer(*r, *c), ()), [_full(x), _full(h)], post_consts,
                     [(D, F32), (D, BF16)])
    ab = matmul(u2, w_gu, "nt", F32, "ffn_in")
    (p,) = rowwise("swiglu", lambda r, c: ((stage_swiglu(*r),), ()), [(ab, D_FF, 0), (ab, D_FF, 1)], [], [(D_FF, BF16)])
    hf = matmul(p, w_d, "nn", F32, "ffn_out")

    def loss_bwd(r, c):
        loss, vjp = jax.vjp(stage_loss, *r, *c)
        dx1, dhf, _, dg2, dlg, dlb_ = vjp(jnp.ones((1, 1), F32))
        return (dx1, dhf), (loss, dg2, dlg, dlb_)

    dx1, dhf, loss, dg2, dln2_g, dln2_b = rowwise(
        "loss_backward", loss_bwd, [_full(x1), _full(hf), _full(tgt)], [g2, ln2_g, ln2_b],
        [(D, F32), (D, BF16)], [(1, 1), vec, vec, vec])
    dp = matmul(dhf, w_d, "nt", F32, "ffn_out_dx")
    dw_d = matmul(p, dhf, "tn", F32, "ffn_out_dw")

    def swiglu_bwd(r, c):
        _, vjp = jax.vjp(stage_swiglu, r[0], r[1])
        da, db_ = vjp(r[2])
        return (jnp.concatenate([da, db_], axis=1),), ()

    (dab,) = rowwise("swiglu_backward", swiglu_bwd, [(ab, D_FF, 0), (ab, D_FF, 1), _full(dp)], [], [(2 * D_FF, BF16)])
    du2 = matmul(dab, w_gu, "nn", F32, "ffn_in_dx")
    dw_gu = matmul(dab, u2, "tn", F32, "ffn_in_dw")

    def post_bwd(r, c):
        _, vjp = jax.vjp(stage_post_mixer, r[0], r[1], *c)
        dx, dh, *dc = vjp((r[2], r[3]))
        return (dx, dh), tuple(dc)

    dx_a, dh, dg1, dln1_g, dln1_b, dsc2, dsh2 = rowwise(
        "post_mixer_backward", post_bwd, [_full(x), _full(h), _full(dx1), _full(du2)], post_consts,
        [(D, F32), (D, BF16)], [vec] * 5)
    dmerged = matmul(dh, w_o, "nt", F32, "out_proj_dx")
    dw_o = matmul(merged, dh, "tn", F32, "out_proj_dw")

    def merge_bwd(r, c):
        _, vjp = jax.vjp(stage_merge, *r[:4])
        dga, dgb, dya, dyb = vjp(r[4])
        return (jnp.concatenate([dga, dgb], axis=1), dya, dyb), ()

    dproj, dya, dyb = rowwise("merge_backward", merge_bwd, gate_rows + [_full(dmerged)], [],
                              [(2 * D, BF16), (D, BF16), (D, BF16)], new_wide=(IN_PAD, COL_GA // (2 * D)))
    dya_in = matmul(dya, w_a, "nt", F32, "branch_a_dx")
    dw_a = matmul(ya_in, dya, "tn", F32, "branch_a_dw")
    dyb_in = matmul(dyb, w_b, "nt", F32, "branch_b_dx")
    dw_b = matmul(yb_in, dyb, "tn", F32, "branch_b_dw")
    gn_after = gn if early is None else gn + early((dw_a, dw_b, dw_o, dw_gu, dw_d))[0:1]
    dproj, dlb, dgn = hgrn_backward(proj, st_a, dya_in, lb, gn_after, dproj)
    dtb_after = dtb if mid is None else dtb + mid(dlb)[0:1, 0:1]
    dxs, dbm, dcm, ddt, dproj, ddtb, dalog, ddsk, dnw = ssd_backward(xc, proj, st_b, dyb_in, dtb_after, alog, dsk, nw, dproj)
    dpre, dconv_w, dconv_b = conv_backward_pre(proj, dxs, dbm, dcm, conv_w, conv_b)
    if late is not None:
        late(dconv_b)
    dproj = conv_backward_input(dpre, conv_w, dproj)
    t = x.shape[0]
    tail = jnp.concatenate([jnp.sum(ddt, axis=0).astype(BF16), jnp.zeros((t, IN_PAD - COL_DT - LANES), BF16)], axis=1)
    dproj = lax.dynamic_update_slice(dproj, tail, (0, COL_DT))
    dw_in = matmul(u1, dproj, "tn", F32, "in_proj_dw")
    du1 = matmul(dproj, w_in, "nt", F32, "in_proj_dx", after=None if last is None else last(dw_in))

    def mod_bwd(r, c):
        _, vjp = jax.vjp(stage_modulate, r[0], *c)
        dx, dsc, dsh = vjp(r[1])
        return (dx + r[2],), (dsc, dsh)

    grad_x, dsc1, dsh1 = rowwise("modulate1_backward", mod_bwd, [_full(x), _full(du1), _full(dx_a)], [sc1, sh1],
                                 [(D, F32)], [vec, vec])
    d_mod = (dsh1, dsc1, dg1, dsh2, dsc2, dg2)
    d_wts = (dw_in, dw_a, dw_b, dw_o, dw_gu, dw_d)
    d_small = (dlb, dgn, dconv_w, dconv_b, jnp.sum(ddtb, axis=0),
               jnp.sum(dalog, axis=0), ddsk.reshape(1, B_INNER), dnw.reshape(1, B_INNER),
               dln1_g, dln1_b, dln2_g, dln2_b)
    return loss, grad_x, d_mod, d_wts, d_small


HBM = pl.BlockSpec(memory_space=pltpu.HBM)
SEM = pl.BlockSpec(memory_space=pltpu.SEMAPHORE)
DATAFLOW = pltpu.SideEffectType.DATAFLOW_SIDE_EFFECTING


def _place():
    return lax.axis_index("x"), lax.axis_index("y"), lax.axis_index("c")


def _other_chips(x, y):
    return [(1 - x, y), (x, 1 - y), (1 - x, 1 - y)]


def _remote(src, dst, send_sem, recv_sem, device):
    return pltpu.make_async_remote_copy(src_ref=src, dst_ref=dst, send_sem=send_sem, recv_sem=recv_sem,
                                        device_id=device, device_id_type=MESH)


def gather_rows(v, name):
    n = v.shape[1]

    def body(v_ref, out_ref, send_sems, recv_sems, local_sem):
        x, y, c = _place()
        mine = pltpu.make_async_copy(v_ref, out_ref.at[4 * x + 2 * y + c], local_sem)
        mine.start()
        sends, recvs = [], []
        for m in range(1, 8):
            px = 1 - x if m & 4 else x
            py = 1 - y if m & 2 else y
            pc = 1 - c if m & 1 else c
            sends.append(_remote(v_ref, out_ref.at[4 * x + 2 * y + c], send_sems.at[m - 1], recv_sems.at[m - 1], (px, py, pc)))
            recvs.append(_remote(v_ref, out_ref.at[4 * px + 2 * py + pc], send_sems.at[m - 1], recv_sems.at[m - 1], (px, py, pc)))
        for cp in sends:
            cp.start()
        for cp in recvs:
            cp.wait_recv()
        for cp in sends:
            cp.wait_send()
        mine.wait()

    return pl.pallas_call(
        body, name=name, in_specs=[HBM], out_specs=HBM,
        out_shape=jax.ShapeDtypeStruct((8, 1, n), v.dtype),
        scratch_shapes=[pltpu.SemaphoreType.DMA((7,)), pltpu.SemaphoreType.DMA((7,)), pltpu.SemaphoreType.DMA],
    )(v)


def exchange_rows(part, name):
    w = part.shape[2]

    def body(p_ref, out_ref, send_sems, recv_sems, local_sem):
        x, y, c = _place()
        k = 2 * x + y
        mine = pltpu.make_async_copy(p_ref.at[4 * x + 2 * y + c], out_ref.at[k], local_sem)
        mine.start()
        sends, recvs = [], []
        for j, (px, py) in enumerate(_other_chips(x, y)):
            sends.append(_remote(p_ref.at[4 * px + 2 * py + c], out_ref.at[k], send_sems.at[j], recv_sems.at[j], (px, py, c)))
            recvs.append(_remote(p_ref.at[4 * px + 2 * py + c], out_ref.at[2 * px + py], send_sems.at[j], recv_sems.at[j], (px, py, c)))
        for cp in sends:
            cp.start()
        for cp in recvs:
            cp.wait_recv()
        for cp in sends:
            cp.wait_send()
        mine.wait()

    return pl.pallas_call(
        body, name=name, in_specs=[HBM], out_specs=HBM,
        out_shape=jax.ShapeDtypeStruct((4, 1, w), part.dtype),
        scratch_shapes=[pltpu.SemaphoreType.DMA((3,)), pltpu.SemaphoreType.DMA((3,)), pltpu.SemaphoreType.DMA],
    )(part)


def _half_of_slot(ref, rows, px, py, pc):
    return ref.at[2 * px + py, pl.ds(pc * (rows // 2), rows // 2), :]


def gather_start(shards, after):
    n = len(shards)

    def body(*refs):
        w_refs, land_refs = refs[:n], refs[n:2 * n]
        send_a, recv_a, send_b, recv_b = refs[2 * n + 1:2 * n + 5]
        token = refs[-1]
        x, y, c = _place()
        for i in range(n):
            rows = shards[i].shape[0]
            for j, (px, py) in enumerate(_other_chips(x, y)):
                sems = (send_a.at[j], recv_a.at[j]) if i == 0 else (send_b.at[j * (n - 1) + i - 1], recv_b.at[j * (n - 1) + i - 1])
                _remote(w_refs[i].at[pl.ds(c * (rows // 2), rows // 2), :], _half_of_slot(land_refs[i], rows, x, y, c),
                        *sems, (px, py, c)).start()
        token[...] = jnp.zeros_like(token)

    hbm = lambda a: pltpu.with_memory_space_constraint(a, pltpu.HBM)
    lands = [lax.empty((4,) + s.shape, s.dtype) for s in shards]
    dma = pltpu.SemaphoreType.DMA
    return pl.pallas_call(
        body, name="gather_start",
        out_shape=(dma((3,)), dma((3,)), dma((3 * (n - 1),)), dma((3 * (n - 1),)),
                   *[pltpu.HBM(a.shape, a.dtype) for a in list(shards) + lands], jax.ShapeDtypeStruct((8, LANES), F32)),
        in_specs=[HBM] * (2 * n) + [pl.BlockSpec(memory_space=pl.ANY)],
        out_specs=(SEM, SEM, SEM, SEM, *[HBM] * (2 * n), pl.BlockSpec(memory_space=pltpu.VMEM)),
        input_output_aliases={i: 4 + i for i in range(2 * n)},
        compiler_params=pltpu.CompilerParams(has_side_effects=DATAFLOW),
    )(*[hbm(a) for a in list(shards) + lands], after)


def gather_wait(send_sems, recv_sems, shards, lands, after, tag):
    n = len(shards)

    def body(*refs):
        w_refs, land_refs = refs[:n], refs[n:2 * n]
        send_ref, recv_ref = refs[2 * n], refs[2 * n + 1]
        x, y, c = _place()
        for i in range(n):
            rows = shards[i].shape[0]
            for j, (px, py) in enumerate(_other_chips(x, y)):
                cp = _remote(w_refs[i].at[pl.ds(c * (rows // 2), rows // 2), :], _half_of_slot(land_refs[i], rows, px, py, c),
                             send_ref.at[j * n + i], recv_ref.at[j * n + i], (px, py, c))
                cp.wait_send()
                cp.wait_recv()

    out = pl.pallas_call(
        body, name="gather_wait_" + tag,
        out_shape=tuple(pltpu.HBM(a.shape, a.dtype) for a in list(shards) + list(lands)),
        in_specs=[HBM] * (2 * n) + [SEM, SEM, pl.BlockSpec(memory_space=pl.ANY)], out_specs=tuple([HBM] * (2 * n)),
        input_output_aliases={i: i for i in range(2 * n)},
        compiler_params=pltpu.CompilerParams(has_side_effects=DATAFLOW),
    )(*shards, *lands, send_sems, recv_sems, after)
    return list(out[n:])


def forward_start(lands, tag):
    n = len(lands)

    def body(*refs):
        land_refs = refs[:n]
        send_sems, recv_sems = refs[n], refs[n + 1]
        token = refs[-1]
        x, y, c = _place()
        for i in range(n):
            rows = lands[i].shape[1]
            for j, (px, py) in enumerate(_other_chips(x, y)):
                mine = _half_of_slot(land_refs[i], rows, px, py, c)
                _remote(mine, mine, send_sems.at[j * n + i], recv_sems.at[j * n + i], (x, y, 1 - c)).start()
        token[...] = jnp.zeros_like(token)

    dma = pltpu.SemaphoreType.DMA
    return pl.pallas_call(
        body, name="forward_start_" + tag,
        out_shape=(dma((3 * n,)), dma((3 * n,)), *[pltpu.HBM(a.shape, a.dtype) for a in lands],
                   jax.ShapeDtypeStruct((8, LANES), F32)),
        in_specs=[HBM] * n, out_specs=(SEM, SEM, *[HBM] * n, pl.BlockSpec(memory_space=pltpu.VMEM)),
        input_output_aliases={i: 2 + i for i in range(n)},
        compiler_params=pltpu.CompilerParams(has_side_effects=DATAFLOW),
    )(*lands)


def forward_wait(started, after, tag):
    send_sems, recv_sems, *rest = started
    lands = rest[:-1]
    n = len(lands)

    def body(*refs):
        land_refs = refs[:n]
        send_ref, recv_ref = refs[n], refs[n + 1]
        x, y, c = _place()
        for i in range(n):
            rows = lands[i].shape[1]
            for j, (px, py) in enumerate(_other_chips(x, y)):
                cp = _remote(_half_of_slot(land_refs[i], rows, px, py, c), _half_of_slot(land_refs[i], rows, px, py, 1 - c),
                             send_ref.at[j * n + i], recv_ref.at[j * n + i], (x, y, 1 - c))
                cp.wait_send()
                cp.wait_recv()

    out = pl.pallas_call(
        body, name="forward_wait_" + tag,
        out_shape=tuple(pltpu.HBM(a.shape, a.dtype) for a in lands),
        in_specs=[HBM] * n + [SEM, SEM, pl.BlockSpec(memory_space=pl.ANY)], out_specs=tuple([HBM] * n),
        input_output_aliases={i: i for i in range(n)},
        compiler_params=pltpu.CompilerParams(has_side_effects=DATAFLOW),
    )(*lands, send_sems, recv_sems, after)
    return list(out)


def pair_exchange(slabs, name):
    n = len(slabs)

    def body(*refs):
        g_refs, out_refs = refs[:n], refs[n:2 * n]
        send_sems, recv_sems = refs[2 * n:]
        x, y, c = _place()
        copies = []
        for i in range(n):
            hr = slabs[i].shape[1] // 2
            cp = _remote(g_refs[i].at[:, pl.ds((1 - c) * hr, hr), :], out_refs[i], send_sems.at[i], recv_sems.at[i], (x, y, 1 - c))
            cp.start()
            copies.append(cp)
        for cp in copies:
            cp.wait()

    return pl.pallas_call(
        body, name=name, in_specs=[HBM] * n, out_specs=[HBM] * n,
        out_shape=[jax.ShapeDtypeStruct((4, s.shape[1] // 2, s.shape[2]), s.dtype) for s in slabs],
        scratch_shapes=[pltpu.SemaphoreType.DMA((n,)), pltpu.SemaphoreType.DMA((n,))],
    )(*slabs)


def pair_start(slabs, tag):
    n = len(slabs)

    def body(*refs):
        g_refs, land_refs = refs[:n], refs[n:2 * n]
        send_sems, recv_sems = refs[2 * n], refs[2 * n + 1]
        token = refs[-1]
        x, y, c = _place()
        for i in range(n):
            hr = slabs[i].shape[1] // 2
            _remote(g_refs[i].at[:, pl.ds((1 - c) * hr, hr), :], land_refs[i], send_sems.at[i], recv_sems.at[i],
                    (x, y, 1 - c)).start()
        token[...] = jnp.zeros_like(token)

    hbm = lambda a: pltpu.with_memory_space_constraint(a, pltpu.HBM)
    lands = [lax.empty((4, s.shape[1] // 2, s.shape[2]), s.dtype) for s in slabs]
    dma = pltpu.SemaphoreType.DMA
    return pl.pallas_call(
        body, name="pair_start_" + tag,
        out_shape=(dma((n,)), dma((n,)), *[pltpu.HBM(a.shape, a.dtype) for a in list(slabs) + lands],
                   jax.ShapeDtypeStruct((8, LANES), F32)),
        in_specs=[HBM] * (2 * n), out_specs=(SEM, SEM, *[HBM] * (2 * n), pl.BlockSpec(memory_space=pltpu.VMEM)),
        input_output_aliases={i: 2 + i for i in range(2 * n)},
        compiler_params=pltpu.CompilerParams(has_side_effects=DATAFLOW),
    )(*[hbm(a) for a in list(slabs) + lands])


def pair_wait(started, after, tag):
    send_sems, recv_sems, *rest = started
    n = (len(rest) - 1) // 2
    slabs, lands = rest[:n], rest[n:2 * n]

    def body(*refs):
        g_refs, land_refs = refs[:n], refs[n:2 * n]
        send_ref, recv_ref = refs[2 * n], refs[2 * n + 1]
        x, y, c = _place()
        for i in range(n):
            hr = slabs[i].shape[1] // 2
            cp = _remote(g_refs[i].at[:, pl.ds((1 - c) * hr, hr), :], land_refs[i], send_ref.at[i], recv_ref.at[i], (x, y, 1 - c))
            cp.wait_send()
            cp.wait_recv()

    out = pl.pallas_call(
        body, name="pair_wait_" + tag,
        out_shape=tuple(pltpu.HBM(a.shape, a.dtype) for a in list(slabs) + list(lands)),
        in_specs=[HBM] * (2 * n) + [SEM, SEM, pl.BlockSpec(memory_space=pl.ANY)], out_specs=tuple([HBM] * (2 * n)),
        input_output_aliases={i: i for i in range(2 * n)},
        compiler_params=pltpu.CompilerParams(has_side_effects=DATAFLOW),
    )(*slabs, *lands, send_sems, recv_sems, after)
    return list(out[:n]), list(out[n:])


def _tile2(rows, cols):
    fits = lambda r, c: r * c * 4 <= BLOCK_BYTES
    if fits(rows, cols):
        return rows, cols
    for r in (1024, 512, 256, 128, 64):
        if rows % r == 0 and fits(r, cols):
            return r, cols
    return rows, next(cols // k for k in (2, 3, 4, 6, 8, 12, 16) if cols % (k * LANES) == 0 and fits(rows, cols // k))


def pair_add(g, p, c, name):
    _, hr, cols = p.shape
    tm, tc = _tile2(hr, cols)
    per = hr // tm

    def body(c_ref, g_ref, p_ref, o_ref):
        o_ref[...] = (g_ref[...] + p_ref[...]).astype(o_ref.dtype)

    return pl.pallas_call(
        body, name=name,
        grid_spec=pltpu.PrefetchScalarGridSpec(
            num_scalar_prefetch=1, grid=(4, per, cols // tc),
            in_specs=[pl.BlockSpec((None, tm, tc), lambda k, i, j, c_ref: (k, c_ref[0] * per + i, j)),
                      pl.BlockSpec((None, tm, tc), lambda k, i, j, c_ref: (k, i, j))],
            out_specs=pl.BlockSpec((None, tm, tc), lambda k, i, j, c_ref: (k, i, j))),
        out_shape=jax.ShapeDtypeStruct((4, hr, cols), BF16),
        compiler_params=_params(("arbitrary", "arbitrary", "arbitrary")),
    )(c.reshape(1).astype(jnp.int32), g, p)


def scatter_start(sums, tag):
    n = len(sums)

    def body(*refs):
        s_refs, land_refs = refs[:n], refs[n:2 * n]
        send_sems, recv_sems = refs[2 * n], refs[2 * n + 1]
        token = refs[-1]
        x, y, c = _place()
        k = 2 * x + y
        for i in range(n):
            for j, (px, py) in enumerate(_other_chips(x, y)):
                _remote(s_refs[i].at[2 * px + py], land_refs[i].at[k], send_sems.at[j * n + i], recv_sems.at[j * n + i],
                        (px, py, c)).start()
        token[...] = jnp.zeros_like(token)

    hbm = lambda a: pltpu.with_memory_space_constraint(a, pltpu.HBM)
    return pl.pallas_call(
        body, name="scatter_start_" + tag,
        out_shape=(pltpu.SemaphoreType.DMA((3 * n,)), pltpu.SemaphoreType.DMA((3 * n,)),
                   *[pltpu.HBM(s.shape, s.dtype) for s in sums], *[pltpu.HBM(s.shape, s.dtype) for s in sums],
                   jax.ShapeDtypeStruct((8, LANES), F32)),
        in_specs=[HBM] * (2 * n), out_specs=(SEM, SEM, *[HBM] * (2 * n), pl.BlockSpec(memory_space=pltpu.VMEM)),
        input_output_aliases={i: 2 + i for i in range(2 * n)},
        compiler_params=pltpu.CompilerParams(has_side_effects=DATAFLOW),
    )(*[hbm(s) for s in sums], *[hbm(lax.empty(s.shape, s.dtype)) for s in sums])


def scatter_wait(started, after, tag):
    send_sems, recv_sems, *rest = started
    n = (len(rest) - 1) // 2
    sums, lands = rest[:n], rest[n:2 * n]

    def body(*refs):
        s_refs, land_refs = refs[:n], refs[n:2 * n]
        send_ref, recv_ref = refs[2 * n], refs[2 * n + 1]
        x, y, c = _place()
        for i in range(n):
            for j, (px, py) in enumerate(_other_chips(x, y)):
                cp = _remote(s_refs[i].at[2 * px + py], land_refs[i].at[2 * px + py], send_ref.at[j * n + i],
                             recv_ref.at[j * n + i], (px, py, c))
                cp.wait_send()
                cp.wait_recv()

    out = pl.pallas_call(
        body, name="scatter_wait_" + tag,
        out_shape=tuple(pltpu.HBM(s.shape, s.dtype) for s in sums + lands),
        in_specs=[HBM] * (2 * n) + [SEM, SEM, pl.BlockSpec(memory_space=pl.ANY)], out_specs=tuple([HBM] * (2 * n)),
        input_output_aliases={i: i for i in range(2 * n)},
        compiler_params=pltpu.CompilerParams(has_side_effects=DATAFLOW),
    )(*sums, *lands, send_sems, recv_sems, after)
    return list(out[n:])


def sum_chips(landed, own, chip, core, name):
    _, hr, cols = landed.shape
    tm, tc = _tile2(hr, cols)
    per = hr // tm

    def body(idx_ref, l0, l1, l2, l3, own_ref, o_ref):
        mine = own_ref[...].astype(F32)
        v = [jnp.where(idx_ref[0] == k, mine, ref[...].astype(F32)) for k, ref in enumerate((l0, l1, l2, l3))]
        o_ref[...] = ((v[0] + v[1]) + v[2]) + v[3]

    slot = lambda k: pl.BlockSpec((None, tm, tc),
                                  lambda i, j, idx: (jnp.where(idx[0] == k, (k + 1) & 3, k), i, j))
    return pl.pallas_call(
        body, name=name,
        grid_spec=pltpu.PrefetchScalarGridSpec(
            num_scalar_prefetch=1, grid=(per, cols // tc),
            in_specs=[slot(0), slot(1), slot(2), slot(3),
                      pl.BlockSpec((None, tm, tc), lambda i, j, idx: (idx[0], i, j))],
            out_specs=pl.BlockSpec((tm, tc), lambda i, j, idx: (idx[1] * per + i, j))),
        out_shape=jax.ShapeDtypeStruct((2 * hr, cols), F32),
        compiler_params=_params(("arbitrary", "arbitrary")),
    )(jnp.stack([chip, core]).astype(jnp.int32), landed, landed, landed, landed, own)


def exchange_halves(bufs):
    n = len(bufs)

    def body(*refs):
        out_refs = refs[n:2 * n]
        send_sems, recv_sems = refs[2 * n:]
        x, y, c = _place()
        sends, recvs = [], []
        for i in range(n):
            hr = bufs[i].shape[0] // 2
            own = out_refs[i].at[pl.ds(c * hr, hr), :]
            other = out_refs[i].at[pl.ds((1 - c) * hr, hr), :]
            sends.append(_remote(own, own, send_sems.at[i], recv_sems.at[i], (x, y, 1 - c)))
            recvs.append(_remote(other, other, send_sems.at[i], recv_sems.at[i], (x, y, 1 - c)))
        for cp in sends:
            cp.start()
        for cp in recvs:
            cp.wait_recv()
        for cp in sends:
            cp.wait_send()

    return pl.pallas_call(
        body, name="exchange_halves", in_specs=[HBM] * n, out_specs=[HBM] * n,
        out_shape=[jax.ShapeDtypeStruct(b.shape, b.dtype) for b in bufs],
        input_output_aliases={i: i for i in range(n)},
        scratch_shapes=[pltpu.SemaphoreType.DMA((n,)), pltpu.SemaphoreType.DMA((n,))],
    )(*bufs)


def _relayout(name, arrays, in_blocks, out_blocks, out_shapes, fn):
    rows = 128
    spec = lambda blk: pl.BlockSpec(blk, (lambda i: (0, i, 0)) if len(blk) == 3 else (lambda i: (i, 0)))

    def body(*refs):
        n_in = len(arrays)
        outs = fn(*[r[...] for r in refs[:n_in]])
        for ref, val in zip(refs[n_in:], outs, strict=True):
            if isinstance(val, list):
                for k, piece in enumerate(val):
                    ref[k] = piece
            else:
                ref[...] = val

    return pl.pallas_call(
        body, name=name, grid=(D // rows,),
        in_specs=[spec(b) for b in in_blocks], out_specs=[spec(b) for b in out_blocks], out_shape=out_shapes,
        compiler_params=_params(("arbitrary",)),
    )(*arrays)


def assemble_in_proj(g):
    def fn(v):
        w = jnp.concatenate([v[k] for k in range(4)], axis=1)
        return (jnp.concatenate([w[:, :ORIG_Z], w[:, ORIG_GA:], w[:, ORIG_XBC:ORIG_DT], w[:, ORIG_Z:ORIG_XBC],
                                 w[:, ORIG_DT:ORIG_GA], jnp.zeros((w.shape[0], IN_PAD - IN_ORIG), w.dtype)], axis=1),)

    cols = g.shape[2]
    return _relayout("assemble_in_proj", [g], [(4, 128, cols)], [(128, IN_PAD)],
                     [jax.ShapeDtypeStruct((D, IN_PAD), g.dtype)], fn)[0]


def split_in_proj(dw):
    cols = IN_ORIG // 4

    def fn(d):
        w = jnp.concatenate([d[:, :COL_GA], d[:, COL_Z:COL_DT], d[:, COL_XBC:COL_Z], d[:, COL_DT:COL_DT + 32],
                             d[:, COL_GA:COL_XBC]], axis=1)
        return ([w[:, k * cols:(k + 1) * cols] for k in range(4)],)

    return _relayout("split_in_proj", [dw], [(128, IN_PAD)], [(4, 128, cols)],
                     [jax.ShapeDtypeStruct((4, D, cols), dw.dtype)], fn)[0]


def ada_prepare(c_all, w_ada, hgrn_lb):
    def body(c_ref, w_ref, lb_ref, mod_ref, row_ref):
        mod_ref[...] = hdot(silu(c_ref[...]), w_ref[...])
        row_ref[...] = sigmoid(lb_ref[0:1, :] - lb_ref[1:2, :])

    return pl.pallas_call(
        body, name="ada_prepare",
        out_shape=[jax.ShapeDtypeStruct((8, w_ada.shape[1]), F32), jax.ShapeDtypeStruct((1, D), F32)],
        compiler_params=pltpu.CompilerParams(vmem_limit_bytes=VMEM_LIMIT),
    )(c_all, w_ada, hgrn_lb)


SMALL_SEGS = (("mod", 6 * D), ("lb", D), ("gnorm", LANES), ("conv_w", 4 * CONV_DIM), ("conv_b", CONV_DIM),
              ("dt_bias", LANES), ("a_log", LANES), ("d", B_INNER), ("ssm_norm", B_INNER),
              ("ln1_g", D), ("ln1_b", D), ("ln2_g", D), ("ln2_b", D), ("loss", LANES))
SMALL_PARAMS = ("b_ada", "hgrn_lb", "hgrn_gnorm", "ssm_conv_b", "ssm_dt_bias", "ssm_a_log", "ssm_d", "ssm_norm",
                "ln1_g", "ln1_b", "ln2_g", "ln2_b")


def finalize_small(g_all, c_all, dmod_cols, params, m, v):
    n_p = len(SMALL_PARAMS)
    offs, o = {}, 0
    for nm, width in SMALL_SEGS:
        offs[nm] = (o, width)
        o += width

    def body(*refs):
        g_ref, c_ref, dm_ref = refs[:3]
        p_refs = refs[3:3 + n_p]
        m_refs = refs[3 + n_p:3 + 2 * n_p]
        v_refs = refs[3 + 2 * n_p:3 + 3 * n_p]
        outs = refs[3 + 3 * n_p:]
        gwa_ref, gcw_ref, loss_ref = outs[:3]
        res = outs[3:]
        total = jnp.sum(g_ref[...], axis=0, keepdims=True)
        seg = lambda nm: total[:, offs[nm][0]:offs[nm][0] + offs[nm][1]]
        loss_ref[...] = seg("loss")
        gwa_ref[...] = hdot(silu(c_ref[...]), dm_ref[...], "tn")
        cw = seg("conv_w")
        for j in range(4):
            gcw_ref[j:j + 1, :] = cw[:, j * CONV_DIM:(j + 1) * CONV_DIM]
        hc = lax.broadcasted_iota(jnp.int32, (B_INNER, LANES), 0)
        hj = lax.broadcasted_iota(jnp.int32, (B_INNER, LANES), 1)
        per_head = ((hc >> 6) == hj).astype(F32)
        heads = lambda nm: hdot(jnp.broadcast_to(seg(nm), (8, B_INNER)), per_head)[0:1, 0:32]
        lbp = sigmoid(p_refs[1][0:1, :] - p_refs[1][1:2, :])
        g_row = seg("lb") * lbp * (1.0 - lbp)
        grads = {"b_ada": seg("mod"), "hgrn_gnorm": seg("gnorm"), "ssm_conv_b": seg("conv_b"),
                 "ssm_dt_bias": seg("dt_bias")[:, 0:32], "ssm_a_log": seg("a_log")[:, 0:32], "ssm_d": heads("d"),
                 "ssm_norm": seg("ssm_norm"), "ln1_g": seg("ln1_g"), "ln1_b": seg("ln1_b"),
                 "ln2_g": seg("ln2_g"), "ln2_b": seg("ln2_b")}
        for i, nm in enumerate(SMALL_PARAMS):
            g_out, d_out, m_out, v_out = res[4 * i:4 * i + 4]
            if nm == "hgrn_lb":
                for row, gv in ((0, g_row), (1, -g_row)):
                    sl = slice(row, row + 1)
                    dl, mn, vn = adamw(p_refs[i][sl, :], gv, m_refs[i][sl, :], v_refs[i][sl, :])
                    g_out[sl, :], d_out[sl, :], m_out[sl, :], v_out[sl, :] = gv, dl, mn, vn
            else:
                gv = grads[nm]
                dl, mn, vn = adamw(p_refs[i][...], gv, m_refs[i][...], v_refs[i][...])
                g_out[...], d_out[...], m_out[...], v_out[...] = gv, dl, mn, vn

    out_shape = [jax.ShapeDtypeStruct((D, dmod_cols.shape[1]), F32), jax.ShapeDtypeStruct((4, CONV_DIM), F32),
                 jax.ShapeDtypeStruct((1, LANES), F32)]
    for p in params:
        out_shape += [jax.ShapeDtypeStruct(p.shape, F32)] * 4
    return pl.pallas_call(
        body, name="finalize_small", out_shape=out_shape,
        compiler_params=pltpu.CompilerParams(vmem_limit_bytes=VMEM_LIMIT),
    )(g_all, c_all, dmod_cols, *params, *m, *v)


def adam_update(w, g, m, v, name):
    rows, cols = w.shape
    tm, tc = _tile2(rows, cols)

    def body(w_ref, g_ref, m_ref, v_ref, d_ref, mo_ref, vo_ref):
        d_ref[...], mo_ref[...], vo_ref[...] = adamw(w_ref[...], g_ref[...], m_ref[...], v_ref[...])

    spec = pl.BlockSpec((tm, tc), lambda i, j: (i, j))
    return pl.pallas_call(
        body, name=name, grid=(rows // tm, cols // tc), in_specs=[spec] * 4, out_specs=[spec] * 3,
        out_shape=[jax.ShapeDtypeStruct((rows, cols), F32)] * 3,
        compiler_params=_params(("arbitrary", "arbitrary")),
    )(w, g, m, v)


def kernel(x, c, w_ada, b_ada, w_in, hgrn_lb, hgrn_gnorm, ssm_conv_w, ssm_conv_b, ssm_dt_bias, ssm_a_log, ssm_d, ssm_norm, w_branch_a, w_branch_b, w_o, ln1_g, ln1_b, w_ffn_gate, w_ffn_up, w_ffn_down, ln2_g, ln2_b, loss_target, m_w_ada, m_b_ada, m_w_in, m_hgrn_lb, m_hgrn_gnorm, m_ssm_conv_w, m_ssm_conv_b, m_ssm_dt_bias, m_ssm_a_log, m_ssm_d, m_ssm_norm, m_w_branch_a, m_w_branch_b, m_w_o, m_ln1_g, m_ln1_b, m_w_ffn_gate, m_w_ffn_up, m_w_ffn_down, m_ln2_g, m_ln2_b, v_w_ada, v_b_ada, v_w_in, v_hgrn_lb, v_hgrn_gnorm, v_ssm_conv_w, v_ssm_conv_b, v_ssm_dt_bias, v_ssm_a_log, v_ssm_d, v_ssm_norm, v_w_branch_a, v_w_branch_b, v_w_o, v_ln1_g, v_ln1_b, v_w_ffn_gate, v_w_ffn_up, v_w_ffn_down, v_ln2_g, v_ln2_b):
    given = dict(locals())
    chip = 2 * lax.axis_index("x") + lax.axis_index("y")
    core = lax.axis_index("c")
    t = x.shape[1]

    first = gather_rows(jnp.concatenate([c, ssm_conv_w.reshape(1, CONV_DIM)], axis=1), "gather_cond").reshape(8, D + CONV_DIM)
    c_all = first[:, :D]
    conv_w = first[0::2, D:].reshape(4, 4, CONV_DIM // 4).transpose(1, 0, 2).reshape(4, CONV_DIM)
    mod_part, lb_row = ada_prepare(c_all, w_ada[0], hgrn_lb)
    mod_cols = w_ada.shape[2]
    mod_row = exchange_rows(mod_part.reshape(8, 1, mod_cols), "exchange_mod").reshape(1, 6 * D) + b_ada
    mod = tuple(mod_row[:, i * D:(i + 1) * D] for i in range(6))

    local = {nm: given[nm][0].T if nm in TRANSPOSED else given[nm][0] for nm in SHARDED}
    shards = [local[nm].astype(BF16) for nm in SHARDED]
    n_w = len(SHARDED)
    send_in, recv_in, send_rest, recv_rest, *flying = gather_start(shards, mod_row)
    sent, lands = flying[:n_w], flying[n_w:2 * n_w]
    with_own = lambda land, shard: lax.dynamic_update_slice(land, shard[None], (chip, 0, 0))

    class Weights:
        def input_projection(self, after):
            land = gather_wait(send_in, recv_in, sent[:1], lands[:1], after, "in")
            (land,) = forward_wait(forward_start(land, "in"), after, "in")
            return assemble_in_proj(with_own(land, shards[0]))

        def start_rest(self, after):
            self.started = forward_start(gather_wait(send_rest, recv_rest, sent[1:], lands[1:], after, "rest"), "rest")
            return self.started[-1]

        def rest(self, after):
            got = {nm: with_own(land, s) for nm, land, s in zip(SHARDED[1:], forward_wait(self.started, after, "rest"), shards[1:], strict=True)}
            whole = lambda nm: got[nm].reshape(4 * got[nm].shape[1], got[nm].shape[2])
            return (whole("w_branch_a"), whole("w_branch_b"), whole("w_o"),
                    jnp.concatenate([whole("w_ffn_gate"), whole("w_ffn_up")], axis=0), whole("w_ffn_down"))

    wts = Weights()

    per_head = lambda p: jnp.pad(p, ((0, 0), (0, LANES - p.shape[1])))
    small = (lb_row, hgrn_gnorm, conv_w, ssm_conv_b, per_head(ssm_dt_bias), per_head(ssm_a_log),
             jnp.repeat(ssm_d[0], B_INNER // 32)[None], ssm_norm, ln1_g, ln1_b, ln2_g, ln2_b)
    by_rows = lambda g: g.reshape(4, g.shape[0] // 4, g.shape[1])
    travelling = {}

    def pair_sums(names, slabs, tag):
        received = pair_exchange(slabs, "pair_exchange_" + tag)
        return [pair_add(s, r, core, "pair_add_" + nm) for nm, s, r in zip(names, slabs, received, strict=True)]

    def start_early(dws):
        dw_a, dw_b, dw_o, dw_gu, dw_d = dws
        d_gate, d_up = by_rows(dw_gu[:D_FF]), by_rows(dw_gu[D_FF:])
        travelling["pair"] = pair_start([by_rows(dw_a), by_rows(dw_b), by_rows(dw_o), d_gate, d_up, by_rows(dw_d)], "early")
        return travelling["pair"][-1]

    def between_scans(after):
        slabs, received = pair_wait(travelling["pair"], after, "early")
        travelling["pairs"] = [pair_add(s, r, core, "pair_add_" + nm) for nm, s, r in zip(SHARDED[1:], slabs, received, strict=True)]
        travelling["started"] = scatter_start(travelling["pairs"], "early")
        return travelling["started"][-1]

    def finish_early(after):
        travelling["landed"] = scatter_wait(travelling["started"], after, "early")

    def start_last(dw_in):
        travelling["pairs_in"] = pair_sums(SHARDED[:1], [split_in_proj(dw_in)], "last")
        travelling["started_in"] = scatter_start(travelling["pairs_in"], "last")
        return travelling["started_in"][-1]

    loss, grad_x, d_mod, d_wts, d_small = local_step(x[0], loss_target[0], mod, wts, small,
                                                     start_early, between_scans, finish_early, start_last)

    d_lb, d_gn, d_cw, d_cb, d_dtb, d_alog, d_dsk, d_nw, d_l1g, d_l1b, d_l2g, d_l2b = d_small
    row = jnp.concatenate(list(d_mod) + [d_lb, d_gn, d_cw.reshape(1, 4 * CONV_DIM), d_cb, d_dtb, d_alog, d_dsk, d_nw,
                                          d_l1g, d_l1b, d_l2g, d_l2b, jnp.pad(loss, ((0, 0), (0, LANES - 1)))], axis=1)
    g_all = gather_rows(row, "gather_small_grads").reshape(8, row.shape[1])
    dmod_cols = lax.dynamic_slice_in_dim(g_all, chip * mod_cols, mod_cols, axis=1)
    fin = finalize_small(g_all, c_all, dmod_cols, [given[n] for n in SMALL_PARAMS],
                         [given["m_" + n] for n in SMALL_PARAMS], [given["v_" + n] for n in SMALL_PARAMS])
    grads, deltas, new_m, new_v = {}, {}, {}, {}
    grads["w_ada"] = fin[0][None]
    grads["ssm_conv_w"] = lax.dynamic_slice_in_dim(fin[1], chip * (CONV_DIM // 4), CONV_DIM // 4, axis=1)[None]
    for i, nm in enumerate(SMALL_PARAMS):
        grads[nm], deltas[nm], new_m[nm], new_v[nm] = fin[3 + 4 * i:7 + 4 * i]

    pairs = travelling["pairs_in"] + travelling["pairs"]
    landed = scatter_wait(travelling["started_in"], fin[3], "last") + travelling["landed"]
    halves = [sum_chips(r, p, chip, core, "sum_chips_" + nm) for nm, r, p in zip(SHARDED, landed, pairs, strict=True)]
    reduced = dict(zip(SHARDED, exchange_halves(halves), strict=True))
    reduced["w_ada"], reduced["ssm_conv_w"] = grads["w_ada"][0], grads["ssm_conv_w"][0]
    reduced["w_in"] = reduced["w_in"].T
    for nm in ("w_ada", "ssm_conv_w") + SHARDED:
        flipped = nm in TRANSPOSED or nm == "w_in"
        work = (lambda a: a[0].T) if flipped else (lambda a: a[0])
        back = (lambda a: a.T[None]) if flipped else (lambda a: a[None])
        d_, m_, v_ = adam_update(work(given[nm]), reduced[nm], work(given["m_" + nm]), work(given["v_" + nm]), "adam_" + nm)
        grads[nm], deltas[nm], new_m[nm], new_v[nm] = back(reduced[nm]), back(d_), back(m_), back(v_)

    names = ("w_ada", "b_ada", "w_in", "hgrn_lb", "hgrn_gnorm", "ssm_conv_w", "ssm_conv_b", "ssm_dt_bias", "ssm_a_log",
             "ssm_d", "ssm_norm", "w_branch_a", "w_branch_b", "w_o", "ln1_g", "ln1_b", "w_ffn_gate", "w_ffn_up",
             "w_ffn_down", "ln2_g", "ln2_b")
    return (fin[2][0, 0], grad_x[None], *[grads[n] for n in names], *[deltas[n] for n in names],
            *[new_m[n] for n in names], *[new_v[n] for n in names])
```

```python
import functools

import jax
import jax.numpy as jnp
from jax import lax
from jax.experimental import pallas as pl
from jax.experimental.pallas import tpu as pltpu

F32, BF16 = jnp.float32, jnp.bfloat16
HI = lax.Precision.HIGHEST
MESH = pl.DeviceIdType.MESH

D = 1024
CHUNK = 64
LANES = 128
N_HEADS_A = 8
N_GROUPS_B = 4
B_INNER = 2048
CONV_DIM = 3072
D_FF = 2816
ALPHA = 2.0 ** 0.25
LN_EPS = 1e-5
RMS_EPS = 1e-6
ADAM_LR, ADAM_B1, ADAM_B2, ADAM_EPS, ADAM_WD, ADAM_STEP = 0.001, 0.9, 0.999, 1e-08, 0.01, 10

IN_ORIG = 11296
IN_PAD = 11520
COL_GA, COL_GB, COL_XBC, COL_Z, COL_DT = 4096, 5120, 6144, 9216, 11264
ORIG_Z, ORIG_XBC, ORIG_DT, ORIG_GA = 4096, 6144, 9216, 9248

SHARDED = ("w_in", "w_branch_a", "w_branch_b", "w_o", "w_ffn_gate", "w_ffn_up", "w_ffn_down")
TRANSPOSED = ("w_ffn_gate", "w_ffn_up")
VMEM_LIMIT = 56 * 1024 * 1024
BLOCK_BYTES = 2 * 1024 * 1024
_DIMS = {"nn": (((1,), (0,)), ((), ())), "nt": (((1,), (1,)), ((), ())), "tn": (((0,), (0,)), ((), ()))}


def _bd(a, b, mode):
    return lax.dot_general(a.astype(BF16), b.astype(BF16), _DIMS[mode], preferred_element_type=F32)


@functools.partial(jax.custom_vjp, nondiff_argnums=(2,))
def bdot(a, b, mode):
    return _bd(a, b, mode)


def _bdot_fwd(a, b, mode):
    return _bd(a, b, mode), (a, b)


def _bdot_bwd(mode, res, g):
    a, b = res
    if mode == "nn":
        return _bd(g, b, "nt"), _bd(a, g, "tn")
    if mode == "nt":
        return _bd(g, b, "nn"), _bd(g, a, "tn")
    return _bd(b, g, "nt"), _bd(a, g, "nn")


bdot.defvjp(_bdot_fwd, _bdot_bwd)


def hdot(a, b, mode="nn"):
    return lax.dot_general(a, b, _DIMS[mode], precision=HI, preferred_element_type=F32)


def _raw(a, b, mode):
    return lax.dot_general(a, b, _DIMS[mode], preferred_element_type=F32)


def _split(x, n):
    parts, rest = [], x
    for _ in range(n):
        p = rest.astype(BF16)
        parts.append(p)
        rest = rest - p.astype(F32)
    return parts


def _od(a, b, mode, exact):
    if exact == 1:
        e = b.astype(BF16)
        p = _split(a, 3)
        return (_raw(p[2], e, mode) + _raw(p[1], e, mode)) + _raw(p[0], e, mode)
    e = a.astype(BF16)
    p = _split(b, 3)
    return (_raw(e, p[2], mode) + _raw(e, p[1], mode)) + _raw(e, p[0], mode)


@functools.partial(jax.custom_vjp, nondiff_argnums=(2, 3))
def odot(a, b, mode, exact):
    return _od(a, b, mode, exact)


def _odot_fwd(a, b, mode, exact):
    return _od(a, b, mode, exact), (a, b)


def _odot_bwd(mode, exact, res, g):
    a, b = res
    if exact == 1:
        da = {"nn": lambda: _od(g, b, "nt", 1), "nt": lambda: _od(g, b, "nn", 1), "tn": lambda: _od(b, g, "nt", 0)}[mode]()
        return da, jnp.zeros_like(b)
    db = {"nn": lambda: _od(a, g, "tn", 0), "nt": lambda: _od(g, a, "tn", 1), "tn": lambda: _od(a, g, "nn", 0)}[mode]()
    return jnp.zeros_like(a), db


odot.defvjp(_odot_fwd, _odot_bwd)


_BDIMS = {"bnn": (((2,), (1,)), ((0,), (0,))), "bnt": (((2,), (2,)), ((0,), (0,))), "btn": (((1,), (1,)), ((0,), (0,)))}


def _braw(a, b, mode):
    return lax.dot_general(a, b, _BDIMS[mode], preferred_element_type=F32)


def _bdb(a, b, mode):
    return _braw(a.astype(BF16), b.astype(BF16), mode)


def _d3b(a, b, mode):
    ah, al = _split(a, 2)
    bh, bl = _split(b, 2)
    return _braw(ah, bh, mode) + (_braw(ah, bl, mode) + _braw(al, bh, mode))


def _batched_bwd(f):
    def bwd(mode, res, g):
        a, b = res
        if mode == "bnn":
            return f(g, b, "bnt"), f(a, g, "btn")
        if mode == "bnt":
            return f(g, b, "bnn"), f(g, a, "btn")
        return f(b, g, "bnt"), f(a, g, "bnn")
    return bwd


@functools.partial(jax.custom_vjp, nondiff_argnums=(2,))
def bdot_b(a, b, mode):
    return _bdb(a, b, mode)


bdot_b.defvjp(lambda a, b, mode: (_bdb(a, b, mode), (a, b)), _batched_bwd(_bdb))


@functools.partial(jax.custom_vjp, nondiff_argnums=(2,))
def dot3_b(a, b, mode):
    return _d3b(a, b, mode)


dot3_b.defvjp(lambda a, b, mode: (_d3b(a, b, mode), (a, b)), _batched_bwd(_d3b))


def _cum(tril3, x, mode):
    e = tril3.astype(BF16)
    p = _split(x, 3)
    return (_braw(e, p[2], mode) + _braw(e, p[1], mode)) + _braw(e, p[0], mode)


@jax.custom_vjp
def chunk_cumsum(tril3, x):
    return _cum(tril3, x, "bnn")


chunk_cumsum.defvjp(lambda t, x: (_cum(t, x, "bnn"), t), lambda t, g: (jnp.zeros_like(t), _cum(t, g, "btn")))


def _unstack(axis, n):
    @jax.custom_vjp
    def un(x):
        return tuple(lax.index_in_dim(x, i, axis, keepdims=False) for i in range(n))

    un.defvjp(lambda x: (un(x), None), lambda _, g: (jnp.stack(g, axis=axis),))
    return un


def _split_last(n, w):
    @jax.custom_vjp
    def sp(x):
        return tuple(x[..., i * w:(i + 1) * w] for i in range(n))

    sp.defvjp(lambda x: (sp(x), None), lambda _, g: (jnp.concatenate(g, axis=-1),))
    return sp


def sigmoid(x):
    return 1.0 / (1.0 + jnp.exp(-x))


def silu(x):
    return x * sigmoid(x)


def softplus(x):
    return jnp.maximum(x, 0.0) + jnp.log1p(jnp.exp(jnp.minimum(x, -x)))


def _ln(x):
    mu = jnp.mean(x, axis=-1, keepdims=True)
    xc = x - mu
    return xc * lax.rsqrt(jnp.mean(xc * xc, axis=-1, keepdims=True) + LN_EPS)


def _tril64():
    r = lax.broadcasted_iota(jnp.int32, (CHUNK, CHUNK), 0)
    c = lax.broadcasted_iota(jnp.int32, (CHUNK, CHUNK), 1)
    return (r >= c).astype(F32)


def hgrn_block(q, fl, iv, gr, st, lb, gn):
    tb = q.shape[0]
    nc = tb // CHUNK
    nh = N_HEADS_A
    heads = _split_last(nh, LANES)
    to4 = lambda a: jnp.stack(heads(a), axis=0).reshape(nh, nc, CHUNK, LANES)
    flat = lambda a: a.reshape(nh * nc, CHUNK, LANES)
    f = lb + (1.0 - lb) * sigmoid(fl)
    gl4, k4, qf4, v4, gr4 = to4(jnp.log(f)), to4(1.0 - f), to4(silu(q) * (128 ** -0.5)), to4(iv), to4(gr)
    tril = _tril64()
    b4 = chunk_cumsum(jnp.broadcast_to(tril[None], (nh * nc, CHUNK, CHUNK)), flat(gl4)).reshape(gl4.shape)
    blast = jnp.sum(gl4, axis=2, keepdims=True)
    ref = lax.stop_gradient(0.5 * blast)
    sc = dot3_b(flat(qf4 * jnp.exp(b4 - ref)), flat(k4 * jnp.exp(ref - b4)), "bnt") * tril
    o_intra = bdot_b(sc, flat(v4), "bnn").reshape(gl4.shape)
    chunks = _unstack(1, nc)
    qe, v_c, kd, dec = chunks(qf4 * jnp.exp(b4)), chunks(v4), chunks(k4 * jnp.exp(blast - b4)), chunks(jnp.exp(blast))
    o_inter = []
    for c in range(nc):
        o_inter.append(bdot_b(qe[c], st, "bnt"))
        st = st * dec[c] + bdot_b(v_c[c], kd[c], "btn")
    o = o_intra + jnp.stack(o_inter, axis=1)
    on = o * lax.rsqrt(jnp.mean(o * o, axis=-1, keepdims=True) + RMS_EPS) * gn
    out = (on * silu(gr4)).reshape(nh, tb, LANES)
    return jnp.concatenate(_unstack(0, nh)(out), axis=1), st


def ssd_consts(g):
    i32 = jnp.int32
    ej = lax.broadcasted_iota(i32, (LANES, 512), 0)
    ec = lax.broadcasted_iota(i32, (LANES, 512), 1)
    expand = (ej == g * 8 + (ec >> 6)).astype(F32)
    ts = lax.broadcasted_iota(i32, (CHUNK, 512), 0)
    tc = lax.broadcasted_iota(i32, (CHUNK, 512), 1)
    itile = (ts == (tc & 63)).astype(F32)
    maskall = ts >= (tc & 63)
    br = lax.broadcasted_iota(i32, (256, 256), 0)
    bc = lax.broadcasted_iota(i32, (256, 256), 1)
    blockmask = ((br >> 6) == (bc >> 6)).astype(F32)
    return expand, itile, maskall, blockmask, _tril64()


def ssd_block(x, bm, cm, dt, z, st, dtb, alog, dsk, nw, cs):
    expand, itile, maskall, blockmask, tril = cs
    tb = x.shape[0]
    nc = tb // CHUNK
    delta_heads = softplus(dt + dtb)
    delta = odot(delta_heads, expand, "nn", 1)
    a = odot(-jnp.exp(alog) * delta_heads, expand, "nn", 1)
    xdt = x * delta
    by_chunk = lambda v: v.reshape(nc, CHUNK, v.shape[-1])
    a3, xdt3, bm3, cm3 = by_chunk(a), by_chunk(xdt), by_chunk(bm), by_chunk(cm)
    acum3 = chunk_cumsum(jnp.broadcast_to(tril[None], (nc, CHUNK, CHUNK)), a3)
    alast3 = jnp.sum(a3, axis=1, keepdims=True)
    cb3 = bdot_b(cm3, jnp.concatenate([bm3] * 8, axis=1), "bnt")
    arow3 = jnp.sum(acum3 * itile, axis=1, keepdims=True)
    dec3 = jnp.exp(jnp.where(maskall, acum3 - arow3, -1e30))
    halves = _split_last(2, 256)
    intra = [bdot_b(m, jnp.concatenate([xh] * 4, axis=1) * blockmask, "bnn")
             for m, xh in zip(halves(cb3 * dec3), halves(xdt3))]
    chunks = _unstack(0, nc)
    cm_c, bm_c, xw_c, dec_c = chunks(cm3), chunks(bm3), chunks(xdt3 * jnp.exp(alast3 - acum3)), chunks(jnp.exp(alast3))
    inter = []
    for c in range(nc):
        inter.append(bdot(cm_c[c], st, "nn"))
        st = st * dec_c[c] + bdot(bm_c[c], xw_c[c], "tn")
    st_new = st
    y = (jnp.concatenate(intra, axis=-1) + jnp.stack(inter, axis=0) * jnp.exp(acum3)).reshape(tb, 512)
    yz = (y + x * dsk) * silu(z)
    return yz * lax.rsqrt(jnp.mean(yz * yz, axis=-1, keepdims=True) + RMS_EPS) * nw, st_new


def adamw(w, g, m, v):
    m = ADAM_B1 * m + (1.0 - ADAM_B1) * g
    v = ADAM_B2 * v + (1.0 - ADAM_B2) * jnp.square(g)
    m_hat = m / (1.0 - ADAM_B1 ** ADAM_STEP)
    v_hat = v / (1.0 - ADAM_B2 ** ADAM_STEP)
    return -ADAM_LR * (m_hat / (jnp.sqrt(v_hat) + ADAM_EPS) + ADAM_WD * w), m, v


def _pick(n, cands):
    for c in cands:
        if n % c == 0:
            return c
    return n


def _params(sem):
    return pltpu.CompilerParams(dimension_semantics=sem, vmem_limit_bytes=VMEM_LIMIT)


def matmul(a, b, mode, out_dtype, name, after=None, rows_per_block=None):
    if mode == "nn":
        (m, k), n = a.shape, b.shape[1]
    elif mode == "nt":
        (m, k), n = a.shape, b.shape[0]
    else:
        (k, m), n = a.shape, b.shape[1]
    tm = rows_per_block or _pick(m, (1408, 1024, 768, 512, 256, 128))
    tn = _pick(n, (1408, 1024, 768, 512, 256, 128))
    tk = _pick(k, (2304, 2048, 1408, 1024, 768, 512, 256, 128))
    nk = k // tk
    a_spec = pl.BlockSpec((tk, tm), lambda i, j, kk: (kk, i)) if mode == "tn" else pl.BlockSpec((tm, tk), lambda i, j, kk: (i, kk))
    b_spec = pl.BlockSpec((tn, tk), lambda i, j, kk: (j, kk)) if mode == "nt" else pl.BlockSpec((tk, tn), lambda i, j, kk: (kk, j))

    order = [] if after is None else [after]

    def body(a_ref, b_ref, *rest):
        o_ref, *acc = rest[len(order):]
        part = _bd(a_ref[...], b_ref[...], mode)
        if nk == 1:
            o_ref[...] = part.astype(o_ref.dtype)
            return
        acc_ref, = acc
        kk = pl.program_id(2)

        @pl.when(kk == 0)
        def _():
            acc_ref[...] = part

        @pl.when(jnp.logical_and(kk > 0, kk < nk - 1))
        def _():
            acc_ref[...] += part

        @pl.when(kk == nk - 1)
        def _():
            o_ref[...] = (acc_ref[...] + part).astype(o_ref.dtype)

    return pl.pallas_call(
        body, name=name, grid=(m // tm, n // tn, nk),
        in_specs=[a_spec, b_spec] + [pl.BlockSpec(memory_space=pl.ANY) for _ in order],
        out_specs=pl.BlockSpec((tm, tn), lambda i, j, kk: (i, j)),
        out_shape=jax.ShapeDtypeStruct((m, n), out_dtype),
        scratch_shapes=[pltpu.VMEM((tm, tn), F32)] if nk > 1 else [],
        compiler_params=_params(("parallel", "parallel", "arbitrary")),
    )(a, b, *order)


def rowwise(name, fn, rows, consts, out_rows, out_accs=(), tm_max=256, into=None, new_wide=None):
    t = rows[0][0].shape[0]
    tm = _pick(t, (tm_max, 128, 64, 32, 16, 8))
    n_r, n_c, n_o = len(rows), len(consts), len(out_rows)
    n_alias = 0 if into is None else 1

    def body(*refs):
        r_in = [r[...] for r in refs[:n_r]]
        c_in = [r[...] for r in refs[n_r:n_r + n_c]]
        refs = refs[:n_r + n_c] + refs[n_r + n_c + n_alias:]
        o_refs = refs[n_r + n_c:n_r + n_c + n_o]
        a_refs = refs[n_r + n_c + n_o:]
        ro, ao = fn(r_in, c_in)
        for ref, val in zip(o_refs, ro, strict=True):
            ref[...] = val.astype(ref.dtype)
        if a_refs:
            @pl.when(pl.program_id(0) == 0)
            def _():
                for ref in a_refs:
                    ref[...] = jnp.zeros_like(ref)

            for ref, val in zip(a_refs, ao, strict=True):
                ref[...] += val

    in_specs = [pl.BlockSpec((tm, w), functools.partial(lambda i, cb: (i, cb), cb=cb)) for _, w, cb in rows]
    in_specs += [pl.BlockSpec(c.shape, lambda i: (0, 0)) for c in consts]
    out_specs = [pl.BlockSpec((tm, w), lambda i: (i, 0)) for w, _ in out_rows]
    out_specs += [pl.BlockSpec(s, lambda i: (0, 0)) for s in out_accs]
    out_shape = [jax.ShapeDtypeStruct((t, w), dt) for w, dt in out_rows]
    out_shape += [jax.ShapeDtypeStruct(s, F32) for s in out_accs]
    operands = [r[0] for r in rows] + list(consts)
    aliases = {}
    if into is not None:
        target, cb = into
        in_specs.append(pl.BlockSpec(memory_space=pl.ANY))
        operands.append(target)
        out_specs[0] = pl.BlockSpec((tm, out_rows[0][0]), lambda i: (i, cb))
        out_shape[0] = jax.ShapeDtypeStruct(target.shape, target.dtype)
        aliases = {len(operands) - 1: 0}
    if new_wide is not None:
        width, cb = new_wide
        out_specs[0] = pl.BlockSpec((tm, out_rows[0][0]), lambda i: (i, cb))
        out_shape[0] = jax.ShapeDtypeStruct((t, width), out_rows[0][1])
    return pl.pallas_call(
        body, name=name, grid=(t // tm,), in_specs=in_specs, out_specs=out_specs, out_shape=out_shape,
        input_output_aliases=aliases, compiler_params=_params(("arbitrary",)),
    )(*operands)


def _full(a):
    return (a, a.shape[1], 0)


def _time_block(t):
    return _pick(t, (256, 128, 64))


def _quarters(ref):
    return [ref[:, seg * D:(seg + 1) * D] for seg in range(4)]


def hgrn_forward(proj, lb, gn):
    t = proj.shape[0]
    tb = _time_block(t)
    nb = t // tb

    def body(qfig_ref, lb_ref, gn_ref, o_ref, st_ref, state):
        @pl.when(pl.program_id(0) == 0)
        def _():
            state[...] = jnp.zeros_like(state)

        st = state[...]
        st_ref[...] = st
        out, st_new = hgrn_block(*_quarters(qfig_ref), st, lb_ref[...], gn_ref[...])
        o_ref[...] = out.astype(o_ref.dtype)
        state[...] = st_new

    return pl.pallas_call(
        body, name="hgrn_forward", grid=(nb,),
        in_specs=[pl.BlockSpec((tb, 4 * D), lambda j: (j, 0)),
                  pl.BlockSpec((1, D), lambda j: (0, 0)), pl.BlockSpec((1, LANES), lambda j: (0, 0))],
        out_specs=[pl.BlockSpec((tb, D), lambda j: (j, 0)),
                   pl.BlockSpec((None, N_HEADS_A, LANES, LANES), lambda j: (j, 0, 0, 0))],
        out_shape=[jax.ShapeDtypeStruct((t, D), BF16),
                   jax.ShapeDtypeStruct((nb, N_HEADS_A, LANES, LANES), F32)],
        scratch_shapes=[pltpu.VMEM((N_HEADS_A, LANES, LANES), F32)],
        compiler_params=_params(("arbitrary",)),
    )(proj, lb, gn)


def hgrn_backward(proj, states, d_out, lb, gn, d_proj):
    t = proj.shape[0]
    tb = _time_block(t)
    nb = t // tb

    def body(qfig_ref, st_ref, do_ref, lb_ref, gn_ref, _, dqfig_ref, dlb_ref, dgn_ref, d_state):
        @pl.when(pl.program_id(0) == 0)
        def _():
            d_state[...] = jnp.zeros_like(d_state)
            dlb_ref[...] = jnp.zeros_like(dlb_ref)
            dgn_ref[...] = jnp.zeros_like(dgn_ref)

        _, vjp = jax.vjp(hgrn_block, *_quarters(qfig_ref), st_ref[...], lb_ref[...], gn_ref[...])
        dq, df, di, dg, dst, dlb, dgn = vjp((do_ref[...], d_state[...]))
        for seg, val in enumerate((dq, df, di, dg)):
            dqfig_ref[:, seg * D:(seg + 1) * D] = val.astype(dqfig_ref.dtype)
        d_state[...] = dst
        dlb_ref[...] += dlb
        dgn_ref[...] += dgn

    rev = lambda j: nb - 1 - j
    return pl.pallas_call(
        body, name="hgrn_backward", grid=(nb,),
        in_specs=[pl.BlockSpec((tb, 4 * D), lambda j: (rev(j), 0)),
                  pl.BlockSpec((None, N_HEADS_A, LANES, LANES), lambda j: (rev(j), 0, 0, 0)),
                  pl.BlockSpec((tb, D), lambda j: (rev(j), 0)),
                  pl.BlockSpec((1, D), lambda j: (0, 0)), pl.BlockSpec((1, LANES), lambda j: (0, 0)),
                  pl.BlockSpec(memory_space=pl.ANY)],
        out_specs=[pl.BlockSpec((tb, 4 * D), lambda j: (rev(j), 0)),
                   pl.BlockSpec((1, D), lambda j: (0, 0)), pl.BlockSpec((1, LANES), lambda j: (0, 0))],
        out_shape=[jax.ShapeDtypeStruct(d_proj.shape, d_proj.dtype), jax.ShapeDtypeStruct((1, D), F32),
                   jax.ShapeDtypeStruct((1, LANES), F32)],
        input_output_aliases={5: 0},
        scratch_shapes=[pltpu.VMEM((N_HEADS_A, LANES, LANES), F32)],
        compiler_params=_params(("arbitrary",)),
    )(proj, states, d_out, lb, gn, d_proj)


def _ssd_in_specs(tb, tmap):
    return [pl.BlockSpec((tb, 512), lambda g, j: (tmap(j), g)),
            pl.BlockSpec((tb, LANES), lambda g, j: (tmap(j), 16 + g)),
            pl.BlockSpec((tb, LANES), lambda g, j: (tmap(j), 20 + g)),
            pl.BlockSpec((tb, LANES), lambda g, j: (tmap(j), COL_DT // LANES)),
            pl.BlockSpec((tb, 512), lambda g, j: (tmap(j), COL_Z // 512 + g))]


def ssd_forward(xc, proj, dtb, alog, dsk, nw):
    t = proj.shape[0]
    tb = _time_block(t)
    nb = t // tb

    def body(x_ref, b_ref, c_ref, dt_ref, z_ref, dtb_ref, alog_ref, dsk_ref, nw_ref, o_ref, st_ref, state):
        @pl.when(pl.program_id(1) == 0)
        def _():
            state[...] = jnp.zeros_like(state)

        st = state[...]
        st_ref[...] = st
        out, st_new = ssd_block(x_ref[...], b_ref[...], c_ref[...], dt_ref[...], z_ref[...], st,
                                dtb_ref[...], alog_ref[...], dsk_ref[...], nw_ref[...], ssd_consts(pl.program_id(0)))
        o_ref[...] = out.astype(o_ref.dtype)
        state[...] = st_new

    vec = pl.BlockSpec((1, 512), lambda g, j: (0, g))
    heads = pl.BlockSpec((1, LANES), lambda g, j: (0, 0))
    return pl.pallas_call(
        body, name="ssd_forward", grid=(N_GROUPS_B, nb),
        in_specs=_ssd_in_specs(tb, lambda j: j) + [heads, heads, vec, vec],
        out_specs=[pl.BlockSpec((tb, 512), lambda g, j: (j, g)),
                   pl.BlockSpec((None, None, LANES, 512), lambda g, j: (j, g, 0, 0))],
        out_shape=[jax.ShapeDtypeStruct((t, B_INNER), BF16),
                   jax.ShapeDtypeStruct((nb, N_GROUPS_B, LANES, 512), F32)],
        scratch_shapes=[pltpu.VMEM((LANES, 512), F32)],
        compiler_params=_params(("arbitrary", "arbitrary")),
    )(xc, xc, xc, proj, proj, dtb, alog, dsk, nw)


def ssd_backward(xc, proj, states, d_out, dtb, alog, dsk, nw, d_proj):
    t = proj.shape[0]
    tb = _time_block(t)
    nb = t // tb
    rev = lambda j: nb - 1 - j

    def body(x_ref, b_ref, c_ref, dt_ref, z_ref, st_ref, do_ref, dtb_ref, alog_ref, dsk_ref, nw_ref, _,
             dx_ref, db_ref, dc_ref, ddt_ref, dz_ref, ddtb_ref, dalog_ref, ddsk_ref, dnw_ref, d_state):
        accs = (ddtb_ref, dalog_ref, ddsk_ref, dnw_ref)

        @pl.when(pl.program_id(1) == 0)
        def _():
            d_state[...] = jnp.zeros_like(d_state)
            for ref in accs:
                ref[...] = jnp.zeros_like(ref)

        cs = ssd_consts(pl.program_id(0))
        fn = lambda *a: ssd_block(*a, cs)
        _, vjp = jax.vjp(fn, x_ref[...], b_ref[...], c_ref[...], dt_ref[...], z_ref[...], st_ref[...],
                         dtb_ref[...], alog_ref[...], dsk_ref[...], nw_ref[...])
        dx, db, dc, ddt, dz, dst, *dpar = vjp((do_ref[...], d_state[...]))
        dx_ref[...] = dx
        db_ref[...] = db
        dc_ref[...] = dc
        ddt_ref[...] = ddt
        dz_ref[...] = dz.astype(dz_ref.dtype)
        d_state[...] = dst
        for ref, val in zip(accs, dpar, strict=True):
            ref[...] += val

    vec = pl.BlockSpec((1, 512), lambda g, j: (0, g))
    heads = pl.BlockSpec((1, LANES), lambda g, j: (0, 0))
    acc = pl.BlockSpec((None, 1, 512), lambda g, j: (g, 0, 0))
    acc_heads = pl.BlockSpec((None, 1, LANES), lambda g, j: (g, 0, 0))
    return pl.pallas_call(
        body, name="ssd_backward", grid=(N_GROUPS_B, nb),
        in_specs=_ssd_in_specs(tb, rev)
        + [pl.BlockSpec((None, None, LANES, 512), lambda g, j: (rev(j), g, 0, 0)),
           pl.BlockSpec((tb, 512), lambda g, j: (rev(j), g))] + [heads, heads, vec, vec] + [pl.BlockSpec(memory_space=pl.ANY)],
        out_specs=[pl.BlockSpec((tb, 512), lambda g, j: (rev(j), g)),
                   pl.BlockSpec((tb, LANES), lambda g, j: (rev(j), g)),
                   pl.BlockSpec((tb, LANES), lambda g, j: (rev(j), g)),
                   pl.BlockSpec((None, tb, LANES), lambda g, j: (g, rev(j), 0)),
                   pl.BlockSpec((tb, 512), lambda g, j: (rev(j), COL_Z // 512 + g)), acc_heads, acc_heads, acc, acc],
        out_shape=[jax.ShapeDtypeStruct((t, B_INNER), F32), jax.ShapeDtypeStruct((t, 512), F32),
                   jax.ShapeDtypeStruct((t, 512), F32), jax.ShapeDtypeStruct((N_GROUPS_B, t, LANES), F32),
                   jax.ShapeDtypeStruct(d_proj.shape, d_proj.dtype)]
        + [jax.ShapeDtypeStruct((N_GROUPS_B, 1, LANES), F32)] * 2 + [jax.ShapeDtypeStruct((N_GROUPS_B, 1, 512), F32)] * 2,
        input_output_aliases={11: 4},
        scratch_shapes=[pltpu.VMEM((LANES, 512), F32)],
        compiler_params=_params(("arbitrary", "arbitrary")),
    )(xc, xc, xc, proj, proj, states, d_out, dtb, alog, dsk, nw, d_proj)


CONV_HALO = 8


def _shift_down(halo_then_tile, s, tm):
    if s == 0:
        return halo_then_tile[CONV_HALO:CONV_HALO + tm]
    return pltpu.roll(halo_then_tile, s, 0)[CONV_HALO:CONV_HALO + tm]


def _conv_pre(cur, prev, w, b, tm):
    stacked = jnp.concatenate([prev, cur], axis=0)
    taps = [_shift_down(stacked, 3 - j, tm) for j in range(4)]
    pre = b + taps[0] * w[0:1] + taps[1] * w[1:2] + taps[2] * w[2:3] + taps[3] * w[3:4]
    return pre, taps


def _conv_specs(t, tm):
    per = tm // CONV_HALO
    cur = pl.BlockSpec((tm, CONV_DIM), lambda i: (i, COL_XBC // CONV_DIM))
    prev = pl.BlockSpec((CONV_HALO, CONV_DIM), lambda i: (jnp.maximum(i * per - 1, 0), COL_XBC // CONV_DIM))
    return cur, prev


def conv_forward(proj, w, b):
    t = proj.shape[0]
    tm = _pick(t, (256, 128, 64))

    def body(cur_ref, prev_ref, w_ref, b_ref, o_ref):
        prev = jnp.where(pl.program_id(0) == 0, 0.0, prev_ref[...])
        pre, _ = _conv_pre(cur_ref[...], prev, w_ref[...], b_ref[...], tm)
        o_ref[...] = silu(pre)

    cur, prev = _conv_specs(t, tm)
    return pl.pallas_call(
        body, name="conv_forward", grid=(t // tm,),
        in_specs=[cur, prev, pl.BlockSpec((4, CONV_DIM), lambda i: (0, 0)), pl.BlockSpec((1, CONV_DIM), lambda i: (0, 0))],
        out_specs=pl.BlockSpec((tm, CONV_DIM), lambda i: (i, 0)),
        out_shape=jax.ShapeDtypeStruct((t, CONV_DIM), F32),
        compiler_params=_params(("arbitrary",)),
    )(proj, proj, w, b)


def conv_backward_pre(proj, dx, db_, dc_, w, b):
    t = proj.shape[0]
    tm = _pick(t, (256, 128, 64))

    def body(cur_ref, prev_ref, dx_ref, dbm_ref, dcm_ref, w_ref, b_ref, dpre_ref, dw_ref, dbias_ref):
        @pl.when(pl.program_id(0) == 0)
        def _():
            dw_ref[...] = jnp.zeros_like(dw_ref)
            dbias_ref[...] = jnp.zeros_like(dbias_ref)

        first = pl.program_id(0) == 0
        for lo, hi, src in ((0, B_INNER, dx_ref), (B_INNER, B_INNER + 512, dbm_ref), (B_INNER + 512, CONV_DIM, dcm_ref)):
            cols = slice(lo, hi)
            prev = jnp.where(first, 0.0, prev_ref[:, cols])
            pre, taps = _conv_pre(cur_ref[:, cols], prev, w_ref[:, cols], b_ref[:, cols], tm)
            sg = sigmoid(pre)
            dpre = src[...] * (sg * (1.0 + pre * (1.0 - sg)))
            dpre_ref[:, cols] = dpre
            dbias_ref[:, cols] += jnp.sum(dpre, axis=0, keepdims=True)
            for j in range(4):
                dw_ref[j:j + 1, cols] += jnp.sum(dpre * taps[j], axis=0, keepdims=True)

    cur, prev = _conv_specs(t, tm)
    row = lambda w_: pl.BlockSpec((tm, w_), lambda i: (i, 0))
    return pl.pallas_call(
        body, name="conv_backward_pre", grid=(t // tm,),
        in_specs=[cur, prev, row(B_INNER), row(512), row(512),
                  pl.BlockSpec((4, CONV_DIM), lambda i: (0, 0)), pl.BlockSpec((1, CONV_DIM), lambda i: (0, 0))],
        out_specs=[row(CONV_DIM), pl.BlockSpec((4, CONV_DIM), lambda i: (0, 0)), pl.BlockSpec((1, CONV_DIM), lambda i: (0, 0))],
        out_shape=[jax.ShapeDtypeStruct((t, CONV_DIM), F32), jax.ShapeDtypeStruct((4, CONV_DIM), F32),
                   jax.ShapeDtypeStruct((1, CONV_DIM), F32)],
        compiler_params=_params(("arbitrary",)),
    )(proj, proj, dx, db_, dc_, w, b)


def conv_backward_input(dpre, w, d_proj):
    t = dpre.shape[0]
    tm = _pick(t, (256, 128, 64))
    per = tm // CONV_HALO
    last = t // CONV_HALO - 1
    nt = t // tm

    def body(cur_ref, nxt_ref, w_ref, _, o_ref):
        nxt = jnp.where(pl.program_id(0) == nt - 1, 0.0, nxt_ref[...])
        stacked = jnp.concatenate([cur_ref[...], nxt], axis=0)
        w_ = w_ref[...]
        acc = stacked[0:tm] * w_[3:4]
        for j in range(3):
            s = 3 - j
            acc = acc + pltpu.roll(stacked, tm + CONV_HALO - s, 0)[0:tm] * w_[j:j + 1]
        o_ref[...] = acc.astype(o_ref.dtype)

    return pl.pallas_call(
        body, name="conv_backward_input", grid=(nt,),
        in_specs=[pl.BlockSpec((tm, CONV_DIM), lambda i: (i, 0)),
                  pl.BlockSpec((CONV_HALO, CONV_DIM), lambda i: (jnp.minimum((i + 1) * per, last), 0)),
                  pl.BlockSpec((4, CONV_DIM), lambda i: (0, 0)), pl.BlockSpec(memory_space=pl.ANY)],
        out_specs=pl.BlockSpec((tm, CONV_DIM), lambda i: (i, COL_XBC // CONV_DIM)),
        out_shape=jax.ShapeDtypeStruct(d_proj.shape, d_proj.dtype),
        input_output_aliases={3: 0},
        compiler_params=_params(("arbitrary",)),
    )(dpre, dpre, w, d_proj)


def stage_modulate(x, sc, sh):
    return _ln(x) * (1.0 + sc) + sh


def stage_merge(ga, gb, ya, yb):
    return sigmoid(ga) * ya + sigmoid(gb) * yb


def stage_post_mixer(x, h, g1, ln_g, ln_b, sc2, sh2):
    x1 = _ln(ALPHA * x + g1 * h) * ln_g + ln_b
    return x1, _ln(x1) * (1.0 + sc2) + sh2


def stage_swiglu(a, b):
    return silu(a) * b


def stage_loss(x1, hf, tgt, g2, ln_g, ln_b):
    x2 = _ln(ALPHA * x1 + g2 * hf) * ln_g + ln_b
    return 0.5 * jnp.sum(jnp.mean(jnp.square(x2 - tgt), axis=-1, keepdims=True), axis=0, keepdims=True)


def local_step(x, tgt, mod, wts, small, early=None, mid=None, late=None, last=None):
    sh1, sc1, g1, sh2, sc2, g2 = mod
    lb, gn, conv_w, conv_b, dtb, alog, dsk, nw, ln1_g, ln1_b, ln2_g, ln2_b = small
    vec = (1, D)

    (u1,) = rowwise("modulate1", lambda r, c: ((stage_modulate(r[0], *c),), ()), [_full(x)], [sc1, sh1], [(D, BF16)])
    w_in = wts.input_projection(u1)
    proj = matmul(u1, w_in, "nn", F32, "in_proj", rows_per_block=min(2048, x.shape[0]))
    ya_in, st_a = hgrn_forward(proj, lb, gn + wts.start_rest(proj)[0:1])
    xc = conv_forward(proj, conv_w, conv_b)
    w_a, w_b, w_o, w_gu, w_d = wts.rest(xc)
    yb_in, st_b = ssd_forward(xc, proj, dtb, alog, dsk, nw)
    ya = matmul(ya_in, w_a, "nn", F32, "branch_a")
    yb = matmul(yb_in, w_b, "nn", F32, "branch_b")
    gate_rows = [(proj, D, COL_GA // D), (proj, D, COL_GB // D), _full(ya), _full(yb)]
    (merged,) = rowwise("merge", lambda r, c: ((stage_merge(*r),), ()), gate_rows, [], [(D, BF16)])
    h = matmul(merged, w_o, "nn", F32, "out_proj")
    post_consts = [g1, ln1_g, ln1_b, sc2, sh2]
    x1, u2 = rowwise("post_mixer", lambda r, c: (stage_post_mixer(*r, *c), ()), [_full(x), _full(h)], post_consts,
                     [(D, F32), (D, BF16)])
    ab = matmul(u2, w_gu, "nt", F32, "ffn_in")
    (p,) = rowwise("swiglu", lambda r, c: ((stage_swiglu(*r),), ()), [(ab, D_FF, 0), (ab, D_FF, 1)], [], [(D_FF, BF16)])
    hf = matmul(p, w_d, "nn", F32, "ffn_out")

    def loss_bwd(r, c):
        loss, vjp = jax.vjp(stage_loss, *r, *c)
        dx1, dhf, _, dg2, dlg, dlb_ = vjp(jnp.ones((1, 1), F32))
        return (dx1, dhf), (loss, dg2, dlg, dlb_)

    dx1, dhf, loss, dg2, dln2_g, dln2_b = rowwise(
        "loss_backward", loss_bwd, [_full(x1), _full(hf), _full(tgt)], [g2, ln2_g, ln2_b],
        [(D, F32), (D, BF16)], [(1, 1), vec, vec, vec])
    dp = matmul(dhf, w_d, "nt", F32, "ffn_out_dx")
    dw_d = matmul(p, dhf, "tn", F32, "ffn_out_dw")

    def swiglu_bwd(r, c):
        _, vjp = jax.vjp(stage_swiglu, r[0], r[1])
        da, db_ = vjp(r[2])
        return (jnp.concatenate([da, db_], axis=1),), ()

    (dab,) = rowwise("swiglu_backward", swiglu_bwd, [(ab, D_FF, 0), (ab, D_FF, 1), _full(dp)], [], [(2 * D_FF, BF16)])
    du2 = matmul(dab, w_gu, "nn", F32, "ffn_in_dx")
    dw_gu = matmul(dab, u2, "tn", F32, "ffn_in_dw")

    def post_bwd(r, c):
        _, vjp = jax.vjp(stage_post_mixer, r[0], r[1], *c)
        dx, dh, *dc = vjp((r[2], r[3]))
        return (dx, dh), tuple(dc)

    dx_a, dh, dg1, dln1_g, dln1_b, dsc2, dsh2 = rowwise(
        "post_mixer_backward", post_bwd, [_full(x), _full(h), _full(dx1), _full(du2)], post_consts,
        [(D, F32), (D, BF16)], [vec] * 5)
    dmerged = matmul(dh, w_o, "nt", F32, "out_proj_dx")
    dw_o = matmul(merged, dh, "tn", F32, "out_proj_dw")

    def merge_bwd(r, c):
        _, vjp = jax.vjp(stage_merge, *r[:4])
        dga, dgb, dya, dyb = vjp(r[4])
        return (jnp.concatenate([dga, dgb], axis=1), dya, dyb), ()

    dproj, dya, dyb = rowwise("merge_backward", merge_bwd, gate_rows + [_full(dmerged)], [],
                              [(2 * D, BF16), (D, BF16), (D, BF16)], new_wide=(IN_PAD, COL_GA // (2 * D)))
    dya_in = matmul(dya, w_a, "nt", F32, "branch_a_dx")
    dw_a = matmul(ya_in, dya, "tn", F32, "branch_a_dw")
    dyb_in = matmul(dyb, w_b, "nt", F32, "branch_b_dx")
    dw_b = matmul(yb_in, dyb, "tn", F32, "branch_b_dw")
    gn_after = gn if early is None else gn + early((dw_a, dw_b, dw_o, dw_gu, dw_d))[0:1]
    dproj, dlb, dgn = hgrn_backward(proj, st_a, dya_in, lb, gn_after, dproj)
    dtb_after = dtb if mid is None else dtb + mid(dlb)[0:1, 0:1]
    dxs, dbm, dcm, ddt, dproj, ddtb, dalog, ddsk, dnw = ssd_backward(xc, proj, st_b, dyb_in, dtb_after, alog, dsk, nw, dproj)
    dpre, dconv_w, dconv_b = conv_backward_pre(proj, dxs, dbm, dcm, conv_w, conv_b)
    if late is not None:
        late(dconv_b)
    dproj = conv_backward_input(dpre, conv_w, dproj)
    t = x.shape[0]
    tail = jnp.concatenate([jnp.sum(ddt, axis=0).astype(BF16), jnp.zeros((t, IN_PAD - COL_DT - LANES), BF16)], axis=1)
    dproj = lax.dynamic_update_slice(dproj, tail, (0, COL_DT))
    dw_in = matmul(u1, dproj, "tn", F32, "in_proj_dw")
    du1 = matmul(dproj, w_in, "nt", F32, "in_proj_dx", after=None if last is None else last(dw_in))

    def mod_bwd(r, c):
        _, vjp = jax.vjp(stage_modulate, r[0], *c)
        dx, dsc, dsh = vjp(r[1])
        return (dx + r[2],), (dsc, dsh)

    grad_x, dsc1, dsh1 = rowwise("modulate1_backward", mod_bwd, [_full(x), _full(du1), _full(dx_a)], [sc1, sh1],
                                 [(D, F32)], [vec, vec])
    d_mod = (dsh1, dsc1, dg1, dsh2, dsc2, dg2)
    d_wts = (dw_in, dw_a, dw_b, dw_o, dw_gu, dw_d)
    d_small = (dlb, dgn, dconv_w, dconv_b, jnp.sum(ddtb, axis=0),
               jnp.sum(dalog, axis=0), ddsk.reshape(1, B_INNER), dnw.reshape(1, B_INNER),
               dln1_g, dln1_b, dln2_g, dln2_b)
    return loss, grad_x, d_mod, d_wts, d_small


HBM = pl.BlockSpec(memory_space=pltpu.HBM)
SEM = pl.BlockSpec(memory_space=pltpu.SEMAPHORE)
DATAFLOW = pltpu.SideEffectType.DATAFLOW_SIDE_EFFECTING


def _place():
    return lax.axis_index("x"), lax.axis_index("y"), lax.axis_index("c")


def _other_chips(x, y):
    return [(1 - x, y), (x, 1 - y), (1 - x, 1 - y)]


def _remote(src, dst, send_sem, recv_sem, device):
    return pltpu.make_async_remote_copy(src_ref=src, dst_ref=dst, send_sem=send_sem, recv_sem=recv_sem,
                                        device_id=device, device_id_type=MESH)


def gather_rows(v, name):
    n = v.shape[1]

    def body(v_ref, out_ref, send_sems, recv_sems, local_sem):
        x, y, c = _place()
        mine = pltpu.make_async_copy(v_ref, out_ref.at[4 * x + 2 * y + c], local_sem)
        mine.start()
        sends, recvs = [], []
        for m in range(1, 8):
            px = 1 - x if m & 4 else x
            py = 1 - y if m & 2 else y
            pc = 1 - c if m & 1 else c
            sends.append(_remote(v_ref, out_ref.at[4 * x + 2 * y + c], send_sems.at[m - 1], recv_sems.at[m - 1], (px, py, pc)))
            recvs.append(_remote(v_ref, out_ref.at[4 * px + 2 * py + pc], send_sems.at[m - 1], recv_sems.at[m - 1], (px, py, pc)))
        for cp in sends:
            cp.start()
        for cp in recvs:
            cp.wait_recv()
        for cp in sends:
            cp.wait_send()
        mine.wait()

    return pl.pallas_call(
        body, name=name, in_specs=[HBM], out_specs=HBM,
        out_shape=jax.ShapeDtypeStruct((8, 1, n), v.dtype),
        scratch_shapes=[pltpu.SemaphoreType.DMA((7,)), pltpu.SemaphoreType.DMA((7,)), pltpu.SemaphoreType.DMA],
    )(v)


def exchange_rows(part, name):
    w = part.shape[2]

    def body(p_ref, out_ref, send_sems, recv_sems, local_sem):
        x, y, c = _place()
        k = 2 * x + y
        mine = pltpu.make_async_copy(p_ref.at[4 * x + 2 * y + c], out_ref.at[k], local_sem)
        mine.start()
        sends, recvs = [], []
        for j, (px, py) in enumerate(_other_chips(x, y)):
            sends.append(_remote(p_ref.at[4 * px + 2 * py + c], out_ref.at[k], send_sems.at[j], recv_sems.at[j], (px, py, c)))
            recvs.append(_remote(p_ref.at[4 * px + 2 * py + c], out_ref.at[2 * px + py], send_sems.at[j], recv_sems.at[j], (px, py, c)))
        for cp in sends:
            cp.start()
        for cp in recvs:
            cp.wait_recv()
        for cp in sends:
            cp.wait_send()
        mine.wait()

    return pl.pallas_call(
        body, name=name, in_specs=[HBM], out_specs=HBM,
        out_shape=jax.ShapeDtypeStruct((4, 1, w), part.dtype),
        scratch_shapes=[pltpu.SemaphoreType.DMA((3,)), pltpu.SemaphoreType.DMA((3,)), pltpu.SemaphoreType.DMA],
    )(part)


def _half_of_slot(ref, rows, px, py, pc):
    return ref.at[2 * px + py, pl.ds(pc * (rows // 2), rows // 2), :]


def gather_start(shards, after):
    n = len(shards)

    def body(*refs):
        w_refs, land_refs = refs[:n], refs[n:2 * n]
        send_a, recv_a, send_b, recv_b = refs[2 * n + 1:2 * n + 5]
        token = refs[-1]
        x, y, c = _place()
        for i in range(n):
            rows = shards[i].shape[0]
            for j, (px, py) in enumerate(_other_chips(x, y)):
                sems = (send_a.at[j], recv_a.at[j]) if i == 0 else (send_b.at[j * (n - 1) + i - 1], recv_b.at[j * (n - 1) + i - 1])
                _remote(w_refs[i].at[pl.ds(c * (rows // 2), rows // 2), :], _half_of_slot(land_refs[i], rows, x, y, c),
                        *sems, (px, py, c)).start()
        token[...] = jnp.zeros_like(token)

    hbm = lambda a: pltpu.with_memory_space_constraint(a, pltpu.HBM)
    lands = [lax.empty((4,) + s.shape, s.dtype) for s in shards]
    dma = pltpu.SemaphoreType.DMA
    return pl.pallas_call(
        body, name="gather_start",
        out_shape=(dma((3,)), dma((3,)), dma((3 * (n - 1),)), dma((3 * (n - 1),)),
                   *[pltpu.HBM(a.shape, a.dtype) for a in list(shards) + lands], jax.ShapeDtypeStruct((8, LANES), F32)),
        in_specs=[HBM] * (2 * n) + [pl.BlockSpec(memory_space=pl.ANY)],
        out_specs=(SEM, SEM, SEM, SEM, *[HBM] * (2 * n), pl.BlockSpec(memory_space=pltpu.VMEM)),
        input_output_aliases={i: 4 + i for i in range(2 * n)},
        compiler_params=pltpu.CompilerParams(has_side_effects=DATAFLOW),
    )(*[hbm(a) for a in list(shards) + lands], after)


def gather_wait(send_sems, recv_sems, shards, lands, after, tag):
    n = len(shards)

    def body(*refs):
        w_refs, land_refs = refs[:n], refs[n:2 * n]
        send_ref, recv_ref = refs[2 * n], refs[2 * n + 1]
        x, y, c = _place()
        for i in range(n):
            rows = shards[i].shape[0]
            for j, (px, py) in enumerate(_other_chips(x, y)):
                cp = _remote(w_refs[i].at[pl.ds(c * (rows // 2), rows // 2), :], _half_of_slot(land_refs[i], rows, px, py, c),
                             send_ref.at[j * n + i], recv_ref.at[j * n + i], (px, py, c))
                cp.wait_send()
                cp.wait_recv()

    out = pl.pallas_call(
        body, name="gather_wait_" + tag,
        out_shape=tuple(pltpu.HBM(a.shape, a.dtype) for a in list(shards) + list(lands)),
        in_specs=[HBM] * (2 * n) + [SEM, SEM, pl.BlockSpec(memory_space=pl.ANY)], out_specs=tuple([HBM] * (2 * n)),
        input_output_aliases={i: i for i in range(2 * n)},
        compiler_params=pltpu.CompilerParams(has_side_effects=DATAFLOW),
    )(*shards, *lands, send_sems, recv_sems, after)
    return list(out[n:])


def forward_start(lands, tag):
    n = len(lands)

    def body(*refs):
        land_refs = refs[:n]
        send_sems, recv_sems = refs[n], refs[n + 1]
        token = refs[-1]
        x, y, c = _place()
        for i in range(n):
            rows = lands[i].shape[1]
            for j, (px, py) in enumerate(_other_chips(x, y)):
                mine = _half_of_slot(land_refs[i], rows, px, py, c)
                _remote(mine, mine, send_sems.at[j * n + i], recv_sems.at[j * n + i], (x, y, 1 - c)).start()
        token[...] = jnp.zeros_like(token)

    dma = pltpu.SemaphoreType.DMA
    return pl.pallas_call(
        body, name="forward_start_" + tag,
        out_shape=(dma((3 * n,)), dma((3 * n,)), *[pltpu.HBM(a.shape, a.dtype) for a in lands],
                   jax.ShapeDtypeStruct((8, LANES), F32)),
        in_specs=[HBM] * n, out_specs=(SEM, SEM, *[HBM] * n, pl.BlockSpec(memory_space=pltpu.VMEM)),
        input_output_aliases={i: 2 + i for i in range(n)},
        compiler_params=pltpu.CompilerParams(has_side_effects=DATAFLOW),
    )(*lands)


def forward_wait(started, after, tag):
    send_sems, recv_sems, *rest = started
    lands = rest[:-1]
    n = len(lands)

    def body(*refs):
        land_refs = refs[:n]
        send_ref, recv_ref = refs[n], refs[n + 1]
        x, y, c = _place()
        for i in range(n):
            rows = lands[i].shape[1]
            for j, (px, py) in enumerate(_other_chips(x, y)):
                cp = _remote(_half_of_slot(land_refs[i], rows, px, py, c), _half_of_slot(land_refs[i], rows, px, py, 1 - c),
                             send_ref.at[j * n + i], recv_ref.at[j * n + i], (x, y, 1 - c))
                cp.wait_send()
                cp.wait_recv()

    out = pl.pallas_call(
        body, name="forward_wait_" + tag,
        out_shape=tuple(pltpu.HBM(a.shape, a.dtype) for a in lands),
        in_specs=[HBM] * n + [SEM, SEM, pl.BlockSpec(memory_space=pl.ANY)], out_specs=tuple([HBM] * n),
        input_output_aliases={i: i for i in range(n)},
        compiler_params=pltpu.CompilerParams(has_side_effects=DATAFLOW),
    )(*lands, send_sems, recv_sems, after)
    return list(out)


def pair_start(slabs, tag):
    n = len(slabs)

    def body(*refs):
        g_refs, land_refs = refs[:n], refs[n:2 * n]
        send_sems, recv_sems = refs[2 * n], refs[2 * n + 1]
        token = refs[-1]
        x, y, c = _place()
        for i in range(n):
            hr = slabs[i].shape[1] // 2
            _remote(g_refs[i].at[:, pl.ds((1 - c) * hr, hr), :], land_refs[i], send_sems.at[i], recv_sems.at[i],
                    (x, y, 1 - c)).start()
        token[...] = jnp.zeros_like(token)

    hbm = lambda a: pltpu.with_memory_space_constraint(a, pltpu.HBM)
    lands = [lax.empty((4, s.shape[1] // 2, s.shape[2]), s.dtype) for s in slabs]
    dma = pltpu.SemaphoreType.DMA
    return pl.pallas_call(
        body, name="pair_start_" + tag,
        out_shape=(dma((n,)), dma((n,)), *[pltpu.HBM(a.shape, a.dtype) for a in list(slabs) + lands],
                   jax.ShapeDtypeStruct((8, LANES), F32)),
        in_specs=[HBM] * (2 * n), out_specs=(SEM, SEM, *[HBM] * (2 * n), pl.BlockSpec(memory_space=pltpu.VMEM)),
        input_output_aliases={i: 2 + i for i in range(2 * n)},
        compiler_params=pltpu.CompilerParams(has_side_effects=DATAFLOW),
    )(*[hbm(a) for a in list(slabs) + lands])


def pair_wait(started, after, tag):
    send_sems, recv_sems, *rest = started
    n = (len(rest) - 1) // 2
    slabs, lands = rest[:n], rest[n:2 * n]

    def body(*refs):
        g_refs, land_refs = refs[:n], refs[n:2 * n]
        send_ref, recv_ref = refs[2 * n], refs[2 * n + 1]
        x, y, c = _place()
        for i in range(n):
            hr = slabs[i].shape[1] // 2
            cp = _remote(g_refs[i].at[:, pl.ds((1 - c) * hr, hr), :], land_refs[i], send_ref.at[i], recv_ref.at[i], (x, y, 1 - c))
            cp.wait_send()
            cp.wait_recv()

    out = pl.pallas_call(
        body, name="pair_wait_" + tag,
        out_shape=tuple(pltpu.HBM(a.shape, a.dtype) for a in list(slabs) + list(lands)),
        in_specs=[HBM] * (2 * n) + [SEM, SEM, pl.BlockSpec(memory_space=pl.ANY)], out_specs=tuple([HBM] * (2 * n)),
        input_output_aliases={i: i for i in range(2 * n)},
        compiler_params=pltpu.CompilerParams(has_side_effects=DATAFLOW),
    )(*slabs, *lands, send_sems, recv_sems, after)
    return list(out[:n]), list(out[n:])


def _tile2(rows, cols):
    fits = lambda r, c: r * c * 4 <= BLOCK_BYTES
    if fits(rows, cols):
        return rows, cols
    for r in (1024, 512, 256, 128, 64):
        if rows % r == 0 and fits(r, cols):
            return r, cols
    return rows, next(cols // k for k in (2, 3, 4, 6, 8, 12, 16) if cols % (k * LANES) == 0 and fits(rows, cols // k))


def pair_add(g, p, c, name):
    _, hr, cols = p.shape
    tm, tc = _tile2(hr, cols)
    per = hr // tm

    def body(c_ref, g_ref, p_ref, o_ref):
        o_ref[...] = (g_ref[...] + p_ref[...]).astype(o_ref.dtype)

    return pl.pallas_call(
        body, name=name,
        grid_spec=pltpu.PrefetchScalarGridSpec(
            num_scalar_prefetch=1, grid=(4, per, cols // tc),
            in_specs=[pl.BlockSpec((None, tm, tc), lambda k, i, j, c_ref: (k, c_ref[0] * per + i, j)),
                      pl.BlockSpec((None, tm, tc), lambda k, i, j, c_ref: (k, i, j))],
            out_specs=pl.BlockSpec((None, tm, tc), lambda k, i, j, c_ref: (k, i, j))),
        out_shape=jax.ShapeDtypeStruct((4, hr, cols), BF16),
        compiler_params=_params(("arbitrary", "arbitrary", "arbitrary")),
    )(c.reshape(1).astype(jnp.int32), g, p)


def scatter_start(sums, tag):
    n = len(sums)

    def body(*refs):
        s_refs, land_refs = refs[:n], refs[n:2 * n]
        send_sems, recv_sems = refs[2 * n], refs[2 * n + 1]
        token = refs[-1]
        x, y, c = _place()
        k = 2 * x + y
        for i in range(n):
            for j, (px, py) in enumerate(_other_chips(x, y)):
                _remote(s_refs[i].at[2 * px + py], land_refs[i].at[k], send_sems.at[j * n + i], recv_sems.at[j * n + i],
                        (px, py, c)).start()
        token[...] = jnp.zeros_like(token)

    hbm = lambda a: pltpu.with_memory_space_constraint(a, pltpu.HBM)
    return pl.pallas_call(
        body, name="scatter_start_" + tag,
        out_shape=(pltpu.SemaphoreType.DMA((3 * n,)), pltpu.SemaphoreType.DMA((3 * n,)),
                   *[pltpu.HBM(s.shape, s.dtype) for s in sums], *[pltpu.HBM(s.shape, s.dtype) for s in sums],
                   jax.ShapeDtypeStruct((8, LANES), F32)),
        in_specs=[HBM] * (2 * n), out_specs=(SEM, SEM, *[HBM] * (2 * n), pl.BlockSpec(memory_space=pltpu.VMEM)),
        input_output_aliases={i: 2 + i for i in range(2 * n)},
        compiler_params=pltpu.CompilerParams(has_side_effects=DATAFLOW),
    )(*[hbm(s) for s in sums], *[hbm(lax.empty(s.shape, s.dtype)) for s in sums])


def scatter_wait(started, after, tag):
    send_sems, recv_sems, *rest = started
    n = (len(rest) - 1) // 2
    sums, lands = rest[:n], rest[n:2 * n]

    def body(*refs):
        s_refs, land_refs = refs[:n], refs[n:2 * n]
        send_ref, recv_ref = refs[2 * n], refs[2 * n + 1]
        x, y, c = _place()
        for i in range(n):
            for j, (px, py) in enumerate(_other_chips(x, y)):
                cp = _remote(s_refs[i].at[2 * px + py], land_refs[i].at[2 * px + py], send_ref.at[j * n + i],
                             recv_ref.at[j * n + i], (px, py, c))
                cp.wait_send()
                cp.wait_recv()

    out = pl.pallas_call(
        body, name="scatter_wait_" + tag,
        out_shape=tuple(pltpu.HBM(s.shape, s.dtype) for s in sums + lands),
        in_specs=[HBM] * (2 * n) + [SEM, SEM, pl.BlockSpec(memory_space=pl.ANY)], out_specs=tuple([HBM] * (2 * n)),
        input_output_aliases={i: i for i in range(2 * n)},
        compiler_params=pltpu.CompilerParams(has_side_effects=DATAFLOW),
    )(*sums, *lands, send_sems, recv_sems, after)
    return list(out[n:])


def sum_chips(landed, own, chip, core, name):
    _, hr, cols = landed.shape
    tm, tc = _tile2(hr, cols)
    per = hr // tm

    def body(idx_ref, l0, l1, l2, l3, own_ref, o_ref):
        mine = own_ref[...].astype(F32)
        v = [jnp.where(idx_ref[0] == k, mine, ref[...].astype(F32)) for k, ref in enumerate((l0, l1, l2, l3))]
        o_ref[...] = ((v[0] + v[1]) + v[2]) + v[3]

    slot = lambda k: pl.BlockSpec((None, tm, tc),
                                  lambda i, j, idx: (jnp.where(idx[0] == k, (k + 1) & 3, k), i, j))
    return pl.pallas_call(
        body, name=name,
        grid_spec=pltpu.PrefetchScalarGridSpec(
            num_scalar_prefetch=1, grid=(per, cols // tc),
            in_specs=[slot(0), slot(1), slot(2), slot(3),
                      pl.BlockSpec((None, tm, tc), lambda i, j, idx: (idx[0], i, j))],
            out_specs=pl.BlockSpec((tm, tc), lambda i, j, idx: (idx[1] * per + i, j))),
        out_shape=jax.ShapeDtypeStruct((2 * hr, cols), F32),
        compiler_params=_params(("arbitrary", "arbitrary")),
    )(jnp.stack([chip, core]).astype(jnp.int32), landed, landed, landed, landed, own)


def exchange_halves(bufs):
    n = len(bufs)

    def body(*refs):
        out_refs = refs[n:2 * n]
        send_sems, recv_sems = refs[2 * n:]
        x, y, c = _place()
        sends, recvs = [], []
        for i in range(n):
            hr = bufs[i].shape[0] // 2
            own = out_refs[i].at[pl.ds(c * hr, hr), :]
            other = out_refs[i].at[pl.ds((1 - c) * hr, hr), :]
            sends.append(_remote(own, own, send_sems.at[i], recv_sems.at[i], (x, y, 1 - c)))
            recvs.append(_remote(other, other, send_sems.at[i], recv_sems.at[i], (x, y, 1 - c)))
        for cp in sends:
            cp.start()
        for cp in recvs:
            cp.wait_recv()
        for cp in sends:
            cp.wait_send()

    return pl.pallas_call(
        body, name="exchange_halves", in_specs=[HBM] * n, out_specs=[HBM] * n,
        out_shape=[jax.ShapeDtypeStruct(b.shape, b.dtype) for b in bufs],
        input_output_aliases={i: i for i in range(n)},
        scratch_shapes=[pltpu.SemaphoreType.DMA((n,)), pltpu.SemaphoreType.DMA((n,))],
    )(*bufs)


def _relayout(name, arrays, in_blocks, out_blocks, out_shapes, fn):
    rows = 128
    spec = lambda blk: pl.BlockSpec(blk, (lambda i: (0, i, 0)) if len(blk) == 3 else (lambda i: (i, 0)))

    def body(*refs):
        n_in = len(arrays)
        outs = fn(*[r[...] for r in refs[:n_in]])
        for ref, val in zip(refs[n_in:], outs, strict=True):
            if isinstance(val, list):
                for k, piece in enumerate(val):
                    ref[k] = piece
            else:
                ref[...] = val

    return pl.pallas_call(
        body, name=name, grid=(D // rows,),
        in_specs=[spec(b) for b in in_blocks], out_specs=[spec(b) for b in out_blocks], out_shape=out_shapes,
        compiler_params=_params(("arbitrary",)),
    )(*arrays)


def assemble_in_proj(g):
    def fn(v):
        w = jnp.concatenate([v[k] for k in range(4)], axis=1)
        return (jnp.concatenate([w[:, :ORIG_Z], w[:, ORIG_GA:], w[:, ORIG_XBC:ORIG_DT], w[:, ORIG_Z:ORIG_XBC],
                                 w[:, ORIG_DT:ORIG_GA], jnp.zeros((w.shape[0], IN_PAD - IN_ORIG), w.dtype)], axis=1),)

    cols = g.shape[2]
    return _relayout("assemble_in_proj", [g], [(4, 128, cols)], [(128, IN_PAD)],
                     [jax.ShapeDtypeStruct((D, IN_PAD), g.dtype)], fn)[0]


def rows_exchange(a, name):
    hr = a.shape[0] // 2

    def body(a_ref, out_ref, send_sem, recv_sem):
        x, y, c = _place()
        cp = _remote(a_ref.at[pl.ds((1 - c) * hr, hr), :], out_ref, send_sem, recv_sem, (x, y, 1 - c))
        cp.start()
        cp.wait()

    return pl.pallas_call(
        body, name=name, in_specs=[HBM], out_specs=HBM,
        out_shape=jax.ShapeDtypeStruct((hr, a.shape[1]), a.dtype),
        scratch_shapes=[pltpu.SemaphoreType.DMA, pltpu.SemaphoreType.DMA],
    )(a)


def split_pair_add(dw, received, core):
    cols = IN_ORIG // 4
    rows, hr = 128, D // 2
    per = hr // rows

    def body(c_ref, own_ref, got_ref, o_ref):
        d = own_ref[...] + got_ref[...]
        w = jnp.concatenate([d[:, :COL_GA], d[:, COL_Z:COL_DT], d[:, COL_XBC:COL_Z], d[:, COL_DT:COL_DT + 32],
                             d[:, COL_GA:COL_XBC]], axis=1)
        for k in range(4):
            o_ref[k] = w[:, k * cols:(k + 1) * cols].astype(o_ref.dtype)

    return pl.pallas_call(
        body, name="split_pair_add",
        grid_spec=pltpu.PrefetchScalarGridSpec(
            num_scalar_prefetch=1, grid=(per,),
            in_specs=[pl.BlockSpec((rows, IN_PAD), lambda i, c_ref: (c_ref[0] * per + i, 0)),
                      pl.BlockSpec((rows, IN_PAD), lambda i, c_ref: (i, 0))],
            out_specs=pl.BlockSpec((4, rows, cols), lambda i, c_ref: (0, i, 0))),
        out_shape=jax.ShapeDtypeStruct((4, hr, cols), BF16),
        compiler_params=_params(("arbitrary",)),
    )(core.reshape(1).astype(jnp.int32), dw, received)


def ada_prepare(c_all, w_ada, hgrn_lb):
    def body(c_ref, w_ref, lb_ref, mod_ref, row_ref):
        mod_ref[...] = hdot(silu(c_ref[...]), w_ref[...])
        row_ref[...] = sigmoid(lb_ref[0:1, :] - lb_ref[1:2, :])

    return pl.pallas_call(
        body, name="ada_prepare",
        out_shape=[jax.ShapeDtypeStruct((8, w_ada.shape[1]), F32), jax.ShapeDtypeStruct((1, D), F32)],
        compiler_params=pltpu.CompilerParams(vmem_limit_bytes=VMEM_LIMIT),
    )(c_all, w_ada, hgrn_lb)


SMALL_SEGS = (("mod", 6 * D), ("lb", D), ("gnorm", LANES), ("conv_w", 4 * CONV_DIM), ("conv_b", CONV_DIM),
              ("dt_bias", LANES), ("a_log", LANES), ("d", B_INNER), ("ssm_norm", B_INNER),
              ("ln1_g", D), ("ln1_b", D), ("ln2_g", D), ("ln2_b", D), ("loss", LANES))
SMALL_PARAMS = ("b_ada", "hgrn_lb", "hgrn_gnorm", "ssm_conv_b", "ssm_dt_bias", "ssm_a_log", "ssm_d", "ssm_norm",
                "ln1_g", "ln1_b", "ln2_g", "ln2_b")


def finalize_small(g_all, c_all, dmod_cols, params, m, v):
    n_p = len(SMALL_PARAMS)
    offs, o = {}, 0
    for nm, width in SMALL_SEGS:
        offs[nm] = (o, width)
        o += width

    def body(*refs):
        g_ref, c_ref, dm_ref = refs[:3]
        p_refs = refs[3:3 + n_p]
        m_refs = refs[3 + n_p:3 + 2 * n_p]
        v_refs = refs[3 + 2 * n_p:3 + 3 * n_p]
        outs = refs[3 + 3 * n_p:]
        gwa_ref, gcw_ref, loss_ref = outs[:3]
        res = outs[3:]
        total = jnp.sum(g_ref[...], axis=0, keepdims=True)
        seg = lambda nm: total[:, offs[nm][0]:offs[nm][0] + offs[nm][1]]
        loss_ref[...] = seg("loss")
        gwa_ref[...] = hdot(silu(c_ref[...]), dm_ref[...], "tn")
        cw = seg("conv_w")
        for j in range(4):
            gcw_ref[j:j + 1, :] = cw[:, j * CONV_DIM:(j + 1) * CONV_DIM]
        hc = lax.broadcasted_iota(jnp.int32, (B_INNER, LANES), 0)
        hj = lax.broadcasted_iota(jnp.int32, (B_INNER, LANES), 1)
        per_head = ((hc >> 6) == hj).astype(F32)
        heads = lambda nm: hdot(jnp.broadcast_to(seg(nm), (8, B_INNER)), per_head)[0:1, 0:32]
        lbp = sigmoid(p_refs[1][0:1, :] - p_refs[1][1:2, :])
        g_row = seg("lb") * lbp * (1.0 - lbp)
        grads = {"b_ada": seg("mod"), "hgrn_gnorm": seg("gnorm"), "ssm_conv_b": seg("conv_b"),
                 "ssm_dt_bias": seg("dt_bias")[:, 0:32], "ssm_a_log": seg("a_log")[:, 0:32], "ssm_d": heads("d"),
                 "ssm_norm": seg("ssm_norm"), "ln1_g": seg("ln1_g"), "ln1_b": seg("ln1_b"),
                 "ln2_g": seg("ln2_g"), "ln2_b": seg("ln2_b")}
        for i, nm in enumerate(SMALL_PARAMS):
            g_out, d_out, m_out, v_out = res[4 * i:4 * i + 4]
            if nm == "hgrn_lb":
                for row, gv in ((0, g_row), (1, -g_row)):
                    sl = slice(row, row + 1)
                    dl, mn, vn = adamw(p_refs[i][sl, :], gv, m_refs[i][sl, :], v_refs[i][sl, :])
                    g_out[sl, :], d_out[sl, :], m_out[sl, :], v_out[sl, :] = gv, dl, mn, vn
            else:
                gv = grads[nm]
                dl, mn, vn = adamw(p_refs[i][...], gv, m_refs[i][...], v_refs[i][...])
                g_out[...], d_out[...], m_out[...], v_out[...] = gv, dl, mn, vn

    out_shape = [jax.ShapeDtypeStruct((D, dmod_cols.shape[1]), F32), jax.ShapeDtypeStruct((4, CONV_DIM), F32),
                 jax.ShapeDtypeStruct((1, LANES), F32)]
    for p in params:
        out_shape += [jax.ShapeDtypeStruct(p.shape, F32)] * 4
    return pl.pallas_call(
        body, name="finalize_small", out_shape=out_shape,
        compiler_params=pltpu.CompilerParams(vmem_limit_bytes=VMEM_LIMIT),
    )(g_all, c_all, dmod_cols, *params, *m, *v)


def adam_update(w, g, m, v, name):
    rows, cols = w.shape
    tm, tc = _tile2(rows, cols)

    def body(w_ref, g_ref, m_ref, v_ref, d_ref, mo_ref, vo_ref):
        d_ref[...], mo_ref[...], vo_ref[...] = adamw(w_ref[...], g_ref[...], m_ref[...], v_ref[...])

    spec = pl.BlockSpec((tm, tc), lambda i, j: (i, j))
    return pl.pallas_call(
        body, name=name, grid=(rows // tm, cols // tc), in_specs=[spec] * 4, out_specs=[spec] * 3,
        out_shape=[jax.ShapeDtypeStruct((rows, cols), F32)] * 3,
        compiler_params=_params(("arbitrary", "arbitrary")),
    )(w, g, m, v)


def kernel(x, c, w_ada, b_ada, w_in, hgrn_lb, hgrn_gnorm, ssm_conv_w, ssm_conv_b, ssm_dt_bias, ssm_a_log, ssm_d, ssm_norm, w_branch_a, w_branch_b, w_o, ln1_g, ln1_b, w_ffn_gate, w_ffn_up, w_ffn_down, ln2_g, ln2_b, loss_target, m_w_ada, m_b_ada, m_w_in, m_hgrn_lb, m_hgrn_gnorm, m_ssm_conv_w, m_ssm_conv_b, m_ssm_dt_bias, m_ssm_a_log, m_ssm_d, m_ssm_norm, m_w_branch_a, m_w_branch_b, m_w_o, m_ln1_g, m_ln1_b, m_w_ffn_gate, m_w_ffn_up, m_w_ffn_down, m_ln2_g, m_ln2_b, v_w_ada, v_b_ada, v_w_in, v_hgrn_lb, v_hgrn_gnorm, v_ssm_conv_w, v_ssm_conv_b, v_ssm_dt_bias, v_ssm_a_log, v_ssm_d, v_ssm_norm, v_w_branch_a, v_w_branch_b, v_w_o, v_ln1_g, v_ln1_b, v_w_ffn_gate, v_w_ffn_up, v_w_ffn_down, v_ln2_g, v_ln2_b):
    given = dict(locals())
    chip = 2 * lax.axis_index("x") + lax.axis_index("y")
    core = lax.axis_index("c")
    t = x.shape[1]

    first = gather_rows(jnp.concatenate([c, ssm_conv_w.reshape(1, CONV_DIM)], axis=1), "gather_cond").reshape(8, D + CONV_DIM)
    c_all = first[:, :D]
    conv_w = first[0::2, D:].reshape(4, 4, CONV_DIM // 4).transpose(1, 0, 2).reshape(4, CONV_DIM)
    mod_part, lb_row = ada_prepare(c_all, w_ada[0], hgrn_lb)
    mod_cols = w_ada.shape[2]
    mod_row = exchange_rows(mod_part.reshape(8, 1, mod_cols), "exchange_mod").reshape(1, 6 * D) + b_ada
    mod = tuple(mod_row[:, i * D:(i + 1) * D] for i in range(6))

    local = {nm: given[nm][0].T if nm in TRANSPOSED else given[nm][0] for nm in SHARDED}
    shards = [local[nm].astype(BF16) for nm in SHARDED]
    n_w = len(SHARDED)
    send_in, recv_in, send_rest, recv_rest, *flying = gather_start(shards, mod_row)
    sent, lands = flying[:n_w], flying[n_w:2 * n_w]
    with_own = lambda land, shard: lax.dynamic_update_slice(land, shard[None], (chip, 0, 0))

    class Weights:
        def input_projection(self, after):
            land = gather_wait(send_in, recv_in, sent[:1], lands[:1], after, "in")
            (land,) = forward_wait(forward_start(land, "in"), after, "in")
            return assemble_in_proj(with_own(land, shards[0]))

        def start_rest(self, after):
            self.started = forward_start(gather_wait(send_rest, recv_rest, sent[1:], lands[1:], after, "rest"), "rest")
            return self.started[-1]

        def rest(self, after):
            got = {nm: with_own(land, s) for nm, land, s in zip(SHARDED[1:], forward_wait(self.started, after, "rest"), shards[1:], strict=True)}
            whole = lambda nm: got[nm].reshape(4 * got[nm].shape[1], got[nm].shape[2])
            return (whole("w_branch_a"), whole("w_branch_b"), whole("w_o"),
                    jnp.concatenate([whole("w_ffn_gate"), whole("w_ffn_up")], axis=0), whole("w_ffn_down"))

    wts = Weights()

    per_head = lambda p: jnp.pad(p, ((0, 0), (0, LANES - p.shape[1])))
    small = (lb_row, hgrn_gnorm, conv_w, ssm_conv_b, per_head(ssm_dt_bias), per_head(ssm_a_log),
             jnp.repeat(ssm_d[0], B_INNER // 32)[None], ssm_norm, ln1_g, ln1_b, ln2_g, ln2_b)
    by_rows = lambda g: g.reshape(4, g.shape[0] // 4, g.shape[1])
    travelling = {}

    def start_early(dws):
        dw_a, dw_b, dw_o, dw_gu, dw_d = dws
        d_gate, d_up = by_rows(dw_gu[:D_FF]), by_rows(dw_gu[D_FF:])
        travelling["pair"] = pair_start([by_rows(dw_a), by_rows(dw_b), by_rows(dw_o), d_gate, d_up, by_rows(dw_d)], "early")
        return travelling["pair"][-1]

    def between_scans(after):
        slabs, received = pair_wait(travelling["pair"], after, "early")
        travelling["pairs"] = [pair_add(s, r, core, "pair_add_" + nm) for nm, s, r in zip(SHARDED[1:], slabs, received, strict=True)]
        travelling["started"] = scatter_start(travelling["pairs"], "early")
        return travelling["started"][-1]

    def finish_early(after):
        travelling["landed"] = scatter_wait(travelling["started"], after, "early")

    def start_last(dw_in):
        travelling["pairs_in"] = [split_pair_add(dw_in, rows_exchange(dw_in, "pair_exchange_last"), core)]
        travelling["started_in"] = scatter_start(travelling["pairs_in"], "last")
        return travelling["started_in"][-1]

    loss, grad_x, d_mod, d_wts, d_small = local_step(x[0], loss_target[0], mod, wts, small,
                                                     start_early, between_scans, finish_early, start_last)

    d_lb, d_gn, d_cw, d_cb, d_dtb, d_alog, d_dsk, d_nw, d_l1g, d_l1b, d_l2g, d_l2b = d_small
    row = jnp.concatenate(list(d_mod) + [d_lb, d_gn, d_cw.reshape(1, 4 * CONV_DIM), d_cb, d_dtb, d_alog, d_dsk, d_nw,
                                          d_l1g, d_l1b, d_l2g, d_l2b, jnp.pad(loss, ((0, 0), (0, LANES - 1)))], axis=1)
    g_all = gather_rows(row, "gather_small_grads").reshape(8, row.shape[1])
    dmod_cols = lax.dynamic_slice_in_dim(g_all, chip * mod_cols, mod_cols, axis=1)
    fin = finalize_small(g_all, c_all, dmod_cols, [given[n] for n in SMALL_PARAMS],
                         [given["m_" + n] for n in SMALL_PARAMS], [given["v_" + n] for n in SMALL_PARAMS])
    grads, deltas, new_m, new_v = {}, {}, {}, {}
    grads["w_ada"] = fin[0][None]
    grads["ssm_conv_w"] = lax.dynamic_slice_in_dim(fin[1], chip * (CONV_DIM // 4), CONV_DIM // 4, axis=1)[None]
    for i, nm in enumerate(SMALL_PARAMS):
        grads[nm], deltas[nm], new_m[nm], new_v[nm] = fin[3 + 4 * i:7 + 4 * i]

    pairs = travelling["pairs_in"] + travelling["pairs"]
    landed = scatter_wait(travelling["started_in"], fin[3], "last") + travelling["landed"]
    halves = [sum_chips(r, p, chip, core, "sum_chips_" + nm) for nm, r, p in zip(SHARDED, landed, pairs, strict=True)]
    reduced = dict(zip(SHARDED, exchange_halves(halves), strict=True))
    reduced["w_ada"], reduced["ssm_conv_w"] = grads["w_ada"][0], grads["ssm_conv_w"][0]
    reduced["w_in"] = reduced["w_in"].T
    for nm in ("w_ada", "ssm_conv_w") + SHARDED:
        flipped = nm in TRANSPOSED or nm == "w_in"
        work = (lambda a: a[0].T) if flipped else (lambda a: a[0])
        back = (lambda a: a.T[None]) if flipped else (lambda a: a[None])
        d_, m_, v_ = adam_update(work(given[nm]), reduced[nm], work(given["m_" + nm]), work(given["v_" + nm]), "adam_" + nm)
        grads[nm], deltas[nm], new_m[nm], new_v[nm] = back(reduced[nm]), back(d_), back(m_), back(v_)

    names = ("w_ada", "b_ada", "w_in", "hgrn_lb", "hgrn_gnorm", "ssm_conv_w", "ssm_conv_b", "ssm_dt_bias", "ssm_a_log",
             "ssm_d", "ssm_norm", "w_branch_a", "w_branch_b", "w_o", "ln1_g", "ln1_b", "w_ffn_gate", "w_ffn_up",
             "w_ffn_down", "ln2_g", "ln2_b")
    return (fin[2][0, 0], grad_x[None], *[grads[n] for n in names], *[deltas[n] for n in names],
            *[new_m[n] for n in names], *[new_v[n] for n in names])
```

```python
import functools

import jax
import jax.numpy as jnp
from jax import lax
from jax.experimental import pallas as pl
from jax.experimental.pallas import tpu as pltpu

F32, BF16 = jnp.float32, jnp.bfloat16
HI = lax.Precision.HIGHEST
MESH = pl.DeviceIdType.MESH

D = 1024
CHUNK = 64
LANES = 128
N_HEADS_A = 8
N_GROUPS_B = 4
B_INNER = 2048
CONV_DIM = 3072
D_FF = 2816
ALPHA = 2.0 ** 0.25
LN_EPS = 1e-5
RMS_EPS = 1e-6
ADAM_LR, ADAM_B1, ADAM_B2, ADAM_EPS, ADAM_WD, ADAM_STEP = 0.001, 0.9, 0.999, 1e-08, 0.01, 10

IN_ORIG = 11296
IN_PAD = 11520
COL_GA, COL_GB, COL_XBC, COL_Z, COL_DT = 4096, 5120, 6144, 9216, 11264
ORIG_Z, ORIG_XBC, ORIG_DT, ORIG_GA = 4096, 6144, 9216, 9248

SHARDED = ("w_in", "w_branch_a", "w_branch_b", "w_o", "w_ffn_gate", "w_ffn_up", "w_ffn_down")
TRANSPOSED = ("w_ffn_gate", "w_ffn_up")
VMEM_LIMIT = 56 * 1024 * 1024
BLOCK_BYTES = 2 * 1024 * 1024
_DIMS = {"nn": (((1,), (0,)), ((), ())), "nt": (((1,), (1,)), ((), ())), "tn": (((0,), (0,)), ((), ()))}


def _bd(a, b, mode):
    return lax.dot_general(a.astype(BF16), b.astype(BF16), _DIMS[mode], preferred_element_type=F32)


@functools.partial(jax.custom_vjp, nondiff_argnums=(2,))
def bdot(a, b, mode):
    return _bd(a, b, mode)


def _bdot_fwd(a, b, mode):
    return _bd(a, b, mode), (a, b)


def _bdot_bwd(mode, res, g):
    a, b = res
    if mode == "nn":
        return _bd(g, b, "nt"), _bd(a, g, "tn")
    if mode == "nt":
        return _bd(g, b, "nn"), _bd(g, a, "tn")
    return _bd(b, g, "nt"), _bd(a, g, "nn")


bdot.defvjp(_bdot_fwd, _bdot_bwd)


def hdot(a, b, mode="nn"):
    return lax.dot_general(a, b, _DIMS[mode], precision=HI, preferred_element_type=F32)


def _raw(a, b, mode):
    return lax.dot_general(a, b, _DIMS[mode], preferred_element_type=F32)


def _split(x, n):
    parts, rest = [], x
    for _ in range(n):
        p = rest.astype(BF16)
        parts.append(p)
        rest = rest - p.astype(F32)
    return parts


def _od(a, b, mode, exact):
    if exact == 1:
        e = b.astype(BF16)
        p = _split(a, 3)
        return (_raw(p[2], e, mode) + _raw(p[1], e, mode)) + _raw(p[0], e, mode)
    e = a.astype(BF16)
    p = _split(b, 3)
    return (_raw(e, p[2], mode) + _raw(e, p[1], mode)) + _raw(e, p[0], mode)


@functools.partial(jax.custom_vjp, nondiff_argnums=(2, 3))
def odot(a, b, mode, exact):
    return _od(a, b, mode, exact)


def _odot_fwd(a, b, mode, exact):
    return _od(a, b, mode, exact), (a, b)


def _odot_bwd(mode, exact, res, g):
    a, b = res
    if exact == 1:
        da = {"nn": lambda: _od(g, b, "nt", 1), "nt": lambda: _od(g, b, "nn", 1), "tn": lambda: _od(b, g, "nt", 0)}[mode]()
        return da, jnp.zeros_like(b)
    db = {"nn": lambda: _od(a, g, "tn", 0), "nt": lambda: _od(g, a, "tn", 1), "tn": lambda: _od(a, g, "nn", 0)}[mode]()
    return jnp.zeros_like(a), db


odot.defvjp(_odot_fwd, _odot_bwd)


_BDIMS = {"bnn": (((2,), (1,)), ((0,), (0,))), "bnt": (((2,), (2,)), ((0,), (0,))), "btn": (((1,), (1,)), ((0,), (0,)))}


def _braw(a, b, mode):
    return lax.dot_general(a, b, _BDIMS[mode], preferred_element_type=F32)


def _bdb(a, b, mode):
    return _braw(a.astype(BF16), b.astype(BF16), mode)


def _d3b(a, b, mode):
    ah, al = _split(a, 2)
    bh, bl = _split(b, 2)
    return _braw(ah, bh, mode) + (_braw(ah, bl, mode) + _braw(al, bh, mode))


def _batched_bwd(f):
    def bwd(mode, res, g):
        a, b = res
        if mode == "bnn":
            return f(g, b, "bnt"), f(a, g, "btn")
        if mode == "bnt":
            return f(g, b, "bnn"), f(g, a, "btn")
        return f(b, g, "bnt"), f(a, g, "bnn")
    return bwd


@functools.partial(jax.custom_vjp, nondiff_argnums=(2,))
def bdot_b(a, b, mode):
    return _bdb(a, b, mode)


bdot_b.defvjp(lambda a, b, mode: (_bdb(a, b, mode), (a, b)), _batched_bwd(_bdb))


@functools.partial(jax.custom_vjp, nondiff_argnums=(2,))
def dot3_b(a, b, mode):
    return _d3b(a, b, mode)


dot3_b.defvjp(lambda a, b, mode: (_d3b(a, b, mode), (a, b)), _batched_bwd(_d3b))


def _cum(tril3, x, mode):
    e = tril3.astype(BF16)
    p = _split(x, 3)
    return (_braw(e, p[2], mode) + _braw(e, p[1], mode)) + _braw(e, p[0], mode)


@jax.custom_vjp
def chunk_cumsum(tril3, x):
    return _cum(tril3, x, "bnn")


chunk_cumsum.defvjp(lambda t, x: (_cum(t, x, "bnn"), t), lambda t, g: (jnp.zeros_like(t), _cum(t, g, "btn")))


def _unstack(axis, n):
    @jax.custom_vjp
    def un(x):
        return tuple(lax.index_in_dim(x, i, axis, keepdims=False) for i in range(n))

    un.defvjp(lambda x: (un(x), None), lambda _, g: (jnp.stack(g, axis=axis),))
    return un


def _split_last(n, w):
    @jax.custom_vjp
    def sp(x):
        return tuple(x[..., i * w:(i + 1) * w] for i in range(n))

    sp.defvjp(lambda x: (sp(x), None), lambda _, g: (jnp.concatenate(g, axis=-1),))
    return sp


def sigmoid(x):
    return 1.0 / (1.0 + jnp.exp(-x))


def silu(x):
    return x * sigmoid(x)


def softplus(x):
    return jnp.maximum(x, 0.0) + jnp.log1p(jnp.exp(jnp.minimum(x, -x)))


def _ln(x):
    mu = jnp.mean(x, axis=-1, keepdims=True)
    xc = x - mu
    return xc * lax.rsqrt(jnp.mean(xc * xc, axis=-1, keepdims=True) + LN_EPS)


def _tril64():
    r = lax.broadcasted_iota(jnp.int32, (CHUNK, CHUNK), 0)
    c = lax.broadcasted_iota(jnp.int32, (CHUNK, CHUNK), 1)
    return (r >= c).astype(F32)


def hgrn_block(q, fl, iv, gr, st, lb, gn):
    tb = q.shape[0]
    nc = tb // CHUNK
    nh = N_HEADS_A
    heads = _split_last(nh, LANES)
    to4 = lambda a: jnp.stack(heads(a), axis=0).reshape(nh, nc, CHUNK, LANES)
    flat = lambda a: a.reshape(nh * nc, CHUNK, LANES)
    f = lb + (1.0 - lb) * sigmoid(fl)
    gl4, k4, qf4, v4, gr4 = to4(jnp.log(f)), to4(1.0 - f), to4(silu(q) * (128 ** -0.5)), to4(iv), to4(gr)
    tril = _tril64()
    b4 = chunk_cumsum(jnp.broadcast_to(tril[None], (nh * nc, CHUNK, CHUNK)), flat(gl4)).reshape(gl4.shape)
    blast = jnp.sum(gl4, axis=2, keepdims=True)
    ref = lax.stop_gradient(0.5 * blast)
    sc = dot3_b(flat(qf4 * jnp.exp(b4 - ref)), flat(k4 * jnp.exp(ref - b4)), "bnt") * tril
    o_intra = bdot_b(sc, flat(v4), "bnn").reshape(gl4.shape)
    chunks = _unstack(1, nc)
    qe, v_c, kd, dec = chunks(qf4 * jnp.exp(b4)), chunks(v4), chunks(k4 * jnp.exp(blast - b4)), chunks(jnp.exp(blast))
    o_inter = []
    for c in range(nc):
        o_inter.append(bdot_b(qe[c], st, "bnt"))
        st = st * dec[c] + bdot_b(v_c[c], kd[c], "btn")
    o = o_intra + jnp.stack(o_inter, axis=1)
    on = o * lax.rsqrt(jnp.mean(o * o, axis=-1, keepdims=True) + RMS_EPS) * gn
    out = (on * silu(gr4)).reshape(nh, tb, LANES)
    return jnp.concatenate(_unstack(0, nh)(out), axis=1), st


def ssd_consts(g):
    i32 = jnp.int32
    ej = lax.broadcasted_iota(i32, (LANES, 512), 0)
    ec = lax.broadcasted_iota(i32, (LANES, 512), 1)
    expand = (ej == g * 8 + (ec >> 6)).astype(F32)
    ts = lax.broadcasted_iota(i32, (CHUNK, 512), 0)
    tc = lax.broadcasted_iota(i32, (CHUNK, 512), 1)
    itile = (ts == (tc & 63)).astype(F32)
    maskall = ts >= (tc & 63)
    br = lax.broadcasted_iota(i32, (256, 256), 0)
    bc = lax.broadcasted_iota(i32, (256, 256), 1)
    blockmask = ((br >> 6) == (bc >> 6)).astype(F32)
    return expand, itile, maskall, blockmask, _tril64()


def ssd_block(x, bm, cm, dt, z, st, dtb, alog, dsk, nw, cs):
    expand, itile, maskall, blockmask, tril = cs
    tb = x.shape[0]
    nc = tb // CHUNK
    delta_heads = softplus(dt + dtb)
    delta = odot(delta_heads, expand, "nn", 1)
    a = odot(-jnp.exp(alog) * delta_heads, expand, "nn", 1)
    xdt = x * delta
    by_chunk = lambda v: v.reshape(nc, CHUNK, v.shape[-1])
    a3, xdt3, bm3, cm3 = by_chunk(a), by_chunk(xdt), by_chunk(bm), by_chunk(cm)
    acum3 = chunk_cumsum(jnp.broadcast_to(tril[None], (nc, CHUNK, CHUNK)), a3)
    alast3 = jnp.sum(a3, axis=1, keepdims=True)
    cb3 = bdot_b(cm3, jnp.concatenate([bm3] * 8, axis=1), "bnt")
    arow3 = jnp.sum(acum3 * itile, axis=1, keepdims=True)
    dec3 = jnp.exp(jnp.where(maskall, acum3 - arow3, -1e30))
    halves = _split_last(2, 256)
    intra = [bdot_b(m, jnp.concatenate([xh] * 4, axis=1) * blockmask, "bnn")
             for m, xh in zip(halves(cb3 * dec3), halves(xdt3))]
    chunks = _unstack(0, nc)
    cm_c, bm_c, xw_c, dec_c = chunks(cm3), chunks(bm3), chunks(xdt3 * jnp.exp(alast3 - acum3)), chunks(jnp.exp(alast3))
    inter = []
    for c in range(nc):
        inter.append(bdot(cm_c[c], st, "nn"))
        st = st * dec_c[c] + bdot(bm_c[c], xw_c[c], "tn")
    st_new = st
    y = (jnp.concatenate(intra, axis=-1) + jnp.stack(inter, axis=0) * jnp.exp(acum3)).reshape(tb, 512)
    yz = (y + x * dsk) * silu(z)
    return yz * lax.rsqrt(jnp.mean(yz * yz, axis=-1, keepdims=True) + RMS_EPS) * nw, st_new


def adamw(w, g, m, v):
    m = ADAM_B1 * m + (1.0 - ADAM_B1) * g
    v = ADAM_B2 * v + (1.0 - ADAM_B2) * jnp.square(g)
    m_hat = m / (1.0 - ADAM_B1 ** ADAM_STEP)
    v_hat = v / (1.0 - ADAM_B2 ** ADAM_STEP)
    return -ADAM_LR * (m_hat / (jnp.sqrt(v_hat) + ADAM_EPS) + ADAM_WD * w), m, v


def _pick(n, cands):
    for c in cands:
        if n % c == 0:
            return c
    return n


def _params(sem):
    return pltpu.CompilerParams(dimension_semantics=sem, vmem_limit_bytes=VMEM_LIMIT)


MATMUL_VMEM_BUDGET = 50 * 1024 * 1024


def matmul(a, b, mode, out_dtype, name, after=None):
    if mode == "nn":
        (m, k), n = a.shape, b.shape[1]
    elif mode == "nt":
        (m, k), n = a.shape, b.shape[0]
    else:
        (k, m), n = a.shape, b.shape[1]
    tk = _pick(k, (2304, 2048, 1408, 1024, 768, 512, 256, 128))
    nk = k // tk
    a_bytes, b_bytes, out_bytes = a.dtype.itemsize, b.dtype.itemsize, jnp.dtype(out_dtype).itemsize

    def vmem(tm_, tn_):
        blocks = 2 * (tm_ * tk * a_bytes + tk * tn_ * b_bytes + tm_ * tn_ * out_bytes)
        return blocks + (tm_ * tn_ * 4 if nk > 1 else 0)

    def traffic(tm_, tn_):
        return (m // tm_) * k * n * b_bytes + (n // tn_ if nk > 1 else 1) * m * k * a_bytes

    sizes = (2304, 2048, 1920, 1408, 1024, 768, 512, 256, 128)
    tiles = [(tm_, tn_) for tm_ in sizes if m % tm_ == 0 for tn_ in sizes if n % tn_ == 0
             if vmem(tm_, tn_) <= MATMUL_VMEM_BUDGET] or [(m, n)]
    tm, tn = min(tiles, key=lambda t: (traffic(*t), -t[0] * t[1]))
    a_spec = pl.BlockSpec((tk, tm), lambda i, j, kk: (kk, i)) if mode == "tn" else pl.BlockSpec((tm, tk), lambda i, j, kk: (i, kk))
    b_spec = pl.BlockSpec((tn, tk), lambda i, j, kk: (j, kk)) if mode == "nt" else pl.BlockSpec((tk, tn), lambda i, j, kk: (kk, j))

    order = [] if after is None else [after]

    def body(a_ref, b_ref, *rest):
        o_ref, *acc = rest[len(order):]
        part = _bd(a_ref[...], b_ref[...], mode)
        if nk == 1:
            o_ref[...] = part.astype(o_ref.dtype)
            return
        acc_ref, = acc
        kk = pl.program_id(2)

        @pl.when(kk == 0)
        def _():
            acc_ref[...] = part

        @pl.when(jnp.logical_and(kk > 0, kk < nk - 1))
        def _():
            acc_ref[...] += part

        @pl.when(kk == nk - 1)
        def _():
            o_ref[...] = (acc_ref[...] + part).astype(o_ref.dtype)

    return pl.pallas_call(
        body, name=name, grid=(m // tm, n // tn, nk),
        in_specs=[a_spec, b_spec] + [pl.BlockSpec(memory_space=pl.ANY) for _ in order],
        out_specs=pl.BlockSpec((tm, tn), lambda i, j, kk: (i, j)),
        out_shape=jax.ShapeDtypeStruct((m, n), out_dtype),
        scratch_shapes=[pltpu.VMEM((tm, tn), F32)] if nk > 1 else [],
        compiler_params=_params(("parallel", "parallel", "arbitrary")),
    )(a, b, *order)


def rowwise(name, fn, rows, consts, out_rows, out_accs=(), tm_max=256, into=None, new_wide=None):
    t = rows[0][0].shape[0]
    tm = _pick(t, (tm_max, 128, 64, 32, 16, 8))
    n_r, n_c, n_o = len(rows), len(consts), len(out_rows)
    n_alias = 0 if into is None else 1

    def body(*refs):
        r_in = [r[...] for r in refs[:n_r]]
        c_in = [r[...] for r in refs[n_r:n_r + n_c]]
        refs = refs[:n_r + n_c] + refs[n_r + n_c + n_alias:]
        o_refs = refs[n_r + n_c:n_r + n_c + n_o]
        a_refs = refs[n_r + n_c + n_o:]
        ro, ao = fn(r_in, c_in)
        for ref, val in zip(o_refs, ro, strict=True):
            ref[...] = val.astype(ref.dtype)
        if a_refs:
            @pl.when(pl.program_id(0) == 0)
            def _():
                for ref in a_refs:
                    ref[...] = jnp.zeros_like(ref)

            for ref, val in zip(a_refs, ao, strict=True):
                ref[...] += val

    in_specs = [pl.BlockSpec((tm, w), functools.partial(lambda i, cb: (i, cb), cb=cb)) for _, w, cb in rows]
    in_specs += [pl.BlockSpec(c.shape, lambda i: (0, 0)) for c in consts]
    out_specs = [pl.BlockSpec((tm, w), lambda i: (i, 0)) for w, _ in out_rows]
    out_specs += [pl.BlockSpec(s, lambda i: (0, 0)) for s in out_accs]
    out_shape = [jax.ShapeDtypeStruct((t, w), dt) for w, dt in out_rows]
    out_shape += [jax.ShapeDtypeStruct(s, F32) for s in out_accs]
    operands = [r[0] for r in rows] + list(consts)
    aliases = {}
    if into is not None:
        target, cb = into
        in_specs.append(pl.BlockSpec(memory_space=pl.ANY))
        operands.append(target)
        out_specs[0] = pl.BlockSpec((tm, out_rows[0][0]), lambda i: (i, cb))
        out_shape[0] = jax.ShapeDtypeStruct(target.shape, target.dtype)
        aliases = {len(operands) - 1: 0}
    if new_wide is not None:
        width, cb = new_wide
        out_specs[0] = pl.BlockSpec((tm, out_rows[0][0]), lambda i: (i, cb))
        out_shape[0] = jax.ShapeDtypeStruct((t, width), out_rows[0][1])
    return pl.pallas_call(
        body, name=name, grid=(t // tm,), in_specs=in_specs, out_specs=out_specs, out_shape=out_shape,
        input_output_aliases=aliases, compiler_params=_params(("arbitrary",)),
    )(*operands)


def _full(a):
    return (a, a.shape[1], 0)


def _time_block(t):
    return _pick(t, (256, 128, 64))


def _quarters(ref):
    return [ref[:, seg * D:(seg + 1) * D] for seg in range(4)]


def hgrn_forward(proj, lb, gn):
    t = proj.shape[0]
    tb = _time_block(t)
    nb = t // tb

    def body(qfig_ref, lb_ref, gn_ref, o_ref, st_ref, state):
        @pl.when(pl.program_id(0) == 0)
        def _():
            state[...] = jnp.zeros_like(state)

        st = state[...]
        st_ref[...] = st
        out, st_new = hgrn_block(*_quarters(qfig_ref), st, lb_ref[...], gn_ref[...])
        o_ref[...] = out.astype(o_ref.dtype)
        state[...] = st_new

    return pl.pallas_call(
        body, name="hgrn_forward", grid=(nb,),
        in_specs=[pl.BlockSpec((tb, 4 * D), lambda j: (j, 0)),
                  pl.BlockSpec((1, D), lambda j: (0, 0)), pl.BlockSpec((1, LANES), lambda j: (0, 0))],
        out_specs=[pl.BlockSpec((tb, D), lambda j: (j, 0)),
                   pl.BlockSpec((None, N_HEADS_A, LANES, LANES), lambda j: (j, 0, 0, 0))],
        out_shape=[jax.ShapeDtypeStruct((t, D), BF16),
                   jax.ShapeDtypeStruct((nb, N_HEADS_A, LANES, LANES), F32)],
        scratch_shapes=[pltpu.VMEM((N_HEADS_A, LANES, LANES), F32)],
        compiler_params=_params(("arbitrary",)),
    )(proj, lb, gn)


def hgrn_backward(proj, states, d_out, lb, gn, d_proj):
    t = proj.shape[0]
    tb = _time_block(t)
    nb = t // tb

    def body(qfig_ref, st_ref, do_ref, lb_ref, gn_ref, _, dqfig_ref, dlb_ref, dgn_ref, d_state):
        @pl.when(pl.program_id(0) == 0)
        def _():
            d_state[...] = jnp.zeros_like(d_state)
            dlb_ref[...] = jnp.zeros_like(dlb_ref)
            dgn_ref[...] = jnp.zeros_like(dgn_ref)

        _, vjp = jax.vjp(hgrn_block, *_quarters(qfig_ref), st_ref[...], lb_ref[...], gn_ref[...])
        dq, df, di, dg, dst, dlb, dgn = vjp((do_ref[...], d_state[...]))
        for seg, val in enumerate((dq, df, di, dg)):
            dqfig_ref[:, seg * D:(seg + 1) * D] = val.astype(dqfig_ref.dtype)
        d_state[...] = dst
        dlb_ref[...] += dlb
        dgn_ref[...] += dgn

    rev = lambda j: nb - 1 - j
    return pl.pallas_call(
        body, name="hgrn_backward", grid=(nb,),
        in_specs=[pl.BlockSpec((tb, 4 * D), lambda j: (rev(j), 0)),
                  pl.BlockSpec((None, N_HEADS_A, LANES, LANES), lambda j: (rev(j), 0, 0, 0)),
                  pl.BlockSpec((tb, D), lambda j: (rev(j), 0)),
                  pl.BlockSpec((1, D), lambda j: (0, 0)), pl.BlockSpec((1, LANES), lambda j: (0, 0)),
                  pl.BlockSpec(memory_space=pl.ANY)],
        out_specs=[pl.BlockSpec((tb, 4 * D), lambda j: (rev(j), 0)),
                   pl.BlockSpec((1, D), lambda j: (0, 0)), pl.BlockSpec((1, LANES), lambda j: (0, 0))],
        out_shape=[jax.ShapeDtypeStruct(d_proj.shape, d_proj.dtype), jax.ShapeDtypeStruct((1, D), F32),
                   jax.ShapeDtypeStruct((1, LANES), F32)],
        input_output_aliases={5: 0},
        scratch_shapes=[pltpu.VMEM((N_HEADS_A, LANES, LANES), F32)],
        compiler_params=_params(("arbitrary",)),
    )(proj, states, d_out, lb, gn, d_proj)


def _ssd_in_specs(tb, tmap):
    return [pl.BlockSpec((tb, 512), lambda g, j: (tmap(j), g)),
            pl.BlockSpec((tb, LANES), lambda g, j: (tmap(j), 16 + g)),
            pl.BlockSpec((tb, LANES), lambda g, j: (tmap(j), 20 + g)),
            pl.BlockSpec((tb, LANES), lambda g, j: (tmap(j), COL_DT // LANES)),
            pl.BlockSpec((tb, 512), lambda g, j: (tmap(j), COL_Z // 512 + g))]


def ssd_forward(xc, proj, dtb, alog, dsk, nw):
    t = proj.shape[0]
    tb = _time_block(t)
    nb = t // tb

    def body(x_ref, b_ref, c_ref, dt_ref, z_ref, dtb_ref, alog_ref, dsk_ref, nw_ref, o_ref, st_ref, state):
        @pl.when(pl.program_id(1) == 0)
        def _():
            state[...] = jnp.zeros_like(state)

        st = state[...]
        st_ref[...] = st
        out, st_new = ssd_block(x_ref[...], b_ref[...], c_ref[...], dt_ref[...], z_ref[...], st,
                                dtb_ref[...], alog_ref[...], dsk_ref[...], nw_ref[...], ssd_consts(pl.program_id(0)))
        o_ref[...] = out.astype(o_ref.dtype)
        state[...] = st_new

    vec = pl.BlockSpec((1, 512), lambda g, j: (0, g))
    heads = pl.BlockSpec((1, LANES), lambda g, j: (0, 0))
    return pl.pallas_call(
        body, name="ssd_forward", grid=(N_GROUPS_B, nb),
        in_specs=_ssd_in_specs(tb, lambda j: j) + [heads, heads, vec, vec],
        out_specs=[pl.BlockSpec((tb, 512), lambda g, j: (j, g)),
                   pl.BlockSpec((None, None, LANES, 512), lambda g, j: (j, g, 0, 0))],
        out_shape=[jax.ShapeDtypeStruct((t, B_INNER), BF16),
                   jax.ShapeDtypeStruct((nb, N_GROUPS_B, LANES, 512), F32)],
        scratch_shapes=[pltpu.VMEM((LANES, 512), F32)],
        compiler_params=_params(("arbitrary", "arbitrary")),
    )(xc, xc, xc, proj, proj, dtb, alog, dsk, nw)


def ssd_backward(xc, proj, states, d_out, dtb, alog, dsk, nw, d_proj):
    t = proj.shape[0]
    tb = _time_block(t)
    nb = t // tb
    rev = lambda j: nb - 1 - j

    def body(x_ref, b_ref, c_ref, dt_ref, z_ref, st_ref, do_ref, dtb_ref, alog_ref, dsk_ref, nw_ref, _,
             dx_ref, db_ref, dc_ref, ddt_ref, dz_ref, ddtb_ref, dalog_ref, ddsk_ref, dnw_ref, d_state):
        accs = (ddtb_ref, dalog_ref, ddsk_ref, dnw_ref)

        @pl.when(pl.program_id(1) == 0)
        def _():
            d_state[...] = jnp.zeros_like(d_state)
            for ref in accs:
                ref[...] = jnp.zeros_like(ref)

        cs = ssd_consts(pl.program_id(0))
        fn = lambda *a: ssd_block(*a, cs)
        _, vjp = jax.vjp(fn, x_ref[...], b_ref[...], c_ref[...], dt_ref[...], z_ref[...], st_ref[...],
                         dtb_ref[...], alog_ref[...], dsk_ref[...], nw_ref[...])
        dx, db, dc, ddt, dz, dst, *dpar = vjp((do_ref[...], d_state[...]))
        dx_ref[...] = dx
        db_ref[...] = db
        dc_ref[...] = dc
        ddt_ref[...] = ddt
        dz_ref[...] = dz.astype(dz_ref.dtype)
        d_state[...] = dst
        for ref, val in zip(accs, dpar, strict=True):
            ref[...] += val

    vec = pl.BlockSpec((1, 512), lambda g, j: (0, g))
    heads = pl.BlockSpec((1, LANES), lambda g, j: (0, 0))
    acc = pl.BlockSpec((None, 1, 512), lambda g, j: (g, 0, 0))
    acc_heads = pl.BlockSpec((None, 1, LANES), lambda g, j: (g, 0, 0))
    return pl.pallas_call(
        body, name="ssd_backward", grid=(N_GROUPS_B, nb),
        in_specs=_ssd_in_specs(tb, rev)
        + [pl.BlockSpec((None, None, LANES, 512), lambda g, j: (rev(j), g, 0, 0)),
           pl.BlockSpec((tb, 512), lambda g, j: (rev(j), g))] + [heads, heads, vec, vec] + [pl.BlockSpec(memory_space=pl.ANY)],
        out_specs=[pl.BlockSpec((tb, 512), lambda g, j: (rev(j), g)),
                   pl.BlockSpec((tb, LANES), lambda g, j: (rev(j), g)),
                   pl.BlockSpec((tb, LANES), lambda g, j: (rev(j), g)),
                   pl.BlockSpec((None, tb, LANES), lambda g, j: (g, rev(j), 0)),
                   pl.BlockSpec((tb, 512), lambda g, j: (rev(j), COL_Z // 512 + g)), acc_heads, acc_heads, acc, acc],
        out_shape=[jax.ShapeDtypeStruct((t, B_INNER), F32), jax.ShapeDtypeStruct((t, 512), F32),
                   jax.ShapeDtypeStruct((t, 512), F32), jax.ShapeDtypeStruct((N_GROUPS_B, t, LANES), F32),
                   jax.ShapeDtypeStruct(d_proj.shape, d_proj.dtype)]
        + [jax.ShapeDtypeStruct((N_GROUPS_B, 1, LANES), F32)] * 2 + [jax.ShapeDtypeStruct((N_GROUPS_B, 1, 512), F32)] * 2,
        input_output_aliases={11: 4},
        scratch_shapes=[pltpu.VMEM((LANES, 512), F32)],
        compiler_params=_params(("arbitrary", "arbitrary")),
    )(xc, xc, xc, proj, proj, states, d_out, dtb, alog, dsk, nw, d_proj)


CONV_HALO = 8


def _shift_down(halo_then_tile, s, tm):
    if s == 0:
        return halo_then_tile[CONV_HALO:CONV_HALO + tm]
    return pltpu.roll(halo_then_tile, s, 0)[CONV_HALO:CONV_HALO + tm]


def _conv_pre(cur, prev, w, b, tm):
    stacked = jnp.concatenate([prev, cur], axis=0)
    taps = [_shift_down(stacked, 3 - j, tm) for j in range(4)]
    pre = b + taps[0] * w[0:1] + taps[1] * w[1:2] + taps[2] * w[2:3] + taps[3] * w[3:4]
    return pre, taps


def _conv_specs(t, tm):
    per = tm // CONV_HALO
    cur = pl.BlockSpec((tm, CONV_DIM), lambda i: (i, COL_XBC // CONV_DIM))
    prev = pl.BlockSpec((CONV_HALO, CONV_DIM), lambda i: (jnp.maximum(i * per - 1, 0), COL_XBC // CONV_DIM))
    return cur, prev


def conv_forward(proj, w, b):
    t = proj.shape[0]
    tm = _pick(t, (256, 128, 64))

    def body(cur_ref, prev_ref, w_ref, b_ref, o_ref):
        prev = jnp.where(pl.program_id(0) == 0, 0.0, prev_ref[...])
        pre, _ = _conv_pre(cur_ref[...], prev, w_ref[...], b_ref[...], tm)
        o_ref[...] = silu(pre)

    cur, prev = _conv_specs(t, tm)
    return pl.pallas_call(
        body, name="conv_forward", grid=(t // tm,),
        in_specs=[cur, prev, pl.BlockSpec((4, CONV_DIM), lambda i: (0, 0)), pl.BlockSpec((1, CONV_DIM), lambda i: (0, 0))],
        out_specs=pl.BlockSpec((tm, CONV_DIM), lambda i: (i, 0)),
        out_shape=jax.ShapeDtypeStruct((t, CONV_DIM), F32),
        compiler_params=_params(("arbitrary",)),
    )(proj, proj, w, b)


def conv_backward_pre(proj, dx, db_, dc_, w, b):
    t = proj.shape[0]
    tm = _pick(t, (256, 128, 64))

    def body(cur_ref, prev_ref, dx_ref, dbm_ref, dcm_ref, w_ref, b_ref, dpre_ref, dw_ref, dbias_ref):
        @pl.when(pl.program_id(0) == 0)
        def _():
            dw_ref[...] = jnp.zeros_like(dw_ref)
            dbias_ref[...] = jnp.zeros_like(dbias_ref)

        first = pl.program_id(0) == 0
        for lo, hi, src in ((0, B_INNER, dx_ref), (B_INNER, B_INNER + 512, dbm_ref), (B_INNER + 512, CONV_DIM, dcm_ref)):
            cols = slice(lo, hi)
            prev = jnp.where(first, 0.0, prev_ref[:, cols])
            pre, taps = _conv_pre(cur_ref[:, cols], prev, w_ref[:, cols], b_ref[:, cols], tm)
            sg = sigmoid(pre)
            dpre = src[...] * (sg * (1.0 + pre * (1.0 - sg)))
            dpre_ref[:, cols] = dpre
            dbias_ref[:, cols] += jnp.sum(dpre, axis=0, keepdims=True)
            for j in range(4):
                dw_ref[j:j + 1, cols] += jnp.sum(dpre * taps[j], axis=0, keepdims=True)

    cur, prev = _conv_specs(t, tm)
    row = lambda w_: pl.BlockSpec((tm, w_), lambda i: (i, 0))
    return pl.pallas_call(
        body, name="conv_backward_pre", grid=(t // tm,),
        in_specs=[cur, prev, row(B_INNER), row(512), row(512),
                  pl.BlockSpec((4, CONV_DIM), lambda i: (0, 0)), pl.BlockSpec((1, CONV_DIM), lambda i: (0, 0))],
        out_specs=[row(CONV_DIM), pl.BlockSpec((4, CONV_DIM), lambda i: (0, 0)), pl.BlockSpec((1, CONV_DIM), lambda i: (0, 0))],
        out_shape=[jax.ShapeDtypeStruct((t, CONV_DIM), F32), jax.ShapeDtypeStruct((4, CONV_DIM), F32),
                   jax.ShapeDtypeStruct((1, CONV_DIM), F32)],
        compiler_params=_params(("arbitrary",)),
    )(proj, proj, dx, db_, dc_, w, b)


def conv_backward_input(dpre, w, d_proj):
    t = dpre.shape[0]
    tm = _pick(t, (256, 128, 64))
    per = tm // CONV_HALO
    last = t // CONV_HALO - 1
    nt = t // tm

    def body(cur_ref, nxt_ref, w_ref, _, o_ref):
        nxt = jnp.where(pl.program_id(0) == nt - 1, 0.0, nxt_ref[...])
        stacked = jnp.concatenate([cur_ref[...], nxt], axis=0)
        w_ = w_ref[...]
        acc = stacked[0:tm] * w_[3:4]
        for j in range(3):
            s = 3 - j
            acc = acc + pltpu.roll(stacked, tm + CONV_HALO - s, 0)[0:tm] * w_[j:j + 1]
        o_ref[...] = acc.astype(o_ref.dtype)

    return pl.pallas_call(
        body, name="conv_backward_input", grid=(nt,),
        in_specs=[pl.BlockSpec((tm, CONV_DIM), lambda i: (i, 0)),
                  pl.BlockSpec((CONV_HALO, CONV_DIM), lambda i: (jnp.minimum((i + 1) * per, last), 0)),
                  pl.BlockSpec((4, CONV_DIM), lambda i: (0, 0)), pl.BlockSpec(memory_space=pl.ANY)],
        out_specs=pl.BlockSpec((tm, CONV_DIM), lambda i: (i, COL_XBC // CONV_DIM)),
        out_shape=jax.ShapeDtypeStruct(d_proj.shape, d_proj.dtype),
        input_output_aliases={3: 0},
        compiler_params=_params(("arbitrary",)),
    )(dpre, dpre, w, d_proj)


def stage_modulate(x, sc, sh):
    return _ln(x) * (1.0 + sc) + sh


def stage_merge(ga, gb, ya, yb):
    return sigmoid(ga) * ya + sigmoid(gb) * yb


def stage_post_mixer(x, h, g1, ln_g, ln_b, sc2, sh2):
    x1 = _ln(ALPHA * x + g1 * h) * ln_g + ln_b
    return x1, _ln(x1) * (1.0 + sc2) + sh2


def stage_swiglu(a, b):
    return silu(a) * b


def stage_loss(x1, hf, tgt, g2, ln_g, ln_b):
    x2 = _ln(ALPHA * x1 + g2 * hf) * ln_g + ln_b
    return 0.5 * jnp.sum(jnp.mean(jnp.square(x2 - tgt), axis=-1, keepdims=True), axis=0, keepdims=True)


def local_step(x, tgt, mod, wts, small, early=None, mid=None, late=None, last=None):
    sh1, sc1, g1, sh2, sc2, g2 = mod
    lb, gn, conv_w, conv_b, dtb, alog, dsk, nw, ln1_g, ln1_b, ln2_g, ln2_b = small
    vec = (1, D)

    (u1,) = rowwise("modulate1", lambda r, c: ((stage_modulate(r[0], *c),), ()), [_full(x)], [sc1, sh1], [(D, BF16)])
    w_in = wts.input_projection(u1)
    proj = matmul(u1, w_in, "nn", F32, "in_proj")
    ya_in, st_a = hgrn_forward(proj, lb, gn + wts.start_rest(proj)[0:1])
    xc = conv_forward(proj, conv_w, conv_b)
    w_a, w_b, w_o, w_gu, w_d = wts.rest(xc)
    yb_in, st_b = ssd_forward(xc, proj, dtb, alog, dsk, nw)
    ya = matmul(ya_in, w_a, "nn", F32, "branch_a")
    yb = matmul(yb_in, w_b, "nn", F32, "branch_b")
    gate_rows = [(proj, D, COL_GA // D), (proj, D, COL_GB // D), _full(ya), _full(yb)]
    (merged,) = rowwise("merge", lambda r, c: ((stage_merge(*r),), ()), gate_rows, [], [(D, BF16)])
    h = matmul(merged, w_o, "nn", F32, "out_proj")
    post_consts = [g1, ln1_g, ln1_b, sc2, sh2]
    x1, u2 = rowwise("post_mixer", lambda r, c: (stage_post_mixer(*r, *c), ()), [_full(x), _full(h)], post_consts,
                     [(D, F32), (D, BF16)])
    ab = matmul(u2, w_gu, "nt", F32, "ffn_in")
    (p,) = rowwise("swiglu", lambda r, c: ((stage_swiglu(*r),), ()), [(ab, D_FF, 0), (ab, D_FF, 1)], [], [(D_FF, BF16)])
    hf = matmul(p, w_d, "nn", F32, "ffn_out")

    def loss_bwd(r, c):
        loss, vjp = jax.vjp(stage_loss, *r, *c)
        dx1, dhf, _, dg2, dlg, dlb_ = vjp(jnp.ones((1, 1), F32))
        return (dx1, dhf), (loss, dg2, dlg, dlb_)

    dx1, dhf, loss, dg2, dln2_g, dln2_b = rowwise(
        "loss_backward", loss_bwd, [_full(x1), _full(hf), _full(tgt)], [g2, ln2_g, ln2_b],
        [(D, F32), (D, BF16)], [(1, 1), vec, vec, vec])
    dp = matmul(dhf, w_d, "nt", F32, "ffn_out_dx")
    dw_d = matmul(p, dhf, "tn", F32, "ffn_out_dw")

    def swiglu_bwd(r, c):
        _, vjp = jax.vjp(stage_swiglu, r[0], r[1])
        da, db_ = vjp(r[2])
        return (jnp.concatenate([da, db_], axis=1),), ()

    (dab,) = rowwise("swiglu_backward", swiglu_bwd, [(ab, D_FF, 0), (ab, D_FF, 1), _full(dp)], [], [(2 * D_FF, BF16)])
    du2 = matmul(dab, w_gu, "nn", F32, "ffn_in_dx")
    dw_gu = matmul(dab, u2, "tn", F32, "ffn_in_dw")

    def post_bwd(r, c):
        _, vjp = jax.vjp(stage_post_mixer, r[0], r[1], *c)
        dx, dh, *dc = vjp((r[2], r[3]))
        return (dx, dh), tuple(dc)

    dx_a, dh, dg1, dln1_g, dln1_b, dsc2, dsh2 = rowwise(
        "post_mixer_backward", post_bwd, [_full(x), _full(h), _full(dx1), _full(du2)], post_consts,
        [(D, F32), (D, BF16)], [vec] * 5)
    dmerged = matmul(dh, w_o, "nt", F32, "out_proj_dx")
    dw_o = matmul(merged, dh, "tn", F32, "out_proj_dw")

    def merge_bwd(r, c):
        _, vjp = jax.vjp(stage_merge, *r[:4])
        dga, dgb, dya, dyb = vjp(r[4])
        return (jnp.concatenate([dga, dgb], axis=1), dya, dyb), ()

    dproj, dya, dyb = rowwise("merge_backward", merge_bwd, gate_rows + [_full(dmerged)], [],
                              [(2 * D, BF16), (D, BF16), (D, BF16)], new_wide=(IN_PAD, COL_GA // (2 * D)))
    dya_in = matmul(dya, w_a, "nt", F32, "branch_a_dx")
    dw_a = matmul(ya_in, dya, "tn", F32, "branch_a_dw")
    dyb_in = matmul(dyb, w_b, "nt", F32, "branch_b_dx")
    dw_b = matmul(yb_in, dyb, "tn", F32, "branch_b_dw")
    gn_after = gn if early is None else gn + early((dw_a, dw_b, dw_o, dw_gu, dw_d))[0:1]
    dproj, dlb, dgn = hgrn_backward(proj, st_a, dya_in, lb, gn_after, dproj)
    dtb_after = dtb if mid is None else dtb + mid(dlb)[0:1, 0:1]
    dxs, dbm, dcm, ddt, dproj, ddtb, dalog, ddsk, dnw = ssd_backward(xc, proj, st_b, dyb_in, dtb_after, alog, dsk, nw, dproj)
    dpre, dconv_w, dconv_b = conv_backward_pre(proj, dxs, dbm, dcm, conv_w, conv_b)
    if late is not None:
        late(dconv_b)
    dproj = conv_backward_input(dpre, conv_w, dproj)
    t = x.shape[0]
    tail = jnp.concatenate([jnp.sum(ddt, axis=0).astype(BF16), jnp.zeros((t, IN_PAD - COL_DT - LANES), BF16)], axis=1)
    dproj = lax.dynamic_update_slice(dproj, tail, (0, COL_DT))
    dw_in = matmul(u1, dproj, "tn", F32, "in_proj_dw")
    du1 = matmul(dproj, w_in, "nt", F32, "in_proj_dx", after=None if last is None else last(dw_in))

    def mod_bwd(r, c):
        _, vjp = jax.vjp(stage_modulate, r[0], *c)
        dx, dsc, dsh = vjp(r[1])
        return (dx + r[2],), (dsc, dsh)

    grad_x, dsc1, dsh1 = rowwise("modulate1_backward", mod_bwd, [_full(x), _full(du1), _full(dx_a)], [sc1, sh1],
                                 [(D, F32)], [vec, vec])
    d_mod = (dsh1, dsc1, dg1, dsh2, dsc2, dg2)
    d_wts = (dw_in, dw_a, dw_b, dw_o, dw_gu, dw_d)
    d_small = (dlb, dgn, dconv_w, dconv_b, jnp.sum(ddtb, axis=0),
               jnp.sum(dalog, axis=0), ddsk.reshape(1, B_INNER), dnw.reshape(1, B_INNER),
               dln1_g, dln1_b, dln2_g, dln2_b)
    return loss, grad_x, d_mod, d_wts, d_small


HBM = pl.BlockSpec(memory_space=pltpu.HBM)
SEM = pl.BlockSpec(memory_space=pltpu.SEMAPHORE)
DATAFLOW = pltpu.SideEffectType.DATAFLOW_SIDE_EFFECTING


def _place():
    return lax.axis_index("x"), lax.axis_index("y"), lax.axis_index("c")


def _other_chips(x, y):
    return [(1 - x, y), (x, 1 - y), (1 - x, 1 - y)]


def _remote(src, dst, send_sem, recv_sem, device):
    return pltpu.make_async_remote_copy(src_ref=src, dst_ref=dst, send_sem=send_sem, recv_sem=recv_sem,
                                        device_id=device, device_id_type=MESH)


def gather_rows(v, name):
    n = v.shape[1]

    def body(v_ref, out_ref, send_sems, recv_sems, local_sem):
        x, y, c = _place()
        mine = pltpu.make_async_copy(v_ref, out_ref.at[4 * x + 2 * y + c], local_sem)
        mine.start()
        sends, recvs = [], []
        for m in range(1, 8):
            px = 1 - x if m & 4 else x
            py = 1 - y if m & 2 else y
            pc = 1 - c if m & 1 else c
            sends.append(_remote(v_ref, out_ref.at[4 * x + 2 * y + c], send_sems.at[m - 1], recv_sems.at[m - 1], (px, py, pc)))
            recvs.append(_remote(v_ref, out_ref.at[4 * px + 2 * py + pc], send_sems.at[m - 1], recv_sems.at[m - 1], (px, py, pc)))
        for cp in sends:
            cp.start()
        for cp in recvs:
            cp.wait_recv()
        for cp in sends:
            cp.wait_send()
        mine.wait()

    return pl.pallas_call(
        body, name=name, in_specs=[HBM], out_specs=HBM,
        out_shape=jax.ShapeDtypeStruct((8, 1, n), v.dtype),
        scratch_shapes=[pltpu.SemaphoreType.DMA((7,)), pltpu.SemaphoreType.DMA((7,)), pltpu.SemaphoreType.DMA],
    )(v)


def exchange_rows(part, name):
    w = part.shape[2]

    def body(p_ref, out_ref, send_sems, recv_sems, local_sem):
        x, y, c = _place()
        k = 2 * x + y
        mine = pltpu.make_async_copy(p_ref.at[4 * x + 2 * y + c], out_ref.at[k], local_sem)
        mine.start()
        sends, recvs = [], []
        for j, (px, py) in enumerate(_other_chips(x, y)):
            sends.append(_remote(p_ref.at[4 * px + 2 * py + c], out_ref.at[k], send_sems.at[j], recv_sems.at[j], (px, py, c)))
            recvs.append(_remote(p_ref.at[4 * px + 2 * py + c], out_ref.at[2 * px + py], send_sems.at[j], recv_sems.at[j], (px, py, c)))
        for cp in sends:
            cp.start()
        for cp in recvs:
            cp.wait_recv()
        for cp in sends:
            cp.wait_send()
        mine.wait()

    return pl.pallas_call(
        body, name=name, in_specs=[HBM], out_specs=HBM,
        out_shape=jax.ShapeDtypeStruct((4, 1, w), part.dtype),
        scratch_shapes=[pltpu.SemaphoreType.DMA((3,)), pltpu.SemaphoreType.DMA((3,)), pltpu.SemaphoreType.DMA],
    )(part)


def _half_of_slot(ref, rows, px, py, pc):
    return ref.at[2 * px + py, pl.ds(pc * (rows // 2), rows // 2), :]


def gather_start(shards, after):
    n = len(shards)

    def body(*refs):
        w_refs, land_refs = refs[:n], refs[n:2 * n]
        send_a, recv_a, send_b, recv_b = refs[2 * n + 1:2 * n + 5]
        token = refs[-1]
        x, y, c = _place()
        for i in range(n):
            rows = shards[i].shape[0]
            for j, (px, py) in enumerate(_other_chips(x, y)):
                sems = (send_a.at[j], recv_a.at[j]) if i == 0 else (send_b.at[j * (n - 1) + i - 1], recv_b.at[j * (n - 1) + i - 1])
                _remote(w_refs[i].at[pl.ds(c * (rows // 2), rows // 2), :], _half_of_slot(land_refs[i], rows, x, y, c),
                        *sems, (px, py, c)).start()
        token[...] = jnp.zeros_like(token)

    hbm = lambda a: pltpu.with_memory_space_constraint(a, pltpu.HBM)
    lands = [lax.empty((4,) + s.shape, s.dtype) for s in shards]
    dma = pltpu.SemaphoreType.DMA
    return pl.pallas_call(
        body, name="gather_start",
        out_shape=(dma((3,)), dma((3,)), dma((3 * (n - 1),)), dma((3 * (n - 1),)),
                   *[pltpu.HBM(a.shape, a.dtype) for a in list(shards) + lands], jax.ShapeDtypeStruct((8, LANES), F32)),
        in_specs=[HBM] * (2 * n) + [pl.BlockSpec(memory_space=pl.ANY)],
        out_specs=(SEM, SEM, SEM, SEM, *[HBM] * (2 * n), pl.BlockSpec(memory_space=pltpu.VMEM)),
        input_output_aliases={i: 4 + i for i in range(2 * n)},
        compiler_params=pltpu.CompilerParams(has_side_effects=DATAFLOW),
    )(*[hbm(a) for a in list(shards) + lands], after)


def gather_wait(send_sems, recv_sems, shards, lands, after, tag):
    n = len(shards)

    def body(*refs):
        w_refs, land_refs = refs[:n], refs[n:2 * n]
        send_ref, recv_ref = refs[2 * n], refs[2 * n + 1]
        x, y, c = _place()
        for i in range(n):
            rows = shards[i].shape[0]
            for j, (px, py) in enumerate(_other_chips(x, y)):
                cp = _remote(w_refs[i].at[pl.ds(c * (rows // 2), rows // 2), :], _half_of_slot(land_refs[i], rows, px, py, c),
                             send_ref.at[j * n + i], recv_ref.at[j * n + i], (px, py, c))
                cp.wait_send()
                cp.wait_recv()

    out = pl.pallas_call(
        body, name="gather_wait_" + tag,
        out_shape=tuple(pltpu.HBM(a.shape, a.dtype) for a in list(shards) + list(lands)),
        in_specs=[HBM] * (2 * n) + [SEM, SEM, pl.BlockSpec(memory_space=pl.ANY)], out_specs=tuple([HBM] * (2 * n)),
        input_output_aliases={i: i for i in range(2 * n)},
        compiler_params=pltpu.CompilerParams(has_side_effects=DATAFLOW),
    )(*shards, *lands, send_sems, recv_sems, after)
    return list(out[n:])


def forward_start(lands, tag):
    n = len(lands)

    def body(*refs):
        land_refs = refs[:n]
        send_sems, recv_sems = refs[n], refs[n + 1]
        token = refs[-1]
        x, y, c = _place()
        for i in range(n):
            rows = lands[i].shape[1]
            for j, (px, py) in enumerate(_other_chips(x, y)):
                mine = _half_of_slot(land_refs[i], rows, px, py, c)
                _remote(mine, mine, send_sems.at[j * n + i], recv_sems.at[j * n + i], (x, y, 1 - c)).start()
        token[...] = jnp.zeros_like(token)

    dma = pltpu.SemaphoreType.DMA
    return pl.pallas_call(
        body, name="forward_start_" + tag,
        out_shape=(dma((3 * n,)), dma((3 * n,)), *[pltpu.HBM(a.shape, a.dtype) for a in lands],
                   jax.ShapeDtypeStruct((8, LANES), F32)),
        in_specs=[HBM] * n, out_specs=(SEM, SEM, *[HBM] * n, pl.BlockSpec(memory_space=pltpu.VMEM)),
        input_output_aliases={i: 2 + i for i in range(n)},
        compiler_params=pltpu.CompilerParams(has_side_effects=DATAFLOW),
    )(*lands)


def forward_wait(started, after, tag):
    send_sems, recv_sems, *rest = started
    lands = rest[:-1]
    n = len(lands)

    def body(*refs):
        land_refs = refs[:n]
        send_ref, recv_ref = refs[n], refs[n + 1]
        x, y, c = _place()
        for i in range(n):
            rows = lands[i].shape[1]
            for j, (px, py) in enumerate(_other_chips(x, y)):
                cp = _remote(_half_of_slot(land_refs[i], rows, px, py, c), _half_of_slot(land_refs[i], rows, px, py, 1 - c),
                             send_ref.at[j * n + i], recv_ref.at[j * n + i], (x, y, 1 - c))
                cp.wait_send()
                cp.wait_recv()

    out = pl.pallas_call(
        body, name="forward_wait_" + tag,
        out_shape=tuple(pltpu.HBM(a.shape, a.dtype) for a in lands),
        in_specs=[HBM] * n + [SEM, SEM, pl.BlockSpec(memory_space=pl.ANY)], out_specs=tuple([HBM] * n),
        input_output_aliases={i: i for i in range(n)},
        compiler_params=pltpu.CompilerParams(has_side_effects=DATAFLOW),
    )(*lands, send_sems, recv_sems, after)
    return list(out)


def pair_start(slabs, tag):
    n = len(slabs)

    def body(*refs):
        g_refs, land_refs = refs[:n], refs[n:2 * n]
        send_sems, recv_sems = refs[2 * n], refs[2 * n + 1]
        token = refs[-1]
        x, y, c = _place()
        for i in range(n):
            hr = slabs[i].shape[1] // 2
            _remote(g_refs[i].at[:, pl.ds((1 - c) * hr, hr), :], land_refs[i], send_sems.at[i], recv_sems.at[i],
                    (x, y, 1 - c)).start()
        token[...] = jnp.zeros_like(token)

    hbm = lambda a: pltpu.with_memory_space_constraint(a, pltpu.HBM)
    lands = [lax.empty((4, s.shape[1] // 2, s.shape[2]), s.dtype) for s in slabs]
    dma = pltpu.SemaphoreType.DMA
    return pl.pallas_call(
        body, name="pair_start_" + tag,
        out_shape=(dma((n,)), dma((n,)), *[pltpu.HBM(a.shape, a.dtype) for a in list(slabs) + lands],
                   jax.ShapeDtypeStruct((8, LANES), F32)),
        in_specs=[HBM] * (2 * n), out_specs=(SEM, SEM, *[HBM] * (2 * n), pl.BlockSpec(memory_space=pltpu.VMEM)),
        input_output_aliases={i: 2 + i for i in range(2 * n)},
        compiler_params=pltpu.CompilerParams(has_side_effects=DATAFLOW),
    )(*[hbm(a) for a in list(slabs) + lands])


def pair_wait(started, after, tag):
    send_sems, recv_sems, *rest = started
    n = (len(rest) - 1) // 2
    slabs, lands = rest[:n], rest[n:2 * n]

    def body(*refs):
        g_refs, land_refs = refs[:n], refs[n:2 * n]
        send_ref, recv_ref = refs[2 * n], refs[2 * n + 1]
        x, y, c = _place()
        for i in range(n):
            hr = slabs[i].shape[1] // 2
            cp = _remote(g_refs[i].at[:, pl.ds((1 - c) * hr, hr), :], land_refs[i], send_ref.at[i], recv_ref.at[i], (x, y, 1 - c))
            cp.wait_send()
            cp.wait_recv()

    out = pl.pallas_call(
        body, name="pair_wait_" + tag,
        out_shape=tuple(pltpu.HBM(a.shape, a.dtype) for a in list(slabs) + list(lands)),
        in_specs=[HBM] * (2 * n) + [SEM, SEM, pl.BlockSpec(memory_space=pl.ANY)], out_specs=tuple([HBM] * (2 * n)),
        input_output_aliases={i: i for i in range(2 * n)},
        compiler_params=pltpu.CompilerParams(has_side_effects=DATAFLOW),
    )(*slabs, *lands, send_sems, recv_sems, after)
    return list(out[:n]), list(out[n:])


def _tile2(rows, cols):
    fits = lambda r, c: r * c * 4 <= BLOCK_BYTES
    if fits(rows, cols):
        return rows, cols
    for r in (1024, 512, 256, 128, 64):
        if rows % r == 0 and fits(r, cols):
            return r, cols
    return rows, next(cols // k for k in (2, 3, 4, 6, 8, 12, 16) if cols % (k * LANES) == 0 and fits(rows, cols // k))


def pair_add(g, p, c, name):
    _, hr, cols = p.shape
    tm, tc = _tile2(hr, cols)
    per = hr // tm

    def body(c_ref, g_ref, p_ref, o_ref):
        o_ref[...] = (g_ref[...] + p_ref[...]).astype(o_ref.dtype)

    return pl.pallas_call(
        body, name=name,
        grid_spec=pltpu.PrefetchScalarGridSpec(
            num_scalar_prefetch=1, grid=(4, per, cols // tc),
            in_specs=[pl.BlockSpec((None, tm, tc), lambda k, i, j, c_ref: (k, c_ref[0] * per + i, j)),
                      pl.BlockSpec((None, tm, tc), lambda k, i, j, c_ref: (k, i, j))],
            out_specs=pl.BlockSpec((None, tm, tc), lambda k, i, j, c_ref: (k, i, j))),
        out_shape=jax.ShapeDtypeStruct((4, hr, cols), BF16),
        compiler_params=_params(("arbitrary", "arbitrary", "arbitrary")),
    )(c.reshape(1).astype(jnp.int32), g, p)


def scatter_start(sums, tag):
    n = len(sums)

    def body(*refs):
        s_refs, land_refs = refs[:n], refs[n:2 * n]
        send_sems, recv_sems = refs[2 * n], refs[2 * n + 1]
        token = refs[-1]
        x, y, c = _place()
        k = 2 * x + y
        for i in range(n):
            for j, (px, py) in enumerate(_other_chips(x, y)):
                _remote(s_refs[i].at[2 * px + py], land_refs[i].at[k], send_sems.at[j * n + i], recv_sems.at[j * n + i],
                        (px, py, c)).start()
        token[...] = jnp.zeros_like(token)

    hbm = lambda a: pltpu.with_memory_space_constraint(a, pltpu.HBM)
    return pl.pallas_call(
        body, name="scatter_start_" + tag,
        out_shape=(pltpu.SemaphoreType.DMA((3 * n,)), pltpu.SemaphoreType.DMA((3 * n,)),
                   *[pltpu.HBM(s.shape, s.dtype) for s in sums], *[pltpu.HBM(s.shape, s.dtype) for s in sums],
                   jax.ShapeDtypeStruct((8, LANES), F32)),
        in_specs=[HBM] * (2 * n), out_specs=(SEM, SEM, *[HBM] * (2 * n), pl.BlockSpec(memory_space=pltpu.VMEM)),
        input_output_aliases={i: 2 + i for i in range(2 * n)},
        compiler_params=pltpu.CompilerParams(has_side_effects=DATAFLOW),
    )(*[hbm(s) for s in sums], *[hbm(lax.empty(s.shape, s.dtype)) for s in sums])


def scatter_wait(started, after, tag):
    send_sems, recv_sems, *rest = started
    n = (len(rest) - 1) // 2
    sums, lands = rest[:n], rest[n:2 * n]

    def body(*refs):
        s_refs, land_refs = refs[:n], refs[n:2 * n]
        send_ref, recv_ref = refs[2 * n], refs[2 * n + 1]
        x, y, c = _place()
        for i in range(n):
            for j, (px, py) in enumerate(_other_chips(x, y)):
                cp = _remote(s_refs[i].at[2 * px + py], land_refs[i].at[2 * px + py], send_ref.at[j * n + i],
                             recv_ref.at[j * n + i], (px, py, c))
                cp.wait_send()
                cp.wait_recv()

    out = pl.pallas_call(
        body, name="scatter_wait_" + tag,
        out_shape=tuple(pltpu.HBM(s.shape, s.dtype) for s in sums + lands),
        in_specs=[HBM] * (2 * n) + [SEM, SEM, pl.BlockSpec(memory_space=pl.ANY)], out_specs=tuple([HBM] * (2 * n)),
        input_output_aliases={i: i for i in range(2 * n)},
        compiler_params=pltpu.CompilerParams(has_side_effects=DATAFLOW),
    )(*sums, *lands, send_sems, recv_sems, after)
    return list(out[n:])


def sum_chips(landed, own, chip, core, name):
    _, hr, cols = landed.shape
    tm, tc = _tile2(hr, cols)
    per = hr // tm

    def body(idx_ref, l0, l1, l2, l3, own_ref, o_ref):
        mine = own_ref[...].astype(F32)
        v = [jnp.where(idx_ref[0] == k, mine, ref[...].astype(F32)) for k, ref in enumerate((l0, l1, l2, l3))]
        o_ref[...] = ((v[0] + v[1]) + v[2]) + v[3]

    slot = lambda k: pl.BlockSpec((None, tm, tc),
                                  lambda i, j, idx: (jnp.where(idx[0] == k, (k + 1) & 3, k), i, j))
    return pl.pallas_call(
        body, name=name,
        grid_spec=pltpu.PrefetchScalarGridSpec(
            num_scalar_prefetch=1, grid=(per, cols // tc),
            in_specs=[slot(0), slot(1), slot(2), slot(3),
                      pl.BlockSpec((None, tm, tc), lambda i, j, idx: (idx[0], i, j))],
            out_specs=pl.BlockSpec((tm, tc), lambda i, j, idx: (idx[1] * per + i, j))),
        out_shape=jax.ShapeDtypeStruct((2 * hr, cols), F32),
        compiler_params=_params(("arbitrary", "arbitrary")),
    )(jnp.stack([chip, core]).astype(jnp.int32), landed, landed, landed, landed, own)


def exchange_halves(bufs):
    n = len(bufs)

    def body(*refs):
        out_refs = refs[n:2 * n]
        send_sems, recv_sems = refs[2 * n:]
        x, y, c = _place()
        sends, recvs = [], []
        for i in range(n):
            hr = bufs[i].shape[0] // 2
            own = out_refs[i].at[pl.ds(c * hr, hr), :]
            other = out_refs[i].at[pl.ds((1 - c) * hr, hr), :]
            sends.append(_remote(own, own, send_sems.at[i], recv_sems.at[i], (x, y, 1 - c)))
            recvs.append(_remote(other, other, send_sems.at[i], recv_sems.at[i], (x, y, 1 - c)))
        for cp in sends:
            cp.start()
        for cp in recvs:
            cp.wait_recv()
        for cp in sends:
            cp.wait_send()

    return pl.pallas_call(
        body, name="exchange_halves", in_specs=[HBM] * n, out_specs=[HBM] * n,
        out_shape=[jax.ShapeDtypeStruct(b.shape, b.dtype) for b in bufs],
        input_output_aliases={i: i for i in range(n)},
        scratch_shapes=[pltpu.SemaphoreType.DMA((n,)), pltpu.SemaphoreType.DMA((n,))],
    )(*bufs)


def _relayout(name, arrays, in_blocks, out_blocks, out_shapes, fn):
    rows = 128
    spec = lambda blk: pl.BlockSpec(blk, (lambda i: (0, i, 0)) if len(blk) == 3 else (lambda i: (i, 0)))

    def body(*refs):
        n_in = len(arrays)
        outs = fn(*[r[...] for r in refs[:n_in]])
        for ref, val in zip(refs[n_in:], outs, strict=True):
            if isinstance(val, list):
                for k, piece in enumerate(val):
                    ref[k] = piece
            else:
                ref[...] = val

    return pl.pallas_call(
        body, name=name, grid=(D // rows,),
        in_specs=[spec(b) for b in in_blocks], out_specs=[spec(b) for b in out_blocks], out_shape=out_shapes,
        compiler_params=_params(("arbitrary",)),
    )(*arrays)


def assemble_in_proj(g):
    def fn(v):
        w = jnp.concatenate([v[k] for k in range(4)], axis=1)
        return (jnp.concatenate([w[:, :ORIG_Z], w[:, ORIG_GA:], w[:, ORIG_XBC:ORIG_DT], w[:, ORIG_Z:ORIG_XBC],
                                 w[:, ORIG_DT:ORIG_GA], jnp.zeros((w.shape[0], IN_PAD - IN_ORIG), w.dtype)], axis=1),)

    cols = g.shape[2]
    return _relayout("assemble_in_proj", [g], [(4, 128, cols)], [(128, IN_PAD)],
                     [jax.ShapeDtypeStruct((D, IN_PAD), g.dtype)], fn)[0]


def rows_exchange(a, name):
    hr = a.shape[0] // 2

    def body(a_ref, out_ref, send_sem, recv_sem):
        x, y, c = _place()
        cp = _remote(a_ref.at[pl.ds((1 - c) * hr, hr), :], out_ref, send_sem, recv_sem, (x, y, 1 - c))
        cp.start()
        cp.wait()

    return pl.pallas_call(
        body, name=name, in_specs=[HBM], out_specs=HBM,
        out_shape=jax.ShapeDtypeStruct((hr, a.shape[1]), a.dtype),
        scratch_shapes=[pltpu.SemaphoreType.DMA, pltpu.SemaphoreType.DMA],
    )(a)


def split_pair_add(dw, received, core):
    cols = IN_ORIG // 4
    rows, hr = 128, D // 2
    per = hr // rows

    def body(c_ref, own_ref, got_ref, o_ref):
        d = own_ref[...] + got_ref[...]
        w = jnp.concatenate([d[:, :COL_GA], d[:, COL_Z:COL_DT], d[:, COL_XBC:COL_Z], d[:, COL_DT:COL_DT + 32],
                             d[:, COL_GA:COL_XBC]], axis=1)
        for k in range(4):
            o_ref[k] = w[:, k * cols:(k + 1) * cols].astype(o_ref.dtype)

    return pl.pallas_call(
        body, name="split_pair_add",
        grid_spec=pltpu.PrefetchScalarGridSpec(
            num_scalar_prefetch=1, grid=(per,),
            in_specs=[pl.BlockSpec((rows, IN_PAD), lambda i, c_ref: (c_ref[0] * per + i, 0)),
                      pl.BlockSpec((rows, IN_PAD), lambda i, c_ref: (i, 0))],
            out_specs=pl.BlockSpec((4, rows, cols), lambda i, c_ref: (0, i, 0))),
        out_shape=jax.ShapeDtypeStruct((4, hr, cols), BF16),
        compiler_params=_params(("arbitrary",)),
    )(core.reshape(1).astype(jnp.int32), dw, received)


def ada_prepare(c_all, w_ada, hgrn_lb):
    def body(c_ref, w_ref, lb_ref, mod_ref, row_ref):
        mod_ref[...] = hdot(silu(c_ref[...]), w_ref[...])
        row_ref[...] = sigmoid(lb_ref[0:1, :] - lb_ref[1:2, :])

    return pl.pallas_call(
        body, name="ada_prepare",
        out_shape=[jax.ShapeDtypeStruct((8, w_ada.shape[1]), F32), jax.ShapeDtypeStruct((1, D), F32)],
        compiler_params=pltpu.CompilerParams(vmem_limit_bytes=VMEM_LIMIT),
    )(c_all, w_ada, hgrn_lb)


SMALL_SEGS = (("mod", 6 * D), ("lb", D), ("gnorm", LANES), ("conv_w", 4 * CONV_DIM), ("conv_b", CONV_DIM),
              ("dt_bias", LANES), ("a_log", LANES), ("d", B_INNER), ("ssm_norm", B_INNER),
              ("ln1_g", D), ("ln1_b", D), ("ln2_g", D), ("ln2_b", D), ("loss", LANES))
SMALL_PARAMS = ("b_ada", "hgrn_lb", "hgrn_gnorm", "ssm_conv_b", "ssm_dt_bias", "ssm_a_log", "ssm_d", "ssm_norm",
                "ln1_g", "ln1_b", "ln2_g", "ln2_b")


def finalize_small(g_all, c_all, dmod_cols, params, m, v):
    n_p = len(SMALL_PARAMS)
    offs, o = {}, 0
    for nm, width in SMALL_SEGS:
        offs[nm] = (o, width)
        o += width

    def body(*refs):
        g_ref, c_ref, dm_ref = refs[:3]
        p_refs = refs[3:3 + n_p]
        m_refs = refs[3 + n_p:3 + 2 * n_p]
        v_refs = refs[3 + 2 * n_p:3 + 3 * n_p]
        outs = refs[3 + 3 * n_p:]
        gwa_ref, gcw_ref, loss_ref = outs[:3]
        res = outs[3:]
        total = jnp.sum(g_ref[...], axis=0, keepdims=True)
        seg = lambda nm: total[:, offs[nm][0]:offs[nm][0] + offs[nm][1]]
        loss_ref[...] = seg("loss")
        gwa_ref[...] = hdot(silu(c_ref[...]), dm_ref[...], "tn")
        cw = seg("conv_w")
        for j in range(4):
            gcw_ref[j:j + 1, :] = cw[:, j * CONV_DIM:(j + 1) * CONV_DIM]
        hc = lax.broadcasted_iota(jnp.int32, (B_INNER, LANES), 0)
        hj = lax.broadcasted_iota(jnp.int32, (B_INNER, LANES), 1)
        per_head = ((hc >> 6) == hj).astype(F32)
        heads = lambda nm: hdot(jnp.broadcast_to(seg(nm), (8, B_INNER)), per_head)[0:1, 0:32]
        lbp = sigmoid(p_refs[1][0:1, :] - p_refs[1][1:2, :])
        g_row = seg("lb") * lbp * (1.0 - lbp)
        grads = {"b_ada": seg("mod"), "hgrn_gnorm": seg("gnorm"), "ssm_conv_b": seg("conv_b"),
                 "ssm_dt_bias": seg("dt_bias")[:, 0:32], "ssm_a_log": seg("a_log")[:, 0:32], "ssm_d": heads("d"),
                 "ssm_norm": seg("ssm_norm"), "ln1_g": seg("ln1_g"), "ln1_b": seg("ln1_b"),
                 "ln2_g": seg("ln2_g"), "ln2_b": seg("ln2_b")}
        for i, nm in enumerate(SMALL_PARAMS):
            g_out, d_out, m_out, v_out = res[4 * i:4 * i + 4]
            if nm == "hgrn_lb":
                for row, gv in ((0, g_row), (1, -g_row)):
                    sl = slice(row, row + 1)
                    dl, mn, vn = adamw(p_refs[i][sl, :], gv, m_refs[i][sl, :], v_refs[i][sl, :])
                    g_out[sl, :], d_out[sl, :], m_out[sl, :], v_out[sl, :] = gv, dl, mn, vn
            else:
                gv = grads[nm]
                dl, mn, vn = adamw(p_refs[i][...], gv, m_refs[i][...], v_refs[i][...])
                g_out[...], d_out[...], m_out[...], v_out[...] = gv, dl, mn, vn

    out_shape = [jax.ShapeDtypeStruct((D, dmod_cols.shape[1]), F32), jax.ShapeDtypeStruct((4, CONV_DIM), F32),
                 jax.ShapeDtypeStruct((1, LANES), F32)]
    for p in params:
        out_shape += [jax.ShapeDtypeStruct(p.shape, F32)] * 4
    return pl.pallas_call(
        body, name="finalize_small", out_shape=out_shape,
        compiler_params=pltpu.CompilerParams(vmem_limit_bytes=VMEM_LIMIT),
    )(g_all, c_all, dmod_cols, *params, *m, *v)


def adam_update(w, g, m, v, name):
    rows, cols = w.shape
    tm, tc = _tile2(rows, cols)

    def body(w_ref, g_ref, m_ref, v_ref, d_ref, mo_ref, vo_ref):
        d_ref[...], mo_ref[...], vo_ref[...] = adamw(w_ref[...], g_ref[...], m_ref[...], v_ref[...])

    spec = pl.BlockSpec((tm, tc), lambda i, j: (i, j))
    return pl.pallas_call(
        body, name=name, grid=(rows // tm, cols // tc), in_specs=[spec] * 4, out_specs=[spec] * 3,
        out_shape=[jax.ShapeDtypeStruct((rows, cols), F32)] * 3,
        compiler_params=_params(("arbitrary", "arbitrary")),
    )(w, g, m, v)


def kernel(x, c, w_ada, b_ada, w_in, hgrn_lb, hgrn_gnorm, ssm_conv_w, ssm_conv_b, ssm_dt_bias, ssm_a_log, ssm_d, ssm_norm, w_branch_a, w_branch_b, w_o, ln1_g, ln1_b, w_ffn_gate, w_ffn_up, w_ffn_down, ln2_g, ln2_b, loss_target, m_w_ada, m_b_ada, m_w_in, m_hgrn_lb, m_hgrn_gnorm, m_ssm_conv_w, m_ssm_conv_b, m_ssm_dt_bias, m_ssm_a_log, m_ssm_d, m_ssm_norm, m_w_branch_a, m_w_branch_b, m_w_o, m_ln1_g, m_ln1_b, m_w_ffn_gate, m_w_ffn_up, m_w_ffn_down, m_ln2_g, m_ln2_b, v_w_ada, v_b_ada, v_w_in, v_hgrn_lb, v_hgrn_gnorm, v_ssm_conv_w, v_ssm_conv_b, v_ssm_dt_bias, v_ssm_a_log, v_ssm_d, v_ssm_norm, v_w_branch_a, v_w_branch_b, v_w_o, v_ln1_g, v_ln1_b, v_w_ffn_gate, v_w_ffn_up, v_w_ffn_down, v_ln2_g, v_ln2_b):
    given = dict(locals())
    chip = 2 * lax.axis_index("x") + lax.axis_index("y")
    core = lax.axis_index("c")
    t = x.shape[1]

    first = gather_rows(jnp.concatenate([c, ssm_conv_w.reshape(1, CONV_DIM)], axis=1), "gather_cond").reshape(8, D + CONV_DIM)
    c_all = first[:, :D]
    conv_w = first[0::2, D:].reshape(4, 4, CONV_DIM // 4).transpose(1, 0, 2).reshape(4, CONV_DIM)
    mod_part, lb_row = ada_prepare(c_all, w_ada[0], hgrn_lb)
    mod_cols = w_ada.shape[2]
    mod_row = exchange_rows(mod_part.reshape(8, 1, mod_cols), "exchange_mod").reshape(1, 6 * D) + b_ada
    mod = tuple(mod_row[:, i * D:(i + 1) * D] for i in range(6))

    local = {nm: given[nm][0].T if nm in TRANSPOSED else given[nm][0] for nm in SHARDED}
    shards = [local[nm].astype(BF16) for nm in SHARDED]
    n_w = len(SHARDED)
    send_in, recv_in, send_rest, recv_rest, *flying = gather_start(shards, mod_row)
    sent, lands = flying[:n_w], flying[n_w:2 * n_w]
    with_own = lambda land, shard: lax.dynamic_update_slice(land, shard[None], (chip, 0, 0))

    class Weights:
        def input_projection(self, after):
            land = gather_wait(send_in, recv_in, sent[:1], lands[:1], after, "in")
            (land,) = forward_wait(forward_start(land, "in"), after, "in")
            return assemble_in_proj(with_own(land, shards[0]))

        def start_rest(self, after):
            self.started = forward_start(gather_wait(send_rest, recv_rest, sent[1:], lands[1:], after, "rest"), "rest")
            return self.started[-1]

        def rest(self, after):
            got = {nm: with_own(land, s) for nm, land, s in zip(SHARDED[1:], forward_wait(self.started, after, "rest"), shards[1:], strict=True)}
            whole = lambda nm: got[nm].reshape(4 * got[nm].shape[1], got[nm].shape[2])
            return (whole("w_branch_a"), whole("w_branch_b"), whole("w_o"),
                    jnp.concatenate([whole("w_ffn_gate"), whole("w_ffn_up")], axis=0), whole("w_ffn_down"))

    wts = Weights()

    per_head = lambda p: jnp.pad(p, ((0, 0), (0, LANES - p.shape[1])))
    small = (lb_row, hgrn_gnorm, conv_w, ssm_conv_b, per_head(ssm_dt_bias), per_head(ssm_a_log),
             jnp.repeat(ssm_d[0], B_INNER // 32)[None], ssm_norm, ln1_g, ln1_b, ln2_g, ln2_b)
    by_rows = lambda g: g.reshape(4, g.shape[0] // 4, g.shape[1])
    travelling = {}

    def start_early(dws):
        dw_a, dw_b, dw_o, dw_gu, dw_d = dws
        d_gate, d_up = by_rows(dw_gu[:D_FF]), by_rows(dw_gu[D_FF:])
        travelling["pair"] = pair_start([by_rows(dw_a), by_rows(dw_b), by_rows(dw_o), d_gate, d_up, by_rows(dw_d)], "early")
        return travelling["pair"][-1]

    def between_scans(after):
        slabs, received = pair_wait(travelling["pair"], after, "early")
        travelling["pairs"] = [pair_add(s, r, core, "pair_add_" + nm) for nm, s, r in zip(SHARDED[1:], slabs, received, strict=True)]
        travelling["started"] = scatter_start(travelling["pairs"], "early")
        return travelling["started"][-1]

    def finish_early(after):
        travelling["landed"] = scatter_wait(travelling["started"], after, "early")

    def start_last(dw_in):
        travelling["pairs_in"] = [split_pair_add(dw_in, rows_exchange(dw_in, "pair_exchange_last"), core)]
        travelling["started_in"] = scatter_start(travelling["pairs_in"], "last")
        return travelling["started_in"][-1]

    loss, grad_x, d_mod, d_wts, d_small = local_step(x[0], loss_target[0], mod, wts, small,
                                                     start_early, between_scans, finish_early, start_last)

    d_lb, d_gn, d_cw, d_cb, d_dtb, d_alog, d_dsk, d_nw, d_l1g, d_l1b, d_l2g, d_l2b = d_small
    row = jnp.concatenate(list(d_mod) + [d_lb, d_gn, d_cw.reshape(1, 4 * CONV_DIM), d_cb, d_dtb, d_alog, d_dsk, d_nw,
                                          d_l1g, d_l1b, d_l2g, d_l2b, jnp.pad(loss, ((0, 0), (0, LANES - 1)))], axis=1)
    g_all = gather_rows(row, "gather_small_grads").reshape(8, row.shape[1])
    dmod_cols = lax.dynamic_slice_in_dim(g_all, chip * mod_cols, mod_cols, axis=1)
    fin = finalize_small(g_all, c_all, dmod_cols, [given[n] for n in SMALL_PARAMS],
                         [given["m_" + n] for n in SMALL_PARAMS], [given["v_" + n] for n in SMALL_PARAMS])
    grads, deltas, new_m, new_v = {}, {}, {}, {}
    grads["w_ada"] = fin[0][None]
    grads["ssm_conv_w"] = lax.dynamic_slice_in_dim(fin[1], chip * (CONV_DIM // 4), CONV_DIM // 4, axis=1)[None]
    for i, nm in enumerate(SMALL_PARAMS):
        grads[nm], deltas[nm], new_m[nm], new_v[nm] = fin[3 + 4 * i:7 + 4 * i]

    pairs = travelling["pairs_in"] + travelling["pairs"]
    landed = scatter_wait(travelling["started_in"], fin[3], "last") + travelling["landed"]
    halves = [sum_chips(r, p, chip, core, "sum_chips_" + nm) for nm, r, p in zip(SHARDED, landed, pairs, strict=True)]
    reduced = dict(zip(SHARDED, exchange_halves(halves), strict=True))
    reduced["w_ada"], reduced["ssm_conv_w"] = grads["w_ada"][0], grads["ssm_conv_w"][0]
    reduced["w_in"] = reduced["w_in"].T
    for nm in ("w_ada", "ssm_conv_w") + SHARDED:
        flipped = nm in TRANSPOSED or nm == "w_in"
        work = (lambda a: a[0].T) if flipped else (lambda a: a[0])
        back = (lambda a: a.T[None]) if flipped else (lambda a: a[None])
        d_, m_, v_ = adam_update(work(given[nm]), reduced[nm], work(given["m_" + nm]), work(given["v_" + nm]), "adam_" + nm)
        grads[nm], deltas[nm], new_m[nm], new_v[nm] = back(reduced[nm]), back(d_), back(m_), back(v_)

    names = ("w_ada", "b_ada", "w_in", "hgrn_lb", "hgrn_gnorm", "ssm_conv_w", "ssm_conv_b", "ssm_dt_bias", "ssm_a_log",
             "ssm_d", "ssm_norm", "w_branch_a", "w_branch_b", "w_o", "ln1_g", "ln1_b", "w_ffn_gate", "w_ffn_up",
             "w_ffn_down", "ln2_g", "ln2_b")
    return (fin[2][0, 0], grad_x[None], *[grads[n] for n in names], *[deltas[n] for n in names],
            *[new_m[n] for n in names], *[new_v[n] for n in names])
```

```python
import functools

import jax
import jax.numpy as jnp
from jax import lax
from jax.experimental import pallas as pl
from jax.experimental.pallas import tpu as pltpu

F32, BF16 = jnp.float32, jnp.bfloat16
HI = lax.Precision.HIGHEST
MESH = pl.DeviceIdType.MESH

D = 1024
CHUNK = 64
LANES = 128
N_HEADS_A = 8
N_GROUPS_B = 4
B_INNER = 2048
CONV_DIM = 3072
D_FF = 2816
ALPHA = 2.0 ** 0.25
LN_EPS = 1e-5
RMS_EPS = 1e-6
ADAM_LR, ADAM_B1, ADAM_B2, ADAM_EPS, ADAM_WD, ADAM_STEP = 0.001, 0.9, 0.999, 1e-08, 0.01, 10

IN_ORIG = 11296
IN_PAD = 11520
COL_GA, COL_GB, COL_XBC, COL_Z, COL_DT = 4096, 5120, 6144, 9216, 11264
ORIG_Z, ORIG_XBC, ORIG_DT, ORIG_GA = 4096, 6144, 9216, 9248

SHARDED = ("w_in", "w_branch_a", "w_branch_b", "w_o", "w_ffn_gate", "w_ffn_up", "w_ffn_down")
TRANSPOSED = ("w_ffn_gate", "w_ffn_up")
VMEM_LIMIT = 56 * 1024 * 1024
BLOCK_BYTES = 2 * 1024 * 1024
_DIMS = {"nn": (((1,), (0,)), ((), ())), "nt": (((1,), (1,)), ((), ())), "tn": (((0,), (0,)), ((), ()))}


def _bd(a, b, mode):
    return lax.dot_general(a.astype(BF16), b.astype(BF16), _DIMS[mode], preferred_element_type=F32)


@functools.partial(jax.custom_vjp, nondiff_argnums=(2,))
def bdot(a, b, mode):
    return _bd(a, b, mode)


def _bdot_fwd(a, b, mode):
    return _bd(a, b, mode), (a, b)


def _bdot_bwd(mode, res, g):
    a, b = res
    if mode == "nn":
        return _bd(g, b, "nt"), _bd(a, g, "tn")
    if mode == "nt":
        return _bd(g, b, "nn"), _bd(g, a, "tn")
    return _bd(b, g, "nt"), _bd(a, g, "nn")


bdot.defvjp(_bdot_fwd, _bdot_bwd)


def hdot(a, b, mode="nn"):
    return lax.dot_general(a, b, _DIMS[mode], precision=HI, preferred_element_type=F32)


def _raw(a, b, mode):
    return lax.dot_general(a, b, _DIMS[mode], preferred_element_type=F32)


def _split(x, n):
    parts, rest = [], x
    for _ in range(n):
        p = rest.astype(BF16)
        parts.append(p)
        rest = rest - p.astype(F32)
    return parts


def _od(a, b, mode, exact):
    if exact == 1:
        e = b.astype(BF16)
        p = _split(a, 3)
        return (_raw(p[2], e, mode) + _raw(p[1], e, mode)) + _raw(p[0], e, mode)
    e = a.astype(BF16)
    p = _split(b, 3)
    return (_raw(e, p[2], mode) + _raw(e, p[1], mode)) + _raw(e, p[0], mode)


@functools.partial(jax.custom_vjp, nondiff_argnums=(2, 3))
def odot(a, b, mode, exact):
    return _od(a, b, mode, exact)


def _odot_fwd(a, b, mode, exact):
    return _od(a, b, mode, exact), (a, b)


def _odot_bwd(mode, exact, res, g):
    a, b = res
    if exact == 1:
        da = {"nn": lambda: _od(g, b, "nt", 1), "nt": lambda: _od(g, b, "nn", 1), "tn": lambda: _od(b, g, "nt", 0)}[mode]()
        return da, jnp.zeros_like(b)
    db = {"nn": lambda: _od(a, g, "tn", 0), "nt": lambda: _od(g, a, "tn", 1), "tn": lambda: _od(a, g, "nn", 0)}[mode]()
    return jnp.zeros_like(a), db


odot.defvjp(_odot_fwd, _odot_bwd)


_BDIMS = {"bnn": (((2,), (1,)), ((0,), (0,))), "bnt": (((2,), (2,)), ((0,), (0,))), "btn": (((1,), (1,)), ((0,), (0,)))}


def _braw(a, b, mode):
    return lax.dot_general(a, b, _BDIMS[mode], preferred_element_type=F32)


def _bdb(a, b, mode):
    return _braw(a.astype(BF16), b.astype(BF16), mode)


def _d3b(a, b, mode):
    ah, al = _split(a, 2)
    bh, bl = _split(b, 2)
    return _braw(ah, bh, mode) + (_braw(ah, bl, mode) + _braw(al, bh, mode))


def _batched_bwd(f):
    def bwd(mode, res, g):
        a, b = res
        if mode == "bnn":
            return f(g, b, "bnt"), f(a, g, "btn")
        if mode == "bnt":
            return f(g, b, "bnn"), f(g, a, "btn")
        return f(b, g, "bnt"), f(a, g, "bnn")
    return bwd


@functools.partial(jax.custom_vjp, nondiff_argnums=(2,))
def bdot_b(a, b, mode):
    return _bdb(a, b, mode)


bdot_b.defvjp(lambda a, b, mode: (_bdb(a, b, mode), (a, b)), _batched_bwd(_bdb))


@functools.partial(jax.custom_vjp, nondiff_argnums=(2,))
def dot3_b(a, b, mode):
    return _d3b(a, b, mode)


dot3_b.defvjp(lambda a, b, mode: (_d3b(a, b, mode), (a, b)), _batched_bwd(_d3b))


def _cum(tril3, x, mode):
    e = tril3.astype(BF16)
    p = _split(x, 3)
    return (_braw(e, p[2], mode) + _braw(e, p[1], mode)) + _braw(e, p[0], mode)


@jax.custom_vjp
def chunk_cumsum(tril3, x):
    return _cum(tril3, x, "bnn")


chunk_cumsum.defvjp(lambda t, x: (_cum(t, x, "bnn"), t), lambda t, g: (jnp.zeros_like(t), _cum(t, g, "btn")))


def _unstack(axis, n):
    @jax.custom_vjp
    def un(x):
        return tuple(lax.index_in_dim(x, i, axis, keepdims=False) for i in range(n))

    un.defvjp(lambda x: (un(x), None), lambda _, g: (jnp.stack(g, axis=axis),))
    return un


def _split_last(n, w):
    @jax.custom_vjp
    def sp(x):
        return tuple(x[..., i * w:(i + 1) * w] for i in range(n))

    sp.defvjp(lambda x: (sp(x), None), lambda _, g: (jnp.concatenate(g, axis=-1),))
    return sp


def sigmoid(x):
    return 1.0 / (1.0 + jnp.exp(-x))


def silu(x):
    return x * sigmoid(x)


def softplus(x):
    return jnp.maximum(x, 0.0) + jnp.log1p(jnp.exp(jnp.minimum(x, -x)))


def _ln(x):
    mu = jnp.mean(x, axis=-1, keepdims=True)
    xc = x - mu
    return xc * lax.rsqrt(jnp.mean(xc * xc, axis=-1, keepdims=True) + LN_EPS)


def _tril64():
    r = lax.broadcasted_iota(jnp.int32, (CHUNK, CHUNK), 0)
    c = lax.broadcasted_iota(jnp.int32, (CHUNK, CHUNK), 1)
    return (r >= c).astype(F32)


def hgrn_block(q, fl, iv, gr, st, lb, gn):
    tb = q.shape[0]
    nc = tb // CHUNK
    nh = N_HEADS_A
    heads = _split_last(nh, LANES)
    to4 = lambda a: jnp.stack(heads(a), axis=0).reshape(nh, nc, CHUNK, LANES)
    flat = lambda a: a.reshape(nh * nc, CHUNK, LANES)
    f = lb + (1.0 - lb) * sigmoid(fl)
    gl4, k4, qf4, v4, gr4 = to4(jnp.log(f)), to4(1.0 - f), to4(silu(q) * (128 ** -0.5)), to4(iv), to4(gr)
    tril = _tril64()
    b4 = chunk_cumsum(jnp.broadcast_to(tril[None], (nh * nc, CHUNK, CHUNK)), flat(gl4)).reshape(gl4.shape)
    blast = jnp.sum(gl4, axis=2, keepdims=True)
    ref = lax.stop_gradient(0.5 * blast)
    sc = dot3_b(flat(qf4 * jnp.exp(b4 - ref)), flat(k4 * jnp.exp(ref - b4)), "bnt") * tril
    o_intra = bdot_b(sc, flat(v4), "bnn").reshape(gl4.shape)
    chunks = _unstack(1, nc)
    qe, v_c, kd, dec = chunks(qf4 * jnp.exp(b4)), chunks(v4), chunks(k4 * jnp.exp(blast - b4)), chunks(jnp.exp(blast))
    o_inter = []
    for c in range(nc):
        o_inter.append(bdot_b(qe[c], st, "bnt"))
        st = st * dec[c] + bdot_b(v_c[c], kd[c], "btn")
    o = o_intra + jnp.stack(o_inter, axis=1)
    on = o * lax.rsqrt(jnp.mean(o * o, axis=-1, keepdims=True) + RMS_EPS) * gn
    out = (on * silu(gr4)).reshape(nh, tb, LANES)
    return jnp.concatenate(_unstack(0, nh)(out), axis=1), st


def ssd_consts(g):
    i32 = jnp.int32
    ej = lax.broadcasted_iota(i32, (LANES, 512), 0)
    ec = lax.broadcasted_iota(i32, (LANES, 512), 1)
    expand = (ej == g * 8 + (ec >> 6)).astype(F32)
    ts = lax.broadcasted_iota(i32, (CHUNK, 512), 0)
    tc = lax.broadcasted_iota(i32, (CHUNK, 512), 1)
    itile = (ts == (tc & 63)).astype(F32)
    maskall = ts >= (tc & 63)
    br = lax.broadcasted_iota(i32, (256, 256), 0)
    bc = lax.broadcasted_iota(i32, (256, 256), 1)
    blockmask = ((br >> 6) == (bc >> 6)).astype(F32)
    return expand, itile, maskall, blockmask, _tril64()


def ssd_block(x, bm, cm, dt, z, st, dtb, alog, dsk, nw, cs):
    expand, itile, maskall, blockmask, tril = cs
    tb = x.shape[0]
    nc = tb // CHUNK
    delta_heads = softplus(dt + dtb)
    delta = odot(delta_heads, expand, "nn", 1)
    a = odot(-jnp.exp(alog) * delta_heads, expand, "nn", 1)
    xdt = x * delta
    by_chunk = lambda v: v.reshape(nc, CHUNK, v.shape[-1])
    a3, xdt3, bm3, cm3 = by_chunk(a), by_chunk(xdt), by_chunk(bm), by_chunk(cm)
    acum3 = chunk_cumsum(jnp.broadcast_to(tril[None], (nc, CHUNK, CHUNK)), a3)
    alast3 = jnp.sum(a3, axis=1, keepdims=True)
    cb3 = bdot_b(cm3, jnp.concatenate([bm3] * 8, axis=1), "bnt")
    arow3 = jnp.sum(acum3 * itile, axis=1, keepdims=True)
    dec3 = jnp.exp(jnp.where(maskall, acum3 - arow3, -1e30))
    halves = _split_last(2, 256)
    intra = [bdot_b(m, jnp.concatenate([xh] * 4, axis=1) * blockmask, "bnn")
             for m, xh in zip(halves(cb3 * dec3), halves(xdt3))]
    chunks = _unstack(0, nc)
    cm_c, bm_c, xw_c, dec_c = chunks(cm3), chunks(bm3), chunks(xdt3 * jnp.exp(alast3 - acum3)), chunks(jnp.exp(alast3))
    inter = []
    for c in range(nc):
        inter.append(bdot(cm_c[c], st, "nn"))
        st = st * dec_c[c] + bdot(bm_c[c], xw_c[c], "tn")
    st_new = st
    y = (jnp.concatenate(intra, axis=-1) + jnp.stack(inter, axis=0) * jnp.exp(acum3)).reshape(tb, 512)
    yz = (y + x * dsk) * silu(z)
    return yz * lax.rsqrt(jnp.mean(yz * yz, axis=-1, keepdims=True) + RMS_EPS) * nw, st_new


def adamw(w, g, m, v):
    m = ADAM_B1 * m + (1.0 - ADAM_B1) * g
    v = ADAM_B2 * v + (1.0 - ADAM_B2) * jnp.square(g)
    m_hat = m / (1.0 - ADAM_B1 ** ADAM_STEP)
    v_hat = v / (1.0 - ADAM_B2 ** ADAM_STEP)
    return -ADAM_LR * (m_hat / (jnp.sqrt(v_hat) + ADAM_EPS) + ADAM_WD * w), m, v


def _pick(n, cands):
    for c in cands:
        if n % c == 0:
            return c
    return n


def _params(sem):
    return pltpu.CompilerParams(dimension_semantics=sem, vmem_limit_bytes=VMEM_LIMIT)


MATMUL_VMEM_BUDGET = 50 * 1024 * 1024
MATMUL_MIN_STEPS = 4


def matmul(a, b, mode, out_dtype, name, after=None):
    if mode == "nn":
        (m, k), n = a.shape, b.shape[1]
    elif mode == "nt":
        (m, k), n = a.shape, b.shape[0]
    else:
        (k, m), n = a.shape, b.shape[1]
    tk = _pick(k, (2304, 2048, 1408, 1024, 768, 512, 256, 128))
    nk = k // tk
    a_bytes, b_bytes, out_bytes = a.dtype.itemsize, b.dtype.itemsize, jnp.dtype(out_dtype).itemsize

    def vmem(tm_, tn_):
        blocks = 2 * (tm_ * tk * a_bytes + tk * tn_ * b_bytes + tm_ * tn_ * out_bytes)
        return blocks + (tm_ * tn_ * 4 if nk > 1 else 0)

    def traffic(tm_, tn_):
        return (m // tm_) * k * n * b_bytes + (n // tn_ if nk > 1 else 1) * m * k * a_bytes

    sizes = (2304, 2048, 1920, 1408, 1024, 768, 512, 256, 128)
    tiles = [(tm_, tn_) for tm_ in sizes if m % tm_ == 0 for tn_ in sizes if n % tn_ == 0
             if vmem(tm_, tn_) <= MATMUL_VMEM_BUDGET] or [(m, n)]
    pipelined = [t for t in tiles if (m // t[0]) * (n // t[1]) * nk >= MATMUL_MIN_STEPS]
    tm, tn = min(pipelined or tiles, key=lambda t: (traffic(*t), -t[0] * t[1]))
    a_spec = pl.BlockSpec((tk, tm), lambda i, j, kk: (kk, i)) if mode == "tn" else pl.BlockSpec((tm, tk), lambda i, j, kk: (i, kk))
    b_spec = pl.BlockSpec((tn, tk), lambda i, j, kk: (j, kk)) if mode == "nt" else pl.BlockSpec((tk, tn), lambda i, j, kk: (kk, j))

    order = [] if after is None else [after]

    def body(a_ref, b_ref, *rest):
        o_ref, *acc = rest[len(order):]
        part = _bd(a_ref[...], b_ref[...], mode)
        if nk == 1:
            o_ref[...] = part.astype(o_ref.dtype)
            return
        acc_ref, = acc
        kk = pl.program_id(2)

        @pl.when(kk == 0)
        def _():
            acc_ref[...] = part

        @pl.when(jnp.logical_and(kk > 0, kk < nk - 1))
        def _():
            acc_ref[...] += part

        @pl.when(kk == nk - 1)
        def _():
            o_ref[...] = (acc_ref[...] + part).astype(o_ref.dtype)

    return pl.pallas_call(
        body, name=name, grid=(m // tm, n // tn, nk),
        in_specs=[a_spec, b_spec] + [pl.BlockSpec(memory_space=pl.ANY) for _ in order],
        out_specs=pl.BlockSpec((tm, tn), lambda i, j, kk: (i, j)),
        out_shape=jax.ShapeDtypeStruct((m, n), out_dtype),
        scratch_shapes=[pltpu.VMEM((tm, tn), F32)] if nk > 1 else [],
        compiler_params=_params(("parallel", "parallel", "arbitrary")),
    )(a, b, *order)


def rowwise(name, fn, rows, consts, out_rows, out_accs=(), tm_max=256, into=None, new_wide=None):
    t = rows[0][0].shape[0]
    tm = _pick(t, (tm_max, 128, 64, 32, 16, 8))
    n_r, n_c, n_o = len(rows), len(consts), len(out_rows)
    n_alias = 0 if into is None else 1

    def body(*refs):
        r_in = [r[...] for r in refs[:n_r]]
        c_in = [r[...] for r in refs[n_r:n_r + n_c]]
        refs = refs[:n_r + n_c] + refs[n_r + n_c + n_alias:]
        o_refs = refs[n_r + n_c:n_r + n_c + n_o]
        a_refs = refs[n_r + n_c + n_o:]
        ro, ao = fn(r_in, c_in)
        for ref, val in zip(o_refs, ro, strict=True):
            ref[...] = val.astype(ref.dtype)
        if a_refs:
            @pl.when(pl.program_id(0) == 0)
            def _():
                for ref in a_refs:
                    ref[...] = jnp.zeros_like(ref)

            for ref, val in zip(a_refs, ao, strict=True):
                ref[...] += val

    in_specs = [pl.BlockSpec((tm, w), functools.partial(lambda i, cb: (i, cb), cb=cb)) for _, w, cb in rows]
    in_specs += [pl.BlockSpec(c.shape, lambda i: (0, 0)) for c in consts]
    out_specs = [pl.BlockSpec((tm, w), lambda i: (i, 0)) for w, _ in out_rows]
    out_specs += [pl.BlockSpec(s, lambda i: (0, 0)) for s in out_accs]
    out_shape = [jax.ShapeDtypeStruct((t, w), dt) for w, dt in out_rows]
    out_shape += [jax.ShapeDtypeStruct(s, F32) for s in out_accs]
    operands = [r[0] for r in rows] + list(consts)
    aliases = {}
    if into is not None:
        target, cb = into
        in_specs.append(pl.BlockSpec(memory_space=pl.ANY))
        operands.append(target)
        out_specs[0] = pl.BlockSpec((tm, out_rows[0][0]), lambda i: (i, cb))
        out_shape[0] = jax.ShapeDtypeStruct(target.shape, target.dtype)
        aliases = {len(operands) - 1: 0}
    if new_wide is not None:
        width, cb = new_wide
        out_specs[0] = pl.BlockSpec((tm, out_rows[0][0]), lambda i: (i, cb))
        out_shape[0] = jax.ShapeDtypeStruct((t, width), out_rows[0][1])
    return pl.pallas_call(
        body, name=name, grid=(t // tm,), in_specs=in_specs, out_specs=out_specs, out_shape=out_shape,
        input_output_aliases=aliases, compiler_params=_params(("arbitrary",)),
    )(*operands)


def _full(a):
    return (a, a.shape[1], 0)


def _time_block(t):
    return _pick(t, (256, 128, 64))


def _quarters(ref):
    return [ref[:, seg * D:(seg + 1) * D] for seg in range(4)]


def hgrn_forward(proj, lb, gn):
    t = proj.shape[0]
    tb = _time_block(t)
    nb = t // tb

    def body(qfig_ref, lb_ref, gn_ref, o_ref, st_ref, state):
        @pl.when(pl.program_id(0) == 0)
        def _():
            state[...] = jnp.zeros_like(state)

        st = state[...]
        st_ref[...] = st
        out, st_new = hgrn_block(*_quarters(qfig_ref), st, lb_ref[...], gn_ref[...])
        o_ref[...] = out.astype(o_ref.dtype)
        state[...] = st_new

    return pl.pallas_call(
        body, name="hgrn_forward", grid=(nb,),
        in_specs=[pl.BlockSpec((tb, 4 * D), lambda j: (j, 0)),
                  pl.BlockSpec((1, D), lambda j: (0, 0)), pl.BlockSpec((1, LANES), lambda j: (0, 0))],
        out_specs=[pl.BlockSpec((tb, D), lambda j: (j, 0)),
                   pl.BlockSpec((None, N_HEADS_A, LANES, LANES), lambda j: (j, 0, 0, 0))],
        out_shape=[jax.ShapeDtypeStruct((t, D), BF16),
                   jax.ShapeDtypeStruct((nb, N_HEADS_A, LANES, LANES), F32)],
        scratch_shapes=[pltpu.VMEM((N_HEADS_A, LANES, LANES), F32)],
        compiler_params=_params(("arbitrary",)),
    )(proj, lb, gn)


def hgrn_backward(proj, states, d_out, lb, gn, d_proj):
    t = proj.shape[0]
    tb = _time_block(t)
    nb = t // tb

    def body(qfig_ref, st_ref, do_ref, lb_ref, gn_ref, _, dqfig_ref, dlb_ref, dgn_ref, d_state):
        @pl.when(pl.program_id(0) == 0)
        def _():
            d_state[...] = jnp.zeros_like(d_state)
            dlb_ref[...] = jnp.zeros_like(dlb_ref)
            dgn_ref[...] = jnp.zeros_like(dgn_ref)

        _, vjp = jax.vjp(hgrn_block, *_quarters(qfig_ref), st_ref[...], lb_ref[...], gn_ref[...])
        dq, df, di, dg, dst, dlb, dgn = vjp((do_ref[...], d_state[...]))
        for seg, val in enumerate((dq, df, di, dg)):
            dqfig_ref[:, seg * D:(seg + 1) * D] = val.astype(dqfig_ref.dtype)
        d_state[...] = dst
        dlb_ref[...] += dlb
        dgn_ref[...] += dgn

    rev = lambda j: nb - 1 - j
    return pl.pallas_call(
        body, name="hgrn_backward", grid=(nb,),
        in_specs=[pl.BlockSpec((tb, 4 * D), lambda j: (rev(j), 0)),
                  pl.BlockSpec((None, N_HEADS_A, LANES, LANES), lambda j: (rev(j), 0, 0, 0)),
                  pl.BlockSpec((tb, D), lambda j: (rev(j), 0)),
                  pl.BlockSpec((1, D), lambda j: (0, 0)), pl.BlockSpec((1, LANES), lambda j: (0, 0)),
                  pl.BlockSpec(memory_space=pl.ANY)],
        out_specs=[pl.BlockSpec((tb, 4 * D), lambda j: (rev(j), 0)),
                   pl.BlockSpec((1, D), lambda j: (0, 0)), pl.BlockSpec((1, LANES), lambda j: (0, 0))],
        out_shape=[jax.ShapeDtypeStruct(d_proj.shape, d_proj.dtype), jax.ShapeDtypeStruct((1, D), F32),
                   jax.ShapeDtypeStruct((1, LANES), F32)],
        input_output_aliases={5: 0},
        scratch_shapes=[pltpu.VMEM((N_HEADS_A, LANES, LANES), F32)],
        compiler_params=_params(("arbitrary",)),
    )(proj, states, d_out, lb, gn, d_proj)


def _ssd_in_specs(tb, tmap):
    return [pl.BlockSpec((tb, 512), lambda g, j: (tmap(j), g)),
            pl.BlockSpec((tb, LANES), lambda g, j: (tmap(j), 16 + g)),
            pl.BlockSpec((tb, LANES), lambda g, j: (tmap(j), 20 + g)),
            pl.BlockSpec((tb, LANES), lambda g, j: (tmap(j), COL_DT // LANES)),
            pl.BlockSpec((tb, 512), lambda g, j: (tmap(j), COL_Z // 512 + g))]


def ssd_forward(xc, proj, dtb, alog, dsk, nw):
    t = proj.shape[0]
    tb = _time_block(t)
    nb = t // tb

    def body(x_ref, b_ref, c_ref, dt_ref, z_ref, dtb_ref, alog_ref, dsk_ref, nw_ref, o_ref, st_ref, state):
        @pl.when(pl.program_id(1) == 0)
        def _():
            state[...] = jnp.zeros_like(state)

        st = state[...]
        st_ref[...] = st
        out, st_new = ssd_block(x_ref[...], b_ref[...], c_ref[...], dt_ref[...], z_ref[...], st,
                                dtb_ref[...], alog_ref[...], dsk_ref[...], nw_ref[...], ssd_consts(pl.program_id(0)))
        o_ref[...] = out.astype(o_ref.dtype)
        state[...] = st_new

    vec = pl.BlockSpec((1, 512), lambda g, j: (0, g))
    heads = pl.BlockSpec((1, LANES), lambda g, j: (0, 0))
    return pl.pallas_call(
        body, name="ssd_forward", grid=(N_GROUPS_B, nb),
        in_specs=_ssd_in_specs(tb, lambda j: j) + [heads, heads, vec, vec],
        out_specs=[pl.BlockSpec((tb, 512), lambda g, j: (j, g)),
                   pl.BlockSpec((None, None, LANES, 512), lambda g, j: (j, g, 0, 0))],
        out_shape=[jax.ShapeDtypeStruct((t, B_INNER), BF16),
                   jax.ShapeDtypeStruct((nb, N_GROUPS_B, LANES, 512), F32)],
        scratch_shapes=[pltpu.VMEM((LANES, 512), F32)],
        compiler_params=_params(("arbitrary", "arbitrary")),
    )(xc, xc, xc, proj, proj, dtb, alog, dsk, nw)


def ssd_backward(xc, proj, states, d_out, dtb, alog, dsk, nw, d_proj):
    t = proj.shape[0]
    tb = _time_block(t)
    nb = t // tb
    rev = lambda j: nb - 1 - j

    def body(x_ref, b_ref, c_ref, dt_ref, z_ref, st_ref, do_ref, dtb_ref, alog_ref, dsk_ref, nw_ref, _,
             dx_ref, db_ref, dc_ref, ddt_ref, dz_ref, ddtb_ref, dalog_ref, ddsk_ref, dnw_ref, d_state):
        accs = (ddtb_ref, dalog_ref, ddsk_ref, dnw_ref)

        @pl.when(pl.program_id(1) == 0)
        def _():
            d_state[...] = jnp.zeros_like(d_state)
            for ref in accs:
                ref[...] = jnp.zeros_like(ref)

        cs = ssd_consts(pl.program_id(0))
        fn = lambda *a: ssd_block(*a, cs)
        _, vjp = jax.vjp(fn, x_ref[...], b_ref[...], c_ref[...], dt_ref[...], z_ref[...], st_ref[...],
                         dtb_ref[...], alog_ref[...], dsk_ref[...], nw_ref[...])
        dx, db, dc, ddt, dz, dst, *dpar = vjp((do_ref[...], d_state[...]))
        dx_ref[...] = dx
        db_ref[...] = db
        dc_ref[...] = dc
        ddt_ref[...] = ddt
        dz_ref[...] = dz.astype(dz_ref.dtype)
        d_state[...] = dst
        for ref, val in zip(accs, dpar, strict=True):
            ref[...] += val

    vec = pl.BlockSpec((1, 512), lambda g, j: (0, g))
    heads = pl.BlockSpec((1, LANES), lambda g, j: (0, 0))
    acc = pl.BlockSpec((None, 1, 512), lambda g, j: (g, 0, 0))
    acc_heads = pl.BlockSpec((None, 1, LANES), lambda g, j: (g, 0, 0))
    return pl.pallas_call(
        body, name="ssd_backward", grid=(N_GROUPS_B, nb),
        in_specs=_ssd_in_specs(tb, rev)
        + [pl.BlockSpec((None, None, LANES, 512), lambda g, j: (rev(j), g, 0, 0)),
           pl.BlockSpec((tb, 512), lambda g, j: (rev(j), g))] + [heads, heads, vec, vec] + [pl.BlockSpec(memory_space=pl.ANY)],
        out_specs=[pl.BlockSpec((tb, 512), lambda g, j: (rev(j), g)),
                   pl.BlockSpec((tb, LANES), lambda g, j: (rev(j), g)),
                   pl.BlockSpec((tb, LANES), lambda g, j: (rev(j), g)),
                   pl.BlockSpec((None, tb, LANES), lambda g, j: (g, rev(j), 0)),
                   pl.BlockSpec((tb, 512), lambda g, j: (rev(j), COL_Z // 512 + g)), acc_heads, acc_heads, acc, acc],
        out_shape=[jax.ShapeDtypeStruct((t, B_INNER), F32), jax.ShapeDtypeStruct((t, 512), F32),
                   jax.ShapeDtypeStruct((t, 512), F32), jax.ShapeDtypeStruct((N_GROUPS_B, t, LANES), F32),
                   jax.ShapeDtypeStruct(d_proj.shape, d_proj.dtype)]
        + [jax.ShapeDtypeStruct((N_GROUPS_B, 1, LANES), F32)] * 2 + [jax.ShapeDtypeStruct((N_GROUPS_B, 1, 512), F32)] * 2,
        input_output_aliases={11: 4},
        scratch_shapes=[pltpu.VMEM((LANES, 512), F32)],
        compiler_params=_params(("arbitrary", "arbitrary")),
    )(xc, xc, xc, proj, proj, states, d_out, dtb, alog, dsk, nw, d_proj)


CONV_HALO = 8


def _shift_down(halo_then_tile, s, tm):
    if s == 0:
        return halo_then_tile[CONV_HALO:CONV_HALO + tm]
    return pltpu.roll(halo_then_tile, s, 0)[CONV_HALO:CONV_HALO + tm]


def _conv_pre(cur, prev, w, b, tm):
    stacked = jnp.concatenate([prev, cur], axis=0)
    taps = [_shift_down(stacked, 3 - j, tm) for j in range(4)]
    pre = b + taps[0] * w[0:1] + taps[1] * w[1:2] + taps[2] * w[2:3] + taps[3] * w[3:4]
    return pre, taps


def _conv_specs(t, tm):
    per = tm // CONV_HALO
    cur = pl.BlockSpec((tm, CONV_DIM), lambda i: (i, COL_XBC // CONV_DIM))
    prev = pl.BlockSpec((CONV_HALO, CONV_DIM), lambda i: (jnp.maximum(i * per - 1, 0), COL_XBC // CONV_DIM))
    return cur, prev


def conv_forward(proj, w, b):
    t = proj.shape[0]
    tm = _pick(t, (256, 128, 64))

    def body(cur_ref, prev_ref, w_ref, b_ref, o_ref):
        prev = jnp.where(pl.program_id(0) == 0, 0.0, prev_ref[...])
        pre, _ = _conv_pre(cur_ref[...], prev, w_ref[...], b_ref[...], tm)
        o_ref[...] = silu(pre)

    cur, prev = _conv_specs(t, tm)
    return pl.pallas_call(
        body, name="conv_forward", grid=(t // tm,),
        in_specs=[cur, prev, pl.BlockSpec((4, CONV_DIM), lambda i: (0, 0)), pl.BlockSpec((1, CONV_DIM), lambda i: (0, 0))],
        out_specs=pl.BlockSpec((tm, CONV_DIM), lambda i: (i, 0)),
        out_shape=jax.ShapeDtypeStruct((t, CONV_DIM), F32),
        compiler_params=_params(("arbitrary",)),
    )(proj, proj, w, b)


def conv_backward(proj, dx, db_, dc_, w, b, d_proj):
    t = proj.shape[0]
    tm = _pick(t, (256, 128, 64))
    per = tm // CONV_HALO
    nt = t // tm
    rev = lambda i: nt - 1 - i

    def body(cur_ref, prev_ref, dx_ref, dbm_ref, dcm_ref, w_ref, b_ref, _, o_ref, dw_ref, dbias_ref, later):
        @pl.when(pl.program_id(0) == 0)
        def _():
            dw_ref[...] = jnp.zeros_like(dw_ref)
            dbias_ref[...] = jnp.zeros_like(dbias_ref)
            later[...] = jnp.zeros_like(later)

        first_tile = pl.program_id(0) == nt - 1
        for lo, hi, src in ((0, B_INNER, dx_ref), (B_INNER, B_INNER + 512, dbm_ref), (B_INNER + 512, CONV_DIM, dcm_ref)):
            cols = slice(lo, hi)
            prev = jnp.where(first_tile, 0.0, prev_ref[:, cols])
            w_ = w_ref[:, cols]
            pre, taps = _conv_pre(cur_ref[:, cols], prev, w_, b_ref[:, cols], tm)
            sg = sigmoid(pre)
            dpre = src[...] * (sg * (1.0 + pre * (1.0 - sg)))
            dbias_ref[:, cols] += jnp.sum(dpre, axis=0, keepdims=True)
            for j in range(4):
                dw_ref[j:j + 1, cols] += jnp.sum(dpre * taps[j], axis=0, keepdims=True)
            stacked = jnp.concatenate([dpre, later[:, cols]], axis=0)
            acc = dpre * w_[3:4]
            for j in range(3):
                acc = acc + pltpu.roll(stacked, tm + CONV_HALO - (3 - j), 0)[0:tm] * w_[j:j + 1]
            o_ref[:, cols] = acc.astype(o_ref.dtype)
            later[:, cols] = dpre[0:CONV_HALO]

    row = lambda w_: pl.BlockSpec((tm, w_), lambda i: (rev(i), 0))
    whole = lambda r: pl.BlockSpec((r, CONV_DIM), lambda i: (0, 0))
    return pl.pallas_call(
        body, name="conv_backward", grid=(nt,),
        in_specs=[pl.BlockSpec((tm, CONV_DIM), lambda i: (rev(i), COL_XBC // CONV_DIM)),
                  pl.BlockSpec((CONV_HALO, CONV_DIM), lambda i: (jnp.maximum(rev(i) * per - 1, 0), COL_XBC // CONV_DIM)),
                  row(B_INNER), row(512), row(512), whole(4), whole(1), pl.BlockSpec(memory_space=pl.ANY)],
        out_specs=[pl.BlockSpec((tm, CONV_DIM), lambda i: (rev(i), COL_XBC // CONV_DIM)), whole(4), whole(1)],
        out_shape=[jax.ShapeDtypeStruct(d_proj.shape, d_proj.dtype), jax.ShapeDtypeStruct((4, CONV_DIM), F32),
                   jax.ShapeDtypeStruct((1, CONV_DIM), F32)],
        input_output_aliases={7: 0},
        scratch_shapes=[pltpu.VMEM((CONV_HALO, CONV_DIM), F32)],
        compiler_params=_params(("arbitrary",)),
    )(proj, proj, dx, db_, dc_, w, b, d_proj)


def stage_modulate(x, sc, sh):
    return _ln(x) * (1.0 + sc) + sh


def stage_merge(ga, gb, ya, yb):
    return sigmoid(ga) * ya + sigmoid(gb) * yb


def stage_post_mixer(x, h, g1, ln_g, ln_b, sc2, sh2):
    x1 = _ln(ALPHA * x + g1 * h) * ln_g + ln_b
    return x1, _ln(x1) * (1.0 + sc2) + sh2


def stage_swiglu(a, b):
    return silu(a) * b


def stage_loss(x1, hf, tgt, g2, ln_g, ln_b):
    x2 = _ln(ALPHA * x1 + g2 * hf) * ln_g + ln_b
    return 0.5 * jnp.sum(jnp.mean(jnp.square(x2 - tgt), axis=-1, keepdims=True), axis=0, keepdims=True)


def local_step(x, tgt, mod, wts, small, early=None, mid=None, late=None, last=None):
    sh1, sc1, g1, sh2, sc2, g2 = mod
    lb, gn, conv_w, conv_b, dtb, alog, dsk, nw, ln1_g, ln1_b, ln2_g, ln2_b = small
    vec = (1, D)

    (u1,) = rowwise("modulate1", lambda r, c: ((stage_modulate(r[0], *c),), ()), [_full(x)], [sc1, sh1], [(D, BF16)])
    w_in = wts.input_projection(u1)
    proj = matmul(u1, w_in, "nn", F32, "in_proj")
    ya_in, st_a = hgrn_forward(proj, lb, gn + wts.start_rest(proj)[0:1])
    xc = conv_forward(proj, conv_w, conv_b)
    w_a, w_b, w_o, w_gu, w_d = wts.rest(xc)
    yb_in, st_b = ssd_forward(xc, proj, dtb, alog, dsk, nw)
    ya = matmul(ya_in, w_a, "nn", F32, "branch_a")
    yb = matmul(yb_in, w_b, "nn", F32, "branch_b")
    gate_rows = [(proj, D, COL_GA // D), (proj, D, COL_GB // D), _full(ya), _full(yb)]
    (merged,) = rowwise("merge", lambda r, c: ((stage_merge(*r),), ()), gate_rows, [], [(D, BF16)])
    h = matmul(merged, w_o, "nn", F32, "out_proj")
    post_consts = [g1, ln1_g, ln1_b, sc2, sh2]
    x1, u2 = rowwise("post_mixer", lambda r, c: (stage_post_mixer(*r, *c), ()), [_full(x), _full(h)], post_consts,
                     [(D, F32), (D, BF16)])
    ab = matmul(u2, w_gu, "nt", F32, "ffn_in")
    (p,) = rowwise("swiglu", lambda r, c: ((stage_swiglu(*r),), ()), [(ab, D_FF, 0), (ab, D_FF, 1)], [], [(D_FF, BF16)])
    hf = matmul(p, w_d, "nn", F32, "ffn_out")

    def loss_bwd(r, c):
        loss, vjp = jax.vjp(stage_loss, *r, *c)
        dx1, dhf, _, dg2, dlg, dlb_ = vjp(jnp.ones((1, 1), F32))
        return (dx1, dhf), (loss, dg2, dlg, dlb_)

    dx1, dhf, loss, dg2, dln2_g, dln2_b = rowwise(
        "loss_backward", loss_bwd, [_full(x1), _full(hf), _full(tgt)], [g2, ln2_g, ln2_b],
        [(D, F32), (D, BF16)], [(1, 1), vec, vec, vec])
    dp = matmul(dhf, w_d, "nt", F32, "ffn_out_dx")
    dw_d = matmul(p, dhf, "tn", F32, "ffn_out_dw")

    def swiglu_bwd(r, c):
        _, vjp = jax.vjp(stage_swiglu, r[0], r[1])
        da, db_ = vjp(r[2])
        return (jnp.concatenate([da, db_], axis=1),), ()

    (dab,) = rowwise("swiglu_backward", swiglu_bwd, [(ab, D_FF, 0), (ab, D_FF, 1), _full(dp)], [], [(2 * D_FF, BF16)])
    du2 = matmul(dab, w_gu, "nn", F32, "ffn_in_dx")
    dw_gu = matmul(dab, u2, "tn", F32, "ffn_in_dw")

    def post_bwd(r, c):
        _, vjp = jax.vjp(stage_post_mixer, r[0], r[1], *c)
        dx, dh, *dc = vjp((r[2], r[3]))
        return (dx, dh), tuple(dc)

    dx_a, dh, dg1, dln1_g, dln1_b, dsc2, dsh2 = rowwise(
        "post_mixer_backward", post_bwd, [_full(x), _full(h), _full(dx1), _full(du2)], post_consts,
        [(D, F32), (D, BF16)], [vec] * 5)
    dmerged = matmul(dh, w_o, "nt", F32, "out_proj_dx")
    dw_o = matmul(merged, dh, "tn", F32, "out_proj_dw")

    def merge_bwd(r, c):
        _, vjp = jax.vjp(stage_merge, *r[:4])
        dga, dgb, dya, dyb = vjp(r[4])
        return (jnp.concatenate([dga, dgb], axis=1), dya, dyb), ()

    dproj, dya, dyb = rowwise("merge_backward", merge_bwd, gate_rows + [_full(dmerged)], [],
                              [(2 * D, BF16), (D, BF16), (D, BF16)], new_wide=(IN_PAD, COL_GA // (2 * D)))
    dya_in = matmul(dya, w_a, "nt", F32, "branch_a_dx")
    dw_a = matmul(ya_in, dya, "tn", F32, "branch_a_dw")
    dyb_in = matmul(dyb, w_b, "nt", F32, "branch_b_dx")
    dw_b = matmul(yb_in, dyb, "tn", F32, "branch_b_dw")
    gn_after = gn if early is None else gn + early((dw_a, dw_b, dw_o, dw_gu, dw_d))[0:1]
    dproj, dlb, dgn = hgrn_backward(proj, st_a, dya_in, lb, gn_after, dproj)
    dtb_after = dtb if mid is None else dtb + mid(dlb)[0:1, 0:1]
    dxs, dbm, dcm, ddt, dproj, ddtb, dalog, ddsk, dnw = ssd_backward(xc, proj, st_b, dyb_in, dtb_after, alog, dsk, nw, dproj)
    dproj, dconv_w, dconv_b = conv_backward(proj, dxs, dbm, dcm, conv_w, conv_b, dproj)
    if late is not None:
        late(dconv_b)
    t = x.shape[0]
    tail = jnp.concatenate([jnp.sum(ddt, axis=0).astype(BF16), jnp.zeros((t, IN_PAD - COL_DT - LANES), BF16)], axis=1)
    dproj = lax.dynamic_update_slice(dproj, tail, (0, COL_DT))
    dw_in = matmul(u1, dproj, "tn", F32, "in_proj_dw")
    du1 = matmul(dproj, w_in, "nt", F32, "in_proj_dx", after=None if last is None else last(dw_in))

    def mod_bwd(r, c):
        _, vjp = jax.vjp(stage_modulate, r[0], *c)
        dx, dsc, dsh = vjp(r[1])
        return (dx + r[2],), (dsc, dsh)

    grad_x, dsc1, dsh1 = rowwise("modulate1_backward", mod_bwd, [_full(x), _full(du1), _full(dx_a)], [sc1, sh1],
                                 [(D, F32)], [vec, vec])
    d_mod = (dsh1, dsc1, dg1, dsh2, dsc2, dg2)
    d_wts = (dw_in, dw_a, dw_b, dw_o, dw_gu, dw_d)
    d_small = (dlb, dgn, dconv_w, dconv_b, jnp.sum(ddtb, axis=0),
               jnp.sum(dalog, axis=0), ddsk.reshape(1, B_INNER), dnw.reshape(1, B_INNER),
               dln1_g, dln1_b, dln2_g, dln2_b)
    return loss, grad_x, d_mod, d_wts, d_small


HBM = pl.BlockSpec(memory_space=pltpu.HBM)
SEM = pl.BlockSpec(memory_space=pltpu.SEMAPHORE)
DATAFLOW = pltpu.SideEffectType.DATAFLOW_SIDE_EFFECTING


def _place():
    return lax.axis_index("x"), lax.axis_index("y"), lax.axis_index("c")


def _other_chips(x, y):
    return [(1 - x, y), (x, 1 - y), (1 - x, 1 - y)]


def _remote(src, dst, send_sem, recv_sem, device):
    return pltpu.make_async_remote_copy(src_ref=src, dst_ref=dst, send_sem=send_sem, recv_sem=recv_sem,
                                        device_id=device, device_id_type=MESH)


def gather_rows(v, name):
    n = v.shape[1]

    def body(v_ref, out_ref, send_sems, recv_sems, local_sem):
        x, y, c = _place()
        mine = pltpu.make_async_copy(v_ref, out_ref.at[4 * x + 2 * y + c], local_sem)
        mine.start()
        sends, recvs = [], []
        for m in range(1, 8):
            px = 1 - x if m & 4 else x
            py = 1 - y if m & 2 else y
            pc = 1 - c if m & 1 else c
            sends.append(_remote(v_ref, out_ref.at[4 * x + 2 * y + c], send_sems.at[m - 1], recv_sems.at[m - 1], (px, py, pc)))
            recvs.append(_remote(v_ref, out_ref.at[4 * px + 2 * py + pc], send_sems.at[m - 1], recv_sems.at[m - 1], (px, py, pc)))
        for cp in sends:
            cp.start()
        for cp in recvs:
            cp.wait_recv()
        for cp in sends:
            cp.wait_send()
        mine.wait()

    return pl.pallas_call(
        body, name=name, in_specs=[HBM], out_specs=HBM,
        out_shape=jax.ShapeDtypeStruct((8, 1, n), v.dtype),
        scratch_shapes=[pltpu.SemaphoreType.DMA((7,)), pltpu.SemaphoreType.DMA((7,)), pltpu.SemaphoreType.DMA],
    )(v)


def exchange_rows(part, name):
    w = part.shape[2]

    def body(p_ref, out_ref, send_sems, recv_sems, local_sem):
        x, y, c = _place()
        k = 2 * x + y
        mine = pltpu.make_async_copy(p_ref.at[4 * x + 2 * y + c], out_ref.at[k], local_sem)
        mine.start()
        sends, recvs = [], []
        for j, (px, py) in enumerate(_other_chips(x, y)):
            sends.append(_remote(p_ref.at[4 * px + 2 * py + c], out_ref.at[k], send_sems.at[j], recv_sems.at[j], (px, py, c)))
            recvs.append(_remote(p_ref.at[4 * px + 2 * py + c], out_ref.at[2 * px + py], send_sems.at[j], recv_sems.at[j], (px, py, c)))
        for cp in sends:
            cp.start()
        for cp in recvs:
            cp.wait_recv()
        for cp in sends:
            cp.wait_send()
        mine.wait()

    return pl.pallas_call(
        body, name=name, in_specs=[HBM], out_specs=HBM,
        out_shape=jax.ShapeDtypeStruct((4, 1, w), part.dtype),
        scratch_shapes=[pltpu.SemaphoreType.DMA((3,)), pltpu.SemaphoreType.DMA((3,)), pltpu.SemaphoreType.DMA],
    )(part)


def _half_of_slot(ref, rows, px, py, pc):
    return ref.at[2 * px + py, pl.ds(pc * (rows // 2), rows // 2), :]


def gather_start(shards, after):
    n = len(shards)

    def body(*refs):
        w_refs, land_refs = refs[:n], refs[n:2 * n]
        send_a, recv_a, send_b, recv_b = refs[2 * n + 1:2 * n + 5]
        token = refs[-1]
        x, y, c = _place()
        for i in range(n):
            rows = shards[i].shape[0]
            for j, (px, py) in enumerate(_other_chips(x, y)):
                sems = (send_a.at[j], recv_a.at[j]) if i == 0 else (send_b.at[j * (n - 1) + i - 1], recv_b.at[j * (n - 1) + i - 1])
                _remote(w_refs[i].at[pl.ds(c * (rows // 2), rows // 2), :], _half_of_slot(land_refs[i], rows, x, y, c),
                        *sems, (px, py, c)).start()
        token[...] = jnp.zeros_like(token)

    hbm = lambda a: pltpu.with_memory_space_constraint(a, pltpu.HBM)
    lands = [lax.empty((4,) + s.shape, s.dtype) for s in shards]
    dma = pltpu.SemaphoreType.DMA
    return pl.pallas_call(
        body, name="gather_start",
        out_shape=(dma((3,)), dma((3,)), dma((3 * (n - 1),)), dma((3 * (n - 1),)),
                   *[pltpu.HBM(a.shape, a.dtype) for a in list(shards) + lands], jax.ShapeDtypeStruct((8, LANES), F32)),
        in_specs=[HBM] * (2 * n) + [pl.BlockSpec(memory_space=pl.ANY)],
        out_specs=(SEM, SEM, SEM, SEM, *[HBM] * (2 * n), pl.BlockSpec(memory_space=pltpu.VMEM)),
        input_output_aliases={i: 4 + i for i in range(2 * n)},
        compiler_params=pltpu.CompilerParams(has_side_effects=DATAFLOW),
    )(*[hbm(a) for a in list(shards) + lands], after)


def gather_wait(send_sems, recv_sems, shards, lands, after, tag):
    n = len(shards)

    def body(*refs):
        w_refs, land_refs = refs[:n], refs[n:2 * n]
        send_ref, recv_ref = refs[2 * n], refs[2 * n + 1]
        x, y, c = _place()
        for i in range(n):
            rows = shards[i].shape[0]
            for j, (px, py) in enumerate(_other_chips(x, y)):
                cp = _remote(w_refs[i].at[pl.ds(c * (rows // 2), rows // 2), :], _half_of_slot(land_refs[i], rows, px, py, c),
                             send_ref.at[j * n + i], recv_ref.at[j * n + i], (px, py, c))
                cp.wait_send()
                cp.wait_recv()

    out = pl.pallas_call(
        body, name="gather_wait_" + tag,
        out_shape=tuple(pltpu.HBM(a.shape, a.dtype) for a in list(shards) + list(lands)),
        in_specs=[HBM] * (2 * n) + [SEM, SEM, pl.BlockSpec(memory_space=pl.ANY)], out_specs=tuple([HBM] * (2 * n)),
        input_output_aliases={i: i for i in range(2 * n)},
        compiler_params=pltpu.CompilerParams(has_side_effects=DATAFLOW),
    )(*shards, *lands, send_sems, recv_sems, after)
    return list(out[:n]), list(out[n:])


def forward_start(lands, tag):
    n = len(lands)

    def body(*refs):
        land_refs = refs[:n]
        send_sems, recv_sems = refs[n], refs[n + 1]
        token = refs[-1]
        x, y, c = _place()
        for i in range(n):
            rows = lands[i].shape[1]
            for j, (px, py) in enumerate(_other_chips(x, y)):
                mine = _half_of_slot(land_refs[i], rows, px, py, c)
                _remote(mine, mine, send_sems.at[j * n + i], recv_sems.at[j * n + i], (x, y, 1 - c)).start()
        token[...] = jnp.zeros_like(token)

    dma = pltpu.SemaphoreType.DMA
    return pl.pallas_call(
        body, name="forward_start_" + tag,
        out_shape=(dma((3 * n,)), dma((3 * n,)), *[pltpu.HBM(a.shape, a.dtype) for a in lands],
                   jax.ShapeDtypeStruct((8, LANES), F32)),
        in_specs=[HBM] * n, out_specs=(SEM, SEM, *[HBM] * n, pl.BlockSpec(memory_space=pltpu.VMEM)),
        input_output_aliases={i: 2 + i for i in range(n)},
        compiler_params=pltpu.CompilerParams(has_side_effects=DATAFLOW),
    )(*lands)


def forward_wait(started, after, tag):
    send_sems, recv_sems, *rest = started
    lands = rest[:-1]
    n = len(lands)

    def body(*refs):
        land_refs = refs[:n]
        send_ref, recv_ref = refs[n], refs[n + 1]
        x, y, c = _place()
        for i in range(n):
            rows = lands[i].shape[1]
            for j, (px, py) in enumerate(_other_chips(x, y)):
                cp = _remote(_half_of_slot(land_refs[i], rows, px, py, c), _half_of_slot(land_refs[i], rows, px, py, 1 - c),
                             send_ref.at[j * n + i], recv_ref.at[j * n + i], (x, y, 1 - c))
                cp.wait_send()
                cp.wait_recv()

    out = pl.pallas_call(
        body, name="forward_wait_" + tag,
        out_shape=tuple(pltpu.HBM(a.shape, a.dtype) for a in lands),
        in_specs=[HBM] * n + [SEM, SEM, pl.BlockSpec(memory_space=pl.ANY)], out_specs=tuple([HBM] * n),
        input_output_aliases={i: i for i in range(n)},
        compiler_params=pltpu.CompilerParams(has_side_effects=DATAFLOW),
    )(*lands, send_sems, recv_sems, after)
    return list(out)


def pair_start(slabs, tag):
    n = len(slabs)

    def body(*refs):
        g_refs, land_refs = refs[:n], refs[n:2 * n]
        send_sems, recv_sems = refs[2 * n], refs[2 * n + 1]
        token = refs[-1]
        x, y, c = _place()
        for i in range(n):
            hr = slabs[i].shape[1] // 2
            _remote(g_refs[i].at[:, pl.ds((1 - c) * hr, hr), :], land_refs[i], send_sems.at[i], recv_sems.at[i],
                    (x, y, 1 - c)).start()
        token[...] = jnp.zeros_like(token)

    hbm = lambda a: pltpu.with_memory_space_constraint(a, pltpu.HBM)
    lands = [lax.empty((4, s.shape[1] // 2, s.shape[2]), s.dtype) for s in slabs]
    dma = pltpu.SemaphoreType.DMA
    return pl.pallas_call(
        body, name="pair_start_" + tag,
        out_shape=(dma((n,)), dma((n,)), *[pltpu.HBM(a.shape, a.dtype) for a in list(slabs) + lands],
                   jax.ShapeDtypeStruct((8, LANES), F32)),
        in_specs=[HBM] * (2 * n), out_specs=(SEM, SEM, *[HBM] * (2 * n), pl.BlockSpec(memory_space=pltpu.VMEM)),
        input_output_aliases={i: 2 + i for i in range(2 * n)},
        compiler_params=pltpu.CompilerParams(has_side_effects=DATAFLOW),
    )(*[hbm(a) for a in list(slabs) + lands])


def pair_wait(started, after, tag):
    send_sems, recv_sems, *rest = started
    n = (len(rest) - 1) // 2
    slabs, lands = rest[:n], rest[n:2 * n]

    def body(*refs):
        g_refs, land_refs = refs[:n], refs[n:2 * n]
        send_ref, recv_ref = refs[2 * n], refs[2 * n + 1]
        x, y, c = _place()
        for i in range(n):
            hr = slabs[i].shape[1] // 2
            cp = _remote(g_refs[i].at[:, pl.ds((1 - c) * hr, hr), :], land_refs[i], send_ref.at[i], recv_ref.at[i], (x, y, 1 - c))
            cp.wait_send()
            cp.wait_recv()

    out = pl.pallas_call(
        body, name="pair_wait_" + tag,
        out_shape=tuple(pltpu.HBM(a.shape, a.dtype) for a in list(slabs) + list(lands)),
        in_specs=[HBM] * (2 * n) + [SEM, SEM, pl.BlockSpec(memory_space=pl.ANY)], out_specs=tuple([HBM] * (2 * n)),
        input_output_aliases={i: i for i in range(2 * n)},
        compiler_params=pltpu.CompilerParams(has_side_effects=DATAFLOW),
    )(*slabs, *lands, send_sems, recv_sems, after)
    return list(out[:n]), list(out[n:])


def _tile2(rows, cols):
    fits = lambda r, c: r * c * 4 <= BLOCK_BYTES
    if fits(rows, cols):
        return rows, cols
    for r in (1024, 512, 256, 128, 64):
        if rows % r == 0 and fits(r, cols):
            return r, cols
    return rows, next(cols // k for k in (2, 3, 4, 6, 8, 12, 16) if cols % (k * LANES) == 0 and fits(rows, cols // k))


def pair_add(g, p, c, name):
    _, hr, cols = p.shape
    tm, tc = _tile2(hr, cols)
    per = hr // tm

    def body(c_ref, g_ref, p_ref, o_ref):
        o_ref[...] = (g_ref[...] + p_ref[...]).astype(o_ref.dtype)

    return pl.pallas_call(
        body, name=name,
        grid_spec=pltpu.PrefetchScalarGridSpec(
            num_scalar_prefetch=1, grid=(4, per, cols // tc),
            in_specs=[pl.BlockSpec((None, tm, tc), lambda k, i, j, c_ref: (k, c_ref[0] * per + i, j)),
                      pl.BlockSpec((None, tm, tc), lambda k, i, j, c_ref: (k, i, j))],
            out_specs=pl.BlockSpec((None, tm, tc), lambda k, i, j, c_ref: (k, i, j))),
        out_shape=jax.ShapeDtypeStruct((4, hr, cols), BF16),
        compiler_params=_params(("arbitrary", "arbitrary", "arbitrary")),
    )(c.reshape(1).astype(jnp.int32), g, p)


def scatter_start(sums, tag):
    n = len(sums)

    def body(*refs):
        s_refs, land_refs = refs[:n], refs[n:2 * n]
        send_sems, recv_sems = refs[2 * n], refs[2 * n + 1]
        token = refs[-1]
        x, y, c = _place()
        k = 2 * x + y
        for i in range(n):
            for j, (px, py) in enumerate(_other_chips(x, y)):
                _remote(s_refs[i].at[2 * px + py], land_refs[i].at[k], send_sems.at[j * n + i], recv_sems.at[j * n + i],
                        (px, py, c)).start()
        token[...] = jnp.zeros_like(token)

    hbm = lambda a: pltpu.with_memory_space_constraint(a, pltpu.HBM)
    return pl.pallas_call(
        body, name="scatter_start_" + tag,
        out_shape=(pltpu.SemaphoreType.DMA((3 * n,)), pltpu.SemaphoreType.DMA((3 * n,)),
                   *[pltpu.HBM(s.shape, s.dtype) for s in sums], *[pltpu.HBM(s.shape, s.dtype) for s in sums],
                   jax.ShapeDtypeStruct((8, LANES), F32)),
        in_specs=[HBM] * (2 * n), out_specs=(SEM, SEM, *[HBM] * (2 * n), pl.BlockSpec(memory_space=pltpu.VMEM)),
        input_output_aliases={i: 2 + i for i in range(2 * n)},
        compiler_params=pltpu.CompilerParams(has_side_effects=DATAFLOW),
    )(*[hbm(s) for s in sums], *[hbm(lax.empty(s.shape, s.dtype)) for s in sums])


def scatter_wait(started, after, tag):
    send_sems, recv_sems, *rest = started
    n = (len(rest) - 1) // 2
    sums, lands = rest[:n], rest[n:2 * n]

    def body(*refs):
        s_refs, land_refs = refs[:n], refs[n:2 * n]
        send_ref, recv_ref = refs[2 * n], refs[2 * n + 1]
        x, y, c = _place()
        for i in range(n):
            for j, (px, py) in enumerate(_other_chips(x, y)):
                cp = _remote(s_refs[i].at[2 * px + py], land_refs[i].at[2 * px + py], send_ref.at[j * n + i],
                             recv_ref.at[j * n + i], (px, py, c))
                cp.wait_send()
                cp.wait_recv()

    out = pl.pallas_call(
        body, name="scatter_wait_" + tag,
        out_shape=tuple(pltpu.HBM(s.shape, s.dtype) for s in sums + lands),
        in_specs=[HBM] * (2 * n) + [SEM, SEM, pl.BlockSpec(memory_space=pl.ANY)], out_specs=tuple([HBM] * (2 * n)),
        input_output_aliases={i: i for i in range(2 * n)},
        compiler_params=pltpu.CompilerParams(has_side_effects=DATAFLOW),
    )(*sums, *lands, send_sems, recv_sems, after)
    return list(out[:n]), list(out[n:])


def sum_chips(landed, own, chip, core, name):
    _, hr, cols = landed.shape
    tm, tc = _tile2(hr, cols)
    per = hr // tm

    def body(idx_ref, l0, l1, l2, l3, own_ref, o_ref):
        mine = own_ref[...].astype(F32)
        v = [jnp.where(idx_ref[0] == k, mine, ref[...].astype(F32)) for k, ref in enumerate((l0, l1, l2, l3))]
        o_ref[...] = ((v[0] + v[1]) + v[2]) + v[3]

    slot = lambda k: pl.BlockSpec((None, tm, tc),
                                  lambda i, j, idx: (jnp.where(idx[0] == k, (k + 1) & 3, k), i, j))
    return pl.pallas_call(
        body, name=name,
        grid_spec=pltpu.PrefetchScalarGridSpec(
            num_scalar_prefetch=1, grid=(per, cols // tc),
            in_specs=[slot(0), slot(1), slot(2), slot(3),
                      pl.BlockSpec((None, tm, tc), lambda i, j, idx: (idx[0], i, j))],
            out_specs=pl.BlockSpec((tm, tc), lambda i, j, idx: (idx[1] * per + i, j))),
        out_shape=jax.ShapeDtypeStruct((2 * hr, cols), F32),
        compiler_params=_params(("arbitrary", "arbitrary")),
    )(jnp.stack([chip, core]).astype(jnp.int32), landed, landed, landed, landed, own)


def exchange_halves(bufs):
    n = len(bufs)

    def body(*refs):
        out_refs = refs[n:2 * n]
        send_sems, recv_sems = refs[2 * n:]
        x, y, c = _place()
        sends, recvs = [], []
        for i in range(n):
            hr = bufs[i].shape[0] // 2
            own = out_refs[i].at[pl.ds(c * hr, hr), :]
            other = out_refs[i].at[pl.ds((1 - c) * hr, hr), :]
            sends.append(_remote(own, own, send_sems.at[i], recv_sems.at[i], (x, y, 1 - c)))
            recvs.append(_remote(other, other, send_sems.at[i], recv_sems.at[i], (x, y, 1 - c)))
        for cp in sends:
            cp.start()
        for cp in recvs:
            cp.wait_recv()
        for cp in sends:
            cp.wait_send()

    return pl.pallas_call(
        body, name="exchange_halves", in_specs=[HBM] * n, out_specs=[HBM] * n,
        out_shape=[jax.ShapeDtypeStruct(b.shape, b.dtype) for b in bufs],
        input_output_aliases={i: i for i in range(n)},
        scratch_shapes=[pltpu.SemaphoreType.DMA((n,)), pltpu.SemaphoreType.DMA((n,))],
    )(*bufs)


def _relayout(name, arrays, in_blocks, out_blocks, out_shapes, fn):
    rows = 128
    spec = lambda blk: pl.BlockSpec(blk, (lambda i: (0, i, 0)) if len(blk) == 3 else (lambda i: (i, 0)))

    def body(*refs):
        n_in = len(arrays)
        outs = fn(*[r[...] for r in refs[:n_in]])
        for ref, val in zip(refs[n_in:], outs, strict=True):
            if isinstance(val, list):
                for k, piece in enumerate(val):
                    ref[k] = piece
            else:
                ref[...] = val

    return pl.pallas_call(
        body, name=name, grid=(D // rows,),
        in_specs=[spec(b) for b in in_blocks], out_specs=[spec(b) for b in out_blocks], out_shape=out_shapes,
        compiler_params=_params(("arbitrary",)),
    )(*arrays)


def assemble_in_proj(g):
    def fn(v):
        w = jnp.concatenate([v[k] for k in range(4)], axis=1)
        return (jnp.concatenate([w[:, :ORIG_Z], w[:, ORIG_GA:], w[:, ORIG_XBC:ORIG_DT], w[:, ORIG_Z:ORIG_XBC],
                                 w[:, ORIG_DT:ORIG_GA], jnp.zeros((w.shape[0], IN_PAD - IN_ORIG), w.dtype)], axis=1),)

    cols = g.shape[2]
    return _relayout("assemble_in_proj", [g], [(4, 128, cols)], [(128, IN_PAD)],
                     [jax.ShapeDtypeStruct((D, IN_PAD), g.dtype)], fn)[0]


def rows_exchange(a, name):
    hr = a.shape[0] // 2

    def body(a_ref, out_ref, send_sem, recv_sem):
        x, y, c = _place()
        cp = _remote(a_ref.at[pl.ds((1 - c) * hr, hr), :], out_ref, send_sem, recv_sem, (x, y, 1 - c))
        cp.start()
        cp.wait()

    return pl.pallas_call(
        body, name=name, in_specs=[HBM], out_specs=HBM,
        out_shape=jax.ShapeDtypeStruct((hr, a.shape[1]), a.dtype),
        scratch_shapes=[pltpu.SemaphoreType.DMA, pltpu.SemaphoreType.DMA],
    )(a)


def split_pair_add(dw, received, core):
    cols = IN_ORIG // 4
    rows, hr = 128, D // 2
    per = hr // rows

    def body(c_ref, own_ref, got_ref, o_ref):
        d = own_ref[...] + got_ref[...]
        w = jnp.concatenate([d[:, :COL_GA], d[:, COL_Z:COL_DT], d[:, COL_XBC:COL_Z], d[:, COL_DT:COL_DT + 32],
                             d[:, COL_GA:COL_XBC]], axis=1)
        for k in range(4):
            o_ref[k] = w[:, k * cols:(k + 1) * cols].astype(o_ref.dtype)

    return pl.pallas_call(
        body, name="split_pair_add",
        grid_spec=pltpu.PrefetchScalarGridSpec(
            num_scalar_prefetch=1, grid=(per,),
            in_specs=[pl.BlockSpec((rows, IN_PAD), lambda i, c_ref: (c_ref[0] * per + i, 0)),
                      pl.BlockSpec((rows, IN_PAD), lambda i, c_ref: (i, 0))],
            out_specs=pl.BlockSpec((4, rows, cols), lambda i, c_ref: (0, i, 0))),
        out_shape=jax.ShapeDtypeStruct((4, hr, cols), BF16),
        compiler_params=_params(("arbitrary",)),
    )(core.reshape(1).astype(jnp.int32), dw, received)


def ada_prepare(c_all, w_ada, hgrn_lb):
    def body(c_ref, w_ref, lb_ref, mod_ref, row_ref):
        mod_ref[...] = hdot(silu(c_ref[...]), w_ref[...])
        row_ref[...] = sigmoid(lb_ref[0:1, :] - lb_ref[1:2, :])

    return pl.pallas_call(
        body, name="ada_prepare",
        out_shape=[jax.ShapeDtypeStruct((8, w_ada.shape[1]), F32), jax.ShapeDtypeStruct((1, D), F32)],
        compiler_params=pltpu.CompilerParams(vmem_limit_bytes=VMEM_LIMIT),
    )(c_all, w_ada, hgrn_lb)


SMALL_SEGS = (("mod", 6 * D), ("lb", D), ("gnorm", LANES), ("conv_w", 4 * CONV_DIM), ("conv_b", CONV_DIM),
              ("dt_bias", LANES), ("a_log", LANES), ("d", B_INNER), ("ssm_norm", B_INNER),
              ("ln1_g", D), ("ln1_b", D), ("ln2_g", D), ("ln2_b", D), ("loss", LANES))
SMALL_PARAMS = ("b_ada", "hgrn_lb", "hgrn_gnorm", "ssm_conv_b", "ssm_dt_bias", "ssm_a_log", "ssm_d", "ssm_norm",
                "ln1_g", "ln1_b", "ln2_g", "ln2_b")


def finalize_small(g_all, c_all, dmod_cols, params, m, v):
    n_p = len(SMALL_PARAMS)
    offs, o = {}, 0
    for nm, width in SMALL_SEGS:
        offs[nm] = (o, width)
        o += width

    def body(*refs):
        g_ref, c_ref, dm_ref = refs[:3]
        p_refs = refs[3:3 + n_p]
        m_refs = refs[3 + n_p:3 + 2 * n_p]
        v_refs = refs[3 + 2 * n_p:3 + 3 * n_p]
        outs = refs[3 + 3 * n_p:]
        gwa_ref, gcw_ref, loss_ref = outs[:3]
        res = outs[3:]
        total = jnp.sum(g_ref[...], axis=0, keepdims=True)
        seg = lambda nm: total[:, offs[nm][0]:offs[nm][0] + offs[nm][1]]
        loss_ref[...] = seg("loss")
        gwa_ref[...] = hdot(silu(c_ref[...]), dm_ref[...], "tn")
        cw = seg("conv_w")
        for j in range(4):
            gcw_ref[j:j + 1, :] = cw[:, j * CONV_DIM:(j + 1) * CONV_DIM]
        hc = lax.broadcasted_iota(jnp.int32, (B_INNER, LANES), 0)
        hj = lax.broadcasted_iota(jnp.int32, (B_INNER, LANES), 1)
        per_head = ((hc >> 6) == hj).astype(F32)
        heads = lambda nm: hdot(jnp.broadcast_to(seg(nm), (8, B_INNER)), per_head)[0:1, 0:32]
        lbp = sigmoid(p_refs[1][0:1, :] - p_refs[1][1:2, :])
        g_row = seg("lb") * lbp * (1.0 - lbp)
        grads = {"b_ada": seg("mod"), "hgrn_gnorm": seg("gnorm"), "ssm_conv_b": seg("conv_b"),
                 "ssm_dt_bias": seg("dt_bias")[:, 0:32], "ssm_a_log": seg("a_log")[:, 0:32], "ssm_d": heads("d"),
                 "ssm_norm": seg("ssm_norm"), "ln1_g": seg("ln1_g"), "ln1_b": seg("ln1_b"),
                 "ln2_g": seg("ln2_g"), "ln2_b": seg("ln2_b")}
        for i, nm in enumerate(SMALL_PARAMS):
            g_out, d_out, m_out, v_out = res[4 * i:4 * i + 4]
            if nm == "hgrn_lb":
                for row, gv in ((0, g_row), (1, -g_row)):
                    sl = slice(row, row + 1)
                    dl, mn, vn = adamw(p_refs[i][sl, :], gv, m_refs[i][sl, :], v_refs[i][sl, :])
                    g_out[sl, :], d_out[sl, :], m_out[sl, :], v_out[sl, :] = gv, dl, mn, vn
            else:
                gv = grads[nm]
                dl, mn, vn = adamw(p_refs[i][...], gv, m_refs[i][...], v_refs[i][...])
                g_out[...], d_out[...], m_out[...], v_out[...] = gv, dl, mn, vn

    out_shape = [jax.ShapeDtypeStruct((D, dmod_cols.shape[1]), F32), jax.ShapeDtypeStruct((4, CONV_DIM), F32),
                 jax.ShapeDtypeStruct((1, LANES), F32)]
    for p in params:
        out_shape += [jax.ShapeDtypeStruct(p.shape, F32)] * 4
    return pl.pallas_call(
        body, name="finalize_small", out_shape=out_shape,
        compiler_params=pltpu.CompilerParams(vmem_limit_bytes=VMEM_LIMIT),
    )(g_all, c_all, dmod_cols, *params, *m, *v)


def adam_update(w, g, m, v, name):
    rows, cols = w.shape
    tm, tc = _tile2(rows, cols)

    def body(w_ref, g_ref, m_ref, v_ref, d_ref, mo_ref, vo_ref):
        d_ref[...], mo_ref[...], vo_ref[...] = adamw(w_ref[...], g_ref[...], m_ref[...], v_ref[...])

    spec = pl.BlockSpec((tm, tc), lambda i, j: (i, j))
    return pl.pallas_call(
        body, name=name, grid=(rows // tm, cols // tc), in_specs=[spec] * 4, out_specs=[spec] * 3,
        out_shape=[jax.ShapeDtypeStruct((rows, cols), F32)] * 3,
        compiler_params=_params(("arbitrary", "arbitrary")),
    )(w, g, m, v)


def kernel(x, c, w_ada, b_ada, w_in, hgrn_lb, hgrn_gnorm, ssm_conv_w, ssm_conv_b, ssm_dt_bias, ssm_a_log, ssm_d, ssm_norm, w_branch_a, w_branch_b, w_o, ln1_g, ln1_b, w_ffn_gate, w_ffn_up, w_ffn_down, ln2_g, ln2_b, loss_target, m_w_ada, m_b_ada, m_w_in, m_hgrn_lb, m_hgrn_gnorm, m_ssm_conv_w, m_ssm_conv_b, m_ssm_dt_bias, m_ssm_a_log, m_ssm_d, m_ssm_norm, m_w_branch_a, m_w_branch_b, m_w_o, m_ln1_g, m_ln1_b, m_w_ffn_gate, m_w_ffn_up, m_w_ffn_down, m_ln2_g, m_ln2_b, v_w_ada, v_b_ada, v_w_in, v_hgrn_lb, v_hgrn_gnorm, v_ssm_conv_w, v_ssm_conv_b, v_ssm_dt_bias, v_ssm_a_log, v_ssm_d, v_ssm_norm, v_w_branch_a, v_w_branch_b, v_w_o, v_ln1_g, v_ln1_b, v_w_ffn_gate, v_w_ffn_up, v_w_ffn_down, v_ln2_g, v_ln2_b):
    given = dict(locals())
    chip = 2 * lax.axis_index("x") + lax.axis_index("y")
    core = lax.axis_index("c")
    t = x.shape[1]

    first = gather_rows(jnp.concatenate([c, ssm_conv_w.reshape(1, CONV_DIM)], axis=1), "gather_cond").reshape(8, D + CONV_DIM)
    c_all = first[:, :D]
    conv_w = first[0::2, D:].reshape(4, 4, CONV_DIM // 4).transpose(1, 0, 2).reshape(4, CONV_DIM)
    mod_part, lb_row = ada_prepare(c_all, w_ada[0], hgrn_lb)
    mod_cols = w_ada.shape[2]
    mod_row = exchange_rows(mod_part.reshape(8, 1, mod_cols), "exchange_mod").reshape(1, 6 * D) + b_ada
    mod = tuple(mod_row[:, i * D:(i + 1) * D] for i in range(6))

    local = {nm: given[nm][0].T if nm in TRANSPOSED else given[nm][0] for nm in SHARDED}
    shards = [local[nm].astype(BF16) for nm in SHARDED]
    n_w = len(SHARDED)
    send_in, recv_in, send_rest, recv_rest, *flying = gather_start(shards, mod_row)
    sent, lands = flying[:n_w], flying[n_w:2 * n_w]
    with_own = lambda land, shard: lax.dynamic_update_slice(land, shard[None], (chip, 0, 0))

    class Weights:
        def input_projection(self, after):
            (own,), land = gather_wait(send_in, recv_in, sent[:1], lands[:1], after, "in")
            (land,) = forward_wait(forward_start(land, "in"), after, "in")
            return assemble_in_proj(with_own(land, own))

        def start_rest(self, after):
            self.own, landed = gather_wait(send_rest, recv_rest, sent[1:], lands[1:], after, "rest")
            self.started = forward_start(landed, "rest")
            return self.started[-1]

        def rest(self, after):
            got = {nm: with_own(land, s) for nm, land, s in zip(SHARDED[1:], forward_wait(self.started, after, "rest"), self.own, strict=True)}
            whole = lambda nm: got[nm].reshape(4 * got[nm].shape[1], got[nm].shape[2])
            return (whole("w_branch_a"), whole("w_branch_b"), whole("w_o"),
                    jnp.concatenate([whole("w_ffn_gate"), whole("w_ffn_up")], axis=0), whole("w_ffn_down"))

    wts = Weights()

    per_head = lambda p: jnp.pad(p, ((0, 0), (0, LANES - p.shape[1])))
    small = (lb_row, hgrn_gnorm, conv_w, ssm_conv_b, per_head(ssm_dt_bias), per_head(ssm_a_log),
             jnp.repeat(ssm_d[0], B_INNER // 32)[None], ssm_norm, ln1_g, ln1_b, ln2_g, ln2_b)
    by_rows = lambda g: g.reshape(4, g.shape[0] // 4, g.shape[1])
    travelling = {}

    def start_early(dws):
        dw_a, dw_b, dw_o, dw_gu, dw_d = dws
        d_gate, d_up = by_rows(dw_gu[:D_FF]), by_rows(dw_gu[D_FF:])
        travelling["pair"] = pair_start([by_rows(dw_a), by_rows(dw_b), by_rows(dw_o), d_gate, d_up, by_rows(dw_d)], "early")
        return travelling["pair"][-1]

    def between_scans(after):
        slabs, received = pair_wait(travelling["pair"], after, "early")
        travelling["pairs"] = [pair_add(s, r, core, "pair_add_" + nm) for nm, s, r in zip(SHARDED[1:], slabs, received, strict=True)]
        travelling["started"] = scatter_start(travelling["pairs"], "early")
        return travelling["started"][-1]

    def finish_early(after):
        travelling["pairs"], travelling["landed"] = scatter_wait(travelling["started"], after, "early")

    def start_last(dw_in):
        travelling["pairs_in"] = [split_pair_add(dw_in, rows_exchange(dw_in, "pair_exchange_last"), core)]
        travelling["started_in"] = scatter_start(travelling["pairs_in"], "last")
        return travelling["started_in"][-1]

    loss, grad_x, d_mod, d_wts, d_small = local_step(x[0], loss_target[0], mod, wts, small,
                                                     start_early, between_scans, finish_early, start_last)

    d_lb, d_gn, d_cw, d_cb, d_dtb, d_alog, d_dsk, d_nw, d_l1g, d_l1b, d_l2g, d_l2b = d_small
    row = jnp.concatenate(list(d_mod) + [d_lb, d_gn, d_cw.reshape(1, 4 * CONV_DIM), d_cb, d_dtb, d_alog, d_dsk, d_nw,
                                          d_l1g, d_l1b, d_l2g, d_l2b, jnp.pad(loss, ((0, 0), (0, LANES - 1)))], axis=1)
    g_all = gather_rows(row, "gather_small_grads").reshape(8, row.shape[1])
    dmod_cols = lax.dynamic_slice_in_dim(g_all, chip * mod_cols, mod_cols, axis=1)
    fin = finalize_small(g_all, c_all, dmod_cols, [given[n] for n in SMALL_PARAMS],
                         [given["m_" + n] for n in SMALL_PARAMS], [given["v_" + n] for n in SMALL_PARAMS])
    grads, deltas, new_m, new_v = {}, {}, {}, {}
    grads["w_ada"] = fin[0][None]
    grads["ssm_conv_w"] = lax.dynamic_slice_in_dim(fin[1], chip * (CONV_DIM // 4), CONV_DIM // 4, axis=1)[None]
    for i, nm in enumerate(SMALL_PARAMS):
        grads[nm], deltas[nm], new_m[nm], new_v[nm] = fin[3 + 4 * i:7 + 4 * i]

    pairs_in, landed_in = scatter_wait(travelling["started_in"], fin[3], "last")
    pairs, landed = pairs_in + travelling["pairs"], landed_in + travelling["landed"]
    halves = [sum_chips(r, p, chip, core, "sum_chips_" + nm) for nm, r, p in zip(SHARDED, landed, pairs, strict=True)]
    reduced = dict(zip(SHARDED, exchange_halves(halves), strict=True))
    reduced["w_ada"], reduced["ssm_conv_w"] = grads["w_ada"][0], grads["ssm_conv_w"][0]
    reduced["w_in"] = reduced["w_in"].T
    for nm in ("w_ada", "ssm_conv_w") + SHARDED:
        flipped = nm in TRANSPOSED or nm == "w_in"
        work = (lambda a: a[0].T) if flipped else (lambda a: a[0])
        back = (lambda a: a.T[None]) if flipped else (lambda a: a[None])
        d_, m_, v_ = adam_update(work(given[nm]), reduced[nm], work(given["m_" + nm]), work(given["v_" + nm]), "adam_" + nm)
        grads[nm], deltas[nm], new_m[nm], new_v[nm] = back(reduced[nm]), back(d_), back(m_), back(v_)

    names = ("w_ada", "b_ada", "w_in", "hgrn_lb", "hgrn_gnorm", "ssm_conv_w", "ssm_conv_b", "ssm_dt_bias", "ssm_a_log",
             "ssm_d", "ssm_norm", "w_branch_a", "w_branch_b", "w_o", "ln1_g", "ln1_b", "w_ffn_gate", "w_ffn_up",
             "w_ffn_down", "ln2_g", "ln2_b")
    return (fin[2][0, 0], grad_x[None], *[grads[n] for n in names], *[deltas[n] for n in names],
            *[new_m[n] for n in names], *[new_v[n] for n in names])
```

```python
import functools

import jax
import jax.numpy as jnp
from jax import lax
from jax.experimental import pallas as pl
from jax.experimental.pallas import tpu as pltpu

F32, BF16 = jnp.float32, jnp.bfloat16
HI = lax.Precision.HIGHEST
MESH = pl.DeviceIdType.MESH

D = 1024
CHUNK = 64
LANES = 128
N_HEADS_A = 8
N_GROUPS_B = 4
B_INNER = 2048
CONV_DIM = 3072
D_FF = 2816
ALPHA = 2.0 ** 0.25
LN_EPS = 1e-5
RMS_EPS = 1e-6
ADAM_LR, ADAM_B1, ADAM_B2, ADAM_EPS, ADAM_WD, ADAM_STEP = 0.001, 0.9, 0.999, 1e-08, 0.01, 10

IN_ORIG = 11296
IN_PAD = 11520
COL_GA, COL_GB, COL_XBC, COL_Z, COL_DT = 4096, 5120, 6144, 9216, 11264
ORIG_Z, ORIG_XBC, ORIG_DT, ORIG_GA = 4096, 6144, 9216, 9248

SHARDED = ("w_in", "w_branch_a", "w_branch_b", "w_o", "w_ffn_in", "w_ffn_down")
FFN_SHARD = D_FF // 4
VMEM_LIMIT = 56 * 1024 * 1024
BLOCK_BYTES = 2 * 1024 * 1024
_DIMS = {"nn": (((1,), (0,)), ((), ())), "nt": (((1,), (1,)), ((), ())), "tn": (((0,), (0,)), ((), ()))}


def _bd(a, b, mode):
    return lax.dot_general(a.astype(BF16), b.astype(BF16), _DIMS[mode], preferred_element_type=F32)


@functools.partial(jax.custom_vjp, nondiff_argnums=(2,))
def bdot(a, b, mode):
    return _bd(a, b, mode)


def _bdot_fwd(a, b, mode):
    return _bd(a, b, mode), (a, b)


def _bdot_bwd(mode, res, g):
    a, b = res
    if mode == "nn":
        return _bd(g, b, "nt"), _bd(a, g, "tn")
    if mode == "nt":
        return _bd(g, b, "nn"), _bd(g, a, "tn")
    return _bd(b, g, "nt"), _bd(a, g, "nn")


bdot.defvjp(_bdot_fwd, _bdot_bwd)


def hdot(a, b, mode="nn"):
    return lax.dot_general(a, b, _DIMS[mode], precision=HI, preferred_element_type=F32)


def _raw(a, b, mode):
    return lax.dot_general(a, b, _DIMS[mode], preferred_element_type=F32)


def _split(x, n):
    parts, rest = [], x
    for _ in range(n):
        p = rest.astype(BF16)
        parts.append(p)
        rest = rest - p.astype(F32)
    return parts


def _od(a, b, mode, exact):
    if exact == 1:
        e = b.astype(BF16)
        p = _split(a, 3)
        return (_raw(p[2], e, mode) + _raw(p[1], e, mode)) + _raw(p[0], e, mode)
    e = a.astype(BF16)
    p = _split(b, 3)
    return (_raw(e, p[2], mode) + _raw(e, p[1], mode)) + _raw(e, p[0], mode)


@functools.partial(jax.custom_vjp, nondiff_argnums=(2, 3))
def odot(a, b, mode, exact):
    return _od(a, b, mode, exact)


def _odot_fwd(a, b, mode, exact):
    return _od(a, b, mode, exact), (a, b)


def _odot_bwd(mode, exact, res, g):
    a, b = res
    if exact == 1:
        da = {"nn": lambda: _od(g, b, "nt", 1), "nt": lambda: _od(g, b, "nn", 1), "tn": lambda: _od(b, g, "nt", 0)}[mode]()
        return da, jnp.zeros_like(b)
    db = {"nn": lambda: _od(a, g, "tn", 0), "nt": lambda: _od(g, a, "tn", 1), "tn": lambda: _od(a, g, "nn", 0)}[mode]()
    return jnp.zeros_like(a), db


odot.defvjp(_odot_fwd, _odot_bwd)


_BDIMS = {"bnn": (((2,), (1,)), ((0,), (0,))), "bnt": (((2,), (2,)), ((0,), (0,))), "btn": (((1,), (1,)), ((0,), (0,)))}


def _braw(a, b, mode):
    return lax.dot_general(a, b, _BDIMS[mode], preferred_element_type=F32)


def _bdb(a, b, mode):
    return _braw(a.astype(BF16), b.astype(BF16), mode)


def _d3b(a, b, mode):
    ah, al = _split(a, 2)
    bh, bl = _split(b, 2)
    return _braw(ah, bh, mode) + (_braw(ah, bl, mode) + _braw(al, bh, mode))


def _batched_bwd(f):
    def bwd(mode, res, g):
        a, b = res
        if mode == "bnn":
            return f(g, b, "bnt"), f(a, g, "btn")
        if mode == "bnt":
            return f(g, b, "bnn"), f(g, a, "btn")
        return f(b, g, "bnt"), f(a, g, "bnn")
    return bwd


@functools.partial(jax.custom_vjp, nondiff_argnums=(2,))
def bdot_b(a, b, mode):
    return _bdb(a, b, mode)


bdot_b.defvjp(lambda a, b, mode: (_bdb(a, b, mode), (a, b)), _batched_bwd(_bdb))


@functools.partial(jax.custom_vjp, nondiff_argnums=(2,))
def dot3_b(a, b, mode):
    return _d3b(a, b, mode)


dot3_b.defvjp(lambda a, b, mode: (_d3b(a, b, mode), (a, b)), _batched_bwd(_d3b))


def _cum(tril3, x, mode):
    e = tril3.astype(BF16)
    p = _split(x, 3)
    return (_braw(e, p[2], mode) + _braw(e, p[1], mode)) + _braw(e, p[0], mode)


@jax.custom_vjp
def chunk_cumsum(tril3, x):
    return _cum(tril3, x, "bnn")


chunk_cumsum.defvjp(lambda t, x: (_cum(t, x, "bnn"), t), lambda t, g: (jnp.zeros_like(t), _cum(t, g, "btn")))


def _unstack(axis, n):
    @jax.custom_vjp
    def un(x):
        return tuple(lax.index_in_dim(x, i, axis, keepdims=False) for i in range(n))

    un.defvjp(lambda x: (un(x), None), lambda _, g: (jnp.stack(g, axis=axis),))
    return un


def _split_last(n, w):
    @jax.custom_vjp
    def sp(x):
        return tuple(x[..., i * w:(i + 1) * w] for i in range(n))

    sp.defvjp(lambda x: (sp(x), None), lambda _, g: (jnp.concatenate(g, axis=-1),))
    return sp


def sigmoid(x):
    return 1.0 / (1.0 + jnp.exp(-x))


def silu(x):
    return x * sigmoid(x)


def softplus(x):
    return jnp.maximum(x, 0.0) + jnp.log1p(jnp.exp(jnp.minimum(x, -x)))


def _ln(x):
    mu = jnp.mean(x, axis=-1, keepdims=True)
    xc = x - mu
    return xc * lax.rsqrt(jnp.mean(xc * xc, axis=-1, keepdims=True) + LN_EPS)


def _tril64():
    r = lax.broadcasted_iota(jnp.int32, (CHUNK, CHUNK), 0)
    c = lax.broadcasted_iota(jnp.int32, (CHUNK, CHUNK), 1)
    return (r >= c).astype(F32)


def hgrn_block(q, fl, iv, gr, st, lb, gn):
    tb = q.shape[0]
    nc = tb // CHUNK
    nh = N_HEADS_A
    heads = _split_last(nh, LANES)
    to4 = lambda a: jnp.stack(heads(a), axis=0).reshape(nh, nc, CHUNK, LANES)
    flat = lambda a: a.reshape(nh * nc, CHUNK, LANES)
    f = lb + (1.0 - lb) * sigmoid(fl)
    gl4, k4, qf4, v4, gr4 = to4(jnp.log(f)), to4(1.0 - f), to4(silu(q) * (128 ** -0.5)), to4(iv), to4(gr)
    tril = _tril64()
    b4 = chunk_cumsum(jnp.broadcast_to(tril[None], (nh * nc, CHUNK, CHUNK)), flat(gl4)).reshape(gl4.shape)
    blast = jnp.sum(gl4, axis=2, keepdims=True)
    ref = lax.stop_gradient(0.5 * blast)
    sc = dot3_b(flat(qf4 * jnp.exp(b4 - ref)), flat(k4 * jnp.exp(ref - b4)), "bnt") * tril
    o_intra = bdot_b(sc, flat(v4), "bnn").reshape(gl4.shape)
    chunks = _unstack(1, nc)
    qe, v_c, kd, dec = chunks(qf4 * jnp.exp(b4)), chunks(v4), chunks(k4 * jnp.exp(blast - b4)), chunks(jnp.exp(blast))
    o_inter = []
    for c in range(nc):
        o_inter.append(bdot_b(qe[c], st, "bnt"))
        st = st * dec[c] + bdot_b(v_c[c], kd[c], "btn")
    o = o_intra + jnp.stack(o_inter, axis=1)
    on = o * lax.rsqrt(jnp.mean(o * o, axis=-1, keepdims=True) + RMS_EPS) * gn
    out = (on * silu(gr4)).reshape(nh, tb, LANES)
    return jnp.concatenate(_unstack(0, nh)(out), axis=1), st


def ssd_consts(g):
    i32 = jnp.int32
    ej = lax.broadcasted_iota(i32, (LANES, 512), 0)
    ec = lax.broadcasted_iota(i32, (LANES, 512), 1)
    expand = (ej == g * 8 + (ec >> 6)).astype(F32)
    ts = lax.broadcasted_iota(i32, (CHUNK, 512), 0)
    tc = lax.broadcasted_iota(i32, (CHUNK, 512), 1)
    itile = (ts == (tc & 63)).astype(F32)
    maskall = ts >= (tc & 63)
    br = lax.broadcasted_iota(i32, (256, 256), 0)
    bc = lax.broadcasted_iota(i32, (256, 256), 1)
    blockmask = ((br >> 6) == (bc >> 6)).astype(F32)
    return expand, itile, maskall, blockmask, _tril64()


def ssd_block(x, bm, cm, dt, z, st, dtb, alog, dsk, nw, cs):
    expand, itile, maskall, blockmask, tril = cs
    tb = x.shape[0]
    nc = tb // CHUNK
    delta_heads = softplus(dt + dtb)
    delta = odot(delta_heads, expand, "nn", 1)
    a = odot(-jnp.exp(alog) * delta_heads, expand, "nn", 1)
    xdt = x * delta
    by_chunk = lambda v: v.reshape(nc, CHUNK, v.shape[-1])
    a3, xdt3, bm3, cm3 = by_chunk(a), by_chunk(xdt), by_chunk(bm), by_chunk(cm)
    acum3 = chunk_cumsum(jnp.broadcast_to(tril[None], (nc, CHUNK, CHUNK)), a3)
    alast3 = jnp.sum(a3, axis=1, keepdims=True)
    cb3 = bdot_b(cm3, jnp.concatenate([bm3] * 8, axis=1), "bnt")
    arow3 = jnp.sum(acum3 * itile, axis=1, keepdims=True)
    dec3 = jnp.exp(jnp.where(maskall, acum3 - arow3, -1e30))
    halves = _split_last(2, 256)
    intra = [bdot_b(m, jnp.concatenate([xh] * 4, axis=1) * blockmask, "bnn")
             for m, xh in zip(halves(cb3 * dec3), halves(xdt3))]
    chunks = _unstack(0, nc)
    cm_c, bm_c, xw_c, dec_c = chunks(cm3), chunks(bm3), chunks(xdt3 * jnp.exp(alast3 - acum3)), chunks(jnp.exp(alast3))
    inter = []
    for c in range(nc):
        inter.append(bdot(cm_c[c], st, "nn"))
        st = st * dec_c[c] + bdot(bm_c[c], xw_c[c], "tn")
    st_new = st
    y = (jnp.concatenate(intra, axis=-1) + jnp.stack(inter, axis=0) * jnp.exp(acum3)).reshape(tb, 512)
    yz = (y + x * dsk) * silu(z)
    return yz * lax.rsqrt(jnp.mean(yz * yz, axis=-1, keepdims=True) + RMS_EPS) * nw, st_new


def adamw(w, g, m, v):
    m = ADAM_B1 * m + (1.0 - ADAM_B1) * g
    v = ADAM_B2 * v + (1.0 - ADAM_B2) * jnp.square(g)
    m_hat = m / (1.0 - ADAM_B1 ** ADAM_STEP)
    v_hat = v / (1.0 - ADAM_B2 ** ADAM_STEP)
    return -ADAM_LR * (m_hat / (jnp.sqrt(v_hat) + ADAM_EPS) + ADAM_WD * w), m, v


def _pick(n, cands):
    for c in cands:
        if n % c == 0:
            return c
    return n


def _params(sem):
    return pltpu.CompilerParams(dimension_semantics=sem, vmem_limit_bytes=VMEM_LIMIT)


MATMUL_VMEM_BUDGET = 50 * 1024 * 1024
MATMUL_MIN_STEPS = 4


def matmul(a, b, mode, out_dtype, name, after=None):
    if mode == "nn":
        (m, k), n = a.shape, b.shape[1]
    elif mode == "nt":
        (m, k), n = a.shape, b.shape[0]
    else:
        (k, m), n = a.shape, b.shape[1]
    tk = _pick(k, (2304, 2048, 1408, 1024, 768, 512, 256, 128))
    nk = k // tk
    a_bytes, b_bytes, out_bytes = a.dtype.itemsize, b.dtype.itemsize, jnp.dtype(out_dtype).itemsize

    def vmem(tm_, tn_):
        blocks = 2 * (tm_ * tk * a_bytes + tk * tn_ * b_bytes + tm_ * tn_ * out_bytes)
        return blocks + (tm_ * tn_ * 4 if nk > 1 else 0)

    def traffic(tm_, tn_):
        return (m // tm_) * k * n * b_bytes + (n // tn_ if nk > 1 else 1) * m * k * a_bytes

    sizes = (2304, 2048, 1920, 1408, 1024, 768, 512, 256, 128)
    tiles = [(tm_, tn_) for tm_ in sizes if m % tm_ == 0 for tn_ in sizes if n % tn_ == 0
             if vmem(tm_, tn_) <= MATMUL_VMEM_BUDGET] or [(m, n)]
    pipelined = [t for t in tiles if (m // t[0]) * (n // t[1]) * nk >= MATMUL_MIN_STEPS]
    tm, tn = min(pipelined or tiles, key=lambda t: (traffic(*t), -t[0] * t[1]))
    a_spec = pl.BlockSpec((tk, tm), lambda i, j, kk: (kk, i)) if mode == "tn" else pl.BlockSpec((tm, tk), lambda i, j, kk: (i, kk))
    b_spec = pl.BlockSpec((tn, tk), lambda i, j, kk: (j, kk)) if mode == "nt" else pl.BlockSpec((tk, tn), lambda i, j, kk: (kk, j))

    order = [] if after is None else [after]

    def body(a_ref, b_ref, *rest):
        o_ref, *acc = rest[len(order):]
        part = _bd(a_ref[...], b_ref[...], mode)
        if nk == 1:
            o_ref[...] = part.astype(o_ref.dtype)
            return
        acc_ref, = acc
        kk = pl.program_id(2)

        @pl.when(kk == 0)
        def _():
            acc_ref[...] = part

        @pl.when(jnp.logical_and(kk > 0, kk < nk - 1))
        def _():
            acc_ref[...] += part

        @pl.when(kk == nk - 1)
        def _():
            o_ref[...] = (acc_ref[...] + part).astype(o_ref.dtype)

    return pl.pallas_call(
        body, name=name, grid=(m // tm, n // tn, nk),
        in_specs=[a_spec, b_spec] + [pl.BlockSpec(memory_space=pl.ANY) for _ in order],
        out_specs=pl.BlockSpec((tm, tn), lambda i, j, kk: (i, j)),
        out_shape=jax.ShapeDtypeStruct((m, n), out_dtype),
        scratch_shapes=[pltpu.VMEM((tm, tn), F32)] if nk > 1 else [],
        compiler_params=_params(("parallel", "parallel", "arbitrary")),
    )(a, b, *order)


def rowwise(name, fn, rows, consts, out_rows, out_accs=(), tm_max=256, into=None, new_wide=None):
    t = rows[0][0].shape[0]
    tm = _pick(t, (tm_max, 128, 64, 32, 16, 8))
    n_r, n_c, n_o = len(rows), len(consts), len(out_rows)
    n_alias = 0 if into is None else 1

    def body(*refs):
        r_in = [r[...] for r in refs[:n_r]]
        c_in = [r[...] for r in refs[n_r:n_r + n_c]]
        refs = refs[:n_r + n_c] + refs[n_r + n_c + n_alias:]
        o_refs = refs[n_r + n_c:n_r + n_c + n_o]
        a_refs = refs[n_r + n_c + n_o:]
        ro, ao = fn(r_in, c_in)
        for ref, val in zip(o_refs, ro, strict=True):
            ref[...] = val.astype(ref.dtype)
        if a_refs:
            @pl.when(pl.program_id(0) == 0)
            def _():
                for ref in a_refs:
                    ref[...] = jnp.zeros_like(ref)

            for ref, val in zip(a_refs, ao, strict=True):
                ref[...] += val

    in_specs = [pl.BlockSpec((tm, w), functools.partial(lambda i, cb: (i, cb), cb=cb)) for _, w, cb in rows]
    in_specs += [pl.BlockSpec(c.shape, lambda i: (0, 0)) for c in consts]
    out_specs = [pl.BlockSpec((tm, w), lambda i: (i, 0)) for w, _ in out_rows]
    out_specs += [pl.BlockSpec(s, lambda i: (0, 0)) for s in out_accs]
    out_shape = [jax.ShapeDtypeStruct((t, w), dt) for w, dt in out_rows]
    out_shape += [jax.ShapeDtypeStruct(s, F32) for s in out_accs]
    operands = [r[0] for r in rows] + list(consts)
    aliases = {}
    if into is not None:
        target, cb = into
        in_specs.append(pl.BlockSpec(memory_space=pl.ANY))
        operands.append(target)
        out_specs[0] = pl.BlockSpec((tm, out_rows[0][0]), lambda i: (i, cb))
        out_shape[0] = jax.ShapeDtypeStruct(target.shape, target.dtype)
        aliases = {len(operands) - 1: 0}
    if new_wide is not None:
        width, cb = new_wide
        out_specs[0] = pl.BlockSpec((tm, out_rows[0][0]), lambda i: (i, cb))
        out_shape[0] = jax.ShapeDtypeStruct((t, width), out_rows[0][1])
    return pl.pallas_call(
        body, name=name, grid=(t // tm,), in_specs=in_specs, out_specs=out_specs, out_shape=out_shape,
        input_output_aliases=aliases, compiler_params=_params(("arbitrary",)),
    )(*operands)


def _full(a):
    return (a, a.shape[1], 0)


HGRN_TIME_BLOCK = 256
SSD_TIME_BLOCK = 512


def _time_block(t, most=HGRN_TIME_BLOCK):
    return _pick(t, tuple(b for b in (512, 256, 128, 64) if b <= most))


def _quarters(ref):
    return [ref[:, seg * D:(seg + 1) * D] for seg in range(4)]


def hgrn_forward(proj, lb, gn):
    t = proj.shape[0]
    tb = _time_block(t)
    nb = t // tb

    def body(qfig_ref, lb_ref, gn_ref, o_ref, st_ref, state):
        @pl.when(pl.program_id(0) == 0)
        def _():
            state[...] = jnp.zeros_like(state)

        st = state[...]
        st_ref[...] = st
        out, st_new = hgrn_block(*_quarters(qfig_ref), st, lb_ref[...], gn_ref[...])
        o_ref[...] = out.astype(o_ref.dtype)
        state[...] = st_new

    return pl.pallas_call(
        body, name="hgrn_forward", grid=(nb,),
        in_specs=[pl.BlockSpec((tb, 4 * D), lambda j: (j, 0)),
                  pl.BlockSpec((1, D), lambda j: (0, 0)), pl.BlockSpec((1, LANES), lambda j: (0, 0))],
        out_specs=[pl.BlockSpec((tb, D), lambda j: (j, 0)),
                   pl.BlockSpec((None, N_HEADS_A, LANES, LANES), lambda j: (j, 0, 0, 0))],
        out_shape=[jax.ShapeDtypeStruct((t, D), BF16),
                   jax.ShapeDtypeStruct((nb, N_HEADS_A, LANES, LANES), F32)],
        scratch_shapes=[pltpu.VMEM((N_HEADS_A, LANES, LANES), F32)],
        compiler_params=_params(("arbitrary",)),
    )(proj, lb, gn)


def hgrn_backward(proj, states, d_out, lb, gn, d_proj):
    t = proj.shape[0]
    tb = _time_block(t)
    nb = t // tb

    def body(qfig_ref, st_ref, do_ref, lb_ref, gn_ref, _, dqfig_ref, dlb_ref, dgn_ref, d_state):
        @pl.when(pl.program_id(0) == 0)
        def _():
            d_state[...] = jnp.zeros_like(d_state)
            dlb_ref[...] = jnp.zeros_like(dlb_ref)
            dgn_ref[...] = jnp.zeros_like(dgn_ref)

        _, vjp = jax.vjp(hgrn_block, *_quarters(qfig_ref), st_ref[...], lb_ref[...], gn_ref[...])
        dq, df, di, dg, dst, dlb, dgn = vjp((do_ref[...], d_state[...]))
        for seg, val in enumerate((dq, df, di, dg)):
            dqfig_ref[:, seg * D:(seg + 1) * D] = val.astype(dqfig_ref.dtype)
        d_state[...] = dst
        dlb_ref[...] += dlb
        dgn_ref[...] += dgn

    rev = lambda j: nb - 1 - j
    return pl.pallas_call(
        body, name="hgrn_backward", grid=(nb,),
        in_specs=[pl.BlockSpec((tb, 4 * D), lambda j: (rev(j), 0)),
                  pl.BlockSpec((None, N_HEADS_A, LANES, LANES), lambda j: (rev(j), 0, 0, 0)),
                  pl.BlockSpec((tb, D), lambda j: (rev(j), 0)),
                  pl.BlockSpec((1, D), lambda j: (0, 0)), pl.BlockSpec((1, LANES), lambda j: (0, 0)),
                  pl.BlockSpec(memory_space=pl.ANY)],
        out_specs=[pl.BlockSpec((tb, 4 * D), lambda j: (rev(j), 0)),
                   pl.BlockSpec((1, D), lambda j: (0, 0)), pl.BlockSpec((1, LANES), lambda j: (0, 0))],
        out_shape=[jax.ShapeDtypeStruct(d_proj.shape, d_proj.dtype), jax.ShapeDtypeStruct((1, D), F32),
                   jax.ShapeDtypeStruct((1, LANES), F32)],
        input_output_aliases={5: 0},
        scratch_shapes=[pltpu.VMEM((N_HEADS_A, LANES, LANES), F32)],
        compiler_params=_params(("arbitrary",)),
    )(proj, states, d_out, lb, gn, d_proj)


def _ssd_in_specs(tb, tmap):
    return [pl.BlockSpec((tb, 512), lambda g, j: (tmap(j), g)),
            pl.BlockSpec((tb, LANES), lambda g, j: (tmap(j), 16 + g)),
            pl.BlockSpec((tb, LANES), lambda g, j: (tmap(j), 20 + g)),
            pl.BlockSpec((tb, LANES), lambda g, j: (tmap(j), COL_DT // LANES)),
            pl.BlockSpec((tb, 512), lambda g, j: (tmap(j), COL_Z // 512 + g))]


def ssd_forward(xc, proj, dtb, alog, dsk, nw):
    t = proj.shape[0]
    tb = _time_block(t, SSD_TIME_BLOCK)
    nb = t // tb

    def body(x_ref, b_ref, c_ref, dt_ref, z_ref, dtb_ref, alog_ref, dsk_ref, nw_ref, o_ref, st_ref, state):
        @pl.when(pl.program_id(1) == 0)
        def _():
            state[...] = jnp.zeros_like(state)

        st = state[...]
        st_ref[...] = st
        out, st_new = ssd_block(x_ref[...], b_ref[...], c_ref[...], dt_ref[...], z_ref[...], st,
                                dtb_ref[...], alog_ref[...], dsk_ref[...], nw_ref[...], ssd_consts(pl.program_id(0)))
        o_ref[...] = out.astype(o_ref.dtype)
        state[...] = st_new

    vec = pl.BlockSpec((1, 512), lambda g, j: (0, g))
    heads = pl.BlockSpec((1, LANES), lambda g, j: (0, 0))
    return pl.pallas_call(
        body, name="ssd_forward", grid=(N_GROUPS_B, nb),
        in_specs=_ssd_in_specs(tb, lambda j: j) + [heads, heads, vec, vec],
        out_specs=[pl.BlockSpec((tb, 512), lambda g, j: (j, g)),
                   pl.BlockSpec((None, None, LANES, 512), lambda g, j: (j, g, 0, 0))],
        out_shape=[jax.ShapeDtypeStruct((t, B_INNER), BF16),
                   jax.ShapeDtypeStruct((nb, N_GROUPS_B, LANES, 512), F32)],
        scratch_shapes=[pltpu.VMEM((LANES, 512), F32)],
        compiler_params=_params(("arbitrary", "arbitrary")),
    )(xc, xc, xc, proj, proj, dtb, alog, dsk, nw)


def ssd_backward(xc, proj, states, d_out, dtb, alog, dsk, nw, d_proj):
    t = proj.shape[0]
    tb = _time_block(t, SSD_TIME_BLOCK)
    nb = t // tb
    rev = lambda j: nb - 1 - j

    def body(x_ref, b_ref, c_ref, dt_ref, z_ref, st_ref, do_ref, dtb_ref, alog_ref, dsk_ref, nw_ref, _,
             dx_ref, db_ref, dc_ref, ddt_ref, dz_ref, ddtb_ref, dalog_ref, ddsk_ref, dnw_ref, d_state):
        accs = (ddtb_ref, dalog_ref, ddsk_ref, dnw_ref)

        @pl.when(pl.program_id(1) == 0)
        def _():
            d_state[...] = jnp.zeros_like(d_state)
            for ref in accs:
                ref[...] = jnp.zeros_like(ref)

        cs = ssd_consts(pl.program_id(0))
        fn = lambda *a: ssd_block(*a, cs)
        _, vjp = jax.vjp(fn, x_ref[...], b_ref[...], c_ref[...], dt_ref[...], z_ref[...], st_ref[...],
                         dtb_ref[...], alog_ref[...], dsk_ref[...], nw_ref[...])
        dx, db, dc, ddt, dz, dst, *dpar = vjp((do_ref[...], d_state[...]))
        dx_ref[...] = dx
        db_ref[...] = db
        dc_ref[...] = dc
        ddt_ref[...] = ddt
        dz_ref[...] = dz.astype(dz_ref.dtype)
        d_state[...] = dst
        for ref, val in zip(accs, dpar, strict=True):
            ref[...] += val

    vec = pl.BlockSpec((1, 512), lambda g, j: (0, g))
    heads = pl.BlockSpec((1, LANES), lambda g, j: (0, 0))
    acc = pl.BlockSpec((None, 1, 512), lambda g, j: (g, 0, 0))
    acc_heads = pl.BlockSpec((None, 1, LANES), lambda g, j: (g, 0, 0))
    return pl.pallas_call(
        body, name="ssd_backward", grid=(N_GROUPS_B, nb),
        in_specs=_ssd_in_specs(tb, rev)
        + [pl.BlockSpec((None, None, LANES, 512), lambda g, j: (rev(j), g, 0, 0)),
           pl.BlockSpec((tb, 512), lambda g, j: (rev(j), g))] + [heads, heads, vec, vec] + [pl.BlockSpec(memory_space=pl.ANY)],
        out_specs=[pl.BlockSpec((tb, 512), lambda g, j: (rev(j), g)),
                   pl.BlockSpec((tb, LANES), lambda g, j: (rev(j), g)),
                   pl.BlockSpec((tb, LANES), lambda g, j: (rev(j), g)),
                   pl.BlockSpec((None, tb, LANES), lambda g, j: (g, rev(j), 0)),
                   pl.BlockSpec((tb, 512), lambda g, j: (rev(j), COL_Z // 512 + g)), acc_heads, acc_heads, acc, acc],
        out_shape=[jax.ShapeDtypeStruct((t, B_INNER), F32), jax.ShapeDtypeStruct((t, 512), F32),
                   jax.ShapeDtypeStruct((t, 512), F32), jax.ShapeDtypeStruct((N_GROUPS_B, t, LANES), F32),
                   jax.ShapeDtypeStruct(d_proj.shape, d_proj.dtype)]
        + [jax.ShapeDtypeStruct((N_GROUPS_B, 1, LANES), F32)] * 2 + [jax.ShapeDtypeStruct((N_GROUPS_B, 1, 512), F32)] * 2,
        input_output_aliases={11: 4},
        scratch_shapes=[pltpu.VMEM((LANES, 512), F32)],
        compiler_params=_params(("arbitrary", "arbitrary")),
    )(xc, xc, xc, proj, proj, states, d_out, dtb, alog, dsk, nw, d_proj)


CONV_HALO = 8


def _shift_down(halo_then_tile, s, tm):
    if s == 0:
        return halo_then_tile[CONV_HALO:CONV_HALO + tm]
    return pltpu.roll(halo_then_tile, s, 0)[CONV_HALO:CONV_HALO + tm]


def _conv_pre(cur, prev, w, b, tm):
    stacked = jnp.concatenate([prev, cur], axis=0)
    taps = [_shift_down(stacked, 3 - j, tm) for j in range(4)]
    pre = b + taps[0] * w[0:1] + taps[1] * w[1:2] + taps[2] * w[2:3] + taps[3] * w[3:4]
    return pre, taps


def _conv_specs(t, tm):
    per = tm // CONV_HALO
    cur = pl.BlockSpec((tm, CONV_DIM), lambda i: (i, COL_XBC // CONV_DIM))
    prev = pl.BlockSpec((CONV_HALO, CONV_DIM), lambda i: (jnp.maximum(i * per - 1, 0), COL_XBC // CONV_DIM))
    return cur, prev


def conv_forward(proj, w, b):
    t = proj.shape[0]
    tm = _pick(t, (256, 128, 64))

    def body(cur_ref, prev_ref, w_ref, b_ref, o_ref):
        prev = jnp.where(pl.program_id(0) == 0, 0.0, prev_ref[...])
        pre, _ = _conv_pre(cur_ref[...], prev, w_ref[...], b_ref[...], tm)
        o_ref[...] = silu(pre)

    cur, prev = _conv_specs(t, tm)
    return pl.pallas_call(
        body, name="conv_forward", grid=(t // tm,),
        in_specs=[cur, prev, pl.BlockSpec((4, CONV_DIM), lambda i: (0, 0)), pl.BlockSpec((1, CONV_DIM), lambda i: (0, 0))],
        out_specs=pl.BlockSpec((tm, CONV_DIM), lambda i: (i, 0)),
        out_shape=jax.ShapeDtypeStruct((t, CONV_DIM), F32),
        compiler_params=_params(("arbitrary",)),
    )(proj, proj, w, b)


def conv_backward(proj, dx, db_, dc_, w, b, d_proj):
    t = proj.shape[0]
    tm = _pick(t, (256, 128, 64))
    per = tm // CONV_HALO
    nt = t // tm
    rev = lambda i: nt - 1 - i

    def body(cur_ref, prev_ref, dx_ref, dbm_ref, dcm_ref, w_ref, b_ref, _, o_ref, dw_ref, dbias_ref, later):
        @pl.when(pl.program_id(0) == 0)
        def _():
            dw_ref[...] = jnp.zeros_like(dw_ref)
            dbias_ref[...] = jnp.zeros_like(dbias_ref)
            later[...] = jnp.zeros_like(later)

        first_tile = pl.program_id(0) == nt - 1
        for lo, hi, src in ((0, B_INNER, dx_ref), (B_INNER, B_INNER + 512, dbm_ref), (B_INNER + 512, CONV_DIM, dcm_ref)):
            cols = slice(lo, hi)
            prev = jnp.where(first_tile, 0.0, prev_ref[:, cols])
            w_ = w_ref[:, cols]
            pre, taps = _conv_pre(cur_ref[:, cols], prev, w_, b_ref[:, cols], tm)
            sg = sigmoid(pre)
            dpre = src[...] * (sg * (1.0 + pre * (1.0 - sg)))
            dbias_ref[:, cols] += jnp.sum(dpre, axis=0, keepdims=True)
            for j in range(4):
                dw_ref[j:j + 1, cols] += jnp.sum(dpre * taps[j], axis=0, keepdims=True)
            stacked = jnp.concatenate([dpre, later[:, cols]], axis=0)
            acc = dpre * w_[3:4]
            for j in range(3):
                acc = acc + pltpu.roll(stacked, tm + CONV_HALO - (3 - j), 0)[0:tm] * w_[j:j + 1]
            o_ref[:, cols] = acc.astype(o_ref.dtype)
            later[:, cols] = dpre[0:CONV_HALO]

    row = lambda w_: pl.BlockSpec((tm, w_), lambda i: (rev(i), 0))
    whole = lambda r: pl.BlockSpec((r, CONV_DIM), lambda i: (0, 0))
    return pl.pallas_call(
        body, name="conv_backward", grid=(nt,),
        in_specs=[pl.BlockSpec((tm, CONV_DIM), lambda i: (rev(i), COL_XBC // CONV_DIM)),
                  pl.BlockSpec((CONV_HALO, CONV_DIM), lambda i: (jnp.maximum(rev(i) * per - 1, 0), COL_XBC // CONV_DIM)),
                  row(B_INNER), row(512), row(512), whole(4), whole(1), pl.BlockSpec(memory_space=pl.ANY)],
        out_specs=[pl.BlockSpec((tm, CONV_DIM), lambda i: (rev(i), COL_XBC // CONV_DIM)), whole(4), whole(1)],
        out_shape=[jax.ShapeDtypeStruct(d_proj.shape, d_proj.dtype), jax.ShapeDtypeStruct((4, CONV_DIM), F32),
                   jax.ShapeDtypeStruct((1, CONV_DIM), F32)],
        input_output_aliases={7: 0},
        scratch_shapes=[pltpu.VMEM((CONV_HALO, CONV_DIM), F32)],
        compiler_params=_params(("arbitrary",)),
    )(proj, proj, dx, db_, dc_, w, b, d_proj)


def stage_modulate(x, sc, sh):
    return _ln(x) * (1.0 + sc) + sh


def stage_merge(ga, gb, ya, yb):
    return sigmoid(ga) * ya + sigmoid(gb) * yb


def stage_post_mixer(x, h, g1, ln_g, ln_b, sc2, sh2):
    x1 = _ln(ALPHA * x + g1 * h) * ln_g + ln_b
    return x1, _ln(x1) * (1.0 + sc2) + sh2


def stage_swiglu(a, b):
    return silu(a) * b


def gate_up(ab):
    w = FFN_SHARD
    return (jnp.concatenate([ab[:, 2 * w * k:2 * w * k + w] for k in range(4)], axis=1),
            jnp.concatenate([ab[:, 2 * w * k + w:2 * w * (k + 1)] for k in range(4)], axis=1))


def per_chip(gate, up):
    w = FFN_SHARD
    return jnp.concatenate([part[:, w * k:w * (k + 1)] for k in range(4) for part in (gate, up)], axis=1)


def stage_loss(x1, hf, tgt, g2, ln_g, ln_b):
    x2 = _ln(ALPHA * x1 + g2 * hf) * ln_g + ln_b
    return 0.5 * jnp.sum(jnp.mean(jnp.square(x2 - tgt), axis=-1, keepdims=True), axis=0, keepdims=True)


def local_step(x, tgt, mod, wts, small, early=None, mid=None, late=None, last=None):
    sh1, sc1, g1, sh2, sc2, g2 = mod
    lb, gn, conv_w, conv_b, dtb, alog, dsk, nw, ln1_g, ln1_b, ln2_g, ln2_b = small
    vec = (1, D)

    (u1,) = rowwise("modulate1", lambda r, c: ((stage_modulate(r[0], *c),), ()), [_full(x)], [sc1, sh1], [(D, BF16)])
    w_in = wts.input_projection(u1)
    proj = matmul(u1, w_in, "nn", F32, "in_proj")
    ya_in, st_a = hgrn_forward(proj, lb, gn + wts.start_rest(proj)[0:1])
    xc = conv_forward(proj, conv_w, conv_b)
    w_a, w_b, w_o, w_gu, w_d = wts.rest(xc)
    yb_in, st_b = ssd_forward(xc, proj, dtb, alog, dsk, nw)
    ya = matmul(ya_in, w_a, "nn", F32, "branch_a")
    yb = matmul(yb_in, w_b, "nn", F32, "branch_b")
    gate_rows = [(proj, D, COL_GA // D), (proj, D, COL_GB // D), _full(ya), _full(yb)]
    (merged,) = rowwise("merge", lambda r, c: ((stage_merge(*r),), ()), gate_rows, [], [(D, BF16)])
    h = matmul(merged, w_o, "nn", F32, "out_proj")
    post_consts = [g1, ln1_g, ln1_b, sc2, sh2]
    x1, u2 = rowwise("post_mixer", lambda r, c: (stage_post_mixer(*r, *c), ()), [_full(x), _full(h)], post_consts,
                     [(D, F32), (D, BF16)])
    ab = matmul(u2, w_gu, "nt", F32, "ffn_in")
    (p,) = rowwise("swiglu", lambda r, c: ((stage_swiglu(*gate_up(r[0])),), ()), [_full(ab)], [], [(D_FF, BF16)])
    hf = matmul(p, w_d, "nn", F32, "ffn_out")

    def loss_bwd(r, c):
        loss, vjp = jax.vjp(stage_loss, *r, *c)
        dx1, dhf, _, dg2, dlg, dlb_ = vjp(jnp.ones((1, 1), F32))
        return (dx1, dhf), (loss, dg2, dlg, dlb_)

    dx1, dhf, loss, dg2, dln2_g, dln2_b = rowwise(
        "loss_backward", loss_bwd, [_full(x1), _full(hf), _full(tgt)], [g2, ln2_g, ln2_b],
        [(D, F32), (D, BF16)], [(1, 1), vec, vec, vec])
    dp = matmul(dhf, w_d, "nt", F32, "ffn_out_dx")
    dw_d = matmul(p, dhf, "tn", F32, "ffn_out_dw")

    def swiglu_bwd(r, c):
        _, vjp = jax.vjp(stage_swiglu, *gate_up(r[0]))
        return (per_chip(*vjp(r[1])),), ()

    (dab,) = rowwise("swiglu_backward", swiglu_bwd, [_full(ab), _full(dp)], [], [(2 * D_FF, BF16)])
    du2 = matmul(dab, w_gu, "nn", F32, "ffn_in_dx")
    dw_gu = matmul(dab, u2, "tn", F32, "ffn_in_dw")

    def post_bwd(r, c):
        _, vjp = jax.vjp(stage_post_mixer, r[0], r[1], *c)
        dx, dh, *dc = vjp((r[2], r[3]))
        return (dx, dh), tuple(dc)

    dx_a, dh, dg1, dln1_g, dln1_b, dsc2, dsh2 = rowwise(
        "post_mixer_backward", post_bwd, [_full(x), _full(h), _full(dx1), _full(du2)], post_consts,
        [(D, F32), (D, BF16)], [vec] * 5)
    dmerged = matmul(dh, w_o, "nt", F32, "out_proj_dx")
    dw_o = matmul(merged, dh, "tn", F32, "out_proj_dw")

    def merge_bwd(r, c):
        _, vjp = jax.vjp(stage_merge, *r[:4])
        dga, dgb, dya, dyb = vjp(r[4])
        return (jnp.concatenate([dga, dgb], axis=1), dya, dyb), ()

    dproj, dya, dyb = rowwise("merge_backward", merge_bwd, gate_rows + [_full(dmerged)], [],
                              [(2 * D, BF16), (D, BF16), (D, BF16)], new_wide=(IN_PAD, COL_GA // (2 * D)))
    dya_in = matmul(dya, w_a, "nt", F32, "branch_a_dx")
    dw_a = matmul(ya_in, dya, "tn", F32, "branch_a_dw")
    dyb_in = matmul(dyb, w_b, "nt", F32, "branch_b_dx")
    dw_b = matmul(yb_in, dyb, "tn", F32, "branch_b_dw")
    gn_after = gn if early is None else gn + early((dw_a, dw_b, dw_o, dw_gu, dw_d))[0:1]
    dproj, dlb, dgn = hgrn_backward(proj, st_a, dya_in, lb, gn_after, dproj)
    dtb_after = dtb if mid is None else dtb + mid(dlb)[0:1, 0:1]
    dxs, dbm, dcm, ddt, dproj, ddtb, dalog, ddsk, dnw = ssd_backward(xc, proj, st_b, dyb_in, dtb_after, alog, dsk, nw, dproj)
    dproj, dconv_w, dconv_b = conv_backward(proj, dxs, dbm, dcm, conv_w, conv_b, dproj)
    if late is not None:
        late(dconv_b)
    t = x.shape[0]
    tail = jnp.concatenate([jnp.sum(ddt, axis=0).astype(BF16), jnp.zeros((t, IN_PAD - COL_DT - LANES), BF16)], axis=1)
    dproj = lax.dynamic_update_slice(dproj, tail, (0, COL_DT))
    dw_in = matmul(u1, dproj, "tn", F32, "in_proj_dw")
    du1 = matmul(dproj, w_in, "nt", F32, "in_proj_dx", after=None if last is None else last(dw_in))

    def mod_bwd(r, c):
        _, vjp = jax.vjp(stage_modulate, r[0], *c)
        dx, dsc, dsh = vjp(r[1])
        return (dx + r[2],), (dsc, dsh)

    grad_x, dsc1, dsh1 = rowwise("modulate1_backward", mod_bwd, [_full(x), _full(du1), _full(dx_a)], [sc1, sh1],
                                 [(D, F32)], [vec, vec])
    d_mod = (dsh1, dsc1, dg1, dsh2, dsc2, dg2)
    d_wts = (dw_in, dw_a, dw_b, dw_o, dw_gu, dw_d)
    d_small = (dlb, dgn, dconv_w, dconv_b, jnp.sum(ddtb, axis=0),
               jnp.sum(dalog, axis=0), ddsk.reshape(1, B_INNER), dnw.reshape(1, B_INNER),
               dln1_g, dln1_b, dln2_g, dln2_b)
    return loss, grad_x, d_mod, d_wts, d_small


HBM = pl.BlockSpec(memory_space=pltpu.HBM)
SEM = pl.BlockSpec(memory_space=pltpu.SEMAPHORE)
DATAFLOW = pltpu.SideEffectType.DATAFLOW_SIDE_EFFECTING


def _place():
    return lax.axis_index("x"), lax.axis_index("y"), lax.axis_index("c")


def _other_chips(x, y):
    return [(1 - x, y), (x, 1 - y), (1 - x, 1 - y)]


def _remote(src, dst, send_sem, recv_sem, device):
    return pltpu.make_async_remote_copy(src_ref=src, dst_ref=dst, send_sem=send_sem, recv_sem=recv_sem,
                                        device_id=device, device_id_type=MESH)


def gather_rows(v, name):
    n = v.shape[1]

    def body(v_ref, out_ref, send_sems, recv_sems, local_sem):
        x, y, c = _place()
        mine = pltpu.make_async_copy(v_ref, out_ref.at[4 * x + 2 * y + c], local_sem)
        mine.start()
        sends, recvs = [], []
        for m in range(1, 8):
            px = 1 - x if m & 4 else x
            py = 1 - y if m & 2 else y
            pc = 1 - c if m & 1 else c
            sends.append(_remote(v_ref, out_ref.at[4 * x + 2 * y + c], send_sems.at[m - 1], recv_sems.at[m - 1], (px, py, pc)))
            recvs.append(_remote(v_ref, out_ref.at[4 * px + 2 * py + pc], send_sems.at[m - 1], recv_sems.at[m - 1], (px, py, pc)))
        for cp in sends:
            cp.start()
        for cp in recvs:
            cp.wait_recv()
        for cp in sends:
            cp.wait_send()
        mine.wait()

    return pl.pallas_call(
        body, name=name, in_specs=[HBM], out_specs=HBM,
        out_shape=jax.ShapeDtypeStruct((8, 1, n), v.dtype),
        scratch_shapes=[pltpu.SemaphoreType.DMA((7,)), pltpu.SemaphoreType.DMA((7,)), pltpu.SemaphoreType.DMA],
    )(v)


def exchange_rows(part, name):
    w = part.shape[2]

    def body(p_ref, out_ref, send_sems, recv_sems, local_sem):
        x, y, c = _place()
        k = 2 * x + y
        mine = pltpu.make_async_copy(p_ref.at[4 * x + 2 * y + c], out_ref.at[k], local_sem)
        mine.start()
        sends, recvs = [], []
        for j, (px, py) in enumerate(_other_chips(x, y)):
            sends.append(_remote(p_ref.at[4 * px + 2 * py + c], out_ref.at[k], send_sems.at[j], recv_sems.at[j], (px, py, c)))
            recvs.append(_remote(p_ref.at[4 * px + 2 * py + c], out_ref.at[2 * px + py], send_sems.at[j], recv_sems.at[j], (px, py, c)))
        for cp in sends:
            cp.start()
        for cp in recvs:
            cp.wait_recv()
        for cp in sends:
            cp.wait_send()
        mine.wait()

    return pl.pallas_call(
        body, name=name, in_specs=[HBM], out_specs=HBM,
        out_shape=jax.ShapeDtypeStruct((4, 1, w), part.dtype),
        scratch_shapes=[pltpu.SemaphoreType.DMA((3,)), pltpu.SemaphoreType.DMA((3,)), pltpu.SemaphoreType.DMA],
    )(part)


def _half_of_slot(ref, rows, px, py, pc):
    return ref.at[2 * px + py, pl.ds(pc * (rows // 2), rows // 2), :]


def gather_start(shards, after):
    n = len(shards)

    def body(*refs):
        w_refs, land_refs = refs[:n], refs[n:2 * n]
        send_a, recv_a, send_b, recv_b = refs[2 * n + 1:2 * n + 5]
        token = refs[-1]
        x, y, c = _place()
        for i in range(n):
            rows = shards[i].shape[0]
            for j, (px, py) in enumerate(_other_chips(x, y)):
                sems = (send_a.at[j], recv_a.at[j]) if i == 0 else (send_b.at[j * (n - 1) + i - 1], recv_b.at[j * (n - 1) + i - 1])
                _remote(w_refs[i].at[pl.ds(c * (rows // 2), rows // 2), :], _half_of_slot(land_refs[i], rows, x, y, c),
                        *sems, (px, py, c)).start()
        token[...] = jnp.zeros_like(token)

    hbm = lambda a: pltpu.with_memory_space_constraint(a, pltpu.HBM)
    lands = [lax.empty((4,) + s.shape, s.dtype) for s in shards]
    dma = pltpu.SemaphoreType.DMA
    return pl.pallas_call(
        body, name="gather_start",
        out_shape=(dma((3,)), dma((3,)), dma((3 * (n - 1),)), dma((3 * (n - 1),)),
                   *[pltpu.HBM(a.shape, a.dtype) for a in list(shards) + lands], jax.ShapeDtypeStruct((8, LANES), F32)),
        in_specs=[HBM] * (2 * n) + [pl.BlockSpec(memory_space=pl.ANY)],
        out_specs=(SEM, SEM, SEM, SEM, *[HBM] * (2 * n), pl.BlockSpec(memory_space=pltpu.VMEM)),
        input_output_aliases={i: 4 + i for i in range(2 * n)},
        compiler_params=pltpu.CompilerParams(has_side_effects=DATAFLOW),
    )(*[hbm(a) for a in list(shards) + lands], after)


def gather_wait(send_sems, recv_sems, shards, lands, after, tag):
    n = len(shards)

    def body(*refs):
        w_refs, land_refs = refs[:n], refs[n:2 * n]
        send_ref, recv_ref = refs[2 * n], refs[2 * n + 1]
        x, y, c = _place()
        for i in range(n):
            rows = shards[i].shape[0]
            for j, (px, py) in enumerate(_other_chips(x, y)):
                cp = _remote(w_refs[i].at[pl.ds(c * (rows // 2), rows // 2), :], _half_of_slot(land_refs[i], rows, px, py, c),
                             send_ref.at[j * n + i], recv_ref.at[j * n + i], (px, py, c))
                cp.wait_send()
                cp.wait_recv()

    out = pl.pallas_call(
        body, name="gather_wait_" + tag,
        out_shape=tuple(pltpu.HBM(a.shape, a.dtype) for a in list(shards) + list(lands)),
        in_specs=[HBM] * (2 * n) + [SEM, SEM, pl.BlockSpec(memory_space=pl.ANY)], out_specs=tuple([HBM] * (2 * n)),
        input_output_aliases={i: i for i in range(2 * n)},
        compiler_params=pltpu.CompilerParams(has_side_effects=DATAFLOW),
    )(*shards, *lands, send_sems, recv_sems, after)
    return list(out[:n]), list(out[n:])


def forward_start(lands, tag):
    n = len(lands)

    def body(*refs):
        land_refs = refs[:n]
        send_sems, recv_sems = refs[n], refs[n + 1]
        token = refs[-1]
        x, y, c = _place()
        for i in range(n):
            rows = lands[i].shape[1]
            for j, (px, py) in enumerate(_other_chips(x, y)):
                mine = _half_of_slot(land_refs[i], rows, px, py, c)
                _remote(mine, mine, send_sems.at[j * n + i], recv_sems.at[j * n + i], (x, y, 1 - c)).start()
        token[...] = jnp.zeros_like(token)

    dma = pltpu.SemaphoreType.DMA
    return pl.pallas_call(
        body, name="forward_start_" + tag,
        out_shape=(dma((3 * n,)), dma((3 * n,)), *[pltpu.HBM(a.shape, a.dtype) for a in lands],
                   jax.ShapeDtypeStruct((8, LANES), F32)),
        in_specs=[HBM] * n, out_specs=(SEM, SEM, *[HBM] * n, pl.BlockSpec(memory_space=pltpu.VMEM)),
        input_output_aliases={i: 2 + i for i in range(n)},
        compiler_params=pltpu.CompilerParams(has_side_effects=DATAFLOW),
    )(*lands)


def forward_wait(started, after, tag):
    send_sems, recv_sems, *rest = started
    lands = rest[:-1]
    n = len(lands)

    def body(*refs):
        land_refs = refs[:n]
        send_ref, recv_ref = refs[n], refs[n + 1]
        x, y, c = _place()
        for i in range(n):
            rows = lands[i].shape[1]
            for j, (px, py) in enumerate(_other_chips(x, y)):
                cp = _remote(_half_of_slot(land_refs[i], rows, px, py, c), _half_of_slot(land_refs[i], rows, px, py, 1 - c),
                             send_ref.at[j * n + i], recv_ref.at[j * n + i], (x, y, 1 - c))
                cp.wait_send()
                cp.wait_recv()

    out = pl.pallas_call(
        body, name="forward_wait_" + tag,
        out_shape=tuple(pltpu.HBM(a.shape, a.dtype) for a in lands),
        in_specs=[HBM] * n + [SEM, SEM, pl.BlockSpec(memory_space=pl.ANY)], out_specs=tuple([HBM] * n),
        input_output_aliases={i: i for i in range(n)},
        compiler_params=pltpu.CompilerParams(has_side_effects=DATAFLOW),
    )(*lands, send_sems, recv_sems, after)
    return list(out)


def pair_start(slabs, tag):
    n = len(slabs)

    def body(*refs):
        g_refs, land_refs = refs[:n], refs[n:2 * n]
        send_sems, recv_sems = refs[2 * n], refs[2 * n + 1]
        token = refs[-1]
        x, y, c = _place()
        for i in range(n):
            hr = slabs[i].shape[1] // 2
            _remote(g_refs[i].at[:, pl.ds((1 - c) * hr, hr), :], land_refs[i], send_sems.at[i], recv_sems.at[i],
                    (x, y, 1 - c)).start()
        token[...] = jnp.zeros_like(token)

    hbm = lambda a: pltpu.with_memory_space_constraint(a, pltpu.HBM)
    lands = [lax.empty((4, s.shape[1] // 2, s.shape[2]), s.dtype) for s in slabs]
    dma = pltpu.SemaphoreType.DMA
    return pl.pallas_call(
        body, name="pair_start_" + tag,
        out_shape=(dma((n,)), dma((n,)), *[pltpu.HBM(a.shape, a.dtype) for a in list(slabs) + lands],
                   jax.ShapeDtypeStruct((8, LANES), F32)),
        in_specs=[HBM] * (2 * n), out_specs=(SEM, SEM, *[HBM] * (2 * n), pl.BlockSpec(memory_space=pltpu.VMEM)),
        input_output_aliases={i: 2 + i for i in range(2 * n)},
        compiler_params=pltpu.CompilerParams(has_side_effects=DATAFLOW),
    )(*[hbm(a) for a in list(slabs) + lands])


def pair_wait(started, after, tag):
    send_sems, recv_sems, *rest = started
    n = (len(rest) - 1) // 2
    slabs, lands = rest[:n], rest[n:2 * n]

    def body(*refs):
        g_refs, land_refs = refs[:n], refs[n:2 * n]
        send_ref, recv_ref = refs[2 * n], refs[2 * n + 1]
        x, y, c = _place()
        for i in range(n):
            hr = slabs[i].shape[1] // 2
            cp = _remote(g_refs[i].at[:, pl.ds((1 - c) * hr, hr), :], land_refs[i], send_ref.at[i], recv_ref.at[i], (x, y, 1 - c))
            cp.wait_send()
            cp.wait_recv()

    out = pl.pallas_call(
        body, name="pair_wait_" + tag,
        out_shape=tuple(pltpu.HBM(a.shape, a.dtype) for a in list(slabs) + list(lands)),
        in_specs=[HBM] * (2 * n) + [SEM, SEM, pl.BlockSpec(memory_space=pl.ANY)], out_specs=tuple([HBM] * (2 * n)),
        input_output_aliases={i: i for i in range(2 * n)},
        compiler_params=pltpu.CompilerParams(has_side_effects=DATAFLOW),
    )(*slabs, *lands, send_sems, recv_sems, after)
    return list(out[:n]), list(out[n:])


def _tile2(rows, cols):
    fits = lambda r, c: r * c * 4 <= BLOCK_BYTES
    if fits(rows, cols):
        return rows, cols
    for r in (1024, 512, 256, 128, 64):
        if rows % r == 0 and fits(r, cols):
            return r, cols
    return rows, next(cols // k for k in (2, 3, 4, 6, 8, 12, 16) if cols % (k * LANES) == 0 and fits(rows, cols // k))


def pair_add(g, p, c, name):
    _, hr, cols = p.shape
    tm, tc = _tile2(hr, cols)
    per = hr // tm

    def body(c_ref, g_ref, p_ref, o_ref):
        o_ref[...] = (g_ref[...] + p_ref[...]).astype(o_ref.dtype)

    return pl.pallas_call(
        body, name=name,
        grid_spec=pltpu.PrefetchScalarGridSpec(
            num_scalar_prefetch=1, grid=(4, per, cols // tc),
            in_specs=[pl.BlockSpec((None, tm, tc), lambda k, i, j, c_ref: (k, c_ref[0] * per + i, j)),
                      pl.BlockSpec((None, tm, tc), lambda k, i, j, c_ref: (k, i, j))],
            out_specs=pl.BlockSpec((None, tm, tc), lambda k, i, j, c_ref: (k, i, j))),
        out_shape=jax.ShapeDtypeStruct((4, hr, cols), BF16),
        compiler_params=_params(("arbitrary", "arbitrary", "arbitrary")),
    )(c.reshape(1).astype(jnp.int32), g, p)


def scatter_start(sums, tag):
    n = len(sums)

    def body(*refs):
        s_refs, land_refs = refs[:n], refs[n:2 * n]
        send_sems, recv_sems = refs[2 * n], refs[2 * n + 1]
        token = refs[-1]
        x, y, c = _place()
        k = 2 * x + y
        for i in range(n):
            for j, (px, py) in enumerate(_other_chips(x, y)):
                _remote(s_refs[i].at[2 * px + py], land_refs[i].at[k], send_sems.at[j * n + i], recv_sems.at[j * n + i],
                        (px, py, c)).start()
        token[...] = jnp.zeros_like(token)

    hbm = lambda a: pltpu.with_memory_space_constraint(a, pltpu.HBM)
    return pl.pallas_call(
        body, name="scatter_start_" + tag,
        out_shape=(pltpu.SemaphoreType.DMA((3 * n,)), pltpu.SemaphoreType.DMA((3 * n,)),
                   *[pltpu.HBM(s.shape, s.dtype) for s in sums], *[pltpu.HBM(s.shape, s.dtype) for s in sums],
                   jax.ShapeDtypeStruct((8, LANES), F32)),
        in_specs=[HBM] * (2 * n), out_specs=(SEM, SEM, *[HBM] * (2 * n), pl.BlockSpec(memory_space=pltpu.VMEM)),
        input_output_aliases={i: 2 + i for i in range(2 * n)},
        compiler_params=pltpu.CompilerParams(has_side_effects=DATAFLOW),
    )(*[hbm(s) for s in sums], *[hbm(lax.empty(s.shape, s.dtype)) for s in sums])


def scatter_wait(started, after, tag):
    send_sems, recv_sems, *rest = started
    n = (len(rest) - 1) // 2
    sums, lands = rest[:n], rest[n:2 * n]

    def body(*refs):
        s_refs, land_refs = refs[:n], refs[n:2 * n]
        send_ref, recv_ref = refs[2 * n], refs[2 * n + 1]
        x, y, c = _place()
        for i in range(n):
            for j, (px, py) in enumerate(_other_chips(x, y)):
                cp = _remote(s_refs[i].at[2 * px + py], land_refs[i].at[2 * px + py], send_ref.at[j * n + i],
                             recv_ref.at[j * n + i], (px, py, c))
                cp.wait_send()
                cp.wait_recv()

    out = pl.pallas_call(
        body, name="scatter_wait_" + tag,
        out_shape=tuple(pltpu.HBM(s.shape, s.dtype) for s in sums + lands),
        in_specs=[HBM] * (2 * n) + [SEM, SEM, pl.BlockSpec(memory_space=pl.ANY)], out_specs=tuple([HBM] * (2 * n)),
        input_output_aliases={i: i for i in range(2 * n)},
        compiler_params=pltpu.CompilerParams(has_side_effects=DATAFLOW),
    )(*sums, *lands, send_sems, recv_sems, after)
    return list(out[:n]), list(out[n:])


def sum_chips(landed, own, chip, core, name):
    _, hr, cols = landed.shape
    tm, tc = _tile2(hr, cols)
    per = hr // tm

    def body(idx_ref, l0, l1, l2, l3, own_ref, o_ref):
        mine = own_ref[...].astype(F32)
        v = [jnp.where(idx_ref[0] == k, mine, ref[...].astype(F32)) for k, ref in enumerate((l0, l1, l2, l3))]
        o_ref[...] = ((v[0] + v[1]) + v[2]) + v[3]

    slot = lambda k: pl.BlockSpec((None, tm, tc),
                                  lambda i, j, idx: (jnp.where(idx[0] == k, (k + 1) & 3, k), i, j))
    return pl.pallas_call(
        body, name=name,
        grid_spec=pltpu.PrefetchScalarGridSpec(
            num_scalar_prefetch=1, grid=(per, cols // tc),
            in_specs=[slot(0), slot(1), slot(2), slot(3),
                      pl.BlockSpec((None, tm, tc), lambda i, j, idx: (idx[0], i, j))],
            out_specs=pl.BlockSpec((tm, tc), lambda i, j, idx: (idx[1] * per + i, j))),
        out_shape=jax.ShapeDtypeStruct((2 * hr, cols), F32),
        compiler_params=_params(("arbitrary", "arbitrary")),
    )(jnp.stack([chip, core]).astype(jnp.int32), landed, landed, landed, landed, own)


def exchange_halves(bufs):
    n = len(bufs)

    def body(*refs):
        out_refs = refs[n:2 * n]
        send_sems, recv_sems = refs[2 * n:]
        x, y, c = _place()
        sends, recvs = [], []
        for i in range(n):
            hr = bufs[i].shape[0] // 2
            own = out_refs[i].at[pl.ds(c * hr, hr), :]
            other = out_refs[i].at[pl.ds((1 - c) * hr, hr), :]
            sends.append(_remote(own, own, send_sems.at[i], recv_sems.at[i], (x, y, 1 - c)))
            recvs.append(_remote(other, other, send_sems.at[i], recv_sems.at[i], (x, y, 1 - c)))
        for cp in sends:
            cp.start()
        for cp in recvs:
            cp.wait_recv()
        for cp in sends:
            cp.wait_send()

    return pl.pallas_call(
        body, name="exchange_halves", in_specs=[HBM] * n, out_specs=[HBM] * n,
        out_shape=[jax.ShapeDtypeStruct(b.shape, b.dtype) for b in bufs],
        input_output_aliases={i: i for i in range(n)},
        scratch_shapes=[pltpu.SemaphoreType.DMA((n,)), pltpu.SemaphoreType.DMA((n,))],
    )(*bufs)


def _relayout(name, arrays, in_blocks, out_blocks, out_shapes, fn):
    rows = 128
    spec = lambda blk: pl.BlockSpec(blk, (lambda i: (0, i, 0)) if len(blk) == 3 else (lambda i: (i, 0)))

    def body(*refs):
        n_in = len(arrays)
        outs = fn(*[r[...] for r in refs[:n_in]])
        for ref, val in zip(refs[n_in:], outs, strict=True):
            if isinstance(val, list):
                for k, piece in enumerate(val):
                    ref[k] = piece
            else:
                ref[...] = val

    return pl.pallas_call(
        body, name=name, grid=(D // rows,),
        in_specs=[spec(b) for b in in_blocks], out_specs=[spec(b) for b in out_blocks], out_shape=out_shapes,
        compiler_params=_params(("arbitrary",)),
    )(*arrays)


def assemble_in_proj(g):
    def fn(v):
        w = jnp.concatenate([v[k] for k in range(4)], axis=1)
        return (jnp.concatenate([w[:, :ORIG_Z], w[:, ORIG_GA:], w[:, ORIG_XBC:ORIG_DT], w[:, ORIG_Z:ORIG_XBC],
                                 w[:, ORIG_DT:ORIG_GA], jnp.zeros((w.shape[0], IN_PAD - IN_ORIG), w.dtype)], axis=1),)

    cols = g.shape[2]
    return _relayout("assemble_in_proj", [g], [(4, 128, cols)], [(128, IN_PAD)],
                     [jax.ShapeDtypeStruct((D, IN_PAD), g.dtype)], fn)[0]


def rows_exchange(a, name):
    hr = a.shape[0] // 2

    def body(a_ref, out_ref, send_sem, recv_sem):
        x, y, c = _place()
        cp = _remote(a_ref.at[pl.ds((1 - c) * hr, hr), :], out_ref, send_sem, recv_sem, (x, y, 1 - c))
        cp.start()
        cp.wait()

    return pl.pallas_call(
        body, name=name, in_specs=[HBM], out_specs=HBM,
        out_shape=jax.ShapeDtypeStruct((hr, a.shape[1]), a.dtype),
        scratch_shapes=[pltpu.SemaphoreType.DMA, pltpu.SemaphoreType.DMA],
    )(a)


def split_pair_add(dw, received, core):
    cols = IN_ORIG // 4
    rows, hr = 128, D // 2
    per = hr // rows

    def body(c_ref, own_ref, got_ref, o_ref):
        d = own_ref[...] + got_ref[...]
        w = jnp.concatenate([d[:, :COL_GA], d[:, COL_Z:COL_DT], d[:, COL_XBC:COL_Z], d[:, COL_DT:COL_DT + 32],
                             d[:, COL_GA:COL_XBC]], axis=1)
        for k in range(4):
            o_ref[k] = w[:, k * cols:(k + 1) * cols].astype(o_ref.dtype)

    return pl.pallas_call(
        body, name="split_pair_add",
        grid_spec=pltpu.PrefetchScalarGridSpec(
            num_scalar_prefetch=1, grid=(per,),
            in_specs=[pl.BlockSpec((rows, IN_PAD), lambda i, c_ref: (c_ref[0] * per + i, 0)),
                      pl.BlockSpec((rows, IN_PAD), lambda i, c_ref: (i, 0))],
            out_specs=pl.BlockSpec((4, rows, cols), lambda i, c_ref: (0, i, 0))),
        out_shape=jax.ShapeDtypeStruct((4, hr, cols), BF16),
        compiler_params=_params(("arbitrary",)),
    )(core.reshape(1).astype(jnp.int32), dw, received)


def ada_prepare(c_all, w_ada, hgrn_lb):
    def body(c_ref, w_ref, lb_ref, mod_ref, row_ref):
        mod_ref[...] = hdot(silu(c_ref[...]), w_ref[...])
        row_ref[...] = sigmoid(lb_ref[0:1, :] - lb_ref[1:2, :])

    return pl.pallas_call(
        body, name="ada_prepare",
        out_shape=[jax.ShapeDtypeStruct((8, w_ada.shape[1]), F32), jax.ShapeDtypeStruct((1, D), F32)],
        compiler_params=pltpu.CompilerParams(vmem_limit_bytes=VMEM_LIMIT),
    )(c_all, w_ada, hgrn_lb)


SMALL_SEGS = (("mod", 6 * D), ("lb", D), ("gnorm", LANES), ("conv_w", 4 * CONV_DIM), ("conv_b", CONV_DIM),
              ("dt_bias", LANES), ("a_log", LANES), ("d", B_INNER), ("ssm_norm", B_INNER),
              ("ln1_g", D), ("ln1_b", D), ("ln2_g", D), ("ln2_b", D), ("loss", LANES))
SMALL_PARAMS = ("b_ada", "hgrn_lb", "hgrn_gnorm", "ssm_conv_b", "ssm_dt_bias", "ssm_a_log", "ssm_d", "ssm_norm",
                "ln1_g", "ln1_b", "ln2_g", "ln2_b")


def finalize_small(g_all, c_all, dmod_cols, params, m, v):
    n_p = len(SMALL_PARAMS)
    offs, o = {}, 0
    for nm, width in SMALL_SEGS:
        offs[nm] = (o, width)
        o += width

    def body(*refs):
        g_ref, c_ref, dm_ref = refs[:3]
        p_refs = refs[3:3 + n_p]
        m_refs = refs[3 + n_p:3 + 2 * n_p]
        v_refs = refs[3 + 2 * n_p:3 + 3 * n_p]
        outs = refs[3 + 3 * n_p:]
        gwa_ref, gcw_ref, loss_ref = outs[:3]
        res = outs[3:]
        total = jnp.sum(g_ref[...], axis=0, keepdims=True)
        seg = lambda nm: total[:, offs[nm][0]:offs[nm][0] + offs[nm][1]]
        loss_ref[...] = seg("loss")
        gwa_ref[...] = hdot(silu(c_ref[...]), dm_ref[...], "tn")
        cw = seg("conv_w")
        for j in range(4):
            gcw_ref[j:j + 1, :] = cw[:, j * CONV_DIM:(j + 1) * CONV_DIM]
        hc = lax.broadcasted_iota(jnp.int32, (B_INNER, LANES), 0)
        hj = lax.broadcasted_iota(jnp.int32, (B_INNER, LANES), 1)
        per_head = ((hc >> 6) == hj).astype(F32)
        heads = lambda nm: hdot(jnp.broadcast_to(seg(nm), (8, B_INNER)), per_head)[0:1, 0:32]
        lbp = sigmoid(p_refs[1][0:1, :] - p_refs[1][1:2, :])
        g_row = seg("lb") * lbp * (1.0 - lbp)
        grads = {"b_ada": seg("mod"), "hgrn_gnorm": seg("gnorm"), "ssm_conv_b": seg("conv_b"),
                 "ssm_dt_bias": seg("dt_bias")[:, 0:32], "ssm_a_log": seg("a_log")[:, 0:32], "ssm_d": heads("d"),
                 "ssm_norm": seg("ssm_norm"), "ln1_g": seg("ln1_g"), "ln1_b": seg("ln1_b"),
                 "ln2_g": seg("ln2_g"), "ln2_b": seg("ln2_b")}
        for i, nm in enumerate(SMALL_PARAMS):
            g_out, d_out, m_out, v_out = res[4 * i:4 * i + 4]
            if nm == "hgrn_lb":
                for row, gv in ((0, g_row), (1, -g_row)):
                    sl = slice(row, row + 1)
                    dl, mn, vn = adamw(p_refs[i][sl, :], gv, m_refs[i][sl, :], v_refs[i][sl, :])
                    g_out[sl, :], d_out[sl, :], m_out[sl, :], v_out[sl, :] = gv, dl, mn, vn
            else:
                gv = grads[nm]
                dl, mn, vn = adamw(p_refs[i][...], gv, m_refs[i][...], v_refs[i][...])
                g_out[...], d_out[...], m_out[...], v_out[...] = gv, dl, mn, vn

    out_shape = [jax.ShapeDtypeStruct((D, dmod_cols.shape[1]), F32), jax.ShapeDtypeStruct((4, CONV_DIM), F32),
                 jax.ShapeDtypeStruct((1, LANES), F32)]
    for p in params:
        out_shape += [jax.ShapeDtypeStruct(p.shape, F32)] * 4
    return pl.pallas_call(
        body, name="finalize_small", out_shape=out_shape,
        compiler_params=pltpu.CompilerParams(vmem_limit_bytes=VMEM_LIMIT),
    )(g_all, c_all, dmod_cols, *params, *m, *v)


def adam_update(w, g, m, v, name):
    rows, cols = w.shape
    tm, tc = _tile2(rows, cols)

    def body(w_ref, g_ref, m_ref, v_ref, d_ref, mo_ref, vo_ref):
        d_ref[...], mo_ref[...], vo_ref[...] = adamw(w_ref[...], g_ref[...], m_ref[...], v_ref[...])

    spec = pl.BlockSpec((tm, tc), lambda i, j: (i, j))
    return pl.pallas_call(
        body, name=name, grid=(rows // tm, cols // tc), in_specs=[spec] * 4, out_specs=[spec] * 3,
        out_shape=[jax.ShapeDtypeStruct((rows, cols), F32)] * 3,
        compiler_params=_params(("arbitrary", "arbitrary")),
    )(w, g, m, v)


def kernel(x, c, w_ada, b_ada, w_in, hgrn_lb, hgrn_gnorm, ssm_conv_w, ssm_conv_b, ssm_dt_bias, ssm_a_log, ssm_d, ssm_norm, w_branch_a, w_branch_b, w_o, ln1_g, ln1_b, w_ffn_gate, w_ffn_up, w_ffn_down, ln2_g, ln2_b, loss_target, m_w_ada, m_b_ada, m_w_in, m_hgrn_lb, m_hgrn_gnorm, m_ssm_conv_w, m_ssm_conv_b, m_ssm_dt_bias, m_ssm_a_log, m_ssm_d, m_ssm_norm, m_w_branch_a, m_w_branch_b, m_w_o, m_ln1_g, m_ln1_b, m_w_ffn_gate, m_w_ffn_up, m_w_ffn_down, m_ln2_g, m_ln2_b, v_w_ada, v_b_ada, v_w_in, v_hgrn_lb, v_hgrn_gnorm, v_ssm_conv_w, v_ssm_conv_b, v_ssm_dt_bias, v_ssm_a_log, v_ssm_d, v_ssm_norm, v_w_branch_a, v_w_branch_b, v_w_o, v_ln1_g, v_ln1_b, v_w_ffn_gate, v_w_ffn_up, v_w_ffn_down, v_ln2_g, v_ln2_b):
    given = dict(locals())
    chip = 2 * lax.axis_index("x") + lax.axis_index("y")
    core = lax.axis_index("c")
    t = x.shape[1]

    first = gather_rows(jnp.concatenate([c, ssm_conv_w.reshape(1, CONV_DIM)], axis=1), "gather_cond").reshape(8, D + CONV_DIM)
    c_all = first[:, :D]
    conv_w = first[0::2, D:].reshape(4, 4, CONV_DIM // 4).transpose(1, 0, 2).reshape(4, CONV_DIM)
    mod_part, lb_row = ada_prepare(c_all, w_ada[0], hgrn_lb)
    mod_cols = w_ada.shape[2]
    mod_row = exchange_rows(mod_part.reshape(8, 1, mod_cols), "exchange_mod").reshape(1, 6 * D) + b_ada
    mod = tuple(mod_row[:, i * D:(i + 1) * D] for i in range(6))

    local = {nm: given[nm][0] for nm in SHARDED if nm != "w_ffn_in"}
    local["w_ffn_in"] = jnp.concatenate([w_ffn_gate[0].T, w_ffn_up[0].T], axis=0)
    shards = [local[nm].astype(BF16) for nm in SHARDED]
    n_w = len(SHARDED)
    send_in, recv_in, send_rest, recv_rest, *flying = gather_start(shards, mod_row)
    sent, lands = flying[:n_w], flying[n_w:2 * n_w]
    with_own = lambda land, shard: lax.dynamic_update_slice(land, shard[None], (chip, 0, 0))

    class Weights:
        def input_projection(self, after):
            (own,), land = gather_wait(send_in, recv_in, sent[:1], lands[:1], after, "in")
            (land,) = forward_wait(forward_start(land, "in"), after, "in")
            return assemble_in_proj(with_own(land, own))

        def start_rest(self, after):
            self.own, landed = gather_wait(send_rest, recv_rest, sent[1:], lands[1:], after, "rest")
            self.started = forward_start(landed, "rest")
            return self.started[-1]

        def rest(self, after):
            got = {nm: with_own(land, s) for nm, land, s in zip(SHARDED[1:], forward_wait(self.started, after, "rest"), self.own, strict=True)}
            whole = lambda nm: got[nm].reshape(4 * got[nm].shape[1], got[nm].shape[2])
            return tuple(whole(nm) for nm in SHARDED[1:])

    wts = Weights()

    per_head = lambda p: jnp.pad(p, ((0, 0), (0, LANES - p.shape[1])))
    small = (lb_row, hgrn_gnorm, conv_w, ssm_conv_b, per_head(ssm_dt_bias), per_head(ssm_a_log),
             jnp.repeat(ssm_d[0], B_INNER // 32)[None], ssm_norm, ln1_g, ln1_b, ln2_g, ln2_b)
    by_rows = lambda g: g.reshape(4, g.shape[0] // 4, g.shape[1])
    travelling = {}

    def start_early(dws):
        travelling["pair"] = pair_start([by_rows(dw) for dw in dws], "early")
        return travelling["pair"][-1]

    def between_scans(after):
        slabs, received = pair_wait(travelling["pair"], after, "early")
        travelling["pairs"] = [pair_add(s, r, core, "pair_add_" + nm) for nm, s, r in zip(SHARDED[1:], slabs, received, strict=True)]
        travelling["started"] = scatter_start(travelling["pairs"], "early")
        return travelling["started"][-1]

    def finish_early(after):
        travelling["pairs"], travelling["landed"] = scatter_wait(travelling["started"], after, "early")

    def start_last(dw_in):
        travelling["pairs_in"] = [split_pair_add(dw_in, rows_exchange(dw_in, "pair_exchange_last"), core)]
        travelling["started_in"] = scatter_start(travelling["pairs_in"], "last")
        return travelling["started_in"][-1]

    loss, grad_x, d_mod, d_wts, d_small = local_step(x[0], loss_target[0], mod, wts, small,
                                                     start_early, between_scans, finish_early, start_last)

    d_lb, d_gn, d_cw, d_cb, d_dtb, d_alog, d_dsk, d_nw, d_l1g, d_l1b, d_l2g, d_l2b = d_small
    row = jnp.concatenate(list(d_mod) + [d_lb, d_gn, d_cw.reshape(1, 4 * CONV_DIM), d_cb, d_dtb, d_alog, d_dsk, d_nw,
                                          d_l1g, d_l1b, d_l2g, d_l2b, jnp.pad(loss, ((0, 0), (0, LANES - 1)))], axis=1)
    g_all = gather_rows(row, "gather_small_grads").reshape(8, row.shape[1])
    dmod_cols = lax.dynamic_slice_in_dim(g_all, chip * mod_cols, mod_cols, axis=1)
    fin = finalize_small(g_all, c_all, dmod_cols, [given[n] for n in SMALL_PARAMS],
                         [given["m_" + n] for n in SMALL_PARAMS], [given["v_" + n] for n in SMALL_PARAMS])
    grads, deltas, new_m, new_v = {}, {}, {}, {}
    grads["w_ada"] = fin[0][None]
    grads["ssm_conv_w"] = lax.dynamic_slice_in_dim(fin[1], chip * (CONV_DIM // 4), CONV_DIM // 4, axis=1)[None]
    for i, nm in enumerate(SMALL_PARAMS):
        grads[nm], deltas[nm], new_m[nm], new_v[nm] = fin[3 + 4 * i:7 + 4 * i]

    pairs_in, landed_in = scatter_wait(travelling["started_in"], fin[3], "last")
    pairs, landed = pairs_in + travelling["pairs"], landed_in + travelling["landed"]
    halves = [sum_chips(r, p, chip, core, "sum_chips_" + nm) for nm, r, p in zip(SHARDED, landed, pairs, strict=True)]
    reduced = dict(zip(SHARDED, exchange_halves(halves), strict=True))
    reduced["w_ada"], reduced["ssm_conv_w"] = grads["w_ada"][0], grads["ssm_conv_w"][0]
    reduced["w_in"] = reduced["w_in"].T
    reduced["w_ffn_gate"], reduced["w_ffn_up"] = reduced["w_ffn_in"][:FFN_SHARD], reduced["w_ffn_in"][FFN_SHARD:]
    for nm in ("w_ada", "ssm_conv_w", "w_in", "w_branch_a", "w_branch_b", "w_o", "w_ffn_gate", "w_ffn_up", "w_ffn_down"):
        flipped = nm in ("w_in", "w_ffn_gate", "w_ffn_up")
        work = (lambda a: a[0].T) if flipped else (lambda a: a[0])
        back = (lambda a: a.T[None]) if flipped else (lambda a: a[None])
        d_, m_, v_ = adam_update(work(given[nm]), reduced[nm], work(given["m_" + nm]), work(given["v_" + nm]), "adam_" + nm)
        grads[nm], deltas[nm], new_m[nm], new_v[nm] = back(reduced[nm]), back(d_), back(m_), back(v_)

    names = ("w_ada", "b_ada", "w_in", "hgrn_lb", "hgrn_gnorm", "ssm_conv_w", "ssm_conv_b", "ssm_dt_bias", "ssm_a_log",
             "ssm_d", "ssm_norm", "w_branch_a", "w_branch_b", "w_o", "ln1_g", "ln1_b", "w_ffn_gate", "w_ffn_up",
             "w_ffn_down", "ln2_g", "ln2_b")
    return (fin[2][0, 0], grad_x[None], *[grads[n] for n in names], *[deltas[n] for n in names],
            *[new_m[n] for n in names], *[new_v[n] for n in names])
```

```python
import functools

import jax
import jax.numpy as jnp
from jax import lax
from jax.experimental import pallas as pl
from jax.experimental.pallas import tpu as pltpu

F32, BF16 = jnp.float32, jnp.bfloat16
HI = lax.Precision.HIGHEST
MESH = pl.DeviceIdType.MESH

D = 1024
CHUNK = 64
LANES = 128
N_HEADS_A = 8
N_GROUPS_B = 4
B_INNER = 2048
CONV_DIM = 3072
D_FF = 2816
ALPHA = 2.0 ** 0.25
LN_EPS = 1e-5
RMS_EPS = 1e-6
ADAM_LR, ADAM_B1, ADAM_B2, ADAM_EPS, ADAM_WD, ADAM_STEP = 0.001, 0.9, 0.999, 1e-08, 0.01, 10

IN_ORIG = 11296
IN_PAD = 11520
COL_GA, COL_GB, COL_XBC, COL_Z, COL_DT = 4096, 5120, 6144, 9216, 11264
ORIG_Z, ORIG_XBC, ORIG_DT, ORIG_GA = 4096, 6144, 9216, 9248

SHARDED = ("w_in", "w_branch_a", "w_branch_b", "w_o", "w_ffn_in", "w_ffn_down")
FFN_SHARD = D_FF // 4
VMEM_LIMIT = 56 * 1024 * 1024
BLOCK_BYTES = 2 * 1024 * 1024
_DIMS = {"nn": (((1,), (0,)), ((), ())), "nt": (((1,), (1,)), ((), ())), "tn": (((0,), (0,)), ((), ()))}


def _bd(a, b, mode):
    return lax.dot_general(a.astype(BF16), b.astype(BF16), _DIMS[mode], preferred_element_type=F32)


@functools.partial(jax.custom_vjp, nondiff_argnums=(2,))
def bdot(a, b, mode):
    return _bd(a, b, mode)


def _bdot_fwd(a, b, mode):
    return _bd(a, b, mode), (a, b)


def _bdot_bwd(mode, res, g):
    a, b = res
    if mode == "nn":
        return _bd(g, b, "nt"), _bd(a, g, "tn")
    if mode == "nt":
        return _bd(g, b, "nn"), _bd(g, a, "tn")
    return _bd(b, g, "nt"), _bd(a, g, "nn")


bdot.defvjp(_bdot_fwd, _bdot_bwd)


def hdot(a, b, mode="nn"):
    return lax.dot_general(a, b, _DIMS[mode], precision=HI, preferred_element_type=F32)


def _raw(a, b, mode):
    return lax.dot_general(a, b, _DIMS[mode], preferred_element_type=F32)


def _split(x, n):
    parts, rest = [], x
    for _ in range(n):
        p = rest.astype(BF16)
        parts.append(p)
        rest = rest - p.astype(F32)
    return parts


def _od(a, b, mode, exact):
    if exact == 1:
        e = b.astype(BF16)
        p = _split(a, 3)
        return (_raw(p[2], e, mode) + _raw(p[1], e, mode)) + _raw(p[0], e, mode)
    e = a.astype(BF16)
    p = _split(b, 3)
    return (_raw(e, p[2], mode) + _raw(e, p[1], mode)) + _raw(e, p[0], mode)


@functools.partial(jax.custom_vjp, nondiff_argnums=(2, 3))
def odot(a, b, mode, exact):
    return _od(a, b, mode, exact)


def _odot_fwd(a, b, mode, exact):
    return _od(a, b, mode, exact), (a, b)


def _odot_bwd(mode, exact, res, g):
    a, b = res
    if exact == 1:
        da = {"nn": lambda: _od(g, b, "nt", 1), "nt": lambda: _od(g, b, "nn", 1), "tn": lambda: _od(b, g, "nt", 0)}[mode]()
        return da, jnp.zeros_like(b)
    db = {"nn": lambda: _od(a, g, "tn", 0), "nt": lambda: _od(g, a, "tn", 1), "tn": lambda: _od(a, g, "nn", 0)}[mode]()
    return jnp.zeros_like(a), db


odot.defvjp(_odot_fwd, _odot_bwd)


_BDIMS = {"bnn": (((2,), (1,)), ((0,), (0,))), "bnt": (((2,), (2,)), ((0,), (0,))), "btn": (((1,), (1,)), ((0,), (0,)))}


def _braw(a, b, mode):
    return lax.dot_general(a, b, _BDIMS[mode], preferred_element_type=F32)


def _bdb(a, b, mode):
    return _braw(a.astype(BF16), b.astype(BF16), mode)


def _d3b(a, b, mode):
    ah, al = _split(a, 2)
    bh, bl = _split(b, 2)
    return _braw(ah, bh, mode) + (_braw(ah, bl, mode) + _braw(al, bh, mode))


def _batched_bwd(f):
    def bwd(mode, res, g):
        a, b = res
        if mode == "bnn":
            return f(g, b, "bnt"), f(a, g, "btn")
        if mode == "bnt":
            return f(g, b, "bnn"), f(g, a, "btn")
        return f(b, g, "bnt"), f(a, g, "bnn")
    return bwd


@functools.partial(jax.custom_vjp, nondiff_argnums=(2,))
def bdot_b(a, b, mode):
    return _bdb(a, b, mode)


bdot_b.defvjp(lambda a, b, mode: (_bdb(a, b, mode), (a, b)), _batched_bwd(_bdb))


@functools.partial(jax.custom_vjp, nondiff_argnums=(2,))
def dot3_b(a, b, mode):
    return _d3b(a, b, mode)


dot3_b.defvjp(lambda a, b, mode: (_d3b(a, b, mode), (a, b)), _batched_bwd(_d3b))


def _cum(tril3, x, mode):
    e = tril3.astype(BF16)
    p = _split(x, 3)
    return (_braw(e, p[2], mode) + _braw(e, p[1], mode)) + _braw(e, p[0], mode)


@jax.custom_vjp
def chunk_cumsum(tril3, x):
    return _cum(tril3, x, "bnn")


chunk_cumsum.defvjp(lambda t, x: (_cum(t, x, "bnn"), t), lambda t, g: (jnp.zeros_like(t), _cum(t, g, "btn")))


def _unstack(axis, n):
    @jax.custom_vjp
    def un(x):
        return tuple(lax.index_in_dim(x, i, axis, keepdims=False) for i in range(n))

    un.defvjp(lambda x: (un(x), None), lambda _, g: (jnp.stack(g, axis=axis),))
    return un


def _split_last(n, w):
    @jax.custom_vjp
    def sp(x):
        return tuple(x[..., i * w:(i + 1) * w] for i in range(n))

    sp.defvjp(lambda x: (sp(x), None), lambda _, g: (jnp.concatenate(g, axis=-1),))
    return sp


def sigmoid(x):
    return 1.0 / (1.0 + jnp.exp(-x))


def silu(x):
    return x * sigmoid(x)


def softplus(x):
    return jnp.maximum(x, 0.0) + jnp.log1p(jnp.exp(jnp.minimum(x, -x)))


def _ln(x):
    mu = jnp.mean(x, axis=-1, keepdims=True)
    xc = x - mu
    return xc * lax.rsqrt(jnp.mean(xc * xc, axis=-1, keepdims=True) + LN_EPS)


def _tril64():
    r = lax.broadcasted_iota(jnp.int32, (CHUNK, CHUNK), 0)
    c = lax.broadcasted_iota(jnp.int32, (CHUNK, CHUNK), 1)
    return (r >= c).astype(F32)


def hgrn_block(q, fl, iv, gr, st, lb, gn):
    tb = q.shape[0]
    nc = tb // CHUNK
    nh = N_HEADS_A
    heads = _split_last(nh, LANES)
    to4 = lambda a: jnp.stack(heads(a), axis=0).reshape(nh, nc, CHUNK, LANES)
    flat = lambda a: a.reshape(nh * nc, CHUNK, LANES)
    f = lb + (1.0 - lb) * sigmoid(fl)
    gl4, k4, qf4, v4, gr4 = to4(jnp.log(f)), to4(1.0 - f), to4(silu(q) * (128 ** -0.5)), to4(iv), to4(gr)
    tril = _tril64()
    b4 = chunk_cumsum(jnp.broadcast_to(tril[None], (nh * nc, CHUNK, CHUNK)), flat(gl4)).reshape(gl4.shape)
    blast = jnp.sum(gl4, axis=2, keepdims=True)
    ref = lax.stop_gradient(0.5 * blast)
    sc = dot3_b(flat(qf4 * jnp.exp(b4 - ref)), flat(k4 * jnp.exp(ref - b4)), "bnt") * tril
    o_intra = bdot_b(sc, flat(v4), "bnn").reshape(gl4.shape)
    chunks = _unstack(1, nc)
    qe, v_c, kd, dec = chunks(qf4 * jnp.exp(b4)), chunks(v4), chunks(k4 * jnp.exp(blast - b4)), chunks(jnp.exp(blast))
    o_inter = []
    for c in range(nc):
        o_inter.append(bdot_b(qe[c], st, "bnt"))
        st = st * dec[c] + bdot_b(v_c[c], kd[c], "btn")
    o = o_intra + jnp.stack(o_inter, axis=1)
    on = o * lax.rsqrt(jnp.mean(o * o, axis=-1, keepdims=True) + RMS_EPS) * gn
    out = (on * silu(gr4)).reshape(nh, tb, LANES)
    return jnp.concatenate(_unstack(0, nh)(out), axis=1), st


def ssd_consts(g):
    i32 = jnp.int32
    ej = lax.broadcasted_iota(i32, (LANES, 512), 0)
    ec = lax.broadcasted_iota(i32, (LANES, 512), 1)
    expand = (ej == g * 8 + (ec >> 6)).astype(F32)
    ts = lax.broadcasted_iota(i32, (CHUNK, 512), 0)
    tc = lax.broadcasted_iota(i32, (CHUNK, 512), 1)
    itile = (ts == (tc & 63)).astype(F32)
    maskall = ts >= (tc & 63)
    br = lax.broadcasted_iota(i32, (256, 256), 0)
    bc = lax.broadcasted_iota(i32, (256, 256), 1)
    blockmask = ((br >> 6) == (bc >> 6)).astype(F32)
    return expand, itile, maskall, blockmask, _tril64()


def ssd_block(x, bm, cm, dt, z, st, dtb, alog, dsk, nw, cs):
    expand, itile, maskall, blockmask, tril = cs
    tb = x.shape[0]
    nc = tb // CHUNK
    delta_heads = softplus(dt + dtb)
    delta = odot(delta_heads, expand, "nn", 1)
    a = odot(-jnp.exp(alog) * delta_heads, expand, "nn", 1)
    xdt = x * delta
    by_chunk = lambda v: v.reshape(nc, CHUNK, v.shape[-1])
    a3, xdt3, bm3, cm3 = by_chunk(a), by_chunk(xdt), by_chunk(bm), by_chunk(cm)
    acum3 = chunk_cumsum(jnp.broadcast_to(tril[None], (nc, CHUNK, CHUNK)), a3)
    alast3 = jnp.sum(a3, axis=1, keepdims=True)
    cb3 = bdot_b(cm3, jnp.concatenate([bm3] * 8, axis=1), "bnt")
    arow3 = jnp.sum(acum3 * itile, axis=1, keepdims=True)
    dec3 = jnp.exp(jnp.where(maskall, acum3 - arow3, -1e30))
    halves = _split_last(2, 256)
    intra = [bdot_b(m, jnp.concatenate([xh] * 4, axis=1) * blockmask, "bnn")
             for m, xh in zip(halves(cb3 * dec3), halves(xdt3))]
    chunks = _unstack(0, nc)
    cm_c, bm_c, xw_c, dec_c = chunks(cm3), chunks(bm3), chunks(xdt3 * jnp.exp(alast3 - acum3)), chunks(jnp.exp(alast3))
    inter = []
    for c in range(nc):
        inter.append(bdot(cm_c[c], st, "nn"))
        st = st * dec_c[c] + bdot(bm_c[c], xw_c[c], "tn")
    st_new = st
    y = (jnp.concatenate(intra, axis=-1) + jnp.stack(inter, axis=0) * jnp.exp(acum3)).reshape(tb, 512)
    yz = (y + x * dsk) * silu(z)
    return yz * lax.rsqrt(jnp.mean(yz * yz, axis=-1, keepdims=True) + RMS_EPS) * nw, st_new


def adamw(w, g, m, v):
    m = ADAM_B1 * m + (1.0 - ADAM_B1) * g
    v = ADAM_B2 * v + (1.0 - ADAM_B2) * jnp.square(g)
    m_hat = m / (1.0 - ADAM_B1 ** ADAM_STEP)
    v_hat = v / (1.0 - ADAM_B2 ** ADAM_STEP)
    return -ADAM_LR * (m_hat / (jnp.sqrt(v_hat) + ADAM_EPS) + ADAM_WD * w), m, v


def _pick(n, cands):
    for c in cands:
        if n % c == 0:
            return c
    return n


def _params(sem):
    return pltpu.CompilerParams(dimension_semantics=sem, vmem_limit_bytes=VMEM_LIMIT)


MATMUL_VMEM_BUDGET = 50 * 1024 * 1024
MATMUL_MIN_STEPS = 4


def matmul(a, b, mode, out_dtype, name, after=None):
    if mode == "nn":
        (m, k), n = a.shape, b.shape[1]
    elif mode == "nt":
        (m, k), n = a.shape, b.shape[0]
    else:
        (k, m), n = a.shape, b.shape[1]
    tk = _pick(k, (2304, 2048, 1408, 1024, 768, 512, 256, 128))
    nk = k // tk
    a_bytes, b_bytes, out_bytes = a.dtype.itemsize, b.dtype.itemsize, jnp.dtype(out_dtype).itemsize

    def vmem(tm_, tn_):
        blocks = 2 * (tm_ * tk * a_bytes + tk * tn_ * b_bytes + tm_ * tn_ * out_bytes)
        return blocks + (tm_ * tn_ * 4 if nk > 1 else 0)

    def traffic(tm_, tn_):
        return (m // tm_) * k * n * b_bytes + (n // tn_ if nk > 1 else 1) * m * k * a_bytes

    sizes = (2304, 2048, 1920, 1408, 1024, 768, 512, 256, 128)
    tiles = [(tm_, tn_) for tm_ in sizes if m % tm_ == 0 for tn_ in sizes if n % tn_ == 0
             if vmem(tm_, tn_) <= MATMUL_VMEM_BUDGET] or [(m, n)]
    pipelined = [t for t in tiles if (m // t[0]) * (n // t[1]) * nk >= MATMUL_MIN_STEPS]
    tm, tn = min(pipelined or tiles, key=lambda t: (traffic(*t), -t[0] * t[1]))
    a_spec = pl.BlockSpec((tk, tm), lambda i, j, kk: (kk, i)) if mode == "tn" else pl.BlockSpec((tm, tk), lambda i, j, kk: (i, kk))
    b_spec = pl.BlockSpec((tn, tk), lambda i, j, kk: (j, kk)) if mode == "nt" else pl.BlockSpec((tk, tn), lambda i, j, kk: (kk, j))

    order = [] if after is None else [after]

    def body(a_ref, b_ref, *rest):
        o_ref, *acc = rest[len(order):]
        part = _bd(a_ref[...], b_ref[...], mode)
        if nk == 1:
            o_ref[...] = part.astype(o_ref.dtype)
            return
        acc_ref, = acc
        kk = pl.program_id(2)

        @pl.when(kk == 0)
        def _():
            acc_ref[...] = part

        @pl.when(jnp.logical_and(kk > 0, kk < nk - 1))
        def _():
            acc_ref[...] += part

        @pl.when(kk == nk - 1)
        def _():
            o_ref[...] = (acc_ref[...] + part).astype(o_ref.dtype)

    return pl.pallas_call(
        body, name=name, grid=(m // tm, n // tn, nk),
        in_specs=[a_spec, b_spec] + [pl.BlockSpec(memory_space=pl.ANY) for _ in order],
        out_specs=pl.BlockSpec((tm, tn), lambda i, j, kk: (i, j)),
        out_shape=jax.ShapeDtypeStruct((m, n), out_dtype),
        scratch_shapes=[pltpu.VMEM((tm, tn), F32)] if nk > 1 else [],
        compiler_params=_params(("parallel", "parallel", "arbitrary")),
    )(a, b, *order)


def rowwise(name, fn, rows, consts, out_rows, out_accs=(), tm_max=256, into=None, new_wide=None):
    t = rows[0][0].shape[0]
    tm = _pick(t, (tm_max, 128, 64, 32, 16, 8))
    n_r, n_c, n_o = len(rows), len(consts), len(out_rows)
    n_alias = 0 if into is None else 1

    def body(*refs):
        r_in = [r[...] for r in refs[:n_r]]
        c_in = [r[...] for r in refs[n_r:n_r + n_c]]
        refs = refs[:n_r + n_c] + refs[n_r + n_c + n_alias:]
        o_refs = refs[n_r + n_c:n_r + n_c + n_o]
        a_refs = refs[n_r + n_c + n_o:]
        ro, ao = fn(r_in, c_in)
        for ref, val in zip(o_refs, ro, strict=True):
            ref[...] = val.astype(ref.dtype)
        if a_refs:
            @pl.when(pl.program_id(0) == 0)
            def _():
                for ref in a_refs:
                    ref[...] = jnp.zeros_like(ref)

            for ref, val in zip(a_refs, ao, strict=True):
                ref[...] += val

    in_specs = [pl.BlockSpec((tm, w), functools.partial(lambda i, cb: (i, cb), cb=cb)) for _, w, cb in rows]
    in_specs += [pl.BlockSpec(c.shape, lambda i: (0, 0)) for c in consts]
    out_specs = [pl.BlockSpec((tm, w), lambda i: (i, 0)) for w, _ in out_rows]
    out_specs += [pl.BlockSpec(s, lambda i: (0, 0)) for s in out_accs]
    out_shape = [jax.ShapeDtypeStruct((t, w), dt) for w, dt in out_rows]
    out_shape += [jax.ShapeDtypeStruct(s, F32) for s in out_accs]
    operands = [r[0] for r in rows] + list(consts)
    aliases = {}
    if into is not None:
        target, cb = into
        in_specs.append(pl.BlockSpec(memory_space=pl.ANY))
        operands.append(target)
        out_specs[0] = pl.BlockSpec((tm, out_rows[0][0]), lambda i: (i, cb))
        out_shape[0] = jax.ShapeDtypeStruct(target.shape, target.dtype)
        aliases = {len(operands) - 1: 0}
    if new_wide is not None:
        width, cb = new_wide
        out_specs[0] = pl.BlockSpec((tm, out_rows[0][0]), lambda i: (i, cb))
        out_shape[0] = jax.ShapeDtypeStruct((t, width), out_rows[0][1])
    return pl.pallas_call(
        body, name=name, grid=(t // tm,), in_specs=in_specs, out_specs=out_specs, out_shape=out_shape,
        input_output_aliases=aliases, compiler_params=_params(("arbitrary",)),
    )(*operands)


def _full(a):
    return (a, a.shape[1], 0)


HGRN_TIME_BLOCK = 256
SSD_TIME_BLOCK = 512


def _time_block(t, most=HGRN_TIME_BLOCK):
    return _pick(t, tuple(b for b in (512, 256, 128, 64) if b <= most))


def _quarters(ref):
    return [ref[:, seg * D:(seg + 1) * D] for seg in range(4)]


def hgrn_forward(proj, lb, gn):
    t = proj.shape[0]
    tb = _time_block(t)
    nb = t // tb

    def body(qfig_ref, lb_ref, gn_ref, o_ref, st_ref, state):
        @pl.when(pl.program_id(0) == 0)
        def _():
            state[...] = jnp.zeros_like(state)

        st = state[...]
        st_ref[...] = st
        out, st_new = hgrn_block(*_quarters(qfig_ref), st, lb_ref[...], gn_ref[...])
        o_ref[...] = out.astype(o_ref.dtype)
        state[...] = st_new

    return pl.pallas_call(
        body, name="hgrn_forward", grid=(nb,),
        in_specs=[pl.BlockSpec((tb, 4 * D), lambda j: (j, 0)),
                  pl.BlockSpec((1, D), lambda j: (0, 0)), pl.BlockSpec((1, LANES), lambda j: (0, 0))],
        out_specs=[pl.BlockSpec((tb, D), lambda j: (j, 0)),
                   pl.BlockSpec((None, N_HEADS_A, LANES, LANES), lambda j: (j, 0, 0, 0))],
        out_shape=[jax.ShapeDtypeStruct((t, D), BF16),
                   jax.ShapeDtypeStruct((nb, N_HEADS_A, LANES, LANES), F32)],
        scratch_shapes=[pltpu.VMEM((N_HEADS_A, LANES, LANES), F32)],
        compiler_params=_params(("arbitrary",)),
    )(proj, lb, gn)


def hgrn_backward(proj, states, d_out, lb, gn, d_proj):
    t = proj.shape[0]
    tb = _time_block(t)
    nb = t // tb

    def body(qfig_ref, st_ref, do_ref, lb_ref, gn_ref, _, dqfig_ref, dlb_ref, dgn_ref, d_state):
        @pl.when(pl.program_id(0) == 0)
        def _():
            d_state[...] = jnp.zeros_like(d_state)
            dlb_ref[...] = jnp.zeros_like(dlb_ref)
            dgn_ref[...] = jnp.zeros_like(dgn_ref)

        _, vjp = jax.vjp(hgrn_block, *_quarters(qfig_ref), st_ref[...], lb_ref[...], gn_ref[...])
        dq, df, di, dg, dst, dlb, dgn = vjp((do_ref[...], d_state[...]))
        for seg, val in enumerate((dq, df, di, dg)):
            dqfig_ref[:, seg * D:(seg + 1) * D] = val.astype(dqfig_ref.dtype)
        d_state[...] = dst
        dlb_ref[...] += dlb
        dgn_ref[...] += dgn

    rev = lambda j: nb - 1 - j
    return pl.pallas_call(
        body, name="hgrn_backward", grid=(nb,),
        in_specs=[pl.BlockSpec((tb, 4 * D), lambda j: (rev(j), 0)),
                  pl.BlockSpec((None, N_HEADS_A, LANES, LANES), lambda j: (rev(j), 0, 0, 0)),
                  pl.BlockSpec((tb, D), lambda j: (rev(j), 0)),
                  pl.BlockSpec((1, D), lambda j: (0, 0)), pl.BlockSpec((1, LANES), lambda j: (0, 0)),
                  pl.BlockSpec(memory_space=pl.ANY)],
        out_specs=[pl.BlockSpec((tb, 4 * D), lambda j: (rev(j), 0)),
                   pl.BlockSpec((1, D), lambda j: (0, 0)), pl.BlockSpec((1, LANES), lambda j: (0, 0))],
        out_shape=[jax.ShapeDtypeStruct(d_proj.shape, d_proj.dtype), jax.ShapeDtypeStruct((1, D), F32),
                   jax.ShapeDtypeStruct((1, LANES), F32)],
        input_output_aliases={5: 0},
        scratch_shapes=[pltpu.VMEM((N_HEADS_A, LANES, LANES), F32)],
        compiler_params=_params(("arbitrary",)),
    )(proj, states, d_out, lb, gn, d_proj)


def _ssd_in_specs(tb, tmap):
    return [pl.BlockSpec((tb, 512), lambda g, j: (tmap(j), g)),
            pl.BlockSpec((tb, LANES), lambda g, j: (tmap(j), 16 + g)),
            pl.BlockSpec((tb, LANES), lambda g, j: (tmap(j), 20 + g)),
            pl.BlockSpec((tb, LANES), lambda g, j: (tmap(j), COL_DT // LANES)),
            pl.BlockSpec((tb, 512), lambda g, j: (tmap(j), COL_Z // 512 + g))]


def ssd_forward(xc, proj, dtb, alog, dsk, nw):
    t = proj.shape[0]
    tb = _time_block(t, SSD_TIME_BLOCK)
    nb = t // tb

    def body(x_ref, b_ref, c_ref, dt_ref, z_ref, dtb_ref, alog_ref, dsk_ref, nw_ref, o_ref, st_ref, state):
        @pl.when(pl.program_id(1) == 0)
        def _():
            state[...] = jnp.zeros_like(state)

        st = state[...]
        st_ref[...] = st
        out, st_new = ssd_block(x_ref[...], b_ref[...], c_ref[...], dt_ref[...], z_ref[...], st,
                                dtb_ref[...], alog_ref[...], dsk_ref[...], nw_ref[...], ssd_consts(pl.program_id(0)))
        o_ref[...] = out.astype(o_ref.dtype)
        state[...] = st_new

    vec = pl.BlockSpec((1, 512), lambda g, j: (0, g))
    heads = pl.BlockSpec((1, LANES), lambda g, j: (0, 0))
    return pl.pallas_call(
        body, name="ssd_forward", grid=(N_GROUPS_B, nb),
        in_specs=_ssd_in_specs(tb, lambda j: j) + [heads, heads, vec, vec],
        out_specs=[pl.BlockSpec((tb, 512), lambda g, j: (j, g)),
                   pl.BlockSpec((None, None, LANES, 512), lambda g, j: (j, g, 0, 0))],
        out_shape=[jax.ShapeDtypeStruct((t, B_INNER), BF16),
                   jax.ShapeDtypeStruct((nb, N_GROUPS_B, LANES, 512), F32)],
        scratch_shapes=[pltpu.VMEM((LANES, 512), F32)],
        compiler_params=_params(("arbitrary", "arbitrary")),
    )(xc, xc, xc, proj, proj, dtb, alog, dsk, nw)


def ssd_backward(xc, proj, states, d_out, dtb, alog, dsk, nw, d_proj):
    t = proj.shape[0]
    tb = _time_block(t, SSD_TIME_BLOCK)
    nb = t // tb
    rev = lambda j: nb - 1 - j

    def body(x_ref, b_ref, c_ref, dt_ref, z_ref, st_ref, do_ref, dtb_ref, alog_ref, dsk_ref, nw_ref, _,
             dx_ref, db_ref, dc_ref, ddt_ref, dz_ref, ddtb_ref, dalog_ref, ddsk_ref, dnw_ref, d_state):
        accs = (ddtb_ref, dalog_ref, ddsk_ref, dnw_ref)

        @pl.when(pl.program_id(1) == 0)
        def _():
            d_state[...] = jnp.zeros_like(d_state)
            for ref in accs:
                ref[...] = jnp.zeros_like(ref)

        cs = ssd_consts(pl.program_id(0))
        fn = lambda *a: ssd_block(*a, cs)
        _, vjp = jax.vjp(fn, x_ref[...], b_ref[...], c_ref[...], dt_ref[...], z_ref[...], st_ref[...],
                         dtb_ref[...], alog_ref[...], dsk_ref[...], nw_ref[...])
        dx, db, dc, ddt, dz, dst, *dpar = vjp((do_ref[...], d_state[...]))
        dx_ref[...] = dx
        db_ref[...] = db
        dc_ref[...] = dc
        ddt_ref[...] = ddt
        dz_ref[...] = dz.astype(dz_ref.dtype)
        d_state[...] = dst
        for ref, val in zip(accs, dpar, strict=True):
            ref[...] += val

    vec = pl.BlockSpec((1, 512), lambda g, j: (0, g))
    heads = pl.BlockSpec((1, LANES), lambda g, j: (0, 0))
    acc = pl.BlockSpec((None, 1, 512), lambda g, j: (g, 0, 0))
    acc_heads = pl.BlockSpec((None, 1, LANES), lambda g, j: (g, 0, 0))
    return pl.pallas_call(
        body, name="ssd_backward", grid=(N_GROUPS_B, nb),
        in_specs=_ssd_in_specs(tb, rev)
        + [pl.BlockSpec((None, None, LANES, 512), lambda g, j: (rev(j), g, 0, 0)),
           pl.BlockSpec((tb, 512), lambda g, j: (rev(j), g))] + [heads, heads, vec, vec] + [pl.BlockSpec(memory_space=pl.ANY)],
        out_specs=[pl.BlockSpec((tb, 512), lambda g, j: (rev(j), g)),
                   pl.BlockSpec((tb, LANES), lambda g, j: (rev(j), g)),
                   pl.BlockSpec((tb, LANES), lambda g, j: (rev(j), g)),
                   pl.BlockSpec((None, tb, LANES), lambda g, j: (g, rev(j), 0)),
                   pl.BlockSpec((tb, 512), lambda g, j: (rev(j), COL_Z // 512 + g)), acc_heads, acc_heads, acc, acc],
        out_shape=[jax.ShapeDtypeStruct((t, B_INNER), F32), jax.ShapeDtypeStruct((t, 512), F32),
                   jax.ShapeDtypeStruct((t, 512), F32), jax.ShapeDtypeStruct((N_GROUPS_B, t, LANES), F32),
                   jax.ShapeDtypeStruct(d_proj.shape, d_proj.dtype)]
        + [jax.ShapeDtypeStruct((N_GROUPS_B, 1, LANES), F32)] * 2 + [jax.ShapeDtypeStruct((N_GROUPS_B, 1, 512), F32)] * 2,
        input_output_aliases={11: 4},
        scratch_shapes=[pltpu.VMEM((LANES, 512), F32)],
        compiler_params=_params(("arbitrary", "arbitrary")),
    )(xc, xc, xc, proj, proj, states, d_out, dtb, alog, dsk, nw, d_proj)


CONV_HALO = 8


def _shift_down(halo_then_tile, s, tm):
    if s == 0:
        return halo_then_tile[CONV_HALO:CONV_HALO + tm]
    return pltpu.roll(halo_then_tile, s, 0)[CONV_HALO:CONV_HALO + tm]


def _conv_pre(cur, prev, w, b, tm):
    stacked = jnp.concatenate([prev, cur], axis=0)
    taps = [_shift_down(stacked, 3 - j, tm) for j in range(4)]
    pre = b + taps[0] * w[0:1] + taps[1] * w[1:2] + taps[2] * w[2:3] + taps[3] * w[3:4]
    return pre, taps


def _conv_specs(t, tm):
    per = tm // CONV_HALO
    cur = pl.BlockSpec((tm, CONV_DIM), lambda i: (i, COL_XBC // CONV_DIM))
    prev = pl.BlockSpec((CONV_HALO, CONV_DIM), lambda i: (jnp.maximum(i * per - 1, 0), COL_XBC // CONV_DIM))
    return cur, prev


def conv_forward(proj, w, b):
    t = proj.shape[0]
    tm = _pick(t, (256, 128, 64))

    def body(cur_ref, prev_ref, w_ref, b_ref, o_ref):
        prev = jnp.where(pl.program_id(0) == 0, 0.0, prev_ref[...])
        pre, _ = _conv_pre(cur_ref[...], prev, w_ref[...], b_ref[...], tm)
        o_ref[...] = silu(pre)

    cur, prev = _conv_specs(t, tm)
    return pl.pallas_call(
        body, name="conv_forward", grid=(t // tm,),
        in_specs=[cur, prev, pl.BlockSpec((4, CONV_DIM), lambda i: (0, 0)), pl.BlockSpec((1, CONV_DIM), lambda i: (0, 0))],
        out_specs=pl.BlockSpec((tm, CONV_DIM), lambda i: (i, 0)),
        out_shape=jax.ShapeDtypeStruct((t, CONV_DIM), F32),
        compiler_params=_params(("arbitrary",)),
    )(proj, proj, w, b)


def conv_backward(proj, dx, db_, dc_, w, b, d_proj):
    t = proj.shape[0]
    tm = _pick(t, (256, 128, 64))
    per = tm // CONV_HALO
    nt = t // tm
    rev = lambda i: nt - 1 - i

    def body(cur_ref, prev_ref, dx_ref, dbm_ref, dcm_ref, w_ref, b_ref, _, o_ref, dw_ref, dbias_ref, later):
        @pl.when(pl.program_id(0) == 0)
        def _():
            dw_ref[...] = jnp.zeros_like(dw_ref)
            dbias_ref[...] = jnp.zeros_like(dbias_ref)
            later[...] = jnp.zeros_like(later)

        first_tile = pl.program_id(0) == nt - 1
        for lo, hi, src in ((0, B_INNER, dx_ref), (B_INNER, B_INNER + 512, dbm_ref), (B_INNER + 512, CONV_DIM, dcm_ref)):
            cols = slice(lo, hi)
            prev = jnp.where(first_tile, 0.0, prev_ref[:, cols])
            w_ = w_ref[:, cols]
            pre, taps = _conv_pre(cur_ref[:, cols], prev, w_, b_ref[:, cols], tm)
            sg = sigmoid(pre)
            dpre = src[...] * (sg * (1.0 + pre * (1.0 - sg)))
            dbias_ref[:, cols] += jnp.sum(dpre, axis=0, keepdims=True)
            for j in range(4):
                dw_ref[j:j + 1, cols] += jnp.sum(dpre * taps[j], axis=0, keepdims=True)
            stacked = jnp.concatenate([dpre, later[:, cols]], axis=0)
            acc = dpre * w_[3:4]
            for j in range(3):
                acc = acc + pltpu.roll(stacked, tm + CONV_HALO - (3 - j), 0)[0:tm] * w_[j:j + 1]
            o_ref[:, cols] = acc.astype(o_ref.dtype)
            later[:, cols] = dpre[0:CONV_HALO]

    row = lambda w_: pl.BlockSpec((tm, w_), lambda i: (rev(i), 0))
    whole = lambda r: pl.BlockSpec((r, CONV_DIM), lambda i: (0, 0))
    return pl.pallas_call(
        body, name="conv_backward", grid=(nt,),
        in_specs=[pl.BlockSpec((tm, CONV_DIM), lambda i: (rev(i), COL_XBC // CONV_DIM)),
                  pl.BlockSpec((CONV_HALO, CONV_DIM), lambda i: (jnp.maximum(rev(i) * per - 1, 0), COL_XBC // CONV_DIM)),
                  row(B_INNER), row(512), row(512), whole(4), whole(1), pl.BlockSpec(memory_space=pl.ANY)],
        out_specs=[pl.BlockSpec((tm, CONV_DIM), lambda i: (rev(i), COL_XBC // CONV_DIM)), whole(4), whole(1)],
        out_shape=[jax.ShapeDtypeStruct(d_proj.shape, d_proj.dtype), jax.ShapeDtypeStruct((4, CONV_DIM), F32),
                   jax.ShapeDtypeStruct((1, CONV_DIM), F32)],
        input_output_aliases={7: 0},
        scratch_shapes=[pltpu.VMEM((CONV_HALO, CONV_DIM), F32)],
        compiler_params=_params(("arbitrary",)),
    )(proj, proj, dx, db_, dc_, w, b, d_proj)


def stage_modulate(x, sc, sh):
    return _ln(x) * (1.0 + sc) + sh


def stage_merge(ga, gb, ya, yb):
    return sigmoid(ga) * ya + sigmoid(gb) * yb


def stage_post_mixer(x, h, g1, ln_g, ln_b, sc2, sh2):
    x1 = _ln(ALPHA * x + g1 * h) * ln_g + ln_b
    return x1, _ln(x1) * (1.0 + sc2) + sh2


def stage_swiglu(a, b):
    return silu(a) * b


def gate_up(ab):
    w = FFN_SHARD
    return (jnp.concatenate([ab[:, 2 * w * k:2 * w * k + w] for k in range(4)], axis=1),
            jnp.concatenate([ab[:, 2 * w * k + w:2 * w * (k + 1)] for k in range(4)], axis=1))


def per_chip(gate, up):
    w = FFN_SHARD
    return jnp.concatenate([part[:, w * k:w * (k + 1)] for k in range(4) for part in (gate, up)], axis=1)


def stage_loss(x1, hf, tgt, g2, ln_g, ln_b):
    x2 = _ln(ALPHA * x1 + g2 * hf) * ln_g + ln_b
    return 0.5 * jnp.sum(jnp.mean(jnp.square(x2 - tgt), axis=-1, keepdims=True), axis=0, keepdims=True)


def local_step(x, tgt, mod, wts, small, early=None, mid=None, late=None, last=None):
    sh1, sc1, g1, sh2, sc2, g2 = mod
    lb, gn, conv_w, conv_b, dtb, alog, dsk, nw, ln1_g, ln1_b, ln2_g, ln2_b = small
    vec = (1, D)

    (u1,) = rowwise("modulate1", lambda r, c: ((stage_modulate(r[0], *c),), ()), [_full(x)], [sc1, sh1], [(D, BF16)])
    w_in = wts.input_projection(u1)
    proj = matmul(u1, w_in, "nn", F32, "in_proj")
    ya_in, st_a = hgrn_forward(proj, lb, gn + wts.start_rest(proj)[0:1])
    xc = conv_forward(proj, conv_w, conv_b)
    w_a, w_b, w_o, w_gu, w_d = wts.rest(xc)
    yb_in, st_b = ssd_forward(xc, proj, dtb, alog, dsk, nw)
    ya = matmul(ya_in, w_a, "nn", F32, "branch_a")
    yb = matmul(yb_in, w_b, "nn", F32, "branch_b")
    gate_rows = [(proj, D, COL_GA // D), (proj, D, COL_GB // D), _full(ya), _full(yb)]
    (merged,) = rowwise("merge", lambda r, c: ((stage_merge(*r),), ()), gate_rows, [], [(D, BF16)])
    h = matmul(merged, w_o, "nn", F32, "out_proj")
    post_consts = [g1, ln1_g, ln1_b, sc2, sh2]
    x1, u2 = rowwise("post_mixer", lambda r, c: (stage_post_mixer(*r, *c), ()), [_full(x), _full(h)], post_consts,
                     [(D, F32), (D, BF16)])
    ab = matmul(u2, w_gu, "nt", F32, "ffn_in")
    (p,) = rowwise("swiglu", lambda r, c: ((stage_swiglu(*gate_up(r[0])),), ()), [_full(ab)], [], [(D_FF, BF16)])
    hf = matmul(p, w_d, "nn", F32, "ffn_out")

    def loss_bwd(r, c):
        loss, vjp = jax.vjp(stage_loss, *r, *c)
        dx1, dhf, _, dg2, dlg, dlb_ = vjp(jnp.ones((1, 1), F32))
        return (dx1, dhf), (loss, dg2, dlg, dlb_)

    dx1, dhf, loss, dg2, dln2_g, dln2_b = rowwise(
        "loss_backward", loss_bwd, [_full(x1), _full(hf), _full(tgt)], [g2, ln2_g, ln2_b],
        [(D, F32), (D, BF16)], [(1, 1), vec, vec, vec])
    dp = matmul(dhf, w_d, "nt", F32, "ffn_out_dx")
    dw_d = matmul(p, dhf, "tn", F32, "ffn_out_dw")

    def swiglu_bwd(r, c):
        _, vjp = jax.vjp(stage_swiglu, *gate_up(r[0]))
        return (per_chip(*vjp(r[1])),), ()

    (dab,) = rowwise("swiglu_backward", swiglu_bwd, [_full(ab), _full(dp)], [], [(2 * D_FF, BF16)])
    du2 = matmul(dab, w_gu, "nn", F32, "ffn_in_dx")
    dw_gu = matmul(dab, u2, "tn", F32, "ffn_in_dw")

    def post_bwd(r, c):
        _, vjp = jax.vjp(stage_post_mixer, r[0], r[1], *c)
        dx, dh, *dc = vjp((r[2], r[3]))
        return (dx, dh), tuple(dc)

    dx_a, dh, dg1, dln1_g, dln1_b, dsc2, dsh2 = rowwise(
        "post_mixer_backward", post_bwd, [_full(x), _full(h), _full(dx1), _full(du2)], post_consts,
        [(D, F32), (D, BF16)], [vec] * 5)
    dmerged = matmul(dh, w_o, "nt", F32, "out_proj_dx")
    dw_o = matmul(merged, dh, "tn", F32, "out_proj_dw")

    def merge_bwd(r, c):
        _, vjp = jax.vjp(stage_merge, *r[:4])
        dga, dgb, dya, dyb = vjp(r[4])
        return (jnp.concatenate([dga, dgb], axis=1), dya, dyb), ()

    dproj, dya, dyb = rowwise("merge_backward", merge_bwd, gate_rows + [_full(dmerged)], [],
                              [(2 * D, BF16), (D, BF16), (D, BF16)], new_wide=(IN_PAD, COL_GA // (2 * D)))
    dya_in = matmul(dya, w_a, "nt", F32, "branch_a_dx")
    dw_a = matmul(ya_in, dya, "tn", F32, "branch_a_dw")
    dyb_in = matmul(dyb, w_b, "nt", F32, "branch_b_dx")
    dw_b = matmul(yb_in, dyb, "tn", F32, "branch_b_dw")
    gn_after = gn if early is None else gn + early((dw_a, dw_b, dw_o, dw_gu, dw_d))[0:1]
    dproj, dlb, dgn = hgrn_backward(proj, st_a, dya_in, lb, gn_after, dproj)
    dtb_after = dtb if mid is None else dtb + mid(dlb)[0:1, 0:1]
    dxs, dbm, dcm, ddt, dproj, ddtb, dalog, ddsk, dnw = ssd_backward(xc, proj, st_b, dyb_in, dtb_after, alog, dsk, nw, dproj)
    dproj, dconv_w, dconv_b = conv_backward(proj, dxs, dbm, dcm, conv_w, conv_b, dproj)
    if late is not None:
        late(dconv_b)
    t = x.shape[0]
    tail = jnp.concatenate([jnp.sum(ddt, axis=0).astype(BF16), jnp.zeros((t, IN_PAD - COL_DT - LANES), BF16)], axis=1)
    dproj = lax.dynamic_update_slice(dproj, tail, (0, COL_DT))
    dw_in = matmul(u1, dproj, "tn", F32, "in_proj_dw")
    du1 = matmul(dproj, w_in, "nt", F32, "in_proj_dx", after=None if last is None else last(dw_in))

    def mod_bwd(r, c):
        _, vjp = jax.vjp(stage_modulate, r[0], *c)
        dx, dsc, dsh = vjp(r[1])
        return (dx + r[2],), (dsc, dsh)

    grad_x, dsc1, dsh1 = rowwise("modulate1_backward", mod_bwd, [_full(x), _full(du1), _full(dx_a)], [sc1, sh1],
                                 [(D, F32)], [vec, vec])
    d_mod = (dsh1, dsc1, dg1, dsh2, dsc2, dg2)
    d_wts = (dw_in, dw_a, dw_b, dw_o, dw_gu, dw_d)
    d_small = (dlb, dgn, dconv_w, dconv_b, jnp.sum(ddtb, axis=0),
               jnp.sum(dalog, axis=0), ddsk.reshape(1, B_INNER), dnw.reshape(1, B_INNER),
               dln1_g, dln1_b, dln2_g, dln2_b)
    return loss, grad_x, d_mod, d_wts, d_small


HBM = pl.BlockSpec(memory_space=pltpu.HBM)
SEM = pl.BlockSpec(memory_space=pltpu.SEMAPHORE)
DATAFLOW = pltpu.SideEffectType.DATAFLOW_SIDE_EFFECTING


def _place():
    return lax.axis_index("x"), lax.axis_index("y"), lax.axis_index("c")


def _other_chips(x, y):
    return [(1 - x, y), (x, 1 - y), (1 - x, 1 - y)]


def _remote(src, dst, send_sem, recv_sem, device):
    return pltpu.make_async_remote_copy(src_ref=src, dst_ref=dst, send_sem=send_sem, recv_sem=recv_sem,
                                        device_id=device, device_id_type=MESH)


def gather_rows(v, name):
    n = v.shape[1]

    def body(v_ref, out_ref, send_sems, recv_sems, local_sem):
        x, y, c = _place()
        mine = pltpu.make_async_copy(v_ref, out_ref.at[4 * x + 2 * y + c], local_sem)
        mine.start()
        sends, recvs = [], []
        for m in range(1, 8):
            px = 1 - x if m & 4 else x
            py = 1 - y if m & 2 else y
            pc = 1 - c if m & 1 else c
            sends.append(_remote(v_ref, out_ref.at[4 * x + 2 * y + c], send_sems.at[m - 1], recv_sems.at[m - 1], (px, py, pc)))
            recvs.append(_remote(v_ref, out_ref.at[4 * px + 2 * py + pc], send_sems.at[m - 1], recv_sems.at[m - 1], (px, py, pc)))
        for cp in sends:
            cp.start()
        for cp in recvs:
            cp.wait_recv()
        for cp in sends:
            cp.wait_send()
        mine.wait()

    return pl.pallas_call(
        body, name=name, in_specs=[HBM], out_specs=HBM,
        out_shape=jax.ShapeDtypeStruct((8, 1, n), v.dtype),
        scratch_shapes=[pltpu.SemaphoreType.DMA((7,)), pltpu.SemaphoreType.DMA((7,)), pltpu.SemaphoreType.DMA],
    )(v)


def exchange_rows(part, name):
    w = part.shape[2]

    def body(p_ref, out_ref, send_sems, recv_sems, local_sem):
        x, y, c = _place()
        k = 2 * x + y
        mine = pltpu.make_async_copy(p_ref.at[4 * x + 2 * y + c], out_ref.at[k], local_sem)
        mine.start()
        sends, recvs = [], []
        for j, (px, py) in enumerate(_other_chips(x, y)):
            sends.append(_remote(p_ref.at[4 * px + 2 * py + c], out_ref.at[k], send_sems.at[j], recv_sems.at[j], (px, py, c)))
            recvs.append(_remote(p_ref.at[4 * px + 2 * py + c], out_ref.at[2 * px + py], send_sems.at[j], recv_sems.at[j], (px, py, c)))
        for cp in sends:
            cp.start()
        for cp in recvs:
            cp.wait_recv()
        for cp in sends:
            cp.wait_send()
        mine.wait()

    return pl.pallas_call(
        body, name=name, in_specs=[HBM], out_specs=HBM,
        out_shape=jax.ShapeDtypeStruct((4, 1, w), part.dtype),
        scratch_shapes=[pltpu.SemaphoreType.DMA((3,)), pltpu.SemaphoreType.DMA((3,)), pltpu.SemaphoreType.DMA],
    )(part)


def _half_of_slot(ref, rows, px, py, pc):
    return ref.at[2 * px + py, pl.ds(pc * (rows // 2), rows // 2), :]


def gather_start(shards, after, tag):
    n = len(shards)

    def body(*refs):
        w_refs, land_refs = refs[:n], refs[n:2 * n]
        send_sems, recv_sems = refs[2 * n + 1], refs[2 * n + 2]
        token = refs[-1]
        x, y, c = _place()
        for i in range(n):
            rows = shards[i].shape[0]
            for j, (px, py) in enumerate(_other_chips(x, y)):
                _remote(w_refs[i].at[pl.ds(c * (rows // 2), rows // 2), :], _half_of_slot(land_refs[i], rows, x, y, c),
                        send_sems.at[j * n + i], recv_sems.at[j * n + i], (px, py, c)).start()
        token[...] = jnp.zeros_like(token)

    hbm = lambda a: pltpu.with_memory_space_constraint(a, pltpu.HBM)
    lands = [lax.empty((4,) + s.shape, s.dtype) for s in shards]
    dma = pltpu.SemaphoreType.DMA
    return pl.pallas_call(
        body, name="gather_start_" + tag,
        out_shape=(dma((3 * n,)), dma((3 * n,)),
                   *[pltpu.HBM(a.shape, a.dtype) for a in list(shards) + lands], jax.ShapeDtypeStruct((8, LANES), F32)),
        in_specs=[HBM] * (2 * n) + [pl.BlockSpec(memory_space=pl.ANY)],
        out_specs=(SEM, SEM, *[HBM] * (2 * n), pl.BlockSpec(memory_space=pltpu.VMEM)),
        input_output_aliases={i: 2 + i for i in range(2 * n)},
        compiler_params=pltpu.CompilerParams(has_side_effects=DATAFLOW),
    )(*[hbm(a) for a in list(shards) + lands], after)


def gather_wait(send_sems, recv_sems, shards, lands, after, tag):
    n = len(shards)

    def body(*refs):
        w_refs, land_refs = refs[:n], refs[n:2 * n]
        send_ref, recv_ref = refs[2 * n], refs[2 * n + 1]
        x, y, c = _place()
        for i in range(n):
            rows = shards[i].shape[0]
            for j, (px, py) in enumerate(_other_chips(x, y)):
                cp = _remote(w_refs[i].at[pl.ds(c * (rows // 2), rows // 2), :], _half_of_slot(land_refs[i], rows, px, py, c),
                             send_ref.at[j * n + i], recv_ref.at[j * n + i], (px, py, c))
                cp.wait_send()
                cp.wait_recv()

    out = pl.pallas_call(
        body, name="gather_wait_" + tag,
        out_shape=tuple(pltpu.HBM(a.shape, a.dtype) for a in list(shards) + list(lands)),
        in_specs=[HBM] * (2 * n) + [SEM, SEM, pl.BlockSpec(memory_space=pl.ANY)], out_specs=tuple([HBM] * (2 * n)),
        input_output_aliases={i: i for i in range(2 * n)},
        compiler_params=pltpu.CompilerParams(has_side_effects=DATAFLOW),
    )(*shards, *lands, send_sems, recv_sems, after)
    return list(out[:n]), list(out[n:])


def forward_start(lands, tag):
    n = len(lands)

    def body(*refs):
        land_refs = refs[:n]
        send_sems, recv_sems = refs[n], refs[n + 1]
        token = refs[-1]
        x, y, c = _place()
        for i in range(n):
            rows = lands[i].shape[1]
            for j, (px, py) in enumerate(_other_chips(x, y)):
                mine = _half_of_slot(land_refs[i], rows, px, py, c)
                _remote(mine, mine, send_sems.at[j * n + i], recv_sems.at[j * n + i], (x, y, 1 - c)).start()
        token[...] = jnp.zeros_like(token)

    dma = pltpu.SemaphoreType.DMA
    return pl.pallas_call(
        body, name="forward_start_" + tag,
        out_shape=(dma((3 * n,)), dma((3 * n,)), *[pltpu.HBM(a.shape, a.dtype) for a in lands],
                   jax.ShapeDtypeStruct((8, LANES), F32)),
        in_specs=[HBM] * n, out_specs=(SEM, SEM, *[HBM] * n, pl.BlockSpec(memory_space=pltpu.VMEM)),
        input_output_aliases={i: 2 + i for i in range(n)},
        compiler_params=pltpu.CompilerParams(has_side_effects=DATAFLOW),
    )(*lands)


def forward_wait(started, after, tag):
    send_sems, recv_sems, *rest = started
    lands = rest[:-1]
    n = len(lands)

    def body(*refs):
        land_refs = refs[:n]
        send_ref, recv_ref = refs[n], refs[n + 1]
        x, y, c = _place()
        for i in range(n):
            rows = lands[i].shape[1]
            for j, (px, py) in enumerate(_other_chips(x, y)):
                cp = _remote(_half_of_slot(land_refs[i], rows, px, py, c), _half_of_slot(land_refs[i], rows, px, py, 1 - c),
                             send_ref.at[j * n + i], recv_ref.at[j * n + i], (x, y, 1 - c))
                cp.wait_send()
                cp.wait_recv()

    out = pl.pallas_call(
        body, name="forward_wait_" + tag,
        out_shape=tuple(pltpu.HBM(a.shape, a.dtype) for a in lands),
        in_specs=[HBM] * n + [SEM, SEM, pl.BlockSpec(memory_space=pl.ANY)], out_specs=tuple([HBM] * n),
        input_output_aliases={i: i for i in range(n)},
        compiler_params=pltpu.CompilerParams(has_side_effects=DATAFLOW),
    )(*lands, send_sems, recv_sems, after)
    return list(out)


def pair_start(slabs, tag):
    n = len(slabs)

    def body(*refs):
        g_refs, land_refs = refs[:n], refs[n:2 * n]
        send_sems, recv_sems = refs[2 * n], refs[2 * n + 1]
        token = refs[-1]
        x, y, c = _place()
        for i in range(n):
            hr = slabs[i].shape[1] // 2
            _remote(g_refs[i].at[:, pl.ds((1 - c) * hr, hr), :], land_refs[i], send_sems.at[i], recv_sems.at[i],
                    (x, y, 1 - c)).start()
        token[...] = jnp.zeros_like(token)

    hbm = lambda a: pltpu.with_memory_space_constraint(a, pltpu.HBM)
    lands = [lax.empty((4, s.shape[1] // 2, s.shape[2]), s.dtype) for s in slabs]
    dma = pltpu.SemaphoreType.DMA
    return pl.pallas_call(
        body, name="pair_start_" + tag,
        out_shape=(dma((n,)), dma((n,)), *[pltpu.HBM(a.shape, a.dtype) for a in list(slabs) + lands],
                   jax.ShapeDtypeStruct((8, LANES), F32)),
        in_specs=[HBM] * (2 * n), out_specs=(SEM, SEM, *[HBM] * (2 * n), pl.BlockSpec(memory_space=pltpu.VMEM)),
        input_output_aliases={i: 2 + i for i in range(2 * n)},
        compiler_params=pltpu.CompilerParams(has_side_effects=DATAFLOW),
    )(*[hbm(a) for a in list(slabs) + lands])


def pair_wait(started, after, tag):
    send_sems, recv_sems, *rest = started
    n = (len(rest) - 1) // 2
    slabs, lands = rest[:n], rest[n:2 * n]

    def body(*refs):
        g_refs, land_refs = refs[:n], refs[n:2 * n]
        send_ref, recv_ref = refs[2 * n], refs[2 * n + 1]
        x, y, c = _place()
        for i in range(n):
            hr = slabs[i].shape[1] // 2
            cp = _remote(g_refs[i].at[:, pl.ds((1 - c) * hr, hr), :], land_refs[i], send_ref.at[i], recv_ref.at[i], (x, y, 1 - c))
            cp.wait_send()
            cp.wait_recv()

    out = pl.pallas_call(
        body, name="pair_wait_" + tag,
        out_shape=tuple(pltpu.HBM(a.shape, a.dtype) for a in list(slabs) + list(lands)),
        in_specs=[HBM] * (2 * n) + [SEM, SEM, pl.BlockSpec(memory_space=pl.ANY)], out_specs=tuple([HBM] * (2 * n)),
        input_output_aliases={i: i for i in range(2 * n)},
        compiler_params=pltpu.CompilerParams(has_side_effects=DATAFLOW),
    )(*slabs, *lands, send_sems, recv_sems, after)
    return list(out[:n]), list(out[n:])


def _tile2(rows, cols):
    fits = lambda r, c: r * c * 4 <= BLOCK_BYTES
    if fits(rows, cols):
        return rows, cols
    for r in (1024, 512, 256, 128, 64):
        if rows % r == 0 and fits(r, cols):
            return r, cols
    return rows, next(cols // k for k in (2, 3, 4, 6, 8, 12, 16) if cols % (k * LANES) == 0 and fits(rows, cols // k))


def pair_add(g, p, c, name):
    _, hr, cols = p.shape
    tm, tc = _tile2(hr, cols)
    per = hr // tm

    def body(c_ref, g_ref, p_ref, o_ref):
        o_ref[...] = (g_ref[...] + p_ref[...]).astype(o_ref.dtype)

    return pl.pallas_call(
        body, name=name,
        grid_spec=pltpu.PrefetchScalarGridSpec(
            num_scalar_prefetch=1, grid=(4, per, cols // tc),
            in_specs=[pl.BlockSpec((None, tm, tc), lambda k, i, j, c_ref: (k, c_ref[0] * per + i, j)),
                      pl.BlockSpec((None, tm, tc), lambda k, i, j, c_ref: (k, i, j))],
            out_specs=pl.BlockSpec((None, tm, tc), lambda k, i, j, c_ref: (k, i, j))),
        out_shape=jax.ShapeDtypeStruct((4, hr, cols), BF16),
        compiler_params=_params(("arbitrary", "arbitrary", "arbitrary")),
    )(c.reshape(1).astype(jnp.int32), g, p)


def scatter_start(sums, tag):
    n = len(sums)

    def body(*refs):
        s_refs, land_refs = refs[:n], refs[n:2 * n]
        send_sems, recv_sems = refs[2 * n], refs[2 * n + 1]
        token = refs[-1]
        x, y, c = _place()
        k = 2 * x + y
        for i in range(n):
            for j, (px, py) in enumerate(_other_chips(x, y)):
                _remote(s_refs[i].at[2 * px + py], land_refs[i].at[k], send_sems.at[j * n + i], recv_sems.at[j * n + i],
                        (px, py, c)).start()
        token[...] = jnp.zeros_like(token)

    hbm = lambda a: pltpu.with_memory_space_constraint(a, pltpu.HBM)
    return pl.pallas_call(
        body, name="scatter_start_" + tag,
        out_shape=(pltpu.SemaphoreType.DMA((3 * n,)), pltpu.SemaphoreType.DMA((3 * n,)),
                   *[pltpu.HBM(s.shape, s.dtype) for s in sums], *[pltpu.HBM(s.shape, s.dtype) for s in sums],
                   jax.ShapeDtypeStruct((8, LANES), F32)),
        in_specs=[HBM] * (2 * n), out_specs=(SEM, SEM, *[HBM] * (2 * n), pl.BlockSpec(memory_space=pltpu.VMEM)),
        input_output_aliases={i: 2 + i for i in range(2 * n)},
        compiler_params=pltpu.CompilerParams(has_side_effects=DATAFLOW),
    )(*[hbm(s) for s in sums], *[hbm(lax.empty(s.shape, s.dtype)) for s in sums])


def scatter_wait(started, after, tag):
    send_sems, recv_sems, *rest = started
    n = (len(rest) - 1) // 2
    sums, lands = rest[:n], rest[n:2 * n]

    def body(*refs):
        s_refs, land_refs = refs[:n], refs[n:2 * n]
        send_ref, recv_ref = refs[2 * n], refs[2 * n + 1]
        x, y, c = _place()
        for i in range(n):
            for j, (px, py) in enumerate(_other_chips(x, y)):
                cp = _remote(s_refs[i].at[2 * px + py], land_refs[i].at[2 * px + py], send_ref.at[j * n + i],
                             recv_ref.at[j * n + i], (px, py, c))
                cp.wait_send()
                cp.wait_recv()

    out = pl.pallas_call(
        body, name="scatter_wait_" + tag,
        out_shape=tuple(pltpu.HBM(s.shape, s.dtype) for s in sums + lands),
        in_specs=[HBM] * (2 * n) + [SEM, SEM, pl.BlockSpec(memory_space=pl.ANY)], out_specs=tuple([HBM] * (2 * n)),
        input_output_aliases={i: i for i in range(2 * n)},
        compiler_params=pltpu.CompilerParams(has_side_effects=DATAFLOW),
    )(*sums, *lands, send_sems, recv_sems, after)
    return list(out[:n]), list(out[n:])


def sum_chips(landed, own, chip, core, name):
    _, hr, cols = landed.shape
    tm, tc = _tile2(hr, cols)
    per = hr // tm

    def body(idx_ref, l0, l1, l2, l3, own_ref, o_ref):
        mine = own_ref[...].astype(F32)
        v = [jnp.where(idx_ref[0] == k, mine, ref[...].astype(F32)) for k, ref in enumerate((l0, l1, l2, l3))]
        o_ref[...] = ((v[0] + v[1]) + v[2]) + v[3]

    slot = lambda k: pl.BlockSpec((None, tm, tc),
                                  lambda i, j, idx: (jnp.where(idx[0] == k, (k + 1) & 3, k), i, j))
    return pl.pallas_call(
        body, name=name,
        grid_spec=pltpu.PrefetchScalarGridSpec(
            num_scalar_prefetch=1, grid=(per, cols // tc),
            in_specs=[slot(0), slot(1), slot(2), slot(3),
                      pl.BlockSpec((None, tm, tc), lambda i, j, idx: (idx[0], i, j))],
            out_specs=pl.BlockSpec((tm, tc), lambda i, j, idx: (idx[1] * per + i, j))),
        out_shape=jax.ShapeDtypeStruct((2 * hr, cols), F32),
        compiler_params=_params(("arbitrary", "arbitrary")),
    )(jnp.stack([chip, core]).astype(jnp.int32), landed, landed, landed, landed, own)


def exchange_halves(bufs):
    n = len(bufs)

    def body(*refs):
        out_refs = refs[n:2 * n]
        send_sems, recv_sems = refs[2 * n:]
        x, y, c = _place()
        sends, recvs = [], []
        for i in range(n):
            hr = bufs[i].shape[0] // 2
            own = out_refs[i].at[pl.ds(c * hr, hr), :]
            other = out_refs[i].at[pl.ds((1 - c) * hr, hr), :]
            sends.append(_remote(own, own, send_sems.at[i], recv_sems.at[i], (x, y, 1 - c)))
            recvs.append(_remote(other, other, send_sems.at[i], recv_sems.at[i], (x, y, 1 - c)))
        for cp in sends:
            cp.start()
        for cp in recvs:
            cp.wait_recv()
        for cp in sends:
            cp.wait_send()

    return pl.pallas_call(
        body, name="exchange_halves", in_specs=[HBM] * n, out_specs=[HBM] * n,
        out_shape=[jax.ShapeDtypeStruct(b.shape, b.dtype) for b in bufs],
        input_output_aliases={i: i for i in range(n)},
        scratch_shapes=[pltpu.SemaphoreType.DMA((n,)), pltpu.SemaphoreType.DMA((n,))],
    )(*bufs)


def _relayout(name, arrays, in_blocks, out_blocks, out_shapes, fn):
    rows = 128
    spec = lambda blk: pl.BlockSpec(blk, (lambda i: (0, i, 0)) if len(blk) == 3 else (lambda i: (i, 0)))

    def body(*refs):
        n_in = len(arrays)
        outs = fn(*[r[...] for r in refs[:n_in]])
        for ref, val in zip(refs[n_in:], outs, strict=True):
            if isinstance(val, list):
                for k, piece in enumerate(val):
                    ref[k] = piece
            else:
                ref[...] = val

    return pl.pallas_call(
        body, name=name, grid=(D // rows,),
        in_specs=[spec(b) for b in in_blocks], out_specs=[spec(b) for b in out_blocks], out_shape=out_shapes,
        compiler_params=_params(("arbitrary",)),
    )(*arrays)


def assemble_in_proj(g):
    def fn(v):
        w = jnp.concatenate([v[k] for k in range(4)], axis=1)
        return (jnp.concatenate([w[:, :ORIG_Z], w[:, ORIG_GA:], w[:, ORIG_XBC:ORIG_DT], w[:, ORIG_Z:ORIG_XBC],
                                 w[:, ORIG_DT:ORIG_GA], jnp.zeros((w.shape[0], IN_PAD - IN_ORIG), w.dtype)], axis=1),)

    cols = g.shape[2]
    return _relayout("assemble_in_proj", [g], [(4, 128, cols)], [(128, IN_PAD)],
                     [jax.ShapeDtypeStruct((D, IN_PAD), g.dtype)], fn)[0]


def rows_exchange(a, name):
    hr = a.shape[0] // 2

    def body(a_ref, out_ref, send_sem, recv_sem):
        x, y, c = _place()
        cp = _remote(a_ref.at[pl.ds((1 - c) * hr, hr), :], out_ref, send_sem, recv_sem, (x, y, 1 - c))
        cp.start()
        cp.wait()

    return pl.pallas_call(
        body, name=name, in_specs=[HBM], out_specs=HBM,
        out_shape=jax.ShapeDtypeStruct((hr, a.shape[1]), a.dtype),
        scratch_shapes=[pltpu.SemaphoreType.DMA, pltpu.SemaphoreType.DMA],
    )(a)


def split_pair_add(dw, received, core):
    cols = IN_ORIG // 4
    rows, hr = 128, D // 2
    per = hr // rows

    def body(c_ref, own_ref, got_ref, o_ref):
        d = own_ref[...] + got_ref[...]
        w = jnp.concatenate([d[:, :COL_GA], d[:, COL_Z:COL_DT], d[:, COL_XBC:COL_Z], d[:, COL_DT:COL_DT + 32],
                             d[:, COL_GA:COL_XBC]], axis=1)
        for k in range(4):
            o_ref[k] = w[:, k * cols:(k + 1) * cols].astype(o_ref.dtype)

    return pl.pallas_call(
        body, name="split_pair_add",
        grid_spec=pltpu.PrefetchScalarGridSpec(
            num_scalar_prefetch=1, grid=(per,),
            in_specs=[pl.BlockSpec((rows, IN_PAD), lambda i, c_ref: (c_ref[0] * per + i, 0)),
                      pl.BlockSpec((rows, IN_PAD), lambda i, c_ref: (i, 0))],
            out_specs=pl.BlockSpec((4, rows, cols), lambda i, c_ref: (0, i, 0))),
        out_shape=jax.ShapeDtypeStruct((4, hr, cols), BF16),
        compiler_params=_params(("arbitrary",)),
    )(core.reshape(1).astype(jnp.int32), dw, received)


def ada_prepare(c_all, w_ada, hgrn_lb):
    def body(c_ref, w_ref, lb_ref, mod_ref, row_ref):
        mod_ref[...] = hdot(silu(c_ref[...]), w_ref[...])
        row_ref[...] = sigmoid(lb_ref[0:1, :] - lb_ref[1:2, :])

    return pl.pallas_call(
        body, name="ada_prepare",
        out_shape=[jax.ShapeDtypeStruct((8, w_ada.shape[1]), F32), jax.ShapeDtypeStruct((1, D), F32)],
        compiler_params=pltpu.CompilerParams(vmem_limit_bytes=VMEM_LIMIT),
    )(c_all, w_ada, hgrn_lb)


SMALL_SEGS = (("mod", 6 * D), ("lb", D), ("gnorm", LANES), ("conv_w", 4 * CONV_DIM), ("conv_b", CONV_DIM),
              ("dt_bias", LANES), ("a_log", LANES), ("d", B_INNER), ("ssm_norm", B_INNER),
              ("ln1_g", D), ("ln1_b", D), ("ln2_g", D), ("ln2_b", D), ("loss", LANES))
SMALL_PARAMS = ("b_ada", "hgrn_lb", "hgrn_gnorm", "ssm_conv_b", "ssm_dt_bias", "ssm_a_log", "ssm_d", "ssm_norm",
                "ln1_g", "ln1_b", "ln2_g", "ln2_b")


def finalize_small(g_all, c_all, dmod_cols, params, m, v):
    n_p = len(SMALL_PARAMS)
    offs, o = {}, 0
    for nm, width in SMALL_SEGS:
        offs[nm] = (o, width)
        o += width

    def body(*refs):
        g_ref, c_ref, dm_ref = refs[:3]
        p_refs = refs[3:3 + n_p]
        m_refs = refs[3 + n_p:3 + 2 * n_p]
        v_refs = refs[3 + 2 * n_p:3 + 3 * n_p]
        outs = refs[3 + 3 * n_p:]
        gwa_ref, gcw_ref, loss_ref = outs[:3]
        res = outs[3:]
        total = jnp.sum(g_ref[...], axis=0, keepdims=True)
        seg = lambda nm: total[:, offs[nm][0]:offs[nm][0] + offs[nm][1]]
        loss_ref[...] = seg("loss")
        gwa_ref[...] = hdot(silu(c_ref[...]), dm_ref[...], "tn")
        cw = seg("conv_w")
        for j in range(4):
            gcw_ref[j:j + 1, :] = cw[:, j * CONV_DIM:(j + 1) * CONV_DIM]
        hc = lax.broadcasted_iota(jnp.int32, (B_INNER, LANES), 0)
        hj = lax.broadcasted_iota(jnp.int32, (B_INNER, LANES), 1)
        per_head = ((hc >> 6) == hj).astype(F32)
        heads = lambda nm: hdot(jnp.broadcast_to(seg(nm), (8, B_INNER)), per_head)[0:1, 0:32]
        lbp = sigmoid(p_refs[1][0:1, :] - p_refs[1][1:2, :])
        g_row = seg("lb") * lbp * (1.0 - lbp)
        grads = {"b_ada": seg("mod"), "hgrn_gnorm": seg("gnorm"), "ssm_conv_b": seg("conv_b"),
                 "ssm_dt_bias": seg("dt_bias")[:, 0:32], "ssm_a_log": seg("a_log")[:, 0:32], "ssm_d": heads("d"),
                 "ssm_norm": seg("ssm_norm"), "ln1_g": seg("ln1_g"), "ln1_b": seg("ln1_b"),
                 "ln2_g": seg("ln2_g"), "ln2_b": seg("ln2_b")}
        for i, nm in enumerate(SMALL_PARAMS):
            g_out, d_out, m_out, v_out = res[4 * i:4 * i + 4]
            if nm == "hgrn_lb":
                for row, gv in ((0, g_row), (1, -g_row)):
                    sl = slice(row, row + 1)
                    dl, mn, vn = adamw(p_refs[i][sl, :], gv, m_refs[i][sl, :], v_refs[i][sl, :])
                    g_out[sl, :], d_out[sl, :], m_out[sl, :], v_out[sl, :] = gv, dl, mn, vn
            else:
                gv = grads[nm]
                dl, mn, vn = adamw(p_refs[i][...], gv, m_refs[i][...], v_refs[i][...])
                g_out[...], d_out[...], m_out[...], v_out[...] = gv, dl, mn, vn

    out_shape = [jax.ShapeDtypeStruct((D, dmod_cols.shape[1]), F32), jax.ShapeDtypeStruct((4, CONV_DIM), F32),
                 jax.ShapeDtypeStruct((1, LANES), F32)]
    for p in params:
        out_shape += [jax.ShapeDtypeStruct(p.shape, F32)] * 4
    return pl.pallas_call(
        body, name="finalize_small", out_shape=out_shape,
        compiler_params=pltpu.CompilerParams(vmem_limit_bytes=VMEM_LIMIT),
    )(g_all, c_all, dmod_cols, *params, *m, *v)


def adam_update(w, g, m, v, name):
    rows, cols = w.shape
    tm, tc = _tile2(rows, cols)

    def body(w_ref, g_ref, m_ref, v_ref, d_ref, mo_ref, vo_ref):
        d_ref[...], mo_ref[...], vo_ref[...] = adamw(w_ref[...], g_ref[...], m_ref[...], v_ref[...])

    spec = pl.BlockSpec((tm, tc), lambda i, j: (i, j))
    return pl.pallas_call(
        body, name=name, grid=(rows // tm, cols // tc), in_specs=[spec] * 4, out_specs=[spec] * 3,
        out_shape=[jax.ShapeDtypeStruct((rows, cols), F32)] * 3,
        compiler_params=_params(("arbitrary", "arbitrary")),
    )(w, g, m, v)


def kernel(x, c, w_ada, b_ada, w_in, hgrn_lb, hgrn_gnorm, ssm_conv_w, ssm_conv_b, ssm_dt_bias, ssm_a_log, ssm_d, ssm_norm, w_branch_a, w_branch_b, w_o, ln1_g, ln1_b, w_ffn_gate, w_ffn_up, w_ffn_down, ln2_g, ln2_b, loss_target, m_w_ada, m_b_ada, m_w_in, m_hgrn_lb, m_hgrn_gnorm, m_ssm_conv_w, m_ssm_conv_b, m_ssm_dt_bias, m_ssm_a_log, m_ssm_d, m_ssm_norm, m_w_branch_a, m_w_branch_b, m_w_o, m_ln1_g, m_ln1_b, m_w_ffn_gate, m_w_ffn_up, m_w_ffn_down, m_ln2_g, m_ln2_b, v_w_ada, v_b_ada, v_w_in, v_hgrn_lb, v_hgrn_gnorm, v_ssm_conv_w, v_ssm_conv_b, v_ssm_dt_bias, v_ssm_a_log, v_ssm_d, v_ssm_norm, v_w_branch_a, v_w_branch_b, v_w_o, v_ln1_g, v_ln1_b, v_w_ffn_gate, v_w_ffn_up, v_w_ffn_down, v_ln2_g, v_ln2_b):
    given = dict(locals())
    chip = 2 * lax.axis_index("x") + lax.axis_index("y")
    core = lax.axis_index("c")
    t = x.shape[1]

    first = gather_rows(jnp.concatenate([c, ssm_conv_w.reshape(1, CONV_DIM)], axis=1), "gather_cond").reshape(8, D + CONV_DIM)
    c_all = first[:, :D]
    conv_w = first[0::2, D:].reshape(4, 4, CONV_DIM // 4).transpose(1, 0, 2).reshape(4, CONV_DIM)
    mod_part, lb_row = ada_prepare(c_all, w_ada[0], hgrn_lb)
    mod_cols = w_ada.shape[2]
    mod_row = exchange_rows(mod_part.reshape(8, 1, mod_cols), "exchange_mod").reshape(1, 6 * D) + b_ada

    local = {nm: given[nm][0] for nm in SHARDED if nm != "w_ffn_in"}
    local["w_ffn_in"] = jnp.concatenate([w_ffn_gate[0].T, w_ffn_up[0].T], axis=0)
    shards = [local[nm].astype(BF16) for nm in SHARDED]
    send_in, recv_in, sent_in, land_in, started_in = gather_start(shards[:1], mod_row, "in")
    mod_row = mod_row + started_in[0:1, 0:1]
    mod = tuple(mod_row[:, i * D:(i + 1) * D] for i in range(6))
    send_rest, recv_rest, *flying = gather_start(shards[1:], started_in, "rest")
    n_rest = len(SHARDED) - 1
    sent_rest, land_rest = flying[:n_rest], flying[n_rest:2 * n_rest]
    with_own = lambda land, shard: lax.dynamic_update_slice(land, shard[None], (chip, 0, 0))

    class Weights:
        def input_projection(self, after):
            (own,), land = gather_wait(send_in, recv_in, [sent_in], [land_in], after, "in")
            (land,) = forward_wait(forward_start(land, "in"), after, "in")
            return assemble_in_proj(with_own(land, own))

        def start_rest(self, after):
            self.own, landed = gather_wait(send_rest, recv_rest, sent_rest, land_rest, after, "rest")
            self.started = forward_start(landed, "rest")
            return self.started[-1]

        def rest(self, after):
            got = {nm: with_own(land, s) for nm, land, s in zip(SHARDED[1:], forward_wait(self.started, after, "rest"), self.own, strict=True)}
            whole = lambda nm: got[nm].reshape(4 * got[nm].shape[1], got[nm].shape[2])
            return tuple(whole(nm) for nm in SHARDED[1:])

    wts = Weights()

    per_head = lambda p: jnp.pad(p, ((0, 0), (0, LANES - p.shape[1])))
    small = (lb_row, hgrn_gnorm, conv_w, ssm_conv_b, per_head(ssm_dt_bias), per_head(ssm_a_log),
             jnp.repeat(ssm_d[0], B_INNER // 32)[None], ssm_norm, ln1_g, ln1_b, ln2_g, ln2_b)
    by_rows = lambda g: g.reshape(4, g.shape[0] // 4, g.shape[1])
    travelling = {}

    def start_early(dws):
        travelling["pair"] = pair_start([by_rows(dw) for dw in dws], "early")
        return travelling["pair"][-1]

    def between_scans(after):
        slabs, received = pair_wait(travelling["pair"], after, "early")
        travelling["pairs"] = [pair_add(s, r, core, "pair_add_" + nm) for nm, s, r in zip(SHARDED[1:], slabs, received, strict=True)]
        travelling["started"] = scatter_start(travelling["pairs"], "early")
        return travelling["started"][-1]

    def finish_early(after):
        travelling["pairs"], travelling["landed"] = scatter_wait(travelling["started"], after, "early")

    def start_last(dw_in):
        travelling["pairs_in"] = [split_pair_add(dw_in, rows_exchange(dw_in, "pair_exchange_last"), core)]
        travelling["started_in"] = scatter_start(travelling["pairs_in"], "last")
        return travelling["started_in"][-1]

    loss, grad_x, d_mod, d_wts, d_small = local_step(x[0], loss_target[0], mod, wts, small,
                                                     start_early, between_scans, finish_early, start_last)

    d_lb, d_gn, d_cw, d_cb, d_dtb, d_alog, d_dsk, d_nw, d_l1g, d_l1b, d_l2g, d_l2b = d_small
    row = jnp.concatenate(list(d_mod) + [d_lb, d_gn, d_cw.reshape(1, 4 * CONV_DIM), d_cb, d_dtb, d_alog, d_dsk, d_nw,
                                          d_l1g, d_l1b, d_l2g, d_l2b, jnp.pad(loss, ((0, 0), (0, LANES - 1)))], axis=1)
    g_all = gather_rows(row, "gather_small_grads").reshape(8, row.shape[1])
    dmod_cols = lax.dynamic_slice_in_dim(g_all, chip * mod_cols, mod_cols, axis=1)
    fin = finalize_small(g_all, c_all, dmod_cols, [given[n] for n in SMALL_PARAMS],
                         [given["m_" + n] for n in SMALL_PARAMS], [given["v_" + n] for n in SMALL_PARAMS])
    grads, deltas, new_m, new_v = {}, {}, {}, {}
    grads["w_ada"] = fin[0][None]
    grads["ssm_conv_w"] = lax.dynamic_slice_in_dim(fin[1], chip * (CONV_DIM // 4), CONV_DIM // 4, axis=1)[None]
    for i, nm in enumerate(SMALL_PARAMS):
        grads[nm], deltas[nm], new_m[nm], new_v[nm] = fin[3 + 4 * i:7 + 4 * i]

    pairs_in, landed_in = scatter_wait(travelling["started_in"], fin[3], "last")
    pairs, landed = pairs_in + travelling["pairs"], landed_in + travelling["landed"]
    halves = [sum_chips(r, p, chip, core, "sum_chips_" + nm) for nm, r, p in zip(SHARDED, landed, pairs, strict=True)]
    reduced = dict(zip(SHARDED, exchange_halves(halves), strict=True))
    reduced["w_ada"], reduced["ssm_conv_w"] = grads["w_ada"][0], grads["ssm_conv_w"][0]
    reduced["w_in"] = reduced["w_in"].T
    reduced["w_ffn_gate"], reduced["w_ffn_up"] = reduced["w_ffn_in"][:FFN_SHARD], reduced["w_ffn_in"][FFN_SHARD:]
    for nm in ("w_ada", "ssm_conv_w", "w_in", "w_branch_a", "w_branch_b", "w_o", "w_ffn_gate", "w_ffn_up", "w_ffn_down"):
        flipped = nm in ("w_in", "w_ffn_gate", "w_ffn_up")
        work = (lambda a: a[0].T) if flipped else (lambda a: a[0])
        back = (lambda a: a.T[None]) if flipped else (lambda a: a[None])
        d_, m_, v_ = adam_update(work(given[nm]), reduced[nm], work(given["m_" + nm]), work(given["v_" + nm]), "adam_" + nm)
        grads[nm], deltas[nm], new_m[nm], new_v[nm] = back(reduced[nm]), back(d_), back(m_), back(v_)

    names = ("w_ada", "b_ada", "w_in", "hgrn_lb", "hgrn_gnorm", "ssm_conv_w", "ssm_conv_b", "ssm_dt_bias", "ssm_a_log",
             "ssm_d", "ssm_norm", "w_branch_a", "w_branch_b", "w_o", "ln1_g", "ln1_b", "w_ffn_gate", "w_ffn_up",
             "w_ffn_down", "ln2_g", "ln2_b")
    return (fin[2][0, 0], grad_x[None], *[grads[n] for n in names], *[deltas[n] for n in names],
            *[new_m[n] for n in names], *[new_v[n] for n in names])
```

```python
import functools

import jax
import jax.numpy as jnp
from jax import lax
from jax.experimental import pallas as pl
from jax.experimental.pallas import tpu as pltpu

F32, BF16 = jnp.float32, jnp.bfloat16
HI = lax.Precision.HIGHEST
MESH = pl.DeviceIdType.MESH

D = 1024
CHUNK = 64
LANES = 128
N_HEADS_A = 8
N_GROUPS_B = 4
B_INNER = 2048
CONV_DIM = 3072
D_FF = 2816
ALPHA = 2.0 ** 0.25
LN_EPS = 1e-5
RMS_EPS = 1e-6
ADAM_LR, ADAM_B1, ADAM_B2, ADAM_EPS, ADAM_WD, ADAM_STEP = 0.001, 0.9, 0.999, 1e-08, 0.01, 10

IN_ORIG = 11296
IN_PAD = 11520
COL_GA, COL_GB, COL_XBC, COL_Z, COL_DT = 4096, 5120, 6144, 9216, 11264
ORIG_Z, ORIG_XBC, ORIG_DT, ORIG_GA = 4096, 6144, 9216, 9248

SHARDED = ("w_in", "w_branch_a", "w_branch_b", "w_o", "w_ffn_in", "w_ffn_down")
FFN_SHARD = D_FF // 4
VMEM_LIMIT = 56 * 1024 * 1024
BLOCK_BYTES = 2 * 1024 * 1024
_DIMS = {"nn": (((1,), (0,)), ((), ())), "nt": (((1,), (1,)), ((), ())), "tn": (((0,), (0,)), ((), ()))}


def _bd(a, b, mode):
    return lax.dot_general(a.astype(BF16), b.astype(BF16), _DIMS[mode], preferred_element_type=F32)


@functools.partial(jax.custom_vjp, nondiff_argnums=(2,))
def bdot(a, b, mode):
    return _bd(a, b, mode)


def _bdot_fwd(a, b, mode):
    return _bd(a, b, mode), (a, b)


def _bdot_bwd(mode, res, g):
    a, b = res
    if mode == "nn":
        return _bd(g, b, "nt"), _bd(a, g, "tn")
    if mode == "nt":
        return _bd(g, b, "nn"), _bd(g, a, "tn")
    return _bd(b, g, "nt"), _bd(a, g, "nn")


bdot.defvjp(_bdot_fwd, _bdot_bwd)


def hdot(a, b, mode="nn"):
    return lax.dot_general(a, b, _DIMS[mode], precision=HI, preferred_element_type=F32)


def _raw(a, b, mode):
    return lax.dot_general(a, b, _DIMS[mode], preferred_element_type=F32)


def _split(x, n):
    parts, rest = [], x
    for _ in range(n):
        p = rest.astype(BF16)
        parts.append(p)
        rest = rest - p.astype(F32)
    return parts


def _od(a, b, mode, exact):
    if exact == 1:
        e = b.astype(BF16)
        p = _split(a, 3)
        return (_raw(p[2], e, mode) + _raw(p[1], e, mode)) + _raw(p[0], e, mode)
    e = a.astype(BF16)
    p = _split(b, 3)
    return (_raw(e, p[2], mode) + _raw(e, p[1], mode)) + _raw(e, p[0], mode)


@functools.partial(jax.custom_vjp, nondiff_argnums=(2, 3))
def odot(a, b, mode, exact):
    return _od(a, b, mode, exact)


def _odot_fwd(a, b, mode, exact):
    return _od(a, b, mode, exact), (a, b)


def _odot_bwd(mode, exact, res, g):
    a, b = res
    if exact == 1:
        da = {"nn": lambda: _od(g, b, "nt", 1), "nt": lambda: _od(g, b, "nn", 1), "tn": lambda: _od(b, g, "nt", 0)}[mode]()
        return da, jnp.zeros_like(b)
    db = {"nn": lambda: _od(a, g, "tn", 0), "nt": lambda: _od(g, a, "tn", 1), "tn": lambda: _od(a, g, "nn", 0)}[mode]()
    return jnp.zeros_like(a), db


odot.defvjp(_odot_fwd, _odot_bwd)


_BDIMS = {"bnn": (((2,), (1,)), ((0,), (0,))), "bnt": (((2,), (2,)), ((0,), (0,))), "btn": (((1,), (1,)), ((0,), (0,)))}


def _braw(a, b, mode):
    return lax.dot_general(a, b, _BDIMS[mode], preferred_element_type=F32)


def _bdb(a, b, mode):
    return _braw(a.astype(BF16), b.astype(BF16), mode)


def _d3b(a, b, mode):
    ah, al = _split(a, 2)
    bh, bl = _split(b, 2)
    return _braw(ah, bh, mode) + (_braw(ah, bl, mode) + _braw(al, bh, mode))


def _batched_bwd(f):
    def bwd(mode, res, g):
        a, b = res
        if mode == "bnn":
            return f(g, b, "bnt"), f(a, g, "btn")
        if mode == "bnt":
            return f(g, b, "bnn"), f(g, a, "btn")
        return f(b, g, "bnt"), f(a, g, "bnn")
    return bwd


@functools.partial(jax.custom_vjp, nondiff_argnums=(2,))
def bdot_b(a, b, mode):
    return _bdb(a, b, mode)


bdot_b.defvjp(lambda a, b, mode: (_bdb(a, b, mode), (a, b)), _batched_bwd(_bdb))


@functools.partial(jax.custom_vjp, nondiff_argnums=(2,))
def dot3_b(a, b, mode):
    return _d3b(a, b, mode)


dot3_b.defvjp(lambda a, b, mode: (_d3b(a, b, mode), (a, b)), _batched_bwd(_d3b))


def _cum(tril3, x, mode):
    e = tril3.astype(BF16)
    p = _split(x, 3)
    return (_braw(e, p[2], mode) + _braw(e, p[1], mode)) + _braw(e, p[0], mode)


@jax.custom_vjp
def chunk_cumsum(tril3, x):
    return _cum(tril3, x, "bnn")


chunk_cumsum.defvjp(lambda t, x: (_cum(t, x, "bnn"), t), lambda t, g: (jnp.zeros_like(t), _cum(t, g, "btn")))


def _unstack(axis, n):
    @jax.custom_vjp
    def un(x):
        return tuple(lax.index_in_dim(x, i, axis, keepdims=False) for i in range(n))

    un.defvjp(lambda x: (un(x), None), lambda _, g: (jnp.stack(g, axis=axis),))
    return un


def _split_last(n, w):
    @jax.custom_vjp
    def sp(x):
        return tuple(x[..., i * w:(i + 1) * w] for i in range(n))

    sp.defvjp(lambda x: (sp(x), None), lambda _, g: (jnp.concatenate(g, axis=-1),))
    return sp


def sigmoid(x):
    return 1.0 / (1.0 + jnp.exp(-x))


def silu(x):
    return x * sigmoid(x)


def softplus(x):
    return jnp.maximum(x, 0.0) + jnp.log1p(jnp.exp(jnp.minimum(x, -x)))


def _ln(x):
    mu = jnp.mean(x, axis=-1, keepdims=True)
    xc = x - mu
    return xc * lax.rsqrt(jnp.mean(xc * xc, axis=-1, keepdims=True) + LN_EPS)


def _tril64():
    r = lax.broadcasted_iota(jnp.int32, (CHUNK, CHUNK), 0)
    c = lax.broadcasted_iota(jnp.int32, (CHUNK, CHUNK), 1)
    return (r >= c).astype(F32)


def hgrn_block(q, fl, iv, gr, st, lb, gn):
    tb = q.shape[0]
    nc = tb // CHUNK
    nh = N_HEADS_A
    heads = _split_last(nh, LANES)
    to4 = lambda a: jnp.stack(heads(a), axis=0).reshape(nh, nc, CHUNK, LANES)
    flat = lambda a: a.reshape(nh * nc, CHUNK, LANES)
    f = lb + (1.0 - lb) * sigmoid(fl)
    gl4, k4, qf4, v4, gr4 = to4(jnp.log(f)), to4(1.0 - f), to4(silu(q) * (128 ** -0.5)), to4(iv), to4(gr)
    tril = _tril64()
    b4 = chunk_cumsum(jnp.broadcast_to(tril[None], (nh * nc, CHUNK, CHUNK)), flat(gl4)).reshape(gl4.shape)
    blast = jnp.sum(gl4, axis=2, keepdims=True)
    ref = lax.stop_gradient(0.5 * blast)
    sc = dot3_b(flat(qf4 * jnp.exp(b4 - ref)), flat(k4 * jnp.exp(ref - b4)), "bnt") * tril
    o_intra = bdot_b(sc, flat(v4), "bnn").reshape(gl4.shape)
    chunks = _unstack(1, nc)
    qe, v_c, kd, dec = chunks(qf4 * jnp.exp(b4)), chunks(v4), chunks(k4 * jnp.exp(blast - b4)), chunks(jnp.exp(blast))
    o_inter = []
    for c in range(nc):
        o_inter.append(bdot_b(qe[c], st, "bnt"))
        st = st * dec[c] + bdot_b(v_c[c], kd[c], "btn")
    o = o_intra + jnp.stack(o_inter, axis=1)
    on = o * lax.rsqrt(jnp.mean(o * o, axis=-1, keepdims=True) + RMS_EPS) * gn
    out = (on * silu(gr4)).reshape(nh, tb, LANES)
    return jnp.concatenate(_unstack(0, nh)(out), axis=1), st


def ssd_consts(g):
    i32 = jnp.int32
    ej = lax.broadcasted_iota(i32, (LANES, 512), 0)
    ec = lax.broadcasted_iota(i32, (LANES, 512), 1)
    expand = (ej == g * 8 + (ec >> 6)).astype(F32)
    ts = lax.broadcasted_iota(i32, (CHUNK, 512), 0)
    tc = lax.broadcasted_iota(i32, (CHUNK, 512), 1)
    itile = (ts == (tc & 63)).astype(F32)
    maskall = ts >= (tc & 63)
    br = lax.broadcasted_iota(i32, (256, 256), 0)
    bc = lax.broadcasted_iota(i32, (256, 256), 1)
    blockmask = ((br >> 6) == (bc >> 6)).astype(F32)
    return expand, itile, maskall, blockmask, _tril64()


def ssd_block(x, bm, cm, dt, z, st, dtb, alog, dsk, nw, cs):
    expand, itile, maskall, blockmask, tril = cs
    tb = x.shape[0]
    nc = tb // CHUNK
    delta_heads = softplus(dt + dtb)
    delta = odot(delta_heads, expand, "nn", 1)
    a = odot(-jnp.exp(alog) * delta_heads, expand, "nn", 1)
    xdt = x * delta
    by_chunk = lambda v: v.reshape(nc, CHUNK, v.shape[-1])
    a3, xdt3, bm3, cm3 = by_chunk(a), by_chunk(xdt), by_chunk(bm), by_chunk(cm)
    acum3 = chunk_cumsum(jnp.broadcast_to(tril[None], (nc, CHUNK, CHUNK)), a3)
    alast3 = jnp.sum(a3, axis=1, keepdims=True)
    cb3 = bdot_b(cm3, jnp.concatenate([bm3] * 8, axis=1), "bnt")
    arow3 = jnp.sum(acum3 * itile, axis=1, keepdims=True)
    dec3 = jnp.exp(jnp.where(maskall, acum3 - arow3, -1e30))
    halves = _split_last(2, 256)
    intra = [bdot_b(m, jnp.concatenate([xh] * 4, axis=1) * blockmask, "bnn")
             for m, xh in zip(halves(cb3 * dec3), halves(xdt3))]
    chunks = _unstack(0, nc)
    cm_c, bm_c, xw_c, dec_c = chunks(cm3), chunks(bm3), chunks(xdt3 * jnp.exp(alast3 - acum3)), chunks(jnp.exp(alast3))
    inter = []
    for c in range(nc):
        inter.append(bdot(cm_c[c], st, "nn"))
        st = st * dec_c[c] + bdot(bm_c[c], xw_c[c], "tn")
    st_new = st
    y = (jnp.concatenate(intra, axis=-1) + jnp.stack(inter, axis=0) * jnp.exp(acum3)).reshape(tb, 512)
    yz = (y + x * dsk) * silu(z)
    return yz * lax.rsqrt(jnp.mean(yz * yz, axis=-1, keepdims=True) + RMS_EPS) * nw, st_new


def adamw(w, g, m, v):
    m = ADAM_B1 * m + (1.0 - ADAM_B1) * g
    v = ADAM_B2 * v + (1.0 - ADAM_B2) * jnp.square(g)
    m_hat = m / (1.0 - ADAM_B1 ** ADAM_STEP)
    v_hat = v / (1.0 - ADAM_B2 ** ADAM_STEP)
    return -ADAM_LR * (m_hat / (jnp.sqrt(v_hat) + ADAM_EPS) + ADAM_WD * w), m, v


def _pick(n, cands):
    for c in cands:
        if n % c == 0:
            return c
    return n


def _params(sem):
    return pltpu.CompilerParams(dimension_semantics=sem, vmem_limit_bytes=VMEM_LIMIT)


MATMUL_VMEM_BUDGET = 50 * 1024 * 1024
MATMUL_MIN_STEPS = 4


def matmul(a, b, mode, out_dtype, name, after=None):
    if mode == "nn":
        (m, k), n = a.shape, b.shape[1]
    elif mode == "nt":
        (m, k), n = a.shape, b.shape[0]
    else:
        (k, m), n = a.shape, b.shape[1]
    tk = _pick(k, (2304, 2048, 1408, 1024, 768, 512, 256, 128))
    nk = k // tk
    a_bytes, b_bytes, out_bytes = a.dtype.itemsize, b.dtype.itemsize, jnp.dtype(out_dtype).itemsize

    def vmem(tm_, tn_):
        blocks = 2 * (tm_ * tk * a_bytes + tk * tn_ * b_bytes + tm_ * tn_ * out_bytes)
        return blocks + (tm_ * tn_ * 4 if nk > 1 else 0)

    def traffic(tm_, tn_):
        return (m // tm_) * k * n * b_bytes + (n // tn_ if nk > 1 else 1) * m * k * a_bytes

    sizes = (2304, 2048, 1920, 1408, 1024, 768, 512, 256, 128)
    tiles = [(tm_, tn_) for tm_ in sizes if m % tm_ == 0 for tn_ in sizes if n % tn_ == 0
             if vmem(tm_, tn_) <= MATMUL_VMEM_BUDGET] or [(m, n)]
    pipelined = [t for t in tiles if (m // t[0]) * (n // t[1]) * nk >= MATMUL_MIN_STEPS]
    tm, tn = min(pipelined or tiles, key=lambda t: (traffic(*t), -t[0] * t[1]))
    a_spec = pl.BlockSpec((tk, tm), lambda i, j, kk: (kk, i)) if mode == "tn" else pl.BlockSpec((tm, tk), lambda i, j, kk: (i, kk))
    b_spec = pl.BlockSpec((tn, tk), lambda i, j, kk: (j, kk)) if mode == "nt" else pl.BlockSpec((tk, tn), lambda i, j, kk: (kk, j))

    order = [] if after is None else [after]

    def body(a_ref, b_ref, *rest):
        o_ref, *acc = rest[len(order):]
        part = _bd(a_ref[...], b_ref[...], mode)
        if nk == 1:
            o_ref[...] = part.astype(o_ref.dtype)
            return
        acc_ref, = acc
        kk = pl.program_id(2)

        @pl.when(kk == 0)
        def _():
            acc_ref[...] = part

        @pl.when(jnp.logical_and(kk > 0, kk < nk - 1))
        def _():
            acc_ref[...] += part

        @pl.when(kk == nk - 1)
        def _():
            o_ref[...] = (acc_ref[...] + part).astype(o_ref.dtype)

    return pl.pallas_call(
        body, name=name, grid=(m // tm, n // tn, nk),
        in_specs=[a_spec, b_spec] + [pl.BlockSpec(memory_space=pl.ANY) for _ in order],
        out_specs=pl.BlockSpec((tm, tn), lambda i, j, kk: (i, j)),
        out_shape=jax.ShapeDtypeStruct((m, n), out_dtype),
        scratch_shapes=[pltpu.VMEM((tm, tn), F32)] if nk > 1 else [],
        compiler_params=_params(("parallel", "parallel", "arbitrary")),
    )(a, b, *order)


def rowwise(name, fn, rows, consts, out_rows, out_accs=(), tm_max=256, into=None, new_wide=None):
    t = rows[0][0].shape[0]
    tm = _pick(t, (tm_max, 128, 64, 32, 16, 8))
    n_r, n_c, n_o = len(rows), len(consts), len(out_rows)
    n_alias = 0 if into is None else 1

    def body(*refs):
        r_in = [r[...] for r in refs[:n_r]]
        c_in = [r[...] for r in refs[n_r:n_r + n_c]]
        refs = refs[:n_r + n_c] + refs[n_r + n_c + n_alias:]
        o_refs = refs[n_r + n_c:n_r + n_c + n_o]
        a_refs = refs[n_r + n_c + n_o:]
        ro, ao = fn(r_in, c_in)
        for ref, val in zip(o_refs, ro, strict=True):
            ref[...] = val.astype(ref.dtype)
        if a_refs:
            @pl.when(pl.program_id(0) == 0)
            def _():
                for ref in a_refs:
                    ref[...] = jnp.zeros_like(ref)

            for ref, val in zip(a_refs, ao, strict=True):
                ref[...] += val

    in_specs = [pl.BlockSpec((tm, w), functools.partial(lambda i, cb: (i, cb), cb=cb)) for _, w, cb in rows]
    in_specs += [pl.BlockSpec(c.shape, lambda i: (0, 0)) for c in consts]
    out_specs = [pl.BlockSpec((tm, w), lambda i: (i, 0)) for w, _ in out_rows]
    out_specs += [pl.BlockSpec(s, lambda i: (0, 0)) for s in out_accs]
    out_shape = [jax.ShapeDtypeStruct((t, w), dt) for w, dt in out_rows]
    out_shape += [jax.ShapeDtypeStruct(s, F32) for s in out_accs]
    operands = [r[0] for r in rows] + list(consts)
    aliases = {}
    if into is not None:
        target, cb = into
        in_specs.append(pl.BlockSpec(memory_space=pl.ANY))
        operands.append(target)
        out_specs[0] = pl.BlockSpec((tm, out_rows[0][0]), lambda i: (i, cb))
        out_shape[0] = jax.ShapeDtypeStruct(target.shape, target.dtype)
        aliases = {len(operands) - 1: 0}
    if new_wide is not None:
        width, cb = new_wide
        out_specs[0] = pl.BlockSpec((tm, out_rows[0][0]), lambda i: (i, cb))
        out_shape[0] = jax.ShapeDtypeStruct((t, width), out_rows[0][1])
    return pl.pallas_call(
        body, name=name, grid=(t // tm,), in_specs=in_specs, out_specs=out_specs, out_shape=out_shape,
        input_output_aliases=aliases, compiler_params=_params(("arbitrary",)),
    )(*operands)


def _full(a):
    return (a, a.shape[1], 0)


HGRN_TIME_BLOCK = 256
SSD_TIME_BLOCK = 512


def _time_block(t, most=HGRN_TIME_BLOCK):
    return _pick(t, tuple(b for b in (512, 256, 128, 64) if b <= most))


def _quarters(ref):
    return [ref[:, seg * D:(seg + 1) * D] for seg in range(4)]


def hgrn_forward(proj, lb, gn):
    t = proj.shape[0]
    tb = _time_block(t)
    nb = t // tb

    def body(qfig_ref, lb_ref, gn_ref, o_ref, st_ref, state):
        @pl.when(pl.program_id(0) == 0)
        def _():
            state[...] = jnp.zeros_like(state)

        st = state[...]
        st_ref[...] = st
        out, st_new = hgrn_block(*_quarters(qfig_ref), st, lb_ref[...], gn_ref[...])
        o_ref[...] = out.astype(o_ref.dtype)
        state[...] = st_new

    return pl.pallas_call(
        body, name="hgrn_forward", grid=(nb,),
        in_specs=[pl.BlockSpec((tb, 4 * D), lambda j: (j, 0)),
                  pl.BlockSpec((1, D), lambda j: (0, 0)), pl.BlockSpec((1, LANES), lambda j: (0, 0))],
        out_specs=[pl.BlockSpec((tb, D), lambda j: (j, 0)),
                   pl.BlockSpec((None, N_HEADS_A, LANES, LANES), lambda j: (j, 0, 0, 0))],
        out_shape=[jax.ShapeDtypeStruct((t, D), BF16),
                   jax.ShapeDtypeStruct((nb, N_HEADS_A, LANES, LANES), F32)],
        scratch_shapes=[pltpu.VMEM((N_HEADS_A, LANES, LANES), F32)],
        compiler_params=_params(("arbitrary",)),
    )(proj, lb, gn)


def hgrn_backward(proj, states, d_out, lb, gn, d_proj):
    t = proj.shape[0]
    tb = _time_block(t)
    nb = t // tb

    def body(qfig_ref, st_ref, do_ref, lb_ref, gn_ref, _, dqfig_ref, dlb_ref, dgn_ref, d_state):
        @pl.when(pl.program_id(0) == 0)
        def _():
            d_state[...] = jnp.zeros_like(d_state)
            dlb_ref[...] = jnp.zeros_like(dlb_ref)
            dgn_ref[...] = jnp.zeros_like(dgn_ref)

        _, vjp = jax.vjp(hgrn_block, *_quarters(qfig_ref), st_ref[...], lb_ref[...], gn_ref[...])
        dq, df, di, dg, dst, dlb, dgn = vjp((do_ref[...], d_state[...]))
        for seg, val in enumerate((dq, df, di, dg)):
            dqfig_ref[:, seg * D:(seg + 1) * D] = val.astype(dqfig_ref.dtype)
        d_state[...] = dst
        dlb_ref[...] += dlb
        dgn_ref[...] += dgn

    rev = lambda j: nb - 1 - j
    return pl.pallas_call(
        body, name="hgrn_backward", grid=(nb,),
        in_specs=[pl.BlockSpec((tb, 4 * D), lambda j: (rev(j), 0)),
                  pl.BlockSpec((None, N_HEADS_A, LANES, LANES), lambda j: (rev(j), 0, 0, 0)),
                  pl.BlockSpec((tb, D), lambda j: (rev(j), 0)),
                  pl.BlockSpec((1, D), lambda j: (0, 0)), pl.BlockSpec((1, LANES), lambda j: (0, 0)),
                  pl.BlockSpec(memory_space=pl.ANY)],
        out_specs=[pl.BlockSpec((tb, 4 * D), lambda j: (rev(j), 0)),
                   pl.BlockSpec((1, D), lambda j: (0, 0)), pl.BlockSpec((1, LANES), lambda j: (0, 0))],
        out_shape=[jax.ShapeDtypeStruct(d_proj.shape, d_proj.dtype), jax.ShapeDtypeStruct((1, D), F32),
                   jax.ShapeDtypeStruct((1, LANES), F32)],
        input_output_aliases={5: 0},
        scratch_shapes=[pltpu.VMEM((N_HEADS_A, LANES, LANES), F32)],
        compiler_params=_params(("arbitrary",)),
    )(proj, states, d_out, lb, gn, d_proj)


def _ssd_in_specs(tb, tmap):
    return [pl.BlockSpec((tb, 512), lambda g, j: (tmap(j), g)),
            pl.BlockSpec((tb, LANES), lambda g, j: (tmap(j), 16 + g)),
            pl.BlockSpec((tb, LANES), lambda g, j: (tmap(j), 20 + g)),
            pl.BlockSpec((tb, LANES), lambda g, j: (tmap(j), COL_DT // LANES)),
            pl.BlockSpec((tb, 512), lambda g, j: (tmap(j), COL_Z // 512 + g))]


def ssd_forward(xc, proj, dtb, alog, dsk, nw):
    t = proj.shape[0]
    tb = _time_block(t, SSD_TIME_BLOCK)
    nb = t // tb

    def body(x_ref, b_ref, c_ref, dt_ref, z_ref, dtb_ref, alog_ref, dsk_ref, nw_ref, o_ref, st_ref, state):
        @pl.when(pl.program_id(1) == 0)
        def _():
            state[...] = jnp.zeros_like(state)

        st = state[...]
        st_ref[...] = st
        out, st_new = ssd_block(x_ref[...], b_ref[...], c_ref[...], dt_ref[...], z_ref[...], st,
                                dtb_ref[...], alog_ref[...], dsk_ref[...], nw_ref[...], ssd_consts(pl.program_id(0)))
        o_ref[...] = out.astype(o_ref.dtype)
        state[...] = st_new

    vec = pl.BlockSpec((1, 512), lambda g, j: (0, g))
    heads = pl.BlockSpec((1, LANES), lambda g, j: (0, 0))
    return pl.pallas_call(
        body, name="ssd_forward", grid=(N_GROUPS_B, nb),
        in_specs=_ssd_in_specs(tb, lambda j: j) + [heads, heads, vec, vec],
        out_specs=[pl.BlockSpec((tb, 512), lambda g, j: (j, g)),
                   pl.BlockSpec((None, None, LANES, 512), lambda g, j: (j, g, 0, 0))],
        out_shape=[jax.ShapeDtypeStruct((t, B_INNER), BF16),
                   jax.ShapeDtypeStruct((nb, N_GROUPS_B, LANES, 512), F32)],
        scratch_shapes=[pltpu.VMEM((LANES, 512), F32)],
        compiler_params=_params(("arbitrary", "arbitrary")),
    )(xc, xc, xc, proj, proj, dtb, alog, dsk, nw)


def ssd_backward(xc, proj, states, d_out, dtb, alog, dsk, nw, d_proj):
    t = proj.shape[0]
    tb = _time_block(t, SSD_TIME_BLOCK)
    nb = t // tb
    rev = lambda j: nb - 1 - j

    def body(x_ref, b_ref, c_ref, dt_ref, z_ref, st_ref, do_ref, dtb_ref, alog_ref, dsk_ref, nw_ref, _,
             dx_ref, db_ref, dc_ref, ddt_ref, dz_ref, ddtb_ref, dalog_ref, ddsk_ref, dnw_ref, d_state):
        accs = (ddtb_ref, dalog_ref, ddsk_ref, dnw_ref)

        @pl.when(pl.program_id(1) == 0)
        def _():
            d_state[...] = jnp.zeros_like(d_state)
            for ref in accs:
                ref[...] = jnp.zeros_like(ref)

        cs = ssd_consts(pl.program_id(0))
        fn = lambda *a: ssd_block(*a, cs)
        _, vjp = jax.vjp(fn, x_ref[...], b_ref[...], c_ref[...], dt_ref[...], z_ref[...], st_ref[...],
                         dtb_ref[...], alog_ref[...], dsk_ref[...], nw_ref[...])
        dx, db, dc, ddt, dz, dst, *dpar = vjp((do_ref[...], d_state[...]))
        dx_ref[...] = dx
        db_ref[...] = db
        dc_ref[...] = dc
        ddt_ref[...] = ddt
        dz_ref[...] = dz.astype(dz_ref.dtype)
        d_state[...] = dst
        for ref, val in zip(accs, dpar, strict=True):
            ref[...] += val

    vec = pl.BlockSpec((1, 512), lambda g, j: (0, g))
    heads = pl.BlockSpec((1, LANES), lambda g, j: (0, 0))
    acc = pl.BlockSpec((None, 1, 512), lambda g, j: (g, 0, 0))
    acc_heads = pl.BlockSpec((None, 1, LANES), lambda g, j: (g, 0, 0))
    return pl.pallas_call(
        body, name="ssd_backward", grid=(N_GROUPS_B, nb),
        in_specs=_ssd_in_specs(tb, rev)
        + [pl.BlockSpec((None, None, LANES, 512), lambda g, j: (rev(j), g, 0, 0)),
           pl.BlockSpec((tb, 512), lambda g, j: (rev(j), g))] + [heads, heads, vec, vec] + [pl.BlockSpec(memory_space=pl.ANY)],
        out_specs=[pl.BlockSpec((tb, 512), lambda g, j: (rev(j), g)),
                   pl.BlockSpec((tb, LANES), lambda g, j: (rev(j), g)),
                   pl.BlockSpec((tb, LANES), lambda g, j: (rev(j), g)),
                   pl.BlockSpec((None, tb, LANES), lambda g, j: (g, rev(j), 0)),
                   pl.BlockSpec((tb, 512), lambda g, j: (rev(j), COL_Z // 512 + g)), acc_heads, acc_heads, acc, acc],
        out_shape=[jax.ShapeDtypeStruct((t, B_INNER), F32), jax.ShapeDtypeStruct((t, 512), F32),
                   jax.ShapeDtypeStruct((t, 512), F32), jax.ShapeDtypeStruct((N_GROUPS_B, t, LANES), F32),
                   jax.ShapeDtypeStruct(d_proj.shape, d_proj.dtype)]
        + [jax.ShapeDtypeStruct((N_GROUPS_B, 1, LANES), F32)] * 2 + [jax.ShapeDtypeStruct((N_GROUPS_B, 1, 512), F32)] * 2,
        input_output_aliases={11: 4},
        scratch_shapes=[pltpu.VMEM((LANES, 512), F32)],
        compiler_params=_params(("arbitrary", "arbitrary")),
    )(xc, xc, xc, proj, proj, states, d_out, dtb, alog, dsk, nw, d_proj)


CONV_HALO = 8


def _shift_down(halo_then_tile, s, tm):
    if s == 0:
        return halo_then_tile[CONV_HALO:CONV_HALO + tm]
    return pltpu.roll(halo_then_tile, s, 0)[CONV_HALO:CONV_HALO + tm]


def _conv_pre(cur, prev, w, b, tm):
    stacked = jnp.concatenate([prev, cur], axis=0)
    taps = [_shift_down(stacked, 3 - j, tm) for j in range(4)]
    pre = b + taps[0] * w[0:1] + taps[1] * w[1:2] + taps[2] * w[2:3] + taps[3] * w[3:4]
    return pre, taps


def _conv_specs(t, tm):
    per = tm // CONV_HALO
    cur = pl.BlockSpec((tm, CONV_DIM), lambda i: (i, COL_XBC // CONV_DIM))
    prev = pl.BlockSpec((CONV_HALO, CONV_DIM), lambda i: (jnp.maximum(i * per - 1, 0), COL_XBC // CONV_DIM))
    return cur, prev


def conv_forward(proj, w, b):
    t = proj.shape[0]
    tm = _pick(t, (256, 128, 64))

    def body(cur_ref, prev_ref, w_ref, b_ref, o_ref):
        prev = jnp.where(pl.program_id(0) == 0, 0.0, prev_ref[...])
        pre, _ = _conv_pre(cur_ref[...], prev, w_ref[...], b_ref[...], tm)
        o_ref[...] = silu(pre)

    cur, prev = _conv_specs(t, tm)
    return pl.pallas_call(
        body, name="conv_forward", grid=(t // tm,),
        in_specs=[cur, prev, pl.BlockSpec((4, CONV_DIM), lambda i: (0, 0)), pl.BlockSpec((1, CONV_DIM), lambda i: (0, 0))],
        out_specs=pl.BlockSpec((tm, CONV_DIM), lambda i: (i, 0)),
        out_shape=jax.ShapeDtypeStruct((t, CONV_DIM), F32),
        compiler_params=_params(("arbitrary",)),
    )(proj, proj, w, b)


def conv_backward(proj, dx, db_, dc_, w, b, d_proj):
    t = proj.shape[0]
    tm = _pick(t, (256, 128, 64))
    per = tm // CONV_HALO
    nt = t // tm
    rev = lambda i: nt - 1 - i

    def body(cur_ref, prev_ref, dx_ref, dbm_ref, dcm_ref, w_ref, b_ref, _, o_ref, dw_ref, dbias_ref, later):
        @pl.when(pl.program_id(0) == 0)
        def _():
            dw_ref[...] = jnp.zeros_like(dw_ref)
            dbias_ref[...] = jnp.zeros_like(dbias_ref)
            later[...] = jnp.zeros_like(later)

        first_tile = pl.program_id(0) == nt - 1
        for lo, hi, src in ((0, B_INNER, dx_ref), (B_INNER, B_INNER + 512, dbm_ref), (B_INNER + 512, CONV_DIM, dcm_ref)):
            cols = slice(lo, hi)
            prev = jnp.where(first_tile, 0.0, prev_ref[:, cols])
            w_ = w_ref[:, cols]
            pre, taps = _conv_pre(cur_ref[:, cols], prev, w_, b_ref[:, cols], tm)
            sg = sigmoid(pre)
            dpre = src[...] * (sg * (1.0 + pre * (1.0 - sg)))
            dbias_ref[:, cols] += jnp.sum(dpre, axis=0, keepdims=True)
            for j in range(4):
                dw_ref[j:j + 1, cols] += jnp.sum(dpre * taps[j], axis=0, keepdims=True)
            stacked = jnp.concatenate([dpre, later[:, cols]], axis=0)
            acc = dpre * w_[3:4]
            for j in range(3):
                acc = acc + pltpu.roll(stacked, tm + CONV_HALO - (3 - j), 0)[0:tm] * w_[j:j + 1]
            o_ref[:, cols] = acc.astype(o_ref.dtype)
            later[:, cols] = dpre[0:CONV_HALO]

    row = lambda w_: pl.BlockSpec((tm, w_), lambda i: (rev(i), 0))
    whole = lambda r: pl.BlockSpec((r, CONV_DIM), lambda i: (0, 0))
    return pl.pallas_call(
        body, name="conv_backward", grid=(nt,),
        in_specs=[pl.BlockSpec((tm, CONV_DIM), lambda i: (rev(i), COL_XBC // CONV_DIM)),
                  pl.BlockSpec((CONV_HALO, CONV_DIM), lambda i: (jnp.maximum(rev(i) * per - 1, 0), COL_XBC // CONV_DIM)),
                  row(B_INNER), row(512), row(512), whole(4), whole(1), pl.BlockSpec(memory_space=pl.ANY)],
        out_specs=[pl.BlockSpec((tm, CONV_DIM), lambda i: (rev(i), COL_XBC // CONV_DIM)), whole(4), whole(1)],
        out_shape=[jax.ShapeDtypeStruct(d_proj.shape, d_proj.dtype), jax.ShapeDtypeStruct((4, CONV_DIM), F32),
                   jax.ShapeDtypeStruct((1, CONV_DIM), F32)],
        input_output_aliases={7: 0},
        scratch_shapes=[pltpu.VMEM((CONV_HALO, CONV_DIM), F32)],
        compiler_params=_params(("arbitrary",)),
    )(proj, proj, dx, db_, dc_, w, b, d_proj)


def stage_modulate(x, sc, sh):
    return _ln(x) * (1.0 + sc) + sh


def stage_merge(ga, gb, ya, yb):
    return sigmoid(ga) * ya + sigmoid(gb) * yb


def stage_post_mixer(x, h, g1, ln_g, ln_b, sc2, sh2):
    x1 = _ln(ALPHA * x + g1 * h) * ln_g + ln_b
    return x1, _ln(x1) * (1.0 + sc2) + sh2


def stage_swiglu(a, b):
    return silu(a) * b


def gate_up(ab):
    w = FFN_SHARD
    return (jnp.concatenate([ab[:, 2 * w * k:2 * w * k + w] for k in range(4)], axis=1),
            jnp.concatenate([ab[:, 2 * w * k + w:2 * w * (k + 1)] for k in range(4)], axis=1))


def per_chip(gate, up):
    w = FFN_SHARD
    return jnp.concatenate([part[:, w * k:w * (k + 1)] for k in range(4) for part in (gate, up)], axis=1)


def stage_loss(x1, hf, tgt, g2, ln_g, ln_b):
    x2 = _ln(ALPHA * x1 + g2 * hf) * ln_g + ln_b
    return 0.5 * jnp.sum(jnp.mean(jnp.square(x2 - tgt), axis=-1, keepdims=True), axis=0, keepdims=True)


def local_step(x, tgt, mod, wts, small, early=None, mid=None, late=None, last=None):
    sh1, sc1, g1, sh2, sc2, g2 = mod
    lb, gn, conv_w, conv_b, dtb, alog, dsk, nw, ln1_g, ln1_b, ln2_g, ln2_b = small
    vec = (1, D)

    (u1,) = rowwise("modulate1", lambda r, c: ((stage_modulate(r[0], *c),), ()), [_full(x)], [sc1, sh1], [(D, BF16)])
    w_in = wts.input_projection(u1)
    proj = matmul(u1, w_in, "nn", F32, "in_proj")
    ya_in, st_a = hgrn_forward(proj, lb, gn + wts.start_rest(proj)[0:1])
    xc = conv_forward(proj, conv_w, conv_b)
    w_a, w_b, w_o, w_gu, w_d = wts.rest(xc)
    yb_in, st_b = ssd_forward(xc, proj, dtb, alog, dsk, nw)
    ya = matmul(ya_in, w_a, "nn", F32, "branch_a")
    yb = matmul(yb_in, w_b, "nn", F32, "branch_b")
    gate_rows = [(proj, D, COL_GA // D), (proj, D, COL_GB // D), _full(ya), _full(yb)]
    (merged,) = rowwise("merge", lambda r, c: ((stage_merge(*r),), ()), gate_rows, [], [(D, BF16)])
    h = matmul(merged, w_o, "nn", F32, "out_proj")
    post_consts = [g1, ln1_g, ln1_b, sc2, sh2]
    x1, u2 = rowwise("post_mixer", lambda r, c: (stage_post_mixer(*r, *c), ()), [_full(x), _full(h)], post_consts,
                     [(D, F32), (D, BF16)])
    ab = matmul(u2, w_gu, "nt", F32, "ffn_in")
    (p,) = rowwise("swiglu", lambda r, c: ((stage_swiglu(*gate_up(r[0])),), ()), [_full(ab)], [], [(D_FF, BF16)])
    hf = matmul(p, w_d, "nn", F32, "ffn_out")

    def loss_bwd(r, c):
        loss, vjp = jax.vjp(stage_loss, *r, *c)
        dx1, dhf, _, dg2, dlg, dlb_ = vjp(jnp.ones((1, 1), F32))
        return (dx1, dhf), (loss, dg2, dlg, dlb_)

    dx1, dhf, loss, dg2, dln2_g, dln2_b = rowwise(
        "loss_backward", loss_bwd, [_full(x1), _full(hf), _full(tgt)], [g2, ln2_g, ln2_b],
        [(D, F32), (D, BF16)], [(1, 1), vec, vec, vec])
    dp = matmul(dhf, w_d, "nt", F32, "ffn_out_dx")
    dw_d = matmul(p, dhf, "tn", F32, "ffn_out_dw")

    def swiglu_bwd(r, c):
        _, vjp = jax.vjp(stage_swiglu, *gate_up(r[0]))
        return (per_chip(*vjp(r[1])),), ()

    (dab,) = rowwise("swiglu_backward", swiglu_bwd, [_full(ab), _full(dp)], [], [(2 * D_FF, BF16)])
    du2 = matmul(dab, w_gu, "nn", F32, "ffn_in_dx")
    dw_gu = matmul(dab, u2, "tn", F32, "ffn_in_dw")

    def post_bwd(r, c):
        _, vjp = jax.vjp(stage_post_mixer, r[0], r[1], *c)
        dx, dh, *dc = vjp((r[2], r[3]))
        return (dx, dh), tuple(dc)

    dx_a, dh, dg1, dln1_g, dln1_b, dsc2, dsh2 = rowwise(
        "post_mixer_backward", post_bwd, [_full(x), _full(h), _full(dx1), _full(du2)], post_consts,
        [(D, F32), (D, BF16)], [vec] * 5)
    dmerged = matmul(dh, w_o, "nt", F32, "out_proj_dx")
    dw_o = matmul(merged, dh, "tn", F32, "out_proj_dw")

    def merge_bwd(r, c):
        _, vjp = jax.vjp(stage_merge, *r[:4])
        dga, dgb, dya, dyb = vjp(r[4])
        return (jnp.concatenate([dga, dgb], axis=1), dya, dyb), ()

    dproj, dya, dyb = rowwise("merge_backward", merge_bwd, gate_rows + [_full(dmerged)], [],
                              [(2 * D, BF16), (D, BF16), (D, BF16)], new_wide=(IN_PAD, COL_GA // (2 * D)))
    dya_in = matmul(dya, w_a, "nt", F32, "branch_a_dx")
    dw_a = matmul(ya_in, dya, "tn", F32, "branch_a_dw")
    dyb_in = matmul(dyb, w_b, "nt", F32, "branch_b_dx")
    dw_b = matmul(yb_in, dyb, "tn", F32, "branch_b_dw")
    gn_after = gn if early is None else gn + early((dw_a, dw_b, dw_o, dw_gu, dw_d))[0:1]
    dproj, dlb, dgn = hgrn_backward(proj, st_a, dya_in, lb, gn_after, dproj)
    dtb_after = dtb if mid is None else dtb + mid(dlb)[0:1, 0:1]
    dxs, dbm, dcm, ddt, dproj, ddtb, dalog, ddsk, dnw = ssd_backward(xc, proj, st_b, dyb_in, dtb_after, alog, dsk, nw, dproj)
    dproj, dconv_w, dconv_b = conv_backward(proj, dxs, dbm, dcm, conv_w, conv_b, dproj)
    if late is not None:
        late(dconv_b)
    t = x.shape[0]
    tail = jnp.concatenate([jnp.sum(ddt, axis=0).astype(BF16), jnp.zeros((t, IN_PAD - COL_DT - LANES), BF16)], axis=1)
    dproj = lax.dynamic_update_slice(dproj, tail, (0, COL_DT))
    dw_in = matmul(u1, dproj, "tn", F32, "in_proj_dw")
    du1 = matmul(dproj, w_in, "nt", F32, "in_proj_dx", after=None if last is None else last(dw_in))

    def mod_bwd(r, c):
        _, vjp = jax.vjp(stage_modulate, r[0], *c)
        dx, dsc, dsh = vjp(r[1])
        return (dx + r[2],), (dsc, dsh)

    grad_x, dsc1, dsh1 = rowwise("modulate1_backward", mod_bwd, [_full(x), _full(du1), _full(dx_a)], [sc1, sh1],
                                 [(D, F32)], [vec, vec])
    d_mod = (dsh1, dsc1, dg1, dsh2, dsc2, dg2)
    d_wts = (dw_in, dw_a, dw_b, dw_o, dw_gu, dw_d)
    d_small = (dlb, dgn, dconv_w, dconv_b, jnp.sum(ddtb, axis=0),
               jnp.sum(dalog, axis=0), ddsk.reshape(1, B_INNER), dnw.reshape(1, B_INNER),
               dln1_g, dln1_b, dln2_g, dln2_b)
    return loss, grad_x, d_mod, d_wts, d_small


HBM = pl.BlockSpec(memory_space=pltpu.HBM)
SEM = pl.BlockSpec(memory_space=pltpu.SEMAPHORE)
DATAFLOW = pltpu.SideEffectType.DATAFLOW_SIDE_EFFECTING


def _place():
    return lax.axis_index("x"), lax.axis_index("y"), lax.axis_index("c")


def _other_chips(x, y):
    return [(1 - x, y), (x, 1 - y), (1 - x, 1 - y)]


def _remote(src, dst, send_sem, recv_sem, device):
    return pltpu.make_async_remote_copy(src_ref=src, dst_ref=dst, send_sem=send_sem, recv_sem=recv_sem,
                                        device_id=device, device_id_type=MESH)


def gather_rows(v, name):
    n = v.shape[1]

    def body(v_ref, out_ref, send_sems, recv_sems, local_sem):
        x, y, c = _place()
        mine = pltpu.make_async_copy(v_ref, out_ref.at[4 * x + 2 * y + c], local_sem)
        mine.start()
        sends, recvs = [], []
        for m in range(1, 8):
            px = 1 - x if m & 4 else x
            py = 1 - y if m & 2 else y
            pc = 1 - c if m & 1 else c
            sends.append(_remote(v_ref, out_ref.at[4 * x + 2 * y + c], send_sems.at[m - 1], recv_sems.at[m - 1], (px, py, pc)))
            recvs.append(_remote(v_ref, out_ref.at[4 * px + 2 * py + pc], send_sems.at[m - 1], recv_sems.at[m - 1], (px, py, pc)))
        for cp in sends:
            cp.start()
        for cp in recvs:
            cp.wait_recv()
        for cp in sends:
            cp.wait_send()
        mine.wait()

    return pl.pallas_call(
        body, name=name, in_specs=[HBM], out_specs=HBM,
        out_shape=jax.ShapeDtypeStruct((8, 1, n), v.dtype),
        scratch_shapes=[pltpu.SemaphoreType.DMA((7,)), pltpu.SemaphoreType.DMA((7,)), pltpu.SemaphoreType.DMA],
    )(v)


def exchange_rows(part, name):
    w = part.shape[2]

    def body(p_ref, out_ref, send_sems, recv_sems, local_sem):
        x, y, c = _place()
        k = 2 * x + y
        mine = pltpu.make_async_copy(p_ref.at[4 * x + 2 * y + c], out_ref.at[k], local_sem)
        mine.start()
        sends, recvs = [], []
        for j, (px, py) in enumerate(_other_chips(x, y)):
            sends.append(_remote(p_ref.at[4 * px + 2 * py + c], out_ref.at[k], send_sems.at[j], recv_sems.at[j], (px, py, c)))
            recvs.append(_remote(p_ref.at[4 * px + 2 * py + c], out_ref.at[2 * px + py], send_sems.at[j], recv_sems.at[j], (px, py, c)))
        for cp in sends:
            cp.start()
        for cp in recvs:
            cp.wait_recv()
        for cp in sends:
            cp.wait_send()
        mine.wait()

    return pl.pallas_call(
        body, name=name, in_specs=[HBM], out_specs=HBM,
        out_shape=jax.ShapeDtypeStruct((4, 1, w), part.dtype),
        scratch_shapes=[pltpu.SemaphoreType.DMA((3,)), pltpu.SemaphoreType.DMA((3,)), pltpu.SemaphoreType.DMA],
    )(part)


def _half_of_slot(ref, rows, px, py, pc):
    return ref.at[2 * px + py, pl.ds(pc * (rows // 2), rows // 2), :]


def gather_start(shards, after, tag):
    n = len(shards)

    def body(*refs):
        w_refs, land_refs = refs[:n], refs[n:2 * n]
        send_sems, recv_sems = refs[2 * n + 1], refs[2 * n + 2]
        token = refs[-1]
        x, y, c = _place()
        for i in range(n):
            rows = shards[i].shape[0]
            for j, (px, py) in enumerate(_other_chips(x, y)):
                _remote(w_refs[i].at[pl.ds(c * (rows // 2), rows // 2), :], _half_of_slot(land_refs[i], rows, x, y, c),
                        send_sems.at[j * n + i], recv_sems.at[j * n + i], (px, py, c)).start()
        token[...] = jnp.zeros_like(token)

    hbm = lambda a: pltpu.with_memory_space_constraint(a, pltpu.HBM)
    lands = [lax.empty((4,) + s.shape, s.dtype) for s in shards]
    dma = pltpu.SemaphoreType.DMA
    return pl.pallas_call(
        body, name="gather_start_" + tag,
        out_shape=(dma((3 * n,)), dma((3 * n,)),
                   *[pltpu.HBM(a.shape, a.dtype) for a in list(shards) + lands], jax.ShapeDtypeStruct((8, LANES), F32)),
        in_specs=[HBM] * (2 * n) + [pl.BlockSpec(memory_space=pl.ANY)],
        out_specs=(SEM, SEM, *[HBM] * (2 * n), pl.BlockSpec(memory_space=pltpu.VMEM)),
        input_output_aliases={i: 2 + i for i in range(2 * n)},
        compiler_params=pltpu.CompilerParams(has_side_effects=DATAFLOW),
    )(*[hbm(a) for a in list(shards) + lands], after)


def gather_wait(send_sems, recv_sems, shards, lands, after, tag):
    n = len(shards)

    def body(*refs):
        w_refs, land_refs = refs[:n], refs[n:2 * n]
        send_ref, recv_ref = refs[2 * n], refs[2 * n + 1]
        x, y, c = _place()
        for i in range(n):
            rows = shards[i].shape[0]
            for j, (px, py) in enumerate(_other_chips(x, y)):
                cp = _remote(w_refs[i].at[pl.ds(c * (rows // 2), rows // 2), :], _half_of_slot(land_refs[i], rows, px, py, c),
                             send_ref.at[j * n + i], recv_ref.at[j * n + i], (px, py, c))
                cp.wait_send()
                cp.wait_recv()

    out = pl.pallas_call(
        body, name="gather_wait_" + tag,
        out_shape=tuple(pltpu.HBM(a.shape, a.dtype) for a in list(shards) + list(lands)),
        in_specs=[HBM] * (2 * n) + [SEM, SEM, pl.BlockSpec(memory_space=pl.ANY)], out_specs=tuple([HBM] * (2 * n)),
        input_output_aliases={i: i for i in range(2 * n)},
        compiler_params=pltpu.CompilerParams(has_side_effects=DATAFLOW),
    )(*shards, *lands, send_sems, recv_sems, after)
    return list(out[:n]), list(out[n:])


def forward_start(lands, tag):
    n = len(lands)

    def body(*refs):
        land_refs = refs[:n]
        send_sems, recv_sems = refs[n], refs[n + 1]
        token = refs[-1]
        x, y, c = _place()
        for i in range(n):
            rows = lands[i].shape[1]
            for j, (px, py) in enumerate(_other_chips(x, y)):
                mine = _half_of_slot(land_refs[i], rows, px, py, c)
                _remote(mine, mine, send_sems.at[j * n + i], recv_sems.at[j * n + i], (x, y, 1 - c)).start()
        token[...] = jnp.zeros_like(token)

    dma = pltpu.SemaphoreType.DMA
    return pl.pallas_call(
        body, name="forward_start_" + tag,
        out_shape=(dma((3 * n,)), dma((3 * n,)), *[pltpu.HBM(a.shape, a.dtype) for a in lands],
                   jax.ShapeDtypeStruct((8, LANES), F32)),
        in_specs=[HBM] * n, out_specs=(SEM, SEM, *[HBM] * n, pl.BlockSpec(memory_space=pltpu.VMEM)),
        input_output_aliases={i: 2 + i for i in range(n)},
        compiler_params=pltpu.CompilerParams(has_side_effects=DATAFLOW),
    )(*lands)


def forward_wait(started, after, tag):
    send_sems, recv_sems, *rest = started
    lands = rest[:-1]
    n = len(lands)

    def body(*refs):
        land_refs = refs[:n]
        send_ref, recv_ref = refs[n], refs[n + 1]
        x, y, c = _place()
        for i in range(n):
            rows = lands[i].shape[1]
            for j, (px, py) in enumerate(_other_chips(x, y)):
                cp = _remote(_half_of_slot(land_refs[i], rows, px, py, c), _half_of_slot(land_refs[i], rows, px, py, 1 - c),
                             send_ref.at[j * n + i], recv_ref.at[j * n + i], (x, y, 1 - c))
                cp.wait_send()
                cp.wait_recv()

    out = pl.pallas_call(
        body, name="forward_wait_" + tag,
        out_shape=tuple(pltpu.HBM(a.shape, a.dtype) for a in lands),
        in_specs=[HBM] * n + [SEM, SEM, pl.BlockSpec(memory_space=pl.ANY)], out_specs=tuple([HBM] * n),
        input_output_aliases={i: i for i in range(n)},
        compiler_params=pltpu.CompilerParams(has_side_effects=DATAFLOW),
    )(*lands, send_sems, recv_sems, after)
    return list(out)


def pair_start(slabs, tag):
    n = len(slabs)

    def body(*refs):
        g_refs, land_refs = refs[:n], refs[n:2 * n]
        send_sems, recv_sems = refs[2 * n], refs[2 * n + 1]
        token = refs[-1]
        x, y, c = _place()
        for i in range(n):
            hr = slabs[i].shape[1] // 2
            _remote(g_refs[i].at[:, pl.ds((1 - c) * hr, hr), :], land_refs[i], send_sems.at[i], recv_sems.at[i],
                    (x, y, 1 - c)).start()
        token[...] = jnp.zeros_like(token)

    hbm = lambda a: pltpu.with_memory_space_constraint(a, pltpu.HBM)
    lands = [lax.empty((4, s.shape[1] // 2, s.shape[2]), s.dtype) for s in slabs]
    dma = pltpu.SemaphoreType.DMA
    return pl.pallas_call(
        body, name="pair_start_" + tag,
        out_shape=(dma((n,)), dma((n,)), *[pltpu.HBM(a.shape, a.dtype) for a in list(slabs) + lands],
                   jax.ShapeDtypeStruct((8, LANES), F32)),
        in_specs=[HBM] * (2 * n), out_specs=(SEM, SEM, *[HBM] * (2 * n), pl.BlockSpec(memory_space=pltpu.VMEM)),
        input_output_aliases={i: 2 + i for i in range(2 * n)},
        compiler_params=pltpu.CompilerParams(has_side_effects=DATAFLOW),
    )(*[hbm(a) for a in list(slabs) + lands])


def pair_wait(started, after, tag):
    send_sems, recv_sems, *rest = started
    n = (len(rest) - 1) // 2
    slabs, lands = rest[:n], rest[n:2 * n]

    def body(*refs):
        g_refs, land_refs = refs[:n], refs[n:2 * n]
        send_ref, recv_ref = refs[2 * n], refs[2 * n + 1]
        x, y, c = _place()
        for i in range(n):
            hr = slabs[i].shape[1] // 2
            cp = _remote(g_refs[i].at[:, pl.ds((1 - c) * hr, hr), :], land_refs[i], send_ref.at[i], recv_ref.at[i], (x, y, 1 - c))
            cp.wait_send()
            cp.wait_recv()

    out = pl.pallas_call(
        body, name="pair_wait_" + tag,
        out_shape=tuple(pltpu.HBM(a.shape, a.dtype) for a in list(slabs) + list(lands)),
        in_specs=[HBM] * (2 * n) + [SEM, SEM, pl.BlockSpec(memory_space=pl.ANY)], out_specs=tuple([HBM] * (2 * n)),
        input_output_aliases={i: i for i in range(2 * n)},
        compiler_params=pltpu.CompilerParams(has_side_effects=DATAFLOW),
    )(*slabs, *lands, send_sems, recv_sems, after)
    return list(out[:n]), list(out[n:])


def _tile2(rows, cols):
    fits = lambda r, c: r * c * 4 <= BLOCK_BYTES
    if fits(rows, cols):
        return rows, cols
    for r in (1024, 512, 256, 128, 64):
        if rows % r == 0 and fits(r, cols):
            return r, cols
    return rows, next(cols // k for k in (2, 3, 4, 6, 8, 12, 16) if cols % (k * LANES) == 0 and fits(rows, cols // k))


def pair_add(g, p, c, name):
    _, hr, cols = p.shape
    tm, tc = _tile2(hr, cols)
    per = hr // tm

    def body(c_ref, g_ref, p_ref, o_ref):
        o_ref[...] = (g_ref[...] + p_ref[...]).astype(o_ref.dtype)

    return pl.pallas_call(
        body, name=name,
        grid_spec=pltpu.PrefetchScalarGridSpec(
            num_scalar_prefetch=1, grid=(4, per, cols // tc),
            in_specs=[pl.BlockSpec((None, tm, tc), lambda k, i, j, c_ref: (k, c_ref[0] * per + i, j)),
                      pl.BlockSpec((None, tm, tc), lambda k, i, j, c_ref: (k, i, j))],
            out_specs=pl.BlockSpec((None, tm, tc), lambda k, i, j, c_ref: (k, i, j))),
        out_shape=jax.ShapeDtypeStruct((4, hr, cols), BF16),
        compiler_params=_params(("arbitrary", "arbitrary", "arbitrary")),
    )(c.reshape(1).astype(jnp.int32), g, p)


def scatter_start(sums, tag):
    n = len(sums)

    def body(*refs):
        s_refs, land_refs = refs[:n], refs[n:2 * n]
        send_sems, recv_sems = refs[2 * n], refs[2 * n + 1]
        token = refs[-1]
        x, y, c = _place()
        k = 2 * x + y
        for i in range(n):
            for j, (px, py) in enumerate(_other_chips(x, y)):
                _remote(s_refs[i].at[2 * px + py], land_refs[i].at[k], send_sems.at[j * n + i], recv_sems.at[j * n + i],
                        (px, py, c)).start()
        token[...] = jnp.zeros_like(token)

    hbm = lambda a: pltpu.with_memory_space_constraint(a, pltpu.HBM)
    return pl.pallas_call(
        body, name="scatter_start_" + tag,
        out_shape=(pltpu.SemaphoreType.DMA((3 * n,)), pltpu.SemaphoreType.DMA((3 * n,)),
                   *[pltpu.HBM(s.shape, s.dtype) for s in sums], *[pltpu.HBM(s.shape, s.dtype) for s in sums],
                   jax.ShapeDtypeStruct((8, LANES), F32)),
        in_specs=[HBM] * (2 * n), out_specs=(SEM, SEM, *[HBM] * (2 * n), pl.BlockSpec(memory_space=pltpu.VMEM)),
        input_output_aliases={i: 2 + i for i in range(2 * n)},
        compiler_params=pltpu.CompilerParams(has_side_effects=DATAFLOW),
    )(*[hbm(s) for s in sums], *[hbm(lax.empty(s.shape, s.dtype)) for s in sums])


def scatter_wait(started, after, tag):
    send_sems, recv_sems, *rest = started
    n = (len(rest) - 1) // 2
    sums, lands = rest[:n], rest[n:2 * n]

    def body(*refs):
        s_refs, land_refs = refs[:n], refs[n:2 * n]
        send_ref, recv_ref = refs[2 * n], refs[2 * n + 1]
        x, y, c = _place()
        for i in range(n):
            for j, (px, py) in enumerate(_other_chips(x, y)):
                cp = _remote(s_refs[i].at[2 * px + py], land_refs[i].at[2 * px + py], send_ref.at[j * n + i],
                             recv_ref.at[j * n + i], (px, py, c))
                cp.wait_send()
                cp.wait_recv()

    out = pl.pallas_call(
        body, name="scatter_wait_" + tag,
        out_shape=tuple(pltpu.HBM(s.shape, s.dtype) for s in sums + lands),
        in_specs=[HBM] * (2 * n) + [SEM, SEM, pl.BlockSpec(memory_space=pl.ANY)], out_specs=tuple([HBM] * (2 * n)),
        input_output_aliases={i: i for i in range(2 * n)},
        compiler_params=pltpu.CompilerParams(has_side_effects=DATAFLOW),
    )(*sums, *lands, send_sems, recv_sems, after)
    return list(out[:n]), list(out[n:])


def sum_chips(landed, own, chip, core, name):
    _, hr, cols = landed.shape
    tm, tc = _tile2(hr, cols)
    per = hr // tm

    def body(idx_ref, l0, l1, l2, l3, own_ref, o_ref):
        mine = own_ref[...].astype(F32)
        v = [jnp.where(idx_ref[0] == k, mine, ref[...].astype(F32)) for k, ref in enumerate((l0, l1, l2, l3))]
        o_ref[...] = ((v[0] + v[1]) + v[2]) + v[3]

    slot = lambda k: pl.BlockSpec((None, tm, tc),
                                  lambda i, j, idx: (jnp.where(idx[0] == k, (k + 1) & 3, k), i, j))
    return pl.pallas_call(
        body, name=name,
        grid_spec=pltpu.PrefetchScalarGridSpec(
            num_scalar_prefetch=1, grid=(per, cols // tc),
            in_specs=[slot(0), slot(1), slot(2), slot(3),
                      pl.BlockSpec((None, tm, tc), lambda i, j, idx: (idx[0], i, j))],
            out_specs=pl.BlockSpec((tm, tc), lambda i, j, idx: (idx[1] * per + i, j))),
        out_shape=jax.ShapeDtypeStruct((2 * hr, cols), F32),
        compiler_params=_params(("arbitrary", "arbitrary")),
    )(jnp.stack([chip, core]).astype(jnp.int32), landed, landed, landed, landed, own)


def exchange_halves(bufs):
    n = len(bufs)

    def body(*refs):
        out_refs = refs[n:2 * n]
        send_sems, recv_sems = refs[2 * n:]
        x, y, c = _place()
        sends, recvs = [], []
        for i in range(n):
            hr = bufs[i].shape[0] // 2
            own = out_refs[i].at[pl.ds(c * hr, hr), :]
            other = out_refs[i].at[pl.ds((1 - c) * hr, hr), :]
            sends.append(_remote(own, own, send_sems.at[i], recv_sems.at[i], (x, y, 1 - c)))
            recvs.append(_remote(other, other, send_sems.at[i], recv_sems.at[i], (x, y, 1 - c)))
        for cp in sends:
            cp.start()
        for cp in recvs:
            cp.wait_recv()
        for cp in sends:
            cp.wait_send()

    return pl.pallas_call(
        body, name="exchange_halves", in_specs=[HBM] * n, out_specs=[HBM] * n,
        out_shape=[jax.ShapeDtypeStruct(b.shape, b.dtype) for b in bufs],
        input_output_aliases={i: i for i in range(n)},
        scratch_shapes=[pltpu.SemaphoreType.DMA((n,)), pltpu.SemaphoreType.DMA((n,))],
    )(*bufs)


def _relayout(name, arrays, in_blocks, out_blocks, out_shapes, fn):
    rows = 128
    spec = lambda blk: pl.BlockSpec(blk, (lambda i: (0, i, 0)) if len(blk) == 3 else (lambda i: (i, 0)))

    def body(*refs):
        n_in = len(arrays)
        outs = fn(*[r[...] for r in refs[:n_in]])
        for ref, val in zip(refs[n_in:], outs, strict=True):
            if isinstance(val, list):
                for k, piece in enumerate(val):
                    ref[k] = piece
            else:
                ref[...] = val

    return pl.pallas_call(
        body, name=name, grid=(D // rows,),
        in_specs=[spec(b) for b in in_blocks], out_specs=[spec(b) for b in out_blocks], out_shape=out_shapes,
        compiler_params=_params(("arbitrary",)),
    )(*arrays)


def assemble_in_proj(g):
    def fn(v):
        w = jnp.concatenate([v[k] for k in range(4)], axis=1)
        return (jnp.concatenate([w[:, :ORIG_Z], w[:, ORIG_GA:], w[:, ORIG_XBC:ORIG_DT], w[:, ORIG_Z:ORIG_XBC],
                                 w[:, ORIG_DT:ORIG_GA], jnp.zeros((w.shape[0], IN_PAD - IN_ORIG), w.dtype)], axis=1),)

    cols = g.shape[2]
    return _relayout("assemble_in_proj", [g], [(4, 128, cols)], [(128, IN_PAD)],
                     [jax.ShapeDtypeStruct((D, IN_PAD), g.dtype)], fn)[0]


def rows_exchange(a, name):
    hr = a.shape[0] // 2

    def body(a_ref, out_ref, send_sem, recv_sem):
        x, y, c = _place()
        cp = _remote(a_ref.at[pl.ds((1 - c) * hr, hr), :], out_ref, send_sem, recv_sem, (x, y, 1 - c))
        cp.start()
        cp.wait()

    return pl.pallas_call(
        body, name=name, in_specs=[HBM], out_specs=HBM,
        out_shape=jax.ShapeDtypeStruct((hr, a.shape[1]), a.dtype),
        scratch_shapes=[pltpu.SemaphoreType.DMA, pltpu.SemaphoreType.DMA],
    )(a)


def split_pair_add(dw, received, core):
    cols = IN_ORIG // 4
    rows, hr = 128, D // 2
    per = hr // rows

    def body(c_ref, own_ref, got_ref, o_ref):
        d = own_ref[...] + got_ref[...]
        w = jnp.concatenate([d[:, :COL_GA], d[:, COL_Z:COL_DT], d[:, COL_XBC:COL_Z], d[:, COL_DT:COL_DT + 32],
                             d[:, COL_GA:COL_XBC]], axis=1)
        for k in range(4):
            o_ref[k] = w[:, k * cols:(k + 1) * cols].astype(o_ref.dtype)

    return pl.pallas_call(
        body, name="split_pair_add",
        grid_spec=pltpu.PrefetchScalarGridSpec(
            num_scalar_prefetch=1, grid=(per,),
            in_specs=[pl.BlockSpec((rows, IN_PAD), lambda i, c_ref: (c_ref[0] * per + i, 0)),
                      pl.BlockSpec((rows, IN_PAD), lambda i, c_ref: (i, 0))],
            out_specs=pl.BlockSpec((4, rows, cols), lambda i, c_ref: (0, i, 0))),
        out_shape=jax.ShapeDtypeStruct((4, hr, cols), BF16),
        compiler_params=_params(("arbitrary",)),
    )(core.reshape(1).astype(jnp.int32), dw, received)


def ada_prepare(c_all, w_ada, hgrn_lb):
    def body(c_ref, w_ref, lb_ref, mod_ref, row_ref):
        mod_ref[...] = hdot(silu(c_ref[...]), w_ref[...])
        row_ref[...] = sigmoid(lb_ref[0:1, :] - lb_ref[1:2, :])

    return pl.pallas_call(
        body, name="ada_prepare",
        out_shape=[jax.ShapeDtypeStruct((8, w_ada.shape[1]), F32), jax.ShapeDtypeStruct((1, D), F32)],
        compiler_params=pltpu.CompilerParams(vmem_limit_bytes=VMEM_LIMIT),
    )(c_all, w_ada, hgrn_lb)


SMALL_SEGS = (("mod", 6 * D), ("lb", D), ("gnorm", LANES), ("conv_w", 4 * CONV_DIM), ("conv_b", CONV_DIM),
              ("dt_bias", LANES), ("a_log", LANES), ("d", B_INNER), ("ssm_norm", B_INNER),
              ("ln1_g", D), ("ln1_b", D), ("ln2_g", D), ("ln2_b", D), ("loss", LANES))
SMALL_PARAMS = ("b_ada", "hgrn_lb", "hgrn_gnorm", "ssm_conv_b", "ssm_dt_bias", "ssm_a_log", "ssm_d", "ssm_norm",
                "ln1_g", "ln1_b", "ln2_g", "ln2_b")


def finalize_small(g_all, c_all, dmod_cols, params, m, v):
    n_p = len(SMALL_PARAMS)
    offs, o = {}, 0
    for nm, width in SMALL_SEGS:
        offs[nm] = (o, width)
        o += width

    def body(*refs):
        g_ref, c_ref, dm_ref = refs[:3]
        p_refs = refs[3:3 + n_p]
        m_refs = refs[3 + n_p:3 + 2 * n_p]
        v_refs = refs[3 + 2 * n_p:3 + 3 * n_p]
        outs = refs[3 + 3 * n_p:]
        gwa_ref, gcw_ref, loss_ref = outs[:3]
        res = outs[3:]
        total = jnp.sum(g_ref[...], axis=0, keepdims=True)
        seg = lambda nm: total[:, offs[nm][0]:offs[nm][0] + offs[nm][1]]
        loss_ref[...] = seg("loss")
        gwa_ref[...] = hdot(silu(c_ref[...]), dm_ref[...], "tn")
        cw = seg("conv_w")
        for j in range(4):
            gcw_ref[j:j + 1, :] = cw[:, j * CONV_DIM:(j + 1) * CONV_DIM]
        hc = lax.broadcasted_iota(jnp.int32, (B_INNER, LANES), 0)
        hj = lax.broadcasted_iota(jnp.int32, (B_INNER, LANES), 1)
        per_head = ((hc >> 6) == hj).astype(F32)
        heads = lambda nm: hdot(jnp.broadcast_to(seg(nm), (8, B_INNER)), per_head)[0:1, 0:32]
        lbp = sigmoid(p_refs[1][0:1, :] - p_refs[1][1:2, :])
        g_row = seg("lb") * lbp * (1.0 - lbp)
        grads = {"b_ada": seg("mod"), "hgrn_gnorm": seg("gnorm"), "ssm_conv_b": seg("conv_b"),
                 "ssm_dt_bias": seg("dt_bias")[:, 0:32], "ssm_a_log": seg("a_log")[:, 0:32], "ssm_d": heads("d"),
                 "ssm_norm": seg("ssm_norm"), "ln1_g": seg("ln1_g"), "ln1_b": seg("ln1_b"),
                 "ln2_g": seg("ln2_g"), "ln2_b": seg("ln2_b")}
        for i, nm in enumerate(SMALL_PARAMS):
            g_out, d_out, m_out, v_out = res[4 * i:4 * i + 4]
            if nm == "hgrn_lb":
                for row, gv in ((0, g_row), (1, -g_row)):
                    sl = slice(row, row + 1)
                    dl, mn, vn = adamw(p_refs[i][sl, :], gv, m_refs[i][sl, :], v_refs[i][sl, :])
                    g_out[sl, :], d_out[sl, :], m_out[sl, :], v_out[sl, :] = gv, dl, mn, vn
            else:
                gv = grads[nm]
                dl, mn, vn = adamw(p_refs[i][...], gv, m_refs[i][...], v_refs[i][...])
                g_out[...], d_out[...], m_out[...], v_out[...] = gv, dl, mn, vn

    out_shape = [jax.ShapeDtypeStruct((D, dmod_cols.shape[1]), F32), jax.ShapeDtypeStruct((4, CONV_DIM), F32),
                 jax.ShapeDtypeStruct((1, LANES), F32)]
    for p in params:
        out_shape += [jax.ShapeDtypeStruct(p.shape, F32)] * 4
    return pl.pallas_call(
        body, name="finalize_small", out_shape=out_shape,
        compiler_params=pltpu.CompilerParams(vmem_limit_bytes=VMEM_LIMIT),
    )(g_all, c_all, dmod_cols, *params, *m, *v)


def adam_update(w, g, m, v, name):
    rows, cols = w.shape
    tm, tc = _tile2(rows, cols)

    def body(w_ref, g_ref, m_ref, v_ref, d_ref, mo_ref, vo_ref):
        d_ref[...], mo_ref[...], vo_ref[...] = adamw(w_ref[...], g_ref[...], m_ref[...], v_ref[...])

    spec = pl.BlockSpec((tm, tc), lambda i, j: (i, j))
    return pl.pallas_call(
        body, name=name, grid=(rows // tm, cols // tc), in_specs=[spec] * 4, out_specs=[spec] * 3,
        out_shape=[jax.ShapeDtypeStruct((rows, cols), F32)] * 3,
        compiler_params=_params(("arbitrary", "arbitrary")),
    )(w, g, m, v)


def kernel(x, c, w_ada, b_ada, w_in, hgrn_lb, hgrn_gnorm, ssm_conv_w, ssm_conv_b, ssm_dt_bias, ssm_a_log, ssm_d, ssm_norm, w_branch_a, w_branch_b, w_o, ln1_g, ln1_b, w_ffn_gate, w_ffn_up, w_ffn_down, ln2_g, ln2_b, loss_target, m_w_ada, m_b_ada, m_w_in, m_hgrn_lb, m_hgrn_gnorm, m_ssm_conv_w, m_ssm_conv_b, m_ssm_dt_bias, m_ssm_a_log, m_ssm_d, m_ssm_norm, m_w_branch_a, m_w_branch_b, m_w_o, m_ln1_g, m_ln1_b, m_w_ffn_gate, m_w_ffn_up, m_w_ffn_down, m_ln2_g, m_ln2_b, v_w_ada, v_b_ada, v_w_in, v_hgrn_lb, v_hgrn_gnorm, v_ssm_conv_w, v_ssm_conv_b, v_ssm_dt_bias, v_ssm_a_log, v_ssm_d, v_ssm_norm, v_w_branch_a, v_w_branch_b, v_w_o, v_ln1_g, v_ln1_b, v_w_ffn_gate, v_w_ffn_up, v_w_ffn_down, v_ln2_g, v_ln2_b):
    given = dict(locals())
    chip = 2 * lax.axis_index("x") + lax.axis_index("y")
    core = lax.axis_index("c")
    t = x.shape[1]

    first = gather_rows(jnp.concatenate([c, ssm_conv_w.reshape(1, CONV_DIM)], axis=1), "gather_cond").reshape(8, D + CONV_DIM)
    c_all = first[:, :D]
    conv_w = first[0::2, D:].reshape(4, 4, CONV_DIM // 4).transpose(1, 0, 2).reshape(4, CONV_DIM)
    mod_part, lb_row = ada_prepare(c_all, w_ada[0], hgrn_lb)
    mod_cols = w_ada.shape[2]
    mod_row = exchange_rows(mod_part.reshape(8, 1, mod_cols), "exchange_mod").reshape(1, 6 * D) + b_ada

    local = {nm: given[nm][0] for nm in SHARDED if nm != "w_ffn_in"}
    local["w_ffn_in"] = jnp.concatenate([w_ffn_gate[0].T, w_ffn_up[0].T], axis=0)
    shards = [local[nm].astype(BF16) for nm in SHARDED]
    send_in, recv_in, sent_in, land_in, started_in = gather_start(shards[:1], mod_row, "in")
    send_rest, recv_rest, *flying = gather_start(shards[1:], started_in, "rest")
    n_rest = len(SHARDED) - 1
    sent_rest, land_rest, started_rest = flying[:n_rest], flying[n_rest:2 * n_rest], flying[-1]
    mod_row = mod_row + started_rest[0:1, 0:1]
    mod = tuple(mod_row[:, i * D:(i + 1) * D] for i in range(6))
    with_own = lambda land, shard: lax.dynamic_update_slice(land, shard[None], (chip, 0, 0))

    class Weights:
        def input_projection(self, after):
            (own,), land = gather_wait(send_in, recv_in, [sent_in], [land_in], after, "in")
            (land,) = forward_wait(forward_start(land, "in"), after, "in")
            return assemble_in_proj(with_own(land, own))

        def start_rest(self, after):
            self.own, landed = gather_wait(send_rest, recv_rest, sent_rest, land_rest, after, "rest")
            self.started = forward_start(landed, "rest")
            return self.started[-1]

        def rest(self, after):
            got = {nm: with_own(land, s) for nm, land, s in zip(SHARDED[1:], forward_wait(self.started, after, "rest"), self.own, strict=True)}
            whole = lambda nm: got[nm].reshape(4 * got[nm].shape[1], got[nm].shape[2])
            return tuple(whole(nm) for nm in SHARDED[1:])

    wts = Weights()

    per_head = lambda p: jnp.pad(p, ((0, 0), (0, LANES - p.shape[1])))
    small = (lb_row, hgrn_gnorm, conv_w, ssm_conv_b, per_head(ssm_dt_bias), per_head(ssm_a_log),
             jnp.repeat(ssm_d[0], B_INNER // 32)[None], ssm_norm, ln1_g, ln1_b, ln2_g, ln2_b)
    by_rows = lambda g: g.reshape(4, g.shape[0] // 4, g.shape[1])
    travelling = {}

    def start_early(dws):
        travelling["pair"] = pair_start([by_rows(dw) for dw in dws], "early")
        return travelling["pair"][-1]

    def between_scans(after):
        slabs, received = pair_wait(travelling["pair"], after, "early")
        travelling["pairs"] = [pair_add(s, r, core, "pair_add_" + nm) for nm, s, r in zip(SHARDED[1:], slabs, received, strict=True)]
        travelling["started"] = scatter_start(travelling["pairs"], "early")
        return travelling["started"][-1]

    def finish_early(after):
        travelling["pairs"], travelling["landed"] = scatter_wait(travelling["started"], after, "early")

    def start_last(dw_in):
        travelling["pairs_in"] = [split_pair_add(dw_in, rows_exchange(dw_in, "pair_exchange_last"), core)]
        travelling["started_in"] = scatter_start(travelling["pairs_in"], "last")
        return travelling["started_in"][-1]

    loss, grad_x, d_mod, d_wts, d_small = local_step(x[0], loss_target[0], mod, wts, small,
                                                     start_early, between_scans, finish_early, start_last)

    d_lb, d_gn, d_cw, d_cb, d_dtb, d_alog, d_dsk, d_nw, d_l1g, d_l1b, d_l2g, d_l2b = d_small
    row = jnp.concatenate(list(d_mod) + [d_lb, d_gn, d_cw.reshape(1, 4 * CONV_DIM), d_cb, d_dtb, d_alog, d_dsk, d_nw,
                                          d_l1g, d_l1b, d_l2g, d_l2b, jnp.pad(loss, ((0, 0), (0, LANES - 1)))], axis=1)
    g_all = gather_rows(row, "gather_small_grads").reshape(8, row.shape[1])
    dmod_cols = lax.dynamic_slice_in_dim(g_all, chip * mod_cols, mod_cols, axis=1)
    fin = finalize_small(g_all, c_all, dmod_cols, [given[n] for n in SMALL_PARAMS],
                         [given["m_" + n] for n in SMALL_PARAMS], [given["v_" + n] for n in SMALL_PARAMS])
    grads, deltas, new_m, new_v = {}, {}, {}, {}
    grads["w_ada"] = fin[0][None]
    grads["ssm_conv_w"] = lax.dynamic_slice_in_dim(fin[1], chip * (CONV_DIM // 4), CONV_DIM // 4, axis=1)[None]
    for i, nm in enumerate(SMALL_PARAMS):
        grads[nm], deltas[nm], new_m[nm], new_v[nm] = fin[3 + 4 * i:7 + 4 * i]

    pairs_in, landed_in = scatter_wait(travelling["started_in"], fin[3], "last")
    pairs, landed = pairs_in + travelling["pairs"], landed_in + travelling["landed"]
    halves = [sum_chips(r, p, chip, core, "sum_chips_" + nm) for nm, r, p in zip(SHARDED, landed, pairs, strict=True)]
    reduced = dict(zip(SHARDED, exchange_halves(halves), strict=True))
    reduced["w_ada"], reduced["ssm_conv_w"] = grads["w_ada"][0], grads["ssm_conv_w"][0]
    reduced["w_in"] = reduced["w_in"].T
    reduced["w_ffn_gate"], reduced["w_ffn_up"] = reduced["w_ffn_in"][:FFN_SHARD], reduced["w_ffn_in"][FFN_SHARD:]
    for nm in ("w_ada", "ssm_conv_w", "w_in", "w_branch_a", "w_branch_b", "w_o", "w_ffn_gate", "w_ffn_up", "w_ffn_down"):
        flipped = nm in ("w_in", "w_ffn_gate", "w_ffn_up")
        work = (lambda a: a[0].T) if flipped else (lambda a: a[0])
        back = (lambda a: a.T[None]) if flipped else (lambda a: a[None])
        d_, m_, v_ = adam_update(work(given[nm]), reduced[nm], work(given["m_" + nm]), work(given["v_" + nm]), "adam_" + nm)
        grads[nm], deltas[nm], new_m[nm], new_v[nm] = back(reduced[nm]), back(d_), back(m_), back(v_)

    names = ("w_ada", "b_ada", "w_in", "hgrn_lb", "hgrn_gnorm", "ssm_conv_w", "ssm_conv_b", "ssm_dt_bias", "ssm_a_log",
             "ssm_d", "ssm_norm", "w_branch_a", "w_branch_b", "w_o", "ln1_g", "ln1_b", "w_ffn_gate", "w_ffn_up",
             "w_ffn_down", "ln2_g", "ln2_b")
    return (fin[2][0, 0], grad_x[None], *[grads[n] for n in names], *[deltas[n] for n in names],
            *[new_m[n] for n in names], *[new_v[n] for n in names])
```

```python
import functools

import jax
import jax.numpy as jnp
from jax import lax
from jax.experimental import pallas as pl
from jax.experimental.pallas import tpu as pltpu

F32, BF16 = jnp.float32, jnp.bfloat16
HI = lax.Precision.HIGHEST
MESH = pl.DeviceIdType.MESH

D = 1024
CHUNK = 64
LANES = 128
N_HEADS_A = 8
N_GROUPS_B = 4
B_INNER = 2048
CONV_DIM = 3072
D_FF = 2816
ALPHA = 2.0 ** 0.25
LN_EPS = 1e-5
RMS_EPS = 1e-6
ADAM_LR, ADAM_B1, ADAM_B2, ADAM_EPS, ADAM_WD, ADAM_STEP = 0.001, 0.9, 0.999, 1e-08, 0.01, 10

IN_ORIG = 11296
IN_PAD = 11520
COL_GA, COL_GB, COL_XBC, COL_Z, COL_DT = 4096, 5120, 6144, 9216, 11264
ORIG_Z, ORIG_XBC, ORIG_DT, ORIG_GA = 4096, 6144, 9216, 9248

SHARDED = ("w_in", "w_branch_a", "w_branch_b", "w_o", "w_ffn_in", "w_ffn_down")
FFN_SHARD = D_FF // 4
VMEM_LIMIT = 56 * 1024 * 1024
BLOCK_BYTES = 2 * 1024 * 1024
_DIMS = {"nn": (((1,), (0,)), ((), ())), "nt": (((1,), (1,)), ((), ())), "tn": (((0,), (0,)), ((), ()))}


def _bd(a, b, mode):
    return lax.dot_general(a.astype(BF16), b.astype(BF16), _DIMS[mode], preferred_element_type=F32)


@functools.partial(jax.custom_vjp, nondiff_argnums=(2,))
def bdot(a, b, mode):
    return _bd(a, b, mode)


def _bdot_fwd(a, b, mode):
    return _bd(a, b, mode), (a, b)


def _bdot_bwd(mode, res, g):
    a, b = res
    if mode == "nn":
        return _bd(g, b, "nt"), _bd(a, g, "tn")
    if mode == "nt":
        return _bd(g, b, "nn"), _bd(g, a, "tn")
    return _bd(b, g, "nt"), _bd(a, g, "nn")


bdot.defvjp(_bdot_fwd, _bdot_bwd)


def hdot(a, b, mode="nn"):
    return lax.dot_general(a, b, _DIMS[mode], precision=HI, preferred_element_type=F32)


def _raw(a, b, mode):
    return lax.dot_general(a, b, _DIMS[mode], preferred_element_type=F32)


def _split(x, n):
    parts, rest = [], x
    for _ in range(n):
        p = rest.astype(BF16)
        parts.append(p)
        rest = rest - p.astype(F32)
    return parts


def _od(a, b, mode, exact):
    if exact == 1:
        e = b.astype(BF16)
        p = _split(a, 3)
        return (_raw(p[2], e, mode) + _raw(p[1], e, mode)) + _raw(p[0], e, mode)
    e = a.astype(BF16)
    p = _split(b, 3)
    return (_raw(e, p[2], mode) + _raw(e, p[1], mode)) + _raw(e, p[0], mode)


@functools.partial(jax.custom_vjp, nondiff_argnums=(2, 3))
def odot(a, b, mode, exact):
    return _od(a, b, mode, exact)


def _odot_fwd(a, b, mode, exact):
    return _od(a, b, mode, exact), (a, b)


def _odot_bwd(mode, exact, res, g):
    a, b = res
    if exact == 1:
        da = {"nn": lambda: _od(g, b, "nt", 1), "nt": lambda: _od(g, b, "nn", 1), "tn": lambda: _od(b, g, "nt", 0)}[mode]()
        return da, jnp.zeros_like(b)
    db = {"nn": lambda: _od(a, g, "tn", 0), "nt": lambda: _od(g, a, "tn", 1), "tn": lambda: _od(a, g, "nn", 0)}[mode]()
    return jnp.zeros_like(a), db


odot.defvjp(_odot_fwd, _odot_bwd)


_BDIMS = {"bnn": (((2,), (1,)), ((0,), (0,))), "bnt": (((2,), (2,)), ((0,), (0,))), "btn": (((1,), (1,)), ((0,), (0,)))}


def _braw(a, b, mode):
    return lax.dot_general(a, b, _BDIMS[mode], preferred_element_type=F32)


def _bdb(a, b, mode):
    return _braw(a.astype(BF16), b.astype(BF16), mode)


def _d3b(a, b, mode):
    ah, al = _split(a, 2)
    bh, bl = _split(b, 2)
    return _braw(ah, bh, mode) + (_braw(ah, bl, mode) + _braw(al, bh, mode))


def _batched_bwd(f):
    def bwd(mode, res, g):
        a, b = res
        if mode == "bnn":
            return f(g, b, "bnt"), f(a, g, "btn")
        if mode == "bnt":
            return f(g, b, "bnn"), f(g, a, "btn")
        return f(b, g, "bnt"), f(a, g, "bnn")
    return bwd


@functools.partial(jax.custom_vjp, nondiff_argnums=(2,))
def bdot_b(a, b, mode):
    return _bdb(a, b, mode)


bdot_b.defvjp(lambda a, b, mode: (_bdb(a, b, mode), (a, b)), _batched_bwd(_bdb))


@functools.partial(jax.custom_vjp, nondiff_argnums=(2,))
def dot3_b(a, b, mode):
    return _d3b(a, b, mode)


dot3_b.defvjp(lambda a, b, mode: (_d3b(a, b, mode), (a, b)), _batched_bwd(_d3b))


def _cum(tril3, x, mode):
    e = tril3.astype(BF16)
    p = _split(x, 3)
    return (_braw(e, p[2], mode) + _braw(e, p[1], mode)) + _braw(e, p[0], mode)


@jax.custom_vjp
def chunk_cumsum(tril3, x):
    return _cum(tril3, x, "bnn")


chunk_cumsum.defvjp(lambda t, x: (_cum(t, x, "bnn"), t), lambda t, g: (jnp.zeros_like(t), _cum(t, g, "btn")))


def _unstack(axis, n):
    @jax.custom_vjp
    def un(x):
        return tuple(lax.index_in_dim(x, i, axis, keepdims=False) for i in range(n))

    un.defvjp(lambda x: (un(x), None), lambda _, g: (jnp.stack(g, axis=axis),))
    return un


def _split_last(n, w):
    @jax.custom_vjp
    def sp(x):
        return tuple(x[..., i * w:(i + 1) * w] for i in range(n))

    sp.defvjp(lambda x: (sp(x), None), lambda _, g: (jnp.concatenate(g, axis=-1),))
    return sp


def sigmoid(x):
    return 1.0 / (1.0 + jnp.exp(-x))


def silu(x):
    return x * sigmoid(x)


def softplus(x):
    return jnp.maximum(x, 0.0) + jnp.log1p(jnp.exp(jnp.minimum(x, -x)))


def _ln(x):
    mu = jnp.mean(x, axis=-1, keepdims=True)
    xc = x - mu
    return xc * lax.rsqrt(jnp.mean(xc * xc, axis=-1, keepdims=True) + LN_EPS)


def _tril64():
    r = lax.broadcasted_iota(jnp.int32, (CHUNK, CHUNK), 0)
    c = lax.broadcasted_iota(jnp.int32, (CHUNK, CHUNK), 1)
    return (r >= c).astype(F32)


def hgrn_block(q, fl, iv, gr, st, lb, gn):
    tb = q.shape[0]
    nc = tb // CHUNK
    nh = N_HEADS_A
    heads = _split_last(nh, LANES)
    to4 = lambda a: jnp.stack(heads(a), axis=0).reshape(nh, nc, CHUNK, LANES)
    flat = lambda a: a.reshape(nh * nc, CHUNK, LANES)
    f = lb + (1.0 - lb) * sigmoid(fl)
    gl4, k4, qf4, v4, gr4 = to4(jnp.log(f)), to4(1.0 - f), to4(silu(q) * (128 ** -0.5)), to4(iv), to4(gr)
    tril = _tril64()
    b4 = chunk_cumsum(jnp.broadcast_to(tril[None], (nh * nc, CHUNK, CHUNK)), flat(gl4)).reshape(gl4.shape)
    blast = jnp.sum(gl4, axis=2, keepdims=True)
    ref = lax.stop_gradient(0.5 * blast)
    sc = dot3_b(flat(qf4 * jnp.exp(b4 - ref)), flat(k4 * jnp.exp(ref - b4)), "bnt") * tril
    o_intra = bdot_b(sc, flat(v4), "bnn").reshape(gl4.shape)
    chunks = _unstack(1, nc)
    qe, v_c, kd, dec = chunks(qf4 * jnp.exp(b4)), chunks(v4), chunks(k4 * jnp.exp(blast - b4)), chunks(jnp.exp(blast))
    o_inter = []
    for c in range(nc):
        o_inter.append(bdot_b(qe[c], st, "bnt"))
        st = st * dec[c] + bdot_b(v_c[c], kd[c], "btn")
    o = o_intra + jnp.stack(o_inter, axis=1)
    on = o * lax.rsqrt(jnp.mean(o * o, axis=-1, keepdims=True) + RMS_EPS) * gn
    out = (on * silu(gr4)).reshape(nh, tb, LANES)
    return jnp.concatenate(_unstack(0, nh)(out), axis=1), st


def ssd_consts(g):
    i32 = jnp.int32
    ej = lax.broadcasted_iota(i32, (LANES, 512), 0)
    ec = lax.broadcasted_iota(i32, (LANES, 512), 1)
    expand = (ej == g * 8 + (ec >> 6)).astype(F32)
    ts = lax.broadcasted_iota(i32, (CHUNK, 512), 0)
    tc = lax.broadcasted_iota(i32, (CHUNK, 512), 1)
    itile = (ts == (tc & 63)).astype(F32)
    maskall = ts >= (tc & 63)
    br = lax.broadcasted_iota(i32, (256, 256), 0)
    bc = lax.broadcasted_iota(i32, (256, 256), 1)
    blockmask = ((br >> 6) == (bc >> 6)).astype(F32)
    return expand, itile, maskall, blockmask, _tril64()


def ssd_block(x, bm, cm, dt, z, st, dtb, alog, dsk, nw, cs):
    expand, itile, maskall, blockmask, tril = cs
    tb = x.shape[0]
    nc = tb // CHUNK
    delta_heads = softplus(dt + dtb)
    delta = odot(delta_heads, expand, "nn", 1)
    a = odot(-jnp.exp(alog) * delta_heads, expand, "nn", 1)
    xdt = x * delta
    by_chunk = lambda v: v.reshape(nc, CHUNK, v.shape[-1])
    a3, xdt3, bm3, cm3 = by_chunk(a), by_chunk(xdt), by_chunk(bm), by_chunk(cm)
    acum3 = chunk_cumsum(jnp.broadcast_to(tril[None], (nc, CHUNK, CHUNK)), a3)
    alast3 = jnp.sum(a3, axis=1, keepdims=True)
    cb3 = bdot_b(cm3, jnp.concatenate([bm3] * 8, axis=1), "bnt")
    arow3 = jnp.sum(acum3 * itile, axis=1, keepdims=True)
    dec3 = jnp.exp(jnp.where(maskall, acum3 - arow3, -1e30))
    halves = _split_last(2, 256)
    intra = [bdot_b(m, jnp.concatenate([xh] * 4, axis=1) * blockmask, "bnn")
             for m, xh in zip(halves(cb3 * dec3), halves(xdt3))]
    chunks = _unstack(0, nc)
    cm_c, bm_c, xw_c, dec_c = chunks(cm3), chunks(bm3), chunks(xdt3 * jnp.exp(alast3 - acum3)), chunks(jnp.exp(alast3))
    inter = []
    for c in range(nc):
        inter.append(bdot(cm_c[c], st, "nn"))
        st = st * dec_c[c] + bdot(bm_c[c], xw_c[c], "tn")
    st_new = st
    y = (jnp.concatenate(intra, axis=-1) + jnp.stack(inter, axis=0) * jnp.exp(acum3)).reshape(tb, 512)
    yz = (y + x * dsk) * silu(z)
    return yz * lax.rsqrt(jnp.mean(yz * yz, axis=-1, keepdims=True) + RMS_EPS) * nw, st_new


def adamw(w, g, m, v):
    m = ADAM_B1 * m + (1.0 - ADAM_B1) * g
    v = ADAM_B2 * v + (1.0 - ADAM_B2) * jnp.square(g)
    m_hat = m / (1.0 - ADAM_B1 ** ADAM_STEP)
    v_hat = v / (1.0 - ADAM_B2 ** ADAM_STEP)
    return -ADAM_LR * (m_hat / (jnp.sqrt(v_hat) + ADAM_EPS) + ADAM_WD * w), m, v


def _pick(n, cands):
    for c in cands:
        if n % c == 0:
            return c
    return n


def _params(sem):
    return pltpu.CompilerParams(dimension_semantics=sem, vmem_limit_bytes=VMEM_LIMIT)


MATMUL_VMEM_BUDGET = 50 * 1024 * 1024
MATMUL_MIN_STEPS = 4


def matmul(a, b, mode, out_dtype, name, after=None):
    if mode == "nn":
        (m, k), n = a.shape, b.shape[1]
    elif mode == "nt":
        (m, k), n = a.shape, b.shape[0]
    else:
        (k, m), n = a.shape, b.shape[1]
    tk = _pick(k, (2304, 2048, 1408, 1024, 768, 512, 256, 128))
    nk = k // tk
    a_bytes, b_bytes, out_bytes = a.dtype.itemsize, b.dtype.itemsize, jnp.dtype(out_dtype).itemsize

    def vmem(tm_, tn_):
        blocks = 2 * (tm_ * tk * a_bytes + tk * tn_ * b_bytes + tm_ * tn_ * out_bytes)
        return blocks + (tm_ * tn_ * 4 if nk > 1 else 0)

    def traffic(tm_, tn_):
        return (m // tm_) * k * n * b_bytes + (n // tn_ if nk > 1 else 1) * m * k * a_bytes

    sizes = (2304, 2048, 1920, 1408, 1024, 768, 512, 256, 128)
    tiles = [(tm_, tn_) for tm_ in sizes if m % tm_ == 0 for tn_ in sizes if n % tn_ == 0
             if vmem(tm_, tn_) <= MATMUL_VMEM_BUDGET] or [(m, n)]
    pipelined = [t for t in tiles if (m // t[0]) * (n // t[1]) * nk >= MATMUL_MIN_STEPS]
    tm, tn = min(pipelined or tiles, key=lambda t: (traffic(*t), -t[0] * t[1]))
    a_spec = pl.BlockSpec((tk, tm), lambda i, j, kk: (kk, i)) if mode == "tn" else pl.BlockSpec((tm, tk), lambda i, j, kk: (i, kk))
    b_spec = pl.BlockSpec((tn, tk), lambda i, j, kk: (j, kk)) if mode == "nt" else pl.BlockSpec((tk, tn), lambda i, j, kk: (kk, j))

    order = [] if after is None else [after]

    def body(a_ref, b_ref, *rest):
        o_ref, *acc = rest[len(order):]
        part = _bd(a_ref[...], b_ref[...], mode)
        if nk == 1:
            o_ref[...] = part.astype(o_ref.dtype)
            return
        acc_ref, = acc
        kk = pl.program_id(2)

        @pl.when(kk == 0)
        def _():
            acc_ref[...] = part

        @pl.when(jnp.logical_and(kk > 0, kk < nk - 1))
        def _():
            acc_ref[...] += part

        @pl.when(kk == nk - 1)
        def _():
            o_ref[...] = (acc_ref[...] + part).astype(o_ref.dtype)

    return pl.pallas_call(
        body, name=name, grid=(m // tm, n // tn, nk),
        in_specs=[a_spec, b_spec] + [pl.BlockSpec(memory_space=pl.ANY) for _ in order],
        out_specs=pl.BlockSpec((tm, tn), lambda i, j, kk: (i, j)),
        out_shape=jax.ShapeDtypeStruct((m, n), out_dtype),
        scratch_shapes=[pltpu.VMEM((tm, tn), F32)] if nk > 1 else [],
        compiler_params=_params(("parallel", "parallel", "arbitrary")),
    )(a, b, *order)


def rowwise(name, fn, rows, consts, out_rows, out_accs=(), tm_max=512, into=None, new_wide=None):
    t = rows[0][0].shape[0]
    tm = _pick(t, (tm_max, 128, 64, 32, 16, 8))
    n_r, n_c, n_o = len(rows), len(consts), len(out_rows)
    n_alias = 0 if into is None else 1

    def body(*refs):
        r_in = [r[...] for r in refs[:n_r]]
        c_in = [r[...] for r in refs[n_r:n_r + n_c]]
        refs = refs[:n_r + n_c] + refs[n_r + n_c + n_alias:]
        o_refs = refs[n_r + n_c:n_r + n_c + n_o]
        a_refs = refs[n_r + n_c + n_o:]
        ro, ao = fn(r_in, c_in)
        for ref, val in zip(o_refs, ro, strict=True):
            ref[...] = val.astype(ref.dtype)
        if a_refs:
            @pl.when(pl.program_id(0) == 0)
            def _():
                for ref in a_refs:
                    ref[...] = jnp.zeros_like(ref)

            for ref, val in zip(a_refs, ao, strict=True):
                ref[...] += val

    in_specs = [pl.BlockSpec((tm, w), functools.partial(lambda i, cb: (i, cb), cb=cb)) for _, w, cb in rows]
    in_specs += [pl.BlockSpec(c.shape, lambda i: (0, 0)) for c in consts]
    out_specs = [pl.BlockSpec((tm, w), lambda i: (i, 0)) for w, _ in out_rows]
    out_specs += [pl.BlockSpec(s, lambda i: (0, 0)) for s in out_accs]
    out_shape = [jax.ShapeDtypeStruct((t, w), dt) for w, dt in out_rows]
    out_shape += [jax.ShapeDtypeStruct(s, F32) for s in out_accs]
    operands = [r[0] for r in rows] + list(consts)
    aliases = {}
    if into is not None:
        target, cb = into
        in_specs.append(pl.BlockSpec(memory_space=pl.ANY))
        operands.append(target)
        out_specs[0] = pl.BlockSpec((tm, out_rows[0][0]), lambda i: (i, cb))
        out_shape[0] = jax.ShapeDtypeStruct(target.shape, target.dtype)
        aliases = {len(operands) - 1: 0}
    if new_wide is not None:
        width, cb = new_wide
        out_specs[0] = pl.BlockSpec((tm, out_rows[0][0]), lambda i: (i, cb))
        out_shape[0] = jax.ShapeDtypeStruct((t, width), out_rows[0][1])
    return pl.pallas_call(
        body, name=name, grid=(t // tm,), in_specs=in_specs, out_specs=out_specs, out_shape=out_shape,
        input_output_aliases=aliases, compiler_params=_params(("arbitrary",)),
    )(*operands)


def _full(a):
    return (a, a.shape[1], 0)


HGRN_TIME_BLOCK = 256
SSD_TIME_BLOCK = 512


def _time_block(t, most=HGRN_TIME_BLOCK):
    return _pick(t, tuple(b for b in (512, 256, 128, 64) if b <= most))


def _quarters(ref):
    return [ref[:, seg * D:(seg + 1) * D] for seg in range(4)]


def hgrn_forward(proj, lb, gn):
    t = proj.shape[0]
    tb = _time_block(t)
    nb = t // tb

    def body(qfig_ref, lb_ref, gn_ref, o_ref, st_ref, state):
        @pl.when(pl.program_id(0) == 0)
        def _():
            state[...] = jnp.zeros_like(state)

        st = state[...]
        st_ref[...] = st
        out, st_new = hgrn_block(*_quarters(qfig_ref), st, lb_ref[...], gn_ref[...])
        o_ref[...] = out.astype(o_ref.dtype)
        state[...] = st_new

    return pl.pallas_call(
        body, name="hgrn_forward", grid=(nb,),
        in_specs=[pl.BlockSpec((tb, 4 * D), lambda j: (j, 0)),
                  pl.BlockSpec((1, D), lambda j: (0, 0)), pl.BlockSpec((1, LANES), lambda j: (0, 0))],
        out_specs=[pl.BlockSpec((tb, D), lambda j: (j, 0)),
                   pl.BlockSpec((None, N_HEADS_A, LANES, LANES), lambda j: (j, 0, 0, 0))],
        out_shape=[jax.ShapeDtypeStruct((t, D), BF16),
                   jax.ShapeDtypeStruct((nb, N_HEADS_A, LANES, LANES), F32)],
        scratch_shapes=[pltpu.VMEM((N_HEADS_A, LANES, LANES), F32)],
        compiler_params=_params(("arbitrary",)),
    )(proj, lb, gn)


def hgrn_backward(proj, states, d_out, lb, gn, d_proj):
    t = proj.shape[0]
    tb = _time_block(t)
    nb = t // tb

    def body(qfig_ref, st_ref, do_ref, lb_ref, gn_ref, _, dqfig_ref, dlb_ref, dgn_ref, d_state):
        @pl.when(pl.program_id(0) == 0)
        def _():
            d_state[...] = jnp.zeros_like(d_state)
            dlb_ref[...] = jnp.zeros_like(dlb_ref)
            dgn_ref[...] = jnp.zeros_like(dgn_ref)

        _, vjp = jax.vjp(hgrn_block, *_quarters(qfig_ref), st_ref[...], lb_ref[...], gn_ref[...])
        dq, df, di, dg, dst, dlb, dgn = vjp((do_ref[...], d_state[...]))
        for seg, val in enumerate((dq, df, di, dg)):
            dqfig_ref[:, seg * D:(seg + 1) * D] = val.astype(dqfig_ref.dtype)
        d_state[...] = dst
        dlb_ref[...] += dlb
        dgn_ref[...] += dgn

    rev = lambda j: nb - 1 - j
    return pl.pallas_call(
        body, name="hgrn_backward", grid=(nb,),
        in_specs=[pl.BlockSpec((tb, 4 * D), lambda j: (rev(j), 0)),
                  pl.BlockSpec((None, N_HEADS_A, LANES, LANES), lambda j: (rev(j), 0, 0, 0)),
                  pl.BlockSpec((tb, D), lambda j: (rev(j), 0)),
                  pl.BlockSpec((1, D), lambda j: (0, 0)), pl.BlockSpec((1, LANES), lambda j: (0, 0)),
                  pl.BlockSpec(memory_space=pl.ANY)],
        out_specs=[pl.BlockSpec((tb, 4 * D), lambda j: (rev(j), 0)),
                   pl.BlockSpec((1, D), lambda j: (0, 0)), pl.BlockSpec((1, LANES), lambda j: (0, 0))],
        out_shape=[jax.ShapeDtypeStruct(d_proj.shape, d_proj.dtype), jax.ShapeDtypeStruct((1, D), F32),
                   jax.ShapeDtypeStruct((1, LANES), F32)],
        input_output_aliases={5: 0},
        scratch_shapes=[pltpu.VMEM((N_HEADS_A, LANES, LANES), F32)],
        compiler_params=_params(("arbitrary",)),
    )(proj, states, d_out, lb, gn, d_proj)


def _ssd_in_specs(tb, tmap):
    return [pl.BlockSpec((tb, 512), lambda g, j: (tmap(j), g)),
            pl.BlockSpec((tb, LANES), lambda g, j: (tmap(j), 16 + g)),
            pl.BlockSpec((tb, LANES), lambda g, j: (tmap(j), 20 + g)),
            pl.BlockSpec((tb, LANES), lambda g, j: (tmap(j), COL_DT // LANES)),
            pl.BlockSpec((tb, 512), lambda g, j: (tmap(j), COL_Z // 512 + g))]


def ssd_forward(xc, proj, dtb, alog, dsk, nw):
    t = proj.shape[0]
    tb = _time_block(t, SSD_TIME_BLOCK)
    nb = t // tb

    def body(x_ref, b_ref, c_ref, dt_ref, z_ref, dtb_ref, alog_ref, dsk_ref, nw_ref, o_ref, st_ref, state):
        @pl.when(pl.program_id(1) == 0)
        def _():
            state[...] = jnp.zeros_like(state)

        st = state[...]
        st_ref[...] = st
        out, st_new = ssd_block(x_ref[...], b_ref[...], c_ref[...], dt_ref[...], z_ref[...], st,
                                dtb_ref[...], alog_ref[...], dsk_ref[...], nw_ref[...], ssd_consts(pl.program_id(0)))
        o_ref[...] = out.astype(o_ref.dtype)
        state[...] = st_new

    vec = pl.BlockSpec((1, 512), lambda g, j: (0, g))
    heads = pl.BlockSpec((1, LANES), lambda g, j: (0, 0))
    return pl.pallas_call(
        body, name="ssd_forward", grid=(N_GROUPS_B, nb),
        in_specs=_ssd_in_specs(tb, lambda j: j) + [heads, heads, vec, vec],
        out_specs=[pl.BlockSpec((tb, 512), lambda g, j: (j, g)),
                   pl.BlockSpec((None, None, LANES, 512), lambda g, j: (j, g, 0, 0))],
        out_shape=[jax.ShapeDtypeStruct((t, B_INNER), BF16),
                   jax.ShapeDtypeStruct((nb, N_GROUPS_B, LANES, 512), F32)],
        scratch_shapes=[pltpu.VMEM((LANES, 512), F32)],
        compiler_params=_params(("arbitrary", "arbitrary")),
    )(xc, xc, xc, proj, proj, dtb, alog, dsk, nw)


def ssd_backward(xc, proj, states, d_out, dtb, alog, dsk, nw, d_proj):
    t = proj.shape[0]
    tb = _time_block(t, SSD_TIME_BLOCK)
    nb = t // tb
    rev = lambda j: nb - 1 - j

    def body(x_ref, b_ref, c_ref, dt_ref, z_ref, st_ref, do_ref, dtb_ref, alog_ref, dsk_ref, nw_ref, _,
             dx_ref, db_ref, dc_ref, ddt_ref, dz_ref, ddtb_ref, dalog_ref, ddsk_ref, dnw_ref, d_state):
        accs = (ddtb_ref, dalog_ref, ddsk_ref, dnw_ref)

        @pl.when(pl.program_id(1) == 0)
        def _():
            d_state[...] = jnp.zeros_like(d_state)
            for ref in accs:
                ref[...] = jnp.zeros_like(ref)

        cs = ssd_consts(pl.program_id(0))
        fn = lambda *a: ssd_block(*a, cs)
        _, vjp = jax.vjp(fn, x_ref[...], b_ref[...], c_ref[...], dt_ref[...], z_ref[...], st_ref[...],
                         dtb_ref[...], alog_ref[...], dsk_ref[...], nw_ref[...])
        dx, db, dc, ddt, dz, dst, *dpar = vjp((do_ref[...], d_state[...]))
        dx_ref[...] = dx
        db_ref[...] = db
        dc_ref[...] = dc
        ddt_ref[...] = ddt
        dz_ref[...] = dz.astype(dz_ref.dtype)
        d_state[...] = dst
        for ref, val in zip(accs, dpar, strict=True):
            ref[...] += val

    vec = pl.BlockSpec((1, 512), lambda g, j: (0, g))
    heads = pl.BlockSpec((1, LANES), lambda g, j: (0, 0))
    acc = pl.BlockSpec((None, 1, 512), lambda g, j: (g, 0, 0))
    acc_heads = pl.BlockSpec((None, 1, LANES), lambda g, j: (g, 0, 0))
    return pl.pallas_call(
        body, name="ssd_backward", grid=(N_GROUPS_B, nb),
        in_specs=_ssd_in_specs(tb, rev)
        + [pl.BlockSpec((None, None, LANES, 512), lambda g, j: (rev(j), g, 0, 0)),
           pl.BlockSpec((tb, 512), lambda g, j: (rev(j), g))] + [heads, heads, vec, vec] + [pl.BlockSpec(memory_space=pl.ANY)],
        out_specs=[pl.BlockSpec((tb, 512), lambda g, j: (rev(j), g)),
                   pl.BlockSpec((tb, LANES), lambda g, j: (rev(j), g)),
                   pl.BlockSpec((tb, LANES), lambda g, j: (rev(j), g)),
                   pl.BlockSpec((None, tb, LANES), lambda g, j: (g, rev(j), 0)),
                   pl.BlockSpec((tb, 512), lambda g, j: (rev(j), COL_Z // 512 + g)), acc_heads, acc_heads, acc, acc],
        out_shape=[jax.ShapeDtypeStruct((t, B_INNER), F32), jax.ShapeDtypeStruct((t, 512), F32),
                   jax.ShapeDtypeStruct((t, 512), F32), jax.ShapeDtypeStruct((N_GROUPS_B, t, LANES), F32),
                   jax.ShapeDtypeStruct(d_proj.shape, d_proj.dtype)]
        + [jax.ShapeDtypeStruct((N_GROUPS_B, 1, LANES), F32)] * 2 + [jax.ShapeDtypeStruct((N_GROUPS_B, 1, 512), F32)] * 2,
        input_output_aliases={11: 4},
        scratch_shapes=[pltpu.VMEM((LANES, 512), F32)],
        compiler_params=_params(("arbitrary", "arbitrary")),
    )(xc, xc, xc, proj, proj, states, d_out, dtb, alog, dsk, nw, d_proj)


CONV_HALO = 8


def _shift_down(halo_then_tile, s, tm):
    if s == 0:
        return halo_then_tile[CONV_HALO:CONV_HALO + tm]
    return pltpu.roll(halo_then_tile, s, 0)[CONV_HALO:CONV_HALO + tm]


def _conv_pre(cur, prev, w, b, tm):
    stacked = jnp.concatenate([prev, cur], axis=0)
    taps = [_shift_down(stacked, 3 - j, tm) for j in range(4)]
    pre = b + taps[0] * w[0:1] + taps[1] * w[1:2] + taps[2] * w[2:3] + taps[3] * w[3:4]
    return pre, taps


def _conv_specs(t, tm):
    per = tm // CONV_HALO
    cur = pl.BlockSpec((tm, CONV_DIM), lambda i: (i, COL_XBC // CONV_DIM))
    prev = pl.BlockSpec((CONV_HALO, CONV_DIM), lambda i: (jnp.maximum(i * per - 1, 0), COL_XBC // CONV_DIM))
    return cur, prev


def conv_forward(proj, w, b):
    t = proj.shape[0]
    tm = _pick(t, (256, 128, 64))

    def body(cur_ref, prev_ref, w_ref, b_ref, o_ref):
        prev = jnp.where(pl.program_id(0) == 0, 0.0, prev_ref[...])
        pre, _ = _conv_pre(cur_ref[...], prev, w_ref[...], b_ref[...], tm)
        o_ref[...] = silu(pre)

    cur, prev = _conv_specs(t, tm)
    return pl.pallas_call(
        body, name="conv_forward", grid=(t // tm,),
        in_specs=[cur, prev, pl.BlockSpec((4, CONV_DIM), lambda i: (0, 0)), pl.BlockSpec((1, CONV_DIM), lambda i: (0, 0))],
        out_specs=pl.BlockSpec((tm, CONV_DIM), lambda i: (i, 0)),
        out_shape=jax.ShapeDtypeStruct((t, CONV_DIM), F32),
        compiler_params=_params(("arbitrary",)),
    )(proj, proj, w, b)


def conv_backward(proj, dx, db_, dc_, w, b, d_proj):
    t = proj.shape[0]
    tm = _pick(t, (256, 128, 64))
    per = tm // CONV_HALO
    nt = t // tm
    rev = lambda i: nt - 1 - i

    def body(cur_ref, prev_ref, dx_ref, dbm_ref, dcm_ref, w_ref, b_ref, _, o_ref, dw_ref, dbias_ref, later):
        @pl.when(pl.program_id(0) == 0)
        def _():
            dw_ref[...] = jnp.zeros_like(dw_ref)
            dbias_ref[...] = jnp.zeros_like(dbias_ref)
            later[...] = jnp.zeros_like(later)

        first_tile = pl.program_id(0) == nt - 1
        for lo, hi, src in ((0, B_INNER, dx_ref), (B_INNER, B_INNER + 512, dbm_ref), (B_INNER + 512, CONV_DIM, dcm_ref)):
            cols = slice(lo, hi)
            prev = jnp.where(first_tile, 0.0, prev_ref[:, cols])
            w_ = w_ref[:, cols]
            pre, taps = _conv_pre(cur_ref[:, cols], prev, w_, b_ref[:, cols], tm)
            sg = sigmoid(pre)
            dpre = src[...] * (sg * (1.0 + pre * (1.0 - sg)))
            dbias_ref[:, cols] += jnp.sum(dpre, axis=0, keepdims=True)
            for j in range(4):
                dw_ref[j:j + 1, cols] += jnp.sum(dpre * taps[j], axis=0, keepdims=True)
            stacked = jnp.concatenate([dpre, later[:, cols]], axis=0)
            acc = dpre * w_[3:4]
            for j in range(3):
                acc = acc + pltpu.roll(stacked, tm + CONV_HALO - (3 - j), 0)[0:tm] * w_[j:j + 1]
            o_ref[:, cols] = acc.astype(o_ref.dtype)
            later[:, cols] = dpre[0:CONV_HALO]

    row = lambda w_: pl.BlockSpec((tm, w_), lambda i: (rev(i), 0))
    whole = lambda r: pl.BlockSpec((r, CONV_DIM), lambda i: (0, 0))
    return pl.pallas_call(
        body, name="conv_backward", grid=(nt,),
        in_specs=[pl.BlockSpec((tm, CONV_DIM), lambda i: (rev(i), COL_XBC // CONV_DIM)),
                  pl.BlockSpec((CONV_HALO, CONV_DIM), lambda i: (jnp.maximum(rev(i) * per - 1, 0), COL_XBC // CONV_DIM)),
                  row(B_INNER), row(512), row(512), whole(4), whole(1), pl.BlockSpec(memory_space=pl.ANY)],
        out_specs=[pl.BlockSpec((tm, CONV_DIM), lambda i: (rev(i), COL_XBC // CONV_DIM)), whole(4), whole(1)],
        out_shape=[jax.ShapeDtypeStruct(d_proj.shape, d_proj.dtype), jax.ShapeDtypeStruct((4, CONV_DIM), F32),
                   jax.ShapeDtypeStruct((1, CONV_DIM), F32)],
        input_output_aliases={7: 0},
        scratch_shapes=[pltpu.VMEM((CONV_HALO, CONV_DIM), F32)],
        compiler_params=_params(("arbitrary",)),
    )(proj, proj, dx, db_, dc_, w, b, d_proj)


def stage_modulate(x, sc, sh):
    return _ln(x) * (1.0 + sc) + sh


def stage_merge(ga, gb, ya, yb):
    return sigmoid(ga) * ya + sigmoid(gb) * yb


def stage_post_mixer(x, h, g1, ln_g, ln_b, sc2, sh2):
    x1 = _ln(ALPHA * x + g1 * h) * ln_g + ln_b
    return x1, _ln(x1) * (1.0 + sc2) + sh2


def stage_swiglu(a, b):
    return silu(a) * b


def gate_up(ab):
    w = FFN_SHARD
    return (jnp.concatenate([ab[:, 2 * w * k:2 * w * k + w] for k in range(4)], axis=1),
            jnp.concatenate([ab[:, 2 * w * k + w:2 * w * (k + 1)] for k in range(4)], axis=1))


def per_chip(gate, up):
    w = FFN_SHARD
    return jnp.concatenate([part[:, w * k:w * (k + 1)] for k in range(4) for part in (gate, up)], axis=1)


def stage_loss(x1, hf, tgt, g2, ln_g, ln_b):
    x2 = _ln(ALPHA * x1 + g2 * hf) * ln_g + ln_b
    return 0.5 * jnp.sum(jnp.mean(jnp.square(x2 - tgt), axis=-1, keepdims=True), axis=0, keepdims=True)


def local_step(x, tgt, mod, wts, small, early=None, mid=None, late=None, last=None):
    sh1, sc1, g1, sh2, sc2, g2 = mod
    lb, gn, conv_w, conv_b, dtb, alog, dsk, nw, ln1_g, ln1_b, ln2_g, ln2_b = small
    vec = (1, D)

    (u1,) = rowwise("modulate1", lambda r, c: ((stage_modulate(r[0], *c),), ()), [_full(x)], [sc1, sh1], [(D, BF16)])
    w_in = wts.input_projection(u1)
    proj = matmul(u1, w_in, "nn", F32, "in_proj")
    ya_in, st_a = hgrn_forward(proj, lb, gn + wts.start_rest(proj)[0:1])
    xc = conv_forward(proj, conv_w, conv_b)
    w_a, w_b, w_o, w_gu, w_d = wts.rest(xc)
    yb_in, st_b = ssd_forward(xc, proj, dtb, alog, dsk, nw)
    ya = matmul(ya_in, w_a, "nn", F32, "branch_a")
    yb = matmul(yb_in, w_b, "nn", F32, "branch_b")
    gate_rows = [(proj, D, COL_GA // D), (proj, D, COL_GB // D), _full(ya), _full(yb)]
    (merged,) = rowwise("merge", lambda r, c: ((stage_merge(*r),), ()), gate_rows, [], [(D, BF16)])
    h = matmul(merged, w_o, "nn", F32, "out_proj")
    post_consts = [g1, ln1_g, ln1_b, sc2, sh2]
    x1, u2 = rowwise("post_mixer", lambda r, c: (stage_post_mixer(*r, *c), ()), [_full(x), _full(h)], post_consts,
                     [(D, F32), (D, BF16)])
    ab = matmul(u2, w_gu, "nt", F32, "ffn_in")
    (p,) = rowwise("swiglu", lambda r, c: ((stage_swiglu(*gate_up(r[0])),), ()), [_full(ab)], [], [(D_FF, BF16)],
                   tm_max=256)
    hf = matmul(p, w_d, "nn", F32, "ffn_out")

    def loss_bwd(r, c):
        loss, vjp = jax.vjp(stage_loss, *r, *c)
        dx1, dhf, _, dg2, dlg, dlb_ = vjp(jnp.ones((1, 1), F32))
        return (dx1, dhf), (loss, dg2, dlg, dlb_)

    dx1, dhf, loss, dg2, dln2_g, dln2_b = rowwise(
        "loss_backward", loss_bwd, [_full(x1), _full(hf), _full(tgt)], [g2, ln2_g, ln2_b],
        [(D, F32), (D, BF16)], [(1, 1), vec, vec, vec])
    dp = matmul(dhf, w_d, "nt", F32, "ffn_out_dx")
    dw_d = matmul(p, dhf, "tn", F32, "ffn_out_dw")

    def swiglu_bwd(r, c):
        _, vjp = jax.vjp(stage_swiglu, *gate_up(r[0]))
        return (per_chip(*vjp(r[1])),), ()

    (dab,) = rowwise("swiglu_backward", swiglu_bwd, [_full(ab), _full(dp)], [], [(2 * D_FF, BF16)], tm_max=256)
    du2 = matmul(dab, w_gu, "nn", F32, "ffn_in_dx")
    dw_gu = matmul(dab, u2, "tn", F32, "ffn_in_dw")

    def post_bwd(r, c):
        _, vjp = jax.vjp(stage_post_mixer, r[0], r[1], *c)
        dx, dh, *dc = vjp((r[2], r[3]))
        return (dx, dh), tuple(dc)

    dx_a, dh, dg1, dln1_g, dln1_b, dsc2, dsh2 = rowwise(
        "post_mixer_backward", post_bwd, [_full(x), _full(h), _full(dx1), _full(du2)], post_consts,
        [(D, F32), (D, BF16)], [vec] * 5)
    dmerged = matmul(dh, w_o, "nt", F32, "out_proj_dx")
    dw_o = matmul(merged, dh, "tn", F32, "out_proj_dw")

    def merge_bwd(r, c):
        _, vjp = jax.vjp(stage_merge, *r[:4])
        dga, dgb, dya, dyb = vjp(r[4])
        return (jnp.concatenate([dga, dgb], axis=1), dya, dyb), ()

    dproj, dya, dyb = rowwise("merge_backward", merge_bwd, gate_rows + [_full(dmerged)], [],
                              [(2 * D, BF16), (D, BF16), (D, BF16)], new_wide=(IN_PAD, COL_GA // (2 * D)))
    dya_in = matmul(dya, w_a, "nt", F32, "branch_a_dx")
    dw_a = matmul(ya_in, dya, "tn", F32, "branch_a_dw")
    dyb_in = matmul(dyb, w_b, "nt", F32, "branch_b_dx")
    dw_b = matmul(yb_in, dyb, "tn", F32, "branch_b_dw")
    gn_after = gn if early is None else gn + early((dw_a, dw_b, dw_o, dw_gu, dw_d))[0:1]
    dproj, dlb, dgn = hgrn_backward(proj, st_a, dya_in, lb, gn_after, dproj)
    dtb_after = dtb if mid is None else dtb + mid(dlb)[0:1, 0:1]
    dxs, dbm, dcm, ddt, dproj, ddtb, dalog, ddsk, dnw = ssd_backward(xc, proj, st_b, dyb_in, dtb_after, alog, dsk, nw, dproj)
    dproj, dconv_w, dconv_b = conv_backward(proj, dxs, dbm, dcm, conv_w, conv_b, dproj)
    if late is not None:
        late(dconv_b)
    t = x.shape[0]
    tail = jnp.concatenate([jnp.sum(ddt, axis=0).astype(BF16), jnp.zeros((t, IN_PAD - COL_DT - LANES), BF16)], axis=1)
    dproj = lax.dynamic_update_slice(dproj, tail, (0, COL_DT))
    dw_in = matmul(u1, dproj, "tn", F32, "in_proj_dw")
    du1 = matmul(dproj, w_in, "nt", F32, "in_proj_dx", after=None if last is None else last(dw_in))

    def mod_bwd(r, c):
        _, vjp = jax.vjp(stage_modulate, r[0], *c)
        dx, dsc, dsh = vjp(r[1])
        return (dx + r[2],), (dsc, dsh)

    grad_x, dsc1, dsh1 = rowwise("modulate1_backward", mod_bwd, [_full(x), _full(du1), _full(dx_a)], [sc1, sh1],
                                 [(D, F32)], [vec, vec])
    d_mod = (dsh1, dsc1, dg1, dsh2, dsc2, dg2)
    d_wts = (dw_in, dw_a, dw_b, dw_o, dw_gu, dw_d)
    d_small = (dlb, dgn, dconv_w, dconv_b, jnp.sum(ddtb, axis=0),
               jnp.sum(dalog, axis=0), ddsk.reshape(1, B_INNER), dnw.reshape(1, B_INNER),
               dln1_g, dln1_b, dln2_g, dln2_b)
    return loss, grad_x, d_mod, d_wts, d_small


HBM = pl.BlockSpec(memory_space=pltpu.HBM)
SEM = pl.BlockSpec(memory_space=pltpu.SEMAPHORE)
DATAFLOW = pltpu.SideEffectType.DATAFLOW_SIDE_EFFECTING


def _place():
    return lax.axis_index("x"), lax.axis_index("y"), lax.axis_index("c")


def _other_chips(x, y):
    return [(1 - x, y), (x, 1 - y), (1 - x, 1 - y)]


def _remote(src, dst, send_sem, recv_sem, device):
    return pltpu.make_async_remote_copy(src_ref=src, dst_ref=dst, send_sem=send_sem, recv_sem=recv_sem,
                                        device_id=device, device_id_type=MESH)


def gather_rows(v, name):
    n = v.shape[1]

    def body(v_ref, out_ref, send_sems, recv_sems, local_sem):
        x, y, c = _place()
        mine = pltpu.make_async_copy(v_ref, out_ref.at[4 * x + 2 * y + c], local_sem)
        mine.start()
        sends, recvs = [], []
        for m in range(1, 8):
            px = 1 - x if m & 4 else x
            py = 1 - y if m & 2 else y
            pc = 1 - c if m & 1 else c
            sends.append(_remote(v_ref, out_ref.at[4 * x + 2 * y + c], send_sems.at[m - 1], recv_sems.at[m - 1], (px, py, pc)))
            recvs.append(_remote(v_ref, out_ref.at[4 * px + 2 * py + pc], send_sems.at[m - 1], recv_sems.at[m - 1], (px, py, pc)))
        for cp in sends:
            cp.start()
        for cp in recvs:
            cp.wait_recv()
        for cp in sends:
            cp.wait_send()
        mine.wait()

    return pl.pallas_call(
        body, name=name, in_specs=[HBM], out_specs=HBM,
        out_shape=jax.ShapeDtypeStruct((8, 1, n), v.dtype),
        scratch_shapes=[pltpu.SemaphoreType.DMA((7,)), pltpu.SemaphoreType.DMA((7,)), pltpu.SemaphoreType.DMA],
    )(v)


def exchange_rows(part, name):
    w = part.shape[2]

    def body(p_ref, out_ref, send_sems, recv_sems, local_sem):
        x, y, c = _place()
        k = 2 * x + y
        mine = pltpu.make_async_copy(p_ref.at[4 * x + 2 * y + c], out_ref.at[k], local_sem)
        mine.start()
        sends, recvs = [], []
        for j, (px, py) in enumerate(_other_chips(x, y)):
            sends.append(_remote(p_ref.at[4 * px + 2 * py + c], out_ref.at[k], send_sems.at[j], recv_sems.at[j], (px, py, c)))
            recvs.append(_remote(p_ref.at[4 * px + 2 * py + c], out_ref.at[2 * px + py], send_sems.at[j], recv_sems.at[j], (px, py, c)))
        for cp in sends:
            cp.start()
        for cp in recvs:
            cp.wait_recv()
        for cp in sends:
            cp.wait_send()
        mine.wait()

    return pl.pallas_call(
        body, name=name, in_specs=[HBM], out_specs=HBM,
        out_shape=jax.ShapeDtypeStruct((4, 1, w), part.dtype),
        scratch_shapes=[pltpu.SemaphoreType.DMA((3,)), pltpu.SemaphoreType.DMA((3,)), pltpu.SemaphoreType.DMA],
    )(part)


def _half_of_slot(ref, rows, px, py, pc):
    return ref.at[2 * px + py, pl.ds(pc * (rows // 2), rows // 2), :]


def gather_start(shards, after, tag):
    n = len(shards)

    def body(*refs):
        w_refs, land_refs = refs[:n], refs[n:2 * n]
        send_sems, recv_sems = refs[2 * n + 1], refs[2 * n + 2]
        token = refs[-1]
        x, y, c = _place()
        for i in range(n):
            rows = shards[i].shape[0]
            for j, (px, py) in enumerate(_other_chips(x, y)):
                _remote(w_refs[i].at[pl.ds(c * (rows // 2), rows // 2), :], _half_of_slot(land_refs[i], rows, x, y, c),
                        send_sems.at[j * n + i], recv_sems.at[j * n + i], (px, py, c)).start()
        token[...] = jnp.zeros_like(token)

    hbm = lambda a: pltpu.with_memory_space_constraint(a, pltpu.HBM)
    lands = [lax.empty((4,) + s.shape, s.dtype) for s in shards]
    dma = pltpu.SemaphoreType.DMA
    return pl.pallas_call(
        body, name="gather_start_" + tag,
        out_shape=(dma((3 * n,)), dma((3 * n,)),
                   *[pltpu.HBM(a.shape, a.dtype) for a in list(shards) + lands], jax.ShapeDtypeStruct((8, LANES), F32)),
        in_specs=[HBM] * (2 * n) + [pl.BlockSpec(memory_space=pl.ANY)],
        out_specs=(SEM, SEM, *[HBM] * (2 * n), pl.BlockSpec(memory_space=pltpu.VMEM)),
        input_output_aliases={i: 2 + i for i in range(2 * n)},
        compiler_params=pltpu.CompilerParams(has_side_effects=DATAFLOW),
    )(*[hbm(a) for a in list(shards) + lands], after)


def gather_wait(send_sems, recv_sems, shards, lands, after, tag):
    n = len(shards)

    def body(*refs):
        w_refs, land_refs = refs[:n], refs[n:2 * n]
        send_ref, recv_ref = refs[2 * n], refs[2 * n + 1]
        x, y, c = _place()
        for i in range(n):
            rows = shards[i].shape[0]
            for j, (px, py) in enumerate(_other_chips(x, y)):
                cp = _remote(w_refs[i].at[pl.ds(c * (rows // 2), rows // 2), :], _half_of_slot(land_refs[i], rows, px, py, c),
                             send_ref.at[j * n + i], recv_ref.at[j * n + i], (px, py, c))
                cp.wait_send()
                cp.wait_recv()

    out = pl.pallas_call(
        body, name="gather_wait_" + tag,
        out_shape=tuple(pltpu.HBM(a.shape, a.dtype) for a in list(shards) + list(lands)),
        in_specs=[HBM] * (2 * n) + [SEM, SEM, pl.BlockSpec(memory_space=pl.ANY)], out_specs=tuple([HBM] * (2 * n)),
        input_output_aliases={i: i for i in range(2 * n)},
        compiler_params=pltpu.CompilerParams(has_side_effects=DATAFLOW),
    )(*shards, *lands, send_sems, recv_sems, after)
    return list(out[:n]), list(out[n:])


def forward_start(lands, tag):
    n = len(lands)

    def body(*refs):
        land_refs = refs[:n]
        send_sems, recv_sems = refs[n], refs[n + 1]
        token = refs[-1]
        x, y, c = _place()
        for i in range(n):
            rows = lands[i].shape[1]
            for j, (px, py) in enumerate(_other_chips(x, y)):
                mine = _half_of_slot(land_refs[i], rows, px, py, c)
                _remote(mine, mine, send_sems.at[j * n + i], recv_sems.at[j * n + i], (x, y, 1 - c)).start()
        token[...] = jnp.zeros_like(token)

    dma = pltpu.SemaphoreType.DMA
    return pl.pallas_call(
        body, name="forward_start_" + tag,
        out_shape=(dma((3 * n,)), dma((3 * n,)), *[pltpu.HBM(a.shape, a.dtype) for a in lands],
                   jax.ShapeDtypeStruct((8, LANES), F32)),
        in_specs=[HBM] * n, out_specs=(SEM, SEM, *[HBM] * n, pl.BlockSpec(memory_space=pltpu.VMEM)),
        input_output_aliases={i: 2 + i for i in range(n)},
        compiler_params=pltpu.CompilerParams(has_side_effects=DATAFLOW),
    )(*lands)


def forward_wait(started, after, tag):
    send_sems, recv_sems, *rest = started
    lands = rest[:-1]
    n = len(lands)

    def body(*refs):
        land_refs = refs[:n]
        send_ref, recv_ref = refs[n], refs[n + 1]
        x, y, c = _place()
        for i in range(n):
            rows = lands[i].shape[1]
            for j, (px, py) in enumerate(_other_chips(x, y)):
                cp = _remote(_half_of_slot(land_refs[i], rows, px, py, c), _half_of_slot(land_refs[i], rows, px, py, 1 - c),
                             send_ref.at[j * n + i], recv_ref.at[j * n + i], (x, y, 1 - c))
                cp.wait_send()
                cp.wait_recv()

    out = pl.pallas_call(
        body, name="forward_wait_" + tag,
        out_shape=tuple(pltpu.HBM(a.shape, a.dtype) for a in lands),
        in_specs=[HBM] * n + [SEM, SEM, pl.BlockSpec(memory_space=pl.ANY)], out_specs=tuple([HBM] * n),
        input_output_aliases={i: i for i in range(n)},
        compiler_params=pltpu.CompilerParams(has_side_effects=DATAFLOW),
    )(*lands, send_sems, recv_sems, after)
    return list(out)


def pair_start(slabs, tag):
    n = len(slabs)

    def body(*refs):
        g_refs, land_refs = refs[:n], refs[n:2 * n]
        send_sems, recv_sems = refs[2 * n], refs[2 * n + 1]
        token = refs[-1]
        x, y, c = _place()
        for i in range(n):
            hr = slabs[i].shape[1] // 2
            _remote(g_refs[i].at[:, pl.ds((1 - c) * hr, hr), :], land_refs[i], send_sems.at[i], recv_sems.at[i],
                    (x, y, 1 - c)).start()
        token[...] = jnp.zeros_like(token)

    hbm = lambda a: pltpu.with_memory_space_constraint(a, pltpu.HBM)
    lands = [lax.empty((4, s.shape[1] // 2, s.shape[2]), s.dtype) for s in slabs]
    dma = pltpu.SemaphoreType.DMA
    return pl.pallas_call(
        body, name="pair_start_" + tag,
        out_shape=(dma((n,)), dma((n,)), *[pltpu.HBM(a.shape, a.dtype) for a in list(slabs) + lands],
                   jax.ShapeDtypeStruct((8, LANES), F32)),
        in_specs=[HBM] * (2 * n), out_specs=(SEM, SEM, *[HBM] * (2 * n), pl.BlockSpec(memory_space=pltpu.VMEM)),
        input_output_aliases={i: 2 + i for i in range(2 * n)},
        compiler_params=pltpu.CompilerParams(has_side_effects=DATAFLOW),
    )(*[hbm(a) for a in list(slabs) + lands])


def pair_wait(started, after, tag):
    send_sems, recv_sems, *rest = started
    n = (len(rest) - 1) // 2
    slabs, lands = rest[:n], rest[n:2 * n]

    def body(*refs):
        g_refs, land_refs = refs[:n], refs[n:2 * n]
        send_ref, recv_ref = refs[2 * n], refs[2 * n + 1]
        x, y, c = _place()
        for i in range(n):
            hr = slabs[i].shape[1] // 2
            cp = _remote(g_refs[i].at[:, pl.ds((1 - c) * hr, hr), :], land_refs[i], send_ref.at[i], recv_ref.at[i], (x, y, 1 - c))
            cp.wait_send()
            cp.wait_recv()

    out = pl.pallas_call(
        body, name="pair_wait_" + tag,
        out_shape=tuple(pltpu.HBM(a.shape, a.dtype) for a in list(slabs) + list(lands)),
        in_specs=[HBM] * (2 * n) + [SEM, SEM, pl.BlockSpec(memory_space=pl.ANY)], out_specs=tuple([HBM] * (2 * n)),
        input_output_aliases={i: i for i in range(2 * n)},
        compiler_params=pltpu.CompilerParams(has_side_effects=DATAFLOW),
    )(*slabs, *lands, send_sems, recv_sems, after)
    return list(out[:n]), list(out[n:])


def _tile2(rows, cols):
    fits = lambda r, c: r * c * 4 <= BLOCK_BYTES
    if fits(rows, cols):
        return rows, cols
    for r in (1024, 512, 256, 128, 64):
        if rows % r == 0 and fits(r, cols):
            return r, cols
    return rows, next(cols // k for k in (2, 3, 4, 6, 8, 12, 16) if cols % (k * LANES) == 0 and fits(rows, cols // k))


def pair_add(g, p, c, name):
    _, hr, cols = p.shape
    tm, tc = _tile2(hr, cols)
    per = hr // tm

    def body(c_ref, g_ref, p_ref, o_ref):
        o_ref[...] = (g_ref[...] + p_ref[...]).astype(o_ref.dtype)

    return pl.pallas_call(
        body, name=name,
        grid_spec=pltpu.PrefetchScalarGridSpec(
            num_scalar_prefetch=1, grid=(4, per, cols // tc),
            in_specs=[pl.BlockSpec((None, tm, tc), lambda k, i, j, c_ref: (k, c_ref[0] * per + i, j)),
                      pl.BlockSpec((None, tm, tc), lambda k, i, j, c_ref: (k, i, j))],
            out_specs=pl.BlockSpec((None, tm, tc), lambda k, i, j, c_ref: (k, i, j))),
        out_shape=jax.ShapeDtypeStruct((4, hr, cols), BF16),
        compiler_params=_params(("arbitrary", "arbitrary", "arbitrary")),
    )(c.reshape(1).astype(jnp.int32), g, p)


def scatter_start(sums, tag):
    n = len(sums)

    def body(*refs):
        s_refs, land_refs = refs[:n], refs[n:2 * n]
        send_sems, recv_sems = refs[2 * n], refs[2 * n + 1]
        token = refs[-1]
        x, y, c = _place()
        k = 2 * x + y
        for i in range(n):
            for j, (px, py) in enumerate(_other_chips(x, y)):
                _remote(s_refs[i].at[2 * px + py], land_refs[i].at[k], send_sems.at[j * n + i], recv_sems.at[j * n + i],
                        (px, py, c)).start()
        token[...] = jnp.zeros_like(token)

    hbm = lambda a: pltpu.with_memory_space_constraint(a, pltpu.HBM)
    return pl.pallas_call(
        body, name="scatter_start_" + tag,
        out_shape=(pltpu.SemaphoreType.DMA((3 * n,)), pltpu.SemaphoreType.DMA((3 * n,)),
                   *[pltpu.HBM(s.shape, s.dtype) for s in sums], *[pltpu.HBM(s.shape, s.dtype) for s in sums],
                   jax.ShapeDtypeStruct((8, LANES), F32)),
        in_specs=[HBM] * (2 * n), out_specs=(SEM, SEM, *[HBM] * (2 * n), pl.BlockSpec(memory_space=pltpu.VMEM)),
        input_output_aliases={i: 2 + i for i in range(2 * n)},
        compiler_params=pltpu.CompilerParams(has_side_effects=DATAFLOW),
    )(*[hbm(s) for s in sums], *[hbm(lax.empty(s.shape, s.dtype)) for s in sums])


def scatter_wait(started, after, tag):
    send_sems, recv_sems, *rest = started
    n = (len(rest) - 1) // 2
    sums, lands = rest[:n], rest[n:2 * n]

    def body(*refs):
        s_refs, land_refs = refs[:n], refs[n:2 * n]
        send_ref, recv_ref = refs[2 * n], refs[2 * n + 1]
        x, y, c = _place()
        for i in range(n):
            for j, (px, py) in enumerate(_other_chips(x, y)):
                cp = _remote(s_refs[i].at[2 * px + py], land_refs[i].at[2 * px + py], send_ref.at[j * n + i],
                             recv_ref.at[j * n + i], (px, py, c))
                cp.wait_send()
                cp.wait_recv()

    out = pl.pallas_call(
        body, name="scatter_wait_" + tag,
        out_shape=tuple(pltpu.HBM(s.shape, s.dtype) for s in sums + lands),
        in_specs=[HBM] * (2 * n) + [SEM, SEM, pl.BlockSpec(memory_space=pl.ANY)], out_specs=tuple([HBM] * (2 * n)),
        input_output_aliases={i: i for i in range(2 * n)},
        compiler_params=pltpu.CompilerParams(has_side_effects=DATAFLOW),
    )(*sums, *lands, send_sems, recv_sems, after)
    return list(out[:n]), list(out[n:])


def sum_chips(landed, own, chip, core, name):
    _, hr, cols = landed.shape
    tm, tc = _tile2(hr, cols)
    per = hr // tm

    def body(idx_ref, l0, l1, l2, l3, own_ref, o_ref):
        mine = own_ref[...].astype(F32)
        v = [jnp.where(idx_ref[0] == k, mine, ref[...].astype(F32)) for k, ref in enumerate((l0, l1, l2, l3))]
        o_ref[...] = ((v[0] + v[1]) + v[2]) + v[3]

    slot = lambda k: pl.BlockSpec((None, tm, tc),
                                  lambda i, j, idx: (jnp.where(idx[0] == k, (k + 1) & 3, k), i, j))
    return pl.pallas_call(
        body, name=name,
        grid_spec=pltpu.PrefetchScalarGridSpec(
            num_scalar_prefetch=1, grid=(per, cols // tc),
            in_specs=[slot(0), slot(1), slot(2), slot(3),
                      pl.BlockSpec((None, tm, tc), lambda i, j, idx: (idx[0], i, j))],
            out_specs=pl.BlockSpec((tm, tc), lambda i, j, idx: (idx[1] * per + i, j))),
        out_shape=jax.ShapeDtypeStruct((2 * hr, cols), F32),
        compiler_params=_params(("arbitrary", "arbitrary")),
    )(jnp.stack([chip, core]).astype(jnp.int32), landed, landed, landed, landed, own)


def exchange_halves(bufs):
    n = len(bufs)

    def body(*refs):
        out_refs = refs[n:2 * n]
        send_sems, recv_sems = refs[2 * n:]
        x, y, c = _place()
        sends, recvs = [], []
        for i in range(n):
            hr = bufs[i].shape[0] // 2
            own = out_refs[i].at[pl.ds(c * hr, hr), :]
            other = out_refs[i].at[pl.ds((1 - c) * hr, hr), :]
            sends.append(_remote(own, own, send_sems.at[i], recv_sems.at[i], (x, y, 1 - c)))
            recvs.append(_remote(other, other, send_sems.at[i], recv_sems.at[i], (x, y, 1 - c)))
        for cp in sends:
            cp.start()
        for cp in recvs:
            cp.wait_recv()
        for cp in sends:
            cp.wait_send()

    return pl.pallas_call(
        body, name="exchange_halves", in_specs=[HBM] * n, out_specs=[HBM] * n,
        out_shape=[jax.ShapeDtypeStruct(b.shape, b.dtype) for b in bufs],
        input_output_aliases={i: i for i in range(n)},
        scratch_shapes=[pltpu.SemaphoreType.DMA((n,)), pltpu.SemaphoreType.DMA((n,))],
    )(*bufs)


def assemble_in_proj(landed, own, chip):
    rows, cols = 128, own.shape[1]

    def body(idx_ref, l0, l1, l2, l3, own_ref, o_ref):
        mine = own_ref[...]
        w = jnp.concatenate([jnp.where(idx_ref[0] == k, mine, ref[...]) for k, ref in enumerate((l0, l1, l2, l3))], axis=1)
        o_ref[...] = jnp.concatenate([w[:, :ORIG_Z], w[:, ORIG_GA:], w[:, ORIG_XBC:ORIG_DT], w[:, ORIG_Z:ORIG_XBC],
                                      w[:, ORIG_DT:ORIG_GA], jnp.zeros((rows, IN_PAD - IN_ORIG), w.dtype)], axis=1)

    slot = lambda k: pl.BlockSpec((None, rows, cols), lambda i, idx: (jnp.where(idx[0] == k, (k + 1) & 3, k), i, 0))
    return pl.pallas_call(
        body, name="assemble_in_proj",
        grid_spec=pltpu.PrefetchScalarGridSpec(
            num_scalar_prefetch=1, grid=(D // rows,),
            in_specs=[slot(0), slot(1), slot(2), slot(3), pl.BlockSpec((rows, cols), lambda i, idx: (i, 0))],
            out_specs=pl.BlockSpec((rows, IN_PAD), lambda i, idx: (i, 0))),
        out_shape=jax.ShapeDtypeStruct((D, IN_PAD), own.dtype),
        compiler_params=_params(("arbitrary",)),
    )(chip.reshape(1).astype(jnp.int32), landed, landed, landed, landed, own)


def rows_exchange(a, name):
    hr = a.shape[0] // 2

    def body(a_ref, out_ref, send_sem, recv_sem):
        x, y, c = _place()
        cp = _remote(a_ref.at[pl.ds((1 - c) * hr, hr), :], out_ref, send_sem, recv_sem, (x, y, 1 - c))
        cp.start()
        cp.wait()

    return pl.pallas_call(
        body, name=name, in_specs=[HBM], out_specs=HBM,
        out_shape=jax.ShapeDtypeStruct((hr, a.shape[1]), a.dtype),
        scratch_shapes=[pltpu.SemaphoreType.DMA, pltpu.SemaphoreType.DMA],
    )(a)


def split_pair_add(dw, received, core):
    cols = IN_ORIG // 4
    rows, hr = 128, D // 2
    per = hr // rows

    def body(c_ref, own_ref, got_ref, o_ref):
        d = own_ref[...] + got_ref[...]
        w = jnp.concatenate([d[:, :COL_GA], d[:, COL_Z:COL_DT], d[:, COL_XBC:COL_Z], d[:, COL_DT:COL_DT + 32],
                             d[:, COL_GA:COL_XBC]], axis=1)
        for k in range(4):
            o_ref[k] = w[:, k * cols:(k + 1) * cols].astype(o_ref.dtype)

    return pl.pallas_call(
        body, name="split_pair_add",
        grid_spec=pltpu.PrefetchScalarGridSpec(
            num_scalar_prefetch=1, grid=(per,),
            in_specs=[pl.BlockSpec((rows, IN_PAD), lambda i, c_ref: (c_ref[0] * per + i, 0)),
                      pl.BlockSpec((rows, IN_PAD), lambda i, c_ref: (i, 0))],
            out_specs=pl.BlockSpec((4, rows, cols), lambda i, c_ref: (0, i, 0))),
        out_shape=jax.ShapeDtypeStruct((4, hr, cols), BF16),
        compiler_params=_params(("arbitrary",)),
    )(core.reshape(1).astype(jnp.int32), dw, received)


def ada_prepare(c_all, w_ada, hgrn_lb):
    def body(c_ref, w_ref, lb_ref, mod_ref, row_ref):
        mod_ref[...] = hdot(silu(c_ref[...]), w_ref[...])
        row_ref[...] = sigmoid(lb_ref[0:1, :] - lb_ref[1:2, :])

    return pl.pallas_call(
        body, name="ada_prepare",
        out_shape=[jax.ShapeDtypeStruct((8, w_ada.shape[1]), F32), jax.ShapeDtypeStruct((1, D), F32)],
        compiler_params=pltpu.CompilerParams(vmem_limit_bytes=VMEM_LIMIT),
    )(c_all, w_ada, hgrn_lb)


SMALL_SEGS = (("mod", 6 * D), ("lb", D), ("gnorm", LANES), ("conv_w", 4 * CONV_DIM), ("conv_b", CONV_DIM),
              ("dt_bias", LANES), ("a_log", LANES), ("d", B_INNER), ("ssm_norm", B_INNER),
              ("ln1_g", D), ("ln1_b", D), ("ln2_g", D), ("ln2_b", D), ("loss", LANES))
SMALL_PARAMS = ("b_ada", "hgrn_lb", "hgrn_gnorm", "ssm_conv_b", "ssm_dt_bias", "ssm_a_log", "ssm_d", "ssm_norm",
                "ln1_g", "ln1_b", "ln2_g", "ln2_b")


def finalize_small(g_all, c_all, dmod_cols, params, m, v):
    n_p = len(SMALL_PARAMS)
    offs, o = {}, 0
    for nm, width in SMALL_SEGS:
        offs[nm] = (o, width)
        o += width

    def body(*refs):
        g_ref, c_ref, dm_ref = refs[:3]
        p_refs = refs[3:3 + n_p]
        m_refs = refs[3 + n_p:3 + 2 * n_p]
        v_refs = refs[3 + 2 * n_p:3 + 3 * n_p]
        outs = refs[3 + 3 * n_p:]
        gwa_ref, gcw_ref, loss_ref = outs[:3]
        res = outs[3:]
        total = jnp.sum(g_ref[...], axis=0, keepdims=True)
        seg = lambda nm: total[:, offs[nm][0]:offs[nm][0] + offs[nm][1]]
        loss_ref[...] = seg("loss")
        gwa_ref[...] = hdot(silu(c_ref[...]), dm_ref[...], "tn")
        cw = seg("conv_w")
        for j in range(4):
            gcw_ref[j:j + 1, :] = cw[:, j * CONV_DIM:(j + 1) * CONV_DIM]
        hc = lax.broadcasted_iota(jnp.int32, (B_INNER, LANES), 0)
        hj = lax.broadcasted_iota(jnp.int32, (B_INNER, LANES), 1)
        per_head = ((hc >> 6) == hj).astype(F32)
        heads = lambda nm: hdot(jnp.broadcast_to(seg(nm), (8, B_INNER)), per_head)[0:1, 0:32]
        lbp = sigmoid(p_refs[1][0:1, :] - p_refs[1][1:2, :])
        g_row = seg("lb") * lbp * (1.0 - lbp)
        grads = {"b_ada": seg("mod"), "hgrn_gnorm": seg("gnorm"), "ssm_conv_b": seg("conv_b"),
                 "ssm_dt_bias": seg("dt_bias")[:, 0:32], "ssm_a_log": seg("a_log")[:, 0:32], "ssm_d": heads("d"),
                 "ssm_norm": seg("ssm_norm"), "ln1_g": seg("ln1_g"), "ln1_b": seg("ln1_b"),
                 "ln2_g": seg("ln2_g"), "ln2_b": seg("ln2_b")}
        for i, nm in enumerate(SMALL_PARAMS):
            g_out, d_out, m_out, v_out = res[4 * i:4 * i + 4]
            if nm == "hgrn_lb":
                for row, gv in ((0, g_row), (1, -g_row)):
                    sl = slice(row, row + 1)
                    dl, mn, vn = adamw(p_refs[i][sl, :], gv, m_refs[i][sl, :], v_refs[i][sl, :])
                    g_out[sl, :], d_out[sl, :], m_out[sl, :], v_out[sl, :] = gv, dl, mn, vn
            else:
                gv = grads[nm]
                dl, mn, vn = adamw(p_refs[i][...], gv, m_refs[i][...], v_refs[i][...])
                g_out[...], d_out[...], m_out[...], v_out[...] = gv, dl, mn, vn

    out_shape = [jax.ShapeDtypeStruct((D, dmod_cols.shape[1]), F32), jax.ShapeDtypeStruct((4, CONV_DIM), F32),
                 jax.ShapeDtypeStruct((1, LANES), F32)]
    for p in params:
        out_shape += [jax.ShapeDtypeStruct(p.shape, F32)] * 4
    return pl.pallas_call(
        body, name="finalize_small", out_shape=out_shape,
        compiler_params=pltpu.CompilerParams(vmem_limit_bytes=VMEM_LIMIT),
    )(g_all, c_all, dmod_cols, *params, *m, *v)


def adam_update(w, g, m, v, name):
    rows, cols = w.shape
    tm, tc = _tile2(rows, cols)

    def body(w_ref, g_ref, m_ref, v_ref, d_ref, mo_ref, vo_ref):
        d_ref[...], mo_ref[...], vo_ref[...] = adamw(w_ref[...], g_ref[...], m_ref[...], v_ref[...])

    spec = pl.BlockSpec((tm, tc), lambda i, j: (i, j))
    return pl.pallas_call(
        body, name=name, grid=(rows // tm, cols // tc), in_specs=[spec] * 4, out_specs=[spec] * 3,
        out_shape=[jax.ShapeDtypeStruct((rows, cols), F32)] * 3,
        compiler_params=_params(("arbitrary", "arbitrary")),
    )(w, g, m, v)


def kernel(x, c, w_ada, b_ada, w_in, hgrn_lb, hgrn_gnorm, ssm_conv_w, ssm_conv_b, ssm_dt_bias, ssm_a_log, ssm_d, ssm_norm, w_branch_a, w_branch_b, w_o, ln1_g, ln1_b, w_ffn_gate, w_ffn_up, w_ffn_down, ln2_g, ln2_b, loss_target, m_w_ada, m_b_ada, m_w_in, m_hgrn_lb, m_hgrn_gnorm, m_ssm_conv_w, m_ssm_conv_b, m_ssm_dt_bias, m_ssm_a_log, m_ssm_d, m_ssm_norm, m_w_branch_a, m_w_branch_b, m_w_o, m_ln1_g, m_ln1_b, m_w_ffn_gate, m_w_ffn_up, m_w_ffn_down, m_ln2_g, m_ln2_b, v_w_ada, v_b_ada, v_w_in, v_hgrn_lb, v_hgrn_gnorm, v_ssm_conv_w, v_ssm_conv_b, v_ssm_dt_bias, v_ssm_a_log, v_ssm_d, v_ssm_norm, v_w_branch_a, v_w_branch_b, v_w_o, v_ln1_g, v_ln1_b, v_w_ffn_gate, v_w_ffn_up, v_w_ffn_down, v_ln2_g, v_ln2_b):
    given = dict(locals())
    chip = 2 * lax.axis_index("x") + lax.axis_index("y")
    core = lax.axis_index("c")
    t = x.shape[1]

    first = gather_rows(jnp.concatenate([c, ssm_conv_w.reshape(1, CONV_DIM)], axis=1), "gather_cond").reshape(8, D + CONV_DIM)
    c_all = first[:, :D]
    conv_w = first[0::2, D:].reshape(4, 4, CONV_DIM // 4).transpose(1, 0, 2).reshape(4, CONV_DIM)
    mod_part, lb_row = ada_prepare(c_all, w_ada[0], hgrn_lb)
    mod_cols = w_ada.shape[2]
    mod_row = exchange_rows(mod_part.reshape(8, 1, mod_cols), "exchange_mod").reshape(1, 6 * D) + b_ada

    local = {nm: given[nm][0] for nm in SHARDED if nm != "w_ffn_in"}
    local["w_ffn_in"] = jnp.concatenate([w_ffn_gate[0].T, w_ffn_up[0].T], axis=0)
    shards = [local[nm].astype(BF16) for nm in SHARDED]
    send_in, recv_in, sent_in, land_in, started_in = gather_start(shards[:1], mod_row, "in")
    shards = shards[:1] + [(local[nm] + started_in[0, 0]).astype(BF16) for nm in SHARDED[1:]]
    send_rest, recv_rest, *flying = gather_start(shards[1:], started_in, "rest")
    n_rest = len(SHARDED) - 1
    sent_rest, land_rest, started_rest = flying[:n_rest], flying[n_rest:2 * n_rest], flying[-1]
    mod_row = mod_row + started_rest[0:1, 0:1]
    mod = tuple(mod_row[:, i * D:(i + 1) * D] for i in range(6))
    with_own = lambda land, shard: lax.dynamic_update_slice(land, shard[None], (chip, 0, 0))

    class Weights:
        def input_projection(self, after):
            (own,), land = gather_wait(send_in, recv_in, [sent_in], [land_in], after, "in")
            (land,) = forward_wait(forward_start(land, "in"), after, "in")
            return assemble_in_proj(land, own, chip)

        def start_rest(self, after):
            self.own, landed = gather_wait(send_rest, recv_rest, sent_rest, land_rest, after, "rest")
            self.started = forward_start(landed, "rest")
            return self.started[-1]

        def rest(self, after):
            got = {nm: with_own(land, s) for nm, land, s in zip(SHARDED[1:], forward_wait(self.started, after, "rest"), self.own, strict=True)}
            whole = lambda nm: got[nm].reshape(4 * got[nm].shape[1], got[nm].shape[2])
            return tuple(whole(nm) for nm in SHARDED[1:])

    wts = Weights()

    per_head = lambda p: jnp.pad(p, ((0, 0), (0, LANES - p.shape[1])))
    small = (lb_row, hgrn_gnorm, conv_w, ssm_conv_b, per_head(ssm_dt_bias), per_head(ssm_a_log),
             jnp.repeat(ssm_d[0], B_INNER // 32)[None], ssm_norm, ln1_g, ln1_b, ln2_g, ln2_b)
    by_rows = lambda g: g.reshape(4, g.shape[0] // 4, g.shape[1])
    travelling = {}

    def start_early(dws):
        travelling["pair"] = pair_start([by_rows(dw) for dw in dws], "early")
        return travelling["pair"][-1]

    def between_scans(after):
        slabs, received = pair_wait(travelling["pair"], after, "early")
        travelling["pairs"] = [pair_add(s, r, core, "pair_add_" + nm) for nm, s, r in zip(SHARDED[1:], slabs, received, strict=True)]
        travelling["started"] = scatter_start(travelling["pairs"], "early")
        return travelling["started"][-1]

    def finish_early(after):
        travelling["pairs"], travelling["landed"] = scatter_wait(travelling["started"], after, "early")

    def start_last(dw_in):
        travelling["pairs_in"] = [split_pair_add(dw_in, rows_exchange(dw_in, "pair_exchange_last"), core)]
        travelling["started_in"] = scatter_start(travelling["pairs_in"], "last")
        return travelling["started_in"][-1]

    loss, grad_x, d_mod, d_wts, d_small = local_step(x[0], loss_target[0], mod, wts, small,
                                                     start_early, between_scans, finish_early, start_last)

    d_lb, d_gn, d_cw, d_cb, d_dtb, d_alog, d_dsk, d_nw, d_l1g, d_l1b, d_l2g, d_l2b = d_small
    row = jnp.concatenate(list(d_mod) + [d_lb, d_gn, d_cw.reshape(1, 4 * CONV_DIM), d_cb, d_dtb, d_alog, d_dsk, d_nw,
                                          d_l1g, d_l1b, d_l2g, d_l2b, jnp.pad(loss, ((0, 0), (0, LANES - 1)))], axis=1)
    g_all = gather_rows(row, "gather_small_grads").reshape(8, row.shape[1])
    dmod_cols = lax.dynamic_slice_in_dim(g_all, chip * mod_cols, mod_cols, axis=1)
    fin = finalize_small(g_all, c_all, dmod_cols, [given[n] for n in SMALL_PARAMS],
                         [given["m_" + n] for n in SMALL_PARAMS], [given["v_" + n] for n in SMALL_PARAMS])
    grads, deltas, new_m, new_v = {}, {}, {}, {}
    grads["w_ada"] = fin[0][None]
    grads["ssm_conv_w"] = lax.dynamic_slice_in_dim(fin[1], chip * (CONV_DIM // 4), CONV_DIM // 4, axis=1)[None]
    for i, nm in enumerate(SMALL_PARAMS):
        grads[nm], deltas[nm], new_m[nm], new_v[nm] = fin[3 + 4 * i:7 + 4 * i]

    pairs_in, landed_in = scatter_wait(travelling["started_in"], fin[3], "last")
    pairs, landed = pairs_in + travelling["pairs"], landed_in + travelling["landed"]
    halves = [sum_chips(r, p, chip, core, "sum_chips_" + nm) for nm, r, p in zip(SHARDED, landed, pairs, strict=True)]
    reduced = dict(zip(SHARDED, exchange_halves(halves), strict=True))
    reduced["w_ada"], reduced["ssm_conv_w"] = grads["w_ada"][0], grads["ssm_conv_w"][0]
    reduced["w_in"] = reduced["w_in"].T
    reduced["w_ffn_gate"], reduced["w_ffn_up"] = reduced["w_ffn_in"][:FFN_SHARD], reduced["w_ffn_in"][FFN_SHARD:]
    for nm in ("w_ada", "ssm_conv_w", "w_in", "w_branch_a", "w_branch_b", "w_o", "w_ffn_gate", "w_ffn_up", "w_ffn_down"):
        flipped = nm in ("w_in", "w_ffn_gate", "w_ffn_up")
        work = (lambda a: a[0].T) if flipped else (lambda a: a[0])
        back = (lambda a: a.T[None]) if flipped else (lambda a: a[None])
        d_, m_, v_ = adam_update(work(given[nm]), reduced[nm], work(given["m_" + nm]), work(given["v_" + nm]), "adam_" + nm)
        grads[nm], deltas[nm], new_m[nm], new_v[nm] = back(reduced[nm]), back(d_), back(m_), back(v_)

    names = ("w_ada", "b_ada", "w_in", "hgrn_lb", "hgrn_gnorm", "ssm_conv_w", "ssm_conv_b", "ssm_dt_bias", "ssm_a_log",
             "ssm_d", "ssm_norm", "w_branch_a", "w_branch_b", "w_o", "ln1_g", "ln1_b", "w_ffn_gate", "w_ffn_up",
             "w_ffn_down", "ln2_g", "ln2_b")
    return (fin[2][0, 0], grad_x[None], *[grads[n] for n in names], *[deltas[n] for n in names],
            *[new_m[n] for n in names], *[new_v[n] for n in names])
```

```python
import functools

import jax
import jax.numpy as jnp
from jax import lax
from jax.experimental import pallas as pl
from jax.experimental.pallas import tpu as pltpu

F32, BF16 = jnp.float32, jnp.bfloat16
HI = lax.Precision.HIGHEST
MESH = pl.DeviceIdType.MESH

D = 1024
CHUNK = 64
LANES = 128
N_HEADS_A = 8
N_GROUPS_B = 4
B_INNER = 2048
CONV_DIM = 3072
D_FF = 2816
ALPHA = 2.0 ** 0.25
LN_EPS = 1e-5
RMS_EPS = 1e-6
ADAM_LR, ADAM_B1, ADAM_B2, ADAM_EPS, ADAM_WD, ADAM_STEP = 0.001, 0.9, 0.999, 1e-08, 0.01, 10

IN_ORIG = 11296
IN_PAD = 11520
COL_GA, COL_GB, COL_XBC, COL_Z, COL_DT = 4096, 5120, 6144, 9216, 11264
ORIG_Z, ORIG_XBC, ORIG_DT, ORIG_GA = 4096, 6144, 9216, 9248

SHARDED = ("w_in", "w_branch_a", "w_branch_b", "w_o", "w_ffn_in", "w_ffn_down")
FFN_SHARD = D_FF // 4
VMEM_LIMIT = 56 * 1024 * 1024
BLOCK_BYTES = 2 * 1024 * 1024
_DIMS = {"nn": (((1,), (0,)), ((), ())), "nt": (((1,), (1,)), ((), ())), "tn": (((0,), (0,)), ((), ()))}


def _bd(a, b, mode):
    return lax.dot_general(a.astype(BF16), b.astype(BF16), _DIMS[mode], preferred_element_type=F32)


@functools.partial(jax.custom_vjp, nondiff_argnums=(2,))
def bdot(a, b, mode):
    return _bd(a, b, mode)


def _bdot_fwd(a, b, mode):
    return _bd(a, b, mode), (a, b)


def _bdot_bwd(mode, res, g):
    a, b = res
    if mode == "nn":
        return _bd(g, b, "nt"), _bd(a, g, "tn")
    if mode == "nt":
        return _bd(g, b, "nn"), _bd(g, a, "tn")
    return _bd(b, g, "nt"), _bd(a, g, "nn")


bdot.defvjp(_bdot_fwd, _bdot_bwd)


def hdot(a, b, mode="nn"):
    return lax.dot_general(a, b, _DIMS[mode], precision=HI, preferred_element_type=F32)


def _raw(a, b, mode):
    return lax.dot_general(a, b, _DIMS[mode], preferred_element_type=F32)


def _split(x, n):
    parts, rest = [], x
    for _ in range(n):
        p = rest.astype(BF16)
        parts.append(p)
        rest = rest - p.astype(F32)
    return parts


def _od(a, b, mode, exact):
    if exact == 1:
        e = b.astype(BF16)
        p = _split(a, 3)
        return (_raw(p[2], e, mode) + _raw(p[1], e, mode)) + _raw(p[0], e, mode)
    e = a.astype(BF16)
    p = _split(b, 3)
    return (_raw(e, p[2], mode) + _raw(e, p[1], mode)) + _raw(e, p[0], mode)


@functools.partial(jax.custom_vjp, nondiff_argnums=(2, 3))
def odot(a, b, mode, exact):
    return _od(a, b, mode, exact)


def _odot_fwd(a, b, mode, exact):
    return _od(a, b, mode, exact), (a, b)


def _odot_bwd(mode, exact, res, g):
    a, b = res
    if exact == 1:
        da = {"nn": lambda: _od(g, b, "nt", 1), "nt": lambda: _od(g, b, "nn", 1), "tn": lambda: _od(b, g, "nt", 0)}[mode]()
        return da, jnp.zeros_like(b)
    db = {"nn": lambda: _od(a, g, "tn", 0), "nt": lambda: _od(g, a, "tn", 1), "tn": lambda: _od(a, g, "nn", 0)}[mode]()
    return jnp.zeros_like(a), db


odot.defvjp(_odot_fwd, _odot_bwd)


_BDIMS = {"bnn": (((2,), (1,)), ((0,), (0,))), "bnt": (((2,), (2,)), ((0,), (0,))), "btn": (((1,), (1,)), ((0,), (0,)))}


def _braw(a, b, mode):
    return lax.dot_general(a, b, _BDIMS[mode], preferred_element_type=F32)


def _bdb(a, b, mode):
    return _braw(a.astype(BF16), b.astype(BF16), mode)


def _d3b(a, b, mode):
    ah, al = _split(a, 2)
    bh, bl = _split(b, 2)
    return _braw(ah, bh, mode) + (_braw(ah, bl, mode) + _braw(al, bh, mode))


def _batched_bwd(f):
    def bwd(mode, res, g):
        a, b = res
        if mode == "bnn":
            return f(g, b, "bnt"), f(a, g, "btn")
        if mode == "bnt":
            return f(g, b, "bnn"), f(g, a, "btn")
        return f(b, g, "bnt"), f(a, g, "bnn")
    return bwd


@functools.partial(jax.custom_vjp, nondiff_argnums=(2,))
def bdot_b(a, b, mode):
    return _bdb(a, b, mode)


bdot_b.defvjp(lambda a, b, mode: (_bdb(a, b, mode), (a, b)), _batched_bwd(_bdb))


@functools.partial(jax.custom_vjp, nondiff_argnums=(2,))
def dot3_b(a, b, mode):
    return _d3b(a, b, mode)


dot3_b.defvjp(lambda a, b, mode: (_d3b(a, b, mode), (a, b)), _batched_bwd(_d3b))


def _cum(tril3, x, mode):
    e = tril3.astype(BF16)
    p = _split(x, 3)
    return (_braw(e, p[2], mode) + _braw(e, p[1], mode)) + _braw(e, p[0], mode)


@jax.custom_vjp
def chunk_cumsum(tril3, x):
    return _cum(tril3, x, "bnn")


chunk_cumsum.defvjp(lambda t, x: (_cum(t, x, "bnn"), t), lambda t, g: (jnp.zeros_like(t), _cum(t, g, "btn")))


def _unstack(axis, n):
    @jax.custom_vjp
    def un(x):
        return tuple(lax.index_in_dim(x, i, axis, keepdims=False) for i in range(n))

    un.defvjp(lambda x: (un(x), None), lambda _, g: (jnp.stack(g, axis=axis),))
    return un


def _split_last(n, w):
    @jax.custom_vjp
    def sp(x):
        return tuple(x[..., i * w:(i + 1) * w] for i in range(n))

    sp.defvjp(lambda x: (sp(x), None), lambda _, g: (jnp.concatenate(g, axis=-1),))
    return sp


def sigmoid(x):
    return 1.0 / (1.0 + jnp.exp(-x))


def silu(x):
    return x * sigmoid(x)


def softplus(x):
    return jnp.maximum(x, 0.0) + jnp.log1p(jnp.exp(jnp.minimum(x, -x)))


def _ln(x):
    mu = jnp.mean(x, axis=-1, keepdims=True)
    xc = x - mu
    return xc * lax.rsqrt(jnp.mean(xc * xc, axis=-1, keepdims=True) + LN_EPS)


def _tril64():
    r = lax.broadcasted_iota(jnp.int32, (CHUNK, CHUNK), 0)
    c = lax.broadcasted_iota(jnp.int32, (CHUNK, CHUNK), 1)
    return (r >= c).astype(F32)


def hgrn_block(q, fl, iv, gr, st, lb, gn):
    tb = q.shape[0]
    nc = tb // CHUNK
    nh = N_HEADS_A
    heads = _split_last(nh, LANES)
    to4 = lambda a: jnp.stack(heads(a), axis=0).reshape(nh, nc, CHUNK, LANES)
    flat = lambda a: a.reshape(nh * nc, CHUNK, LANES)
    f = lb + (1.0 - lb) * sigmoid(fl)
    gl4, k4, qf4, v4, gr4 = to4(jnp.log(f)), to4(1.0 - f), to4(silu(q) * (128 ** -0.5)), to4(iv), to4(gr)
    tril = _tril64()
    b4 = chunk_cumsum(jnp.broadcast_to(tril[None], (nh * nc, CHUNK, CHUNK)), flat(gl4)).reshape(gl4.shape)
    blast = jnp.sum(gl4, axis=2, keepdims=True)
    ref = lax.stop_gradient(0.5 * blast)
    sc = dot3_b(flat(qf4 * jnp.exp(b4 - ref)), flat(k4 * jnp.exp(ref - b4)), "bnt") * tril
    o_intra = bdot_b(sc, flat(v4), "bnn").reshape(gl4.shape)
    chunks = _unstack(1, nc)
    qe, v_c, kd, dec = chunks(qf4 * jnp.exp(b4)), chunks(v4), chunks(k4 * jnp.exp(blast - b4)), chunks(jnp.exp(blast))
    o_inter = []
    for c in range(nc):
        o_inter.append(bdot_b(qe[c], st, "bnt"))
        st = st * dec[c] + bdot_b(v_c[c], kd[c], "btn")
    o = o_intra + jnp.stack(o_inter, axis=1)
    on = o * lax.rsqrt(jnp.mean(o * o, axis=-1, keepdims=True) + RMS_EPS) * gn
    out = (on * silu(gr4)).reshape(nh, tb, LANES)
    return jnp.concatenate(_unstack(0, nh)(out), axis=1), st


def ssd_consts(g):
    i32 = jnp.int32
    ej = lax.broadcasted_iota(i32, (LANES, 512), 0)
    ec = lax.broadcasted_iota(i32, (LANES, 512), 1)
    expand = (ej == g * 8 + (ec >> 6)).astype(F32)
    ts = lax.broadcasted_iota(i32, (CHUNK, 512), 0)
    tc = lax.broadcasted_iota(i32, (CHUNK, 512), 1)
    itile = (ts == (tc & 63)).astype(F32)
    maskall = ts >= (tc & 63)
    br = lax.broadcasted_iota(i32, (LANES, LANES), 0)
    bc = lax.broadcasted_iota(i32, (LANES, LANES), 1)
    blockmask = ((br >> 6) == (bc >> 6)).astype(F32)
    return expand, itile, maskall, blockmask, _tril64()


def ssd_block(x, bm, cm, dt, z, st, dtb, alog, dsk, nw, cs):
    expand, itile, maskall, blockmask, tril = cs
    tb = x.shape[0]
    nc = tb // CHUNK
    delta = odot(softplus(dt + dtb), expand, "nn", 1)
    a = -jnp.exp(alog) * delta
    xdt = x * delta
    by_chunk = lambda v: v.reshape(nc, CHUNK, v.shape[-1])
    a3, xdt3, bm3, cm3 = by_chunk(a), by_chunk(xdt), by_chunk(bm), by_chunk(cm)
    acum3 = chunk_cumsum(jnp.broadcast_to(tril[None], (nc, CHUNK, CHUNK)), a3)
    alast3 = jnp.sum(a3, axis=1, keepdims=True)
    cb3 = bdot_b(cm3, jnp.concatenate([bm3] * 8, axis=1), "bnt")
    arow3 = jnp.sum(acum3 * itile, axis=1, keepdims=True)
    dec3 = jnp.exp(jnp.where(maskall, acum3 - arow3, -1e30))
    pairs = _split_last(4, LANES)
    intra = [bdot_b(m, jnp.concatenate([xp] * 2, axis=1) * blockmask, "bnn")
             for m, xp in zip(pairs(cb3 * dec3), pairs(xdt3))]
    chunks = _unstack(0, nc)
    cm_c, bm_c, xw_c, dec_c = chunks(cm3), chunks(bm3), chunks(xdt3 * jnp.exp(alast3 - acum3)), chunks(jnp.exp(alast3))
    inter = []
    for c in range(nc):
        inter.append(bdot(cm_c[c], st, "nn"))
        st = st * dec_c[c] + bdot(bm_c[c], xw_c[c], "tn")
    st_new = st
    y = (jnp.concatenate(intra, axis=-1) + jnp.stack(inter, axis=0) * jnp.exp(acum3)).reshape(tb, 512)
    yz = (y + x * dsk) * silu(z)
    return yz * lax.rsqrt(jnp.mean(yz * yz, axis=-1, keepdims=True) + RMS_EPS) * nw, st_new


def adamw(w, g, m, v):
    m = ADAM_B1 * m + (1.0 - ADAM_B1) * g
    v = ADAM_B2 * v + (1.0 - ADAM_B2) * jnp.square(g)
    m_hat = m / (1.0 - ADAM_B1 ** ADAM_STEP)
    v_hat = v / (1.0 - ADAM_B2 ** ADAM_STEP)
    return -ADAM_LR * (m_hat / (jnp.sqrt(v_hat) + ADAM_EPS) + ADAM_WD * w), m, v


def _pick(n, cands):
    for c in cands:
        if n % c == 0:
            return c
    return n


def _params(sem):
    return pltpu.CompilerParams(dimension_semantics=sem, vmem_limit_bytes=VMEM_LIMIT)


MATMUL_VMEM_BUDGET = 50 * 1024 * 1024
MATMUL_MIN_STEPS = 4


def matmul(a, b, mode, out_dtype, name, after=None):
    if mode == "nn":
        (m, k), n = a.shape, b.shape[1]
    elif mode == "nt":
        (m, k), n = a.shape, b.shape[0]
    else:
        (k, m), n = a.shape, b.shape[1]
    tk = _pick(k, (2304, 2048, 1408, 1024, 768, 512, 256, 128))
    nk = k // tk
    a_bytes, b_bytes, out_bytes = a.dtype.itemsize, b.dtype.itemsize, jnp.dtype(out_dtype).itemsize

    def vmem(tm_, tn_):
        blocks = 2 * (tm_ * tk * a_bytes + tk * tn_ * b_bytes + tm_ * tn_ * out_bytes)
        return blocks + (tm_ * tn_ * 4 if nk > 1 else 0)

    def traffic(tm_, tn_):
        return (m // tm_) * k * n * b_bytes + (n // tn_ if nk > 1 else 1) * m * k * a_bytes

    sizes = (2304, 2048, 1920, 1408, 1024, 768, 512, 256, 128)
    tiles = [(tm_, tn_) for tm_ in sizes if m % tm_ == 0 for tn_ in sizes if n % tn_ == 0
             if vmem(tm_, tn_) <= MATMUL_VMEM_BUDGET] or [(m, n)]
    pipelined = [t for t in tiles if (m // t[0]) * (n // t[1]) * nk >= MATMUL_MIN_STEPS]
    tm, tn = min(pipelined or tiles, key=lambda t: (traffic(*t), -t[0] * t[1]))
    a_spec = pl.BlockSpec((tk, tm), lambda i, j, kk: (kk, i)) if mode == "tn" else pl.BlockSpec((tm, tk), lambda i, j, kk: (i, kk))
    b_spec = pl.BlockSpec((tn, tk), lambda i, j, kk: (j, kk)) if mode == "nt" else pl.BlockSpec((tk, tn), lambda i, j, kk: (kk, j))

    order = [] if after is None else [after]

    def body(a_ref, b_ref, *rest):
        o_ref, *acc = rest[len(order):]
        part = _bd(a_ref[...], b_ref[...], mode)
        if nk == 1:
            o_ref[...] = part.astype(o_ref.dtype)
            return
        acc_ref, = acc
        kk = pl.program_id(2)

        @pl.when(kk == 0)
        def _():
            acc_ref[...] = part

        @pl.when(jnp.logical_and(kk > 0, kk < nk - 1))
        def _():
            acc_ref[...] += part

        @pl.when(kk == nk - 1)
        def _():
            o_ref[...] = (acc_ref[...] + part).astype(o_ref.dtype)

    return pl.pallas_call(
        body, name=name, grid=(m // tm, n // tn, nk),
        in_specs=[a_spec, b_spec] + [pl.BlockSpec(memory_space=pl.ANY) for _ in order],
        out_specs=pl.BlockSpec((tm, tn), lambda i, j, kk: (i, j)),
        out_shape=jax.ShapeDtypeStruct((m, n), out_dtype),
        scratch_shapes=[pltpu.VMEM((tm, tn), F32)] if nk > 1 else [],
        compiler_params=_params(("parallel", "parallel", "arbitrary")),
    )(a, b, *order)


def rowwise(name, fn, rows, consts, out_rows, out_accs=(), tm_max=512, into=None, new_wide=None):
    t = rows[0][0].shape[0]
    tm = _pick(t, (tm_max, 128, 64, 32, 16, 8))
    n_r, n_c, n_o = len(rows), len(consts), len(out_rows)
    n_alias = 0 if into is None else 1

    def body(*refs):
        r_in = [r[...] for r in refs[:n_r]]
        c_in = [r[...] for r in refs[n_r:n_r + n_c]]
        refs = refs[:n_r + n_c] + refs[n_r + n_c + n_alias:]
        o_refs = refs[n_r + n_c:n_r + n_c + n_o]
        a_refs = refs[n_r + n_c + n_o:]
        ro, ao = fn(r_in, c_in)
        for ref, val in zip(o_refs, ro, strict=True):
            ref[...] = val.astype(ref.dtype)
        if a_refs:
            @pl.when(pl.program_id(0) == 0)
            def _():
                for ref in a_refs:
                    ref[...] = jnp.zeros_like(ref)

            for ref, val in zip(a_refs, ao, strict=True):
                ref[...] += val

    in_specs = [pl.BlockSpec((tm, w), functools.partial(lambda i, cb: (i, cb), cb=cb)) for _, w, cb in rows]
    in_specs += [pl.BlockSpec(c.shape, lambda i: (0, 0)) for c in consts]
    out_specs = [pl.BlockSpec((tm, w), lambda i: (i, 0)) for w, _ in out_rows]
    out_specs += [pl.BlockSpec(s, lambda i: (0, 0)) for s in out_accs]
    out_shape = [jax.ShapeDtypeStruct((t, w), dt) for w, dt in out_rows]
    out_shape += [jax.ShapeDtypeStruct(s, F32) for s in out_accs]
    operands = [r[0] for r in rows] + list(consts)
    aliases = {}
    if into is not None:
        target, cb = into
        in_specs.append(pl.BlockSpec(memory_space=pl.ANY))
        operands.append(target)
        out_specs[0] = pl.BlockSpec((tm, out_rows[0][0]), lambda i: (i, cb))
        out_shape[0] = jax.ShapeDtypeStruct(target.shape, target.dtype)
        aliases = {len(operands) - 1: 0}
    if new_wide is not None:
        width, cb = new_wide
        out_specs[0] = pl.BlockSpec((tm, out_rows[0][0]), lambda i: (i, cb))
        out_shape[0] = jax.ShapeDtypeStruct((t, width), out_rows[0][1])
    return pl.pallas_call(
        body, name=name, grid=(t // tm,), in_specs=in_specs, out_specs=out_specs, out_shape=out_shape,
        input_output_aliases=aliases, compiler_params=_params(("arbitrary",)),
    )(*operands)


def _full(a):
    return (a, a.shape[1], 0)


HGRN_TIME_BLOCK = 256
SSD_TIME_BLOCK = 512


def _time_block(t, most=HGRN_TIME_BLOCK):
    return _pick(t, tuple(b for b in (512, 256, 128, 64) if b <= most))


def _quarters(ref):
    return [ref[:, seg * D:(seg + 1) * D] for seg in range(4)]


def hgrn_forward(proj, lb, gn):
    t = proj.shape[0]
    tb = _time_block(t)
    nb = t // tb

    def body(qfig_ref, lb_ref, gn_ref, o_ref, st_ref, state):
        @pl.when(pl.program_id(0) == 0)
        def _():
            state[...] = jnp.zeros_like(state)

        st = state[...]
        st_ref[...] = st
        out, st_new = hgrn_block(*_quarters(qfig_ref), st, lb_ref[...], gn_ref[...])
        o_ref[...] = out.astype(o_ref.dtype)
        state[...] = st_new

    return pl.pallas_call(
        body, name="hgrn_forward", grid=(nb,),
        in_specs=[pl.BlockSpec((tb, 4 * D), lambda j: (j, 0)),
                  pl.BlockSpec((1, D), lambda j: (0, 0)), pl.BlockSpec((1, LANES), lambda j: (0, 0))],
        out_specs=[pl.BlockSpec((tb, D), lambda j: (j, 0)),
                   pl.BlockSpec((None, N_HEADS_A, LANES, LANES), lambda j: (j, 0, 0, 0))],
        out_shape=[jax.ShapeDtypeStruct((t, D), BF16),
                   jax.ShapeDtypeStruct((nb, N_HEADS_A, LANES, LANES), F32)],
        scratch_shapes=[pltpu.VMEM((N_HEADS_A, LANES, LANES), F32)],
        compiler_params=_params(("arbitrary",)),
    )(proj, lb, gn)


def hgrn_backward(proj, states, d_out, lb, gn, d_proj):
    t = proj.shape[0]
    tb = _time_block(t)
    nb = t // tb

    def body(qfig_ref, st_ref, do_ref, lb_ref, gn_ref, _, dqfig_ref, dlb_ref, dgn_ref, d_state):
        @pl.when(pl.program_id(0) == 0)
        def _():
            d_state[...] = jnp.zeros_like(d_state)
            dlb_ref[...] = jnp.zeros_like(dlb_ref)
            dgn_ref[...] = jnp.zeros_like(dgn_ref)

        _, vjp = jax.vjp(hgrn_block, *_quarters(qfig_ref), st_ref[...], lb_ref[...], gn_ref[...])
        dq, df, di, dg, dst, dlb, dgn = vjp((do_ref[...], d_state[...]))
        for seg, val in enumerate((dq, df, di, dg)):
            dqfig_ref[:, seg * D:(seg + 1) * D] = val.astype(dqfig_ref.dtype)
        d_state[...] = dst
        dlb_ref[...] += dlb
        dgn_ref[...] += dgn

    rev = lambda j: nb - 1 - j
    return pl.pallas_call(
        body, name="hgrn_backward", grid=(nb,),
        in_specs=[pl.BlockSpec((tb, 4 * D), lambda j: (rev(j), 0)),
                  pl.BlockSpec((None, N_HEADS_A, LANES, LANES), lambda j: (rev(j), 0, 0, 0)),
                  pl.BlockSpec((tb, D), lambda j: (rev(j), 0)),
                  pl.BlockSpec((1, D), lambda j: (0, 0)), pl.BlockSpec((1, LANES), lambda j: (0, 0)),
                  pl.BlockSpec(memory_space=pl.ANY)],
        out_specs=[pl.BlockSpec((tb, 4 * D), lambda j: (rev(j), 0)),
                   pl.BlockSpec((1, D), lambda j: (0, 0)), pl.BlockSpec((1, LANES), lambda j: (0, 0))],
        out_shape=[jax.ShapeDtypeStruct(d_proj.shape, d_proj.dtype), jax.ShapeDtypeStruct((1, D), F32),
                   jax.ShapeDtypeStruct((1, LANES), F32)],
        input_output_aliases={5: 0},
        scratch_shapes=[pltpu.VMEM((N_HEADS_A, LANES, LANES), F32)],
        compiler_params=_params(("arbitrary",)),
    )(proj, states, d_out, lb, gn, d_proj)


def _ssd_in_specs(tb, tmap):
    return [pl.BlockSpec((tb, 512), lambda g, j: (tmap(j), g)),
            pl.BlockSpec((tb, LANES), lambda g, j: (tmap(j), 16 + g)),
            pl.BlockSpec((tb, LANES), lambda g, j: (tmap(j), 20 + g)),
            pl.BlockSpec((tb, LANES), lambda g, j: (tmap(j), COL_DT // LANES)),
            pl.BlockSpec((tb, 512), lambda g, j: (tmap(j), COL_Z // 512 + g))]


def ssd_forward(xc, proj, dtb, alog, dsk, nw):
    t = proj.shape[0]
    tb = _time_block(t, SSD_TIME_BLOCK)
    nb = t // tb

    def body(x_ref, b_ref, c_ref, dt_ref, z_ref, dtb_ref, alog_ref, dsk_ref, nw_ref, o_ref, st_ref, state):
        @pl.when(pl.program_id(1) == 0)
        def _():
            state[...] = jnp.zeros_like(state)

        st = state[...]
        st_ref[...] = st
        out, st_new = ssd_block(x_ref[...], b_ref[...], c_ref[...], dt_ref[...], z_ref[...], st,
                                dtb_ref[...], alog_ref[...], dsk_ref[...], nw_ref[...], ssd_consts(pl.program_id(0)))
        o_ref[...] = out.astype(o_ref.dtype)
        state[...] = st_new

    vec = pl.BlockSpec((1, 512), lambda g, j: (0, g))
    heads = pl.BlockSpec((1, LANES), lambda g, j: (0, 0))
    return pl.pallas_call(
        body, name="ssd_forward", grid=(N_GROUPS_B, nb),
        in_specs=_ssd_in_specs(tb, lambda j: j) + [heads, vec, vec, vec],
        out_specs=[pl.BlockSpec((tb, 512), lambda g, j: (j, g)),
                   pl.BlockSpec((None, None, LANES, 512), lambda g, j: (j, g, 0, 0))],
        out_shape=[jax.ShapeDtypeStruct((t, B_INNER), BF16),
                   jax.ShapeDtypeStruct((nb, N_GROUPS_B, LANES, 512), F32)],
        scratch_shapes=[pltpu.VMEM((LANES, 512), F32)],
        compiler_params=_params(("arbitrary", "arbitrary")),
    )(xc, xc, xc, proj, proj, dtb, alog, dsk, nw)


def ssd_backward(xc, proj, states, d_out, dtb, alog, dsk, nw, d_proj):
    t = proj.shape[0]
    tb = _time_block(t, SSD_TIME_BLOCK)
    nb = t // tb
    rev = lambda j: nb - 1 - j

    def body(x_ref, b_ref, c_ref, dt_ref, z_ref, st_ref, do_ref, dtb_ref, alog_ref, dsk_ref, nw_ref, _,
             dx_ref, db_ref, dc_ref, ddt_ref, dz_ref, ddtb_ref, dalog_ref, ddsk_ref, dnw_ref, d_state):
        accs = (ddtb_ref, dalog_ref, ddsk_ref, dnw_ref)

        @pl.when(pl.program_id(1) == 0)
        def _():
            d_state[...] = jnp.zeros_like(d_state)
            for ref in accs:
                ref[...] = jnp.zeros_like(ref)

        cs = ssd_consts(pl.program_id(0))
        fn = lambda *a: ssd_block(*a, cs)
        _, vjp = jax.vjp(fn, x_ref[...], b_ref[...], c_ref[...], dt_ref[...], z_ref[...], st_ref[...],
                         dtb_ref[...], alog_ref[...], dsk_ref[...], nw_ref[...])
        dx, db, dc, ddt, dz, dst, *dpar = vjp((do_ref[...], d_state[...]))
        dx_ref[...] = dx
        db_ref[...] = db
        dc_ref[...] = dc
        ddt_ref[...] = ddt
        dz_ref[...] = dz.astype(dz_ref.dtype)
        d_state[...] = dst
        for ref, val in zip(accs, dpar, strict=True):
            ref[...] += val

    vec = pl.BlockSpec((1, 512), lambda g, j: (0, g))
    heads = pl.BlockSpec((1, LANES), lambda g, j: (0, 0))
    acc = pl.BlockSpec((None, 1, 512), lambda g, j: (g, 0, 0))
    acc_heads = pl.BlockSpec((None, 1, LANES), lambda g, j: (g, 0, 0))
    return pl.pallas_call(
        body, name="ssd_backward", grid=(N_GROUPS_B, nb),
        in_specs=_ssd_in_specs(tb, rev)
        + [pl.BlockSpec((None, None, LANES, 512), lambda g, j: (rev(j), g, 0, 0)),
           pl.BlockSpec((tb, 512), lambda g, j: (rev(j), g))] + [heads, vec, vec, vec] + [pl.BlockSpec(memory_space=pl.ANY)],
        out_specs=[pl.BlockSpec((tb, 512), lambda g, j: (rev(j), g)),
                   pl.BlockSpec((tb, LANES), lambda g, j: (rev(j), g)),
                   pl.BlockSpec((tb, LANES), lambda g, j: (rev(j), g)),
                   pl.BlockSpec((None, tb, LANES), lambda g, j: (g, rev(j), 0)),
                   pl.BlockSpec((tb, 512), lambda g, j: (rev(j), COL_Z // 512 + g)), acc_heads, acc, acc, acc],
        out_shape=[jax.ShapeDtypeStruct((t, B_INNER), F32), jax.ShapeDtypeStruct((t, 512), F32),
                   jax.ShapeDtypeStruct((t, 512), F32), jax.ShapeDtypeStruct((N_GROUPS_B, t, LANES), F32),
                   jax.ShapeDtypeStruct(d_proj.shape, d_proj.dtype)]
        + [jax.ShapeDtypeStruct((N_GROUPS_B, 1, LANES), F32)] + [jax.ShapeDtypeStruct((N_GROUPS_B, 1, 512), F32)] * 3,
        input_output_aliases={11: 4},
        scratch_shapes=[pltpu.VMEM((LANES, 512), F32)],
        compiler_params=_params(("arbitrary", "arbitrary")),
    )(xc, xc, xc, proj, proj, states, d_out, dtb, alog, dsk, nw, d_proj)


CONV_HALO = 8


def _shift_down(halo_then_tile, s, tm):
    if s == 0:
        return halo_then_tile[CONV_HALO:CONV_HALO + tm]
    return pltpu.roll(halo_then_tile, s, 0)[CONV_HALO:CONV_HALO + tm]


def _conv_pre(cur, prev, w, b, tm):
    stacked = jnp.concatenate([prev, cur], axis=0)
    taps = [_shift_down(stacked, 3 - j, tm) for j in range(4)]
    pre = b + taps[0] * w[0:1] + taps[1] * w[1:2] + taps[2] * w[2:3] + taps[3] * w[3:4]
    return pre, taps


def _conv_specs(t, tm):
    per = tm // CONV_HALO
    cur = pl.BlockSpec((tm, CONV_DIM), lambda i: (i, COL_XBC // CONV_DIM))
    prev = pl.BlockSpec((CONV_HALO, CONV_DIM), lambda i: (jnp.maximum(i * per - 1, 0), COL_XBC // CONV_DIM))
    return cur, prev


def conv_forward(proj, w, b):
    t = proj.shape[0]
    tm = _pick(t, (256, 128, 64))

    def body(cur_ref, prev_ref, w_ref, b_ref, o_ref):
        prev = jnp.where(pl.program_id(0) == 0, 0.0, prev_ref[...])
        pre, _ = _conv_pre(cur_ref[...], prev, w_ref[...], b_ref[...], tm)
        o_ref[...] = silu(pre)

    cur, prev = _conv_specs(t, tm)
    return pl.pallas_call(
        body, name="conv_forward", grid=(t // tm,),
        in_specs=[cur, prev, pl.BlockSpec((4, CONV_DIM), lambda i: (0, 0)), pl.BlockSpec((1, CONV_DIM), lambda i: (0, 0))],
        out_specs=pl.BlockSpec((tm, CONV_DIM), lambda i: (i, 0)),
        out_shape=jax.ShapeDtypeStruct((t, CONV_DIM), F32),
        compiler_params=_params(("arbitrary",)),
    )(proj, proj, w, b)


def conv_backward(proj, dx, db_, dc_, w, b, d_proj):
    t = proj.shape[0]
    tm = _pick(t, (256, 128, 64))
    per = tm // CONV_HALO
    nt = t // tm
    rev = lambda i: nt - 1 - i

    def body(cur_ref, prev_ref, dx_ref, dbm_ref, dcm_ref, w_ref, b_ref, _, o_ref, dw_ref, dbias_ref, later):
        @pl.when(pl.program_id(0) == 0)
        def _():
            dw_ref[...] = jnp.zeros_like(dw_ref)
            dbias_ref[...] = jnp.zeros_like(dbias_ref)
            later[...] = jnp.zeros_like(later)

        first_tile = pl.program_id(0) == nt - 1
        for lo, hi, src in ((0, B_INNER, dx_ref), (B_INNER, B_INNER + 512, dbm_ref), (B_INNER + 512, CONV_DIM, dcm_ref)):
            cols = slice(lo, hi)
            prev = jnp.where(first_tile, 0.0, prev_ref[:, cols])
            w_ = w_ref[:, cols]
            pre, taps = _conv_pre(cur_ref[:, cols], prev, w_, b_ref[:, cols], tm)
            sg = sigmoid(pre)
            dpre = src[...] * (sg * (1.0 + pre * (1.0 - sg)))
            dbias_ref[:, cols] += jnp.sum(dpre, axis=0, keepdims=True)
            for j in range(4):
                dw_ref[j:j + 1, cols] += jnp.sum(dpre * taps[j], axis=0, keepdims=True)
            stacked = jnp.concatenate([dpre, later[:, cols]], axis=0)
            acc = dpre * w_[3:4]
            for j in range(3):
                acc = acc + pltpu.roll(stacked, tm + CONV_HALO - (3 - j), 0)[0:tm] * w_[j:j + 1]
            o_ref[:, cols] = acc.astype(o_ref.dtype)
            later[:, cols] = dpre[0:CONV_HALO]

    row = lambda w_: pl.BlockSpec((tm, w_), lambda i: (rev(i), 0))
    whole = lambda r: pl.BlockSpec((r, CONV_DIM), lambda i: (0, 0))
    return pl.pallas_call(
        body, name="conv_backward", grid=(nt,),
        in_specs=[pl.BlockSpec((tm, CONV_DIM), lambda i: (rev(i), COL_XBC // CONV_DIM)),
                  pl.BlockSpec((CONV_HALO, CONV_DIM), lambda i: (jnp.maximum(rev(i) * per - 1, 0), COL_XBC // CONV_DIM)),
                  row(B_INNER), row(512), row(512), whole(4), whole(1), pl.BlockSpec(memory_space=pl.ANY)],
        out_specs=[pl.BlockSpec((tm, CONV_DIM), lambda i: (rev(i), COL_XBC // CONV_DIM)), whole(4), whole(1)],
        out_shape=[jax.ShapeDtypeStruct(d_proj.shape, d_proj.dtype), jax.ShapeDtypeStruct((4, CONV_DIM), F32),
                   jax.ShapeDtypeStruct((1, CONV_DIM), F32)],
        input_output_aliases={7: 0},
        scratch_shapes=[pltpu.VMEM((CONV_HALO, CONV_DIM), F32)],
        compiler_params=_params(("arbitrary",)),
    )(proj, proj, dx, db_, dc_, w, b, d_proj)


def stage_modulate(x, sc, sh):
    return _ln(x) * (1.0 + sc) + sh


def stage_merge(ga, gb, ya, yb):
    return sigmoid(ga) * ya + sigmoid(gb) * yb


def stage_post_mixer(x, h, g1, ln_g, ln_b, sc2, sh2):
    x1 = _ln(ALPHA * x + g1 * h) * ln_g + ln_b
    return x1, _ln(x1) * (1.0 + sc2) + sh2


def stage_swiglu(a, b):
    return silu(a) * b


def gate_up(ab):
    w = FFN_SHARD
    return (jnp.concatenate([ab[:, 2 * w * k:2 * w * k + w] for k in range(4)], axis=1),
            jnp.concatenate([ab[:, 2 * w * k + w:2 * w * (k + 1)] for k in range(4)], axis=1))


def per_chip(gate, up):
    w = FFN_SHARD
    return jnp.concatenate([part[:, w * k:w * (k + 1)] for k in range(4) for part in (gate, up)], axis=1)


def stage_loss(x1, hf, tgt, g2, ln_g, ln_b):
    x2 = _ln(ALPHA * x1 + g2 * hf) * ln_g + ln_b
    return 0.5 * jnp.sum(jnp.mean(jnp.square(x2 - tgt), axis=-1, keepdims=True), axis=0, keepdims=True)


def local_step(x, tgt, mod, wts, small, early=None, mid=None, late=None, last=None):
    sh1, sc1, g1, sh2, sc2, g2 = mod
    lb, gn, conv_w, conv_b, dtb, alog, dsk, nw, ln1_g, ln1_b, ln2_g, ln2_b = small
    vec = (1, D)

    (u1,) = rowwise("modulate1", lambda r, c: ((stage_modulate(r[0], *c),), ()), [_full(x)], [sc1, sh1], [(D, BF16)])
    w_in = wts.input_projection(u1)
    proj = matmul(u1, w_in, "nn", F32, "in_proj")
    ya_in, st_a = hgrn_forward(proj, lb, gn + wts.start_rest(proj)[0:1])
    xc = conv_forward(proj, conv_w, conv_b)
    w_a, w_b, w_o, w_gu, w_d = wts.rest(xc)
    yb_in, st_b = ssd_forward(xc, proj, dtb, alog, dsk, nw)
    ya = matmul(ya_in, w_a, "nn", F32, "branch_a")
    yb = matmul(yb_in, w_b, "nn", F32, "branch_b")
    gate_rows = [(proj, D, COL_GA // D), (proj, D, COL_GB // D), _full(ya), _full(yb)]
    (merged,) = rowwise("merge", lambda r, c: ((stage_merge(*r),), ()), gate_rows, [], [(D, BF16)])
    h = matmul(merged, w_o, "nn", F32, "out_proj")
    post_consts = [g1, ln1_g, ln1_b, sc2, sh2]
    x1, u2 = rowwise("post_mixer", lambda r, c: (stage_post_mixer(*r, *c), ()), [_full(x), _full(h)], post_consts,
                     [(D, F32), (D, BF16)])
    ab = matmul(u2, w_gu, "nt", F32, "ffn_in")
    (p,) = rowwise("swiglu", lambda r, c: ((stage_swiglu(*gate_up(r[0])),), ()), [_full(ab)], [], [(D_FF, BF16)],
                   tm_max=256)
    hf = matmul(p, w_d, "nn", F32, "ffn_out")

    def loss_bwd(r, c):
        loss, vjp = jax.vjp(stage_loss, *r, *c)
        dx1, dhf, _, dg2, dlg, dlb_ = vjp(jnp.ones((1, 1), F32))
        return (dx1, dhf), (loss, dg2, dlg, dlb_)

    dx1, dhf, loss, dg2, dln2_g, dln2_b = rowwise(
        "loss_backward", loss_bwd, [_full(x1), _full(hf), _full(tgt)], [g2, ln2_g, ln2_b],
        [(D, F32), (D, BF16)], [(1, 1), vec, vec, vec])
    dp = matmul(dhf, w_d, "nt", F32, "ffn_out_dx")
    dw_d = matmul(p, dhf, "tn", F32, "ffn_out_dw")

    def swiglu_bwd(r, c):
        _, vjp = jax.vjp(stage_swiglu, *gate_up(r[0]))
        return (per_chip(*vjp(r[1])),), ()

    (dab,) = rowwise("swiglu_backward", swiglu_bwd, [_full(ab), _full(dp)], [], [(2 * D_FF, BF16)], tm_max=256)
    du2 = matmul(dab, w_gu, "nn", F32, "ffn_in_dx")
    dw_gu = matmul(dab, u2, "tn", F32, "ffn_in_dw")

    def post_bwd(r, c):
        _, vjp = jax.vjp(stage_post_mixer, r[0], r[1], *c)
        dx, dh, *dc = vjp((r[2], r[3]))
        return (dx, dh), tuple(dc)

    dx_a, dh, dg1, dln1_g, dln1_b, dsc2, dsh2 = rowwise(
        "post_mixer_backward", post_bwd, [_full(x), _full(h), _full(dx1), _full(du2)], post_consts,
        [(D, F32), (D, BF16)], [vec] * 5)
    dmerged = matmul(dh, w_o, "nt", F32, "out_proj_dx")
    dw_o = matmul(merged, dh, "tn", F32, "out_proj_dw")

    def merge_bwd(r, c):
        _, vjp = jax.vjp(stage_merge, *r[:4])
        dga, dgb, dya, dyb = vjp(r[4])
        return (jnp.concatenate([dga, dgb], axis=1), dya, dyb), ()

    dproj, dya, dyb = rowwise("merge_backward", merge_bwd, gate_rows + [_full(dmerged)], [],
                              [(2 * D, BF16), (D, BF16), (D, BF16)], new_wide=(IN_PAD, COL_GA // (2 * D)))
    dya_in = matmul(dya, w_a, "nt", F32, "branch_a_dx")
    dw_a = matmul(ya_in, dya, "tn", F32, "branch_a_dw")
    dyb_in = matmul(dyb, w_b, "nt", F32, "branch_b_dx")
    dw_b = matmul(yb_in, dyb, "tn", F32, "branch_b_dw")
    gn_after = gn if early is None else gn + early((dw_a, dw_b, dw_o, dw_gu, dw_d))[0:1]
    dproj, dlb, dgn = hgrn_backward(proj, st_a, dya_in, lb, gn_after, dproj)
    dtb_after = dtb if mid is None else dtb + mid(dlb)[0:1, 0:1]
    dxs, dbm, dcm, ddt, dproj, ddtb, dalog, ddsk, dnw = ssd_backward(xc, proj, st_b, dyb_in, dtb_after, alog, dsk, nw, dproj)
    dproj, dconv_w, dconv_b = conv_backward(proj, dxs, dbm, dcm, conv_w, conv_b, dproj)
    if late is not None:
        late(dconv_b)
    t = x.shape[0]
    tail = jnp.concatenate([jnp.sum(ddt, axis=0).astype(BF16), jnp.zeros((t, IN_PAD - COL_DT - LANES), BF16)], axis=1)
    dproj = lax.dynamic_update_slice(dproj, tail, (0, COL_DT))
    dw_in = matmul(u1, dproj, "tn", F32, "in_proj_dw")
    du1 = matmul(dproj, w_in, "nt", F32, "in_proj_dx", after=None if last is None else last(dw_in))

    def mod_bwd(r, c):
        _, vjp = jax.vjp(stage_modulate, r[0], *c)
        dx, dsc, dsh = vjp(r[1])
        return (dx + r[2],), (dsc, dsh)

    grad_x, dsc1, dsh1 = rowwise("modulate1_backward", mod_bwd, [_full(x), _full(du1), _full(dx_a)], [sc1, sh1],
                                 [(D, F32)], [vec, vec])
    d_mod = (dsh1, dsc1, dg1, dsh2, dsc2, dg2)
    d_wts = (dw_in, dw_a, dw_b, dw_o, dw_gu, dw_d)
    d_small = (dlb, dgn, dconv_w, dconv_b, jnp.sum(ddtb, axis=0),
               dalog.reshape(1, B_INNER), ddsk.reshape(1, B_INNER), dnw.reshape(1, B_INNER),
               dln1_g, dln1_b, dln2_g, dln2_b)
    return loss, grad_x, d_mod, d_wts, d_small


HBM = pl.BlockSpec(memory_space=pltpu.HBM)
SEM = pl.BlockSpec(memory_space=pltpu.SEMAPHORE)
DATAFLOW = pltpu.SideEffectType.DATAFLOW_SIDE_EFFECTING


def _place():
    return lax.axis_index("x"), lax.axis_index("y"), lax.axis_index("c")


def _other_chips(x, y):
    return [(1 - x, y), (x, 1 - y), (1 - x, 1 - y)]


def _remote(src, dst, send_sem, recv_sem, device):
    return pltpu.make_async_remote_copy(src_ref=src, dst_ref=dst, send_sem=send_sem, recv_sem=recv_sem,
                                        device_id=device, device_id_type=MESH)


def gather_rows(v, name):
    n = v.shape[1]

    def body(v_ref, out_ref, send_sems, recv_sems, local_sem):
        x, y, c = _place()
        mine = pltpu.make_async_copy(v_ref, out_ref.at[4 * x + 2 * y + c], local_sem)
        mine.start()
        sends, recvs = [], []
        for m in range(1, 8):
            px = 1 - x if m & 4 else x
            py = 1 - y if m & 2 else y
            pc = 1 - c if m & 1 else c
            sends.append(_remote(v_ref, out_ref.at[4 * x + 2 * y + c], send_sems.at[m - 1], recv_sems.at[m - 1], (px, py, pc)))
            recvs.append(_remote(v_ref, out_ref.at[4 * px + 2 * py + pc], send_sems.at[m - 1], recv_sems.at[m - 1], (px, py, pc)))
        for cp in sends:
            cp.start()
        for cp in recvs:
            cp.wait_recv()
        for cp in sends:
            cp.wait_send()
        mine.wait()

    return pl.pallas_call(
        body, name=name, in_specs=[HBM], out_specs=HBM,
        out_shape=jax.ShapeDtypeStruct((8, 1, n), v.dtype),
        scratch_shapes=[pltpu.SemaphoreType.DMA((7,)), pltpu.SemaphoreType.DMA((7,)), pltpu.SemaphoreType.DMA],
    )(v)


def exchange_rows(part, name):
    w = part.shape[2]

    def body(p_ref, out_ref, send_sems, recv_sems, local_sem):
        x, y, c = _place()
        k = 2 * x + y
        mine = pltpu.make_async_copy(p_ref.at[4 * x + 2 * y + c], out_ref.at[k], local_sem)
        mine.start()
        sends, recvs = [], []
        for j, (px, py) in enumerate(_other_chips(x, y)):
            sends.append(_remote(p_ref.at[4 * px + 2 * py + c], out_ref.at[k], send_sems.at[j], recv_sems.at[j], (px, py, c)))
            recvs.append(_remote(p_ref.at[4 * px + 2 * py + c], out_ref.at[2 * px + py], send_sems.at[j], recv_sems.at[j], (px, py, c)))
        for cp in sends:
            cp.start()
        for cp in recvs:
            cp.wait_recv()
        for cp in sends:
            cp.wait_send()
        mine.wait()

    return pl.pallas_call(
        body, name=name, in_specs=[HBM], out_specs=HBM,
        out_shape=jax.ShapeDtypeStruct((4, 1, w), part.dtype),
        scratch_shapes=[pltpu.SemaphoreType.DMA((3,)), pltpu.SemaphoreType.DMA((3,)), pltpu.SemaphoreType.DMA],
    )(part)


def _half_of_slot(ref, rows, px, py, pc):
    return ref.at[2 * px + py, pl.ds(pc * (rows // 2), rows // 2), :]


def gather_start(shards, after, tag):
    n = len(shards)

    def body(*refs):
        w_refs, land_refs = refs[:n], refs[n:2 * n]
        send_sems, recv_sems = refs[2 * n + 1], refs[2 * n + 2]
        token = refs[-1]
        x, y, c = _place()
        for i in range(n):
            rows = shards[i].shape[0]
            for j, (px, py) in enumerate(_other_chips(x, y)):
                _remote(w_refs[i].at[pl.ds(c * (rows // 2), rows // 2), :], _half_of_slot(land_refs[i], rows, x, y, c),
                        send_sems.at[j * n + i], recv_sems.at[j * n + i], (px, py, c)).start()
        token[...] = jnp.zeros_like(token)

    hbm = lambda a: pltpu.with_memory_space_constraint(a, pltpu.HBM)
    lands = [lax.empty((4,) + s.shape, s.dtype) for s in shards]
    dma = pltpu.SemaphoreType.DMA
    return pl.pallas_call(
        body, name="gather_start_" + tag,
        out_shape=(dma((3 * n,)), dma((3 * n,)),
                   *[pltpu.HBM(a.shape, a.dtype) for a in list(shards) + lands], jax.ShapeDtypeStruct((8, LANES), F32)),
        in_specs=[HBM] * (2 * n) + [pl.BlockSpec(memory_space=pl.ANY)],
        out_specs=(SEM, SEM, *[HBM] * (2 * n), pl.BlockSpec(memory_space=pltpu.VMEM)),
        input_output_aliases={i: 2 + i for i in range(2 * n)},
        compiler_params=pltpu.CompilerParams(has_side_effects=DATAFLOW),
    )(*[hbm(a) for a in list(shards) + lands], after)


def gather_wait(send_sems, recv_sems, shards, lands, after, tag):
    n = len(shards)

    def body(*refs):
        w_refs, land_refs = refs[:n], refs[n:2 * n]
        send_ref, recv_ref = refs[2 * n], refs[2 * n + 1]
        x, y, c = _place()
        for i in range(n):
            rows = shards[i].shape[0]
            for j, (px, py) in enumerate(_other_chips(x, y)):
                cp = _remote(w_refs[i].at[pl.ds(c * (rows // 2), rows // 2), :], _half_of_slot(land_refs[i], rows, px, py, c),
                             send_ref.at[j * n + i], recv_ref.at[j * n + i], (px, py, c))
                cp.wait_send()
                cp.wait_recv()

    out = pl.pallas_call(
        body, name="gather_wait_" + tag,
        out_shape=tuple(pltpu.HBM(a.shape, a.dtype) for a in list(shards) + list(lands)),
        in_specs=[HBM] * (2 * n) + [SEM, SEM, pl.BlockSpec(memory_space=pl.ANY)], out_specs=tuple([HBM] * (2 * n)),
        input_output_aliases={i: i for i in range(2 * n)},
        compiler_params=pltpu.CompilerParams(has_side_effects=DATAFLOW),
    )(*shards, *lands, send_sems, recv_sems, after)
    return list(out[:n]), list(out[n:])


def forward_start(lands, tag):
    n = len(lands)

    def body(*refs):
        land_refs = refs[:n]
        send_sems, recv_sems = refs[n], refs[n + 1]
        token = refs[-1]
        x, y, c = _place()
        for i in range(n):
            rows = lands[i].shape[1]
            for j, (px, py) in enumerate(_other_chips(x, y)):
                mine = _half_of_slot(land_refs[i], rows, px, py, c)
                _remote(mine, mine, send_sems.at[j * n + i], recv_sems.at[j * n + i], (x, y, 1 - c)).start()
        token[...] = jnp.zeros_like(token)

    dma = pltpu.SemaphoreType.DMA
    return pl.pallas_call(
        body, name="forward_start_" + tag,
        out_shape=(dma((3 * n,)), dma((3 * n,)), *[pltpu.HBM(a.shape, a.dtype) for a in lands],
                   jax.ShapeDtypeStruct((8, LANES), F32)),
        in_specs=[HBM] * n, out_specs=(SEM, SEM, *[HBM] * n, pl.BlockSpec(memory_space=pltpu.VMEM)),
        input_output_aliases={i: 2 + i for i in range(n)},
        compiler_params=pltpu.CompilerParams(has_side_effects=DATAFLOW),
    )(*lands)


def forward_wait(started, after, tag):
    send_sems, recv_sems, *rest = started
    lands = rest[:-1]
    n = len(lands)

    def body(*refs):
        land_refs = refs[:n]
        send_ref, recv_ref = refs[n], refs[n + 1]
        x, y, c = _place()
        for i in range(n):
            rows = lands[i].shape[1]
            for j, (px, py) in enumerate(_other_chips(x, y)):
                cp = _remote(_half_of_slot(land_refs[i], rows, px, py, c), _half_of_slot(land_refs[i], rows, px, py, 1 - c),
                             send_ref.at[j * n + i], recv_ref.at[j * n + i], (x, y, 1 - c))
                cp.wait_send()
                cp.wait_recv()

    out = pl.pallas_call(
        body, name="forward_wait_" + tag,
        out_shape=tuple(pltpu.HBM(a.shape, a.dtype) for a in lands),
        in_specs=[HBM] * n + [SEM, SEM, pl.BlockSpec(memory_space=pl.ANY)], out_specs=tuple([HBM] * n),
        input_output_aliases={i: i for i in range(n)},
        compiler_params=pltpu.CompilerParams(has_side_effects=DATAFLOW),
    )(*lands, send_sems, recv_sems, after)
    return list(out)


def pair_start(slabs, tag):
    n = len(slabs)

    def body(*refs):
        g_refs, land_refs = refs[:n], refs[n:2 * n]
        send_sems, recv_sems = refs[2 * n], refs[2 * n + 1]
        token = refs[-1]
        x, y, c = _place()
        for i in range(n):
            hr = slabs[i].shape[1] // 2
            _remote(g_refs[i].at[:, pl.ds((1 - c) * hr, hr), :], land_refs[i], send_sems.at[i], recv_sems.at[i],
                    (x, y, 1 - c)).start()
        token[...] = jnp.zeros_like(token)

    hbm = lambda a: pltpu.with_memory_space_constraint(a, pltpu.HBM)
    lands = [lax.empty((4, s.shape[1] // 2, s.shape[2]), s.dtype) for s in slabs]
    dma = pltpu.SemaphoreType.DMA
    return pl.pallas_call(
        body, name="pair_start_" + tag,
        out_shape=(dma((n,)), dma((n,)), *[pltpu.HBM(a.shape, a.dtype) for a in list(slabs) + lands],
                   jax.ShapeDtypeStruct((8, LANES), F32)),
        in_specs=[HBM] * (2 * n), out_specs=(SEM, SEM, *[HBM] * (2 * n), pl.BlockSpec(memory_space=pltpu.VMEM)),
        input_output_aliases={i: 2 + i for i in range(2 * n)},
        compiler_params=pltpu.CompilerParams(has_side_effects=DATAFLOW),
    )(*[hbm(a) for a in list(slabs) + lands])


def pair_wait(started, after, tag):
    send_sems, recv_sems, *rest = started
    n = (len(rest) - 1) // 2
    slabs, lands = rest[:n], rest[n:2 * n]

    def body(*refs):
        g_refs, land_refs = refs[:n], refs[n:2 * n]
        send_ref, recv_ref = refs[2 * n], refs[2 * n + 1]
        x, y, c = _place()
        for i in range(n):
            hr = slabs[i].shape[1] // 2
            cp = _remote(g_refs[i].at[:, pl.ds((1 - c) * hr, hr), :], land_refs[i], send_ref.at[i], recv_ref.at[i], (x, y, 1 - c))
            cp.wait_send()
            cp.wait_recv()

    out = pl.pallas_call(
        body, name="pair_wait_" + tag,
        out_shape=tuple(pltpu.HBM(a.shape, a.dtype) for a in list(slabs) + list(lands)),
        in_specs=[HBM] * (2 * n) + [SEM, SEM, pl.BlockSpec(memory_space=pl.ANY)], out_specs=tuple([HBM] * (2 * n)),
        input_output_aliases={i: i for i in range(2 * n)},
        compiler_params=pltpu.CompilerParams(has_side_effects=DATAFLOW),
    )(*slabs, *lands, send_sems, recv_sems, after)
    return list(out[:n]), list(out[n:])


def _tile2(rows, cols):
    fits = lambda r, c: r * c * 4 <= BLOCK_BYTES
    if fits(rows, cols):
        return rows, cols
    for r in (1024, 512, 256, 128, 64):
        if rows % r == 0 and fits(r, cols):
            return r, cols
    return rows, next(cols // k for k in (2, 3, 4, 6, 8, 12, 16) if cols % (k * LANES) == 0 and fits(rows, cols // k))


def pair_add(g, p, c, name):
    _, hr, cols = p.shape
    tm, tc = _tile2(hr, cols)
    per = hr // tm

    def body(c_ref, g_ref, p_ref, o_ref):
        o_ref[...] = (g_ref[...] + p_ref[...]).astype(o_ref.dtype)

    return pl.pallas_call(
        body, name=name,
        grid_spec=pltpu.PrefetchScalarGridSpec(
            num_scalar_prefetch=1, grid=(4, per, cols // tc),
            in_specs=[pl.BlockSpec((None, tm, tc), lambda k, i, j, c_ref: (k, c_ref[0] * per + i, j)),
                      pl.BlockSpec((None, tm, tc), lambda k, i, j, c_ref: (k, i, j))],
            out_specs=pl.BlockSpec((None, tm, tc), lambda k, i, j, c_ref: (k, i, j))),
        out_shape=jax.ShapeDtypeStruct((4, hr, cols), BF16),
        compiler_params=_params(("arbitrary", "arbitrary", "arbitrary")),
    )(c.reshape(1).astype(jnp.int32), g, p)


def scatter_start(sums, tag):
    n = len(sums)

    def body(*refs):
        s_refs, land_refs = refs[:n], refs[n:2 * n]
        send_sems, recv_sems = refs[2 * n], refs[2 * n + 1]
        token = refs[-1]
        x, y, c = _place()
        k = 2 * x + y
        for i in range(n):
            for j, (px, py) in enumerate(_other_chips(x, y)):
                _remote(s_refs[i].at[2 * px + py], land_refs[i].at[k], send_sems.at[j * n + i], recv_sems.at[j * n + i],
                        (px, py, c)).start()
        token[...] = jnp.zeros_like(token)

    hbm = lambda a: pltpu.with_memory_space_constraint(a, pltpu.HBM)
    return pl.pallas_call(
        body, name="scatter_start_" + tag,
        out_shape=(pltpu.SemaphoreType.DMA((3 * n,)), pltpu.SemaphoreType.DMA((3 * n,)),
                   *[pltpu.HBM(s.shape, s.dtype) for s in sums], *[pltpu.HBM(s.shape, s.dtype) for s in sums],
                   jax.ShapeDtypeStruct((8, LANES), F32)),
        in_specs=[HBM] * (2 * n), out_specs=(SEM, SEM, *[HBM] * (2 * n), pl.BlockSpec(memory_space=pltpu.VMEM)),
        input_output_aliases={i: 2 + i for i in range(2 * n)},
        compiler_params=pltpu.CompilerParams(has_side_effects=DATAFLOW),
    )(*[hbm(s) for s in sums], *[hbm(lax.empty(s.shape, s.dtype)) for s in sums])


def scatter_wait(started, after, tag):
    send_sems, recv_sems, *rest = started
    n = (len(rest) - 1) // 2
    sums, lands = rest[:n], rest[n:2 * n]

    def body(*refs):
        s_refs, land_refs = refs[:n], refs[n:2 * n]
        send_ref, recv_ref = refs[2 * n], refs[2 * n + 1]
        x, y, c = _place()
        for i in range(n):
            for j, (px, py) in enumerate(_other_chips(x, y)):
                cp = _remote(s_refs[i].at[2 * px + py], land_refs[i].at[2 * px + py], send_ref.at[j * n + i],
                             recv_ref.at[j * n + i], (px, py, c))
                cp.wait_send()
                cp.wait_recv()

    out = pl.pallas_call(
        body, name="scatter_wait_" + tag,
        out_shape=tuple(pltpu.HBM(s.shape, s.dtype) for s in sums + lands),
        in_specs=[HBM] * (2 * n) + [SEM, SEM, pl.BlockSpec(memory_space=pl.ANY)], out_specs=tuple([HBM] * (2 * n)),
        input_output_aliases={i: i for i in range(2 * n)},
        compiler_params=pltpu.CompilerParams(has_side_effects=DATAFLOW),
    )(*sums, *lands, send_sems, recv_sems, after)
    return list(out[:n]), list(out[n:])


def sum_chips(landed, own, chip, core, name):
    _, hr, cols = landed.shape
    tm, tc = _tile2(hr, cols)
    per = hr // tm

    def body(idx_ref, l0, l1, l2, l3, own_ref, o_ref):
        mine = own_ref[...].astype(F32)
        v = [jnp.where(idx_ref[0] == k, mine, ref[...].astype(F32)) for k, ref in enumerate((l0, l1, l2, l3))]
        o_ref[...] = ((v[0] + v[1]) + v[2]) + v[3]

    slot = lambda k: pl.BlockSpec((None, tm, tc),
                                  lambda i, j, idx: (jnp.where(idx[0] == k, (k + 1) & 3, k), i, j))
    return pl.pallas_call(
        body, name=name,
        grid_spec=pltpu.PrefetchScalarGridSpec(
            num_scalar_prefetch=1, grid=(per, cols // tc),
            in_specs=[slot(0), slot(1), slot(2), slot(3),
                      pl.BlockSpec((None, tm, tc), lambda i, j, idx: (idx[0], i, j))],
            out_specs=pl.BlockSpec((tm, tc), lambda i, j, idx: (idx[1] * per + i, j))),
        out_shape=jax.ShapeDtypeStruct((2 * hr, cols), F32),
        compiler_params=_params(("arbitrary", "arbitrary")),
    )(jnp.stack([chip, core]).astype(jnp.int32), landed, landed, landed, landed, own)


def exchange_halves(bufs):
    n = len(bufs)

    def body(*refs):
        out_refs = refs[n:2 * n]
        send_sems, recv_sems = refs[2 * n:]
        x, y, c = _place()
        sends, recvs = [], []
        for i in range(n):
            hr = bufs[i].shape[0] // 2
            own = out_refs[i].at[pl.ds(c * hr, hr), :]
            other = out_refs[i].at[pl.ds((1 - c) * hr, hr), :]
            sends.append(_remote(own, own, send_sems.at[i], recv_sems.at[i], (x, y, 1 - c)))
            recvs.append(_remote(other, other, send_sems.at[i], recv_sems.at[i], (x, y, 1 - c)))
        for cp in sends:
            cp.start()
        for cp in recvs:
            cp.wait_recv()
        for cp in sends:
            cp.wait_send()

    return pl.pallas_call(
        body, name="exchange_halves", in_specs=[HBM] * n, out_specs=[HBM] * n,
        out_shape=[jax.ShapeDtypeStruct(b.shape, b.dtype) for b in bufs],
        input_output_aliases={i: i for i in range(n)},
        scratch_shapes=[pltpu.SemaphoreType.DMA((n,)), pltpu.SemaphoreType.DMA((n,))],
    )(*bufs)


def assemble_in_proj(landed, own, chip):
    rows, cols = 128, own.shape[1]

    def body(idx_ref, l0, l1, l2, l3, own_ref, o_ref):
        mine = own_ref[...]
        w = jnp.concatenate([jnp.where(idx_ref[0] == k, mine, ref[...]) for k, ref in enumerate((l0, l1, l2, l3))], axis=1)
        o_ref[...] = jnp.concatenate([w[:, :ORIG_Z], w[:, ORIG_GA:], w[:, ORIG_XBC:ORIG_DT], w[:, ORIG_Z:ORIG_XBC],
                                      w[:, ORIG_DT:ORIG_GA], jnp.zeros((rows, IN_PAD - IN_ORIG), w.dtype)], axis=1)

    slot = lambda k: pl.BlockSpec((None, rows, cols), lambda i, idx: (jnp.where(idx[0] == k, (k + 1) & 3, k), i, 0))
    return pl.pallas_call(
        body, name="assemble_in_proj",
        grid_spec=pltpu.PrefetchScalarGridSpec(
            num_scalar_prefetch=1, grid=(D // rows,),
            in_specs=[slot(0), slot(1), slot(2), slot(3), pl.BlockSpec((rows, cols), lambda i, idx: (i, 0))],
            out_specs=pl.BlockSpec((rows, IN_PAD), lambda i, idx: (i, 0))),
        out_shape=jax.ShapeDtypeStruct((D, IN_PAD), own.dtype),
        compiler_params=_params(("arbitrary",)),
    )(chip.reshape(1).astype(jnp.int32), landed, landed, landed, landed, own)


def rows_exchange(a, name):
    hr = a.shape[0] // 2

    def body(a_ref, out_ref, send_sem, recv_sem):
        x, y, c = _place()
        cp = _remote(a_ref.at[pl.ds((1 - c) * hr, hr), :], out_ref, send_sem, recv_sem, (x, y, 1 - c))
        cp.start()
        cp.wait()

    return pl.pallas_call(
        body, name=name, in_specs=[HBM], out_specs=HBM,
        out_shape=jax.ShapeDtypeStruct((hr, a.shape[1]), a.dtype),
        scratch_shapes=[pltpu.SemaphoreType.DMA, pltpu.SemaphoreType.DMA],
    )(a)


def split_pair_add(dw, received, core):
    cols = IN_ORIG // 4
    rows, hr = 128, D // 2
    per = hr // rows

    def body(c_ref, own_ref, got_ref, o_ref):
        d = own_ref[...] + got_ref[...]
        w = jnp.concatenate([d[:, :COL_GA], d[:, COL_Z:COL_DT], d[:, COL_XBC:COL_Z], d[:, COL_DT:COL_DT + 32],
                             d[:, COL_GA:COL_XBC]], axis=1)
        for k in range(4):
            o_ref[k] = w[:, k * cols:(k + 1) * cols].astype(o_ref.dtype)

    return pl.pallas_call(
        body, name="split_pair_add",
        grid_spec=pltpu.PrefetchScalarGridSpec(
            num_scalar_prefetch=1, grid=(per,),
            in_specs=[pl.BlockSpec((rows, IN_PAD), lambda i, c_ref: (c_ref[0] * per + i, 0)),
                      pl.BlockSpec((rows, IN_PAD), lambda i, c_ref: (i, 0))],
            out_specs=pl.BlockSpec((4, rows, cols), lambda i, c_ref: (0, i, 0))),
        out_shape=jax.ShapeDtypeStruct((4, hr, cols), BF16),
        compiler_params=_params(("arbitrary",)),
    )(core.reshape(1).astype(jnp.int32), dw, received)


def ada_prepare(c_all, w_ada, hgrn_lb):
    def body(c_ref, w_ref, lb_ref, mod_ref, row_ref):
        mod_ref[...] = hdot(silu(c_ref[...]), w_ref[...])
        row_ref[...] = sigmoid(lb_ref[0:1, :] - lb_ref[1:2, :])

    return pl.pallas_call(
        body, name="ada_prepare",
        out_shape=[jax.ShapeDtypeStruct((8, w_ada.shape[1]), F32), jax.ShapeDtypeStruct((1, D), F32)],
        compiler_params=pltpu.CompilerParams(vmem_limit_bytes=VMEM_LIMIT),
    )(c_all, w_ada, hgrn_lb)


SMALL_SEGS = (("mod", 6 * D), ("lb", D), ("gnorm", LANES), ("conv_w", 4 * CONV_DIM), ("conv_b", CONV_DIM),
              ("dt_bias", LANES), ("a_log", B_INNER), ("d", B_INNER), ("ssm_norm", B_INNER),
              ("ln1_g", D), ("ln1_b", D), ("ln2_g", D), ("ln2_b", D), ("loss", LANES))
SMALL_PARAMS = ("b_ada", "hgrn_lb", "hgrn_gnorm", "ssm_conv_b", "ssm_dt_bias", "ssm_a_log", "ssm_d", "ssm_norm",
                "ln1_g", "ln1_b", "ln2_g", "ln2_b")


def finalize_small(g_all, c_all, dmod_cols, params, m, v):
    n_p = len(SMALL_PARAMS)
    offs, o = {}, 0
    for nm, width in SMALL_SEGS:
        offs[nm] = (o, width)
        o += width

    def body(*refs):
        g_ref, c_ref, dm_ref = refs[:3]
        p_refs = refs[3:3 + n_p]
        m_refs = refs[3 + n_p:3 + 2 * n_p]
        v_refs = refs[3 + 2 * n_p:3 + 3 * n_p]
        outs = refs[3 + 3 * n_p:]
        gwa_ref, gcw_ref, loss_ref = outs[:3]
        res = outs[3:]
        total = jnp.sum(g_ref[...], axis=0, keepdims=True)
        seg = lambda nm: total[:, offs[nm][0]:offs[nm][0] + offs[nm][1]]
        loss_ref[...] = seg("loss")
        gwa_ref[...] = hdot(silu(c_ref[...]), dm_ref[...], "tn")
        cw = seg("conv_w")
        for j in range(4):
            gcw_ref[j:j + 1, :] = cw[:, j * CONV_DIM:(j + 1) * CONV_DIM]
        hc = lax.broadcasted_iota(jnp.int32, (B_INNER, LANES), 0)
        hj = lax.broadcasted_iota(jnp.int32, (B_INNER, LANES), 1)
        per_head = ((hc >> 6) == hj).astype(F32)
        heads = lambda nm: hdot(jnp.broadcast_to(seg(nm), (8, B_INNER)), per_head)[0:1, 0:32]
        lbp = sigmoid(p_refs[1][0:1, :] - p_refs[1][1:2, :])
        g_row = seg("lb") * lbp * (1.0 - lbp)
        grads = {"b_ada": seg("mod"), "hgrn_gnorm": seg("gnorm"), "ssm_conv_b": seg("conv_b"),
                 "ssm_dt_bias": seg("dt_bias")[:, 0:32], "ssm_a_log": heads("a_log"), "ssm_d": heads("d"),
                 "ssm_norm": seg("ssm_norm"), "ln1_g": seg("ln1_g"), "ln1_b": seg("ln1_b"),
                 "ln2_g": seg("ln2_g"), "ln2_b": seg("ln2_b")}
        for i, nm in enumerate(SMALL_PARAMS):
            g_out, d_out, m_out, v_out = res[4 * i:4 * i + 4]
            if nm == "hgrn_lb":
                for row, gv in ((0, g_row), (1, -g_row)):
                    sl = slice(row, row + 1)
                    dl, mn, vn = adamw(p_refs[i][sl, :], gv, m_refs[i][sl, :], v_refs[i][sl, :])
                    g_out[sl, :], d_out[sl, :], m_out[sl, :], v_out[sl, :] = gv, dl, mn, vn
            else:
                gv = grads[nm]
                dl, mn, vn = adamw(p_refs[i][...], gv, m_refs[i][...], v_refs[i][...])
                g_out[...], d_out[...], m_out[...], v_out[...] = gv, dl, mn, vn

    out_shape = [jax.ShapeDtypeStruct((D, dmod_cols.shape[1]), F32), jax.ShapeDtypeStruct((4, CONV_DIM), F32),
                 jax.ShapeDtypeStruct((1, LANES), F32)]
    for p in params:
        out_shape += [jax.ShapeDtypeStruct(p.shape, F32)] * 4
    return pl.pallas_call(
        body, name="finalize_small", out_shape=out_shape,
        compiler_params=pltpu.CompilerParams(vmem_limit_bytes=VMEM_LIMIT),
    )(g_all, c_all, dmod_cols, *params, *m, *v)


def adam_update(w, g, m, v, name):
    rows, cols = w.shape
    tm, tc = _tile2(rows, cols)

    def body(w_ref, g_ref, m_ref, v_ref, d_ref, mo_ref, vo_ref):
        d_ref[...], mo_ref[...], vo_ref[...] = adamw(w_ref[...], g_ref[...], m_ref[...], v_ref[...])

    spec = pl.BlockSpec((tm, tc), lambda i, j: (i, j))
    return pl.pallas_call(
        body, name=name, grid=(rows // tm, cols // tc), in_specs=[spec] * 4, out_specs=[spec] * 3,
        out_shape=[jax.ShapeDtypeStruct((rows, cols), F32)] * 3,
        compiler_params=_params(("arbitrary", "arbitrary")),
    )(w, g, m, v)


def kernel(x, c, w_ada, b_ada, w_in, hgrn_lb, hgrn_gnorm, ssm_conv_w, ssm_conv_b, ssm_dt_bias, ssm_a_log, ssm_d, ssm_norm, w_branch_a, w_branch_b, w_o, ln1_g, ln1_b, w_ffn_gate, w_ffn_up, w_ffn_down, ln2_g, ln2_b, loss_target, m_w_ada, m_b_ada, m_w_in, m_hgrn_lb, m_hgrn_gnorm, m_ssm_conv_w, m_ssm_conv_b, m_ssm_dt_bias, m_ssm_a_log, m_ssm_d, m_ssm_norm, m_w_branch_a, m_w_branch_b, m_w_o, m_ln1_g, m_ln1_b, m_w_ffn_gate, m_w_ffn_up, m_w_ffn_down, m_ln2_g, m_ln2_b, v_w_ada, v_b_ada, v_w_in, v_hgrn_lb, v_hgrn_gnorm, v_ssm_conv_w, v_ssm_conv_b, v_ssm_dt_bias, v_ssm_a_log, v_ssm_d, v_ssm_norm, v_w_branch_a, v_w_branch_b, v_w_o, v_ln1_g, v_ln1_b, v_w_ffn_gate, v_w_ffn_up, v_w_ffn_down, v_ln2_g, v_ln2_b):
    given = dict(locals())
    chip = 2 * lax.axis_index("x") + lax.axis_index("y")
    core = lax.axis_index("c")
    t = x.shape[1]

    first = gather_rows(jnp.concatenate([c, ssm_conv_w.reshape(1, CONV_DIM)], axis=1), "gather_cond").reshape(8, D + CONV_DIM)
    c_all = first[:, :D]
    conv_w = first[0::2, D:].reshape(4, 4, CONV_DIM // 4).transpose(1, 0, 2).reshape(4, CONV_DIM)
    mod_part, lb_row = ada_prepare(c_all, w_ada[0], hgrn_lb)
    mod_cols = w_ada.shape[2]
    mod_row = exchange_rows(mod_part.reshape(8, 1, mod_cols), "exchange_mod").reshape(1, 6 * D) + b_ada

    local = {nm: given[nm][0] for nm in SHARDED if nm != "w_ffn_in"}
    local["w_ffn_in"] = jnp.concatenate([w_ffn_gate[0].T, w_ffn_up[0].T], axis=0)
    shards = [local[nm].astype(BF16) for nm in SHARDED]
    send_in, recv_in, sent_in, land_in, started_in = gather_start(shards[:1], mod_row, "in")
    shards = shards[:1] + [(local[nm] + started_in[0, 0]).astype(BF16) for nm in SHARDED[1:]]
    send_rest, recv_rest, *flying = gather_start(shards[1:], started_in, "rest")
    n_rest = len(SHARDED) - 1
    sent_rest, land_rest, started_rest = flying[:n_rest], flying[n_rest:2 * n_rest], flying[-1]
    mod_row = mod_row + started_rest[0:1, 0:1]
    mod = tuple(mod_row[:, i * D:(i + 1) * D] for i in range(6))
    with_own = lambda land, shard: lax.dynamic_update_slice(land, shard[None], (chip, 0, 0))

    class Weights:
        def input_projection(self, after):
            (own,), land = gather_wait(send_in, recv_in, [sent_in], [land_in], after, "in")
            (land,) = forward_wait(forward_start(land, "in"), after, "in")
            return assemble_in_proj(land, own, chip)

        def start_rest(self, after):
            self.own, landed = gather_wait(send_rest, recv_rest, sent_rest, land_rest, after, "rest")
            self.started = forward_start(landed, "rest")
            return self.started[-1]

        def rest(self, after):
            got = {nm: with_own(land, s) for nm, land, s in zip(SHARDED[1:], forward_wait(self.started, after, "rest"), self.own, strict=True)}
            whole = lambda nm: got[nm].reshape(4 * got[nm].shape[1], got[nm].shape[2])
            return tuple(whole(nm) for nm in SHARDED[1:])

    wts = Weights()

    per_head = lambda p: jnp.pad(p, ((0, 0), (0, LANES - p.shape[1])))
    per_channel = lambda p: jnp.repeat(p[0], B_INNER // 32)[None]
    small = (lb_row, hgrn_gnorm, conv_w, ssm_conv_b, per_head(ssm_dt_bias), per_channel(ssm_a_log),
             per_channel(ssm_d), ssm_norm, ln1_g, ln1_b, ln2_g, ln2_b)
    by_rows = lambda g: g.reshape(4, g.shape[0] // 4, g.shape[1])
    travelling = {}

    def start_early(dws):
        travelling["pair"] = pair_start([by_rows(dw) for dw in dws], "early")
        return travelling["pair"][-1]

    def between_scans(after):
        slabs, received = pair_wait(travelling["pair"], after, "early")
        travelling["pairs"] = [pair_add(s, r, core, "pair_add_" + nm) for nm, s, r in zip(SHARDED[1:], slabs, received, strict=True)]
        travelling["started"] = scatter_start(travelling["pairs"], "early")
        return travelling["started"][-1]

    def finish_early(after):
        travelling["pairs"], travelling["landed"] = scatter_wait(travelling["started"], after, "early")

    def start_last(dw_in):
        travelling["pairs_in"] = [split_pair_add(dw_in, rows_exchange(dw_in, "pair_exchange_last"), core)]
        travelling["started_in"] = scatter_start(travelling["pairs_in"], "last")
        return travelling["started_in"][-1]

    loss, grad_x, d_mod, d_wts, d_small = local_step(x[0], loss_target[0], mod, wts, small,
                                                     start_early, between_scans, finish_early, start_last)

    d_lb, d_gn, d_cw, d_cb, d_dtb, d_alog, d_dsk, d_nw, d_l1g, d_l1b, d_l2g, d_l2b = d_small
    row = jnp.concatenate(list(d_mod) + [d_lb, d_gn, d_cw.reshape(1, 4 * CONV_DIM), d_cb, d_dtb, d_alog, d_dsk, d_nw,
                                          d_l1g, d_l1b, d_l2g, d_l2b, jnp.pad(loss, ((0, 0), (0, LANES - 1)))], axis=1)
    g_all = gather_rows(row, "gather_small_grads").reshape(8, row.shape[1])
    dmod_cols = lax.dynamic_slice_in_dim(g_all, chip * mod_cols, mod_cols, axis=1)
    fin = finalize_small(g_all, c_all, dmod_cols, [given[n] for n in SMALL_PARAMS],
                         [given["m_" + n] for n in SMALL_PARAMS], [given["v_" + n] for n in SMALL_PARAMS])
    grads, deltas, new_m, new_v = {}, {}, {}, {}
    grads["w_ada"] = fin[0][None]
    grads["ssm_conv_w"] = lax.dynamic_slice_in_dim(fin[1], chip * (CONV_DIM // 4), CONV_DIM // 4, axis=1)[None]
    for i, nm in enumerate(SMALL_PARAMS):
        grads[nm], deltas[nm], new_m[nm], new_v[nm] = fin[3 + 4 * i:7 + 4 * i]

    pairs_in, landed_in = scatter_wait(travelling["started_in"], fin[3], "last")
    pairs, landed = pairs_in + travelling["pairs"], landed_in + travelling["landed"]
    halves = [sum_chips(r, p, chip, core, "sum_chips_" + nm) for nm, r, p in zip(SHARDED, landed, pairs, strict=True)]
    reduced = dict(zip(SHARDED, exchange_halves(halves), strict=True))
    reduced["w_ada"], reduced["ssm_conv_w"] = grads["w_ada"][0], grads["ssm_conv_w"][0]
    reduced["w_in"] = reduced["w_in"].T
    reduced["w_ffn_gate"], reduced["w_ffn_up"] = reduced["w_ffn_in"][:FFN_SHARD], reduced["w_ffn_in"][FFN_SHARD:]
    for nm in ("w_ada", "ssm_conv_w", "w_in", "w_branch_a", "w_branch_b", "w_o", "w_ffn_gate", "w_ffn_up", "w_ffn_down"):
        flipped = nm in ("w_in", "w_ffn_gate", "w_ffn_up")
        work = (lambda a: a[0].T) if flipped else (lambda a: a[0])
        back = (lambda a: a.T[None]) if flipped else (lambda a: a[None])
        d_, m_, v_ = adam_update(work(given[nm]), reduced[nm], work(given["m_" + nm]), work(given["v_" + nm]), "adam_" + nm)
        grads[nm], deltas[nm], new_m[nm], new_v[nm] = back(reduced[nm]), back(d_), back(m_), back(v_)

    names = ("w_ada", "b_ada", "w_in", "hgrn_lb", "hgrn_gnorm", "ssm_conv_w", "ssm_conv_b", "ssm_dt_bias", "ssm_a_log",
             "ssm_d", "ssm_norm", "w_branch_a", "w_branch_b", "w_o", "ln1_g", "ln1_b", "w_ffn_gate", "w_ffn_up",
             "w_ffn_down", "ln2_g", "ln2_b")
    return (fin[2][0, 0], grad_x[None], *[grads[n] for n in names], *[deltas[n] for n in names],
            *[new_m[n] for n in names], *[new_v[n] for n in names])
```

```python
import functools

import jax
import jax.numpy as jnp
from jax import lax
from jax.experimental import pallas as pl
from jax.experimental.pallas import tpu as pltpu

F32, BF16 = jnp.float32, jnp.bfloat16
HI = lax.Precision.HIGHEST
MESH = pl.DeviceIdType.MESH

D = 1024
CHUNK = 64
LANES = 128
N_HEADS_A = 8
N_GROUPS_B = 4
B_INNER = 2048
CONV_DIM = 3072
D_FF = 2816
ALPHA = 2.0 ** 0.25
LN_EPS = 1e-5
RMS_EPS = 1e-6
ADAM_LR, ADAM_B1, ADAM_B2, ADAM_EPS, ADAM_WD, ADAM_STEP = 0.001, 0.9, 0.999, 1e-08, 0.01, 10

IN_ORIG = 11296
IN_PAD = 11520
COL_GA, COL_GB, COL_XBC, COL_Z, COL_DT = 4096, 5120, 6144, 9216, 11264
ORIG_Z, ORIG_XBC, ORIG_DT, ORIG_GA = 4096, 6144, 9216, 9248

SHARDED = ("w_in", "w_branch_a", "w_branch_b", "w_o", "w_ffn_in", "w_ffn_down")
FFN_SHARD = D_FF // 4
VMEM_LIMIT = 56 * 1024 * 1024
BLOCK_BYTES = 2 * 1024 * 1024
_DIMS = {"nn": (((1,), (0,)), ((), ())), "nt": (((1,), (1,)), ((), ())), "tn": (((0,), (0,)), ((), ()))}


def _bd(a, b, mode):
    return lax.dot_general(a.astype(BF16), b.astype(BF16), _DIMS[mode], preferred_element_type=F32)


@functools.partial(jax.custom_vjp, nondiff_argnums=(2,))
def bdot(a, b, mode):
    return _bd(a, b, mode)


def _bdot_fwd(a, b, mode):
    return _bd(a, b, mode), (a, b)


def _bdot_bwd(mode, res, g):
    a, b = res
    if mode == "nn":
        return _bd(g, b, "nt"), _bd(a, g, "tn")
    if mode == "nt":
        return _bd(g, b, "nn"), _bd(g, a, "tn")
    return _bd(b, g, "nt"), _bd(a, g, "nn")


bdot.defvjp(_bdot_fwd, _bdot_bwd)


def hdot(a, b, mode="nn"):
    return lax.dot_general(a, b, _DIMS[mode], precision=HI, preferred_element_type=F32)


def _raw(a, b, mode):
    return lax.dot_general(a, b, _DIMS[mode], preferred_element_type=F32)


def _split(x, n):
    parts, rest = [], x
    for _ in range(n):
        p = rest.astype(BF16)
        parts.append(p)
        rest = rest - p.astype(F32)
    return parts


def _od(a, b, mode, exact):
    if exact == 1:
        e = b.astype(BF16)
        p = _split(a, 3)
        return (_raw(p[2], e, mode) + _raw(p[1], e, mode)) + _raw(p[0], e, mode)
    e = a.astype(BF16)
    p = _split(b, 3)
    return (_raw(e, p[2], mode) + _raw(e, p[1], mode)) + _raw(e, p[0], mode)


@functools.partial(jax.custom_vjp, nondiff_argnums=(2, 3))
def odot(a, b, mode, exact):
    return _od(a, b, mode, exact)


def _odot_fwd(a, b, mode, exact):
    return _od(a, b, mode, exact), (a, b)


def _odot_bwd(mode, exact, res, g):
    a, b = res
    if exact == 1:
        da = {"nn": lambda: _od(g, b, "nt", 1), "nt": lambda: _od(g, b, "nn", 1), "tn": lambda: _od(b, g, "nt", 0)}[mode]()
        return da, jnp.zeros_like(b)
    db = {"nn": lambda: _od(a, g, "tn", 0), "nt": lambda: _od(g, a, "tn", 1), "tn": lambda: _od(a, g, "nn", 0)}[mode]()
    return jnp.zeros_like(a), db


odot.defvjp(_odot_fwd, _odot_bwd)


_BDIMS = {"bnn": (((2,), (1,)), ((0,), (0,))), "bnt": (((2,), (2,)), ((0,), (0,))), "btn": (((1,), (1,)), ((0,), (0,)))}


def _braw(a, b, mode):
    return lax.dot_general(a, b, _BDIMS[mode], preferred_element_type=F32)


def _bdb(a, b, mode):
    return _braw(a.astype(BF16), b.astype(BF16), mode)


def _d3b(a, b, mode):
    ah, al = _split(a, 2)
    bh, bl = _split(b, 2)
    return _braw(ah, bh, mode) + (_braw(ah, bl, mode) + _braw(al, bh, mode))


def _batched_bwd(f):
    def bwd(mode, res, g):
        a, b = res
        if mode == "bnn":
            return f(g, b, "bnt"), f(a, g, "btn")
        if mode == "bnt":
            return f(g, b, "bnn"), f(g, a, "btn")
        return f(b, g, "bnt"), f(a, g, "bnn")
    return bwd


@functools.partial(jax.custom_vjp, nondiff_argnums=(2,))
def bdot_b(a, b, mode):
    return _bdb(a, b, mode)


bdot_b.defvjp(lambda a, b, mode: (_bdb(a, b, mode), (a, b)), _batched_bwd(_bdb))


@functools.partial(jax.custom_vjp, nondiff_argnums=(2,))
def dot3_b(a, b, mode):
    return _d3b(a, b, mode)


dot3_b.defvjp(lambda a, b, mode: (_d3b(a, b, mode), (a, b)), _batched_bwd(_d3b))


def _cum(tril3, x, mode):
    e = tril3.astype(BF16)
    p = _split(x, 3)
    return (_braw(e, p[2], mode) + _braw(e, p[1], mode)) + _braw(e, p[0], mode)


@jax.custom_vjp
def chunk_cumsum(tril3, x):
    return _cum(tril3, x, "bnn")


chunk_cumsum.defvjp(lambda t, x: (_cum(t, x, "bnn"), t), lambda t, g: (jnp.zeros_like(t), _cum(t, g, "btn")))


def _unstack(axis, n):
    @jax.custom_vjp
    def un(x):
        return tuple(lax.index_in_dim(x, i, axis, keepdims=False) for i in range(n))

    un.defvjp(lambda x: (un(x), None), lambda _, g: (jnp.stack(g, axis=axis),))
    return un


def _split_last(n, w):
    @jax.custom_vjp
    def sp(x):
        return tuple(x[..., i * w:(i + 1) * w] for i in range(n))

    sp.defvjp(lambda x: (sp(x), None), lambda _, g: (jnp.concatenate(g, axis=-1),))
    return sp


def sigmoid(x):
    return 0.5 * jnp.tanh(0.5 * x) + 0.5


def silu(x):
    return x * sigmoid(x)


def softplus(x):
    return jnp.maximum(x, 0.0) + jnp.log1p(jnp.exp(jnp.minimum(x, -x)))


def _ln(x):
    mu = jnp.mean(x, axis=-1, keepdims=True)
    xc = x - mu
    return xc * lax.rsqrt(jnp.mean(xc * xc, axis=-1, keepdims=True) + LN_EPS)


def _tril64():
    r = lax.broadcasted_iota(jnp.int32, (CHUNK, CHUNK), 0)
    c = lax.broadcasted_iota(jnp.int32, (CHUNK, CHUNK), 1)
    return (r >= c).astype(F32)


def hgrn_block(q, fl, iv, gr, st, lb, gn):
    tb = q.shape[0]
    nc = tb // CHUNK
    nh = N_HEADS_A
    heads = _split_last(nh, LANES)
    to4 = lambda a: jnp.stack(heads(a), axis=0).reshape(nh, nc, CHUNK, LANES)
    flat = lambda a: a.reshape(nh * nc, CHUNK, LANES)
    f = lb + (1.0 - lb) * sigmoid(fl)
    gl4, k4, qf4, v4, gr4 = to4(jnp.log(f)), to4(1.0 - f), to4(silu(q) * (128 ** -0.5)), to4(iv), to4(gr)
    tril = _tril64()
    b4 = chunk_cumsum(jnp.broadcast_to(tril[None], (nh * nc, CHUNK, CHUNK)), flat(gl4)).reshape(gl4.shape)
    blast = jnp.sum(gl4, axis=2, keepdims=True)
    ref = lax.stop_gradient(0.5 * blast)
    qp, kp = qf4 * jnp.exp(b4 - ref), k4 * jnp.exp(ref - b4)
    sc = dot3_b(flat(qp), flat(kp), "bnt") * tril
    o_intra = bdot_b(sc, flat(v4), "bnn").reshape(gl4.shape)
    chunks = _unstack(1, nc)
    qe, v_c, kd, dec = chunks(qp * jnp.exp(ref)), chunks(v4), chunks(kp * jnp.exp(blast - ref)), chunks(jnp.exp(blast))
    o_inter = []
    for c in range(nc):
        o_inter.append(bdot_b(qe[c], st, "bnt"))
        st = st * dec[c] + bdot_b(v_c[c], kd[c], "btn")
    o = o_intra + jnp.stack(o_inter, axis=1)
    on = o * lax.rsqrt(jnp.mean(o * o, axis=-1, keepdims=True) + RMS_EPS) * gn
    out = (on * silu(gr4)).reshape(nh, tb, LANES)
    return jnp.concatenate(_unstack(0, nh)(out), axis=1), st


def ssd_consts(g):
    i32 = jnp.int32
    ej = lax.broadcasted_iota(i32, (LANES, 512), 0)
    ec = lax.broadcasted_iota(i32, (LANES, 512), 1)
    expand = (ej == g * 8 + (ec >> 6)).astype(F32)
    ts = lax.broadcasted_iota(i32, (CHUNK, 512), 0)
    tc = lax.broadcasted_iota(i32, (CHUNK, 512), 1)
    itile = (ts == (tc & 63)).astype(F32)
    maskall = ts >= (tc & 63)
    br = lax.broadcasted_iota(i32, (LANES, LANES), 0)
    bc = lax.broadcasted_iota(i32, (LANES, LANES), 1)
    blockmask = ((br >> 6) == (bc >> 6)).astype(F32)
    return expand, itile, maskall, blockmask, _tril64()


def ssd_block(x, bm, cm, dt, z, st, dtb, alog, dsk, nw, cs):
    expand, itile, maskall, blockmask, tril = cs
    tb = x.shape[0]
    nc = tb // CHUNK
    delta = odot(softplus(dt + dtb), expand, "nn", 1)
    a = -jnp.exp(alog) * delta
    xdt = x * delta
    by_chunk = lambda v: v.reshape(nc, CHUNK, v.shape[-1])
    a3, xdt3, bm3, cm3 = by_chunk(a), by_chunk(xdt), by_chunk(bm), by_chunk(cm)
    acum3 = chunk_cumsum(jnp.broadcast_to(tril[None], (nc, CHUNK, CHUNK)), a3)
    alast3 = jnp.sum(a3, axis=1, keepdims=True)
    cb3 = bdot_b(cm3, jnp.concatenate([bm3] * 8, axis=1), "bnt")
    arow3 = jnp.sum(acum3 * itile, axis=1, keepdims=True)
    dec3 = jnp.exp(jnp.where(maskall, acum3 - arow3, -1e30))
    pairs = _split_last(4, LANES)
    intra = [bdot_b(m, jnp.concatenate([xp] * 2, axis=1) * blockmask, "bnn")
             for m, xp in zip(pairs(cb3 * dec3), pairs(xdt3))]
    chunks = _unstack(0, nc)
    cm_c, bm_c, xw_c, dec_c = chunks(cm3), chunks(bm3), chunks(xdt3 * jnp.exp(alast3 - acum3)), chunks(jnp.exp(alast3))
    inter = []
    for c in range(nc):
        inter.append(bdot(cm_c[c], st, "nn"))
        st = st * dec_c[c] + bdot(bm_c[c], xw_c[c], "tn")
    st_new = st
    y = (jnp.concatenate(intra, axis=-1) + jnp.stack(inter, axis=0) * jnp.exp(acum3)).reshape(tb, 512)
    yz = (y + x * dsk) * silu(z)
    return yz * lax.rsqrt(jnp.mean(yz * yz, axis=-1, keepdims=True) + RMS_EPS) * nw, st_new


def adamw(w, g, m, v):
    m = ADAM_B1 * m + (1.0 - ADAM_B1) * g
    v = ADAM_B2 * v + (1.0 - ADAM_B2) * jnp.square(g)
    m_hat = m / (1.0 - ADAM_B1 ** ADAM_STEP)
    v_hat = v / (1.0 - ADAM_B2 ** ADAM_STEP)
    return -ADAM_LR * (m_hat / (jnp.sqrt(v_hat) + ADAM_EPS) + ADAM_WD * w), m, v


def _pick(n, cands):
    for c in cands:
        if n % c == 0:
            return c
    return n


def _params(sem):
    return pltpu.CompilerParams(dimension_semantics=sem, vmem_limit_bytes=VMEM_LIMIT)


MATMUL_VMEM_BUDGET = 50 * 1024 * 1024
MATMUL_MIN_STEPS = 4


def matmul(a, b, mode, out_dtype, name, after=None):
    if mode == "nn":
        (m, k), n = a.shape, b.shape[1]
    elif mode == "nt":
        (m, k), n = a.shape, b.shape[0]
    else:
        (k, m), n = a.shape, b.shape[1]
    tk = _pick(k, (2304, 2048, 1408, 1024, 768, 512, 256, 128))
    nk = k // tk
    a_bytes, b_bytes, out_bytes = a.dtype.itemsize, b.dtype.itemsize, jnp.dtype(out_dtype).itemsize

    def vmem(tm_, tn_):
        blocks = 2 * (tm_ * tk * a_bytes + tk * tn_ * b_bytes + tm_ * tn_ * out_bytes)
        return blocks + (tm_ * tn_ * 4 if nk > 1 else 0)

    def traffic(tm_, tn_):
        return (m // tm_) * k * n * b_bytes + (n // tn_ if nk > 1 else 1) * m * k * a_bytes

    sizes = (2304, 2048, 1920, 1408, 1024, 768, 512, 256, 128)
    tiles = [(tm_, tn_) for tm_ in sizes if m % tm_ == 0 for tn_ in sizes if n % tn_ == 0
             if vmem(tm_, tn_) <= MATMUL_VMEM_BUDGET] or [(m, n)]
    pipelined = [t for t in tiles if (m // t[0]) * (n // t[1]) * nk >= MATMUL_MIN_STEPS]
    tm, tn = min(pipelined or tiles, key=lambda t: (traffic(*t), -t[0] * t[1]))
    a_spec = pl.BlockSpec((tk, tm), lambda i, j, kk: (kk, i)) if mode == "tn" else pl.BlockSpec((tm, tk), lambda i, j, kk: (i, kk))
    b_spec = pl.BlockSpec((tn, tk), lambda i, j, kk: (j, kk)) if mode == "nt" else pl.BlockSpec((tk, tn), lambda i, j, kk: (kk, j))

    order = [] if after is None else [after]

    def body(a_ref, b_ref, *rest):
        o_ref, *acc = rest[len(order):]
        part = _bd(a_ref[...], b_ref[...], mode)
        if nk == 1:
            o_ref[...] = part.astype(o_ref.dtype)
            return
        acc_ref, = acc
        kk = pl.program_id(2)

        @pl.when(kk == 0)
        def _():
            acc_ref[...] = part

        @pl.when(jnp.logical_and(kk > 0, kk < nk - 1))
        def _():
            acc_ref[...] += part

        @pl.when(kk == nk - 1)
        def _():
            o_ref[...] = (acc_ref[...] + part).astype(o_ref.dtype)

    return pl.pallas_call(
        body, name=name, grid=(m // tm, n // tn, nk),
        in_specs=[a_spec, b_spec] + [pl.BlockSpec(memory_space=pl.ANY) for _ in order],
        out_specs=pl.BlockSpec((tm, tn), lambda i, j, kk: (i, j)),
        out_shape=jax.ShapeDtypeStruct((m, n), out_dtype),
        scratch_shapes=[pltpu.VMEM((tm, tn), F32)] if nk > 1 else [],
        compiler_params=_params(("parallel", "parallel", "arbitrary")),
    )(a, b, *order)


def rowwise(name, fn, rows, consts, out_rows, out_accs=(), tm_max=512, into=None, new_wide=None):
    t = rows[0][0].shape[0]
    tm = _pick(t, (tm_max, 128, 64, 32, 16, 8))
    n_r, n_c, n_o = len(rows), len(consts), len(out_rows)
    n_alias = 0 if into is None else 1

    def body(*refs):
        r_in = [r[...] for r in refs[:n_r]]
        c_in = [r[...] for r in refs[n_r:n_r + n_c]]
        refs = refs[:n_r + n_c] + refs[n_r + n_c + n_alias:]
        o_refs = refs[n_r + n_c:n_r + n_c + n_o]
        a_refs = refs[n_r + n_c + n_o:]
        ro, ao = fn(r_in, c_in)
        for ref, val in zip(o_refs, ro, strict=True):
            ref[...] = val.astype(ref.dtype)
        if a_refs:
            @pl.when(pl.program_id(0) == 0)
            def _():
                for ref in a_refs:
                    ref[...] = jnp.zeros_like(ref)

            for ref, val in zip(a_refs, ao, strict=True):
                ref[...] += val

    in_specs = [pl.BlockSpec((tm, w), functools.partial(lambda i, cb: (i, cb), cb=cb)) for _, w, cb in rows]
    in_specs += [pl.BlockSpec(c.shape, lambda i: (0, 0)) for c in consts]
    out_specs = [pl.BlockSpec((tm, w), lambda i: (i, 0)) for w, _ in out_rows]
    out_specs += [pl.BlockSpec(s, lambda i: (0, 0)) for s in out_accs]
    out_shape = [jax.ShapeDtypeStruct((t, w), dt) for w, dt in out_rows]
    out_shape += [jax.ShapeDtypeStruct(s, F32) for s in out_accs]
    operands = [r[0] for r in rows] + list(consts)
    aliases = {}
    if into is not None:
        target, cb = into
        in_specs.append(pl.BlockSpec(memory_space=pl.ANY))
        operands.append(target)
        out_specs[0] = pl.BlockSpec((tm, out_rows[0][0]), lambda i: (i, cb))
        out_shape[0] = jax.ShapeDtypeStruct(target.shape, target.dtype)
        aliases = {len(operands) - 1: 0}
    if new_wide is not None:
        width, cb = new_wide
        out_specs[0] = pl.BlockSpec((tm, out_rows[0][0]), lambda i: (i, cb))
        out_shape[0] = jax.ShapeDtypeStruct((t, width), out_rows[0][1])
    return pl.pallas_call(
        body, name=name, grid=(t // tm,), in_specs=in_specs, out_specs=out_specs, out_shape=out_shape,
        input_output_aliases=aliases, compiler_params=_params(("arbitrary",)),
    )(*operands)


def _full(a):
    return (a, a.shape[1], 0)


HGRN_TIME_BLOCK = 256
SSD_TIME_BLOCK = 512


def _time_block(t, most=HGRN_TIME_BLOCK):
    return _pick(t, tuple(b for b in (512, 256, 128, 64) if b <= most))


def _quarters(ref):
    return [ref[:, seg * D:(seg + 1) * D] for seg in range(4)]


def hgrn_forward(proj, lb, gn):
    t = proj.shape[0]
    tb = _time_block(t)
    nb = t // tb

    def body(qfig_ref, lb_ref, gn_ref, o_ref, st_ref, state):
        @pl.when(pl.program_id(0) == 0)
        def _():
            state[...] = jnp.zeros_like(state)

        st = state[...]
        st_ref[...] = st
        out, st_new = hgrn_block(*_quarters(qfig_ref), st, lb_ref[...], gn_ref[...])
        o_ref[...] = out.astype(o_ref.dtype)
        state[...] = st_new

    return pl.pallas_call(
        body, name="hgrn_forward", grid=(nb,),
        in_specs=[pl.BlockSpec((tb, 4 * D), lambda j: (j, 0)),
                  pl.BlockSpec((1, D), lambda j: (0, 0)), pl.BlockSpec((1, LANES), lambda j: (0, 0))],
        out_specs=[pl.BlockSpec((tb, D), lambda j: (j, 0)),
                   pl.BlockSpec((None, N_HEADS_A, LANES, LANES), lambda j: (j, 0, 0, 0))],
        out_shape=[jax.ShapeDtypeStruct((t, D), BF16),
                   jax.ShapeDtypeStruct((nb, N_HEADS_A, LANES, LANES), F32)],
        scratch_shapes=[pltpu.VMEM((N_HEADS_A, LANES, LANES), F32)],
        compiler_params=_params(("arbitrary",)),
    )(proj, lb, gn)


def hgrn_backward(proj, states, d_out, lb, gn, d_proj):
    t = proj.shape[0]
    tb = _time_block(t)
    nb = t // tb

    def body(qfig_ref, st_ref, do_ref, lb_ref, gn_ref, _, dqfig_ref, dlb_ref, dgn_ref, d_state):
        @pl.when(pl.program_id(0) == 0)
        def _():
            d_state[...] = jnp.zeros_like(d_state)
            dlb_ref[...] = jnp.zeros_like(dlb_ref)
            dgn_ref[...] = jnp.zeros_like(dgn_ref)

        _, vjp = jax.vjp(hgrn_block, *_quarters(qfig_ref), st_ref[...], lb_ref[...], gn_ref[...])
        dq, df, di, dg, dst, dlb, dgn = vjp((do_ref[...], d_state[...]))
        for seg, val in enumerate((dq, df, di, dg)):
            dqfig_ref[:, seg * D:(seg + 1) * D] = val.astype(dqfig_ref.dtype)
        d_state[...] = dst
        dlb_ref[...] += dlb
        dgn_ref[...] += dgn

    rev = lambda j: nb - 1 - j
    return pl.pallas_call(
        body, name="hgrn_backward", grid=(nb,),
        in_specs=[pl.BlockSpec((tb, 4 * D), lambda j: (rev(j), 0)),
                  pl.BlockSpec((None, N_HEADS_A, LANES, LANES), lambda j: (rev(j), 0, 0, 0)),
                  pl.BlockSpec((tb, D), lambda j: (rev(j), 0)),
                  pl.BlockSpec((1, D), lambda j: (0, 0)), pl.BlockSpec((1, LANES), lambda j: (0, 0)),
                  pl.BlockSpec(memory_space=pl.ANY)],
        out_specs=[pl.BlockSpec((tb, 4 * D), lambda j: (rev(j), 0)),
                   pl.BlockSpec((1, D), lambda j: (0, 0)), pl.BlockSpec((1, LANES), lambda j: (0, 0))],
        out_shape=[jax.ShapeDtypeStruct(d_proj.shape, d_proj.dtype), jax.ShapeDtypeStruct((1, D), F32),
                   jax.ShapeDtypeStruct((1, LANES), F32)],
        input_output_aliases={5: 0},
        scratch_shapes=[pltpu.VMEM((N_HEADS_A, LANES, LANES), F32)],
        compiler_params=_params(("arbitrary",)),
    )(proj, states, d_out, lb, gn, d_proj)


def _ssd_in_specs(tb, tmap):
    return [pl.BlockSpec((tb, 512), lambda g, j: (tmap(j), g)),
            pl.BlockSpec((tb, LANES), lambda g, j: (tmap(j), 16 + g)),
            pl.BlockSpec((tb, LANES), lambda g, j: (tmap(j), 20 + g)),
            pl.BlockSpec((tb, LANES), lambda g, j: (tmap(j), COL_DT // LANES)),
            pl.BlockSpec((tb, 512), lambda g, j: (tmap(j), COL_Z // 512 + g))]


def ssd_forward(xc, proj, dtb, alog, dsk, nw):
    t = proj.shape[0]
    tb = _time_block(t, SSD_TIME_BLOCK)
    nb = t // tb

    def body(x_ref, b_ref, c_ref, dt_ref, z_ref, dtb_ref, alog_ref, dsk_ref, nw_ref, o_ref, st_ref, state):
        @pl.when(pl.program_id(1) == 0)
        def _():
            state[...] = jnp.zeros_like(state)

        st = state[...]
        st_ref[...] = st
        out, st_new = ssd_block(x_ref[...], b_ref[...], c_ref[...], dt_ref[...], z_ref[...], st,
                                dtb_ref[...], alog_ref[...], dsk_ref[...], nw_ref[...], ssd_consts(pl.program_id(0)))
        o_ref[...] = out.astype(o_ref.dtype)
        state[...] = st_new

    vec = pl.BlockSpec((1, 512), lambda g, j: (0, g))
    heads = pl.BlockSpec((1, LANES), lambda g, j: (0, 0))
    return pl.pallas_call(
        body, name="ssd_forward", grid=(N_GROUPS_B, nb),
        in_specs=_ssd_in_specs(tb, lambda j: j) + [heads, vec, vec, vec],
        out_specs=[pl.BlockSpec((tb, 512), lambda g, j: (j, g)),
                   pl.BlockSpec((None, None, LANES, 512), lambda g, j: (j, g, 0, 0))],
        out_shape=[jax.ShapeDtypeStruct((t, B_INNER), BF16),
                   jax.ShapeDtypeStruct((nb, N_GROUPS_B, LANES, 512), F32)],
        scratch_shapes=[pltpu.VMEM((LANES, 512), F32)],
        compiler_params=_params(("arbitrary", "arbitrary")),
    )(xc, xc, xc, proj, proj, dtb, alog, dsk, nw)


def ssd_backward(xc, proj, states, d_out, dtb, alog, dsk, nw, d_proj):
    t = proj.shape[0]
    tb = _time_block(t, SSD_TIME_BLOCK)
    nb = t // tb
    rev = lambda j: nb - 1 - j

    def body(x_ref, b_ref, c_ref, dt_ref, z_ref, st_ref, do_ref, dtb_ref, alog_ref, dsk_ref, nw_ref, _,
             dx_ref, db_ref, dc_ref, ddt_ref, dz_ref, ddtb_ref, dalog_ref, ddsk_ref, dnw_ref, d_state):
        accs = (ddtb_ref, dalog_ref, ddsk_ref, dnw_ref)

        @pl.when(pl.program_id(1) == 0)
        def _():
            d_state[...] = jnp.zeros_like(d_state)
            for ref in accs:
                ref[...] = jnp.zeros_like(ref)

        cs = ssd_consts(pl.program_id(0))
        fn = lambda *a: ssd_block(*a, cs)
        _, vjp = jax.vjp(fn, x_ref[...], b_ref[...], c_ref[...], dt_ref[...], z_ref[...], st_ref[...],
                         dtb_ref[...], alog_ref[...], dsk_ref[...], nw_ref[...])
        dx, db, dc, ddt, dz, dst, *dpar = vjp((do_ref[...], d_state[...]))
        dx_ref[...] = dx
        db_ref[...] = db
        dc_ref[...] = dc
        ddt_ref[...] = ddt
        dz_ref[...] = dz.astype(dz_ref.dtype)
        d_state[...] = dst
        for ref, val in zip(accs, dpar, strict=True):
            ref[...] += val

    vec = pl.BlockSpec((1, 512), lambda g, j: (0, g))
    heads = pl.BlockSpec((1, LANES), lambda g, j: (0, 0))
    acc = pl.BlockSpec((None, 1, 512), lambda g, j: (g, 0, 0))
    acc_heads = pl.BlockSpec((None, 1, LANES), lambda g, j: (g, 0, 0))
    return pl.pallas_call(
        body, name="ssd_backward", grid=(N_GROUPS_B, nb),
        in_specs=_ssd_in_specs(tb, rev)
        + [pl.BlockSpec((None, None, LANES, 512), lambda g, j: (rev(j), g, 0, 0)),
           pl.BlockSpec((tb, 512), lambda g, j: (rev(j), g))] + [heads, vec, vec, vec] + [pl.BlockSpec(memory_space=pl.ANY)],
        out_specs=[pl.BlockSpec((tb, 512), lambda g, j: (rev(j), g)),
                   pl.BlockSpec((tb, LANES), lambda g, j: (rev(j), g)),
                   pl.BlockSpec((tb, LANES), lambda g, j: (rev(j), g)),
                   pl.BlockSpec((None, tb, LANES), lambda g, j: (g, rev(j), 0)),
                   pl.BlockSpec((tb, 512), lambda g, j: (rev(j), COL_Z // 512 + g)), acc_heads, acc, acc, acc],
        out_shape=[jax.ShapeDtypeStruct((t, B_INNER), F32), jax.ShapeDtypeStruct((t, 512), F32),
                   jax.ShapeDtypeStruct((t, 512), F32), jax.ShapeDtypeStruct((N_GROUPS_B, t, LANES), F32),
                   jax.ShapeDtypeStruct(d_proj.shape, d_proj.dtype)]
        + [jax.ShapeDtypeStruct((N_GROUPS_B, 1, LANES), F32)] + [jax.ShapeDtypeStruct((N_GROUPS_B, 1, 512), F32)] * 3,
        input_output_aliases={11: 4},
        scratch_shapes=[pltpu.VMEM((LANES, 512), F32)],
        compiler_params=_params(("arbitrary", "arbitrary")),
    )(xc, xc, xc, proj, proj, states, d_out, dtb, alog, dsk, nw, d_proj)


CONV_HALO = 8


def _shift_down(halo_then_tile, s, tm):
    if s == 0:
        return halo_then_tile[CONV_HALO:CONV_HALO + tm]
    return pltpu.roll(halo_then_tile, s, 0)[CONV_HALO:CONV_HALO + tm]


def _conv_pre(cur, prev, w, b, tm):
    stacked = jnp.concatenate([prev, cur], axis=0)
    taps = [_shift_down(stacked, 3 - j, tm) for j in range(4)]
    pre = b + taps[0] * w[0:1] + taps[1] * w[1:2] + taps[2] * w[2:3] + taps[3] * w[3:4]
    return pre, taps


def _conv_specs(t, tm):
    per = tm // CONV_HALO
    cur = pl.BlockSpec((tm, CONV_DIM), lambda i: (i, COL_XBC // CONV_DIM))
    prev = pl.BlockSpec((CONV_HALO, CONV_DIM), lambda i: (jnp.maximum(i * per - 1, 0), COL_XBC // CONV_DIM))
    return cur, prev


def conv_forward(proj, w, b):
    t = proj.shape[0]
    tm = _pick(t, (256, 128, 64))

    def body(cur_ref, prev_ref, w_ref, b_ref, o_ref):
        prev = jnp.where(pl.program_id(0) == 0, 0.0, prev_ref[...])
        pre, _ = _conv_pre(cur_ref[...], prev, w_ref[...], b_ref[...], tm)
        o_ref[...] = silu(pre)

    cur, prev = _conv_specs(t, tm)
    return pl.pallas_call(
        body, name="conv_forward", grid=(t // tm,),
        in_specs=[cur, prev, pl.BlockSpec((4, CONV_DIM), lambda i: (0, 0)), pl.BlockSpec((1, CONV_DIM), lambda i: (0, 0))],
        out_specs=pl.BlockSpec((tm, CONV_DIM), lambda i: (i, 0)),
        out_shape=jax.ShapeDtypeStruct((t, CONV_DIM), F32),
        compiler_params=_params(("arbitrary",)),
    )(proj, proj, w, b)


def conv_backward(proj, dx, db_, dc_, w, b, d_proj):
    t = proj.shape[0]
    tm = _pick(t, (256, 128, 64))
    per = tm // CONV_HALO
    nt = t // tm
    rev = lambda i: nt - 1 - i

    def body(cur_ref, prev_ref, dx_ref, dbm_ref, dcm_ref, w_ref, b_ref, _, o_ref, dw_ref, dbias_ref, later):
        @pl.when(pl.program_id(0) == 0)
        def _():
            dw_ref[...] = jnp.zeros_like(dw_ref)
            dbias_ref[...] = jnp.zeros_like(dbias_ref)
            later[...] = jnp.zeros_like(later)

        first_tile = pl.program_id(0) == nt - 1
        for lo, hi, src in ((0, B_INNER, dx_ref), (B_INNER, B_INNER + 512, dbm_ref), (B_INNER + 512, CONV_DIM, dcm_ref)):
            cols = slice(lo, hi)
            prev = jnp.where(first_tile, 0.0, prev_ref[:, cols])
            w_ = w_ref[:, cols]
            pre, taps = _conv_pre(cur_ref[:, cols], prev, w_, b_ref[:, cols], tm)
            sg = sigmoid(pre)
            dpre = src[...] * (sg * (1.0 + pre * (1.0 - sg)))
            dbias_ref[:, cols] += jnp.sum(dpre, axis=0, keepdims=True)
            for j in range(4):
                dw_ref[j:j + 1, cols] += jnp.sum(dpre * taps[j], axis=0, keepdims=True)
            stacked = jnp.concatenate([dpre, later[:, cols]], axis=0)
            acc = dpre * w_[3:4]
            for j in range(3):
                acc = acc + pltpu.roll(stacked, tm + CONV_HALO - (3 - j), 0)[0:tm] * w_[j:j + 1]
            o_ref[:, cols] = acc.astype(o_ref.dtype)
            later[:, cols] = dpre[0:CONV_HALO]

    row = lambda w_: pl.BlockSpec((tm, w_), lambda i: (rev(i), 0))
    whole = lambda r: pl.BlockSpec((r, CONV_DIM), lambda i: (0, 0))
    return pl.pallas_call(
        body, name="conv_backward", grid=(nt,),
        in_specs=[pl.BlockSpec((tm, CONV_DIM), lambda i: (rev(i), COL_XBC // CONV_DIM)),
                  pl.BlockSpec((CONV_HALO, CONV_DIM), lambda i: (jnp.maximum(rev(i) * per - 1, 0), COL_XBC // CONV_DIM)),
                  row(B_INNER), row(512), row(512), whole(4), whole(1), pl.BlockSpec(memory_space=pl.ANY)],
        out_specs=[pl.BlockSpec((tm, CONV_DIM), lambda i: (rev(i), COL_XBC // CONV_DIM)), whole(4), whole(1)],
        out_shape=[jax.ShapeDtypeStruct(d_proj.shape, d_proj.dtype), jax.ShapeDtypeStruct((4, CONV_DIM), F32),
                   jax.ShapeDtypeStruct((1, CONV_DIM), F32)],
        input_output_aliases={7: 0},
        scratch_shapes=[pltpu.VMEM((CONV_HALO, CONV_DIM), F32)],
        compiler_params=_params(("arbitrary",)),
    )(proj, proj, dx, db_, dc_, w, b, d_proj)


def stage_modulate(x, sc, sh):
    return _ln(x) * (1.0 + sc) + sh


def stage_merge(ga, gb, ya, yb):
    return sigmoid(ga) * ya + sigmoid(gb) * yb


def stage_post_mixer(x, h, g1, ln_g, ln_b, sc2, sh2):
    x1 = _ln(ALPHA * x + g1 * h) * ln_g + ln_b
    return x1, _ln(x1) * (1.0 + sc2) + sh2


def stage_swiglu(a, b):
    return silu(a) * b


def gate_up(ab):
    w = FFN_SHARD
    return (jnp.concatenate([ab[:, 2 * w * k:2 * w * k + w] for k in range(4)], axis=1),
            jnp.concatenate([ab[:, 2 * w * k + w:2 * w * (k + 1)] for k in range(4)], axis=1))


def per_chip(gate, up):
    w = FFN_SHARD
    return jnp.concatenate([part[:, w * k:w * (k + 1)] for k in range(4) for part in (gate, up)], axis=1)


def stage_loss(x1, hf, tgt, g2, ln_g, ln_b):
    x2 = _ln(ALPHA * x1 + g2 * hf) * ln_g + ln_b
    return 0.5 * jnp.sum(jnp.mean(jnp.square(x2 - tgt), axis=-1, keepdims=True), axis=0, keepdims=True)


def local_step(x, tgt, mod, wts, small, early=None, mid=None, late=None, last=None):
    sh1, sc1, g1, sh2, sc2, g2 = mod
    lb, gn, conv_w, conv_b, dtb, alog, dsk, nw, ln1_g, ln1_b, ln2_g, ln2_b = small
    vec = (1, D)

    (u1,) = rowwise("modulate1", lambda r, c: ((stage_modulate(r[0], *c),), ()), [_full(x)], [sc1, sh1], [(D, BF16)])
    w_in = wts.input_projection(u1)
    proj = matmul(u1, w_in, "nn", F32, "in_proj")
    ya_in, st_a = hgrn_forward(proj, lb, gn + wts.start_rest(proj)[0:1])
    xc = conv_forward(proj, conv_w, conv_b)
    w_a, w_b, w_o, w_gu, w_d = wts.rest(xc)
    yb_in, st_b = ssd_forward(xc, proj, dtb, alog, dsk, nw)
    ya = matmul(ya_in, w_a, "nn", F32, "branch_a")
    yb = matmul(yb_in, w_b, "nn", F32, "branch_b")
    gate_rows = [(proj, D, COL_GA // D), (proj, D, COL_GB // D), _full(ya), _full(yb)]
    (merged,) = rowwise("merge", lambda r, c: ((stage_merge(*r),), ()), gate_rows, [], [(D, BF16)])
    h = matmul(merged, w_o, "nn", F32, "out_proj")
    post_consts = [g1, ln1_g, ln1_b, sc2, sh2]
    x1, u2 = rowwise("post_mixer", lambda r, c: (stage_post_mixer(*r, *c), ()), [_full(x), _full(h)], post_consts,
                     [(D, F32), (D, BF16)])
    ab = matmul(u2, w_gu, "nt", F32, "ffn_in")
    (p,) = rowwise("swiglu", lambda r, c: ((stage_swiglu(*gate_up(r[0])),), ()), [_full(ab)], [], [(D_FF, BF16)],
                   tm_max=256)
    hf = matmul(p, w_d, "nn", F32, "ffn_out")

    def loss_bwd(r, c):
        loss, vjp = jax.vjp(stage_loss, *r, *c)
        dx1, dhf, _, dg2, dlg, dlb_ = vjp(jnp.ones((1, 1), F32))
        return (dx1, dhf), (loss, dg2, dlg, dlb_)

    dx1, dhf, loss, dg2, dln2_g, dln2_b = rowwise(
        "loss_backward", loss_bwd, [_full(x1), _full(hf), _full(tgt)], [g2, ln2_g, ln2_b],
        [(D, F32), (D, BF16)], [(1, 1), vec, vec, vec])
    dp = matmul(dhf, w_d, "nt", F32, "ffn_out_dx")
    dw_d = matmul(p, dhf, "tn", F32, "ffn_out_dw")

    def swiglu_bwd(r, c):
        _, vjp = jax.vjp(stage_swiglu, *gate_up(r[0]))
        return (per_chip(*vjp(r[1])),), ()

    (dab,) = rowwise("swiglu_backward", swiglu_bwd, [_full(ab), _full(dp)], [], [(2 * D_FF, BF16)], tm_max=256)
    du2 = matmul(dab, w_gu, "nn", F32, "ffn_in_dx")
    dw_gu = matmul(dab, u2, "tn", F32, "ffn_in_dw")

    def post_bwd(r, c):
        _, vjp = jax.vjp(stage_post_mixer, r[0], r[1], *c)
        dx, dh, *dc = vjp((r[2], r[3]))
        return (dx, dh), tuple(dc)

    dx_a, dh, dg1, dln1_g, dln1_b, dsc2, dsh2 = rowwise(
        "post_mixer_backward", post_bwd, [_full(x), _full(h), _full(dx1), _full(du2)], post_consts,
        [(D, F32), (D, BF16)], [vec] * 5)
    dmerged = matmul(dh, w_o, "nt", F32, "out_proj_dx")
    dw_o = matmul(merged, dh, "tn", F32, "out_proj_dw")

    def merge_bwd(r, c):
        _, vjp = jax.vjp(stage_merge, *r[:4])
        dga, dgb, dya, dyb = vjp(r[4])
        return (jnp.concatenate([dga, dgb], axis=1), dya, dyb), ()

    dproj, dya, dyb = rowwise("merge_backward", merge_bwd, gate_rows + [_full(dmerged)], [],
                              [(2 * D, BF16), (D, BF16), (D, BF16)], new_wide=(IN_PAD, COL_GA // (2 * D)))
    dya_in = matmul(dya, w_a, "nt", F32, "branch_a_dx")
    dw_a = matmul(ya_in, dya, "tn", F32, "branch_a_dw")
    dyb_in = matmul(dyb, w_b, "nt", F32, "branch_b_dx")
    dw_b = matmul(yb_in, dyb, "tn", F32, "branch_b_dw")
    gn_after = gn if early is None else gn + early((dw_a, dw_b, dw_o, dw_gu, dw_d))[0:1]
    dproj, dlb, dgn = hgrn_backward(proj, st_a, dya_in, lb, gn_after, dproj)
    dtb_after = dtb if mid is None else dtb + mid(dlb)[0:1, 0:1]
    dxs, dbm, dcm, ddt, dproj, ddtb, dalog, ddsk, dnw = ssd_backward(xc, proj, st_b, dyb_in, dtb_after, alog, dsk, nw, dproj)
    dproj, dconv_w, dconv_b = conv_backward(proj, dxs, dbm, dcm, conv_w, conv_b, dproj)
    if late is not None:
        late(dconv_b)
    t = x.shape[0]
    tail = jnp.concatenate([jnp.sum(ddt, axis=0).astype(BF16), jnp.zeros((t, IN_PAD - COL_DT - LANES), BF16)], axis=1)
    dproj = lax.dynamic_update_slice(dproj, tail, (0, COL_DT))
    dw_in = matmul(u1, dproj, "tn", F32, "in_proj_dw")
    du1 = matmul(dproj, w_in, "nt", F32, "in_proj_dx", after=None if last is None else last(dw_in))

    def mod_bwd(r, c):
        _, vjp = jax.vjp(stage_modulate, r[0], *c)
        dx, dsc, dsh = vjp(r[1])
        return (dx + r[2],), (dsc, dsh)

    grad_x, dsc1, dsh1 = rowwise("modulate1_backward", mod_bwd, [_full(x), _full(du1), _full(dx_a)], [sc1, sh1],
                                 [(D, F32)], [vec, vec])
    d_mod = (dsh1, dsc1, dg1, dsh2, dsc2, dg2)
    d_wts = (dw_in, dw_a, dw_b, dw_o, dw_gu, dw_d)
    d_small = (dlb, dgn, dconv_w, dconv_b, jnp.sum(ddtb, axis=0),
               dalog.reshape(1, B_INNER), ddsk.reshape(1, B_INNER), dnw.reshape(1, B_INNER),
               dln1_g, dln1_b, dln2_g, dln2_b)
    return loss, grad_x, d_mod, d_wts, d_small


HBM = pl.BlockSpec(memory_space=pltpu.HBM)
SEM = pl.BlockSpec(memory_space=pltpu.SEMAPHORE)
DATAFLOW = pltpu.SideEffectType.DATAFLOW_SIDE_EFFECTING


def _place():
    return lax.axis_index("x"), lax.axis_index("y"), lax.axis_index("c")


def _other_chips(x, y):
    return [(1 - x, y), (x, 1 - y), (1 - x, 1 - y)]


def _remote(src, dst, send_sem, recv_sem, device):
    return pltpu.make_async_remote_copy(src_ref=src, dst_ref=dst, send_sem=send_sem, recv_sem=recv_sem,
                                        device_id=device, device_id_type=MESH)


def gather_rows(v, name):
    n = v.shape[1]

    def body(v_ref, out_ref, send_sems, recv_sems, local_sem):
        x, y, c = _place()
        mine = pltpu.make_async_copy(v_ref, out_ref.at[4 * x + 2 * y + c], local_sem)
        mine.start()
        sends, recvs = [], []
        for m in range(1, 8):
            px = 1 - x if m & 4 else x
            py = 1 - y if m & 2 else y
            pc = 1 - c if m & 1 else c
            sends.append(_remote(v_ref, out_ref.at[4 * x + 2 * y + c], send_sems.at[m - 1], recv_sems.at[m - 1], (px, py, pc)))
            recvs.append(_remote(v_ref, out_ref.at[4 * px + 2 * py + pc], send_sems.at[m - 1], recv_sems.at[m - 1], (px, py, pc)))
        for cp in sends:
            cp.start()
        for cp in recvs:
            cp.wait_recv()
        for cp in sends:
            cp.wait_send()
        mine.wait()

    return pl.pallas_call(
        body, name=name, in_specs=[HBM], out_specs=HBM,
        out_shape=jax.ShapeDtypeStruct((8, 1, n), v.dtype),
        scratch_shapes=[pltpu.SemaphoreType.DMA((7,)), pltpu.SemaphoreType.DMA((7,)), pltpu.SemaphoreType.DMA],
    )(v)


def exchange_rows(part, name):
    w = part.shape[2]

    def body(p_ref, out_ref, send_sems, recv_sems, local_sem):
        x, y, c = _place()
        k = 2 * x + y
        mine = pltpu.make_async_copy(p_ref.at[4 * x + 2 * y + c], out_ref.at[k], local_sem)
        mine.start()
        sends, recvs = [], []
        for j, (px, py) in enumerate(_other_chips(x, y)):
            sends.append(_remote(p_ref.at[4 * px + 2 * py + c], out_ref.at[k], send_sems.at[j], recv_sems.at[j], (px, py, c)))
            recvs.append(_remote(p_ref.at[4 * px + 2 * py + c], out_ref.at[2 * px + py], send_sems.at[j], recv_sems.at[j], (px, py, c)))
        for cp in sends:
            cp.start()
        for cp in recvs:
            cp.wait_recv()
        for cp in sends:
            cp.wait_send()
        mine.wait()

    return pl.pallas_call(
        body, name=name, in_specs=[HBM], out_specs=HBM,
        out_shape=jax.ShapeDtypeStruct((4, 1, w), part.dtype),
        scratch_shapes=[pltpu.SemaphoreType.DMA((3,)), pltpu.SemaphoreType.DMA((3,)), pltpu.SemaphoreType.DMA],
    )(part)


def _half_of_slot(ref, rows, px, py, pc):
    return ref.at[2 * px + py, pl.ds(pc * (rows // 2), rows // 2), :]


def gather_start(shards, after, tag):
    n = len(shards)

    def body(*refs):
        w_refs, land_refs = refs[:n], refs[n:2 * n]
        send_sems, recv_sems = refs[2 * n + 1], refs[2 * n + 2]
        token = refs[-1]
        x, y, c = _place()
        for i in range(n):
            rows = shards[i].shape[0]
            for j, (px, py) in enumerate(_other_chips(x, y)):
                _remote(w_refs[i].at[pl.ds(c * (rows // 2), rows // 2), :], _half_of_slot(land_refs[i], rows, x, y, c),
                        send_sems.at[j * n + i], recv_sems.at[j * n + i], (px, py, c)).start()
        token[...] = jnp.zeros_like(token)

    hbm = lambda a: pltpu.with_memory_space_constraint(a, pltpu.HBM)
    lands = [lax.empty((4,) + s.shape, s.dtype) for s in shards]
    dma = pltpu.SemaphoreType.DMA
    return pl.pallas_call(
        body, name="gather_start_" + tag,
        out_shape=(dma((3 * n,)), dma((3 * n,)),
                   *[pltpu.HBM(a.shape, a.dtype) for a in list(shards) + lands], jax.ShapeDtypeStruct((8, LANES), F32)),
        in_specs=[HBM] * (2 * n) + [pl.BlockSpec(memory_space=pl.ANY)],
        out_specs=(SEM, SEM, *[HBM] * (2 * n), pl.BlockSpec(memory_space=pltpu.VMEM)),
        input_output_aliases={i: 2 + i for i in range(2 * n)},
        compiler_params=pltpu.CompilerParams(has_side_effects=DATAFLOW),
    )(*[hbm(a) for a in list(shards) + lands], after)


def gather_wait(send_sems, recv_sems, shards, lands, after, tag):
    n = len(shards)

    def body(*refs):
        w_refs, land_refs = refs[:n], refs[n:2 * n]
        send_ref, recv_ref = refs[2 * n], refs[2 * n + 1]
        x, y, c = _place()
        for i in range(n):
            rows = shards[i].shape[0]
            for j, (px, py) in enumerate(_other_chips(x, y)):
                cp = _remote(w_refs[i].at[pl.ds(c * (rows // 2), rows // 2), :], _half_of_slot(land_refs[i], rows, px, py, c),
                             send_ref.at[j * n + i], recv_ref.at[j * n + i], (px, py, c))
                cp.wait_send()
                cp.wait_recv()

    out = pl.pallas_call(
        body, name="gather_wait_" + tag,
        out_shape=tuple(pltpu.HBM(a.shape, a.dtype) for a in list(shards) + list(lands)),
        in_specs=[HBM] * (2 * n) + [SEM, SEM, pl.BlockSpec(memory_space=pl.ANY)], out_specs=tuple([HBM] * (2 * n)),
        input_output_aliases={i: i for i in range(2 * n)},
        compiler_params=pltpu.CompilerParams(has_side_effects=DATAFLOW),
    )(*shards, *lands, send_sems, recv_sems, after)
    return list(out[:n]), list(out[n:])


def forward_start(lands, tag):
    n = len(lands)

    def body(*refs):
        land_refs = refs[:n]
        send_sems, recv_sems = refs[n], refs[n + 1]
        token = refs[-1]
        x, y, c = _place()
        for i in range(n):
            rows = lands[i].shape[1]
            for j, (px, py) in enumerate(_other_chips(x, y)):
                mine = _half_of_slot(land_refs[i], rows, px, py, c)
                _remote(mine, mine, send_sems.at[j * n + i], recv_sems.at[j * n + i], (x, y, 1 - c)).start()
        token[...] = jnp.zeros_like(token)

    dma = pltpu.SemaphoreType.DMA
    return pl.pallas_call(
        body, name="forward_start_" + tag,
        out_shape=(dma((3 * n,)), dma((3 * n,)), *[pltpu.HBM(a.shape, a.dtype) for a in lands],
                   jax.ShapeDtypeStruct((8, LANES), F32)),
        in_specs=[HBM] * n, out_specs=(SEM, SEM, *[HBM] * n, pl.BlockSpec(memory_space=pltpu.VMEM)),
        input_output_aliases={i: 2 + i for i in range(n)},
        compiler_params=pltpu.CompilerParams(has_side_effects=DATAFLOW),
    )(*lands)


def forward_wait(started, after, tag):
    send_sems, recv_sems, *rest = started
    lands = rest[:-1]
    n = len(lands)

    def body(*refs):
        land_refs = refs[:n]
        send_ref, recv_ref = refs[n], refs[n + 1]
        x, y, c = _place()
        for i in range(n):
            rows = lands[i].shape[1]
            for j, (px, py) in enumerate(_other_chips(x, y)):
                cp = _remote(_half_of_slot(land_refs[i], rows, px, py, c), _half_of_slot(land_refs[i], rows, px, py, 1 - c),
                             send_ref.at[j * n + i], recv_ref.at[j * n + i], (x, y, 1 - c))
                cp.wait_send()
                cp.wait_recv()

    out = pl.pallas_call(
        body, name="forward_wait_" + tag,
        out_shape=tuple(pltpu.HBM(a.shape, a.dtype) for a in lands),
        in_specs=[HBM] * n + [SEM, SEM, pl.BlockSpec(memory_space=pl.ANY)], out_specs=tuple([HBM] * n),
        input_output_aliases={i: i for i in range(n)},
        compiler_params=pltpu.CompilerParams(has_side_effects=DATAFLOW),
    )(*lands, send_sems, recv_sems, after)
    return list(out)


def pair_start(slabs, tag):
    n = len(slabs)

    def body(*refs):
        g_refs, land_refs = refs[:n], refs[n:2 * n]
        send_sems, recv_sems = refs[2 * n], refs[2 * n + 1]
        token = refs[-1]
        x, y, c = _place()
        for i in range(n):
            hr = slabs[i].shape[1] // 2
            _remote(g_refs[i].at[:, pl.ds((1 - c) * hr, hr), :], land_refs[i], send_sems.at[i], recv_sems.at[i],
                    (x, y, 1 - c)).start()
        token[...] = jnp.zeros_like(token)

    hbm = lambda a: pltpu.with_memory_space_constraint(a, pltpu.HBM)
    lands = [lax.empty((4, s.shape[1] // 2, s.shape[2]), s.dtype) for s in slabs]
    dma = pltpu.SemaphoreType.DMA
    return pl.pallas_call(
        body, name="pair_start_" + tag,
        out_shape=(dma((n,)), dma((n,)), *[pltpu.HBM(a.shape, a.dtype) for a in list(slabs) + lands],
                   jax.ShapeDtypeStruct((8, LANES), F32)),
        in_specs=[HBM] * (2 * n), out_specs=(SEM, SEM, *[HBM] * (2 * n), pl.BlockSpec(memory_space=pltpu.VMEM)),
        input_output_aliases={i: 2 + i for i in range(2 * n)},
        compiler_params=pltpu.CompilerParams(has_side_effects=DATAFLOW),
    )(*[hbm(a) for a in list(slabs) + lands])


def pair_wait(started, after, tag):
    send_sems, recv_sems, *rest = started
    n = (len(rest) - 1) // 2
    slabs, lands = rest[:n], rest[n:2 * n]

    def body(*refs):
        g_refs, land_refs = refs[:n], refs[n:2 * n]
        send_ref, recv_ref = refs[2 * n], refs[2 * n + 1]
        x, y, c = _place()
        for i in range(n):
            hr = slabs[i].shape[1] // 2
            cp = _remote(g_refs[i].at[:, pl.ds((1 - c) * hr, hr), :], land_refs[i], send_ref.at[i], recv_ref.at[i], (x, y, 1 - c))
            cp.wait_send()
            cp.wait_recv()

    out = pl.pallas_call(
        body, name="pair_wait_" + tag,
        out_shape=tuple(pltpu.HBM(a.shape, a.dtype) for a in list(slabs) + list(lands)),
        in_specs=[HBM] * (2 * n) + [SEM, SEM, pl.BlockSpec(memory_space=pl.ANY)], out_specs=tuple([HBM] * (2 * n)),
        input_output_aliases={i: i for i in range(2 * n)},
        compiler_params=pltpu.CompilerParams(has_side_effects=DATAFLOW),
    )(*slabs, *lands, send_sems, recv_sems, after)
    return list(out[:n]), list(out[n:])


def _tile2(rows, cols):
    fits = lambda r, c: r * c * 4 <= BLOCK_BYTES
    if fits(rows, cols):
        return rows, cols
    for r in (1024, 512, 256, 128, 64):
        if rows % r == 0 and fits(r, cols):
            return r, cols
    return rows, next(cols // k for k in (2, 3, 4, 6, 8, 12, 16) if cols % (k * LANES) == 0 and fits(rows, cols // k))


def pair_add(g, p, c, name):
    _, hr, cols = p.shape
    tm, tc = _tile2(hr, cols)
    per = hr // tm

    def body(c_ref, g_ref, p_ref, o_ref):
        o_ref[...] = (g_ref[...] + p_ref[...]).astype(o_ref.dtype)

    return pl.pallas_call(
        body, name=name,
        grid_spec=pltpu.PrefetchScalarGridSpec(
            num_scalar_prefetch=1, grid=(4, per, cols // tc),
            in_specs=[pl.BlockSpec((None, tm, tc), lambda k, i, j, c_ref: (k, c_ref[0] * per + i, j)),
                      pl.BlockSpec((None, tm, tc), lambda k, i, j, c_ref: (k, i, j))],
            out_specs=pl.BlockSpec((None, tm, tc), lambda k, i, j, c_ref: (k, i, j))),
        out_shape=jax.ShapeDtypeStruct((4, hr, cols), BF16),
        compiler_params=_params(("arbitrary", "arbitrary", "arbitrary")),
    )(c.reshape(1).astype(jnp.int32), g, p)


def scatter_start(sums, tag):
    n = len(sums)

    def body(*refs):
        s_refs, land_refs = refs[:n], refs[n:2 * n]
        send_sems, recv_sems = refs[2 * n], refs[2 * n + 1]
        token = refs[-1]
        x, y, c = _place()
        k = 2 * x + y
        for i in range(n):
            for j, (px, py) in enumerate(_other_chips(x, y)):
                _remote(s_refs[i].at[2 * px + py], land_refs[i].at[k], send_sems.at[j * n + i], recv_sems.at[j * n + i],
                        (px, py, c)).start()
        token[...] = jnp.zeros_like(token)

    hbm = lambda a: pltpu.with_memory_space_constraint(a, pltpu.HBM)
    return pl.pallas_call(
        body, name="scatter_start_" + tag,
        out_shape=(pltpu.SemaphoreType.DMA((3 * n,)), pltpu.SemaphoreType.DMA((3 * n,)),
                   *[pltpu.HBM(s.shape, s.dtype) for s in sums], *[pltpu.HBM(s.shape, s.dtype) for s in sums],
                   jax.ShapeDtypeStruct((8, LANES), F32)),
        in_specs=[HBM] * (2 * n), out_specs=(SEM, SEM, *[HBM] * (2 * n), pl.BlockSpec(memory_space=pltpu.VMEM)),
        input_output_aliases={i: 2 + i for i in range(2 * n)},
        compiler_params=pltpu.CompilerParams(has_side_effects=DATAFLOW),
    )(*[hbm(s) for s in sums], *[hbm(lax.empty(s.shape, s.dtype)) for s in sums])


def scatter_wait(started, after, tag):
    send_sems, recv_sems, *rest = started
    n = (len(rest) - 1) // 2
    sums, lands = rest[:n], rest[n:2 * n]

    def body(*refs):
        s_refs, land_refs = refs[:n], refs[n:2 * n]
        send_ref, recv_ref = refs[2 * n], refs[2 * n + 1]
        x, y, c = _place()
        for i in range(n):
            for j, (px, py) in enumerate(_other_chips(x, y)):
                cp = _remote(s_refs[i].at[2 * px + py], land_refs[i].at[2 * px + py], send_ref.at[j * n + i],
                             recv_ref.at[j * n + i], (px, py, c))
                cp.wait_send()
                cp.wait_recv()

    out = pl.pallas_call(
        body, name="scatter_wait_" + tag,
        out_shape=tuple(pltpu.HBM(s.shape, s.dtype) for s in sums + lands),
        in_specs=[HBM] * (2 * n) + [SEM, SEM, pl.BlockSpec(memory_space=pl.ANY)], out_specs=tuple([HBM] * (2 * n)),
        input_output_aliases={i: i for i in range(2 * n)},
        compiler_params=pltpu.CompilerParams(has_side_effects=DATAFLOW),
    )(*sums, *lands, send_sems, recv_sems, after)
    return list(out[:n]), list(out[n:])


def sum_chips(landed, own, chip, core, name):
    _, hr, cols = landed.shape
    tm, tc = _tile2(hr, cols)
    per = hr // tm

    def body(idx_ref, l0, l1, l2, l3, own_ref, o_ref):
        mine = own_ref[...].astype(F32)
        v = [jnp.where(idx_ref[0] == k, mine, ref[...].astype(F32)) for k, ref in enumerate((l0, l1, l2, l3))]
        o_ref[...] = ((v[0] + v[1]) + v[2]) + v[3]

    slot = lambda k: pl.BlockSpec((None, tm, tc),
                                  lambda i, j, idx: (jnp.where(idx[0] == k, (k + 1) & 3, k), i, j))
    return pl.pallas_call(
        body, name=name,
        grid_spec=pltpu.PrefetchScalarGridSpec(
            num_scalar_prefetch=1, grid=(per, cols // tc),
            in_specs=[slot(0), slot(1), slot(2), slot(3),
                      pl.BlockSpec((None, tm, tc), lambda i, j, idx: (idx[0], i, j))],
            out_specs=pl.BlockSpec((tm, tc), lambda i, j, idx: (idx[1] * per + i, j))),
        out_shape=jax.ShapeDtypeStruct((2 * hr, cols), F32),
        compiler_params=_params(("arbitrary", "arbitrary")),
    )(jnp.stack([chip, core]).astype(jnp.int32), landed, landed, landed, landed, own)


def exchange_halves(bufs):
    n = len(bufs)

    def body(*refs):
        out_refs = refs[n:2 * n]
        send_sems, recv_sems = refs[2 * n:]
        x, y, c = _place()
        sends, recvs = [], []
        for i in range(n):
            hr = bufs[i].shape[0] // 2
            own = out_refs[i].at[pl.ds(c * hr, hr), :]
            other = out_refs[i].at[pl.ds((1 - c) * hr, hr), :]
            sends.append(_remote(own, own, send_sems.at[i], recv_sems.at[i], (x, y, 1 - c)))
            recvs.append(_remote(other, other, send_sems.at[i], recv_sems.at[i], (x, y, 1 - c)))
        for cp in sends:
            cp.start()
        for cp in recvs:
            cp.wait_recv()
        for cp in sends:
            cp.wait_send()

    return pl.pallas_call(
        body, name="exchange_halves", in_specs=[HBM] * n, out_specs=[HBM] * n,
        out_shape=[jax.ShapeDtypeStruct(b.shape, b.dtype) for b in bufs],
        input_output_aliases={i: i for i in range(n)},
        scratch_shapes=[pltpu.SemaphoreType.DMA((n,)), pltpu.SemaphoreType.DMA((n,))],
    )(*bufs)


def assemble_in_proj(landed, own, chip):
    rows, cols = 128, own.shape[1]

    def body(idx_ref, l0, l1, l2, l3, own_ref, o_ref):
        mine = own_ref[...]
        w = jnp.concatenate([jnp.where(idx_ref[0] == k, mine, ref[...]) for k, ref in enumerate((l0, l1, l2, l3))], axis=1)
        o_ref[...] = jnp.concatenate([w[:, :ORIG_Z], w[:, ORIG_GA:], w[:, ORIG_XBC:ORIG_DT], w[:, ORIG_Z:ORIG_XBC],
                                      w[:, ORIG_DT:ORIG_GA], jnp.zeros((rows, IN_PAD - IN_ORIG), w.dtype)], axis=1)

    slot = lambda k: pl.BlockSpec((None, rows, cols), lambda i, idx: (jnp.where(idx[0] == k, (k + 1) & 3, k), i, 0))
    return pl.pallas_call(
        body, name="assemble_in_proj",
        grid_spec=pltpu.PrefetchScalarGridSpec(
            num_scalar_prefetch=1, grid=(D // rows,),
            in_specs=[slot(0), slot(1), slot(2), slot(3), pl.BlockSpec((rows, cols), lambda i, idx: (i, 0))],
            out_specs=pl.BlockSpec((rows, IN_PAD), lambda i, idx: (i, 0))),
        out_shape=jax.ShapeDtypeStruct((D, IN_PAD), own.dtype),
        compiler_params=_params(("arbitrary",)),
    )(chip.reshape(1).astype(jnp.int32), landed, landed, landed, landed, own)


def rows_exchange(a, name):
    hr = a.shape[0] // 2

    def body(a_ref, out_ref, send_sem, recv_sem):
        x, y, c = _place()
        cp = _remote(a_ref.at[pl.ds((1 - c) * hr, hr), :], out_ref, send_sem, recv_sem, (x, y, 1 - c))
        cp.start()
        cp.wait()

    return pl.pallas_call(
        body, name=name, in_specs=[HBM], out_specs=HBM,
        out_shape=jax.ShapeDtypeStruct((hr, a.shape[1]), a.dtype),
        scratch_shapes=[pltpu.SemaphoreType.DMA, pltpu.SemaphoreType.DMA],
    )(a)


def split_pair_add(dw, received, core):
    cols = IN_ORIG // 4
    rows, hr = 128, D // 2
    per = hr // rows

    def body(c_ref, own_ref, got_ref, o_ref):
        d = own_ref[...] + got_ref[...]
        w = jnp.concatenate([d[:, :COL_GA], d[:, COL_Z:COL_DT], d[:, COL_XBC:COL_Z], d[:, COL_DT:COL_DT + 32],
                             d[:, COL_GA:COL_XBC]], axis=1)
        for k in range(4):
            o_ref[k] = w[:, k * cols:(k + 1) * cols].astype(o_ref.dtype)

    return pl.pallas_call(
        body, name="split_pair_add",
        grid_spec=pltpu.PrefetchScalarGridSpec(
            num_scalar_prefetch=1, grid=(per,),
            in_specs=[pl.BlockSpec((rows, IN_PAD), lambda i, c_ref: (c_ref[0] * per + i, 0)),
                      pl.BlockSpec((rows, IN_PAD), lambda i, c_ref: (i, 0))],
            out_specs=pl.BlockSpec((4, rows, cols), lambda i, c_ref: (0, i, 0))),
        out_shape=jax.ShapeDtypeStruct((4, hr, cols), BF16),
        compiler_params=_params(("arbitrary",)),
    )(core.reshape(1).astype(jnp.int32), dw, received)


def ada_prepare(c_all, w_ada, hgrn_lb):
    def body(c_ref, w_ref, lb_ref, mod_ref, row_ref):
        mod_ref[...] = hdot(silu(c_ref[...]), w_ref[...])
        row_ref[...] = sigmoid(lb_ref[0:1, :] - lb_ref[1:2, :])

    return pl.pallas_call(
        body, name="ada_prepare",
        out_shape=[jax.ShapeDtypeStruct((8, w_ada.shape[1]), F32), jax.ShapeDtypeStruct((1, D), F32)],
        compiler_params=pltpu.CompilerParams(vmem_limit_bytes=VMEM_LIMIT),
    )(c_all, w_ada, hgrn_lb)


SMALL_SEGS = (("mod", 6 * D), ("lb", D), ("gnorm", LANES), ("conv_w", 4 * CONV_DIM), ("conv_b", CONV_DIM),
              ("dt_bias", LANES), ("a_log", B_INNER), ("d", B_INNER), ("ssm_norm", B_INNER),
              ("ln1_g", D), ("ln1_b", D), ("ln2_g", D), ("ln2_b", D), ("loss", LANES))
SMALL_PARAMS = ("b_ada", "hgrn_lb", "hgrn_gnorm", "ssm_conv_b", "ssm_dt_bias", "ssm_a_log", "ssm_d", "ssm_norm",
                "ln1_g", "ln1_b", "ln2_g", "ln2_b")


def finalize_small(g_all, c_all, dmod_cols, params, m, v):
    n_p = len(SMALL_PARAMS)
    offs, o = {}, 0
    for nm, width in SMALL_SEGS:
        offs[nm] = (o, width)
        o += width

    def body(*refs):
        g_ref, c_ref, dm_ref = refs[:3]
        p_refs = refs[3:3 + n_p]
        m_refs = refs[3 + n_p:3 + 2 * n_p]
        v_refs = refs[3 + 2 * n_p:3 + 3 * n_p]
        outs = refs[3 + 3 * n_p:]
        gwa_ref, gcw_ref, loss_ref = outs[:3]
        res = outs[3:]
        total = jnp.sum(g_ref[...], axis=0, keepdims=True)
        seg = lambda nm: total[:, offs[nm][0]:offs[nm][0] + offs[nm][1]]
        loss_ref[...] = seg("loss")
        gwa_ref[...] = hdot(silu(c_ref[...]), dm_ref[...], "tn")
        cw = seg("conv_w")
        for j in range(4):
            gcw_ref[j:j + 1, :] = cw[:, j * CONV_DIM:(j + 1) * CONV_DIM]
        hc = lax.broadcasted_iota(jnp.int32, (B_INNER, LANES), 0)
        hj = lax.broadcasted_iota(jnp.int32, (B_INNER, LANES), 1)
        per_head = ((hc >> 6) == hj).astype(F32)
        heads = lambda nm: hdot(jnp.broadcast_to(seg(nm), (8, B_INNER)), per_head)[0:1, 0:32]
        lbp = sigmoid(p_refs[1][0:1, :] - p_refs[1][1:2, :])
        g_row = seg("lb") * lbp * (1.0 - lbp)
        grads = {"b_ada": seg("mod"), "hgrn_gnorm": seg("gnorm"), "ssm_conv_b": seg("conv_b"),
                 "ssm_dt_bias": seg("dt_bias")[:, 0:32], "ssm_a_log": heads("a_log"), "ssm_d": heads("d"),
                 "ssm_norm": seg("ssm_norm"), "ln1_g": seg("ln1_g"), "ln1_b": seg("ln1_b"),
                 "ln2_g": seg("ln2_g"), "ln2_b": seg("ln2_b")}
        for i, nm in enumerate(SMALL_PARAMS):
            g_out, d_out, m_out, v_out = res[4 * i:4 * i + 4]
            if nm == "hgrn_lb":
                for row, gv in ((0, g_row), (1, -g_row)):
                    sl = slice(row, row + 1)
                    dl, mn, vn = adamw(p_refs[i][sl, :], gv, m_refs[i][sl, :], v_refs[i][sl, :])
                    g_out[sl, :], d_out[sl, :], m_out[sl, :], v_out[sl, :] = gv, dl, mn, vn
            else:
                gv = grads[nm]
                dl, mn, vn = adamw(p_refs[i][...], gv, m_refs[i][...], v_refs[i][...])
                g_out[...], d_out[...], m_out[...], v_out[...] = gv, dl, mn, vn

    out_shape = [jax.ShapeDtypeStruct((D, dmod_cols.shape[1]), F32), jax.ShapeDtypeStruct((4, CONV_DIM), F32),
                 jax.ShapeDtypeStruct((1, LANES), F32)]
    for p in params:
        out_shape += [jax.ShapeDtypeStruct(p.shape, F32)] * 4
    return pl.pallas_call(
        body, name="finalize_small", out_shape=out_shape,
        compiler_params=pltpu.CompilerParams(vmem_limit_bytes=VMEM_LIMIT),
    )(g_all, c_all, dmod_cols, *params, *m, *v)


def adam_update(w, g, m, v, name):
    rows, cols = w.shape
    tm, tc = _tile2(rows, cols)

    def body(w_ref, g_ref, m_ref, v_ref, d_ref, mo_ref, vo_ref):
        d_ref[...], mo_ref[...], vo_ref[...] = adamw(w_ref[...], g_ref[...], m_ref[...], v_ref[...])

    spec = pl.BlockSpec((tm, tc), lambda i, j: (i, j))
    return pl.pallas_call(
        body, name=name, grid=(rows // tm, cols // tc), in_specs=[spec] * 4, out_specs=[spec] * 3,
        out_shape=[jax.ShapeDtypeStruct((rows, cols), F32)] * 3,
        compiler_params=_params(("arbitrary", "arbitrary")),
    )(w, g, m, v)


def kernel(x, c, w_ada, b_ada, w_in, hgrn_lb, hgrn_gnorm, ssm_conv_w, ssm_conv_b, ssm_dt_bias, ssm_a_log, ssm_d, ssm_norm, w_branch_a, w_branch_b, w_o, ln1_g, ln1_b, w_ffn_gate, w_ffn_up, w_ffn_down, ln2_g, ln2_b, loss_target, m_w_ada, m_b_ada, m_w_in, m_hgrn_lb, m_hgrn_gnorm, m_ssm_conv_w, m_ssm_conv_b, m_ssm_dt_bias, m_ssm_a_log, m_ssm_d, m_ssm_norm, m_w_branch_a, m_w_branch_b, m_w_o, m_ln1_g, m_ln1_b, m_w_ffn_gate, m_w_ffn_up, m_w_ffn_down, m_ln2_g, m_ln2_b, v_w_ada, v_b_ada, v_w_in, v_hgrn_lb, v_hgrn_gnorm, v_ssm_conv_w, v_ssm_conv_b, v_ssm_dt_bias, v_ssm_a_log, v_ssm_d, v_ssm_norm, v_w_branch_a, v_w_branch_b, v_w_o, v_ln1_g, v_ln1_b, v_w_ffn_gate, v_w_ffn_up, v_w_ffn_down, v_ln2_g, v_ln2_b):
    given = dict(locals())
    chip = 2 * lax.axis_index("x") + lax.axis_index("y")
    core = lax.axis_index("c")
    t = x.shape[1]

    first = gather_rows(jnp.concatenate([c, ssm_conv_w.reshape(1, CONV_DIM)], axis=1), "gather_cond").reshape(8, D + CONV_DIM)
    c_all = first[:, :D]
    conv_w = first[0::2, D:].reshape(4, 4, CONV_DIM // 4).transpose(1, 0, 2).reshape(4, CONV_DIM)
    mod_part, lb_row = ada_prepare(c_all, w_ada[0], hgrn_lb)
    mod_cols = w_ada.shape[2]
    mod_row = exchange_rows(mod_part.reshape(8, 1, mod_cols), "exchange_mod").reshape(1, 6 * D) + b_ada

    local = {nm: given[nm][0] for nm in SHARDED if nm != "w_ffn_in"}
    local["w_ffn_in"] = jnp.concatenate([w_ffn_gate[0].T, w_ffn_up[0].T], axis=0)
    shards = [local[nm].astype(BF16) for nm in SHARDED]
    send_in, recv_in, sent_in, land_in, started_in = gather_start(shards[:1], mod_row, "in")
    shards = shards[:1] + [(local[nm] + started_in[0, 0]).astype(BF16) for nm in SHARDED[1:]]
    send_rest, recv_rest, *flying = gather_start(shards[1:], started_in, "rest")
    n_rest = len(SHARDED) - 1
    sent_rest, land_rest, started_rest = flying[:n_rest], flying[n_rest:2 * n_rest], flying[-1]
    mod_row = mod_row + started_rest[0:1, 0:1]
    mod = tuple(mod_row[:, i * D:(i + 1) * D] for i in range(6))
    with_own = lambda land, shard: lax.dynamic_update_slice(land, shard[None], (chip, 0, 0))

    class Weights:
        def input_projection(self, after):
            (own,), land = gather_wait(send_in, recv_in, [sent_in], [land_in], after, "in")
            (land,) = forward_wait(forward_start(land, "in"), after, "in")
            return assemble_in_proj(land, own, chip)

        def start_rest(self, after):
            self.own, landed = gather_wait(send_rest, recv_rest, sent_rest, land_rest, after, "rest")
            self.started = forward_start(landed, "rest")
            return self.started[-1]

        def rest(self, after):
            got = {nm: with_own(land, s) for nm, land, s in zip(SHARDED[1:], forward_wait(self.started, after, "rest"), self.own, strict=True)}
            whole = lambda nm: got[nm].reshape(4 * got[nm].shape[1], got[nm].shape[2])
            return tuple(whole(nm) for nm in SHARDED[1:])

    wts = Weights()

    per_head = lambda p: jnp.pad(p, ((0, 0), (0, LANES - p.shape[1])))
    per_channel = lambda p: jnp.repeat(p[0], B_INNER // 32)[None]
    small = (lb_row, hgrn_gnorm, conv_w, ssm_conv_b, per_head(ssm_dt_bias), per_channel(ssm_a_log),
             per_channel(ssm_d), ssm_norm, ln1_g, ln1_b, ln2_g, ln2_b)
    by_rows = lambda g: g.reshape(4, g.shape[0] // 4, g.shape[1])
    travelling = {}

    def start_early(dws):
        travelling["pair"] = pair_start([by_rows(dw) for dw in dws], "early")
        return travelling["pair"][-1]

    def between_scans(after):
        slabs, received = pair_wait(travelling["pair"], after, "early")
        travelling["pairs"] = [pair_add(s, r, core, "pair_add_" + nm) for nm, s, r in zip(SHARDED[1:], slabs, received, strict=True)]
        travelling["started"] = scatter_start(travelling["pairs"], "early")
        return travelling["started"][-1]

    def finish_early(after):
        travelling["pairs"], travelling["landed"] = scatter_wait(travelling["started"], after, "early")

    def start_last(dw_in):
        travelling["pairs_in"] = [split_pair_add(dw_in, rows_exchange(dw_in, "pair_exchange_last"), core)]
        travelling["started_in"] = scatter_start(travelling["pairs_in"], "last")
        return travelling["started_in"][-1]

    loss, grad_x, d_mod, d_wts, d_small = local_step(x[0], loss_target[0], mod, wts, small,
                                                     start_early, between_scans, finish_early, start_last)

    d_lb, d_gn, d_cw, d_cb, d_dtb, d_alog, d_dsk, d_nw, d_l1g, d_l1b, d_l2g, d_l2b = d_small
    row = jnp.concatenate(list(d_mod) + [d_lb, d_gn, d_cw.reshape(1, 4 * CONV_DIM), d_cb, d_dtb, d_alog, d_dsk, d_nw,
                                          d_l1g, d_l1b, d_l2g, d_l2b, jnp.pad(loss, ((0, 0), (0, LANES - 1)))], axis=1)
    g_all = gather_rows(row, "gather_small_grads").reshape(8, row.shape[1])
    dmod_cols = lax.dynamic_slice_in_dim(g_all, chip * mod_cols, mod_cols, axis=1)
    fin = finalize_small(g_all, c_all, dmod_cols, [given[n] for n in SMALL_PARAMS],
                         [given["m_" + n] for n in SMALL_PARAMS], [given["v_" + n] for n in SMALL_PARAMS])
    grads, deltas, new_m, new_v = {}, {}, {}, {}
    grads["w_ada"] = fin[0][None]
    grads["ssm_conv_w"] = lax.dynamic_slice_in_dim(fin[1], chip * (CONV_DIM // 4), CONV_DIM // 4, axis=1)[None]
    for i, nm in enumerate(SMALL_PARAMS):
        grads[nm], deltas[nm], new_m[nm], new_v[nm] = fin[3 + 4 * i:7 + 4 * i]

    pairs_in, landed_in = scatter_wait(travelling["started_in"], fin[3], "last")
    pairs, landed = pairs_in + travelling["pairs"], landed_in + travelling["landed"]
    halves = [sum_chips(r, p, chip, core, "sum_chips_" + nm) for nm, r, p in zip(SHARDED, landed, pairs, strict=True)]
    reduced = dict(zip(SHARDED, exchange_halves(halves), strict=True))
    reduced["w_ada"], reduced["ssm_conv_w"] = grads["w_ada"][0], grads["ssm_conv_w"][0]
    reduced["w_in"] = reduced["w_in"].T
    reduced["w_ffn_gate"], reduced["w_ffn_up"] = reduced["w_ffn_in"][:FFN_SHARD], reduced["w_ffn_in"][FFN_SHARD:]
    for nm in ("w_ada", "ssm_conv_w", "w_in", "w_branch_a", "w_branch_b", "w_o", "w_ffn_gate", "w_ffn_up", "w_ffn_down"):
        flipped = nm in ("w_in", "w_ffn_gate", "w_ffn_up")
        work = (lambda a: a[0].T) if flipped else (lambda a: a[0])
        back = (lambda a: a.T[None]) if flipped else (lambda a: a[None])
        d_, m_, v_ = adam_update(work(given[nm]), reduced[nm], work(given["m_" + nm]), work(given["v_" + nm]), "adam_" + nm)
        grads[nm], deltas[nm], new_m[nm], new_v[nm] = back(reduced[nm]), back(d_), back(m_), back(v_)

    names = ("w_ada", "b_ada", "w_in", "hgrn_lb", "hgrn_gnorm", "ssm_conv_w", "ssm_conv_b", "ssm_dt_bias", "ssm_a_log",
             "ssm_d", "ssm_norm", "w_branch_a", "w_branch_b", "w_o", "ln1_g", "ln1_b", "w_ffn_gate", "w_ffn_up",
             "w_ffn_down", "ln2_g", "ln2_b")
    return (fin[2][0, 0], grad_x[None], *[grads[n] for n in names], *[deltas[n] for n in names],
            *[new_m[n] for n in names], *[new_v[n] for n in names])
```

```python
import functools

import jax
import jax.numpy as jnp
from jax import lax
from jax.experimental import pallas as pl
from jax.experimental.pallas import tpu as pltpu

F32, BF16 = jnp.float32, jnp.bfloat16
HI = lax.Precision.HIGHEST
MESH = pl.DeviceIdType.MESH

D = 1024
CHUNK = 64
LANES = 128
N_HEADS_A = 8
N_GROUPS_B = 4
B_INNER = 2048
CONV_DIM = 3072
D_FF = 2816
ALPHA = 2.0 ** 0.25
LN_EPS = 1e-5
RMS_EPS = 1e-6
ADAM_LR, ADAM_B1, ADAM_B2, ADAM_EPS, ADAM_WD, ADAM_STEP = 0.001, 0.9, 0.999, 1e-08, 0.01, 10

IN_ORIG = 11296
IN_PAD = 11520
COL_GA, COL_GB, COL_XBC, COL_Z, COL_DT = 4096, 5120, 6144, 9216, 11264
ORIG_Z, ORIG_XBC, ORIG_DT, ORIG_GA = 4096, 6144, 9216, 9248

SHARDED = ("w_in", "w_branch_a", "w_branch_b", "w_o", "w_ffn_in", "w_ffn_down")
FFN_SHARD = D_FF // 4
VMEM_LIMIT = 56 * 1024 * 1024
BLOCK_BYTES = 2 * 1024 * 1024
_DIMS = {"nn": (((1,), (0,)), ((), ())), "nt": (((1,), (1,)), ((), ())), "tn": (((0,), (0,)), ((), ()))}


def _bd(a, b, mode):
    return lax.dot_general(a.astype(BF16), b.astype(BF16), _DIMS[mode], preferred_element_type=F32)


@functools.partial(jax.custom_vjp, nondiff_argnums=(2,))
def bdot(a, b, mode):
    return _bd(a, b, mode)


def _bdot_fwd(a, b, mode):
    return _bd(a, b, mode), (a, b)


def _bdot_bwd(mode, res, g):
    a, b = res
    if mode == "nn":
        return _bd(g, b, "nt"), _bd(a, g, "tn")
    if mode == "nt":
        return _bd(g, b, "nn"), _bd(g, a, "tn")
    return _bd(b, g, "nt"), _bd(a, g, "nn")


bdot.defvjp(_bdot_fwd, _bdot_bwd)


def hdot(a, b, mode="nn"):
    return lax.dot_general(a, b, _DIMS[mode], precision=HI, preferred_element_type=F32)


def _raw(a, b, mode):
    return lax.dot_general(a, b, _DIMS[mode], preferred_element_type=F32)


def _split(x, n):
    parts, rest = [], x
    for _ in range(n):
        p = rest.astype(BF16)
        parts.append(p)
        rest = rest - p.astype(F32)
    return parts


def _od(a, b, mode, exact):
    if exact == 1:
        e = b.astype(BF16)
        p = _split(a, 3)
        return (_raw(p[2], e, mode) + _raw(p[1], e, mode)) + _raw(p[0], e, mode)
    e = a.astype(BF16)
    p = _split(b, 3)
    return (_raw(e, p[2], mode) + _raw(e, p[1], mode)) + _raw(e, p[0], mode)


@functools.partial(jax.custom_vjp, nondiff_argnums=(2, 3))
def odot(a, b, mode, exact):
    return _od(a, b, mode, exact)


def _odot_fwd(a, b, mode, exact):
    return _od(a, b, mode, exact), (a, b)


def _odot_bwd(mode, exact, res, g):
    a, b = res
    if exact == 1:
        da = {"nn": lambda: _od(g, b, "nt", 1), "nt": lambda: _od(g, b, "nn", 1), "tn": lambda: _od(b, g, "nt", 0)}[mode]()
        return da, jnp.zeros_like(b)
    db = {"nn": lambda: _od(a, g, "tn", 0), "nt": lambda: _od(g, a, "tn", 1), "tn": lambda: _od(a, g, "nn", 0)}[mode]()
    return jnp.zeros_like(a), db


odot.defvjp(_odot_fwd, _odot_bwd)


_BDIMS = {"bnn": (((2,), (1,)), ((0,), (0,))), "bnt": (((2,), (2,)), ((0,), (0,))), "btn": (((1,), (1,)), ((0,), (0,)))}


def _braw(a, b, mode):
    return lax.dot_general(a, b, _BDIMS[mode], preferred_element_type=F32)


def _bdb(a, b, mode):
    return _braw(a.astype(BF16), b.astype(BF16), mode)


def _d3b(a, b, mode):
    ah, al = _split(a, 2)
    bh, bl = _split(b, 2)
    return _braw(ah, bh, mode) + (_braw(ah, bl, mode) + _braw(al, bh, mode))


def _batched_bwd(f):
    def bwd(mode, res, g):
        a, b = res
        if mode == "bnn":
            return f(g, b, "bnt"), f(a, g, "btn")
        if mode == "bnt":
            return f(g, b, "bnn"), f(g, a, "btn")
        return f(b, g, "bnt"), f(a, g, "bnn")
    return bwd


@functools.partial(jax.custom_vjp, nondiff_argnums=(2,))
def bdot_b(a, b, mode):
    return _bdb(a, b, mode)


bdot_b.defvjp(lambda a, b, mode: (_bdb(a, b, mode), (a, b)), _batched_bwd(_bdb))


@functools.partial(jax.custom_vjp, nondiff_argnums=(2,))
def dot3_b(a, b, mode):
    return _d3b(a, b, mode)


dot3_b.defvjp(lambda a, b, mode: (_d3b(a, b, mode), (a, b)), _batched_bwd(_d3b))


def _cum(tril3, x, mode):
    e = tril3.astype(BF16)
    p = _split(x, 3)
    return (_braw(e, p[2], mode) + _braw(e, p[1], mode)) + _braw(e, p[0], mode)


@jax.custom_vjp
def chunk_cumsum(tril3, x):
    return _cum(tril3, x, "bnn")


chunk_cumsum.defvjp(lambda t, x: (_cum(t, x, "bnn"), t), lambda t, g: (jnp.zeros_like(t), _cum(t, g, "btn")))


def _unstack(axis, n):
    @jax.custom_vjp
    def un(x):
        return tuple(lax.index_in_dim(x, i, axis, keepdims=False) for i in range(n))

    un.defvjp(lambda x: (un(x), None), lambda _, g: (jnp.stack(g, axis=axis),))
    return un


def _split_last(n, w):
    @jax.custom_vjp
    def sp(x):
        return tuple(x[..., i * w:(i + 1) * w] for i in range(n))

    sp.defvjp(lambda x: (sp(x), None), lambda _, g: (jnp.concatenate(g, axis=-1),))
    return sp


def sigmoid(x):
    return 0.5 * jnp.tanh(0.5 * x) + 0.5


def silu(x):
    return x * sigmoid(x)


def softplus(x):
    return jnp.maximum(x, 0.0) + jnp.log1p(jnp.exp(jnp.minimum(x, -x)))


def _ln(x):
    mu = jnp.mean(x, axis=-1, keepdims=True)
    xc = x - mu
    return xc * lax.rsqrt(jnp.mean(xc * xc, axis=-1, keepdims=True) + LN_EPS)


def _tril64():
    r = lax.broadcasted_iota(jnp.int32, (CHUNK, CHUNK), 0)
    c = lax.broadcasted_iota(jnp.int32, (CHUNK, CHUNK), 1)
    return (r >= c).astype(F32)


def hgrn_block(q, fl, iv, gr, st, lb, gn):
    tb = q.shape[0]
    nc = tb // CHUNK
    nh = N_HEADS_A
    heads = _split_last(nh, LANES)
    to4 = lambda a: jnp.stack(heads(a), axis=0).reshape(nh, nc, CHUNK, LANES)
    flat = lambda a: a.reshape(nh * nc, CHUNK, LANES)
    f = lb + (1.0 - lb) * sigmoid(fl)
    gl4, k4, qf4, v4, gr4 = to4(jnp.log(f)), to4(1.0 - f), to4(silu(q) * (128 ** -0.5)), to4(iv), to4(gr)
    tril = _tril64()
    b4 = chunk_cumsum(jnp.broadcast_to(tril[None], (nh * nc, CHUNK, CHUNK)), flat(gl4)).reshape(gl4.shape)
    blast = jnp.sum(gl4, axis=2, keepdims=True)
    ref = lax.stop_gradient(0.5 * blast)
    qp, kp = qf4 * jnp.exp(b4 - ref), k4 * jnp.exp(ref - b4)
    sc = dot3_b(flat(qp), flat(kp), "bnt") * tril
    o_intra = bdot_b(sc, flat(v4), "bnn").reshape(gl4.shape)
    chunks = _unstack(1, nc)
    qe, v_c, kd, dec = chunks(qp * jnp.exp(ref)), chunks(v4), chunks(kp * jnp.exp(blast - ref)), chunks(jnp.exp(blast))
    o_inter = []
    for c in range(nc):
        o_inter.append(bdot_b(qe[c], st, "bnt"))
        st = st * dec[c] + bdot_b(v_c[c], kd[c], "btn")
    o = o_intra + jnp.stack(o_inter, axis=1)
    on = o * lax.rsqrt(jnp.mean(o * o, axis=-1, keepdims=True) + RMS_EPS) * gn
    out = (on * silu(gr4)).reshape(nh, tb, LANES)
    return jnp.concatenate(_unstack(0, nh)(out), axis=1), st


def ssd_consts(g):
    i32 = jnp.int32
    ej = lax.broadcasted_iota(i32, (LANES, 512), 0)
    ec = lax.broadcasted_iota(i32, (LANES, 512), 1)
    expand = (ej == g * 8 + (ec >> 6)).astype(F32)
    ts = lax.broadcasted_iota(i32, (CHUNK, 512), 0)
    tc = lax.broadcasted_iota(i32, (CHUNK, 512), 1)
    itile = (ts == (tc & 63)).astype(F32)
    maskall = ts >= (tc & 63)
    br = lax.broadcasted_iota(i32, (LANES, LANES), 0)
    bc = lax.broadcasted_iota(i32, (LANES, LANES), 1)
    blockmask = ((br >> 6) == (bc >> 6)).astype(F32)
    return expand, itile, maskall, blockmask, _tril64()


def ssd_block(x, bm, cm, dt, z, st, dtb, alog, dsk, nw, cs):
    expand, itile, maskall, blockmask, tril = cs
    tb = x.shape[0]
    nc = tb // CHUNK
    delta = odot(softplus(dt + dtb), expand, "nn", 1)
    a = -jnp.exp(alog) * delta
    xdt = x * delta
    by_chunk = lambda v: v.reshape(nc, CHUNK, v.shape[-1])
    a3, xdt3, bm3, cm3 = by_chunk(a), by_chunk(xdt), by_chunk(bm), by_chunk(cm)
    acum3 = chunk_cumsum(jnp.broadcast_to(tril[None], (nc, CHUNK, CHUNK)), a3)
    alast3 = jnp.sum(a3, axis=1, keepdims=True)
    cb3 = bdot_b(cm3, jnp.concatenate([bm3] * 8, axis=1), "bnt")
    arow3 = jnp.sum(acum3 * itile, axis=1, keepdims=True)
    dec3 = jnp.exp(jnp.where(maskall, acum3 - arow3, -1e30))
    pairs = _split_last(4, LANES)
    intra = [bdot_b(m, jnp.concatenate([xp] * 2, axis=1) * blockmask, "bnn")
             for m, xp in zip(pairs(cb3 * dec3), pairs(xdt3))]
    chunks = _unstack(0, nc)
    cm_c, bm_c, xw_c, dec_c = chunks(cm3), chunks(bm3), chunks(xdt3 * jnp.exp(alast3 - acum3)), chunks(jnp.exp(alast3))
    inter = []
    for c in range(nc):
        inter.append(bdot(cm_c[c], st, "nn"))
        st = st * dec_c[c] + bdot(bm_c[c], xw_c[c], "tn")
    st_new = st
    y = (jnp.concatenate(intra, axis=-1) + jnp.stack(inter, axis=0) * jnp.exp(acum3)).reshape(tb, 512)
    yz = (y + x * dsk) * silu(z)
    return yz * lax.rsqrt(jnp.mean(yz * yz, axis=-1, keepdims=True) + RMS_EPS) * nw, st_new


def adamw(w, g, m, v):
    m = ADAM_B1 * m + (1.0 - ADAM_B1) * g
    v = ADAM_B2 * v + (1.0 - ADAM_B2) * jnp.square(g)
    m_hat = m / (1.0 - ADAM_B1 ** ADAM_STEP)
    v_hat = v / (1.0 - ADAM_B2 ** ADAM_STEP)
    return -ADAM_LR * (m_hat / (jnp.sqrt(v_hat) + ADAM_EPS) + ADAM_WD * w), m, v


def _pick(n, cands):
    for c in cands:
        if n % c == 0:
            return c
    return n


def _params(sem):
    return pltpu.CompilerParams(dimension_semantics=sem, vmem_limit_bytes=VMEM_LIMIT)


MATMUL_VMEM_BUDGET = 50 * 1024 * 1024
MATMUL_MIN_STEPS = 4


def matmul(a, b, mode, out_dtype, name, after=None):
    if mode == "nn":
        (m, k), n = a.shape, b.shape[1]
    elif mode == "nt":
        (m, k), n = a.shape, b.shape[0]
    else:
        (k, m), n = a.shape, b.shape[1]
    tk = _pick(k, (2304, 2048, 1408, 1024, 768, 512, 256, 128))
    nk = k // tk
    a_bytes, b_bytes, out_bytes = a.dtype.itemsize, b.dtype.itemsize, jnp.dtype(out_dtype).itemsize

    def vmem(tm_, tn_):
        blocks = 2 * (tm_ * tk * a_bytes + tk * tn_ * b_bytes + tm_ * tn_ * out_bytes)
        return blocks + (tm_ * tn_ * 4 if nk > 1 else 0)

    def traffic(tm_, tn_):
        return (m // tm_) * k * n * b_bytes + (n // tn_ if nk > 1 else 1) * m * k * a_bytes

    sizes = (2304, 2048, 1920, 1408, 1024, 768, 512, 256, 128)
    tiles = [(tm_, tn_) for tm_ in sizes if m % tm_ == 0 for tn_ in sizes if n % tn_ == 0
             if vmem(tm_, tn_) <= MATMUL_VMEM_BUDGET] or [(m, n)]
    pipelined = [t for t in tiles if (m // t[0]) * (n // t[1]) * nk >= MATMUL_MIN_STEPS]
    tm, tn = min(pipelined or tiles, key=lambda t: (traffic(*t), -t[0] * t[1]))
    a_spec = pl.BlockSpec((tk, tm), lambda i, j, kk: (kk, i)) if mode == "tn" else pl.BlockSpec((tm, tk), lambda i, j, kk: (i, kk))
    b_spec = pl.BlockSpec((tn, tk), lambda i, j, kk: (j, kk)) if mode == "nt" else pl.BlockSpec((tk, tn), lambda i, j, kk: (kk, j))

    order = [] if after is None else [after]

    def body(a_ref, b_ref, *rest):
        o_ref, *acc = rest[len(order):]
        part = _bd(a_ref[...], b_ref[...], mode)
        if nk == 1:
            o_ref[...] = part.astype(o_ref.dtype)
            return
        acc_ref, = acc
        kk = pl.program_id(2)

        @pl.when(kk == 0)
        def _():
            acc_ref[...] = part

        @pl.when(jnp.logical_and(kk > 0, kk < nk - 1))
        def _():
            acc_ref[...] += part

        @pl.when(kk == nk - 1)
        def _():
            o_ref[...] = (acc_ref[...] + part).astype(o_ref.dtype)

    return pl.pallas_call(
        body, name=name, grid=(m // tm, n // tn, nk),
        in_specs=[a_spec, b_spec] + [pl.BlockSpec(memory_space=pl.ANY) for _ in order],
        out_specs=pl.BlockSpec((tm, tn), lambda i, j, kk: (i, j)),
        out_shape=jax.ShapeDtypeStruct((m, n), out_dtype),
        scratch_shapes=[pltpu.VMEM((tm, tn), F32)] if nk > 1 else [],
        compiler_params=_params(("parallel", "parallel", "arbitrary")),
    )(a, b, *order)


def rowwise(name, fn, rows, consts, out_rows, out_accs=(), tm_max=512, into=None, new_wide=None):
    t = rows[0][0].shape[0]
    tm = _pick(t, (tm_max, 128, 64, 32, 16, 8))
    n_r, n_c, n_o = len(rows), len(consts), len(out_rows)
    n_alias = 0 if into is None else 1

    def body(*refs):
        r_in = [r[...] for r in refs[:n_r]]
        c_in = [r[...] for r in refs[n_r:n_r + n_c]]
        refs = refs[:n_r + n_c] + refs[n_r + n_c + n_alias:]
        o_refs = refs[n_r + n_c:n_r + n_c + n_o]
        a_refs = refs[n_r + n_c + n_o:]
        ro, ao = fn(r_in, c_in)
        for ref, val in zip(o_refs, ro, strict=True):
            ref[...] = val.astype(ref.dtype)
        if a_refs:
            @pl.when(pl.program_id(0) == 0)
            def _():
                for ref in a_refs:
                    ref[...] = jnp.zeros_like(ref)

            for ref, val in zip(a_refs, ao, strict=True):
                ref[...] += val

    in_specs = [pl.BlockSpec((tm, w), functools.partial(lambda i, cb: (i, cb), cb=cb)) for _, w, cb in rows]
    in_specs += [pl.BlockSpec(c.shape, lambda i: (0, 0)) for c in consts]
    out_specs = [pl.BlockSpec((tm, w), lambda i: (i, 0)) for w, _ in out_rows]
    out_specs += [pl.BlockSpec(s, lambda i: (0, 0)) for s in out_accs]
    out_shape = [jax.ShapeDtypeStruct((t, w), dt) for w, dt in out_rows]
    out_shape += [jax.ShapeDtypeStruct(s, F32) for s in out_accs]
    operands = [r[0] for r in rows] + list(consts)
    aliases = {}
    if into is not None:
        target, cb = into
        in_specs.append(pl.BlockSpec(memory_space=pl.ANY))
        operands.append(target)
        out_specs[0] = pl.BlockSpec((tm, out_rows[0][0]), lambda i: (i, cb))
        out_shape[0] = jax.ShapeDtypeStruct(target.shape, target.dtype)
        aliases = {len(operands) - 1: 0}
    if new_wide is not None:
        width, cb = new_wide
        out_specs[0] = pl.BlockSpec((tm, out_rows[0][0]), lambda i: (i, cb))
        out_shape[0] = jax.ShapeDtypeStruct((t, width), out_rows[0][1])
    return pl.pallas_call(
        body, name=name, grid=(t // tm,), in_specs=in_specs, out_specs=out_specs, out_shape=out_shape,
        input_output_aliases=aliases, compiler_params=_params(("arbitrary",)),
    )(*operands)


def _full(a):
    return (a, a.shape[1], 0)


HGRN_TIME_BLOCK = 256
SSD_TIME_BLOCK = 512


def _time_block(t, most=HGRN_TIME_BLOCK):
    return _pick(t, tuple(b for b in (512, 256, 128, 64) if b <= most))


def _quarters(ref):
    return [ref[:, seg * D:(seg + 1) * D] for seg in range(4)]


def hgrn_forward(proj, lb, gn):
    t = proj.shape[0]
    tb = _time_block(t)
    nb = t // tb

    def body(qfig_ref, lb_ref, gn_ref, o_ref, st_ref, state):
        @pl.when(pl.program_id(0) == 0)
        def _():
            state[...] = jnp.zeros_like(state)

        st = state[...]
        st_ref[...] = st
        out, st_new = hgrn_block(*_quarters(qfig_ref), st, lb_ref[...], gn_ref[...])
        o_ref[...] = out.astype(o_ref.dtype)
        state[...] = st_new

    return pl.pallas_call(
        body, name="hgrn_forward", grid=(nb,),
        in_specs=[pl.BlockSpec((tb, 4 * D), lambda j: (j, 0)),
                  pl.BlockSpec((1, D), lambda j: (0, 0)), pl.BlockSpec((1, LANES), lambda j: (0, 0))],
        out_specs=[pl.BlockSpec((tb, D), lambda j: (j, 0)),
                   pl.BlockSpec((None, N_HEADS_A, LANES, LANES), lambda j: (j, 0, 0, 0))],
        out_shape=[jax.ShapeDtypeStruct((t, D), BF16),
                   jax.ShapeDtypeStruct((nb, N_HEADS_A, LANES, LANES), F32)],
        scratch_shapes=[pltpu.VMEM((N_HEADS_A, LANES, LANES), F32)],
        compiler_params=_params(("arbitrary",)),
    )(proj, lb, gn)


def hgrn_backward(proj, states, d_out, lb, gn, d_proj):
    t = proj.shape[0]
    tb = _time_block(t)
    nb = t // tb

    def body(qfig_ref, st_ref, do_ref, lb_ref, gn_ref, _, dqfig_ref, dlb_ref, dgn_ref, d_state):
        @pl.when(pl.program_id(0) == 0)
        def _():
            d_state[...] = jnp.zeros_like(d_state)
            dlb_ref[...] = jnp.zeros_like(dlb_ref)
            dgn_ref[...] = jnp.zeros_like(dgn_ref)

        _, vjp = jax.vjp(hgrn_block, *_quarters(qfig_ref), st_ref[...], lb_ref[...], gn_ref[...])
        dq, df, di, dg, dst, dlb, dgn = vjp((do_ref[...], d_state[...]))
        for seg, val in enumerate((dq, df, di, dg)):
            dqfig_ref[:, seg * D:(seg + 1) * D] = val.astype(dqfig_ref.dtype)
        d_state[...] = dst
        dlb_ref[...] += dlb
        dgn_ref[...] += dgn

    rev = lambda j: nb - 1 - j
    return pl.pallas_call(
        body, name="hgrn_backward", grid=(nb,),
        in_specs=[pl.BlockSpec((tb, 4 * D), lambda j: (rev(j), 0)),
                  pl.BlockSpec((None, N_HEADS_A, LANES, LANES), lambda j: (rev(j), 0, 0, 0)),
                  pl.BlockSpec((tb, D), lambda j: (rev(j), 0)),
                  pl.BlockSpec((1, D), lambda j: (0, 0)), pl.BlockSpec((1, LANES), lambda j: (0, 0)),
                  pl.BlockSpec(memory_space=pl.ANY)],
        out_specs=[pl.BlockSpec((tb, 4 * D), lambda j: (rev(j), 0)),
                   pl.BlockSpec((1, D), lambda j: (0, 0)), pl.BlockSpec((1, LANES), lambda j: (0, 0))],
        out_shape=[jax.ShapeDtypeStruct(d_proj.shape, d_proj.dtype), jax.ShapeDtypeStruct((1, D), F32),
                   jax.ShapeDtypeStruct((1, LANES), F32)],
        input_output_aliases={5: 0},
        scratch_shapes=[pltpu.VMEM((N_HEADS_A, LANES, LANES), F32)],
        compiler_params=_params(("arbitrary",)),
    )(proj, states, d_out, lb, gn, d_proj)


def _ssd_in_specs(tb, tmap):
    return [pl.BlockSpec((tb, 512), lambda g, j: (tmap(j), g)),
            pl.BlockSpec((tb, LANES), lambda g, j: (tmap(j), 16 + g)),
            pl.BlockSpec((tb, LANES), lambda g, j: (tmap(j), 20 + g)),
            pl.BlockSpec((tb, LANES), lambda g, j: (tmap(j), COL_DT // LANES)),
            pl.BlockSpec((tb, 512), lambda g, j: (tmap(j), COL_Z // 512 + g))]


def ssd_forward(xc, proj, dtb, alog, dsk, nw):
    t = proj.shape[0]
    tb = _time_block(t, SSD_TIME_BLOCK)
    nb = t // tb

    def body(x_ref, b_ref, c_ref, dt_ref, z_ref, dtb_ref, alog_ref, dsk_ref, nw_ref, o_ref, st_ref, state):
        @pl.when(pl.program_id(1) == 0)
        def _():
            state[...] = jnp.zeros_like(state)

        st = state[...]
        st_ref[...] = st
        out, st_new = ssd_block(x_ref[...], b_ref[...], c_ref[...], dt_ref[...], z_ref[...], st,
                                dtb_ref[...], alog_ref[...], dsk_ref[...], nw_ref[...], ssd_consts(pl.program_id(0)))
        o_ref[...] = out.astype(o_ref.dtype)
        state[...] = st_new

    vec = pl.BlockSpec((1, 512), lambda g, j: (0, g))
    heads = pl.BlockSpec((1, LANES), lambda g, j: (0, 0))
    return pl.pallas_call(
        body, name="ssd_forward", grid=(N_GROUPS_B, nb),
        in_specs=_ssd_in_specs(tb, lambda j: j) + [heads, vec, vec, vec],
        out_specs=[pl.BlockSpec((tb, 512), lambda g, j: (j, g)),
                   pl.BlockSpec((None, None, LANES, 512), lambda g, j: (j, g, 0, 0))],
        out_shape=[jax.ShapeDtypeStruct((t, B_INNER), BF16),
                   jax.ShapeDtypeStruct((nb, N_GROUPS_B, LANES, 512), F32)],
        scratch_shapes=[pltpu.VMEM((LANES, 512), F32)],
        compiler_params=_params(("arbitrary", "arbitrary")),
    )(xc, xc, xc, proj, proj, dtb, alog, dsk, nw)


def ssd_backward(xc, proj, states, d_out, dtb, alog, dsk, nw, d_proj):
    t = proj.shape[0]
    tb = _time_block(t, SSD_TIME_BLOCK)
    nb = t // tb
    rev = lambda j: nb - 1 - j

    def body(x_ref, b_ref, c_ref, dt_ref, z_ref, st_ref, do_ref, dtb_ref, alog_ref, dsk_ref, nw_ref, _,
             dx_ref, db_ref, dc_ref, ddt_ref, dz_ref, ddtb_ref, dalog_ref, ddsk_ref, dnw_ref, d_state):
        accs = (ddtb_ref, dalog_ref, ddsk_ref, dnw_ref)

        @pl.when(pl.program_id(1) == 0)
        def _():
            d_state[...] = jnp.zeros_like(d_state)
            for ref in accs:
                ref[...] = jnp.zeros_like(ref)

        cs = ssd_consts(pl.program_id(0))
        fn = lambda *a: ssd_block(*a, cs)
        _, vjp = jax.vjp(fn, x_ref[...], b_ref[...], c_ref[...], dt_ref[...], z_ref[...], st_ref[...],
                         dtb_ref[...], alog_ref[...], dsk_ref[...], nw_ref[...])
        dx, db, dc, ddt, dz, dst, *dpar = vjp((do_ref[...], d_state[...]))
        dx_ref[...] = dx
        db_ref[...] = db
        dc_ref[...] = dc
        ddt_ref[...] = ddt
        dz_ref[...] = dz.astype(dz_ref.dtype)
        d_state[...] = dst
        for ref, val in zip(accs, dpar, strict=True):
            ref[...] += val

    vec = pl.BlockSpec((1, 512), lambda g, j: (0, g))
    heads = pl.BlockSpec((1, LANES), lambda g, j: (0, 0))
    acc = pl.BlockSpec((None, 1, 512), lambda g, j: (g, 0, 0))
    acc_heads = pl.BlockSpec((None, 1, LANES), lambda g, j: (g, 0, 0))
    return pl.pallas_call(
        body, name="ssd_backward", grid=(N_GROUPS_B, nb),
        in_specs=_ssd_in_specs(tb, rev)
        + [pl.BlockSpec((None, None, LANES, 512), lambda g, j: (rev(j), g, 0, 0)),
           pl.BlockSpec((tb, 512), lambda g, j: (rev(j), g))] + [heads, vec, vec, vec] + [pl.BlockSpec(memory_space=pl.ANY)],
        out_specs=[pl.BlockSpec((tb, 512), lambda g, j: (rev(j), g)),
                   pl.BlockSpec((tb, LANES), lambda g, j: (rev(j), g)),
                   pl.BlockSpec((tb, LANES), lambda g, j: (rev(j), g)),
                   pl.BlockSpec((None, tb, LANES), lambda g, j: (g, rev(j), 0)),
                   pl.BlockSpec((tb, 512), lambda g, j: (rev(j), COL_Z // 512 + g)), acc_heads, acc, acc, acc],
        out_shape=[jax.ShapeDtypeStruct((t, B_INNER), F32), jax.ShapeDtypeStruct((t, 512), F32),
                   jax.ShapeDtypeStruct((t, 512), F32), jax.ShapeDtypeStruct((N_GROUPS_B, t, LANES), F32),
                   jax.ShapeDtypeStruct(d_proj.shape, d_proj.dtype)]
        + [jax.ShapeDtypeStruct((N_GROUPS_B, 1, LANES), F32)] + [jax.ShapeDtypeStruct((N_GROUPS_B, 1, 512), F32)] * 3,
        input_output_aliases={11: 4},
        scratch_shapes=[pltpu.VMEM((LANES, 512), F32)],
        compiler_params=_params(("arbitrary", "arbitrary")),
    )(xc, xc, xc, proj, proj, states, d_out, dtb, alog, dsk, nw, d_proj)


CONV_HALO = 8


def _shift_down(halo_then_tile, s, tm):
    if s == 0:
        return halo_then_tile[CONV_HALO:CONV_HALO + tm]
    return pltpu.roll(halo_then_tile, s, 0)[CONV_HALO:CONV_HALO + tm]


def _conv_pre(cur, prev, w, b, tm):
    stacked = jnp.concatenate([prev, cur], axis=0)
    taps = [_shift_down(stacked, 3 - j, tm) for j in range(4)]
    pre = b + taps[0] * w[0:1] + taps[1] * w[1:2] + taps[2] * w[2:3] + taps[3] * w[3:4]
    return pre, taps


def _conv_specs(t, tm):
    per = tm // CONV_HALO
    cur = pl.BlockSpec((tm, CONV_DIM), lambda i: (i, COL_XBC // CONV_DIM))
    prev = pl.BlockSpec((CONV_HALO, CONV_DIM), lambda i: (jnp.maximum(i * per - 1, 0), COL_XBC // CONV_DIM))
    return cur, prev


def conv_forward(proj, w, b):
    t = proj.shape[0]
    tm = _pick(t, (256, 128, 64))

    def body(cur_ref, prev_ref, w_ref, b_ref, o_ref):
        prev = jnp.where(pl.program_id(0) == 0, 0.0, prev_ref[...])
        pre, _ = _conv_pre(cur_ref[...], prev, w_ref[...], b_ref[...], tm)
        o_ref[...] = silu(pre)

    cur, prev = _conv_specs(t, tm)
    return pl.pallas_call(
        body, name="conv_forward", grid=(t // tm,),
        in_specs=[cur, prev, pl.BlockSpec((4, CONV_DIM), lambda i: (0, 0)), pl.BlockSpec((1, CONV_DIM), lambda i: (0, 0))],
        out_specs=pl.BlockSpec((tm, CONV_DIM), lambda i: (i, 0)),
        out_shape=jax.ShapeDtypeStruct((t, CONV_DIM), F32),
        compiler_params=_params(("arbitrary",)),
    )(proj, proj, w, b)


def conv_backward(proj, dx, db_, dc_, w, b, d_proj):
    t = proj.shape[0]
    tm = _pick(t, (256, 128, 64))
    per = tm // CONV_HALO
    nt = t // tm
    rev = lambda i: nt - 1 - i

    def body(cur_ref, prev_ref, dx_ref, dbm_ref, dcm_ref, w_ref, b_ref, _, o_ref, dw_ref, dbias_ref, later):
        @pl.when(pl.program_id(0) == 0)
        def _():
            dw_ref[...] = jnp.zeros_like(dw_ref)
            dbias_ref[...] = jnp.zeros_like(dbias_ref)
            later[...] = jnp.zeros_like(later)

        first_tile = pl.program_id(0) == nt - 1
        for lo, hi, src in ((0, B_INNER, dx_ref), (B_INNER, B_INNER + 512, dbm_ref), (B_INNER + 512, CONV_DIM, dcm_ref)):
            cols = slice(lo, hi)
            prev = jnp.where(first_tile, 0.0, prev_ref[:, cols])
            w_ = w_ref[:, cols]
            pre, taps = _conv_pre(cur_ref[:, cols], prev, w_, b_ref[:, cols], tm)
            sg = sigmoid(pre)
            dpre = src[...] * (sg * (1.0 + pre * (1.0 - sg)))
            dbias_ref[:, cols] += jnp.sum(dpre, axis=0, keepdims=True)
            for j in range(4):
                dw_ref[j:j + 1, cols] += jnp.sum(dpre * taps[j], axis=0, keepdims=True)
            stacked = jnp.concatenate([dpre, later[:, cols]], axis=0)
            acc = dpre * w_[3:4]
            for j in range(3):
                acc = acc + pltpu.roll(stacked, tm + CONV_HALO - (3 - j), 0)[0:tm] * w_[j:j + 1]
            o_ref[:, cols] = acc.astype(o_ref.dtype)
            later[:, cols] = dpre[0:CONV_HALO]

    row = lambda w_: pl.BlockSpec((tm, w_), lambda i: (rev(i), 0))
    whole = lambda r: pl.BlockSpec((r, CONV_DIM), lambda i: (0, 0))
    return pl.pallas_call(
        body, name="conv_backward", grid=(nt,),
        in_specs=[pl.BlockSpec((tm, CONV_DIM), lambda i: (rev(i), COL_XBC // CONV_DIM)),
                  pl.BlockSpec((CONV_HALO, CONV_DIM), lambda i: (jnp.maximum(rev(i) * per - 1, 0), COL_XBC // CONV_DIM)),
                  row(B_INNER), row(512), row(512), whole(4), whole(1), pl.BlockSpec(memory_space=pl.ANY)],
        out_specs=[pl.BlockSpec((tm, CONV_DIM), lambda i: (rev(i), COL_XBC // CONV_DIM)), whole(4), whole(1)],
        out_shape=[jax.ShapeDtypeStruct(d_proj.shape, d_proj.dtype), jax.ShapeDtypeStruct((4, CONV_DIM), F32),
                   jax.ShapeDtypeStruct((1, CONV_DIM), F32)],
        input_output_aliases={7: 0},
        scratch_shapes=[pltpu.VMEM((CONV_HALO, CONV_DIM), F32)],
        compiler_params=_params(("arbitrary",)),
    )(proj, proj, dx, db_, dc_, w, b, d_proj)


def stage_modulate(x, sc, sh):
    return _ln(x) * (1.0 + sc) + sh


def stage_merge(ga, gb, ya, yb):
    return sigmoid(ga) * ya + sigmoid(gb) * yb


def stage_post_mixer(x, h, g1, ln_g, ln_b, sc2, sh2):
    x1 = _ln(ALPHA * x + g1 * h) * ln_g + ln_b
    return x1, _ln(x1) * (1.0 + sc2) + sh2


def stage_swiglu(a, b):
    return silu(a) * b


def gate_up(ab):
    w = FFN_SHARD
    return (jnp.concatenate([ab[:, 2 * w * k:2 * w * k + w] for k in range(4)], axis=1),
            jnp.concatenate([ab[:, 2 * w * k + w:2 * w * (k + 1)] for k in range(4)], axis=1))


def per_chip(gate, up):
    w = FFN_SHARD
    return jnp.concatenate([part[:, w * k:w * (k + 1)] for k in range(4) for part in (gate, up)], axis=1)


def stage_loss(x1, hf, tgt, g2, ln_g, ln_b):
    x2 = _ln(ALPHA * x1 + g2 * hf) * ln_g + ln_b
    return 0.5 * jnp.sum(jnp.mean(jnp.square(x2 - tgt), axis=-1, keepdims=True), axis=0, keepdims=True)


def local_step(x, tgt, mod, wts, small, early=None, mid=None, late=None, last=None):
    sh1, sc1, g1, sh2, sc2, g2 = mod
    lb, gn, conv_w, conv_b, dtb, alog, dsk, nw, ln1_g, ln1_b, ln2_g, ln2_b = small
    vec = (1, D)

    (u1,) = rowwise("modulate1", lambda r, c: ((stage_modulate(r[0], *c),), ()), [_full(x)], [sc1, sh1], [(D, BF16)])
    w_in = wts.input_projection(u1)
    proj = matmul(u1, w_in, "nn", F32, "in_proj")
    ya_in, st_a = hgrn_forward(proj, lb, gn + wts.start_rest(proj)[0:1])
    xc = conv_forward(proj, conv_w, conv_b)
    w_a, w_b, w_o, w_gu, w_d = wts.rest(xc)
    yb_in, st_b = ssd_forward(xc, proj, dtb, alog, dsk, nw)
    ya = matmul(ya_in, w_a, "nn", F32, "branch_a")
    yb = matmul(yb_in, w_b, "nn", F32, "branch_b")
    gate_rows = [(proj, D, COL_GA // D), (proj, D, COL_GB // D), _full(ya), _full(yb)]
    (merged,) = rowwise("merge", lambda r, c: ((stage_merge(*r),), ()), gate_rows, [], [(D, BF16)])
    h = matmul(merged, w_o, "nn", F32, "out_proj")
    post_consts = [g1, ln1_g, ln1_b, sc2, sh2]
    x1, u2 = rowwise("post_mixer", lambda r, c: (stage_post_mixer(*r, *c), ()), [_full(x), _full(h)], post_consts,
                     [(D, F32), (D, BF16)])
    ab = matmul(u2, w_gu, "nt", F32, "ffn_in")
    (p,) = rowwise("swiglu", lambda r, c: ((stage_swiglu(*gate_up(r[0])),), ()), [_full(ab)], [], [(D_FF, BF16)],
                   tm_max=256)
    hf = matmul(p, w_d, "nn", F32, "ffn_out")

    def loss_bwd(r, c):
        loss, vjp = jax.vjp(stage_loss, *r, *c)
        dx1, dhf, _, dg2, dlg, dlb_ = vjp(jnp.ones((1, 1), F32))
        return (dx1, dhf), (loss, dg2, dlg, dlb_)

    dx1, dhf, loss, dg2, dln2_g, dln2_b = rowwise(
        "loss_backward", loss_bwd, [_full(x1), _full(hf), _full(tgt)], [g2, ln2_g, ln2_b],
        [(D, F32), (D, BF16)], [(1, 1), vec, vec, vec])
    dp = matmul(dhf, w_d, "nt", F32, "ffn_out_dx")
    dw_d = matmul(p, dhf, "tn", F32, "ffn_out_dw")

    def swiglu_bwd(r, c):
        _, vjp = jax.vjp(stage_swiglu, *gate_up(r[0]))
        return (per_chip(*vjp(r[1])),), ()

    (dab,) = rowwise("swiglu_backward", swiglu_bwd, [_full(ab), _full(dp)], [], [(2 * D_FF, BF16)], tm_max=256)
    du2 = matmul(dab, w_gu, "nn", F32, "ffn_in_dx")
    dw_gu = matmul(dab, u2, "tn", F32, "ffn_in_dw")

    def post_bwd(r, c):
        _, vjp = jax.vjp(stage_post_mixer, r[0], r[1], *c)
        dx, dh, *dc = vjp((r[2], r[3]))
        return (dx, dh), tuple(dc)

    dx_a, dh, dg1, dln1_g, dln1_b, dsc2, dsh2 = rowwise(
        "post_mixer_backward", post_bwd, [_full(x), _full(h), _full(dx1), _full(du2)], post_consts,
        [(D, F32), (D, BF16)], [vec] * 5)
    dmerged = matmul(dh, w_o, "nt", F32, "out_proj_dx")
    dw_o = matmul(merged, dh, "tn", F32, "out_proj_dw")

    def merge_bwd(r, c):
        _, vjp = jax.vjp(stage_merge, *r[:4])
        dga, dgb, dya, dyb = vjp(r[4])
        return (jnp.concatenate([dga, dgb], axis=1), dya, dyb), ()

    dproj, dya, dyb = rowwise("merge_backward", merge_bwd, gate_rows + [_full(dmerged)], [],
                              [(2 * D, BF16), (D, BF16), (D, BF16)], new_wide=(IN_PAD, COL_GA // (2 * D)))
    dya_in = matmul(dya, w_a, "nt", F32, "branch_a_dx")
    dw_a = matmul(ya_in, dya, "tn", F32, "branch_a_dw")
    dyb_in = matmul(dyb, w_b, "nt", F32, "branch_b_dx")
    dw_b = matmul(yb_in, dyb, "tn", F32, "branch_b_dw")
    gn_after = gn if early is None else gn + early((dw_a, dw_b, dw_o, dw_gu, dw_d))[0:1]
    dproj, dlb, dgn = hgrn_backward(proj, st_a, dya_in, lb, gn_after, dproj)
    dtb_after = dtb if mid is None else dtb + mid(dlb)[0:1, 0:1]
    dxs, dbm, dcm, ddt, dproj, ddtb, dalog, ddsk, dnw = ssd_backward(xc, proj, st_b, dyb_in, dtb_after, alog, dsk, nw, dproj)
    dproj, dconv_w, dconv_b = conv_backward(proj, dxs, dbm, dcm, conv_w, conv_b, dproj)
    if late is not None:
        late(dconv_b)
    t = x.shape[0]
    tail = jnp.concatenate([jnp.sum(ddt, axis=0).astype(BF16), jnp.zeros((t, IN_PAD - COL_DT - LANES), BF16)], axis=1)
    dproj = lax.dynamic_update_slice(dproj, tail, (0, COL_DT))
    dw_in = matmul(u1, dproj, "tn", F32, "in_proj_dw")
    du1 = matmul(dproj, w_in, "nt", F32, "in_proj_dx", after=None if last is None else last(dw_in))

    def mod_bwd(r, c):
        _, vjp = jax.vjp(stage_modulate, r[0], *c)
        dx, dsc, dsh = vjp(r[1])
        return (dx + r[2],), (dsc, dsh)

    grad_x, dsc1, dsh1 = rowwise("modulate1_backward", mod_bwd, [_full(x), _full(du1), _full(dx_a)], [sc1, sh1],
                                 [(D, F32)], [vec, vec])
    d_mod = (dsh1, dsc1, dg1, dsh2, dsc2, dg2)
    d_wts = (dw_in, dw_a, dw_b, dw_o, dw_gu, dw_d)
    d_small = (dlb, dgn, dconv_w, dconv_b, jnp.sum(ddtb, axis=0),
               dalog.reshape(1, B_INNER), ddsk.reshape(1, B_INNER), dnw.reshape(1, B_INNER),
               dln1_g, dln1_b, dln2_g, dln2_b)
    return loss, grad_x, d_mod, d_wts, d_small


HBM = pl.BlockSpec(memory_space=pltpu.HBM)
SEM = pl.BlockSpec(memory_space=pltpu.SEMAPHORE)
DATAFLOW = pltpu.SideEffectType.DATAFLOW_SIDE_EFFECTING


def _place():
    return lax.axis_index("x"), lax.axis_index("y"), lax.axis_index("c")


def _other_chips(x, y):
    return [(1 - x, y), (x, 1 - y), (1 - x, 1 - y)]


def _remote(src, dst, send_sem, recv_sem, device):
    return pltpu.make_async_remote_copy(src_ref=src, dst_ref=dst, send_sem=send_sem, recv_sem=recv_sem,
                                        device_id=device, device_id_type=MESH)


def gather_rows(v, name):
    n = v.shape[1]

    def body(v_ref, out_ref, send_sems, recv_sems, local_sem):
        x, y, c = _place()
        mine = pltpu.make_async_copy(v_ref, out_ref.at[4 * x + 2 * y + c], local_sem)
        mine.start()
        sends, recvs = [], []
        for m in range(1, 8):
            px = 1 - x if m & 4 else x
            py = 1 - y if m & 2 else y
            pc = 1 - c if m & 1 else c
            sends.append(_remote(v_ref, out_ref.at[4 * x + 2 * y + c], send_sems.at[m - 1], recv_sems.at[m - 1], (px, py, pc)))
            recvs.append(_remote(v_ref, out_ref.at[4 * px + 2 * py + pc], send_sems.at[m - 1], recv_sems.at[m - 1], (px, py, pc)))
        for cp in sends:
            cp.start()
        for cp in recvs:
            cp.wait_recv()
        for cp in sends:
            cp.wait_send()
        mine.wait()

    return pl.pallas_call(
        body, name=name, in_specs=[HBM], out_specs=HBM,
        out_shape=jax.ShapeDtypeStruct((8, 1, n), v.dtype),
        scratch_shapes=[pltpu.SemaphoreType.DMA((7,)), pltpu.SemaphoreType.DMA((7,)), pltpu.SemaphoreType.DMA],
    )(v)


def exchange_rows(part, name):
    w = part.shape[2]

    def body(p_ref, out_ref, send_sems, recv_sems, local_sem):
        x, y, c = _place()
        k = 2 * x + y
        mine = pltpu.make_async_copy(p_ref.at[4 * x + 2 * y + c], out_ref.at[k], local_sem)
        mine.start()
        sends, recvs = [], []
        for j, (px, py) in enumerate(_other_chips(x, y)):
            sends.append(_remote(p_ref.at[4 * px + 2 * py + c], out_ref.at[k], send_sems.at[j], recv_sems.at[j], (px, py, c)))
            recvs.append(_remote(p_ref.at[4 * px + 2 * py + c], out_ref.at[2 * px + py], send_sems.at[j], recv_sems.at[j], (px, py, c)))
        for cp in sends:
            cp.start()
        for cp in recvs:
            cp.wait_recv()
        for cp in sends:
            cp.wait_send()
        mine.wait()

    return pl.pallas_call(
        body, name=name, in_specs=[HBM], out_specs=HBM,
        out_shape=jax.ShapeDtypeStruct((4, 1, w), part.dtype),
        scratch_shapes=[pltpu.SemaphoreType.DMA((3,)), pltpu.SemaphoreType.DMA((3,)), pltpu.SemaphoreType.DMA],
    )(part)


def _half_of_slot(ref, rows, px, py, pc):
    return ref.at[2 * px + py, pl.ds(pc * (rows // 2), rows // 2), :]


def gather_start(shards, after, tag):
    n = len(shards)

    def body(*refs):
        w_refs, land_refs = refs[:n], refs[n:2 * n]
        send_sems, recv_sems = refs[2 * n + 1], refs[2 * n + 2]
        token = refs[-1]
        x, y, c = _place()
        for i in range(n):
            rows = shards[i].shape[0]
            for j, (px, py) in enumerate(_other_chips(x, y)):
                _remote(w_refs[i].at[pl.ds(c * (rows // 2), rows // 2), :], _half_of_slot(land_refs[i], rows, x, y, c),
                        send_sems.at[j * n + i], recv_sems.at[j * n + i], (px, py, c)).start()
        token[...] = jnp.zeros_like(token)

    hbm = lambda a: pltpu.with_memory_space_constraint(a, pltpu.HBM)
    lands = [lax.empty((4,) + s.shape, s.dtype) for s in shards]
    dma = pltpu.SemaphoreType.DMA
    return pl.pallas_call(
        body, name="gather_start_" + tag,
        out_shape=(dma((3 * n,)), dma((3 * n,)),
                   *[pltpu.HBM(a.shape, a.dtype) for a in list(shards) + lands], jax.ShapeDtypeStruct((8, LANES), F32)),
        in_specs=[HBM] * (2 * n) + [pl.BlockSpec(memory_space=pl.ANY)],
        out_specs=(SEM, SEM, *[HBM] * (2 * n), pl.BlockSpec(memory_space=pltpu.VMEM)),
        input_output_aliases={i: 2 + i for i in range(2 * n)},
        compiler_params=pltpu.CompilerParams(has_side_effects=DATAFLOW),
    )(*[hbm(a) for a in list(shards) + lands], after)


def gather_wait(send_sems, recv_sems, shards, lands, after, tag):
    n = len(shards)

    def body(*refs):
        w_refs, land_refs = refs[:n], refs[n:2 * n]
        send_ref, recv_ref = refs[2 * n], refs[2 * n + 1]
        x, y, c = _place()
        for i in range(n):
            rows = shards[i].shape[0]
            for j, (px, py) in enumerate(_other_chips(x, y)):
                cp = _remote(w_refs[i].at[pl.ds(c * (rows // 2), rows // 2), :], _half_of_slot(land_refs[i], rows, px, py, c),
                             send_ref.at[j * n + i], recv_ref.at[j * n + i], (px, py, c))
                cp.wait_send()
                cp.wait_recv()

    out = pl.pallas_call(
        body, name="gather_wait_" + tag,
        out_shape=tuple(pltpu.HBM(a.shape, a.dtype) for a in list(shards) + list(lands)),
        in_specs=[HBM] * (2 * n) + [SEM, SEM, pl.BlockSpec(memory_space=pl.ANY)], out_specs=tuple([HBM] * (2 * n)),
        input_output_aliases={i: i for i in range(2 * n)},
        compiler_params=pltpu.CompilerParams(has_side_effects=DATAFLOW),
    )(*shards, *lands, send_sems, recv_sems, after)
    return list(out[:n]), list(out[n:])


def forward_start(lands, tag):
    n = len(lands)

    def body(*refs):
        land_refs = refs[:n]
        send_sems, recv_sems = refs[n], refs[n + 1]
        token = refs[-1]
        x, y, c = _place()
        for i in range(n):
            rows = lands[i].shape[1]
            for j, (px, py) in enumerate(_other_chips(x, y)):
                mine = _half_of_slot(land_refs[i], rows, px, py, c)
                _remote(mine, mine, send_sems.at[j * n + i], recv_sems.at[j * n + i], (x, y, 1 - c)).start()
        token[...] = jnp.zeros_like(token)

    dma = pltpu.SemaphoreType.DMA
    return pl.pallas_call(
        body, name="forward_start_" + tag,
        out_shape=(dma((3 * n,)), dma((3 * n,)), *[pltpu.HBM(a.shape, a.dtype) for a in lands],
                   jax.ShapeDtypeStruct((8, LANES), F32)),
        in_specs=[HBM] * n, out_specs=(SEM, SEM, *[HBM] * n, pl.BlockSpec(memory_space=pltpu.VMEM)),
        input_output_aliases={i: 2 + i for i in range(n)},
        compiler_params=pltpu.CompilerParams(has_side_effects=DATAFLOW),
    )(*lands)


def forward_wait(started, after, tag):
    send_sems, recv_sems, *rest = started
    lands = rest[:-1]
    n = len(lands)

    def body(*refs):
        land_refs = refs[:n]
        send_ref, recv_ref = refs[n], refs[n + 1]
        x, y, c = _place()
        for i in range(n):
            rows = lands[i].shape[1]
            for j, (px, py) in enumerate(_other_chips(x, y)):
                cp = _remote(_half_of_slot(land_refs[i], rows, px, py, c), _half_of_slot(land_refs[i], rows, px, py, 1 - c),
                             send_ref.at[j * n + i], recv_ref.at[j * n + i], (x, y, 1 - c))
                cp.wait_send()
                cp.wait_recv()

    out = pl.pallas_call(
        body, name="forward_wait_" + tag,
        out_shape=tuple(pltpu.HBM(a.shape, a.dtype) for a in lands),
        in_specs=[HBM] * n + [SEM, SEM, pl.BlockSpec(memory_space=pl.ANY)], out_specs=tuple([HBM] * n),
        input_output_aliases={i: i for i in range(n)},
        compiler_params=pltpu.CompilerParams(has_side_effects=DATAFLOW),
    )(*lands, send_sems, recv_sems, after)
    return list(out)


def pair_start(slabs, tag):
    n = len(slabs)

    def body(*refs):
        g_refs, land_refs = refs[:n], refs[n:2 * n]
        send_sems, recv_sems = refs[2 * n], refs[2 * n + 1]
        token = refs[-1]
        x, y, c = _place()
        for i in range(n):
            hr = slabs[i].shape[1] // 2
            _remote(g_refs[i].at[:, pl.ds((1 - c) * hr, hr), :], land_refs[i], send_sems.at[i], recv_sems.at[i],
                    (x, y, 1 - c)).start()
        token[...] = jnp.zeros_like(token)

    hbm = lambda a: pltpu.with_memory_space_constraint(a, pltpu.HBM)
    lands = [lax.empty((4, s.shape[1] // 2, s.shape[2]), s.dtype) for s in slabs]
    dma = pltpu.SemaphoreType.DMA
    return pl.pallas_call(
        body, name="pair_start_" + tag,
        out_shape=(dma((n,)), dma((n,)), *[pltpu.HBM(a.shape, a.dtype) for a in list(slabs) + lands],
                   jax.ShapeDtypeStruct((8, LANES), F32)),
        in_specs=[HBM] * (2 * n), out_specs=(SEM, SEM, *[HBM] * (2 * n), pl.BlockSpec(memory_space=pltpu.VMEM)),
        input_output_aliases={i: 2 + i for i in range(2 * n)},
        compiler_params=pltpu.CompilerParams(has_side_effects=DATAFLOW),
    )(*[hbm(a) for a in list(slabs) + lands])


def pair_wait(started, after, tag):
    send_sems, recv_sems, *rest = started
    n = (len(rest) - 1) // 2
    slabs, lands = rest[:n], rest[n:2 * n]

    def body(*refs):
        g_refs, land_refs = refs[:n], refs[n:2 * n]
        send_ref, recv_ref = refs[2 * n], refs[2 * n + 1]
        x, y, c = _place()
        for i in range(n):
            hr = slabs[i].shape[1] // 2
            cp = _remote(g_refs[i].at[:, pl.ds((1 - c) * hr, hr), :], land_refs[i], send_ref.at[i], recv_ref.at[i], (x, y, 1 - c))
            cp.wait_send()
            cp.wait_recv()

    out = pl.pallas_call(
        body, name="pair_wait_" + tag,
        out_shape=tuple(pltpu.HBM(a.shape, a.dtype) for a in list(slabs) + list(lands)),
        in_specs=[HBM] * (2 * n) + [SEM, SEM, pl.BlockSpec(memory_space=pl.ANY)], out_specs=tuple([HBM] * (2 * n)),
        input_output_aliases={i: i for i in range(2 * n)},
        compiler_params=pltpu.CompilerParams(has_side_effects=DATAFLOW),
    )(*slabs, *lands, send_sems, recv_sems, after)
    return list(out[:n]), list(out[n:])


def _tile2(rows, cols):
    fits = lambda r, c: r * c * 4 <= BLOCK_BYTES
    if fits(rows, cols):
        return rows, cols
    tiles = [(r, cols) for r in (1024, 512, 256, 128, 64) if rows % r == 0 and fits(r, cols)]
    tiles += [(rows, cols // k) for k in (2, 3, 4, 6, 8, 12, 16) if cols % (k * LANES) == 0 and fits(rows, cols // k)]
    return max(tiles, key=lambda t: t[0] * t[1])


def pair_add(g, p, c, name):
    _, hr, cols = p.shape
    tm, tc = _tile2(hr, cols)
    per = hr // tm

    def body(c_ref, g_ref, p_ref, o_ref):
        o_ref[...] = (g_ref[...] + p_ref[...]).astype(o_ref.dtype)

    return pl.pallas_call(
        body, name=name,
        grid_spec=pltpu.PrefetchScalarGridSpec(
            num_scalar_prefetch=1, grid=(4, per, cols // tc),
            in_specs=[pl.BlockSpec((None, tm, tc), lambda k, i, j, c_ref: (k, c_ref[0] * per + i, j)),
                      pl.BlockSpec((None, tm, tc), lambda k, i, j, c_ref: (k, i, j))],
            out_specs=pl.BlockSpec((None, tm, tc), lambda k, i, j, c_ref: (k, i, j))),
        out_shape=jax.ShapeDtypeStruct((4, hr, cols), BF16),
        compiler_params=_params(("arbitrary", "arbitrary", "arbitrary")),
    )(c.reshape(1).astype(jnp.int32), g, p)


def scatter_start(sums, tag):
    n = len(sums)

    def body(*refs):
        s_refs, land_refs = refs[:n], refs[n:2 * n]
        send_sems, recv_sems = refs[2 * n], refs[2 * n + 1]
        token = refs[-1]
        x, y, c = _place()
        k = 2 * x + y
        for i in range(n):
            for j, (px, py) in enumerate(_other_chips(x, y)):
                _remote(s_refs[i].at[2 * px + py], land_refs[i].at[k], send_sems.at[j * n + i], recv_sems.at[j * n + i],
                        (px, py, c)).start()
        token[...] = jnp.zeros_like(token)

    hbm = lambda a: pltpu.with_memory_space_constraint(a, pltpu.HBM)
    return pl.pallas_call(
        body, name="scatter_start_" + tag,
        out_shape=(pltpu.SemaphoreType.DMA((3 * n,)), pltpu.SemaphoreType.DMA((3 * n,)),
                   *[pltpu.HBM(s.shape, s.dtype) for s in sums], *[pltpu.HBM(s.shape, s.dtype) for s in sums],
                   jax.ShapeDtypeStruct((8, LANES), F32)),
        in_specs=[HBM] * (2 * n), out_specs=(SEM, SEM, *[HBM] * (2 * n), pl.BlockSpec(memory_space=pltpu.VMEM)),
        input_output_aliases={i: 2 + i for i in range(2 * n)},
        compiler_params=pltpu.CompilerParams(has_side_effects=DATAFLOW),
    )(*[hbm(s) for s in sums], *[hbm(lax.empty(s.shape, s.dtype)) for s in sums])


def scatter_wait(started, after, tag):
    send_sems, recv_sems, *rest = started
    n = (len(rest) - 1) // 2
    sums, lands = rest[:n], rest[n:2 * n]

    def body(*refs):
        s_refs, land_refs = refs[:n], refs[n:2 * n]
        send_ref, recv_ref = refs[2 * n], refs[2 * n + 1]
        x, y, c = _place()
        for i in range(n):
            for j, (px, py) in enumerate(_other_chips(x, y)):
                cp = _remote(s_refs[i].at[2 * px + py], land_refs[i].at[2 * px + py], send_ref.at[j * n + i],
                             recv_ref.at[j * n + i], (px, py, c))
                cp.wait_send()
                cp.wait_recv()

    out = pl.pallas_call(
        body, name="scatter_wait_" + tag,
        out_shape=tuple(pltpu.HBM(s.shape, s.dtype) for s in sums + lands),
        in_specs=[HBM] * (2 * n) + [SEM, SEM, pl.BlockSpec(memory_space=pl.ANY)], out_specs=tuple([HBM] * (2 * n)),
        input_output_aliases={i: i for i in range(2 * n)},
        compiler_params=pltpu.CompilerParams(has_side_effects=DATAFLOW),
    )(*sums, *lands, send_sems, recv_sems, after)
    return list(out[:n]), list(out[n:])


def sum_chips(landed, own, chip, core, name):
    _, hr, cols = landed.shape
    tm, tc = _tile2(hr, cols)
    per = hr // tm

    def body(idx_ref, l0, l1, l2, l3, own_ref, o_ref):
        mine = own_ref[...].astype(F32)
        v = [jnp.where(idx_ref[0] == k, mine, ref[...].astype(F32)) for k, ref in enumerate((l0, l1, l2, l3))]
        o_ref[...] = ((v[0] + v[1]) + v[2]) + v[3]

    slot = lambda k: pl.BlockSpec((None, tm, tc),
                                  lambda i, j, idx: (jnp.where(idx[0] == k, (k + 1) & 3, k), i, j))
    return pl.pallas_call(
        body, name=name,
        grid_spec=pltpu.PrefetchScalarGridSpec(
            num_scalar_prefetch=1, grid=(per, cols // tc),
            in_specs=[slot(0), slot(1), slot(2), slot(3),
                      pl.BlockSpec((None, tm, tc), lambda i, j, idx: (idx[0], i, j))],
            out_specs=pl.BlockSpec((tm, tc), lambda i, j, idx: (idx[1] * per + i, j))),
        out_shape=jax.ShapeDtypeStruct((2 * hr, cols), F32),
        compiler_params=_params(("arbitrary", "arbitrary")),
    )(jnp.stack([chip, core]).astype(jnp.int32), landed, landed, landed, landed, own)


def exchange_halves(bufs):
    n = len(bufs)

    def body(*refs):
        out_refs = refs[n:2 * n]
        send_sems, recv_sems = refs[2 * n:]
        x, y, c = _place()
        sends, recvs = [], []
        for i in range(n):
            hr = bufs[i].shape[0] // 2
            own = out_refs[i].at[pl.ds(c * hr, hr), :]
            other = out_refs[i].at[pl.ds((1 - c) * hr, hr), :]
            sends.append(_remote(own, own, send_sems.at[i], recv_sems.at[i], (x, y, 1 - c)))
            recvs.append(_remote(other, other, send_sems.at[i], recv_sems.at[i], (x, y, 1 - c)))
        for cp in sends:
            cp.start()
        for cp in recvs:
            cp.wait_recv()
        for cp in sends:
            cp.wait_send()

    return pl.pallas_call(
        body, name="exchange_halves", in_specs=[HBM] * n, out_specs=[HBM] * n,
        out_shape=[jax.ShapeDtypeStruct(b.shape, b.dtype) for b in bufs],
        input_output_aliases={i: i for i in range(n)},
        scratch_shapes=[pltpu.SemaphoreType.DMA((n,)), pltpu.SemaphoreType.DMA((n,))],
    )(*bufs)


def assemble_in_proj(landed, own, chip):
    rows, cols = 128, own.shape[1]

    def body(idx_ref, l0, l1, l2, l3, own_ref, o_ref):
        mine = own_ref[...]
        w = jnp.concatenate([jnp.where(idx_ref[0] == k, mine, ref[...]) for k, ref in enumerate((l0, l1, l2, l3))], axis=1)
        o_ref[...] = jnp.concatenate([w[:, :ORIG_Z], w[:, ORIG_GA:], w[:, ORIG_XBC:ORIG_DT], w[:, ORIG_Z:ORIG_XBC],
                                      w[:, ORIG_DT:ORIG_GA], jnp.zeros((rows, IN_PAD - IN_ORIG), w.dtype)], axis=1)

    slot = lambda k: pl.BlockSpec((None, rows, cols), lambda i, idx: (jnp.where(idx[0] == k, (k + 1) & 3, k), i, 0))
    return pl.pallas_call(
        body, name="assemble_in_proj",
        grid_spec=pltpu.PrefetchScalarGridSpec(
            num_scalar_prefetch=1, grid=(D // rows,),
            in_specs=[slot(0), slot(1), slot(2), slot(3), pl.BlockSpec((rows, cols), lambda i, idx: (i, 0))],
            out_specs=pl.BlockSpec((rows, IN_PAD), lambda i, idx: (i, 0))),
        out_shape=jax.ShapeDtypeStruct((D, IN_PAD), own.dtype),
        compiler_params=_params(("arbitrary",)),
    )(chip.reshape(1).astype(jnp.int32), landed, landed, landed, landed, own)


def rows_exchange(a, name):
    hr = a.shape[0] // 2

    def body(a_ref, out_ref, send_sem, recv_sem):
        x, y, c = _place()
        cp = _remote(a_ref.at[pl.ds((1 - c) * hr, hr), :], out_ref, send_sem, recv_sem, (x, y, 1 - c))
        cp.start()
        cp.wait()

    return pl.pallas_call(
        body, name=name, in_specs=[HBM], out_specs=HBM,
        out_shape=jax.ShapeDtypeStruct((hr, a.shape[1]), a.dtype),
        scratch_shapes=[pltpu.SemaphoreType.DMA, pltpu.SemaphoreType.DMA],
    )(a)


def split_pair_add(dw, received, core):
    cols = IN_ORIG // 4
    rows, hr = 128, D // 2
    per = hr // rows

    def body(c_ref, own_ref, got_ref, o_ref):
        d = own_ref[...] + got_ref[...]
        w = jnp.concatenate([d[:, :COL_GA], d[:, COL_Z:COL_DT], d[:, COL_XBC:COL_Z], d[:, COL_DT:COL_DT + 32],
                             d[:, COL_GA:COL_XBC]], axis=1)
        for k in range(4):
            o_ref[k] = w[:, k * cols:(k + 1) * cols].astype(o_ref.dtype)

    return pl.pallas_call(
        body, name="split_pair_add",
        grid_spec=pltpu.PrefetchScalarGridSpec(
            num_scalar_prefetch=1, grid=(per,),
            in_specs=[pl.BlockSpec((rows, IN_PAD), lambda i, c_ref: (c_ref[0] * per + i, 0)),
                      pl.BlockSpec((rows, IN_PAD), lambda i, c_ref: (i, 0))],
            out_specs=pl.BlockSpec((4, rows, cols), lambda i, c_ref: (0, i, 0))),
        out_shape=jax.ShapeDtypeStruct((4, hr, cols), BF16),
        compiler_params=_params(("arbitrary",)),
    )(core.reshape(1).astype(jnp.int32), dw, received)


def ada_prepare(c_all, w_ada, hgrn_lb):
    def body(c_ref, w_ref, lb_ref, mod_ref, row_ref):
        mod_ref[...] = hdot(silu(c_ref[...]), w_ref[...])
        row_ref[...] = sigmoid(lb_ref[0:1, :] - lb_ref[1:2, :])

    return pl.pallas_call(
        body, name="ada_prepare",
        out_shape=[jax.ShapeDtypeStruct((8, w_ada.shape[1]), F32), jax.ShapeDtypeStruct((1, D), F32)],
        compiler_params=pltpu.CompilerParams(vmem_limit_bytes=VMEM_LIMIT),
    )(c_all, w_ada, hgrn_lb)


SMALL_SEGS = (("mod", 6 * D), ("lb", D), ("gnorm", LANES), ("conv_w", 4 * CONV_DIM), ("conv_b", CONV_DIM),
              ("dt_bias", LANES), ("a_log", B_INNER), ("d", B_INNER), ("ssm_norm", B_INNER),
              ("ln1_g", D), ("ln1_b", D), ("ln2_g", D), ("ln2_b", D), ("loss", LANES))
SMALL_PARAMS = ("b_ada", "hgrn_lb", "hgrn_gnorm", "ssm_conv_b", "ssm_dt_bias", "ssm_a_log", "ssm_d", "ssm_norm",
                "ln1_g", "ln1_b", "ln2_g", "ln2_b")


def finalize_small(g_all, c_all, dmod_cols, params, m, v):
    n_p = len(SMALL_PARAMS)
    offs, o = {}, 0
    for nm, width in SMALL_SEGS:
        offs[nm] = (o, width)
        o += width

    def body(*refs):
        g_ref, c_ref, dm_ref = refs[:3]
        p_refs = refs[3:3 + n_p]
        m_refs = refs[3 + n_p:3 + 2 * n_p]
        v_refs = refs[3 + 2 * n_p:3 + 3 * n_p]
        outs = refs[3 + 3 * n_p:]
        gwa_ref, gcw_ref, loss_ref = outs[:3]
        res = outs[3:]
        total = jnp.sum(g_ref[...], axis=0, keepdims=True)
        seg = lambda nm: total[:, offs[nm][0]:offs[nm][0] + offs[nm][1]]
        loss_ref[...] = seg("loss")
        gwa_ref[...] = hdot(silu(c_ref[...]), dm_ref[...], "tn")
        cw = seg("conv_w")
        for j in range(4):
            gcw_ref[j:j + 1, :] = cw[:, j * CONV_DIM:(j + 1) * CONV_DIM]
        hc = lax.broadcasted_iota(jnp.int32, (B_INNER, LANES), 0)
        hj = lax.broadcasted_iota(jnp.int32, (B_INNER, LANES), 1)
        per_head = ((hc >> 6) == hj).astype(F32)
        heads = lambda nm: hdot(jnp.broadcast_to(seg(nm), (8, B_INNER)), per_head)[0:1, 0:32]
        lbp = sigmoid(p_refs[1][0:1, :] - p_refs[1][1:2, :])
        g_row = seg("lb") * lbp * (1.0 - lbp)
        grads = {"b_ada": seg("mod"), "hgrn_gnorm": seg("gnorm"), "ssm_conv_b": seg("conv_b"),
                 "ssm_dt_bias": seg("dt_bias")[:, 0:32], "ssm_a_log": heads("a_log"), "ssm_d": heads("d"),
                 "ssm_norm": seg("ssm_norm"), "ln1_g": seg("ln1_g"), "ln1_b": seg("ln1_b"),
                 "ln2_g": seg("ln2_g"), "ln2_b": seg("ln2_b")}
        for i, nm in enumerate(SMALL_PARAMS):
            g_out, d_out, m_out, v_out = res[4 * i:4 * i + 4]
            if nm == "hgrn_lb":
                for row, gv in ((0, g_row), (1, -g_row)):
                    sl = slice(row, row + 1)
                    dl, mn, vn = adamw(p_refs[i][sl, :], gv, m_refs[i][sl, :], v_refs[i][sl, :])
                    g_out[sl, :], d_out[sl, :], m_out[sl, :], v_out[sl, :] = gv, dl, mn, vn
            else:
                gv = grads[nm]
                dl, mn, vn = adamw(p_refs[i][...], gv, m_refs[i][...], v_refs[i][...])
                g_out[...], d_out[...], m_out[...], v_out[...] = gv, dl, mn, vn

    out_shape = [jax.ShapeDtypeStruct((D, dmod_cols.shape[1]), F32), jax.ShapeDtypeStruct((4, CONV_DIM), F32),
                 jax.ShapeDtypeStruct((1, LANES), F32)]
    for p in params:
        out_shape += [jax.ShapeDtypeStruct(p.shape, F32)] * 4
    return pl.pallas_call(
        body, name="finalize_small", out_shape=out_shape,
        compiler_params=pltpu.CompilerParams(vmem_limit_bytes=VMEM_LIMIT),
    )(g_all, c_all, dmod_cols, *params, *m, *v)


def adam_update(w, g, m, v, name):
    rows, cols = w.shape
    tm, tc = _tile2(rows, cols)

    def body(w_ref, g_ref, m_ref, v_ref, d_ref, mo_ref, vo_ref):
        d_ref[...], mo_ref[...], vo_ref[...] = adamw(w_ref[...], g_ref[...], m_ref[...], v_ref[...])

    spec = pl.BlockSpec((tm, tc), lambda i, j: (i, j))
    return pl.pallas_call(
        body, name=name, grid=(rows // tm, cols // tc), in_specs=[spec] * 4, out_specs=[spec] * 3,
        out_shape=[jax.ShapeDtypeStruct((rows, cols), F32)] * 3,
        compiler_params=_params(("arbitrary", "arbitrary")),
    )(w, g, m, v)


def kernel(x, c, w_ada, b_ada, w_in, hgrn_lb, hgrn_gnorm, ssm_conv_w, ssm_conv_b, ssm_dt_bias, ssm_a_log, ssm_d, ssm_norm, w_branch_a, w_branch_b, w_o, ln1_g, ln1_b, w_ffn_gate, w_ffn_up, w_ffn_down, ln2_g, ln2_b, loss_target, m_w_ada, m_b_ada, m_w_in, m_hgrn_lb, m_hgrn_gnorm, m_ssm_conv_w, m_ssm_conv_b, m_ssm_dt_bias, m_ssm_a_log, m_ssm_d, m_ssm_norm, m_w_branch_a, m_w_branch_b, m_w_o, m_ln1_g, m_ln1_b, m_w_ffn_gate, m_w_ffn_up, m_w_ffn_down, m_ln2_g, m_ln2_b, v_w_ada, v_b_ada, v_w_in, v_hgrn_lb, v_hgrn_gnorm, v_ssm_conv_w, v_ssm_conv_b, v_ssm_dt_bias, v_ssm_a_log, v_ssm_d, v_ssm_norm, v_w_branch_a, v_w_branch_b, v_w_o, v_ln1_g, v_ln1_b, v_w_ffn_gate, v_w_ffn_up, v_w_ffn_down, v_ln2_g, v_ln2_b):
    given = dict(locals())
    chip = 2 * lax.axis_index("x") + lax.axis_index("y")
    core = lax.axis_index("c")
    t = x.shape[1]

    first = gather_rows(jnp.concatenate([c, ssm_conv_w.reshape(1, CONV_DIM)], axis=1), "gather_cond").reshape(8, D + CONV_DIM)
    c_all = first[:, :D]
    conv_w = first[0::2, D:].reshape(4, 4, CONV_DIM // 4).transpose(1, 0, 2).reshape(4, CONV_DIM)
    mod_part, lb_row = ada_prepare(c_all, w_ada[0], hgrn_lb)
    mod_cols = w_ada.shape[2]
    mod_row = exchange_rows(mod_part.reshape(8, 1, mod_cols), "exchange_mod").reshape(1, 6 * D) + b_ada

    local = {nm: given[nm][0] for nm in SHARDED if nm != "w_ffn_in"}
    local["w_ffn_in"] = jnp.concatenate([w_ffn_gate[0].T, w_ffn_up[0].T], axis=0)
    shards = [local[nm].astype(BF16) for nm in SHARDED]
    send_in, recv_in, sent_in, land_in, started_in = gather_start(shards[:1], mod_row, "in")
    shards = shards[:1] + [(local[nm] + started_in[0, 0]).astype(BF16) for nm in SHARDED[1:]]
    send_rest, recv_rest, *flying = gather_start(shards[1:], started_in, "rest")
    n_rest = len(SHARDED) - 1
    sent_rest, land_rest, started_rest = flying[:n_rest], flying[n_rest:2 * n_rest], flying[-1]
    mod_row = mod_row + started_rest[0:1, 0:1]
    mod = tuple(mod_row[:, i * D:(i + 1) * D] for i in range(6))
    with_own = lambda land, shard: lax.dynamic_update_slice(land, shard[None], (chip, 0, 0))

    class Weights:
        def input_projection(self, after):
            (own,), land = gather_wait(send_in, recv_in, [sent_in], [land_in], after, "in")
            (land,) = forward_wait(forward_start(land, "in"), after, "in")
            return assemble_in_proj(land, own, chip)

        def start_rest(self, after):
            self.own, landed = gather_wait(send_rest, recv_rest, sent_rest, land_rest, after, "rest")
            self.started = forward_start(landed, "rest")
            return self.started[-1]

        def rest(self, after):
            got = {nm: with_own(land, s) for nm, land, s in zip(SHARDED[1:], forward_wait(self.started, after, "rest"), self.own, strict=True)}
            whole = lambda nm: got[nm].reshape(4 * got[nm].shape[1], got[nm].shape[2])
            return tuple(whole(nm) for nm in SHARDED[1:])

    wts = Weights()

    per_head = lambda p: jnp.pad(p, ((0, 0), (0, LANES - p.shape[1])))
    per_channel = lambda p: jnp.repeat(p[0], B_INNER // 32)[None]
    small = (lb_row, hgrn_gnorm, conv_w, ssm_conv_b, per_head(ssm_dt_bias), per_channel(ssm_a_log),
             per_channel(ssm_d), ssm_norm, ln1_g, ln1_b, ln2_g, ln2_b)
    by_rows = lambda g: g.reshape(4, g.shape[0] // 4, g.shape[1])
    travelling = {}

    def start_early(dws):
        travelling["pair"] = pair_start([by_rows(dw) for dw in dws], "early")
        return travelling["pair"][-1]

    def between_scans(after):
        slabs, received = pair_wait(travelling["pair"], after, "early")
        travelling["pairs"] = [pair_add(s, r, core, "pair_add_" + nm) for nm, s, r in zip(SHARDED[1:], slabs, received, strict=True)]
        travelling["started"] = scatter_start(travelling["pairs"], "early")
        return travelling["started"][-1]

    def finish_early(after):
        travelling["pairs"], travelling["landed"] = scatter_wait(travelling["started"], after, "early")

    def start_last(dw_in):
        travelling["pairs_in"] = [split_pair_add(dw_in, rows_exchange(dw_in, "pair_exchange_last"), core)]
        travelling["started_in"] = scatter_start(travelling["pairs_in"], "last")
        return travelling["started_in"][-1]

    loss, grad_x, d_mod, d_wts, d_small = local_step(x[0], loss_target[0], mod, wts, small,
                                                     start_early, between_scans, finish_early, start_last)

    d_lb, d_gn, d_cw, d_cb, d_dtb, d_alog, d_dsk, d_nw, d_l1g, d_l1b, d_l2g, d_l2b = d_small
    row = jnp.concatenate(list(d_mod) + [d_lb, d_gn, d_cw.reshape(1, 4 * CONV_DIM), d_cb, d_dtb, d_alog, d_dsk, d_nw,
                                          d_l1g, d_l1b, d_l2g, d_l2b, jnp.pad(loss, ((0, 0), (0, LANES - 1)))], axis=1)
    g_all = gather_rows(row, "gather_small_grads").reshape(8, row.shape[1])
    dmod_cols = lax.dynamic_slice_in_dim(g_all, chip * mod_cols, mod_cols, axis=1)
    fin = finalize_small(g_all, c_all, dmod_cols, [given[n] for n in SMALL_PARAMS],
                         [given["m_" + n] for n in SMALL_PARAMS], [given["v_" + n] for n in SMALL_PARAMS])
    grads, deltas, new_m, new_v = {}, {}, {}, {}
    grads["w_ada"] = fin[0][None]
    grads["ssm_conv_w"] = lax.dynamic_slice_in_dim(fin[1], chip * (CONV_DIM // 4), CONV_DIM // 4, axis=1)[None]
    for i, nm in enumerate(SMALL_PARAMS):
        grads[nm], deltas[nm], new_m[nm], new_v[nm] = fin[3 + 4 * i:7 + 4 * i]

    pairs_in, landed_in = scatter_wait(travelling["started_in"], fin[3], "last")
    pairs, landed = pairs_in + travelling["pairs"], landed_in + travelling["landed"]
    halves = [sum_chips(r, p, chip, core, "sum_chips_" + nm) for nm, r, p in zip(SHARDED, landed, pairs, strict=True)]
    reduced = dict(zip(SHARDED, exchange_halves(halves), strict=True))
    reduced["w_ada"], reduced["ssm_conv_w"] = grads["w_ada"][0], grads["ssm_conv_w"][0]
    reduced["w_in"] = reduced["w_in"].T
    reduced["w_ffn_gate"], reduced["w_ffn_up"] = reduced["w_ffn_in"][:FFN_SHARD], reduced["w_ffn_in"][FFN_SHARD:]
    for nm in ("w_ada", "ssm_conv_w", "w_in", "w_branch_a", "w_branch_b", "w_o", "w_ffn_gate", "w_ffn_up", "w_ffn_down"):
        flipped = nm in ("w_in", "w_ffn_gate", "w_ffn_up")
        work = (lambda a: a[0].T) if flipped else (lambda a: a[0])
        back = (lambda a: a.T[None]) if flipped else (lambda a: a[None])
        d_, m_, v_ = adam_update(work(given[nm]), reduced[nm], work(given["m_" + nm]), work(given["v_" + nm]), "adam_" + nm)
        grads[nm], deltas[nm], new_m[nm], new_v[nm] = back(reduced[nm]), back(d_), back(m_), back(v_)

    names = ("w_ada", "b_ada", "w_in", "hgrn_lb", "hgrn_gnorm", "ssm_conv_w", "ssm_conv_b", "ssm_dt_bias", "ssm_a_log",
             "ssm_d", "ssm_norm", "w_branch_a", "w_branch_b", "w_o", "ln1_g", "ln1_b", "w_ffn_gate", "w_ffn_up",
             "w_ffn_down", "ln2_g", "ln2_b")
    return (fin[2][0, 0], grad_x[None], *[grads[n] for n in names], *[deltas[n] for n in names],
            *[new_m[n] for n in names], *[new_v[n] for n in names])
```

```python
import functools

import jax
import jax.numpy as jnp
from jax import lax
from jax.experimental import pallas as pl
from jax.experimental.pallas import tpu as pltpu

F32, BF16 = jnp.float32, jnp.bfloat16
HI = lax.Precision.HIGHEST
MESH = pl.DeviceIdType.MESH

D = 1024
CHUNK = 64
LANES = 128
N_HEADS_A = 8
N_GROUPS_B = 4
B_INNER = 2048
CONV_DIM = 3072
D_FF = 2816
ALPHA = 2.0 ** 0.25
LN_EPS = 1e-5
RMS_EPS = 1e-6
ADAM_LR, ADAM_B1, ADAM_B2, ADAM_EPS, ADAM_WD, ADAM_STEP = 0.001, 0.9, 0.999, 1e-08, 0.01, 10

IN_ORIG = 11296
IN_PAD = 11520
COL_GA, COL_GB, COL_XBC, COL_Z, COL_DT = 4096, 5120, 6144, 9216, 11264
ORIG_Z, ORIG_XBC, ORIG_DT, ORIG_GA = 4096, 6144, 9216, 9248

SHARDED = ("w_in", "w_branch_a", "w_branch_b", "w_o", "w_ffn_in", "w_ffn_down")
FFN_SHARD = D_FF // 4
VMEM_LIMIT = 56 * 1024 * 1024
BLOCK_BYTES = 2 * 1024 * 1024
_DIMS = {"nn": (((1,), (0,)), ((), ())), "nt": (((1,), (1,)), ((), ())), "tn": (((0,), (0,)), ((), ()))}


def _bd(a, b, mode):
    return lax.dot_general(a.astype(BF16), b.astype(BF16), _DIMS[mode], preferred_element_type=F32)


@functools.partial(jax.custom_vjp, nondiff_argnums=(2,))
def bdot(a, b, mode):
    return _bd(a, b, mode)


def _bdot_fwd(a, b, mode):
    return _bd(a, b, mode), (a, b)


def _bdot_bwd(mode, res, g):
    a, b = res
    if mode == "nn":
        return _bd(g, b, "nt"), _bd(a, g, "tn")
    if mode == "nt":
        return _bd(g, b, "nn"), _bd(g, a, "tn")
    return _bd(b, g, "nt"), _bd(a, g, "nn")


bdot.defvjp(_bdot_fwd, _bdot_bwd)


def hdot(a, b, mode="nn"):
    return lax.dot_general(a, b, _DIMS[mode], precision=HI, preferred_element_type=F32)


def _raw(a, b, mode):
    return lax.dot_general(a, b, _DIMS[mode], preferred_element_type=F32)


def _split(x, n):
    parts, rest = [], x
    for _ in range(n):
        p = rest.astype(BF16)
        parts.append(p)
        rest = rest - p.astype(F32)
    return parts


def _od(a, b, mode, exact):
    if exact == 1:
        e = b.astype(BF16)
        p = _split(a, 3)
        return (_raw(p[2], e, mode) + _raw(p[1], e, mode)) + _raw(p[0], e, mode)
    e = a.astype(BF16)
    p = _split(b, 3)
    return (_raw(e, p[2], mode) + _raw(e, p[1], mode)) + _raw(e, p[0], mode)


@functools.partial(jax.custom_vjp, nondiff_argnums=(2, 3))
def odot(a, b, mode, exact):
    return _od(a, b, mode, exact)


def _odot_fwd(a, b, mode, exact):
    return _od(a, b, mode, exact), (a, b)


def _odot_bwd(mode, exact, res, g):
    a, b = res
    if exact == 1:
        da = {"nn": lambda: _od(g, b, "nt", 1), "nt": lambda: _od(g, b, "nn", 1), "tn": lambda: _od(b, g, "nt", 0)}[mode]()
        return da, jnp.zeros_like(b)
    db = {"nn": lambda: _od(a, g, "tn", 0), "nt": lambda: _od(g, a, "tn", 1), "tn": lambda: _od(a, g, "nn", 0)}[mode]()
    return jnp.zeros_like(a), db


odot.defvjp(_odot_fwd, _odot_bwd)


_BDIMS = {"bnn": (((2,), (1,)), ((0,), (0,))), "bnt": (((2,), (2,)), ((0,), (0,))), "btn": (((1,), (1,)), ((0,), (0,)))}


def _braw(a, b, mode):
    return lax.dot_general(a, b, _BDIMS[mode], preferred_element_type=F32)


def _bdb(a, b, mode):
    return _braw(a.astype(BF16), b.astype(BF16), mode)


def _d3b(a, b, mode):
    ah, al = _split(a, 2)
    bh, bl = _split(b, 2)
    return _braw(ah, bh, mode) + (_braw(ah, bl, mode) + _braw(al, bh, mode))


def _batched_bwd(f):
    def bwd(mode, res, g):
        a, b = res
        if mode == "bnn":
            return f(g, b, "bnt"), f(a, g, "btn")
        if mode == "bnt":
            return f(g, b, "bnn"), f(g, a, "btn")
        return f(b, g, "bnt"), f(a, g, "bnn")
    return bwd


@functools.partial(jax.custom_vjp, nondiff_argnums=(2,))
def bdot_b(a, b, mode):
    return _bdb(a, b, mode)


bdot_b.defvjp(lambda a, b, mode: (_bdb(a, b, mode), (a, b)), _batched_bwd(_bdb))


@functools.partial(jax.custom_vjp, nondiff_argnums=(2,))
def dot3_b(a, b, mode):
    return _d3b(a, b, mode)


dot3_b.defvjp(lambda a, b, mode: (_d3b(a, b, mode), (a, b)), _batched_bwd(_d3b))


def _cum(tril3, x, mode):
    e = tril3.astype(BF16)
    p = _split(x, 3)
    return (_braw(e, p[2], mode) + _braw(e, p[1], mode)) + _braw(e, p[0], mode)


@jax.custom_vjp
def chunk_cumsum(tril3, x):
    return _cum(tril3, x, "bnn")


chunk_cumsum.defvjp(lambda t, x: (_cum(t, x, "bnn"), t), lambda t, g: (jnp.zeros_like(t), _cum(t, g, "btn")))


def _unstack(axis, n):
    @jax.custom_vjp
    def un(x):
        return tuple(lax.index_in_dim(x, i, axis, keepdims=False) for i in range(n))

    un.defvjp(lambda x: (un(x), None), lambda _, g: (jnp.stack(g, axis=axis),))
    return un


def _split_last(n, w):
    @jax.custom_vjp
    def sp(x):
        return tuple(x[..., i * w:(i + 1) * w] for i in range(n))

    sp.defvjp(lambda x: (sp(x), None), lambda _, g: (jnp.concatenate(g, axis=-1),))
    return sp


def sigmoid(x):
    return 0.5 * jnp.tanh(0.5 * x) + 0.5


def silu(x):
    return x * sigmoid(x)


def softplus(x):
    return jnp.maximum(x, 0.0) + jnp.log1p(jnp.exp(jnp.minimum(x, -x)))


def _ln(x):
    mu = jnp.mean(x, axis=-1, keepdims=True)
    xc = x - mu
    return xc * lax.rsqrt(jnp.mean(xc * xc, axis=-1, keepdims=True) + LN_EPS)


def _tril64():
    r = lax.broadcasted_iota(jnp.int32, (CHUNK, CHUNK), 0)
    c = lax.broadcasted_iota(jnp.int32, (CHUNK, CHUNK), 1)
    return (r >= c).astype(F32)


def hgrn_block(q, fl, iv, gr, st, lb, gn):
    tb = q.shape[0]
    nc = tb // CHUNK
    nh = N_HEADS_A
    heads = _split_last(nh, LANES)
    to4 = lambda a: jnp.stack(heads(a), axis=0).reshape(nh, nc, CHUNK, LANES)
    flat = lambda a: a.reshape(nh * nc, CHUNK, LANES)
    f = lb + (1.0 - lb) * sigmoid(fl)
    gl4, k4, qf4, v4, gr4 = to4(jnp.log(f)), to4(1.0 - f), to4(silu(q) * (128 ** -0.5)), to4(iv), to4(gr)
    tril = _tril64()
    b4 = chunk_cumsum(jnp.broadcast_to(tril[None], (nh * nc, CHUNK, CHUNK)), flat(gl4)).reshape(gl4.shape)
    blast = jnp.sum(gl4, axis=2, keepdims=True)
    ref = lax.stop_gradient(0.5 * blast)
    qp, kp = qf4 * jnp.exp(b4 - ref), k4 * jnp.exp(ref - b4)
    sc = dot3_b(flat(qp), flat(kp), "bnt") * tril
    o_intra = bdot_b(sc, flat(v4), "bnn").reshape(gl4.shape)
    chunks = _unstack(1, nc)
    qe, v_c, kd, dec = chunks(qp * jnp.exp(ref)), chunks(v4), chunks(kp * jnp.exp(blast - ref)), chunks(jnp.exp(blast))
    o_inter = []
    for c in range(nc):
        o_inter.append(bdot_b(qe[c], st, "bnt"))
        st = st * dec[c] + bdot_b(v_c[c], kd[c], "btn")
    o = o_intra + jnp.stack(o_inter, axis=1)
    on = o * lax.rsqrt(jnp.mean(o * o, axis=-1, keepdims=True) + RMS_EPS) * gn
    out = (on * silu(gr4)).reshape(nh, tb, LANES)
    return jnp.concatenate(_unstack(0, nh)(out), axis=1), st


def ssd_consts(g):
    i32 = jnp.int32
    ej = lax.broadcasted_iota(i32, (LANES, 512), 0)
    ec = lax.broadcasted_iota(i32, (LANES, 512), 1)
    expand = (ej == g * 8 + (ec >> 6)).astype(F32)
    ts = lax.broadcasted_iota(i32, (CHUNK, 512), 0)
    tc = lax.broadcasted_iota(i32, (CHUNK, 512), 1)
    itile = (ts == (tc & 63)).astype(F32)
    maskall = ts >= (tc & 63)
    br = lax.broadcasted_iota(i32, (LANES, LANES), 0)
    bc = lax.broadcasted_iota(i32, (LANES, LANES), 1)
    blockmask = ((br >> 6) == (bc >> 6)).astype(F32)
    return expand, itile, maskall, blockmask, _tril64()


def ssd_block(x, bm, cm, dt, z, st, dtb, alog, dsk, nw, cs):
    expand, itile, maskall, blockmask, tril = cs
    tb = x.shape[0]
    nc = tb // CHUNK
    delta = odot(softplus(dt + dtb), expand, "nn", 1)
    a = -jnp.exp(alog) * delta
    xdt = x * delta
    by_chunk = lambda v: v.reshape(nc, CHUNK, v.shape[-1])
    a3, xdt3, bm3, cm3 = by_chunk(a), by_chunk(xdt), by_chunk(bm), by_chunk(cm)
    acum3 = chunk_cumsum(jnp.broadcast_to(tril[None], (nc, CHUNK, CHUNK)), a3)
    alast3 = jnp.sum(a3, axis=1, keepdims=True)
    cb3 = bdot_b(cm3, jnp.concatenate([bm3] * 8, axis=1), "bnt")
    arow3 = jnp.sum(acum3 * itile, axis=1, keepdims=True)
    dec3 = jnp.exp(jnp.where(maskall, acum3 - arow3, -1e30))
    pairs = _split_last(4, LANES)
    intra = [bdot_b(m, jnp.concatenate([xp] * 2, axis=1) * blockmask, "bnn")
             for m, xp in zip(pairs(cb3 * dec3), pairs(xdt3))]
    chunks = _unstack(0, nc)
    cm_c, bm_c, xw_c, dec_c = chunks(cm3), chunks(bm3), chunks(xdt3 * jnp.exp(alast3 - acum3)), chunks(jnp.exp(alast3))
    inter = []
    for c in range(nc):
        inter.append(bdot(cm_c[c], st, "nn"))
        st = st * dec_c[c] + bdot(bm_c[c], xw_c[c], "tn")
    st_new = st
    y = (jnp.concatenate(intra, axis=-1) + jnp.stack(inter, axis=0) * jnp.exp(acum3)).reshape(tb, 512)
    yz = (y + x * dsk) * silu(z)
    return yz * lax.rsqrt(jnp.mean(yz * yz, axis=-1, keepdims=True) + RMS_EPS) * nw, st_new


def adamw(w, g, m, v):
    m = ADAM_B1 * m + (1.0 - ADAM_B1) * g
    v = ADAM_B2 * v + (1.0 - ADAM_B2) * jnp.square(g)
    m_hat = m / (1.0 - ADAM_B1 ** ADAM_STEP)
    v_hat = v / (1.0 - ADAM_B2 ** ADAM_STEP)
    return -ADAM_LR * (m_hat / (jnp.sqrt(v_hat) + ADAM_EPS) + ADAM_WD * w), m, v


def _pick(n, cands):
    for c in cands:
        if n % c == 0:
            return c
    return n


def _params(sem):
    return pltpu.CompilerParams(dimension_semantics=sem, vmem_limit_bytes=VMEM_LIMIT)


MATMUL_VMEM_BUDGET = 50 * 1024 * 1024
MATMUL_MIN_STEPS = 4


def matmul(a, b, mode, out_dtype, name, after=None):
    if mode == "nn":
        (m, k), n = a.shape, b.shape[1]
    elif mode == "nt":
        (m, k), n = a.shape, b.shape[0]
    else:
        (k, m), n = a.shape, b.shape[1]
    tk = _pick(k, (2304, 2048, 1408, 1024, 768, 512, 256, 128))
    nk = k // tk
    a_bytes, b_bytes, out_bytes = a.dtype.itemsize, b.dtype.itemsize, jnp.dtype(out_dtype).itemsize

    def vmem(tm_, tn_):
        blocks = 2 * (tm_ * tk * a_bytes + tk * tn_ * b_bytes + tm_ * tn_ * out_bytes)
        return blocks + (tm_ * tn_ * 4 if nk > 1 else 0)

    def traffic(tm_, tn_):
        return (m // tm_) * k * n * b_bytes + (n // tn_ if nk > 1 else 1) * m * k * a_bytes

    sizes = (2304, 2048, 1920, 1408, 1024, 768, 512, 256, 128)
    tiles = [(tm_, tn_) for tm_ in sizes if m % tm_ == 0 for tn_ in sizes if n % tn_ == 0
             if vmem(tm_, tn_) <= MATMUL_VMEM_BUDGET] or [(m, n)]
    pipelined = [t for t in tiles if (m // t[0]) * (n // t[1]) * nk >= MATMUL_MIN_STEPS]
    tm, tn = min(pipelined or tiles, key=lambda t: (traffic(*t), -t[0] * t[1]))
    a_spec = pl.BlockSpec((tk, tm), lambda i, j, kk: (kk, i)) if mode == "tn" else pl.BlockSpec((tm, tk), lambda i, j, kk: (i, kk))
    b_spec = pl.BlockSpec((tn, tk), lambda i, j, kk: (j, kk)) if mode == "nt" else pl.BlockSpec((tk, tn), lambda i, j, kk: (kk, j))

    order = [] if after is None else [after]

    def body(a_ref, b_ref, *rest):
        o_ref, *acc = rest[len(order):]
        part = _bd(a_ref[...], b_ref[...], mode)
        if nk == 1:
            o_ref[...] = part.astype(o_ref.dtype)
            return
        acc_ref, = acc
        kk = pl.program_id(2)

        @pl.when(kk == 0)
        def _():
            acc_ref[...] = part

        @pl.when(jnp.logical_and(kk > 0, kk < nk - 1))
        def _():
            acc_ref[...] += part

        @pl.when(kk == nk - 1)
        def _():
            o_ref[...] = (acc_ref[...] + part).astype(o_ref.dtype)

    return pl.pallas_call(
        body, name=name, grid=(m // tm, n // tn, nk),
        in_specs=[a_spec, b_spec] + [pl.BlockSpec(memory_space=pl.ANY) for _ in order],
        out_specs=pl.BlockSpec((tm, tn), lambda i, j, kk: (i, j)),
        out_shape=jax.ShapeDtypeStruct((m, n), out_dtype),
        scratch_shapes=[pltpu.VMEM((tm, tn), F32)] if nk > 1 else [],
        compiler_params=_params(("parallel", "parallel", "arbitrary")),
    )(a, b, *order)


def rowwise(name, fn, rows, consts, out_rows, out_accs=(), tm_max=512, into=None, new_wide=None):
    t = rows[0][0].shape[0]
    tm = _pick(t, (tm_max, 128, 64, 32, 16, 8))
    n_r, n_c, n_o = len(rows), len(consts), len(out_rows)
    n_alias = 0 if into is None else 1

    def body(*refs):
        r_in = [r[...] for r in refs[:n_r]]
        c_in = [r[...] for r in refs[n_r:n_r + n_c]]
        refs = refs[:n_r + n_c] + refs[n_r + n_c + n_alias:]
        o_refs = refs[n_r + n_c:n_r + n_c + n_o]
        a_refs = refs[n_r + n_c + n_o:]
        ro, ao = fn(r_in, c_in)
        for ref, val in zip(o_refs, ro, strict=True):
            ref[...] = val.astype(ref.dtype)
        if a_refs:
            @pl.when(pl.program_id(0) == 0)
            def _():
                for ref in a_refs:
                    ref[...] = jnp.zeros_like(ref)

            for ref, val in zip(a_refs, ao, strict=True):
                ref[...] += val

    in_specs = [pl.BlockSpec((tm, w), functools.partial(lambda i, cb: (i, cb), cb=cb)) for _, w, cb in rows]
    in_specs += [pl.BlockSpec(c.shape, lambda i: (0, 0)) for c in consts]
    out_specs = [pl.BlockSpec((tm, w), lambda i: (i, 0)) for w, _ in out_rows]
    out_specs += [pl.BlockSpec(s, lambda i: (0, 0)) for s in out_accs]
    out_shape = [jax.ShapeDtypeStruct((t, w), dt) for w, dt in out_rows]
    out_shape += [jax.ShapeDtypeStruct(s, F32) for s in out_accs]
    operands = [r[0] for r in rows] + list(consts)
    aliases = {}
    if into is not None:
        target, cb = into
        in_specs.append(pl.BlockSpec(memory_space=pl.ANY))
        operands.append(target)
        out_specs[0] = pl.BlockSpec((tm, out_rows[0][0]), lambda i: (i, cb))
        out_shape[0] = jax.ShapeDtypeStruct(target.shape, target.dtype)
        aliases = {len(operands) - 1: 0}
    if new_wide is not None:
        width, cb = new_wide
        out_specs[0] = pl.BlockSpec((tm, out_rows[0][0]), lambda i: (i, cb))
        out_shape[0] = jax.ShapeDtypeStruct((t, width), out_rows[0][1])
    return pl.pallas_call(
        body, name=name, grid=(t // tm,), in_specs=in_specs, out_specs=out_specs, out_shape=out_shape,
        input_output_aliases=aliases, compiler_params=_params(("arbitrary",)),
    )(*operands)


def _full(a):
    return (a, a.shape[1], 0)


HGRN_TIME_BLOCK = 256
SSD_TIME_BLOCK = 512


def _time_block(t, most=HGRN_TIME_BLOCK):
    return _pick(t, tuple(b for b in (512, 256, 128, 64) if b <= most))


def _quarters(ref):
    return [ref[:, seg * D:(seg + 1) * D] for seg in range(4)]


def hgrn_forward(proj, lb, gn):
    t = proj.shape[0]
    tb = _time_block(t)
    nb = t // tb

    def body(qfig_ref, lb_ref, gn_ref, o_ref, st_ref, state):
        @pl.when(pl.program_id(0) == 0)
        def _():
            state[...] = jnp.zeros_like(state)

        st = state[...]
        st_ref[...] = st
        out, st_new = hgrn_block(*_quarters(qfig_ref), st, lb_ref[...], gn_ref[...])
        o_ref[...] = out.astype(o_ref.dtype)
        state[...] = st_new

    return pl.pallas_call(
        body, name="hgrn_forward", grid=(nb,),
        in_specs=[pl.BlockSpec((tb, 4 * D), lambda j: (j, 0)),
                  pl.BlockSpec((1, D), lambda j: (0, 0)), pl.BlockSpec((1, LANES), lambda j: (0, 0))],
        out_specs=[pl.BlockSpec((tb, D), lambda j: (j, 0)),
                   pl.BlockSpec((None, N_HEADS_A, LANES, LANES), lambda j: (j, 0, 0, 0))],
        out_shape=[jax.ShapeDtypeStruct((t, D), BF16),
                   jax.ShapeDtypeStruct((nb, N_HEADS_A, LANES, LANES), F32)],
        scratch_shapes=[pltpu.VMEM((N_HEADS_A, LANES, LANES), F32)],
        compiler_params=_params(("arbitrary",)),
    )(proj, lb, gn)


def hgrn_backward(proj, states, d_out, lb, gn, d_proj):
    t = proj.shape[0]
    tb = _time_block(t)
    nb = t // tb

    def body(qfig_ref, st_ref, do_ref, lb_ref, gn_ref, _, dqfig_ref, dlb_ref, dgn_ref, d_state):
        @pl.when(pl.program_id(0) == 0)
        def _():
            d_state[...] = jnp.zeros_like(d_state)
            dlb_ref[...] = jnp.zeros_like(dlb_ref)
            dgn_ref[...] = jnp.zeros_like(dgn_ref)

        _, vjp = jax.vjp(hgrn_block, *_quarters(qfig_ref), st_ref[...], lb_ref[...], gn_ref[...])
        dq, df, di, dg, dst, dlb, dgn = vjp((do_ref[...], d_state[...]))
        for seg, val in enumerate((dq, df, di, dg)):
            dqfig_ref[:, seg * D:(seg + 1) * D] = val.astype(dqfig_ref.dtype)
        d_state[...] = dst
        dlb_ref[...] += dlb
        dgn_ref[...] += dgn

    rev = lambda j: nb - 1 - j
    return pl.pallas_call(
        body, name="hgrn_backward", grid=(nb,),
        in_specs=[pl.BlockSpec((tb, 4 * D), lambda j: (rev(j), 0)),
                  pl.BlockSpec((None, N_HEADS_A, LANES, LANES), lambda j: (rev(j), 0, 0, 0)),
                  pl.BlockSpec((tb, D), lambda j: (rev(j), 0)),
                  pl.BlockSpec((1, D), lambda j: (0, 0)), pl.BlockSpec((1, LANES), lambda j: (0, 0)),
                  pl.BlockSpec(memory_space=pl.ANY)],
        out_specs=[pl.BlockSpec((tb, 4 * D), lambda j: (rev(j), 0)),
                   pl.BlockSpec((1, D), lambda j: (0, 0)), pl.BlockSpec((1, LANES), lambda j: (0, 0))],
        out_shape=[jax.ShapeDtypeStruct(d_proj.shape, d_proj.dtype), jax.ShapeDtypeStruct((1, D), F32),
                   jax.ShapeDtypeStruct((1, LANES), F32)],
        input_output_aliases={5: 0},
        scratch_shapes=[pltpu.VMEM((N_HEADS_A, LANES, LANES), F32)],
        compiler_params=_params(("arbitrary",)),
    )(proj, states, d_out, lb, gn, d_proj)


def _ssd_in_specs(tb, tmap):
    return [pl.BlockSpec((tb, 512), lambda g, j: (tmap(j), g)),
            pl.BlockSpec((tb, LANES), lambda g, j: (tmap(j), 16 + g)),
            pl.BlockSpec((tb, LANES), lambda g, j: (tmap(j), 20 + g)),
            pl.BlockSpec((tb, LANES), lambda g, j: (tmap(j), COL_DT // LANES)),
            pl.BlockSpec((tb, 512), lambda g, j: (tmap(j), COL_Z // 512 + g))]


def ssd_forward(xc, proj, dtb, alog, dsk, nw):
    t = proj.shape[0]
    tb = _time_block(t, SSD_TIME_BLOCK)
    nb = t // tb

    def body(x_ref, b_ref, c_ref, dt_ref, z_ref, dtb_ref, alog_ref, dsk_ref, nw_ref, o_ref, st_ref, state):
        @pl.when(pl.program_id(1) == 0)
        def _():
            state[...] = jnp.zeros_like(state)

        st = state[...]
        st_ref[...] = st
        out, st_new = ssd_block(x_ref[...], b_ref[...], c_ref[...], dt_ref[...], z_ref[...], st,
                                dtb_ref[...], alog_ref[...], dsk_ref[...], nw_ref[...], ssd_consts(pl.program_id(0)))
        o_ref[...] = out.astype(o_ref.dtype)
        state[...] = st_new

    vec = pl.BlockSpec((1, 512), lambda g, j: (0, g))
    heads = pl.BlockSpec((1, LANES), lambda g, j: (0, 0))
    return pl.pallas_call(
        body, name="ssd_forward", grid=(N_GROUPS_B, nb),
        in_specs=_ssd_in_specs(tb, lambda j: j) + [heads, vec, vec, vec],
        out_specs=[pl.BlockSpec((tb, 512), lambda g, j: (j, g)),
                   pl.BlockSpec((None, None, LANES, 512), lambda g, j: (j, g, 0, 0))],
        out_shape=[jax.ShapeDtypeStruct((t, B_INNER), BF16),
                   jax.ShapeDtypeStruct((nb, N_GROUPS_B, LANES, 512), F32)],
        scratch_shapes=[pltpu.VMEM((LANES, 512), F32)],
        compiler_params=_params(("arbitrary", "arbitrary")),
    )(xc, xc, xc, proj, proj, dtb, alog, dsk, nw)


def ssd_backward(xc, proj, states, d_out, dtb, alog, dsk, nw, d_proj):
    t = proj.shape[0]
    tb = _time_block(t, SSD_TIME_BLOCK)
    nb = t // tb
    rev = lambda j: nb - 1 - j

    def body(x_ref, b_ref, c_ref, dt_ref, z_ref, st_ref, do_ref, dtb_ref, alog_ref, dsk_ref, nw_ref, _,
             dx_ref, db_ref, dc_ref, ddt_ref, dz_ref, ddtb_ref, dalog_ref, ddsk_ref, dnw_ref, d_state):
        accs = (ddtb_ref, dalog_ref, ddsk_ref, dnw_ref)

        @pl.when(pl.program_id(1) == 0)
        def _():
            d_state[...] = jnp.zeros_like(d_state)
            for ref in accs:
                ref[...] = jnp.zeros_like(ref)

        cs = ssd_consts(pl.program_id(0))
        fn = lambda *a: ssd_block(*a, cs)
        _, vjp = jax.vjp(fn, x_ref[...], b_ref[...], c_ref[...], dt_ref[...], z_ref[...], st_ref[...],
                         dtb_ref[...], alog_ref[...], dsk_ref[...], nw_ref[...])
        dx, db, dc, ddt, dz, dst, *dpar = vjp((do_ref[...], d_state[...]))
        dx_ref[...] = dx
        db_ref[...] = db
        dc_ref[...] = dc
        ddt_ref[...] = ddt
        dz_ref[...] = dz.astype(dz_ref.dtype)
        d_state[...] = dst
        for ref, val in zip(accs, dpar, strict=True):
            ref[...] += val

    vec = pl.BlockSpec((1, 512), lambda g, j: (0, g))
    heads = pl.BlockSpec((1, LANES), lambda g, j: (0, 0))
    acc = pl.BlockSpec((None, 1, 512), lambda g, j: (g, 0, 0))
    acc_heads = pl.BlockSpec((None, 1, LANES), lambda g, j: (g, 0, 0))
    return pl.pallas_call(
        body, name="ssd_backward", grid=(N_GROUPS_B, nb),
        in_specs=_ssd_in_specs(tb, rev)
        + [pl.BlockSpec((None, None, LANES, 512), lambda g, j: (rev(j), g, 0, 0)),
           pl.BlockSpec((tb, 512), lambda g, j: (rev(j), g))] + [heads, vec, vec, vec] + [pl.BlockSpec(memory_space=pl.ANY)],
        out_specs=[pl.BlockSpec((tb, 512), lambda g, j: (rev(j), g)),
                   pl.BlockSpec((tb, LANES), lambda g, j: (rev(j), g)),
                   pl.BlockSpec((tb, LANES), lambda g, j: (rev(j), g)),
                   pl.BlockSpec((None, tb, LANES), lambda g, j: (g, rev(j), 0)),
                   pl.BlockSpec((tb, 512), lambda g, j: (rev(j), COL_Z // 512 + g)), acc_heads, acc, acc, acc],
        out_shape=[jax.ShapeDtypeStruct((t, B_INNER), F32), jax.ShapeDtypeStruct((t, 512), F32),
                   jax.ShapeDtypeStruct((t, 512), F32), jax.ShapeDtypeStruct((N_GROUPS_B, t, LANES), F32),
                   jax.ShapeDtypeStruct(d_proj.shape, d_proj.dtype)]
        + [jax.ShapeDtypeStruct((N_GROUPS_B, 1, LANES), F32)] + [jax.ShapeDtypeStruct((N_GROUPS_B, 1, 512), F32)] * 3,
        input_output_aliases={11: 4},
        scratch_shapes=[pltpu.VMEM((LANES, 512), F32)],
        compiler_params=_params(("arbitrary", "arbitrary")),
    )(xc, xc, xc, proj, proj, states, d_out, dtb, alog, dsk, nw, d_proj)


CONV_HALO = 8


def _shift_down(halo_then_tile, s, tm):
    if s == 0:
        return halo_then_tile[CONV_HALO:CONV_HALO + tm]
    return pltpu.roll(halo_then_tile, s, 0)[CONV_HALO:CONV_HALO + tm]


def _conv_pre(cur, prev, w, b, tm):
    stacked = jnp.concatenate([prev, cur], axis=0)
    taps = [_shift_down(stacked, 3 - j, tm) for j in range(4)]
    pre = b + taps[0] * w[0:1] + taps[1] * w[1:2] + taps[2] * w[2:3] + taps[3] * w[3:4]
    return pre, taps


def _conv_specs(t, tm):
    per = tm // CONV_HALO
    cur = pl.BlockSpec((tm, CONV_DIM), lambda i: (i, COL_XBC // CONV_DIM))
    prev = pl.BlockSpec((CONV_HALO, CONV_DIM), lambda i: (jnp.maximum(i * per - 1, 0), COL_XBC // CONV_DIM))
    return cur, prev


def conv_forward(proj, w, b):
    t = proj.shape[0]
    tm = _pick(t, (256, 128, 64))

    def body(cur_ref, prev_ref, w_ref, b_ref, o_ref):
        prev = jnp.where(pl.program_id(0) == 0, 0.0, prev_ref[...])
        pre, _ = _conv_pre(cur_ref[...], prev, w_ref[...], b_ref[...], tm)
        o_ref[...] = silu(pre)

    cur, prev = _conv_specs(t, tm)
    return pl.pallas_call(
        body, name="conv_forward", grid=(t // tm,),
        in_specs=[cur, prev, pl.BlockSpec((4, CONV_DIM), lambda i: (0, 0)), pl.BlockSpec((1, CONV_DIM), lambda i: (0, 0))],
        out_specs=pl.BlockSpec((tm, CONV_DIM), lambda i: (i, 0)),
        out_shape=jax.ShapeDtypeStruct((t, CONV_DIM), F32),
        compiler_params=_params(("arbitrary",)),
    )(proj, proj, w, b)


def conv_backward(proj, dx, db_, dc_, w, b, d_proj):
    t = proj.shape[0]
    tm = _pick(t, (256, 128, 64))
    per = tm // CONV_HALO
    nt = t // tm
    rev = lambda i: nt - 1 - i

    def body(cur_ref, prev_ref, dx_ref, dbm_ref, dcm_ref, w_ref, b_ref, _, o_ref, dw_ref, dbias_ref, later):
        @pl.when(pl.program_id(0) == 0)
        def _():
            dw_ref[...] = jnp.zeros_like(dw_ref)
            dbias_ref[...] = jnp.zeros_like(dbias_ref)
            later[...] = jnp.zeros_like(later)

        first_tile = pl.program_id(0) == nt - 1
        for lo, hi, src in ((0, B_INNER, dx_ref), (B_INNER, B_INNER + 512, dbm_ref), (B_INNER + 512, CONV_DIM, dcm_ref)):
            cols = slice(lo, hi)
            prev = jnp.where(first_tile, 0.0, prev_ref[:, cols])
            w_ = w_ref[:, cols]
            pre, taps = _conv_pre(cur_ref[:, cols], prev, w_, b_ref[:, cols], tm)
            sg = sigmoid(pre)
            dpre = src[...] * (sg * (1.0 + pre * (1.0 - sg)))
            dbias_ref[:, cols] += jnp.sum(dpre, axis=0, keepdims=True)
            for j in range(4):
                dw_ref[j:j + 1, cols] += jnp.sum(dpre * taps[j], axis=0, keepdims=True)
            stacked = jnp.concatenate([dpre, later[:, cols]], axis=0)
            acc = dpre * w_[3:4]
            for j in range(3):
                acc = acc + pltpu.roll(stacked, tm + CONV_HALO - (3 - j), 0)[0:tm] * w_[j:j + 1]
            o_ref[:, cols] = acc.astype(o_ref.dtype)
            later[:, cols] = dpre[0:CONV_HALO]

    row = lambda w_: pl.BlockSpec((tm, w_), lambda i: (rev(i), 0))
    whole = lambda r: pl.BlockSpec((r, CONV_DIM), lambda i: (0, 0))
    return pl.pallas_call(
        body, name="conv_backward", grid=(nt,),
        in_specs=[pl.BlockSpec((tm, CONV_DIM), lambda i: (rev(i), COL_XBC // CONV_DIM)),
                  pl.BlockSpec((CONV_HALO, CONV_DIM), lambda i: (jnp.maximum(rev(i) * per - 1, 0), COL_XBC // CONV_DIM)),
                  row(B_INNER), row(512), row(512), whole(4), whole(1), pl.BlockSpec(memory_space=pl.ANY)],
        out_specs=[pl.BlockSpec((tm, CONV_DIM), lambda i: (rev(i), COL_XBC // CONV_DIM)), whole(4), whole(1)],
        out_shape=[jax.ShapeDtypeStruct(d_proj.shape, d_proj.dtype), jax.ShapeDtypeStruct((4, CONV_DIM), F32),
                   jax.ShapeDtypeStruct((1, CONV_DIM), F32)],
        input_output_aliases={7: 0},
        scratch_shapes=[pltpu.VMEM((CONV_HALO, CONV_DIM), F32)],
        compiler_params=_params(("arbitrary",)),
    )(proj, proj, dx, db_, dc_, w, b, d_proj)


def stage_modulate(x, sc, sh):
    return _ln(x) * (1.0 + sc) + sh


def stage_merge(ga, gb, ya, yb):
    return sigmoid(ga) * ya + sigmoid(gb) * yb


def stage_post_mixer(x, h, g1, ln_g, ln_b, sc2, sh2):
    x1 = _ln(ALPHA * x + g1 * h) * ln_g + ln_b
    return x1, _ln(x1) * (1.0 + sc2) + sh2


def stage_swiglu(a, b):
    return silu(a) * b


def gate_up(ab):
    w = FFN_SHARD
    return (jnp.concatenate([ab[:, 2 * w * k:2 * w * k + w] for k in range(4)], axis=1),
            jnp.concatenate([ab[:, 2 * w * k + w:2 * w * (k + 1)] for k in range(4)], axis=1))


def per_chip(gate, up):
    w = FFN_SHARD
    return jnp.concatenate([part[:, w * k:w * (k + 1)] for k in range(4) for part in (gate, up)], axis=1)


def stage_loss(x1, hf, tgt, g2, ln_g, ln_b):
    x2 = _ln(ALPHA * x1 + g2 * hf) * ln_g + ln_b
    return 0.5 * jnp.sum(jnp.mean(jnp.square(x2 - tgt), axis=-1, keepdims=True), axis=0, keepdims=True)


def local_step(x, tgt, mod, wts, small, early=None, mid=None, late=None, last=None):
    sh1, sc1, g1, sh2, sc2, g2 = mod
    lb, gn, conv_w, conv_b, dtb, alog, dsk, nw, ln1_g, ln1_b, ln2_g, ln2_b = small
    vec = (1, D)

    (u1,) = rowwise("modulate1", lambda r, c: ((stage_modulate(r[0], *c),), ()), [_full(x)], [sc1, sh1], [(D, BF16)])
    w_in = wts.input_projection(u1)
    proj = matmul(u1, w_in, "nn", F32, "in_proj")
    ya_in, st_a = hgrn_forward(proj, lb, gn + wts.start_rest(proj)[0:1])
    xc = conv_forward(proj, conv_w, conv_b)
    w_a, w_b, w_o, w_gu, w_d = wts.rest(xc)
    yb_in, st_b = ssd_forward(xc, proj, dtb, alog, dsk, nw)
    ya = matmul(ya_in, w_a, "nn", F32, "branch_a")
    yb = matmul(yb_in, w_b, "nn", F32, "branch_b")
    gate_rows = [(proj, D, COL_GA // D), (proj, D, COL_GB // D), _full(ya), _full(yb)]
    (merged,) = rowwise("merge", lambda r, c: ((stage_merge(*r),), ()), gate_rows, [], [(D, BF16)])
    h = matmul(merged, w_o, "nn", F32, "out_proj")
    post_consts = [g1, ln1_g, ln1_b, sc2, sh2]
    x1, u2 = rowwise("post_mixer", lambda r, c: (stage_post_mixer(*r, *c), ()), [_full(x), _full(h)], post_consts,
                     [(D, F32), (D, BF16)])
    ab = matmul(u2, w_gu, "nt", F32, "ffn_in")
    (p,) = rowwise("swiglu", lambda r, c: ((stage_swiglu(*gate_up(r[0])),), ()), [_full(ab)], [], [(D_FF, BF16)],
                   tm_max=256)
    hf = matmul(p, w_d, "nn", F32, "ffn_out")

    def loss_bwd(r, c):
        loss, vjp = jax.vjp(stage_loss, *r, *c)
        dx1, dhf, _, dg2, dlg, dlb_ = vjp(jnp.ones((1, 1), F32))
        return (dx1, dhf), (loss, dg2, dlg, dlb_)

    dx1, dhf, loss, dg2, dln2_g, dln2_b = rowwise(
        "loss_backward", loss_bwd, [_full(x1), _full(hf), _full(tgt)], [g2, ln2_g, ln2_b],
        [(D, F32), (D, BF16)], [(1, 1), vec, vec, vec])
    dp = matmul(dhf, w_d, "nt", F32, "ffn_out_dx")
    dw_d = matmul(p, dhf, "tn", F32, "ffn_out_dw")

    def swiglu_bwd(r, c):
        _, vjp = jax.vjp(stage_swiglu, *gate_up(r[0]))
        return (per_chip(*vjp(r[1])),), ()

    (dab,) = rowwise("swiglu_backward", swiglu_bwd, [_full(ab), _full(dp)], [], [(2 * D_FF, BF16)], tm_max=256)
    du2 = matmul(dab, w_gu, "nn", F32, "ffn_in_dx")
    dw_gu = matmul(dab, u2, "tn", F32, "ffn_in_dw")

    def post_bwd(r, c):
        _, vjp = jax.vjp(stage_post_mixer, r[0], r[1], *c)
        dx, dh, *dc = vjp((r[2], r[3]))
        return (dx, dh), tuple(dc)

    dx_a, dh, dg1, dln1_g, dln1_b, dsc2, dsh2 = rowwise(
        "post_mixer_backward", post_bwd, [_full(x), _full(h), _full(dx1), _full(du2)], post_consts,
        [(D, F32), (D, BF16)], [vec] * 5)
    dmerged = matmul(dh, w_o, "nt", F32, "out_proj_dx")
    dw_o = matmul(merged, dh, "tn", F32, "out_proj_dw")

    def merge_bwd(r, c):
        _, vjp = jax.vjp(stage_merge, *r[:4])
        dga, dgb, dya, dyb = vjp(r[4])
        return (jnp.concatenate([dga, dgb], axis=1), dya, dyb), ()

    dproj, dya, dyb = rowwise("merge_backward", merge_bwd, gate_rows + [_full(dmerged)], [],
                              [(2 * D, BF16), (D, BF16), (D, BF16)], new_wide=(IN_PAD, COL_GA // (2 * D)))
    dya_in = matmul(dya, w_a, "nt", F32, "branch_a_dx")
    dw_a = matmul(ya_in, dya, "tn", F32, "branch_a_dw")
    dyb_in = matmul(dyb, w_b, "nt", F32, "branch_b_dx")
    dw_b = matmul(yb_in, dyb, "tn", F32, "branch_b_dw")
    gn_after = gn if early is None else gn + early((dw_a, dw_b, dw_o, dw_gu, dw_d))[0:1]
    dproj, dlb, dgn = hgrn_backward(proj, st_a, dya_in, lb, gn_after, dproj)
    dtb_after = dtb if mid is None else dtb + mid(dlb)[0:1, 0:1]
    dxs, dbm, dcm, ddt, dproj, ddtb, dalog, ddsk, dnw = ssd_backward(xc, proj, st_b, dyb_in, dtb_after, alog, dsk, nw, dproj)
    dproj, dconv_w, dconv_b = conv_backward(proj, dxs, dbm, dcm, conv_w, conv_b, dproj)
    if late is not None:
        late(dconv_b)
    t = x.shape[0]
    tail = jnp.concatenate([jnp.sum(ddt, axis=0).astype(BF16), jnp.zeros((t, IN_PAD - COL_DT - LANES), BF16)], axis=1)
    dproj = lax.dynamic_update_slice(dproj, tail, (0, COL_DT))
    if last is None:
        dw_in, started = matmul(u1, dproj, "tn", F32, "in_proj_dw"), None
    else:
        dw_in, started = None, last(u1, dproj)
    du1 = matmul(dproj, w_in, "nt", F32, "in_proj_dx", after=started)

    def mod_bwd(r, c):
        _, vjp = jax.vjp(stage_modulate, r[0], *c)
        dx, dsc, dsh = vjp(r[1])
        return (dx + r[2],), (dsc, dsh)

    grad_x, dsc1, dsh1 = rowwise("modulate1_backward", mod_bwd, [_full(x), _full(du1), _full(dx_a)], [sc1, sh1],
                                 [(D, F32)], [vec, vec])
    d_mod = (dsh1, dsc1, dg1, dsh2, dsc2, dg2)
    d_wts = (dw_in, dw_a, dw_b, dw_o, dw_gu, dw_d)
    d_small = (dlb, dgn, dconv_w, dconv_b, jnp.sum(ddtb, axis=0),
               dalog.reshape(1, B_INNER), ddsk.reshape(1, B_INNER), dnw.reshape(1, B_INNER),
               dln1_g, dln1_b, dln2_g, dln2_b)
    return loss, grad_x, d_mod, d_wts, d_small


HBM = pl.BlockSpec(memory_space=pltpu.HBM)
SEM = pl.BlockSpec(memory_space=pltpu.SEMAPHORE)
DATAFLOW = pltpu.SideEffectType.DATAFLOW_SIDE_EFFECTING


def _place():
    return lax.axis_index("x"), lax.axis_index("y"), lax.axis_index("c")


def _other_chips(x, y):
    return [(1 - x, y), (x, 1 - y), (1 - x, 1 - y)]


def _remote(src, dst, send_sem, recv_sem, device):
    return pltpu.make_async_remote_copy(src_ref=src, dst_ref=dst, send_sem=send_sem, recv_sem=recv_sem,
                                        device_id=device, device_id_type=MESH)


def gather_rows(v, name):
    n = v.shape[1]

    def body(v_ref, out_ref, send_sems, recv_sems, local_sem):
        x, y, c = _place()
        mine = pltpu.make_async_copy(v_ref, out_ref.at[4 * x + 2 * y + c], local_sem)
        mine.start()
        sends, recvs = [], []
        for m in range(1, 8):
            px = 1 - x if m & 4 else x
            py = 1 - y if m & 2 else y
            pc = 1 - c if m & 1 else c
            sends.append(_remote(v_ref, out_ref.at[4 * x + 2 * y + c], send_sems.at[m - 1], recv_sems.at[m - 1], (px, py, pc)))
            recvs.append(_remote(v_ref, out_ref.at[4 * px + 2 * py + pc], send_sems.at[m - 1], recv_sems.at[m - 1], (px, py, pc)))
        for cp in sends:
            cp.start()
        for cp in recvs:
            cp.wait_recv()
        for cp in sends:
            cp.wait_send()
        mine.wait()

    return pl.pallas_call(
        body, name=name, in_specs=[HBM], out_specs=HBM,
        out_shape=jax.ShapeDtypeStruct((8, 1, n), v.dtype),
        scratch_shapes=[pltpu.SemaphoreType.DMA((7,)), pltpu.SemaphoreType.DMA((7,)), pltpu.SemaphoreType.DMA],
    )(v)


def exchange_rows(part, name):
    w = part.shape[2]

    def body(p_ref, out_ref, send_sems, recv_sems, local_sem):
        x, y, c = _place()
        k = 2 * x + y
        mine = pltpu.make_async_copy(p_ref.at[4 * x + 2 * y + c], out_ref.at[k], local_sem)
        mine.start()
        sends, recvs = [], []
        for j, (px, py) in enumerate(_other_chips(x, y)):
            sends.append(_remote(p_ref.at[4 * px + 2 * py + c], out_ref.at[k], send_sems.at[j], recv_sems.at[j], (px, py, c)))
            recvs.append(_remote(p_ref.at[4 * px + 2 * py + c], out_ref.at[2 * px + py], send_sems.at[j], recv_sems.at[j], (px, py, c)))
        for cp in sends:
            cp.start()
        for cp in recvs:
            cp.wait_recv()
        for cp in sends:
            cp.wait_send()
        mine.wait()

    return pl.pallas_call(
        body, name=name, in_specs=[HBM], out_specs=HBM,
        out_shape=jax.ShapeDtypeStruct((4, 1, w), part.dtype),
        scratch_shapes=[pltpu.SemaphoreType.DMA((3,)), pltpu.SemaphoreType.DMA((3,)), pltpu.SemaphoreType.DMA],
    )(part)


def _half_of_slot(ref, rows, px, py, pc):
    return ref.at[2 * px + py, pl.ds(pc * (rows // 2), rows // 2), :]


def gather_start(shards, after, tag):
    n = len(shards)

    def body(*refs):
        w_refs, land_refs = refs[:n], refs[n:2 * n]
        send_sems, recv_sems = refs[2 * n + 1], refs[2 * n + 2]
        token = refs[-1]
        x, y, c = _place()
        for i in range(n):
            rows = shards[i].shape[0]
            for j, (px, py) in enumerate(_other_chips(x, y)):
                _remote(w_refs[i].at[pl.ds(c * (rows // 2), rows // 2), :], _half_of_slot(land_refs[i], rows, x, y, c),
                        send_sems.at[j * n + i], recv_sems.at[j * n + i], (px, py, c)).start()
        token[...] = jnp.zeros_like(token)

    hbm = lambda a: pltpu.with_memory_space_constraint(a, pltpu.HBM)
    lands = [lax.empty((4,) + s.shape, s.dtype) for s in shards]
    dma = pltpu.SemaphoreType.DMA
    return pl.pallas_call(
        body, name="gather_start_" + tag,
        out_shape=(dma((3 * n,)), dma((3 * n,)),
                   *[pltpu.HBM(a.shape, a.dtype) for a in list(shards) + lands], jax.ShapeDtypeStruct((8, LANES), F32)),
        in_specs=[HBM] * (2 * n) + [pl.BlockSpec(memory_space=pl.ANY)],
        out_specs=(SEM, SEM, *[HBM] * (2 * n), pl.BlockSpec(memory_space=pltpu.VMEM)),
        input_output_aliases={i: 2 + i for i in range(2 * n)},
        compiler_params=pltpu.CompilerParams(has_side_effects=DATAFLOW),
    )(*[hbm(a) for a in list(shards) + lands], after)


def gather_wait(send_sems, recv_sems, shards, lands, after, tag):
    n = len(shards)

    def body(*refs):
        w_refs, land_refs = refs[:n], refs[n:2 * n]
        send_ref, recv_ref = refs[2 * n], refs[2 * n + 1]
        x, y, c = _place()
        for i in range(n):
            rows = shards[i].shape[0]
            for j, (px, py) in enumerate(_other_chips(x, y)):
                cp = _remote(w_refs[i].at[pl.ds(c * (rows // 2), rows // 2), :], _half_of_slot(land_refs[i], rows, px, py, c),
                             send_ref.at[j * n + i], recv_ref.at[j * n + i], (px, py, c))
                cp.wait_send()
                cp.wait_recv()

    out = pl.pallas_call(
        body, name="gather_wait_" + tag,
        out_shape=tuple(pltpu.HBM(a.shape, a.dtype) for a in list(shards) + list(lands)),
        in_specs=[HBM] * (2 * n) + [SEM, SEM, pl.BlockSpec(memory_space=pl.ANY)], out_specs=tuple([HBM] * (2 * n)),
        input_output_aliases={i: i for i in range(2 * n)},
        compiler_params=pltpu.CompilerParams(has_side_effects=DATAFLOW),
    )(*shards, *lands, send_sems, recv_sems, after)
    return list(out[:n]), list(out[n:])


def forward_start(lands, tag):
    n = len(lands)

    def body(*refs):
        land_refs = refs[:n]
        send_sems, recv_sems = refs[n], refs[n + 1]
        token = refs[-1]
        x, y, c = _place()
        for i in range(n):
            rows = lands[i].shape[1]
            for j, (px, py) in enumerate(_other_chips(x, y)):
                mine = _half_of_slot(land_refs[i], rows, px, py, c)
                _remote(mine, mine, send_sems.at[j * n + i], recv_sems.at[j * n + i], (x, y, 1 - c)).start()
        token[...] = jnp.zeros_like(token)

    dma = pltpu.SemaphoreType.DMA
    return pl.pallas_call(
        body, name="forward_start_" + tag,
        out_shape=(dma((3 * n,)), dma((3 * n,)), *[pltpu.HBM(a.shape, a.dtype) for a in lands],
                   jax.ShapeDtypeStruct((8, LANES), F32)),
        in_specs=[HBM] * n, out_specs=(SEM, SEM, *[HBM] * n, pl.BlockSpec(memory_space=pltpu.VMEM)),
        input_output_aliases={i: 2 + i for i in range(n)},
        compiler_params=pltpu.CompilerParams(has_side_effects=DATAFLOW),
    )(*lands)


def forward_wait(started, after, tag):
    send_sems, recv_sems, *rest = started
    lands = rest[:-1]
    n = len(lands)

    def body(*refs):
        land_refs = refs[:n]
        send_ref, recv_ref = refs[n], refs[n + 1]
        x, y, c = _place()
        for i in range(n):
            rows = lands[i].shape[1]
            for j, (px, py) in enumerate(_other_chips(x, y)):
                cp = _remote(_half_of_slot(land_refs[i], rows, px, py, c), _half_of_slot(land_refs[i], rows, px, py, 1 - c),
                             send_ref.at[j * n + i], recv_ref.at[j * n + i], (x, y, 1 - c))
                cp.wait_send()
                cp.wait_recv()

    out = pl.pallas_call(
        body, name="forward_wait_" + tag,
        out_shape=tuple(pltpu.HBM(a.shape, a.dtype) for a in lands),
        in_specs=[HBM] * n + [SEM, SEM, pl.BlockSpec(memory_space=pl.ANY)], out_specs=tuple([HBM] * n),
        input_output_aliases={i: i for i in range(n)},
        compiler_params=pltpu.CompilerParams(has_side_effects=DATAFLOW),
    )(*lands, send_sems, recv_sems, after)
    return list(out)


def pair_start(slabs, tag):
    n = len(slabs)

    def body(*refs):
        g_refs, land_refs = refs[:n], refs[n:2 * n]
        send_sems, recv_sems = refs[2 * n], refs[2 * n + 1]
        token = refs[-1]
        x, y, c = _place()
        for i in range(n):
            hr = slabs[i].shape[1] // 2
            _remote(g_refs[i].at[:, pl.ds((1 - c) * hr, hr), :], land_refs[i], send_sems.at[i], recv_sems.at[i],
                    (x, y, 1 - c)).start()
        token[...] = jnp.zeros_like(token)

    hbm = lambda a: pltpu.with_memory_space_constraint(a, pltpu.HBM)
    lands = [lax.empty((4, s.shape[1] // 2, s.shape[2]), s.dtype) for s in slabs]
    dma = pltpu.SemaphoreType.DMA
    return pl.pallas_call(
        body, name="pair_start_" + tag,
        out_shape=(dma((n,)), dma((n,)), *[pltpu.HBM(a.shape, a.dtype) for a in list(slabs) + lands],
                   jax.ShapeDtypeStruct((8, LANES), F32)),
        in_specs=[HBM] * (2 * n), out_specs=(SEM, SEM, *[HBM] * (2 * n), pl.BlockSpec(memory_space=pltpu.VMEM)),
        input_output_aliases={i: 2 + i for i in range(2 * n)},
        compiler_params=pltpu.CompilerParams(has_side_effects=DATAFLOW),
    )(*[hbm(a) for a in list(slabs) + lands])


def pair_wait(started, after, tag):
    send_sems, recv_sems, *rest = started
    n = (len(rest) - 1) // 2
    slabs, lands = rest[:n], rest[n:2 * n]

    def body(*refs):
        g_refs, land_refs = refs[:n], refs[n:2 * n]
        send_ref, recv_ref = refs[2 * n], refs[2 * n + 1]
        x, y, c = _place()
        for i in range(n):
            hr = slabs[i].shape[1] // 2
            cp = _remote(g_refs[i].at[:, pl.ds((1 - c) * hr, hr), :], land_refs[i], send_ref.at[i], recv_ref.at[i], (x, y, 1 - c))
            cp.wait_send()
            cp.wait_recv()

    out = pl.pallas_call(
        body, name="pair_wait_" + tag,
        out_shape=tuple(pltpu.HBM(a.shape, a.dtype) for a in list(slabs) + list(lands)),
        in_specs=[HBM] * (2 * n) + [SEM, SEM, pl.BlockSpec(memory_space=pl.ANY)], out_specs=tuple([HBM] * (2 * n)),
        input_output_aliases={i: i for i in range(2 * n)},
        compiler_params=pltpu.CompilerParams(has_side_effects=DATAFLOW),
    )(*slabs, *lands, send_sems, recv_sems, after)
    return list(out[:n]), list(out[n:])


def _tile2(rows, cols):
    fits = lambda r, c: r * c * 4 <= BLOCK_BYTES
    if fits(rows, cols):
        return rows, cols
    tiles = [(r, cols) for r in (1024, 512, 256, 128, 64) if rows % r == 0 and fits(r, cols)]
    tiles += [(rows, cols // k) for k in (2, 3, 4, 6, 8, 12, 16) if cols % (k * LANES) == 0 and fits(rows, cols // k)]
    return max(tiles, key=lambda t: t[0] * t[1])


def pair_add(g, p, c, name):
    _, hr, cols = p.shape
    tm, tc = _tile2(hr, cols)
    per = hr // tm

    def body(c_ref, g_ref, p_ref, o_ref):
        o_ref[...] = (g_ref[...] + p_ref[...]).astype(o_ref.dtype)

    return pl.pallas_call(
        body, name=name,
        grid_spec=pltpu.PrefetchScalarGridSpec(
            num_scalar_prefetch=1, grid=(4, per, cols // tc),
            in_specs=[pl.BlockSpec((None, tm, tc), lambda k, i, j, c_ref: (k, c_ref[0] * per + i, j)),
                      pl.BlockSpec((None, tm, tc), lambda k, i, j, c_ref: (k, i, j))],
            out_specs=pl.BlockSpec((None, tm, tc), lambda k, i, j, c_ref: (k, i, j))),
        out_shape=jax.ShapeDtypeStruct((4, hr, cols), BF16),
        compiler_params=_params(("arbitrary", "arbitrary", "arbitrary")),
    )(c.reshape(1).astype(jnp.int32), g, p)


def scatter_start(sums, tag):
    n = len(sums)

    def body(*refs):
        s_refs, land_refs = refs[:n], refs[n:2 * n]
        send_sems, recv_sems = refs[2 * n], refs[2 * n + 1]
        token = refs[-1]
        x, y, c = _place()
        k = 2 * x + y
        for i in range(n):
            for j, (px, py) in enumerate(_other_chips(x, y)):
                _remote(s_refs[i].at[2 * px + py], land_refs[i].at[k], send_sems.at[j * n + i], recv_sems.at[j * n + i],
                        (px, py, c)).start()
        token[...] = jnp.zeros_like(token)

    hbm = lambda a: pltpu.with_memory_space_constraint(a, pltpu.HBM)
    return pl.pallas_call(
        body, name="scatter_start_" + tag,
        out_shape=(pltpu.SemaphoreType.DMA((3 * n,)), pltpu.SemaphoreType.DMA((3 * n,)),
                   *[pltpu.HBM(s.shape, s.dtype) for s in sums], *[pltpu.HBM(s.shape, s.dtype) for s in sums],
                   jax.ShapeDtypeStruct((8, LANES), F32)),
        in_specs=[HBM] * (2 * n), out_specs=(SEM, SEM, *[HBM] * (2 * n), pl.BlockSpec(memory_space=pltpu.VMEM)),
        input_output_aliases={i: 2 + i for i in range(2 * n)},
        compiler_params=pltpu.CompilerParams(has_side_effects=DATAFLOW),
    )(*[hbm(s) for s in sums], *[hbm(lax.empty(s.shape, s.dtype)) for s in sums])


def scatter_wait(started, after, tag):
    send_sems, recv_sems, *rest = started
    n = (len(rest) - 1) // 2
    sums, lands = rest[:n], rest[n:2 * n]

    def body(*refs):
        s_refs, land_refs = refs[:n], refs[n:2 * n]
        send_ref, recv_ref = refs[2 * n], refs[2 * n + 1]
        x, y, c = _place()
        for i in range(n):
            for j, (px, py) in enumerate(_other_chips(x, y)):
                cp = _remote(s_refs[i].at[2 * px + py], land_refs[i].at[2 * px + py], send_ref.at[j * n + i],
                             recv_ref.at[j * n + i], (px, py, c))
                cp.wait_send()
                cp.wait_recv()

    out = pl.pallas_call(
        body, name="scatter_wait_" + tag,
        out_shape=tuple(pltpu.HBM(s.shape, s.dtype) for s in sums + lands),
        in_specs=[HBM] * (2 * n) + [SEM, SEM, pl.BlockSpec(memory_space=pl.ANY)], out_specs=tuple([HBM] * (2 * n)),
        input_output_aliases={i: i for i in range(2 * n)},
        compiler_params=pltpu.CompilerParams(has_side_effects=DATAFLOW),
    )(*sums, *lands, send_sems, recv_sems, after)
    return list(out[:n]), list(out[n:])


def sum_chips(landed, own, chip, core, name):
    _, hr, cols = landed.shape
    tm, tc = _tile2(hr, cols)
    per = hr // tm

    def body(idx_ref, l0, l1, l2, l3, own_ref, o_ref):
        mine = own_ref[...].astype(F32)
        v = [jnp.where(idx_ref[0] == k, mine, ref[...].astype(F32)) for k, ref in enumerate((l0, l1, l2, l3))]
        o_ref[...] = ((v[0] + v[1]) + v[2]) + v[3]

    slot = lambda k: pl.BlockSpec((None, tm, tc),
                                  lambda i, j, idx: (jnp.where(idx[0] == k, (k + 1) & 3, k), i, j))
    return pl.pallas_call(
        body, name=name,
        grid_spec=pltpu.PrefetchScalarGridSpec(
            num_scalar_prefetch=1, grid=(per, cols // tc),
            in_specs=[slot(0), slot(1), slot(2), slot(3),
                      pl.BlockSpec((None, tm, tc), lambda i, j, idx: (idx[0], i, j))],
            out_specs=pl.BlockSpec((tm, tc), lambda i, j, idx: (idx[1] * per + i, j))),
        out_shape=jax.ShapeDtypeStruct((2 * hr, cols), F32),
        compiler_params=_params(("arbitrary", "arbitrary")),
    )(jnp.stack([chip, core]).astype(jnp.int32), landed, landed, landed, landed, own)


def exchange_halves(bufs):
    n = len(bufs)

    def body(*refs):
        out_refs = refs[n:2 * n]
        send_sems, recv_sems = refs[2 * n:]
        x, y, c = _place()
        sends, recvs = [], []
        for i in range(n):
            hr = bufs[i].shape[0] // 2
            own = out_refs[i].at[pl.ds(c * hr, hr), :]
            other = out_refs[i].at[pl.ds((1 - c) * hr, hr), :]
            sends.append(_remote(own, own, send_sems.at[i], recv_sems.at[i], (x, y, 1 - c)))
            recvs.append(_remote(other, other, send_sems.at[i], recv_sems.at[i], (x, y, 1 - c)))
        for cp in sends:
            cp.start()
        for cp in recvs:
            cp.wait_recv()
        for cp in sends:
            cp.wait_send()

    return pl.pallas_call(
        body, name="exchange_halves", in_specs=[HBM] * n, out_specs=[HBM] * n,
        out_shape=[jax.ShapeDtypeStruct(b.shape, b.dtype) for b in bufs],
        input_output_aliases={i: i for i in range(n)},
        scratch_shapes=[pltpu.SemaphoreType.DMA((n,)), pltpu.SemaphoreType.DMA((n,))],
    )(*bufs)


def assemble_in_proj(landed, own, chip):
    rows, cols = 128, own.shape[1]

    def body(idx_ref, l0, l1, l2, l3, own_ref, o_ref):
        mine = own_ref[...]
        w = jnp.concatenate([jnp.where(idx_ref[0] == k, mine, ref[...]) for k, ref in enumerate((l0, l1, l2, l3))], axis=1)
        o_ref[...] = jnp.concatenate([w[:, :ORIG_Z], w[:, ORIG_GA:], w[:, ORIG_XBC:ORIG_DT], w[:, ORIG_Z:ORIG_XBC],
                                      w[:, ORIG_DT:ORIG_GA], jnp.zeros((rows, IN_PAD - IN_ORIG), w.dtype)], axis=1)

    slot = lambda k: pl.BlockSpec((None, rows, cols), lambda i, idx: (jnp.where(idx[0] == k, (k + 1) & 3, k), i, 0))
    return pl.pallas_call(
        body, name="assemble_in_proj",
        grid_spec=pltpu.PrefetchScalarGridSpec(
            num_scalar_prefetch=1, grid=(D // rows,),
            in_specs=[slot(0), slot(1), slot(2), slot(3), pl.BlockSpec((rows, cols), lambda i, idx: (i, 0))],
            out_specs=pl.BlockSpec((rows, IN_PAD), lambda i, idx: (i, 0))),
        out_shape=jax.ShapeDtypeStruct((D, IN_PAD), own.dtype),
        compiler_params=_params(("arbitrary",)),
    )(chip.reshape(1).astype(jnp.int32), landed, landed, landed, landed, own)


def matmul_rows_half(a, b, which, name, after=None):
    k, m = a.shape
    n = b.shape[1]
    hm = m // 2
    tn = _pick(n, (1920, 1024, 768, 512, 256, 128))
    tk = _pick(k, (2048, 1024, 512, 256, 128))
    nk = k // tk
    order = [] if after is None else [after]

    def body(idx_ref, a_ref, b_ref, *rest):
        o_ref, acc_ref = rest[len(order):]
        kk = pl.program_id(1)
        part = _bd(a_ref[...], b_ref[...], "tn")

        @pl.when(kk == 0)
        def _():
            acc_ref[...] = part

        @pl.when(kk > 0)
        def _():
            acc_ref[...] += part

        @pl.when(kk == nk - 1)
        def _():
            o_ref[...] = acc_ref[...]

    return pl.pallas_call(
        body, name=name,
        grid_spec=pltpu.PrefetchScalarGridSpec(
            num_scalar_prefetch=1, grid=(n // tn, nk),
            in_specs=[pl.BlockSpec((tk, hm), lambda j, kk, idx: (kk, idx[0])),
                      pl.BlockSpec((tk, tn), lambda j, kk, idx: (kk, j))] + [pl.BlockSpec(memory_space=pl.ANY) for _ in order],
            out_specs=pl.BlockSpec((hm, tn), lambda j, kk, idx: (0, j)),
            scratch_shapes=[pltpu.VMEM((hm, tn), F32)]),
        out_shape=jax.ShapeDtypeStruct((hm, n), F32),
        compiler_params=_params(("parallel", "arbitrary")),
    )(which.reshape(1).astype(jnp.int32), a, b, *order)


def rows_start(a, tag):
    def body(a_ref, land_ref, send_sem, recv_sem, a_thru, land_thru, token):
        x, y, c = _place()
        _remote(a_ref, land_ref, send_sem, recv_sem, (x, y, 1 - c)).start()
        token[...] = jnp.zeros_like(token)

    hbm = lambda v: pltpu.with_memory_space_constraint(v, pltpu.HBM)
    dma = pltpu.SemaphoreType.DMA
    return pl.pallas_call(
        body, name="rows_start_" + tag,
        out_shape=(dma(()), dma(()), pltpu.HBM(a.shape, a.dtype), pltpu.HBM(a.shape, a.dtype),
                   jax.ShapeDtypeStruct((8, LANES), F32)),
        in_specs=[HBM, HBM], out_specs=(SEM, SEM, HBM, HBM, pl.BlockSpec(memory_space=pltpu.VMEM)),
        input_output_aliases={0: 2, 1: 3},
        compiler_params=pltpu.CompilerParams(has_side_effects=DATAFLOW),
    )(hbm(a), hbm(lax.empty(a.shape, a.dtype)))


def rows_wait(started, after, tag):
    send_sem, recv_sem, a, land, _ = started

    def body(a_ref, land_ref, send_ref, recv_ref, after_ref, a_dead, got_ref):
        x, y, c = _place()
        cp = _remote(a_ref, land_ref, send_ref, recv_ref, (x, y, 1 - c))
        cp.wait_send()
        cp.wait_recv()

    return pl.pallas_call(
        body, name="rows_wait_" + tag,
        out_shape=(pltpu.HBM(a.shape, a.dtype), pltpu.HBM(a.shape, a.dtype)),
        in_specs=[HBM, HBM, SEM, SEM, pl.BlockSpec(memory_space=pl.ANY)], out_specs=(HBM, HBM),
        input_output_aliases={0: 0, 1: 1},
        compiler_params=pltpu.CompilerParams(has_side_effects=DATAFLOW),
    )(a, land, send_sem, recv_sem, after)[1]


def split_pair_add(own, received):
    cols = IN_ORIG // 4
    rows, hr = 128, own.shape[0]

    def body(own_ref, got_ref, o_ref):
        d = own_ref[...] + got_ref[...]
        w = jnp.concatenate([d[:, :COL_GA], d[:, COL_Z:COL_DT], d[:, COL_XBC:COL_Z], d[:, COL_DT:COL_DT + 32],
                             d[:, COL_GA:COL_XBC]], axis=1)
        for k in range(4):
            o_ref[k] = w[:, k * cols:(k + 1) * cols].astype(o_ref.dtype)

    half = pl.BlockSpec((rows, IN_PAD), lambda i: (i, 0))
    return pl.pallas_call(
        body, name="split_pair_add", grid=(hr // rows,), in_specs=[half, half],
        out_specs=pl.BlockSpec((4, rows, cols), lambda i: (0, i, 0)),
        out_shape=jax.ShapeDtypeStruct((4, hr, cols), BF16),
        compiler_params=_params(("arbitrary",)),
    )(own, received)


def ada_prepare(c_all, w_ada, hgrn_lb):
    def body(c_ref, w_ref, lb_ref, mod_ref, row_ref):
        mod_ref[...] = hdot(silu(c_ref[...]), w_ref[...])
        row_ref[...] = sigmoid(lb_ref[0:1, :] - lb_ref[1:2, :])

    return pl.pallas_call(
        body, name="ada_prepare",
        out_shape=[jax.ShapeDtypeStruct((8, w_ada.shape[1]), F32), jax.ShapeDtypeStruct((1, D), F32)],
        compiler_params=pltpu.CompilerParams(vmem_limit_bytes=VMEM_LIMIT),
    )(c_all, w_ada, hgrn_lb)


SMALL_SEGS = (("mod", 6 * D), ("lb", D), ("gnorm", LANES), ("conv_w", 4 * CONV_DIM), ("conv_b", CONV_DIM),
              ("dt_bias", LANES), ("a_log", B_INNER), ("d", B_INNER), ("ssm_norm", B_INNER),
              ("ln1_g", D), ("ln1_b", D), ("ln2_g", D), ("ln2_b", D), ("loss", LANES))
SMALL_PARAMS = ("b_ada", "hgrn_lb", "hgrn_gnorm", "ssm_conv_b", "ssm_dt_bias", "ssm_a_log", "ssm_d", "ssm_norm",
                "ln1_g", "ln1_b", "ln2_g", "ln2_b")


def finalize_small(g_all, c_all, dmod_cols, params, m, v):
    n_p = len(SMALL_PARAMS)
    offs, o = {}, 0
    for nm, width in SMALL_SEGS:
        offs[nm] = (o, width)
        o += width

    def body(*refs):
        g_ref, c_ref, dm_ref = refs[:3]
        p_refs = refs[3:3 + n_p]
        m_refs = refs[3 + n_p:3 + 2 * n_p]
        v_refs = refs[3 + 2 * n_p:3 + 3 * n_p]
        outs = refs[3 + 3 * n_p:]
        gwa_ref, gcw_ref, loss_ref = outs[:3]
        res = outs[3:]
        total = jnp.sum(g_ref[...], axis=0, keepdims=True)
        seg = lambda nm: total[:, offs[nm][0]:offs[nm][0] + offs[nm][1]]
        loss_ref[...] = seg("loss")
        gwa_ref[...] = hdot(silu(c_ref[...]), dm_ref[...], "tn")
        cw = seg("conv_w")
        for j in range(4):
            gcw_ref[j:j + 1, :] = cw[:, j * CONV_DIM:(j + 1) * CONV_DIM]
        hc = lax.broadcasted_iota(jnp.int32, (B_INNER, LANES), 0)
        hj = lax.broadcasted_iota(jnp.int32, (B_INNER, LANES), 1)
        per_head = ((hc >> 6) == hj).astype(F32)
        heads = lambda nm: hdot(jnp.broadcast_to(seg(nm), (8, B_INNER)), per_head)[0:1, 0:32]
        lbp = sigmoid(p_refs[1][0:1, :] - p_refs[1][1:2, :])
        g_row = seg("lb") * lbp * (1.0 - lbp)
        grads = {"b_ada": seg("mod"), "hgrn_gnorm": seg("gnorm"), "ssm_conv_b": seg("conv_b"),
                 "ssm_dt_bias": seg("dt_bias")[:, 0:32], "ssm_a_log": heads("a_log"), "ssm_d": heads("d"),
                 "ssm_norm": seg("ssm_norm"), "ln1_g": seg("ln1_g"), "ln1_b": seg("ln1_b"),
                 "ln2_g": seg("ln2_g"), "ln2_b": seg("ln2_b")}
        for i, nm in enumerate(SMALL_PARAMS):
            g_out, d_out, m_out, v_out = res[4 * i:4 * i + 4]
            if nm == "hgrn_lb":
                for row, gv in ((0, g_row), (1, -g_row)):
                    sl = slice(row, row + 1)
                    dl, mn, vn = adamw(p_refs[i][sl, :], gv, m_refs[i][sl, :], v_refs[i][sl, :])
                    g_out[sl, :], d_out[sl, :], m_out[sl, :], v_out[sl, :] = gv, dl, mn, vn
            else:
                gv = grads[nm]
                dl, mn, vn = adamw(p_refs[i][...], gv, m_refs[i][...], v_refs[i][...])
                g_out[...], d_out[...], m_out[...], v_out[...] = gv, dl, mn, vn

    out_shape = [jax.ShapeDtypeStruct((D, dmod_cols.shape[1]), F32), jax.ShapeDtypeStruct((4, CONV_DIM), F32),
                 jax.ShapeDtypeStruct((1, LANES), F32)]
    for p in params:
        out_shape += [jax.ShapeDtypeStruct(p.shape, F32)] * 4
    return pl.pallas_call(
        body, name="finalize_small", out_shape=out_shape,
        compiler_params=pltpu.CompilerParams(vmem_limit_bytes=VMEM_LIMIT),
    )(g_all, c_all, dmod_cols, *params, *m, *v)


def adam_update(w, g, m, v, name):
    rows, cols = w.shape
    tm, tc = _tile2(rows, cols)

    def body(w_ref, g_ref, m_ref, v_ref, d_ref, mo_ref, vo_ref):
        d_ref[...], mo_ref[...], vo_ref[...] = adamw(w_ref[...], g_ref[...], m_ref[...], v_ref[...])

    spec = pl.BlockSpec((tm, tc), lambda i, j: (i, j))
    return pl.pallas_call(
        body, name=name, grid=(rows // tm, cols // tc), in_specs=[spec] * 4, out_specs=[spec] * 3,
        out_shape=[jax.ShapeDtypeStruct((rows, cols), F32)] * 3,
        compiler_params=_params(("arbitrary", "arbitrary")),
    )(w, g, m, v)


def kernel(x, c, w_ada, b_ada, w_in, hgrn_lb, hgrn_gnorm, ssm_conv_w, ssm_conv_b, ssm_dt_bias, ssm_a_log, ssm_d, ssm_norm, w_branch_a, w_branch_b, w_o, ln1_g, ln1_b, w_ffn_gate, w_ffn_up, w_ffn_down, ln2_g, ln2_b, loss_target, m_w_ada, m_b_ada, m_w_in, m_hgrn_lb, m_hgrn_gnorm, m_ssm_conv_w, m_ssm_conv_b, m_ssm_dt_bias, m_ssm_a_log, m_ssm_d, m_ssm_norm, m_w_branch_a, m_w_branch_b, m_w_o, m_ln1_g, m_ln1_b, m_w_ffn_gate, m_w_ffn_up, m_w_ffn_down, m_ln2_g, m_ln2_b, v_w_ada, v_b_ada, v_w_in, v_hgrn_lb, v_hgrn_gnorm, v_ssm_conv_w, v_ssm_conv_b, v_ssm_dt_bias, v_ssm_a_log, v_ssm_d, v_ssm_norm, v_w_branch_a, v_w_branch_b, v_w_o, v_ln1_g, v_ln1_b, v_w_ffn_gate, v_w_ffn_up, v_w_ffn_down, v_ln2_g, v_ln2_b):
    given = dict(locals())
    chip = 2 * lax.axis_index("x") + lax.axis_index("y")
    core = lax.axis_index("c")
    t = x.shape[1]

    first = gather_rows(jnp.concatenate([c, ssm_conv_w.reshape(1, CONV_DIM)], axis=1), "gather_cond").reshape(8, D + CONV_DIM)
    c_all = first[:, :D]
    conv_w = first[0::2, D:].reshape(4, 4, CONV_DIM // 4).transpose(1, 0, 2).reshape(4, CONV_DIM)
    mod_part, lb_row = ada_prepare(c_all, w_ada[0], hgrn_lb)
    mod_cols = w_ada.shape[2]
    mod_row = exchange_rows(mod_part.reshape(8, 1, mod_cols), "exchange_mod").reshape(1, 6 * D) + b_ada

    local = {nm: given[nm][0] for nm in SHARDED if nm != "w_ffn_in"}
    local["w_ffn_in"] = jnp.concatenate([w_ffn_gate[0].T, w_ffn_up[0].T], axis=0)
    shards = [local[nm].astype(BF16) for nm in SHARDED]
    send_in, recv_in, sent_in, land_in, started_in = gather_start(shards[:1], mod_row, "in")
    shards = shards[:1] + [(local[nm] + started_in[0, 0]).astype(BF16) for nm in SHARDED[1:]]
    send_rest, recv_rest, *flying = gather_start(shards[1:], started_in, "rest")
    n_rest = len(SHARDED) - 1
    sent_rest, land_rest, started_rest = flying[:n_rest], flying[n_rest:2 * n_rest], flying[-1]
    mod_row = mod_row + started_rest[0:1, 0:1]
    mod = tuple(mod_row[:, i * D:(i + 1) * D] for i in range(6))
    with_own = lambda land, shard: lax.dynamic_update_slice(land, shard[None], (chip, 0, 0))

    class Weights:
        def input_projection(self, after):
            (own,), land = gather_wait(send_in, recv_in, [sent_in], [land_in], after, "in")
            (land,) = forward_wait(forward_start(land, "in"), after, "in")
            return assemble_in_proj(land, own, chip)

        def start_rest(self, after):
            self.own, landed = gather_wait(send_rest, recv_rest, sent_rest, land_rest, after, "rest")
            self.started = forward_start(landed, "rest")
            return self.started[-1]

        def rest(self, after):
            got = {nm: with_own(land, s) for nm, land, s in zip(SHARDED[1:], forward_wait(self.started, after, "rest"), self.own, strict=True)}
            whole = lambda nm: got[nm].reshape(4 * got[nm].shape[1], got[nm].shape[2])
            return tuple(whole(nm) for nm in SHARDED[1:])

    wts = Weights()

    per_head = lambda p: jnp.pad(p, ((0, 0), (0, LANES - p.shape[1])))
    per_channel = lambda p: jnp.repeat(p[0], B_INNER // 32)[None]
    small = (lb_row, hgrn_gnorm, conv_w, ssm_conv_b, per_head(ssm_dt_bias), per_channel(ssm_a_log),
             per_channel(ssm_d), ssm_norm, ln1_g, ln1_b, ln2_g, ln2_b)
    by_rows = lambda g: g.reshape(4, g.shape[0] // 4, g.shape[1])
    travelling = {}

    def start_early(dws):
        travelling["pair"] = pair_start([by_rows(dw) for dw in dws], "early")
        return travelling["pair"][-1]

    def between_scans(after):
        slabs, received = pair_wait(travelling["pair"], after, "early")
        travelling["pairs"] = [pair_add(s, r, core, "pair_add_" + nm) for nm, s, r in zip(SHARDED[1:], slabs, received, strict=True)]
        travelling["started"] = scatter_start(travelling["pairs"], "early")
        return travelling["started"][-1]

    def finish_early(after):
        travelling["pairs"], travelling["landed"] = scatter_wait(travelling["started"], after, "early")

    def start_last(u1, dproj):
        sending = rows_start(matmul_rows_half(u1, dproj, 1 - core, "in_proj_dw_sibling"), "last")
        own = matmul_rows_half(u1, dproj, core, "in_proj_dw_own", after=sending[-1])
        travelling["pairs_in"] = [split_pair_add(own, rows_wait(sending, own, "last"))]
        travelling["started_in"] = scatter_start(travelling["pairs_in"], "last")
        return travelling["started_in"][-1]

    loss, grad_x, d_mod, d_wts, d_small = local_step(x[0], loss_target[0], mod, wts, small,
                                                     start_early, between_scans, finish_early, start_last)

    d_lb, d_gn, d_cw, d_cb, d_dtb, d_alog, d_dsk, d_nw, d_l1g, d_l1b, d_l2g, d_l2b = d_small
    row = jnp.concatenate(list(d_mod) + [d_lb, d_gn, d_cw.reshape(1, 4 * CONV_DIM), d_cb, d_dtb, d_alog, d_dsk, d_nw,
                                          d_l1g, d_l1b, d_l2g, d_l2b, jnp.pad(loss, ((0, 0), (0, LANES - 1)))], axis=1)
    g_all = gather_rows(row, "gather_small_grads").reshape(8, row.shape[1])
    dmod_cols = lax.dynamic_slice_in_dim(g_all, chip * mod_cols, mod_cols, axis=1)
    fin = finalize_small(g_all, c_all, dmod_cols, [given[n] for n in SMALL_PARAMS],
                         [given["m_" + n] for n in SMALL_PARAMS], [given["v_" + n] for n in SMALL_PARAMS])
    grads, deltas, new_m, new_v = {}, {}, {}, {}
    grads["w_ada"] = fin[0][None]
    grads["ssm_conv_w"] = lax.dynamic_slice_in_dim(fin[1], chip * (CONV_DIM // 4), CONV_DIM // 4, axis=1)[None]
    for i, nm in enumerate(SMALL_PARAMS):
        grads[nm], deltas[nm], new_m[nm], new_v[nm] = fin[3 + 4 * i:7 + 4 * i]

    pairs_in, landed_in = scatter_wait(travelling["started_in"], fin[3], "last")
    pairs, landed = pairs_in + travelling["pairs"], landed_in + travelling["landed"]
    halves = [sum_chips(r, p, chip, core, "sum_chips_" + nm) for nm, r, p in zip(SHARDED, landed, pairs, strict=True)]
    reduced = dict(zip(SHARDED, exchange_halves(halves), strict=True))
    reduced["w_ada"], reduced["ssm_conv_w"] = grads["w_ada"][0], grads["ssm_conv_w"][0]
    reduced["w_in"] = reduced["w_in"].T
    reduced["w_ffn_gate"], reduced["w_ffn_up"] = reduced["w_ffn_in"][:FFN_SHARD], reduced["w_ffn_in"][FFN_SHARD:]
    for nm in ("w_ada", "ssm_conv_w", "w_in", "w_branch_a", "w_branch_b", "w_o", "w_ffn_gate", "w_ffn_up", "w_ffn_down"):
        flipped = nm in ("w_in", "w_ffn_gate", "w_ffn_up")
        work = (lambda a: a[0].T) if flipped else (lambda a: a[0])
        back = (lambda a: a.T[None]) if flipped else (lambda a: a[None])
        d_, m_, v_ = adam_update(work(given[nm]), reduced[nm], work(given["m_" + nm]), work(given["v_" + nm]), "adam_" + nm)
        grads[nm], deltas[nm], new_m[nm], new_v[nm] = back(reduced[nm]), back(d_), back(m_), back(v_)

    names = ("w_ada", "b_ada", "w_in", "hgrn_lb", "hgrn_gnorm", "ssm_conv_w", "ssm_conv_b", "ssm_dt_bias", "ssm_a_log",
             "ssm_d", "ssm_norm", "w_branch_a", "w_branch_b", "w_o", "ln1_g", "ln1_b", "w_ffn_gate", "w_ffn_up",
             "w_ffn_down", "ln2_g", "ln2_b")
    return (fin[2][0, 0], grad_x[None], *[grads[n] for n in names], *[deltas[n] for n in names],
            *[new_m[n] for n in names], *[new_v[n] for n in names])
```

```python
import functools

import jax
import jax.numpy as jnp
from jax import lax
from jax.experimental import pallas as pl
from jax.experimental.pallas import tpu as pltpu

F32, BF16 = jnp.float32, jnp.bfloat16
HI = lax.Precision.HIGHEST
MESH = pl.DeviceIdType.MESH

D = 1024
CHUNK = 64
LANES = 128
N_HEADS_A = 8
N_GROUPS_B = 4
B_INNER = 2048
CONV_DIM = 3072
D_FF = 2816
ALPHA = 2.0 ** 0.25
LN_EPS = 1e-5
RMS_EPS = 1e-6
ADAM_LR, ADAM_B1, ADAM_B2, ADAM_EPS, ADAM_WD, ADAM_STEP = 0.001, 0.9, 0.999, 1e-08, 0.01, 10

IN_ORIG = 11296
IN_PAD = 11520
COL_GA, COL_GB, COL_XBC, COL_Z, COL_DT = 4096, 5120, 6144, 9216, 11264
ORIG_Z, ORIG_XBC, ORIG_DT, ORIG_GA = 4096, 6144, 9216, 9248

SHARDED = ("w_in", "w_branch_a", "w_branch_b", "w_o", "w_ffn_in", "w_ffn_down")
FFN_SHARD = D_FF // 4
VMEM_LIMIT = 56 * 1024 * 1024
BLOCK_BYTES = 2 * 1024 * 1024
_DIMS = {"nn": (((1,), (0,)), ((), ())), "nt": (((1,), (1,)), ((), ())), "tn": (((0,), (0,)), ((), ()))}


def _bd(a, b, mode):
    return lax.dot_general(a.astype(BF16), b.astype(BF16), _DIMS[mode], preferred_element_type=F32)


@functools.partial(jax.custom_vjp, nondiff_argnums=(2,))
def bdot(a, b, mode):
    return _bd(a, b, mode)


def _bdot_fwd(a, b, mode):
    return _bd(a, b, mode), (a, b)


def _bdot_bwd(mode, res, g):
    a, b = res
    if mode == "nn":
        return _bd(g, b, "nt"), _bd(a, g, "tn")
    if mode == "nt":
        return _bd(g, b, "nn"), _bd(g, a, "tn")
    return _bd(b, g, "nt"), _bd(a, g, "nn")


bdot.defvjp(_bdot_fwd, _bdot_bwd)


def hdot(a, b, mode="nn"):
    return lax.dot_general(a, b, _DIMS[mode], precision=HI, preferred_element_type=F32)


def _raw(a, b, mode):
    return lax.dot_general(a, b, _DIMS[mode], preferred_element_type=F32)


def _split(x, n):
    parts, rest = [], x
    for _ in range(n):
        p = rest.astype(BF16)
        parts.append(p)
        rest = rest - p.astype(F32)
    return parts


def _od(a, b, mode, exact):
    if exact == 1:
        e = b.astype(BF16)
        p = _split(a, 3)
        return (_raw(p[2], e, mode) + _raw(p[1], e, mode)) + _raw(p[0], e, mode)
    e = a.astype(BF16)
    p = _split(b, 3)
    return (_raw(e, p[2], mode) + _raw(e, p[1], mode)) + _raw(e, p[0], mode)


@functools.partial(jax.custom_vjp, nondiff_argnums=(2, 3))
def odot(a, b, mode, exact):
    return _od(a, b, mode, exact)


def _odot_fwd(a, b, mode, exact):
    return _od(a, b, mode, exact), (a, b)


def _odot_bwd(mode, exact, res, g):
    a, b = res
    if exact == 1:
        da = {"nn": lambda: _od(g, b, "nt", 1), "nt": lambda: _od(g, b, "nn", 1), "tn": lambda: _od(b, g, "nt", 0)}[mode]()
        return da, jnp.zeros_like(b)
    db = {"nn": lambda: _od(a, g, "tn", 0), "nt": lambda: _od(g, a, "tn", 1), "tn": lambda: _od(a, g, "nn", 0)}[mode]()
    return jnp.zeros_like(a), db


odot.defvjp(_odot_fwd, _odot_bwd)


_BDIMS = {"bnn": (((2,), (1,)), ((0,), (0,))), "bnt": (((2,), (2,)), ((0,), (0,))), "btn": (((1,), (1,)), ((0,), (0,)))}


def _braw(a, b, mode):
    return lax.dot_general(a, b, _BDIMS[mode], preferred_element_type=F32)


def _bdb(a, b, mode):
    return _braw(a.astype(BF16), b.astype(BF16), mode)


def _d3b(a, b, mode):
    ah, al = _split(a, 2)
    bh, bl = _split(b, 2)
    return _braw(ah, bh, mode) + (_braw(ah, bl, mode) + _braw(al, bh, mode))


def _batched_bwd(f):
    def bwd(mode, res, g):
        a, b = res
        if mode == "bnn":
            return f(g, b, "bnt"), f(a, g, "btn")
        if mode == "bnt":
            return f(g, b, "bnn"), f(g, a, "btn")
        return f(b, g, "bnt"), f(a, g, "bnn")
    return bwd


@functools.partial(jax.custom_vjp, nondiff_argnums=(2,))
def bdot_b(a, b, mode):
    return _bdb(a, b, mode)


bdot_b.defvjp(lambda a, b, mode: (_bdb(a, b, mode), (a, b)), _batched_bwd(_bdb))


@functools.partial(jax.custom_vjp, nondiff_argnums=(2,))
def dot3_b(a, b, mode):
    return _d3b(a, b, mode)


dot3_b.defvjp(lambda a, b, mode: (_d3b(a, b, mode), (a, b)), _batched_bwd(_d3b))


def _cum(tril3, x, mode):
    e = tril3.astype(BF16)
    p = _split(x, 3)
    return (_braw(e, p[2], mode) + _braw(e, p[1], mode)) + _braw(e, p[0], mode)


@jax.custom_vjp
def chunk_cumsum(tril3, x):
    return _cum(tril3, x, "bnn")


chunk_cumsum.defvjp(lambda t, x: (_cum(t, x, "bnn"), t), lambda t, g: (jnp.zeros_like(t), _cum(t, g, "btn")))


def _unstack(axis, n):
    @jax.custom_vjp
    def un(x):
        return tuple(lax.index_in_dim(x, i, axis, keepdims=False) for i in range(n))

    un.defvjp(lambda x: (un(x), None), lambda _, g: (jnp.stack(g, axis=axis),))
    return un


def _split_last(n, w):
    @jax.custom_vjp
    def sp(x):
        return tuple(x[..., i * w:(i + 1) * w] for i in range(n))

    sp.defvjp(lambda x: (sp(x), None), lambda _, g: (jnp.concatenate(g, axis=-1),))
    return sp


def sigmoid(x):
    return 0.5 * jnp.tanh(0.5 * x) + 0.5


def silu(x):
    return x * sigmoid(x)


def softplus(x):
    return jnp.maximum(x, 0.0) + jnp.log1p(jnp.exp(jnp.minimum(x, -x)))


def _ln(x):
    mu = jnp.mean(x, axis=-1, keepdims=True)
    xc = x - mu
    return xc * lax.rsqrt(jnp.mean(xc * xc, axis=-1, keepdims=True) + LN_EPS)


def _tril64():
    r = lax.broadcasted_iota(jnp.int32, (CHUNK, CHUNK), 0)
    c = lax.broadcasted_iota(jnp.int32, (CHUNK, CHUNK), 1)
    return (r >= c).astype(F32)


def hgrn_block(q, fl, iv, gr, st, lb, gn):
    tb = q.shape[0]
    nc = tb // CHUNK
    nh = N_HEADS_A
    heads = _split_last(nh, LANES)
    to4 = lambda a: jnp.stack(heads(a), axis=0).reshape(nh, nc, CHUNK, LANES)
    flat = lambda a: a.reshape(nh * nc, CHUNK, LANES)
    f = lb + (1.0 - lb) * sigmoid(fl)
    gl4, k4, qf4, v4, gr4 = to4(jnp.log(f)), to4(1.0 - f), to4(silu(q) * (128 ** -0.5)), to4(iv), to4(gr)
    tril = _tril64()
    b4 = chunk_cumsum(jnp.broadcast_to(tril[None], (nh * nc, CHUNK, CHUNK)), flat(gl4)).reshape(gl4.shape)
    blast = jnp.sum(gl4, axis=2, keepdims=True)
    ref = lax.stop_gradient(0.5 * blast)
    qp, kp = qf4 * jnp.exp(b4 - ref), k4 * jnp.exp(ref - b4)
    sc = dot3_b(flat(qp), flat(kp), "bnt") * tril
    o_intra = bdot_b(sc, flat(v4), "bnn").reshape(gl4.shape)
    chunks = _unstack(1, nc)
    qe, v_c, kd, dec = chunks(qp * jnp.exp(ref)), chunks(v4), chunks(kp * jnp.exp(blast - ref)), chunks(jnp.exp(blast))
    o_inter = []
    for c in range(nc):
        o_inter.append(bdot_b(qe[c], st, "bnt"))
        st = st * dec[c] + bdot_b(v_c[c], kd[c], "btn")
    o = o_intra + jnp.stack(o_inter, axis=1)
    on = o * lax.rsqrt(jnp.mean(o * o, axis=-1, keepdims=True) + RMS_EPS) * gn
    out = (on * silu(gr4)).reshape(nh, tb, LANES)
    return jnp.concatenate(_unstack(0, nh)(out), axis=1), st


def ssd_consts(g):
    i32 = jnp.int32
    ej = lax.broadcasted_iota(i32, (LANES, 512), 0)
    ec = lax.broadcasted_iota(i32, (LANES, 512), 1)
    expand = (ej == g * 8 + (ec >> 6)).astype(F32)
    ts = lax.broadcasted_iota(i32, (CHUNK, 512), 0)
    tc = lax.broadcasted_iota(i32, (CHUNK, 512), 1)
    itile = (ts == (tc & 63)).astype(F32)
    maskall = ts >= (tc & 63)
    br = lax.broadcasted_iota(i32, (LANES, LANES), 0)
    bc = lax.broadcasted_iota(i32, (LANES, LANES), 1)
    blockmask = ((br >> 6) == (bc >> 6)).astype(F32)
    return expand, itile, maskall, blockmask, _tril64()


def ssd_block(x, bm, cm, dt, z, st, dtb, alog, dsk, nw, cs):
    expand, itile, maskall, blockmask, tril = cs
    tb = x.shape[0]
    nc = tb // CHUNK
    delta = odot(softplus(dt + dtb), expand, "nn", 1)
    a = -jnp.exp(alog) * delta
    xdt = x * delta
    by_chunk = lambda v: v.reshape(nc, CHUNK, v.shape[-1])
    a3, xdt3, bm3, cm3 = by_chunk(a), by_chunk(xdt), by_chunk(bm), by_chunk(cm)
    acum3 = chunk_cumsum(jnp.broadcast_to(tril[None], (nc, CHUNK, CHUNK)), a3)
    alast3 = jnp.sum(a3, axis=1, keepdims=True)
    cb3 = bdot_b(cm3, jnp.concatenate([bm3] * 8, axis=1), "bnt")
    arow3 = jnp.sum(acum3 * itile, axis=1, keepdims=True)
    dec3 = jnp.exp(jnp.where(maskall, acum3 - arow3, -1e30))
    pairs = _split_last(4, LANES)
    intra = [bdot_b(m, jnp.concatenate([xp] * 2, axis=1) * blockmask, "bnn")
             for m, xp in zip(pairs(cb3 * dec3), pairs(xdt3))]
    chunks = _unstack(0, nc)
    cm_c, bm_c, xw_c, dec_c = chunks(cm3), chunks(bm3), chunks(xdt3 * jnp.exp(alast3 - acum3)), chunks(jnp.exp(alast3))
    inter = []
    for c in range(nc):
        inter.append(bdot(cm_c[c], st, "nn"))
        st = st * dec_c[c] + bdot(bm_c[c], xw_c[c], "tn")
    st_new = st
    y = (jnp.concatenate(intra, axis=-1) + jnp.stack(inter, axis=0) * jnp.exp(acum3)).reshape(tb, 512)
    yz = (y + x * dsk) * silu(z)
    return yz * lax.rsqrt(jnp.mean(yz * yz, axis=-1, keepdims=True) + RMS_EPS) * nw, st_new


def adamw(w, g, m, v):
    m = ADAM_B1 * m + (1.0 - ADAM_B1) * g
    v = ADAM_B2 * v + (1.0 - ADAM_B2) * jnp.square(g)
    m_hat = m / (1.0 - ADAM_B1 ** ADAM_STEP)
    v_hat = v / (1.0 - ADAM_B2 ** ADAM_STEP)
    return -ADAM_LR * (m_hat / (jnp.sqrt(v_hat) + ADAM_EPS) + ADAM_WD * w), m, v


def _pick(n, cands):
    for c in cands:
        if n % c == 0:
            return c
    return n


def _params(sem):
    return pltpu.CompilerParams(dimension_semantics=sem, vmem_limit_bytes=VMEM_LIMIT)


MATMUL_VMEM_BUDGET = 50 * 1024 * 1024
MATMUL_MIN_STEPS = 4


def matmul(a, b, mode, out_dtype, name, after=None):
    if mode == "nn":
        (m, k), n = a.shape, b.shape[1]
    elif mode == "nt":
        (m, k), n = a.shape, b.shape[0]
    else:
        (k, m), n = a.shape, b.shape[1]
    a_bytes, b_bytes, out_bytes = a.dtype.itemsize, b.dtype.itemsize, jnp.dtype(out_dtype).itemsize
    k_sizes = (2304, 2048, 1408, 1024, 768, 512, 256, 128)
    usual_tk = _pick(k, k_sizes)

    def vmem(tm_, tn_, tk_):
        blocks = 2 * (tm_ * tk_ * a_bytes + tk_ * tn_ * b_bytes + tm_ * tn_ * out_bytes)
        return blocks + (tm_ * tn_ * 4 if tk_ < k else 0)

    def traffic(tm_, tn_, tk_):
        return (m // tm_) * k * n * b_bytes + (n // tn_ if tk_ < k else 1) * m * k * a_bytes

    sizes = (2304, 2048, 1920, 1408, 1024, 768, 512, 256, 128)
    tiles = [(tm_, tn_, tk_) for tm_ in sizes if m % tm_ == 0 for tn_ in sizes if n % tn_ == 0
             for tk_ in (k,) + k_sizes if k % tk_ == 0 if vmem(tm_, tn_, tk_) <= MATMUL_VMEM_BUDGET] or [(m, n, k)]
    pipelined = [t for t in tiles if (m // t[0]) * (n // t[1]) * (k // t[2]) >= MATMUL_MIN_STEPS]
    tm, tn, tk = min(pipelined or tiles, key=lambda t: (traffic(*t), t[2] != usual_tk, -t[0] * t[1]))
    nk = k // tk
    a_spec = pl.BlockSpec((tk, tm), lambda i, j, kk: (kk, i)) if mode == "tn" else pl.BlockSpec((tm, tk), lambda i, j, kk: (i, kk))
    b_spec = pl.BlockSpec((tn, tk), lambda i, j, kk: (j, kk)) if mode == "nt" else pl.BlockSpec((tk, tn), lambda i, j, kk: (kk, j))

    order = [] if after is None else [after]

    def body(a_ref, b_ref, *rest):
        o_ref, *acc = rest[len(order):]
        part = _bd(a_ref[...], b_ref[...], mode)
        if nk == 1:
            o_ref[...] = part.astype(o_ref.dtype)
            return
        acc_ref, = acc
        kk = pl.program_id(2)

        @pl.when(kk == 0)
        def _():
            acc_ref[...] = part

        @pl.when(jnp.logical_and(kk > 0, kk < nk - 1))
        def _():
            acc_ref[...] += part

        @pl.when(kk == nk - 1)
        def _():
            o_ref[...] = (acc_ref[...] + part).astype(o_ref.dtype)

    return pl.pallas_call(
        body, name=name, grid=(m // tm, n // tn, nk),
        in_specs=[a_spec, b_spec] + [pl.BlockSpec(memory_space=pl.ANY) for _ in order],
        out_specs=pl.BlockSpec((tm, tn), lambda i, j, kk: (i, j)),
        out_shape=jax.ShapeDtypeStruct((m, n), out_dtype),
        scratch_shapes=[pltpu.VMEM((tm, tn), F32)] if nk > 1 else [],
        compiler_params=_params(("parallel", "parallel", "arbitrary")),
    )(a, b, *order)


def rowwise(name, fn, rows, consts, out_rows, out_accs=(), tm_max=512, into=None, new_wide=None):
    t = rows[0][0].shape[0]
    tm = _pick(t, (tm_max, 128, 64, 32, 16, 8))
    n_r, n_c, n_o = len(rows), len(consts), len(out_rows)
    n_alias = 0 if into is None else 1

    def body(*refs):
        r_in = [r[...] for r in refs[:n_r]]
        c_in = [r[...] for r in refs[n_r:n_r + n_c]]
        refs = refs[:n_r + n_c] + refs[n_r + n_c + n_alias:]
        o_refs = refs[n_r + n_c:n_r + n_c + n_o]
        a_refs = refs[n_r + n_c + n_o:]
        ro, ao = fn(r_in, c_in)
        for ref, val in zip(o_refs, ro, strict=True):
            ref[...] = val.astype(ref.dtype)
        if a_refs:
            @pl.when(pl.program_id(0) == 0)
            def _():
                for ref in a_refs:
                    ref[...] = jnp.zeros_like(ref)

            for ref, val in zip(a_refs, ao, strict=True):
                ref[...] += val

    in_specs = [pl.BlockSpec((tm, w), functools.partial(lambda i, cb: (i, cb), cb=cb)) for _, w, cb in rows]
    in_specs += [pl.BlockSpec(c.shape, lambda i: (0, 0)) for c in consts]
    out_specs = [pl.BlockSpec((tm, w), lambda i: (i, 0)) for w, _ in out_rows]
    out_specs += [pl.BlockSpec(s, lambda i: (0, 0)) for s in out_accs]
    out_shape = [jax.ShapeDtypeStruct((t, w), dt) for w, dt in out_rows]
    out_shape += [jax.ShapeDtypeStruct(s, F32) for s in out_accs]
    operands = [r[0] for r in rows] + list(consts)
    aliases = {}
    if into is not None:
        target, cb = into
        in_specs.append(pl.BlockSpec(memory_space=pl.ANY))
        operands.append(target)
        out_specs[0] = pl.BlockSpec((tm, out_rows[0][0]), lambda i: (i, cb))
        out_shape[0] = jax.ShapeDtypeStruct(target.shape, target.dtype)
        aliases = {len(operands) - 1: 0}
    if new_wide is not None:
        width, cb = new_wide
        out_specs[0] = pl.BlockSpec((tm, out_rows[0][0]), lambda i: (i, cb))
        out_shape[0] = jax.ShapeDtypeStruct((t, width), out_rows[0][1])
    return pl.pallas_call(
        body, name=name, grid=(t // tm,), in_specs=in_specs, out_specs=out_specs, out_shape=out_shape,
        input_output_aliases=aliases, compiler_params=_params(("arbitrary",)),
    )(*operands)


def _full(a):
    return (a, a.shape[1], 0)


HGRN_TIME_BLOCK = 256
SSD_TIME_BLOCK = 512


def _time_block(t, most=HGRN_TIME_BLOCK):
    return _pick(t, tuple(b for b in (512, 256, 128, 64) if b <= most))


def _quarters(ref):
    return [ref[:, seg * D:(seg + 1) * D] for seg in range(4)]


def hgrn_forward(proj, lb, gn):
    t = proj.shape[0]
    tb = _time_block(t)
    nb = t // tb

    def body(qfig_ref, lb_ref, gn_ref, o_ref, st_ref, state):
        @pl.when(pl.program_id(0) == 0)
        def _():
            state[...] = jnp.zeros_like(state)

        st = state[...]
        st_ref[...] = st
        out, st_new = hgrn_block(*_quarters(qfig_ref), st, lb_ref[...], gn_ref[...])
        o_ref[...] = out.astype(o_ref.dtype)
        state[...] = st_new

    return pl.pallas_call(
        body, name="hgrn_forward", grid=(nb,),
        in_specs=[pl.BlockSpec((tb, 4 * D), lambda j: (j, 0)),
                  pl.BlockSpec((1, D), lambda j: (0, 0)), pl.BlockSpec((1, LANES), lambda j: (0, 0))],
        out_specs=[pl.BlockSpec((tb, D), lambda j: (j, 0)),
                   pl.BlockSpec((None, N_HEADS_A, LANES, LANES), lambda j: (j, 0, 0, 0))],
        out_shape=[jax.ShapeDtypeStruct((t, D), BF16),
                   jax.ShapeDtypeStruct((nb, N_HEADS_A, LANES, LANES), F32)],
        scratch_shapes=[pltpu.VMEM((N_HEADS_A, LANES, LANES), F32)],
        compiler_params=_params(("arbitrary",)),
    )(proj, lb, gn)


def hgrn_backward(proj, states, d_out, lb, gn, d_proj):
    t = proj.shape[0]
    tb = _time_block(t)
    nb = t // tb

    def body(qfig_ref, st_ref, do_ref, lb_ref, gn_ref, _, dqfig_ref, dlb_ref, dgn_ref, d_state):
        @pl.when(pl.program_id(0) == 0)
        def _():
            d_state[...] = jnp.zeros_like(d_state)
            dlb_ref[...] = jnp.zeros_like(dlb_ref)
            dgn_ref[...] = jnp.zeros_like(dgn_ref)

        _, vjp = jax.vjp(hgrn_block, *_quarters(qfig_ref), st_ref[...], lb_ref[...], gn_ref[...])
        dq, df, di, dg, dst, dlb, dgn = vjp((do_ref[...], d_state[...]))
        for seg, val in enumerate((dq, df, di, dg)):
            dqfig_ref[:, seg * D:(seg + 1) * D] = val.astype(dqfig_ref.dtype)
        d_state[...] = dst
        dlb_ref[...] += dlb
        dgn_ref[...] += dgn

    rev = lambda j: nb - 1 - j
    return pl.pallas_call(
        body, name="hgrn_backward", grid=(nb,),
        in_specs=[pl.BlockSpec((tb, 4 * D), lambda j: (rev(j), 0)),
                  pl.BlockSpec((None, N_HEADS_A, LANES, LANES), lambda j: (rev(j), 0, 0, 0)),
                  pl.BlockSpec((tb, D), lambda j: (rev(j), 0)),
                  pl.BlockSpec((1, D), lambda j: (0, 0)), pl.BlockSpec((1, LANES), lambda j: (0, 0)),
                  pl.BlockSpec(memory_space=pl.ANY)],
        out_specs=[pl.BlockSpec((tb, 4 * D), lambda j: (rev(j), 0)),
                   pl.BlockSpec((1, D), lambda j: (0, 0)), pl.BlockSpec((1, LANES), lambda j: (0, 0))],
        out_shape=[jax.ShapeDtypeStruct(d_proj.shape, d_proj.dtype), jax.ShapeDtypeStruct((1, D), F32),
                   jax.ShapeDtypeStruct((1, LANES), F32)],
        input_output_aliases={5: 0},
        scratch_shapes=[pltpu.VMEM((N_HEADS_A, LANES, LANES), F32)],
        compiler_params=_params(("arbitrary",)),
    )(proj, states, d_out, lb, gn, d_proj)


def _ssd_in_specs(tb, tmap):
    return [pl.BlockSpec((tb, 512), lambda g, j: (tmap(j), g)),
            pl.BlockSpec((tb, LANES), lambda g, j: (tmap(j), 16 + g)),
            pl.BlockSpec((tb, LANES), lambda g, j: (tmap(j), 20 + g)),
            pl.BlockSpec((tb, LANES), lambda g, j: (tmap(j), COL_DT // LANES)),
            pl.BlockSpec((tb, 512), lambda g, j: (tmap(j), COL_Z // 512 + g))]


def ssd_forward(xc, proj, dtb, alog, dsk, nw):
    t = proj.shape[0]
    tb = _time_block(t, SSD_TIME_BLOCK)
    nb = t // tb

    def body(x_ref, b_ref, c_ref, dt_ref, z_ref, dtb_ref, alog_ref, dsk_ref, nw_ref, o_ref, st_ref, state):
        @pl.when(pl.program_id(1) == 0)
        def _():
            state[...] = jnp.zeros_like(state)

        st = state[...]
        st_ref[...] = st
        out, st_new = ssd_block(x_ref[...], b_ref[...], c_ref[...], dt_ref[...], z_ref[...], st,
                                dtb_ref[...], alog_ref[...], dsk_ref[...], nw_ref[...], ssd_consts(pl.program_id(0)))
        o_ref[...] = out.astype(o_ref.dtype)
        state[...] = st_new

    vec = pl.BlockSpec((1, 512), lambda g, j: (0, g))
    heads = pl.BlockSpec((1, LANES), lambda g, j: (0, 0))
    return pl.pallas_call(
        body, name="ssd_forward", grid=(N_GROUPS_B, nb),
        in_specs=_ssd_in_specs(tb, lambda j: j) + [heads, vec, vec, vec],
        out_specs=[pl.BlockSpec((tb, 512), lambda g, j: (j, g)),
                   pl.BlockSpec((None, None, LANES, 512), lambda g, j: (j, g, 0, 0))],
        out_shape=[jax.ShapeDtypeStruct((t, B_INNER), BF16),
                   jax.ShapeDtypeStruct((nb, N_GROUPS_B, LANES, 512), F32)],
        scratch_shapes=[pltpu.VMEM((LANES, 512), F32)],
        compiler_params=_params(("arbitrary", "arbitrary")),
    )(xc, xc, xc, proj, proj, dtb, alog, dsk, nw)


def ssd_backward(xc, proj, states, d_out, dtb, alog, dsk, nw, d_proj):
    t = proj.shape[0]
    tb = _time_block(t, SSD_TIME_BLOCK)
    nb = t // tb
    rev = lambda j: nb - 1 - j

    def body(x_ref, b_ref, c_ref, dt_ref, z_ref, st_ref, do_ref, dtb_ref, alog_ref, dsk_ref, nw_ref, _,
             dx_ref, db_ref, dc_ref, ddt_ref, dz_ref, ddtb_ref, dalog_ref, ddsk_ref, dnw_ref, d_state):
        accs = (ddtb_ref, dalog_ref, ddsk_ref, dnw_ref)

        @pl.when(pl.program_id(1) == 0)
        def _():
            d_state[...] = jnp.zeros_like(d_state)
            for ref in accs:
                ref[...] = jnp.zeros_like(ref)

        cs = ssd_consts(pl.program_id(0))
        fn = lambda *a: ssd_block(*a, cs)
        _, vjp = jax.vjp(fn, x_ref[...], b_ref[...], c_ref[...], dt_ref[...], z_ref[...], st_ref[...],
                         dtb_ref[...], alog_ref[...], dsk_ref[...], nw_ref[...])
        dx, db, dc, ddt, dz, dst, *dpar = vjp((do_ref[...], d_state[...]))
        dx_ref[...] = dx
        db_ref[...] = db
        dc_ref[...] = dc
        ddt_ref[...] = ddt
        dz_ref[...] = dz.astype(dz_ref.dtype)
        d_state[...] = dst
        for ref, val in zip(accs, dpar, strict=True):
            ref[...] += val

    vec = pl.BlockSpec((1, 512), lambda g, j: (0, g))
    heads = pl.BlockSpec((1, LANES), lambda g, j: (0, 0))
    acc = pl.BlockSpec((None, 1, 512), lambda g, j: (g, 0, 0))
    acc_heads = pl.BlockSpec((None, 1, LANES), lambda g, j: (g, 0, 0))
    return pl.pallas_call(
        body, name="ssd_backward", grid=(N_GROUPS_B, nb),
        in_specs=_ssd_in_specs(tb, rev)
        + [pl.BlockSpec((None, None, LANES, 512), lambda g, j: (rev(j), g, 0, 0)),
           pl.BlockSpec((tb, 512), lambda g, j: (rev(j), g))] + [heads, vec, vec, vec] + [pl.BlockSpec(memory_space=pl.ANY)],
        out_specs=[pl.BlockSpec((tb, 512), lambda g, j: (rev(j), g)),
                   pl.BlockSpec((tb, LANES), lambda g, j: (rev(j), g)),
                   pl.BlockSpec((tb, LANES), lambda g, j: (rev(j), g)),
                   pl.BlockSpec((None, tb, LANES), lambda g, j: (g, rev(j), 0)),
                   pl.BlockSpec((tb, 512), lambda g, j: (rev(j), COL_Z // 512 + g)), acc_heads, acc, acc, acc],
        out_shape=[jax.ShapeDtypeStruct((t, B_INNER), F32), jax.ShapeDtypeStruct((t, 512), F32),
                   jax.ShapeDtypeStruct((t, 512), F32), jax.ShapeDtypeStruct((N_GROUPS_B, t, LANES), F32),
                   jax.ShapeDtypeStruct(d_proj.shape, d_proj.dtype)]
        + [jax.ShapeDtypeStruct((N_GROUPS_B, 1, LANES), F32)] + [jax.ShapeDtypeStruct((N_GROUPS_B, 1, 512), F32)] * 3,
        input_output_aliases={11: 4},
        scratch_shapes=[pltpu.VMEM((LANES, 512), F32)],
        compiler_params=_params(("arbitrary", "arbitrary")),
    )(xc, xc, xc, proj, proj, states, d_out, dtb, alog, dsk, nw, d_proj)


CONV_HALO = 8


def _shift_down(halo_then_tile, s, tm):
    if s == 0:
        return halo_then_tile[CONV_HALO:CONV_HALO + tm]
    return pltpu.roll(halo_then_tile, s, 0)[CONV_HALO:CONV_HALO + tm]


def _conv_pre(cur, prev, w, b, tm):
    stacked = jnp.concatenate([prev, cur], axis=0)
    taps = [_shift_down(stacked, 3 - j, tm) for j in range(4)]
    pre = b + taps[0] * w[0:1] + taps[1] * w[1:2] + taps[2] * w[2:3] + taps[3] * w[3:4]
    return pre, taps


def _conv_specs(t, tm):
    per = tm // CONV_HALO
    cur = pl.BlockSpec((tm, CONV_DIM), lambda i: (i, COL_XBC // CONV_DIM))
    prev = pl.BlockSpec((CONV_HALO, CONV_DIM), lambda i: (jnp.maximum(i * per - 1, 0), COL_XBC // CONV_DIM))
    return cur, prev


def conv_forward(proj, w, b):
    t = proj.shape[0]
    tm = _pick(t, (256, 128, 64))

    def body(cur_ref, prev_ref, w_ref, b_ref, o_ref):
        prev = jnp.where(pl.program_id(0) == 0, 0.0, prev_ref[...])
        pre, _ = _conv_pre(cur_ref[...], prev, w_ref[...], b_ref[...], tm)
        o_ref[...] = silu(pre)

    cur, prev = _conv_specs(t, tm)
    return pl.pallas_call(
        body, name="conv_forward", grid=(t // tm,),
        in_specs=[cur, prev, pl.BlockSpec((4, CONV_DIM), lambda i: (0, 0)), pl.BlockSpec((1, CONV_DIM), lambda i: (0, 0))],
        out_specs=pl.BlockSpec((tm, CONV_DIM), lambda i: (i, 0)),
        out_shape=jax.ShapeDtypeStruct((t, CONV_DIM), F32),
        compiler_params=_params(("arbitrary",)),
    )(proj, proj, w, b)


def conv_backward(proj, dx, db_, dc_, w, b, d_proj):
    t = proj.shape[0]
    tm = _pick(t, (256, 128, 64))
    per = tm // CONV_HALO
    nt = t // tm
    rev = lambda i: nt - 1 - i

    def body(cur_ref, prev_ref, dx_ref, dbm_ref, dcm_ref, w_ref, b_ref, _, o_ref, dw_ref, dbias_ref, later):
        @pl.when(pl.program_id(0) == 0)
        def _():
            dw_ref[...] = jnp.zeros_like(dw_ref)
            dbias_ref[...] = jnp.zeros_like(dbias_ref)
            later[...] = jnp.zeros_like(later)

        first_tile = pl.program_id(0) == nt - 1
        for lo, hi, src in ((0, B_INNER, dx_ref), (B_INNER, B_INNER + 512, dbm_ref), (B_INNER + 512, CONV_DIM, dcm_ref)):
            cols = slice(lo, hi)
            prev = jnp.where(first_tile, 0.0, prev_ref[:, cols])
            w_ = w_ref[:, cols]
            pre, taps = _conv_pre(cur_ref[:, cols], prev, w_, b_ref[:, cols], tm)
            sg = sigmoid(pre)
            dpre = src[...] * (sg * (1.0 + pre * (1.0 - sg)))
            dbias_ref[:, cols] += jnp.sum(dpre, axis=0, keepdims=True)
            for j in range(4):
                dw_ref[j:j + 1, cols] += jnp.sum(dpre * taps[j], axis=0, keepdims=True)
            stacked = jnp.concatenate([dpre, later[:, cols]], axis=0)
            acc = dpre * w_[3:4]
            for j in range(3):
                acc = acc + pltpu.roll(stacked, tm + CONV_HALO - (3 - j), 0)[0:tm] * w_[j:j + 1]
            o_ref[:, cols] = acc.astype(o_ref.dtype)
            later[:, cols] = dpre[0:CONV_HALO]

    row = lambda w_: pl.BlockSpec((tm, w_), lambda i: (rev(i), 0))
    whole = lambda r: pl.BlockSpec((r, CONV_DIM), lambda i: (0, 0))
    return pl.pallas_call(
        body, name="conv_backward", grid=(nt,),
        in_specs=[pl.BlockSpec((tm, CONV_DIM), lambda i: (rev(i), COL_XBC // CONV_DIM)),
                  pl.BlockSpec((CONV_HALO, CONV_DIM), lambda i: (jnp.maximum(rev(i) * per - 1, 0), COL_XBC // CONV_DIM)),
                  row(B_INNER), row(512), row(512), whole(4), whole(1), pl.BlockSpec(memory_space=pl.ANY)],
        out_specs=[pl.BlockSpec((tm, CONV_DIM), lambda i: (rev(i), COL_XBC // CONV_DIM)), whole(4), whole(1)],
        out_shape=[jax.ShapeDtypeStruct(d_proj.shape, d_proj.dtype), jax.ShapeDtypeStruct((4, CONV_DIM), F32),
                   jax.ShapeDtypeStruct((1, CONV_DIM), F32)],
        input_output_aliases={7: 0},
        scratch_shapes=[pltpu.VMEM((CONV_HALO, CONV_DIM), F32)],
        compiler_params=_params(("arbitrary",)),
    )(proj, proj, dx, db_, dc_, w, b, d_proj)


def stage_modulate(x, sc, sh):
    return _ln(x) * (1.0 + sc) + sh


def stage_merge(ga, gb, ya, yb):
    return sigmoid(ga) * ya + sigmoid(gb) * yb


def stage_post_mixer(x, h, g1, ln_g, ln_b, sc2, sh2):
    x1 = _ln(ALPHA * x + g1 * h) * ln_g + ln_b
    return x1, _ln(x1) * (1.0 + sc2) + sh2


def stage_swiglu(a, b):
    return silu(a) * b


def gate_up(ab):
    w = FFN_SHARD
    return (jnp.concatenate([ab[:, 2 * w * k:2 * w * k + w] for k in range(4)], axis=1),
            jnp.concatenate([ab[:, 2 * w * k + w:2 * w * (k + 1)] for k in range(4)], axis=1))


def per_chip(gate, up):
    w = FFN_SHARD
    return jnp.concatenate([part[:, w * k:w * (k + 1)] for k in range(4) for part in (gate, up)], axis=1)


def stage_loss(x1, hf, tgt, g2, ln_g, ln_b):
    x2 = _ln(ALPHA * x1 + g2 * hf) * ln_g + ln_b
    return 0.5 * jnp.sum(jnp.mean(jnp.square(x2 - tgt), axis=-1, keepdims=True), axis=0, keepdims=True)


def local_step(x, tgt, mod, wts, small, early=None, mid=None, late=None, last=None):
    sh1, sc1, g1, sh2, sc2, g2 = mod
    lb, gn, conv_w, conv_b, dtb, alog, dsk, nw, ln1_g, ln1_b, ln2_g, ln2_b = small
    vec = (1, D)

    (u1,) = rowwise("modulate1", lambda r, c: ((stage_modulate(r[0], *c),), ()), [_full(x)], [sc1, sh1], [(D, BF16)])
    w_in = wts.input_projection(u1)
    proj = matmul(u1, w_in, "nn", F32, "in_proj")
    ya_in, st_a = hgrn_forward(proj, lb, gn + wts.start_rest(proj)[0:1])
    xc = conv_forward(proj, conv_w, conv_b)
    w_a, w_b, w_o, w_gu, w_d = wts.rest(xc)
    yb_in, st_b = ssd_forward(xc, proj, dtb, alog, dsk, nw)
    ya = matmul(ya_in, w_a, "nn", F32, "branch_a")
    yb = matmul(yb_in, w_b, "nn", F32, "branch_b")
    gate_rows = [(proj, D, COL_GA // D), (proj, D, COL_GB // D), _full(ya), _full(yb)]
    (merged,) = rowwise("merge", lambda r, c: ((stage_merge(*r),), ()), gate_rows, [], [(D, BF16)])
    h = matmul(merged, w_o, "nn", F32, "out_proj")
    post_consts = [g1, ln1_g, ln1_b, sc2, sh2]
    x1, u2 = rowwise("post_mixer", lambda r, c: (stage_post_mixer(*r, *c), ()), [_full(x), _full(h)], post_consts,
                     [(D, F32), (D, BF16)])
    ab = matmul(u2, w_gu, "nt", F32, "ffn_in")
    (p,) = rowwise("swiglu", lambda r, c: ((stage_swiglu(*gate_up(r[0])),), ()), [_full(ab)], [], [(D_FF, BF16)],
                   tm_max=256)
    hf = matmul(p, w_d, "nn", F32, "ffn_out")

    def loss_bwd(r, c):
        loss, vjp = jax.vjp(stage_loss, *r, *c)
        dx1, dhf, _, dg2, dlg, dlb_ = vjp(jnp.ones((1, 1), F32))
        return (dx1, dhf), (loss, dg2, dlg, dlb_)

    dx1, dhf, loss, dg2, dln2_g, dln2_b = rowwise(
        "loss_backward", loss_bwd, [_full(x1), _full(hf), _full(tgt)], [g2, ln2_g, ln2_b],
        [(D, F32), (D, BF16)], [(1, 1), vec, vec, vec])
    dp = matmul(dhf, w_d, "nt", F32, "ffn_out_dx")
    dw_d = matmul(p, dhf, "tn", F32, "ffn_out_dw")

    def swiglu_bwd(r, c):
        _, vjp = jax.vjp(stage_swiglu, *gate_up(r[0]))
        return (per_chip(*vjp(r[1])),), ()

    (dab,) = rowwise("swiglu_backward", swiglu_bwd, [_full(ab), _full(dp)], [], [(2 * D_FF, BF16)], tm_max=256)
    du2 = matmul(dab, w_gu, "nn", F32, "ffn_in_dx")
    dw_gu = matmul(dab, u2, "tn", F32, "ffn_in_dw")

    def post_bwd(r, c):
        _, vjp = jax.vjp(stage_post_mixer, r[0], r[1], *c)
        dx, dh, *dc = vjp((r[2], r[3]))
        return (dx, dh), tuple(dc)

    dx_a, dh, dg1, dln1_g, dln1_b, dsc2, dsh2 = rowwise(
        "post_mixer_backward", post_bwd, [_full(x), _full(h), _full(dx1), _full(du2)], post_consts,
        [(D, F32), (D, BF16)], [vec] * 5)
    dmerged = matmul(dh, w_o, "nt", F32, "out_proj_dx")
    dw_o = matmul(merged, dh, "tn", F32, "out_proj_dw")

    def merge_bwd(r, c):
        _, vjp = jax.vjp(stage_merge, *r[:4])
        dga, dgb, dya, dyb = vjp(r[4])
        return (jnp.concatenate([dga, dgb], axis=1), dya, dyb), ()

    dproj, dya, dyb = rowwise("merge_backward", merge_bwd, gate_rows + [_full(dmerged)], [],
                              [(2 * D, BF16), (D, BF16), (D, BF16)], new_wide=(IN_PAD, COL_GA // (2 * D)))
    dya_in = matmul(dya, w_a, "nt", F32, "branch_a_dx")
    dw_a = matmul(ya_in, dya, "tn", F32, "branch_a_dw")
    dyb_in = matmul(dyb, w_b, "nt", F32, "branch_b_dx")
    dw_b = matmul(yb_in, dyb, "tn", F32, "branch_b_dw")
    gn_after = gn if early is None else gn + early((dw_a, dw_b, dw_o, dw_gu, dw_d))[0:1]
    dproj, dlb, dgn = hgrn_backward(proj, st_a, dya_in, lb, gn_after, dproj)
    dtb_after = dtb if mid is None else dtb + mid(dlb)[0:1, 0:1]
    dxs, dbm, dcm, ddt, dproj, ddtb, dalog, ddsk, dnw = ssd_backward(xc, proj, st_b, dyb_in, dtb_after, alog, dsk, nw, dproj)
    dproj, dconv_w, dconv_b = conv_backward(proj, dxs, dbm, dcm, conv_w, conv_b, dproj)
    if late is not None:
        late(dconv_b)
    t = x.shape[0]
    tail = jnp.concatenate([jnp.sum(ddt, axis=0).astype(BF16), jnp.zeros((t, IN_PAD - COL_DT - LANES), BF16)], axis=1)
    dproj = lax.dynamic_update_slice(dproj, tail, (0, COL_DT))
    dw_in = matmul(u1, dproj, "tn", F32, "in_proj_dw")
    du1 = matmul(dproj, w_in, "nt", F32, "in_proj_dx", after=None if last is None else last(dw_in))

    def mod_bwd(r, c):
        _, vjp = jax.vjp(stage_modulate, r[0], *c)
        dx, dsc, dsh = vjp(r[1])
        return (dx + r[2],), (dsc, dsh)

    grad_x, dsc1, dsh1 = rowwise("modulate1_backward", mod_bwd, [_full(x), _full(du1), _full(dx_a)], [sc1, sh1],
                                 [(D, F32)], [vec, vec])
    d_mod = (dsh1, dsc1, dg1, dsh2, dsc2, dg2)
    d_wts = (dw_in, dw_a, dw_b, dw_o, dw_gu, dw_d)
    d_small = (dlb, dgn, dconv_w, dconv_b, jnp.sum(ddtb, axis=0),
               dalog.reshape(1, B_INNER), ddsk.reshape(1, B_INNER), dnw.reshape(1, B_INNER),
               dln1_g, dln1_b, dln2_g, dln2_b)
    return loss, grad_x, d_mod, d_wts, d_small


HBM = pl.BlockSpec(memory_space=pltpu.HBM)
SEM = pl.BlockSpec(memory_space=pltpu.SEMAPHORE)
DATAFLOW = pltpu.SideEffectType.DATAFLOW_SIDE_EFFECTING


def _place():
    return lax.axis_index("x"), lax.axis_index("y"), lax.axis_index("c")


def _other_chips(x, y):
    return [(1 - x, y), (x, 1 - y), (1 - x, 1 - y)]


def _remote(src, dst, send_sem, recv_sem, device):
    return pltpu.make_async_remote_copy(src_ref=src, dst_ref=dst, send_sem=send_sem, recv_sem=recv_sem,
                                        device_id=device, device_id_type=MESH)


def gather_rows(v, name):
    n = v.shape[1]

    def body(v_ref, out_ref, send_sems, recv_sems, local_sem):
        x, y, c = _place()
        mine = pltpu.make_async_copy(v_ref, out_ref.at[4 * x + 2 * y + c], local_sem)
        mine.start()
        sends, recvs = [], []
        for m in range(1, 8):
            px = 1 - x if m & 4 else x
            py = 1 - y if m & 2 else y
            pc = 1 - c if m & 1 else c
            sends.append(_remote(v_ref, out_ref.at[4 * x + 2 * y + c], send_sems.at[m - 1], recv_sems.at[m - 1], (px, py, pc)))
            recvs.append(_remote(v_ref, out_ref.at[4 * px + 2 * py + pc], send_sems.at[m - 1], recv_sems.at[m - 1], (px, py, pc)))
        for cp in sends:
            cp.start()
        for cp in recvs:
            cp.wait_recv()
        for cp in sends:
            cp.wait_send()
        mine.wait()

    return pl.pallas_call(
        body, name=name, in_specs=[HBM], out_specs=HBM,
        out_shape=jax.ShapeDtypeStruct((8, 1, n), v.dtype),
        scratch_shapes=[pltpu.SemaphoreType.DMA((7,)), pltpu.SemaphoreType.DMA((7,)), pltpu.SemaphoreType.DMA],
    )(v)


def exchange_rows(part, name):
    w = part.shape[2]

    def body(p_ref, out_ref, send_sems, recv_sems, local_sem):
        x, y, c = _place()
        k = 2 * x + y
        mine = pltpu.make_async_copy(p_ref.at[4 * x + 2 * y + c], out_ref.at[k], local_sem)
        mine.start()
        sends, recvs = [], []
        for j, (px, py) in enumerate(_other_chips(x, y)):
            sends.append(_remote(p_ref.at[4 * px + 2 * py + c], out_ref.at[k], send_sems.at[j], recv_sems.at[j], (px, py, c)))
            recvs.append(_remote(p_ref.at[4 * px + 2 * py + c], out_ref.at[2 * px + py], send_sems.at[j], recv_sems.at[j], (px, py, c)))
        for cp in sends:
            cp.start()
        for cp in recvs:
            cp.wait_recv()
        for cp in sends:
            cp.wait_send()
        mine.wait()

    return pl.pallas_call(
        body, name=name, in_specs=[HBM], out_specs=HBM,
        out_shape=jax.ShapeDtypeStruct((4, 1, w), part.dtype),
        scratch_shapes=[pltpu.SemaphoreType.DMA((3,)), pltpu.SemaphoreType.DMA((3,)), pltpu.SemaphoreType.DMA],
    )(part)


def _half_of_slot(ref, rows, px, py, pc):
    return ref.at[2 * px + py, pl.ds(pc * (rows // 2), rows // 2), :]


def gather_start(shards, after, tag):
    n = len(shards)

    def body(*refs):
        w_refs, land_refs = refs[:n], refs[n:2 * n]
        send_sems, recv_sems = refs[2 * n + 1], refs[2 * n + 2]
        token = refs[-1]
        x, y, c = _place()
        for i in range(n):
            rows = shards[i].shape[0]
            for j, (px, py) in enumerate(_other_chips(x, y)):
                _remote(w_refs[i].at[pl.ds(c * (rows // 2), rows // 2), :], _half_of_slot(land_refs[i], rows, x, y, c),
                        send_sems.at[j * n + i], recv_sems.at[j * n + i], (px, py, c)).start()
        token[...] = jnp.zeros_like(token)

    hbm = lambda a: pltpu.with_memory_space_constraint(a, pltpu.HBM)
    lands = [lax.empty((4,) + s.shape, s.dtype) for s in shards]
    dma = pltpu.SemaphoreType.DMA
    return pl.pallas_call(
        body, name="gather_start_" + tag,
        out_shape=(dma((3 * n,)), dma((3 * n,)),
                   *[pltpu.HBM(a.shape, a.dtype) for a in list(shards) + lands], jax.ShapeDtypeStruct((8, LANES), F32)),
        in_specs=[HBM] * (2 * n) + [pl.BlockSpec(memory_space=pl.ANY)],
        out_specs=(SEM, SEM, *[HBM] * (2 * n), pl.BlockSpec(memory_space=pltpu.VMEM)),
        input_output_aliases={i: 2 + i for i in range(2 * n)},
        compiler_params=pltpu.CompilerParams(has_side_effects=DATAFLOW),
    )(*[hbm(a) for a in list(shards) + lands], after)


def gather_wait(send_sems, recv_sems, shards, lands, after, tag):
    n = len(shards)

    def body(*refs):
        w_refs, land_refs = refs[:n], refs[n:2 * n]
        send_ref, recv_ref = refs[2 * n], refs[2 * n + 1]
        x, y, c = _place()
        for i in range(n):
            rows = shards[i].shape[0]
            for j, (px, py) in enumerate(_other_chips(x, y)):
                cp = _remote(w_refs[i].at[pl.ds(c * (rows // 2), rows // 2), :], _half_of_slot(land_refs[i], rows, px, py, c),
                             send_ref.at[j * n + i], recv_ref.at[j * n + i], (px, py, c))
                cp.wait_send()
                cp.wait_recv()

    out = pl.pallas_call(
        body, name="gather_wait_" + tag,
        out_shape=tuple(pltpu.HBM(a.shape, a.dtype) for a in list(shards) + list(lands)),
        in_specs=[HBM] * (2 * n) + [SEM, SEM, pl.BlockSpec(memory_space=pl.ANY)], out_specs=tuple([HBM] * (2 * n)),
        input_output_aliases={i: i for i in range(2 * n)},
        compiler_params=pltpu.CompilerParams(has_side_effects=DATAFLOW),
    )(*shards, *lands, send_sems, recv_sems, after)
    return list(out[:n]), list(out[n:])


def forward_start(lands, tag):
    n = len(lands)

    def body(*refs):
        land_refs = refs[:n]
        send_sems, recv_sems = refs[n], refs[n + 1]
        token = refs[-1]
        x, y, c = _place()
        for i in range(n):
            rows = lands[i].shape[1]
            for j, (px, py) in enumerate(_other_chips(x, y)):
                mine = _half_of_slot(land_refs[i], rows, px, py, c)
                _remote(mine, mine, send_sems.at[j * n + i], recv_sems.at[j * n + i], (x, y, 1 - c)).start()
        token[...] = jnp.zeros_like(token)

    dma = pltpu.SemaphoreType.DMA
    return pl.pallas_call(
        body, name="forward_start_" + tag,
        out_shape=(dma((3 * n,)), dma((3 * n,)), *[pltpu.HBM(a.shape, a.dtype) for a in lands],
                   jax.ShapeDtypeStruct((8, LANES), F32)),
        in_specs=[HBM] * n, out_specs=(SEM, SEM, *[HBM] * n, pl.BlockSpec(memory_space=pltpu.VMEM)),
        input_output_aliases={i: 2 + i for i in range(n)},
        compiler_params=pltpu.CompilerParams(has_side_effects=DATAFLOW),
    )(*lands)


def forward_wait(started, after, tag):
    send_sems, recv_sems, *rest = started
    lands = rest[:-1]
    n = len(lands)

    def body(*refs):
        land_refs = refs[:n]
        send_ref, recv_ref = refs[n], refs[n + 1]
        x, y, c = _place()
        for i in range(n):
            rows = lands[i].shape[1]
            for j, (px, py) in enumerate(_other_chips(x, y)):
                cp = _remote(_half_of_slot(land_refs[i], rows, px, py, c), _half_of_slot(land_refs[i], rows, px, py, 1 - c),
                             send_ref.at[j * n + i], recv_ref.at[j * n + i], (x, y, 1 - c))
                cp.wait_send()
                cp.wait_recv()

    out = pl.pallas_call(
        body, name="forward_wait_" + tag,
        out_shape=tuple(pltpu.HBM(a.shape, a.dtype) for a in lands),
        in_specs=[HBM] * n + [SEM, SEM, pl.BlockSpec(memory_space=pl.ANY)], out_specs=tuple([HBM] * n),
        input_output_aliases={i: i for i in range(n)},
        compiler_params=pltpu.CompilerParams(has_side_effects=DATAFLOW),
    )(*lands, send_sems, recv_sems, after)
    return list(out)


def pair_start(slabs, tag):
    n = len(slabs)

    def body(*refs):
        g_refs, land_refs = refs[:n], refs[n:2 * n]
        send_sems, recv_sems = refs[2 * n], refs[2 * n + 1]
        token = refs[-1]
        x, y, c = _place()
        for i in range(n):
            hr = slabs[i].shape[1] // 2
            _remote(g_refs[i].at[:, pl.ds((1 - c) * hr, hr), :], land_refs[i], send_sems.at[i], recv_sems.at[i],
                    (x, y, 1 - c)).start()
        token[...] = jnp.zeros_like(token)

    hbm = lambda a: pltpu.with_memory_space_constraint(a, pltpu.HBM)
    lands = [lax.empty((4, s.shape[1] // 2, s.shape[2]), s.dtype) for s in slabs]
    dma = pltpu.SemaphoreType.DMA
    return pl.pallas_call(
        body, name="pair_start_" + tag,
        out_shape=(dma((n,)), dma((n,)), *[pltpu.HBM(a.shape, a.dtype) for a in list(slabs) + lands],
                   jax.ShapeDtypeStruct((8, LANES), F32)),
        in_specs=[HBM] * (2 * n), out_specs=(SEM, SEM, *[HBM] * (2 * n), pl.BlockSpec(memory_space=pltpu.VMEM)),
        input_output_aliases={i: 2 + i for i in range(2 * n)},
        compiler_params=pltpu.CompilerParams(has_side_effects=DATAFLOW),
    )(*[hbm(a) for a in list(slabs) + lands])


def pair_wait(started, after, tag):
    send_sems, recv_sems, *rest = started
    n = (len(rest) - 1) // 2
    slabs, lands = rest[:n], rest[n:2 * n]

    def body(*refs):
        g_refs, land_refs = refs[:n], refs[n:2 * n]
        send_ref, recv_ref = refs[2 * n], refs[2 * n + 1]
        x, y, c = _place()
        for i in range(n):
            hr = slabs[i].shape[1] // 2
            cp = _remote(g_refs[i].at[:, pl.ds((1 - c) * hr, hr), :], land_refs[i], send_ref.at[i], recv_ref.at[i], (x, y, 1 - c))
            cp.wait_send()
            cp.wait_recv()

    out = pl.pallas_call(
        body, name="pair_wait_" + tag,
        out_shape=tuple(pltpu.HBM(a.shape, a.dtype) for a in list(slabs) + list(lands)),
        in_specs=[HBM] * (2 * n) + [SEM, SEM, pl.BlockSpec(memory_space=pl.ANY)], out_specs=tuple([HBM] * (2 * n)),
        input_output_aliases={i: i for i in range(2 * n)},
        compiler_params=pltpu.CompilerParams(has_side_effects=DATAFLOW),
    )(*slabs, *lands, send_sems, recv_sems, after)
    return list(out[:n]), list(out[n:])


def _tile2(rows, cols):
    fits = lambda r, c: r * c * 4 <= BLOCK_BYTES
    if fits(rows, cols):
        return rows, cols
    tiles = [(r, cols) for r in (1024, 512, 256, 128, 64) if rows % r == 0 and fits(r, cols)]
    tiles += [(rows, cols // k) for k in (2, 3, 4, 6, 8, 12, 16) if cols % (k * LANES) == 0 and fits(rows, cols // k)]
    return max(tiles, key=lambda t: t[0] * t[1])


def pair_add(g, p, c, name):
    _, hr, cols = p.shape
    tm, tc = _tile2(hr, cols)
    per = hr // tm

    def body(c_ref, g_ref, p_ref, o_ref):
        o_ref[...] = (g_ref[...] + p_ref[...]).astype(o_ref.dtype)

    return pl.pallas_call(
        body, name=name,
        grid_spec=pltpu.PrefetchScalarGridSpec(
            num_scalar_prefetch=1, grid=(4, per, cols // tc),
            in_specs=[pl.BlockSpec((None, tm, tc), lambda k, i, j, c_ref: (k, c_ref[0] * per + i, j)),
                      pl.BlockSpec((None, tm, tc), lambda k, i, j, c_ref: (k, i, j))],
            out_specs=pl.BlockSpec((None, tm, tc), lambda k, i, j, c_ref: (k, i, j))),
        out_shape=jax.ShapeDtypeStruct((4, hr, cols), BF16),
        compiler_params=_params(("arbitrary", "arbitrary", "arbitrary")),
    )(c.reshape(1).astype(jnp.int32), g, p)


def scatter_start(sums, tag):
    n = len(sums)

    def body(*refs):
        s_refs, land_refs = refs[:n], refs[n:2 * n]
        send_sems, recv_sems = refs[2 * n], refs[2 * n + 1]
        token = refs[-1]
        x, y, c = _place()
        k = 2 * x + y
        for i in range(n):
            for j, (px, py) in enumerate(_other_chips(x, y)):
                _remote(s_refs[i].at[2 * px + py], land_refs[i].at[k], send_sems.at[j * n + i], recv_sems.at[j * n + i],
                        (px, py, c)).start()
        token[...] = jnp.zeros_like(token)

    hbm = lambda a: pltpu.with_memory_space_constraint(a, pltpu.HBM)
    return pl.pallas_call(
        body, name="scatter_start_" + tag,
        out_shape=(pltpu.SemaphoreType.DMA((3 * n,)), pltpu.SemaphoreType.DMA((3 * n,)),
                   *[pltpu.HBM(s.shape, s.dtype) for s in sums], *[pltpu.HBM(s.shape, s.dtype) for s in sums],
                   jax.ShapeDtypeStruct((8, LANES), F32)),
        in_specs=[HBM] * (2 * n), out_specs=(SEM, SEM, *[HBM] * (2 * n), pl.BlockSpec(memory_space=pltpu.VMEM)),
        input_output_aliases={i: 2 + i for i in range(2 * n)},
        compiler_params=pltpu.CompilerParams(has_side_effects=DATAFLOW),
    )(*[hbm(s) for s in sums], *[hbm(lax.empty(s.shape, s.dtype)) for s in sums])


def scatter_wait(started, after, tag):
    send_sems, recv_sems, *rest = started
    n = (len(rest) - 1) // 2
    sums, lands = rest[:n], rest[n:2 * n]

    def body(*refs):
        s_refs, land_refs = refs[:n], refs[n:2 * n]
        send_ref, recv_ref = refs[2 * n], refs[2 * n + 1]
        x, y, c = _place()
        for i in range(n):
            for j, (px, py) in enumerate(_other_chips(x, y)):
                cp = _remote(s_refs[i].at[2 * px + py], land_refs[i].at[2 * px + py], send_ref.at[j * n + i],
                             recv_ref.at[j * n + i], (px, py, c))
                cp.wait_send()
                cp.wait_recv()

    out = pl.pallas_call(
        body, name="scatter_wait_" + tag,
        out_shape=tuple(pltpu.HBM(s.shape, s.dtype) for s in sums + lands),
        in_specs=[HBM] * (2 * n) + [SEM, SEM, pl.BlockSpec(memory_space=pl.ANY)], out_specs=tuple([HBM] * (2 * n)),
        input_output_aliases={i: i for i in range(2 * n)},
        compiler_params=pltpu.CompilerParams(has_side_effects=DATAFLOW),
    )(*sums, *lands, send_sems, recv_sems, after)
    return list(out[:n]), list(out[n:])


def sum_chips(landed, own, chip, core, name):
    _, hr, cols = landed.shape
    tm, tc = _tile2(hr, cols)
    per = hr // tm

    def body(idx_ref, l0, l1, l2, l3, own_ref, o_ref):
        mine = own_ref[...].astype(F32)
        v = [jnp.where(idx_ref[0] == k, mine, ref[...].astype(F32)) for k, ref in enumerate((l0, l1, l2, l3))]
        o_ref[...] = ((v[0] + v[1]) + v[2]) + v[3]

    slot = lambda k: pl.BlockSpec((None, tm, tc),
                                  lambda i, j, idx: (jnp.where(idx[0] == k, (k + 1) & 3, k), i, j))
    return pl.pallas_call(
        body, name=name,
        grid_spec=pltpu.PrefetchScalarGridSpec(
            num_scalar_prefetch=1, grid=(per, cols // tc),
            in_specs=[slot(0), slot(1), slot(2), slot(3),
                      pl.BlockSpec((None, tm, tc), lambda i, j, idx: (idx[0], i, j))],
            out_specs=pl.BlockSpec((tm, tc), lambda i, j, idx: (idx[1] * per + i, j))),
        out_shape=jax.ShapeDtypeStruct((2 * hr, cols), F32),
        compiler_params=_params(("arbitrary", "arbitrary")),
    )(jnp.stack([chip, core]).astype(jnp.int32), landed, landed, landed, landed, own)


def exchange_halves(bufs):
    n = len(bufs)

    def body(*refs):
        out_refs = refs[n:2 * n]
        send_sems, recv_sems = refs[2 * n:]
        x, y, c = _place()
        sends, recvs = [], []
        for i in range(n):
            hr = bufs[i].shape[0] // 2
            own = out_refs[i].at[pl.ds(c * hr, hr), :]
            other = out_refs[i].at[pl.ds((1 - c) * hr, hr), :]
            sends.append(_remote(own, own, send_sems.at[i], recv_sems.at[i], (x, y, 1 - c)))
            recvs.append(_remote(other, other, send_sems.at[i], recv_sems.at[i], (x, y, 1 - c)))
        for cp in sends:
            cp.start()
        for cp in recvs:
            cp.wait_recv()
        for cp in sends:
            cp.wait_send()

    return pl.pallas_call(
        body, name="exchange_halves", in_specs=[HBM] * n, out_specs=[HBM] * n,
        out_shape=[jax.ShapeDtypeStruct(b.shape, b.dtype) for b in bufs],
        input_output_aliases={i: i for i in range(n)},
        scratch_shapes=[pltpu.SemaphoreType.DMA((n,)), pltpu.SemaphoreType.DMA((n,))],
    )(*bufs)


def assemble_in_proj(landed, own, chip):
    rows, cols = 128, own.shape[1]

    def body(idx_ref, l0, l1, l2, l3, own_ref, o_ref):
        mine = own_ref[...]
        w = jnp.concatenate([jnp.where(idx_ref[0] == k, mine, ref[...]) for k, ref in enumerate((l0, l1, l2, l3))], axis=1)
        o_ref[...] = jnp.concatenate([w[:, :ORIG_Z], w[:, ORIG_GA:], w[:, ORIG_XBC:ORIG_DT], w[:, ORIG_Z:ORIG_XBC],
                                      w[:, ORIG_DT:ORIG_GA], jnp.zeros((rows, IN_PAD - IN_ORIG), w.dtype)], axis=1)

    slot = lambda k: pl.BlockSpec((None, rows, cols), lambda i, idx: (jnp.where(idx[0] == k, (k + 1) & 3, k), i, 0))
    return pl.pallas_call(
        body, name="assemble_in_proj",
        grid_spec=pltpu.PrefetchScalarGridSpec(
            num_scalar_prefetch=1, grid=(D // rows,),
            in_specs=[slot(0), slot(1), slot(2), slot(3), pl.BlockSpec((rows, cols), lambda i, idx: (i, 0))],
            out_specs=pl.BlockSpec((rows, IN_PAD), lambda i, idx: (i, 0))),
        out_shape=jax.ShapeDtypeStruct((D, IN_PAD), own.dtype),
        compiler_params=_params(("arbitrary",)),
    )(chip.reshape(1).astype(jnp.int32), landed, landed, landed, landed, own)


def rows_exchange(a, name):
    hr = a.shape[0] // 2

    def body(a_ref, out_ref, send_sem, recv_sem):
        x, y, c = _place()
        cp = _remote(a_ref.at[pl.ds((1 - c) * hr, hr), :], out_ref, send_sem, recv_sem, (x, y, 1 - c))
        cp.start()
        cp.wait()

    return pl.pallas_call(
        body, name=name, in_specs=[HBM], out_specs=HBM,
        out_shape=jax.ShapeDtypeStruct((hr, a.shape[1]), a.dtype),
        scratch_shapes=[pltpu.SemaphoreType.DMA, pltpu.SemaphoreType.DMA],
    )(a)


def split_pair_add(dw, received, core):
    cols = IN_ORIG // 4
    rows, hr = 128, D // 2
    per = hr // rows

    def body(c_ref, own_ref, got_ref, o_ref):
        d = own_ref[...] + got_ref[...]
        w = jnp.concatenate([d[:, :COL_GA], d[:, COL_Z:COL_DT], d[:, COL_XBC:COL_Z], d[:, COL_DT:COL_DT + 32],
                             d[:, COL_GA:COL_XBC]], axis=1)
        for k in range(4):
            o_ref[k] = w[:, k * cols:(k + 1) * cols].astype(o_ref.dtype)

    return pl.pallas_call(
        body, name="split_pair_add",
        grid_spec=pltpu.PrefetchScalarGridSpec(
            num_scalar_prefetch=1, grid=(per,),
            in_specs=[pl.BlockSpec((rows, IN_PAD), lambda i, c_ref: (c_ref[0] * per + i, 0)),
                      pl.BlockSpec((rows, IN_PAD), lambda i, c_ref: (i, 0))],
            out_specs=pl.BlockSpec((4, rows, cols), lambda i, c_ref: (0, i, 0))),
        out_shape=jax.ShapeDtypeStruct((4, hr, cols), BF16),
        compiler_params=_params(("arbitrary",)),
    )(core.reshape(1).astype(jnp.int32), dw, received)


def ada_prepare(c_all, w_ada, hgrn_lb):
    def body(c_ref, w_ref, lb_ref, mod_ref, row_ref):
        mod_ref[...] = hdot(silu(c_ref[...]), w_ref[...])
        row_ref[...] = sigmoid(lb_ref[0:1, :] - lb_ref[1:2, :])

    return pl.pallas_call(
        body, name="ada_prepare",
        out_shape=[jax.ShapeDtypeStruct((8, w_ada.shape[1]), F32), jax.ShapeDtypeStruct((1, D), F32)],
        compiler_params=pltpu.CompilerParams(vmem_limit_bytes=VMEM_LIMIT),
    )(c_all, w_ada, hgrn_lb)


SMALL_SEGS = (("mod", 6 * D), ("lb", D), ("gnorm", LANES), ("conv_w", 4 * CONV_DIM), ("conv_b", CONV_DIM),
              ("dt_bias", LANES), ("a_log", B_INNER), ("d", B_INNER), ("ssm_norm", B_INNER),
              ("ln1_g", D), ("ln1_b", D), ("ln2_g", D), ("ln2_b", D), ("loss", LANES))
SMALL_PARAMS = ("b_ada", "hgrn_lb", "hgrn_gnorm", "ssm_conv_b", "ssm_dt_bias", "ssm_a_log", "ssm_d", "ssm_norm",
                "ln1_g", "ln1_b", "ln2_g", "ln2_b")


def finalize_small(g_all, c_all, dmod_cols, params, m, v):
    n_p = len(SMALL_PARAMS)
    offs, o = {}, 0
    for nm, width in SMALL_SEGS:
        offs[nm] = (o, width)
        o += width

    def body(*refs):
        g_ref, c_ref, dm_ref = refs[:3]
        p_refs = refs[3:3 + n_p]
        m_refs = refs[3 + n_p:3 + 2 * n_p]
        v_refs = refs[3 + 2 * n_p:3 + 3 * n_p]
        outs = refs[3 + 3 * n_p:]
        gwa_ref, gcw_ref, loss_ref = outs[:3]
        res = outs[3:]
        total = jnp.sum(g_ref[...], axis=0, keepdims=True)
        seg = lambda nm: total[:, offs[nm][0]:offs[nm][0] + offs[nm][1]]
        loss_ref[...] = seg("loss")
        gwa_ref[...] = hdot(silu(c_ref[...]), dm_ref[...], "tn")
        cw = seg("conv_w")
        for j in range(4):
            gcw_ref[j:j + 1, :] = cw[:, j * CONV_DIM:(j + 1) * CONV_DIM]
        hc = lax.broadcasted_iota(jnp.int32, (B_INNER, LANES), 0)
        hj = lax.broadcasted_iota(jnp.int32, (B_INNER, LANES), 1)
        per_head = ((hc >> 6) == hj).astype(F32)
        heads = lambda nm: hdot(jnp.broadcast_to(seg(nm), (8, B_INNER)), per_head)[0:1, 0:32]
        lbp = sigmoid(p_refs[1][0:1, :] - p_refs[1][1:2, :])
        g_row = seg("lb") * lbp * (1.0 - lbp)
        grads = {"b_ada": seg("mod"), "hgrn_gnorm": seg("gnorm"), "ssm_conv_b": seg("conv_b"),
                 "ssm_dt_bias": seg("dt_bias")[:, 0:32], "ssm_a_log": heads("a_log"), "ssm_d": heads("d"),
                 "ssm_norm": seg("ssm_norm"), "ln1_g": seg("ln1_g"), "ln1_b": seg("ln1_b"),
                 "ln2_g": seg("ln2_g"), "ln2_b": seg("ln2_b")}
        for i, nm in enumerate(SMALL_PARAMS):
            g_out, d_out, m_out, v_out = res[4 * i:4 * i + 4]
            if nm == "hgrn_lb":
                for row, gv in ((0, g_row), (1, -g_row)):
                    sl = slice(row, row + 1)
                    dl, mn, vn = adamw(p_refs[i][sl, :], gv, m_refs[i][sl, :], v_refs[i][sl, :])
                    g_out[sl, :], d_out[sl, :], m_out[sl, :], v_out[sl, :] = gv, dl, mn, vn
            else:
                gv = grads[nm]
                dl, mn, vn = adamw(p_refs[i][...], gv, m_refs[i][...], v_refs[i][...])
                g_out[...], d_out[...], m_out[...], v_out[...] = gv, dl, mn, vn

    out_shape = [jax.ShapeDtypeStruct((D, dmod_cols.shape[1]), F32), jax.ShapeDtypeStruct((4, CONV_DIM), F32),
                 jax.ShapeDtypeStruct((1, LANES), F32)]
    for p in params:
        out_shape += [jax.ShapeDtypeStruct(p.shape, F32)] * 4
    return pl.pallas_call(
        body, name="finalize_small", out_shape=out_shape,
        compiler_params=pltpu.CompilerParams(vmem_limit_bytes=VMEM_LIMIT),
    )(g_all, c_all, dmod_cols, *params, *m, *v)


def adam_update(w, g, m, v, name):
    rows, cols = w.shape
    tm, tc = _tile2(rows, cols)

    def body(w_ref, g_ref, m_ref, v_ref, d_ref, mo_ref, vo_ref):
        d_ref[...], mo_ref[...], vo_ref[...] = adamw(w_ref[...], g_ref[...], m_ref[...], v_ref[...])

    spec = pl.BlockSpec((tm, tc), lambda i, j: (i, j))
    return pl.pallas_call(
        body, name=name, grid=(rows // tm, cols // tc), in_specs=[spec] * 4, out_specs=[spec] * 3,
        out_shape=[jax.ShapeDtypeStruct((rows, cols), F32)] * 3,
        compiler_params=_params(("arbitrary", "arbitrary")),
    )(w, g, m, v)


def kernel(x, c, w_ada, b_ada, w_in, hgrn_lb, hgrn_gnorm, ssm_conv_w, ssm_conv_b, ssm_dt_bias, ssm_a_log, ssm_d, ssm_norm, w_branch_a, w_branch_b, w_o, ln1_g, ln1_b, w_ffn_gate, w_ffn_up, w_ffn_down, ln2_g, ln2_b, loss_target, m_w_ada, m_b_ada, m_w_in, m_hgrn_lb, m_hgrn_gnorm, m_ssm_conv_w, m_ssm_conv_b, m_ssm_dt_bias, m_ssm_a_log, m_ssm_d, m_ssm_norm, m_w_branch_a, m_w_branch_b, m_w_o, m_ln1_g, m_ln1_b, m_w_ffn_gate, m_w_ffn_up, m_w_ffn_down, m_ln2_g, m_ln2_b, v_w_ada, v_b_ada, v_w_in, v_hgrn_lb, v_hgrn_gnorm, v_ssm_conv_w, v_ssm_conv_b, v_ssm_dt_bias, v_ssm_a_log, v_ssm_d, v_ssm_norm, v_w_branch_a, v_w_branch_b, v_w_o, v_ln1_g, v_ln1_b, v_w_ffn_gate, v_w_ffn_up, v_w_ffn_down, v_ln2_g, v_ln2_b):
    given = dict(locals())
    chip = 2 * lax.axis_index("x") + lax.axis_index("y")
    core = lax.axis_index("c")
    t = x.shape[1]

    first = gather_rows(jnp.concatenate([c, ssm_conv_w.reshape(1, CONV_DIM)], axis=1), "gather_cond").reshape(8, D + CONV_DIM)
    c_all = first[:, :D]
    conv_w = first[0::2, D:].reshape(4, 4, CONV_DIM // 4).transpose(1, 0, 2).reshape(4, CONV_DIM)
    mod_part, lb_row = ada_prepare(c_all, w_ada[0], hgrn_lb)
    mod_cols = w_ada.shape[2]
    mod_row = exchange_rows(mod_part.reshape(8, 1, mod_cols), "exchange_mod").reshape(1, 6 * D) + b_ada

    local = {nm: given[nm][0] for nm in SHARDED if nm != "w_ffn_in"}
    local["w_ffn_in"] = jnp.concatenate([w_ffn_gate[0].T, w_ffn_up[0].T], axis=0)
    shards = [local[nm].astype(BF16) for nm in SHARDED]
    send_in, recv_in, sent_in, land_in, started_in = gather_start(shards[:1], mod_row, "in")
    shards = shards[:1] + [(local[nm] + started_in[0, 0]).astype(BF16) for nm in SHARDED[1:]]
    send_rest, recv_rest, *flying = gather_start(shards[1:], started_in, "rest")
    n_rest = len(SHARDED) - 1
    sent_rest, land_rest, started_rest = flying[:n_rest], flying[n_rest:2 * n_rest], flying[-1]
    mod_row = mod_row + started_rest[0:1, 0:1]
    mod = tuple(mod_row[:, i * D:(i + 1) * D] for i in range(6))
    with_own = lambda land, shard: lax.dynamic_update_slice(land, shard[None], (chip, 0, 0))

    class Weights:
        def input_projection(self, after):
            (own,), land = gather_wait(send_in, recv_in, [sent_in], [land_in], after, "in")
            (land,) = forward_wait(forward_start(land, "in"), after, "in")
            return assemble_in_proj(land, own, chip)

        def start_rest(self, after):
            self.own, landed = gather_wait(send_rest, recv_rest, sent_rest, land_rest, after, "rest")
            self.started = forward_start(landed, "rest")
            return self.started[-1]

        def rest(self, after):
            got = {nm: with_own(land, s) for nm, land, s in zip(SHARDED[1:], forward_wait(self.started, after, "rest"), self.own, strict=True)}
            whole = lambda nm: got[nm].reshape(4 * got[nm].shape[1], got[nm].shape[2])
            return tuple(whole(nm) for nm in SHARDED[1:])

    wts = Weights()

    per_head = lambda p: jnp.pad(p, ((0, 0), (0, LANES - p.shape[1])))
    per_channel = lambda p: jnp.repeat(p[0], B_INNER // 32)[None]
    small = (lb_row, hgrn_gnorm, conv_w, ssm_conv_b, per_head(ssm_dt_bias), per_channel(ssm_a_log),
             per_channel(ssm_d), ssm_norm, ln1_g, ln1_b, ln2_g, ln2_b)
    by_rows = lambda g: g.reshape(4, g.shape[0] // 4, g.shape[1])
    travelling = {}

    def start_early(dws):
        travelling["pair"] = pair_start([by_rows(dw) for dw in dws], "early")
        return travelling["pair"][-1]

    def between_scans(after):
        slabs, received = pair_wait(travelling["pair"], after, "early")
        travelling["pairs"] = [pair_add(s, r, core, "pair_add_" + nm) for nm, s, r in zip(SHARDED[1:], slabs, received, strict=True)]
        travelling["started"] = scatter_start(travelling["pairs"], "early")
        return travelling["started"][-1]

    def finish_early(after):
        travelling["pairs"], travelling["landed"] = scatter_wait(travelling["started"], after, "early")

    def start_last(dw_in):
        travelling["pairs_in"] = [split_pair_add(dw_in, rows_exchange(dw_in, "pair_exchange_last"), core)]
        travelling["started_in"] = scatter_start(travelling["pairs_in"], "last")
        return travelling["started_in"][-1]

    loss, grad_x, d_mod, d_wts, d_small = local_step(x[0], loss_target[0], mod, wts, small,
                                                     start_early, between_scans, finish_early, start_last)

    d_lb, d_gn, d_cw, d_cb, d_dtb, d_alog, d_dsk, d_nw, d_l1g, d_l1b, d_l2g, d_l2b = d_small
    row = jnp.concatenate(list(d_mod) + [d_lb, d_gn, d_cw.reshape(1, 4 * CONV_DIM), d_cb, d_dtb, d_alog, d_dsk, d_nw,
                                          d_l1g, d_l1b, d_l2g, d_l2b, jnp.pad(loss, ((0, 0), (0, LANES - 1)))], axis=1)
    g_all = gather_rows(row, "gather_small_grads").reshape(8, row.shape[1])
    dmod_cols = lax.dynamic_slice_in_dim(g_all, chip * mod_cols, mod_cols, axis=1)
    fin = finalize_small(g_all, c_all, dmod_cols, [given[n] for n in SMALL_PARAMS],
                         [given["m_" + n] for n in SMALL_PARAMS], [given["v_" + n] for n in SMALL_PARAMS])
    grads, deltas, new_m, new_v = {}, {}, {}, {}
    grads["w_ada"] = fin[0][None]
    grads["ssm_conv_w"] = lax.dynamic_slice_in_dim(fin[1], chip * (CONV_DIM // 4), CONV_DIM // 4, axis=1)[None]
    for i, nm in enumerate(SMALL_PARAMS):
        grads[nm], deltas[nm], new_m[nm], new_v[nm] = fin[3 + 4 * i:7 + 4 * i]

    pairs_in, landed_in = scatter_wait(travelling["started_in"], fin[3], "last")
    pairs, landed = pairs_in + travelling["pairs"], landed_in + travelling["landed"]
    halves = [sum_chips(r, p, chip, core, "sum_chips_" + nm) for nm, r, p in zip(SHARDED, landed, pairs, strict=True)]
    reduced = dict(zip(SHARDED, exchange_halves(halves), strict=True))
    reduced["w_ada"], reduced["ssm_conv_w"] = grads["w_ada"][0], grads["ssm_conv_w"][0]
    reduced["w_in"] = reduced["w_in"].T
    reduced["w_ffn_gate"], reduced["w_ffn_up"] = reduced["w_ffn_in"][:FFN_SHARD], reduced["w_ffn_in"][FFN_SHARD:]
    for nm in ("w_ada", "ssm_conv_w", "w_in", "w_branch_a", "w_branch_b", "w_o", "w_ffn_gate", "w_ffn_up", "w_ffn_down"):
        flipped = nm in ("w_in", "w_ffn_gate", "w_ffn_up")
        work = (lambda a: a[0].T) if flipped else (lambda a: a[0])
        back = (lambda a: a.T[None]) if flipped else (lambda a: a[None])
        d_, m_, v_ = adam_update(work(given[nm]), reduced[nm], work(given["m_" + nm]), work(given["v_" + nm]), "adam_" + nm)
        grads[nm], deltas[nm], new_m[nm], new_v[nm] = back(reduced[nm]), back(d_), back(m_), back(v_)

    names = ("w_ada", "b_ada", "w_in", "hgrn_lb", "hgrn_gnorm", "ssm_conv_w", "ssm_conv_b", "ssm_dt_bias", "ssm_a_log",
             "ssm_d", "ssm_norm", "w_branch_a", "w_branch_b", "w_o", "ln1_g", "ln1_b", "w_ffn_gate", "w_ffn_up",
             "w_ffn_down", "ln2_g", "ln2_b")
    return (fin[2][0, 0], grad_x[None], *[grads[n] for n in names], *[deltas[n] for n in names],
            *[new_m[n] for n in names], *[new_v[n] for n in names])
```

```python
import functools

import jax
import jax.numpy as jnp
from jax import lax
from jax.experimental import pallas as pl
from jax.experimental.pallas import tpu as pltpu

F32, BF16 = jnp.float32, jnp.bfloat16
HI = lax.Precision.HIGHEST
MESH = pl.DeviceIdType.MESH

D = 1024
CHUNK = 64
LANES = 128
N_HEADS_A = 8
N_GROUPS_B = 4
B_INNER = 2048
CONV_DIM = 3072
D_FF = 2816
ALPHA = 2.0 ** 0.25
LN_EPS = 1e-5
RMS_EPS = 1e-6
ADAM_LR, ADAM_B1, ADAM_B2, ADAM_EPS, ADAM_WD, ADAM_STEP = 0.001, 0.9, 0.999, 1e-08, 0.01, 10

IN_ORIG = 11296
IN_PAD = 11520
COL_GA, COL_GB, COL_XBC, COL_Z, COL_DT = 4096, 5120, 6144, 9216, 11264
ORIG_Z, ORIG_XBC, ORIG_DT, ORIG_GA = 4096, 6144, 9216, 9248

SHARDED = ("w_in", "w_branch_a", "w_branch_b", "w_o", "w_ffn_in", "w_ffn_down")
FFN_SHARD = D_FF // 4
VMEM_LIMIT = 56 * 1024 * 1024
BLOCK_BYTES = 2 * 1024 * 1024
_DIMS = {"nn": (((1,), (0,)), ((), ())), "nt": (((1,), (1,)), ((), ())), "tn": (((0,), (0,)), ((), ()))}


def _bd(a, b, mode):
    return lax.dot_general(a.astype(BF16), b.astype(BF16), _DIMS[mode], preferred_element_type=F32)


@functools.partial(jax.custom_vjp, nondiff_argnums=(2,))
def bdot(a, b, mode):
    return _bd(a, b, mode)


def _bdot_fwd(a, b, mode):
    return _bd(a, b, mode), (a, b)


def _bdot_bwd(mode, res, g):
    a, b = res
    if mode == "nn":
        return _bd(g, b, "nt"), _bd(a, g, "tn")
    if mode == "nt":
        return _bd(g, b, "nn"), _bd(g, a, "tn")
    return _bd(b, g, "nt"), _bd(a, g, "nn")


bdot.defvjp(_bdot_fwd, _bdot_bwd)


def hdot(a, b, mode="nn"):
    return lax.dot_general(a, b, _DIMS[mode], precision=HI, preferred_element_type=F32)


def _raw(a, b, mode):
    return lax.dot_general(a, b, _DIMS[mode], preferred_element_type=F32)


def _split(x, n):
    parts, rest = [], x
    for _ in range(n):
        p = rest.astype(BF16)
        parts.append(p)
        rest = rest - p.astype(F32)
    return parts


def _od(a, b, mode, exact):
    if exact == 1:
        e = b.astype(BF16)
        p = _split(a, 3)
        return (_raw(p[2], e, mode) + _raw(p[1], e, mode)) + _raw(p[0], e, mode)
    e = a.astype(BF16)
    p = _split(b, 3)
    return (_raw(e, p[2], mode) + _raw(e, p[1], mode)) + _raw(e, p[0], mode)


@functools.partial(jax.custom_vjp, nondiff_argnums=(2, 3))
def odot(a, b, mode, exact):
    return _od(a, b, mode, exact)


def _odot_fwd(a, b, mode, exact):
    return _od(a, b, mode, exact), (a, b)


def _odot_bwd(mode, exact, res, g):
    a, b = res
    if exact == 1:
        da = {"nn": lambda: _od(g, b, "nt", 1), "nt": lambda: _od(g, b, "nn", 1), "tn": lambda: _od(b, g, "nt", 0)}[mode]()
        return da, jnp.zeros_like(b)
    db = {"nn": lambda: _od(a, g, "tn", 0), "nt": lambda: _od(g, a, "tn", 1), "tn": lambda: _od(a, g, "nn", 0)}[mode]()
    return jnp.zeros_like(a), db


odot.defvjp(_odot_fwd, _odot_bwd)


_BDIMS = {"bnn": (((2,), (1,)), ((0,), (0,))), "bnt": (((2,), (2,)), ((0,), (0,))), "btn": (((1,), (1,)), ((0,), (0,)))}


def _braw(a, b, mode):
    return lax.dot_general(a, b, _BDIMS[mode], preferred_element_type=F32)


def _bdb(a, b, mode):
    return _braw(a.astype(BF16), b.astype(BF16), mode)


def _d3b(a, b, mode):
    ah, al = _split(a, 2)
    bh, bl = _split(b, 2)
    return _braw(ah, bh, mode) + (_braw(ah, bl, mode) + _braw(al, bh, mode))


def _batched_bwd(f):
    def bwd(mode, res, g):
        a, b = res
        if mode == "bnn":
            return f(g, b, "bnt"), f(a, g, "btn")
        if mode == "bnt":
            return f(g, b, "bnn"), f(g, a, "btn")
        return f(b, g, "bnt"), f(a, g, "bnn")
    return bwd


@functools.partial(jax.custom_vjp, nondiff_argnums=(2,))
def bdot_b(a, b, mode):
    return _bdb(a, b, mode)


bdot_b.defvjp(lambda a, b, mode: (_bdb(a, b, mode), (a, b)), _batched_bwd(_bdb))


@functools.partial(jax.custom_vjp, nondiff_argnums=(2,))
def dot3_b(a, b, mode):
    return _d3b(a, b, mode)


dot3_b.defvjp(lambda a, b, mode: (_d3b(a, b, mode), (a, b)), _batched_bwd(_d3b))


def _cum(tril3, x, mode):
    e = tril3.astype(BF16)
    p = _split(x, 3)
    return (_braw(e, p[2], mode) + _braw(e, p[1], mode)) + _braw(e, p[0], mode)


@jax.custom_vjp
def chunk_cumsum(tril3, x):
    return _cum(tril3, x, "bnn")


chunk_cumsum.defvjp(lambda t, x: (_cum(t, x, "bnn"), t), lambda t, g: (jnp.zeros_like(t), _cum(t, g, "btn")))


def _unstack(axis, n):
    @jax.custom_vjp
    def un(x):
        return tuple(lax.index_in_dim(x, i, axis, keepdims=False) for i in range(n))

    un.defvjp(lambda x: (un(x), None), lambda _, g: (jnp.stack(g, axis=axis),))
    return un


def _split_last(n, w):
    @jax.custom_vjp
    def sp(x):
        return tuple(x[..., i * w:(i + 1) * w] for i in range(n))

    sp.defvjp(lambda x: (sp(x), None), lambda _, g: (jnp.concatenate(g, axis=-1),))
    return sp


def sigmoid(x):
    return 0.5 * jnp.tanh(0.5 * x) + 0.5


def silu(x):
    return x * sigmoid(x)


def softplus(x):
    return jnp.maximum(x, 0.0) + jnp.log1p(jnp.exp(jnp.minimum(x, -x)))


def _ln(x):
    mu = jnp.mean(x, axis=-1, keepdims=True)
    xc = x - mu
    return xc * lax.rsqrt(jnp.mean(xc * xc, axis=-1, keepdims=True) + LN_EPS)


def _tril64():
    r = lax.broadcasted_iota(jnp.int32, (CHUNK, CHUNK), 0)
    c = lax.broadcasted_iota(jnp.int32, (CHUNK, CHUNK), 1)
    return (r >= c).astype(F32)


def hgrn_block(q, fl, iv, gr, st, lb, gn):
    tb = q.shape[0]
    nc = tb // CHUNK
    nh = N_HEADS_A
    heads = _split_last(nh, LANES)
    to4 = lambda a: jnp.stack(heads(a), axis=0).reshape(nh, nc, CHUNK, LANES)
    flat = lambda a: a.reshape(nh * nc, CHUNK, LANES)
    f = lb + (1.0 - lb) * sigmoid(fl)
    gl4, k4, qf4, v4, gr4 = to4(jnp.log(f)), to4(1.0 - f), to4(silu(q) * (128 ** -0.5)), to4(iv), to4(gr)
    tril = _tril64()
    b4 = chunk_cumsum(jnp.broadcast_to(tril[None], (nh * nc, CHUNK, CHUNK)), flat(gl4)).reshape(gl4.shape)
    blast = jnp.sum(gl4, axis=2, keepdims=True)
    ref = lax.stop_gradient(0.5 * blast)
    qp, kp = qf4 * jnp.exp(b4 - ref), k4 * jnp.exp(ref - b4)
    sc = dot3_b(flat(qp), flat(kp), "bnt") * tril
    o_intra = bdot_b(sc, flat(v4), "bnn").reshape(gl4.shape)
    chunks = _unstack(1, nc)
    qe, v_c, kd, dec = chunks(qp * jnp.exp(ref)), chunks(v4), chunks(kp * jnp.exp(blast - ref)), chunks(jnp.exp(blast))
    o_inter = []
    for c in range(nc):
        o_inter.append(bdot_b(qe[c], st, "bnt"))
        st = st * dec[c] + bdot_b(v_c[c], kd[c], "btn")
    o = o_intra + jnp.stack(o_inter, axis=1)
    on = o * lax.rsqrt(jnp.mean(o * o, axis=-1, keepdims=True) + RMS_EPS) * gn
    out = (on * silu(gr4)).reshape(nh, tb, LANES)
    return jnp.concatenate(_unstack(0, nh)(out), axis=1), st


def ssd_consts(g):
    i32 = jnp.int32
    ej = lax.broadcasted_iota(i32, (LANES, 512), 0)
    ec = lax.broadcasted_iota(i32, (LANES, 512), 1)
    expand = (ej == g * 8 + (ec >> 6)).astype(F32)
    ts = lax.broadcasted_iota(i32, (CHUNK, 512), 0)
    tc = lax.broadcasted_iota(i32, (CHUNK, 512), 1)
    itile = (ts == (tc & 63)).astype(F32)
    maskall = ts >= (tc & 63)
    br = lax.broadcasted_iota(i32, (LANES, LANES), 0)
    bc = lax.broadcasted_iota(i32, (LANES, LANES), 1)
    blockmask = ((br >> 6) == (bc >> 6)).astype(F32)
    return expand, itile, maskall, blockmask, _tril64()


def ssd_block(x, bm, cm, dt, z, st, dtb, alog, dsk, nw, cs):
    expand, itile, maskall, blockmask, tril = cs
    tb = x.shape[0]
    nc = tb // CHUNK
    delta = odot(softplus(dt + dtb), expand, "nn", 1)
    a = -jnp.exp(alog) * delta
    xdt = x * delta
    by_chunk = lambda v: v.reshape(nc, CHUNK, v.shape[-1])
    a3, xdt3, bm3, cm3 = by_chunk(a), by_chunk(xdt), by_chunk(bm), by_chunk(cm)
    acum3 = chunk_cumsum(jnp.broadcast_to(tril[None], (nc, CHUNK, CHUNK)), a3)
    alast3 = jnp.sum(a3, axis=1, keepdims=True)
    cb3 = bdot_b(cm3, jnp.concatenate([bm3] * 8, axis=1), "bnt")
    arow3 = jnp.sum(acum3 * itile, axis=1, keepdims=True)
    dec3 = jnp.exp(jnp.where(maskall, acum3 - arow3, -1e30))
    pairs = _split_last(4, LANES)
    intra = [bdot_b(m, jnp.concatenate([xp] * 2, axis=1) * blockmask, "bnn")
             for m, xp in zip(pairs(cb3 * dec3), pairs(xdt3))]
    chunks = _unstack(0, nc)
    cm_c, bm_c, xw_c, dec_c = chunks(cm3), chunks(bm3), chunks(xdt3 * jnp.exp(alast3 - acum3)), chunks(jnp.exp(alast3))
    inter = []
    for c in range(nc):
        inter.append(bdot(cm_c[c], st, "nn"))
        st = st * dec_c[c] + bdot(bm_c[c], xw_c[c], "tn")
    st_new = st
    y = (jnp.concatenate(intra, axis=-1) + jnp.stack(inter, axis=0) * jnp.exp(acum3)).reshape(tb, 512)
    yz = (y + x * dsk) * silu(z)
    return yz * lax.rsqrt(jnp.mean(yz * yz, axis=-1, keepdims=True) + RMS_EPS) * nw, st_new


def adamw(w, g, m, v):
    m = ADAM_B1 * m + (1.0 - ADAM_B1) * g
    v = ADAM_B2 * v + (1.0 - ADAM_B2) * jnp.square(g)
    m_hat = m / (1.0 - ADAM_B1 ** ADAM_STEP)
    v_hat = v / (1.0 - ADAM_B2 ** ADAM_STEP)
    return -ADAM_LR * (m_hat / (jnp.sqrt(v_hat) + ADAM_EPS) + ADAM_WD * w), m, v


def _pick(n, cands):
    for c in cands:
        if n % c == 0:
            return c
    return n


def _params(sem):
    return pltpu.CompilerParams(dimension_semantics=sem, vmem_limit_bytes=VMEM_LIMIT)


MATMUL_VMEM_BUDGET = 50 * 1024 * 1024
MATMUL_MIN_STEPS = 4


def matmul(a, b, mode, out_dtype, name, after=None):
    if mode == "nn":
        (m, k), n = a.shape, b.shape[1]
    elif mode == "nt":
        (m, k), n = a.shape, b.shape[0]
    else:
        (k, m), n = a.shape, b.shape[1]
    a_bytes, b_bytes, out_bytes = a.dtype.itemsize, b.dtype.itemsize, jnp.dtype(out_dtype).itemsize
    k_sizes = (2304, 2048, 1408, 1024, 768, 512, 256, 128)
    usual_tk = _pick(k, k_sizes)

    def vmem(tm_, tn_, tk_):
        blocks = 2 * (tm_ * tk_ * a_bytes + tk_ * tn_ * b_bytes + tm_ * tn_ * out_bytes)
        return blocks + (tm_ * tn_ * 4 if tk_ < k else 0)

    def traffic(tm_, tn_, tk_):
        return (m // tm_) * k * n * b_bytes + (n // tn_ if tk_ < k else 1) * m * k * a_bytes

    sizes = (2304, 2048, 1920, 1408, 1024, 768, 512, 256, 128)
    tiles = [(tm_, tn_, tk_) for tm_ in sizes if m % tm_ == 0 for tn_ in sizes if n % tn_ == 0
             for tk_ in {k, usual_tk} if vmem(tm_, tn_, tk_) <= MATMUL_VMEM_BUDGET] or [(m, n, k)]
    pipelined = [t for t in tiles if (m // t[0]) * (n // t[1]) * (k // t[2]) >= MATMUL_MIN_STEPS]
    tm, tn, tk = min(pipelined or tiles, key=lambda t: (traffic(*t), t[2] != usual_tk, -t[0] * t[1]))
    nk = k // tk
    a_spec = pl.BlockSpec((tk, tm), lambda i, j, kk: (kk, i)) if mode == "tn" else pl.BlockSpec((tm, tk), lambda i, j, kk: (i, kk))
    b_spec = pl.BlockSpec((tn, tk), lambda i, j, kk: (j, kk)) if mode == "nt" else pl.BlockSpec((tk, tn), lambda i, j, kk: (kk, j))

    order = [] if after is None else [after]

    def body(a_ref, b_ref, *rest):
        o_ref, *acc = rest[len(order):]
        part = _bd(a_ref[...], b_ref[...], mode)
        if nk == 1:
            o_ref[...] = part.astype(o_ref.dtype)
            return
        acc_ref, = acc
        kk = pl.program_id(2)

        @pl.when(kk == 0)
        def _():
            acc_ref[...] = part

        @pl.when(jnp.logical_and(kk > 0, kk < nk - 1))
        def _():
            acc_ref[...] += part

        @pl.when(kk == nk - 1)
        def _():
            o_ref[...] = (acc_ref[...] + part).astype(o_ref.dtype)

    return pl.pallas_call(
        body, name=name, grid=(m // tm, n // tn, nk),
        in_specs=[a_spec, b_spec] + [pl.BlockSpec(memory_space=pl.ANY) for _ in order],
        out_specs=pl.BlockSpec((tm, tn), lambda i, j, kk: (i, j)),
        out_shape=jax.ShapeDtypeStruct((m, n), out_dtype),
        scratch_shapes=[pltpu.VMEM((tm, tn), F32)] if nk > 1 else [],
        compiler_params=_params(("parallel", "parallel", "arbitrary")),
    )(a, b, *order)


def rowwise(name, fn, rows, consts, out_rows, out_accs=(), tm_max=512, into=None, new_wide=None):
    t = rows[0][0].shape[0]
    tm = _pick(t, (tm_max, 128, 64, 32, 16, 8))
    n_r, n_c, n_o = len(rows), len(consts), len(out_rows)
    n_alias = 0 if into is None else 1

    def body(*refs):
        r_in = [r[...] for r in refs[:n_r]]
        c_in = [r[...] for r in refs[n_r:n_r + n_c]]
        refs = refs[:n_r + n_c] + refs[n_r + n_c + n_alias:]
        o_refs = refs[n_r + n_c:n_r + n_c + n_o]
        a_refs = refs[n_r + n_c + n_o:]
        ro, ao = fn(r_in, c_in)
        for ref, val in zip(o_refs, ro, strict=True):
            ref[...] = val.astype(ref.dtype)
        if a_refs:
            @pl.when(pl.program_id(0) == 0)
            def _():
                for ref in a_refs:
                    ref[...] = jnp.zeros_like(ref)

            for ref, val in zip(a_refs, ao, strict=True):
                ref[...] += val

    in_specs = [pl.BlockSpec((tm, w), functools.partial(lambda i, cb: (i, cb), cb=cb)) for _, w, cb in rows]
    in_specs += [pl.BlockSpec(c.shape, lambda i: (0, 0)) for c in consts]
    out_specs = [pl.BlockSpec((tm, w), lambda i: (i, 0)) for w, _ in out_rows]
    out_specs += [pl.BlockSpec(s, lambda i: (0, 0)) for s in out_accs]
    out_shape = [jax.ShapeDtypeStruct((t, w), dt) for w, dt in out_rows]
    out_shape += [jax.ShapeDtypeStruct(s, F32) for s in out_accs]
    operands = [r[0] for r in rows] + list(consts)
    aliases = {}
    if into is not None:
        target, cb = into
        in_specs.append(pl.BlockSpec(memory_space=pl.ANY))
        operands.append(target)
        out_specs[0] = pl.BlockSpec((tm, out_rows[0][0]), lambda i: (i, cb))
        out_shape[0] = jax.ShapeDtypeStruct(target.shape, target.dtype)
        aliases = {len(operands) - 1: 0}
    if new_wide is not None:
        width, cb = new_wide
        out_specs[0] = pl.BlockSpec((tm, out_rows[0][0]), lambda i: (i, cb))
        out_shape[0] = jax.ShapeDtypeStruct((t, width), out_rows[0][1])
    return pl.pallas_call(
        body, name=name, grid=(t // tm,), in_specs=in_specs, out_specs=out_specs, out_shape=out_shape,
        input_output_aliases=aliases, compiler_params=_params(("arbitrary",)),
    )(*operands)


def _full(a):
    return (a, a.shape[1], 0)


HGRN_TIME_BLOCK = 256
SSD_TIME_BLOCK = 512


def _time_block(t, most=HGRN_TIME_BLOCK):
    return _pick(t, tuple(b for b in (512, 256, 128, 64) if b <= most))


def _quarters(ref):
    return [ref[:, seg * D:(seg + 1) * D] for seg in range(4)]


def hgrn_forward(proj, lb, gn):
    t = proj.shape[0]
    tb = _time_block(t)
    nb = t // tb

    def body(qfig_ref, lb_ref, gn_ref, o_ref, st_ref, state):
        @pl.when(pl.program_id(0) == 0)
        def _():
            state[...] = jnp.zeros_like(state)

        st = state[...]
        st_ref[...] = st
        out, st_new = hgrn_block(*_quarters(qfig_ref), st, lb_ref[...], gn_ref[...])
        o_ref[...] = out.astype(o_ref.dtype)
        state[...] = st_new

    return pl.pallas_call(
        body, name="hgrn_forward", grid=(nb,),
        in_specs=[pl.BlockSpec((tb, 4 * D), lambda j: (j, 0)),
                  pl.BlockSpec((1, D), lambda j: (0, 0)), pl.BlockSpec((1, LANES), lambda j: (0, 0))],
        out_specs=[pl.BlockSpec((tb, D), lambda j: (j, 0)),
                   pl.BlockSpec((None, N_HEADS_A, LANES, LANES), lambda j: (j, 0, 0, 0))],
        out_shape=[jax.ShapeDtypeStruct((t, D), BF16),
                   jax.ShapeDtypeStruct((nb, N_HEADS_A, LANES, LANES), F32)],
        scratch_shapes=[pltpu.VMEM((N_HEADS_A, LANES, LANES), F32)],
        compiler_params=_params(("arbitrary",)),
    )(proj, lb, gn)


def hgrn_backward(proj, states, d_out, lb, gn, d_proj):
    t = proj.shape[0]
    tb = _time_block(t)
    nb = t // tb

    def body(qfig_ref, st_ref, do_ref, lb_ref, gn_ref, _, dqfig_ref, dlb_ref, dgn_ref, d_state):
        @pl.when(pl.program_id(0) == 0)
        def _():
            d_state[...] = jnp.zeros_like(d_state)
            dlb_ref[...] = jnp.zeros_like(dlb_ref)
            dgn_ref[...] = jnp.zeros_like(dgn_ref)

        _, vjp = jax.vjp(hgrn_block, *_quarters(qfig_ref), st_ref[...], lb_ref[...], gn_ref[...])
        dq, df, di, dg, dst, dlb, dgn = vjp((do_ref[...], d_state[...]))
        for seg, val in enumerate((dq, df, di, dg)):
            dqfig_ref[:, seg * D:(seg + 1) * D] = val.astype(dqfig_ref.dtype)
        d_state[...] = dst
        dlb_ref[...] += dlb
        dgn_ref[...] += dgn

    rev = lambda j: nb - 1 - j
    return pl.pallas_call(
        body, name="hgrn_backward", grid=(nb,),
        in_specs=[pl.BlockSpec((tb, 4 * D), lambda j: (rev(j), 0)),
                  pl.BlockSpec((None, N_HEADS_A, LANES, LANES), lambda j: (rev(j), 0, 0, 0)),
                  pl.BlockSpec((tb, D), lambda j: (rev(j), 0)),
                  pl.BlockSpec((1, D), lambda j: (0, 0)), pl.BlockSpec((1, LANES), lambda j: (0, 0)),
                  pl.BlockSpec(memory_space=pl.ANY)],
        out_specs=[pl.BlockSpec((tb, 4 * D), lambda j: (rev(j), 0)),
                   pl.BlockSpec((1, D), lambda j: (0, 0)), pl.BlockSpec((1, LANES), lambda j: (0, 0))],
        out_shape=[jax.ShapeDtypeStruct(d_proj.shape, d_proj.dtype), jax.ShapeDtypeStruct((1, D), F32),
                   jax.ShapeDtypeStruct((1, LANES), F32)],
        input_output_aliases={5: 0},
        scratch_shapes=[pltpu.VMEM((N_HEADS_A, LANES, LANES), F32)],
        compiler_params=_params(("arbitrary",)),
    )(proj, states, d_out, lb, gn, d_proj)


def _ssd_in_specs(tb, tmap):
    return [pl.BlockSpec((tb, 512), lambda g, j: (tmap(j), g)),
            pl.BlockSpec((tb, LANES), lambda g, j: (tmap(j), 16 + g)),
            pl.BlockSpec((tb, LANES), lambda g, j: (tmap(j), 20 + g)),
            pl.BlockSpec((tb, LANES), lambda g, j: (tmap(j), COL_DT // LANES)),
            pl.BlockSpec((tb, 512), lambda g, j: (tmap(j), COL_Z // 512 + g))]


def ssd_forward(xc, proj, dtb, alog, dsk, nw):
    t = proj.shape[0]
    tb = _time_block(t, SSD_TIME_BLOCK)
    nb = t // tb

    def body(x_ref, b_ref, c_ref, dt_ref, z_ref, dtb_ref, alog_ref, dsk_ref, nw_ref, o_ref, st_ref, state):
        @pl.when(pl.program_id(1) == 0)
        def _():
            state[...] = jnp.zeros_like(state)

        st = state[...]
        st_ref[...] = st
        out, st_new = ssd_block(x_ref[...], b_ref[...], c_ref[...], dt_ref[...], z_ref[...], st,
                                dtb_ref[...], alog_ref[...], dsk_ref[...], nw_ref[...], ssd_consts(pl.program_id(0)))
        o_ref[...] = out.astype(o_ref.dtype)
        state[...] = st_new

    vec = pl.BlockSpec((1, 512), lambda g, j: (0, g))
    heads = pl.BlockSpec((1, LANES), lambda g, j: (0, 0))
    return pl.pallas_call(
        body, name="ssd_forward", grid=(N_GROUPS_B, nb),
        in_specs=_ssd_in_specs(tb, lambda j: j) + [heads, vec, vec, vec],
        out_specs=[pl.BlockSpec((tb, 512), lambda g, j: (j, g)),
                   pl.BlockSpec((None, None, LANES, 512), lambda g, j: (j, g, 0, 0))],
        out_shape=[jax.ShapeDtypeStruct((t, B_INNER), BF16),
                   jax.ShapeDtypeStruct((nb, N_GROUPS_B, LANES, 512), F32)],
        scratch_shapes=[pltpu.VMEM((LANES, 512), F32)],
        compiler_params=_params(("arbitrary", "arbitrary")),
    )(xc, xc, xc, proj, proj, dtb, alog, dsk, nw)


def ssd_backward(xc, proj, states, d_out, dtb, alog, dsk, nw, d_proj):
    t = proj.shape[0]
    tb = _time_block(t, SSD_TIME_BLOCK)
    nb = t // tb
    rev = lambda j: nb - 1 - j

    def body(x_ref, b_ref, c_ref, dt_ref, z_ref, st_ref, do_ref, dtb_ref, alog_ref, dsk_ref, nw_ref, _,
             dx_ref, db_ref, dc_ref, ddt_ref, dz_ref, ddtb_ref, dalog_ref, ddsk_ref, dnw_ref, d_state):
        accs = (ddtb_ref, dalog_ref, ddsk_ref, dnw_ref)

        @pl.when(pl.program_id(1) == 0)
        def _():
            d_state[...] = jnp.zeros_like(d_state)
            for ref in accs:
                ref[...] = jnp.zeros_like(ref)

        cs = ssd_consts(pl.program_id(0))
        fn = lambda *a: ssd_block(*a, cs)
        _, vjp = jax.vjp(fn, x_ref[...], b_ref[...], c_ref[...], dt_ref[...], z_ref[...], st_ref[...],
                         dtb_ref[...], alog_ref[...], dsk_ref[...], nw_ref[...])
        dx, db, dc, ddt, dz, dst, *dpar = vjp((do_ref[...], d_state[...]))
        dx_ref[...] = dx
        db_ref[...] = db
        dc_ref[...] = dc
        ddt_ref[...] = ddt
        dz_ref[...] = dz.astype(dz_ref.dtype)
        d_state[...] = dst
        for ref, val in zip(accs, dpar, strict=True):
            ref[...] += val

    vec = pl.BlockSpec((1, 512), lambda g, j: (0, g))
    heads = pl.BlockSpec((1, LANES), lambda g, j: (0, 0))
    acc = pl.BlockSpec((None, 1, 512), lambda g, j: (g, 0, 0))
    acc_heads = pl.BlockSpec((None, 1, LANES), lambda g, j: (g, 0, 0))
    return pl.pallas_call(
        body, name="ssd_backward", grid=(N_GROUPS_B, nb),
        in_specs=_ssd_in_specs(tb, rev)
        + [pl.BlockSpec((None, None, LANES, 512), lambda g, j: (rev(j), g, 0, 0)),
           pl.BlockSpec((tb, 512), lambda g, j: (rev(j), g))] + [heads, vec, vec, vec] + [pl.BlockSpec(memory_space=pl.ANY)],
        out_specs=[pl.BlockSpec((tb, 512), lambda g, j: (rev(j), g)),
                   pl.BlockSpec((tb, LANES), lambda g, j: (rev(j), g)),
                   pl.BlockSpec((tb, LANES), lambda g, j: (rev(j), g)),
                   pl.BlockSpec((None, tb, LANES), lambda g, j: (g, rev(j), 0)),
                   pl.BlockSpec((tb, 512), lambda g, j: (rev(j), COL_Z // 512 + g)), acc_heads, acc, acc, acc],
        out_shape=[jax.ShapeDtypeStruct((t, B_INNER), F32), jax.ShapeDtypeStruct((t, 512), F32),
                   jax.ShapeDtypeStruct((t, 512), F32), jax.ShapeDtypeStruct((N_GROUPS_B, t, LANES), F32),
                   jax.ShapeDtypeStruct(d_proj.shape, d_proj.dtype)]
        + [jax.ShapeDtypeStruct((N_GROUPS_B, 1, LANES), F32)] + [jax.ShapeDtypeStruct((N_GROUPS_B, 1, 512), F32)] * 3,
        input_output_aliases={11: 4},
        scratch_shapes=[pltpu.VMEM((LANES, 512), F32)],
        compiler_params=_params(("arbitrary", "arbitrary")),
    )(xc, xc, xc, proj, proj, states, d_out, dtb, alog, dsk, nw, d_proj)


CONV_HALO = 8


def _shift_down(halo_then_tile, s, tm):
    if s == 0:
        return halo_then_tile[CONV_HALO:CONV_HALO + tm]
    return pltpu.roll(halo_then_tile, s, 0)[CONV_HALO:CONV_HALO + tm]


def _conv_pre(cur, prev, w, b, tm):
    stacked = jnp.concatenate([prev, cur], axis=0)
    taps = [_shift_down(stacked, 3 - j, tm) for j in range(4)]
    pre = b + taps[0] * w[0:1] + taps[1] * w[1:2] + taps[2] * w[2:3] + taps[3] * w[3:4]
    return pre, taps


def _conv_specs(t, tm):
    per = tm // CONV_HALO
    cur = pl.BlockSpec((tm, CONV_DIM), lambda i: (i, COL_XBC // CONV_DIM))
    prev = pl.BlockSpec((CONV_HALO, CONV_DIM), lambda i: (jnp.maximum(i * per - 1, 0), COL_XBC // CONV_DIM))
    return cur, prev


def conv_forward(proj, w, b):
    t = proj.shape[0]
    tm = _pick(t, (256, 128, 64))

    def body(cur_ref, prev_ref, w_ref, b_ref, o_ref):
        prev = jnp.where(pl.program_id(0) == 0, 0.0, prev_ref[...])
        pre, _ = _conv_pre(cur_ref[...], prev, w_ref[...], b_ref[...], tm)
        o_ref[...] = silu(pre)

    cur, prev = _conv_specs(t, tm)
    return pl.pallas_call(
        body, name="conv_forward", grid=(t // tm,),
        in_specs=[cur, prev, pl.BlockSpec((4, CONV_DIM), lambda i: (0, 0)), pl.BlockSpec((1, CONV_DIM), lambda i: (0, 0))],
        out_specs=pl.BlockSpec((tm, CONV_DIM), lambda i: (i, 0)),
        out_shape=jax.ShapeDtypeStruct((t, CONV_DIM), F32),
        compiler_params=_params(("arbitrary",)),
    )(proj, proj, w, b)


def conv_backward(proj, dx, db_, dc_, w, b, d_proj):
    t = proj.shape[0]
    tm = _pick(t, (256, 128, 64))
    per = tm // CONV_HALO
    nt = t // tm
    rev = lambda i: nt - 1 - i

    def body(cur_ref, prev_ref, dx_ref, dbm_ref, dcm_ref, w_ref, b_ref, _, o_ref, dw_ref, dbias_ref, later):
        @pl.when(pl.program_id(0) == 0)
        def _():
            dw_ref[...] = jnp.zeros_like(dw_ref)
            dbias_ref[...] = jnp.zeros_like(dbias_ref)
            later[...] = jnp.zeros_like(later)

        first_tile = pl.program_id(0) == nt - 1
        for lo, hi, src in ((0, B_INNER, dx_ref), (B_INNER, B_INNER + 512, dbm_ref), (B_INNER + 512, CONV_DIM, dcm_ref)):
            cols = slice(lo, hi)
            prev = jnp.where(first_tile, 0.0, prev_ref[:, cols])
            w_ = w_ref[:, cols]
            pre, taps = _conv_pre(cur_ref[:, cols], prev, w_, b_ref[:, cols], tm)
            sg = sigmoid(pre)
            dpre = src[...] * (sg * (1.0 + pre * (1.0 - sg)))
            dbias_ref[:, cols] += jnp.sum(dpre, axis=0, keepdims=True)
            for j in range(4):
                dw_ref[j:j + 1, cols] += jnp.sum(dpre * taps[j], axis=0, keepdims=True)
            stacked = jnp.concatenate([dpre, later[:, cols]], axis=0)
            acc = dpre * w_[3:4]
            for j in range(3):
                acc = acc + pltpu.roll(stacked, tm + CONV_HALO - (3 - j), 0)[0:tm] * w_[j:j + 1]
            o_ref[:, cols] = acc.astype(o_ref.dtype)
            later[:, cols] = dpre[0:CONV_HALO]

    row = lambda w_: pl.BlockSpec((tm, w_), lambda i: (rev(i), 0))
    whole = lambda r: pl.BlockSpec((r, CONV_DIM), lambda i: (0, 0))
    return pl.pallas_call(
        body, name="conv_backward", grid=(nt,),
        in_specs=[pl.BlockSpec((tm, CONV_DIM), lambda i: (rev(i), COL_XBC // CONV_DIM)),
                  pl.BlockSpec((CONV_HALO, CONV_DIM), lambda i: (jnp.maximum(rev(i) * per - 1, 0), COL_XBC // CONV_DIM)),
                  row(B_INNER), row(512), row(512), whole(4), whole(1), pl.BlockSpec(memory_space=pl.ANY)],
        out_specs=[pl.BlockSpec((tm, CONV_DIM), lambda i: (rev(i), COL_XBC // CONV_DIM)), whole(4), whole(1)],
        out_shape=[jax.ShapeDtypeStruct(d_proj.shape, d_proj.dtype), jax.ShapeDtypeStruct((4, CONV_DIM), F32),
                   jax.ShapeDtypeStruct((1, CONV_DIM), F32)],
        input_output_aliases={7: 0},
        scratch_shapes=[pltpu.VMEM((CONV_HALO, CONV_DIM), F32)],
        compiler_params=_params(("arbitrary",)),
    )(proj, proj, dx, db_, dc_, w, b, d_proj)


def stage_modulate(x, sc, sh):
    return _ln(x) * (1.0 + sc) + sh


def stage_merge(ga, gb, ya, yb):
    return sigmoid(ga) * ya + sigmoid(gb) * yb


def stage_post_mixer(x, h, g1, ln_g, ln_b, sc2, sh2):
    x1 = _ln(ALPHA * x + g1 * h) * ln_g + ln_b
    return x1, _ln(x1) * (1.0 + sc2) + sh2


def stage_swiglu(a, b):
    return silu(a) * b


def gate_up(ab):
    w = FFN_SHARD
    return (jnp.concatenate([ab[:, 2 * w * k:2 * w * k + w] for k in range(4)], axis=1),
            jnp.concatenate([ab[:, 2 * w * k + w:2 * w * (k + 1)] for k in range(4)], axis=1))


def per_chip(gate, up):
    w = FFN_SHARD
    return jnp.concatenate([part[:, w * k:w * (k + 1)] for k in range(4) for part in (gate, up)], axis=1)


def stage_loss(x1, hf, tgt, g2, ln_g, ln_b):
    x2 = _ln(ALPHA * x1 + g2 * hf) * ln_g + ln_b
    return 0.5 * jnp.sum(jnp.mean(jnp.square(x2 - tgt), axis=-1, keepdims=True), axis=0, keepdims=True)


def local_step(x, tgt, mod, wts, small, early=None, mid=None, late=None, last=None):
    sh1, sc1, g1, sh2, sc2, g2 = mod
    lb, gn, conv_w, conv_b, dtb, alog, dsk, nw, ln1_g, ln1_b, ln2_g, ln2_b = small
    vec = (1, D)

    (u1,) = rowwise("modulate1", lambda r, c: ((stage_modulate(r[0], *c),), ()), [_full(x)], [sc1, sh1], [(D, BF16)])
    w_in = wts.input_projection(u1)
    proj = matmul(u1, w_in, "nn", F32, "in_proj")
    ya_in, st_a = hgrn_forward(proj, lb, gn + wts.start_rest(proj)[0:1])
    xc = conv_forward(proj, conv_w, conv_b)
    w_a, w_b, w_o, w_gu, w_d = wts.rest(xc)
    yb_in, st_b = ssd_forward(xc, proj, dtb, alog, dsk, nw)
    ya = matmul(ya_in, w_a, "nn", F32, "branch_a")
    yb = matmul(yb_in, w_b, "nn", F32, "branch_b")
    gate_rows = [(proj, D, COL_GA // D), (proj, D, COL_GB // D), _full(ya), _full(yb)]
    (merged,) = rowwise("merge", lambda r, c: ((stage_merge(*r),), ()), gate_rows, [], [(D, BF16)])
    h = matmul(merged, w_o, "nn", F32, "out_proj")
    post_consts = [g1, ln1_g, ln1_b, sc2, sh2]
    x1, u2 = rowwise("post_mixer", lambda r, c: (stage_post_mixer(*r, *c), ()), [_full(x), _full(h)], post_consts,
                     [(D, F32), (D, BF16)])
    ab = matmul(u2, w_gu, "nt", F32, "ffn_in")
    (p,) = rowwise("swiglu", lambda r, c: ((stage_swiglu(*gate_up(r[0])),), ()), [_full(ab)], [], [(D_FF, BF16)],
                   tm_max=256)
    hf = matmul(p, w_d, "nn", F32, "ffn_out")

    def loss_bwd(r, c):
        loss, vjp = jax.vjp(stage_loss, *r, *c)
        dx1, dhf, _, dg2, dlg, dlb_ = vjp(jnp.ones((1, 1), F32))
        return (dx1, dhf), (loss, dg2, dlg, dlb_)

    dx1, dhf, loss, dg2, dln2_g, dln2_b = rowwise(
        "loss_backward", loss_bwd, [_full(x1), _full(hf), _full(tgt)], [g2, ln2_g, ln2_b],
        [(D, F32), (D, BF16)], [(1, 1), vec, vec, vec])
    dp = matmul(dhf, w_d, "nt", F32, "ffn_out_dx")
    dw_d = matmul(p, dhf, "tn", F32, "ffn_out_dw")

    def swiglu_bwd(r, c):
        _, vjp = jax.vjp(stage_swiglu, *gate_up(r[0]))
        return (per_chip(*vjp(r[1])),), ()

    (dab,) = rowwise("swiglu_backward", swiglu_bwd, [_full(ab), _full(dp)], [], [(2 * D_FF, BF16)], tm_max=256)
    du2 = matmul(dab, w_gu, "nn", F32, "ffn_in_dx")
    dw_gu = matmul(dab, u2, "tn", F32, "ffn_in_dw")

    def post_bwd(r, c):
        _, vjp = jax.vjp(stage_post_mixer, r[0], r[1], *c)
        dx, dh, *dc = vjp((r[2], r[3]))
        return (dx, dh), tuple(dc)

    dx_a, dh, dg1, dln1_g, dln1_b, dsc2, dsh2 = rowwise(
        "post_mixer_backward", post_bwd, [_full(x), _full(h), _full(dx1), _full(du2)], post_consts,
        [(D, F32), (D, BF16)], [vec] * 5)
    dmerged = matmul(dh, w_o, "nt", F32, "out_proj_dx")
    dw_o = matmul(merged, dh, "tn", F32, "out_proj_dw")

    def merge_bwd(r, c):
        _, vjp = jax.vjp(stage_merge, *r[:4])
        dga, dgb, dya, dyb = vjp(r[4])
        return (jnp.concatenate([dga, dgb], axis=1), dya, dyb), ()

    dproj, dya, dyb = rowwise("merge_backward", merge_bwd, gate_rows + [_full(dmerged)], [],
                              [(2 * D, BF16), (D, BF16), (D, BF16)], new_wide=(IN_PAD, COL_GA // (2 * D)))
    dya_in = matmul(dya, w_a, "nt", F32, "branch_a_dx")
    dw_a = matmul(ya_in, dya, "tn", F32, "branch_a_dw")
    dyb_in = matmul(dyb, w_b, "nt", F32, "branch_b_dx")
    dw_b = matmul(yb_in, dyb, "tn", F32, "branch_b_dw")
    gn_after = gn if early is None else gn + early((dw_a, dw_b, dw_o, dw_gu, dw_d))[0:1]
    dproj, dlb, dgn = hgrn_backward(proj, st_a, dya_in, lb, gn_after, dproj)
    dtb_after = dtb if mid is None else dtb + mid(dlb)[0:1, 0:1]
    dxs, dbm, dcm, ddt, dproj, ddtb, dalog, ddsk, dnw = ssd_backward(xc, proj, st_b, dyb_in, dtb_after, alog, dsk, nw, dproj)
    dproj, dconv_w, dconv_b = conv_backward(proj, dxs, dbm, dcm, conv_w, conv_b, dproj)
    if late is not None:
        late(dconv_b)
    t = x.shape[0]
    tail = jnp.concatenate([jnp.sum(ddt, axis=0).astype(BF16), jnp.zeros((t, IN_PAD - COL_DT - LANES), BF16)], axis=1)
    dproj = lax.dynamic_update_slice(dproj, tail, (0, COL_DT))
    dw_in = matmul(u1, dproj, "tn", F32, "in_proj_dw")
    du1 = matmul(dproj, w_in, "nt", F32, "in_proj_dx", after=None if last is None else last(dw_in))

    def mod_bwd(r, c):
        _, vjp = jax.vjp(stage_modulate, r[0], *c)
        dx, dsc, dsh = vjp(r[1])
        return (dx + r[2],), (dsc, dsh)

    grad_x, dsc1, dsh1 = rowwise("modulate1_backward", mod_bwd, [_full(x), _full(du1), _full(dx_a)], [sc1, sh1],
                                 [(D, F32)], [vec, vec])
    d_mod = (dsh1, dsc1, dg1, dsh2, dsc2, dg2)
    d_wts = (dw_in, dw_a, dw_b, dw_o, dw_gu, dw_d)
    d_small = (dlb, dgn, dconv_w, dconv_b, jnp.sum(ddtb, axis=0),
               dalog.reshape(1, B_INNER), ddsk.reshape(1, B_INNER), dnw.reshape(1, B_INNER),
               dln1_g, dln1_b, dln2_g, dln2_b)
    return loss, grad_x, d_mod, d_wts, d_small


HBM = pl.BlockSpec(memory_space=pltpu.HBM)
SEM = pl.BlockSpec(memory_space=pltpu.SEMAPHORE)
DATAFLOW = pltpu.SideEffectType.DATAFLOW_SIDE_EFFECTING


def _place():
    return lax.axis_index("x"), lax.axis_index("y"), lax.axis_index("c")


def _other_chips(x, y):
    return [(1 - x, y), (x, 1 - y), (1 - x, 1 - y)]


def _remote(src, dst, send_sem, recv_sem, device):
    return pltpu.make_async_remote_copy(src_ref=src, dst_ref=dst, send_sem=send_sem, recv_sem=recv_sem,
                                        device_id=device, device_id_type=MESH)


def gather_rows(v, name):
    n = v.shape[1]

    def body(v_ref, out_ref, send_sems, recv_sems, local_sem):
        x, y, c = _place()
        mine = pltpu.make_async_copy(v_ref, out_ref.at[4 * x + 2 * y + c], local_sem)
        mine.start()
        sends, recvs = [], []
        for m in range(1, 8):
            px = 1 - x if m & 4 else x
            py = 1 - y if m & 2 else y
            pc = 1 - c if m & 1 else c
            sends.append(_remote(v_ref, out_ref.at[4 * x + 2 * y + c], send_sems.at[m - 1], recv_sems.at[m - 1], (px, py, pc)))
            recvs.append(_remote(v_ref, out_ref.at[4 * px + 2 * py + pc], send_sems.at[m - 1], recv_sems.at[m - 1], (px, py, pc)))
        for cp in sends:
            cp.start()
        for cp in recvs:
            cp.wait_recv()
        for cp in sends:
            cp.wait_send()
        mine.wait()

    return pl.pallas_call(
        body, name=name, in_specs=[HBM], out_specs=HBM,
        out_shape=jax.ShapeDtypeStruct((8, 1, n), v.dtype),
        scratch_shapes=[pltpu.SemaphoreType.DMA((7,)), pltpu.SemaphoreType.DMA((7,)), pltpu.SemaphoreType.DMA],
    )(v)


def exchange_rows(part, name):
    w = part.shape[2]

    def body(p_ref, out_ref, send_sems, recv_sems, local_sem):
        x, y, c = _place()
        k = 2 * x + y
        mine = pltpu.make_async_copy(p_ref.at[4 * x + 2 * y + c], out_ref.at[k], local_sem)
        mine.start()
        sends, recvs = [], []
        for j, (px, py) in enumerate(_other_chips(x, y)):
            sends.append(_remote(p_ref.at[4 * px + 2 * py + c], out_ref.at[k], send_sems.at[j], recv_sems.at[j], (px, py, c)))
            recvs.append(_remote(p_ref.at[4 * px + 2 * py + c], out_ref.at[2 * px + py], send_sems.at[j], recv_sems.at[j], (px, py, c)))
        for cp in sends:
            cp.start()
        for cp in recvs:
            cp.wait_recv()
        for cp in sends:
            cp.wait_send()
        mine.wait()

    return pl.pallas_call(
        body, name=name, in_specs=[HBM], out_specs=HBM,
        out_shape=jax.ShapeDtypeStruct((4, 1, w), part.dtype),
        scratch_shapes=[pltpu.SemaphoreType.DMA((3,)), pltpu.SemaphoreType.DMA((3,)), pltpu.SemaphoreType.DMA],
    )(part)


def _half_of_slot(ref, rows, px, py, pc):
    return ref.at[2 * px + py, pl.ds(pc * (rows // 2), rows // 2), :]


def gather_start(shards, after, tag):
    n = len(shards)

    def body(*refs):
        w_refs, land_refs = refs[:n], refs[n:2 * n]
        send_sems, recv_sems = refs[2 * n + 1], refs[2 * n + 2]
        token = refs[-1]
        x, y, c = _place()
        for i in range(n):
            rows = shards[i].shape[0]
            for j, (px, py) in enumerate(_other_chips(x, y)):
                _remote(w_refs[i].at[pl.ds(c * (rows // 2), rows // 2), :], _half_of_slot(land_refs[i], rows, x, y, c),
                        send_sems.at[j * n + i], recv_sems.at[j * n + i], (px, py, c)).start()
        token[...] = jnp.zeros_like(token)

    hbm = lambda a: pltpu.with_memory_space_constraint(a, pltpu.HBM)
    lands = [lax.empty((4,) + s.shape, s.dtype) for s in shards]
    dma = pltpu.SemaphoreType.DMA
    return pl.pallas_call(
        body, name="gather_start_" + tag,
        out_shape=(dma((3 * n,)), dma((3 * n,)),
                   *[pltpu.HBM(a.shape, a.dtype) for a in list(shards) + lands], jax.ShapeDtypeStruct((8, LANES), F32)),
        in_specs=[HBM] * (2 * n) + [pl.BlockSpec(memory_space=pl.ANY)],
        out_specs=(SEM, SEM, *[HBM] * (2 * n), pl.BlockSpec(memory_space=pltpu.VMEM)),
        input_output_aliases={i: 2 + i for i in range(2 * n)},
        compiler_params=pltpu.CompilerParams(has_side_effects=DATAFLOW),
    )(*[hbm(a) for a in list(shards) + lands], after)


def gather_wait(send_sems, recv_sems, shards, lands, after, tag):
    n = len(shards)

    def body(*refs):
        w_refs, land_refs = refs[:n], refs[n:2 * n]
        send_ref, recv_ref = refs[2 * n], refs[2 * n + 1]
        x, y, c = _place()
        for i in range(n):
            rows = shards[i].shape[0]
            for j, (px, py) in enumerate(_other_chips(x, y)):
                cp = _remote(w_refs[i].at[pl.ds(c * (rows // 2), rows // 2), :], _half_of_slot(land_refs[i], rows, px, py, c),
                             send_ref.at[j * n + i], recv_ref.at[j * n + i], (px, py, c))
                cp.wait_send()
                cp.wait_recv()

    out = pl.pallas_call(
        body, name="gather_wait_" + tag,
        out_shape=tuple(pltpu.HBM(a.shape, a.dtype) for a in list(shards) + list(lands)),
        in_specs=[HBM] * (2 * n) + [SEM, SEM, pl.BlockSpec(memory_space=pl.ANY)], out_specs=tuple([HBM] * (2 * n)),
        input_output_aliases={i: i for i in range(2 * n)},
        compiler_params=pltpu.CompilerParams(has_side_effects=DATAFLOW),
    )(*shards, *lands, send_sems, recv_sems, after)
    return list(out[:n]), list(out[n:])


def forward_start(lands, tag):
    n = len(lands)

    def body(*refs):
        land_refs = refs[:n]
        send_sems, recv_sems = refs[n], refs[n + 1]
        token = refs[-1]
        x, y, c = _place()
        for i in range(n):
            rows = lands[i].shape[1]
            for j, (px, py) in enumerate(_other_chips(x, y)):
                mine = _half_of_slot(land_refs[i], rows, px, py, c)
                _remote(mine, mine, send_sems.at[j * n + i], recv_sems.at[j * n + i], (x, y, 1 - c)).start()
        token[...] = jnp.zeros_like(token)

    dma = pltpu.SemaphoreType.DMA
    return pl.pallas_call(
        body, name="forward_start_" + tag,
        out_shape=(dma((3 * n,)), dma((3 * n,)), *[pltpu.HBM(a.shape, a.dtype) for a in lands],
                   jax.ShapeDtypeStruct((8, LANES), F32)),
        in_specs=[HBM] * n, out_specs=(SEM, SEM, *[HBM] * n, pl.BlockSpec(memory_space=pltpu.VMEM)),
        input_output_aliases={i: 2 + i for i in range(n)},
        compiler_params=pltpu.CompilerParams(has_side_effects=DATAFLOW),
    )(*lands)


def forward_wait(started, after, tag):
    send_sems, recv_sems, *rest = started
    lands = rest[:-1]
    n = len(lands)

    def body(*refs):
        land_refs = refs[:n]
        send_ref, recv_ref = refs[n], refs[n + 1]
        x, y, c = _place()
        for i in range(n):
            rows = lands[i].shape[1]
            for j, (px, py) in enumerate(_other_chips(x, y)):
                cp = _remote(_half_of_slot(land_refs[i], rows, px, py, c), _half_of_slot(land_refs[i], rows, px, py, 1 - c),
                             send_ref.at[j * n + i], recv_ref.at[j * n + i], (x, y, 1 - c))
                cp.wait_send()
                cp.wait_recv()

    out = pl.pallas_call(
        body, name="forward_wait_" + tag,
        out_shape=tuple(pltpu.HBM(a.shape, a.dtype) for a in lands),
        in_specs=[HBM] * n + [SEM, SEM, pl.BlockSpec(memory_space=pl.ANY)], out_specs=tuple([HBM] * n),
        input_output_aliases={i: i for i in range(n)},
        compiler_params=pltpu.CompilerParams(has_side_effects=DATAFLOW),
    )(*lands, send_sems, recv_sems, after)
    return list(out)


def pair_start(slabs, tag):
    n = len(slabs)

    def body(*refs):
        g_refs, land_refs = refs[:n], refs[n:2 * n]
        send_sems, recv_sems = refs[2 * n], refs[2 * n + 1]
        token = refs[-1]
        x, y, c = _place()
        for i in range(n):
            hr = slabs[i].shape[1] // 2
            _remote(g_refs[i].at[:, pl.ds((1 - c) * hr, hr), :], land_refs[i], send_sems.at[i], recv_sems.at[i],
                    (x, y, 1 - c)).start()
        token[...] = jnp.zeros_like(token)

    hbm = lambda a: pltpu.with_memory_space_constraint(a, pltpu.HBM)
    lands = [lax.empty((4, s.shape[1] // 2, s.shape[2]), s.dtype) for s in slabs]
    dma = pltpu.SemaphoreType.DMA
    return pl.pallas_call(
        body, name="pair_start_" + tag,
        out_shape=(dma((n,)), dma((n,)), *[pltpu.HBM(a.shape, a.dtype) for a in list(slabs) + lands],
                   jax.ShapeDtypeStruct((8, LANES), F32)),
        in_specs=[HBM] * (2 * n), out_specs=(SEM, SEM, *[HBM] * (2 * n), pl.BlockSpec(memory_space=pltpu.VMEM)),
        input_output_aliases={i: 2 + i for i in range(2 * n)},
        compiler_params=pltpu.CompilerParams(has_side_effects=DATAFLOW),
    )(*[hbm(a) for a in list(slabs) + lands])


def pair_wait(started, after, tag):
    send_sems, recv_sems, *rest = started
    n = (len(rest) - 1) // 2
    slabs, lands = rest[:n], rest[n:2 * n]

    def body(*refs):
        g_refs, land_refs = refs[:n], refs[n:2 * n]
        send_ref, recv_ref = refs[2 * n], refs[2 * n + 1]
        x, y, c = _place()
        for i in range(n):
            hr = slabs[i].shape[1] // 2
            cp = _remote(g_refs[i].at[:, pl.ds((1 - c) * hr, hr), :], land_refs[i], send_ref.at[i], recv_ref.at[i], (x, y, 1 - c))
            cp.wait_send()
            cp.wait_recv()

    out = pl.pallas_call(
        body, name="pair_wait_" + tag,
        out_shape=tuple(pltpu.HBM(a.shape, a.dtype) for a in list(slabs) + list(lands)),
        in_specs=[HBM] * (2 * n) + [SEM, SEM, pl.BlockSpec(memory_space=pl.ANY)], out_specs=tuple([HBM] * (2 * n)),
        input_output_aliases={i: i for i in range(2 * n)},
        compiler_params=pltpu.CompilerParams(has_side_effects=DATAFLOW),
    )(*slabs, *lands, send_sems, recv_sems, after)
    return list(out[:n]), list(out[n:])


def _tile2(rows, cols):
    fits = lambda r, c: r * c * 4 <= BLOCK_BYTES
    if fits(rows, cols):
        return rows, cols
    tiles = [(r, cols) for r in (1024, 512, 256, 128, 64) if rows % r == 0 and fits(r, cols)]
    tiles += [(rows, cols // k) for k in (2, 3, 4, 6, 8, 12, 16) if cols % (k * LANES) == 0 and fits(rows, cols // k)]
    return max(tiles, key=lambda t: t[0] * t[1])


def pair_add(g, p, c, name):
    _, hr, cols = p.shape
    tm, tc = _tile2(hr, cols)
    per = hr // tm

    def body(c_ref, g_ref, p_ref, o_ref):
        o_ref[...] = (g_ref[...] + p_ref[...]).astype(o_ref.dtype)

    return pl.pallas_call(
        body, name=name,
        grid_spec=pltpu.PrefetchScalarGridSpec(
            num_scalar_prefetch=1, grid=(4, per, cols // tc),
            in_specs=[pl.BlockSpec((None, tm, tc), lambda k, i, j, c_ref: (k, c_ref[0] * per + i, j)),
                      pl.BlockSpec((None, tm, tc), lambda k, i, j, c_ref: (k, i, j))],
            out_specs=pl.BlockSpec((None, tm, tc), lambda k, i, j, c_ref: (k, i, j))),
        out_shape=jax.ShapeDtypeStruct((4, hr, cols), BF16),
        compiler_params=_params(("arbitrary", "arbitrary", "arbitrary")),
    )(c.reshape(1).astype(jnp.int32), g, p)


def scatter_start(sums, tag):
    n = len(sums)

    def body(*refs):
        s_refs, land_refs = refs[:n], refs[n:2 * n]
        send_sems, recv_sems = refs[2 * n], refs[2 * n + 1]
        token = refs[-1]
        x, y, c = _place()
        k = 2 * x + y
        for i in range(n):
            for j, (px, py) in enumerate(_other_chips(x, y)):
                _remote(s_refs[i].at[2 * px + py], land_refs[i].at[k], send_sems.at[j * n + i], recv_sems.at[j * n + i],
                        (px, py, c)).start()
        token[...] = jnp.zeros_like(token)

    hbm = lambda a: pltpu.with_memory_space_constraint(a, pltpu.HBM)
    return pl.pallas_call(
        body, name="scatter_start_" + tag,
        out_shape=(pltpu.SemaphoreType.DMA((3 * n,)), pltpu.SemaphoreType.DMA((3 * n,)),
                   *[pltpu.HBM(s.shape, s.dtype) for s in sums], *[pltpu.HBM(s.shape, s.dtype) for s in sums],
                   jax.ShapeDtypeStruct((8, LANES), F32)),
        in_specs=[HBM] * (2 * n), out_specs=(SEM, SEM, *[HBM] * (2 * n), pl.BlockSpec(memory_space=pltpu.VMEM)),
        input_output_aliases={i: 2 + i for i in range(2 * n)},
        compiler_params=pltpu.CompilerParams(has_side_effects=DATAFLOW),
    )(*[hbm(s) for s in sums], *[hbm(lax.empty(s.shape, s.dtype)) for s in sums])


def scatter_wait(started, after, tag):
    send_sems, recv_sems, *rest = started
    n = (len(rest) - 1) // 2
    sums, lands = rest[:n], rest[n:2 * n]

    def body(*refs):
        s_refs, land_refs = refs[:n], refs[n:2 * n]
        send_ref, recv_ref = refs[2 * n], refs[2 * n + 1]
        x, y, c = _place()
        for i in range(n):
            for j, (px, py) in enumerate(_other_chips(x, y)):
                cp = _remote(s_refs[i].at[2 * px + py], land_refs[i].at[2 * px + py], send_ref.at[j * n + i],
                             recv_ref.at[j * n + i], (px, py, c))
                cp.wait_send()
                cp.wait_recv()

    out = pl.pallas_call(
        body, name="scatter_wait_" + tag,
        out_shape=tuple(pltpu.HBM(s.shape, s.dtype) for s in sums + lands),
        in_specs=[HBM] * (2 * n) + [SEM, SEM, pl.BlockSpec(memory_space=pl.ANY)], out_specs=tuple([HBM] * (2 * n)),
        input_output_aliases={i: i for i in range(2 * n)},
        compiler_params=pltpu.CompilerParams(has_side_effects=DATAFLOW),
    )(*sums, *lands, send_sems, recv_sems, after)
    return list(out[:n]), list(out[n:])


def sum_chips(landed, own, chip, core, name):
    _, hr, cols = landed.shape
    tm, tc = _tile2(hr, cols)
    per = hr // tm

    def body(idx_ref, l0, l1, l2, l3, own_ref, o_ref):
        mine = own_ref[...].astype(F32)
        v = [jnp.where(idx_ref[0] == k, mine, ref[...].astype(F32)) for k, ref in enumerate((l0, l1, l2, l3))]
        o_ref[...] = ((v[0] + v[1]) + v[2]) + v[3]

    slot = lambda k: pl.BlockSpec((None, tm, tc),
                                  lambda i, j, idx: (jnp.where(idx[0] == k, (k + 1) & 3, k), i, j))
    return pl.pallas_call(
        body, name=name,
        grid_spec=pltpu.PrefetchScalarGridSpec(
            num_scalar_prefetch=1, grid=(per, cols // tc),
            in_specs=[slot(0), slot(1), slot(2), slot(3),
                      pl.BlockSpec((None, tm, tc), lambda i, j, idx: (idx[0], i, j))],
            out_specs=pl.BlockSpec((tm, tc), lambda i, j, idx: (idx[1] * per + i, j))),
        out_shape=jax.ShapeDtypeStruct((2 * hr, cols), F32),
        compiler_params=_params(("arbitrary", "arbitrary")),
    )(jnp.stack([chip, core]).astype(jnp.int32), landed, landed, landed, landed, own)


def exchange_halves(bufs):
    n = len(bufs)

    def body(*refs):
        out_refs = refs[n:2 * n]
        send_sems, recv_sems = refs[2 * n:]
        x, y, c = _place()
        sends, recvs = [], []
        for i in range(n):
            hr = bufs[i].shape[0] // 2
            own = out_refs[i].at[pl.ds(c * hr, hr), :]
            other = out_refs[i].at[pl.ds((1 - c) * hr, hr), :]
            sends.append(_remote(own, own, send_sems.at[i], recv_sems.at[i], (x, y, 1 - c)))
            recvs.append(_remote(other, other, send_sems.at[i], recv_sems.at[i], (x, y, 1 - c)))
        for cp in sends:
            cp.start()
        for cp in recvs:
            cp.wait_recv()
        for cp in sends:
            cp.wait_send()

    return pl.pallas_call(
        body, name="exchange_halves", in_specs=[HBM] * n, out_specs=[HBM] * n,
        out_shape=[jax.ShapeDtypeStruct(b.shape, b.dtype) for b in bufs],
        input_output_aliases={i: i for i in range(n)},
        scratch_shapes=[pltpu.SemaphoreType.DMA((n,)), pltpu.SemaphoreType.DMA((n,))],
    )(*bufs)


def assemble_in_proj(landed, own, chip):
    rows, cols = 128, own.shape[1]

    def body(idx_ref, l0, l1, l2, l3, own_ref, o_ref):
        mine = own_ref[...]
        w = jnp.concatenate([jnp.where(idx_ref[0] == k, mine, ref[...]) for k, ref in enumerate((l0, l1, l2, l3))], axis=1)
        o_ref[...] = jnp.concatenate([w[:, :ORIG_Z], w[:, ORIG_GA:], w[:, ORIG_XBC:ORIG_DT], w[:, ORIG_Z:ORIG_XBC],
                                      w[:, ORIG_DT:ORIG_GA], jnp.zeros((rows, IN_PAD - IN_ORIG), w.dtype)], axis=1)

    slot = lambda k: pl.BlockSpec((None, rows, cols), lambda i, idx: (jnp.where(idx[0] == k, (k + 1) & 3, k), i, 0))
    return pl.pallas_call(
        body, name="assemble_in_proj",
        grid_spec=pltpu.PrefetchScalarGridSpec(
            num_scalar_prefetch=1, grid=(D // rows,),
            in_specs=[slot(0), slot(1), slot(2), slot(3), pl.BlockSpec((rows, cols), lambda i, idx: (i, 0))],
            out_specs=pl.BlockSpec((rows, IN_PAD), lambda i, idx: (i, 0))),
        out_shape=jax.ShapeDtypeStruct((D, IN_PAD), own.dtype),
        compiler_params=_params(("arbitrary",)),
    )(chip.reshape(1).astype(jnp.int32), landed, landed, landed, landed, own)


def rows_exchange(a, name):
    hr = a.shape[0] // 2

    def body(a_ref, out_ref, send_sem, recv_sem):
        x, y, c = _place()
        cp = _remote(a_ref.at[pl.ds((1 - c) * hr, hr), :], out_ref, send_sem, recv_sem, (x, y, 1 - c))
        cp.start()
        cp.wait()

    return pl.pallas_call(
        body, name=name, in_specs=[HBM], out_specs=HBM,
        out_shape=jax.ShapeDtypeStruct((hr, a.shape[1]), a.dtype),
        scratch_shapes=[pltpu.SemaphoreType.DMA, pltpu.SemaphoreType.DMA],
    )(a)


def split_pair_add(dw, received, core):
    cols = IN_ORIG // 4
    rows, hr = 128, D // 2
    per = hr // rows

    def body(c_ref, own_ref, got_ref, o_ref):
        d = own_ref[...] + got_ref[...]
        w = jnp.concatenate([d[:, :COL_GA], d[:, COL_Z:COL_DT], d[:, COL_XBC:COL_Z], d[:, COL_DT:COL_DT + 32],
                             d[:, COL_GA:COL_XBC]], axis=1)
        for k in range(4):
            o_ref[k] = w[:, k * cols:(k + 1) * cols].astype(o_ref.dtype)

    return pl.pallas_call(
        body, name="split_pair_add",
        grid_spec=pltpu.PrefetchScalarGridSpec(
            num_scalar_prefetch=1, grid=(per,),
            in_specs=[pl.BlockSpec((rows, IN_PAD), lambda i, c_ref: (c_ref[0] * per + i, 0)),
                      pl.BlockSpec((rows, IN_PAD), lambda i, c_ref: (i, 0))],
            out_specs=pl.BlockSpec((4, rows, cols), lambda i, c_ref: (0, i, 0))),
        out_shape=jax.ShapeDtypeStruct((4, hr, cols), BF16),
        compiler_params=_params(("arbitrary",)),
    )(core.reshape(1).astype(jnp.int32), dw, received)


def ada_prepare(c_all, w_ada, hgrn_lb):
    def body(c_ref, w_ref, lb_ref, mod_ref, row_ref):
        mod_ref[...] = hdot(silu(c_ref[...]), w_ref[...])
        row_ref[...] = sigmoid(lb_ref[0:1, :] - lb_ref[1:2, :])

    return pl.pallas_call(
        body, name="ada_prepare",
        out_shape=[jax.ShapeDtypeStruct((8, w_ada.shape[1]), F32), jax.ShapeDtypeStruct((1, D), F32)],
        compiler_params=pltpu.CompilerParams(vmem_limit_bytes=VMEM_LIMIT),
    )(c_all, w_ada, hgrn_lb)


SMALL_SEGS = (("mod", 6 * D), ("lb", D), ("gnorm", LANES), ("conv_w", 4 * CONV_DIM), ("conv_b", CONV_DIM),
              ("dt_bias", LANES), ("a_log", B_INNER), ("d", B_INNER), ("ssm_norm", B_INNER),
              ("ln1_g", D), ("ln1_b", D), ("ln2_g", D), ("ln2_b", D), ("loss", LANES))
SMALL_PARAMS = ("b_ada", "hgrn_lb", "hgrn_gnorm", "ssm_conv_b", "ssm_dt_bias", "ssm_a_log", "ssm_d", "ssm_norm",
                "ln1_g", "ln1_b", "ln2_g", "ln2_b")


def finalize_small(g_all, c_all, dmod_cols, params, m, v):
    n_p = len(SMALL_PARAMS)
    offs, o = {}, 0
    for nm, width in SMALL_SEGS:
        offs[nm] = (o, width)
        o += width

    def body(*refs):
        g_ref, c_ref, dm_ref = refs[:3]
        p_refs = refs[3:3 + n_p]
        m_refs = refs[3 + n_p:3 + 2 * n_p]
        v_refs = refs[3 + 2 * n_p:3 + 3 * n_p]
        outs = refs[3 + 3 * n_p:]
        gwa_ref, gcw_ref, loss_ref = outs[:3]
        res = outs[3:]
        total = jnp.sum(g_ref[...], axis=0, keepdims=True)
        seg = lambda nm: total[:, offs[nm][0]:offs[nm][0] + offs[nm][1]]
        loss_ref[...] = seg("loss")
        gwa_ref[...] = hdot(silu(c_ref[...]), dm_ref[...], "tn")
        cw = seg("conv_w")
        for j in range(4):
            gcw_ref[j:j + 1, :] = cw[:, j * CONV_DIM:(j + 1) * CONV_DIM]
        hc = lax.broadcasted_iota(jnp.int32, (B_INNER, LANES), 0)
        hj = lax.broadcasted_iota(jnp.int32, (B_INNER, LANES), 1)
        per_head = ((hc >> 6) == hj).astype(F32)
        heads = lambda nm: hdot(jnp.broadcast_to(seg(nm), (8, B_INNER)), per_head)[0:1, 0:32]
        lbp = sigmoid(p_refs[1][0:1, :] - p_refs[1][1:2, :])
        g_row = seg("lb") * lbp * (1.0 - lbp)
        grads = {"b_ada": seg("mod"), "hgrn_gnorm": seg("gnorm"), "ssm_conv_b": seg("conv_b"),
                 "ssm_dt_bias": seg("dt_bias")[:, 0:32], "ssm_a_log": heads("a_log"), "ssm_d": heads("d"),
                 "ssm_norm": seg("ssm_norm"), "ln1_g": seg("ln1_g"), "ln1_b": seg("ln1_b"),
                 "ln2_g": seg("ln2_g"), "ln2_b": seg("ln2_b")}
        for i, nm in enumerate(SMALL_PARAMS):
            g_out, d_out, m_out, v_out = res[4 * i:4 * i + 4]
            if nm == "hgrn_lb":
                for row, gv in ((0, g_row), (1, -g_row)):
                    sl = slice(row, row + 1)
                    dl, mn, vn = adamw(p_refs[i][sl, :], gv, m_refs[i][sl, :], v_refs[i][sl, :])
                    g_out[sl, :], d_out[sl, :], m_out[sl, :], v_out[sl, :] = gv, dl, mn, vn
            else:
                gv = grads[nm]
                dl, mn, vn = adamw(p_refs[i][...], gv, m_refs[i][...], v_refs[i][...])
                g_out[...], d_out[...], m_out[...], v_out[...] = gv, dl, mn, vn

    out_shape = [jax.ShapeDtypeStruct((D, dmod_cols.shape[1]), F32), jax.ShapeDtypeStruct((4, CONV_DIM), F32),
                 jax.ShapeDtypeStruct((1, LANES), F32)]
    for p in params:
        out_shape += [jax.ShapeDtypeStruct(p.shape, F32)] * 4
    return pl.pallas_call(
        body, name="finalize_small", out_shape=out_shape,
        compiler_params=pltpu.CompilerParams(vmem_limit_bytes=VMEM_LIMIT),
    )(g_all, c_all, dmod_cols, *params, *m, *v)


def adam_update(w, g, m, v, name):
    rows, cols = w.shape
    tm, tc = _tile2(rows, cols)

    def body(w_ref, g_ref, m_ref, v_ref, d_ref, mo_ref, vo_ref):
        d_ref[...], mo_ref[...], vo_ref[...] = adamw(w_ref[...], g_ref[...], m_ref[...], v_ref[...])

    spec = pl.BlockSpec((tm, tc), lambda i, j: (i, j))
    return pl.pallas_call(
        body, name=name, grid=(rows // tm, cols // tc), in_specs=[spec] * 4, out_specs=[spec] * 3,
        out_shape=[jax.ShapeDtypeStruct((rows, cols), F32)] * 3,
        compiler_params=_params(("arbitrary", "arbitrary")),
    )(w, g, m, v)


def kernel(x, c, w_ada, b_ada, w_in, hgrn_lb, hgrn_gnorm, ssm_conv_w, ssm_conv_b, ssm_dt_bias, ssm_a_log, ssm_d, ssm_norm, w_branch_a, w_branch_b, w_o, ln1_g, ln1_b, w_ffn_gate, w_ffn_up, w_ffn_down, ln2_g, ln2_b, loss_target, m_w_ada, m_b_ada, m_w_in, m_hgrn_lb, m_hgrn_gnorm, m_ssm_conv_w, m_ssm_conv_b, m_ssm_dt_bias, m_ssm_a_log, m_ssm_d, m_ssm_norm, m_w_branch_a, m_w_branch_b, m_w_o, m_ln1_g, m_ln1_b, m_w_ffn_gate, m_w_ffn_up, m_w_ffn_down, m_ln2_g, m_ln2_b, v_w_ada, v_b_ada, v_w_in, v_hgrn_lb, v_hgrn_gnorm, v_ssm_conv_w, v_ssm_conv_b, v_ssm_dt_bias, v_ssm_a_log, v_ssm_d, v_ssm_norm, v_w_branch_a, v_w_branch_b, v_w_o, v_ln1_g, v_ln1_b, v_w_ffn_gate, v_w_ffn_up, v_w_ffn_down, v_ln2_g, v_ln2_b):
    given = dict(locals())
    chip = 2 * lax.axis_index("x") + lax.axis_index("y")
    core = lax.axis_index("c")
    t = x.shape[1]

    first = gather_rows(jnp.concatenate([c, ssm_conv_w.reshape(1, CONV_DIM)], axis=1), "gather_cond").reshape(8, D + CONV_DIM)
    c_all = first[:, :D]
    conv_w = first[0::2, D:].reshape(4, 4, CONV_DIM // 4).transpose(1, 0, 2).reshape(4, CONV_DIM)
    mod_part, lb_row = ada_prepare(c_all, w_ada[0], hgrn_lb)
    mod_cols = w_ada.shape[2]
    mod_row = exchange_rows(mod_part.reshape(8, 1, mod_cols), "exchange_mod").reshape(1, 6 * D) + b_ada

    local = {nm: given[nm][0] for nm in SHARDED if nm != "w_ffn_in"}
    local["w_ffn_in"] = jnp.concatenate([w_ffn_gate[0].T, w_ffn_up[0].T], axis=0)
    shards = [local[nm].astype(BF16) for nm in SHARDED]
    send_in, recv_in, sent_in, land_in, started_in = gather_start(shards[:1], mod_row, "in")
    shards = shards[:1] + [(local[nm] + started_in[0, 0]).astype(BF16) for nm in SHARDED[1:]]
    send_rest, recv_rest, *flying = gather_start(shards[1:], started_in, "rest")
    n_rest = len(SHARDED) - 1
    sent_rest, land_rest, started_rest = flying[:n_rest], flying[n_rest:2 * n_rest], flying[-1]
    mod_row = mod_row + started_rest[0:1, 0:1]
    mod = tuple(mod_row[:, i * D:(i + 1) * D] for i in range(6))
    with_own = lambda land, shard: lax.dynamic_update_slice(land, shard[None], (chip, 0, 0))

    class Weights:
        def input_projection(self, after):
            (own,), land = gather_wait(send_in, recv_in, [sent_in], [land_in], after, "in")
            (land,) = forward_wait(forward_start(land, "in"), after, "in")
            return assemble_in_proj(land, own, chip)

        def start_rest(self, after):
            self.own, landed = gather_wait(send_rest, recv_rest, sent_rest, land_rest, after, "rest")
            self.started = forward_start(landed, "rest")
            return self.started[-1]

        def rest(self, after):
            got = {nm: with_own(land, s) for nm, land, s in zip(SHARDED[1:], forward_wait(self.started, after, "rest"), self.own, strict=True)}
            whole = lambda nm: got[nm].reshape(4 * got[nm].shape[1], got[nm].shape[2])
            return tuple(whole(nm) for nm in SHARDED[1:])

    wts = Weights()

    per_head = lambda p: jnp.pad(p, ((0, 0), (0, LANES - p.shape[1])))
    per_channel = lambda p: jnp.repeat(p[0], B_INNER // 32)[None]
    small = (lb_row, hgrn_gnorm, conv_w, ssm_conv_b, per_head(ssm_dt_bias), per_channel(ssm_a_log),
             per_channel(ssm_d), ssm_norm, ln1_g, ln1_b, ln2_g, ln2_b)
    by_rows = lambda g: g.reshape(4, g.shape[0] // 4, g.shape[1])
    travelling = {}

    def start_early(dws):
        travelling["pair"] = pair_start([by_rows(dw) for dw in dws], "early")
        return travelling["pair"][-1]

    def between_scans(after):
        slabs, received = pair_wait(travelling["pair"], after, "early")
        travelling["pairs"] = [pair_add(s, r, core, "pair_add_" + nm) for nm, s, r in zip(SHARDED[1:], slabs, received, strict=True)]
        travelling["started"] = scatter_start(travelling["pairs"], "early")
        return travelling["started"][-1]

    def finish_early(after):
        travelling["pairs"], travelling["landed"] = scatter_wait(travelling["started"], after, "early")

    def start_last(dw_in):
        travelling["pairs_in"] = [split_pair_add(dw_in, rows_exchange(dw_in, "pair_exchange_last"), core)]
        travelling["started_in"] = scatter_start(travelling["pairs_in"], "last")
        return travelling["started_in"][-1]

    loss, grad_x, d_mod, d_wts, d_small = local_step(x[0], loss_target[0], mod, wts, small,
                                                     start_early, between_scans, finish_early, start_last)

    d_lb, d_gn, d_cw, d_cb, d_dtb, d_alog, d_dsk, d_nw, d_l1g, d_l1b, d_l2g, d_l2b = d_small
    row = jnp.concatenate(list(d_mod) + [d_lb, d_gn, d_cw.reshape(1, 4 * CONV_DIM), d_cb, d_dtb, d_alog, d_dsk, d_nw,
                                          d_l1g, d_l1b, d_l2g, d_l2b, jnp.pad(loss, ((0, 0), (0, LANES - 1)))], axis=1)
    g_all = gather_rows(row, "gather_small_grads").reshape(8, row.shape[1])
    dmod_cols = lax.dynamic_slice_in_dim(g_all, chip * mod_cols, mod_cols, axis=1)
    fin = finalize_small(g_all, c_all, dmod_cols, [given[n] for n in SMALL_PARAMS],
                         [given["m_" + n] for n in SMALL_PARAMS], [given["v_" + n] for n in SMALL_PARAMS])
    grads, deltas, new_m, new_v = {}, {}, {}, {}
    grads["w_ada"] = fin[0][None]
    grads["ssm_conv_w"] = lax.dynamic_slice_in_dim(fin[1], chip * (CONV_DIM // 4), CONV_DIM // 4, axis=1)[None]
    for i, nm in enumerate(SMALL_PARAMS):
        grads[nm], deltas[nm], new_m[nm], new_v[nm] = fin[3 + 4 * i:7 + 4 * i]

    pairs_in, landed_in = scatter_wait(travelling["started_in"], fin[3], "last")
    pairs, landed = pairs_in + travelling["pairs"], landed_in + travelling["landed"]
    halves = [sum_chips(r, p, chip, core, "sum_chips_" + nm) for nm, r, p in zip(SHARDED, landed, pairs, strict=True)]
    reduced = dict(zip(SHARDED, exchange_halves(halves), strict=True))
    reduced["w_ada"], reduced["ssm_conv_w"] = grads["w_ada"][0], grads["ssm_conv_w"][0]
    reduced["w_in"] = reduced["w_in"].T
    reduced["w_ffn_gate"], reduced["w_ffn_up"] = reduced["w_ffn_in"][:FFN_SHARD], reduced["w_ffn_in"][FFN_SHARD:]
    for nm in ("w_ada", "ssm_conv_w", "w_in", "w_branch_a", "w_branch_b", "w_o", "w_ffn_gate", "w_ffn_up", "w_ffn_down"):
        flipped = nm in ("w_in", "w_ffn_gate", "w_ffn_up")
        work = (lambda a: a[0].T) if flipped else (lambda a: a[0])
        back = (lambda a: a.T[None]) if flipped else (lambda a: a[None])
        d_, m_, v_ = adam_update(work(given[nm]), reduced[nm], work(given["m_" + nm]), work(given["v_" + nm]), "adam_" + nm)
        grads[nm], deltas[nm], new_m[nm], new_v[nm] = back(reduced[nm]), back(d_), back(m_), back(v_)

    names = ("w_ada", "b_ada", "w_in", "hgrn_lb", "hgrn_gnorm", "ssm_conv_w", "ssm_conv_b", "ssm_dt_bias", "ssm_a_log",
             "ssm_d", "ssm_norm", "w_branch_a", "w_branch_b", "w_o", "ln1_g", "ln1_b", "w_ffn_gate", "w_ffn_up",
             "w_ffn_down", "ln2_g", "ln2_b")
    return (fin[2][0, 0], grad_x[None], *[grads[n] for n in names], *[deltas[n] for n in names],
            *[new_m[n] for n in names], *[new_v[n] for n in names])
```

```python
import functools

import jax
import jax.numpy as jnp
from jax import lax
from jax.experimental import pallas as pl
from jax.experimental.pallas import tpu as pltpu

F32, BF16 = jnp.float32, jnp.bfloat16
HI = lax.Precision.HIGHEST
MESH = pl.DeviceIdType.MESH

D = 1024
CHUNK = 64
LANES = 128
N_HEADS_A = 8
N_GROUPS_B = 4
B_INNER = 2048
CONV_DIM = 3072
D_FF = 2816
ALPHA = 2.0 ** 0.25
LN_EPS = 1e-5
RMS_EPS = 1e-6
ADAM_LR, ADAM_B1, ADAM_B2, ADAM_EPS, ADAM_WD, ADAM_STEP = 0.001, 0.9, 0.999, 1e-08, 0.01, 10

IN_ORIG = 11296
IN_PAD = 11520
COL_GA, COL_GB, COL_XBC, COL_Z, COL_DT = 4096, 5120, 6144, 9216, 11264
ORIG_Z, ORIG_XBC, ORIG_DT, ORIG_GA = 4096, 6144, 9216, 9248

SHARDED = ("w_in", "w_branch_a", "w_branch_b", "w_o", "w_ffn_in", "w_ffn_down")
FFN_SHARD = D_FF // 4
VMEM_LIMIT = 56 * 1024 * 1024
BLOCK_BYTES = 2 * 1024 * 1024
_DIMS = {"nn": (((1,), (0,)), ((), ())), "nt": (((1,), (1,)), ((), ())), "tn": (((0,), (0,)), ((), ()))}


def _bd(a, b, mode):
    return lax.dot_general(a.astype(BF16), b.astype(BF16), _DIMS[mode], preferred_element_type=F32)


@functools.partial(jax.custom_vjp, nondiff_argnums=(2,))
def bdot(a, b, mode):
    return _bd(a, b, mode)


def _bdot_fwd(a, b, mode):
    return _bd(a, b, mode), (a, b)


def _bdot_bwd(mode, res, g):
    a, b = res
    if mode == "nn":
        return _bd(g, b, "nt"), _bd(a, g, "tn")
    if mode == "nt":
        return _bd(g, b, "nn"), _bd(g, a, "tn")
    return _bd(b, g, "nt"), _bd(a, g, "nn")


bdot.defvjp(_bdot_fwd, _bdot_bwd)


def hdot(a, b, mode="nn"):
    return lax.dot_general(a, b, _DIMS[mode], precision=HI, preferred_element_type=F32)


def _raw(a, b, mode):
    return lax.dot_general(a, b, _DIMS[mode], preferred_element_type=F32)


def _split(x, n):
    parts, rest = [], x
    for _ in range(n):
        p = rest.astype(BF16)
        parts.append(p)
        rest = rest - p.astype(F32)
    return parts


def _od(a, b, mode, exact):
    if exact == 1:
        e = b.astype(BF16)
        p = _split(a, 3)
        return (_raw(p[2], e, mode) + _raw(p[1], e, mode)) + _raw(p[0], e, mode)
    e = a.astype(BF16)
    p = _split(b, 3)
    return (_raw(e, p[2], mode) + _raw(e, p[1], mode)) + _raw(e, p[0], mode)


@functools.partial(jax.custom_vjp, nondiff_argnums=(2, 3))
def odot(a, b, mode, exact):
    return _od(a, b, mode, exact)


def _odot_fwd(a, b, mode, exact):
    return _od(a, b, mode, exact), (a, b)


def _odot_bwd(mode, exact, res, g):
    a, b = res
    if exact == 1:
        da = {"nn": lambda: _od(g, b, "nt", 1), "nt": lambda: _od(g, b, "nn", 1), "tn": lambda: _od(b, g, "nt", 0)}[mode]()
        return da, jnp.zeros_like(b)
    db = {"nn": lambda: _od(a, g, "tn", 0), "nt": lambda: _od(g, a, "tn", 1), "tn": lambda: _od(a, g, "nn", 0)}[mode]()
    return jnp.zeros_like(a), db


odot.defvjp(_odot_fwd, _odot_bwd)


_BDIMS = {"bnn": (((2,), (1,)), ((0,), (0,))), "bnt": (((2,), (2,)), ((0,), (0,))), "btn": (((1,), (1,)), ((0,), (0,)))}


def _braw(a, b, mode):
    return lax.dot_general(a, b, _BDIMS[mode], preferred_element_type=F32)


def _bdb(a, b, mode):
    return _braw(a.astype(BF16), b.astype(BF16), mode)


def _d3b(a, b, mode):
    ah, al = _split(a, 2)
    bh, bl = _split(b, 2)
    return _braw(ah, bh, mode) + (_braw(ah, bl, mode) + _braw(al, bh, mode))


def _batched_bwd(f):
    def bwd(mode, res, g):
        a, b = res
        if mode == "bnn":
            return f(g, b, "bnt"), f(a, g, "btn")
        if mode == "bnt":
            return f(g, b, "bnn"), f(g, a, "btn")
        return f(b, g, "bnt"), f(a, g, "bnn")
    return bwd


@functools.partial(jax.custom_vjp, nondiff_argnums=(2,))
def bdot_b(a, b, mode):
    return _bdb(a, b, mode)


bdot_b.defvjp(lambda a, b, mode: (_bdb(a, b, mode), (a, b)), _batched_bwd(_bdb))


@functools.partial(jax.custom_vjp, nondiff_argnums=(2,))
def dot3_b(a, b, mode):
    return _d3b(a, b, mode)


dot3_b.defvjp(lambda a, b, mode: (_d3b(a, b, mode), (a, b)), _batched_bwd(_d3b))


def _cum(tril3, x, mode):
    e = tril3.astype(BF16)
    p = _split(x, 3)
    return (_braw(e, p[2], mode) + _braw(e, p[1], mode)) + _braw(e, p[0], mode)


@jax.custom_vjp
def chunk_cumsum(tril3, x):
    return _cum(tril3, x, "bnn")


chunk_cumsum.defvjp(lambda t, x: (_cum(t, x, "bnn"), t), lambda t, g: (jnp.zeros_like(t), _cum(t, g, "btn")))


def _unstack(axis, n):
    @jax.custom_vjp
    def un(x):
        return tuple(lax.index_in_dim(x, i, axis, keepdims=False) for i in range(n))

    un.defvjp(lambda x: (un(x), None), lambda _, g: (jnp.stack(g, axis=axis),))
    return un


def _split_last(n, w):
    @jax.custom_vjp
    def sp(x):
        return tuple(x[..., i * w:(i + 1) * w] for i in range(n))

    sp.defvjp(lambda x: (sp(x), None), lambda _, g: (jnp.concatenate(g, axis=-1),))
    return sp


def sigmoid(x):
    return 0.5 * jnp.tanh(0.5 * x) + 0.5


def silu(x):
    return x * sigmoid(x)


def softplus(x):
    return jnp.maximum(x, 0.0) + jnp.log1p(jnp.exp(jnp.minimum(x, -x)))


def _ln(x):
    mu = jnp.mean(x, axis=-1, keepdims=True)
    xc = x - mu
    return xc * lax.rsqrt(jnp.mean(xc * xc, axis=-1, keepdims=True) + LN_EPS)


def _tril64():
    r = lax.broadcasted_iota(jnp.int32, (CHUNK, CHUNK), 0)
    c = lax.broadcasted_iota(jnp.int32, (CHUNK, CHUNK), 1)
    return (r >= c).astype(F32)


def hgrn_block(q, fl, iv, gr, st, lb, gn):
    tb = q.shape[0]
    nc = tb // CHUNK
    nh = N_HEADS_A
    heads = _split_last(nh, LANES)
    to4 = lambda a: jnp.stack(heads(a), axis=0).reshape(nh, nc, CHUNK, LANES)
    flat = lambda a: a.reshape(nh * nc, CHUNK, LANES)
    f = lb + (1.0 - lb) * sigmoid(fl)
    gl4, k4, qf4, v4, gr4 = to4(jnp.log(f)), to4(1.0 - f), to4(silu(q) * (128 ** -0.5)), to4(iv), to4(gr)
    tril = _tril64()
    b4 = chunk_cumsum(jnp.broadcast_to(tril[None], (nh * nc, CHUNK, CHUNK)), flat(gl4)).reshape(gl4.shape)
    blast = jnp.sum(gl4, axis=2, keepdims=True)
    ref = lax.stop_gradient(0.5 * blast)
    qp, kp = qf4 * jnp.exp(b4 - ref), k4 * jnp.exp(ref - b4)
    sc = dot3_b(flat(qp), flat(kp), "bnt") * tril
    o_intra = bdot_b(sc, flat(v4), "bnn").reshape(gl4.shape)
    chunks = _unstack(1, nc)
    qe, v_c, kd, dec = chunks(qp * jnp.exp(ref)), chunks(v4), chunks(kp * jnp.exp(blast - ref)), chunks(jnp.exp(blast))
    o_inter = []
    for c in range(nc):
        o_inter.append(bdot_b(qe[c], st, "bnt"))
        st = st * dec[c] + bdot_b(v_c[c], kd[c], "btn")
    o = o_intra + jnp.stack(o_inter, axis=1)
    on = o * lax.rsqrt(jnp.mean(o * o, axis=-1, keepdims=True) + RMS_EPS) * gn
    out = (on * silu(gr4)).reshape(nh, tb, LANES)
    return jnp.concatenate(_unstack(0, nh)(out), axis=1), st


def ssd_consts(g):
    i32 = jnp.int32
    ej = lax.broadcasted_iota(i32, (LANES, 512), 0)
    ec = lax.broadcasted_iota(i32, (LANES, 512), 1)
    expand = (ej == g * 8 + (ec >> 6)).astype(F32)
    ts = lax.broadcasted_iota(i32, (CHUNK, 512), 0)
    tc = lax.broadcasted_iota(i32, (CHUNK, 512), 1)
    itile = (ts == (tc & 63)).astype(F32)
    maskall = ts >= (tc & 63)
    br = lax.broadcasted_iota(i32, (LANES, LANES), 0)
    bc = lax.broadcasted_iota(i32, (LANES, LANES), 1)
    blockmask = ((br >> 6) == (bc >> 6)).astype(F32)
    return expand, itile, maskall, blockmask, _tril64()


def ssd_block(x, bm, cm, dt, z, st, dtb, alog, dsk, nw, cs):
    expand, itile, maskall, blockmask, tril = cs
    tb = x.shape[0]
    nc = tb // CHUNK
    delta = odot(softplus(dt + dtb), expand, "nn", 1)
    a = -jnp.exp(alog) * delta
    xdt = x * delta
    by_chunk = lambda v: v.reshape(nc, CHUNK, v.shape[-1])
    a3, xdt3, bm3, cm3 = by_chunk(a), by_chunk(xdt), by_chunk(bm), by_chunk(cm)
    acum3 = chunk_cumsum(jnp.broadcast_to(tril[None], (nc, CHUNK, CHUNK)), a3)
    alast3 = jnp.sum(a3, axis=1, keepdims=True)
    cb3 = bdot_b(cm3, jnp.concatenate([bm3] * 8, axis=1), "bnt")
    arow3 = jnp.sum(acum3 * itile, axis=1, keepdims=True)
    dec3 = jnp.exp(jnp.where(maskall, acum3 - arow3, -1e30))
    pairs = _split_last(4, LANES)
    intra = [bdot_b(m, jnp.concatenate([xp] * 2, axis=1) * blockmask, "bnn")
             for m, xp in zip(pairs(cb3 * dec3), pairs(xdt3))]
    chunks = _unstack(0, nc)
    cm_c, bm_c, xw_c, dec_c = chunks(cm3), chunks(bm3), chunks(xdt3 * jnp.exp(alast3 - acum3)), chunks(jnp.exp(alast3))
    inter = []
    for c in range(nc):
        inter.append(bdot(cm_c[c], st, "nn"))
        st = st * dec_c[c] + bdot(bm_c[c], xw_c[c], "tn")
    st_new = st
    y = (jnp.concatenate(intra, axis=-1) + jnp.stack(inter, axis=0) * jnp.exp(acum3)).reshape(tb, 512)
    yz = (y + x * dsk) * silu(z)
    return yz * lax.rsqrt(jnp.mean(yz * yz, axis=-1, keepdims=True) + RMS_EPS) * nw, st_new


def adamw(w, g, m, v):
    m = ADAM_B1 * m + (1.0 - ADAM_B1) * g
    v = ADAM_B2 * v + (1.0 - ADAM_B2) * jnp.square(g)
    m_hat = m / (1.0 - ADAM_B1 ** ADAM_STEP)
    v_hat = v / (1.0 - ADAM_B2 ** ADAM_STEP)
    return -ADAM_LR * (m_hat / (jnp.sqrt(v_hat) + ADAM_EPS) + ADAM_WD * w), m, v


def _pick(n, cands):
    for c in cands:
        if n % c == 0:
            return c
    return n


def _params(sem):
    return pltpu.CompilerParams(dimension_semantics=sem, vmem_limit_bytes=VMEM_LIMIT)


MATMUL_VMEM_BUDGET = 50 * 1024 * 1024
MATMUL_MIN_STEPS = 4


def matmul(a, b, mode, out_dtype, name, after=None, b_cols=None):
    if mode == "nn":
        (m, k), n = a.shape, b.shape[1]
    elif mode == "nt":
        (m, k), n = a.shape, b.shape[0]
    else:
        (k, m), n = a.shape, b.shape[1]
    first_col, n = (0, n) if b_cols is None else b_cols
    a_bytes, b_bytes, out_bytes = a.dtype.itemsize, b.dtype.itemsize, jnp.dtype(out_dtype).itemsize
    k_sizes = (2304, 2048, 1408, 1024, 768, 512, 256, 128)
    usual_tk = _pick(k, k_sizes)

    def vmem(tm_, tn_, tk_):
        blocks = 2 * (tm_ * tk_ * a_bytes + tk_ * tn_ * b_bytes + tm_ * tn_ * out_bytes)
        return blocks + (tm_ * tn_ * 4 if tk_ < k else 0)

    def traffic(tm_, tn_, tk_):
        return (m // tm_) * k * n * b_bytes + (n // tn_ if tk_ < k else 1) * m * k * a_bytes

    sizes = (2304, 2048, 1920, 1408, 1024, 768, 512, 256, 128)
    tiles = [(tm_, tn_, tk_) for tm_ in sizes if m % tm_ == 0 for tn_ in sizes if n % tn_ == 0
             for tk_ in {k, usual_tk} if vmem(tm_, tn_, tk_) <= MATMUL_VMEM_BUDGET] or [(m, n, k)]
    pipelined = [t for t in tiles if (m // t[0]) * (n // t[1]) * (k // t[2]) >= MATMUL_MIN_STEPS]
    tm, tn, tk = min(pipelined or tiles, key=lambda t: (traffic(*t), t[2] != usual_tk, -t[0] * t[1]))
    nk = k // tk
    a_spec = pl.BlockSpec((tk, tm), lambda i, j, kk: (kk, i)) if mode == "tn" else pl.BlockSpec((tm, tk), lambda i, j, kk: (i, kk))
    assert first_col % tn == 0 and (mode != "nt" or b_cols is None)
    skip = first_col // tn
    b_spec = pl.BlockSpec((tn, tk), lambda i, j, kk: (j, kk)) if mode == "nt" else pl.BlockSpec((tk, tn), lambda i, j, kk: (kk, j + skip))

    order = [] if after is None else [after]

    def body(a_ref, b_ref, *rest):
        o_ref, *acc = rest[len(order):]
        part = _bd(a_ref[...], b_ref[...], mode)
        if nk == 1:
            o_ref[...] = part.astype(o_ref.dtype)
            return
        acc_ref, = acc
        kk = pl.program_id(2)

        @pl.when(kk == 0)
        def _():
            acc_ref[...] = part

        @pl.when(jnp.logical_and(kk > 0, kk < nk - 1))
        def _():
            acc_ref[...] += part

        @pl.when(kk == nk - 1)
        def _():
            o_ref[...] = (acc_ref[...] + part).astype(o_ref.dtype)

    return pl.pallas_call(
        body, name=name, grid=(m // tm, n // tn, nk),
        in_specs=[a_spec, b_spec] + [pl.BlockSpec(memory_space=pl.ANY) for _ in order],
        out_specs=pl.BlockSpec((tm, tn), lambda i, j, kk: (i, j)),
        out_shape=jax.ShapeDtypeStruct((m, n), out_dtype),
        scratch_shapes=[pltpu.VMEM((tm, tn), F32)] if nk > 1 else [],
        compiler_params=_params(("parallel", "parallel", "arbitrary")),
    )(a, b, *order)


def rowwise(name, fn, rows, consts, out_rows, out_accs=(), tm_max=512, into=None, new_wide=None):
    t = rows[0][0].shape[0]
    tm = _pick(t, (tm_max, 128, 64, 32, 16, 8))
    n_r, n_c, n_o = len(rows), len(consts), len(out_rows)
    n_alias = 0 if into is None else 1

    def body(*refs):
        r_in = [r[...] for r in refs[:n_r]]
        c_in = [r[...] for r in refs[n_r:n_r + n_c]]
        refs = refs[:n_r + n_c] + refs[n_r + n_c + n_alias:]
        o_refs = refs[n_r + n_c:n_r + n_c + n_o]
        a_refs = refs[n_r + n_c + n_o:]
        ro, ao = fn(r_in, c_in)
        for ref, val in zip(o_refs, ro, strict=True):
            ref[...] = val.astype(ref.dtype)
        if a_refs:
            @pl.when(pl.program_id(0) == 0)
            def _():
                for ref in a_refs:
                    ref[...] = jnp.zeros_like(ref)

            for ref, val in zip(a_refs, ao, strict=True):
                ref[...] += val

    in_specs = [pl.BlockSpec((tm, w), functools.partial(lambda i, cb: (i, cb), cb=cb)) for _, w, cb in rows]
    in_specs += [pl.BlockSpec(c.shape, lambda i: (0, 0)) for c in consts]
    out_specs = [pl.BlockSpec((tm, w), lambda i: (i, 0)) for w, _ in out_rows]
    out_specs += [pl.BlockSpec(s, lambda i: (0, 0)) for s in out_accs]
    out_shape = [jax.ShapeDtypeStruct((t, w), dt) for w, dt in out_rows]
    out_shape += [jax.ShapeDtypeStruct(s, F32) for s in out_accs]
    operands = [r[0] for r in rows] + list(consts)
    aliases = {}
    if into is not None:
        target, cb = into
        in_specs.append(pl.BlockSpec(memory_space=pl.ANY))
        operands.append(target)
        out_specs[0] = pl.BlockSpec((tm, out_rows[0][0]), lambda i: (i, cb))
        out_shape[0] = jax.ShapeDtypeStruct(target.shape, target.dtype)
        aliases = {len(operands) - 1: 0}
    if new_wide is not None:
        width, cb = new_wide
        out_specs[0] = pl.BlockSpec((tm, out_rows[0][0]), lambda i: (i, cb))
        out_shape[0] = jax.ShapeDtypeStruct((t, width), out_rows[0][1])
    return pl.pallas_call(
        body, name=name, grid=(t // tm,), in_specs=in_specs, out_specs=out_specs, out_shape=out_shape,
        input_output_aliases=aliases, compiler_params=_params(("arbitrary",)),
    )(*operands)


def _full(a):
    return (a, a.shape[1], 0)


HGRN_TIME_BLOCK = 256
SSD_TIME_BLOCK = 512


def _time_block(t, most=HGRN_TIME_BLOCK):
    return _pick(t, tuple(b for b in (512, 256, 128, 64) if b <= most))


def _quarters(ref):
    return [ref[:, seg * D:(seg + 1) * D] for seg in range(4)]


def hgrn_forward(proj, lb, gn):
    t = proj.shape[0]
    tb = _time_block(t)
    nb = t // tb

    def body(qfig_ref, lb_ref, gn_ref, o_ref, st_ref, state):
        @pl.when(pl.program_id(0) == 0)
        def _():
            state[...] = jnp.zeros_like(state)

        st = state[...]
        st_ref[...] = st
        out, st_new = hgrn_block(*_quarters(qfig_ref), st, lb_ref[...], gn_ref[...])
        o_ref[...] = out.astype(o_ref.dtype)
        state[...] = st_new

    return pl.pallas_call(
        body, name="hgrn_forward", grid=(nb,),
        in_specs=[pl.BlockSpec((tb, 4 * D), lambda j: (j, 0)),
                  pl.BlockSpec((1, D), lambda j: (0, 0)), pl.BlockSpec((1, LANES), lambda j: (0, 0))],
        out_specs=[pl.BlockSpec((tb, D), lambda j: (j, 0)),
                   pl.BlockSpec((None, N_HEADS_A, LANES, LANES), lambda j: (j, 0, 0, 0))],
        out_shape=[jax.ShapeDtypeStruct((t, D), BF16),
                   jax.ShapeDtypeStruct((nb, N_HEADS_A, LANES, LANES), F32)],
        scratch_shapes=[pltpu.VMEM((N_HEADS_A, LANES, LANES), F32)],
        compiler_params=_params(("arbitrary",)),
    )(proj, lb, gn)


def hgrn_backward(proj, states, d_out, lb, gn, d_proj):
    t = proj.shape[0]
    tb = _time_block(t)
    nb = t // tb

    def body(qfig_ref, st_ref, do_ref, lb_ref, gn_ref, _, dqfig_ref, dlb_ref, dgn_ref, d_state):
        @pl.when(pl.program_id(0) == 0)
        def _():
            d_state[...] = jnp.zeros_like(d_state)
            dlb_ref[...] = jnp.zeros_like(dlb_ref)
            dgn_ref[...] = jnp.zeros_like(dgn_ref)

        _, vjp = jax.vjp(hgrn_block, *_quarters(qfig_ref), st_ref[...], lb_ref[...], gn_ref[...])
        dq, df, di, dg, dst, dlb, dgn = vjp((do_ref[...], d_state[...]))
        for seg, val in enumerate((dq, df, di, dg)):
            dqfig_ref[:, seg * D:(seg + 1) * D] = val.astype(dqfig_ref.dtype)
        d_state[...] = dst
        dlb_ref[...] += dlb
        dgn_ref[...] += dgn

    rev = lambda j: nb - 1 - j
    return pl.pallas_call(
        body, name="hgrn_backward", grid=(nb,),
        in_specs=[pl.BlockSpec((tb, 4 * D), lambda j: (rev(j), 0)),
                  pl.BlockSpec((None, N_HEADS_A, LANES, LANES), lambda j: (rev(j), 0, 0, 0)),
                  pl.BlockSpec((tb, D), lambda j: (rev(j), 0)),
                  pl.BlockSpec((1, D), lambda j: (0, 0)), pl.BlockSpec((1, LANES), lambda j: (0, 0)),
                  pl.BlockSpec(memory_space=pl.ANY)],
        out_specs=[pl.BlockSpec((tb, 4 * D), lambda j: (rev(j), 0)),
                   pl.BlockSpec((1, D), lambda j: (0, 0)), pl.BlockSpec((1, LANES), lambda j: (0, 0))],
        out_shape=[jax.ShapeDtypeStruct(d_proj.shape, d_proj.dtype), jax.ShapeDtypeStruct((1, D), F32),
                   jax.ShapeDtypeStruct((1, LANES), F32)],
        input_output_aliases={5: 0},
        scratch_shapes=[pltpu.VMEM((N_HEADS_A, LANES, LANES), F32)],
        compiler_params=_params(("arbitrary",)),
    )(proj, states, d_out, lb, gn, d_proj)


def _ssd_in_specs(tb, tmap):
    return [pl.BlockSpec((tb, 512), lambda g, j: (tmap(j), g)),
            pl.BlockSpec((tb, LANES), lambda g, j: (tmap(j), 16 + g)),
            pl.BlockSpec((tb, LANES), lambda g, j: (tmap(j), 20 + g)),
            pl.BlockSpec((tb, LANES), lambda g, j: (tmap(j), COL_DT // LANES)),
            pl.BlockSpec((tb, 512), lambda g, j: (tmap(j), COL_Z // 512 + g))]


def ssd_forward(xc, proj, dtb, alog, dsk, nw):
    t = proj.shape[0]
    tb = _time_block(t, SSD_TIME_BLOCK)
    nb = t // tb

    def body(x_ref, b_ref, c_ref, dt_ref, z_ref, dtb_ref, alog_ref, dsk_ref, nw_ref, o_ref, st_ref, state):
        @pl.when(pl.program_id(1) == 0)
        def _():
            state[...] = jnp.zeros_like(state)

        st = state[...]
        st_ref[...] = st
        out, st_new = ssd_block(x_ref[...], b_ref[...], c_ref[...], dt_ref[...], z_ref[...], st,
                                dtb_ref[...], alog_ref[...], dsk_ref[...], nw_ref[...], ssd_consts(pl.program_id(0)))
        o_ref[...] = out.astype(o_ref.dtype)
        state[...] = st_new

    vec = pl.BlockSpec((1, 512), lambda g, j: (0, g))
    heads = pl.BlockSpec((1, LANES), lambda g, j: (0, 0))
    return pl.pallas_call(
        body, name="ssd_forward", grid=(N_GROUPS_B, nb),
        in_specs=_ssd_in_specs(tb, lambda j: j) + [heads, vec, vec, vec],
        out_specs=[pl.BlockSpec((tb, 512), lambda g, j: (j, g)),
                   pl.BlockSpec((None, None, LANES, 512), lambda g, j: (j, g, 0, 0))],
        out_shape=[jax.ShapeDtypeStruct((t, B_INNER), BF16),
                   jax.ShapeDtypeStruct((nb, N_GROUPS_B, LANES, 512), F32)],
        scratch_shapes=[pltpu.VMEM((LANES, 512), F32)],
        compiler_params=_params(("arbitrary", "arbitrary")),
    )(xc, xc, xc, proj, proj, dtb, alog, dsk, nw)


def ssd_backward(xc, proj, states, d_out, dtb, alog, dsk, nw, d_proj):
    t = proj.shape[0]
    tb = _time_block(t, SSD_TIME_BLOCK)
    nb = t // tb
    rev = lambda j: nb - 1 - j

    def body(x_ref, b_ref, c_ref, dt_ref, z_ref, st_ref, do_ref, dtb_ref, alog_ref, dsk_ref, nw_ref, _,
             dx_ref, db_ref, dc_ref, ddt_ref, dz_ref, ddtb_ref, dalog_ref, ddsk_ref, dnw_ref, d_state):
        accs = (ddtb_ref, dalog_ref, ddsk_ref, dnw_ref)

        @pl.when(pl.program_id(1) == 0)
        def _():
            d_state[...] = jnp.zeros_like(d_state)
            for ref in accs:
                ref[...] = jnp.zeros_like(ref)

        cs = ssd_consts(pl.program_id(0))
        fn = lambda *a: ssd_block(*a, cs)
        _, vjp = jax.vjp(fn, x_ref[...], b_ref[...], c_ref[...], dt_ref[...], z_ref[...], st_ref[...],
                         dtb_ref[...], alog_ref[...], dsk_ref[...], nw_ref[...])
        dx, db, dc, ddt, dz, dst, *dpar = vjp((do_ref[...], d_state[...]))
        dx_ref[...] = dx
        db_ref[...] = db
        dc_ref[...] = dc
        ddt_ref[...] = ddt
        dz_ref[...] = dz.astype(dz_ref.dtype)
        d_state[...] = dst
        for ref, val in zip(accs, dpar, strict=True):
            ref[...] += val

    vec = pl.BlockSpec((1, 512), lambda g, j: (0, g))
    heads = pl.BlockSpec((1, LANES), lambda g, j: (0, 0))
    acc = pl.BlockSpec((None, 1, 512), lambda g, j: (g, 0, 0))
    acc_heads = pl.BlockSpec((None, 1, LANES), lambda g, j: (g, 0, 0))
    return pl.pallas_call(
        body, name="ssd_backward", grid=(N_GROUPS_B, nb),
        in_specs=_ssd_in_specs(tb, rev)
        + [pl.BlockSpec((None, None, LANES, 512), lambda g, j: (rev(j), g, 0, 0)),
           pl.BlockSpec((tb, 512), lambda g, j: (rev(j), g))] + [heads, vec, vec, vec] + [pl.BlockSpec(memory_space=pl.ANY)],
        out_specs=[pl.BlockSpec((tb, 512), lambda g, j: (rev(j), g)),
                   pl.BlockSpec((tb, LANES), lambda g, j: (rev(j), g)),
                   pl.BlockSpec((tb, LANES), lambda g, j: (rev(j), g)),
                   pl.BlockSpec((None, tb, LANES), lambda g, j: (g, rev(j), 0)),
                   pl.BlockSpec((tb, 512), lambda g, j: (rev(j), COL_Z // 512 + g)), acc_heads, acc, acc, acc],
        out_shape=[jax.ShapeDtypeStruct((t, B_INNER), F32), jax.ShapeDtypeStruct((t, 512), F32),
                   jax.ShapeDtypeStruct((t, 512), F32), jax.ShapeDtypeStruct((N_GROUPS_B, t, LANES), F32),
                   jax.ShapeDtypeStruct(d_proj.shape, d_proj.dtype)]
        + [jax.ShapeDtypeStruct((N_GROUPS_B, 1, LANES), F32)] + [jax.ShapeDtypeStruct((N_GROUPS_B, 1, 512), F32)] * 3,
        input_output_aliases={11: 4},
        scratch_shapes=[pltpu.VMEM((LANES, 512), F32)],
        compiler_params=_params(("arbitrary", "arbitrary")),
    )(xc, xc, xc, proj, proj, states, d_out, dtb, alog, dsk, nw, d_proj)


CONV_HALO = 8


def _shift_down(halo_then_tile, s, tm):
    if s == 0:
        return halo_then_tile[CONV_HALO:CONV_HALO + tm]
    return pltpu.roll(halo_then_tile, s, 0)[CONV_HALO:CONV_HALO + tm]


def _conv_pre(cur, prev, w, b, tm):
    stacked = jnp.concatenate([prev, cur], axis=0)
    taps = [_shift_down(stacked, 3 - j, tm) for j in range(4)]
    pre = b + taps[0] * w[0:1] + taps[1] * w[1:2] + taps[2] * w[2:3] + taps[3] * w[3:4]
    return pre, taps


def _conv_specs(t, tm):
    per = tm // CONV_HALO
    cur = pl.BlockSpec((tm, CONV_DIM), lambda i: (i, COL_XBC // CONV_DIM))
    prev = pl.BlockSpec((CONV_HALO, CONV_DIM), lambda i: (jnp.maximum(i * per - 1, 0), COL_XBC // CONV_DIM))
    return cur, prev


def conv_forward(proj, w, b):
    t = proj.shape[0]
    tm = _pick(t, (256, 128, 64))

    def body(cur_ref, prev_ref, w_ref, b_ref, o_ref):
        prev = jnp.where(pl.program_id(0) == 0, 0.0, prev_ref[...])
        pre, _ = _conv_pre(cur_ref[...], prev, w_ref[...], b_ref[...], tm)
        o_ref[...] = silu(pre)

    cur, prev = _conv_specs(t, tm)
    return pl.pallas_call(
        body, name="conv_forward", grid=(t // tm,),
        in_specs=[cur, prev, pl.BlockSpec((4, CONV_DIM), lambda i: (0, 0)), pl.BlockSpec((1, CONV_DIM), lambda i: (0, 0))],
        out_specs=pl.BlockSpec((tm, CONV_DIM), lambda i: (i, 0)),
        out_shape=jax.ShapeDtypeStruct((t, CONV_DIM), F32),
        compiler_params=_params(("arbitrary",)),
    )(proj, proj, w, b)


def conv_backward(proj, dx, db_, dc_, w, b, d_proj):
    t = proj.shape[0]
    tm = _pick(t, (256, 128, 64))
    per = tm // CONV_HALO
    nt = t // tm
    rev = lambda i: nt - 1 - i

    def body(cur_ref, prev_ref, dx_ref, dbm_ref, dcm_ref, w_ref, b_ref, _, o_ref, dw_ref, dbias_ref, later):
        @pl.when(pl.program_id(0) == 0)
        def _():
            dw_ref[...] = jnp.zeros_like(dw_ref)
            dbias_ref[...] = jnp.zeros_like(dbias_ref)
            later[...] = jnp.zeros_like(later)

        first_tile = pl.program_id(0) == nt - 1
        for lo, hi, src in ((0, B_INNER, dx_ref), (B_INNER, B_INNER + 512, dbm_ref), (B_INNER + 512, CONV_DIM, dcm_ref)):
            cols = slice(lo, hi)
            prev = jnp.where(first_tile, 0.0, prev_ref[:, cols])
            w_ = w_ref[:, cols]
            pre, taps = _conv_pre(cur_ref[:, cols], prev, w_, b_ref[:, cols], tm)
            sg = sigmoid(pre)
            dpre = src[...] * (sg * (1.0 + pre * (1.0 - sg)))
            dbias_ref[:, cols] += jnp.sum(dpre, axis=0, keepdims=True)
            for j in range(4):
                dw_ref[j:j + 1, cols] += jnp.sum(dpre * taps[j], axis=0, keepdims=True)
            stacked = jnp.concatenate([dpre, later[:, cols]], axis=0)
            acc = dpre * w_[3:4]
            for j in range(3):
                acc = acc + pltpu.roll(stacked, tm + CONV_HALO - (3 - j), 0)[0:tm] * w_[j:j + 1]
            o_ref[:, cols] = acc.astype(o_ref.dtype)
            later[:, cols] = dpre[0:CONV_HALO]

    row = lambda w_: pl.BlockSpec((tm, w_), lambda i: (rev(i), 0))
    whole = lambda r: pl.BlockSpec((r, CONV_DIM), lambda i: (0, 0))
    return pl.pallas_call(
        body, name="conv_backward", grid=(nt,),
        in_specs=[pl.BlockSpec((tm, CONV_DIM), lambda i: (rev(i), COL_XBC // CONV_DIM)),
                  pl.BlockSpec((CONV_HALO, CONV_DIM), lambda i: (jnp.maximum(rev(i) * per - 1, 0), COL_XBC // CONV_DIM)),
                  row(B_INNER), row(512), row(512), whole(4), whole(1), pl.BlockSpec(memory_space=pl.ANY)],
        out_specs=[pl.BlockSpec((tm, CONV_DIM), lambda i: (rev(i), COL_XBC // CONV_DIM)), whole(4), whole(1)],
        out_shape=[jax.ShapeDtypeStruct(d_proj.shape, d_proj.dtype), jax.ShapeDtypeStruct((4, CONV_DIM), F32),
                   jax.ShapeDtypeStruct((1, CONV_DIM), F32)],
        input_output_aliases={7: 0},
        scratch_shapes=[pltpu.VMEM((CONV_HALO, CONV_DIM), F32)],
        compiler_params=_params(("arbitrary",)),
    )(proj, proj, dx, db_, dc_, w, b, d_proj)


def stage_modulate(x, sc, sh):
    return _ln(x) * (1.0 + sc) + sh


def stage_merge(ga, gb, ya, yb):
    return sigmoid(ga) * ya + sigmoid(gb) * yb


def stage_post_mixer(x, h, g1, ln_g, ln_b, sc2, sh2):
    x1 = _ln(ALPHA * x + g1 * h) * ln_g + ln_b
    return x1, _ln(x1) * (1.0 + sc2) + sh2


def stage_swiglu(a, b):
    return silu(a) * b


def gate_up(ab):
    w = FFN_SHARD
    return (jnp.concatenate([ab[:, 2 * w * k:2 * w * k + w] for k in range(4)], axis=1),
            jnp.concatenate([ab[:, 2 * w * k + w:2 * w * (k + 1)] for k in range(4)], axis=1))


def per_chip(gate, up):
    w = FFN_SHARD
    return jnp.concatenate([part[:, w * k:w * (k + 1)] for k in range(4) for part in (gate, up)], axis=1)


def stage_loss(x1, hf, tgt, g2, ln_g, ln_b):
    x2 = _ln(ALPHA * x1 + g2 * hf) * ln_g + ln_b
    return 0.5 * jnp.sum(jnp.mean(jnp.square(x2 - tgt), axis=-1, keepdims=True), axis=0, keepdims=True)


def local_step(x, tgt, mod, wts, small, early=None, mid=None, late=None, last=None):
    sh1, sc1, g1, sh2, sc2, g2 = mod
    lb, gn, conv_w, conv_b, dtb, alog, dsk, nw, ln1_g, ln1_b, ln2_g, ln2_b = small
    vec = (1, D)

    (u1,) = rowwise("modulate1", lambda r, c: ((stage_modulate(r[0], *c),), ()), [_full(x)], [sc1, sh1], [(D, BF16)])
    w_in = wts.input_projection(u1)
    proj = matmul(u1, w_in, "nn", F32, "in_proj")
    ya_in, st_a = hgrn_forward(proj, lb, gn + wts.start_rest(proj)[0:1])
    xc = conv_forward(proj, conv_w, conv_b)
    w_a, w_b, w_o, w_gu, w_d = wts.rest(xc)
    yb_in, st_b = ssd_forward(xc, proj, dtb, alog, dsk, nw)
    ya = matmul(ya_in, w_a, "nn", F32, "branch_a")
    yb = matmul(yb_in, w_b, "nn", F32, "branch_b")
    gate_rows = [(proj, D, COL_GA // D), (proj, D, COL_GB // D), _full(ya), _full(yb)]
    (merged,) = rowwise("merge", lambda r, c: ((stage_merge(*r),), ()), gate_rows, [], [(D, BF16)])
    h = matmul(merged, w_o, "nn", F32, "out_proj")
    post_consts = [g1, ln1_g, ln1_b, sc2, sh2]
    x1, u2 = rowwise("post_mixer", lambda r, c: (stage_post_mixer(*r, *c), ()), [_full(x), _full(h)], post_consts,
                     [(D, F32), (D, BF16)])
    ab = matmul(u2, w_gu, "nt", F32, "ffn_in")
    (p,) = rowwise("swiglu", lambda r, c: ((stage_swiglu(*gate_up(r[0])),), ()), [_full(ab)], [], [(D_FF, BF16)],
                   tm_max=256)
    hf = matmul(p, w_d, "nn", F32, "ffn_out")

    def loss_bwd(r, c):
        loss, vjp = jax.vjp(stage_loss, *r, *c)
        dx1, dhf, _, dg2, dlg, dlb_ = vjp(jnp.ones((1, 1), F32))
        return (dx1, dhf), (loss, dg2, dlg, dlb_)

    dx1, dhf, loss, dg2, dln2_g, dln2_b = rowwise(
        "loss_backward", loss_bwd, [_full(x1), _full(hf), _full(tgt)], [g2, ln2_g, ln2_b],
        [(D, F32), (D, BF16)], [(1, 1), vec, vec, vec])
    dp = matmul(dhf, w_d, "nt", F32, "ffn_out_dx")
    dw_d = matmul(p, dhf, "tn", F32, "ffn_out_dw")

    def swiglu_bwd(r, c):
        _, vjp = jax.vjp(stage_swiglu, *gate_up(r[0]))
        return (per_chip(*vjp(r[1])),), ()

    (dab,) = rowwise("swiglu_backward", swiglu_bwd, [_full(ab), _full(dp)], [], [(2 * D_FF, BF16)], tm_max=256)
    du2 = matmul(dab, w_gu, "nn", F32, "ffn_in_dx")
    dw_gu = matmul(dab, u2, "tn", F32, "ffn_in_dw")

    def post_bwd(r, c):
        _, vjp = jax.vjp(stage_post_mixer, r[0], r[1], *c)
        dx, dh, *dc = vjp((r[2], r[3]))
        return (dx, dh), tuple(dc)

    dx_a, dh, dg1, dln1_g, dln1_b, dsc2, dsh2 = rowwise(
        "post_mixer_backward", post_bwd, [_full(x), _full(h), _full(dx1), _full(du2)], post_consts,
        [(D, F32), (D, BF16)], [vec] * 5)
    dmerged = matmul(dh, w_o, "nt", F32, "out_proj_dx")
    dw_o = matmul(merged, dh, "tn", F32, "out_proj_dw")

    def merge_bwd(r, c):
        _, vjp = jax.vjp(stage_merge, *r[:4])
        dga, dgb, dya, dyb = vjp(r[4])
        return (jnp.concatenate([dga, dgb], axis=1), dya, dyb), ()

    dproj, dya, dyb = rowwise("merge_backward", merge_bwd, gate_rows + [_full(dmerged)], [],
                              [(2 * D, BF16), (D, BF16), (D, BF16)], new_wide=(IN_PAD, COL_GA // (2 * D)))
    dya_in = matmul(dya, w_a, "nt", F32, "branch_a_dx")
    dw_a = matmul(ya_in, dya, "tn", F32, "branch_a_dw")
    dyb_in = matmul(dyb, w_b, "nt", F32, "branch_b_dx")
    dw_b = matmul(yb_in, dyb, "tn", F32, "branch_b_dw")
    gn_after = gn if early is None else gn + early((dw_a, dw_b, dw_o, dw_gu, dw_d))[0:1]
    dproj, dlb, dgn = hgrn_backward(proj, st_a, dya_in, lb, gn_after, dproj)
    dtb_after = dtb if mid is None else dtb + mid(dlb)[0:1, 0:1]
    dxs, dbm, dcm, ddt, dproj, ddtb, dalog, ddsk, dnw = ssd_backward(xc, proj, st_b, dyb_in, dtb_after, alog, dsk, nw, dproj)
    dproj, dconv_w, dconv_b = conv_backward(proj, dxs, dbm, dcm, conv_w, conv_b, dproj)
    if late is not None:
        late(dconv_b)
    t = x.shape[0]
    tail = jnp.concatenate([jnp.sum(ddt, axis=0).astype(BF16), jnp.zeros((t, IN_PAD - COL_DT - LANES), BF16)], axis=1)
    dproj = lax.dynamic_update_slice(dproj, tail, (0, COL_DT))
    if last is None:
        dw_in, started = matmul(u1, dproj, "tn", F32, "in_proj_dw"), None
    else:
        dw_in, started = None, last(u1, dproj)
    du1 = matmul(dproj, w_in, "nt", F32, "in_proj_dx", after=started)

    def mod_bwd(r, c):
        _, vjp = jax.vjp(stage_modulate, r[0], *c)
        dx, dsc, dsh = vjp(r[1])
        return (dx + r[2],), (dsc, dsh)

    grad_x, dsc1, dsh1 = rowwise("modulate1_backward", mod_bwd, [_full(x), _full(du1), _full(dx_a)], [sc1, sh1],
                                 [(D, F32)], [vec, vec])
    d_mod = (dsh1, dsc1, dg1, dsh2, dsc2, dg2)
    d_wts = (dw_in, dw_a, dw_b, dw_o, dw_gu, dw_d)
    d_small = (dlb, dgn, dconv_w, dconv_b, jnp.sum(ddtb, axis=0),
               dalog.reshape(1, B_INNER), ddsk.reshape(1, B_INNER), dnw.reshape(1, B_INNER),
               dln1_g, dln1_b, dln2_g, dln2_b)
    return loss, grad_x, d_mod, d_wts, d_small


HBM = pl.BlockSpec(memory_space=pltpu.HBM)
SEM = pl.BlockSpec(memory_space=pltpu.SEMAPHORE)
DATAFLOW = pltpu.SideEffectType.DATAFLOW_SIDE_EFFECTING


def _place():
    return lax.axis_index("x"), lax.axis_index("y"), lax.axis_index("c")


def _other_chips(x, y):
    return [(1 - x, y), (x, 1 - y), (1 - x, 1 - y)]


def _remote(src, dst, send_sem, recv_sem, device):
    return pltpu.make_async_remote_copy(src_ref=src, dst_ref=dst, send_sem=send_sem, recv_sem=recv_sem,
                                        device_id=device, device_id_type=MESH)


def gather_rows(v, name):
    n = v.shape[1]

    def body(v_ref, out_ref, send_sems, recv_sems, local_sem):
        x, y, c = _place()
        mine = pltpu.make_async_copy(v_ref, out_ref.at[4 * x + 2 * y + c], local_sem)
        mine.start()
        sends, recvs = [], []
        for m in range(1, 8):
            px = 1 - x if m & 4 else x
            py = 1 - y if m & 2 else y
            pc = 1 - c if m & 1 else c
            sends.append(_remote(v_ref, out_ref.at[4 * x + 2 * y + c], send_sems.at[m - 1], recv_sems.at[m - 1], (px, py, pc)))
            recvs.append(_remote(v_ref, out_ref.at[4 * px + 2 * py + pc], send_sems.at[m - 1], recv_sems.at[m - 1], (px, py, pc)))
        for cp in sends:
            cp.start()
        for cp in recvs:
            cp.wait_recv()
        for cp in sends:
            cp.wait_send()
        mine.wait()

    return pl.pallas_call(
        body, name=name, in_specs=[HBM], out_specs=HBM,
        out_shape=jax.ShapeDtypeStruct((8, 1, n), v.dtype),
        scratch_shapes=[pltpu.SemaphoreType.DMA((7,)), pltpu.SemaphoreType.DMA((7,)), pltpu.SemaphoreType.DMA],
    )(v)


def exchange_rows(part, name):
    w = part.shape[2]

    def body(p_ref, out_ref, send_sems, recv_sems, local_sem):
        x, y, c = _place()
        k = 2 * x + y
        mine = pltpu.make_async_copy(p_ref.at[4 * x + 2 * y + c], out_ref.at[k], local_sem)
        mine.start()
        sends, recvs = [], []
        for j, (px, py) in enumerate(_other_chips(x, y)):
            sends.append(_remote(p_ref.at[4 * px + 2 * py + c], out_ref.at[k], send_sems.at[j], recv_sems.at[j], (px, py, c)))
            recvs.append(_remote(p_ref.at[4 * px + 2 * py + c], out_ref.at[2 * px + py], send_sems.at[j], recv_sems.at[j], (px, py, c)))
        for cp in sends:
            cp.start()
        for cp in recvs:
            cp.wait_recv()
        for cp in sends:
            cp.wait_send()
        mine.wait()

    return pl.pallas_call(
        body, name=name, in_specs=[HBM], out_specs=HBM,
        out_shape=jax.ShapeDtypeStruct((4, 1, w), part.dtype),
        scratch_shapes=[pltpu.SemaphoreType.DMA((3,)), pltpu.SemaphoreType.DMA((3,)), pltpu.SemaphoreType.DMA],
    )(part)


def _half_of_slot(ref, rows, px, py, pc):
    return ref.at[2 * px + py, pl.ds(pc * (rows // 2), rows // 2), :]


def gather_start(shards, after, tag):
    n = len(shards)

    def body(*refs):
        w_refs, land_refs = refs[:n], refs[n:2 * n]
        send_sems, recv_sems = refs[2 * n + 1], refs[2 * n + 2]
        token = refs[-1]
        x, y, c = _place()
        for i in range(n):
            rows = shards[i].shape[0]
            for j, (px, py) in enumerate(_other_chips(x, y)):
                _remote(w_refs[i].at[pl.ds(c * (rows // 2), rows // 2), :], _half_of_slot(land_refs[i], rows, x, y, c),
                        send_sems.at[j * n + i], recv_sems.at[j * n + i], (px, py, c)).start()
        token[...] = jnp.zeros_like(token)

    hbm = lambda a: pltpu.with_memory_space_constraint(a, pltpu.HBM)
    lands = [lax.empty((4,) + s.shape, s.dtype) for s in shards]
    dma = pltpu.SemaphoreType.DMA
    return pl.pallas_call(
        body, name="gather_start_" + tag,
        out_shape=(dma((3 * n,)), dma((3 * n,)),
                   *[pltpu.HBM(a.shape, a.dtype) for a in list(shards) + lands], jax.ShapeDtypeStruct((8, LANES), F32)),
        in_specs=[HBM] * (2 * n) + [pl.BlockSpec(memory_space=pl.ANY)],
        out_specs=(SEM, SEM, *[HBM] * (2 * n), pl.BlockSpec(memory_space=pltpu.VMEM)),
        input_output_aliases={i: 2 + i for i in range(2 * n)},
        compiler_params=pltpu.CompilerParams(has_side_effects=DATAFLOW),
    )(*[hbm(a) for a in list(shards) + lands], after)


def gather_wait(send_sems, recv_sems, shards, lands, after, tag):
    n = len(shards)

    def body(*refs):
        w_refs, land_refs = refs[:n], refs[n:2 * n]
        send_ref, recv_ref = refs[2 * n], refs[2 * n + 1]
        x, y, c = _place()
        for i in range(n):
            rows = shards[i].shape[0]
            for j, (px, py) in enumerate(_other_chips(x, y)):
                cp = _remote(w_refs[i].at[pl.ds(c * (rows // 2), rows // 2), :], _half_of_slot(land_refs[i], rows, px, py, c),
                             send_ref.at[j * n + i], recv_ref.at[j * n + i], (px, py, c))
                cp.wait_send()
                cp.wait_recv()

    out = pl.pallas_call(
        body, name="gather_wait_" + tag,
        out_shape=tuple(pltpu.HBM(a.shape, a.dtype) for a in list(shards) + list(lands)),
        in_specs=[HBM] * (2 * n) + [SEM, SEM, pl.BlockSpec(memory_space=pl.ANY)], out_specs=tuple([HBM] * (2 * n)),
        input_output_aliases={i: i for i in range(2 * n)},
        compiler_params=pltpu.CompilerParams(has_side_effects=DATAFLOW),
    )(*shards, *lands, send_sems, recv_sems, after)
    return list(out[:n]), list(out[n:])


def forward_start(lands, tag):
    n = len(lands)

    def body(*refs):
        land_refs = refs[:n]
        send_sems, recv_sems = refs[n], refs[n + 1]
        token = refs[-1]
        x, y, c = _place()
        for i in range(n):
            rows = lands[i].shape[1]
            for j, (px, py) in enumerate(_other_chips(x, y)):
                mine = _half_of_slot(land_refs[i], rows, px, py, c)
                _remote(mine, mine, send_sems.at[j * n + i], recv_sems.at[j * n + i], (x, y, 1 - c)).start()
        token[...] = jnp.zeros_like(token)

    dma = pltpu.SemaphoreType.DMA
    return pl.pallas_call(
        body, name="forward_start_" + tag,
        out_shape=(dma((3 * n,)), dma((3 * n,)), *[pltpu.HBM(a.shape, a.dtype) for a in lands],
                   jax.ShapeDtypeStruct((8, LANES), F32)),
        in_specs=[HBM] * n, out_specs=(SEM, SEM, *[HBM] * n, pl.BlockSpec(memory_space=pltpu.VMEM)),
        input_output_aliases={i: 2 + i for i in range(n)},
        compiler_params=pltpu.CompilerParams(has_side_effects=DATAFLOW),
    )(*lands)


def forward_wait(started, after, tag):
    send_sems, recv_sems, *rest = started
    lands = rest[:-1]
    n = len(lands)

    def body(*refs):
        land_refs = refs[:n]
        send_ref, recv_ref = refs[n], refs[n + 1]
        x, y, c = _place()
        for i in range(n):
            rows = lands[i].shape[1]
            for j, (px, py) in enumerate(_other_chips(x, y)):
                cp = _remote(_half_of_slot(land_refs[i], rows, px, py, c), _half_of_slot(land_refs[i], rows, px, py, 1 - c),
                             send_ref.at[j * n + i], recv_ref.at[j * n + i], (x, y, 1 - c))
                cp.wait_send()
                cp.wait_recv()

    out = pl.pallas_call(
        body, name="forward_wait_" + tag,
        out_shape=tuple(pltpu.HBM(a.shape, a.dtype) for a in lands),
        in_specs=[HBM] * n + [SEM, SEM, pl.BlockSpec(memory_space=pl.ANY)], out_specs=tuple([HBM] * n),
        input_output_aliases={i: i for i in range(n)},
        compiler_params=pltpu.CompilerParams(has_side_effects=DATAFLOW),
    )(*lands, send_sems, recv_sems, after)
    return list(out)


def pair_start(slabs, tag):
    n = len(slabs)

    def body(*refs):
        g_refs, land_refs = refs[:n], refs[n:2 * n]
        send_sems, recv_sems = refs[2 * n], refs[2 * n + 1]
        token = refs[-1]
        x, y, c = _place()
        for i in range(n):
            hr = slabs[i].shape[1] // 2
            _remote(g_refs[i].at[:, pl.ds((1 - c) * hr, hr), :], land_refs[i], send_sems.at[i], recv_sems.at[i],
                    (x, y, 1 - c)).start()
        token[...] = jnp.zeros_like(token)

    hbm = lambda a: pltpu.with_memory_space_constraint(a, pltpu.HBM)
    lands = [lax.empty((4, s.shape[1] // 2, s.shape[2]), s.dtype) for s in slabs]
    dma = pltpu.SemaphoreType.DMA
    return pl.pallas_call(
        body, name="pair_start_" + tag,
        out_shape=(dma((n,)), dma((n,)), *[pltpu.HBM(a.shape, a.dtype) for a in list(slabs) + lands],
                   jax.ShapeDtypeStruct((8, LANES), F32)),
        in_specs=[HBM] * (2 * n), out_specs=(SEM, SEM, *[HBM] * (2 * n), pl.BlockSpec(memory_space=pltpu.VMEM)),
        input_output_aliases={i: 2 + i for i in range(2 * n)},
        compiler_params=pltpu.CompilerParams(has_side_effects=DATAFLOW),
    )(*[hbm(a) for a in list(slabs) + lands])


def pair_wait(started, after, tag):
    send_sems, recv_sems, *rest = started
    n = (len(rest) - 1) // 2
    slabs, lands = rest[:n], rest[n:2 * n]

    def body(*refs):
        g_refs, land_refs = refs[:n], refs[n:2 * n]
        send_ref, recv_ref = refs[2 * n], refs[2 * n + 1]
        x, y, c = _place()
        for i in range(n):
            hr = slabs[i].shape[1] // 2
            cp = _remote(g_refs[i].at[:, pl.ds((1 - c) * hr, hr), :], land_refs[i], send_ref.at[i], recv_ref.at[i], (x, y, 1 - c))
            cp.wait_send()
            cp.wait_recv()

    out = pl.pallas_call(
        body, name="pair_wait_" + tag,
        out_shape=tuple(pltpu.HBM(a.shape, a.dtype) for a in list(slabs) + list(lands)),
        in_specs=[HBM] * (2 * n) + [SEM, SEM, pl.BlockSpec(memory_space=pl.ANY)], out_specs=tuple([HBM] * (2 * n)),
        input_output_aliases={i: i for i in range(2 * n)},
        compiler_params=pltpu.CompilerParams(has_side_effects=DATAFLOW),
    )(*slabs, *lands, send_sems, recv_sems, after)
    return list(out[:n]), list(out[n:])


def _tile2(rows, cols):
    fits = lambda r, c: r * c * 4 <= BLOCK_BYTES
    if fits(rows, cols):
        return rows, cols
    tiles = [(r, cols) for r in (1024, 512, 256, 128, 64) if rows % r == 0 and fits(r, cols)]
    tiles += [(rows, cols // k) for k in (2, 3, 4, 6, 8, 12, 16) if cols % (k * LANES) == 0 and fits(rows, cols // k)]
    return max(tiles, key=lambda t: t[0] * t[1])


def pair_add(g, p, c, name):
    _, hr, cols = p.shape
    tm, tc = _tile2(hr, cols)
    per = hr // tm

    def body(c_ref, g_ref, p_ref, o_ref):
        o_ref[...] = (g_ref[...] + p_ref[...]).astype(o_ref.dtype)

    return pl.pallas_call(
        body, name=name,
        grid_spec=pltpu.PrefetchScalarGridSpec(
            num_scalar_prefetch=1, grid=(4, per, cols // tc),
            in_specs=[pl.BlockSpec((None, tm, tc), lambda k, i, j, c_ref: (k, c_ref[0] * per + i, j)),
                      pl.BlockSpec((None, tm, tc), lambda k, i, j, c_ref: (k, i, j))],
            out_specs=pl.BlockSpec((None, tm, tc), lambda k, i, j, c_ref: (k, i, j))),
        out_shape=jax.ShapeDtypeStruct((4, hr, cols), BF16),
        compiler_params=_params(("arbitrary", "arbitrary", "arbitrary")),
    )(c.reshape(1).astype(jnp.int32), g, p)


def scatter_start(sums, tag):
    n = len(sums)

    def body(*refs):
        s_refs, land_refs = refs[:n], refs[n:2 * n]
        send_sems, recv_sems = refs[2 * n], refs[2 * n + 1]
        token = refs[-1]
        x, y, c = _place()
        k = 2 * x + y
        for i in range(n):
            for j, (px, py) in enumerate(_other_chips(x, y)):
                _remote(s_refs[i].at[2 * px + py], land_refs[i].at[k], send_sems.at[j * n + i], recv_sems.at[j * n + i],
                        (px, py, c)).start()
        token[...] = jnp.zeros_like(token)

    hbm = lambda a: pltpu.with_memory_space_constraint(a, pltpu.HBM)
    return pl.pallas_call(
        body, name="scatter_start_" + tag,
        out_shape=(pltpu.SemaphoreType.DMA((3 * n,)), pltpu.SemaphoreType.DMA((3 * n,)),
                   *[pltpu.HBM(s.shape, s.dtype) for s in sums], *[pltpu.HBM(s.shape, s.dtype) for s in sums],
                   jax.ShapeDtypeStruct((8, LANES), F32)),
        in_specs=[HBM] * (2 * n), out_specs=(SEM, SEM, *[HBM] * (2 * n), pl.BlockSpec(memory_space=pltpu.VMEM)),
        input_output_aliases={i: 2 + i for i in range(2 * n)},
        compiler_params=pltpu.CompilerParams(has_side_effects=DATAFLOW),
    )(*[hbm(s) for s in sums], *[hbm(lax.empty(s.shape, s.dtype)) for s in sums])


def scatter_wait(started, after, tag):
    send_sems, recv_sems, *rest = started
    n = (len(rest) - 1) // 2
    sums, lands = rest[:n], rest[n:2 * n]

    def body(*refs):
        s_refs, land_refs = refs[:n], refs[n:2 * n]
        send_ref, recv_ref = refs[2 * n], refs[2 * n + 1]
        x, y, c = _place()
        for i in range(n):
            for j, (px, py) in enumerate(_other_chips(x, y)):
                cp = _remote(s_refs[i].at[2 * px + py], land_refs[i].at[2 * px + py], send_ref.at[j * n + i],
                             recv_ref.at[j * n + i], (px, py, c))
                cp.wait_send()
                cp.wait_recv()

    out = pl.pallas_call(
        body, name="scatter_wait_" + tag,
        out_shape=tuple(pltpu.HBM(s.shape, s.dtype) for s in sums + lands),
        in_specs=[HBM] * (2 * n) + [SEM, SEM, pl.BlockSpec(memory_space=pl.ANY)], out_specs=tuple([HBM] * (2 * n)),
        input_output_aliases={i: i for i in range(2 * n)},
        compiler_params=pltpu.CompilerParams(has_side_effects=DATAFLOW),
    )(*sums, *lands, send_sems, recv_sems, after)
    return list(out[:n]), list(out[n:])


def sum_chips(landed, own, chip, core, name):
    _, hr, cols = landed.shape
    tm, tc = _tile2(hr, cols)
    per = hr // tm

    def body(idx_ref, l0, l1, l2, l3, own_ref, o_ref):
        mine = own_ref[...].astype(F32)
        v = [jnp.where(idx_ref[0] == k, mine, ref[...].astype(F32)) for k, ref in enumerate((l0, l1, l2, l3))]
        o_ref[...] = ((v[0] + v[1]) + v[2]) + v[3]

    slot = lambda k: pl.BlockSpec((None, tm, tc),
                                  lambda i, j, idx: (jnp.where(idx[0] == k, (k + 1) & 3, k), i, j))
    return pl.pallas_call(
        body, name=name,
        grid_spec=pltpu.PrefetchScalarGridSpec(
            num_scalar_prefetch=1, grid=(per, cols // tc),
            in_specs=[slot(0), slot(1), slot(2), slot(3),
                      pl.BlockSpec((None, tm, tc), lambda i, j, idx: (idx[0], i, j))],
            out_specs=pl.BlockSpec((tm, tc), lambda i, j, idx: (idx[1] * per + i, j))),
        out_shape=jax.ShapeDtypeStruct((2 * hr, cols), F32),
        compiler_params=_params(("arbitrary", "arbitrary")),
    )(jnp.stack([chip, core]).astype(jnp.int32), landed, landed, landed, landed, own)


def exchange_halves(bufs):
    n = len(bufs)

    def body(*refs):
        out_refs = refs[n:2 * n]
        send_sems, recv_sems = refs[2 * n:]
        x, y, c = _place()
        sends, recvs = [], []
        for i in range(n):
            hr = bufs[i].shape[0] // 2
            own = out_refs[i].at[pl.ds(c * hr, hr), :]
            other = out_refs[i].at[pl.ds((1 - c) * hr, hr), :]
            sends.append(_remote(own, own, send_sems.at[i], recv_sems.at[i], (x, y, 1 - c)))
            recvs.append(_remote(other, other, send_sems.at[i], recv_sems.at[i], (x, y, 1 - c)))
        for cp in sends:
            cp.start()
        for cp in recvs:
            cp.wait_recv()
        for cp in sends:
            cp.wait_send()

    return pl.pallas_call(
        body, name="exchange_halves", in_specs=[HBM] * n, out_specs=[HBM] * n,
        out_shape=[jax.ShapeDtypeStruct(b.shape, b.dtype) for b in bufs],
        input_output_aliases={i: i for i in range(n)},
        scratch_shapes=[pltpu.SemaphoreType.DMA((n,)), pltpu.SemaphoreType.DMA((n,))],
    )(*bufs)


def assemble_in_proj(landed, own, chip):
    rows, cols = 128, own.shape[1]

    def body(idx_ref, l0, l1, l2, l3, own_ref, o_ref):
        mine = own_ref[...]
        w = jnp.concatenate([jnp.where(idx_ref[0] == k, mine, ref[...]) for k, ref in enumerate((l0, l1, l2, l3))], axis=1)
        o_ref[...] = jnp.concatenate([w[:, :ORIG_Z], w[:, ORIG_GA:], w[:, ORIG_XBC:ORIG_DT], w[:, ORIG_Z:ORIG_XBC],
                                      w[:, ORIG_DT:ORIG_GA], jnp.zeros((rows, IN_PAD - IN_ORIG), w.dtype)], axis=1)

    slot = lambda k: pl.BlockSpec((None, rows, cols), lambda i, idx: (jnp.where(idx[0] == k, (k + 1) & 3, k), i, 0))
    return pl.pallas_call(
        body, name="assemble_in_proj",
        grid_spec=pltpu.PrefetchScalarGridSpec(
            num_scalar_prefetch=1, grid=(D // rows,),
            in_specs=[slot(0), slot(1), slot(2), slot(3), pl.BlockSpec((rows, cols), lambda i, idx: (i, 0))],
            out_specs=pl.BlockSpec((rows, IN_PAD), lambda i, idx: (i, 0))),
        out_shape=jax.ShapeDtypeStruct((D, IN_PAD), own.dtype),
        compiler_params=_params(("arbitrary",)),
    )(chip.reshape(1).astype(jnp.int32), landed, landed, landed, landed, own)


def rows_exchange(a, name):
    hr = a.shape[0] // 2

    def body(a_ref, out_ref, send_sem, recv_sem):
        x, y, c = _place()
        cp = _remote(a_ref.at[pl.ds((1 - c) * hr, hr), :], out_ref, send_sem, recv_sem, (x, y, 1 - c))
        cp.start()
        cp.wait()

    return pl.pallas_call(
        body, name=name, in_specs=[HBM], out_specs=HBM,
        out_shape=jax.ShapeDtypeStruct((hr, a.shape[1]), a.dtype),
        scratch_shapes=[pltpu.SemaphoreType.DMA, pltpu.SemaphoreType.DMA],
    )(a)


def rows_start(a, tag):
    hr = a.shape[0] // 2

    def body(a_ref, land_ref, send_sem, recv_sem, a_thru, land_thru, token):
        x, y, c = _place()
        _remote(a_ref.at[pl.ds((1 - c) * hr, hr), :], land_ref, send_sem, recv_sem, (x, y, 1 - c)).start()
        token[...] = jnp.zeros_like(token)

    hbm = lambda v: pltpu.with_memory_space_constraint(v, pltpu.HBM)
    dma = pltpu.SemaphoreType.DMA
    return pl.pallas_call(
        body, name="rows_start_" + tag,
        out_shape=(dma(()), dma(()), pltpu.HBM(a.shape, a.dtype), pltpu.HBM((hr, a.shape[1]), a.dtype),
                   jax.ShapeDtypeStruct((8, LANES), F32)),
        in_specs=[HBM, HBM], out_specs=(SEM, SEM, HBM, HBM, pl.BlockSpec(memory_space=pltpu.VMEM)),
        input_output_aliases={0: 2, 1: 3},
        compiler_params=pltpu.CompilerParams(has_side_effects=DATAFLOW),
    )(hbm(a), hbm(lax.empty((hr, a.shape[1]), a.dtype)))


def rows_wait(started, after, tag):
    send_sem, recv_sem, a, land, _ = started
    hr = a.shape[0] // 2

    def body(a_ref, land_ref, send_ref, recv_ref, after_ref, a_thru, got_ref):
        x, y, c = _place()
        cp = _remote(a_ref.at[pl.ds((1 - c) * hr, hr), :], land_ref, send_ref, recv_ref, (x, y, 1 - c))
        cp.wait_send()
        cp.wait_recv()

    return pl.pallas_call(
        body, name="rows_wait_" + tag,
        out_shape=(pltpu.HBM(a.shape, a.dtype), pltpu.HBM(land.shape, land.dtype)),
        in_specs=[HBM, HBM, SEM, SEM, pl.BlockSpec(memory_space=pl.ANY)], out_specs=(HBM, HBM),
        input_output_aliases={0: 0, 1: 1},
        compiler_params=pltpu.CompilerParams(has_side_effects=DATAFLOW),
    )(a, land, send_sem, recv_sem, after)


def split_pair_add(pieces, received, core):
    cols = IN_ORIG // 4
    rows, hr = 128, D // 2
    per = hr // rows
    n_p = len(pieces)

    def body(c_ref, *refs):
        o_ref = refs[-1]
        d = jnp.concatenate([refs[i][...] + refs[n_p + i][...] for i in range(n_p)], axis=1)
        w = jnp.concatenate([d[:, :COL_GA], d[:, COL_Z:COL_DT], d[:, COL_XBC:COL_Z], d[:, COL_DT:COL_DT + 32],
                             d[:, COL_GA:COL_XBC]], axis=1)
        for k in range(4):
            o_ref[k] = w[:, k * cols:(k + 1) * cols].astype(o_ref.dtype)

    return pl.pallas_call(
        body, name="split_pair_add",
        grid_spec=pltpu.PrefetchScalarGridSpec(
            num_scalar_prefetch=1, grid=(per,),
            in_specs=[pl.BlockSpec((rows, p.shape[1]), lambda i, c_ref: (c_ref[0] * per + i, 0)) for p in pieces]
            + [pl.BlockSpec((rows, p.shape[1]), lambda i, c_ref: (i, 0)) for p in received],
            out_specs=pl.BlockSpec((4, rows, cols), lambda i, c_ref: (0, i, 0))),
        out_shape=jax.ShapeDtypeStruct((4, hr, cols), BF16),
        compiler_params=_params(("arbitrary",)),
    )(core.reshape(1).astype(jnp.int32), *pieces, *received)


def ada_prepare(c_all, w_ada, hgrn_lb):
    def body(c_ref, w_ref, lb_ref, mod_ref, row_ref):
        mod_ref[...] = hdot(silu(c_ref[...]), w_ref[...])
        row_ref[...] = sigmoid(lb_ref[0:1, :] - lb_ref[1:2, :])

    return pl.pallas_call(
        body, name="ada_prepare",
        out_shape=[jax.ShapeDtypeStruct((8, w_ada.shape[1]), F32), jax.ShapeDtypeStruct((1, D), F32)],
        compiler_params=pltpu.CompilerParams(vmem_limit_bytes=VMEM_LIMIT),
    )(c_all, w_ada, hgrn_lb)


SMALL_SEGS = (("mod", 6 * D), ("lb", D), ("gnorm", LANES), ("conv_w", 4 * CONV_DIM), ("conv_b", CONV_DIM),
              ("dt_bias", LANES), ("a_log", B_INNER), ("d", B_INNER), ("ssm_norm", B_INNER),
              ("ln1_g", D), ("ln1_b", D), ("ln2_g", D), ("ln2_b", D), ("loss", LANES))
SMALL_PARAMS = ("b_ada", "hgrn_lb", "hgrn_gnorm", "ssm_conv_b", "ssm_dt_bias", "ssm_a_log", "ssm_d", "ssm_norm",
                "ln1_g", "ln1_b", "ln2_g", "ln2_b")


def finalize_small(g_all, c_all, dmod_cols, params, m, v):
    n_p = len(SMALL_PARAMS)
    offs, o = {}, 0
    for nm, width in SMALL_SEGS:
        offs[nm] = (o, width)
        o += width

    def body(*refs):
        g_ref, c_ref, dm_ref = refs[:3]
        p_refs = refs[3:3 + n_p]
        m_refs = refs[3 + n_p:3 + 2 * n_p]
        v_refs = refs[3 + 2 * n_p:3 + 3 * n_p]
        outs = refs[3 + 3 * n_p:]
        gwa_ref, gcw_ref, loss_ref = outs[:3]
        res = outs[3:]
        total = jnp.sum(g_ref[...], axis=0, keepdims=True)
        seg = lambda nm: total[:, offs[nm][0]:offs[nm][0] + offs[nm][1]]
        loss_ref[...] = seg("loss")
        gwa_ref[...] = hdot(silu(c_ref[...]), dm_ref[...], "tn")
        cw = seg("conv_w")
        for j in range(4):
            gcw_ref[j:j + 1, :] = cw[:, j * CONV_DIM:(j + 1) * CONV_DIM]
        hc = lax.broadcasted_iota(jnp.int32, (B_INNER, LANES), 0)
        hj = lax.broadcasted_iota(jnp.int32, (B_INNER, LANES), 1)
        per_head = ((hc >> 6) == hj).astype(F32)
        heads = lambda nm: hdot(jnp.broadcast_to(seg(nm), (8, B_INNER)), per_head)[0:1, 0:32]
        lbp = sigmoid(p_refs[1][0:1, :] - p_refs[1][1:2, :])
        g_row = seg("lb") * lbp * (1.0 - lbp)
        grads = {"b_ada": seg("mod"), "hgrn_gnorm": seg("gnorm"), "ssm_conv_b": seg("conv_b"),
                 "ssm_dt_bias": seg("dt_bias")[:, 0:32], "ssm_a_log": heads("a_log"), "ssm_d": heads("d"),
                 "ssm_norm": seg("ssm_norm"), "ln1_g": seg("ln1_g"), "ln1_b": seg("ln1_b"),
                 "ln2_g": seg("ln2_g"), "ln2_b": seg("ln2_b")}
        for i, nm in enumerate(SMALL_PARAMS):
            g_out, d_out, m_out, v_out = res[4 * i:4 * i + 4]
            if nm == "hgrn_lb":
                for row, gv in ((0, g_row), (1, -g_row)):
                    sl = slice(row, row + 1)
                    dl, mn, vn = adamw(p_refs[i][sl, :], gv, m_refs[i][sl, :], v_refs[i][sl, :])
                    g_out[sl, :], d_out[sl, :], m_out[sl, :], v_out[sl, :] = gv, dl, mn, vn
            else:
                gv = grads[nm]
                dl, mn, vn = adamw(p_refs[i][...], gv, m_refs[i][...], v_refs[i][...])
                g_out[...], d_out[...], m_out[...], v_out[...] = gv, dl, mn, vn

    out_shape = [jax.ShapeDtypeStruct((D, dmod_cols.shape[1]), F32), jax.ShapeDtypeStruct((4, CONV_DIM), F32),
                 jax.ShapeDtypeStruct((1, LANES), F32)]
    for p in params:
        out_shape += [jax.ShapeDtypeStruct(p.shape, F32)] * 4
    return pl.pallas_call(
        body, name="finalize_small", out_shape=out_shape,
        compiler_params=pltpu.CompilerParams(vmem_limit_bytes=VMEM_LIMIT),
    )(g_all, c_all, dmod_cols, *params, *m, *v)


def adam_update(w, g, m, v, name):
    rows, cols = w.shape
    tm, tc = _tile2(rows, cols)

    def body(w_ref, g_ref, m_ref, v_ref, d_ref, mo_ref, vo_ref):
        d_ref[...], mo_ref[...], vo_ref[...] = adamw(w_ref[...], g_ref[...], m_ref[...], v_ref[...])

    spec = pl.BlockSpec((tm, tc), lambda i, j: (i, j))
    return pl.pallas_call(
        body, name=name, grid=(rows // tm, cols // tc), in_specs=[spec] * 4, out_specs=[spec] * 3,
        out_shape=[jax.ShapeDtypeStruct((rows, cols), F32)] * 3,
        compiler_params=_params(("arbitrary", "arbitrary")),
    )(w, g, m, v)


def kernel(x, c, w_ada, b_ada, w_in, hgrn_lb, hgrn_gnorm, ssm_conv_w, ssm_conv_b, ssm_dt_bias, ssm_a_log, ssm_d, ssm_norm, w_branch_a, w_branch_b, w_o, ln1_g, ln1_b, w_ffn_gate, w_ffn_up, w_ffn_down, ln2_g, ln2_b, loss_target, m_w_ada, m_b_ada, m_w_in, m_hgrn_lb, m_hgrn_gnorm, m_ssm_conv_w, m_ssm_conv_b, m_ssm_dt_bias, m_ssm_a_log, m_ssm_d, m_ssm_norm, m_w_branch_a, m_w_branch_b, m_w_o, m_ln1_g, m_ln1_b, m_w_ffn_gate, m_w_ffn_up, m_w_ffn_down, m_ln2_g, m_ln2_b, v_w_ada, v_b_ada, v_w_in, v_hgrn_lb, v_hgrn_gnorm, v_ssm_conv_w, v_ssm_conv_b, v_ssm_dt_bias, v_ssm_a_log, v_ssm_d, v_ssm_norm, v_w_branch_a, v_w_branch_b, v_w_o, v_ln1_g, v_ln1_b, v_w_ffn_gate, v_w_ffn_up, v_w_ffn_down, v_ln2_g, v_ln2_b):
    given = dict(locals())
    chip = 2 * lax.axis_index("x") + lax.axis_index("y")
    core = lax.axis_index("c")
    t = x.shape[1]

    first = gather_rows(jnp.concatenate([c, ssm_conv_w.reshape(1, CONV_DIM)], axis=1), "gather_cond").reshape(8, D + CONV_DIM)
    c_all = first[:, :D]
    conv_w = first[0::2, D:].reshape(4, 4, CONV_DIM // 4).transpose(1, 0, 2).reshape(4, CONV_DIM)
    mod_part, lb_row = ada_prepare(c_all, w_ada[0], hgrn_lb)
    mod_cols = w_ada.shape[2]
    mod_row = exchange_rows(mod_part.reshape(8, 1, mod_cols), "exchange_mod").reshape(1, 6 * D) + b_ada

    local = {nm: given[nm][0] for nm in SHARDED if nm != "w_ffn_in"}
    local["w_ffn_in"] = jnp.concatenate([w_ffn_gate[0].T, w_ffn_up[0].T], axis=0)
    shards = [local[nm].astype(BF16) for nm in SHARDED]
    send_in, recv_in, sent_in, land_in, started_in = gather_start(shards[:1], mod_row, "in")
    shards = shards[:1] + [(local[nm] + started_in[0, 0]).astype(BF16) for nm in SHARDED[1:]]
    send_rest, recv_rest, *flying = gather_start(shards[1:], started_in, "rest")
    n_rest = len(SHARDED) - 1
    sent_rest, land_rest, started_rest = flying[:n_rest], flying[n_rest:2 * n_rest], flying[-1]
    mod_row = mod_row + started_rest[0:1, 0:1]
    mod = tuple(mod_row[:, i * D:(i + 1) * D] for i in range(6))
    with_own = lambda land, shard: lax.dynamic_update_slice(land, shard[None], (chip, 0, 0))

    class Weights:
        def input_projection(self, after):
            (own,), land = gather_wait(send_in, recv_in, [sent_in], [land_in], after, "in")
            (land,) = forward_wait(forward_start(land, "in"), after, "in")
            return assemble_in_proj(land, own, chip)

        def start_rest(self, after):
            self.own, landed = gather_wait(send_rest, recv_rest, sent_rest, land_rest, after, "rest")
            self.started = forward_start(landed, "rest")
            return self.started[-1]

        def rest(self, after):
            got = {nm: with_own(land, s) for nm, land, s in zip(SHARDED[1:], forward_wait(self.started, after, "rest"), self.own, strict=True)}
            whole = lambda nm: got[nm].reshape(4 * got[nm].shape[1], got[nm].shape[2])
            return tuple(whole(nm) for nm in SHARDED[1:])

    wts = Weights()

    per_head = lambda p: jnp.pad(p, ((0, 0), (0, LANES - p.shape[1])))
    per_channel = lambda p: jnp.repeat(p[0], B_INNER // 32)[None]
    small = (lb_row, hgrn_gnorm, conv_w, ssm_conv_b, per_head(ssm_dt_bias), per_channel(ssm_a_log),
             per_channel(ssm_d), ssm_norm, ln1_g, ln1_b, ln2_g, ln2_b)
    by_rows = lambda g: g.reshape(4, g.shape[0] // 4, g.shape[1])
    travelling = {}

    def start_early(dws):
        travelling["pair"] = pair_start([by_rows(dw) for dw in dws], "early")
        return travelling["pair"][-1]

    def between_scans(after):
        slabs, received = pair_wait(travelling["pair"], after, "early")
        travelling["pairs"] = [pair_add(s, r, core, "pair_add_" + nm) for nm, s, r in zip(SHARDED[1:], slabs, received, strict=True)]
        travelling["started"] = scatter_start(travelling["pairs"], "early")
        return travelling["started"][-1]

    def finish_early(after):
        travelling["pairs"], travelling["landed"] = scatter_wait(travelling["started"], after, "early")

    def start_last(u1, dproj):
        wide = 2 * IN_PAD // 3
        first = matmul(u1, dproj, "tn", F32, "in_proj_dw_first", b_cols=(0, wide))
        sending = rows_start(first, "last")
        second = matmul(u1, dproj, "tn", F32, "in_proj_dw_second", after=sending[-1], b_cols=(wide, IN_PAD - wide))
        first, got_first = rows_wait(sending, second, "last")
        got_second = rows_exchange(second, "pair_exchange_last")
        travelling["pairs_in"] = [split_pair_add([first, second], [got_first, got_second], core)]
        travelling["started_in"] = scatter_start(travelling["pairs_in"], "last")
        return travelling["started_in"][-1]

    loss, grad_x, d_mod, d_wts, d_small = local_step(x[0], loss_target[0], mod, wts, small,
                                                     start_early, between_scans, finish_early, start_last)

    d_lb, d_gn, d_cw, d_cb, d_dtb, d_alog, d_dsk, d_nw, d_l1g, d_l1b, d_l2g, d_l2b = d_small
    row = jnp.concatenate(list(d_mod) + [d_lb, d_gn, d_cw.reshape(1, 4 * CONV_DIM), d_cb, d_dtb, d_alog, d_dsk, d_nw,
                                          d_l1g, d_l1b, d_l2g, d_l2b, jnp.pad(loss, ((0, 0), (0, LANES - 1)))], axis=1)
    g_all = gather_rows(row, "gather_small_grads").reshape(8, row.shape[1])
    dmod_cols = lax.dynamic_slice_in_dim(g_all, chip * mod_cols, mod_cols, axis=1)
    fin = finalize_small(g_all, c_all, dmod_cols, [given[n] for n in SMALL_PARAMS],
                         [given["m_" + n] for n in SMALL_PARAMS], [given["v_" + n] for n in SMALL_PARAMS])
    grads, deltas, new_m, new_v = {}, {}, {}, {}
    grads["w_ada"] = fin[0][None]
    grads["ssm_conv_w"] = lax.dynamic_slice_in_dim(fin[1], chip * (CONV_DIM // 4), CONV_DIM // 4, axis=1)[None]
    for i, nm in enumerate(SMALL_PARAMS):
        grads[nm], deltas[nm], new_m[nm], new_v[nm] = fin[3 + 4 * i:7 + 4 * i]

    pairs_in, landed_in = scatter_wait(travelling["started_in"], fin[3], "last")
    pairs, landed = pairs_in + travelling["pairs"], landed_in + travelling["landed"]
    halves = [sum_chips(r, p, chip, core, "sum_chips_" + nm) for nm, r, p in zip(SHARDED, landed, pairs, strict=True)]
    reduced = dict(zip(SHARDED, exchange_halves(halves), strict=True))
    reduced["w_ada"], reduced["ssm_conv_w"] = grads["w_ada"][0], grads["ssm_conv_w"][0]
    reduced["w_in"] = reduced["w_in"].T
    reduced["w_ffn_gate"], reduced["w_ffn_up"] = reduced["w_ffn_in"][:FFN_SHARD], reduced["w_ffn_in"][FFN_SHARD:]
    for nm in ("w_ada", "ssm_conv_w", "w_in", "w_branch_a", "w_branch_b", "w_o", "w_ffn_gate", "w_ffn_up", "w_ffn_down"):
        flipped = nm in ("w_in", "w_ffn_gate", "w_ffn_up")
        work = (lambda a: a[0].T) if flipped else (lambda a: a[0])
        back = (lambda a: a.T[None]) if flipped else (lambda a: a[None])
        d_, m_, v_ = adam_update(work(given[nm]), reduced[nm], work(given["m_" + nm]), work(given["v_" + nm]), "adam_" + nm)
        grads[nm], deltas[nm], new_m[nm], new_v[nm] = back(reduced[nm]), back(d_), back(m_), back(v_)

    names = ("w_ada", "b_ada", "w_in", "hgrn_lb", "hgrn_gnorm", "ssm_conv_w", "ssm_conv_b", "ssm_dt_bias", "ssm_a_log",
             "ssm_d", "ssm_norm", "w_branch_a", "w_branch_b", "w_o", "ln1_g", "ln1_b", "w_ffn_gate", "w_ffn_up",
             "w_ffn_down", "ln2_g", "ln2_b")
    return (fin[2][0, 0], grad_x[None], *[grads[n] for n in names], *[deltas[n] for n in names],
            *[new_m[n] for n in names], *[new_v[n] for n in names])
```

```python
import functools

import jax
import jax.numpy as jnp
from jax import lax
from jax.experimental import pallas as pl
from jax.experimental.pallas import tpu as pltpu

F32, BF16 = jnp.float32, jnp.bfloat16
HI = lax.Precision.HIGHEST
MESH = pl.DeviceIdType.MESH

D = 1024
CHUNK = 64
LANES = 128
N_HEADS_A = 8
N_GROUPS_B = 4
B_INNER = 2048
CONV_DIM = 3072
D_FF = 2816
ALPHA = 2.0 ** 0.25
LN_EPS = 1e-5
RMS_EPS = 1e-6
ADAM_LR, ADAM_B1, ADAM_B2, ADAM_EPS, ADAM_WD, ADAM_STEP = 0.001, 0.9, 0.999, 1e-08, 0.01, 10

IN_ORIG = 11296
IN_PAD = 11520
COL_GA, COL_GB, COL_XBC, COL_Z, COL_DT = 4096, 5120, 6144, 9216, 11264
ORIG_Z, ORIG_XBC, ORIG_DT, ORIG_GA = 4096, 6144, 9216, 9248

SHARDED = ("w_in", "w_branch_a", "w_branch_b", "w_o", "w_ffn_in", "w_ffn_down")
FFN_SHARD = D_FF // 4
VMEM_LIMIT = 56 * 1024 * 1024
BLOCK_BYTES = 2 * 1024 * 1024
_DIMS = {"nn": (((1,), (0,)), ((), ())), "nt": (((1,), (1,)), ((), ())), "tn": (((0,), (0,)), ((), ()))}


def _bd(a, b, mode):
    return lax.dot_general(a.astype(BF16), b.astype(BF16), _DIMS[mode], preferred_element_type=F32)


@functools.partial(jax.custom_vjp, nondiff_argnums=(2,))
def bdot(a, b, mode):
    return _bd(a, b, mode)


def _bdot_fwd(a, b, mode):
    return _bd(a, b, mode), (a, b)


def _bdot_bwd(mode, res, g):
    a, b = res
    if mode == "nn":
        return _bd(g, b, "nt"), _bd(a, g, "tn")
    if mode == "nt":
        return _bd(g, b, "nn"), _bd(g, a, "tn")
    return _bd(b, g, "nt"), _bd(a, g, "nn")


bdot.defvjp(_bdot_fwd, _bdot_bwd)


def hdot(a, b, mode="nn"):
    return lax.dot_general(a, b, _DIMS[mode], precision=HI, preferred_element_type=F32)


def _raw(a, b, mode):
    return lax.dot_general(a, b, _DIMS[mode], preferred_element_type=F32)


def _split(x, n):
    parts, rest = [], x
    for _ in range(n):
        p = rest.astype(BF16)
        parts.append(p)
        rest = rest - p.astype(F32)
    return parts


def _od(a, b, mode, exact):
    if exact == 1:
        e = b.astype(BF16)
        p = _split(a, 3)
        return (_raw(p[2], e, mode) + _raw(p[1], e, mode)) + _raw(p[0], e, mode)
    e = a.astype(BF16)
    p = _split(b, 3)
    return (_raw(e, p[2], mode) + _raw(e, p[1], mode)) + _raw(e, p[0], mode)


@functools.partial(jax.custom_vjp, nondiff_argnums=(2, 3))
def odot(a, b, mode, exact):
    return _od(a, b, mode, exact)


def _odot_fwd(a, b, mode, exact):
    return _od(a, b, mode, exact), (a, b)


def _odot_bwd(mode, exact, res, g):
    a, b = res
    if exact == 1:
        da = {"nn": lambda: _od(g, b, "nt", 1), "nt": lambda: _od(g, b, "nn", 1), "tn": lambda: _od(b, g, "nt", 0)}[mode]()
        return da, jnp.zeros_like(b)
    db = {"nn": lambda: _od(a, g, "tn", 0), "nt": lambda: _od(g, a, "tn", 1), "tn": lambda: _od(a, g, "nn", 0)}[mode]()
    return jnp.zeros_like(a), db


odot.defvjp(_odot_fwd, _odot_bwd)


_BDIMS = {"bnn": (((2,), (1,)), ((0,), (0,))), "bnt": (((2,), (2,)), ((0,), (0,))), "btn": (((1,), (1,)), ((0,), (0,)))}


def _braw(a, b, mode):
    return lax.dot_general(a, b, _BDIMS[mode], preferred_element_type=F32)


def _bdb(a, b, mode):
    return _braw(a.astype(BF16), b.astype(BF16), mode)


def _d3b(a, b, mode):
    ah, al = _split(a, 2)
    bh, bl = _split(b, 2)
    return _braw(ah, bh, mode) + (_braw(ah, bl, mode) + _braw(al, bh, mode))


def _batched_bwd(f):
    def bwd(mode, res, g):
        a, b = res
        if mode == "bnn":
            return f(g, b, "bnt"), f(a, g, "btn")
        if mode == "bnt":
            return f(g, b, "bnn"), f(g, a, "btn")
        return f(b, g, "bnt"), f(a, g, "bnn")
    return bwd


@functools.partial(jax.custom_vjp, nondiff_argnums=(2,))
def bdot_b(a, b, mode):
    return _bdb(a, b, mode)


bdot_b.defvjp(lambda a, b, mode: (_bdb(a, b, mode), (a, b)), _batched_bwd(_bdb))


@functools.partial(jax.custom_vjp, nondiff_argnums=(2,))
def dot3_b(a, b, mode):
    return _d3b(a, b, mode)


dot3_b.defvjp(lambda a, b, mode: (_d3b(a, b, mode), (a, b)), _batched_bwd(_d3b))


def _cum(tril3, x, mode):
    e = tril3.astype(BF16)
    p = _split(x, 3)
    return (_braw(e, p[2], mode) + _braw(e, p[1], mode)) + _braw(e, p[0], mode)


@jax.custom_vjp
def chunk_cumsum(tril3, x):
    return _cum(tril3, x, "bnn")


chunk_cumsum.defvjp(lambda t, x: (_cum(t, x, "bnn"), t), lambda t, g: (jnp.zeros_like(t), _cum(t, g, "btn")))


def _unstack(axis, n):
    @jax.custom_vjp
    def un(x):
        return tuple(lax.index_in_dim(x, i, axis, keepdims=False) for i in range(n))

    un.defvjp(lambda x: (un(x), None), lambda _, g: (jnp.stack(g, axis=axis),))
    return un


def _split_last(n, w):
    @jax.custom_vjp
    def sp(x):
        return tuple(x[..., i * w:(i + 1) * w] for i in range(n))

    sp.defvjp(lambda x: (sp(x), None), lambda _, g: (jnp.concatenate(g, axis=-1),))
    return sp


def sigmoid(x):
    return 0.5 * jnp.tanh(0.5 * x) + 0.5


def silu(x):
    return x * sigmoid(x)


def softplus(x):
    return jnp.maximum(x, 0.0) + jnp.log1p(jnp.exp(jnp.minimum(x, -x)))


def _ln(x):
    mu = jnp.mean(x, axis=-1, keepdims=True)
    xc = x - mu
    return xc * lax.rsqrt(jnp.mean(xc * xc, axis=-1, keepdims=True) + LN_EPS)


def _tril64():
    r = lax.broadcasted_iota(jnp.int32, (CHUNK, CHUNK), 0)
    c = lax.broadcasted_iota(jnp.int32, (CHUNK, CHUNK), 1)
    return (r >= c).astype(F32)


def hgrn_block(q, fl, iv, gr, st, lb, gn):
    tb = q.shape[0]
    nc = tb // CHUNK
    nh = N_HEADS_A
    heads = _split_last(nh, LANES)
    to4 = lambda a: jnp.stack(heads(a), axis=0).reshape(nh, nc, CHUNK, LANES)
    flat = lambda a: a.reshape(nh * nc, CHUNK, LANES)
    f = lb + (1.0 - lb) * sigmoid(fl)
    gl4, k4, qf4, v4, gr4 = to4(jnp.log(f)), to4(1.0 - f), to4(silu(q) * (128 ** -0.5)), to4(iv), to4(gr)
    tril = _tril64()
    b4 = chunk_cumsum(jnp.broadcast_to(tril[None], (nh * nc, CHUNK, CHUNK)), flat(gl4)).reshape(gl4.shape)
    blast = jnp.sum(gl4, axis=2, keepdims=True)
    ref = lax.stop_gradient(0.5 * blast)
    qp, kp = qf4 * jnp.exp(b4 - ref), k4 * jnp.exp(ref - b4)
    sc = dot3_b(flat(qp), flat(kp), "bnt") * tril
    o_intra = bdot_b(sc, flat(v4), "bnn").reshape(gl4.shape)
    chunks = _unstack(1, nc)
    qe, v_c, kd, dec = chunks(qp * jnp.exp(ref)), chunks(v4), chunks(kp * jnp.exp(blast - ref)), chunks(jnp.exp(blast))
    o_inter = []
    for c in range(nc):
        o_inter.append(bdot_b(qe[c], st, "bnt"))
        st = st * dec[c] + bdot_b(v_c[c], kd[c], "btn")
    o = o_intra + jnp.stack(o_inter, axis=1)
    on = o * lax.rsqrt(jnp.mean(o * o, axis=-1, keepdims=True) + RMS_EPS) * gn
    out = (on * silu(gr4)).reshape(nh, tb, LANES)
    return jnp.concatenate(_unstack(0, nh)(out), axis=1), st


def ssd_consts(g):
    i32 = jnp.int32
    ej = lax.broadcasted_iota(i32, (LANES, 512), 0)
    ec = lax.broadcasted_iota(i32, (LANES, 512), 1)
    expand = (ej == g * 8 + (ec >> 6)).astype(F32)
    ts = lax.broadcasted_iota(i32, (CHUNK, 512), 0)
    tc = lax.broadcasted_iota(i32, (CHUNK, 512), 1)
    itile = (ts == (tc & 63)).astype(F32)
    maskall = ts >= (tc & 63)
    br = lax.broadcasted_iota(i32, (LANES, LANES), 0)
    bc = lax.broadcasted_iota(i32, (LANES, LANES), 1)
    blockmask = ((br >> 6) == (bc >> 6)).astype(F32)
    return expand, itile, maskall, blockmask, _tril64()


def ssd_block(x, bm, cm, dt, z, st, dtb, alog, dsk, nw, cs):
    expand, itile, maskall, blockmask, tril = cs
    tb = x.shape[0]
    nc = tb // CHUNK
    delta = odot(softplus(dt + dtb), expand, "nn", 1)
    a = -jnp.exp(alog) * delta
    xdt = x * delta
    by_chunk = lambda v: v.reshape(nc, CHUNK, v.shape[-1])
    a3, xdt3, bm3, cm3 = by_chunk(a), by_chunk(xdt), by_chunk(bm), by_chunk(cm)
    acum3 = chunk_cumsum(jnp.broadcast_to(tril[None], (nc, CHUNK, CHUNK)), a3)
    alast3 = jnp.sum(a3, axis=1, keepdims=True)
    cb3 = bdot_b(cm3, jnp.concatenate([bm3] * 8, axis=1), "bnt")
    arow3 = jnp.sum(acum3 * itile, axis=1, keepdims=True)
    dec3 = jnp.exp(jnp.where(maskall, acum3 - arow3, -1e30))
    pairs = _split_last(4, LANES)
    intra = [bdot_b(m, jnp.concatenate([xp] * 2, axis=1) * blockmask, "bnn")
             for m, xp in zip(pairs(cb3 * dec3), pairs(xdt3))]
    chunks = _unstack(0, nc)
    cm_c, bm_c, xw_c, dec_c = chunks(cm3), chunks(bm3), chunks(xdt3 * jnp.exp(alast3 - acum3)), chunks(jnp.exp(alast3))
    inter = []
    for c in range(nc):
        inter.append(bdot(cm_c[c], st, "nn"))
        st = st * dec_c[c] + bdot(bm_c[c], xw_c[c], "tn")
    st_new = st
    y = (jnp.concatenate(intra, axis=-1) + jnp.stack(inter, axis=0) * jnp.exp(acum3)).reshape(tb, 512)
    yz = (y + x * dsk) * silu(z)
    return yz * lax.rsqrt(jnp.mean(yz * yz, axis=-1, keepdims=True) + RMS_EPS) * nw, st_new


def adamw(w, g, m, v):
    m = ADAM_B1 * m + (1.0 - ADAM_B1) * g
    v = ADAM_B2 * v + (1.0 - ADAM_B2) * jnp.square(g)
    m_hat = m / (1.0 - ADAM_B1 ** ADAM_STEP)
    v_hat = v / (1.0 - ADAM_B2 ** ADAM_STEP)
    return -ADAM_LR * (m_hat / (jnp.sqrt(v_hat) + ADAM_EPS) + ADAM_WD * w), m, v


def _pick(n, cands):
    for c in cands:
        if n % c == 0:
            return c
    return n


def _params(sem):
    return pltpu.CompilerParams(dimension_semantics=sem, vmem_limit_bytes=VMEM_LIMIT)


MATMUL_VMEM_BUDGET = 50 * 1024 * 1024
MATMUL_MIN_STEPS = 4


def matmul(a, b, mode, out_dtype, name, after=None, b_cols=None):
    if mode == "nn":
        (m, k), n = a.shape, b.shape[1]
    elif mode == "nt":
        (m, k), n = a.shape, b.shape[0]
    else:
        (k, m), n = a.shape, b.shape[1]
    first_col, n = (0, n) if b_cols is None else b_cols
    a_bytes, b_bytes, out_bytes = a.dtype.itemsize, b.dtype.itemsize, jnp.dtype(out_dtype).itemsize
    k_sizes = (2304, 2048, 1408, 1024, 768, 512, 256, 128)
    usual_tk = _pick(k, k_sizes)

    def vmem(tm_, tn_, tk_):
        blocks = 2 * (tm_ * tk_ * a_bytes + tk_ * tn_ * b_bytes + tm_ * tn_ * out_bytes)
        return blocks + (tm_ * tn_ * 4 if tk_ < k else 0)

    def traffic(tm_, tn_, tk_):
        return (m // tm_) * k * n * b_bytes + (n // tn_ if tk_ < k else 1) * m * k * a_bytes

    sizes = (2304, 2048, 1920, 1408, 1024, 768, 512, 256, 128)
    tiles = [(tm_, tn_, tk_) for tm_ in sizes if m % tm_ == 0 for tn_ in sizes if n % tn_ == 0
             for tk_ in {k, usual_tk} if vmem(tm_, tn_, tk_) <= MATMUL_VMEM_BUDGET] or [(m, n, k)]
    pipelined = [t for t in tiles if (m // t[0]) * (n // t[1]) * (k // t[2]) >= MATMUL_MIN_STEPS]
    tm, tn, tk = min(pipelined or tiles, key=lambda t: (traffic(*t), t[2] != usual_tk, -t[0] * t[1]))
    nk = k // tk
    a_spec = pl.BlockSpec((tk, tm), lambda i, j, kk: (kk, i)) if mode == "tn" else pl.BlockSpec((tm, tk), lambda i, j, kk: (i, kk))
    assert first_col % tn == 0 and (mode != "nt" or b_cols is None)
    skip = first_col // tn
    b_spec = pl.BlockSpec((tn, tk), lambda i, j, kk: (j, kk)) if mode == "nt" else pl.BlockSpec((tk, tn), lambda i, j, kk: (kk, j + skip))

    order = [] if after is None else [after]

    def body(a_ref, b_ref, *rest):
        o_ref, *acc = rest[len(order):]
        part = _bd(a_ref[...], b_ref[...], mode)
        if nk == 1:
            o_ref[...] = part.astype(o_ref.dtype)
            return
        acc_ref, = acc
        kk = pl.program_id(2)

        @pl.when(kk == 0)
        def _():
            acc_ref[...] = part

        @pl.when(jnp.logical_and(kk > 0, kk < nk - 1))
        def _():
            acc_ref[...] += part

        @pl.when(kk == nk - 1)
        def _():
            o_ref[...] = (acc_ref[...] + part).astype(o_ref.dtype)

    return pl.pallas_call(
        body, name=name, grid=(m // tm, n // tn, nk),
        in_specs=[a_spec, b_spec] + [pl.BlockSpec(memory_space=pl.ANY) for _ in order],
        out_specs=pl.BlockSpec((tm, tn), lambda i, j, kk: (i, j)),
        out_shape=jax.ShapeDtypeStruct((m, n), out_dtype),
        scratch_shapes=[pltpu.VMEM((tm, tn), F32)] if nk > 1 else [],
        compiler_params=_params(("parallel", "parallel", "arbitrary")),
    )(a, b, *order)


def rowwise(name, fn, rows, consts, out_rows, out_accs=(), tm_max=512, into=None, new_wide=None):
    t = rows[0][0].shape[0]
    tm = _pick(t, (tm_max, 128, 64, 32, 16, 8))
    n_r, n_c, n_o = len(rows), len(consts), len(out_rows)
    n_alias = 0 if into is None else 1

    def body(*refs):
        r_in = [r[...] for r in refs[:n_r]]
        c_in = [r[...] for r in refs[n_r:n_r + n_c]]
        refs = refs[:n_r + n_c] + refs[n_r + n_c + n_alias:]
        o_refs = refs[n_r + n_c:n_r + n_c + n_o]
        a_refs = refs[n_r + n_c + n_o:]
        ro, ao = fn(r_in, c_in)
        for ref, val in zip(o_refs, ro, strict=True):
            ref[...] = val.astype(ref.dtype)
        if a_refs:
            @pl.when(pl.program_id(0) == 0)
            def _():
                for ref in a_refs:
                    ref[...] = jnp.zeros_like(ref)

            for ref, val in zip(a_refs, ao, strict=True):
                ref[...] += val

    in_specs = [pl.BlockSpec((tm, w), functools.partial(lambda i, cb: (i, cb), cb=cb)) for _, w, cb in rows]
    in_specs += [pl.BlockSpec(c.shape, lambda i: (0, 0)) for c in consts]
    out_specs = [pl.BlockSpec((tm, w), lambda i: (i, 0)) for w, _ in out_rows]
    out_specs += [pl.BlockSpec(s, lambda i: (0, 0)) for s in out_accs]
    out_shape = [jax.ShapeDtypeStruct((t, w), dt) for w, dt in out_rows]
    out_shape += [jax.ShapeDtypeStruct(s, F32) for s in out_accs]
    operands = [r[0] for r in rows] + list(consts)
    aliases = {}
    if into is not None:
        target, cb = into
        in_specs.append(pl.BlockSpec(memory_space=pl.ANY))
        operands.append(target)
        out_specs[0] = pl.BlockSpec((tm, out_rows[0][0]), lambda i: (i, cb))
        out_shape[0] = jax.ShapeDtypeStruct(target.shape, target.dtype)
        aliases = {len(operands) - 1: 0}
    if new_wide is not None:
        width, cb = new_wide
        out_specs[0] = pl.BlockSpec((tm, out_rows[0][0]), lambda i: (i, cb))
        out_shape[0] = jax.ShapeDtypeStruct((t, width), out_rows[0][1])
    return pl.pallas_call(
        body, name=name, grid=(t // tm,), in_specs=in_specs, out_specs=out_specs, out_shape=out_shape,
        input_output_aliases=aliases, compiler_params=_params(("arbitrary",)),
    )(*operands)


def _full(a):
    return (a, a.shape[1], 0)


HGRN_TIME_BLOCK = 256
SSD_TIME_BLOCK = 512


def _time_block(t, most=HGRN_TIME_BLOCK):
    return _pick(t, tuple(b for b in (512, 256, 128, 64) if b <= most))


def _quarters(ref):
    return [ref[:, seg * D:(seg + 1) * D] for seg in range(4)]


def hgrn_forward(proj, lb, gn):
    t = proj.shape[0]
    tb = _time_block(t)
    nb = t // tb

    def body(qfig_ref, lb_ref, gn_ref, o_ref, st_ref, state):
        @pl.when(pl.program_id(0) == 0)
        def _():
            state[...] = jnp.zeros_like(state)

        st = state[...]
        st_ref[...] = st
        out, st_new = hgrn_block(*_quarters(qfig_ref), st, lb_ref[...], gn_ref[...])
        o_ref[...] = out.astype(o_ref.dtype)
        state[...] = st_new

    return pl.pallas_call(
        body, name="hgrn_forward", grid=(nb,),
        in_specs=[pl.BlockSpec((tb, 4 * D), lambda j: (j, 0)),
                  pl.BlockSpec((1, D), lambda j: (0, 0)), pl.BlockSpec((1, LANES), lambda j: (0, 0))],
        out_specs=[pl.BlockSpec((tb, D), lambda j: (j, 0)),
                   pl.BlockSpec((None, N_HEADS_A, LANES, LANES), lambda j: (j, 0, 0, 0))],
        out_shape=[jax.ShapeDtypeStruct((t, D), BF16),
                   jax.ShapeDtypeStruct((nb, N_HEADS_A, LANES, LANES), F32)],
        scratch_shapes=[pltpu.VMEM((N_HEADS_A, LANES, LANES), F32)],
        compiler_params=_params(("arbitrary",)),
    )(proj, lb, gn)


def hgrn_backward(proj, states, d_out, lb, gn, d_proj):
    t = proj.shape[0]
    tb = _time_block(t)
    nb = t // tb

    def body(qfig_ref, st_ref, do_ref, lb_ref, gn_ref, _, dqfig_ref, dlb_ref, dgn_ref, d_state):
        @pl.when(pl.program_id(0) == 0)
        def _():
            d_state[...] = jnp.zeros_like(d_state)
            dlb_ref[...] = jnp.zeros_like(dlb_ref)
            dgn_ref[...] = jnp.zeros_like(dgn_ref)

        _, vjp = jax.vjp(hgrn_block, *_quarters(qfig_ref), st_ref[...], lb_ref[...], gn_ref[...])
        dq, df, di, dg, dst, dlb, dgn = vjp((do_ref[...], d_state[...]))
        for seg, val in enumerate((dq, df, di, dg)):
            dqfig_ref[:, seg * D:(seg + 1) * D] = val.astype(dqfig_ref.dtype)
        d_state[...] = dst
        dlb_ref[...] += dlb
        dgn_ref[...] += dgn

    rev = lambda j: nb - 1 - j
    return pl.pallas_call(
        body, name="hgrn_backward", grid=(nb,),
        in_specs=[pl.BlockSpec((tb, 4 * D), lambda j: (rev(j), 0)),
                  pl.BlockSpec((None, N_HEADS_A, LANES, LANES), lambda j: (rev(j), 0, 0, 0)),
                  pl.BlockSpec((tb, D), lambda j: (rev(j), 0)),
                  pl.BlockSpec((1, D), lambda j: (0, 0)), pl.BlockSpec((1, LANES), lambda j: (0, 0)),
                  pl.BlockSpec(memory_space=pl.ANY)],
        out_specs=[pl.BlockSpec((tb, 4 * D), lambda j: (rev(j), 0)),
                   pl.BlockSpec((1, D), lambda j: (0, 0)), pl.BlockSpec((1, LANES), lambda j: (0, 0))],
        out_shape=[jax.ShapeDtypeStruct(d_proj.shape, d_proj.dtype), jax.ShapeDtypeStruct((1, D), F32),
                   jax.ShapeDtypeStruct((1, LANES), F32)],
        input_output_aliases={5: 0},
        scratch_shapes=[pltpu.VMEM((N_HEADS_A, LANES, LANES), F32)],
        compiler_params=_params(("arbitrary",)),
    )(proj, states, d_out, lb, gn, d_proj)


def _ssd_in_specs(tb, tmap):
    return [pl.BlockSpec((tb, 512), lambda g, j: (tmap(j), g)),
            pl.BlockSpec((tb, LANES), lambda g, j: (tmap(j), 16 + g)),
            pl.BlockSpec((tb, LANES), lambda g, j: (tmap(j), 20 + g)),
            pl.BlockSpec((tb, LANES), lambda g, j: (tmap(j), COL_DT // LANES)),
            pl.BlockSpec((tb, 512), lambda g, j: (tmap(j), COL_Z // 512 + g))]


def ssd_forward(xc, proj, dtb, alog, dsk, nw):
    t = proj.shape[0]
    tb = _time_block(t, SSD_TIME_BLOCK)
    nb = t // tb

    def body(x_ref, b_ref, c_ref, dt_ref, z_ref, dtb_ref, alog_ref, dsk_ref, nw_ref, o_ref, st_ref, state):
        @pl.when(pl.program_id(1) == 0)
        def _():
            state[...] = jnp.zeros_like(state)

        st = state[...]
        st_ref[...] = st
        out, st_new = ssd_block(x_ref[...], b_ref[...], c_ref[...], dt_ref[...], z_ref[...], st,
                                dtb_ref[...], alog_ref[...], dsk_ref[...], nw_ref[...], ssd_consts(pl.program_id(0)))
        o_ref[...] = out.astype(o_ref.dtype)
        state[...] = st_new

    vec = pl.BlockSpec((1, 512), lambda g, j: (0, g))
    heads = pl.BlockSpec((1, LANES), lambda g, j: (0, 0))
    return pl.pallas_call(
        body, name="ssd_forward", grid=(N_GROUPS_B, nb),
        in_specs=_ssd_in_specs(tb, lambda j: j) + [heads, vec, vec, vec],
        out_specs=[pl.BlockSpec((tb, 512), lambda g, j: (j, g)),
                   pl.BlockSpec((None, None, LANES, 512), lambda g, j: (j, g, 0, 0))],
        out_shape=[jax.ShapeDtypeStruct((t, B_INNER), BF16),
                   jax.ShapeDtypeStruct((nb, N_GROUPS_B, LANES, 512), F32)],
        scratch_shapes=[pltpu.VMEM((LANES, 512), F32)],
        compiler_params=_params(("arbitrary", "arbitrary")),
    )(xc, xc, xc, proj, proj, dtb, alog, dsk, nw)


def ssd_backward(xc, proj, states, d_out, dtb, alog, dsk, nw, d_proj):
    t = proj.shape[0]
    tb = _time_block(t, SSD_TIME_BLOCK)
    nb = t // tb
    rev = lambda j: nb - 1 - j

    def body(x_ref, b_ref, c_ref, dt_ref, z_ref, st_ref, do_ref, dtb_ref, alog_ref, dsk_ref, nw_ref, _,
             dx_ref, db_ref, dc_ref, ddt_ref, dz_ref, ddtb_ref, dalog_ref, ddsk_ref, dnw_ref, d_state):
        accs = (ddtb_ref, dalog_ref, ddsk_ref, dnw_ref)

        @pl.when(pl.program_id(1) == 0)
        def _():
            d_state[...] = jnp.zeros_like(d_state)
            for ref in accs:
                ref[...] = jnp.zeros_like(ref)

        cs = ssd_consts(pl.program_id(0))
        fn = lambda *a: ssd_block(*a, cs)
        _, vjp = jax.vjp(fn, x_ref[...], b_ref[...], c_ref[...], dt_ref[...], z_ref[...], st_ref[...],
                         dtb_ref[...], alog_ref[...], dsk_ref[...], nw_ref[...])
        dx, db, dc, ddt, dz, dst, *dpar = vjp((do_ref[...], d_state[...]))
        dx_ref[...] = dx
        db_ref[...] = db
        dc_ref[...] = dc
        ddt_ref[...] = ddt
        dz_ref[...] = dz.astype(dz_ref.dtype)
        d_state[...] = dst
        for ref, val in zip(accs, dpar, strict=True):
            ref[...] += val

    vec = pl.BlockSpec((1, 512), lambda g, j: (0, g))
    heads = pl.BlockSpec((1, LANES), lambda g, j: (0, 0))
    acc = pl.BlockSpec((None, 1, 512), lambda g, j: (g, 0, 0))
    acc_heads = pl.BlockSpec((None, 1, LANES), lambda g, j: (g, 0, 0))
    return pl.pallas_call(
        body, name="ssd_backward", grid=(N_GROUPS_B, nb),
        in_specs=_ssd_in_specs(tb, rev)
        + [pl.BlockSpec((None, None, LANES, 512), lambda g, j: (rev(j), g, 0, 0)),
           pl.BlockSpec((tb, 512), lambda g, j: (rev(j), g))] + [heads, vec, vec, vec] + [pl.BlockSpec(memory_space=pl.ANY)],
        out_specs=[pl.BlockSpec((tb, 512), lambda g, j: (rev(j), g)),
                   pl.BlockSpec((tb, LANES), lambda g, j: (rev(j), g)),
                   pl.BlockSpec((tb, LANES), lambda g, j: (rev(j), g)),
                   pl.BlockSpec((None, tb, LANES), lambda g, j: (g, rev(j), 0)),
                   pl.BlockSpec((tb, 512), lambda g, j: (rev(j), COL_Z // 512 + g)), acc_heads, acc, acc, acc],
        out_shape=[jax.ShapeDtypeStruct((t, B_INNER), F32), jax.ShapeDtypeStruct((t, 512), F32),
                   jax.ShapeDtypeStruct((t, 512), F32), jax.ShapeDtypeStruct((N_GROUPS_B, t, LANES), F32),
                   jax.ShapeDtypeStruct(d_proj.shape, d_proj.dtype)]
        + [jax.ShapeDtypeStruct((N_GROUPS_B, 1, LANES), F32)] + [jax.ShapeDtypeStruct((N_GROUPS_B, 1, 512), F32)] * 3,
        input_output_aliases={11: 4},
        scratch_shapes=[pltpu.VMEM((LANES, 512), F32)],
        compiler_params=_params(("arbitrary", "arbitrary")),
    )(xc, xc, xc, proj, proj, states, d_out, dtb, alog, dsk, nw, d_proj)


CONV_HALO = 8


def _shift_down(halo_then_tile, s, tm):
    if s == 0:
        return halo_then_tile[CONV_HALO:CONV_HALO + tm]
    return pltpu.roll(halo_then_tile, s, 0)[CONV_HALO:CONV_HALO + tm]


def _conv_pre(cur, prev, w, b, tm):
    stacked = jnp.concatenate([prev, cur], axis=0)
    taps = [_shift_down(stacked, 3 - j, tm) for j in range(4)]
    pre = b + taps[0] * w[0:1] + taps[1] * w[1:2] + taps[2] * w[2:3] + taps[3] * w[3:4]
    return pre, taps


def _conv_specs(t, tm):
    per = tm // CONV_HALO
    cur = pl.BlockSpec((tm, CONV_DIM), lambda i: (i, COL_XBC // CONV_DIM))
    prev = pl.BlockSpec((CONV_HALO, CONV_DIM), lambda i: (jnp.maximum(i * per - 1, 0), COL_XBC // CONV_DIM))
    return cur, prev


def conv_forward(proj, w, b):
    t = proj.shape[0]
    tm = _pick(t, (256, 128, 64))

    def body(cur_ref, prev_ref, w_ref, b_ref, o_ref):
        prev = jnp.where(pl.program_id(0) == 0, 0.0, prev_ref[...])
        pre, _ = _conv_pre(cur_ref[...], prev, w_ref[...], b_ref[...], tm)
        o_ref[...] = silu(pre)

    cur, prev = _conv_specs(t, tm)
    return pl.pallas_call(
        body, name="conv_forward", grid=(t // tm,),
        in_specs=[cur, prev, pl.BlockSpec((4, CONV_DIM), lambda i: (0, 0)), pl.BlockSpec((1, CONV_DIM), lambda i: (0, 0))],
        out_specs=pl.BlockSpec((tm, CONV_DIM), lambda i: (i, 0)),
        out_shape=jax.ShapeDtypeStruct((t, CONV_DIM), F32),
        compiler_params=_params(("arbitrary",)),
    )(proj, proj, w, b)


def conv_backward(proj, dx, db_, dc_, w, b, d_proj):
    t = proj.shape[0]
    tm = _pick(t, (256, 128, 64))
    per = tm // CONV_HALO
    nt = t // tm
    rev = lambda i: nt - 1 - i

    def body(cur_ref, prev_ref, dx_ref, dbm_ref, dcm_ref, w_ref, b_ref, _, o_ref, dw_ref, dbias_ref, later):
        @pl.when(pl.program_id(0) == 0)
        def _():
            dw_ref[...] = jnp.zeros_like(dw_ref)
            dbias_ref[...] = jnp.zeros_like(dbias_ref)
            later[...] = jnp.zeros_like(later)

        first_tile = pl.program_id(0) == nt - 1
        for lo, hi, src in ((0, B_INNER, dx_ref), (B_INNER, B_INNER + 512, dbm_ref), (B_INNER + 512, CONV_DIM, dcm_ref)):
            cols = slice(lo, hi)
            prev = jnp.where(first_tile, 0.0, prev_ref[:, cols])
            w_ = w_ref[:, cols]
            pre, taps = _conv_pre(cur_ref[:, cols], prev, w_, b_ref[:, cols], tm)
            sg = sigmoid(pre)
            dpre = src[...] * (sg * (1.0 + pre * (1.0 - sg)))
            dbias_ref[:, cols] += jnp.sum(dpre, axis=0, keepdims=True)
            for j in range(4):
                dw_ref[j:j + 1, cols] += jnp.sum(dpre * taps[j], axis=0, keepdims=True)
            stacked = jnp.concatenate([dpre, later[:, cols]], axis=0)
            acc = dpre * w_[3:4]
            for j in range(3):
                acc = acc + pltpu.roll(stacked, tm + CONV_HALO - (3 - j), 0)[0:tm] * w_[j:j + 1]
            o_ref[:, cols] = acc.astype(o_ref.dtype)
            later[:, cols] = dpre[0:CONV_HALO]

    row = lambda w_: pl.BlockSpec((tm, w_), lambda i: (rev(i), 0))
    whole = lambda r: pl.BlockSpec((r, CONV_DIM), lambda i: (0, 0))
    return pl.pallas_call(
        body, name="conv_backward", grid=(nt,),
        in_specs=[pl.BlockSpec((tm, CONV_DIM), lambda i: (rev(i), COL_XBC // CONV_DIM)),
                  pl.BlockSpec((CONV_HALO, CONV_DIM), lambda i: (jnp.maximum(rev(i) * per - 1, 0), COL_XBC // CONV_DIM)),
                  row(B_INNER), row(512), row(512), whole(4), whole(1), pl.BlockSpec(memory_space=pl.ANY)],
        out_specs=[pl.BlockSpec((tm, CONV_DIM), lambda i: (rev(i), COL_XBC // CONV_DIM)), whole(4), whole(1)],
        out_shape=[jax.ShapeDtypeStruct(d_proj.shape, d_proj.dtype), jax.ShapeDtypeStruct((4, CONV_DIM), F32),
                   jax.ShapeDtypeStruct((1, CONV_DIM), F32)],
        input_output_aliases={7: 0},
        scratch_shapes=[pltpu.VMEM((CONV_HALO, CONV_DIM), F32)],
        compiler_params=_params(("arbitrary",)),
    )(proj, proj, dx, db_, dc_, w, b, d_proj)


def stage_modulate(x, sc, sh):
    return _ln(x) * (1.0 + sc) + sh


def stage_merge(ga, gb, ya, yb):
    return sigmoid(ga) * ya + sigmoid(gb) * yb


def stage_post_mixer(x, h, g1, ln_g, ln_b, sc2, sh2):
    x1 = _ln(ALPHA * x + g1 * h) * ln_g + ln_b
    return x1, _ln(x1) * (1.0 + sc2) + sh2


def stage_swiglu(a, b):
    return silu(a) * b


def gate_up(ab):
    w = FFN_SHARD
    return (jnp.concatenate([ab[:, 2 * w * k:2 * w * k + w] for k in range(4)], axis=1),
            jnp.concatenate([ab[:, 2 * w * k + w:2 * w * (k + 1)] for k in range(4)], axis=1))


def per_chip(gate, up):
    w = FFN_SHARD
    return jnp.concatenate([part[:, w * k:w * (k + 1)] for k in range(4) for part in (gate, up)], axis=1)


def stage_loss(x1, hf, tgt, g2, ln_g, ln_b):
    x2 = _ln(ALPHA * x1 + g2 * hf) * ln_g + ln_b
    return 0.5 * jnp.sum(jnp.mean(jnp.square(x2 - tgt), axis=-1, keepdims=True), axis=0, keepdims=True)


def local_step(x, tgt, mod, wts, small, early=None, mid=None, late=None, last=None):
    sh1, sc1, g1, sh2, sc2, g2 = mod
    lb, gn, conv_w, conv_b, dtb, alog, dsk, nw, ln1_g, ln1_b, ln2_g, ln2_b = small
    vec = (1, D)

    (u1,) = rowwise("modulate1", lambda r, c: ((stage_modulate(r[0], *c),), ()), [_full(x)], [sc1, sh1], [(D, BF16)])
    w_in = wts.input_projection(u1)
    proj = matmul(u1, w_in, "nn", F32, "in_proj")
    ya_in, st_a = hgrn_forward(proj, lb, gn + wts.start_rest(proj)[0:1])
    xc = conv_forward(proj, conv_w, conv_b)
    w_a, w_b, w_o, w_gu, w_d = wts.rest(xc)
    yb_in, st_b = ssd_forward(xc, proj, dtb, alog, dsk, nw)
    ya = matmul(ya_in, w_a, "nn", F32, "branch_a")
    yb = matmul(yb_in, w_b, "nn", F32, "branch_b")
    gate_rows = [(proj, D, COL_GA // D), (proj, D, COL_GB // D), _full(ya), _full(yb)]
    (merged,) = rowwise("merge", lambda r, c: ((stage_merge(*r),), ()), gate_rows, [], [(D, BF16)])
    h = matmul(merged, w_o, "nn", F32, "out_proj")
    post_consts = [g1, ln1_g, ln1_b, sc2, sh2]
    x1, u2 = rowwise("post_mixer", lambda r, c: (stage_post_mixer(*r, *c), ()), [_full(x), _full(h)], post_consts,
                     [(D, F32), (D, BF16)])
    ab = matmul(u2, w_gu, "nt", F32, "ffn_in")
    (p,) = rowwise("swiglu", lambda r, c: ((stage_swiglu(*gate_up(r[0])),), ()), [_full(ab)], [], [(D_FF, BF16)],
                   tm_max=256)
    hf = matmul(p, w_d, "nn", F32, "ffn_out")

    def loss_bwd(r, c):
        loss, vjp = jax.vjp(stage_loss, *r, *c)
        dx1, dhf, _, dg2, dlg, dlb_ = vjp(jnp.ones((1, 1), F32))
        return (dx1, dhf), (loss, dg2, dlg, dlb_)

    dx1, dhf, loss, dg2, dln2_g, dln2_b = rowwise(
        "loss_backward", loss_bwd, [_full(x1), _full(hf), _full(tgt)], [g2, ln2_g, ln2_b],
        [(D, F32), (D, BF16)], [(1, 1), vec, vec, vec])
    dp = matmul(dhf, w_d, "nt", F32, "ffn_out_dx")
    dw_d = matmul(p, dhf, "tn", F32, "ffn_out_dw")

    def swiglu_bwd(r, c):
        _, vjp = jax.vjp(stage_swiglu, *gate_up(r[0]))
        return (per_chip(*vjp(r[1])),), ()

    (dab,) = rowwise("swiglu_backward", swiglu_bwd, [_full(ab), _full(dp)], [], [(2 * D_FF, BF16)], tm_max=256)
    du2 = matmul(dab, w_gu, "nn", F32, "ffn_in_dx")
    dw_gu = matmul(dab, u2, "tn", F32, "ffn_in_dw")

    def post_bwd(r, c):
        _, vjp = jax.vjp(stage_post_mixer, r[0], r[1], *c)
        dx, dh, *dc = vjp((r[2], r[3]))
        return (dx, dh), tuple(dc)

    dx_a, dh, dg1, dln1_g, dln1_b, dsc2, dsh2 = rowwise(
        "post_mixer_backward", post_bwd, [_full(x), _full(h), _full(dx1), _full(du2)], post_consts,
        [(D, F32), (D, BF16)], [vec] * 5)
    dmerged = matmul(dh, w_o, "nt", F32, "out_proj_dx")
    dw_o = matmul(merged, dh, "tn", F32, "out_proj_dw")

    def merge_bwd(r, c):
        _, vjp = jax.vjp(stage_merge, *r[:4])
        dga, dgb, dya, dyb = vjp(r[4])
        return (jnp.concatenate([dga, dgb], axis=1), dya, dyb), ()

    dproj, dya, dyb = rowwise("merge_backward", merge_bwd, gate_rows + [_full(dmerged)], [],
                              [(2 * D, BF16), (D, BF16), (D, BF16)], new_wide=(IN_PAD, COL_GA // (2 * D)))
    dya_in = matmul(dya, w_a, "nt", F32, "branch_a_dx")
    dw_a = matmul(ya_in, dya, "tn", F32, "branch_a_dw")
    dyb_in = matmul(dyb, w_b, "nt", F32, "branch_b_dx")
    dw_b = matmul(yb_in, dyb, "tn", F32, "branch_b_dw")
    gn_after = gn if early is None else gn + early((dw_a, dw_b, dw_o, dw_gu, dw_d))[0:1]
    dproj, dlb, dgn = hgrn_backward(proj, st_a, dya_in, lb, gn_after, dproj)
    dtb_after = dtb if mid is None else dtb + mid(dlb)[0:1, 0:1]
    dxs, dbm, dcm, ddt, dproj, ddtb, dalog, ddsk, dnw = ssd_backward(xc, proj, st_b, dyb_in, dtb_after, alog, dsk, nw, dproj)
    dproj, dconv_w, dconv_b = conv_backward(proj, dxs, dbm, dcm, conv_w, conv_b, dproj)
    if late is not None:
        late(dconv_b)
    t = x.shape[0]
    tail = jnp.concatenate([jnp.sum(ddt, axis=0).astype(BF16), jnp.zeros((t, IN_PAD - COL_DT - LANES), BF16)], axis=1)
    dproj = lax.dynamic_update_slice(dproj, tail, (0, COL_DT))
    if last is None:
        dw_in, started = matmul(u1, dproj, "tn", F32, "in_proj_dw"), None
    else:
        dw_in, started = None, last(u1, dproj)
    du1 = matmul(dproj, w_in, "nt", F32, "in_proj_dx", after=started)

    def mod_bwd(r, c):
        _, vjp = jax.vjp(stage_modulate, r[0], *c)
        dx, dsc, dsh = vjp(r[1])
        return (dx + r[2],), (dsc, dsh)

    grad_x, dsc1, dsh1 = rowwise("modulate1_backward", mod_bwd, [_full(x), _full(du1), _full(dx_a)], [sc1, sh1],
                                 [(D, F32)], [vec, vec])
    d_mod = (dsh1, dsc1, dg1, dsh2, dsc2, dg2)
    d_wts = (dw_in, dw_a, dw_b, dw_o, dw_gu, dw_d)
    d_small = (dlb, dgn, dconv_w, dconv_b, jnp.sum(ddtb, axis=0),
               dalog.reshape(1, B_INNER), ddsk.reshape(1, B_INNER), dnw.reshape(1, B_INNER),
               dln1_g, dln1_b, dln2_g, dln2_b)
    return loss, grad_x, d_mod, d_wts, d_small


HBM = pl.BlockSpec(memory_space=pltpu.HBM)
SEM = pl.BlockSpec(memory_space=pltpu.SEMAPHORE)
DATAFLOW = pltpu.SideEffectType.DATAFLOW_SIDE_EFFECTING


def _place():
    return lax.axis_index("x"), lax.axis_index("y"), lax.axis_index("c")


def _other_chips(x, y):
    return [(1 - x, y), (x, 1 - y), (1 - x, 1 - y)]


def _remote(src, dst, send_sem, recv_sem, device):
    return pltpu.make_async_remote_copy(src_ref=src, dst_ref=dst, send_sem=send_sem, recv_sem=recv_sem,
                                        device_id=device, device_id_type=MESH)


def gather_rows(v, name):
    n = v.shape[1]

    def body(v_ref, out_ref, send_sems, recv_sems, local_sem):
        x, y, c = _place()
        mine = pltpu.make_async_copy(v_ref, out_ref.at[4 * x + 2 * y + c], local_sem)
        mine.start()
        sends, recvs = [], []
        for m in range(1, 8):
            px = 1 - x if m & 4 else x
            py = 1 - y if m & 2 else y
            pc = 1 - c if m & 1 else c
            sends.append(_remote(v_ref, out_ref.at[4 * x + 2 * y + c], send_sems.at[m - 1], recv_sems.at[m - 1], (px, py, pc)))
            recvs.append(_remote(v_ref, out_ref.at[4 * px + 2 * py + pc], send_sems.at[m - 1], recv_sems.at[m - 1], (px, py, pc)))
        for cp in sends:
            cp.start()
        for cp in recvs:
            cp.wait_recv()
        for cp in sends:
            cp.wait_send()
        mine.wait()

    return pl.pallas_call(
        body, name=name, in_specs=[HBM], out_specs=HBM,
        out_shape=jax.ShapeDtypeStruct((8, 1, n), v.dtype),
        scratch_shapes=[pltpu.SemaphoreType.DMA((7,)), pltpu.SemaphoreType.DMA((7,)), pltpu.SemaphoreType.DMA],
    )(v)


def exchange_rows(part, name):
    w = part.shape[2]

    def body(p_ref, out_ref, send_sems, recv_sems, local_sem):
        x, y, c = _place()
        k = 2 * x + y
        mine = pltpu.make_async_copy(p_ref.at[4 * x + 2 * y + c], out_ref.at[k], local_sem)
        mine.start()
        sends, recvs = [], []
        for j, (px, py) in enumerate(_other_chips(x, y)):
            sends.append(_remote(p_ref.at[4 * px + 2 * py + c], out_ref.at[k], send_sems.at[j], recv_sems.at[j], (px, py, c)))
            recvs.append(_remote(p_ref.at[4 * px + 2 * py + c], out_ref.at[2 * px + py], send_sems.at[j], recv_sems.at[j], (px, py, c)))
        for cp in sends:
            cp.start()
        for cp in recvs:
            cp.wait_recv()
        for cp in sends:
            cp.wait_send()
        mine.wait()

    return pl.pallas_call(
        body, name=name, in_specs=[HBM], out_specs=HBM,
        out_shape=jax.ShapeDtypeStruct((4, 1, w), part.dtype),
        scratch_shapes=[pltpu.SemaphoreType.DMA((3,)), pltpu.SemaphoreType.DMA((3,)), pltpu.SemaphoreType.DMA],
    )(part)


def _half_of_slot(ref, rows, px, py, pc):
    return ref.at[2 * px + py, pl.ds(pc * (rows // 2), rows // 2), :]


def gather_start(shards, after, tag):
    n = len(shards)

    def body(*refs):
        w_refs, land_refs = refs[:n], refs[n:2 * n]
        send_sems, recv_sems = refs[2 * n + 1], refs[2 * n + 2]
        token = refs[-1]
        x, y, c = _place()
        for i in range(n):
            rows = shards[i].shape[0]
            for j, (px, py) in enumerate(_other_chips(x, y)):
                _remote(w_refs[i].at[pl.ds(c * (rows // 2), rows // 2), :], _half_of_slot(land_refs[i], rows, x, y, c),
                        send_sems.at[j * n + i], recv_sems.at[j * n + i], (px, py, c)).start()
        token[...] = jnp.zeros_like(token)

    hbm = lambda a: pltpu.with_memory_space_constraint(a, pltpu.HBM)
    lands = [lax.empty((4,) + s.shape, s.dtype) for s in shards]
    dma = pltpu.SemaphoreType.DMA
    return pl.pallas_call(
        body, name="gather_start_" + tag,
        out_shape=(dma((3 * n,)), dma((3 * n,)),
                   *[pltpu.HBM(a.shape, a.dtype) for a in list(shards) + lands], jax.ShapeDtypeStruct((8, LANES), F32)),
        in_specs=[HBM] * (2 * n) + [pl.BlockSpec(memory_space=pl.ANY)],
        out_specs=(SEM, SEM, *[HBM] * (2 * n), pl.BlockSpec(memory_space=pltpu.VMEM)),
        input_output_aliases={i: 2 + i for i in range(2 * n)},
        compiler_params=pltpu.CompilerParams(has_side_effects=DATAFLOW),
    )(*[hbm(a) for a in list(shards) + lands], after)


def gather_wait(send_sems, recv_sems, shards, lands, after, tag):
    n = len(shards)

    def body(*refs):
        w_refs, land_refs = refs[:n], refs[n:2 * n]
        send_ref, recv_ref = refs[2 * n], refs[2 * n + 1]
        x, y, c = _place()
        for i in range(n):
            rows = shards[i].shape[0]
            for j, (px, py) in enumerate(_other_chips(x, y)):
                cp = _remote(w_refs[i].at[pl.ds(c * (rows // 2), rows // 2), :], _half_of_slot(land_refs[i], rows, px, py, c),
                             send_ref.at[j * n + i], recv_ref.at[j * n + i], (px, py, c))
                cp.wait_send()
                cp.wait_recv()

    out = pl.pallas_call(
        body, name="gather_wait_" + tag,
        out_shape=tuple(pltpu.HBM(a.shape, a.dtype) for a in list(shards) + list(lands)),
        in_specs=[HBM] * (2 * n) + [SEM, SEM, pl.BlockSpec(memory_space=pl.ANY)], out_specs=tuple([HBM] * (2 * n)),
        input_output_aliases={i: i for i in range(2 * n)},
        compiler_params=pltpu.CompilerParams(has_side_effects=DATAFLOW),
    )(*shards, *lands, send_sems, recv_sems, after)
    return list(out[:n]), list(out[n:])


def forward_start(lands, tag):
    n = len(lands)

    def body(*refs):
        land_refs = refs[:n]
        send_sems, recv_sems = refs[n], refs[n + 1]
        token = refs[-1]
        x, y, c = _place()
        for i in range(n):
            rows = lands[i].shape[1]
            for j, (px, py) in enumerate(_other_chips(x, y)):
                mine = _half_of_slot(land_refs[i], rows, px, py, c)
                _remote(mine, mine, send_sems.at[j * n + i], recv_sems.at[j * n + i], (x, y, 1 - c)).start()
        token[...] = jnp.zeros_like(token)

    dma = pltpu.SemaphoreType.DMA
    return pl.pallas_call(
        body, name="forward_start_" + tag,
        out_shape=(dma((3 * n,)), dma((3 * n,)), *[pltpu.HBM(a.shape, a.dtype) for a in lands],
                   jax.ShapeDtypeStruct((8, LANES), F32)),
        in_specs=[HBM] * n, out_specs=(SEM, SEM, *[HBM] * n, pl.BlockSpec(memory_space=pltpu.VMEM)),
        input_output_aliases={i: 2 + i for i in range(n)},
        compiler_params=pltpu.CompilerParams(has_side_effects=DATAFLOW),
    )(*lands)


def forward_wait(started, after, tag):
    send_sems, recv_sems, *rest = started
    lands = rest[:-1]
    n = len(lands)

    def body(*refs):
        land_refs = refs[:n]
        send_ref, recv_ref = refs[n], refs[n + 1]
        x, y, c = _place()
        for i in range(n):
            rows = lands[i].shape[1]
            for j, (px, py) in enumerate(_other_chips(x, y)):
                cp = _remote(_half_of_slot(land_refs[i], rows, px, py, c), _half_of_slot(land_refs[i], rows, px, py, 1 - c),
                             send_ref.at[j * n + i], recv_ref.at[j * n + i], (x, y, 1 - c))
                cp.wait_send()
                cp.wait_recv()

    out = pl.pallas_call(
        body, name="forward_wait_" + tag,
        out_shape=tuple(pltpu.HBM(a.shape, a.dtype) for a in lands),
        in_specs=[HBM] * n + [SEM, SEM, pl.BlockSpec(memory_space=pl.ANY)], out_specs=tuple([HBM] * n),
        input_output_aliases={i: i for i in range(n)},
        compiler_params=pltpu.CompilerParams(has_side_effects=DATAFLOW),
    )(*lands, send_sems, recv_sems, after)
    return list(out)


def pair_start(slabs, tag):
    n = len(slabs)

    def body(*refs):
        g_refs, land_refs = refs[:n], refs[n:2 * n]
        send_sems, recv_sems = refs[2 * n], refs[2 * n + 1]
        token = refs[-1]
        x, y, c = _place()
        for i in range(n):
            hr = slabs[i].shape[1] // 2
            _remote(g_refs[i].at[:, pl.ds((1 - c) * hr, hr), :], land_refs[i], send_sems.at[i], recv_sems.at[i],
                    (x, y, 1 - c)).start()
        token[...] = jnp.zeros_like(token)

    hbm = lambda a: pltpu.with_memory_space_constraint(a, pltpu.HBM)
    lands = [lax.empty((4, s.shape[1] // 2, s.shape[2]), s.dtype) for s in slabs]
    dma = pltpu.SemaphoreType.DMA
    return pl.pallas_call(
        body, name="pair_start_" + tag,
        out_shape=(dma((n,)), dma((n,)), *[pltpu.HBM(a.shape, a.dtype) for a in list(slabs) + lands],
                   jax.ShapeDtypeStruct((8, LANES), F32)),
        in_specs=[HBM] * (2 * n), out_specs=(SEM, SEM, *[HBM] * (2 * n), pl.BlockSpec(memory_space=pltpu.VMEM)),
        input_output_aliases={i: 2 + i for i in range(2 * n)},
        compiler_params=pltpu.CompilerParams(has_side_effects=DATAFLOW),
    )(*[hbm(a) for a in list(slabs) + lands])


def pair_wait(started, after, tag):
    send_sems, recv_sems, *rest = started
    n = (len(rest) - 1) // 2
    slabs, lands = rest[:n], rest[n:2 * n]

    def body(*refs):
        g_refs, land_refs = refs[:n], refs[n:2 * n]
        send_ref, recv_ref = refs[2 * n], refs[2 * n + 1]
        x, y, c = _place()
        for i in range(n):
            hr = slabs[i].shape[1] // 2
            cp = _remote(g_refs[i].at[:, pl.ds((1 - c) * hr, hr), :], land_refs[i], send_ref.at[i], recv_ref.at[i], (x, y, 1 - c))
            cp.wait_send()
            cp.wait_recv()

    out = pl.pallas_call(
        body, name="pair_wait_" + tag,
        out_shape=tuple(pltpu.HBM(a.shape, a.dtype) for a in list(slabs) + list(lands)),
        in_specs=[HBM] * (2 * n) + [SEM, SEM, pl.BlockSpec(memory_space=pl.ANY)], out_specs=tuple([HBM] * (2 * n)),
        input_output_aliases={i: i for i in range(2 * n)},
        compiler_params=pltpu.CompilerParams(has_side_effects=DATAFLOW),
    )(*slabs, *lands, send_sems, recv_sems, after)
    return list(out[:n]), list(out[n:])


def _tile2(rows, cols):
    fits = lambda r, c: r * c * 4 <= BLOCK_BYTES
    if fits(rows, cols):
        return rows, cols
    tiles = [(r, cols) for r in (1024, 512, 256, 128, 64) if rows % r == 0 and fits(r, cols)]
    tiles += [(rows, cols // k) for k in (2, 3, 4, 6, 8, 12, 16) if cols % (k * LANES) == 0 and fits(rows, cols // k)]
    return max(tiles, key=lambda t: t[0] * t[1])


def pair_add(g, p, c, name):
    _, hr, cols = p.shape
    tm, tc = _tile2(hr, cols)
    per = hr // tm

    def body(c_ref, g_ref, p_ref, o_ref):
        o_ref[...] = (g_ref[...] + p_ref[...]).astype(o_ref.dtype)

    return pl.pallas_call(
        body, name=name,
        grid_spec=pltpu.PrefetchScalarGridSpec(
            num_scalar_prefetch=1, grid=(4, per, cols // tc),
            in_specs=[pl.BlockSpec((None, tm, tc), lambda k, i, j, c_ref: (k, c_ref[0] * per + i, j)),
                      pl.BlockSpec((None, tm, tc), lambda k, i, j, c_ref: (k, i, j))],
            out_specs=pl.BlockSpec((None, tm, tc), lambda k, i, j, c_ref: (k, i, j))),
        out_shape=jax.ShapeDtypeStruct((4, hr, cols), BF16),
        compiler_params=_params(("arbitrary", "arbitrary", "arbitrary")),
    )(c.reshape(1).astype(jnp.int32), g, p)


def scatter_start(sums, tag):
    n = len(sums)

    def body(*refs):
        s_refs, land_refs = refs[:n], refs[n:2 * n]
        send_sems, recv_sems = refs[2 * n], refs[2 * n + 1]
        token = refs[-1]
        x, y, c = _place()
        k = 2 * x + y
        for i in range(n):
            for j, (px, py) in enumerate(_other_chips(x, y)):
                _remote(s_refs[i].at[2 * px + py], land_refs[i].at[k], send_sems.at[j * n + i], recv_sems.at[j * n + i],
                        (px, py, c)).start()
        token[...] = jnp.zeros_like(token)

    hbm = lambda a: pltpu.with_memory_space_constraint(a, pltpu.HBM)
    return pl.pallas_call(
        body, name="scatter_start_" + tag,
        out_shape=(pltpu.SemaphoreType.DMA((3 * n,)), pltpu.SemaphoreType.DMA((3 * n,)),
                   *[pltpu.HBM(s.shape, s.dtype) for s in sums], *[pltpu.HBM(s.shape, s.dtype) for s in sums],
                   jax.ShapeDtypeStruct((8, LANES), F32)),
        in_specs=[HBM] * (2 * n), out_specs=(SEM, SEM, *[HBM] * (2 * n), pl.BlockSpec(memory_space=pltpu.VMEM)),
        input_output_aliases={i: 2 + i for i in range(2 * n)},
        compiler_params=pltpu.CompilerParams(has_side_effects=DATAFLOW),
    )(*[hbm(s) for s in sums], *[hbm(lax.empty(s.shape, s.dtype)) for s in sums])


def scatter_wait(started, after, tag):
    send_sems, recv_sems, *rest = started
    n = (len(rest) - 1) // 2
    sums, lands = rest[:n], rest[n:2 * n]

    def body(*refs):
        s_refs, land_refs = refs[:n], refs[n:2 * n]
        send_ref, recv_ref = refs[2 * n], refs[2 * n + 1]
        x, y, c = _place()
        for i in range(n):
            for j, (px, py) in enumerate(_other_chips(x, y)):
                cp = _remote(s_refs[i].at[2 * px + py], land_refs[i].at[2 * px + py], send_ref.at[j * n + i],
                             recv_ref.at[j * n + i], (px, py, c))
                cp.wait_send()
                cp.wait_recv()

    out = pl.pallas_call(
        body, name="scatter_wait_" + tag,
        out_shape=tuple(pltpu.HBM(s.shape, s.dtype) for s in sums + lands),
        in_specs=[HBM] * (2 * n) + [SEM, SEM, pl.BlockSpec(memory_space=pl.ANY)], out_specs=tuple([HBM] * (2 * n)),
        input_output_aliases={i: i for i in range(2 * n)},
        compiler_params=pltpu.CompilerParams(has_side_effects=DATAFLOW),
    )(*sums, *lands, send_sems, recv_sems, after)
    return list(out[:n]), list(out[n:])


def sum_chips(landed, own, chip, core, name):
    _, hr, cols = landed.shape
    tm, tc = _tile2(hr, cols)
    per = hr // tm

    def body(idx_ref, l0, l1, l2, l3, own_ref, o_ref):
        mine = own_ref[...].astype(F32)
        v = [jnp.where(idx_ref[0] == k, mine, ref[...].astype(F32)) for k, ref in enumerate((l0, l1, l2, l3))]
        o_ref[...] = ((v[0] + v[1]) + v[2]) + v[3]

    slot = lambda k: pl.BlockSpec((None, tm, tc),
                                  lambda i, j, idx: (jnp.where(idx[0] == k, (k + 1) & 3, k), i, j))
    return pl.pallas_call(
        body, name=name,
        grid_spec=pltpu.PrefetchScalarGridSpec(
            num_scalar_prefetch=1, grid=(per, cols // tc),
            in_specs=[slot(0), slot(1), slot(2), slot(3),
                      pl.BlockSpec((None, tm, tc), lambda i, j, idx: (idx[0], i, j))],
            out_specs=pl.BlockSpec((tm, tc), lambda i, j, idx: (idx[1] * per + i, j))),
        out_shape=jax.ShapeDtypeStruct((2 * hr, cols), F32),
        compiler_params=_params(("arbitrary", "arbitrary")),
    )(jnp.stack([chip, core]).astype(jnp.int32), landed, landed, landed, landed, own)


def _halves_copies(refs, shapes, send_sems, recv_sems):
    x, y, c = _place()
    copies = []
    for i, ref in enumerate(refs):
        hr = shapes[i][0] // 2
        own = ref.at[pl.ds(c * hr, hr), :]
        other = ref.at[pl.ds((1 - c) * hr, hr), :]
        copies.append((_remote(own, own, send_sems.at[i], recv_sems.at[i], (x, y, 1 - c)),
                       _remote(other, other, send_sems.at[i], recv_sems.at[i], (x, y, 1 - c))))
    return copies


def halves_start(bufs):
    n = len(bufs)
    shapes = [b.shape for b in bufs]

    def body(*refs):
        for sent, _ in _halves_copies(refs[:n], shapes, refs[n], refs[n + 1]):
            sent.start()
        refs[-1][...] = jnp.zeros_like(refs[-1])

    hbm = lambda a: pltpu.with_memory_space_constraint(a, pltpu.HBM)
    dma = pltpu.SemaphoreType.DMA
    return pl.pallas_call(
        body, name="halves_start",
        out_shape=(dma((n,)), dma((n,)), *[pltpu.HBM(b.shape, b.dtype) for b in bufs],
                   jax.ShapeDtypeStruct((8, LANES), F32)),
        in_specs=[HBM] * n, out_specs=(SEM, SEM, *[HBM] * n, pl.BlockSpec(memory_space=pltpu.VMEM)),
        input_output_aliases={i: 2 + i for i in range(n)},
        compiler_params=pltpu.CompilerParams(has_side_effects=DATAFLOW),
    )(*[hbm(b) for b in bufs])


def halves_wait(started, after):
    send_sems, recv_sems, *bufs, _ = started
    n = len(bufs)
    shapes = [b.shape for b in bufs]

    def body(*refs):
        for sent, received in _halves_copies(refs[:n], shapes, refs[n], refs[n + 1]):
            received.wait_recv()
            sent.wait_send()

    return pl.pallas_call(
        body, name="halves_wait",
        out_shape=tuple(pltpu.HBM(b.shape, b.dtype) for b in bufs),
        in_specs=[HBM] * n + [SEM, SEM, pl.BlockSpec(memory_space=pl.ANY)], out_specs=tuple([HBM] * n),
        input_output_aliases={i: i for i in range(n)},
        compiler_params=pltpu.CompilerParams(has_side_effects=DATAFLOW),
    )(*bufs, send_sems, recv_sems, after)


def assemble_in_proj(landed, own, chip):
    rows, cols = 128, own.shape[1]

    def body(idx_ref, l0, l1, l2, l3, own_ref, o_ref):
        mine = own_ref[...]
        w = jnp.concatenate([jnp.where(idx_ref[0] == k, mine, ref[...]) for k, ref in enumerate((l0, l1, l2, l3))], axis=1)
        o_ref[...] = jnp.concatenate([w[:, :ORIG_Z], w[:, ORIG_GA:], w[:, ORIG_XBC:ORIG_DT], w[:, ORIG_Z:ORIG_XBC],
                                      w[:, ORIG_DT:ORIG_GA], jnp.zeros((rows, IN_PAD - IN_ORIG), w.dtype)], axis=1)

    slot = lambda k: pl.BlockSpec((None, rows, cols), lambda i, idx: (jnp.where(idx[0] == k, (k + 1) & 3, k), i, 0))
    return pl.pallas_call(
        body, name="assemble_in_proj",
        grid_spec=pltpu.PrefetchScalarGridSpec(
            num_scalar_prefetch=1, grid=(D // rows,),
            in_specs=[slot(0), slot(1), slot(2), slot(3), pl.BlockSpec((rows, cols), lambda i, idx: (i, 0))],
            out_specs=pl.BlockSpec((rows, IN_PAD), lambda i, idx: (i, 0))),
        out_shape=jax.ShapeDtypeStruct((D, IN_PAD), own.dtype),
        compiler_params=_params(("arbitrary",)),
    )(chip.reshape(1).astype(jnp.int32), landed, landed, landed, landed, own)


def rows_exchange(a, name):
    hr = a.shape[0] // 2

    def body(a_ref, out_ref, send_sem, recv_sem):
        x, y, c = _place()
        cp = _remote(a_ref.at[pl.ds((1 - c) * hr, hr), :], out_ref, send_sem, recv_sem, (x, y, 1 - c))
        cp.start()
        cp.wait()

    return pl.pallas_call(
        body, name=name, in_specs=[HBM], out_specs=HBM,
        out_shape=jax.ShapeDtypeStruct((hr, a.shape[1]), a.dtype),
        scratch_shapes=[pltpu.SemaphoreType.DMA, pltpu.SemaphoreType.DMA],
    )(a)


def rows_start(a, tag):
    hr = a.shape[0] // 2

    def body(a_ref, land_ref, send_sem, recv_sem, a_thru, land_thru, token):
        x, y, c = _place()
        _remote(a_ref.at[pl.ds((1 - c) * hr, hr), :], land_ref, send_sem, recv_sem, (x, y, 1 - c)).start()
        token[...] = jnp.zeros_like(token)

    hbm = lambda v: pltpu.with_memory_space_constraint(v, pltpu.HBM)
    dma = pltpu.SemaphoreType.DMA
    return pl.pallas_call(
        body, name="rows_start_" + tag,
        out_shape=(dma(()), dma(()), pltpu.HBM(a.shape, a.dtype), pltpu.HBM((hr, a.shape[1]), a.dtype),
                   jax.ShapeDtypeStruct((8, LANES), F32)),
        in_specs=[HBM, HBM], out_specs=(SEM, SEM, HBM, HBM, pl.BlockSpec(memory_space=pltpu.VMEM)),
        input_output_aliases={0: 2, 1: 3},
        compiler_params=pltpu.CompilerParams(has_side_effects=DATAFLOW),
    )(hbm(a), hbm(lax.empty((hr, a.shape[1]), a.dtype)))


def rows_wait(started, after, tag):
    send_sem, recv_sem, a, land, _ = started
    hr = a.shape[0] // 2

    def body(a_ref, land_ref, send_ref, recv_ref, after_ref, a_thru, got_ref):
        x, y, c = _place()
        cp = _remote(a_ref.at[pl.ds((1 - c) * hr, hr), :], land_ref, send_ref, recv_ref, (x, y, 1 - c))
        cp.wait_send()
        cp.wait_recv()

    return pl.pallas_call(
        body, name="rows_wait_" + tag,
        out_shape=(pltpu.HBM(a.shape, a.dtype), pltpu.HBM(land.shape, land.dtype)),
        in_specs=[HBM, HBM, SEM, SEM, pl.BlockSpec(memory_space=pl.ANY)], out_specs=(HBM, HBM),
        input_output_aliases={0: 0, 1: 1},
        compiler_params=pltpu.CompilerParams(has_side_effects=DATAFLOW),
    )(a, land, send_sem, recv_sem, after)


def split_pair_add(pieces, received, core):
    cols = IN_ORIG // 4
    rows, hr = 128, D // 2
    per = hr // rows
    n_p = len(pieces)

    def body(c_ref, *refs):
        o_ref = refs[-1]
        d = jnp.concatenate([refs[i][...] + refs[n_p + i][...] for i in range(n_p)], axis=1)
        w = jnp.concatenate([d[:, :COL_GA], d[:, COL_Z:COL_DT], d[:, COL_XBC:COL_Z], d[:, COL_DT:COL_DT + 32],
                             d[:, COL_GA:COL_XBC]], axis=1)
        for k in range(4):
            o_ref[k] = w[:, k * cols:(k + 1) * cols].astype(o_ref.dtype)

    return pl.pallas_call(
        body, name="split_pair_add",
        grid_spec=pltpu.PrefetchScalarGridSpec(
            num_scalar_prefetch=1, grid=(per,),
            in_specs=[pl.BlockSpec((rows, p.shape[1]), lambda i, c_ref: (c_ref[0] * per + i, 0)) for p in pieces]
            + [pl.BlockSpec((rows, p.shape[1]), lambda i, c_ref: (i, 0)) for p in received],
            out_specs=pl.BlockSpec((4, rows, cols), lambda i, c_ref: (0, i, 0))),
        out_shape=jax.ShapeDtypeStruct((4, hr, cols), BF16),
        compiler_params=_params(("arbitrary",)),
    )(core.reshape(1).astype(jnp.int32), *pieces, *received)


def ada_prepare(c_all, w_ada, hgrn_lb):
    def body(c_ref, w_ref, lb_ref, mod_ref, row_ref):
        mod_ref[...] = hdot(silu(c_ref[...]), w_ref[...])
        row_ref[...] = sigmoid(lb_ref[0:1, :] - lb_ref[1:2, :])

    return pl.pallas_call(
        body, name="ada_prepare",
        out_shape=[jax.ShapeDtypeStruct((8, w_ada.shape[1]), F32), jax.ShapeDtypeStruct((1, D), F32)],
        compiler_params=pltpu.CompilerParams(vmem_limit_bytes=VMEM_LIMIT),
    )(c_all, w_ada, hgrn_lb)


SMALL_SEGS = (("mod", 6 * D), ("lb", D), ("gnorm", LANES), ("conv_w", 4 * CONV_DIM), ("conv_b", CONV_DIM),
              ("dt_bias", LANES), ("a_log", B_INNER), ("d", B_INNER), ("ssm_norm", B_INNER),
              ("ln1_g", D), ("ln1_b", D), ("ln2_g", D), ("ln2_b", D), ("loss", LANES))
SMALL_PARAMS = ("b_ada", "hgrn_lb", "hgrn_gnorm", "ssm_conv_b", "ssm_dt_bias", "ssm_a_log", "ssm_d", "ssm_norm",
                "ln1_g", "ln1_b", "ln2_g", "ln2_b")


def finalize_small(g_all, c_all, dmod_cols, params, m, v):
    n_p = len(SMALL_PARAMS)
    offs, o = {}, 0
    for nm, width in SMALL_SEGS:
        offs[nm] = (o, width)
        o += width

    def body(*refs):
        g_ref, c_ref, dm_ref = refs[:3]
        p_refs = refs[3:3 + n_p]
        m_refs = refs[3 + n_p:3 + 2 * n_p]
        v_refs = refs[3 + 2 * n_p:3 + 3 * n_p]
        outs = refs[3 + 3 * n_p:]
        gwa_ref, gcw_ref, loss_ref = outs[:3]
        res = outs[3:]
        total = jnp.sum(g_ref[...], axis=0, keepdims=True)
        seg = lambda nm: total[:, offs[nm][0]:offs[nm][0] + offs[nm][1]]
        loss_ref[...] = seg("loss")
        gwa_ref[...] = hdot(silu(c_ref[...]), dm_ref[...], "tn")
        cw = seg("conv_w")
        for j in range(4):
            gcw_ref[j:j + 1, :] = cw[:, j * CONV_DIM:(j + 1) * CONV_DIM]
        hc = lax.broadcasted_iota(jnp.int32, (B_INNER, LANES), 0)
        hj = lax.broadcasted_iota(jnp.int32, (B_INNER, LANES), 1)
        per_head = ((hc >> 6) == hj).astype(F32)
        heads = lambda nm: hdot(jnp.broadcast_to(seg(nm), (8, B_INNER)), per_head)[0:1, 0:32]
        lbp = sigmoid(p_refs[1][0:1, :] - p_refs[1][1:2, :])
        g_row = seg("lb") * lbp * (1.0 - lbp)
        grads = {"b_ada": seg("mod"), "hgrn_gnorm": seg("gnorm"), "ssm_conv_b": seg("conv_b"),
                 "ssm_dt_bias": seg("dt_bias")[:, 0:32], "ssm_a_log": heads("a_log"), "ssm_d": heads("d"),
                 "ssm_norm": seg("ssm_norm"), "ln1_g": seg("ln1_g"), "ln1_b": seg("ln1_b"),
                 "ln2_g": seg("ln2_g"), "ln2_b": seg("ln2_b")}
        for i, nm in enumerate(SMALL_PARAMS):
            g_out, d_out, m_out, v_out = res[4 * i:4 * i + 4]
            if nm == "hgrn_lb":
                for row, gv in ((0, g_row), (1, -g_row)):
                    sl = slice(row, row + 1)
                    dl, mn, vn = adamw(p_refs[i][sl, :], gv, m_refs[i][sl, :], v_refs[i][sl, :])
                    g_out[sl, :], d_out[sl, :], m_out[sl, :], v_out[sl, :] = gv, dl, mn, vn
            else:
                gv = grads[nm]
                dl, mn, vn = adamw(p_refs[i][...], gv, m_refs[i][...], v_refs[i][...])
                g_out[...], d_out[...], m_out[...], v_out[...] = gv, dl, mn, vn

    out_shape = [jax.ShapeDtypeStruct((D, dmod_cols.shape[1]), F32), jax.ShapeDtypeStruct((4, CONV_DIM), F32),
                 jax.ShapeDtypeStruct((1, LANES), F32)]
    for p in params:
        out_shape += [jax.ShapeDtypeStruct(p.shape, F32)] * 4
    return pl.pallas_call(
        body, name="finalize_small", out_shape=out_shape,
        compiler_params=pltpu.CompilerParams(vmem_limit_bytes=VMEM_LIMIT),
    )(g_all, c_all, dmod_cols, *params, *m, *v)


def adam_update(w, g, m, v, name, after=None):
    rows, cols = w.shape
    tm, tc = _tile2(rows, cols)
    order = [] if after is None else [after]

    def body(w_ref, g_ref, m_ref, v_ref, *rest):
        d_ref, mo_ref, vo_ref = rest[len(order):]
        d_ref[...], mo_ref[...], vo_ref[...] = adamw(w_ref[...], g_ref[...], m_ref[...], v_ref[...])

    spec = pl.BlockSpec((tm, tc), lambda i, j: (i, j))
    return pl.pallas_call(
        body, name=name, grid=(rows // tm, cols // tc),
        in_specs=[spec] * 4 + [pl.BlockSpec(memory_space=pl.ANY) for _ in order], out_specs=[spec] * 3,
        out_shape=[jax.ShapeDtypeStruct((rows, cols), F32)] * 3,
        compiler_params=_params(("arbitrary", "arbitrary")),
    )(w, g, m, v, *order)


def kernel(x, c, w_ada, b_ada, w_in, hgrn_lb, hgrn_gnorm, ssm_conv_w, ssm_conv_b, ssm_dt_bias, ssm_a_log, ssm_d, ssm_norm, w_branch_a, w_branch_b, w_o, ln1_g, ln1_b, w_ffn_gate, w_ffn_up, w_ffn_down, ln2_g, ln2_b, loss_target, m_w_ada, m_b_ada, m_w_in, m_hgrn_lb, m_hgrn_gnorm, m_ssm_conv_w, m_ssm_conv_b, m_ssm_dt_bias, m_ssm_a_log, m_ssm_d, m_ssm_norm, m_w_branch_a, m_w_branch_b, m_w_o, m_ln1_g, m_ln1_b, m_w_ffn_gate, m_w_ffn_up, m_w_ffn_down, m_ln2_g, m_ln2_b, v_w_ada, v_b_ada, v_w_in, v_hgrn_lb, v_hgrn_gnorm, v_ssm_conv_w, v_ssm_conv_b, v_ssm_dt_bias, v_ssm_a_log, v_ssm_d, v_ssm_norm, v_w_branch_a, v_w_branch_b, v_w_o, v_ln1_g, v_ln1_b, v_w_ffn_gate, v_w_ffn_up, v_w_ffn_down, v_ln2_g, v_ln2_b):
    given = dict(locals())
    chip = 2 * lax.axis_index("x") + lax.axis_index("y")
    core = lax.axis_index("c")
    t = x.shape[1]

    first = gather_rows(jnp.concatenate([c, ssm_conv_w.reshape(1, CONV_DIM)], axis=1), "gather_cond").reshape(8, D + CONV_DIM)
    c_all = first[:, :D]
    conv_w = first[0::2, D:].reshape(4, 4, CONV_DIM // 4).transpose(1, 0, 2).reshape(4, CONV_DIM)
    mod_part, lb_row = ada_prepare(c_all, w_ada[0], hgrn_lb)
    mod_cols = w_ada.shape[2]
    mod_row = exchange_rows(mod_part.reshape(8, 1, mod_cols), "exchange_mod").reshape(1, 6 * D) + b_ada

    local = {nm: given[nm][0] for nm in SHARDED if nm != "w_ffn_in"}
    local["w_ffn_in"] = jnp.concatenate([w_ffn_gate[0].T, w_ffn_up[0].T], axis=0)
    shards = [local[nm].astype(BF16) for nm in SHARDED]
    send_in, recv_in, sent_in, land_in, started_in = gather_start(shards[:1], mod_row, "in")
    shards = shards[:1] + [(local[nm] + started_in[0, 0]).astype(BF16) for nm in SHARDED[1:]]
    send_rest, recv_rest, *flying = gather_start(shards[1:], started_in, "rest")
    n_rest = len(SHARDED) - 1
    sent_rest, land_rest, started_rest = flying[:n_rest], flying[n_rest:2 * n_rest], flying[-1]
    mod_row = mod_row + started_rest[0:1, 0:1]
    mod = tuple(mod_row[:, i * D:(i + 1) * D] for i in range(6))
    with_own = lambda land, shard: lax.dynamic_update_slice(land, shard[None], (chip, 0, 0))

    class Weights:
        def input_projection(self, after):
            (own,), land = gather_wait(send_in, recv_in, [sent_in], [land_in], after, "in")
            (land,) = forward_wait(forward_start(land, "in"), after, "in")
            return assemble_in_proj(land, own, chip)

        def start_rest(self, after):
            self.own, landed = gather_wait(send_rest, recv_rest, sent_rest, land_rest, after, "rest")
            self.started = forward_start(landed, "rest")
            return self.started[-1]

        def rest(self, after):
            got = {nm: with_own(land, s) for nm, land, s in zip(SHARDED[1:], forward_wait(self.started, after, "rest"), self.own, strict=True)}
            whole = lambda nm: got[nm].reshape(4 * got[nm].shape[1], got[nm].shape[2])
            return tuple(whole(nm) for nm in SHARDED[1:])

    wts = Weights()

    per_head = lambda p: jnp.pad(p, ((0, 0), (0, LANES - p.shape[1])))
    per_channel = lambda p: jnp.repeat(p[0], B_INNER // 32)[None]
    small = (lb_row, hgrn_gnorm, conv_w, ssm_conv_b, per_head(ssm_dt_bias), per_channel(ssm_a_log),
             per_channel(ssm_d), ssm_norm, ln1_g, ln1_b, ln2_g, ln2_b)
    by_rows = lambda g: g.reshape(4, g.shape[0] // 4, g.shape[1])
    travelling = {}

    def start_early(dws):
        travelling["pair"] = pair_start([by_rows(dw) for dw in dws], "early")
        return travelling["pair"][-1]

    def between_scans(after):
        slabs, received = pair_wait(travelling["pair"], after, "early")
        travelling["pairs"] = [pair_add(s, r, core, "pair_add_" + nm) for nm, s, r in zip(SHARDED[1:], slabs, received, strict=True)]
        travelling["started"] = scatter_start(travelling["pairs"], "early")
        return travelling["started"][-1]

    def finish_early(after):
        travelling["pairs"], travelling["landed"] = scatter_wait(travelling["started"], after, "early")

    def start_last(u1, dproj):
        wide = 2 * IN_PAD // 3
        first = matmul(u1, dproj, "tn", F32, "in_proj_dw_first", b_cols=(0, wide))
        sending = rows_start(first, "last")
        second = matmul(u1, dproj, "tn", F32, "in_proj_dw_second", after=sending[-1], b_cols=(wide, IN_PAD - wide))
        first, got_first = rows_wait(sending, second, "last")
        got_second = rows_exchange(second, "pair_exchange_last")
        travelling["pairs_in"] = [split_pair_add([first, second], [got_first, got_second], core)]
        travelling["started_in"] = scatter_start(travelling["pairs_in"], "last")
        return travelling["started_in"][-1]

    loss, grad_x, d_mod, d_wts, d_small = local_step(x[0], loss_target[0], mod, wts, small,
                                                     start_early, between_scans, finish_early, start_last)

    d_lb, d_gn, d_cw, d_cb, d_dtb, d_alog, d_dsk, d_nw, d_l1g, d_l1b, d_l2g, d_l2b = d_small
    row = jnp.concatenate(list(d_mod) + [d_lb, d_gn, d_cw.reshape(1, 4 * CONV_DIM), d_cb, d_dtb, d_alog, d_dsk, d_nw,
                                          d_l1g, d_l1b, d_l2g, d_l2b, jnp.pad(loss, ((0, 0), (0, LANES - 1)))], axis=1)
    g_all = gather_rows(row, "gather_small_grads").reshape(8, row.shape[1])
    dmod_cols = lax.dynamic_slice_in_dim(g_all, chip * mod_cols, mod_cols, axis=1)
    fin = finalize_small(g_all, c_all, dmod_cols, [given[n] for n in SMALL_PARAMS],
                         [given["m_" + n] for n in SMALL_PARAMS], [given["v_" + n] for n in SMALL_PARAMS])
    grads, deltas, new_m, new_v = {}, {}, {}, {}
    grads["w_ada"] = fin[0][None]
    grads["ssm_conv_w"] = lax.dynamic_slice_in_dim(fin[1], chip * (CONV_DIM // 4), CONV_DIM // 4, axis=1)[None]
    for i, nm in enumerate(SMALL_PARAMS):
        grads[nm], deltas[nm], new_m[nm], new_v[nm] = fin[3 + 4 * i:7 + 4 * i]

    pairs_in, landed_in = scatter_wait(travelling["started_in"], fin[3], "last")
    pairs, landed = pairs_in + travelling["pairs"], landed_in + travelling["landed"]
    halves = [sum_chips(r, p, chip, core, "sum_chips_" + nm) for nm, r, p in zip(SHARDED, landed, pairs, strict=True)]
    exchanging = halves_start(halves)
    reduced = {"w_ada": grads["w_ada"][0], "ssm_conv_w": grads["ssm_conv_w"][0]}

    def update(nm, after=None):
        flipped = nm in ("w_in", "w_ffn_gate", "w_ffn_up")
        work = (lambda a: a[0].T) if flipped else (lambda a: a[0])
        back = (lambda a: a.T[None]) if flipped else (lambda a: a[None])
        d_, m_, v_ = adam_update(work(given[nm]), reduced[nm], work(given["m_" + nm]), work(given["v_" + nm]),
                                 "adam_" + nm, after)
        grads[nm], deltas[nm], new_m[nm], new_v[nm] = back(reduced[nm]), back(d_), back(m_), back(v_)

    update("w_ada", exchanging[-1])
    update("ssm_conv_w", exchanging[-1])
    reduced.update(zip(SHARDED, halves_wait(exchanging, new_m["w_ada"]), strict=True))
    reduced["w_in"] = reduced["w_in"].T
    reduced["w_ffn_gate"], reduced["w_ffn_up"] = reduced["w_ffn_in"][:FFN_SHARD], reduced["w_ffn_in"][FFN_SHARD:]
    for nm in ("w_in", "w_branch_a", "w_branch_b", "w_o", "w_ffn_gate", "w_ffn_up", "w_ffn_down"):
        update(nm)

    names = ("w_ada", "b_ada", "w_in", "hgrn_lb", "hgrn_gnorm", "ssm_conv_w", "ssm_conv_b", "ssm_dt_bias", "ssm_a_log",
             "ssm_d", "ssm_norm", "w_branch_a", "w_branch_b", "w_o", "ln1_g", "ln1_b", "w_ffn_gate", "w_ffn_up",
             "w_ffn_down", "ln2_g", "ln2_b")
    return (fin[2][0, 0], grad_x[None], *[grads[n] for n in names], *[deltas[n] for n in names],
            *[new_m[n] for n in names], *[new_v[n] for n in names])
```

```python
import functools

import jax
import jax.numpy as jnp
from jax import lax
from jax.experimental import pallas as pl
from jax.experimental.pallas import tpu as pltpu

F32, BF16 = jnp.float32, jnp.bfloat16
HI = lax.Precision.HIGHEST
MESH = pl.DeviceIdType.MESH

D = 1024
CHUNK = 64
LANES = 128
N_HEADS_A = 8
N_GROUPS_B = 4
B_INNER = 2048
CONV_DIM = 3072
D_FF = 2816
ALPHA = 2.0 ** 0.25
LN_EPS = 1e-5
RMS_EPS = 1e-6
ADAM_LR, ADAM_B1, ADAM_B2, ADAM_EPS, ADAM_WD, ADAM_STEP = 0.001, 0.9, 0.999, 1e-08, 0.01, 10

IN_ORIG = 11296
IN_PAD = 11520
COL_GA, COL_GB, COL_XBC, COL_Z, COL_DT = 4096, 5120, 6144, 9216, 11264
ORIG_Z, ORIG_XBC, ORIG_DT, ORIG_GA = 4096, 6144, 9216, 9248

SHARDED = ("w_in", "w_branch_a", "w_branch_b", "w_o", "w_ffn_in", "w_ffn_down")
FFN_SHARD = D_FF // 4
VMEM_LIMIT = 56 * 1024 * 1024
BLOCK_BYTES = 2 * 1024 * 1024
_DIMS = {"nn": (((1,), (0,)), ((), ())), "nt": (((1,), (1,)), ((), ())), "tn": (((0,), (0,)), ((), ()))}


def _bd(a, b, mode):
    return lax.dot_general(a.astype(BF16), b.astype(BF16), _DIMS[mode], preferred_element_type=F32)


@functools.partial(jax.custom_vjp, nondiff_argnums=(2,))
def bdot(a, b, mode):
    return _bd(a, b, mode)


def _bdot_fwd(a, b, mode):
    return _bd(a, b, mode), (a, b)


def _bdot_bwd(mode, res, g):
    a, b = res
    if mode == "nn":
        return _bd(g, b, "nt"), _bd(a, g, "tn")
    if mode == "nt":
        return _bd(g, b, "nn"), _bd(g, a, "tn")
    return _bd(b, g, "nt"), _bd(a, g, "nn")


bdot.defvjp(_bdot_fwd, _bdot_bwd)


def hdot(a, b, mode="nn"):
    return lax.dot_general(a, b, _DIMS[mode], precision=HI, preferred_element_type=F32)


def _raw(a, b, mode):
    return lax.dot_general(a, b, _DIMS[mode], preferred_element_type=F32)


def _split(x, n):
    parts, rest = [], x
    for _ in range(n):
        p = rest.astype(BF16)
        parts.append(p)
        rest = rest - p.astype(F32)
    return parts


def _od(a, b, mode, exact):
    if exact == 1:
        e = b.astype(BF16)
        p = _split(a, 3)
        return (_raw(p[2], e, mode) + _raw(p[1], e, mode)) + _raw(p[0], e, mode)
    e = a.astype(BF16)
    p = _split(b, 3)
    return (_raw(e, p[2], mode) + _raw(e, p[1], mode)) + _raw(e, p[0], mode)


@functools.partial(jax.custom_vjp, nondiff_argnums=(2, 3))
def odot(a, b, mode, exact):
    return _od(a, b, mode, exact)


def _odot_fwd(a, b, mode, exact):
    return _od(a, b, mode, exact), (a, b)


def _odot_bwd(mode, exact, res, g):
    a, b = res
    if exact == 1:
        da = {"nn": lambda: _od(g, b, "nt", 1), "nt": lambda: _od(g, b, "nn", 1), "tn": lambda: _od(b, g, "nt", 0)}[mode]()
        return da, jnp.zeros_like(b)
    db = {"nn": lambda: _od(a, g, "tn", 0), "nt": lambda: _od(g, a, "tn", 1), "tn": lambda: _od(a, g, "nn", 0)}[mode]()
    return jnp.zeros_like(a), db


odot.defvjp(_odot_fwd, _odot_bwd)


_BDIMS = {"bnn": (((2,), (1,)), ((0,), (0,))), "bnt": (((2,), (2,)), ((0,), (0,))), "btn": (((1,), (1,)), ((0,), (0,)))}


def _braw(a, b, mode):
    return lax.dot_general(a, b, _BDIMS[mode], preferred_element_type=F32)


def _bdb(a, b, mode):
    return _braw(a.astype(BF16), b.astype(BF16), mode)


def _d3b(a, b, mode):
    ah, al = _split(a, 2)
    bh, bl = _split(b, 2)
    return _braw(ah, bh, mode) + (_braw(ah, bl, mode) + _braw(al, bh, mode))


def _batched_bwd(f):
    def bwd(mode, res, g):
        a, b = res
        if mode == "bnn":
            return f(g, b, "bnt"), f(a, g, "btn")
        if mode == "bnt":
            return f(g, b, "bnn"), f(g, a, "btn")
        return f(b, g, "bnt"), f(a, g, "bnn")
    return bwd


@functools.partial(jax.custom_vjp, nondiff_argnums=(2,))
def bdot_b(a, b, mode):
    return _bdb(a, b, mode)


bdot_b.defvjp(lambda a, b, mode: (_bdb(a, b, mode), (a, b)), _batched_bwd(_bdb))


@functools.partial(jax.custom_vjp, nondiff_argnums=(2,))
def dot3_b(a, b, mode):
    return _d3b(a, b, mode)


dot3_b.defvjp(lambda a, b, mode: (_d3b(a, b, mode), (a, b)), _batched_bwd(_d3b))


def _cum(tril3, x, mode):
    e = tril3.astype(BF16)
    p = _split(x, 3)
    return (_braw(e, p[2], mode) + _braw(e, p[1], mode)) + _braw(e, p[0], mode)


@jax.custom_vjp
def chunk_cumsum(tril3, x):
    return _cum(tril3, x, "bnn")


chunk_cumsum.defvjp(lambda t, x: (_cum(t, x, "bnn"), t), lambda t, g: (jnp.zeros_like(t), _cum(t, g, "btn")))


def _unstack(axis, n):
    @jax.custom_vjp
    def un(x):
        return tuple(lax.index_in_dim(x, i, axis, keepdims=False) for i in range(n))

    un.defvjp(lambda x: (un(x), None), lambda _, g: (jnp.stack(g, axis=axis),))
    return un


def _split_last(n, w):
    @jax.custom_vjp
    def sp(x):
        return tuple(x[..., i * w:(i + 1) * w] for i in range(n))

    sp.defvjp(lambda x: (sp(x), None), lambda _, g: (jnp.concatenate(g, axis=-1),))
    return sp


def sigmoid(x):
    return 0.5 * jnp.tanh(0.5 * x) + 0.5


def silu(x):
    return x * sigmoid(x)


def softplus(x):
    return jnp.maximum(x, 0.0) + jnp.log1p(jnp.exp(jnp.minimum(x, -x)))


def _ln(x):
    mu = jnp.mean(x, axis=-1, keepdims=True)
    xc = x - mu
    return xc * lax.rsqrt(jnp.mean(xc * xc, axis=-1, keepdims=True) + LN_EPS)


def _tril64():
    r = lax.broadcasted_iota(jnp.int32, (CHUNK, CHUNK), 0)
    c = lax.broadcasted_iota(jnp.int32, (CHUNK, CHUNK), 1)
    return (r >= c).astype(F32)


def hgrn_block(q, fl, iv, gr, st, lb, gn):
    tb = q.shape[0]
    nc = tb // CHUNK
    nh = N_HEADS_A
    heads = _split_last(nh, LANES)
    to4 = lambda a: jnp.stack(heads(a), axis=0).reshape(nh, nc, CHUNK, LANES)
    flat = lambda a: a.reshape(nh * nc, CHUNK, LANES)
    f = lb + (1.0 - lb) * sigmoid(fl)
    gl4, k4, qf4, v4, gr4 = to4(jnp.log(f)), to4(1.0 - f), to4(silu(q) * (128 ** -0.5)), to4(iv), to4(gr)
    tril = _tril64()
    b4 = chunk_cumsum(jnp.broadcast_to(tril[None], (nh * nc, CHUNK, CHUNK)), flat(gl4)).reshape(gl4.shape)
    blast = jnp.sum(gl4, axis=2, keepdims=True)
    ref = lax.stop_gradient(0.5 * blast)
    qp, kp = qf4 * jnp.exp(b4 - ref), k4 * jnp.exp(ref - b4)
    sc = dot3_b(flat(qp), flat(kp), "bnt") * tril
    o_intra = bdot_b(sc, flat(v4), "bnn").reshape(gl4.shape)
    chunks = _unstack(1, nc)
    qe, v_c, kd, dec = chunks(qp * jnp.exp(ref)), chunks(v4), chunks(kp * jnp.exp(blast - ref)), chunks(jnp.exp(blast))
    o_inter = []
    for c in range(nc):
        o_inter.append(bdot_b(qe[c], st, "bnt"))
        st = st * dec[c] + bdot_b(v_c[c], kd[c], "btn")
    o = o_intra + jnp.stack(o_inter, axis=1)
    on = o * lax.rsqrt(jnp.mean(o * o, axis=-1, keepdims=True) + RMS_EPS) * gn
    out = (on * silu(gr4)).reshape(nh, tb, LANES)
    return jnp.concatenate(_unstack(0, nh)(out), axis=1), st


def ssd_consts(g):
    i32 = jnp.int32
    ej = lax.broadcasted_iota(i32, (LANES, 512), 0)
    ec = lax.broadcasted_iota(i32, (LANES, 512), 1)
    expand = (ej == g * 8 + (ec >> 6)).astype(F32)
    ts = lax.broadcasted_iota(i32, (CHUNK, 512), 0)
    tc = lax.broadcasted_iota(i32, (CHUNK, 512), 1)
    itile = (ts == (tc & 63)).astype(F32)
    maskall = ts >= (tc & 63)
    br = lax.broadcasted_iota(i32, (LANES, LANES), 0)
    bc = lax.broadcasted_iota(i32, (LANES, LANES), 1)
    blockmask = ((br >> 6) == (bc >> 6)).astype(F32)
    return expand, itile, maskall, blockmask, _tril64()


def ssd_block(x, bm, cm, dt, z, st, dtb, alog, dsk, nw, cs):
    expand, itile, maskall, blockmask, tril = cs
    tb = x.shape[0]
    nc = tb // CHUNK
    delta = odot(softplus(dt + dtb), expand, "nn", 1)
    a = -jnp.exp(alog) * delta
    xdt = x * delta
    by_chunk = lambda v: v.reshape(nc, CHUNK, v.shape[-1])
    a3, xdt3, bm3, cm3 = by_chunk(a), by_chunk(xdt), by_chunk(bm), by_chunk(cm)
    acum3 = chunk_cumsum(jnp.broadcast_to(tril[None], (nc, CHUNK, CHUNK)), a3)
    alast3 = jnp.sum(a3, axis=1, keepdims=True)
    cb3 = bdot_b(cm3, jnp.concatenate([bm3] * 8, axis=1), "bnt")
    arow3 = jnp.sum(acum3 * itile, axis=1, keepdims=True)
    dec3 = jnp.exp(jnp.where(maskall, acum3 - arow3, -1e30))
    pairs = _split_last(4, LANES)
    intra = [bdot_b(m, jnp.concatenate([xp] * 2, axis=1) * blockmask, "bnn")
             for m, xp in zip(pairs(cb3 * dec3), pairs(xdt3))]
    chunks = _unstack(0, nc)
    cm_c, bm_c, xw_c, dec_c = chunks(cm3), chunks(bm3), chunks(xdt3 * jnp.exp(alast3 - acum3)), chunks(jnp.exp(alast3))
    inter = []
    for c in range(nc):
        inter.append(bdot(cm_c[c], st, "nn"))
        st = st * dec_c[c] + bdot(bm_c[c], xw_c[c], "tn")
    st_new = st
    y = (jnp.concatenate(intra, axis=-1) + jnp.stack(inter, axis=0) * jnp.exp(acum3)).reshape(tb, 512)
    yz = (y + x * dsk) * silu(z)
    return yz * lax.rsqrt(jnp.mean(yz * yz, axis=-1, keepdims=True) + RMS_EPS) * nw, st_new


def adamw(w, g, m, v):
    m = ADAM_B1 * m + (1.0 - ADAM_B1) * g
    v = ADAM_B2 * v + (1.0 - ADAM_B2) * jnp.square(g)
    m_hat = m / (1.0 - ADAM_B1 ** ADAM_STEP)
    v_hat = v / (1.0 - ADAM_B2 ** ADAM_STEP)
    return -ADAM_LR * (m_hat / (jnp.sqrt(v_hat) + ADAM_EPS) + ADAM_WD * w), m, v


def _pick(n, cands):
    for c in cands:
        if n % c == 0:
            return c
    return n


def _params(sem):
    return pltpu.CompilerParams(dimension_semantics=sem, vmem_limit_bytes=VMEM_LIMIT)


MATMUL_VMEM_BUDGET = 50 * 1024 * 1024
MATMUL_MIN_STEPS = 4
MXU_WIDTH = 256


def matmul(a, b, mode, out_dtype, name, after=None, b_cols=None):
    if mode == "nn":
        (m, k), n = a.shape, b.shape[1]
    elif mode == "nt":
        (m, k), n = a.shape, b.shape[0]
    else:
        (k, m), n = a.shape, b.shape[1]
    first_col, n = (0, n) if b_cols is None else b_cols
    a_bytes, b_bytes, out_bytes = a.dtype.itemsize, b.dtype.itemsize, jnp.dtype(out_dtype).itemsize
    k_sizes = (2304, 2048, 1408, 1024, 768, 512, 256, 128)
    usual_tk = _pick(k, k_sizes)

    def vmem(tm_, tn_, tk_):
        blocks = 2 * (tm_ * tk_ * a_bytes + tk_ * tn_ * b_bytes + tm_ * tn_ * out_bytes)
        return blocks + (tm_ * tn_ * 4 if tk_ < k else 0)

    def traffic(tm_, tn_, tk_):
        return (m // tm_) * k * n * b_bytes + (n // tn_ if tk_ < k else 1) * m * k * a_bytes

    sizes = (2304, 2048, 1920, 1408, 1024, 768, 512, 256, 128)
    tiles = [(tm_, tn_, tk_) for tm_ in sizes if m % tm_ == 0 for tn_ in sizes if n % tn_ == 0
             for tk_ in {k, usual_tk} if vmem(tm_, tn_, tk_) <= MATMUL_VMEM_BUDGET] or [(m, n, k)]
    pipelined = [t for t in tiles if (m // t[0]) * (n // t[1]) * (k // t[2]) >= MATMUL_MIN_STEPS]
    tm, tn, tk = min(pipelined or tiles, key=lambda t: (traffic(*t), t[2] != usual_tk, t[1] % MXU_WIDTH != 0, -t[0] * t[1]))
    nk = k // tk
    a_spec = pl.BlockSpec((tk, tm), lambda i, j, kk: (kk, i)) if mode == "tn" else pl.BlockSpec((tm, tk), lambda i, j, kk: (i, kk))
    assert first_col % tn == 0 and (mode != "nt" or b_cols is None)
    skip = first_col // tn
    b_spec = pl.BlockSpec((tn, tk), lambda i, j, kk: (j, kk)) if mode == "nt" else pl.BlockSpec((tk, tn), lambda i, j, kk: (kk, j + skip))

    order = [] if after is None else [after]

    def body(a_ref, b_ref, *rest):
        o_ref, *acc = rest[len(order):]
        part = _bd(a_ref[...], b_ref[...], mode)
        if nk == 1:
            o_ref[...] = part.astype(o_ref.dtype)
            return
        acc_ref, = acc
        kk = pl.program_id(2)

        @pl.when(kk == 0)
        def _():
            acc_ref[...] = part

        @pl.when(jnp.logical_and(kk > 0, kk < nk - 1))
        def _():
            acc_ref[...] += part

        @pl.when(kk == nk - 1)
        def _():
            o_ref[...] = (acc_ref[...] + part).astype(o_ref.dtype)

    return pl.pallas_call(
        body, name=name, grid=(m // tm, n // tn, nk),
        in_specs=[a_spec, b_spec] + [pl.BlockSpec(memory_space=pl.ANY) for _ in order],
        out_specs=pl.BlockSpec((tm, tn), lambda i, j, kk: (i, j)),
        out_shape=jax.ShapeDtypeStruct((m, n), out_dtype),
        scratch_shapes=[pltpu.VMEM((tm, tn), F32)] if nk > 1 else [],
        compiler_params=_params(("parallel", "parallel", "arbitrary")),
    )(a, b, *order)


def rowwise(name, fn, rows, consts, out_rows, out_accs=(), tm_max=512, into=None, new_wide=None):
    t = rows[0][0].shape[0]
    tm = _pick(t, (tm_max, 128, 64, 32, 16, 8))
    n_r, n_c, n_o = len(rows), len(consts), len(out_rows)
    n_alias = 0 if into is None else 1

    def body(*refs):
        r_in = [r[...] for r in refs[:n_r]]
        c_in = [r[...] for r in refs[n_r:n_r + n_c]]
        refs = refs[:n_r + n_c] + refs[n_r + n_c + n_alias:]
        o_refs = refs[n_r + n_c:n_r + n_c + n_o]
        a_refs = refs[n_r + n_c + n_o:]
        ro, ao = fn(r_in, c_in)
        for ref, val in zip(o_refs, ro, strict=True):
            ref[...] = val.astype(ref.dtype)
        if a_refs:
            @pl.when(pl.program_id(0) == 0)
            def _():
                for ref in a_refs:
                    ref[...] = jnp.zeros_like(ref)

            for ref, val in zip(a_refs, ao, strict=True):
                ref[...] += val

    in_specs = [pl.BlockSpec((tm, w), functools.partial(lambda i, cb: (i, cb), cb=cb)) for _, w, cb in rows]
    in_specs += [pl.BlockSpec(c.shape, lambda i: (0, 0)) for c in consts]
    out_specs = [pl.BlockSpec((tm, w), lambda i: (i, 0)) for w, _ in out_rows]
    out_specs += [pl.BlockSpec(s, lambda i: (0, 0)) for s in out_accs]
    out_shape = [jax.ShapeDtypeStruct((t, w), dt) for w, dt in out_rows]
    out_shape += [jax.ShapeDtypeStruct(s, F32) for s in out_accs]
    operands = [r[0] for r in rows] + list(consts)
    aliases = {}
    if into is not None:
        target, cb = into
        in_specs.append(pl.BlockSpec(memory_space=pl.ANY))
        operands.append(target)
        out_specs[0] = pl.BlockSpec((tm, out_rows[0][0]), lambda i: (i, cb))
        out_shape[0] = jax.ShapeDtypeStruct(target.shape, target.dtype)
        aliases = {len(operands) - 1: 0}
    if new_wide is not None:
        width, cb = new_wide
        out_specs[0] = pl.BlockSpec((tm, out_rows[0][0]), lambda i: (i, cb))
        out_shape[0] = jax.ShapeDtypeStruct((t, width), out_rows[0][1])
    return pl.pallas_call(
        body, name=name, grid=(t // tm,), in_specs=in_specs, out_specs=out_specs, out_shape=out_shape,
        input_output_aliases=aliases, compiler_params=_params(("arbitrary",)),
    )(*operands)


def _full(a):
    return (a, a.shape[1], 0)


HGRN_TIME_BLOCK = 256
SSD_TIME_BLOCK = 512


def _time_block(t, most=HGRN_TIME_BLOCK):
    return _pick(t, tuple(b for b in (512, 256, 128, 64) if b <= most))


def _quarters(ref):
    return [ref[:, seg * D:(seg + 1) * D] for seg in range(4)]


def hgrn_forward(proj, lb, gn):
    t = proj.shape[0]
    tb = _time_block(t)
    nb = t // tb

    def body(qfig_ref, lb_ref, gn_ref, o_ref, st_ref, state):
        @pl.when(pl.program_id(0) == 0)
        def _():
            state[...] = jnp.zeros_like(state)

        st = state[...]
        st_ref[...] = st
        out, st_new = hgrn_block(*_quarters(qfig_ref), st, lb_ref[...], gn_ref[...])
        o_ref[...] = out.astype(o_ref.dtype)
        state[...] = st_new

    return pl.pallas_call(
        body, name="hgrn_forward", grid=(nb,),
        in_specs=[pl.BlockSpec((tb, 4 * D), lambda j: (j, 0)),
                  pl.BlockSpec((1, D), lambda j: (0, 0)), pl.BlockSpec((1, LANES), lambda j: (0, 0))],
        out_specs=[pl.BlockSpec((tb, D), lambda j: (j, 0)),
                   pl.BlockSpec((None, N_HEADS_A, LANES, LANES), lambda j: (j, 0, 0, 0))],
        out_shape=[jax.ShapeDtypeStruct((t, D), BF16),
                   jax.ShapeDtypeStruct((nb, N_HEADS_A, LANES, LANES), F32)],
        scratch_shapes=[pltpu.VMEM((N_HEADS_A, LANES, LANES), F32)],
        compiler_params=_params(("arbitrary",)),
    )(proj, lb, gn)


def hgrn_backward(proj, states, d_out, lb, gn, d_proj):
    t = proj.shape[0]
    tb = _time_block(t)
    nb = t // tb

    def body(qfig_ref, st_ref, do_ref, lb_ref, gn_ref, _, dqfig_ref, dlb_ref, dgn_ref, d_state):
        @pl.when(pl.program_id(0) == 0)
        def _():
            d_state[...] = jnp.zeros_like(d_state)
            dlb_ref[...] = jnp.zeros_like(dlb_ref)
            dgn_ref[...] = jnp.zeros_like(dgn_ref)

        _, vjp = jax.vjp(hgrn_block, *_quarters(qfig_ref), st_ref[...], lb_ref[...], gn_ref[...])
        dq, df, di, dg, dst, dlb, dgn = vjp((do_ref[...], d_state[...]))
        for seg, val in enumerate((dq, df, di, dg)):
            dqfig_ref[:, seg * D:(seg + 1) * D] = val.astype(dqfig_ref.dtype)
        d_state[...] = dst
        dlb_ref[...] += dlb
        dgn_ref[...] += dgn

    rev = lambda j: nb - 1 - j
    return pl.pallas_call(
        body, name="hgrn_backward", grid=(nb,),
        in_specs=[pl.BlockSpec((tb, 4 * D), lambda j: (rev(j), 0)),
                  pl.BlockSpec((None, N_HEADS_A, LANES, LANES), lambda j: (rev(j), 0, 0, 0)),
                  pl.BlockSpec((tb, D), lambda j: (rev(j), 0)),
                  pl.BlockSpec((1, D), lambda j: (0, 0)), pl.BlockSpec((1, LANES), lambda j: (0, 0)),
                  pl.BlockSpec(memory_space=pl.ANY)],
        out_specs=[pl.BlockSpec((tb, 4 * D), lambda j: (rev(j), 0)),
                   pl.BlockSpec((1, D), lambda j: (0, 0)), pl.BlockSpec((1, LANES), lambda j: (0, 0))],
        out_shape=[jax.ShapeDtypeStruct(d_proj.shape, d_proj.dtype), jax.ShapeDtypeStruct((1, D), F32),
                   jax.ShapeDtypeStruct((1, LANES), F32)],
        input_output_aliases={5: 0},
        scratch_shapes=[pltpu.VMEM((N_HEADS_A, LANES, LANES), F32)],
        compiler_params=_params(("arbitrary",)),
    )(proj, states, d_out, lb, gn, d_proj)


def _ssd_in_specs(tb, tmap):
    return [pl.BlockSpec((tb, 512), lambda g, j: (tmap(j), g)),
            pl.BlockSpec((tb, LANES), lambda g, j: (tmap(j), 16 + g)),
            pl.BlockSpec((tb, LANES), lambda g, j: (tmap(j), 20 + g)),
            pl.BlockSpec((tb, LANES), lambda g, j: (tmap(j), COL_DT // LANES)),
            pl.BlockSpec((tb, 512), lambda g, j: (tmap(j), COL_Z // 512 + g))]


def ssd_forward(xc, proj, dtb, alog, dsk, nw):
    t = proj.shape[0]
    tb = _time_block(t, SSD_TIME_BLOCK)
    nb = t // tb

    def body(x_ref, b_ref, c_ref, dt_ref, z_ref, dtb_ref, alog_ref, dsk_ref, nw_ref, o_ref, st_ref, state):
        @pl.when(pl.program_id(1) == 0)
        def _():
            state[...] = jnp.zeros_like(state)

        st = state[...]
        st_ref[...] = st
        out, st_new = ssd_block(x_ref[...], b_ref[...], c_ref[...], dt_ref[...], z_ref[...], st,
                                dtb_ref[...], alog_ref[...], dsk_ref[...], nw_ref[...], ssd_consts(pl.program_id(0)))
        o_ref[...] = out.astype(o_ref.dtype)
        state[...] = st_new

    vec = pl.BlockSpec((1, 512), lambda g, j: (0, g))
    heads = pl.BlockSpec((1, LANES), lambda g, j: (0, 0))
    return pl.pallas_call(
        body, name="ssd_forward", grid=(N_GROUPS_B, nb),
        in_specs=_ssd_in_specs(tb, lambda j: j) + [heads, vec, vec, vec],
        out_specs=[pl.BlockSpec((tb, 512), lambda g, j: (j, g)),
                   pl.BlockSpec((None, None, LANES, 512), lambda g, j: (j, g, 0, 0))],
        out_shape=[jax.ShapeDtypeStruct((t, B_INNER), BF16),
                   jax.ShapeDtypeStruct((nb, N_GROUPS_B, LANES, 512), F32)],
        scratch_shapes=[pltpu.VMEM((LANES, 512), F32)],
        compiler_params=_params(("arbitrary", "arbitrary")),
    )(xc, xc, xc, proj, proj, dtb, alog, dsk, nw)


def ssd_backward(xc, proj, states, d_out, dtb, alog, dsk, nw, d_proj):
    t = proj.shape[0]
    tb = _time_block(t, SSD_TIME_BLOCK)
    nb = t // tb
    rev = lambda j: nb - 1 - j

    def body(x_ref, b_ref, c_ref, dt_ref, z_ref, st_ref, do_ref, dtb_ref, alog_ref, dsk_ref, nw_ref, _,
             dx_ref, db_ref, dc_ref, ddt_ref, dz_ref, ddtb_ref, dalog_ref, ddsk_ref, dnw_ref, d_state):
        accs = (ddtb_ref, dalog_ref, ddsk_ref, dnw_ref)

        @pl.when(pl.program_id(1) == 0)
        def _():
            d_state[...] = jnp.zeros_like(d_state)
            for ref in accs:
                ref[...] = jnp.zeros_like(ref)

        cs = ssd_consts(pl.program_id(0))
        fn = lambda *a: ssd_block(*a, cs)
        _, vjp = jax.vjp(fn, x_ref[...], b_ref[...], c_ref[...], dt_ref[...], z_ref[...], st_ref[...],
                         dtb_ref[...], alog_ref[...], dsk_ref[...], nw_ref[...])
        dx, db, dc, ddt, dz, dst, *dpar = vjp((do_ref[...], d_state[...]))
        dx_ref[...] = dx
        db_ref[...] = db
        dc_ref[...] = dc
        ddt_ref[...] = ddt
        dz_ref[...] = dz.astype(dz_ref.dtype)
        d_state[...] = dst
        for ref, val in zip(accs, dpar, strict=True):
            ref[...] += val

    vec = pl.BlockSpec((1, 512), lambda g, j: (0, g))
    heads = pl.BlockSpec((1, LANES), lambda g, j: (0, 0))
    acc = pl.BlockSpec((None, 1, 512), lambda g, j: (g, 0, 0))
    acc_heads = pl.BlockSpec((None, 1, LANES), lambda g, j: (g, 0, 0))
    return pl.pallas_call(
        body, name="ssd_backward", grid=(N_GROUPS_B, nb),
        in_specs=_ssd_in_specs(tb, rev)
        + [pl.BlockSpec((None, None, LANES, 512), lambda g, j: (rev(j), g, 0, 0)),
           pl.BlockSpec((tb, 512), lambda g, j: (rev(j), g))] + [heads, vec, vec, vec] + [pl.BlockSpec(memory_space=pl.ANY)],
        out_specs=[pl.BlockSpec((tb, 512), lambda g, j: (rev(j), g)),
                   pl.BlockSpec((tb, LANES), lambda g, j: (rev(j), g)),
                   pl.BlockSpec((tb, LANES), lambda g, j: (rev(j), g)),
                   pl.BlockSpec((None, tb, LANES), lambda g, j: (g, rev(j), 0)),
                   pl.BlockSpec((tb, 512), lambda g, j: (rev(j), COL_Z // 512 + g)), acc_heads, acc, acc, acc],
        out_shape=[jax.ShapeDtypeStruct((t, B_INNER), F32), jax.ShapeDtypeStruct((t, 512), F32),
                   jax.ShapeDtypeStruct((t, 512), F32), jax.ShapeDtypeStruct((N_GROUPS_B, t, LANES), F32),
                   jax.ShapeDtypeStruct(d_proj.shape, d_proj.dtype)]
        + [jax.ShapeDtypeStruct((N_GROUPS_B, 1, LANES), F32)] + [jax.ShapeDtypeStruct((N_GROUPS_B, 1, 512), F32)] * 3,
        input_output_aliases={11: 4},
        scratch_shapes=[pltpu.VMEM((LANES, 512), F32)],
        compiler_params=_params(("arbitrary", "arbitrary")),
    )(xc, xc, xc, proj, proj, states, d_out, dtb, alog, dsk, nw, d_proj)


CONV_HALO = 8


def _shift_down(halo_then_tile, s, tm):
    if s == 0:
        return halo_then_tile[CONV_HALO:CONV_HALO + tm]
    return pltpu.roll(halo_then_tile, s, 0)[CONV_HALO:CONV_HALO + tm]


def _conv_pre(cur, prev, w, b, tm):
    stacked = jnp.concatenate([prev, cur], axis=0)
    taps = [_shift_down(stacked, 3 - j, tm) for j in range(4)]
    pre = b + taps[0] * w[0:1] + taps[1] * w[1:2] + taps[2] * w[2:3] + taps[3] * w[3:4]
    return pre, taps


def _conv_specs(t, tm):
    per = tm // CONV_HALO
    cur = pl.BlockSpec((tm, CONV_DIM), lambda i: (i, COL_XBC // CONV_DIM))
    prev = pl.BlockSpec((CONV_HALO, CONV_DIM), lambda i: (jnp.maximum(i * per - 1, 0), COL_XBC // CONV_DIM))
    return cur, prev


def conv_forward(proj, w, b):
    t = proj.shape[0]
    tm = _pick(t, (256, 128, 64))

    def body(cur_ref, prev_ref, w_ref, b_ref, o_ref):
        prev = jnp.where(pl.program_id(0) == 0, 0.0, prev_ref[...])
        pre, _ = _conv_pre(cur_ref[...], prev, w_ref[...], b_ref[...], tm)
        o_ref[...] = silu(pre)

    cur, prev = _conv_specs(t, tm)
    return pl.pallas_call(
        body, name="conv_forward", grid=(t // tm,),
        in_specs=[cur, prev, pl.BlockSpec((4, CONV_DIM), lambda i: (0, 0)), pl.BlockSpec((1, CONV_DIM), lambda i: (0, 0))],
        out_specs=pl.BlockSpec((tm, CONV_DIM), lambda i: (i, 0)),
        out_shape=jax.ShapeDtypeStruct((t, CONV_DIM), F32),
        compiler_params=_params(("arbitrary",)),
    )(proj, proj, w, b)


def conv_backward(proj, dx, db_, dc_, w, b, d_proj):
    t = proj.shape[0]
    tm = _pick(t, (256, 128, 64))
    per = tm // CONV_HALO
    nt = t // tm
    rev = lambda i: nt - 1 - i

    def body(cur_ref, prev_ref, dx_ref, dbm_ref, dcm_ref, w_ref, b_ref, _, o_ref, dw_ref, dbias_ref, later):
        @pl.when(pl.program_id(0) == 0)
        def _():
            dw_ref[...] = jnp.zeros_like(dw_ref)
            dbias_ref[...] = jnp.zeros_like(dbias_ref)
            later[...] = jnp.zeros_like(later)

        first_tile = pl.program_id(0) == nt - 1
        for lo, hi, src in ((0, B_INNER, dx_ref), (B_INNER, B_INNER + 512, dbm_ref), (B_INNER + 512, CONV_DIM, dcm_ref)):
            cols = slice(lo, hi)
            prev = jnp.where(first_tile, 0.0, prev_ref[:, cols])
            w_ = w_ref[:, cols]
            pre, taps = _conv_pre(cur_ref[:, cols], prev, w_, b_ref[:, cols], tm)
            sg = sigmoid(pre)
            dpre = src[...] * (sg * (1.0 + pre * (1.0 - sg)))
            dbias_ref[:, cols] += jnp.sum(dpre, axis=0, keepdims=True)
            for j in range(4):
                dw_ref[j:j + 1, cols] += jnp.sum(dpre * taps[j], axis=0, keepdims=True)
            stacked = jnp.concatenate([dpre, later[:, cols]], axis=0)
            acc = dpre * w_[3:4]
            for j in range(3):
                acc = acc + pltpu.roll(stacked, tm + CONV_HALO - (3 - j), 0)[0:tm] * w_[j:j + 1]
            o_ref[:, cols] = acc.astype(o_ref.dtype)
            later[:, cols] = dpre[0:CONV_HALO]

    row = lambda w_: pl.BlockSpec((tm, w_), lambda i: (rev(i), 0))
    whole = lambda r: pl.BlockSpec((r, CONV_DIM), lambda i: (0, 0))
    return pl.pallas_call(
        body, name="conv_backward", grid=(nt,),
        in_specs=[pl.BlockSpec((tm, CONV_DIM), lambda i: (rev(i), COL_XBC // CONV_DIM)),
                  pl.BlockSpec((CONV_HALO, CONV_DIM), lambda i: (jnp.maximum(rev(i) * per - 1, 0), COL_XBC // CONV_DIM)),
                  row(B_INNER), row(512), row(512), whole(4), whole(1), pl.BlockSpec(memory_space=pl.ANY)],
        out_specs=[pl.BlockSpec((tm, CONV_DIM), lambda i: (rev(i), COL_XBC // CONV_DIM)), whole(4), whole(1)],
        out_shape=[jax.ShapeDtypeStruct(d_proj.shape, d_proj.dtype), jax.ShapeDtypeStruct((4, CONV_DIM), F32),
                   jax.ShapeDtypeStruct((1, CONV_DIM), F32)],
        input_output_aliases={7: 0},
        scratch_shapes=[pltpu.VMEM((CONV_HALO, CONV_DIM), F32)],
        compiler_params=_params(("arbitrary",)),
    )(proj, proj, dx, db_, dc_, w, b, d_proj)


def stage_modulate(x, sc, sh):
    return _ln(x) * (1.0 + sc) + sh


def stage_merge(ga, gb, ya, yb):
    return sigmoid(ga) * ya + sigmoid(gb) * yb


def stage_post_mixer(x, h, g1, ln_g, ln_b, sc2, sh2):
    x1 = _ln(ALPHA * x + g1 * h) * ln_g + ln_b
    return x1, _ln(x1) * (1.0 + sc2) + sh2


def stage_swiglu(a, b):
    return silu(a) * b


def gate_up(ab):
    w = FFN_SHARD
    return (jnp.concatenate([ab[:, 2 * w * k:2 * w * k + w] for k in range(4)], axis=1),
            jnp.concatenate([ab[:, 2 * w * k + w:2 * w * (k + 1)] for k in range(4)], axis=1))


def per_chip(gate, up):
    w = FFN_SHARD
    return jnp.concatenate([part[:, w * k:w * (k + 1)] for k in range(4) for part in (gate, up)], axis=1)


def stage_loss(x1, hf, tgt, g2, ln_g, ln_b):
    x2 = _ln(ALPHA * x1 + g2 * hf) * ln_g + ln_b
    return 0.5 * jnp.sum(jnp.mean(jnp.square(x2 - tgt), axis=-1, keepdims=True), axis=0, keepdims=True)


def local_step(x, tgt, mod, wts, small, early=None, mid=None, late=None, last=None):
    sh1, sc1, g1, sh2, sc2, g2 = mod
    lb, gn, conv_w, conv_b, dtb, alog, dsk, nw, ln1_g, ln1_b, ln2_g, ln2_b = small
    vec = (1, D)

    (u1,) = rowwise("modulate1", lambda r, c: ((stage_modulate(r[0], *c),), ()), [_full(x)], [sc1, sh1], [(D, BF16)])
    w_in = wts.input_projection(u1)
    proj = matmul(u1, w_in, "nn", F32, "in_proj")
    ya_in, st_a = hgrn_forward(proj, lb, gn + wts.start_rest(proj)[0:1])
    xc = conv_forward(proj, conv_w, conv_b)
    w_a, w_b, w_o, w_gu, w_d = wts.rest(xc)
    yb_in, st_b = ssd_forward(xc, proj, dtb, alog, dsk, nw)
    ya = matmul(ya_in, w_a, "nn", F32, "branch_a")
    yb = matmul(yb_in, w_b, "nn", F32, "branch_b")
    gate_rows = [(proj, D, COL_GA // D), (proj, D, COL_GB // D), _full(ya), _full(yb)]
    (merged,) = rowwise("merge", lambda r, c: ((stage_merge(*r),), ()), gate_rows, [], [(D, BF16)])
    h = matmul(merged, w_o, "nn", F32, "out_proj")
    post_consts = [g1, ln1_g, ln1_b, sc2, sh2]
    x1, u2 = rowwise("post_mixer", lambda r, c: (stage_post_mixer(*r, *c), ()), [_full(x), _full(h)], post_consts,
                     [(D, F32), (D, BF16)])
    ab = matmul(u2, w_gu, "nt", F32, "ffn_in")
    (p,) = rowwise("swiglu", lambda r, c: ((stage_swiglu(*gate_up(r[0])),), ()), [_full(ab)], [], [(D_FF, BF16)],
                   tm_max=256)
    hf = matmul(p, w_d, "nn", F32, "ffn_out")

    def loss_bwd(r, c):
        loss, vjp = jax.vjp(stage_loss, *r, *c)
        dx1, dhf, _, dg2, dlg, dlb_ = vjp(jnp.ones((1, 1), F32))
        return (dx1, dhf), (loss, dg2, dlg, dlb_)

    dx1, dhf, loss, dg2, dln2_g, dln2_b = rowwise(
        "loss_backward", loss_bwd, [_full(x1), _full(hf), _full(tgt)], [g2, ln2_g, ln2_b],
        [(D, F32), (D, BF16)], [(1, 1), vec, vec, vec])
    dp = matmul(dhf, w_d, "nt", F32, "ffn_out_dx")
    dw_d = matmul(p, dhf, "tn", F32, "ffn_out_dw")

    def swiglu_bwd(r, c):
        _, vjp = jax.vjp(stage_swiglu, *gate_up(r[0]))
        return (per_chip(*vjp(r[1])),), ()

    (dab,) = rowwise("swiglu_backward", swiglu_bwd, [_full(ab), _full(dp)], [], [(2 * D_FF, BF16)], tm_max=256)
    du2 = matmul(dab, w_gu, "nn", F32, "ffn_in_dx")
    dw_gu = matmul(dab, u2, "tn", F32, "ffn_in_dw")

    def post_bwd(r, c):
        _, vjp = jax.vjp(stage_post_mixer, r[0], r[1], *c)
        dx, dh, *dc = vjp((r[2], r[3]))
        return (dx, dh), tuple(dc)

    dx_a, dh, dg1, dln1_g, dln1_b, dsc2, dsh2 = rowwise(
        "post_mixer_backward", post_bwd, [_full(x), _full(h), _full(dx1), _full(du2)], post_consts,
        [(D, F32), (D, BF16)], [vec] * 5)
    dmerged = matmul(dh, w_o, "nt", F32, "out_proj_dx")
    dw_o = matmul(merged, dh, "tn", F32, "out_proj_dw")

    def merge_bwd(r, c):
        _, vjp = jax.vjp(stage_merge, *r[:4])
        dga, dgb, dya, dyb = vjp(r[4])
        return (jnp.concatenate([dga, dgb], axis=1), dya, dyb), ()

    dproj, dya, dyb = rowwise("merge_backward", merge_bwd, gate_rows + [_full(dmerged)], [],
                              [(2 * D, BF16), (D, BF16), (D, BF16)], new_wide=(IN_PAD, COL_GA // (2 * D)))
    dya_in = matmul(dya, w_a, "nt", F32, "branch_a_dx")
    dw_a = matmul(ya_in, dya, "tn", F32, "branch_a_dw")
    dyb_in = matmul(dyb, w_b, "nt", F32, "branch_b_dx")
    dw_b = matmul(yb_in, dyb, "tn", F32, "branch_b_dw")
    gn_after = gn if early is None else gn + early((dw_a, dw_b, dw_o, dw_gu, dw_d))[0:1]
    dproj, dlb, dgn = hgrn_backward(proj, st_a, dya_in, lb, gn_after, dproj)
    dtb_after = dtb if mid is None else dtb + mid(dlb)[0:1, 0:1]
    dxs, dbm, dcm, ddt, dproj, ddtb, dalog, ddsk, dnw = ssd_backward(xc, proj, st_b, dyb_in, dtb_after, alog, dsk, nw, dproj)
    dproj, dconv_w, dconv_b = conv_backward(proj, dxs, dbm, dcm, conv_w, conv_b, dproj)
    if late is not None:
        late(dconv_b)
    t = x.shape[0]
    tail = jnp.concatenate([jnp.sum(ddt, axis=0).astype(BF16), jnp.zeros((t, IN_PAD - COL_DT - LANES), BF16)], axis=1)
    dproj = lax.dynamic_update_slice(dproj, tail, (0, COL_DT))
    if last is None:
        dw_in, started = matmul(u1, dproj, "tn", F32, "in_proj_dw"), None
    else:
        dw_in, started = None, last(u1, dproj)
    du1 = matmul(dproj, w_in, "nt", F32, "in_proj_dx", after=started)

    def mod_bwd(r, c):
        _, vjp = jax.vjp(stage_modulate, r[0], *c)
        dx, dsc, dsh = vjp(r[1])
        return (dx + r[2],), (dsc, dsh)

    grad_x, dsc1, dsh1 = rowwise("modulate1_backward", mod_bwd, [_full(x), _full(du1), _full(dx_a)], [sc1, sh1],
                                 [(D, F32)], [vec, vec])
    d_mod = (dsh1, dsc1, dg1, dsh2, dsc2, dg2)
    d_wts = (dw_in, dw_a, dw_b, dw_o, dw_gu, dw_d)
    d_small = (dlb, dgn, dconv_w, dconv_b, jnp.sum(ddtb, axis=0),
               dalog.reshape(1, B_INNER), ddsk.reshape(1, B_INNER), dnw.reshape(1, B_INNER),
               dln1_g, dln1_b, dln2_g, dln2_b)
    return loss, grad_x, d_mod, d_wts, d_small


HBM = pl.BlockSpec(memory_space=pltpu.HBM)
SEM = pl.BlockSpec(memory_space=pltpu.SEMAPHORE)
DATAFLOW = pltpu.SideEffectType.DATAFLOW_SIDE_EFFECTING


def _place():
    return lax.axis_index("x"), lax.axis_index("y"), lax.axis_index("c")


def _other_chips(x, y):
    return [(1 - x, y), (x, 1 - y), (1 - x, 1 - y)]


def _remote(src, dst, send_sem, recv_sem, device):
    return pltpu.make_async_remote_copy(src_ref=src, dst_ref=dst, send_sem=send_sem, recv_sem=recv_sem,
                                        device_id=device, device_id_type=MESH)


def gather_rows(v, name):
    n = v.shape[1]

    def body(v_ref, out_ref, send_sems, recv_sems, local_sem):
        x, y, c = _place()
        mine = pltpu.make_async_copy(v_ref, out_ref.at[4 * x + 2 * y + c], local_sem)
        mine.start()
        sends, recvs = [], []
        for m in range(1, 8):
            px = 1 - x if m & 4 else x
            py = 1 - y if m & 2 else y
            pc = 1 - c if m & 1 else c
            sends.append(_remote(v_ref, out_ref.at[4 * x + 2 * y + c], send_sems.at[m - 1], recv_sems.at[m - 1], (px, py, pc)))
            recvs.append(_remote(v_ref, out_ref.at[4 * px + 2 * py + pc], send_sems.at[m - 1], recv_sems.at[m - 1], (px, py, pc)))
        for cp in sends:
            cp.start()
        for cp in recvs:
            cp.wait_recv()
        for cp in sends:
            cp.wait_send()
        mine.wait()

    return pl.pallas_call(
        body, name=name, in_specs=[HBM], out_specs=HBM,
        out_shape=jax.ShapeDtypeStruct((8, 1, n), v.dtype),
        scratch_shapes=[pltpu.SemaphoreType.DMA((7,)), pltpu.SemaphoreType.DMA((7,)), pltpu.SemaphoreType.DMA],
    )(v)


def exchange_rows(part, name):
    w = part.shape[2]

    def body(p_ref, out_ref, send_sems, recv_sems, local_sem):
        x, y, c = _place()
        k = 2 * x + y
        mine = pltpu.make_async_copy(p_ref.at[4 * x + 2 * y + c], out_ref.at[k], local_sem)
        mine.start()
        sends, recvs = [], []
        for j, (px, py) in enumerate(_other_chips(x, y)):
            sends.append(_remote(p_ref.at[4 * px + 2 * py + c], out_ref.at[k], send_sems.at[j], recv_sems.at[j], (px, py, c)))
            recvs.append(_remote(p_ref.at[4 * px + 2 * py + c], out_ref.at[2 * px + py], send_sems.at[j], recv_sems.at[j], (px, py, c)))
        for cp in sends:
            cp.start()
        for cp in recvs:
            cp.wait_recv()
        for cp in sends:
            cp.wait_send()
        mine.wait()

    return pl.pallas_call(
        body, name=name, in_specs=[HBM], out_specs=HBM,
        out_shape=jax.ShapeDtypeStruct((4, 1, w), part.dtype),
        scratch_shapes=[pltpu.SemaphoreType.DMA((3,)), pltpu.SemaphoreType.DMA((3,)), pltpu.SemaphoreType.DMA],
    )(part)


def _half_of_slot(ref, rows, px, py, pc):
    return ref.at[2 * px + py, pl.ds(pc * (rows // 2), rows // 2), :]


def gather_start(shards, after, tag):
    n = len(shards)

    def body(*refs):
        w_refs, land_refs = refs[:n], refs[n:2 * n]
        send_sems, recv_sems = refs[2 * n + 1], refs[2 * n + 2]
        token = refs[-1]
        x, y, c = _place()
        for i in range(n):
            rows = shards[i].shape[0]
            for j, (px, py) in enumerate(_other_chips(x, y)):
                _remote(w_refs[i].at[pl.ds(c * (rows // 2), rows // 2), :], _half_of_slot(land_refs[i], rows, x, y, c),
                        send_sems.at[j * n + i], recv_sems.at[j * n + i], (px, py, c)).start()
        token[...] = jnp.zeros_like(token)

    hbm = lambda a: pltpu.with_memory_space_constraint(a, pltpu.HBM)
    lands = [lax.empty((4,) + s.shape, s.dtype) for s in shards]
    dma = pltpu.SemaphoreType.DMA
    return pl.pallas_call(
        body, name="gather_start_" + tag,
        out_shape=(dma((3 * n,)), dma((3 * n,)),
                   *[pltpu.HBM(a.shape, a.dtype) for a in list(shards) + lands], jax.ShapeDtypeStruct((8, LANES), F32)),
        in_specs=[HBM] * (2 * n) + [pl.BlockSpec(memory_space=pl.ANY)],
        out_specs=(SEM, SEM, *[HBM] * (2 * n), pl.BlockSpec(memory_space=pltpu.VMEM)),
        input_output_aliases={i: 2 + i for i in range(2 * n)},
        compiler_params=pltpu.CompilerParams(has_side_effects=DATAFLOW),
    )(*[hbm(a) for a in list(shards) + lands], after)


def gather_wait(send_sems, recv_sems, shards, lands, after, tag):
    n = len(shards)

    def body(*refs):
        w_refs, land_refs = refs[:n], refs[n:2 * n]
        send_ref, recv_ref = refs[2 * n], refs[2 * n + 1]
        x, y, c = _place()
        for i in range(n):
            rows = shards[i].shape[0]
            for j, (px, py) in enumerate(_other_chips(x, y)):
                cp = _remote(w_refs[i].at[pl.ds(c * (rows // 2), rows // 2), :], _half_of_slot(land_refs[i], rows, px, py, c),
                             send_ref.at[j * n + i], recv_ref.at[j * n + i], (px, py, c))
                cp.wait_send()
                cp.wait_recv()

    out = pl.pallas_call(
        body, name="gather_wait_" + tag,
        out_shape=tuple(pltpu.HBM(a.shape, a.dtype) for a in list(shards) + list(lands)),
        in_specs=[HBM] * (2 * n) + [SEM, SEM, pl.BlockSpec(memory_space=pl.ANY)], out_specs=tuple([HBM] * (2 * n)),
        input_output_aliases={i: i for i in range(2 * n)},
        compiler_params=pltpu.CompilerParams(has_side_effects=DATAFLOW),
    )(*shards, *lands, send_sems, recv_sems, after)
    return list(out[:n]), list(out[n:])


def forward_start(lands, tag):
    n = len(lands)

    def body(*refs):
        land_refs = refs[:n]
        send_sems, recv_sems = refs[n], refs[n + 1]
        token = refs[-1]
        x, y, c = _place()
        for i in range(n):
            rows = lands[i].shape[1]
            for j, (px, py) in enumerate(_other_chips(x, y)):
                mine = _half_of_slot(land_refs[i], rows, px, py, c)
                _remote(mine, mine, send_sems.at[j * n + i], recv_sems.at[j * n + i], (x, y, 1 - c)).start()
        token[...] = jnp.zeros_like(token)

    dma = pltpu.SemaphoreType.DMA
    return pl.pallas_call(
        body, name="forward_start_" + tag,
        out_shape=(dma((3 * n,)), dma((3 * n,)), *[pltpu.HBM(a.shape, a.dtype) for a in lands],
                   jax.ShapeDtypeStruct((8, LANES), F32)),
        in_specs=[HBM] * n, out_specs=(SEM, SEM, *[HBM] * n, pl.BlockSpec(memory_space=pltpu.VMEM)),
        input_output_aliases={i: 2 + i for i in range(n)},
        compiler_params=pltpu.CompilerParams(has_side_effects=DATAFLOW),
    )(*lands)


def forward_wait(started, after, tag):
    send_sems, recv_sems, *rest = started
    lands = rest[:-1]
    n = len(lands)

    def body(*refs):
        land_refs = refs[:n]
        send_ref, recv_ref = refs[n], refs[n + 1]
        x, y, c = _place()
        for i in range(n):
            rows = lands[i].shape[1]
            for j, (px, py) in enumerate(_other_chips(x, y)):
                cp = _remote(_half_of_slot(land_refs[i], rows, px, py, c), _half_of_slot(land_refs[i], rows, px, py, 1 - c),
                             send_ref.at[j * n + i], recv_ref.at[j * n + i], (x, y, 1 - c))
                cp.wait_send()
                cp.wait_recv()

    out = pl.pallas_call(
        body, name="forward_wait_" + tag,
        out_shape=tuple(pltpu.HBM(a.shape, a.dtype) for a in lands),
        in_specs=[HBM] * n + [SEM, SEM, pl.BlockSpec(memory_space=pl.ANY)], out_specs=tuple([HBM] * n),
        input_output_aliases={i: i for i in range(n)},
        compiler_params=pltpu.CompilerParams(has_side_effects=DATAFLOW),
    )(*lands, send_sems, recv_sems, after)
    return list(out)


def pair_start(slabs, tag):
    n = len(slabs)

    def body(*refs):
        g_refs, land_refs = refs[:n], refs[n:2 * n]
        send_sems, recv_sems = refs[2 * n], refs[2 * n + 1]
        token = refs[-1]
        x, y, c = _place()
        for i in range(n):
            hr = slabs[i].shape[1] // 2
            _remote(g_refs[i].at[:, pl.ds((1 - c) * hr, hr), :], land_refs[i], send_sems.at[i], recv_sems.at[i],
                    (x, y, 1 - c)).start()
        token[...] = jnp.zeros_like(token)

    hbm = lambda a: pltpu.with_memory_space_constraint(a, pltpu.HBM)
    lands = [lax.empty((4, s.shape[1] // 2, s.shape[2]), s.dtype) for s in slabs]
    dma = pltpu.SemaphoreType.DMA
    return pl.pallas_call(
        body, name="pair_start_" + tag,
        out_shape=(dma((n,)), dma((n,)), *[pltpu.HBM(a.shape, a.dtype) for a in list(slabs) + lands],
                   jax.ShapeDtypeStruct((8, LANES), F32)),
        in_specs=[HBM] * (2 * n), out_specs=(SEM, SEM, *[HBM] * (2 * n), pl.BlockSpec(memory_space=pltpu.VMEM)),
        input_output_aliases={i: 2 + i for i in range(2 * n)},
        compiler_params=pltpu.CompilerParams(has_side_effects=DATAFLOW),
    )(*[hbm(a) for a in list(slabs) + lands])


def pair_wait(started, after, tag):
    send_sems, recv_sems, *rest = started
    n = (len(rest) - 1) // 2
    slabs, lands = rest[:n], rest[n:2 * n]

    def body(*refs):
        g_refs, land_refs = refs[:n], refs[n:2 * n]
        send_ref, recv_ref = refs[2 * n], refs[2 * n + 1]
        x, y, c = _place()
        for i in range(n):
            hr = slabs[i].shape[1] // 2
            cp = _remote(g_refs[i].at[:, pl.ds((1 - c) * hr, hr), :], land_refs[i], send_ref.at[i], recv_ref.at[i], (x, y, 1 - c))
            cp.wait_send()
            cp.wait_recv()

    out = pl.pallas_call(
        body, name="pair_wait_" + tag,
        out_shape=tuple(pltpu.HBM(a.shape, a.dtype) for a in list(slabs) + list(lands)),
        in_specs=[HBM] * (2 * n) + [SEM, SEM, pl.BlockSpec(memory_space=pl.ANY)], out_specs=tuple([HBM] * (2 * n)),
        input_output_aliases={i: i for i in range(2 * n)},
        compiler_params=pltpu.CompilerParams(has_side_effects=DATAFLOW),
    )(*slabs, *lands, send_sems, recv_sems, after)
    return list(out[:n]), list(out[n:])


def _tile2(rows, cols):
    fits = lambda r, c: r * c * 4 <= BLOCK_BYTES
    if fits(rows, cols):
        return rows, cols
    tiles = [(r, cols) for r in (1024, 512, 256, 128, 64) if rows % r == 0 and fits(r, cols)]
    tiles += [(rows, cols // k) for k in (2, 3, 4, 6, 8, 12, 16) if cols % (k * LANES) == 0 and fits(rows, cols // k)]
    return max(tiles, key=lambda t: t[0] * t[1])


def pair_add(g, p, c, name):
    _, hr, cols = p.shape
    tm, tc = _tile2(hr, cols)
    per = hr // tm

    def body(c_ref, g_ref, p_ref, o_ref):
        o_ref[...] = (g_ref[...] + p_ref[...]).astype(o_ref.dtype)

    return pl.pallas_call(
        body, name=name,
        grid_spec=pltpu.PrefetchScalarGridSpec(
            num_scalar_prefetch=1, grid=(4, per, cols // tc),
            in_specs=[pl.BlockSpec((None, tm, tc), lambda k, i, j, c_ref: (k, c_ref[0] * per + i, j)),
                      pl.BlockSpec((None, tm, tc), lambda k, i, j, c_ref: (k, i, j))],
            out_specs=pl.BlockSpec((None, tm, tc), lambda k, i, j, c_ref: (k, i, j))),
        out_shape=jax.ShapeDtypeStruct((4, hr, cols), BF16),
        compiler_params=_params(("arbitrary", "arbitrary", "arbitrary")),
    )(c.reshape(1).astype(jnp.int32), g, p)


def scatter_start(sums, tag):
    n = len(sums)

    def body(*refs):
        s_refs, land_refs = refs[:n], refs[n:2 * n]
        send_sems, recv_sems = refs[2 * n], refs[2 * n + 1]
        token = refs[-1]
        x, y, c = _place()
        k = 2 * x + y
        for i in range(n):
            for j, (px, py) in enumerate(_other_chips(x, y)):
                _remote(s_refs[i].at[2 * px + py], land_refs[i].at[k], send_sems.at[j * n + i], recv_sems.at[j * n + i],
                        (px, py, c)).start()
        token[...] = jnp.zeros_like(token)

    hbm = lambda a: pltpu.with_memory_space_constraint(a, pltpu.HBM)
    return pl.pallas_call(
        body, name="scatter_start_" + tag,
        out_shape=(pltpu.SemaphoreType.DMA((3 * n,)), pltpu.SemaphoreType.DMA((3 * n,)),
                   *[pltpu.HBM(s.shape, s.dtype) for s in sums], *[pltpu.HBM(s.shape, s.dtype) for s in sums],
                   jax.ShapeDtypeStruct((8, LANES), F32)),
        in_specs=[HBM] * (2 * n), out_specs=(SEM, SEM, *[HBM] * (2 * n), pl.BlockSpec(memory_space=pltpu.VMEM)),
        input_output_aliases={i: 2 + i for i in range(2 * n)},
        compiler_params=pltpu.CompilerParams(has_side_effects=DATAFLOW),
    )(*[hbm(s) for s in sums], *[hbm(lax.empty(s.shape, s.dtype)) for s in sums])


def scatter_wait(started, after, tag):
    send_sems, recv_sems, *rest = started
    n = (len(rest) - 1) // 2
    sums, lands = rest[:n], rest[n:2 * n]

    def body(*refs):
        s_refs, land_refs = refs[:n], refs[n:2 * n]
        send_ref, recv_ref = refs[2 * n], refs[2 * n + 1]
        x, y, c = _place()
        for i in range(n):
            for j, (px, py) in enumerate(_other_chips(x, y)):
                cp = _remote(s_refs[i].at[2 * px + py], land_refs[i].at[2 * px + py], send_ref.at[j * n + i],
                             recv_ref.at[j * n + i], (px, py, c))
                cp.wait_send()
                cp.wait_recv()

    out = pl.pallas_call(
        body, name="scatter_wait_" + tag,
        out_shape=tuple(pltpu.HBM(s.shape, s.dtype) for s in sums + lands),
        in_specs=[HBM] * (2 * n) + [SEM, SEM, pl.BlockSpec(memory_space=pl.ANY)], out_specs=tuple([HBM] * (2 * n)),
        input_output_aliases={i: i for i in range(2 * n)},
        compiler_params=pltpu.CompilerParams(has_side_effects=DATAFLOW),
    )(*sums, *lands, send_sems, recv_sems, after)
    return list(out[:n]), list(out[n:])


def sum_chips(landed, own, chip, core, name):
    _, hr, cols = landed.shape
    tm, tc = _tile2(hr, cols)
    per = hr // tm

    def body(idx_ref, l0, l1, l2, l3, own_ref, o_ref):
        mine = own_ref[...].astype(F32)
        v = [jnp.where(idx_ref[0] == k, mine, ref[...].astype(F32)) for k, ref in enumerate((l0, l1, l2, l3))]
        o_ref[...] = ((v[0] + v[1]) + v[2]) + v[3]

    slot = lambda k: pl.BlockSpec((None, tm, tc),
                                  lambda i, j, idx: (jnp.where(idx[0] == k, (k + 1) & 3, k), i, j))
    return pl.pallas_call(
        body, name=name,
        grid_spec=pltpu.PrefetchScalarGridSpec(
            num_scalar_prefetch=1, grid=(per, cols // tc),
            in_specs=[slot(0), slot(1), slot(2), slot(3),
                      pl.BlockSpec((None, tm, tc), lambda i, j, idx: (idx[0], i, j))],
            out_specs=pl.BlockSpec((tm, tc), lambda i, j, idx: (idx[1] * per + i, j))),
        out_shape=jax.ShapeDtypeStruct((2 * hr, cols), F32),
        compiler_params=_params(("arbitrary", "arbitrary")),
    )(jnp.stack([chip, core]).astype(jnp.int32), landed, landed, landed, landed, own)


def _halves_copies(refs, shapes, send_sems, recv_sems):
    x, y, c = _place()
    copies = []
    for i, ref in enumerate(refs):
        hr = shapes[i][0] // 2
        own = ref.at[pl.ds(c * hr, hr), :]
        other = ref.at[pl.ds((1 - c) * hr, hr), :]
        copies.append((_remote(own, own, send_sems.at[i], recv_sems.at[i], (x, y, 1 - c)),
                       _remote(other, other, send_sems.at[i], recv_sems.at[i], (x, y, 1 - c))))
    return copies


def halves_start(bufs):
    n = len(bufs)
    shapes = [b.shape for b in bufs]

    def body(*refs):
        for sent, _ in _halves_copies(refs[:n], shapes, refs[n], refs[n + 1]):
            sent.start()
        refs[-1][...] = jnp.zeros_like(refs[-1])

    hbm = lambda a: pltpu.with_memory_space_constraint(a, pltpu.HBM)
    dma = pltpu.SemaphoreType.DMA
    return pl.pallas_call(
        body, name="halves_start",
        out_shape=(dma((n,)), dma((n,)), *[pltpu.HBM(b.shape, b.dtype) for b in bufs],
                   jax.ShapeDtypeStruct((8, LANES), F32)),
        in_specs=[HBM] * n, out_specs=(SEM, SEM, *[HBM] * n, pl.BlockSpec(memory_space=pltpu.VMEM)),
        input_output_aliases={i: 2 + i for i in range(n)},
        compiler_params=pltpu.CompilerParams(has_side_effects=DATAFLOW),
    )(*[hbm(b) for b in bufs])


def halves_wait(started, after):
    send_sems, recv_sems, *bufs, _ = started
    n = len(bufs)
    shapes = [b.shape for b in bufs]

    def body(*refs):
        for sent, received in _halves_copies(refs[:n], shapes, refs[n], refs[n + 1]):
            received.wait_recv()
            sent.wait_send()

    return pl.pallas_call(
        body, name="halves_wait",
        out_shape=tuple(pltpu.HBM(b.shape, b.dtype) for b in bufs),
        in_specs=[HBM] * n + [SEM, SEM, pl.BlockSpec(memory_space=pl.ANY)], out_specs=tuple([HBM] * n),
        input_output_aliases={i: i for i in range(n)},
        compiler_params=pltpu.CompilerParams(has_side_effects=DATAFLOW),
    )(*bufs, send_sems, recv_sems, after)


def assemble_in_proj(landed, own, chip):
    rows, cols = 128, own.shape[1]

    def body(idx_ref, l0, l1, l2, l3, own_ref, o_ref):
        mine = own_ref[...]
        w = jnp.concatenate([jnp.where(idx_ref[0] == k, mine, ref[...]) for k, ref in enumerate((l0, l1, l2, l3))], axis=1)
        o_ref[...] = jnp.concatenate([w[:, :ORIG_Z], w[:, ORIG_GA:], w[:, ORIG_XBC:ORIG_DT], w[:, ORIG_Z:ORIG_XBC],
                                      w[:, ORIG_DT:ORIG_GA], jnp.zeros((rows, IN_PAD - IN_ORIG), w.dtype)], axis=1)

    slot = lambda k: pl.BlockSpec((None, rows, cols), lambda i, idx: (jnp.where(idx[0] == k, (k + 1) & 3, k), i, 0))
    return pl.pallas_call(
        body, name="assemble_in_proj",
        grid_spec=pltpu.PrefetchScalarGridSpec(
            num_scalar_prefetch=1, grid=(D // rows,),
            in_specs=[slot(0), slot(1), slot(2), slot(3), pl.BlockSpec((rows, cols), lambda i, idx: (i, 0))],
            out_specs=pl.BlockSpec((rows, IN_PAD), lambda i, idx: (i, 0))),
        out_shape=jax.ShapeDtypeStruct((D, IN_PAD), own.dtype),
        compiler_params=_params(("arbitrary",)),
    )(chip.reshape(1).astype(jnp.int32), landed, landed, landed, landed, own)


def rows_exchange(a, name):
    hr = a.shape[0] // 2

    def body(a_ref, out_ref, send_sem, recv_sem):
        x, y, c = _place()
        cp = _remote(a_ref.at[pl.ds((1 - c) * hr, hr), :], out_ref, send_sem, recv_sem, (x, y, 1 - c))
        cp.start()
        cp.wait()

    return pl.pallas_call(
        body, name=name, in_specs=[HBM], out_specs=HBM,
        out_shape=jax.ShapeDtypeStruct((hr, a.shape[1]), a.dtype),
        scratch_shapes=[pltpu.SemaphoreType.DMA, pltpu.SemaphoreType.DMA],
    )(a)


def rows_start(a, tag):
    hr = a.shape[0] // 2

    def body(a_ref, land_ref, send_sem, recv_sem, a_thru, land_thru, token):
        x, y, c = _place()
        _remote(a_ref.at[pl.ds((1 - c) * hr, hr), :], land_ref, send_sem, recv_sem, (x, y, 1 - c)).start()
        token[...] = jnp.zeros_like(token)

    hbm = lambda v: pltpu.with_memory_space_constraint(v, pltpu.HBM)
    dma = pltpu.SemaphoreType.DMA
    return pl.pallas_call(
        body, name="rows_start_" + tag,
        out_shape=(dma(()), dma(()), pltpu.HBM(a.shape, a.dtype), pltpu.HBM((hr, a.shape[1]), a.dtype),
                   jax.ShapeDtypeStruct((8, LANES), F32)),
        in_specs=[HBM, HBM], out_specs=(SEM, SEM, HBM, HBM, pl.BlockSpec(memory_space=pltpu.VMEM)),
        input_output_aliases={0: 2, 1: 3},
        compiler_params=pltpu.CompilerParams(has_side_effects=DATAFLOW),
    )(hbm(a), hbm(lax.empty((hr, a.shape[1]), a.dtype)))


def rows_wait(started, after, tag):
    send_sem, recv_sem, a, land, _ = started
    hr = a.shape[0] // 2

    def body(a_ref, land_ref, send_ref, recv_ref, after_ref, a_thru, got_ref):
        x, y, c = _place()
        cp = _remote(a_ref.at[pl.ds((1 - c) * hr, hr), :], land_ref, send_ref, recv_ref, (x, y, 1 - c))
        cp.wait_send()
        cp.wait_recv()

    return pl.pallas_call(
        body, name="rows_wait_" + tag,
        out_shape=(pltpu.HBM(a.shape, a.dtype), pltpu.HBM(land.shape, land.dtype)),
        in_specs=[HBM, HBM, SEM, SEM, pl.BlockSpec(memory_space=pl.ANY)], out_specs=(HBM, HBM),
        input_output_aliases={0: 0, 1: 1},
        compiler_params=pltpu.CompilerParams(has_side_effects=DATAFLOW),
    )(a, land, send_sem, recv_sem, after)


def split_pair_add(pieces, received, core):
    cols = IN_ORIG // 4
    rows, hr = 128, D // 2
    per = hr // rows
    n_p = len(pieces)

    def body(c_ref, *refs):
        o_ref = refs[-1]
        d = jnp.concatenate([refs[i][...] + refs[n_p + i][...] for i in range(n_p)], axis=1)
        w = jnp.concatenate([d[:, :COL_GA], d[:, COL_Z:COL_DT], d[:, COL_XBC:COL_Z], d[:, COL_DT:COL_DT + 32],
                             d[:, COL_GA:COL_XBC]], axis=1)
        for k in range(4):
            o_ref[k] = w[:, k * cols:(k + 1) * cols].astype(o_ref.dtype)

    return pl.pallas_call(
        body, name="split_pair_add",
        grid_spec=pltpu.PrefetchScalarGridSpec(
            num_scalar_prefetch=1, grid=(per,),
            in_specs=[pl.BlockSpec((rows, p.shape[1]), lambda i, c_ref: (c_ref[0] * per + i, 0)) for p in pieces]
            + [pl.BlockSpec((rows, p.shape[1]), lambda i, c_ref: (i, 0)) for p in received],
            out_specs=pl.BlockSpec((4, rows, cols), lambda i, c_ref: (0, i, 0))),
        out_shape=jax.ShapeDtypeStruct((4, hr, cols), BF16),
        compiler_params=_params(("arbitrary",)),
    )(core.reshape(1).astype(jnp.int32), *pieces, *received)


def ada_prepare(c_all, w_ada, hgrn_lb):
    def body(c_ref, w_ref, lb_ref, mod_ref, row_ref):
        mod_ref[...] = hdot(silu(c_ref[...]), w_ref[...])
        row_ref[...] = sigmoid(lb_ref[0:1, :] - lb_ref[1:2, :])

    return pl.pallas_call(
        body, name="ada_prepare",
        out_shape=[jax.ShapeDtypeStruct((8, w_ada.shape[1]), F32), jax.ShapeDtypeStruct((1, D), F32)],
        compiler_params=pltpu.CompilerParams(vmem_limit_bytes=VMEM_LIMIT),
    )(c_all, w_ada, hgrn_lb)


SMALL_SEGS = (("mod", 6 * D), ("lb", D), ("gnorm", LANES), ("conv_w", 4 * CONV_DIM), ("conv_b", CONV_DIM),
              ("dt_bias", LANES), ("a_log", B_INNER), ("d", B_INNER), ("ssm_norm", B_INNER),
              ("ln1_g", D), ("ln1_b", D), ("ln2_g", D), ("ln2_b", D), ("loss", LANES))
SMALL_PARAMS = ("b_ada", "hgrn_lb", "hgrn_gnorm", "ssm_conv_b", "ssm_dt_bias", "ssm_a_log", "ssm_d", "ssm_norm",
                "ln1_g", "ln1_b", "ln2_g", "ln2_b")


def finalize_small(g_all, c_all, dmod_cols, params, m, v):
    n_p = len(SMALL_PARAMS)
    offs, o = {}, 0
    for nm, width in SMALL_SEGS:
        offs[nm] = (o, width)
        o += width

    def body(*refs):
        g_ref, c_ref, dm_ref = refs[:3]
        p_refs = refs[3:3 + n_p]
        m_refs = refs[3 + n_p:3 + 2 * n_p]
        v_refs = refs[3 + 2 * n_p:3 + 3 * n_p]
        outs = refs[3 + 3 * n_p:]
        gwa_ref, gcw_ref, loss_ref = outs[:3]
        res = outs[3:]
        total = jnp.sum(g_ref[...], axis=0, keepdims=True)
        seg = lambda nm: total[:, offs[nm][0]:offs[nm][0] + offs[nm][1]]
        loss_ref[...] = seg("loss")
        gwa_ref[...] = hdot(silu(c_ref[...]), dm_ref[...], "tn")
        cw = seg("conv_w")
        for j in range(4):
            gcw_ref[j:j + 1, :] = cw[:, j * CONV_DIM:(j + 1) * CONV_DIM]
        hc = lax.broadcasted_iota(jnp.int32, (B_INNER, LANES), 0)
        hj = lax.broadcasted_iota(jnp.int32, (B_INNER, LANES), 1)
        per_head = ((hc >> 6) == hj).astype(F32)
        heads = lambda nm: hdot(jnp.broadcast_to(seg(nm), (8, B_INNER)), per_head)[0:1, 0:32]
        lbp = sigmoid(p_refs[1][0:1, :] - p_refs[1][1:2, :])
        g_row = seg("lb") * lbp * (1.0 - lbp)
        grads = {"b_ada": seg("mod"), "hgrn_gnorm": seg("gnorm"), "ssm_conv_b": seg("conv_b"),
                 "ssm_dt_bias": seg("dt_bias")[:, 0:32], "ssm_a_log": heads("a_log"), "ssm_d": heads("d"),
                 "ssm_norm": seg("ssm_norm"), "ln1_g": seg("ln1_g"), "ln1_b": seg("ln1_b"),
                 "ln2_g": seg("ln2_g"), "ln2_b": seg("ln2_b")}
        for i, nm in enumerate(SMALL_PARAMS):
            g_out, d_out, m_out, v_out = res[4 * i:4 * i + 4]
            if nm == "hgrn_lb":
                for row, gv in ((0, g_row), (1, -g_row)):
                    sl = slice(row, row + 1)
                    dl, mn, vn = adamw(p_refs[i][sl, :], gv, m_refs[i][sl, :], v_refs[i][sl, :])
                    g_out[sl, :], d_out[sl, :], m_out[sl, :], v_out[sl, :] = gv, dl, mn, vn
            else:
                gv = grads[nm]
                dl, mn, vn = adamw(p_refs[i][...], gv, m_refs[i][...], v_refs[i][...])
                g_out[...], d_out[...], m_out[...], v_out[...] = gv, dl, mn, vn

    out_shape = [jax.ShapeDtypeStruct((D, dmod_cols.shape[1]), F32), jax.ShapeDtypeStruct((4, CONV_DIM), F32),
                 jax.ShapeDtypeStruct((1, LANES), F32)]
    for p in params:
        out_shape += [jax.ShapeDtypeStruct(p.shape, F32)] * 4
    return pl.pallas_call(
        body, name="finalize_small", out_shape=out_shape,
        compiler_params=pltpu.CompilerParams(vmem_limit_bytes=VMEM_LIMIT),
    )(g_all, c_all, dmod_cols, *params, *m, *v)


def adam_update(w, g, m, v, name, after=None):
    rows, cols = w.shape
    tm, tc = _tile2(rows, cols)
    order = [] if after is None else [after]

    def body(w_ref, g_ref, m_ref, v_ref, *rest):
        d_ref, mo_ref, vo_ref = rest[len(order):]
        d_ref[...], mo_ref[...], vo_ref[...] = adamw(w_ref[...], g_ref[...], m_ref[...], v_ref[...])

    spec = pl.BlockSpec((tm, tc), lambda i, j: (i, j))
    return pl.pallas_call(
        body, name=name, grid=(rows // tm, cols // tc),
        in_specs=[spec] * 4 + [pl.BlockSpec(memory_space=pl.ANY) for _ in order], out_specs=[spec] * 3,
        out_shape=[jax.ShapeDtypeStruct((rows, cols), F32)] * 3,
        compiler_params=_params(("arbitrary", "arbitrary")),
    )(w, g, m, v, *order)


def kernel(x, c, w_ada, b_ada, w_in, hgrn_lb, hgrn_gnorm, ssm_conv_w, ssm_conv_b, ssm_dt_bias, ssm_a_log, ssm_d, ssm_norm, w_branch_a, w_branch_b, w_o, ln1_g, ln1_b, w_ffn_gate, w_ffn_up, w_ffn_down, ln2_g, ln2_b, loss_target, m_w_ada, m_b_ada, m_w_in, m_hgrn_lb, m_hgrn_gnorm, m_ssm_conv_w, m_ssm_conv_b, m_ssm_dt_bias, m_ssm_a_log, m_ssm_d, m_ssm_norm, m_w_branch_a, m_w_branch_b, m_w_o, m_ln1_g, m_ln1_b, m_w_ffn_gate, m_w_ffn_up, m_w_ffn_down, m_ln2_g, m_ln2_b, v_w_ada, v_b_ada, v_w_in, v_hgrn_lb, v_hgrn_gnorm, v_ssm_conv_w, v_ssm_conv_b, v_ssm_dt_bias, v_ssm_a_log, v_ssm_d, v_ssm_norm, v_w_branch_a, v_w_branch_b, v_w_o, v_ln1_g, v_ln1_b, v_w_ffn_gate, v_w_ffn_up, v_w_ffn_down, v_ln2_g, v_ln2_b):
    given = dict(locals())
    chip = 2 * lax.axis_index("x") + lax.axis_index("y")
    core = lax.axis_index("c")
    t = x.shape[1]

    first = gather_rows(jnp.concatenate([c, ssm_conv_w.reshape(1, CONV_DIM)], axis=1), "gather_cond").reshape(8, D + CONV_DIM)
    c_all = first[:, :D]
    conv_w = first[0::2, D:].reshape(4, 4, CONV_DIM // 4).transpose(1, 0, 2).reshape(4, CONV_DIM)
    mod_part, lb_row = ada_prepare(c_all, w_ada[0], hgrn_lb)
    mod_cols = w_ada.shape[2]
    mod_row = exchange_rows(mod_part.reshape(8, 1, mod_cols), "exchange_mod").reshape(1, 6 * D) + b_ada

    local = {nm: given[nm][0] for nm in SHARDED if nm != "w_ffn_in"}
    local["w_ffn_in"] = jnp.concatenate([w_ffn_gate[0].T, w_ffn_up[0].T], axis=0)
    shards = [local[nm].astype(BF16) for nm in SHARDED]
    send_in, recv_in, sent_in, land_in, started_in = gather_start(shards[:1], mod_row, "in")
    shards = shards[:1] + [(local[nm] + started_in[0, 0]).astype(BF16) for nm in SHARDED[1:]]
    send_rest, recv_rest, *flying = gather_start(shards[1:], started_in, "rest")
    n_rest = len(SHARDED) - 1
    sent_rest, land_rest, started_rest = flying[:n_rest], flying[n_rest:2 * n_rest], flying[-1]
    mod_row = mod_row + started_rest[0:1, 0:1]
    mod = tuple(mod_row[:, i * D:(i + 1) * D] for i in range(6))
    with_own = lambda land, shard: lax.dynamic_update_slice(land, shard[None], (chip, 0, 0))

    class Weights:
        def input_projection(self, after):
            (own,), land = gather_wait(send_in, recv_in, [sent_in], [land_in], after, "in")
            (land,) = forward_wait(forward_start(land, "in"), after, "in")
            return assemble_in_proj(land, own, chip)

        def start_rest(self, after):
            self.own, landed = gather_wait(send_rest, recv_rest, sent_rest, land_rest, after, "rest")
            self.started = forward_start(landed, "rest")
            return self.started[-1]

        def rest(self, after):
            got = {nm: with_own(land, s) for nm, land, s in zip(SHARDED[1:], forward_wait(self.started, after, "rest"), self.own, strict=True)}
            whole = lambda nm: got[nm].reshape(4 * got[nm].shape[1], got[nm].shape[2])
            return tuple(whole(nm) for nm in SHARDED[1:])

    wts = Weights()

    per_head = lambda p: jnp.pad(p, ((0, 0), (0, LANES - p.shape[1])))
    per_channel = lambda p: jnp.repeat(p[0], B_INNER // 32)[None]
    small = (lb_row, hgrn_gnorm, conv_w, ssm_conv_b, per_head(ssm_dt_bias), per_channel(ssm_a_log),
             per_channel(ssm_d), ssm_norm, ln1_g, ln1_b, ln2_g, ln2_b)
    by_rows = lambda g: g.reshape(4, g.shape[0] // 4, g.shape[1])
    travelling = {}

    def start_early(dws):
        travelling["pair"] = pair_start([by_rows(dw) for dw in dws], "early")
        return travelling["pair"][-1]

    def between_scans(after):
        slabs, received = pair_wait(travelling["pair"], after, "early")
        travelling["pairs"] = [pair_add(s, r, core, "pair_add_" + nm) for nm, s, r in zip(SHARDED[1:], slabs, received, strict=True)]
        travelling["started"] = scatter_start(travelling["pairs"], "early")
        return travelling["started"][-1]

    def finish_early(after):
        travelling["pairs"], travelling["landed"] = scatter_wait(travelling["started"], after, "early")

    def start_last(u1, dproj):
        wide = 2 * IN_PAD // 3
        first = matmul(u1, dproj, "tn", F32, "in_proj_dw_first", b_cols=(0, wide))
        sending = rows_start(first, "last")
        second = matmul(u1, dproj, "tn", F32, "in_proj_dw_second", after=sending[-1], b_cols=(wide, IN_PAD - wide))
        first, got_first = rows_wait(sending, second, "last")
        got_second = rows_exchange(second, "pair_exchange_last")
        travelling["pairs_in"] = [split_pair_add([first, second], [got_first, got_second], core)]
        travelling["started_in"] = scatter_start(travelling["pairs_in"], "last")
        return travelling["started_in"][-1]

    loss, grad_x, d_mod, d_wts, d_small = local_step(x[0], loss_target[0], mod, wts, small,
                                                     start_early, between_scans, finish_early, start_last)

    d_lb, d_gn, d_cw, d_cb, d_dtb, d_alog, d_dsk, d_nw, d_l1g, d_l1b, d_l2g, d_l2b = d_small
    row = jnp.concatenate(list(d_mod) + [d_lb, d_gn, d_cw.reshape(1, 4 * CONV_DIM), d_cb, d_dtb, d_alog, d_dsk, d_nw,
                                          d_l1g, d_l1b, d_l2g, d_l2b, jnp.pad(loss, ((0, 0), (0, LANES - 1)))], axis=1)
    g_all = gather_rows(row, "gather_small_grads").reshape(8, row.shape[1])
    dmod_cols = lax.dynamic_slice_in_dim(g_all, chip * mod_cols, mod_cols, axis=1)
    fin = finalize_small(g_all, c_all, dmod_cols, [given[n] for n in SMALL_PARAMS],
                         [given["m_" + n] for n in SMALL_PARAMS], [given["v_" + n] for n in SMALL_PARAMS])
    grads, deltas, new_m, new_v = {}, {}, {}, {}
    grads["w_ada"] = fin[0][None]
    grads["ssm_conv_w"] = lax.dynamic_slice_in_dim(fin[1], chip * (CONV_DIM // 4), CONV_DIM // 4, axis=1)[None]
    for i, nm in enumerate(SMALL_PARAMS):
        grads[nm], deltas[nm], new_m[nm], new_v[nm] = fin[3 + 4 * i:7 + 4 * i]

    pairs_in, landed_in = scatter_wait(travelling["started_in"], fin[3], "last")
    pairs, landed = pairs_in + travelling["pairs"], landed_in + travelling["landed"]
    halves = [sum_chips(r, p, chip, core, "sum_chips_" + nm) for nm, r, p in zip(SHARDED, landed, pairs, strict=True)]
    exchanging = halves_start(halves)
    reduced = {"w_ada": grads["w_ada"][0], "ssm_conv_w": grads["ssm_conv_w"][0]}

    def update(nm, after=None):
        flipped = nm in ("w_in", "w_ffn_gate", "w_ffn_up")
        work = (lambda a: a[0].T) if flipped else (lambda a: a[0])
        back = (lambda a: a.T[None]) if flipped else (lambda a: a[None])
        d_, m_, v_ = adam_update(work(given[nm]), reduced[nm], work(given["m_" + nm]), work(given["v_" + nm]),
                                 "adam_" + nm, after)
        grads[nm], deltas[nm], new_m[nm], new_v[nm] = back(reduced[nm]), back(d_), back(m_), back(v_)

    update("w_ada", exchanging[-1])
    update("ssm_conv_w", exchanging[-1])
    reduced.update(zip(SHARDED, halves_wait(exchanging, new_m["w_ada"]), strict=True))
    reduced["w_in"] = reduced["w_in"].T
    reduced["w_ffn_gate"], reduced["w_ffn_up"] = reduced["w_ffn_in"][:FFN_SHARD], reduced["w_ffn_in"][FFN_SHARD:]
    for nm in ("w_in", "w_branch_a", "w_branch_b", "w_o", "w_ffn_gate", "w_ffn_up", "w_ffn_down"):
        update(nm)

    names = ("w_ada", "b_ada", "w_in", "hgrn_lb", "hgrn_gnorm", "ssm_conv_w", "ssm_conv_b", "ssm_dt_bias", "ssm_a_log",
             "ssm_d", "ssm_norm", "w_branch_a", "w_branch_b", "w_o", "ln1_g", "ln1_b", "w_ffn_gate", "w_ffn_up",
             "w_ffn_down", "ln2_g", "ln2_b")
    return (fin[2][0, 0], grad_x[None], *[grads[n] for n in names], *[deltas[n] for n in names],
            *[new_m[n] for n in names], *[new_v[n] for n in names])
```

```python
import functools

import jax
import jax.numpy as jnp
from jax import lax
from jax.experimental import pallas as pl
from jax.experimental.pallas import tpu as pltpu

F32, BF16 = jnp.float32, jnp.bfloat16
HI = lax.Precision.HIGHEST
MESH = pl.DeviceIdType.MESH

D = 1024
CHUNK = 64
LANES = 128
N_HEADS_A = 8
N_GROUPS_B = 4
B_INNER = 2048
CONV_DIM = 3072
D_FF = 2816
ALPHA = 2.0 ** 0.25
LN_EPS = 1e-5
RMS_EPS = 1e-6
ADAM_LR, ADAM_B1, ADAM_B2, ADAM_EPS, ADAM_WD, ADAM_STEP = 0.001, 0.9, 0.999, 1e-08, 0.01, 10

IN_ORIG = 11296
IN_PAD = 11520
COL_GA, COL_GB, COL_XBC, COL_Z, COL_DT = 4096, 5120, 6144, 9216, 11264
ORIG_Z, ORIG_XBC, ORIG_DT, ORIG_GA = 4096, 6144, 9216, 9248

SHARDED = ("w_in", "w_branch_a", "w_branch_b", "w_o", "w_ffn_in", "w_ffn_down")
FFN_SHARD = D_FF // 4
VMEM_LIMIT = 56 * 1024 * 1024
BLOCK_BYTES = 2 * 1024 * 1024
_DIMS = {"nn": (((1,), (0,)), ((), ())), "nt": (((1,), (1,)), ((), ())), "tn": (((0,), (0,)), ((), ()))}


def _bd(a, b, mode):
    return lax.dot_general(a.astype(BF16), b.astype(BF16), _DIMS[mode], preferred_element_type=F32)


@functools.partial(jax.custom_vjp, nondiff_argnums=(2,))
def bdot(a, b, mode):
    return _bd(a, b, mode)


def _bdot_fwd(a, b, mode):
    return _bd(a, b, mode), (a, b)


def _bdot_bwd(mode, res, g):
    a, b = res
    if mode == "nn":
        return _bd(g, b, "nt"), _bd(a, g, "tn")
    if mode == "nt":
        return _bd(g, b, "nn"), _bd(g, a, "tn")
    return _bd(b, g, "nt"), _bd(a, g, "nn")


bdot.defvjp(_bdot_fwd, _bdot_bwd)


def hdot(a, b, mode="nn"):
    return lax.dot_general(a, b, _DIMS[mode], precision=HI, preferred_element_type=F32)


def _raw(a, b, mode):
    return lax.dot_general(a, b, _DIMS[mode], preferred_element_type=F32)


def _split(x, n):
    parts, rest = [], x
    for _ in range(n):
        p = rest.astype(BF16)
        parts.append(p)
        rest = rest - p.astype(F32)
    return parts


def _od(a, b, mode, exact):
    if exact == 1:
        e = b.astype(BF16)
        p = _split(a, 3)
        return (_raw(p[2], e, mode) + _raw(p[1], e, mode)) + _raw(p[0], e, mode)
    e = a.astype(BF16)
    p = _split(b, 3)
    return (_raw(e, p[2], mode) + _raw(e, p[1], mode)) + _raw(e, p[0], mode)


@functools.partial(jax.custom_vjp, nondiff_argnums=(2, 3))
def odot(a, b, mode, exact):
    return _od(a, b, mode, exact)


def _odot_fwd(a, b, mode, exact):
    return _od(a, b, mode, exact), (a, b)


def _odot_bwd(mode, exact, res, g):
    a, b = res
    if exact == 1:
        da = {"nn": lambda: _od(g, b, "nt", 1), "nt": lambda: _od(g, b, "nn", 1), "tn": lambda: _od(b, g, "nt", 0)}[mode]()
        return da, jnp.zeros_like(b)
    db = {"nn": lambda: _od(a, g, "tn", 0), "nt": lambda: _od(g, a, "tn", 1), "tn": lambda: _od(a, g, "nn", 0)}[mode]()
    return jnp.zeros_like(a), db


odot.defvjp(_odot_fwd, _odot_bwd)


_BDIMS = {"bnn": (((2,), (1,)), ((0,), (0,))), "bnt": (((2,), (2,)), ((0,), (0,))), "btn": (((1,), (1,)), ((0,), (0,)))}


def _braw(a, b, mode):
    return lax.dot_general(a, b, _BDIMS[mode], preferred_element_type=F32)


def _bdb(a, b, mode):
    return _braw(a.astype(BF16), b.astype(BF16), mode)


def _d3b(a, b, mode):
    ah, al = _split(a, 2)
    bh, bl = _split(b, 2)
    return _braw(ah, bh, mode) + (_braw(ah, bl, mode) + _braw(al, bh, mode))


def _batched_bwd(f):
    def bwd(mode, res, g):
        a, b = res
        if mode == "bnn":
            return f(g, b, "bnt"), f(a, g, "btn")
        if mode == "bnt":
            return f(g, b, "bnn"), f(g, a, "btn")
        return f(b, g, "bnt"), f(a, g, "bnn")
    return bwd


@functools.partial(jax.custom_vjp, nondiff_argnums=(2,))
def bdot_b(a, b, mode):
    return _bdb(a, b, mode)


bdot_b.defvjp(lambda a, b, mode: (_bdb(a, b, mode), (a, b)), _batched_bwd(_bdb))


@functools.partial(jax.custom_vjp, nondiff_argnums=(2,))
def dot3_b(a, b, mode):
    return _d3b(a, b, mode)


dot3_b.defvjp(lambda a, b, mode: (_d3b(a, b, mode), (a, b)), _batched_bwd(_d3b))


def _cum(tril3, x, mode):
    e = tril3.astype(BF16)
    p = _split(x, 3)
    return (_braw(e, p[2], mode) + _braw(e, p[1], mode)) + _braw(e, p[0], mode)


@jax.custom_vjp
def chunk_cumsum(tril3, x):
    return _cum(tril3, x, "bnn")


chunk_cumsum.defvjp(lambda t, x: (_cum(t, x, "bnn"), t), lambda t, g: (jnp.zeros_like(t), _cum(t, g, "btn")))


def _unstack(axis, n):
    @jax.custom_vjp
    def un(x):
        return tuple(lax.index_in_dim(x, i, axis, keepdims=False) for i in range(n))

    un.defvjp(lambda x: (un(x), None), lambda _, g: (jnp.stack(g, axis=axis),))
    return un


def _split_last(n, w):
    @jax.custom_vjp
    def sp(x):
        return tuple(x[..., i * w:(i + 1) * w] for i in range(n))

    sp.defvjp(lambda x: (sp(x), None), lambda _, g: (jnp.concatenate(g, axis=-1),))
    return sp


def sigmoid(x):
    return 0.5 * jnp.tanh(0.5 * x) + 0.5


def silu(x):
    return x * sigmoid(x)


def softplus(x):
    return jnp.maximum(x, 0.0) + jnp.log1p(jnp.exp(jnp.minimum(x, -x)))


def _ln(x):
    mu = jnp.mean(x, axis=-1, keepdims=True)
    xc = x - mu
    return xc * lax.rsqrt(jnp.mean(xc * xc, axis=-1, keepdims=True) + LN_EPS)


def _tril64():
    r = lax.broadcasted_iota(jnp.int32, (CHUNK, CHUNK), 0)
    c = lax.broadcasted_iota(jnp.int32, (CHUNK, CHUNK), 1)
    return (r >= c).astype(F32)


def hgrn_block(q, fl, iv, gr, st, lb, gn):
    tb = q.shape[0]
    nc = tb // CHUNK
    nh = N_HEADS_A
    heads = _split_last(nh, LANES)
    to4 = lambda a: jnp.stack(heads(a), axis=0).reshape(nh, nc, CHUNK, LANES)
    flat = lambda a: a.reshape(nh * nc, CHUNK, LANES)
    f = lb + (1.0 - lb) * sigmoid(fl)
    gl4, k4, qf4, v4, gr4 = to4(jnp.log(f)), to4(1.0 - f), to4(silu(q) * (128 ** -0.5)), to4(iv), to4(gr)
    tril = _tril64()
    b4 = chunk_cumsum(jnp.broadcast_to(tril[None], (nh * nc, CHUNK, CHUNK)), flat(gl4)).reshape(gl4.shape)
    blast = jnp.sum(gl4, axis=2, keepdims=True)
    ref = lax.stop_gradient(0.5 * blast)
    qp, kp = qf4 * jnp.exp(b4 - ref), k4 * jnp.exp(ref - b4)
    sc = dot3_b(flat(qp), flat(kp), "bnt") * tril
    o_intra = bdot_b(sc, flat(v4), "bnn").reshape(gl4.shape)
    chunks = _unstack(1, nc)
    qe, v_c, kd, dec = chunks(qp * jnp.exp(ref)), chunks(v4), chunks(kp * jnp.exp(blast - ref)), chunks(jnp.exp(blast))
    o_inter = []
    for c in range(nc):
        o_inter.append(bdot_b(qe[c], st, "bnt"))
        st = st * dec[c] + bdot_b(v_c[c], kd[c], "btn")
    o = o_intra + jnp.stack(o_inter, axis=1)
    on = o * lax.rsqrt(jnp.mean(o * o, axis=-1, keepdims=True) + RMS_EPS) * gn
    out = (on * silu(gr4)).reshape(nh, tb, LANES)
    return jnp.concatenate(_unstack(0, nh)(out), axis=1), st


def ssd_consts(g):
    i32 = jnp.int32
    ej = lax.broadcasted_iota(i32, (LANES, 512), 0)
    ec = lax.broadcasted_iota(i32, (LANES, 512), 1)
    expand = (ej == g * 8 + (ec >> 6)).astype(F32)
    ts = lax.broadcasted_iota(i32, (CHUNK, 512), 0)
    tc = lax.broadcasted_iota(i32, (CHUNK, 512), 1)
    itile = (ts == (tc & 63)).astype(F32)
    maskall = ts >= (tc & 63)
    br = lax.broadcasted_iota(i32, (LANES, LANES), 0)
    bc = lax.broadcasted_iota(i32, (LANES, LANES), 1)
    blockmask = ((br >> 6) == (bc >> 6)).astype(F32)
    return expand, itile, maskall, blockmask, _tril64()


def ssd_block(x, bm, cm, dt, z, st, dtb, alog, dsk, nw, cs):
    expand, itile, maskall, blockmask, tril = cs
    tb = x.shape[0]
    nc = tb // CHUNK
    delta = odot(softplus(dt + dtb), expand, "nn", 1)
    a = -jnp.exp(alog) * delta
    xdt = x * delta
    by_chunk = lambda v: v.reshape(nc, CHUNK, v.shape[-1])
    a3, xdt3, bm3, cm3 = by_chunk(a), by_chunk(xdt), by_chunk(bm), by_chunk(cm)
    acum3 = chunk_cumsum(jnp.broadcast_to(tril[None], (nc, CHUNK, CHUNK)), a3)
    alast3 = jnp.sum(a3, axis=1, keepdims=True)
    cb3 = bdot_b(cm3, jnp.concatenate([bm3] * 8, axis=1), "bnt")
    arow3 = jnp.sum(acum3 * itile, axis=1, keepdims=True)
    dec3 = jnp.exp(jnp.where(maskall, acum3 - arow3, -1e30))
    pairs = _split_last(4, LANES)
    intra = [bdot_b(m, jnp.concatenate([xp] * 2, axis=1) * blockmask, "bnn")
             for m, xp in zip(pairs(cb3 * dec3), pairs(xdt3))]
    chunks = _unstack(0, nc)
    cm_c, bm_c, xw_c, dec_c = chunks(cm3), chunks(bm3), chunks(xdt3 * jnp.exp(alast3 - acum3)), chunks(jnp.exp(alast3))
    inter = []
    for c in range(nc):
        inter.append(bdot(cm_c[c], st, "nn"))
        st = st * dec_c[c] + bdot(bm_c[c], xw_c[c], "tn")
    st_new = st
    y = (jnp.concatenate(intra, axis=-1) + jnp.stack(inter, axis=0) * jnp.exp(acum3)).reshape(tb, 512)
    yz = (y + x * dsk) * silu(z)
    return yz * lax.rsqrt(jnp.mean(yz * yz, axis=-1, keepdims=True) + RMS_EPS) * nw, st_new


def adamw(w, g, m, v):
    m = ADAM_B1 * m + (1.0 - ADAM_B1) * g
    v = ADAM_B2 * v + (1.0 - ADAM_B2) * jnp.square(g)
    m_hat = m / (1.0 - ADAM_B1 ** ADAM_STEP)
    v_hat = v / (1.0 - ADAM_B2 ** ADAM_STEP)
    return -ADAM_LR * (m_hat / (jnp.sqrt(v_hat) + ADAM_EPS) + ADAM_WD * w), m, v


def _pick(n, cands):
    for c in cands:
        if n % c == 0:
            return c
    return n


def _params(sem):
    return pltpu.CompilerParams(dimension_semantics=sem, vmem_limit_bytes=VMEM_LIMIT)


MATMUL_VMEM_BUDGET = 50 * 1024 * 1024
MATMUL_MIN_STEPS = 4
RING_SLOTS = 3


def matmul(a, b, mode, out_dtype, name, after=None, b_cols=None):
    if mode == "nn":
        (m, k), n = a.shape, b.shape[1]
    elif mode == "nt":
        (m, k), n = a.shape, b.shape[0]
    else:
        (k, m), n = a.shape, b.shape[1]
    first_col, n = (0, n) if b_cols is None else b_cols
    a_bytes, b_bytes, out_bytes = a.dtype.itemsize, b.dtype.itemsize, jnp.dtype(out_dtype).itemsize
    k_sizes = (2304, 2048, 1408, 1024, 768, 512, 256, 128)
    usual_tk = _pick(k, k_sizes)

    def vmem(tm_, tn_, tk_):
        blocks = 2 * (tm_ * tk_ * a_bytes + tk_ * tn_ * b_bytes + tm_ * tn_ * out_bytes)
        ring = (RING_SLOTS - 1) * tk_ * tn_ * b_bytes if tk_ == k and mode != "nt" else 0
        return blocks + ring + (tm_ * tn_ * 4 if tk_ < k else 0)

    def traffic(tm_, tn_, tk_):
        return (m // tm_) * k * n * b_bytes + (n // tn_ if tk_ < k else 1) * m * k * a_bytes

    sizes = (2304, 2048, 1920, 1408, 1024, 768, 512, 256, 128)
    tiles = [(tm_, tn_, tk_) for tm_ in sizes if m % tm_ == 0 for tn_ in sizes if n % tn_ == 0
             for tk_ in {k, usual_tk} if vmem(tm_, tn_, tk_) <= MATMUL_VMEM_BUDGET] or [(m, n, k)]
    pipelined = [t for t in tiles if (m // t[0]) * (n // t[1]) * (k // t[2]) >= MATMUL_MIN_STEPS]
    tm, tn, tk = min(pipelined or tiles, key=lambda t: (traffic(*t), t[2] != usual_tk, -t[0] * t[1]))
    nk = k // tk
    a_spec = pl.BlockSpec((tk, tm), lambda i, j, kk: (kk, i)) if mode == "tn" else pl.BlockSpec((tm, tk), lambda i, j, kk: (i, kk))
    assert first_col % tn == 0 and (mode != "nt" or b_cols is None)
    skip = first_col // tn
    b_spec = pl.BlockSpec((tn, tk), lambda i, j, kk: (j, kk)) if mode == "nt" else pl.BlockSpec((tk, tn), lambda i, j, kk: (kk, j + skip))

    order = [] if after is None else [after]
    nj = n // tn
    steps = (m // tm) * nj
    if nk == 1 and mode != "nt" and steps >= RING_SLOTS:
        def fetch(b_ref, buf, sems, t):
            col = pl.multiple_of((lax.rem(t, nj) + skip) * tn, LANES)
            slot = lax.rem(t, RING_SLOTS)
            return pltpu.make_async_copy(b_ref.at[:, pl.ds(col, tn)], buf.at[slot], sems.at[slot])

        def ring_body(a_ref, b_ref, *rest):
            o_ref, buf, sems = rest[len(order):]
            s = pl.program_id(0) * nj + pl.program_id(1)

            @pl.when(s == 0)
            def _():
                for t in range(RING_SLOTS - 1):
                    fetch(b_ref, buf, sems, s + t).start()

            @pl.when(s + RING_SLOTS - 1 < steps)
            def _():
                fetch(b_ref, buf, sems, s + RING_SLOTS - 1).start()

            fetch(b_ref, buf, sems, s).wait()
            o_ref[...] = _bd(a_ref[...], buf[lax.rem(s, RING_SLOTS)], mode).astype(o_ref.dtype)

        return pl.pallas_call(
            ring_body, name=name, grid=(m // tm, nj, 1),
            in_specs=[a_spec, pl.BlockSpec(memory_space=pl.ANY)] + [pl.BlockSpec(memory_space=pl.ANY) for _ in order],
            out_specs=pl.BlockSpec((tm, tn), lambda i, j, kk: (i, j)),
            out_shape=jax.ShapeDtypeStruct((m, n), out_dtype),
            scratch_shapes=[pltpu.VMEM((RING_SLOTS, tk, tn), b.dtype), pltpu.SemaphoreType.DMA((RING_SLOTS,))],
            compiler_params=_params(("arbitrary", "arbitrary", "arbitrary")),
        )(a, b, *order)

    def body(a_ref, b_ref, *rest):
        o_ref, *acc = rest[len(order):]
        part = _bd(a_ref[...], b_ref[...], mode)
        if nk == 1:
            o_ref[...] = part.astype(o_ref.dtype)
            return
        acc_ref, = acc
        kk = pl.program_id(2)

        @pl.when(kk == 0)
        def _():
            acc_ref[...] = part

        @pl.when(jnp.logical_and(kk > 0, kk < nk - 1))
        def _():
            acc_ref[...] += part

        @pl.when(kk == nk - 1)
        def _():
            o_ref[...] = (acc_ref[...] + part).astype(o_ref.dtype)

    return pl.pallas_call(
        body, name=name, grid=(m // tm, n // tn, nk),
        in_specs=[a_spec, b_spec] + [pl.BlockSpec(memory_space=pl.ANY) for _ in order],
        out_specs=pl.BlockSpec((tm, tn), lambda i, j, kk: (i, j)),
        out_shape=jax.ShapeDtypeStruct((m, n), out_dtype),
        scratch_shapes=[pltpu.VMEM((tm, tn), F32)] if nk > 1 else [],
        compiler_params=_params(("parallel", "parallel", "arbitrary")),
    )(a, b, *order)


def rowwise(name, fn, rows, consts, out_rows, out_accs=(), tm_max=512, into=None, new_wide=None):
    t = rows[0][0].shape[0]
    tm = _pick(t, (tm_max, 128, 64, 32, 16, 8))
    n_r, n_c, n_o = len(rows), len(consts), len(out_rows)
    n_alias = 0 if into is None else 1

    def body(*refs):
        r_in = [r[...] for r in refs[:n_r]]
        c_in = [r[...] for r in refs[n_r:n_r + n_c]]
        refs = refs[:n_r + n_c] + refs[n_r + n_c + n_alias:]
        o_refs = refs[n_r + n_c:n_r + n_c + n_o]
        a_refs = refs[n_r + n_c + n_o:]
        ro, ao = fn(r_in, c_in)
        for ref, val in zip(o_refs, ro, strict=True):
            ref[...] = val.astype(ref.dtype)
        if a_refs:
            @pl.when(pl.program_id(0) == 0)
            def _():
                for ref in a_refs:
                    ref[...] = jnp.zeros_like(ref)

            for ref, val in zip(a_refs, ao, strict=True):
                ref[...] += val

    in_specs = [pl.BlockSpec((tm, w), functools.partial(lambda i, cb: (i, cb), cb=cb)) for _, w, cb in rows]
    in_specs += [pl.BlockSpec(c.shape, lambda i: (0, 0)) for c in consts]
    out_specs = [pl.BlockSpec((tm, w), lambda i: (i, 0)) for w, _ in out_rows]
    out_specs += [pl.BlockSpec(s, lambda i: (0, 0)) for s in out_accs]
    out_shape = [jax.ShapeDtypeStruct((t, w), dt) for w, dt in out_rows]
    out_shape += [jax.ShapeDtypeStruct(s, F32) for s in out_accs]
    operands = [r[0] for r in rows] + list(consts)
    aliases = {}
    if into is not None:
        target, cb = into
        in_specs.append(pl.BlockSpec(memory_space=pl.ANY))
        operands.append(target)
        out_specs[0] = pl.BlockSpec((tm, out_rows[0][0]), lambda i: (i, cb))
        out_shape[0] = jax.ShapeDtypeStruct(target.shape, target.dtype)
        aliases = {len(operands) - 1: 0}
    if new_wide is not None:
        width, cb = new_wide
        out_specs[0] = pl.BlockSpec((tm, out_rows[0][0]), lambda i: (i, cb))
        out_shape[0] = jax.ShapeDtypeStruct((t, width), out_rows[0][1])
    return pl.pallas_call(
        body, name=name, grid=(t // tm,), in_specs=in_specs, out_specs=out_specs, out_shape=out_shape,
        input_output_aliases=aliases, compiler_params=_params(("arbitrary",)),
    )(*operands)


def _full(a):
    return (a, a.shape[1], 0)


HGRN_TIME_BLOCK = 256
SSD_TIME_BLOCK = 512


def _time_block(t, most=HGRN_TIME_BLOCK):
    return _pick(t, tuple(b for b in (512, 256, 128, 64) if b <= most))


def _quarters(ref):
    return [ref[:, seg * D:(seg + 1) * D] for seg in range(4)]


def hgrn_forward(proj, lb, gn):
    t = proj.shape[0]
    tb = _time_block(t)
    nb = t // tb

    def body(qfig_ref, lb_ref, gn_ref, o_ref, st_ref, state):
        @pl.when(pl.program_id(0) == 0)
        def _():
            state[...] = jnp.zeros_like(state)

        st = state[...]
        st_ref[...] = st
        out, st_new = hgrn_block(*_quarters(qfig_ref), st, lb_ref[...], gn_ref[...])
        o_ref[...] = out.astype(o_ref.dtype)
        state[...] = st_new

    return pl.pallas_call(
        body, name="hgrn_forward", grid=(nb,),
        in_specs=[pl.BlockSpec((tb, 4 * D), lambda j: (j, 0)),
                  pl.BlockSpec((1, D), lambda j: (0, 0)), pl.BlockSpec((1, LANES), lambda j: (0, 0))],
        out_specs=[pl.BlockSpec((tb, D), lambda j: (j, 0)),
                   pl.BlockSpec((None, N_HEADS_A, LANES, LANES), lambda j: (j, 0, 0, 0))],
        out_shape=[jax.ShapeDtypeStruct((t, D), BF16),
                   jax.ShapeDtypeStruct((nb, N_HEADS_A, LANES, LANES), F32)],
        scratch_shapes=[pltpu.VMEM((N_HEADS_A, LANES, LANES), F32)],
        compiler_params=_params(("arbitrary",)),
    )(proj, lb, gn)


def hgrn_backward(proj, states, d_out, lb, gn, d_proj):
    t = proj.shape[0]
    tb = _time_block(t)
    nb = t // tb

    def body(qfig_ref, st_ref, do_ref, lb_ref, gn_ref, _, dqfig_ref, dlb_ref, dgn_ref, d_state):
        @pl.when(pl.program_id(0) == 0)
        def _():
            d_state[...] = jnp.zeros_like(d_state)
            dlb_ref[...] = jnp.zeros_like(dlb_ref)
            dgn_ref[...] = jnp.zeros_like(dgn_ref)

        _, vjp = jax.vjp(hgrn_block, *_quarters(qfig_ref), st_ref[...], lb_ref[...], gn_ref[...])
        dq, df, di, dg, dst, dlb, dgn = vjp((do_ref[...], d_state[...]))
        for seg, val in enumerate((dq, df, di, dg)):
            dqfig_ref[:, seg * D:(seg + 1) * D] = val.astype(dqfig_ref.dtype)
        d_state[...] = dst
        dlb_ref[...] += dlb
        dgn_ref[...] += dgn

    rev = lambda j: nb - 1 - j
    return pl.pallas_call(
        body, name="hgrn_backward", grid=(nb,),
        in_specs=[pl.BlockSpec((tb, 4 * D), lambda j: (rev(j), 0)),
                  pl.BlockSpec((None, N_HEADS_A, LANES, LANES), lambda j: (rev(j), 0, 0, 0)),
                  pl.BlockSpec((tb, D), lambda j: (rev(j), 0)),
                  pl.BlockSpec((1, D), lambda j: (0, 0)), pl.BlockSpec((1, LANES), lambda j: (0, 0)),
                  pl.BlockSpec(memory_space=pl.ANY)],
        out_specs=[pl.BlockSpec((tb, 4 * D), lambda j: (rev(j), 0)),
                   pl.BlockSpec((1, D), lambda j: (0, 0)), pl.BlockSpec((1, LANES), lambda j: (0, 0))],
        out_shape=[jax.ShapeDtypeStruct(d_proj.shape, d_proj.dtype), jax.ShapeDtypeStruct((1, D), F32),
                   jax.ShapeDtypeStruct((1, LANES), F32)],
        input_output_aliases={5: 0},
        scratch_shapes=[pltpu.VMEM((N_HEADS_A, LANES, LANES), F32)],
        compiler_params=_params(("arbitrary",)),
    )(proj, states, d_out, lb, gn, d_proj)


def _ssd_in_specs(tb, tmap):
    return [pl.BlockSpec((tb, 512), lambda g, j: (tmap(j), g)),
            pl.BlockSpec((tb, LANES), lambda g, j: (tmap(j), 16 + g)),
            pl.BlockSpec((tb, LANES), lambda g, j: (tmap(j), 20 + g)),
            pl.BlockSpec((tb, LANES), lambda g, j: (tmap(j), COL_DT // LANES)),
            pl.BlockSpec((tb, 512), lambda g, j: (tmap(j), COL_Z // 512 + g))]


def ssd_forward(xc, proj, dtb, alog, dsk, nw):
    t = proj.shape[0]
    tb = _time_block(t, SSD_TIME_BLOCK)
    nb = t // tb

    def body(x_ref, b_ref, c_ref, dt_ref, z_ref, dtb_ref, alog_ref, dsk_ref, nw_ref, o_ref, st_ref, state):
        @pl.when(pl.program_id(1) == 0)
        def _():
            state[...] = jnp.zeros_like(state)

        st = state[...]
        st_ref[...] = st
        out, st_new = ssd_block(x_ref[...], b_ref[...], c_ref[...], dt_ref[...], z_ref[...], st,
                                dtb_ref[...], alog_ref[...], dsk_ref[...], nw_ref[...], ssd_consts(pl.program_id(0)))
        o_ref[...] = out.astype(o_ref.dtype)
        state[...] = st_new

    vec = pl.BlockSpec((1, 512), lambda g, j: (0, g))
    heads = pl.BlockSpec((1, LANES), lambda g, j: (0, 0))
    return pl.pallas_call(
        body, name="ssd_forward", grid=(N_GROUPS_B, nb),
        in_specs=_ssd_in_specs(tb, lambda j: j) + [heads, vec, vec, vec],
        out_specs=[pl.BlockSpec((tb, 512), lambda g, j: (j, g)),
                   pl.BlockSpec((None, None, LANES, 512), lambda g, j: (j, g, 0, 0))],
        out_shape=[jax.ShapeDtypeStruct((t, B_INNER), BF16),
                   jax.ShapeDtypeStruct((nb, N_GROUPS_B, LANES, 512), F32)],
        scratch_shapes=[pltpu.VMEM((LANES, 512), F32)],
        compiler_params=_params(("arbitrary", "arbitrary")),
    )(xc, xc, xc, proj, proj, dtb, alog, dsk, nw)


def ssd_backward(xc, proj, states, d_out, dtb, alog, dsk, nw, d_proj):
    t = proj.shape[0]
    tb = _time_block(t, SSD_TIME_BLOCK)
    nb = t // tb
    rev = lambda j: nb - 1 - j

    def body(x_ref, b_ref, c_ref, dt_ref, z_ref, st_ref, do_ref, dtb_ref, alog_ref, dsk_ref, nw_ref, _,
             dx_ref, db_ref, dc_ref, ddt_ref, dz_ref, ddtb_ref, dalog_ref, ddsk_ref, dnw_ref, d_state):
        accs = (ddtb_ref, dalog_ref, ddsk_ref, dnw_ref)

        @pl.when(pl.program_id(1) == 0)
        def _():
            d_state[...] = jnp.zeros_like(d_state)
            for ref in accs:
                ref[...] = jnp.zeros_like(ref)

        cs = ssd_consts(pl.program_id(0))
        fn = lambda *a: ssd_block(*a, cs)
        _, vjp = jax.vjp(fn, x_ref[...], b_ref[...], c_ref[...], dt_ref[...], z_ref[...], st_ref[...],
                         dtb_ref[...], alog_ref[...], dsk_ref[...], nw_ref[...])
        dx, db, dc, ddt, dz, dst, *dpar = vjp((do_ref[...], d_state[...]))
        dx_ref[...] = dx
        db_ref[...] = db
        dc_ref[...] = dc
        ddt_ref[...] = ddt
        dz_ref[...] = dz.astype(dz_ref.dtype)
        d_state[...] = dst
        for ref, val in zip(accs, dpar, strict=True):
            ref[...] += val

    vec = pl.BlockSpec((1, 512), lambda g, j: (0, g))
    heads = pl.BlockSpec((1, LANES), lambda g, j: (0, 0))
    acc = pl.BlockSpec((None, 1, 512), lambda g, j: (g, 0, 0))
    acc_heads = pl.BlockSpec((None, 1, LANES), lambda g, j: (g, 0, 0))
    return pl.pallas_call(
        body, name="ssd_backward", grid=(N_GROUPS_B, nb),
        in_specs=_ssd_in_specs(tb, rev)
        + [pl.BlockSpec((None, None, LANES, 512), lambda g, j: (rev(j), g, 0, 0)),
           pl.BlockSpec((tb, 512), lambda g, j: (rev(j), g))] + [heads, vec, vec, vec] + [pl.BlockSpec(memory_space=pl.ANY)],
        out_specs=[pl.BlockSpec((tb, 512), lambda g, j: (rev(j), g)),
                   pl.BlockSpec((tb, LANES), lambda g, j: (rev(j), g)),
                   pl.BlockSpec((tb, LANES), lambda g, j: (rev(j), g)),
                   pl.BlockSpec((None, tb, LANES), lambda g, j: (g, rev(j), 0)),
                   pl.BlockSpec((tb, 512), lambda g, j: (rev(j), COL_Z // 512 + g)), acc_heads, acc, acc, acc],
        out_shape=[jax.ShapeDtypeStruct((t, B_INNER), F32), jax.ShapeDtypeStruct((t, 512), F32),
                   jax.ShapeDtypeStruct((t, 512), F32), jax.ShapeDtypeStruct((N_GROUPS_B, t, LANES), F32),
                   jax.ShapeDtypeStruct(d_proj.shape, d_proj.dtype)]
        + [jax.ShapeDtypeStruct((N_GROUPS_B, 1, LANES), F32)] + [jax.ShapeDtypeStruct((N_GROUPS_B, 1, 512), F32)] * 3,
        input_output_aliases={11: 4},
        scratch_shapes=[pltpu.VMEM((LANES, 512), F32)],
        compiler_params=_params(("arbitrary", "arbitrary")),
    )(xc, xc, xc, proj, proj, states, d_out, dtb, alog, dsk, nw, d_proj)


CONV_HALO = 8


def _shift_down(halo_then_tile, s, tm):
    if s == 0:
        return halo_then_tile[CONV_HALO:CONV_HALO + tm]
    return pltpu.roll(halo_then_tile, s, 0)[CONV_HALO:CONV_HALO + tm]


def _conv_pre(cur, prev, w, b, tm):
    stacked = jnp.concatenate([prev, cur], axis=0)
    taps = [_shift_down(stacked, 3 - j, tm) for j in range(4)]
    pre = b + taps[0] * w[0:1] + taps[1] * w[1:2] + taps[2] * w[2:3] + taps[3] * w[3:4]
    return pre, taps


def _conv_specs(t, tm):
    per = tm // CONV_HALO
    cur = pl.BlockSpec((tm, CONV_DIM), lambda i: (i, COL_XBC // CONV_DIM))
    prev = pl.BlockSpec((CONV_HALO, CONV_DIM), lambda i: (jnp.maximum(i * per - 1, 0), COL_XBC // CONV_DIM))
    return cur, prev


def conv_forward(proj, w, b):
    t = proj.shape[0]
    tm = _pick(t, (256, 128, 64))

    def body(cur_ref, prev_ref, w_ref, b_ref, o_ref):
        prev = jnp.where(pl.program_id(0) == 0, 0.0, prev_ref[...])
        pre, _ = _conv_pre(cur_ref[...], prev, w_ref[...], b_ref[...], tm)
        o_ref[...] = silu(pre)

    cur, prev = _conv_specs(t, tm)
    return pl.pallas_call(
        body, name="conv_forward", grid=(t // tm,),
        in_specs=[cur, prev, pl.BlockSpec((4, CONV_DIM), lambda i: (0, 0)), pl.BlockSpec((1, CONV_DIM), lambda i: (0, 0))],
        out_specs=pl.BlockSpec((tm, CONV_DIM), lambda i: (i, 0)),
        out_shape=jax.ShapeDtypeStruct((t, CONV_DIM), F32),
        compiler_params=_params(("arbitrary",)),
    )(proj, proj, w, b)


def conv_backward(proj, dx, db_, dc_, w, b, d_proj):
    t = proj.shape[0]
    tm = _pick(t, (256, 128, 64))
    per = tm // CONV_HALO
    nt = t // tm
    rev = lambda i: nt - 1 - i

    def body(cur_ref, prev_ref, dx_ref, dbm_ref, dcm_ref, w_ref, b_ref, _, o_ref, dw_ref, dbias_ref, later):
        @pl.when(pl.program_id(0) == 0)
        def _():
            dw_ref[...] = jnp.zeros_like(dw_ref)
            dbias_ref[...] = jnp.zeros_like(dbias_ref)
            later[...] = jnp.zeros_like(later)

        first_tile = pl.program_id(0) == nt - 1
        for lo, hi, src in ((0, B_INNER, dx_ref), (B_INNER, B_INNER + 512, dbm_ref), (B_INNER + 512, CONV_DIM, dcm_ref)):
            cols = slice(lo, hi)
            prev = jnp.where(first_tile, 0.0, prev_ref[:, cols])
            w_ = w_ref[:, cols]
            pre, taps = _conv_pre(cur_ref[:, cols], prev, w_, b_ref[:, cols], tm)
            sg = sigmoid(pre)
            dpre = src[...] * (sg * (1.0 + pre * (1.0 - sg)))
            dbias_ref[:, cols] += jnp.sum(dpre, axis=0, keepdims=True)
            for j in range(4):
                dw_ref[j:j + 1, cols] += jnp.sum(dpre * taps[j], axis=0, keepdims=True)
            stacked = jnp.concatenate([dpre, later[:, cols]], axis=0)
            acc = dpre * w_[3:4]
            for j in range(3):
                acc = acc + pltpu.roll(stacked, tm + CONV_HALO - (3 - j), 0)[0:tm] * w_[j:j + 1]
            o_ref[:, cols] = acc.astype(o_ref.dtype)
            later[:, cols] = dpre[0:CONV_HALO]

    row = lambda w_: pl.BlockSpec((tm, w_), lambda i: (rev(i), 0))
    whole = lambda r: pl.BlockSpec((r, CONV_DIM), lambda i: (0, 0))
    return pl.pallas_call(
        body, name="conv_backward", grid=(nt,),
        in_specs=[pl.BlockSpec((tm, CONV_DIM), lambda i: (rev(i), COL_XBC // CONV_DIM)),
                  pl.BlockSpec((CONV_HALO, CONV_DIM), lambda i: (jnp.maximum(rev(i) * per - 1, 0), COL_XBC // CONV_DIM)),
                  row(B_INNER), row(512), row(512), whole(4), whole(1), pl.BlockSpec(memory_space=pl.ANY)],
        out_specs=[pl.BlockSpec((tm, CONV_DIM), lambda i: (rev(i), COL_XBC // CONV_DIM)), whole(4), whole(1)],
        out_shape=[jax.ShapeDtypeStruct(d_proj.shape, d_proj.dtype), jax.ShapeDtypeStruct((4, CONV_DIM), F32),
                   jax.ShapeDtypeStruct((1, CONV_DIM), F32)],
        input_output_aliases={7: 0},
        scratch_shapes=[pltpu.VMEM((CONV_HALO, CONV_DIM), F32)],
        compiler_params=_params(("arbitrary",)),
    )(proj, proj, dx, db_, dc_, w, b, d_proj)


def stage_modulate(x, sc, sh):
    return _ln(x) * (1.0 + sc) + sh


def stage_merge(ga, gb, ya, yb):
    return sigmoid(ga) * ya + sigmoid(gb) * yb


def stage_post_mixer(x, h, g1, ln_g, ln_b, sc2, sh2):
    x1 = _ln(ALPHA * x + g1 * h) * ln_g + ln_b
    return x1, _ln(x1) * (1.0 + sc2) + sh2


def stage_swiglu(a, b):
    return silu(a) * b


def gate_up(ab):
    w = FFN_SHARD
    return (jnp.concatenate([ab[:, 2 * w * k:2 * w * k + w] for k in range(4)], axis=1),
            jnp.concatenate([ab[:, 2 * w * k + w:2 * w * (k + 1)] for k in range(4)], axis=1))


def per_chip(gate, up):
    w = FFN_SHARD
    return jnp.concatenate([part[:, w * k:w * (k + 1)] for k in range(4) for part in (gate, up)], axis=1)


def stage_loss(x1, hf, tgt, g2, ln_g, ln_b):
    x2 = _ln(ALPHA * x1 + g2 * hf) * ln_g + ln_b
    return 0.5 * jnp.sum(jnp.mean(jnp.square(x2 - tgt), axis=-1, keepdims=True), axis=0, keepdims=True)


def local_step(x, tgt, mod, wts, small, early=None, mid=None, late=None, last=None):
    sh1, sc1, g1, sh2, sc2, g2 = mod
    lb, gn, conv_w, conv_b, dtb, alog, dsk, nw, ln1_g, ln1_b, ln2_g, ln2_b = small
    vec = (1, D)

    (u1,) = rowwise("modulate1", lambda r, c: ((stage_modulate(r[0], *c),), ()), [_full(x)], [sc1, sh1], [(D, BF16)])
    w_in = wts.input_projection(u1)
    proj = matmul(u1, w_in, "nn", F32, "in_proj")
    ya_in, st_a = hgrn_forward(proj, lb, gn + wts.start_rest(proj)[0:1])
    xc = conv_forward(proj, conv_w, conv_b)
    w_a, w_b, w_o, w_gu, w_d = wts.rest(xc)
    yb_in, st_b = ssd_forward(xc, proj, dtb, alog, dsk, nw)
    ya = matmul(ya_in, w_a, "nn", F32, "branch_a")
    yb = matmul(yb_in, w_b, "nn", F32, "branch_b")
    gate_rows = [(proj, D, COL_GA // D), (proj, D, COL_GB // D), _full(ya), _full(yb)]
    (merged,) = rowwise("merge", lambda r, c: ((stage_merge(*r),), ()), gate_rows, [], [(D, BF16)])
    h = matmul(merged, w_o, "nn", F32, "out_proj")
    post_consts = [g1, ln1_g, ln1_b, sc2, sh2]
    x1, u2 = rowwise("post_mixer", lambda r, c: (stage_post_mixer(*r, *c), ()), [_full(x), _full(h)], post_consts,
                     [(D, F32), (D, BF16)])
    ab = matmul(u2, w_gu, "nt", F32, "ffn_in")
    (p,) = rowwise("swiglu", lambda r, c: ((stage_swiglu(*gate_up(r[0])),), ()), [_full(ab)], [], [(D_FF, BF16)],
                   tm_max=256)
    hf = matmul(p, w_d, "nn", F32, "ffn_out")

    def loss_bwd(r, c):
        loss, vjp = jax.vjp(stage_loss, *r, *c)
        dx1, dhf, _, dg2, dlg, dlb_ = vjp(jnp.ones((1, 1), F32))
        return (dx1, dhf), (loss, dg2, dlg, dlb_)

    dx1, dhf, loss, dg2, dln2_g, dln2_b = rowwise(
        "loss_backward", loss_bwd, [_full(x1), _full(hf), _full(tgt)], [g2, ln2_g, ln2_b],
        [(D, F32), (D, BF16)], [(1, 1), vec, vec, vec])
    dp = matmul(dhf, w_d, "nt", F32, "ffn_out_dx")
    dw_d = matmul(p, dhf, "tn", F32, "ffn_out_dw")

    def swiglu_bwd(r, c):
        _, vjp = jax.vjp(stage_swiglu, *gate_up(r[0]))
        return (per_chip(*vjp(r[1])),), ()

    (dab,) = rowwise("swiglu_backward", swiglu_bwd, [_full(ab), _full(dp)], [], [(2 * D_FF, BF16)], tm_max=256)
    du2 = matmul(dab, w_gu, "nn", F32, "ffn_in_dx")
    dw_gu = matmul(dab, u2, "tn", F32, "ffn_in_dw")

    def post_bwd(r, c):
        _, vjp = jax.vjp(stage_post_mixer, r[0], r[1], *c)
        dx, dh, *dc = vjp((r[2], r[3]))
        return (dx, dh), tuple(dc)

    dx_a, dh, dg1, dln1_g, dln1_b, dsc2, dsh2 = rowwise(
        "post_mixer_backward", post_bwd, [_full(x), _full(h), _full(dx1), _full(du2)], post_consts,
        [(D, F32), (D, BF16)], [vec] * 5)
    dmerged = matmul(dh, w_o, "nt", F32, "out_proj_dx")
    dw_o = matmul(merged, dh, "tn", F32, "out_proj_dw")

    def merge_bwd(r, c):
        _, vjp = jax.vjp(stage_merge, *r[:4])
        dga, dgb, dya, dyb = vjp(r[4])
        return (jnp.concatenate([dga, dgb], axis=1), dya, dyb), ()

    dproj, dya, dyb = rowwise("merge_backward", merge_bwd, gate_rows + [_full(dmerged)], [],
                              [(2 * D, BF16), (D, BF16), (D, BF16)], new_wide=(IN_PAD, COL_GA // (2 * D)))
    dya_in = matmul(dya, w_a, "nt", F32, "branch_a_dx")
    dw_a = matmul(ya_in, dya, "tn", F32, "branch_a_dw")
    dyb_in = matmul(dyb, w_b, "nt", F32, "branch_b_dx")
    dw_b = matmul(yb_in, dyb, "tn", F32, "branch_b_dw")
    gn_after = gn if early is None else gn + early((dw_a, dw_b, dw_o, dw_gu, dw_d))[0:1]
    dproj, dlb, dgn = hgrn_backward(proj, st_a, dya_in, lb, gn_after, dproj)
    dtb_after = dtb if mid is None else dtb + mid(dlb)[0:1, 0:1]
    dxs, dbm, dcm, ddt, dproj, ddtb, dalog, ddsk, dnw = ssd_backward(xc, proj, st_b, dyb_in, dtb_after, alog, dsk, nw, dproj)
    dproj, dconv_w, dconv_b = conv_backward(proj, dxs, dbm, dcm, conv_w, conv_b, dproj)
    if late is not None:
        late(dconv_b)
    t = x.shape[0]
    tail = jnp.concatenate([jnp.sum(ddt, axis=0).astype(BF16), jnp.zeros((t, IN_PAD - COL_DT - LANES), BF16)], axis=1)
    dproj = lax.dynamic_update_slice(dproj, tail, (0, COL_DT))
    if last is None:
        dw_in, started = matmul(u1, dproj, "tn", F32, "in_proj_dw"), None
    else:
        dw_in, started = None, last(u1, dproj)
    du1 = matmul(dproj, w_in, "nt", F32, "in_proj_dx", after=started)

    def mod_bwd(r, c):
        _, vjp = jax.vjp(stage_modulate, r[0], *c)
        dx, dsc, dsh = vjp(r[1])
        return (dx + r[2],), (dsc, dsh)

    grad_x, dsc1, dsh1 = rowwise("modulate1_backward", mod_bwd, [_full(x), _full(du1), _full(dx_a)], [sc1, sh1],
                                 [(D, F32)], [vec, vec])
    d_mod = (dsh1, dsc1, dg1, dsh2, dsc2, dg2)
    d_wts = (dw_in, dw_a, dw_b, dw_o, dw_gu, dw_d)
    d_small = (dlb, dgn, dconv_w, dconv_b, jnp.sum(ddtb, axis=0),
               dalog.reshape(1, B_INNER), ddsk.reshape(1, B_INNER), dnw.reshape(1, B_INNER),
               dln1_g, dln1_b, dln2_g, dln2_b)
    return loss, grad_x, d_mod, d_wts, d_small


HBM = pl.BlockSpec(memory_space=pltpu.HBM)
SEM = pl.BlockSpec(memory_space=pltpu.SEMAPHORE)
DATAFLOW = pltpu.SideEffectType.DATAFLOW_SIDE_EFFECTING


def _place():
    return lax.axis_index("x"), lax.axis_index("y"), lax.axis_index("c")


def _other_chips(x, y):
    return [(1 - x, y), (x, 1 - y), (1 - x, 1 - y)]


def _remote(src, dst, send_sem, recv_sem, device):
    return pltpu.make_async_remote_copy(src_ref=src, dst_ref=dst, send_sem=send_sem, recv_sem=recv_sem,
                                        device_id=device, device_id_type=MESH)


def gather_rows(v, name):
    n = v.shape[1]

    def body(v_ref, out_ref, send_sems, recv_sems, local_sem):
        x, y, c = _place()
        mine = pltpu.make_async_copy(v_ref, out_ref.at[4 * x + 2 * y + c], local_sem)
        mine.start()
        sends, recvs = [], []
        for m in range(1, 8):
            px = 1 - x if m & 4 else x
            py = 1 - y if m & 2 else y
            pc = 1 - c if m & 1 else c
            sends.append(_remote(v_ref, out_ref.at[4 * x + 2 * y + c], send_sems.at[m - 1], recv_sems.at[m - 1], (px, py, pc)))
            recvs.append(_remote(v_ref, out_ref.at[4 * px + 2 * py + pc], send_sems.at[m - 1], recv_sems.at[m - 1], (px, py, pc)))
        for cp in sends:
            cp.start()
        for cp in recvs:
            cp.wait_recv()
        for cp in sends:
            cp.wait_send()
        mine.wait()

    return pl.pallas_call(
        body, name=name, in_specs=[HBM], out_specs=HBM,
        out_shape=jax.ShapeDtypeStruct((8, 1, n), v.dtype),
        scratch_shapes=[pltpu.SemaphoreType.DMA((7,)), pltpu.SemaphoreType.DMA((7,)), pltpu.SemaphoreType.DMA],
    )(v)


def exchange_rows(part, name):
    w = part.shape[2]

    def body(p_ref, out_ref, send_sems, recv_sems, local_sem):
        x, y, c = _place()
        k = 2 * x + y
        mine = pltpu.make_async_copy(p_ref.at[4 * x + 2 * y + c], out_ref.at[k], local_sem)
        mine.start()
        sends, recvs = [], []
        for j, (px, py) in enumerate(_other_chips(x, y)):
            sends.append(_remote(p_ref.at[4 * px + 2 * py + c], out_ref.at[k], send_sems.at[j], recv_sems.at[j], (px, py, c)))
            recvs.append(_remote(p_ref.at[4 * px + 2 * py + c], out_ref.at[2 * px + py], send_sems.at[j], recv_sems.at[j], (px, py, c)))
        for cp in sends:
            cp.start()
        for cp in recvs:
            cp.wait_recv()
        for cp in sends:
            cp.wait_send()
        mine.wait()

    return pl.pallas_call(
        body, name=name, in_specs=[HBM], out_specs=HBM,
        out_shape=jax.ShapeDtypeStruct((4, 1, w), part.dtype),
        scratch_shapes=[pltpu.SemaphoreType.DMA((3,)), pltpu.SemaphoreType.DMA((3,)), pltpu.SemaphoreType.DMA],
    )(part)


def _half_of_slot(ref, rows, px, py, pc):
    return ref.at[2 * px + py, pl.ds(pc * (rows // 2), rows // 2), :]


def gather_start(shards, after, tag):
    n = len(shards)

    def body(*refs):
        w_refs, land_refs = refs[:n], refs[n:2 * n]
        send_sems, recv_sems = refs[2 * n + 1], refs[2 * n + 2]
        token = refs[-1]
        x, y, c = _place()
        for i in range(n):
            rows = shards[i].shape[0]
            for j, (px, py) in enumerate(_other_chips(x, y)):
                _remote(w_refs[i].at[pl.ds(c * (rows // 2), rows // 2), :], _half_of_slot(land_refs[i], rows, x, y, c),
                        send_sems.at[j * n + i], recv_sems.at[j * n + i], (px, py, c)).start()
        token[...] = jnp.zeros_like(token)

    hbm = lambda a: pltpu.with_memory_space_constraint(a, pltpu.HBM)
    lands = [lax.empty((4,) + s.shape, s.dtype) for s in shards]
    dma = pltpu.SemaphoreType.DMA
    return pl.pallas_call(
        body, name="gather_start_" + tag,
        out_shape=(dma((3 * n,)), dma((3 * n,)),
                   *[pltpu.HBM(a.shape, a.dtype) for a in list(shards) + lands], jax.ShapeDtypeStruct((8, LANES), F32)),
        in_specs=[HBM] * (2 * n) + [pl.BlockSpec(memory_space=pl.ANY)],
        out_specs=(SEM, SEM, *[HBM] * (2 * n), pl.BlockSpec(memory_space=pltpu.VMEM)),
        input_output_aliases={i: 2 + i for i in range(2 * n)},
        compiler_params=pltpu.CompilerParams(has_side_effects=DATAFLOW),
    )(*[hbm(a) for a in list(shards) + lands], after)


def gather_wait(send_sems, recv_sems, shards, lands, after, tag):
    n = len(shards)

    def body(*refs):
        w_refs, land_refs = refs[:n], refs[n:2 * n]
        send_ref, recv_ref = refs[2 * n], refs[2 * n + 1]
        x, y, c = _place()
        for i in range(n):
            rows = shards[i].shape[0]
            for j, (px, py) in enumerate(_other_chips(x, y)):
                cp = _remote(w_refs[i].at[pl.ds(c * (rows // 2), rows // 2), :], _half_of_slot(land_refs[i], rows, px, py, c),
                             send_ref.at[j * n + i], recv_ref.at[j * n + i], (px, py, c))
                cp.wait_send()
                cp.wait_recv()

    out = pl.pallas_call(
        body, name="gather_wait_" + tag,
        out_shape=tuple(pltpu.HBM(a.shape, a.dtype) for a in list(shards) + list(lands)),
        in_specs=[HBM] * (2 * n) + [SEM, SEM, pl.BlockSpec(memory_space=pl.ANY)], out_specs=tuple([HBM] * (2 * n)),
        input_output_aliases={i: i for i in range(2 * n)},
        compiler_params=pltpu.CompilerParams(has_side_effects=DATAFLOW),
    )(*shards, *lands, send_sems, recv_sems, after)
    return list(out[:n]), list(out[n:])


def forward_start(lands, tag):
    n = len(lands)

    def body(*refs):
        land_refs = refs[:n]
        send_sems, recv_sems = refs[n], refs[n + 1]
        token = refs[-1]
        x, y, c = _place()
        for i in range(n):
            rows = lands[i].shape[1]
            for j, (px, py) in enumerate(_other_chips(x, y)):
                mine = _half_of_slot(land_refs[i], rows, px, py, c)
                _remote(mine, mine, send_sems.at[j * n + i], recv_sems.at[j * n + i], (x, y, 1 - c)).start()
        token[...] = jnp.zeros_like(token)

    dma = pltpu.SemaphoreType.DMA
    return pl.pallas_call(
        body, name="forward_start_" + tag,
        out_shape=(dma((3 * n,)), dma((3 * n,)), *[pltpu.HBM(a.shape, a.dtype) for a in lands],
                   jax.ShapeDtypeStruct((8, LANES), F32)),
        in_specs=[HBM] * n, out_specs=(SEM, SEM, *[HBM] * n, pl.BlockSpec(memory_space=pltpu.VMEM)),
        input_output_aliases={i: 2 + i for i in range(n)},
        compiler_params=pltpu.CompilerParams(has_side_effects=DATAFLOW),
    )(*lands)


def forward_wait(started, after, tag):
    send_sems, recv_sems, *rest = started
    lands = rest[:-1]
    n = len(lands)

    def body(*refs):
        land_refs = refs[:n]
        send_ref, recv_ref = refs[n], refs[n + 1]
        x, y, c = _place()
        for i in range(n):
            rows = lands[i].shape[1]
            for j, (px, py) in enumerate(_other_chips(x, y)):
                cp = _remote(_half_of_slot(land_refs[i], rows, px, py, c), _half_of_slot(land_refs[i], rows, px, py, 1 - c),
                             send_ref.at[j * n + i], recv_ref.at[j * n + i], (x, y, 1 - c))
                cp.wait_send()
                cp.wait_recv()

    out = pl.pallas_call(
        body, name="forward_wait_" + tag,
        out_shape=tuple(pltpu.HBM(a.shape, a.dtype) for a in lands),
        in_specs=[HBM] * n + [SEM, SEM, pl.BlockSpec(memory_space=pl.ANY)], out_specs=tuple([HBM] * n),
        input_output_aliases={i: i for i in range(n)},
        compiler_params=pltpu.CompilerParams(has_side_effects=DATAFLOW),
    )(*lands, send_sems, recv_sems, after)
    return list(out)


def pair_start(slabs, tag):
    n = len(slabs)

    def body(*refs):
        g_refs, land_refs = refs[:n], refs[n:2 * n]
        send_sems, recv_sems = refs[2 * n], refs[2 * n + 1]
        token = refs[-1]
        x, y, c = _place()
        for i in range(n):
            hr = slabs[i].shape[1] // 2
            _remote(g_refs[i].at[:, pl.ds((1 - c) * hr, hr), :], land_refs[i], send_sems.at[i], recv_sems.at[i],
                    (x, y, 1 - c)).start()
        token[...] = jnp.zeros_like(token)

    hbm = lambda a: pltpu.with_memory_space_constraint(a, pltpu.HBM)
    lands = [lax.empty((4, s.shape[1] // 2, s.shape[2]), s.dtype) for s in slabs]
    dma = pltpu.SemaphoreType.DMA
    return pl.pallas_call(
        body, name="pair_start_" + tag,
        out_shape=(dma((n,)), dma((n,)), *[pltpu.HBM(a.shape, a.dtype) for a in list(slabs) + lands],
                   jax.ShapeDtypeStruct((8, LANES), F32)),
        in_specs=[HBM] * (2 * n), out_specs=(SEM, SEM, *[HBM] * (2 * n), pl.BlockSpec(memory_space=pltpu.VMEM)),
        input_output_aliases={i: 2 + i for i in range(2 * n)},
        compiler_params=pltpu.CompilerParams(has_side_effects=DATAFLOW),
    )(*[hbm(a) for a in list(slabs) + lands])


def pair_wait(started, after, tag):
    send_sems, recv_sems, *rest = started
    n = (len(rest) - 1) // 2
    slabs, lands = rest[:n], rest[n:2 * n]

    def body(*refs):
        g_refs, land_refs = refs[:n], refs[n:2 * n]
        send_ref, recv_ref = refs[2 * n], refs[2 * n + 1]
        x, y, c = _place()
        for i in range(n):
            hr = slabs[i].shape[1] // 2
            cp = _remote(g_refs[i].at[:, pl.ds((1 - c) * hr, hr), :], land_refs[i], send_ref.at[i], recv_ref.at[i], (x, y, 1 - c))
            cp.wait_send()
            cp.wait_recv()

    out = pl.pallas_call(
        body, name="pair_wait_" + tag,
        out_shape=tuple(pltpu.HBM(a.shape, a.dtype) for a in list(slabs) + list(lands)),
        in_specs=[HBM] * (2 * n) + [SEM, SEM, pl.BlockSpec(memory_space=pl.ANY)], out_specs=tuple([HBM] * (2 * n)),
        input_output_aliases={i: i for i in range(2 * n)},
        compiler_params=pltpu.CompilerParams(has_side_effects=DATAFLOW),
    )(*slabs, *lands, send_sems, recv_sems, after)
    return list(out[:n]), list(out[n:])


def _tile2(rows, cols):
    fits = lambda r, c: r * c * 4 <= BLOCK_BYTES
    if fits(rows, cols):
        return rows, cols
    tiles = [(r, cols) for r in (1024, 512, 256, 128, 64) if rows % r == 0 and fits(r, cols)]
    tiles += [(rows, cols // k) for k in (2, 3, 4, 6, 8, 12, 16) if cols % (k * LANES) == 0 and fits(rows, cols // k)]
    return max(tiles, key=lambda t: t[0] * t[1])


def pair_add(g, p, c, name):
    _, hr, cols = p.shape
    tm, tc = _tile2(hr, cols)
    per = hr // tm

    def body(c_ref, g_ref, p_ref, o_ref):
        o_ref[...] = (g_ref[...] + p_ref[...]).astype(o_ref.dtype)

    return pl.pallas_call(
        body, name=name,
        grid_spec=pltpu.PrefetchScalarGridSpec(
            num_scalar_prefetch=1, grid=(4, per, cols // tc),
            in_specs=[pl.BlockSpec((None, tm, tc), lambda k, i, j, c_ref: (k, c_ref[0] * per + i, j)),
                      pl.BlockSpec((None, tm, tc), lambda k, i, j, c_ref: (k, i, j))],
            out_specs=pl.BlockSpec((None, tm, tc), lambda k, i, j, c_ref: (k, i, j))),
        out_shape=jax.ShapeDtypeStruct((4, hr, cols), BF16),
        compiler_params=_params(("arbitrary", "arbitrary", "arbitrary")),
    )(c.reshape(1).astype(jnp.int32), g, p)


def scatter_start(sums, tag):
    n = len(sums)

    def body(*refs):
        s_refs, land_refs = refs[:n], refs[n:2 * n]
        send_sems, recv_sems = refs[2 * n], refs[2 * n + 1]
        token = refs[-1]
        x, y, c = _place()
        k = 2 * x + y
        for i in range(n):
            for j, (px, py) in enumerate(_other_chips(x, y)):
                _remote(s_refs[i].at[2 * px + py], land_refs[i].at[k], send_sems.at[j * n + i], recv_sems.at[j * n + i],
                        (px, py, c)).start()
        token[...] = jnp.zeros_like(token)

    hbm = lambda a: pltpu.with_memory_space_constraint(a, pltpu.HBM)
    return pl.pallas_call(
        body, name="scatter_start_" + tag,
        out_shape=(pltpu.SemaphoreType.DMA((3 * n,)), pltpu.SemaphoreType.DMA((3 * n,)),
                   *[pltpu.HBM(s.shape, s.dtype) for s in sums], *[pltpu.HBM(s.shape, s.dtype) for s in sums],
                   jax.ShapeDtypeStruct((8, LANES), F32)),
        in_specs=[HBM] * (2 * n), out_specs=(SEM, SEM, *[HBM] * (2 * n), pl.BlockSpec(memory_space=pltpu.VMEM)),
        input_output_aliases={i: 2 + i for i in range(2 * n)},
        compiler_params=pltpu.CompilerParams(has_side_effects=DATAFLOW),
    )(*[hbm(s) for s in sums], *[hbm(lax.empty(s.shape, s.dtype)) for s in sums])


def scatter_wait(started, after, tag):
    send_sems, recv_sems, *rest = started
    n = (len(rest) - 1) // 2
    sums, lands = rest[:n], rest[n:2 * n]

    def body(*refs):
        s_refs, land_refs = refs[:n], refs[n:2 * n]
        send_ref, recv_ref = refs[2 * n], refs[2 * n + 1]
        x, y, c = _place()
        for i in range(n):
            for j, (px, py) in enumerate(_other_chips(x, y)):
                cp = _remote(s_refs[i].at[2 * px + py], land_refs[i].at[2 * px + py], send_ref.at[j * n + i],
                             recv_ref.at[j * n + i], (px, py, c))
                cp.wait_send()
                cp.wait_recv()

    out = pl.pallas_call(
        body, name="scatter_wait_" + tag,
        out_shape=tuple(pltpu.HBM(s.shape, s.dtype) for s in sums + lands),
        in_specs=[HBM] * (2 * n) + [SEM, SEM, pl.BlockSpec(memory_space=pl.ANY)], out_specs=tuple([HBM] * (2 * n)),
        input_output_aliases={i: i for i in range(2 * n)},
        compiler_params=pltpu.CompilerParams(has_side_effects=DATAFLOW),
    )(*sums, *lands, send_sems, recv_sems, after)
    return list(out[:n]), list(out[n:])


def sum_chips(landed, own, chip, core, name):
    _, hr, cols = landed.shape
    tm, tc = _tile2(hr, cols)
    per = hr // tm

    def body(idx_ref, l0, l1, l2, l3, own_ref, o_ref):
        mine = own_ref[...].astype(F32)
        v = [jnp.where(idx_ref[0] == k, mine, ref[...].astype(F32)) for k, ref in enumerate((l0, l1, l2, l3))]
        o_ref[...] = ((v[0] + v[1]) + v[2]) + v[3]

    slot = lambda k: pl.BlockSpec((None, tm, tc),
                                  lambda i, j, idx: (jnp.where(idx[0] == k, (k + 1) & 3, k), i, j))
    return pl.pallas_call(
        body, name=name,
        grid_spec=pltpu.PrefetchScalarGridSpec(
            num_scalar_prefetch=1, grid=(per, cols // tc),
            in_specs=[slot(0), slot(1), slot(2), slot(3),
                      pl.BlockSpec((None, tm, tc), lambda i, j, idx: (idx[0], i, j))],
            out_specs=pl.BlockSpec((tm, tc), lambda i, j, idx: (idx[1] * per + i, j))),
        out_shape=jax.ShapeDtypeStruct((2 * hr, cols), F32),
        compiler_params=_params(("arbitrary", "arbitrary")),
    )(jnp.stack([chip, core]).astype(jnp.int32), landed, landed, landed, landed, own)


def _halves_copies(refs, shapes, send_sems, recv_sems):
    x, y, c = _place()
    copies = []
    for i, ref in enumerate(refs):
        hr = shapes[i][0] // 2
        own = ref.at[pl.ds(c * hr, hr), :]
        other = ref.at[pl.ds((1 - c) * hr, hr), :]
        copies.append((_remote(own, own, send_sems.at[i], recv_sems.at[i], (x, y, 1 - c)),
                       _remote(other, other, send_sems.at[i], recv_sems.at[i], (x, y, 1 - c))))
    return copies


def halves_start(bufs):
    n = len(bufs)
    shapes = [b.shape for b in bufs]

    def body(*refs):
        for sent, _ in _halves_copies(refs[:n], shapes, refs[n], refs[n + 1]):
            sent.start()
        refs[-1][...] = jnp.zeros_like(refs[-1])

    hbm = lambda a: pltpu.with_memory_space_constraint(a, pltpu.HBM)
    dma = pltpu.SemaphoreType.DMA
    return pl.pallas_call(
        body, name="halves_start",
        out_shape=(dma((n,)), dma((n,)), *[pltpu.HBM(b.shape, b.dtype) for b in bufs],
                   jax.ShapeDtypeStruct((8, LANES), F32)),
        in_specs=[HBM] * n, out_specs=(SEM, SEM, *[HBM] * n, pl.BlockSpec(memory_space=pltpu.VMEM)),
        input_output_aliases={i: 2 + i for i in range(n)},
        compiler_params=pltpu.CompilerParams(has_side_effects=DATAFLOW),
    )(*[hbm(b) for b in bufs])


def halves_wait(started, after):
    send_sems, recv_sems, *bufs, _ = started
    n = len(bufs)
    shapes = [b.shape for b in bufs]

    def body(*refs):
        for sent, received in _halves_copies(refs[:n], shapes, refs[n], refs[n + 1]):
            received.wait_recv()
            sent.wait_send()

    return pl.pallas_call(
        body, name="halves_wait",
        out_shape=tuple(pltpu.HBM(b.shape, b.dtype) for b in bufs),
        in_specs=[HBM] * n + [SEM, SEM, pl.BlockSpec(memory_space=pl.ANY)], out_specs=tuple([HBM] * n),
        input_output_aliases={i: i for i in range(n)},
        compiler_params=pltpu.CompilerParams(has_side_effects=DATAFLOW),
    )(*bufs, send_sems, recv_sems, after)


def assemble_in_proj(landed, own, chip):
    rows, cols = 128, own.shape[1]

    def body(idx_ref, l0, l1, l2, l3, own_ref, o_ref):
        mine = own_ref[...]
        w = jnp.concatenate([jnp.where(idx_ref[0] == k, mine, ref[...]) for k, ref in enumerate((l0, l1, l2, l3))], axis=1)
        o_ref[...] = jnp.concatenate([w[:, :ORIG_Z], w[:, ORIG_GA:], w[:, ORIG_XBC:ORIG_DT], w[:, ORIG_Z:ORIG_XBC],
                                      w[:, ORIG_DT:ORIG_GA], jnp.zeros((rows, IN_PAD - IN_ORIG), w.dtype)], axis=1)

    slot = lambda k: pl.BlockSpec((None, rows, cols), lambda i, idx: (jnp.where(idx[0] == k, (k + 1) & 3, k), i, 0))
    return pl.pallas_call(
        body, name="assemble_in_proj",
        grid_spec=pltpu.PrefetchScalarGridSpec(
            num_scalar_prefetch=1, grid=(D // rows,),
            in_specs=[slot(0), slot(1), slot(2), slot(3), pl.BlockSpec((rows, cols), lambda i, idx: (i, 0))],
            out_specs=pl.BlockSpec((rows, IN_PAD), lambda i, idx: (i, 0))),
        out_shape=jax.ShapeDtypeStruct((D, IN_PAD), own.dtype),
        compiler_params=_params(("arbitrary",)),
    )(chip.reshape(1).astype(jnp.int32), landed, landed, landed, landed, own)


def rows_exchange(a, name):
    hr = a.shape[0] // 2

    def body(a_ref, out_ref, send_sem, recv_sem):
        x, y, c = _place()
        cp = _remote(a_ref.at[pl.ds((1 - c) * hr, hr), :], out_ref, send_sem, recv_sem, (x, y, 1 - c))
        cp.start()
        cp.wait()

    return pl.pallas_call(
        body, name=name, in_specs=[HBM], out_specs=HBM,
        out_shape=jax.ShapeDtypeStruct((hr, a.shape[1]), a.dtype),
        scratch_shapes=[pltpu.SemaphoreType.DMA, pltpu.SemaphoreType.DMA],
    )(a)


def rows_start(a, tag):
    hr = a.shape[0] // 2

    def body(a_ref, land_ref, send_sem, recv_sem, a_thru, land_thru, token):
        x, y, c = _place()
        _remote(a_ref.at[pl.ds((1 - c) * hr, hr), :], land_ref, send_sem, recv_sem, (x, y, 1 - c)).start()
        token[...] = jnp.zeros_like(token)

    hbm = lambda v: pltpu.with_memory_space_constraint(v, pltpu.HBM)
    dma = pltpu.SemaphoreType.DMA
    return pl.pallas_call(
        body, name="rows_start_" + tag,
        out_shape=(dma(()), dma(()), pltpu.HBM(a.shape, a.dtype), pltpu.HBM((hr, a.shape[1]), a.dtype),
                   jax.ShapeDtypeStruct((8, LANES), F32)),
        in_specs=[HBM, HBM], out_specs=(SEM, SEM, HBM, HBM, pl.BlockSpec(memory_space=pltpu.VMEM)),
        input_output_aliases={0: 2, 1: 3},
        compiler_params=pltpu.CompilerParams(has_side_effects=DATAFLOW),
    )(hbm(a), hbm(lax.empty((hr, a.shape[1]), a.dtype)))


def rows_wait(started, after, tag):
    send_sem, recv_sem, a, land, _ = started
    hr = a.shape[0] // 2

    def body(a_ref, land_ref, send_ref, recv_ref, after_ref, a_thru, got_ref):
        x, y, c = _place()
        cp = _remote(a_ref.at[pl.ds((1 - c) * hr, hr), :], land_ref, send_ref, recv_ref, (x, y, 1 - c))
        cp.wait_send()
        cp.wait_recv()

    return pl.pallas_call(
        body, name="rows_wait_" + tag,
        out_shape=(pltpu.HBM(a.shape, a.dtype), pltpu.HBM(land.shape, land.dtype)),
        in_specs=[HBM, HBM, SEM, SEM, pl.BlockSpec(memory_space=pl.ANY)], out_specs=(HBM, HBM),
        input_output_aliases={0: 0, 1: 1},
        compiler_params=pltpu.CompilerParams(has_side_effects=DATAFLOW),
    )(a, land, send_sem, recv_sem, after)


def split_pair_add(pieces, received, core):
    cols = IN_ORIG // 4
    rows, hr = 128, D // 2
    per = hr // rows
    n_p = len(pieces)

    def body(c_ref, *refs):
        o_ref = refs[-1]
        d = jnp.concatenate([refs[i][...] + refs[n_p + i][...] for i in range(n_p)], axis=1)
        w = jnp.concatenate([d[:, :COL_GA], d[:, COL_Z:COL_DT], d[:, COL_XBC:COL_Z], d[:, COL_DT:COL_DT + 32],
                             d[:, COL_GA:COL_XBC]], axis=1)
        for k in range(4):
            o_ref[k] = w[:, k * cols:(k + 1) * cols].astype(o_ref.dtype)

    return pl.pallas_call(
        body, name="split_pair_add",
        grid_spec=pltpu.PrefetchScalarGridSpec(
            num_scalar_prefetch=1, grid=(per,),
            in_specs=[pl.BlockSpec((rows, p.shape[1]), lambda i, c_ref: (c_ref[0] * per + i, 0)) for p in pieces]
            + [pl.BlockSpec((rows, p.shape[1]), lambda i, c_ref: (i, 0)) for p in received],
            out_specs=pl.BlockSpec((4, rows, cols), lambda i, c_ref: (0, i, 0))),
        out_shape=jax.ShapeDtypeStruct((4, hr, cols), BF16),
        compiler_params=_params(("arbitrary",)),
    )(core.reshape(1).astype(jnp.int32), *pieces, *received)


def ada_prepare(c_all, w_ada, hgrn_lb):
    def body(c_ref, w_ref, lb_ref, mod_ref, row_ref):
        mod_ref[...] = hdot(silu(c_ref[...]), w_ref[...])
        row_ref[...] = sigmoid(lb_ref[0:1, :] - lb_ref[1:2, :])

    return pl.pallas_call(
        body, name="ada_prepare",
        out_shape=[jax.ShapeDtypeStruct((8, w_ada.shape[1]), F32), jax.ShapeDtypeStruct((1, D), F32)],
        compiler_params=pltpu.CompilerParams(vmem_limit_bytes=VMEM_LIMIT),
    )(c_all, w_ada, hgrn_lb)


SMALL_SEGS = (("mod", 6 * D), ("lb", D), ("gnorm", LANES), ("conv_w", 4 * CONV_DIM), ("conv_b", CONV_DIM),
              ("dt_bias", LANES), ("a_log", B_INNER), ("d", B_INNER), ("ssm_norm", B_INNER),
              ("ln1_g", D), ("ln1_b", D), ("ln2_g", D), ("ln2_b", D), ("loss", LANES))
SMALL_PARAMS = ("b_ada", "hgrn_lb", "hgrn_gnorm", "ssm_conv_b", "ssm_dt_bias", "ssm_a_log", "ssm_d", "ssm_norm",
                "ln1_g", "ln1_b", "ln2_g", "ln2_b")


def finalize_small(g_all, c_all, dmod_cols, params, m, v):
    n_p = len(SMALL_PARAMS)
    offs, o = {}, 0
    for nm, width in SMALL_SEGS:
        offs[nm] = (o, width)
        o += width

    def body(*refs):
        g_ref, c_ref, dm_ref = refs[:3]
        p_refs = refs[3:3 + n_p]
        m_refs = refs[3 + n_p:3 + 2 * n_p]
        v_refs = refs[3 + 2 * n_p:3 + 3 * n_p]
        outs = refs[3 + 3 * n_p:]
        gwa_ref, gcw_ref, loss_ref = outs[:3]
        res = outs[3:]
        total = jnp.sum(g_ref[...], axis=0, keepdims=True)
        seg = lambda nm: total[:, offs[nm][0]:offs[nm][0] + offs[nm][1]]
        loss_ref[...] = seg("loss")
        gwa_ref[...] = hdot(silu(c_ref[...]), dm_ref[...], "tn")
        cw = seg("conv_w")
        for j in range(4):
            gcw_ref[j:j + 1, :] = cw[:, j * CONV_DIM:(j + 1) * CONV_DIM]
        hc = lax.broadcasted_iota(jnp.int32, (B_INNER, LANES), 0)
        hj = lax.broadcasted_iota(jnp.int32, (B_INNER, LANES), 1)
        per_head = ((hc >> 6) == hj).astype(F32)
        heads = lambda nm: hdot(jnp.broadcast_to(seg(nm), (8, B_INNER)), per_head)[0:1, 0:32]
        lbp = sigmoid(p_refs[1][0:1, :] - p_refs[1][1:2, :])
        g_row = seg("lb") * lbp * (1.0 - lbp)
        grads = {"b_ada": seg("mod"), "hgrn_gnorm": seg("gnorm"), "ssm_conv_b": seg("conv_b"),
                 "ssm_dt_bias": seg("dt_bias")[:, 0:32], "ssm_a_log": heads("a_log"), "ssm_d": heads("d"),
                 "ssm_norm": seg("ssm_norm"), "ln1_g": seg("ln1_g"), "ln1_b": seg("ln1_b"),
                 "ln2_g": seg("ln2_g"), "ln2_b": seg("ln2_b")}
        for i, nm in enumerate(SMALL_PARAMS):
            g_out, d_out, m_out, v_out = res[4 * i:4 * i + 4]
            if nm == "hgrn_lb":
                for row, gv in ((0, g_row), (1, -g_row)):
                    sl = slice(row, row + 1)
                    dl, mn, vn = adamw(p_refs[i][sl, :], gv, m_refs[i][sl, :], v_refs[i][sl, :])
                    g_out[sl, :], d_out[sl, :], m_out[sl, :], v_out[sl, :] = gv, dl, mn, vn
            else:
                gv = grads[nm]
                dl, mn, vn = adamw(p_refs[i][...], gv, m_refs[i][...], v_refs[i][...])
                g_out[...], d_out[...], m_out[...], v_out[...] = gv, dl, mn, vn

    out_shape = [jax.ShapeDtypeStruct((D, dmod_cols.shape[1]), F32), jax.ShapeDtypeStruct((4, CONV_DIM), F32),
                 jax.ShapeDtypeStruct((1, LANES), F32)]
    for p in params:
        out_shape += [jax.ShapeDtypeStruct(p.shape, F32)] * 4
    return pl.pallas_call(
        body, name="finalize_small", out_shape=out_shape,
        compiler_params=pltpu.CompilerParams(vmem_limit_bytes=VMEM_LIMIT),
    )(g_all, c_all, dmod_cols, *params, *m, *v)


def adam_update(w, g, m, v, name, after=None):
    rows, cols = w.shape
    tm, tc = _tile2(rows, cols)
    order = [] if after is None else [after]

    def body(w_ref, g_ref, m_ref, v_ref, *rest):
        d_ref, mo_ref, vo_ref = rest[len(order):]
        d_ref[...], mo_ref[...], vo_ref[...] = adamw(w_ref[...], g_ref[...], m_ref[...], v_ref[...])

    spec = pl.BlockSpec((tm, tc), lambda i, j: (i, j))
    return pl.pallas_call(
        body, name=name, grid=(rows // tm, cols // tc),
        in_specs=[spec] * 4 + [pl.BlockSpec(memory_space=pl.ANY) for _ in order], out_specs=[spec] * 3,
        out_shape=[jax.ShapeDtypeStruct((rows, cols), F32)] * 3,
        compiler_params=_params(("arbitrary", "arbitrary")),
    )(w, g, m, v, *order)


def kernel(x, c, w_ada, b_ada, w_in, hgrn_lb, hgrn_gnorm, ssm_conv_w, ssm_conv_b, ssm_dt_bias, ssm_a_log, ssm_d, ssm_norm, w_branch_a, w_branch_b, w_o, ln1_g, ln1_b, w_ffn_gate, w_ffn_up, w_ffn_down, ln2_g, ln2_b, loss_target, m_w_ada, m_b_ada, m_w_in, m_hgrn_lb, m_hgrn_gnorm, m_ssm_conv_w, m_ssm_conv_b, m_ssm_dt_bias, m_ssm_a_log, m_ssm_d, m_ssm_norm, m_w_branch_a, m_w_branch_b, m_w_o, m_ln1_g, m_ln1_b, m_w_ffn_gate, m_w_ffn_up, m_w_ffn_down, m_ln2_g, m_ln2_b, v_w_ada, v_b_ada, v_w_in, v_hgrn_lb, v_hgrn_gnorm, v_ssm_conv_w, v_ssm_conv_b, v_ssm_dt_bias, v_ssm_a_log, v_ssm_d, v_ssm_norm, v_w_branch_a, v_w_branch_b, v_w_o, v_ln1_g, v_ln1_b, v_w_ffn_gate, v_w_ffn_up, v_w_ffn_down, v_ln2_g, v_ln2_b):
    given = dict(locals())
    chip = 2 * lax.axis_index("x") + lax.axis_index("y")
    core = lax.axis_index("c")
    t = x.shape[1]

    first = gather_rows(jnp.concatenate([c, ssm_conv_w.reshape(1, CONV_DIM)], axis=1), "gather_cond").reshape(8, D + CONV_DIM)
    c_all = first[:, :D]
    conv_w = first[0::2, D:].reshape(4, 4, CONV_DIM // 4).transpose(1, 0, 2).reshape(4, CONV_DIM)
    mod_part, lb_row = ada_prepare(c_all, w_ada[0], hgrn_lb)
    mod_cols = w_ada.shape[2]
    mod_row = exchange_rows(mod_part.reshape(8, 1, mod_cols), "exchange_mod").reshape(1, 6 * D) + b_ada

    local = {nm: given[nm][0] for nm in SHARDED if nm != "w_ffn_in"}
    local["w_ffn_in"] = jnp.concatenate([w_ffn_gate[0].T, w_ffn_up[0].T], axis=0)
    shards = [local[nm].astype(BF16) for nm in SHARDED]
    send_in, recv_in, sent_in, land_in, started_in = gather_start(shards[:1], mod_row, "in")
    shards = shards[:1] + [(local[nm] + started_in[0, 0]).astype(BF16) for nm in SHARDED[1:]]
    send_rest, recv_rest, *flying = gather_start(shards[1:], started_in, "rest")
    n_rest = len(SHARDED) - 1
    sent_rest, land_rest, started_rest = flying[:n_rest], flying[n_rest:2 * n_rest], flying[-1]
    mod_row = mod_row + started_rest[0:1, 0:1]
    mod = tuple(mod_row[:, i * D:(i + 1) * D] for i in range(6))
    with_own = lambda land, shard: lax.dynamic_update_slice(land, shard[None], (chip, 0, 0))

    class Weights:
        def input_projection(self, after):
            (own,), land = gather_wait(send_in, recv_in, [sent_in], [land_in], after, "in")
            (land,) = forward_wait(forward_start(land, "in"), after, "in")
            return assemble_in_proj(land, own, chip)

        def start_rest(self, after):
            self.own, landed = gather_wait(send_rest, recv_rest, sent_rest, land_rest, after, "rest")
            self.started = forward_start(landed, "rest")
            return self.started[-1]

        def rest(self, after):
            got = {nm: with_own(land, s) for nm, land, s in zip(SHARDED[1:], forward_wait(self.started, after, "rest"), self.own, strict=True)}
            whole = lambda nm: got[nm].reshape(4 * got[nm].shape[1], got[nm].shape[2])
            return tuple(whole(nm) for nm in SHARDED[1:])

    wts = Weights()

    per_head = lambda p: jnp.pad(p, ((0, 0), (0, LANES - p.shape[1])))
    per_channel = lambda p: jnp.repeat(p[0], B_INNER // 32)[None]
    small = (lb_row, hgrn_gnorm, conv_w, ssm_conv_b, per_head(ssm_dt_bias), per_channel(ssm_a_log),
             per_channel(ssm_d), ssm_norm, ln1_g, ln1_b, ln2_g, ln2_b)
    by_rows = lambda g: g.reshape(4, g.shape[0] // 4, g.shape[1])
    travelling = {}

    def start_early(dws):
        travelling["pair"] = pair_start([by_rows(dw) for dw in dws], "early")
        return travelling["pair"][-1]

    def between_scans(after):
        slabs, received = pair_wait(travelling["pair"], after, "early")
        travelling["pairs"] = [pair_add(s, r, core, "pair_add_" + nm) for nm, s, r in zip(SHARDED[1:], slabs, received, strict=True)]
        travelling["started"] = scatter_start(travelling["pairs"], "early")
        return travelling["started"][-1]

    def finish_early(after):
        travelling["pairs"], travelling["landed"] = scatter_wait(travelling["started"], after, "early")

    def start_last(u1, dproj):
        wide = 2 * IN_PAD // 3
        first = matmul(u1, dproj, "tn", F32, "in_proj_dw_first", b_cols=(0, wide))
        sending = rows_start(first, "last")
        second = matmul(u1, dproj, "tn", F32, "in_proj_dw_second", after=sending[-1], b_cols=(wide, IN_PAD - wide))
        first, got_first = rows_wait(sending, second, "last")
        got_second = rows_exchange(second, "pair_exchange_last")
        travelling["pairs_in"] = [split_pair_add([first, second], [got_first, got_second], core)]
        travelling["started_in"] = scatter_start(travelling["pairs_in"], "last")
        return travelling["started_in"][-1]

    loss, grad_x, d_mod, d_wts, d_small = local_step(x[0], loss_target[0], mod, wts, small,
                                                     start_early, between_scans, finish_early, start_last)

    d_lb, d_gn, d_cw, d_cb, d_dtb, d_alog, d_dsk, d_nw, d_l1g, d_l1b, d_l2g, d_l2b = d_small
    row = jnp.concatenate(list(d_mod) + [d_lb, d_gn, d_cw.reshape(1, 4 * CONV_DIM), d_cb, d_dtb, d_alog, d_dsk, d_nw,
                                          d_l1g, d_l1b, d_l2g, d_l2b, jnp.pad(loss, ((0, 0), (0, LANES - 1)))], axis=1)
    g_all = gather_rows(row, "gather_small_grads").reshape(8, row.shape[1])
    dmod_cols = lax.dynamic_slice_in_dim(g_all, chip * mod_cols, mod_cols, axis=1)
    fin = finalize_small(g_all, c_all, dmod_cols, [given[n] for n in SMALL_PARAMS],
                         [given["m_" + n] for n in SMALL_PARAMS], [given["v_" + n] for n in SMALL_PARAMS])
    grads, deltas, new_m, new_v = {}, {}, {}, {}
    grads["w_ada"] = fin[0][None]
    grads["ssm_conv_w"] = lax.dynamic_slice_in_dim(fin[1], chip * (CONV_DIM // 4), CONV_DIM // 4, axis=1)[None]
    for i, nm in enumerate(SMALL_PARAMS):
        grads[nm], deltas[nm], new_m[nm], new_v[nm] = fin[3 + 4 * i:7 + 4 * i]

    pairs_in, landed_in = scatter_wait(travelling["started_in"], fin[3], "last")
    pairs, landed = pairs_in + travelling["pairs"], landed_in + travelling["landed"]
    halves = [sum_chips(r, p, chip, core, "sum_chips_" + nm) for nm, r, p in zip(SHARDED, landed, pairs, strict=True)]
    exchanging = halves_start(halves)
    reduced = {"w_ada": grads["w_ada"][0], "ssm_conv_w": grads["ssm_conv_w"][0]}

    def update(nm, after=None):
        flipped = nm in ("w_in", "w_ffn_gate", "w_ffn_up")
        work = (lambda a: a[0].T) if flipped else (lambda a: a[0])
        back = (lambda a: a.T[None]) if flipped else (lambda a: a[None])
        d_, m_, v_ = adam_update(work(given[nm]), reduced[nm], work(given["m_" + nm]), work(given["v_" + nm]),
                                 "adam_" + nm, after)
        grads[nm], deltas[nm], new_m[nm], new_v[nm] = back(reduced[nm]), back(d_), back(m_), back(v_)

    update("w_ada", exchanging[-1])
    update("ssm_conv_w", exchanging[-1])
    reduced.update(zip(SHARDED, halves_wait(exchanging, new_m["w_ada"]), strict=True))
    reduced["w_in"] = reduced["w_in"].T
    reduced["w_ffn_gate"], reduced["w_ffn_up"] = reduced["w_ffn_in"][:FFN_SHARD], reduced["w_ffn_in"][FFN_SHARD:]
    for nm in ("w_in", "w_branch_a", "w_branch_b", "w_o", "w_ffn_gate", "w_ffn_up", "w_ffn_down"):
        update(nm)

    names = ("w_ada", "b_ada", "w_in", "hgrn_lb", "hgrn_gnorm", "ssm_conv_w", "ssm_conv_b", "ssm_dt_bias", "ssm_a_log",
             "ssm_d", "ssm_norm", "w_branch_a", "w_branch_b", "w_o", "ln1_g", "ln1_b", "w_ffn_gate", "w_ffn_up",
             "w_ffn_down", "ln2_g", "ln2_b")
    return (fin[2][0, 0], grad_x[None], *[grads[n] for n in names], *[deltas[n] for n in names],
            *[new_m[n] for n in names], *[new_v[n] for n in names])
```

```python
import functools

import jax
import jax.numpy as jnp
from jax import lax
from jax.experimental import pallas as pl
from jax.experimental.pallas import tpu as pltpu

F32, BF16 = jnp.float32, jnp.bfloat16
HI = lax.Precision.HIGHEST
MESH = pl.DeviceIdType.MESH

D = 1024
CHUNK = 64
LANES = 128
N_HEADS_A = 8
N_GROUPS_B = 4
B_INNER = 2048
CONV_DIM = 3072
D_FF = 2816
ALPHA = 2.0 ** 0.25
LN_EPS = 1e-5
RMS_EPS = 1e-6
ADAM_LR, ADAM_B1, ADAM_B2, ADAM_EPS, ADAM_WD, ADAM_STEP = 0.001, 0.9, 0.999, 1e-08, 0.01, 10

IN_ORIG = 11296
IN_PAD = 11520
COL_GA, COL_GB, COL_XBC, COL_Z, COL_DT = 4096, 5120, 6144, 9216, 11264
ORIG_Z, ORIG_XBC, ORIG_DT, ORIG_GA = 4096, 6144, 9216, 9248

SHARDED = ("w_in", "w_branch_a", "w_branch_b", "w_o", "w_ffn_in", "w_ffn_down")
FFN_SHARD = D_FF // 4
VMEM_LIMIT = 56 * 1024 * 1024
BLOCK_BYTES = 2 * 1024 * 1024
_DIMS = {"nn": (((1,), (0,)), ((), ())), "nt": (((1,), (1,)), ((), ())), "tn": (((0,), (0,)), ((), ()))}


def _bd(a, b, mode):
    return lax.dot_general(a.astype(BF16), b.astype(BF16), _DIMS[mode], preferred_element_type=F32)


@functools.partial(jax.custom_vjp, nondiff_argnums=(2,))
def bdot(a, b, mode):
    return _bd(a, b, mode)


def _bdot_fwd(a, b, mode):
    return _bd(a, b, mode), (a, b)


def _bdot_bwd(mode, res, g):
    a, b = res
    if mode == "nn":
        return _bd(g, b, "nt"), _bd(a, g, "tn")
    if mode == "nt":
        return _bd(g, b, "nn"), _bd(g, a, "tn")
    return _bd(b, g, "nt"), _bd(a, g, "nn")


bdot.defvjp(_bdot_fwd, _bdot_bwd)


def hdot(a, b, mode="nn"):
    return lax.dot_general(a, b, _DIMS[mode], precision=HI, preferred_element_type=F32)


def _raw(a, b, mode):
    return lax.dot_general(a, b, _DIMS[mode], preferred_element_type=F32)


def _split(x, n):
    parts, rest = [], x
    for _ in range(n):
        p = rest.astype(BF16)
        parts.append(p)
        rest = rest - p.astype(F32)
    return parts


def _od(a, b, mode, exact):
    if exact == 1:
        e = b.astype(BF16)
        p = _split(a, 3)
        return (_raw(p[2], e, mode) + _raw(p[1], e, mode)) + _raw(p[0], e, mode)
    e = a.astype(BF16)
    p = _split(b, 3)
    return (_raw(e, p[2], mode) + _raw(e, p[1], mode)) + _raw(e, p[0], mode)


@functools.partial(jax.custom_vjp, nondiff_argnums=(2, 3))
def odot(a, b, mode, exact):
    return _od(a, b, mode, exact)


def _odot_fwd(a, b, mode, exact):
    return _od(a, b, mode, exact), (a, b)


def _odot_bwd(mode, exact, res, g):
    a, b = res
    if exact == 1:
        da = {"nn": lambda: _od(g, b, "nt", 1), "nt": lambda: _od(g, b, "nn", 1), "tn": lambda: _od(b, g, "nt", 0)}[mode]()
        return da, jnp.zeros_like(b)
    db = {"nn": lambda: _od(a, g, "tn", 0), "nt": lambda: _od(g, a, "tn", 1), "tn": lambda: _od(a, g, "nn", 0)}[mode]()
    return jnp.zeros_like(a), db


odot.defvjp(_odot_fwd, _odot_bwd)


_BDIMS = {"bnn": (((2,), (1,)), ((0,), (0,))), "bnt": (((2,), (2,)), ((0,), (0,))), "btn": (((1,), (1,)), ((0,), (0,)))}


def _braw(a, b, mode):
    return lax.dot_general(a, b, _BDIMS[mode], preferred_element_type=F32)


def _bdb(a, b, mode):
    return _braw(a.astype(BF16), b.astype(BF16), mode)


def _d3b(a, b, mode):
    ah, al = _split(a, 2)
    bh, bl = _split(b, 2)
    return _braw(ah, bh, mode) + (_braw(ah, bl, mode) + _braw(al, bh, mode))


def _batched_bwd(f):
    def bwd(mode, res, g):
        a, b = res
        if mode == "bnn":
            return f(g, b, "bnt"), f(a, g, "btn")
        if mode == "bnt":
            return f(g, b, "bnn"), f(g, a, "btn")
        return f(b, g, "bnt"), f(a, g, "bnn")
    return bwd


@functools.partial(jax.custom_vjp, nondiff_argnums=(2,))
def bdot_b(a, b, mode):
    return _bdb(a, b, mode)


bdot_b.defvjp(lambda a, b, mode: (_bdb(a, b, mode), (a, b)), _batched_bwd(_bdb))


@functools.partial(jax.custom_vjp, nondiff_argnums=(2,))
def dot3_b(a, b, mode):
    return _d3b(a, b, mode)


dot3_b.defvjp(lambda a, b, mode: (_d3b(a, b, mode), (a, b)), _batched_bwd(_d3b))


def _cum(tril3, x, mode):
    e = tril3.astype(BF16)
    p = _split(x, 3)
    return (_braw(e, p[2], mode) + _braw(e, p[1], mode)) + _braw(e, p[0], mode)


@jax.custom_vjp
def chunk_cumsum(tril3, x):
    return _cum(tril3, x, "bnn")


chunk_cumsum.defvjp(lambda t, x: (_cum(t, x, "bnn"), t), lambda t, g: (jnp.zeros_like(t), _cum(t, g, "btn")))


def _unstack(axis, n):
    @jax.custom_vjp
    def un(x):
        return tuple(lax.index_in_dim(x, i, axis, keepdims=False) for i in range(n))

    un.defvjp(lambda x: (un(x), None), lambda _, g: (jnp.stack(g, axis=axis),))
    return un


def _split_last(n, w):
    @jax.custom_vjp
    def sp(x):
        return tuple(x[..., i * w:(i + 1) * w] for i in range(n))

    sp.defvjp(lambda x: (sp(x), None), lambda _, g: (jnp.concatenate(g, axis=-1),))
    return sp


def sigmoid(x):
    return 0.5 * jnp.tanh(0.5 * x) + 0.5


def silu(x):
    return x * sigmoid(x)


def softplus(x):
    return jnp.maximum(x, 0.0) + jnp.log1p(jnp.exp(jnp.minimum(x, -x)))


def _ln(x):
    mu = jnp.mean(x, axis=-1, keepdims=True)
    xc = x - mu
    return xc * lax.rsqrt(jnp.mean(xc * xc, axis=-1, keepdims=True) + LN_EPS)


def _tril64():
    r = lax.broadcasted_iota(jnp.int32, (CHUNK, CHUNK), 0)
    c = lax.broadcasted_iota(jnp.int32, (CHUNK, CHUNK), 1)
    return (r >= c).astype(F32)


def hgrn_block(q, fl, iv, gr, st, lb, gn):
    tb = q.shape[0]
    nc = tb // CHUNK
    nh = N_HEADS_A
    heads = _split_last(nh, LANES)
    to4 = lambda a: jnp.stack(heads(a), axis=0).reshape(nh, nc, CHUNK, LANES)
    flat = lambda a: a.reshape(nh * nc, CHUNK, LANES)
    f = lb + (1.0 - lb) * sigmoid(fl)
    gl4, k4, qf4, v4, gr4 = to4(jnp.log(f)), to4(1.0 - f), to4(silu(q) * (128 ** -0.5)), to4(iv), to4(gr)
    tril = _tril64()
    b4 = chunk_cumsum(jnp.broadcast_to(tril[None], (nh * nc, CHUNK, CHUNK)), flat(gl4)).reshape(gl4.shape)
    blast = jnp.sum(gl4, axis=2, keepdims=True)
    ref = lax.stop_gradient(0.5 * blast)
    qp, kp = qf4 * jnp.exp(b4 - ref), k4 * jnp.exp(ref - b4)
    sc = dot3_b(flat(qp), flat(kp), "bnt") * tril
    o_intra = bdot_b(sc, flat(v4), "bnn").reshape(gl4.shape)
    chunks = _unstack(1, nc)
    qe, v_c, kd, dec = chunks(qp * jnp.exp(ref)), chunks(v4), chunks(kp * jnp.exp(blast - ref)), chunks(jnp.exp(blast))
    o_inter = []
    for c in range(nc):
        o_inter.append(bdot_b(qe[c], st, "bnt"))
        st = st * dec[c] + bdot_b(v_c[c], kd[c], "btn")
    o = o_intra + jnp.stack(o_inter, axis=1)
    on = o * lax.rsqrt(jnp.mean(o * o, axis=-1, keepdims=True) + RMS_EPS) * gn
    out = (on * silu(gr4)).reshape(nh, tb, LANES)
    return jnp.concatenate(_unstack(0, nh)(out), axis=1), st


def ssd_consts(g):
    i32 = jnp.int32
    ej = lax.broadcasted_iota(i32, (LANES, 512), 0)
    ec = lax.broadcasted_iota(i32, (LANES, 512), 1)
    expand = (ej == g * 8 + (ec >> 6)).astype(F32)
    ts = lax.broadcasted_iota(i32, (CHUNK, 512), 0)
    tc = lax.broadcasted_iota(i32, (CHUNK, 512), 1)
    itile = (ts == (tc & 63)).astype(F32)
    maskall = ts >= (tc & 63)
    br = lax.broadcasted_iota(i32, (LANES, LANES), 0)
    bc = lax.broadcasted_iota(i32, (LANES, LANES), 1)
    blockmask = ((br >> 6) == (bc >> 6)).astype(F32)
    return expand, itile, maskall, blockmask, _tril64()


def ssd_block(x, bm, cm, dt, z, st, dtb, alog, dsk, nw, cs):
    expand, itile, maskall, blockmask, tril = cs
    tb = x.shape[0]
    nc = tb // CHUNK
    delta = odot(softplus(dt + dtb), expand, "nn", 1)
    a = -jnp.exp(alog) * delta
    xdt = x * delta
    by_chunk = lambda v: v.reshape(nc, CHUNK, v.shape[-1])
    a3, xdt3, bm3, cm3 = by_chunk(a), by_chunk(xdt), by_chunk(bm), by_chunk(cm)
    acum3 = chunk_cumsum(jnp.broadcast_to(tril[None], (nc, CHUNK, CHUNK)), a3)
    alast3 = jnp.sum(a3, axis=1, keepdims=True)
    cb3 = bdot_b(cm3, jnp.concatenate([bm3] * 8, axis=1), "bnt")
    arow3 = jnp.sum(acum3 * itile, axis=1, keepdims=True)
    dec3 = jnp.exp(jnp.where(maskall, acum3 - arow3, -1e30))
    pairs = _split_last(4, LANES)
    intra = [bdot_b(m, jnp.concatenate([xp] * 2, axis=1) * blockmask, "bnn")
             for m, xp in zip(pairs(cb3 * dec3), pairs(xdt3))]
    chunks = _unstack(0, nc)
    cm_c, bm_c, xw_c, dec_c = chunks(cm3), chunks(bm3), chunks(xdt3 * jnp.exp(alast3 - acum3)), chunks(jnp.exp(alast3))
    inter = []
    for c in range(nc):
        inter.append(bdot(cm_c[c], st, "nn"))
        st = st * dec_c[c] + bdot(bm_c[c], xw_c[c], "tn")
    st_new = st
    y = (jnp.concatenate(intra, axis=-1) + jnp.stack(inter, axis=0) * jnp.exp(acum3)).reshape(tb, 512)
    yz = (y + x * dsk) * silu(z)
    return yz * lax.rsqrt(jnp.mean(yz * yz, axis=-1, keepdims=True) + RMS_EPS) * nw, st_new


def adamw(w, g, m, v):
    m = ADAM_B1 * m + (1.0 - ADAM_B1) * g
    v = ADAM_B2 * v + (1.0 - ADAM_B2) * jnp.square(g)
    m_hat = m / (1.0 - ADAM_B1 ** ADAM_STEP)
    v_hat = v / (1.0 - ADAM_B2 ** ADAM_STEP)
    return -ADAM_LR * (m_hat / (jnp.sqrt(v_hat) + ADAM_EPS) + ADAM_WD * w), m, v


def _pick(n, cands):
    for c in cands:
        if n % c == 0:
            return c
    return n


def _params(sem):
    return pltpu.CompilerParams(dimension_semantics=sem, vmem_limit_bytes=VMEM_LIMIT)


MATMUL_VMEM_BUDGET = 50 * 1024 * 1024
MATMUL_MIN_STEPS = 4


def matmul(a, b, mode, out_dtype, name, after=None, b_cols=None):
    if mode == "nn":
        (m, k), n = a.shape, b.shape[1]
    elif mode == "nt":
        (m, k), n = a.shape, b.shape[0]
    else:
        (k, m), n = a.shape, b.shape[1]
    first_col, n = (0, n) if b_cols is None else b_cols
    a_bytes, b_bytes, out_bytes = a.dtype.itemsize, b.dtype.itemsize, jnp.dtype(out_dtype).itemsize
    k_sizes = (2304, 2048, 1408, 1024, 768, 512, 256, 128)
    usual_tk = _pick(k, k_sizes)

    def vmem(tm_, tn_, tk_):
        blocks = 2 * (tm_ * tk_ * a_bytes + tk_ * tn_ * b_bytes + tm_ * tn_ * out_bytes)
        return blocks + (tm_ * tn_ * 4 if tk_ < k else 0)

    def traffic(tm_, tn_, tk_):
        return (m // tm_) * k * n * b_bytes + (n // tn_ if tk_ < k else 1) * m * k * a_bytes

    sizes = (2304, 2048, 1920, 1408, 1024, 768, 512, 256, 128)
    tiles = [(tm_, tn_, tk_) for tm_ in sizes if m % tm_ == 0 for tn_ in sizes if n % tn_ == 0
             for tk_ in {k, usual_tk} if vmem(tm_, tn_, tk_) <= MATMUL_VMEM_BUDGET] or [(m, n, k)]
    pipelined = [t for t in tiles if (m // t[0]) * (n // t[1]) * (k // t[2]) >= MATMUL_MIN_STEPS]
    tm, tn, tk = min(pipelined or tiles, key=lambda t: (traffic(*t), t[2] != usual_tk, -t[0] * t[1]))
    nk = k // tk
    a_spec = pl.BlockSpec((tk, tm), lambda i, j, kk: (kk, i)) if mode == "tn" else pl.BlockSpec((tm, tk), lambda i, j, kk: (i, kk))
    assert first_col % tn == 0 and (mode != "nt" or b_cols is None)
    skip = first_col // tn
    b_spec = pl.BlockSpec((tn, tk), lambda i, j, kk: (j, kk)) if mode == "nt" else pl.BlockSpec((tk, tn), lambda i, j, kk: (kk, j + skip))

    order = [] if after is None else [after]

    def body(a_ref, b_ref, *rest):
        o_ref, *acc = rest[len(order):]
        part = _bd(a_ref[...], b_ref[...], mode)
        if nk == 1:
            o_ref[...] = part.astype(o_ref.dtype)
            return
        acc_ref, = acc
        kk = pl.program_id(2)

        @pl.when(kk == 0)
        def _():
            acc_ref[...] = part

        @pl.when(jnp.logical_and(kk > 0, kk < nk - 1))
        def _():
            acc_ref[...] += part

        @pl.when(kk == nk - 1)
        def _():
            o_ref[...] = (acc_ref[...] + part).astype(o_ref.dtype)

    return pl.pallas_call(
        body, name=name, grid=(m // tm, n // tn, nk),
        in_specs=[a_spec, b_spec] + [pl.BlockSpec(memory_space=pl.ANY) for _ in order],
        out_specs=pl.BlockSpec((tm, tn), lambda i, j, kk: (i, j)),
        out_shape=jax.ShapeDtypeStruct((m, n), out_dtype),
        scratch_shapes=[pltpu.VMEM((tm, tn), F32)] if nk > 1 else [],
        compiler_params=_params(("parallel", "parallel", "arbitrary")),
    )(a, b, *order)


def rowwise(name, fn, rows, consts, out_rows, out_accs=(), tm_max=512, into=None, new_wide=None):
    t = rows[0][0].shape[0]
    tm = _pick(t, (tm_max, 128, 64, 32, 16, 8))
    n_r, n_c, n_o = len(rows), len(consts), len(out_rows)
    n_alias = 0 if into is None else 1

    def body(*refs):
        r_in = [r[...] for r in refs[:n_r]]
        c_in = [r[...] for r in refs[n_r:n_r + n_c]]
        refs = refs[:n_r + n_c] + refs[n_r + n_c + n_alias:]
        o_refs = refs[n_r + n_c:n_r + n_c + n_o]
        a_refs = refs[n_r + n_c + n_o:]
        ro, ao = fn(r_in, c_in)
        for ref, val in zip(o_refs, ro, strict=True):
            ref[...] = val.astype(ref.dtype)
        if a_refs:
            @pl.when(pl.program_id(0) == 0)
            def _():
                for ref in a_refs:
                    ref[...] = jnp.zeros_like(ref)

            for ref, val in zip(a_refs, ao, strict=True):
                ref[...] += val

    in_specs = [pl.BlockSpec((tm, w), functools.partial(lambda i, cb: (i, cb), cb=cb)) for _, w, cb in rows]
    in_specs += [pl.BlockSpec(c.shape, lambda i: (0, 0)) for c in consts]
    out_specs = [pl.BlockSpec((tm, w), lambda i: (i, 0)) for w, _ in out_rows]
    out_specs += [pl.BlockSpec(s, lambda i: (0, 0)) for s in out_accs]
    out_shape = [jax.ShapeDtypeStruct((t, w), dt) for w, dt in out_rows]
    out_shape += [jax.ShapeDtypeStruct(s, F32) for s in out_accs]
    operands = [r[0] for r in rows] + list(consts)
    aliases = {}
    if into is not None:
        target, cb = into
        in_specs.append(pl.BlockSpec(memory_space=pl.ANY))
        operands.append(target)
        out_specs[0] = pl.BlockSpec((tm, out_rows[0][0]), lambda i: (i, cb))
        out_shape[0] = jax.ShapeDtypeStruct(target.shape, target.dtype)
        aliases = {len(operands) - 1: 0}
    if new_wide is not None:
        width, cb = new_wide
        out_specs[0] = pl.BlockSpec((tm, out_rows[0][0]), lambda i: (i, cb))
        out_shape[0] = jax.ShapeDtypeStruct((t, width), out_rows[0][1])
    return pl.pallas_call(
        body, name=name, grid=(t // tm,), in_specs=in_specs, out_specs=out_specs, out_shape=out_shape,
        input_output_aliases=aliases, compiler_params=_params(("arbitrary",)),
    )(*operands)


def _full(a):
    return (a, a.shape[1], 0)


HGRN_TIME_BLOCK = 256
SSD_TIME_BLOCK = 512


def _time_block(t, most=HGRN_TIME_BLOCK):
    return _pick(t, tuple(b for b in (512, 256, 128, 64) if b <= most))


def _quarters(ref):
    return [ref[:, seg * D:(seg + 1) * D] for seg in range(4)]


def hgrn_forward(proj, lb, gn):
    t = proj.shape[0]
    tb = _time_block(t)
    nb = t // tb

    def body(qfig_ref, lb_ref, gn_ref, o_ref, st_ref, state):
        @pl.when(pl.program_id(0) == 0)
        def _():
            state[...] = jnp.zeros_like(state)

        st = state[...]
        st_ref[...] = st
        out, st_new = hgrn_block(*_quarters(qfig_ref), st, lb_ref[...], gn_ref[...])
        o_ref[...] = out.astype(o_ref.dtype)
        state[...] = st_new

    return pl.pallas_call(
        body, name="hgrn_forward", grid=(nb,),
        in_specs=[pl.BlockSpec((tb, 4 * D), lambda j: (j, 0)),
                  pl.BlockSpec((1, D), lambda j: (0, 0)), pl.BlockSpec((1, LANES), lambda j: (0, 0))],
        out_specs=[pl.BlockSpec((tb, D), lambda j: (j, 0)),
                   pl.BlockSpec((None, N_HEADS_A, LANES, LANES), lambda j: (j, 0, 0, 0))],
        out_shape=[jax.ShapeDtypeStruct((t, D), BF16),
                   jax.ShapeDtypeStruct((nb, N_HEADS_A, LANES, LANES), F32)],
        scratch_shapes=[pltpu.VMEM((N_HEADS_A, LANES, LANES), F32)],
        compiler_params=_params(("arbitrary",)),
    )(proj, lb, gn)


def hgrn_backward(proj, states, d_out, lb, gn, d_proj):
    t = proj.shape[0]
    tb = _time_block(t)
    nb = t // tb

    def body(qfig_ref, st_ref, do_ref, lb_ref, gn_ref, _, dqfig_ref, dlb_ref, dgn_ref, d_state):
        @pl.when(pl.program_id(0) == 0)
        def _():
            d_state[...] = jnp.zeros_like(d_state)
            dlb_ref[...] = jnp.zeros_like(dlb_ref)
            dgn_ref[...] = jnp.zeros_like(dgn_ref)

        _, vjp = jax.vjp(hgrn_block, *_quarters(qfig_ref), st_ref[...], lb_ref[...], gn_ref[...])
        dq, df, di, dg, dst, dlb, dgn = vjp((do_ref[...], d_state[...]))
        for seg, val in enumerate((dq, df, di, dg)):
            dqfig_ref[:, seg * D:(seg + 1) * D] = val.astype(dqfig_ref.dtype)
        d_state[...] = dst
        dlb_ref[...] += dlb
        dgn_ref[...] += dgn

    rev = lambda j: nb - 1 - j
    return pl.pallas_call(
        body, name="hgrn_backward", grid=(nb,),
        in_specs=[pl.BlockSpec((tb, 4 * D), lambda j: (rev(j), 0)),
                  pl.BlockSpec((None, N_HEADS_A, LANES, LANES), lambda j: (rev(j), 0, 0, 0)),
                  pl.BlockSpec((tb, D), lambda j: (rev(j), 0)),
                  pl.BlockSpec((1, D), lambda j: (0, 0)), pl.BlockSpec((1, LANES), lambda j: (0, 0)),
                  pl.BlockSpec(memory_space=pl.ANY)],
        out_specs=[pl.BlockSpec((tb, 4 * D), lambda j: (rev(j), 0)),
                   pl.BlockSpec((1, D), lambda j: (0, 0)), pl.BlockSpec((1, LANES), lambda j: (0, 0))],
        out_shape=[jax.ShapeDtypeStruct(d_proj.shape, d_proj.dtype), jax.ShapeDtypeStruct((1, D), F32),
                   jax.ShapeDtypeStruct((1, LANES), F32)],
        input_output_aliases={5: 0},
        scratch_shapes=[pltpu.VMEM((N_HEADS_A, LANES, LANES), F32)],
        compiler_params=_params(("arbitrary",)),
    )(proj, states, d_out, lb, gn, d_proj)


def _ssd_in_specs(tb, tmap):
    return [pl.BlockSpec((tb, 512), lambda g, j: (tmap(j), g)),
            pl.BlockSpec((tb, LANES), lambda g, j: (tmap(j), 16 + g)),
            pl.BlockSpec((tb, LANES), lambda g, j: (tmap(j), 20 + g)),
            pl.BlockSpec((tb, LANES), lambda g, j: (tmap(j), COL_DT // LANES)),
            pl.BlockSpec((tb, 512), lambda g, j: (tmap(j), COL_Z // 512 + g))]


def ssd_forward(xc, proj, dtb, alog, dsk, nw):
    t = proj.shape[0]
    tb = _time_block(t, SSD_TIME_BLOCK)
    nb = t // tb

    def body(x_ref, b_ref, c_ref, dt_ref, z_ref, dtb_ref, alog_ref, dsk_ref, nw_ref, o_ref, st_ref, state):
        @pl.when(pl.program_id(1) == 0)
        def _():
            state[...] = jnp.zeros_like(state)

        st = state[...]
        st_ref[...] = st
        out, st_new = ssd_block(x_ref[...], b_ref[...], c_ref[...], dt_ref[...], z_ref[...], st,
                                dtb_ref[...], alog_ref[...], dsk_ref[...], nw_ref[...], ssd_consts(pl.program_id(0)))
        o_ref[...] = out.astype(o_ref.dtype)
        state[...] = st_new

    vec = pl.BlockSpec((1, 512), lambda g, j: (0, g))
    heads = pl.BlockSpec((1, LANES), lambda g, j: (0, 0))
    return pl.pallas_call(
        body, name="ssd_forward", grid=(N_GROUPS_B, nb),
        in_specs=_ssd_in_specs(tb, lambda j: j) + [heads, vec, vec, vec],
        out_specs=[pl.BlockSpec((tb, 512), lambda g, j: (j, g)),
                   pl.BlockSpec((None, None, LANES, 512), lambda g, j: (j, g, 0, 0))],
        out_shape=[jax.ShapeDtypeStruct((t, B_INNER), BF16),
                   jax.ShapeDtypeStruct((nb, N_GROUPS_B, LANES, 512), F32)],
        scratch_shapes=[pltpu.VMEM((LANES, 512), F32)],
        compiler_params=_params(("arbitrary", "arbitrary")),
    )(xc, xc, xc, proj, proj, dtb, alog, dsk, nw)


def ssd_backward(xc, proj, states, d_out, dtb, alog, dsk, nw, d_proj):
    t = proj.shape[0]
    tb = _time_block(t, SSD_TIME_BLOCK)
    nb = t // tb
    rev = lambda j: nb - 1 - j

    def body(x_ref, b_ref, c_ref, dt_ref, z_ref, st_ref, do_ref, dtb_ref, alog_ref, dsk_ref, nw_ref, _,
             dx_ref, db_ref, dc_ref, ddt_ref, dz_ref, ddtb_ref, dalog_ref, ddsk_ref, dnw_ref, d_state):
        accs = (ddtb_ref, dalog_ref, ddsk_ref, dnw_ref)

        @pl.when(pl.program_id(1) == 0)
        def _():
            d_state[...] = jnp.zeros_like(d_state)
            for ref in accs:
                ref[...] = jnp.zeros_like(ref)

        cs = ssd_consts(pl.program_id(0))
        fn = lambda *a: ssd_block(*a, cs)
        _, vjp = jax.vjp(fn, x_ref[...], b_ref[...], c_ref[...], dt_ref[...], z_ref[...], st_ref[...],
                         dtb_ref[...], alog_ref[...], dsk_ref[...], nw_ref[...])
        dx, db, dc, ddt, dz, dst, *dpar = vjp((do_ref[...], d_state[...]))
        dx_ref[...] = dx
        db_ref[...] = db
        dc_ref[...] = dc
        ddt_ref[...] = ddt
        dz_ref[...] = dz.astype(dz_ref.dtype)
        d_state[...] = dst
        for ref, val in zip(accs, dpar, strict=True):
            ref[...] += val

    vec = pl.BlockSpec((1, 512), lambda g, j: (0, g))
    heads = pl.BlockSpec((1, LANES), lambda g, j: (0, 0))
    acc = pl.BlockSpec((None, 1, 512), lambda g, j: (g, 0, 0))
    acc_heads = pl.BlockSpec((None, 1, LANES), lambda g, j: (g, 0, 0))
    return pl.pallas_call(
        body, name="ssd_backward", grid=(N_GROUPS_B, nb),
        in_specs=_ssd_in_specs(tb, rev)
        + [pl.BlockSpec((None, None, LANES, 512), lambda g, j: (rev(j), g, 0, 0)),
           pl.BlockSpec((tb, 512), lambda g, j: (rev(j), g))] + [heads, vec, vec, vec] + [pl.BlockSpec(memory_space=pl.ANY)],
        out_specs=[pl.BlockSpec((tb, 512), lambda g, j: (rev(j), g)),
                   pl.BlockSpec((tb, LANES), lambda g, j: (rev(j), g)),
                   pl.BlockSpec((tb, LANES), lambda g, j: (rev(j), g)),
                   pl.BlockSpec((None, tb, LANES), lambda g, j: (g, rev(j), 0)),
                   pl.BlockSpec((tb, 512), lambda g, j: (rev(j), COL_Z // 512 + g)), acc_heads, acc, acc, acc],
        out_shape=[jax.ShapeDtypeStruct((t, B_INNER), F32), jax.ShapeDtypeStruct((t, 512), F32),
                   jax.ShapeDtypeStruct((t, 512), F32), jax.ShapeDtypeStruct((N_GROUPS_B, t, LANES), F32),
                   jax.ShapeDtypeStruct(d_proj.shape, d_proj.dtype)]
        + [jax.ShapeDtypeStruct((N_GROUPS_B, 1, LANES), F32)] + [jax.ShapeDtypeStruct((N_GROUPS_B, 1, 512), F32)] * 3,
        input_output_aliases={11: 4},
        scratch_shapes=[pltpu.VMEM((LANES, 512), F32)],
        compiler_params=_params(("arbitrary", "arbitrary")),
    )(xc, xc, xc, proj, proj, states, d_out, dtb, alog, dsk, nw, d_proj)


CONV_HALO = 8


def _shift_down(halo_then_tile, s, tm):
    if s == 0:
        return halo_then_tile[CONV_HALO:CONV_HALO + tm]
    return pltpu.roll(halo_then_tile, s, 0)[CONV_HALO:CONV_HALO + tm]


def _conv_pre(cur, prev, w, b, tm):
    stacked = jnp.concatenate([prev, cur], axis=0)
    taps = [_shift_down(stacked, 3 - j, tm) for j in range(4)]
    pre = b + taps[0] * w[0:1] + taps[1] * w[1:2] + taps[2] * w[2:3] + taps[3] * w[3:4]
    return pre, taps


def _conv_specs(t, tm):
    per = tm // CONV_HALO
    cur = pl.BlockSpec((tm, CONV_DIM), lambda i: (i, COL_XBC // CONV_DIM))
    prev = pl.BlockSpec((CONV_HALO, CONV_DIM), lambda i: (jnp.maximum(i * per - 1, 0), COL_XBC // CONV_DIM))
    return cur, prev


def conv_forward(proj, w, b):
    t = proj.shape[0]
    tm = _pick(t, (256, 128, 64))

    def body(cur_ref, prev_ref, w_ref, b_ref, o_ref):
        prev = jnp.where(pl.program_id(0) == 0, 0.0, prev_ref[...])
        pre, _ = _conv_pre(cur_ref[...], prev, w_ref[...], b_ref[...], tm)
        o_ref[...] = silu(pre)

    cur, prev = _conv_specs(t, tm)
    return pl.pallas_call(
        body, name="conv_forward", grid=(t // tm,),
        in_specs=[cur, prev, pl.BlockSpec((4, CONV_DIM), lambda i: (0, 0)), pl.BlockSpec((1, CONV_DIM), lambda i: (0, 0))],
        out_specs=pl.BlockSpec((tm, CONV_DIM), lambda i: (i, 0)),
        out_shape=jax.ShapeDtypeStruct((t, CONV_DIM), F32),
        compiler_params=_params(("arbitrary",)),
    )(proj, proj, w, b)


def conv_backward(proj, dx, db_, dc_, w, b, d_proj):
    t = proj.shape[0]
    tm = _pick(t, (256, 128, 64))
    per = tm // CONV_HALO
    nt = t // tm
    rev = lambda i: nt - 1 - i

    def body(cur_ref, prev_ref, dx_ref, dbm_ref, dcm_ref, w_ref, b_ref, _, o_ref, dw_ref, dbias_ref, later):
        @pl.when(pl.program_id(0) == 0)
        def _():
            dw_ref[...] = jnp.zeros_like(dw_ref)
            dbias_ref[...] = jnp.zeros_like(dbias_ref)
            later[...] = jnp.zeros_like(later)

        first_tile = pl.program_id(0) == nt - 1
        for lo, hi, src in ((0, B_INNER, dx_ref), (B_INNER, B_INNER + 512, dbm_ref), (B_INNER + 512, CONV_DIM, dcm_ref)):
            cols = slice(lo, hi)
            prev = jnp.where(first_tile, 0.0, prev_ref[:, cols])
            w_ = w_ref[:, cols]
            pre, taps = _conv_pre(cur_ref[:, cols], prev, w_, b_ref[:, cols], tm)
            sg = sigmoid(pre)
            dpre = src[...] * (sg * (1.0 + pre * (1.0 - sg)))
            dbias_ref[:, cols] += jnp.sum(dpre, axis=0, keepdims=True)
            for j in range(4):
                dw_ref[j:j + 1, cols] += jnp.sum(dpre * taps[j], axis=0, keepdims=True)
            stacked = jnp.concatenate([dpre, later[:, cols]], axis=0)
            acc = dpre * w_[3:4]
            for j in range(3):
                acc = acc + pltpu.roll(stacked, tm + CONV_HALO - (3 - j), 0)[0:tm] * w_[j:j + 1]
            o_ref[:, cols] = acc.astype(o_ref.dtype)
            later[:, cols] = dpre[0:CONV_HALO]

    row = lambda w_: pl.BlockSpec((tm, w_), lambda i: (rev(i), 0))
    whole = lambda r: pl.BlockSpec((r, CONV_DIM), lambda i: (0, 0))
    return pl.pallas_call(
        body, name="conv_backward", grid=(nt,),
        in_specs=[pl.BlockSpec((tm, CONV_DIM), lambda i: (rev(i), COL_XBC // CONV_DIM)),
                  pl.BlockSpec((CONV_HALO, CONV_DIM), lambda i: (jnp.maximum(rev(i) * per - 1, 0), COL_XBC // CONV_DIM)),
                  row(B_INNER), row(512), row(512), whole(4), whole(1), pl.BlockSpec(memory_space=pl.ANY)],
        out_specs=[pl.BlockSpec((tm, CONV_DIM), lambda i: (rev(i), COL_XBC // CONV_DIM)), whole(4), whole(1)],
        out_shape=[jax.ShapeDtypeStruct(d_proj.shape, d_proj.dtype), jax.ShapeDtypeStruct((4, CONV_DIM), F32),
                   jax.ShapeDtypeStruct((1, CONV_DIM), F32)],
        input_output_aliases={7: 0},
        scratch_shapes=[pltpu.VMEM((CONV_HALO, CONV_DIM), F32)],
        compiler_params=_params(("arbitrary",)),
    )(proj, proj, dx, db_, dc_, w, b, d_proj)


def stage_modulate(x, sc, sh):
    return _ln(x) * (1.0 + sc) + sh


def stage_merge(ga, gb, ya, yb):
    return sigmoid(ga) * ya + sigmoid(gb) * yb


def stage_post_mixer(x, h, g1, ln_g, ln_b, sc2, sh2):
    x1 = _ln(ALPHA * x + g1 * h) * ln_g + ln_b
    return x1, _ln(x1) * (1.0 + sc2) + sh2


def stage_swiglu(a, b):
    return silu(a) * b


def gate_up(ab):
    w = FFN_SHARD
    return (jnp.concatenate([ab[:, 2 * w * k:2 * w * k + w] for k in range(4)], axis=1),
            jnp.concatenate([ab[:, 2 * w * k + w:2 * w * (k + 1)] for k in range(4)], axis=1))


def per_chip(gate, up):
    w = FFN_SHARD
    return jnp.concatenate([part[:, w * k:w * (k + 1)] for k in range(4) for part in (gate, up)], axis=1)


def stage_loss(x1, hf, tgt, g2, ln_g, ln_b):
    x2 = _ln(ALPHA * x1 + g2 * hf) * ln_g + ln_b
    return 0.5 * jnp.sum(jnp.mean(jnp.square(x2 - tgt), axis=-1, keepdims=True), axis=0, keepdims=True)


def local_step(x, tgt, mod, wts, small, early=None, mid=None, late=None, last=None):
    sh1, sc1, g1, sh2, sc2, g2 = mod
    lb, gn, conv_w, conv_b, dtb, alog, dsk, nw, ln1_g, ln1_b, ln2_g, ln2_b = small
    vec = (1, D)

    (u1,) = rowwise("modulate1", lambda r, c: ((stage_modulate(r[0], *c),), ()), [_full(x)], [sc1, sh1], [(D, BF16)])
    w_in = wts.input_projection(u1)
    proj = matmul(u1, w_in, "nn", F32, "in_proj")
    ya_in, st_a = hgrn_forward(proj, lb, gn + wts.start_rest(proj)[0:1])
    xc = conv_forward(proj, conv_w, conv_b)
    w_a, w_b, w_o, w_gu, w_d = wts.rest(xc)
    yb_in, st_b = ssd_forward(xc, proj, dtb, alog, dsk, nw)
    ya = matmul(ya_in, w_a, "nn", F32, "branch_a")
    yb = matmul(yb_in, w_b, "nn", F32, "branch_b")
    gate_rows = [(proj, D, COL_GA // D), (proj, D, COL_GB // D), _full(ya), _full(yb)]
    (merged,) = rowwise("merge", lambda r, c: ((stage_merge(*r),), ()), gate_rows, [], [(D, BF16)])
    h = matmul(merged, w_o, "nn", F32, "out_proj")
    post_consts = [g1, ln1_g, ln1_b, sc2, sh2]
    x1, u2 = rowwise("post_mixer", lambda r, c: (stage_post_mixer(*r, *c), ()), [_full(x), _full(h)], post_consts,
                     [(D, F32), (D, BF16)])
    ab = matmul(u2, w_gu, "nt", F32, "ffn_in")
    (p,) = rowwise("swiglu", lambda r, c: ((stage_swiglu(*gate_up(r[0])),), ()), [_full(ab)], [], [(D_FF, BF16)],
                   tm_max=256)
    hf = matmul(p, w_d, "nn", F32, "ffn_out")

    def loss_bwd(r, c):
        loss, vjp = jax.vjp(stage_loss, *r, *c)
        dx1, dhf, _, dg2, dlg, dlb_ = vjp(jnp.ones((1, 1), F32))
        return (dx1, dhf), (loss, dg2, dlg, dlb_)

    dx1, dhf, loss, dg2, dln2_g, dln2_b = rowwise(
        "loss_backward", loss_bwd, [_full(x1), _full(hf), _full(tgt)], [g2, ln2_g, ln2_b],
        [(D, F32), (D, BF16)], [(1, 1), vec, vec, vec])
    dp = matmul(dhf, w_d, "nt", F32, "ffn_out_dx")
    dw_d = matmul(p, dhf, "tn", F32, "ffn_out_dw")

    def swiglu_bwd(r, c):
        _, vjp = jax.vjp(stage_swiglu, *gate_up(r[0]))
        return (per_chip(*vjp(r[1])),), ()

    (dab,) = rowwise("swiglu_backward", swiglu_bwd, [_full(ab), _full(dp)], [], [(2 * D_FF, BF16)], tm_max=256)
    du2 = matmul(dab, w_gu, "nn", F32, "ffn_in_dx")
    dw_gu = matmul(dab, u2, "tn", F32, "ffn_in_dw")

    def post_bwd(r, c):
        _, vjp = jax.vjp(stage_post_mixer, r[0], r[1], *c)
        dx, dh, *dc = vjp((r[2], r[3]))
        return (dx, dh), tuple(dc)

    dx_a, dh, dg1, dln1_g, dln1_b, dsc2, dsh2 = rowwise(
        "post_mixer_backward", post_bwd, [_full(x), _full(h), _full(dx1), _full(du2)], post_consts,
        [(D, F32), (D, BF16)], [vec] * 5)
    dmerged = matmul(dh, w_o, "nt", F32, "out_proj_dx")
    dw_o = matmul(merged, dh, "tn", F32, "out_proj_dw")

    def merge_bwd(r, c):
        _, vjp = jax.vjp(stage_merge, *r[:4])
        dga, dgb, dya, dyb = vjp(r[4])
        return (jnp.concatenate([dga, dgb], axis=1), dya, dyb), ()

    dproj, dya, dyb = rowwise("merge_backward", merge_bwd, gate_rows + [_full(dmerged)], [],
                              [(2 * D, BF16), (D, BF16), (D, BF16)], new_wide=(IN_PAD, COL_GA // (2 * D)))
    dya_in = matmul(dya, w_a, "nt", F32, "branch_a_dx")
    dw_a = matmul(ya_in, dya, "tn", F32, "branch_a_dw")
    dyb_in = matmul(dyb, w_b, "nt", F32, "branch_b_dx")
    dw_b = matmul(yb_in, dyb, "tn", F32, "branch_b_dw")
    gn_after = gn if early is None else gn + early((dw_a, dw_b, dw_o, dw_gu, dw_d))[0:1]
    dproj, dlb, dgn = hgrn_backward(proj, st_a, dya_in, lb, gn_after, dproj)
    dtb_after = dtb if mid is None else dtb + mid(dlb)[0:1, 0:1]
    dxs, dbm, dcm, ddt, dproj, ddtb, dalog, ddsk, dnw = ssd_backward(xc, proj, st_b, dyb_in, dtb_after, alog, dsk, nw, dproj)
    dproj, dconv_w, dconv_b = conv_backward(proj, dxs, dbm, dcm, conv_w, conv_b, dproj)
    if late is not None:
        late(dconv_b)
    t = x.shape[0]
    tail = jnp.concatenate([jnp.sum(ddt, axis=0).astype(BF16), jnp.zeros((t, IN_PAD - COL_DT - LANES), BF16)], axis=1)
    dproj = lax.dynamic_update_slice(dproj, tail, (0, COL_DT))
    if last is None:
        dw_in, started = matmul(u1, dproj, "tn", F32, "in_proj_dw"), None
    else:
        dw_in, started = None, last(u1, dproj)
    du1 = matmul(dproj, w_in, "nt", F32, "in_proj_dx", after=started)

    def mod_bwd(r, c):
        _, vjp = jax.vjp(stage_modulate, r[0], *c)
        dx, dsc, dsh = vjp(r[1])
        return (dx + r[2],), (dsc, dsh)

    grad_x, dsc1, dsh1 = rowwise("modulate1_backward", mod_bwd, [_full(x), _full(du1), _full(dx_a)], [sc1, sh1],
                                 [(D, F32)], [vec, vec])
    d_mod = (dsh1, dsc1, dg1, dsh2, dsc2, dg2)
    d_wts = (dw_in, dw_a, dw_b, dw_o, dw_gu, dw_d)
    d_small = (dlb, dgn, dconv_w, dconv_b, jnp.sum(ddtb, axis=0),
               dalog.reshape(1, B_INNER), ddsk.reshape(1, B_INNER), dnw.reshape(1, B_INNER),
               dln1_g, dln1_b, dln2_g, dln2_b)
    return loss, grad_x, d_mod, d_wts, d_small


HBM = pl.BlockSpec(memory_space=pltpu.HBM)
SEM = pl.BlockSpec(memory_space=pltpu.SEMAPHORE)
DATAFLOW = pltpu.SideEffectType.DATAFLOW_SIDE_EFFECTING


def _place():
    return lax.axis_index("x"), lax.axis_index("y"), lax.axis_index("c")


def _other_chips(x, y):
    return [(1 - x, y), (x, 1 - y), (1 - x, 1 - y)]


def _remote(src, dst, send_sem, recv_sem, device):
    return pltpu.make_async_remote_copy(src_ref=src, dst_ref=dst, send_sem=send_sem, recv_sem=recv_sem,
                                        device_id=device, device_id_type=MESH)


def gather_rows(v, name):
    n = v.shape[1]

    def body(v_ref, out_ref, send_sems, recv_sems, local_sem):
        x, y, c = _place()
        mine = pltpu.make_async_copy(v_ref, out_ref.at[4 * x + 2 * y + c], local_sem)
        mine.start()
        sends, recvs = [], []
        for m in range(1, 8):
            px = 1 - x if m & 4 else x
            py = 1 - y if m & 2 else y
            pc = 1 - c if m & 1 else c
            sends.append(_remote(v_ref, out_ref.at[4 * x + 2 * y + c], send_sems.at[m - 1], recv_sems.at[m - 1], (px, py, pc)))
            recvs.append(_remote(v_ref, out_ref.at[4 * px + 2 * py + pc], send_sems.at[m - 1], recv_sems.at[m - 1], (px, py, pc)))
        for cp in sends:
            cp.start()
        for cp in recvs:
            cp.wait_recv()
        for cp in sends:
            cp.wait_send()
        mine.wait()

    return pl.pallas_call(
        body, name=name, in_specs=[HBM], out_specs=HBM,
        out_shape=jax.ShapeDtypeStruct((8, 1, n), v.dtype),
        scratch_shapes=[pltpu.SemaphoreType.DMA((7,)), pltpu.SemaphoreType.DMA((7,)), pltpu.SemaphoreType.DMA],
    )(v)


def exchange_rows(part, name):
    w = part.shape[2]

    def body(p_ref, out_ref, send_sems, recv_sems, local_sem):
        x, y, c = _place()
        k = 2 * x + y
        mine = pltpu.make_async_copy(p_ref.at[4 * x + 2 * y + c], out_ref.at[k], local_sem)
        mine.start()
        sends, recvs = [], []
        for j, (px, py) in enumerate(_other_chips(x, y)):
            sends.append(_remote(p_ref.at[4 * px + 2 * py + c], out_ref.at[k], send_sems.at[j], recv_sems.at[j], (px, py, c)))
            recvs.append(_remote(p_ref.at[4 * px + 2 * py + c], out_ref.at[2 * px + py], send_sems.at[j], recv_sems.at[j], (px, py, c)))
        for cp in sends:
            cp.start()
        for cp in recvs:
            cp.wait_recv()
        for cp in sends:
            cp.wait_send()
        mine.wait()

    return pl.pallas_call(
        body, name=name, in_specs=[HBM], out_specs=HBM,
        out_shape=jax.ShapeDtypeStruct((4, 1, w), part.dtype),
        scratch_shapes=[pltpu.SemaphoreType.DMA((3,)), pltpu.SemaphoreType.DMA((3,)), pltpu.SemaphoreType.DMA],
    )(part)


def _half_of_slot(ref, rows, px, py, pc):
    return ref.at[2 * px + py, pl.ds(pc * (rows // 2), rows // 2), :]


def gather_start(shards, after, tag):
    n = len(shards)

    def body(*refs):
        w_refs, land_refs = refs[:n], refs[n:2 * n]
        send_sems, recv_sems = refs[2 * n + 1], refs[2 * n + 2]
        token = refs[-1]
        x, y, c = _place()
        for i in range(n):
            rows = shards[i].shape[0]
            for j, (px, py) in enumerate(_other_chips(x, y)):
                _remote(w_refs[i].at[pl.ds(c * (rows // 2), rows // 2), :], _half_of_slot(land_refs[i], rows, x, y, c),
                        send_sems.at[j * n + i], recv_sems.at[j * n + i], (px, py, c)).start()
        token[...] = jnp.zeros_like(token)

    hbm = lambda a: pltpu.with_memory_space_constraint(a, pltpu.HBM)
    lands = [lax.empty((4,) + s.shape, s.dtype) for s in shards]
    dma = pltpu.SemaphoreType.DMA
    return pl.pallas_call(
        body, name="gather_start_" + tag,
        out_shape=(dma((3 * n,)), dma((3 * n,)),
                   *[pltpu.HBM(a.shape, a.dtype) for a in list(shards) + lands], jax.ShapeDtypeStruct((8, LANES), F32)),
        in_specs=[HBM] * (2 * n) + [pl.BlockSpec(memory_space=pl.ANY)],
        out_specs=(SEM, SEM, *[HBM] * (2 * n), pl.BlockSpec(memory_space=pltpu.VMEM)),
        input_output_aliases={i: 2 + i for i in range(2 * n)},
        compiler_params=pltpu.CompilerParams(has_side_effects=DATAFLOW),
    )(*[hbm(a) for a in list(shards) + lands], after)


def gather_wait(send_sems, recv_sems, shards, lands, after, tag):
    n = len(shards)

    def body(*refs):
        w_refs, land_refs = refs[:n], refs[n:2 * n]
        send_ref, recv_ref = refs[2 * n], refs[2 * n + 1]
        x, y, c = _place()
        for i in range(n):
            rows = shards[i].shape[0]
            for j, (px, py) in enumerate(_other_chips(x, y)):
                cp = _remote(w_refs[i].at[pl.ds(c * (rows // 2), rows // 2), :], _half_of_slot(land_refs[i], rows, px, py, c),
                             send_ref.at[j * n + i], recv_ref.at[j * n + i], (px, py, c))
                cp.wait_send()
                cp.wait_recv()

    out = pl.pallas_call(
        body, name="gather_wait_" + tag,
        out_shape=tuple(pltpu.HBM(a.shape, a.dtype) for a in list(shards) + list(lands)),
        in_specs=[HBM] * (2 * n) + [SEM, SEM, pl.BlockSpec(memory_space=pl.ANY)], out_specs=tuple([HBM] * (2 * n)),
        input_output_aliases={i: i for i in range(2 * n)},
        compiler_params=pltpu.CompilerParams(has_side_effects=DATAFLOW),
    )(*shards, *lands, send_sems, recv_sems, after)
    return list(out[:n]), list(out[n:])


def forward_start(lands, tag):
    n = len(lands)

    def body(*refs):
        land_refs = refs[:n]
        send_sems, recv_sems = refs[n], refs[n + 1]
        token = refs[-1]
        x, y, c = _place()
        for i in range(n):
            rows = lands[i].shape[1]
            for j, (px, py) in enumerate(_other_chips(x, y)):
                mine = _half_of_slot(land_refs[i], rows, px, py, c)
                _remote(mine, mine, send_sems.at[j * n + i], recv_sems.at[j * n + i], (x, y, 1 - c)).start()
        token[...] = jnp.zeros_like(token)

    dma = pltpu.SemaphoreType.DMA
    return pl.pallas_call(
        body, name="forward_start_" + tag,
        out_shape=(dma((3 * n,)), dma((3 * n,)), *[pltpu.HBM(a.shape, a.dtype) for a in lands],
                   jax.ShapeDtypeStruct((8, LANES), F32)),
        in_specs=[HBM] * n, out_specs=(SEM, SEM, *[HBM] * n, pl.BlockSpec(memory_space=pltpu.VMEM)),
        input_output_aliases={i: 2 + i for i in range(n)},
        compiler_params=pltpu.CompilerParams(has_side_effects=DATAFLOW),
    )(*lands)


def forward_wait(started, after, tag):
    send_sems, recv_sems, *rest = started
    lands = rest[:-1]
    n = len(lands)

    def body(*refs):
        land_refs = refs[:n]
        send_ref, recv_ref = refs[n], refs[n + 1]
        x, y, c = _place()
        for i in range(n):
            rows = lands[i].shape[1]
            for j, (px, py) in enumerate(_other_chips(x, y)):
                cp = _remote(_half_of_slot(land_refs[i], rows, px, py, c), _half_of_slot(land_refs[i], rows, px, py, 1 - c),
                             send_ref.at[j * n + i], recv_ref.at[j * n + i], (x, y, 1 - c))
                cp.wait_send()
                cp.wait_recv()

    out = pl.pallas_call(
        body, name="forward_wait_" + tag,
        out_shape=tuple(pltpu.HBM(a.shape, a.dtype) for a in lands),
        in_specs=[HBM] * n + [SEM, SEM, pl.BlockSpec(memory_space=pl.ANY)], out_specs=tuple([HBM] * n),
        input_output_aliases={i: i for i in range(n)},
        compiler_params=pltpu.CompilerParams(has_side_effects=DATAFLOW),
    )(*lands, send_sems, recv_sems, after)
    return list(out)


def pair_start(slabs, tag):
    n = len(slabs)

    def body(*refs):
        g_refs, land_refs = refs[:n], refs[n:2 * n]
        send_sems, recv_sems = refs[2 * n], refs[2 * n + 1]
        token = refs[-1]
        x, y, c = _place()
        for i in range(n):
            hr = slabs[i].shape[1] // 2
            _remote(g_refs[i].at[:, pl.ds((1 - c) * hr, hr), :], land_refs[i], send_sems.at[i], recv_sems.at[i],
                    (x, y, 1 - c)).start()
        token[...] = jnp.zeros_like(token)

    hbm = lambda a: pltpu.with_memory_space_constraint(a, pltpu.HBM)
    lands = [lax.empty((4, s.shape[1] // 2, s.shape[2]), s.dtype) for s in slabs]
    dma = pltpu.SemaphoreType.DMA
    return pl.pallas_call(
        body, name="pair_start_" + tag,
        out_shape=(dma((n,)), dma((n,)), *[pltpu.HBM(a.shape, a.dtype) for a in list(slabs) + lands],
                   jax.ShapeDtypeStruct((8, LANES), F32)),
        in_specs=[HBM] * (2 * n), out_specs=(SEM, SEM, *[HBM] * (2 * n), pl.BlockSpec(memory_space=pltpu.VMEM)),
        input_output_aliases={i: 2 + i for i in range(2 * n)},
        compiler_params=pltpu.CompilerParams(has_side_effects=DATAFLOW),
    )(*[hbm(a) for a in list(slabs) + lands])


def pair_wait(started, after, tag):
    send_sems, recv_sems, *rest = started
    n = (len(rest) - 1) // 2
    slabs, lands = rest[:n], rest[n:2 * n]

    def body(*refs):
        g_refs, land_refs = refs[:n], refs[n:2 * n]
        send_ref, recv_ref = refs[2 * n], refs[2 * n + 1]
        x, y, c = _place()
        for i in range(n):
            hr = slabs[i].shape[1] // 2
            cp = _remote(g_refs[i].at[:, pl.ds((1 - c) * hr, hr), :], land_refs[i], send_ref.at[i], recv_ref.at[i], (x, y, 1 - c))
            cp.wait_send()
            cp.wait_recv()

    out = pl.pallas_call(
        body, name="pair_wait_" + tag,
        out_shape=tuple(pltpu.HBM(a.shape, a.dtype) for a in list(slabs) + list(lands)),
        in_specs=[HBM] * (2 * n) + [SEM, SEM, pl.BlockSpec(memory_space=pl.ANY)], out_specs=tuple([HBM] * (2 * n)),
        input_output_aliases={i: i for i in range(2 * n)},
        compiler_params=pltpu.CompilerParams(has_side_effects=DATAFLOW),
    )(*slabs, *lands, send_sems, recv_sems, after)
    return list(out[:n]), list(out[n:])


def _tile2(rows, cols):
    fits = lambda r, c: r * c * 4 <= BLOCK_BYTES
    if fits(rows, cols):
        return rows, cols
    tiles = [(r, cols) for r in (1024, 512, 256, 128, 64) if rows % r == 0 and fits(r, cols)]
    tiles += [(rows, cols // k) for k in (2, 3, 4, 6, 8, 12, 16) if cols % (k * LANES) == 0 and fits(rows, cols // k)]
    return max(tiles, key=lambda t: t[0] * t[1])


def pair_add(g, p, c, name):
    _, hr, cols = p.shape
    tm, tc = _tile2(hr, cols)
    per = hr // tm

    def body(c_ref, g_ref, p_ref, o_ref):
        o_ref[...] = (g_ref[...] + p_ref[...]).astype(o_ref.dtype)

    return pl.pallas_call(
        body, name=name,
        grid_spec=pltpu.PrefetchScalarGridSpec(
            num_scalar_prefetch=1, grid=(4, per, cols // tc),
            in_specs=[pl.BlockSpec((None, tm, tc), lambda k, i, j, c_ref: (k, c_ref[0] * per + i, j)),
                      pl.BlockSpec((None, tm, tc), lambda k, i, j, c_ref: (k, i, j))],
            out_specs=pl.BlockSpec((None, tm, tc), lambda k, i, j, c_ref: (k, i, j))),
        out_shape=jax.ShapeDtypeStruct((4, hr, cols), BF16),
        compiler_params=_params(("arbitrary", "arbitrary", "arbitrary")),
    )(c.reshape(1).astype(jnp.int32), g, p)


def scatter_start(sums, tag):
    n = len(sums)

    def body(*refs):
        s_refs, land_refs = refs[:n], refs[n:2 * n]
        send_sems, recv_sems = refs[2 * n], refs[2 * n + 1]
        token = refs[-1]
        x, y, c = _place()
        k = 2 * x + y
        for i in range(n):
            for j, (px, py) in enumerate(_other_chips(x, y)):
                _remote(s_refs[i].at[2 * px + py], land_refs[i].at[k], send_sems.at[j * n + i], recv_sems.at[j * n + i],
                        (px, py, c)).start()
        token[...] = jnp.zeros_like(token)

    hbm = lambda a: pltpu.with_memory_space_constraint(a, pltpu.HBM)
    return pl.pallas_call(
        body, name="scatter_start_" + tag,
        out_shape=(pltpu.SemaphoreType.DMA((3 * n,)), pltpu.SemaphoreType.DMA((3 * n,)),
                   *[pltpu.HBM(s.shape, s.dtype) for s in sums], *[pltpu.HBM(s.shape, s.dtype) for s in sums],
                   jax.ShapeDtypeStruct((8, LANES), F32)),
        in_specs=[HBM] * (2 * n), out_specs=(SEM, SEM, *[HBM] * (2 * n), pl.BlockSpec(memory_space=pltpu.VMEM)),
        input_output_aliases={i: 2 + i for i in range(2 * n)},
        compiler_params=pltpu.CompilerParams(has_side_effects=DATAFLOW),
    )(*[hbm(s) for s in sums], *[hbm(lax.empty(s.shape, s.dtype)) for s in sums])


def scatter_wait(started, after, tag):
    send_sems, recv_sems, *rest = started
    n = (len(rest) - 1) // 2
    sums, lands = rest[:n], rest[n:2 * n]

    def body(*refs):
        s_refs, land_refs = refs[:n], refs[n:2 * n]
        send_ref, recv_ref = refs[2 * n], refs[2 * n + 1]
        x, y, c = _place()
        for i in range(n):
            for j, (px, py) in enumerate(_other_chips(x, y)):
                cp = _remote(s_refs[i].at[2 * px + py], land_refs[i].at[2 * px + py], send_ref.at[j * n + i],
                             recv_ref.at[j * n + i], (px, py, c))
                cp.wait_send()
                cp.wait_recv()

    out = pl.pallas_call(
        body, name="scatter_wait_" + tag,
        out_shape=tuple(pltpu.HBM(s.shape, s.dtype) for s in sums + lands),
        in_specs=[HBM] * (2 * n) + [SEM, SEM, pl.BlockSpec(memory_space=pl.ANY)], out_specs=tuple([HBM] * (2 * n)),
        input_output_aliases={i: i for i in range(2 * n)},
        compiler_params=pltpu.CompilerParams(has_side_effects=DATAFLOW),
    )(*sums, *lands, send_sems, recv_sems, after)
    return list(out[:n]), list(out[n:])


def sum_chips(landed, own, chip, core, name):
    _, hr, cols = landed.shape
    tm, tc = _tile2(hr, cols)
    per = hr // tm

    def body(idx_ref, l0, l1, l2, l3, own_ref, o_ref):
        mine = own_ref[...].astype(F32)
        v = [jnp.where(idx_ref[0] == k, mine, ref[...].astype(F32)) for k, ref in enumerate((l0, l1, l2, l3))]
        o_ref[...] = ((v[0] + v[1]) + v[2]) + v[3]

    slot = lambda k: pl.BlockSpec((None, tm, tc),
                                  lambda i, j, idx: (jnp.where(idx[0] == k, (k + 1) & 3, k), i, j))
    return pl.pallas_call(
        body, name=name,
        grid_spec=pltpu.PrefetchScalarGridSpec(
            num_scalar_prefetch=1, grid=(per, cols // tc),
            in_specs=[slot(0), slot(1), slot(2), slot(3),
                      pl.BlockSpec((None, tm, tc), lambda i, j, idx: (idx[0], i, j))],
            out_specs=pl.BlockSpec((tm, tc), lambda i, j, idx: (idx[1] * per + i, j))),
        out_shape=jax.ShapeDtypeStruct((2 * hr, cols), F32),
        compiler_params=_params(("arbitrary", "arbitrary")),
    )(jnp.stack([chip, core]).astype(jnp.int32), landed, landed, landed, landed, own)


def _halves_copies(refs, shapes, send_sems, recv_sems):
    x, y, c = _place()
    copies = []
    for i, ref in enumerate(refs):
        hr = shapes[i][0] // 2
        own = ref.at[pl.ds(c * hr, hr), :]
        other = ref.at[pl.ds((1 - c) * hr, hr), :]
        copies.append((_remote(own, own, send_sems.at[i], recv_sems.at[i], (x, y, 1 - c)),
                       _remote(other, other, send_sems.at[i], recv_sems.at[i], (x, y, 1 - c))))
    return copies


def halves_start(bufs):
    n = len(bufs)
    shapes = [b.shape for b in bufs]

    def body(*refs):
        for sent, _ in _halves_copies(refs[:n], shapes, refs[n], refs[n + 1]):
            sent.start()
        refs[-1][...] = jnp.zeros_like(refs[-1])

    hbm = lambda a: pltpu.with_memory_space_constraint(a, pltpu.HBM)
    dma = pltpu.SemaphoreType.DMA
    return pl.pallas_call(
        body, name="halves_start",
        out_shape=(dma((n,)), dma((n,)), *[pltpu.HBM(b.shape, b.dtype) for b in bufs],
                   jax.ShapeDtypeStruct((8, LANES), F32)),
        in_specs=[HBM] * n, out_specs=(SEM, SEM, *[HBM] * n, pl.BlockSpec(memory_space=pltpu.VMEM)),
        input_output_aliases={i: 2 + i for i in range(n)},
        compiler_params=pltpu.CompilerParams(has_side_effects=DATAFLOW),
    )(*[hbm(b) for b in bufs])


def halves_wait(started, after):
    send_sems, recv_sems, *bufs, _ = started
    n = len(bufs)
    shapes = [b.shape for b in bufs]

    def body(*refs):
        for sent, received in _halves_copies(refs[:n], shapes, refs[n], refs[n + 1]):
            received.wait_recv()
            sent.wait_send()

    return pl.pallas_call(
        body, name="halves_wait",
        out_shape=tuple(pltpu.HBM(b.shape, b.dtype) for b in bufs),
        in_specs=[HBM] * n + [SEM, SEM, pl.BlockSpec(memory_space=pl.ANY)], out_specs=tuple([HBM] * n),
        input_output_aliases={i: i for i in range(n)},
        compiler_params=pltpu.CompilerParams(has_side_effects=DATAFLOW),
    )(*bufs, send_sems, recv_sems, after)


def assemble_in_proj(landed, own, chip):
    rows, cols = 128, own.shape[1]

    def body(idx_ref, l0, l1, l2, l3, own_ref, o_ref):
        mine = own_ref[...]
        w = jnp.concatenate([jnp.where(idx_ref[0] == k, mine, ref[...]) for k, ref in enumerate((l0, l1, l2, l3))], axis=1)
        o_ref[...] = jnp.concatenate([w[:, :ORIG_Z], w[:, ORIG_GA:], w[:, ORIG_XBC:ORIG_DT], w[:, ORIG_Z:ORIG_XBC],
                                      w[:, ORIG_DT:ORIG_GA], jnp.zeros((rows, IN_PAD - IN_ORIG), w.dtype)], axis=1)

    slot = lambda k: pl.BlockSpec((None, rows, cols), lambda i, idx: (jnp.where(idx[0] == k, (k + 1) & 3, k), i, 0))
    return pl.pallas_call(
        body, name="assemble_in_proj",
        grid_spec=pltpu.PrefetchScalarGridSpec(
            num_scalar_prefetch=1, grid=(D // rows,),
            in_specs=[slot(0), slot(1), slot(2), slot(3), pl.BlockSpec((rows, cols), lambda i, idx: (i, 0))],
            out_specs=pl.BlockSpec((rows, IN_PAD), lambda i, idx: (i, 0))),
        out_shape=jax.ShapeDtypeStruct((D, IN_PAD), own.dtype),
        compiler_params=_params(("arbitrary",)),
    )(chip.reshape(1).astype(jnp.int32), landed, landed, landed, landed, own)


def rows_exchange(a, name):
    hr = a.shape[0] // 2

    def body(a_ref, out_ref, send_sem, recv_sem):
        x, y, c = _place()
        cp = _remote(a_ref.at[pl.ds((1 - c) * hr, hr), :], out_ref, send_sem, recv_sem, (x, y, 1 - c))
        cp.start()
        cp.wait()

    return pl.pallas_call(
        body, name=name, in_specs=[HBM], out_specs=HBM,
        out_shape=jax.ShapeDtypeStruct((hr, a.shape[1]), a.dtype),
        scratch_shapes=[pltpu.SemaphoreType.DMA, pltpu.SemaphoreType.DMA],
    )(a)


def rows_start(a, tag):
    hr = a.shape[0] // 2

    def body(a_ref, land_ref, send_sem, recv_sem, a_thru, land_thru, token):
        x, y, c = _place()
        _remote(a_ref.at[pl.ds((1 - c) * hr, hr), :], land_ref, send_sem, recv_sem, (x, y, 1 - c)).start()
        token[...] = jnp.zeros_like(token)

    hbm = lambda v: pltpu.with_memory_space_constraint(v, pltpu.HBM)
    dma = pltpu.SemaphoreType.DMA
    return pl.pallas_call(
        body, name="rows_start_" + tag,
        out_shape=(dma(()), dma(()), pltpu.HBM(a.shape, a.dtype), pltpu.HBM((hr, a.shape[1]), a.dtype),
                   jax.ShapeDtypeStruct((8, LANES), F32)),
        in_specs=[HBM, HBM], out_specs=(SEM, SEM, HBM, HBM, pl.BlockSpec(memory_space=pltpu.VMEM)),
        input_output_aliases={0: 2, 1: 3},
        compiler_params=pltpu.CompilerParams(has_side_effects=DATAFLOW),
    )(hbm(a), hbm(lax.empty((hr, a.shape[1]), a.dtype)))


def rows_wait(started, after, tag):
    send_sem, recv_sem, a, land, _ = started
    hr = a.shape[0] // 2

    def body(a_ref, land_ref, send_ref, recv_ref, after_ref, a_thru, got_ref):
        x, y, c = _place()
        cp = _remote(a_ref.at[pl.ds((1 - c) * hr, hr), :], land_ref, send_ref, recv_ref, (x, y, 1 - c))
        cp.wait_send()
        cp.wait_recv()

    return pl.pallas_call(
        body, name="rows_wait_" + tag,
        out_shape=(pltpu.HBM(a.shape, a.dtype), pltpu.HBM(land.shape, land.dtype)),
        in_specs=[HBM, HBM, SEM, SEM, pl.BlockSpec(memory_space=pl.ANY)], out_specs=(HBM, HBM),
        input_output_aliases={0: 0, 1: 1},
        compiler_params=pltpu.CompilerParams(has_side_effects=DATAFLOW),
    )(a, land, send_sem, recv_sem, after)


def split_pair_add(pieces, received, core):
    cols = IN_ORIG // 4
    rows, hr = 128, D // 2
    per = hr // rows
    n_p = len(pieces)

    def body(c_ref, *refs):
        o_ref = refs[-1]
        d = jnp.concatenate([refs[i][...] + refs[n_p + i][...] for i in range(n_p)], axis=1)
        w = jnp.concatenate([d[:, :COL_GA], d[:, COL_Z:COL_DT], d[:, COL_XBC:COL_Z], d[:, COL_DT:COL_DT + 32],
                             d[:, COL_GA:COL_XBC]], axis=1)
        for k in range(4):
            o_ref[k] = w[:, k * cols:(k + 1) * cols].astype(o_ref.dtype)

    return pl.pallas_call(
        body, name="split_pair_add",
        grid_spec=pltpu.PrefetchScalarGridSpec(
            num_scalar_prefetch=1, grid=(per,),
            in_specs=[pl.BlockSpec((rows, p.shape[1]), lambda i, c_ref: (c_ref[0] * per + i, 0)) for p in pieces]
            + [pl.BlockSpec((rows, p.shape[1]), lambda i, c_ref: (i, 0)) for p in received],
            out_specs=pl.BlockSpec((4, rows, cols), lambda i, c_ref: (0, i, 0))),
        out_shape=jax.ShapeDtypeStruct((4, hr, cols), BF16),
        compiler_params=_params(("arbitrary",)),
    )(core.reshape(1).astype(jnp.int32), *pieces, *received)


def ada_prepare(c_all, w_ada, hgrn_lb):
    def body(c_ref, w_ref, lb_ref, mod_ref, row_ref):
        mod_ref[...] = hdot(silu(c_ref[...]), w_ref[...])
        row_ref[...] = sigmoid(lb_ref[0:1, :] - lb_ref[1:2, :])

    return pl.pallas_call(
        body, name="ada_prepare",
        out_shape=[jax.ShapeDtypeStruct((8, w_ada.shape[1]), F32), jax.ShapeDtypeStruct((1, D), F32)],
        compiler_params=pltpu.CompilerParams(vmem_limit_bytes=VMEM_LIMIT),
    )(c_all, w_ada, hgrn_lb)


SMALL_SEGS = (("mod", 6 * D), ("lb", D), ("gnorm", LANES), ("conv_w", 4 * CONV_DIM), ("conv_b", CONV_DIM),
              ("dt_bias", LANES), ("a_log", B_INNER), ("d", B_INNER), ("ssm_norm", B_INNER),
              ("ln1_g", D), ("ln1_b", D), ("ln2_g", D), ("ln2_b", D), ("loss", LANES))
SMALL_PARAMS = ("b_ada", "hgrn_lb", "hgrn_gnorm", "ssm_conv_b", "ssm_dt_bias", "ssm_a_log", "ssm_d", "ssm_norm",
                "ln1_g", "ln1_b", "ln2_g", "ln2_b")


def finalize_small(g_all, c_all, dmod_cols, params, m, v):
    n_p = len(SMALL_PARAMS)
    offs, o = {}, 0
    for nm, width in SMALL_SEGS:
        offs[nm] = (o, width)
        o += width

    def body(*refs):
        g_ref, c_ref, dm_ref = refs[:3]
        p_refs = refs[3:3 + n_p]
        m_refs = refs[3 + n_p:3 + 2 * n_p]
        v_refs = refs[3 + 2 * n_p:3 + 3 * n_p]
        outs = refs[3 + 3 * n_p:]
        gwa_ref, gcw_ref, loss_ref = outs[:3]
        res = outs[3:]
        total = jnp.sum(g_ref[...], axis=0, keepdims=True)
        seg = lambda nm: total[:, offs[nm][0]:offs[nm][0] + offs[nm][1]]
        loss_ref[...] = seg("loss")
        gwa_ref[...] = hdot(silu(c_ref[...]), dm_ref[...], "tn")
        cw = seg("conv_w")
        for j in range(4):
            gcw_ref[j:j + 1, :] = cw[:, j * CONV_DIM:(j + 1) * CONV_DIM]
        hc = lax.broadcasted_iota(jnp.int32, (B_INNER, LANES), 0)
        hj = lax.broadcasted_iota(jnp.int32, (B_INNER, LANES), 1)
        per_head = ((hc >> 6) == hj).astype(F32)
        heads = lambda nm: hdot(jnp.broadcast_to(seg(nm), (8, B_INNER)), per_head)[0:1, 0:32]
        lbp = sigmoid(p_refs[1][0:1, :] - p_refs[1][1:2, :])
        g_row = seg("lb") * lbp * (1.0 - lbp)
        grads = {"b_ada": seg("mod"), "hgrn_gnorm": seg("gnorm"), "ssm_conv_b": seg("conv_b"),
                 "ssm_dt_bias": seg("dt_bias")[:, 0:32], "ssm_a_log": heads("a_log"), "ssm_d": heads("d"),
                 "ssm_norm": seg("ssm_norm"), "ln1_g": seg("ln1_g"), "ln1_b": seg("ln1_b"),
                 "ln2_g": seg("ln2_g"), "ln2_b": seg("ln2_b")}
        for i, nm in enumerate(SMALL_PARAMS):
            g_out, d_out, m_out, v_out = res[4 * i:4 * i + 4]
            if nm == "hgrn_lb":
                for row, gv in ((0, g_row), (1, -g_row)):
                    sl = slice(row, row + 1)
                    dl, mn, vn = adamw(p_refs[i][sl, :], gv, m_refs[i][sl, :], v_refs[i][sl, :])
                    g_out[sl, :], d_out[sl, :], m_out[sl, :], v_out[sl, :] = gv, dl, mn, vn
            else:
                gv = grads[nm]
                dl, mn, vn = adamw(p_refs[i][...], gv, m_refs[i][...], v_refs[i][...])
                g_out[...], d_out[...], m_out[...], v_out[...] = gv, dl, mn, vn

    out_shape = [jax.ShapeDtypeStruct((D, dmod_cols.shape[1]), F32), jax.ShapeDtypeStruct((4, CONV_DIM), F32),
                 jax.ShapeDtypeStruct((1, LANES), F32)]
    for p in params:
        out_shape += [jax.ShapeDtypeStruct(p.shape, F32)] * 4
    return pl.pallas_call(
        body, name="finalize_small", out_shape=out_shape,
        compiler_params=pltpu.CompilerParams(vmem_limit_bytes=VMEM_LIMIT),
    )(g_all, c_all, dmod_cols, *params, *m, *v)


def adam_update(w, g, m, v, name, after=None):
    rows, cols = w.shape
    tm, tc = _tile2(rows, cols)
    order = [] if after is None else [after]

    def body(w_ref, g_ref, m_ref, v_ref, *rest):
        d_ref, mo_ref, vo_ref = rest[len(order):]
        d_ref[...], mo_ref[...], vo_ref[...] = adamw(w_ref[...], g_ref[...], m_ref[...], v_ref[...])

    spec = pl.BlockSpec((tm, tc), lambda i, j: (i, j))
    return pl.pallas_call(
        body, name=name, grid=(rows // tm, cols // tc),
        in_specs=[spec] * 4 + [pl.BlockSpec(memory_space=pl.ANY) for _ in order], out_specs=[spec] * 3,
        out_shape=[jax.ShapeDtypeStruct((rows, cols), F32)] * 3,
        compiler_params=_params(("arbitrary", "arbitrary")),
    )(w, g, m, v, *order)


def kernel(x, c, w_ada, b_ada, w_in, hgrn_lb, hgrn_gnorm, ssm_conv_w, ssm_conv_b, ssm_dt_bias, ssm_a_log, ssm_d, ssm_norm, w_branch_a, w_branch_b, w_o, ln1_g, ln1_b, w_ffn_gate, w_ffn_up, w_ffn_down, ln2_g, ln2_b, loss_target, m_w_ada, m_b_ada, m_w_in, m_hgrn_lb, m_hgrn_gnorm, m_ssm_conv_w, m_ssm_conv_b, m_ssm_dt_bias, m_ssm_a_log, m_ssm_d, m_ssm_norm, m_w_branch_a, m_w_branch_b, m_w_o, m_ln1_g, m_ln1_b, m_w_ffn_gate, m_w_ffn_up, m_w_ffn_down, m_ln2_g, m_ln2_b, v_w_ada, v_b_ada, v_w_in, v_hgrn_lb, v_hgrn_gnorm, v_ssm_conv_w, v_ssm_conv_b, v_ssm_dt_bias, v_ssm_a_log, v_ssm_d, v_ssm_norm, v_w_branch_a, v_w_branch_b, v_w_o, v_ln1_g, v_ln1_b, v_w_ffn_gate, v_w_ffn_up, v_w_ffn_down, v_ln2_g, v_ln2_b):
    given = dict(locals())
    chip = 2 * lax.axis_index("x") + lax.axis_index("y")
    core = lax.axis_index("c")
    t = x.shape[1]

    first = gather_rows(jnp.concatenate([c, ssm_conv_w.reshape(1, CONV_DIM)], axis=1), "gather_cond").reshape(8, D + CONV_DIM)
    c_all = first[:, :D]
    conv_w = first[0::2, D:].reshape(4, 4, CONV_DIM // 4).transpose(1, 0, 2).reshape(4, CONV_DIM)
    mod_part, lb_row = ada_prepare(c_all, w_ada[0], hgrn_lb)
    mod_cols = w_ada.shape[2]
    mod_row = exchange_rows(mod_part.reshape(8, 1, mod_cols), "exchange_mod").reshape(1, 6 * D) + b_ada

    local = {nm: given[nm][0] for nm in SHARDED if nm != "w_ffn_in"}
    local["w_ffn_in"] = jnp.concatenate([w_ffn_gate[0].T, w_ffn_up[0].T], axis=0)
    shards = [local[nm].astype(BF16) for nm in SHARDED]
    send_in, recv_in, sent_in, land_in, started_in = gather_start(shards[:1], mod_row, "in")
    shards = shards[:1] + [(local[nm] + started_in[0, 0]).astype(BF16) for nm in SHARDED[1:]]
    send_rest, recv_rest, *flying = gather_start(shards[1:], started_in, "rest")
    n_rest = len(SHARDED) - 1
    sent_rest, land_rest, started_rest = flying[:n_rest], flying[n_rest:2 * n_rest], flying[-1]
    mod_row = mod_row + started_rest[0:1, 0:1]
    mod = tuple(mod_row[:, i * D:(i + 1) * D] for i in range(6))
    with_own = lambda land, shard: lax.dynamic_update_slice(land, shard[None], (chip, 0, 0))

    class Weights:
        def input_projection(self, after):
            (own,), land = gather_wait(send_in, recv_in, [sent_in], [land_in], after, "in")
            (land,) = forward_wait(forward_start(land, "in"), after, "in")
            return assemble_in_proj(land, own, chip)

        def start_rest(self, after):
            self.own, landed = gather_wait(send_rest, recv_rest, sent_rest, land_rest, after, "rest")
            self.started = forward_start(landed, "rest")
            return self.started[-1]

        def rest(self, after):
            got = {nm: with_own(land, s) for nm, land, s in zip(SHARDED[1:], forward_wait(self.started, after, "rest"), self.own, strict=True)}
            whole = lambda nm: got[nm].reshape(4 * got[nm].shape[1], got[nm].shape[2])
            return tuple(whole(nm) for nm in SHARDED[1:])

    wts = Weights()

    per_head = lambda p: jnp.pad(p, ((0, 0), (0, LANES - p.shape[1])))
    per_channel = lambda p: jnp.repeat(p[0], B_INNER // 32)[None]
    small = (lb_row, hgrn_gnorm, conv_w, ssm_conv_b, per_head(ssm_dt_bias), per_channel(ssm_a_log),
             per_channel(ssm_d), ssm_norm, ln1_g, ln1_b, ln2_g, ln2_b)
    by_rows = lambda g: g.reshape(4, g.shape[0] // 4, g.shape[1])
    travelling = {}

    def start_early(dws):
        travelling["pair"] = pair_start([by_rows(dw) for dw in dws], "early")
        return travelling["pair"][-1]

    def between_scans(after):
        slabs, received = pair_wait(travelling["pair"], after, "early")
        travelling["pairs"] = [pair_add(s, r, core, "pair_add_" + nm) for nm, s, r in zip(SHARDED[1:], slabs, received, strict=True)]
        travelling["started"] = scatter_start(travelling["pairs"], "early")
        return travelling["started"][-1]

    def finish_early(after):
        travelling["pairs"], travelling["landed"] = scatter_wait(travelling["started"], after, "early")

    def start_last(u1, dproj):
        wide = 2 * IN_PAD // 3
        first = matmul(u1, dproj, "tn", F32, "in_proj_dw_first", b_cols=(0, wide))
        sending = rows_start(first, "last")
        second = matmul(u1, dproj, "tn", F32, "in_proj_dw_second", after=sending[-1], b_cols=(wide, IN_PAD - wide))
        first, got_first = rows_wait(sending, second, "last")
        got_second = rows_exchange(second, "pair_exchange_last")
        travelling["pairs_in"] = [split_pair_add([first, second], [got_first, got_second], core)]
        travelling["started_in"] = scatter_start(travelling["pairs_in"], "last")
        return travelling["started_in"][-1]

    loss, grad_x, d_mod, d_wts, d_small = local_step(x[0], loss_target[0], mod, wts, small,
                                                     start_early, between_scans, finish_early, start_last)

    d_lb, d_gn, d_cw, d_cb, d_dtb, d_alog, d_dsk, d_nw, d_l1g, d_l1b, d_l2g, d_l2b = d_small
    row = jnp.concatenate(list(d_mod) + [d_lb, d_gn, d_cw.reshape(1, 4 * CONV_DIM), d_cb, d_dtb, d_alog, d_dsk, d_nw,
                                          d_l1g, d_l1b, d_l2g, d_l2b, jnp.pad(loss, ((0, 0), (0, LANES - 1)))], axis=1)
    pairs_in, landed_in = scatter_wait(travelling["started_in"], row, "last")
    pairs, landed = pairs_in + travelling["pairs"], landed_in + travelling["landed"]
    halves = [sum_chips(r, p, chip, core, "sum_chips_" + nm) for nm, r, p in zip(SHARDED, landed, pairs, strict=True)]
    exchanging = halves_start(halves)
    row = row + exchanging[-1][0:1, 0:1]
    g_all = gather_rows(row, "gather_small_grads").reshape(8, row.shape[1])
    dmod_cols = lax.dynamic_slice_in_dim(g_all, chip * mod_cols, mod_cols, axis=1)
    fin = finalize_small(g_all, c_all, dmod_cols, [given[n] for n in SMALL_PARAMS],
                         [given["m_" + n] for n in SMALL_PARAMS], [given["v_" + n] for n in SMALL_PARAMS])
    grads, deltas, new_m, new_v = {}, {}, {}, {}
    grads["w_ada"] = fin[0][None]
    grads["ssm_conv_w"] = lax.dynamic_slice_in_dim(fin[1], chip * (CONV_DIM // 4), CONV_DIM // 4, axis=1)[None]
    for i, nm in enumerate(SMALL_PARAMS):
        grads[nm], deltas[nm], new_m[nm], new_v[nm] = fin[3 + 4 * i:7 + 4 * i]

    reduced ={"w_ada": grads["w_ada"][0], "ssm_conv_w": grads["ssm_conv_w"][0]}

    def update(nm, after=None):
        flipped = nm in ("w_in", "w_ffn_gate", "w_ffn_up")
        work = (lambda a: a[0].T) if flipped else (lambda a: a[0])
        back = (lambda a: a.T[None]) if flipped else (lambda a: a[None])
        d_, m_, v_ = adam_update(work(given[nm]), reduced[nm], work(given["m_" + nm]), work(given["v_" + nm]),
                                 "adam_" + nm, after)
        grads[nm], deltas[nm], new_m[nm], new_v[nm] = back(reduced[nm]), back(d_), back(m_), back(v_)

    update("w_ada", exchanging[-1])
    update("ssm_conv_w", exchanging[-1])
    reduced.update(zip(SHARDED, halves_wait(exchanging, new_m["w_ada"]), strict=True))
    reduced["w_in"] = reduced["w_in"].T
    reduced["w_ffn_gate"], reduced["w_ffn_up"] = reduced["w_ffn_in"][:FFN_SHARD], reduced["w_ffn_in"][FFN_SHARD:]
    for nm in ("w_in", "w_branch_a", "w_branch_b", "w_o", "w_ffn_gate", "w_ffn_up", "w_ffn_down"):
        update(nm)

    names = ("w_ada", "b_ada", "w_in", "hgrn_lb", "hgrn_gnorm", "ssm_conv_w", "ssm_conv_b", "ssm_dt_bias", "ssm_a_log",
             "ssm_d", "ssm_norm", "w_branch_a", "w_branch_b", "w_o", "ln1_g", "ln1_b", "w_ffn_gate", "w_ffn_up",
             "w_ffn_down", "ln2_g", "ln2_b")
    return (fin[2][0, 0], grad_x[None], *[grads[n] for n in names], *[deltas[n] for n in names],
            *[new_m[n] for n in names], *[new_v[n] for n in names])
```
